```python
import math
import jax, jax.numpy as jnp
from jax import lax
import numpy as np

D_MODEL = 1024
BATCH = 8
SEQ = 2048
DEPTH = 2

HEAD_DIM = 64
N_Q_HEADS = 8
N_KV_HEADS = 2
WINDOW = 128
BLOCK = 128
N_BUCKETS = 32
MAX_DISTANCE = 128
SSM_HEADS = 8
SSM_HEAD_DIM = 64
SSM_GROUPS = 2
SSM_STATE = 128
CONV_WIDTH = 4
CHUNK = 128
D_FF = 4 * D_MODEL

D_ATTN = N_Q_HEADS * HEAD_DIM
D_KV = N_KV_HEADS * HEAD_DIM
D_SSM = SSM_HEADS * SSM_HEAD_DIM
D_BC = SSM_GROUPS * SSM_STATE
D_CONV = D_SSM + 2 * D_BC
D_MIX = D_ATTN + D_SSM
D_IN = D_ATTN + 2 * D_KV + D_SSM + D_CONV + SSM_HEADS
SPLITS = [D_ATTN, D_ATTN + D_KV, D_ATTN + 2 * D_KV, D_ATTN + 2 * D_KV + D_SSM,
          D_ATTN + 2 * D_KV + D_SSM + D_CONV]
EPS = 1e-6

kernel_name = "hymba_swa_sink_ssd_hybrid"


def rms_norm(x, g):
    xf = x.astype(jnp.float32)
    y = xf * lax.rsqrt(jnp.mean(jnp.square(xf), axis=-1, keepdims=True) + EPS)
    return (y * g.astype(jnp.float32)).astype(x.dtype)


def t5_causal_bucket(dist):
    max_exact = N_BUCKETS // 2
    d_f = jnp.maximum(dist, 1).astype(jnp.float32)
    large = max_exact + (jnp.log(d_f / max_exact) / math.log(MAX_DISTANCE / max_exact)
                         * (N_BUCKETS - max_exact)).astype(jnp.int32)
    large = jnp.minimum(large, N_BUCKETS - 1)
    return jnp.where(dist < max_exact, dist, large)


def band_bias_and_mask(rel_bias, n_blocks):
    qi = jnp.arange(BLOCK)[:, None]
    kj = jnp.arange(2 * BLOCK)[None, :]
    dist = qi + BLOCK - kj
    in_window = (dist >= 0) & (dist < WINDOW)
    bucket = t5_causal_bucket(jnp.clip(dist, 0, None))
    bias = jnp.transpose(rel_bias[bucket], (2, 0, 1))
    key_pos = jnp.arange(n_blocks)[:, None] * BLOCK + kj - BLOCK
    mask = in_window[None] & (key_pos >= 0)[:, None, :]
    return bias, mask


def sliding_window_attention(q, k, v, q_gain, k_gain, sinks, bias, mask):
    b, s = q.shape[:2]
    nb = s // BLOCK
    g = N_Q_HEADS // N_KV_HEADS
    q = rms_norm(q.reshape(b, s, N_Q_HEADS, HEAD_DIM), q_gain)
    k = rms_norm(k.reshape(b, s, N_KV_HEADS, HEAD_DIM), k_gain)
    v = v.reshape(b, s, N_KV_HEADS, HEAD_DIM)
    qb = q.reshape(b, nb, BLOCK, N_KV_HEADS, g, HEAD_DIM)

    def band(t):
        t = t.reshape(b, nb, BLOCK, N_KV_HEADS, HEAD_DIM)
        prev = jnp.pad(t, ((0, 0), (1, 0), (0, 0), (0, 0), (0, 0)))[:, :-1]
        return jnp.concatenate([prev, t], axis=2)

    kb, vb = band(k), band(v)
    scores = jnp.einsum('bnqhgd,bnkhd->bnhgqk', qb, kb).astype(jnp.float32) * (HEAD_DIM ** -0.5)
    scores = scores + bias.reshape(N_KV_HEADS, g, BLOCK, 2 * BLOCK).astype(jnp.float32)
    scores = jnp.where(mask[None, :, None, None], scores, -jnp.inf)
    sink = jnp.broadcast_to(sinks.reshape(N_KV_HEADS, g, 1, 1).astype(jnp.float32),
                            scores.shape[:-1] + (1,))
    probs = jax.nn.softmax(jnp.concatenate([scores, sink], axis=-1), axis=-1)[..., :-1]
    out = jnp.einsum('bnhgqk,bnkhd->bnqhgd', probs.astype(v.dtype), vb)
    return out.reshape(b, s, D_ATTN)


def causal_depthwise_conv(u, w, bias):
    out = lax.conv_general_dilated(u, w[:, None, :], window_strides=(1,),
                                   padding=[(CONV_WIDTH - 1, 0)],
                                   dimension_numbers=('NWC', 'WIO', 'NWC'),
                                   feature_group_count=u.shape[-1])
    return out + bias


def ssd_mixer(z, xbc, dt_raw, conv_w, conv_b, dt_bias, a_log, d_skip, norm_g):
    f32 = jnp.float32
    b, s = z.shape[:2]
    nc = s // CHUNK
    r = SSM_HEADS // SSM_GROUPS
    xbc = jax.nn.silu(causal_depthwise_conv(xbc, conv_w, conv_b))
    xs, bm, cm = jnp.split(xbc, [D_SSM, D_SSM + D_BC], axis=-1)
    xs = xs.astype(f32).reshape(b, nc, CHUNK, SSM_GROUPS, r, SSM_HEAD_DIM)
    bm = bm.astype(f32).reshape(b, nc, CHUNK, SSM_GROUPS, SSM_STATE)
    cm = cm.astype(f32).reshape(b, nc, CHUNK, SSM_GROUPS, SSM_STATE)
    dt = jax.nn.softplus(dt_raw.astype(f32) + dt_bias.astype(f32)).reshape(b, nc, CHUNK, SSM_GROUPS, r)
    a = -jnp.exp(a_log.astype(f32)).reshape(SSM_GROUPS, r)
    a_cs = jnp.cumsum(dt * a, axis=2)
    xdt = xs * dt[..., None]
    li = jnp.arange(CHUNK)
    causal = (li[:, None] >= li[None, :])[:, :, None, None]
    seg = a_cs[:, :, :, None] - a_cs[:, :, None, :]
    decay = jnp.exp(jnp.where(causal, seg, -jnp.inf))
    cb = jnp.einsum('bclgn,bcsgn->bclsg', cm, bm)
    y_diag = jnp.einsum('bclsgr,bcsgrp->bclgrp', cb[..., None] * decay, xdt)
    decay_to_end = jnp.exp(a_cs[:, :, -1:] - a_cs)
    states = jnp.einsum('bclgn,bclgr,bclgrp->bcgrpn', bm, decay_to_end, xdt)
    chunk_decay = jnp.exp(a_cs[:, :, -1])

    def step(h, inp):
        st, dec = inp
        return h * dec[..., None, None] + st, h

    h0 = jnp.zeros((b, SSM_GROUPS, r, SSM_HEAD_DIM, SSM_STATE), f32)
    _, prev = lax.scan(step, h0, (jnp.moveaxis(states, 1, 0), jnp.moveaxis(chunk_decay, 1, 0)))
    prev = jnp.moveaxis(prev, 0, 1)
    y_off = jnp.einsum('bclgn,bcgrpn,bclgr->bclgrp', cm, prev, jnp.exp(a_cs))
    y = y_diag + y_off + xs * d_skip.astype(f32).reshape(SSM_GROUPS, r)[:, :, None]
    y = y.reshape(b, s, D_SSM) * jax.nn.silu(z.astype(f32))
    yg = y.reshape(b, s, SSM_GROUPS, D_SSM // SSM_GROUPS)
    yg = yg * lax.rsqrt(jnp.mean(jnp.square(yg), axis=-1, keepdims=True) + EPS)
    y = yg.reshape(b, s, D_SSM) * norm_g.astype(f32)
    return y.astype(z.dtype)


def _fwd_setup_inputs(seed: int = 0) -> dict:
    key = jax.random.key(seed)
    ks = jax.random.split(key, 20)
    nrm = jax.random.normal
    dt0 = jnp.exp(jax.random.uniform(ks[9], (DEPTH, SSM_HEADS), minval=math.log(1e-3), maxval=math.log(1e-1)))
    return {
        "x": nrm(ks[0], (BATCH, SEQ, D_MODEL), jnp.float32),
        "mix_norm_g": 1.0 + 0.01 * nrm(ks[1], (DEPTH, D_MODEL), jnp.float32),
        "w_in": nrm(ks[2], (DEPTH, D_MODEL, D_IN), jnp.float32) * D_MODEL ** -0.5,
        "q_gain": 1.0 + 0.01 * nrm(ks[3], (DEPTH, HEAD_DIM), jnp.float32),
        "k_gain": 1.0 + 0.01 * nrm(ks[4], (DEPTH, HEAD_DIM), jnp.float32),
        "sinks": 0.5 * nrm(ks[5], (DEPTH, N_Q_HEADS), jnp.float32),
        "rel_bias": 0.1 * nrm(ks[6], (N_BUCKETS, N_Q_HEADS), jnp.float32),
        "conv_w": nrm(ks[7], (DEPTH, CONV_WIDTH, D_CONV), jnp.float32) * CONV_WIDTH ** -0.5,
        "conv_b": 0.01 * nrm(ks[8], (DEPTH, D_CONV), jnp.float32),
        "dt_bias": dt0 + jnp.log(-jnp.expm1(-dt0)),
        "a_log": jnp.log(jax.random.uniform(ks[10], (DEPTH, SSM_HEADS), minval=1.0, maxval=16.0)),
        "d_skip": 1.0 + 0.01 * nrm(ks[11], (DEPTH, SSM_HEADS), jnp.float32),
        "ssm_norm_g": 1.0 + 0.01 * nrm(ks[12], (DEPTH, D_SSM), jnp.float32),
        "w_out": nrm(ks[13], (DEPTH, D_MIX, D_MODEL), jnp.float32) * D_MIX ** -0.5,
        "mlp_norm_g": 1.0 + 0.01 * nrm(ks[14], (DEPTH, D_MODEL), jnp.float32),
        "w_up": nrm(ks[15], (DEPTH, D_MODEL, D_FF), jnp.float32) * D_MODEL ** -0.5,
        "w_down": nrm(ks[16], (DEPTH, D_FF, D_MODEL), jnp.float32) * D_FF ** -0.5,
    }


def _fwd_reference(x, mix_norm_g, w_in, q_gain, k_gain, sinks, rel_bias, conv_w, conv_b,
              dt_bias, a_log, d_skip, ssm_norm_g, w_out, mlp_norm_g, w_up, w_down):
    bias, mask = band_bias_and_mask(rel_bias, x.shape[1] // BLOCK)
    for l in range(DEPTH):
        h = rms_norm(x, mix_norm_g[l])
        proj = h @ w_in[l]
        q, k, v, z, xbc, dt_raw = jnp.split(proj, SPLITS, axis=-1)
        attn = sliding_window_attention(q, k, v, q_gain[l], k_gain[l], sinks[l], bias, mask)
        ssm = ssd_mixer(z, xbc, dt_raw, conv_w[l], conv_b[l], dt_bias[l], a_log[l],
                        d_skip[l], ssm_norm_g[l])
        x = x + jnp.concatenate([attn, ssm], axis=-1) @ w_out[l]
        h = rms_norm(x, mlp_norm_g[l])
        x = x + jnp.square(jax.nn.relu(h @ w_up[l])) @ w_down[l]
    return x


import jax as _jax
import jax.numpy as _jnp

TWIN_FORMAT = 'train_step'
FWD_PARAMS = ['x', 'mix_norm_g', 'w_in', 'q_gain', 'k_gain', 'sinks', 'rel_bias', 'conv_w', 'conv_b', 'dt_bias', 'a_log', 'd_skip', 'ssm_norm_g', 'w_out', 'mlp_norm_g', 'w_up', 'w_down']
TWIN_WEIGHTS = ['mix_norm_g', 'w_in', 'q_gain', 'k_gain', 'sinks', 'rel_bias', 'conv_w', 'conv_b', 'dt_bias', 'a_log', 'd_skip', 'ssm_norm_g', 'w_out', 'mlp_norm_g', 'w_up', 'w_down']
TWIN_DIFF_INPUT = 'x'
TWIN_INPUTS = ['x', 'mix_norm_g', 'w_in', 'q_gain', 'k_gain', 'sinks', 'rel_bias', 'conv_w', 'conv_b', 'dt_bias', 'a_log', 'd_skip', 'ssm_norm_g', 'w_out', 'mlp_norm_g', 'w_up', 'w_down', 'loss_target', 'm_mix_norm_g', 'm_w_in', 'm_q_gain', 'm_k_gain', 'm_sinks', 'm_rel_bias', 'm_conv_w', 'm_conv_b', 'm_dt_bias', 'm_a_log', 'm_d_skip', 'm_ssm_norm_g', 'm_w_out', 'm_mlp_norm_g', 'm_w_up', 'm_w_down', 'v_mix_norm_g', 'v_w_in', 'v_q_gain', 'v_k_gain', 'v_sinks', 'v_rel_bias', 'v_conv_w', 'v_conv_b', 'v_dt_bias', 'v_a_log', 'v_d_skip', 'v_ssm_norm_g', 'v_w_out', 'v_mlp_norm_g', 'v_w_up', 'v_w_down']
TWIN_OUTPUTS = ['loss', 'grad_x', 'grad_mix_norm_g', 'grad_w_in', 'grad_q_gain', 'grad_k_gain', 'grad_sinks', 'grad_rel_bias', 'grad_conv_w', 'grad_conv_b', 'grad_dt_bias', 'grad_a_log', 'grad_d_skip', 'grad_ssm_norm_g', 'grad_w_out', 'grad_mlp_norm_g', 'grad_w_up', 'grad_w_down', 'delta_mix_norm_g', 'delta_w_in', 'delta_q_gain', 'delta_k_gain', 'delta_sinks', 'delta_rel_bias', 'delta_conv_w', 'delta_conv_b', 'delta_dt_bias', 'delta_a_log', 'delta_d_skip', 'delta_ssm_norm_g', 'delta_w_out', 'delta_mlp_norm_g', 'delta_w_up', 'delta_w_down', 'new_m_mix_norm_g', 'new_m_w_in', 'new_m_q_gain', 'new_m_k_gain', 'new_m_sinks', 'new_m_rel_bias', 'new_m_conv_w', 'new_m_conv_b', 'new_m_dt_bias', 'new_m_a_log', 'new_m_d_skip', 'new_m_ssm_norm_g', 'new_m_w_out', 'new_m_mlp_norm_g', 'new_m_w_up', 'new_m_w_down', 'new_v_mix_norm_g', 'new_v_w_in', 'new_v_q_gain', 'new_v_k_gain', 'new_v_sinks', 'new_v_rel_bias', 'new_v_conv_w', 'new_v_conv_b', 'new_v_dt_bias', 'new_v_a_log', 'new_v_d_skip', 'new_v_ssm_norm_g', 'new_v_w_out', 'new_v_mlp_norm_g', 'new_v_w_up', 'new_v_w_down']
TWIN_LEAF_KINDS = {'loss': 'loss', 'grad_x': 'grad_x', 'grad_mix_norm_g': 'grad_w', 'grad_w_in': 'grad_w', 'grad_q_gain': 'grad_w', 'grad_k_gain': 'grad_w', 'grad_sinks': 'grad_w', 'grad_rel_bias': 'grad_w', 'grad_conv_w': 'grad_w', 'grad_conv_b': 'grad_w', 'grad_dt_bias': 'grad_w', 'grad_a_log': 'grad_w', 'grad_d_skip': 'grad_w', 'grad_ssm_norm_g': 'grad_w', 'grad_w_out': 'grad_w', 'grad_mlp_norm_g': 'grad_w', 'grad_w_up': 'grad_w', 'grad_w_down': 'grad_w', 'delta_mix_norm_g': 'delta_w', 'delta_w_in': 'delta_w', 'delta_q_gain': 'delta_w', 'delta_k_gain': 'delta_w', 'delta_sinks': 'delta_w', 'delta_rel_bias': 'delta_w', 'delta_conv_w': 'delta_w', 'delta_conv_b': 'delta_w', 'delta_dt_bias': 'delta_w', 'delta_a_log': 'delta_w', 'delta_d_skip': 'delta_w', 'delta_ssm_norm_g': 'delta_w', 'delta_w_out': 'delta_w', 'delta_mlp_norm_g': 'delta_w', 'delta_w_up': 'delta_w', 'delta_w_down': 'delta_w', 'new_m_mix_norm_g': 'new_m', 'new_m_w_in': 'new_m', 'new_m_q_gain': 'new_m', 'new_m_k_gain': 'new_m', 'new_m_sinks': 'new_m', 'new_m_rel_bias': 'new_m', 'new_m_conv_w': 'new_m', 'new_m_conv_b': 'new_m', 'new_m_dt_bias': 'new_m', 'new_m_a_log': 'new_m', 'new_m_d_skip': 'new_m', 'new_m_ssm_norm_g': 'new_m', 'new_m_w_out': 'new_m', 'new_m_mlp_norm_g': 'new_m', 'new_m_w_up': 'new_m', 'new_m_w_down': 'new_m', 'new_v_mix_norm_g': 'new_v', 'new_v_w_in': 'new_v', 'new_v_q_gain': 'new_v', 'new_v_k_gain': 'new_v', 'new_v_sinks': 'new_v', 'new_v_rel_bias': 'new_v', 'new_v_conv_w': 'new_v', 'new_v_conv_b': 'new_v', 'new_v_dt_bias': 'new_v', 'new_v_a_log': 'new_v', 'new_v_d_skip': 'new_v', 'new_v_ssm_norm_g': 'new_v', 'new_v_w_out': 'new_v', 'new_v_mlp_norm_g': 'new_v', 'new_v_w_up': 'new_v', 'new_v_w_down': 'new_v'}


def _forward(args):
    return _fwd_reference(*[args[k] for k in FWD_PARAMS])


def _output_shape():
    out = _jax.eval_shape(lambda: _forward(_fwd_setup_inputs(0)))
    return out.shape, out.dtype

N_MICROBATCH = 1
ADAM_LR = 0.001
ADAM_B1 = 0.9
ADAM_B2 = 0.999
ADAM_EPS = 1e-08
ADAM_WD = 0.01
ADAM_STEP = 10
PER_EXAMPLE_BATCH_AXIS = {'x': 0, 'loss_target': 0}
SHARED_INPUTS = []
_WEIGHT_DTYPES = {'mix_norm_g': _jnp.float32, 'w_in': _jnp.float32, 'q_gain': _jnp.float32, 'k_gain': _jnp.float32, 'sinks': _jnp.float32, 'rel_bias': _jnp.float32, 'conv_w': _jnp.float32, 'conv_b': _jnp.float32, 'dt_bias': _jnp.float32, 'a_log': _jnp.float32, 'd_skip': _jnp.float32, 'ssm_norm_g': _jnp.float32, 'w_out': _jnp.float32, 'mlp_norm_g': _jnp.float32, 'w_up': _jnp.float32, 'w_down': _jnp.float32}
MOMENT_SCALE = {'mix_norm_g': 3.811241e+00, 'w_in': 2.210728e+00, 'q_gain': 1.597345e+00, 'k_gain': 1.601297e+00, 'sinks': 9.441474e-01, 'rel_bias': 2.488617e-01, 'conv_w': 2.093000e+00, 'conv_b': 6.806156e+00, 'dt_bias': 1.606943e+00, 'a_log': 1.346614e+01, 'd_skip': 7.984750e+00, 'ssm_norm_g': 2.467656e+01, 'w_out': 4.482086e+00, 'mlp_norm_g': 4.853435e+01, 'w_up': 2.331165e+00, 'w_down': 8.896435e+00}


def _to_microbatches(a, axis):
    t = _jnp.moveaxis(a, axis, 0)
    t = t.reshape((N_MICROBATCH, t.shape[0] // N_MICROBATCH) + t.shape[1:])
    return _jnp.moveaxis(t, 1, axis + 1)


def setup_inputs(seed: int = 0) -> dict:
    inp = _fwd_setup_inputs(seed)
    key = _jax.random.fold_in(_jax.random.key(seed), 7919)
    shape, _ = _output_shape()
    out = dict(inp)
    out["loss_target"] = _jax.random.normal(_jax.random.fold_in(key, 0), shape, _jnp.float32)
    for i, name in enumerate(TWIN_WEIGHTS):
        w = inp[name].astype(_jnp.float32)
        if MOMENT_SCALE is None:
            s = _jnp.sqrt(_jnp.mean(_jnp.square(w)) + 1e-30)
        else:
            s = MOMENT_SCALE[name]
        km, kv = _jax.random.split(_jax.random.fold_in(key, i + 1))
        out[name] = w
        out["m_" + name] = s * _jax.random.normal(km, w.shape, _jnp.float32)
        out["v_" + name] = (s * s) * _jax.random.uniform(kv, w.shape, _jnp.float32, 0.5, 1.5)
    if N_MICROBATCH > 1:
        for name, axis in PER_EXAMPLE_BATCH_AXIS.items():
            out[name] = _to_microbatches(out[name], axis)
    return {'x': out['x'], 'mix_norm_g': out['mix_norm_g'], 'w_in': out['w_in'], 'q_gain': out['q_gain'], 'k_gain': out['k_gain'], 'sinks': out['sinks'], 'rel_bias': out['rel_bias'], 'conv_w': out['conv_w'], 'conv_b': out['conv_b'], 'dt_bias': out['dt_bias'], 'a_log': out['a_log'], 'd_skip': out['d_skip'], 'ssm_norm_g': out['ssm_norm_g'], 'w_out': out['w_out'], 'mlp_norm_g': out['mlp_norm_g'], 'w_up': out['w_up'], 'w_down': out['w_down'], 'loss_target': out['loss_target'], 'm_mix_norm_g': out['m_mix_norm_g'], 'm_w_in': out['m_w_in'], 'm_q_gain': out['m_q_gain'], 'm_k_gain': out['m_k_gain'], 'm_sinks': out['m_sinks'], 'm_rel_bias': out['m_rel_bias'], 'm_conv_w': out['m_conv_w'], 'm_conv_b': out['m_conv_b'], 'm_dt_bias': out['m_dt_bias'], 'm_a_log': out['m_a_log'], 'm_d_skip': out['m_d_skip'], 'm_ssm_norm_g': out['m_ssm_norm_g'], 'm_w_out': out['m_w_out'], 'm_mlp_norm_g': out['m_mlp_norm_g'], 'm_w_up': out['m_w_up'], 'm_w_down': out['m_w_down'], 'v_mix_norm_g': out['v_mix_norm_g'], 'v_w_in': out['v_w_in'], 'v_q_gain': out['v_q_gain'], 'v_k_gain': out['v_k_gain'], 'v_sinks': out['v_sinks'], 'v_rel_bias': out['v_rel_bias'], 'v_conv_w': out['v_conv_w'], 'v_conv_b': out['v_conv_b'], 'v_dt_bias': out['v_dt_bias'], 'v_a_log': out['v_a_log'], 'v_d_skip': out['v_d_skip'], 'v_ssm_norm_g': out['v_ssm_norm_g'], 'v_w_out': out['v_w_out'], 'v_mlp_norm_g': out['v_mlp_norm_g'], 'v_w_up': out['v_w_up'], 'v_w_down': out['v_w_down']}


def _loss(weights, diff, rest, loss_target):
    with _jax.named_scope("forward"):
        args = {**rest, TWIN_DIFF_INPUT: diff, **{k: w.astype(_WEIGHT_DTYPES[k]) for k, w in weights.items()}}
        y = _forward(args)
    with _jax.named_scope("loss_head"):
        err = _jnp.square(y.astype(_jnp.float32) - loss_target)
        return 0.5 * _jnp.sum(_jnp.mean(err, axis=-1)) if err.ndim else 0.5 * err


def _adamw(w, g, m, v):
    m = ADAM_B1 * m + (1.0 - ADAM_B1) * g
    v = ADAM_B2 * v + (1.0 - ADAM_B2) * _jnp.square(g)
    m_hat = m / (1.0 - ADAM_B1 ** ADAM_STEP)
    v_hat = v / (1.0 - ADAM_B2 ** ADAM_STEP)
    delta = -ADAM_LR * (m_hat / (_jnp.sqrt(v_hat) + ADAM_EPS) + ADAM_WD * w)
    return delta, m, v


def reference(x, mix_norm_g, w_in, q_gain, k_gain, sinks, rel_bias, conv_w, conv_b, dt_bias, a_log, d_skip, ssm_norm_g, w_out, mlp_norm_g, w_up, w_down, loss_target, m_mix_norm_g, m_w_in, m_q_gain, m_k_gain, m_sinks, m_rel_bias, m_conv_w, m_conv_b, m_dt_bias, m_a_log, m_d_skip, m_ssm_norm_g, m_w_out, m_mlp_norm_g, m_w_up, m_w_down, v_mix_norm_g, v_w_in, v_q_gain, v_k_gain, v_sinks, v_rel_bias, v_conv_w, v_conv_b, v_dt_bias, v_a_log, v_d_skip, v_ssm_norm_g, v_w_out, v_mlp_norm_g, v_w_up, v_w_down):
    given = dict(x=x, mix_norm_g=mix_norm_g, w_in=w_in, q_gain=q_gain, k_gain=k_gain, sinks=sinks, rel_bias=rel_bias, conv_w=conv_w, conv_b=conv_b, dt_bias=dt_bias, a_log=a_log, d_skip=d_skip, ssm_norm_g=ssm_norm_g, w_out=w_out, mlp_norm_g=mlp_norm_g, w_up=w_up, w_down=w_down, loss_target=loss_target, m_mix_norm_g=m_mix_norm_g, m_w_in=m_w_in, m_q_gain=m_q_gain, m_k_gain=m_k_gain, m_sinks=m_sinks, m_rel_bias=m_rel_bias, m_conv_w=m_conv_w, m_conv_b=m_conv_b, m_dt_bias=m_dt_bias, m_a_log=m_a_log, m_d_skip=m_d_skip, m_ssm_norm_g=m_ssm_norm_g, m_w_out=m_w_out, m_mlp_norm_g=m_mlp_norm_g, m_w_up=m_w_up, m_w_down=m_w_down, v_mix_norm_g=v_mix_norm_g, v_w_in=v_w_in, v_q_gain=v_q_gain, v_k_gain=v_k_gain, v_sinks=v_sinks, v_rel_bias=v_rel_bias, v_conv_w=v_conv_w, v_conv_b=v_conv_b, v_dt_bias=v_dt_bias, v_a_log=v_a_log, v_d_skip=v_d_skip, v_ssm_norm_g=v_ssm_norm_g, v_w_out=v_w_out, v_mlp_norm_g=v_mlp_norm_g, v_w_up=v_w_up, v_w_down=v_w_down)
    weights = {n: given[n] for n in TWIN_WEIGHTS}
    shared = {n: given[n] for n in SHARED_INPUTS}
    per_example = {n: given[n] for n in ['x']}
    grad_fn = _jax.value_and_grad(_loss, argnums=(0, 1))

    def one_microbatch(ex, loss_target):
        ex = dict(ex)
        diff = ex.pop(TWIN_DIFF_INPUT)
        return grad_fn(weights, diff, {**shared, **ex}, loss_target)

    if N_MICROBATCH == 1:
        loss, (grad_w, grad_x) = one_microbatch(per_example, given["loss_target"])
    else:
        def body(carry, xs):
            loss_sum, grad_sum = carry
            l_k, (gw_k, gx_k) = one_microbatch(xs[0], xs[1])
            with _jax.named_scope("update"):
                return (loss_sum + l_k, _jax.tree.map(_jnp.add, grad_sum, gw_k)), gx_k

        init = (_jnp.zeros((), _jnp.float32), _jax.tree.map(_jnp.zeros_like, weights))
        (loss, grad_w), grad_x = _jax.lax.scan(body, init, (per_example, given["loss_target"]))
    with _jax.named_scope("update"):
        delta_w, new_m, new_v = {}, {}, {}
        for n in TWIN_WEIGHTS:
            delta_w[n], new_m[n], new_v[n] = _adamw(weights[n], grad_w[n], given["m_" + n], given["v_" + n])
    return (loss, grad_x, *[grad_w[n] for n in TWIN_WEIGHTS], *[delta_w[n] for n in TWIN_WEIGHTS],
            *[new_m[n] for n in TWIN_WEIGHTS], *[new_v[n] for n in TWIN_WEIGHTS])
```

```python
import functools

import numpy as np
import jax
import jax.numpy as jnp
from jax import lax
from jax.experimental import pallas as pl
from jax.experimental.pallas import tpu as pltpu

f32 = jnp.float32
bf16 = jnp.bfloat16

SEQ = 2048
D_MODEL = 1024
DEPTH = 2
HEAD_DIM = 64
N_Q_HEADS = 8
N_KV_HEADS = 2
Q_PER_KV = N_Q_HEADS // N_KV_HEADS
BLOCK = 128
N_BLOCKS = SEQ // BLOCK
N_BUCKETS = 32
MAX_DISTANCE = 128
SSM_HEADS = 8
SSM_HEAD_DIM = 64
SSM_GROUPS = 2
HEADS_PER_GROUP = SSM_HEADS // SSM_GROUPS
SSM_STATE = 128
CONV_WIDTH = 4
CHUNK = 128
N_CHUNKS = SEQ // CHUNK
D_FF = 4 * D_MODEL
D_ATTN = N_Q_HEADS * HEAD_DIM
D_KV = N_KV_HEADS * HEAD_DIM
D_SSM = SSM_HEADS * SSM_HEAD_DIM
D_BC = SSM_GROUPS * SSM_STATE
D_CONV = D_SSM + 2 * D_BC
D_IN = D_ATTN + 2 * D_KV + D_SSM + D_CONV + SSM_HEADS
EPS = 1e-6
NEG = -1e30
N_CHIPS = 4
FF_TILE = D_FF // N_CHIPS

LANE = 128
PW = D_ATTN + D_SSM + D_CONV + 2 * D_KV + LANE
OFF_Q, OFF_Z, OFF_X, OFF_K, OFF_V, OFF_DT = 0, 512, 1024, 2048, 2176, 2304

ADAM_LR = 0.001
ADAM_B1 = 0.9
ADAM_B2 = 0.999
ADAM_EPS = 1e-08
ADAM_WD = 0.01
ADAM_STEP = 10

VMEM_LIMIT = 56 * 1024 * 1024


def _params(*sem):
    return pltpu.CompilerParams(dimension_semantics=tuple(sem), vmem_limit_bytes=VMEM_LIMIT)


def _bdot(a, b):
    return jnp.dot(a.astype(bf16), b.astype(bf16), preferred_element_type=f32)


def _bdot_nt(a, b):
    return lax.dot_general(a.astype(bf16), b.astype(bf16), (((1,), (1,)), ((), ())), preferred_element_type=f32)


def _bdot_tn(a, b):
    return lax.dot_general(a.astype(bf16), b.astype(bf16), (((0,), (0,)), ((), ())), preferred_element_type=f32)


def _hdot(a, b):
    return jnp.dot(a, b, precision=lax.Precision.HIGHEST, preferred_element_type=f32)


def _sigmoid(x):
    return 1.0 / (1.0 + jnp.exp(-x))


def _softplus(x):
    return jnp.maximum(x, 0.0) + jnp.log1p(jnp.exp(-jnp.abs(x)))


def _rms(x):
    return lax.rsqrt(jnp.mean(x * x, axis=-1, keepdims=True) + EPS)


def _rms_bwd(dy, xhat, r, g):
    t = dy * g
    return r * (t - xhat * jnp.mean(t * xhat, axis=-1, keepdims=True))


def _full(shape):
    return pl.BlockSpec(shape, lambda *_: (0,) * len(shape))


def _to_aligned(w):
    q, k, v, z, xbc, dt = jnp.split(w, [512, 640, 768, 1280, 2304], axis=-1)
    pad = jnp.zeros(w.shape[:-1] + (LANE - SSM_HEADS,), w.dtype)
    return jnp.concatenate([q, z, xbc, k, v, dt, pad], axis=-1)


def _from_aligned(w):
    q, z, xbc, k, v, dt = (w[..., OFF_Q:OFF_Z], w[..., OFF_Z:OFF_X], w[..., OFF_X:OFF_K], w[..., OFF_K:OFF_V],
                           w[..., OFF_V:OFF_DT], w[..., OFF_DT:OFF_DT + SSM_HEADS])
    return jnp.concatenate([q, k, v, z, xbc, dt], axis=-1)


def _bucket_table():
    qi = np.arange(BLOCK)[:, None]
    kj = np.arange(2 * BLOCK)[None, :]
    dist = qi + BLOCK - kj
    ok = (dist >= 0) & (dist < 128)
    d = np.clip(dist, 0, None)
    max_exact = N_BUCKETS // 2
    d_f = np.maximum(d, 1).astype(np.float32)
    large = max_exact + (np.log(d_f / np.float32(max_exact)) / np.float32(np.log(MAX_DISTANCE / max_exact))
                         * np.float32(N_BUCKETS - max_exact)).astype(np.int32)
    large = np.minimum(large, N_BUCKETS - 1)
    bucket = np.where(d < max_exact, d, large)
    return np.where(ok, bucket, -1).astype(np.int32)


def bias_build(rel_bias, bucket):
    def body(rel_ref, bkt_ref, o_ref):
        bkt = bkt_ref[...]
        for h in range(N_Q_HEADS):
            acc = jnp.where(bkt < 0, NEG, 0.0).astype(f32)
            for b in range(N_BUCKETS):
                acc = acc + jnp.where(bkt == b, rel_ref[b, h], 0.0)
            o_ref[h] = acc

    return pl.pallas_call(
        body, name="bias_build", out_shape=jax.ShapeDtypeStruct((N_Q_HEADS, BLOCK, 2 * BLOCK), f32),
        in_specs=[pl.BlockSpec(memory_space=pltpu.SMEM), pl.BlockSpec(memory_space=pltpu.VMEM)],
        out_specs=pl.BlockSpec(memory_space=pltpu.VMEM),
    )(rel_bias, bucket)


def bias_bwd(dband0, dband1, bucket):
    def body(d0_ref, d1_ref, bkt_ref, o_ref):
        bkt = bkt_ref[...]
        o_ref[...] = jnp.zeros_like(o_ref)
        for h in range(N_Q_HEADS):
            d = d0_ref[h] + d1_ref[h]
            for b in range(N_BUCKETS):
                part = jnp.sum(jnp.where(bkt == b, d, 0.0), axis=1, keepdims=True)
                o_ref[b:b + 1, h:h + 1] = jnp.sum(part, axis=0, keepdims=True)

    return pl.pallas_call(
        body, name="bias_bwd", out_shape=jax.ShapeDtypeStruct((N_BUCKETS, LANE), f32),
    )(dband0, dband1, bucket)


def in_fwd(x, g, w, tm=256):
    def body(x_ref, g_ref, w_ref, o_ref):
        xv = x_ref[...]
        h = xv * _rms(xv) * g_ref[...]
        o_ref[...] = _bdot(h, w_ref[...])

    return pl.pallas_call(
        body, name="in_fwd", grid=(SEQ // tm,),
        in_specs=[pl.BlockSpec((tm, D_MODEL), lambda i: (i, 0)), _full((1, D_MODEL)), _full((D_MODEL, PW))],
        out_specs=pl.BlockSpec((tm, PW), lambda i: (i, 0)),
        out_shape=jax.ShapeDtypeStruct((SEQ, PW), f32),
        compiler_params=_params("arbitrary"),
    )(x, g, w)


def in_bwd(dq, dz, dxbc, dk, dv, ddt, x, g, w, dres, tm=256):
    def body(dq_ref, dz_ref, dx_ref, dk_ref, dv_ref, ddt_ref, x_ref, g_ref, w_ref, dres_ref, o_ref, dw_ref, dg_ref):
        i = pl.program_id(0)

        @pl.when(i == 0)
        def _():
            dw_ref[...] = jnp.zeros_like(dw_ref)
            dg_ref[...] = jnp.zeros_like(dg_ref)

        dproj = jnp.concatenate([dq_ref[...], dz_ref[...], dx_ref[...], dk_ref[...], dv_ref[...], ddt_ref[...]],
                                axis=-1).astype(bf16)
        xv = x_ref[...]
        r = _rms(xv)
        xhat = xv * r
        gv = g_ref[...]
        h = xhat * gv
        dw_ref[...] += _bdot_tn(h, dproj)
        dh = _bdot_nt(dproj, w_ref[...])
        dg_ref[...] += jnp.sum(dh * xhat, axis=0, keepdims=True)
        o_ref[...] = dres_ref[...] + _rms_bwd(dh, xhat, r, gv)

    tok = lambda w_: pl.BlockSpec((tm, w_), lambda i: (i, 0))
    return pl.pallas_call(
        body, name="in_bwd", grid=(SEQ // tm,),
        in_specs=[tok(D_ATTN), tok(D_SSM), tok(D_CONV), tok(D_KV // 1), tok(D_KV // 1), tok(LANE), tok(D_MODEL),
                  _full((1, D_MODEL)), _full((D_MODEL, PW)), tok(D_MODEL)],
        out_specs=[tok(D_MODEL), _full((D_MODEL, PW)), _full((1, D_MODEL))],
        out_shape=[jax.ShapeDtypeStruct((SEQ, D_MODEL), f32), jax.ShapeDtypeStruct((D_MODEL, PW), f32),
                   jax.ShapeDtypeStruct((1, D_MODEL), f32)],
        compiler_params=_params("arbitrary"),
    )(dq, dz, dxbc, dk, dv, ddt, x, g, w, dres)


def _attn_probs(qn, kn, bias_h, sink, first, col):
    s = _bdot_nt(qn, kn) * (HEAD_DIM ** -0.5) + bias_h
    s = jnp.where(jnp.logical_and(first, col < BLOCK), NEG, s)
    m = jnp.maximum(jnp.max(s, axis=-1, keepdims=True), sink)
    p = jnp.exp(s - m)
    psink = jnp.exp(sink - m)
    inv = 1.0 / (jnp.sum(p, axis=-1, keepdims=True) + psink)
    return p * inv, psink * inv


def attn_fwd(proj, q_gain, k_gain, sinks, bias):
    kcol, vcol = OFF_K // D_KV, OFF_V // D_KV

    def body(q_ref, kc_ref, kp_ref, vc_ref, vp_ref, qg_ref, kg_ref, sink_ref, bias_ref, o_ref):
        n = pl.program_id(0)
        first = n == 0
        col = lax.broadcasted_iota(jnp.int32, (BLOCK, 2 * BLOCK), 1)
        k2 = jnp.concatenate([kp_ref[...], kc_ref[...]], axis=0)
        v2 = jnp.concatenate([vp_ref[...], vc_ref[...]], axis=0)
        qg, kg = qg_ref[...], kg_ref[...]
        for hk in range(N_KV_HEADS):
            kk = k2[:, hk * HEAD_DIM:(hk + 1) * HEAD_DIM]
            kn = (kk * _rms(kk) * kg).astype(bf16)
            vb = v2[:, hk * HEAD_DIM:(hk + 1) * HEAD_DIM].astype(bf16)
            for gq in range(Q_PER_KV):
                h = hk * Q_PER_KV + gq
                qq = q_ref[:, h * HEAD_DIM:(h + 1) * HEAD_DIM]
                qn = qq * _rms(qq) * qg
                p, _ = _attn_probs(qn, kn, bias_ref[h], sink_ref[h], first, col)
                o_ref[:, h * HEAD_DIM:(h + 1) * HEAD_DIM] = _bdot(p, vb)

    prev = lambda n: jnp.maximum(n - 1, 0)
    return pl.pallas_call(
        body, name="attn_fwd", grid=(N_BLOCKS,),
        in_specs=[pl.BlockSpec((BLOCK, D_ATTN), lambda n: (n, 0)),
                  pl.BlockSpec((BLOCK, D_KV), lambda n: (n, kcol)), pl.BlockSpec((BLOCK, D_KV), lambda n: (prev(n), kcol)),
                  pl.BlockSpec((BLOCK, D_KV), lambda n: (n, vcol)), pl.BlockSpec((BLOCK, D_KV), lambda n: (prev(n), vcol)),
                  _full((1, HEAD_DIM)), _full((1, HEAD_DIM)), pl.BlockSpec(memory_space=pltpu.SMEM),
                  _full((N_Q_HEADS, BLOCK, 2 * BLOCK))],
        out_specs=pl.BlockSpec((BLOCK, D_ATTN), lambda n: (n, 0)),
        out_shape=jax.ShapeDtypeStruct((SEQ, D_ATTN), f32),
        compiler_params=_params("arbitrary"),
    )(proj, proj, proj, proj, proj, q_gain, k_gain, sinks, bias)


def attn_bwd(proj, d_out, q_gain, k_gain, sinks, bias):
    kcol, vcol = OFF_K // D_KV, OFF_V // D_KV

    def body(q_ref, kc_ref, kp_ref, vc_ref, vp_ref, do_ref, qg_ref, kg_ref, sink_ref, bias_ref,
             dq_ref, dk_ref, dv_ref, dband_ref, dsink_ref, dqg_ref, dkg_ref, dkn_scr, dv_scr):
        i = pl.program_id(0)
        first = i == N_BLOCKS - 1

        @pl.when(i == 0)
        def _():
            for ref in (dband_ref, dsink_ref, dqg_ref, dkg_ref, dkn_scr, dv_scr):
                ref[...] = jnp.zeros_like(ref)

        col = lax.broadcasted_iota(jnp.int32, (BLOCK, 2 * BLOCK), 1)
        k2 = jnp.concatenate([kp_ref[...], kc_ref[...]], axis=0)
        v2 = jnp.concatenate([vp_ref[...], vc_ref[...]], axis=0)
        qg, kg = qg_ref[...], kg_ref[...]
        scale = HEAD_DIM ** -0.5
        for hk in range(N_KV_HEADS):
            sl = slice(hk * HEAD_DIM, (hk + 1) * HEAD_DIM)
            kk = k2[:, sl]
            rk = _rms(kk)
            khat = kk * rk
            kn = (khat * kg).astype(bf16)
            vb = v2[:, sl].astype(bf16)
            dkn = jnp.zeros((2 * BLOCK, HEAD_DIM), f32)
            dvv = jnp.zeros((2 * BLOCK, HEAD_DIM), f32)
            for gq in range(Q_PER_KV):
                h = hk * Q_PER_KV + gq
                hs = slice(h * HEAD_DIM, (h + 1) * HEAD_DIM)
                qq = q_ref[:, hs]
                rq = _rms(qq)
                qhat = qq * rq
                qn = qhat * qg
                p, psink = _attn_probs(qn, kn, bias_ref[h], sink_ref[h], first, col)
                d_o = do_ref[:, hs]
                dp = _bdot_nt(d_o, vb)
                delta = jnp.sum(p * dp, axis=-1, keepdims=True)
                ds = p * (dp - delta)
                dband_ref[h] += ds
                dsink_ref[:, h:h + 1] += -jnp.sum(psink * delta, axis=0, keepdims=True)
                dqn = _bdot(ds, kn) * scale
                dkn = dkn + _bdot_tn(ds, qn) * scale
                dvv = dvv + _bdot_tn(p, d_o)
                dqg_ref[...] += jnp.sum(dqn * qhat, axis=0, keepdims=True)
                dq_ref[:, hs] = _rms_bwd(dqn, qhat, rq, qg)
            dkn_cur = dkn[BLOCK:] + dkn_scr[:, sl]
            dkn_scr[:, sl] = dkn[:BLOCK]
            khat_c, rk_c = khat[BLOCK:], rk[BLOCK:]
            dkg_ref[...] += jnp.sum(dkn_cur * khat_c, axis=0, keepdims=True)
            dk_ref[:, sl] = _rms_bwd(dkn_cur, khat_c, rk_c, kg)
            dv_ref[:, sl] = dvv[BLOCK:] + dv_scr[:, sl]
            dv_scr[:, sl] = dvv[:BLOCK]

    blk = lambda i: N_BLOCKS - 1 - i
    prev = lambda i: jnp.maximum(N_BLOCKS - 2 - i, 0)
    return pl.pallas_call(
        body, name="attn_bwd", grid=(N_BLOCKS,),
        in_specs=[pl.BlockSpec((BLOCK, D_ATTN), lambda i: (blk(i), 0)),
                  pl.BlockSpec((BLOCK, D_KV), lambda i: (blk(i), kcol)), pl.BlockSpec((BLOCK, D_KV), lambda i: (prev(i), kcol)),
                  pl.BlockSpec((BLOCK, D_KV), lambda i: (blk(i), vcol)), pl.BlockSpec((BLOCK, D_KV), lambda i: (prev(i), vcol)),
                  pl.BlockSpec((BLOCK, D_ATTN), lambda i: (blk(i), 0)),
                  _full((1, HEAD_DIM)), _full((1, HEAD_DIM)), pl.BlockSpec(memory_space=pltpu.SMEM),
                  _full((N_Q_HEADS, BLOCK, 2 * BLOCK))],
        out_specs=[pl.BlockSpec((BLOCK, D_ATTN), lambda i: (blk(i), 0)), pl.BlockSpec((BLOCK, D_KV), lambda i: (blk(i), 0)),
                   pl.BlockSpec((BLOCK, D_KV), lambda i: (blk(i), 0)), _full((N_Q_HEADS, BLOCK, 2 * BLOCK)),
                   _full((1, LANE)), _full((1, HEAD_DIM)), _full((1, HEAD_DIM))],
        out_shape=[jax.ShapeDtypeStruct((SEQ, D_ATTN), f32), jax.ShapeDtypeStruct((SEQ, D_KV), f32),
                   jax.ShapeDtypeStruct((SEQ, D_KV), f32), jax.ShapeDtypeStruct((N_Q_HEADS, BLOCK, 2 * BLOCK), f32),
                   jax.ShapeDtypeStruct((1, LANE), f32), jax.ShapeDtypeStruct((1, HEAD_DIM), f32),
                   jax.ShapeDtypeStruct((1, HEAD_DIM), f32)],
        scratch_shapes=[pltpu.VMEM((BLOCK, D_KV), f32), pltpu.VMEM((BLOCK, D_KV), f32)],
        compiler_params=_params("arbitrary"),
    )(proj, proj, proj, proj, proj, d_out, q_gain, k_gain, sinks, bias)


def _shift_down(u, s, row):
    if s == 0:
        return u
    return jnp.where(row >= s, pltpu.roll(u, s, 0), 0.0)


def _shift_up(u, s, row):
    if s == 0:
        return u
    return jnp.where(row < SEQ - s, pltpu.roll(u, SEQ - s, 0), 0.0)


def conv_fwd(proj, conv_w, conv_b):
    xcol = OFF_X // LANE

    def body(u_ref, w_ref, b_ref, o_ref):
        u = u_ref[...]
        row = lax.broadcasted_iota(jnp.int32, u.shape, 0)
        pre = b_ref[...] + jnp.zeros_like(u)
        for k in range(CONV_WIDTH):
            pre = pre + w_ref[k:k + 1, :] * _shift_down(u, CONV_WIDTH - 1 - k, row)
        o_ref[...] = pre * _sigmoid(pre)

    return pl.pallas_call(
        body, name="conv_fwd", grid=(D_CONV // LANE,),
        in_specs=[pl.BlockSpec((SEQ, LANE), lambda j: (0, xcol + j)), pl.BlockSpec((CONV_WIDTH, LANE), lambda j: (0, j)),
                  pl.BlockSpec((1, LANE), lambda j: (0, j))],
        out_specs=pl.BlockSpec((SEQ, LANE), lambda j: (0, j)),
        out_shape=jax.ShapeDtypeStruct((SEQ, D_CONV), f32),
        compiler_params=_params("arbitrary"),
    )(proj, conv_w, conv_b)


def conv_bwd(proj, d_act, conv_w, conv_b):
    xcol = OFF_X // LANE

    def body(u_ref, da_ref, w_ref, b_ref, du_ref, dw_ref, db_ref):
        u = u_ref[...]
        row = lax.broadcasted_iota(jnp.int32, u.shape, 0)
        shifted = [_shift_down(u, CONV_WIDTH - 1 - k, row) for k in range(CONV_WIDTH)]
        pre = b_ref[...] + jnp.zeros_like(u)
        for k in range(CONV_WIDTH):
            pre = pre + w_ref[k:k + 1, :] * shifted[k]
        sg = _sigmoid(pre)
        dpre = da_ref[...] * (sg * (1.0 + pre * (1.0 - sg)))
        db_ref[...] = jnp.sum(dpre, axis=0, keepdims=True)
        du = jnp.zeros_like(u)
        for k in range(CONV_WIDTH):
            dw_ref[k:k + 1, :] = jnp.sum(dpre * shifted[k], axis=0, keepdims=True)
            du = du + w_ref[k:k + 1, :] * _shift_up(dpre, CONV_WIDTH - 1 - k, row)
        du_ref[...] = du

    return pl.pallas_call(
        body, name="conv_bwd", grid=(D_CONV // LANE,),
        in_specs=[pl.BlockSpec((SEQ, LANE), lambda j: (0, xcol + j)), pl.BlockSpec((SEQ, LANE), lambda j: (0, j)),
                  pl.BlockSpec((CONV_WIDTH, LANE), lambda j: (0, j)), pl.BlockSpec((1, LANE), lambda j: (0, j))],
        out_specs=[pl.BlockSpec((SEQ, LANE), lambda j: (0, j)), pl.BlockSpec((CONV_WIDTH, LANE), lambda j: (0, j)),
                   pl.BlockSpec((1, LANE), lambda j: (0, j))],
        out_shape=[jax.ShapeDtypeStruct((SEQ, D_CONV), f32), jax.ShapeDtypeStruct((CONV_WIDTH, D_CONV), f32),
                   jax.ShapeDtypeStruct((1, D_CONV), f32)],
        compiler_params=_params("arbitrary"),
    )(proj, d_act, conv_w, conv_b)


def _ssd_chunk_common(dt_raw, dtb, alog):
    row = lax.broadcasted_iota(jnp.int32, (CHUNK, CHUNK), 0)
    col = lax.broadcasted_iota(jnp.int32, (CHUNK, CHUNK), 1)
    tri = (row >= col).astype(f32)
    strict = (row > col).astype(f32)
    dtp = _softplus(dt_raw + dtb)
    a_row = -jnp.exp(alog)
    d_a = dtp * a_row
    cs = _hdot(tri, d_a)
    cs_last = cs[CHUNK - 1:CHUNK, :]
    return row, col, tri, strict, dtp, a_row, d_a, cs, cs_last


def _seg_decay(tri, strict, d_a_h, row, col):
    seg = _hdot(tri, d_a_h * strict)
    return jnp.where(row >= col, jnp.exp(seg), 0.0)


def ssd_fwd(act, proj, dt_bias, a_log, d_skip, norm_g):
    zcol, dtcol = OFF_Z // D_SSM, OFF_DT // LANE
    gw = D_SSM // SSM_GROUPS

    def body(act_ref, z_ref, dt_ref, dtb_ref, alog_ref, dsk_ref, ng_ref, out_ref, ypre_ref, st_ref, state, ybuf):
        c = pl.program_id(0)

        @pl.when(c == 0)
        def _():
            state[...] = jnp.zeros_like(state)

        row, col, tri, strict, dtp, a_row, d_a, cs, cs_last = _ssd_chunk_common(dt_ref[...], dtb_ref[...], alog_ref[...])
        e_cs = jnp.exp(cs)
        dte = jnp.exp(cs_last - cs)
        ecl = jnp.exp(cs_last)
        dsk = dsk_ref[...]
        for g in range(SSM_GROUPS):
            bg = act_ref[:, D_SSM + g * SSM_STATE:D_SSM + (g + 1) * SSM_STATE]
            cg = act_ref[:, D_SSM + D_BC + g * SSM_STATE:D_SSM + D_BC + (g + 1) * SSM_STATE]
            cb = _bdot_nt(cg, bg)
            for r in range(HEADS_PER_GROUP):
                hd = g * HEADS_PER_GROUP + r
                hs = slice(hd * SSM_HEAD_DIM, (hd + 1) * SSM_HEAD_DIM)
                hl = slice(hd, hd + 1)
                x_h = act_ref[:, hs]
                xdt = x_h * dtp[:, hl]
                lm = _seg_decay(tri, strict, d_a[:, hl], row, col)
                prev = state[hd]
                st_ref[0, hd] = prev
                y = _bdot(cb * lm, xdt) + e_cs[:, hl] * _bdot(cg, prev) + x_h * dsk[:, hl]
                ybuf[:, hs] = y
                state[hd] = prev * ecl[:, hl] + _bdot_tn(bg, xdt * dte[:, hl])
        y = ybuf[...]
        ypre_ref[...] = y
        z = z_ref[...]
        yz = y * (z * _sigmoid(z))
        ng = ng_ref[...]
        for g in range(SSM_GROUPS):
            gs = slice(g * gw, (g + 1) * gw)
            part = yz[:, gs]
            out_ref[:, gs] = part * _rms(part) * ng[:, gs]

    return pl.pallas_call(
        body, name="ssd_fwd", grid=(N_CHUNKS,),
        in_specs=[pl.BlockSpec((CHUNK, D_CONV), lambda c: (c, 0)), pl.BlockSpec((CHUNK, D_SSM), lambda c: (c, zcol)),
                  pl.BlockSpec((CHUNK, LANE), lambda c: (c, dtcol)), _full((1, LANE)), _full((1, LANE)), _full((1, LANE)),
                  _full((1, D_SSM))],
        out_specs=[pl.BlockSpec((CHUNK, D_SSM), lambda c: (c, 0)), pl.BlockSpec((CHUNK, D_SSM), lambda c: (c, 0)),
                   pl.BlockSpec((1, SSM_HEADS, SSM_STATE, SSM_HEAD_DIM), lambda c: (c, 0, 0, 0))],
        out_shape=[jax.ShapeDtypeStruct((SEQ, D_SSM), f32), jax.ShapeDtypeStruct((SEQ, D_SSM), f32),
                   jax.ShapeDtypeStruct((N_CHUNKS, SSM_HEADS, SSM_STATE, SSM_HEAD_DIM), f32)],
        scratch_shapes=[pltpu.VMEM((SSM_HEADS, SSM_STATE, SSM_HEAD_DIM), f32), pltpu.VMEM((CHUNK, D_SSM), f32)],
        compiler_params=_params("arbitrary"),
    )(act, proj, proj, dt_bias, a_log, d_skip, norm_g)


def ssd_bwd(act, proj, ypre, states, d_out, dt_bias, a_log, d_skip, norm_g):
    zcol, dtcol = OFF_Z // D_SSM, OFF_DT // LANE
    gw = D_SSM // SSM_GROUPS

    def body(act_ref, z_ref, dt_ref, ypre_ref, st_ref, do_ref, dtb_ref, alog_ref, dsk_ref, ng_ref,
             dact_ref, ddt_ref, dz_ref, dng_ref, dpar_ref, dstate, dybuf):
        i = pl.program_id(0)

        @pl.when(i == 0)
        def _():
            for ref in (dng_ref, dpar_ref, dstate):
                ref[...] = jnp.zeros_like(ref)

        y = ypre_ref[...]
        z = z_ref[...]
        sgz = _sigmoid(z)
        sz = z * sgz
        yz = y * sz
        ng = ng_ref[...]
        d_o = do_ref[...]
        for g in range(SSM_GROUPS):
            gs = slice(g * gw, (g + 1) * gw)
            part = yz[:, gs]
            r = _rms(part)
            yhat = part * r
            dng_ref[:, gs] += jnp.sum(d_o[:, gs] * yhat, axis=0, keepdims=True)
            dyz = _rms_bwd(d_o[:, gs], yhat, r, ng[:, gs])
            dybuf[:, gs] = dyz * sz[:, gs]
            dz_ref[:, gs] = dyz * y[:, gs] * (sgz[:, gs] * (1.0 + z[:, gs] * (1.0 - sgz[:, gs])))

        row, col, tri, strict, dtp, a_row, d_a, cs, cs_last = _ssd_chunk_common(dt_ref[...], dtb_ref[...], alog_ref[...])
        upper = (row <= col).astype(f32)
        lane = lax.broadcasted_iota(jnp.int32, (CHUNK, LANE), 1)
        lane1 = lax.broadcasted_iota(jnp.int32, (1, LANE), 1)
        e_cs = jnp.exp(cs)
        dte = jnp.exp(cs_last - cs)
        ecl = jnp.exp(cs_last)
        dsk = dsk_ref[...]
        ddt_mat = jnp.zeros((CHUNK, LANE), f32)
        dcs_mat = jnp.zeros((CHUNK, LANE), f32)
        dda_mat = jnp.zeros((CHUNK, LANE), f32)
        dcsl_row = jnp.zeros((1, LANE), f32)
        dd_row = jnp.zeros((1, LANE), f32)
        for g in range(SSM_GROUPS):
            bsl = slice(D_SSM + g * SSM_STATE, D_SSM + (g + 1) * SSM_STATE)
            csl = slice(D_SSM + D_BC + g * SSM_STATE, D_SSM + D_BC + (g + 1) * SSM_STATE)
            bg = act_ref[:, bsl]
            cg = act_ref[:, csl]
            cb = _bdot_nt(cg, bg)
            dcb = jnp.zeros((CHUNK, CHUNK), f32)
            dbg = jnp.zeros((CHUNK, SSM_STATE), f32)
            dcg = jnp.zeros((CHUNK, SSM_STATE), f32)
            for rr in range(HEADS_PER_GROUP):
                hd = g * HEADS_PER_GROUP + rr
                hs = slice(hd * SSM_HEAD_DIM, (hd + 1) * SSM_HEAD_DIM)
                hl = slice(hd, hd + 1)
                x_h = act_ref[:, hs]
                dt_h = dtp[:, hl]
                e_h = e_cs[:, hl]
                dte_h = dte[:, hl]
                ecl_h = ecl[:, hl]
                xdt = x_h * dt_h
                lm = _seg_decay(tri, strict, d_a[:, hl], row, col)
                m = cb * lm
                prev = st_ref[0, hd]
                dy = dybuf[:, hs]
                dh = dstate[hd]
                dd_row = dd_row + jnp.where(lane1 == hd, jnp.sum(jnp.sum(dy * x_h, axis=1, keepdims=True), axis=0, keepdims=True), 0.0)
                dx = dy * dsk[:, hl]
                gmat = _bdot(cg, prev)
                dg = dy * e_h
                dcg = dcg + _bdot_nt(dg, prev)
                dprev = _bdot_tn(cg, dg)
                dcs_h = jnp.sum(dy * gmat, axis=1, keepdims=True) * e_h
                dm = _bdot_nt(dy, xdt)
                dxdt = _bdot_tn(m, dy)
                dcb = dcb + dm * lm
                dseg = dm * m
                dda_h = jnp.sum(_hdot(upper, dseg) * strict, axis=1, keepdims=True)
                wmat = xdt * dte_h
                dbg = dbg + _bdot_nt(wmat, dh)
                dw = _bdot(bg, dh)
                dxdt = dxdt + dw * dte_h
                ddte = jnp.sum(dw * xdt, axis=1, keepdims=True) * dte_h
                dcs_h = dcs_h - ddte
                dcsl = jnp.sum(ddte, axis=0, keepdims=True)
                dcsl = dcsl + jnp.sum(jnp.sum(dh * prev, axis=1, keepdims=True), axis=0, keepdims=True) * ecl_h
                dstate[hd] = dprev + dh * ecl_h
                dact_ref[:, hs] = dx + dxdt * dt_h
                ddt_h = jnp.sum(dxdt * x_h, axis=1, keepdims=True)
                ddt_mat = jnp.where(lane == hd, ddt_h, ddt_mat)
                dcs_mat = jnp.where(lane == hd, dcs_h, dcs_mat)
                dda_mat = jnp.where(lane == hd, dda_h, dda_mat)
                dcsl_row = jnp.where(lane1 == hd, dcsl, dcsl_row)
            dact_ref[:, bsl] = dbg + _bdot_tn(dcb, cg)
            dact_ref[:, csl] = dcg + _bdot(dcb, bg)
        rowl = lax.broadcasted_iota(jnp.int32, (CHUNK, LANE), 0)
        dcs_mat = dcs_mat + jnp.where(rowl == CHUNK - 1, dcsl_row, 0.0)
        dda = dda_mat + _hdot(upper, dcs_mat)
        ddt_mat = ddt_mat + dda * a_row
        da_row = jnp.sum(dda * dtp, axis=0, keepdims=True)
        ddt_raw = ddt_mat * _sigmoid(dt_ref[...] + dtb_ref[...])
        ddt_ref[...] = ddt_raw
        dpar_ref[0:1, :] += jnp.sum(ddt_raw, axis=0, keepdims=True)
        dpar_ref[1:2, :] += da_row * a_row
        dpar_ref[2:3, :] += dd_row

    blk = lambda i: N_CHUNKS - 1 - i
    return pl.pallas_call(
        body, name="ssd_bwd", grid=(N_CHUNKS,),
        in_specs=[pl.BlockSpec((CHUNK, D_CONV), lambda i: (blk(i), 0)), pl.BlockSpec((CHUNK, D_SSM), lambda i: (blk(i), zcol)),
                  pl.BlockSpec((CHUNK, LANE), lambda i: (blk(i), dtcol)), pl.BlockSpec((CHUNK, D_SSM), lambda i: (blk(i), 0)),
                  pl.BlockSpec((1, SSM_HEADS, SSM_STATE, SSM_HEAD_DIM), lambda i: (blk(i), 0, 0, 0)),
                  pl.BlockSpec((CHUNK, D_SSM), lambda i: (blk(i), 0)),
                  _full((1, LANE)), _full((1, LANE)), _full((1, LANE)), _full((1, D_SSM))],
        out_specs=[pl.BlockSpec((CHUNK, D_CONV), lambda i: (blk(i), 0)), pl.BlockSpec((CHUNK, LANE), lambda i: (blk(i), 0)),
                   pl.BlockSpec((CHUNK, D_SSM), lambda i: (blk(i), 0)), _full((1, D_SSM)), _full((8, LANE))],
        out_shape=[jax.ShapeDtypeStruct((SEQ, D_CONV), f32), jax.ShapeDtypeStruct((SEQ, LANE), f32),
                   jax.ShapeDtypeStruct((SEQ, D_SSM), f32), jax.ShapeDtypeStruct((1, D_SSM), f32),
                   jax.ShapeDtypeStruct((8, LANE), f32)],
        scratch_shapes=[pltpu.VMEM((SSM_HEADS, SSM_STATE, SSM_HEAD_DIM), f32), pltpu.VMEM((CHUNK, D_SSM), f32)],
        compiler_params=_params("arbitrary"),
    )(act, proj, proj, ypre, states, d_out, dt_bias, a_log, d_skip, norm_g)


def out_fwd(x, attn, ssm, w_out, tm=512):
    def body(x_ref, a_ref, s_ref, w_ref, o_ref):
        o_ref[...] = x_ref[...] + _bdot(a_ref[...], w_ref[:D_ATTN, :]) + _bdot(s_ref[...], w_ref[D_ATTN:, :])

    tok = lambda w_: pl.BlockSpec((tm, w_), lambda i: (i, 0))
    return pl.pallas_call(
        body, name="out_fwd", grid=(SEQ // tm,),
        in_specs=[tok(D_MODEL), tok(D_ATTN), tok(D_SSM), _full((D_MODEL, D_MODEL))],
        out_specs=tok(D_MODEL), out_shape=jax.ShapeDtypeStruct((SEQ, D_MODEL), f32),
        compiler_params=_params("arbitrary"),
    )(x, attn, ssm, w_out)


def out_bwd(dx1, attn, ssm, w_out, tm=512):
    def body(d_ref, a_ref, s_ref, w_ref, da_ref, ds_ref, dw_ref):
        @pl.when(pl.program_id(0) == 0)
        def _():
            dw_ref[...] = jnp.zeros_like(dw_ref)

        d = d_ref[...].astype(bf16)
        dcat = _bdot_nt(d, w_ref[...])
        da_ref[...] = dcat[:, :D_ATTN]
        ds_ref[...] = dcat[:, D_ATTN:]
        dw_ref[:D_ATTN, :] += _bdot_tn(a_ref[...], d)
        dw_ref[D_ATTN:, :] += _bdot_tn(s_ref[...], d)

    tok = lambda w_: pl.BlockSpec((tm, w_), lambda i: (i, 0))
    return pl.pallas_call(
        body, name="out_bwd", grid=(SEQ // tm,),
        in_specs=[tok(D_MODEL), tok(D_ATTN), tok(D_SSM), _full((D_MODEL, D_MODEL))],
        out_specs=[tok(D_ATTN), tok(D_SSM), _full((D_MODEL, D_MODEL))],
        out_shape=[jax.ShapeDtypeStruct((SEQ, D_ATTN), f32), jax.ShapeDtypeStruct((SEQ, D_SSM), f32),
                   jax.ShapeDtypeStruct((D_MODEL, D_MODEL), f32)],
        compiler_params=_params("arbitrary"),
    )(dx1, attn, ssm, w_out)


def mlp_fwd(x1, g, w_up, w_down, layer, tm=512):
    def body(x_ref, g_ref, wu_ref, wd_ref, o_ref, u_ref, h_scr):
        j = pl.program_id(1)

        @pl.when(j == 0)
        def _():
            xv = x_ref[...]
            h_scr[...] = (xv * _rms(xv) * g_ref[...]).astype(bf16)
            o_ref[...] = xv

        u = jnp.dot(h_scr[...], wu_ref[...], preferred_element_type=f32)
        u_ref[...] = u
        a = jnp.square(jnp.maximum(u, 0.0))
        o_ref[...] += _bdot(a, wd_ref[...])

    return pl.pallas_call(
        body, name="mlp_fwd", grid=(SEQ // tm, N_CHIPS),
        in_specs=[pl.BlockSpec((tm, D_MODEL), lambda i, j: (i, 0)), _full((1, D_MODEL)),
                  pl.BlockSpec((None, None, D_MODEL, FF_TILE), lambda i, j: (j, layer, 0, 0)),
                  pl.BlockSpec((None, None, FF_TILE, D_MODEL), lambda i, j: (j, layer, 0, 0))],
        out_specs=[pl.BlockSpec((tm, D_MODEL), lambda i, j: (i, 0)), pl.BlockSpec((tm, FF_TILE), lambda i, j: (i, j))],
        out_shape=[jax.ShapeDtypeStruct((SEQ, D_MODEL), f32), jax.ShapeDtypeStruct((SEQ, D_FF), f32)],
        scratch_shapes=[pltpu.VMEM((tm, D_MODEL), bf16)],
        compiler_params=_params("arbitrary", "arbitrary"),
    )(x1, g, w_up, w_down)


def mlp_bwd_data(dx2, u, x1, g, w_up, w_down, layer, tm=512):
    def body(d_ref, u_ref, x_ref, g_ref, wu_ref, wd_ref, dx_ref, du_ref, dg_ref, dh_scr):
        i, j = pl.program_id(0), pl.program_id(1)

        @pl.when(jnp.logical_and(i == 0, j == 0))
        def _():
            dg_ref[...] = jnp.zeros_like(dg_ref)

        @pl.when(j == 0)
        def _():
            dh_scr[...] = jnp.zeros_like(dh_scr)

        da = _bdot_nt(d_ref[...], wd_ref[...])
        du = (da * (2.0 * jnp.maximum(u_ref[...], 0.0))).astype(bf16)
        du_ref[...] = du
        dh_scr[...] += _bdot_nt(du, wu_ref[...])

        @pl.when(j == N_CHIPS - 1)
        def _():
            xv = x_ref[...]
            r = _rms(xv)
            xhat = xv * r
            dh = dh_scr[...]
            dg_ref[...] += jnp.sum(dh * xhat, axis=0, keepdims=True)
            dx_ref[...] = d_ref[...] + _rms_bwd(dh, xhat, r, g_ref[...])

    return pl.pallas_call(
        body, name="mlp_bwd_data", grid=(SEQ // tm, N_CHIPS),
        in_specs=[pl.BlockSpec((tm, D_MODEL), lambda i, j: (i, 0)), pl.BlockSpec((tm, FF_TILE), lambda i, j: (i, j)),
                  pl.BlockSpec((tm, D_MODEL), lambda i, j: (i, 0)), _full((1, D_MODEL)),
                  pl.BlockSpec((None, None, D_MODEL, FF_TILE), lambda i, j: (j, layer, 0, 0)),
                  pl.BlockSpec((None, None, FF_TILE, D_MODEL), lambda i, j: (j, layer, 0, 0))],
        out_specs=[pl.BlockSpec((tm, D_MODEL), lambda i, j: (i, 0)), pl.BlockSpec((tm, FF_TILE), lambda i, j: (i, j)),
                   _full((1, D_MODEL))],
        out_shape=[jax.ShapeDtypeStruct((SEQ, D_MODEL), f32), jax.ShapeDtypeStruct((SEQ, D_FF), bf16),
                   jax.ShapeDtypeStruct((1, D_MODEL), f32)],
        scratch_shapes=[pltpu.VMEM((tm, D_MODEL), f32)],
        compiler_params=_params("arbitrary", "arbitrary"),
    )(dx2, u, x1, g, w_up, w_down)


def mlp_bwd_weights(dx2, u, du, x1, g, tm=512):
    def body(d_ref, u_ref, du_ref, x_ref, g_ref, dwu_ref, dwd_ref):
        @pl.when(pl.program_id(1) == 0)
        def _():
            dwu_ref[...] = jnp.zeros_like(dwu_ref)
            dwd_ref[...] = jnp.zeros_like(dwd_ref)

        xv = x_ref[...]
        h = xv * _rms(xv) * g_ref[...]
        dwu_ref[...] += _bdot_tn(h, du_ref[...])
        a = jnp.square(jnp.maximum(u_ref[...], 0.0))
        dwd_ref[...] += _bdot_tn(a, d_ref[...])

    return pl.pallas_call(
        body, name="mlp_bwd_weights", grid=(N_CHIPS, SEQ // tm),
        in_specs=[pl.BlockSpec((tm, D_MODEL), lambda j, i: (i, 0)), pl.BlockSpec((tm, FF_TILE), lambda j, i: (i, j)),
                  pl.BlockSpec((tm, FF_TILE), lambda j, i: (i, j)), pl.BlockSpec((tm, D_MODEL), lambda j, i: (i, 0)),
                  _full((1, D_MODEL))],
        out_specs=[pl.BlockSpec((None, D_MODEL, FF_TILE), lambda j, i: (j, 0, 0)),
                   pl.BlockSpec((None, FF_TILE, D_MODEL), lambda j, i: (j, 0, 0))],
        out_shape=[jax.ShapeDtypeStruct((N_CHIPS, D_MODEL, FF_TILE), f32), jax.ShapeDtypeStruct((N_CHIPS, FF_TILE, D_MODEL), f32)],
        compiler_params=_params("arbitrary", "arbitrary"),
    )(dx2, u, du, x1, g)


def loss_head(y, target, tm=512):
    def body(y_ref, t_ref, dy_ref, l_ref):
        @pl.when(pl.program_id(0) == 0)
        def _():
            l_ref[...] = jnp.zeros_like(l_ref)

        d = y_ref[...] - t_ref[...]
        dy_ref[...] = d * (1.0 / D_MODEL)
        part = jnp.sum(jnp.mean(d * d, axis=-1, keepdims=True), axis=0, keepdims=True)
        l_ref[...] += 0.5 * part

    tok = pl.BlockSpec((tm, D_MODEL), lambda i: (i, 0))
    return pl.pallas_call(
        body, name="loss_head", grid=(SEQ // tm,), in_specs=[tok, tok], out_specs=[tok, _full((1, 1))],
        out_shape=[jax.ShapeDtypeStruct((SEQ, D_MODEL), f32), jax.ShapeDtypeStruct((1, 1), f32)],
        compiler_params=_params("arbitrary"),
    )(y, target)


def _pad_lane(v):
    return jnp.pad(v, (0, LANE - v.shape[0]))[None, :]


def local_step(x, target, w):
    bucket = jnp.asarray(_bucket_table())
    bias = bias_build(w["rel_bias"], bucket)
    saved = []
    for l in range(DEPTH):
        g_mix = w["mix_norm_g"][l][None, :]
        proj = in_fwd(x, g_mix, w["w_in"][l])
        qg, kg = w["q_gain"][l][None, :], w["k_gain"][l][None, :]
        attn = attn_fwd(proj, qg, kg, w["sinks"][l], bias)
        conv_b = w["conv_b"][l][None, :]
        act = conv_fwd(proj, w["conv_w"][l], conv_b)
        dtb, alog, dsk = _pad_lane(w["dt_bias"][l]), _pad_lane(w["a_log"][l]), _pad_lane(w["d_skip"][l])
        ng = w["ssm_norm_g"][l][None, :]
        ssm, ypre, states = ssd_fwd(act, proj, dtb, alog, dsk, ng)
        x1 = out_fwd(x, attn, ssm, w["w_out"][l])
        g_mlp = w["mlp_norm_g"][l][None, :]
        x2, u = mlp_fwd(x1, g_mlp, w["w_up"], w["w_down"], l)
        saved.append(dict(x=x, proj=proj, attn=attn, act=act, ssm=ssm, ypre=ypre, states=states, x1=x1, u=u,
                          g_mix=g_mix, qg=qg, kg=kg, conv_b=conv_b, dtb=dtb, alog=alog, dsk=dsk, ng=ng, g_mlp=g_mlp))
        x = x2
    dx, loss = loss_head(x, target)
    grads = [None] * DEPTH
    dbands = [None] * DEPTH
    for l in reversed(range(DEPTH)):
        s = saved[l]
        dx1, du, dg_mlp = mlp_bwd_data(dx, s["u"], s["x1"], s["g_mlp"], w["w_up"], w["w_down"], l)
        dw_up, dw_down = mlp_bwd_weights(dx, s["u"], du, s["x1"], s["g_mlp"])
        dattn, dssm, dw_out = out_bwd(dx1, s["attn"], s["ssm"], w["w_out"][l])
        dact, ddt, dz, dng, dpar = ssd_bwd(s["act"], s["proj"], s["ypre"], s["states"], dssm, s["dtb"], s["alog"], s["dsk"], s["ng"])
        dxbc, dconv_w, dconv_b = conv_bwd(s["proj"], dact, w["conv_w"][l], s["conv_b"])
        dq, dk, dv, dband, dsink, dqg, dkg = attn_bwd(s["proj"], dattn, s["qg"], s["kg"], w["sinks"][l], bias)
        dx, dw_in, dg_mix = in_bwd(dq, dz, dxbc, dk, dv, ddt, s["x"], s["g_mix"], w["w_in"][l], dx1)
        dbands[l] = dband
        grads[l] = dict(mix_norm_g=dg_mix[0], w_in=dw_in, q_gain=dqg[0], k_gain=dkg[0], sinks=dsink[0, :N_Q_HEADS],
                        conv_w=dconv_w, conv_b=dconv_b[0], dt_bias=dpar[0, :SSM_HEADS], a_log=dpar[1, :SSM_HEADS],
                        d_skip=dpar[2, :SSM_HEADS], ssm_norm_g=dng[0], w_out=dw_out, mlp_norm_g=dg_mlp[0],
                        w_up=dw_up, w_down=dw_down)
    big = ("w_in", "w_out", "w_up", "w_down")
    out = {k: [grads[l][k] for l in range(DEPTH)] if k in big else jnp.stack([grads[l][k] for l in range(DEPTH)])
           for k in grads[0]}
    out["rel_bias"] = bias_bwd(dbands[0], dbands[1], bucket)[:, :N_Q_HEADS]
    return loss, dx, out


MESH = pl.DeviceIdType.MESH
HBM = pl.BlockSpec(memory_space=pltpu.HBM)
N_PEER_CHIPS = N_CHIPS - 1
N_DEVICES = 8


def _coords():
    return lax.axis_index("x"), lax.axis_index("y"), lax.axis_index("c")


def _peer_chips(x, y):
    return [(1 - x, y), (x, 1 - y), (1 - x, 1 - y)]


def _remote(src, dst, send_sem, recv_sem, device):
    return pltpu.make_async_remote_copy(src_ref=src, dst_ref=dst, send_sem=send_sem, recv_sem=recv_sem,
                                        device_id=device, device_id_type=MESH)


def all_gather_weights(shards, conv):
    n = len(shards)

    def body(*refs):
        srcs, conv_ref = refs[:n], refs[n]
        dsts, gconv = refs[n + 1:2 * n + 1], refs[2 * n + 1]
        lsem, ssem, rsem, fsem, frsem = refs[2 * n + 2:]
        x, y, c = _coords()
        k_me = 2 * x + y
        chips = _peer_chips(x, y)
        sibling = (x, y, 1 - c)

        def half(w, k, cc):
            hr = shards[w].shape[1] // 2
            return dsts[w].at[k, :, pl.ds(cc * hr, hr), :]

        def my_half(w):
            hr = shards[w].shape[1] // 2
            return srcs[w].at[:, pl.ds(c * hr, hr), :]

        local = [pltpu.make_async_copy(srcs[w], dsts[w].at[k_me], lsem.at[w]) for w in range(n)]
        local.append(pltpu.make_async_copy(conv_ref, gconv.at[k_me], lsem.at[n]))
        for cp in local:
            cp.start()
        sends = []
        for j, chip in enumerate(chips):
            for w in range(n):
                sends.append(_remote(my_half(w), half(w, k_me, c), ssem.at[w * 3 + j], rsem.at[w * 3 + j], (*chip, c)))
            sends.append(_remote(conv_ref, gconv.at[k_me], ssem.at[n * 3 + j], rsem.at[n * 3 + j], (*chip, c)))
        for cp in sends:
            cp.start()
        passed = []
        for j, chip in enumerate(chips):
            kj = 2 * chip[0] + chip[1]
            for w in range(n):
                got = half(w, kj, c)
                _remote(got, got, ssem.at[w * 3 + j], rsem.at[w * 3 + j], (*chip, c)).wait_recv()
                fw = _remote(got, got, fsem.at[w * 3 + j], frsem.at[w * 3 + j], sibling)
                fw.start()
                passed.append(fw)
            _remote(gconv.at[kj], gconv.at[kj], ssem.at[n * 3 + j], rsem.at[n * 3 + j], (*chip, c)).wait_recv()
        for j, chip in enumerate(chips):
            kj = 2 * chip[0] + chip[1]
            for w in range(n):
                got = half(w, kj, 1 - c)
                _remote(got, got, fsem.at[w * 3 + j], frsem.at[w * 3 + j], sibling).wait_recv()
        for cp in sends + passed:
            cp.wait_send()
        for cp in local:
            cp.wait()

    out_shape = [jax.ShapeDtypeStruct((N_CHIPS,) + s.shape, s.dtype) for s in shards]
    out_shape.append(jax.ShapeDtypeStruct((N_CHIPS,) + conv.shape, conv.dtype))
    n_ici = (n + 1) * N_PEER_CHIPS
    return pl.pallas_call(
        body, name="all_gather_weights", out_shape=out_shape,
        in_specs=[HBM] * (n + 1), out_specs=[HBM] * (n + 1),
        scratch_shapes=[pltpu.SemaphoreType.DMA((n + 1,)), pltpu.SemaphoreType.DMA((n_ici,)), pltpu.SemaphoreType.DMA((n_ici,)),
                        pltpu.SemaphoreType.DMA((n * N_PEER_CHIPS,)), pltpu.SemaphoreType.DMA((n * N_PEER_CHIPS,))],
    )(*shards, conv)


def small_all_reduce(vec):
    def body(v_ref, o_ref, gat, ssem, rsem):
        x, y, c = _coords()
        me = 4 * x + 2 * y + c
        gat[me] = v_ref[...]
        sends = []
        for t in range(1, N_DEVICES):
            peer = (x ^ (t >> 2), y ^ ((t >> 1) & 1), c ^ (t & 1))
            cp = _remote(v_ref, gat.at[me], ssem.at[t - 1], rsem.at[t - 1], peer)
            cp.start()
            sends.append(cp)
        for t in range(1, N_DEVICES):
            peer = (x ^ (t >> 2), y ^ ((t >> 1) & 1), c ^ (t & 1))
            slot = gat.at[4 * peer[0] + 2 * peer[1] + peer[2]]
            _remote(slot, slot, ssem.at[t - 1], rsem.at[t - 1], peer).wait_recv()
        for cp in sends:
            cp.wait_send()
        acc = gat[0]
        for d in range(1, N_DEVICES):
            acc = acc + gat[d]
        o_ref[...] = acc

    return pl.pallas_call(
        body, name="small_all_reduce", out_shape=jax.ShapeDtypeStruct(vec.shape, vec.dtype),
        in_specs=[pl.BlockSpec(memory_space=pltpu.VMEM)], out_specs=pl.BlockSpec(memory_space=pltpu.VMEM),
        scratch_shapes=[pltpu.VMEM((N_DEVICES,) + vec.shape, vec.dtype), pltpu.SemaphoreType.DMA((N_DEVICES - 1,)),
                        pltpu.SemaphoreType.DMA((N_DEVICES - 1,))],
    )(vec)


def sibling_exchange(grads):
    n = len(grads)

    def body(*refs):
        srcs, dsts, ssem, rsem = refs[:n], refs[n:2 * n], refs[2 * n], refs[2 * n + 1]
        x, y, c = _coords()
        cps = [_remote(srcs[i].at[:, 1 - c], dsts[i], ssem.at[i], rsem.at[i], (x, y, 1 - c)) for i in range(n)]
        for cp in cps:
            cp.start()
        for cp in cps:
            cp.wait()

    return pl.pallas_call(
        body, name="sibling_exchange",
        out_shape=[jax.ShapeDtypeStruct((N_CHIPS,) + g.shape[2:], g.dtype) for g in grads],
        in_specs=[HBM] * n, out_specs=[HBM] * n,
        scratch_shapes=[pltpu.SemaphoreType.DMA((n,)), pltpu.SemaphoreType.DMA((n,))],
    )(*grads)


def pair_sum(g0, g1, r0, r1, c_arr, tr):
    _, _, rb, cc = g0.shape
    nr = rb // tr

    def body(c_ref, g0_ref, g1_ref, r0_ref, r1_ref, p_ref, pb_ref):
        l = pl.program_id(0)

        @pl.when(l == 0)
        def _():
            s = g0_ref[...] + r0_ref[...]
            p_ref[...] = s
            pb_ref[...] = s.astype(bf16)

        @pl.when(l == 1)
        def _():
            s = g1_ref[...] + r1_ref[...]
            p_ref[...] = s
            pb_ref[...] = s.astype(bf16)

    lay0 = lambda l, a, last: jnp.where(l == 0, a, last)
    lay1 = lambda l, a: jnp.where(l == 1, a, 0)
    out_spec = pl.BlockSpec((None, None, tr, cc), lambda l, k, r, c: (l, k, r, 0))
    return pl.pallas_call(
        body, name="pair_sum",
        grid_spec=pltpu.PrefetchScalarGridSpec(
            num_scalar_prefetch=1, grid=(DEPTH, N_CHIPS, nr),
            in_specs=[pl.BlockSpec((None, None, tr, cc), lambda l, k, r, c: (lay0(l, k, N_CHIPS - 1), c[0], lay0(l, r, nr - 1), 0)),
                      pl.BlockSpec((None, None, tr, cc), lambda l, k, r, c: (lay1(l, k), c[0], lay1(l, r), 0)),
                      pl.BlockSpec((None, tr, cc), lambda l, k, r, c: (lay0(l, k, N_CHIPS - 1), lay0(l, r, nr - 1), 0)),
                      pl.BlockSpec((None, tr, cc), lambda l, k, r, c: (lay1(l, k), lay1(l, r), 0))],
            out_specs=[out_spec, out_spec]),
        out_shape=[jax.ShapeDtypeStruct((DEPTH, N_CHIPS, rb, cc), f32), jax.ShapeDtypeStruct((DEPTH, N_CHIPS, rb, cc), bf16)],
        compiler_params=_params("arbitrary", "arbitrary", "arbitrary"),
    )(c_arr, g0, g1, r0, r1)


def chip_exchange(parts):
    n = len(parts)

    def body(*refs):
        srcs, dsts, ssem, rsem = refs[:n], refs[n:2 * n], refs[2 * n], refs[2 * n + 1]
        x, y, c = _coords()
        cps = []
        for j, chip in enumerate(_peer_chips(x, y)):
            kj = 2 * chip[0] + chip[1]
            for i in range(n):
                cps.append(_remote(srcs[i].at[:, kj], dsts[i].at[j], ssem.at[i * 3 + j], rsem.at[i * 3 + j], (*chip, c)))
        for cp in cps:
            cp.start()
        for cp in cps:
            cp.wait()

    return pl.pallas_call(
        body, name="chip_exchange",
        out_shape=[jax.ShapeDtypeStruct((N_PEER_CHIPS, DEPTH) + p.shape[2:], p.dtype) for p in parts],
        in_specs=[HBM] * n, out_specs=[HBM] * n,
        scratch_shapes=[pltpu.SemaphoreType.DMA((n * N_PEER_CHIPS,)), pltpu.SemaphoreType.DMA((n * N_PEER_CHIPS,))],
    )(*parts)


def shard_sum(p32, r2, k_arr, tr):
    _, _, rb, cc = p32.shape
    nr = rb // tr

    def body(k_ref, p_ref, r_ref, o_ref):
        o_ref[...] = ((p_ref[...] + r_ref[0].astype(f32)) + r_ref[1].astype(f32)) + r_ref[2].astype(f32)

    return pl.pallas_call(
        body, name="shard_sum",
        grid_spec=pltpu.PrefetchScalarGridSpec(
            num_scalar_prefetch=1, grid=(DEPTH, nr),
            in_specs=[pl.BlockSpec((None, None, tr, cc), lambda l, r, k: (l, k[0], r, 0)),
                      pl.BlockSpec((N_PEER_CHIPS, None, tr, cc), lambda l, r, k: (0, l, r, 0))],
            out_specs=pl.BlockSpec((None, tr, cc), lambda l, r, k: (l, r, 0))),
        out_shape=jax.ShapeDtypeStruct((DEPTH, rb, cc), f32),
        compiler_params=_params("arbitrary", "arbitrary"),
    )(k_arr, p32, r2)


def sibling_share(reds):
    n = len(reds)

    def body(*refs):
        srcs, dsts, lsem, ssem, rsem = refs[:n], refs[n:2 * n], refs[2 * n], refs[2 * n + 1], refs[2 * n + 2]
        x, y, c = _coords()
        local = [pltpu.make_async_copy(srcs[i], dsts[i].at[:, c], lsem.at[i]) for i in range(n)]
        cps = [_remote(srcs[i], dsts[i].at[:, c], ssem.at[i], rsem.at[i], (x, y, 1 - c)) for i in range(n)]
        for cp in local + cps:
            cp.start()
        for cp in cps:
            cp.wait_send()
        for i in range(n):
            other = dsts[i].at[:, 1 - c]
            _remote(other, other, ssem.at[i], rsem.at[i], (x, y, 1 - c)).wait_recv()
        for cp in local:
            cp.wait()

    return pl.pallas_call(
        body, name="sibling_share",
        out_shape=[jax.ShapeDtypeStruct((DEPTH, 2) + r.shape[1:], r.dtype) for r in reds],
        in_specs=[HBM] * n, out_specs=[HBM] * n,
        scratch_shapes=[pltpu.SemaphoreType.DMA((n,)), pltpu.SemaphoreType.DMA((n,)), pltpu.SemaphoreType.DMA((n,))],
    )(*reds)


def adamw(w, g, m, v, tr):
    rows, cols = w.shape

    def body(w_ref, g_ref, m_ref, v_ref, d_ref, nm_ref, nv_ref):
        gv = g_ref[...]
        m_new = ADAM_B1 * m_ref[...] + (1.0 - ADAM_B1) * gv
        v_new = ADAM_B2 * v_ref[...] + (1.0 - ADAM_B2) * jnp.square(gv)
        m_hat = m_new / (1.0 - ADAM_B1 ** ADAM_STEP)
        v_hat = v_new / (1.0 - ADAM_B2 ** ADAM_STEP)
        d_ref[...] = -ADAM_LR * (m_hat / (jnp.sqrt(v_hat) + ADAM_EPS) + ADAM_WD * w_ref[...])
        nm_ref[...] = m_new
        nv_ref[...] = v_new

    spec = pl.BlockSpec((tr, cols), lambda i: (i, 0))
    return pl.pallas_call(
        body, name="adamw", grid=(rows // tr,), in_specs=[spec] * 4, out_specs=[spec] * 3,
        out_shape=[jax.ShapeDtypeStruct((rows, cols), f32)] * 3,
        compiler_params=_params("arbitrary"),
    )(w, g, m, v)


WEIGHTS = ("mix_norm_g", "w_in", "q_gain", "k_gain", "sinks", "rel_bias", "conv_w", "conv_b", "dt_bias", "a_log", "d_skip",
           "ssm_norm_g", "w_out", "mlp_norm_g", "w_up", "w_down")
BIG = ("w_in", "w_out", "w_up", "w_down")
SMALL = tuple(n for n in WEIGHTS if n not in BIG)
PACK_COLS = 1024
PACK_ROWS = 16


def _pack(named):
    flat = jnp.concatenate([named[n].reshape(-1) for n in SMALL])
    return jnp.pad(flat, (0, PACK_ROWS * PACK_COLS - flat.shape[0])).reshape(PACK_ROWS, PACK_COLS)


def _unpack(buf, shapes):
    flat = buf.reshape(-1)
    out, at = {}, 0
    for n in SMALL:
        size = int(np.prod(shapes[n]))
        out[n] = flat[at:at + size].reshape(shapes[n])
        at += size
    return out


def kernel(x, mix_norm_g, w_in, q_gain, k_gain, sinks, rel_bias, conv_w, conv_b, dt_bias, a_log, d_skip, ssm_norm_g, w_out, mlp_norm_g, w_up, w_down, loss_target, m_mix_norm_g, m_w_in, m_q_gain, m_k_gain, m_sinks, m_rel_bias, m_conv_w, m_conv_b, m_dt_bias, m_a_log, m_d_skip, m_ssm_norm_g, m_w_out, m_mlp_norm_g, m_w_up, m_w_down, v_mix_norm_g, v_w_in, v_q_gain, v_k_gain, v_sinks, v_rel_bias, v_conv_w, v_conv_b, v_dt_bias, v_a_log, v_d_skip, v_ssm_norm_g, v_w_out, v_mlp_norm_g, v_w_up, v_w_down):
    wts = dict(mix_norm_g=mix_norm_g, w_in=w_in, q_gain=q_gain, k_gain=k_gain, sinks=sinks, rel_bias=rel_bias, conv_w=conv_w,
               conv_b=conv_b, dt_bias=dt_bias, a_log=a_log, d_skip=d_skip, ssm_norm_g=ssm_norm_g, w_out=w_out,
               mlp_norm_g=mlp_norm_g, w_up=w_up, w_down=w_down)
    mom = dict(mix_norm_g=m_mix_norm_g, w_in=m_w_in, q_gain=m_q_gain, k_gain=m_k_gain, sinks=m_sinks, rel_bias=m_rel_bias,
               conv_w=m_conv_w, conv_b=m_conv_b, dt_bias=m_dt_bias, a_log=m_a_log, d_skip=m_d_skip, ssm_norm_g=m_ssm_norm_g,
               w_out=m_w_out, mlp_norm_g=m_mlp_norm_g, w_up=m_w_up, w_down=m_w_down)
    var = dict(mix_norm_g=v_mix_norm_g, w_in=v_w_in, q_gain=v_q_gain, k_gain=v_k_gain, sinks=v_sinks, rel_bias=v_rel_bias,
               conv_w=v_conv_w, conv_b=v_conv_b, dt_bias=v_dt_bias, a_log=v_a_log, d_skip=v_d_skip, ssm_norm_g=v_ssm_norm_g,
               w_out=v_w_out, mlp_norm_g=v_mlp_norm_g, w_up=v_w_up, w_down=v_w_down)
    xi, yi, ci = _coords()
    k_me = 2 * xi + yi
    c_arr = jnp.reshape(ci, (1,)).astype(jnp.int32)
    k_arr = jnp.reshape(k_me, (1,)).astype(jnp.int32)

    g_in, g_out, g_up, g_dn, g_conv = all_gather_weights([wts[n].astype(bf16) for n in BIG], conv_w)
    full = {n: wts[n] for n in SMALL}
    full["w_in"] = _to_aligned(jnp.transpose(g_in, (1, 2, 0, 3)).reshape(DEPTH, D_MODEL, D_IN))
    full["w_out"] = jnp.transpose(g_out, (1, 0, 2, 3)).reshape(DEPTH, D_MODEL, D_MODEL)
    full["w_up"], full["w_down"] = g_up, g_dn
    full["conv_w"] = jnp.transpose(g_conv, (1, 2, 0, 3)).reshape(DEPTH, CONV_WIDTH, D_CONV)

    loss, dx, grads = local_step(x[0], loss_target[0], full)
    loss = lax.psum(loss[0, 0], ("x", "y", "c"))

    small_shapes = {n: grads[n].shape for n in SMALL}
    small = _unpack(small_all_reduce(_pack(grads)), small_shapes)
    cols = conv_w.shape[-1]
    small["conv_w"] = lax.dynamic_slice_in_dim(small["conv_w"], k_me * cols, cols, axis=2)

    def view(name, g):
        if name == "w_in":
            g = jnp.transpose(_from_aligned(g).reshape(D_MODEL, N_CHIPS, D_IN // N_CHIPS), (1, 0, 2))
        rows = wts[name].shape[1] // 2
        return g.reshape(N_CHIPS, 2, rows, wts[name].shape[2])

    views = [view(n, grads[n][l]) for n in BIG for l in range(DEPTH)]
    recv = sibling_exchange(views)
    p32s, pbs = [], []
    for i, n in enumerate(BIG):
        tr = min(256, views[2 * i].shape[2])
        p32, pb = pair_sum(views[2 * i], views[2 * i + 1], recv[2 * i], recv[2 * i + 1], c_arr, tr)
        p32s.append(p32)
        pbs.append(pb)
    r2s = chip_exchange(pbs)
    reds = [shard_sum(p32s[i], r2s[i], k_arr, min(256, p32s[i].shape[2])) for i in range(len(BIG))]
    fulls = sibling_share(reds)

    g_out_d, d_out_d, m_out_d, v_out_d = {}, {}, {}, {}
    for i, n in enumerate(BIG):
        shape = wts[n].shape
        g2 = fulls[i].reshape(shape[0] * shape[1], shape[2])
        d, nm, nv = adamw(wts[n].reshape(g2.shape), g2, mom[n].reshape(g2.shape), var[n].reshape(g2.shape), 256)
        g_out_d[n], d_out_d[n], m_out_d[n], v_out_d[n] = (t.reshape(shape) for t in (g2, d, nm, nv))
    shard_shapes = {n: wts[n].shape for n in SMALL}
    d, nm, nv = adamw(_pack(wts), _pack(small), _pack(mom), _pack(var), PACK_ROWS)
    for dst, buf in ((d_out_d, d), (m_out_d, nm), (v_out_d, nv)):
        dst.update(_unpack(buf, shard_shapes))
    g_out_d.update(small)

    return (loss, dx[None], *[g_out_d[n] for n in WEIGHTS], *[d_out_d[n] for n in WEIGHTS],
            *[m_out_d[n] for n in WEIGHTS], *[v_out_d[n] for n in WEIGHTS])
```

```python
import functools

import numpy as np
import jax
import jax.numpy as jnp
from jax import lax
from jax.experimental import pallas as pl
from jax.experimental.pallas import tpu as pltpu

f32 = jnp.float32
bf16 = jnp.bfloat16

SEQ = 2048
D_MODEL = 1024
DEPTH = 2
HEAD_DIM = 64
N_Q_HEADS = 8
N_KV_HEADS = 2
Q_PER_KV = N_Q_HEADS // N_KV_HEADS
BLOCK = 128
N_BLOCKS = SEQ // BLOCK
N_BUCKETS = 32
MAX_DISTANCE = 128
SSM_HEADS = 8
SSM_HEAD_DIM = 64
SSM_GROUPS = 2
HEADS_PER_GROUP = SSM_HEADS // SSM_GROUPS
SSM_STATE = 128
CONV_WIDTH = 4
CHUNK = 128
N_CHUNKS = SEQ // CHUNK
D_FF = 4 * D_MODEL
D_ATTN = N_Q_HEADS * HEAD_DIM
D_KV = N_KV_HEADS * HEAD_DIM
D_SSM = SSM_HEADS * SSM_HEAD_DIM
D_BC = SSM_GROUPS * SSM_STATE
D_CONV = D_SSM + 2 * D_BC
D_IN = D_ATTN + 2 * D_KV + D_SSM + D_CONV + SSM_HEADS
EPS = 1e-6
NEG = -1e30
N_CHIPS = 4
FF_TILE = D_FF // N_CHIPS

LANE = 128
PW = D_ATTN + D_SSM + D_CONV + 2 * D_KV + LANE
OFF_Q, OFF_Z, OFF_X, OFF_K, OFF_V, OFF_DT = 0, 512, 1024, 2048, 2176, 2304

ADAM_LR = 0.001
ADAM_B1 = 0.9
ADAM_B2 = 0.999
ADAM_EPS = 1e-08
ADAM_WD = 0.01
ADAM_STEP = 10

VMEM_LIMIT = 56 * 1024 * 1024


def _params(*sem):
    return pltpu.CompilerParams(dimension_semantics=tuple(sem), vmem_limit_bytes=VMEM_LIMIT)


def _bdot(a, b):
    return jnp.dot(a.astype(bf16), b.astype(bf16), preferred_element_type=f32)


def _bdot_nt(a, b):
    return lax.dot_general(a.astype(bf16), b.astype(bf16), (((1,), (1,)), ((), ())), preferred_element_type=f32)


def _bdot_tn(a, b):
    return lax.dot_general(a.astype(bf16), b.astype(bf16), (((0,), (0,)), ((), ())), preferred_element_type=f32)


def _hdot(a, b):
    return jnp.dot(a, b, precision=lax.Precision.HIGHEST, preferred_element_type=f32)


def _sigmoid(x):
    return 1.0 / (1.0 + jnp.exp(-x))


def _softplus(x):
    return jnp.maximum(x, 0.0) + jnp.log1p(jnp.exp(-jnp.abs(x)))


def _rms(x):
    return lax.rsqrt(jnp.mean(x * x, axis=-1, keepdims=True) + EPS)


def _rms_bwd(dy, xhat, r, g):
    t = dy * g
    return r * (t - xhat * jnp.mean(t * xhat, axis=-1, keepdims=True))


def _full(shape):
    return pl.BlockSpec(shape, lambda *_: (0,) * len(shape))


def _to_aligned(w):
    q, k, v, z, xbc, dt = jnp.split(w, [512, 640, 768, 1280, 2304], axis=-1)
    pad = jnp.zeros(w.shape[:-1] + (LANE - SSM_HEADS,), w.dtype)
    return jnp.concatenate([q, z, xbc, k, v, dt, pad], axis=-1)


def _from_aligned(w):
    q, z, xbc, k, v, dt = (w[..., OFF_Q:OFF_Z], w[..., OFF_Z:OFF_X], w[..., OFF_X:OFF_K], w[..., OFF_K:OFF_V],
                           w[..., OFF_V:OFF_DT], w[..., OFF_DT:OFF_DT + SSM_HEADS])
    return jnp.concatenate([q, k, v, z, xbc, dt], axis=-1)


def _bucket_table():
    qi = np.arange(BLOCK)[:, None]
    kj = np.arange(2 * BLOCK)[None, :]
    dist = qi + BLOCK - kj
    ok = (dist >= 0) & (dist < 128)
    d = np.clip(dist, 0, None)
    max_exact = N_BUCKETS // 2
    d_f = np.maximum(d, 1).astype(np.float32)
    large = max_exact + (np.log(d_f / np.float32(max_exact)) / np.float32(np.log(MAX_DISTANCE / max_exact))
                         * np.float32(N_BUCKETS - max_exact)).astype(np.int32)
    large = np.minimum(large, N_BUCKETS - 1)
    bucket = np.where(d < max_exact, d, large)
    return np.where(ok, bucket, -1).astype(np.int32)


def bias_build(rel_bias, bucket):
    def body(rel_ref, bkt_ref, o_ref):
        bkt = bkt_ref[...]
        for h in range(N_Q_HEADS):
            acc = jnp.where(bkt < 0, NEG, 0.0).astype(f32)
            for b in range(N_BUCKETS):
                acc = acc + jnp.where(bkt == b, rel_ref[b, h], 0.0)
            o_ref[h] = acc

    return pl.pallas_call(
        body, name="bias_build", out_shape=jax.ShapeDtypeStruct((N_Q_HEADS, BLOCK, 2 * BLOCK), f32),
        in_specs=[pl.BlockSpec(memory_space=pltpu.SMEM), pl.BlockSpec(memory_space=pltpu.VMEM)],
        out_specs=pl.BlockSpec(memory_space=pltpu.VMEM),
    )(rel_bias, bucket)


def bias_bwd(dband0, dband1, bucket):
    def body(d0_ref, d1_ref, bkt_ref, o_ref):
        bkt = bkt_ref[...]
        o_ref[...] = jnp.zeros_like(o_ref)
        for h in range(N_Q_HEADS):
            d = d0_ref[h] + d1_ref[h]
            for b in range(N_BUCKETS):
                part = jnp.sum(jnp.where(bkt == b, d, 0.0), axis=1, keepdims=True)
                o_ref[b:b + 1, h:h + 1] = jnp.sum(part, axis=0, keepdims=True)

    return pl.pallas_call(
        body, name="bias_bwd", out_shape=jax.ShapeDtypeStruct((N_BUCKETS, LANE), f32),
    )(dband0, dband1, bucket)


def in_fwd(x, g, w, tm=256):
    def body(x_ref, g_ref, w_ref, o_ref):
        xv = x_ref[...]
        h = xv * _rms(xv) * g_ref[...]
        o_ref[...] = _bdot(h, w_ref[...])

    return pl.pallas_call(
        body, name="in_fwd", grid=(SEQ // tm,),
        in_specs=[pl.BlockSpec((tm, D_MODEL), lambda i: (i, 0)), _full((1, D_MODEL)), _full((D_MODEL, PW))],
        out_specs=pl.BlockSpec((tm, PW), lambda i: (i, 0)),
        out_shape=jax.ShapeDtypeStruct((SEQ, PW), f32),
        compiler_params=_params("arbitrary"),
    )(x, g, w)


def in_bwd(dq, dz, dxbc, dk, dv, ddt, x, g, w, dres, tm=256):
    def body(dq_ref, dz_ref, dx_ref, dk_ref, dv_ref, ddt_ref, x_ref, g_ref, w_ref, dres_ref, o_ref, dw_ref, dg_ref):
        i = pl.program_id(0)

        @pl.when(i == 0)
        def _():
            dw_ref[...] = jnp.zeros_like(dw_ref)
            dg_ref[...] = jnp.zeros_like(dg_ref)

        dproj = jnp.concatenate([dq_ref[...], dz_ref[...], dx_ref[...], dk_ref[...], dv_ref[...], ddt_ref[...]],
                                axis=-1).astype(bf16)
        xv = x_ref[...]
        r = _rms(xv)
        xhat = xv * r
        gv = g_ref[...]
        h = xhat * gv
        dw_ref[...] += _bdot_tn(h, dproj)
        dh = _bdot_nt(dproj, w_ref[...])
        dg_ref[...] += jnp.sum(dh * xhat, axis=0, keepdims=True)
        o_ref[...] = dres_ref[...] + _rms_bwd(dh, xhat, r, gv)

    tok = lambda w_: pl.BlockSpec((tm, w_), lambda i: (i, 0))
    return pl.pallas_call(
        body, name="in_bwd", grid=(SEQ // tm,),
        in_specs=[tok(D_ATTN), tok(D_SSM), tok(D_CONV), tok(D_KV // 1), tok(D_KV // 1), tok(LANE), tok(D_MODEL),
                  _full((1, D_MODEL)), _full((D_MODEL, PW)), tok(D_MODEL)],
        out_specs=[tok(D_MODEL), _full((D_MODEL, PW)), _full((1, D_MODEL))],
        out_shape=[jax.ShapeDtypeStruct((SEQ, D_MODEL), f32), jax.ShapeDtypeStruct((D_MODEL, PW), f32),
                   jax.ShapeDtypeStruct((1, D_MODEL), f32)],
        compiler_params=_params("arbitrary"),
    )(dq, dz, dxbc, dk, dv, ddt, x, g, w, dres)


def _attn_probs(qn, kn, bias_h, sink, first, col):
    s = _bdot_nt(qn, kn) * (HEAD_DIM ** -0.5) + bias_h
    s = jnp.where(jnp.logical_and(first, col < BLOCK), NEG, s)
    m = jnp.maximum(jnp.max(s, axis=-1, keepdims=True), sink)
    p = jnp.exp(s - m)
    psink = jnp.exp(sink - m)
    inv = 1.0 / (jnp.sum(p, axis=-1, keepdims=True) + psink)
    return p * inv, psink * inv


def attn_fwd(proj, q_gain, k_gain, sinks, bias):
    kcol, vcol = OFF_K // D_KV, OFF_V // D_KV

    def body(q_ref, kc_ref, kp_ref, vc_ref, vp_ref, qg_ref, kg_ref, sink_ref, bias_ref, o_ref):
        n = pl.program_id(0)
        first = n == 0
        col = lax.broadcasted_iota(jnp.int32, (BLOCK, 2 * BLOCK), 1)
        k2 = jnp.concatenate([kp_ref[...], kc_ref[...]], axis=0)
        v2 = jnp.concatenate([vp_ref[...], vc_ref[...]], axis=0)
        qg, kg = qg_ref[...], kg_ref[...]
        for hk in range(N_KV_HEADS):
            kk = k2[:, hk * HEAD_DIM:(hk + 1) * HEAD_DIM]
            kn = (kk * _rms(kk) * kg).astype(bf16)
            vb = v2[:, hk * HEAD_DIM:(hk + 1) * HEAD_DIM].astype(bf16)
            for gq in range(Q_PER_KV):
                h = hk * Q_PER_KV + gq
                qq = q_ref[:, h * HEAD_DIM:(h + 1) * HEAD_DIM]
                qn = qq * _rms(qq) * qg
                p, _ = _attn_probs(qn, kn, bias_ref[h], sink_ref[h], first, col)
                o_ref[:, h * HEAD_DIM:(h + 1) * HEAD_DIM] = _bdot(p, vb)

    prev = lambda n: jnp.maximum(n - 1, 0)
    return pl.pallas_call(
        body, name="attn_fwd", grid=(N_BLOCKS,),
        in_specs=[pl.BlockSpec((BLOCK, D_ATTN), lambda n: (n, 0)),
                  pl.BlockSpec((BLOCK, D_KV), lambda n: (n, kcol)), pl.BlockSpec((BLOCK, D_KV), lambda n: (prev(n), kcol)),
                  pl.BlockSpec((BLOCK, D_KV), lambda n: (n, vcol)), pl.BlockSpec((BLOCK, D_KV), lambda n: (prev(n), vcol)),
                  _full((1, HEAD_DIM)), _full((1, HEAD_DIM)), pl.BlockSpec(memory_space=pltpu.SMEM),
                  _full((N_Q_HEADS, BLOCK, 2 * BLOCK))],
        out_specs=pl.BlockSpec((BLOCK, D_ATTN), lambda n: (n, 0)),
        out_shape=jax.ShapeDtypeStruct((SEQ, D_ATTN), f32),
        compiler_params=_params("arbitrary"),
    )(proj, proj, proj, proj, proj, q_gain, k_gain, sinks, bias)


def attn_bwd(proj, d_out, q_gain, k_gain, sinks, bias):
    kcol, vcol = OFF_K // D_KV, OFF_V // D_KV

    def body(q_ref, kc_ref, kp_ref, vc_ref, vp_ref, do_ref, qg_ref, kg_ref, sink_ref, bias_ref,
             dq_ref, dk_ref, dv_ref, dband_ref, dsink_ref, dqg_ref, dkg_ref, dkn_scr, dv_scr):
        i = pl.program_id(0)
        first = i == N_BLOCKS - 1

        @pl.when(i == 0)
        def _():
            for ref in (dband_ref, dsink_ref, dqg_ref, dkg_ref, dkn_scr, dv_scr):
                ref[...] = jnp.zeros_like(ref)

        col = lax.broadcasted_iota(jnp.int32, (BLOCK, 2 * BLOCK), 1)
        k2 = jnp.concatenate([kp_ref[...], kc_ref[...]], axis=0)
        v2 = jnp.concatenate([vp_ref[...], vc_ref[...]], axis=0)
        qg, kg = qg_ref[...], kg_ref[...]
        scale = HEAD_DIM ** -0.5
        for hk in range(N_KV_HEADS):
            sl = slice(hk * HEAD_DIM, (hk + 1) * HEAD_DIM)
            kk = k2[:, sl]
            rk = _rms(kk)
            khat = kk * rk
            kn = (khat * kg).astype(bf16)
            vb = v2[:, sl].astype(bf16)
            dkn = jnp.zeros((2 * BLOCK, HEAD_DIM), f32)
            dvv = jnp.zeros((2 * BLOCK, HEAD_DIM), f32)
            for gq in range(Q_PER_KV):
                h = hk * Q_PER_KV + gq
                hs = slice(h * HEAD_DIM, (h + 1) * HEAD_DIM)
                qq = q_ref[:, hs]
                rq = _rms(qq)
                qhat = qq * rq
                qn = qhat * qg
                p, psink = _attn_probs(qn, kn, bias_ref[h], sink_ref[h], first, col)
                d_o = do_ref[:, hs]
                dp = _bdot_nt(d_o, vb)
                delta = jnp.sum(p * dp, axis=-1, keepdims=True)
                ds = p * (dp - delta)
                dband_ref[h] += ds
                dsink_ref[:, h:h + 1] += -jnp.sum(psink * delta, axis=0, keepdims=True)
                dqn = _bdot(ds, kn) * scale
                dkn = dkn + _bdot_tn(ds, qn) * scale
                dvv = dvv + _bdot_tn(p, d_o)
                dqg_ref[...] += jnp.sum(dqn * qhat, axis=0, keepdims=True)
                dq_ref[:, hs] = _rms_bwd(dqn, qhat, rq, qg)
            dkn_cur = dkn[BLOCK:] + dkn_scr[:, sl]
            dkn_scr[:, sl] = dkn[:BLOCK]
            khat_c, rk_c = khat[BLOCK:], rk[BLOCK:]
            dkg_ref[...] += jnp.sum(dkn_cur * khat_c, axis=0, keepdims=True)
            dk_ref[:, sl] = _rms_bwd(dkn_cur, khat_c, rk_c, kg)
            dv_ref[:, sl] = dvv[BLOCK:] + dv_scr[:, sl]
            dv_scr[:, sl] = dvv[:BLOCK]

    blk = lambda i: N_BLOCKS - 1 - i
    prev = lambda i: jnp.maximum(N_BLOCKS - 2 - i, 0)
    return pl.pallas_call(
        body, name="attn_bwd", grid=(N_BLOCKS,),
        in_specs=[pl.BlockSpec((BLOCK, D_ATTN), lambda i: (blk(i), 0)),
                  pl.BlockSpec((BLOCK, D_KV), lambda i: (blk(i), kcol)), pl.BlockSpec((BLOCK, D_KV), lambda i: (prev(i), kcol)),
                  pl.BlockSpec((BLOCK, D_KV), lambda i: (blk(i), vcol)), pl.BlockSpec((BLOCK, D_KV), lambda i: (prev(i), vcol)),
                  pl.BlockSpec((BLOCK, D_ATTN), lambda i: (blk(i), 0)),
                  _full((1, HEAD_DIM)), _full((1, HEAD_DIM)), pl.BlockSpec(memory_space=pltpu.SMEM),
                  _full((N_Q_HEADS, BLOCK, 2 * BLOCK))],
        out_specs=[pl.BlockSpec((BLOCK, D_ATTN), lambda i: (blk(i), 0)), pl.BlockSpec((BLOCK, D_KV), lambda i: (blk(i), 0)),
                   pl.BlockSpec((BLOCK, D_KV), lambda i: (blk(i), 0)), _full((N_Q_HEADS, BLOCK, 2 * BLOCK)),
                   _full((1, LANE)), _full((1, HEAD_DIM)), _full((1, HEAD_DIM))],
        out_shape=[jax.ShapeDtypeStruct((SEQ, D_ATTN), f32), jax.ShapeDtypeStruct((SEQ, D_KV), f32),
                   jax.ShapeDtypeStruct((SEQ, D_KV), f32), jax.ShapeDtypeStruct((N_Q_HEADS, BLOCK, 2 * BLOCK), f32),
                   jax.ShapeDtypeStruct((1, LANE), f32), jax.ShapeDtypeStruct((1, HEAD_DIM), f32),
                   jax.ShapeDtypeStruct((1, HEAD_DIM), f32)],
        scratch_shapes=[pltpu.VMEM((BLOCK, D_KV), f32), pltpu.VMEM((BLOCK, D_KV), f32)],
        compiler_params=_params("arbitrary"),
    )(proj, proj, proj, proj, proj, d_out, q_gain, k_gain, sinks, bias)


def _shift_down(u, s, row):
    if s == 0:
        return u
    return jnp.where(row >= s, pltpu.roll(u, s, 0), 0.0)


def _shift_up(u, s, row):
    if s == 0:
        return u
    return jnp.where(row < SEQ - s, pltpu.roll(u, SEQ - s, 0), 0.0)


def conv_fwd(proj, conv_w, conv_b):
    xcol = OFF_X // LANE

    def body(u_ref, w_ref, b_ref, o_ref):
        u = u_ref[...]
        row = lax.broadcasted_iota(jnp.int32, u.shape, 0)
        pre = b_ref[...] + jnp.zeros_like(u)
        for k in range(CONV_WIDTH):
            pre = pre + w_ref[k:k + 1, :] * _shift_down(u, CONV_WIDTH - 1 - k, row)
        o_ref[...] = pre * _sigmoid(pre)

    return pl.pallas_call(
        body, name="conv_fwd", grid=(D_CONV // LANE,),
        in_specs=[pl.BlockSpec((SEQ, LANE), lambda j: (0, xcol + j)), pl.BlockSpec((CONV_WIDTH, LANE), lambda j: (0, j)),
                  pl.BlockSpec((1, LANE), lambda j: (0, j))],
        out_specs=pl.BlockSpec((SEQ, LANE), lambda j: (0, j)),
        out_shape=jax.ShapeDtypeStruct((SEQ, D_CONV), f32),
        compiler_params=_params("arbitrary"),
    )(proj, conv_w, conv_b)


def conv_bwd(proj, d_act, conv_w, conv_b):
    xcol = OFF_X // LANE

    def body(u_ref, da_ref, w_ref, b_ref, du_ref, dw_ref, db_ref):
        u = u_ref[...]
        row = lax.broadcasted_iota(jnp.int32, u.shape, 0)
        shifted = [_shift_down(u, CONV_WIDTH - 1 - k, row) for k in range(CONV_WIDTH)]
        pre = b_ref[...] + jnp.zeros_like(u)
        for k in range(CONV_WIDTH):
            pre = pre + w_ref[k:k + 1, :] * shifted[k]
        sg = _sigmoid(pre)
        dpre = da_ref[...] * (sg * (1.0 + pre * (1.0 - sg)))
        db_ref[...] = jnp.sum(dpre, axis=0, keepdims=True)
        du = jnp.zeros_like(u)
        for k in range(CONV_WIDTH):
            dw_ref[k:k + 1, :] = jnp.sum(dpre * shifted[k], axis=0, keepdims=True)
            du = du + w_ref[k:k + 1, :] * _shift_up(dpre, CONV_WIDTH - 1 - k, row)
        du_ref[...] = du

    return pl.pallas_call(
        body, name="conv_bwd", grid=(D_CONV // LANE,),
        in_specs=[pl.BlockSpec((SEQ, LANE), lambda j: (0, xcol + j)), pl.BlockSpec((SEQ, LANE), lambda j: (0, j)),
                  pl.BlockSpec((CONV_WIDTH, LANE), lambda j: (0, j)), pl.BlockSpec((1, LANE), lambda j: (0, j))],
        out_specs=[pl.BlockSpec((SEQ, LANE), lambda j: (0, j)), pl.BlockSpec((CONV_WIDTH, LANE), lambda j: (0, j)),
                   pl.BlockSpec((1, LANE), lambda j: (0, j))],
        out_shape=[jax.ShapeDtypeStruct((SEQ, D_CONV), f32), jax.ShapeDtypeStruct((CONV_WIDTH, D_CONV), f32),
                   jax.ShapeDtypeStruct((1, D_CONV), f32)],
        compiler_params=_params("arbitrary"),
    )(proj, d_act, conv_w, conv_b)


def _ssd_chunk_common(dt_raw, dtb, alog):
    row = lax.broadcasted_iota(jnp.int32, (CHUNK, CHUNK), 0)
    col = lax.broadcasted_iota(jnp.int32, (CHUNK, CHUNK), 1)
    tri = (row >= col).astype(f32)
    strict = (row > col).astype(f32)
    dtp = _softplus(dt_raw + dtb)
    a_row = -jnp.exp(alog)
    d_a = dtp * a_row
    cs = _hdot(tri, d_a)
    cs_last = cs[CHUNK - 1:CHUNK, :]
    return row, col, tri, strict, dtp, a_row, d_a, cs, cs_last


def _seg_decay(tri, strict, d_a_h, row, col):
    seg = _hdot(tri, d_a_h * strict)
    return jnp.where(row >= col, jnp.exp(seg), 0.0)


def ssd_fwd(act, proj, dt_bias, a_log, d_skip, norm_g):
    zcol, dtcol = OFF_Z // D_SSM, OFF_DT // LANE
    gw = D_SSM // SSM_GROUPS

    def body(act_ref, z_ref, dt_ref, dtb_ref, alog_ref, dsk_ref, ng_ref, out_ref, ypre_ref, st_ref, state, ybuf):
        c = pl.program_id(0)

        @pl.when(c == 0)
        def _():
            state[...] = jnp.zeros_like(state)

        row, col, tri, strict, dtp, a_row, d_a, cs, cs_last = _ssd_chunk_common(dt_ref[...], dtb_ref[...], alog_ref[...])
        e_cs = jnp.exp(cs)
        dte = jnp.exp(cs_last - cs)
        ecl = jnp.exp(cs_last)
        dsk = dsk_ref[...]
        for g in range(SSM_GROUPS):
            bg = act_ref[:, D_SSM + g * SSM_STATE:D_SSM + (g + 1) * SSM_STATE]
            cg = act_ref[:, D_SSM + D_BC + g * SSM_STATE:D_SSM + D_BC + (g + 1) * SSM_STATE]
            cb = _bdot_nt(cg, bg)
            for r in range(HEADS_PER_GROUP):
                hd = g * HEADS_PER_GROUP + r
                hs = slice(hd * SSM_HEAD_DIM, (hd + 1) * SSM_HEAD_DIM)
                hl = slice(hd, hd + 1)
                x_h = act_ref[:, hs]
                xdt = x_h * dtp[:, hl]
                lm = _seg_decay(tri, strict, d_a[:, hl], row, col)
                prev = state[hd]
                st_ref[0, hd] = prev
                y = _bdot(cb * lm, xdt) + e_cs[:, hl] * _bdot(cg, prev) + x_h * dsk[:, hl]
                ybuf[:, hs] = y
                state[hd] = prev * ecl[:, hl] + _bdot_tn(bg, xdt * dte[:, hl])
        y = ybuf[...]
        ypre_ref[...] = y
        z = z_ref[...]
        yz = y * (z * _sigmoid(z))
        ng = ng_ref[...]
        for g in range(SSM_GROUPS):
            gs = slice(g * gw, (g + 1) * gw)
            part = yz[:, gs]
            out_ref[:, gs] = part * _rms(part) * ng[:, gs]

    return pl.pallas_call(
        body, name="ssd_fwd", grid=(N_CHUNKS,),
        in_specs=[pl.BlockSpec((CHUNK, D_CONV), lambda c: (c, 0)), pl.BlockSpec((CHUNK, D_SSM), lambda c: (c, zcol)),
                  pl.BlockSpec((CHUNK, LANE), lambda c: (c, dtcol)), _full((1, LANE)), _full((1, LANE)), _full((1, LANE)),
                  _full((1, D_SSM))],
        out_specs=[pl.BlockSpec((CHUNK, D_SSM), lambda c: (c, 0)), pl.BlockSpec((CHUNK, D_SSM), lambda c: (c, 0)),
                   pl.BlockSpec((1, SSM_HEADS, SSM_STATE, SSM_HEAD_DIM), lambda c: (c, 0, 0, 0))],
        out_shape=[jax.ShapeDtypeStruct((SEQ, D_SSM), f32), jax.ShapeDtypeStruct((SEQ, D_SSM), f32),
                   jax.ShapeDtypeStruct((N_CHUNKS, SSM_HEADS, SSM_STATE, SSM_HEAD_DIM), f32)],
        scratch_shapes=[pltpu.VMEM((SSM_HEADS, SSM_STATE, SSM_HEAD_DIM), f32), pltpu.VMEM((CHUNK, D_SSM), f32)],
        compiler_params=_params("arbitrary"),
    )(act, proj, proj, dt_bias, a_log, d_skip, norm_g)


def ssd_bwd(act, proj, ypre, states, d_out, dt_bias, a_log, d_skip, norm_g):
    zcol, dtcol = OFF_Z // D_SSM, OFF_DT // LANE
    gw = D_SSM // SSM_GROUPS

    def body(act_ref, z_ref, dt_ref, ypre_ref, st_ref, do_ref, dtb_ref, alog_ref, dsk_ref, ng_ref,
             dact_ref, ddt_ref, dz_ref, dng_ref, dpar_ref, dstate, dybuf):
        i = pl.program_id(0)

        @pl.when(i == 0)
        def _():
            for ref in (dng_ref, dpar_ref, dstate):
                ref[...] = jnp.zeros_like(ref)

        y = ypre_ref[...]
        z = z_ref[...]
        sgz = _sigmoid(z)
        sz = z * sgz
        yz = y * sz
        ng = ng_ref[...]
        d_o = do_ref[...]
        for g in range(SSM_GROUPS):
            gs = slice(g * gw, (g + 1) * gw)
            part = yz[:, gs]
            r = _rms(part)
            yhat = part * r
            dng_ref[:, gs] += jnp.sum(d_o[:, gs] * yhat, axis=0, keepdims=True)
            dyz = _rms_bwd(d_o[:, gs], yhat, r, ng[:, gs])
            dybuf[:, gs] = dyz * sz[:, gs]
            dz_ref[:, gs] = dyz * y[:, gs] * (sgz[:, gs] * (1.0 + z[:, gs] * (1.0 - sgz[:, gs])))

        row, col, tri, strict, dtp, a_row, d_a, cs, cs_last = _ssd_chunk_common(dt_ref[...], dtb_ref[...], alog_ref[...])
        upper = (row <= col).astype(f32)
        lane = lax.broadcasted_iota(jnp.int32, (CHUNK, LANE), 1)
        lane1 = lax.broadcasted_iota(jnp.int32, (1, LANE), 1)
        e_cs = jnp.exp(cs)
        dte = jnp.exp(cs_last - cs)
        ecl = jnp.exp(cs_last)
        dsk = dsk_ref[...]
        ddt_mat = jnp.zeros((CHUNK, LANE), f32)
        dcs_mat = jnp.zeros((CHUNK, LANE), f32)
        dda_mat = jnp.zeros((CHUNK, LANE), f32)
        dcsl_row = jnp.zeros((1, LANE), f32)
        dd_row = jnp.zeros((1, LANE), f32)
        for g in range(SSM_GROUPS):
            bsl = slice(D_SSM + g * SSM_STATE, D_SSM + (g + 1) * SSM_STATE)
            csl = slice(D_SSM + D_BC + g * SSM_STATE, D_SSM + D_BC + (g + 1) * SSM_STATE)
            bg = act_ref[:, bsl]
            cg = act_ref[:, csl]
            cb = _bdot_nt(cg, bg)
            dcb = jnp.zeros((CHUNK, CHUNK), f32)
            dbg = jnp.zeros((CHUNK, SSM_STATE), f32)
            dcg = jnp.zeros((CHUNK, SSM_STATE), f32)
            for rr in range(HEADS_PER_GROUP):
                hd = g * HEADS_PER_GROUP + rr
                hs = slice(hd * SSM_HEAD_DIM, (hd + 1) * SSM_HEAD_DIM)
                hl = slice(hd, hd + 1)
                x_h = act_ref[:, hs]
                dt_h = dtp[:, hl]
                e_h = e_cs[:, hl]
                dte_h = dte[:, hl]
                ecl_h = ecl[:, hl]
                xdt = x_h * dt_h
                lm = _seg_decay(tri, strict, d_a[:, hl], row, col)
                m = cb * lm
                prev = st_ref[0, hd]
                dy = dybuf[:, hs]
                dh = dstate[hd]
                dd_row = dd_row + jnp.where(lane1 == hd, jnp.sum(jnp.sum(dy * x_h, axis=1, keepdims=True), axis=0, keepdims=True), 0.0)
                dx = dy * dsk[:, hl]
                gmat = _bdot(cg, prev)
                dg = dy * e_h
                dcg = dcg + _bdot_nt(dg, prev)
                dprev = _bdot_tn(cg, dg)
                dcs_h = jnp.sum(dy * gmat, axis=1, keepdims=True) * e_h
                dm = _bdot_nt(dy, xdt)
                dxdt = _bdot_tn(m, dy)
                dcb = dcb + dm * lm
                dseg = dm * m
                dda_h = jnp.sum(_hdot(upper, dseg) * strict, axis=1, keepdims=True)
                wmat = xdt * dte_h
                dbg = dbg + _bdot_nt(wmat, dh)
                dw = _bdot(bg, dh)
                dxdt = dxdt + dw * dte_h
                ddte = jnp.sum(dw * xdt, axis=1, keepdims=True) * dte_h
                dcs_h = dcs_h - ddte
                dcsl = jnp.sum(ddte, axis=0, keepdims=True)
                dcsl = dcsl + jnp.sum(jnp.sum(dh * prev, axis=1, keepdims=True), axis=0, keepdims=True) * ecl_h
                dstate[hd] = dprev + dh * ecl_h
                dact_ref[:, hs] = dx + dxdt * dt_h
                ddt_h = jnp.sum(dxdt * x_h, axis=1, keepdims=True)
                ddt_mat = jnp.where(lane == hd, ddt_h, ddt_mat)
                dcs_mat = jnp.where(lane == hd, dcs_h, dcs_mat)
                dda_mat = jnp.where(lane == hd, dda_h, dda_mat)
                dcsl_row = jnp.where(lane1 == hd, dcsl, dcsl_row)
            dact_ref[:, bsl] = dbg + _bdot_tn(dcb, cg)
            dact_ref[:, csl] = dcg + _bdot(dcb, bg)
        rowl = lax.broadcasted_iota(jnp.int32, (CHUNK, LANE), 0)
        dcs_mat = dcs_mat + jnp.where(rowl == CHUNK - 1, dcsl_row, 0.0)
        dda = dda_mat + _hdot(upper, dcs_mat)
        ddt_mat = ddt_mat + dda * a_row
        da_row = jnp.sum(dda * dtp, axis=0, keepdims=True)
        ddt_raw = ddt_mat * _sigmoid(dt_ref[...] + dtb_ref[...])
        ddt_ref[...] = ddt_raw
        dpar_ref[0:1, :] += jnp.sum(ddt_raw, axis=0, keepdims=True)
        dpar_ref[1:2, :] += da_row * a_row
        dpar_ref[2:3, :] += dd_row

    blk = lambda i: N_CHUNKS - 1 - i
    return pl.pallas_call(
        body, name="ssd_bwd", grid=(N_CHUNKS,),
        in_specs=[pl.BlockSpec((CHUNK, D_CONV), lambda i: (blk(i), 0)), pl.BlockSpec((CHUNK, D_SSM), lambda i: (blk(i), zcol)),
                  pl.BlockSpec((CHUNK, LANE), lambda i: (blk(i), dtcol)), pl.BlockSpec((CHUNK, D_SSM), lambda i: (blk(i), 0)),
                  pl.BlockSpec((1, SSM_HEADS, SSM_STATE, SSM_HEAD_DIM), lambda i: (blk(i), 0, 0, 0)),
                  pl.BlockSpec((CHUNK, D_SSM), lambda i: (blk(i), 0)),
                  _full((1, LANE)), _full((1, LANE)), _full((1, LANE)), _full((1, D_SSM))],
        out_specs=[pl.BlockSpec((CHUNK, D_CONV), lambda i: (blk(i), 0)), pl.BlockSpec((CHUNK, LANE), lambda i: (blk(i), 0)),
                   pl.BlockSpec((CHUNK, D_SSM), lambda i: (blk(i), 0)), _full((1, D_SSM)), _full((8, LANE))],
        out_shape=[jax.ShapeDtypeStruct((SEQ, D_CONV), f32), jax.ShapeDtypeStruct((SEQ, LANE), f32),
                   jax.ShapeDtypeStruct((SEQ, D_SSM), f32), jax.ShapeDtypeStruct((1, D_SSM), f32),
                   jax.ShapeDtypeStruct((8, LANE), f32)],
        scratch_shapes=[pltpu.VMEM((SSM_HEADS, SSM_STATE, SSM_HEAD_DIM), f32), pltpu.VMEM((CHUNK, D_SSM), f32)],
        compiler_params=_params("arbitrary"),
    )(act, proj, proj, ypre, states, d_out, dt_bias, a_log, d_skip, norm_g)


def out_fwd(x, attn, ssm, w_out, tm=512):
    def body(x_ref, a_ref, s_ref, w_ref, o_ref):
        o_ref[...] = x_ref[...] + _bdot(a_ref[...], w_ref[:D_ATTN, :]) + _bdot(s_ref[...], w_ref[D_ATTN:, :])

    tok = lambda w_: pl.BlockSpec((tm, w_), lambda i: (i, 0))
    return pl.pallas_call(
        body, name="out_fwd", grid=(SEQ // tm,),
        in_specs=[tok(D_MODEL), tok(D_ATTN), tok(D_SSM), _full((D_MODEL, D_MODEL))],
        out_specs=tok(D_MODEL), out_shape=jax.ShapeDtypeStruct((SEQ, D_MODEL), f32),
        compiler_params=_params("arbitrary"),
    )(x, attn, ssm, w_out)


def out_bwd(dx1, attn, ssm, w_out, tm=512):
    def body(d_ref, a_ref, s_ref, w_ref, da_ref, ds_ref, dw_ref):
        @pl.when(pl.program_id(0) == 0)
        def _():
            dw_ref[...] = jnp.zeros_like(dw_ref)

        d = d_ref[...].astype(bf16)
        dcat = _bdot_nt(d, w_ref[...])
        da_ref[...] = dcat[:, :D_ATTN]
        ds_ref[...] = dcat[:, D_ATTN:]
        dw_ref[:D_ATTN, :] += _bdot_tn(a_ref[...], d)
        dw_ref[D_ATTN:, :] += _bdot_tn(s_ref[...], d)

    tok = lambda w_: pl.BlockSpec((tm, w_), lambda i: (i, 0))
    return pl.pallas_call(
        body, name="out_bwd", grid=(SEQ // tm,),
        in_specs=[tok(D_MODEL), tok(D_ATTN), tok(D_SSM), _full((D_MODEL, D_MODEL))],
        out_specs=[tok(D_ATTN), tok(D_SSM), _full((D_MODEL, D_MODEL))],
        out_shape=[jax.ShapeDtypeStruct((SEQ, D_ATTN), f32), jax.ShapeDtypeStruct((SEQ, D_SSM), f32),
                   jax.ShapeDtypeStruct((D_MODEL, D_MODEL), f32)],
        compiler_params=_params("arbitrary"),
    )(dx1, attn, ssm, w_out)


def mlp_fwd(x1, g, w_up, w_down, layer, tm=512):
    def body(x_ref, g_ref, wu_ref, wd_ref, o_ref, u_ref, h_scr):
        j = pl.program_id(1)

        @pl.when(j == 0)
        def _():
            xv = x_ref[...]
            h_scr[...] = (xv * _rms(xv) * g_ref[...]).astype(bf16)
            o_ref[...] = xv

        u = jnp.dot(h_scr[...], wu_ref[...], preferred_element_type=f32)
        u_ref[...] = u
        a = jnp.square(jnp.maximum(u, 0.0))
        o_ref[...] += _bdot(a, wd_ref[...])

    return pl.pallas_call(
        body, name="mlp_fwd", grid=(SEQ // tm, N_CHIPS),
        in_specs=[pl.BlockSpec((tm, D_MODEL), lambda i, j: (i, 0)), _full((1, D_MODEL)),
                  pl.BlockSpec((None, None, D_MODEL, FF_TILE), lambda i, j: (j, layer, 0, 0)),
                  pl.BlockSpec((None, None, FF_TILE, D_MODEL), lambda i, j: (j, layer, 0, 0))],
        out_specs=[pl.BlockSpec((tm, D_MODEL), lambda i, j: (i, 0)), pl.BlockSpec((tm, FF_TILE), lambda i, j: (i, j))],
        out_shape=[jax.ShapeDtypeStruct((SEQ, D_MODEL), f32), jax.ShapeDtypeStruct((SEQ, D_FF), f32)],
        scratch_shapes=[pltpu.VMEM((tm, D_MODEL), bf16)],
        compiler_params=_params("arbitrary", "arbitrary"),
    )(x1, g, w_up, w_down)


def mlp_bwd_data(dx2, u, x1, g, w_up, w_down, layer, tm=512):
    def body(d_ref, u_ref, x_ref, g_ref, wu_ref, wd_ref, dx_ref, du_ref, dg_ref, dh_scr):
        i, j = pl.program_id(0), pl.program_id(1)

        @pl.when(jnp.logical_and(i == 0, j == 0))
        def _():
            dg_ref[...] = jnp.zeros_like(dg_ref)

        @pl.when(j == 0)
        def _():
            dh_scr[...] = jnp.zeros_like(dh_scr)

        da = _bdot_nt(d_ref[...], wd_ref[...])
        du = (da * (2.0 * jnp.maximum(u_ref[...], 0.0))).astype(bf16)
        du_ref[...] = du
        dh_scr[...] += _bdot_nt(du, wu_ref[...])

        @pl.when(j == N_CHIPS - 1)
        def _():
            xv = x_ref[...]
            r = _rms(xv)
            xhat = xv * r
            dh = dh_scr[...]
            dg_ref[...] += jnp.sum(dh * xhat, axis=0, keepdims=True)
            dx_ref[...] = d_ref[...] + _rms_bwd(dh, xhat, r, g_ref[...])

    return pl.pallas_call(
        body, name="mlp_bwd_data", grid=(SEQ // tm, N_CHIPS),
        in_specs=[pl.BlockSpec((tm, D_MODEL), lambda i, j: (i, 0)), pl.BlockSpec((tm, FF_TILE), lambda i, j: (i, j)),
                  pl.BlockSpec((tm, D_MODEL), lambda i, j: (i, 0)), _full((1, D_MODEL)),
                  pl.BlockSpec((None, None, D_MODEL, FF_TILE), lambda i, j: (j, layer, 0, 0)),
                  pl.BlockSpec((None, None, FF_TILE, D_MODEL), lambda i, j: (j, layer, 0, 0))],
        out_specs=[pl.BlockSpec((tm, D_MODEL), lambda i, j: (i, 0)), pl.BlockSpec((tm, FF_TILE), lambda i, j: (i, j)),
                   _full((1, D_MODEL))],
        out_shape=[jax.ShapeDtypeStruct((SEQ, D_MODEL), f32), jax.ShapeDtypeStruct((SEQ, D_FF), bf16),
                   jax.ShapeDtypeStruct((1, D_MODEL), f32)],
        scratch_shapes=[pltpu.VMEM((tm, D_MODEL), f32)],
        compiler_params=_params("arbitrary", "arbitrary"),
    )(dx2, u, x1, g, w_up, w_down)


def mlp_bwd_weights(dx2, u, du, x1, g, tm=512):
    def body(d_ref, u_ref, du_ref, x_ref, g_ref, dwu_ref, dwd_ref):
        @pl.when(pl.program_id(1) == 0)
        def _():
            dwu_ref[...] = jnp.zeros_like(dwu_ref)
            dwd_ref[...] = jnp.zeros_like(dwd_ref)

        xv = x_ref[...]
        h = xv * _rms(xv) * g_ref[...]
        dwu_ref[...] += _bdot_tn(h, du_ref[...])
        a = jnp.square(jnp.maximum(u_ref[...], 0.0))
        dwd_ref[...] += _bdot_tn(a, d_ref[...])

    return pl.pallas_call(
        body, name="mlp_bwd_weights", grid=(N_CHIPS, SEQ // tm),
        in_specs=[pl.BlockSpec((tm, D_MODEL), lambda j, i: (i, 0)), pl.BlockSpec((tm, FF_TILE), lambda j, i: (i, j)),
                  pl.BlockSpec((tm, FF_TILE), lambda j, i: (i, j)), pl.BlockSpec((tm, D_MODEL), lambda j, i: (i, 0)),
                  _full((1, D_MODEL))],
        out_specs=[pl.BlockSpec((None, D_MODEL, FF_TILE), lambda j, i: (j, 0, 0)),
                   pl.BlockSpec((None, FF_TILE, D_MODEL), lambda j, i: (j, 0, 0))],
        out_shape=[jax.ShapeDtypeStruct((N_CHIPS, D_MODEL, FF_TILE), f32), jax.ShapeDtypeStruct((N_CHIPS, FF_TILE, D_MODEL), f32)],
        compiler_params=_params("arbitrary", "arbitrary"),
    )(dx2, u, du, x1, g)


def loss_head(y, target, tm=512):
    def body(y_ref, t_ref, dy_ref, l_ref):
        @pl.when(pl.program_id(0) == 0)
        def _():
            l_ref[...] = jnp.zeros_like(l_ref)

        d = y_ref[...] - t_ref[...]
        dy_ref[...] = d * (1.0 / D_MODEL)
        part = jnp.sum(jnp.mean(d * d, axis=-1, keepdims=True), axis=0, keepdims=True)
        l_ref[...] += 0.5 * part

    tok = pl.BlockSpec((tm, D_MODEL), lambda i: (i, 0))
    return pl.pallas_call(
        body, name="loss_head", grid=(SEQ // tm,), in_specs=[tok, tok], out_specs=[tok, _full((1, 1))],
        out_shape=[jax.ShapeDtypeStruct((SEQ, D_MODEL), f32), jax.ShapeDtypeStruct((1, 1), f32)],
        compiler_params=_params("arbitrary"),
    )(y, target)


def _pad_lane(v):
    return jnp.pad(v, (0, LANE - v.shape[0]))[None, :]


def local_step(x, target, w):
    bucket = jnp.asarray(_bucket_table())
    bias = bias_build(w["rel_bias"], bucket)
    saved = []
    for l in range(DEPTH):
        g_mix = w["mix_norm_g"][l][None, :]
        proj = in_fwd(x, g_mix, w["w_in"][l])
        qg, kg = w["q_gain"][l][None, :], w["k_gain"][l][None, :]
        attn = attn_fwd(proj, qg, kg, w["sinks"][l], bias)
        conv_b = w["conv_b"][l][None, :]
        act = conv_fwd(proj, w["conv_w"][l], conv_b)
        dtb, alog, dsk = _pad_lane(w["dt_bias"][l]), _pad_lane(w["a_log"][l]), _pad_lane(w["d_skip"][l])
        ng = w["ssm_norm_g"][l][None, :]
        ssm, ypre, states = ssd_fwd(act, proj, dtb, alog, dsk, ng)
        x1 = out_fwd(x, attn, ssm, w["w_out"][l])
        g_mlp = w["mlp_norm_g"][l][None, :]
        x2, u = mlp_fwd(x1, g_mlp, w["w_up"], w["w_down"], l)
        saved.append(dict(x=x, proj=proj, attn=attn, act=act, ssm=ssm, ypre=ypre, states=states, x1=x1, u=u,
                          g_mix=g_mix, qg=qg, kg=kg, conv_b=conv_b, dtb=dtb, alog=alog, dsk=dsk, ng=ng, g_mlp=g_mlp))
        x = x2
    dx, loss = loss_head(x, target)
    grads = [None] * DEPTH
    dbands = [None] * DEPTH
    for l in reversed(range(DEPTH)):
        s = saved[l]
        dx1, du, dg_mlp = mlp_bwd_data(dx, s["u"], s["x1"], s["g_mlp"], w["w_up"], w["w_down"], l)
        dw_up, dw_down = mlp_bwd_weights(dx, s["u"], du, s["x1"], s["g_mlp"])
        dattn, dssm, dw_out = out_bwd(dx1, s["attn"], s["ssm"], w["w_out"][l])
        dact, ddt, dz, dng, dpar = ssd_bwd(s["act"], s["proj"], s["ypre"], s["states"], dssm, s["dtb"], s["alog"], s["dsk"], s["ng"])
        dxbc, dconv_w, dconv_b = conv_bwd(s["proj"], dact, w["conv_w"][l], s["conv_b"])
        dq, dk, dv, dband, dsink, dqg, dkg = attn_bwd(s["proj"], dattn, s["qg"], s["kg"], w["sinks"][l], bias)
        dx, dw_in, dg_mix = in_bwd(dq, dz, dxbc, dk, dv, ddt, s["x"], s["g_mix"], w["w_in"][l], dx1)
        dbands[l] = dband
        grads[l] = dict(mix_norm_g=dg_mix[0], w_in=dw_in, q_gain=dqg[0], k_gain=dkg[0], sinks=dsink[0, :N_Q_HEADS],
                        conv_w=dconv_w, conv_b=dconv_b[0], dt_bias=dpar[0, :SSM_HEADS], a_log=dpar[1, :SSM_HEADS],
                        d_skip=dpar[2, :SSM_HEADS], ssm_norm_g=dng[0], w_out=dw_out, mlp_norm_g=dg_mlp[0],
                        w_up=dw_up, w_down=dw_down)
    big = ("w_in", "w_out", "w_up", "w_down")
    out = {k: [grads[l][k] for l in range(DEPTH)] if k in big else jnp.stack([grads[l][k] for l in range(DEPTH)])
           for k in grads[0]}
    out["rel_bias"] = bias_bwd(dbands[0], dbands[1], bucket)[:, :N_Q_HEADS]
    return loss, dx, out


MESH = pl.DeviceIdType.MESH
HBM = pl.BlockSpec(memory_space=pltpu.HBM)
N_PEER_CHIPS = N_CHIPS - 1
N_DEVICES = 8


def _coords():
    return lax.axis_index("x"), lax.axis_index("y"), lax.axis_index("c")


def _peer_chips(x, y):
    return [(1 - x, y), (x, 1 - y), (1 - x, 1 - y)]


def _remote(src, dst, send_sem, recv_sem, device):
    return pltpu.make_async_remote_copy(src_ref=src, dst_ref=dst, send_sem=send_sem, recv_sem=recv_sem,
                                        device_id=device, device_id_type=MESH)


def all_gather_weights(shards, conv):
    n = len(shards)

    def body(*refs):
        srcs, conv_ref = refs[:n], refs[n]
        dsts, gconv = refs[n + 1:2 * n + 1], refs[2 * n + 1]
        ssem, rsem, fsem, frsem = refs[2 * n + 2:]
        x, y, c = _coords()
        k_me = 2 * x + y
        chips = _peer_chips(x, y)
        sibling = (x, y, 1 - c)

        def half(w, k, cc):
            hr = shards[w].shape[1] // 2
            return dsts[w].at[k, :, pl.ds(cc * hr, hr), :]

        def my_half(w):
            hr = shards[w].shape[1] // 2
            return srcs[w].at[:, pl.ds(c * hr, hr), :]

        sends = []
        for j, chip in enumerate(chips):
            for w in range(n):
                sends.append(_remote(my_half(w), half(w, k_me, c), ssem.at[w * 3 + j], rsem.at[w * 3 + j], (*chip, c)))
            sends.append(_remote(conv_ref, gconv.at[k_me], ssem.at[n * 3 + j], rsem.at[n * 3 + j], (*chip, c)))
        for cp in sends:
            cp.start()
        passed = []
        for j, chip in enumerate(chips):
            kj = 2 * chip[0] + chip[1]
            for w in range(n):
                got = half(w, kj, c)
                _remote(got, got, ssem.at[w * 3 + j], rsem.at[w * 3 + j], (*chip, c)).wait_recv()
                fw = _remote(got, got, fsem.at[w * 3 + j], frsem.at[w * 3 + j], sibling)
                fw.start()
                passed.append(fw)
            _remote(gconv.at[kj], gconv.at[kj], ssem.at[n * 3 + j], rsem.at[n * 3 + j], (*chip, c)).wait_recv()
        for j, chip in enumerate(chips):
            kj = 2 * chip[0] + chip[1]
            for w in range(n):
                got = half(w, kj, 1 - c)
                _remote(got, got, fsem.at[w * 3 + j], frsem.at[w * 3 + j], sibling).wait_recv()
        for cp in sends + passed:
            cp.wait_send()

    out_shape = [jax.ShapeDtypeStruct((N_CHIPS,) + s.shape, s.dtype) for s in shards]
    out_shape.append(jax.ShapeDtypeStruct((N_CHIPS,) + conv.shape, conv.dtype))
    n_ici = (n + 1) * N_PEER_CHIPS
    return pl.pallas_call(
        body, name="all_gather_weights", out_shape=out_shape,
        in_specs=[HBM] * (n + 1), out_specs=[HBM] * (n + 1),
        scratch_shapes=[pltpu.SemaphoreType.DMA((n_ici,)), pltpu.SemaphoreType.DMA((n_ici,)),
                        pltpu.SemaphoreType.DMA((n * N_PEER_CHIPS,)), pltpu.SemaphoreType.DMA((n * N_PEER_CHIPS,))],
    )(*shards, conv)


def small_all_reduce(vec):
    def body(v_ref, o_ref, gat, ssem, rsem):
        x, y, c = _coords()
        me = 4 * x + 2 * y + c
        gat[me] = v_ref[...]
        sends = []
        for t in range(1, N_DEVICES):
            peer = (x ^ (t >> 2), y ^ ((t >> 1) & 1), c ^ (t & 1))
            cp = _remote(v_ref, gat.at[me], ssem.at[t - 1], rsem.at[t - 1], peer)
            cp.start()
            sends.append(cp)
        for t in range(1, N_DEVICES):
            peer = (x ^ (t >> 2), y ^ ((t >> 1) & 1), c ^ (t & 1))
            slot = gat.at[4 * peer[0] + 2 * peer[1] + peer[2]]
            _remote(slot, slot, ssem.at[t - 1], rsem.at[t - 1], peer).wait_recv()
        for cp in sends:
            cp.wait_send()
        acc = gat[0]
        for d in range(1, N_DEVICES):
            acc = acc + gat[d]
        o_ref[...] = acc

    return pl.pallas_call(
        body, name="small_all_reduce", out_shape=jax.ShapeDtypeStruct(vec.shape, vec.dtype),
        in_specs=[pl.BlockSpec(memory_space=pltpu.VMEM)], out_specs=pl.BlockSpec(memory_space=pltpu.VMEM),
        scratch_shapes=[pltpu.VMEM((N_DEVICES,) + vec.shape, vec.dtype), pltpu.SemaphoreType.DMA((N_DEVICES - 1,)),
                        pltpu.SemaphoreType.DMA((N_DEVICES - 1,))],
    )(vec)


def sibling_exchange(grads):
    n = len(grads)

    def body(*refs):
        srcs, dsts, ssem, rsem = refs[:n], refs[n:2 * n], refs[2 * n], refs[2 * n + 1]
        x, y, c = _coords()
        cps = [_remote(srcs[i].at[:, 1 - c], dsts[i], ssem.at[i], rsem.at[i], (x, y, 1 - c)) for i in range(n)]
        for cp in cps:
            cp.start()
        for cp in cps:
            cp.wait()

    return pl.pallas_call(
        body, name="sibling_exchange",
        out_shape=[jax.ShapeDtypeStruct((N_CHIPS,) + g.shape[2:], g.dtype) for g in grads],
        in_specs=[HBM] * n, out_specs=[HBM] * n,
        scratch_shapes=[pltpu.SemaphoreType.DMA((n,)), pltpu.SemaphoreType.DMA((n,))],
    )(*grads)


def pair_sum(g0, g1, r0, r1, c_arr, tr):
    _, _, rb, cc = g0.shape
    nr = rb // tr

    def body(c_ref, g0_ref, g1_ref, r0_ref, r1_ref, p_ref, pb_ref):
        l = pl.program_id(0)

        @pl.when(l == 0)
        def _():
            s = g0_ref[...] + r0_ref[...]
            p_ref[...] = s
            pb_ref[...] = s.astype(bf16)

        @pl.when(l == 1)
        def _():
            s = g1_ref[...] + r1_ref[...]
            p_ref[...] = s
            pb_ref[...] = s.astype(bf16)

    lay0 = lambda l, a, last: jnp.where(l == 0, a, last)
    lay1 = lambda l, a: jnp.where(l == 1, a, 0)
    out_spec = pl.BlockSpec((None, None, tr, cc), lambda l, k, r, c: (l, k, r, 0))
    return pl.pallas_call(
        body, name="pair_sum",
        grid_spec=pltpu.PrefetchScalarGridSpec(
            num_scalar_prefetch=1, grid=(DEPTH, N_CHIPS, nr),
            in_specs=[pl.BlockSpec((None, None, tr, cc), lambda l, k, r, c: (lay0(l, k, N_CHIPS - 1), c[0], lay0(l, r, nr - 1), 0)),
                      pl.BlockSpec((None, None, tr, cc), lambda l, k, r, c: (lay1(l, k), c[0], lay1(l, r), 0)),
                      pl.BlockSpec((None, tr, cc), lambda l, k, r, c: (lay0(l, k, N_CHIPS - 1), lay0(l, r, nr - 1), 0)),
                      pl.BlockSpec((None, tr, cc), lambda l, k, r, c: (lay1(l, k), lay1(l, r), 0))],
            out_specs=[out_spec, out_spec]),
        out_shape=[jax.ShapeDtypeStruct((DEPTH, N_CHIPS, rb, cc), f32), jax.ShapeDtypeStruct((DEPTH, N_CHIPS, rb, cc), bf16)],
        compiler_params=_params("arbitrary", "arbitrary", "arbitrary"),
    )(c_arr, g0, g1, r0, r1)


def chip_exchange(parts):
    n = len(parts)

    def body(*refs):
        srcs, dsts, ssem, rsem = refs[:n], refs[n:2 * n], refs[2 * n], refs[2 * n + 1]
        x, y, c = _coords()
        cps = []
        for j, chip in enumerate(_peer_chips(x, y)):
            kj = 2 * chip[0] + chip[1]
            for i in range(n):
                cps.append(_remote(srcs[i].at[:, kj], dsts[i].at[j], ssem.at[i * 3 + j], rsem.at[i * 3 + j], (*chip, c)))
        for cp in cps:
            cp.start()
        for cp in cps:
            cp.wait()

    return pl.pallas_call(
        body, name="chip_exchange",
        out_shape=[jax.ShapeDtypeStruct((N_PEER_CHIPS, DEPTH) + p.shape[2:], p.dtype) for p in parts],
        in_specs=[HBM] * n, out_specs=[HBM] * n,
        scratch_shapes=[pltpu.SemaphoreType.DMA((n * N_PEER_CHIPS,)), pltpu.SemaphoreType.DMA((n * N_PEER_CHIPS,))],
    )(*parts)


def shard_sum(p32, r2, kc_arr, tr):
    _, _, rb, cc = p32.shape
    nr = rb // tr

    def body(kc_ref, p_ref, r_ref, o_ref):
        o_ref[...] = ((p_ref[...] + r_ref[0].astype(f32)) + r_ref[1].astype(f32)) + r_ref[2].astype(f32)

    return pl.pallas_call(
        body, name="shard_sum",
        grid_spec=pltpu.PrefetchScalarGridSpec(
            num_scalar_prefetch=1, grid=(DEPTH, nr),
            in_specs=[pl.BlockSpec((None, None, tr, cc), lambda l, r, kc: (l, kc[0], r, 0)),
                      pl.BlockSpec((N_PEER_CHIPS, None, tr, cc), lambda l, r, kc: (0, l, r, 0))],
            out_specs=pl.BlockSpec((None, None, tr, cc), lambda l, r, kc: (l, kc[1], r, 0))),
        out_shape=jax.ShapeDtypeStruct((DEPTH, 2, rb, cc), f32),
        compiler_params=_params("arbitrary", "arbitrary"),
    )(kc_arr, p32, r2)


def sibling_share(bufs):
    n = len(bufs)

    def body(*refs):
        bufs_, ssem, rsem = refs[n:2 * n], refs[2 * n], refs[2 * n + 1]
        x, y, c = _coords()
        cps = [_remote(bufs_[i].at[:, c], bufs_[i].at[:, c], ssem.at[i], rsem.at[i], (x, y, 1 - c)) for i in range(n)]
        for cp in cps:
            cp.start()
        for cp in cps:
            cp.wait_send()
        for i in range(n):
            other = bufs_[i].at[:, 1 - c]
            _remote(other, other, ssem.at[i], rsem.at[i], (x, y, 1 - c)).wait_recv()

    return pl.pallas_call(
        body, name="sibling_share",
        out_shape=[jax.ShapeDtypeStruct(b.shape, b.dtype) for b in bufs],
        in_specs=[HBM] * n, out_specs=[HBM] * n, input_output_aliases={i: i for i in range(n)},
        scratch_shapes=[pltpu.SemaphoreType.DMA((n,)), pltpu.SemaphoreType.DMA((n,))],
    )(*bufs)


def _adamw_math(w, g, m, v):
    m_new = ADAM_B1 * m + (1.0 - ADAM_B1) * g
    v_new = ADAM_B2 * v + (1.0 - ADAM_B2) * jnp.square(g)
    m_hat = m_new / (1.0 - ADAM_B1 ** ADAM_STEP)
    v_hat = v_new / (1.0 - ADAM_B2 ** ADAM_STEP)
    delta = -ADAM_LR * (m_hat / (jnp.sqrt(v_hat) + ADAM_EPS) + ADAM_WD * w)
    return delta, m_new, v_new


def adamw_shard(w, g, m, v):
    depth, rows, cols = w.shape
    half = rows // 2
    tr = min(256, half)
    nr = half // tr

    def body(w_ref, g_ref, m_ref, v_ref, go_ref, d_ref, nm_ref, nv_ref):
        gv = g_ref[...]
        go_ref[...] = gv
        d_ref[...], nm_ref[...], nv_ref[...] = _adamw_math(w_ref[...], gv, m_ref[...], v_ref[...])

    spec = pl.BlockSpec((None, tr, cols), lambda l, h, r: (l, h * nr + r, 0))
    gspec = pl.BlockSpec((None, None, tr, cols), lambda l, h, r: (l, h, r, 0))
    return pl.pallas_call(
        body, name="adamw_shard", grid=(depth, 2, nr), in_specs=[spec, gspec, spec, spec], out_specs=[spec] * 4,
        out_shape=[jax.ShapeDtypeStruct(w.shape, f32)] * 4,
        compiler_params=_params("arbitrary", "arbitrary", "arbitrary"),
    )(w, g, m, v)


def adamw_small(w, g, m, v):
    def body(w_ref, g_ref, m_ref, v_ref, d_ref, nm_ref, nv_ref):
        d_ref[...], nm_ref[...], nv_ref[...] = _adamw_math(w_ref[...], g_ref[...], m_ref[...], v_ref[...])

    return pl.pallas_call(
        body, name="adamw_small", out_shape=[jax.ShapeDtypeStruct(w.shape, f32)] * 3,
    )(w, g, m, v)


WEIGHTS = ("mix_norm_g", "w_in", "q_gain", "k_gain", "sinks", "rel_bias", "conv_w", "conv_b", "dt_bias", "a_log", "d_skip",
           "ssm_norm_g", "w_out", "mlp_norm_g", "w_up", "w_down")
BIG = ("w_in", "w_out", "w_up", "w_down")
SMALL = tuple(n for n in WEIGHTS if n not in BIG)
PACK_COLS = 1024
PACK_ROWS = 16


def _pack(named):
    flat = jnp.concatenate([named[n].reshape(-1) for n in SMALL])
    return jnp.pad(flat, (0, PACK_ROWS * PACK_COLS - flat.shape[0])).reshape(PACK_ROWS, PACK_COLS)


def _unpack(buf, shapes):
    flat = buf.reshape(-1)
    out, at = {}, 0
    for n in SMALL:
        size = int(np.prod(shapes[n]))
        out[n] = flat[at:at + size].reshape(shapes[n])
        at += size
    return out


def kernel(x, mix_norm_g, w_in, q_gain, k_gain, sinks, rel_bias, conv_w, conv_b, dt_bias, a_log, d_skip, ssm_norm_g, w_out, mlp_norm_g, w_up, w_down, loss_target, m_mix_norm_g, m_w_in, m_q_gain, m_k_gain, m_sinks, m_rel_bias, m_conv_w, m_conv_b, m_dt_bias, m_a_log, m_d_skip, m_ssm_norm_g, m_w_out, m_mlp_norm_g, m_w_up, m_w_down, v_mix_norm_g, v_w_in, v_q_gain, v_k_gain, v_sinks, v_rel_bias, v_conv_w, v_conv_b, v_dt_bias, v_a_log, v_d_skip, v_ssm_norm_g, v_w_out, v_mlp_norm_g, v_w_up, v_w_down):
    wts = dict(mix_norm_g=mix_norm_g, w_in=w_in, q_gain=q_gain, k_gain=k_gain, sinks=sinks, rel_bias=rel_bias, conv_w=conv_w,
               conv_b=conv_b, dt_bias=dt_bias, a_log=a_log, d_skip=d_skip, ssm_norm_g=ssm_norm_g, w_out=w_out,
               mlp_norm_g=mlp_norm_g, w_up=w_up, w_down=w_down)
    mom = dict(mix_norm_g=m_mix_norm_g, w_in=m_w_in, q_gain=m_q_gain, k_gain=m_k_gain, sinks=m_sinks, rel_bias=m_rel_bias,
               conv_w=m_conv_w, conv_b=m_conv_b, dt_bias=m_dt_bias, a_log=m_a_log, d_skip=m_d_skip, ssm_norm_g=m_ssm_norm_g,
               w_out=m_w_out, mlp_norm_g=m_mlp_norm_g, w_up=m_w_up, w_down=m_w_down)
    var = dict(mix_norm_g=v_mix_norm_g, w_in=v_w_in, q_gain=v_q_gain, k_gain=v_k_gain, sinks=v_sinks, rel_bias=v_rel_bias,
               conv_w=v_conv_w, conv_b=v_conv_b, dt_bias=v_dt_bias, a_log=v_a_log, d_skip=v_d_skip, ssm_norm_g=v_ssm_norm_g,
               w_out=v_w_out, mlp_norm_g=v_mlp_norm_g, w_up=v_w_up, w_down=v_w_down)
    xi, yi, ci = _coords()
    k_me = 2 * xi + yi
    c_arr = jnp.reshape(ci, (1,)).astype(jnp.int32)
    kc_arr = jnp.stack([k_me, ci]).astype(jnp.int32)

    mine = [wts[n].astype(bf16) for n in BIG] + [conv_w]
    gathered = all_gather_weights(mine[:-1], conv_w)
    g_in, g_out, g_up, g_dn, g_conv = [lax.dynamic_update_slice(g, s[None], (k_me,) + (0,) * s.ndim)
                                       for g, s in zip(gathered, mine)]
    full = {n: wts[n] for n in SMALL}
    full["w_in"] = _to_aligned(jnp.transpose(g_in, (1, 2, 0, 3)).reshape(DEPTH, D_MODEL, D_IN))
    full["w_out"] = jnp.transpose(g_out, (1, 0, 2, 3)).reshape(DEPTH, D_MODEL, D_MODEL)
    full["w_up"], full["w_down"] = g_up, g_dn
    full["conv_w"] = jnp.transpose(g_conv, (1, 2, 0, 3)).reshape(DEPTH, CONV_WIDTH, D_CONV)

    loss, dx, grads = local_step(x[0], loss_target[0], full)
    loss = lax.psum(loss[0, 0], ("x", "y", "c"))

    small_shapes = {n: grads[n].shape for n in SMALL}
    small = _unpack(small_all_reduce(_pack(grads)), small_shapes)
    cols = conv_w.shape[-1]
    small["conv_w"] = lax.dynamic_slice_in_dim(small["conv_w"], k_me * cols, cols, axis=2)

    def view(name, g):
        if name == "w_in":
            g = jnp.transpose(_from_aligned(g).reshape(D_MODEL, N_CHIPS, D_IN // N_CHIPS), (1, 0, 2))
        rows = wts[name].shape[1] // 2
        return g.reshape(N_CHIPS, 2, rows, wts[name].shape[2])

    views = [view(n, grads[n][l]) for n in BIG for l in range(DEPTH)]
    recv = sibling_exchange(views)
    p32s, pbs = [], []
    for i, n in enumerate(BIG):
        tr = min(256, views[2 * i].shape[2])
        p32, pb = pair_sum(views[2 * i], views[2 * i + 1], recv[2 * i], recv[2 * i + 1], c_arr, tr)
        p32s.append(p32)
        pbs.append(pb)
    r2s = chip_exchange(pbs)
    reds = [shard_sum(p32s[i], r2s[i], kc_arr, min(256, p32s[i].shape[2])) for i in range(len(BIG))]
    fulls = sibling_share(reds)

    g_out_d, d_out_d, m_out_d, v_out_d = {}, {}, {}, {}
    for i, n in enumerate(BIG):
        g_out_d[n], d_out_d[n], m_out_d[n], v_out_d[n] = adamw_shard(wts[n], fulls[i], mom[n], var[n])
    shard_shapes = {n: wts[n].shape for n in SMALL}
    d, nm, nv = adamw_small(_pack(wts), _pack(small), _pack(mom), _pack(var))
    for dst, buf in ((d_out_d, d), (m_out_d, nm), (v_out_d, nv)):
        dst.update(_unpack(buf, shard_shapes))
    g_out_d.update(small)

    return (loss, dx[None], *[g_out_d[n] for n in WEIGHTS], *[d_out_d[n] for n in WEIGHTS],
            *[m_out_d[n] for n in WEIGHTS], *[v_out_d[n] for n in WEIGHTS])
```

```python
import functools

import numpy as np
import jax
import jax.numpy as jnp
from jax import lax
from jax.experimental import pallas as pl
from jax.experimental.pallas import tpu as pltpu

f32 = jnp.float32
bf16 = jnp.bfloat16

SEQ = 2048
D_MODEL = 1024
DEPTH = 2
HEAD_DIM = 64
N_Q_HEADS = 8
N_KV_HEADS = 2
Q_PER_KV = N_Q_HEADS // N_KV_HEADS
BLOCK = 128
N_BLOCKS = SEQ // BLOCK
N_BUCKETS = 32
MAX_DISTANCE = 128
SSM_HEADS = 8
SSM_HEAD_DIM = 64
SSM_GROUPS = 2
HEADS_PER_GROUP = SSM_HEADS // SSM_GROUPS
SSM_STATE = 128
CONV_WIDTH = 4
CHUNK = 128
N_CHUNKS = SEQ // CHUNK
D_FF = 4 * D_MODEL
D_ATTN = N_Q_HEADS * HEAD_DIM
D_KV = N_KV_HEADS * HEAD_DIM
D_SSM = SSM_HEADS * SSM_HEAD_DIM
D_BC = SSM_GROUPS * SSM_STATE
D_CONV = D_SSM + 2 * D_BC
D_IN = D_ATTN + 2 * D_KV + D_SSM + D_CONV + SSM_HEADS
EPS = 1e-6
NEG = -1e30
N_CHIPS = 4
FF_TILE = D_FF // N_CHIPS

LANE = 128
PW = D_ATTN + D_SSM + D_CONV + 2 * D_KV + LANE
OFF_Q, OFF_Z, OFF_X, OFF_K, OFF_V, OFF_DT = 0, 512, 1024, 2048, 2176, 2304

ADAM_LR = 0.001
ADAM_B1 = 0.9
ADAM_B2 = 0.999
ADAM_EPS = 1e-08
ADAM_WD = 0.01
ADAM_STEP = 10

VMEM_LIMIT = 56 * 1024 * 1024


def _params(*sem):
    return pltpu.CompilerParams(dimension_semantics=tuple(sem), vmem_limit_bytes=VMEM_LIMIT)


def _bdot(a, b):
    return jnp.dot(a.astype(bf16), b.astype(bf16), preferred_element_type=f32)


def _bdot_nt(a, b):
    return lax.dot_general(a.astype(bf16), b.astype(bf16), (((1,), (1,)), ((), ())), preferred_element_type=f32)


def _bdot_tn(a, b):
    return lax.dot_general(a.astype(bf16), b.astype(bf16), (((0,), (0,)), ((), ())), preferred_element_type=f32)


def _hdot(a, b):
    return jnp.dot(a, b, precision=lax.Precision.HIGHEST, preferred_element_type=f32)


def _sigmoid(x):
    return 1.0 / (1.0 + jnp.exp(-x))


def _softplus(x):
    return jnp.maximum(x, 0.0) + jnp.log1p(jnp.exp(-jnp.abs(x)))


def _rms(x):
    return lax.rsqrt(jnp.mean(x * x, axis=-1, keepdims=True) + EPS)


def _rms_bwd(dy, xhat, r, g):
    t = dy * g
    return r * (t - xhat * jnp.mean(t * xhat, axis=-1, keepdims=True))


def _full(shape):
    return pl.BlockSpec(shape, lambda *_: (0,) * len(shape))


def _to_aligned(w):
    q, k, v, z, xbc, dt = jnp.split(w, [512, 640, 768, 1280, 2304], axis=-1)
    pad = jnp.zeros(w.shape[:-1] + (LANE - SSM_HEADS,), w.dtype)
    return jnp.concatenate([q, z, xbc, k, v, dt, pad], axis=-1)


def _from_aligned(w):
    q, z, xbc, k, v, dt = (w[..., OFF_Q:OFF_Z], w[..., OFF_Z:OFF_X], w[..., OFF_X:OFF_K], w[..., OFF_K:OFF_V],
                           w[..., OFF_V:OFF_DT], w[..., OFF_DT:OFF_DT + SSM_HEADS])
    return jnp.concatenate([q, k, v, z, xbc, dt], axis=-1)


def _bucket_table():
    qi = np.arange(BLOCK)[:, None]
    kj = np.arange(2 * BLOCK)[None, :]
    dist = qi + BLOCK - kj
    ok = (dist >= 0) & (dist < 128)
    d = np.clip(dist, 0, None)
    max_exact = N_BUCKETS // 2
    d_f = np.maximum(d, 1).astype(np.float32)
    large = max_exact + (np.log(d_f / np.float32(max_exact)) / np.float32(np.log(MAX_DISTANCE / max_exact))
                         * np.float32(N_BUCKETS - max_exact)).astype(np.int32)
    large = np.minimum(large, N_BUCKETS - 1)
    bucket = np.where(d < max_exact, d, large)
    return np.where(ok, bucket, -1).astype(np.int32)


def bias_build(rel_bias, bucket):
    def body(rel_ref, bkt_ref, o_ref):
        bkt = bkt_ref[...]
        for h in range(N_Q_HEADS):
            acc = jnp.where(bkt < 0, NEG, 0.0).astype(f32)
            for b in range(N_BUCKETS):
                acc = acc + jnp.where(bkt == b, rel_ref[b, h], 0.0)
            o_ref[h] = acc

    return pl.pallas_call(
        body, name="bias_build", out_shape=jax.ShapeDtypeStruct((N_Q_HEADS, BLOCK, 2 * BLOCK), f32),
        in_specs=[pl.BlockSpec(memory_space=pltpu.SMEM), pl.BlockSpec(memory_space=pltpu.VMEM)],
        out_specs=pl.BlockSpec(memory_space=pltpu.VMEM),
    )(rel_bias, bucket)


def bias_bwd(dband0, dband1, bucket):
    def body(d0_ref, d1_ref, bkt_ref, o_ref):
        bkt = bkt_ref[...]
        o_ref[...] = jnp.zeros_like(o_ref)
        for h in range(N_Q_HEADS):
            d = d0_ref[h] + d1_ref[h]
            for b in range(N_BUCKETS):
                part = jnp.sum(jnp.where(bkt == b, d, 0.0), axis=1, keepdims=True)
                o_ref[b:b + 1, h:h + 1] = jnp.sum(part, axis=0, keepdims=True)

    return pl.pallas_call(
        body, name="bias_bwd", out_shape=jax.ShapeDtypeStruct((N_BUCKETS, LANE), f32),
    )(dband0, dband1, bucket)


def in_fwd(x, g, w, tm=256):
    def body(x_ref, g_ref, w_ref, o_ref):
        xv = x_ref[...]
        h = xv * _rms(xv) * g_ref[...]
        o_ref[...] = _bdot(h, w_ref[...])

    return pl.pallas_call(
        body, name="in_fwd", grid=(SEQ // tm,),
        in_specs=[pl.BlockSpec((tm, D_MODEL), lambda i: (i, 0)), _full((1, D_MODEL)), _full((D_MODEL, PW))],
        out_specs=pl.BlockSpec((tm, PW), lambda i: (i, 0)),
        out_shape=jax.ShapeDtypeStruct((SEQ, PW), f32),
        compiler_params=_params("arbitrary"),
    )(x, g, w)


def in_bwd(dq, dz, dxbc, dk, dv, ddt, x, g, w, dres, tm=256):
    def body(dq_ref, dz_ref, dx_ref, dk_ref, dv_ref, ddt_ref, x_ref, g_ref, w_ref, dres_ref, o_ref, dw_ref, dg_ref):
        i = pl.program_id(0)

        @pl.when(i == 0)
        def _():
            dw_ref[...] = jnp.zeros_like(dw_ref)
            dg_ref[...] = jnp.zeros_like(dg_ref)

        dproj = jnp.concatenate([dq_ref[...], dz_ref[...], dx_ref[...], dk_ref[...], dv_ref[...], ddt_ref[...]],
                                axis=-1).astype(bf16)
        xv = x_ref[...]
        r = _rms(xv)
        xhat = xv * r
        gv = g_ref[...]
        h = xhat * gv
        dw_ref[...] += _bdot_tn(h, dproj)
        dh = _bdot_nt(dproj, w_ref[...])
        dg_ref[...] += jnp.sum(dh * xhat, axis=0, keepdims=True)
        o_ref[...] = dres_ref[...] + _rms_bwd(dh, xhat, r, gv)

    tok = lambda w_: pl.BlockSpec((tm, w_), lambda i: (i, 0))
    return pl.pallas_call(
        body, name="in_bwd", grid=(SEQ // tm,),
        in_specs=[tok(D_ATTN), tok(D_SSM), tok(D_CONV), tok(D_KV // 1), tok(D_KV // 1), tok(LANE), tok(D_MODEL),
                  _full((1, D_MODEL)), _full((D_MODEL, PW)), tok(D_MODEL)],
        out_specs=[tok(D_MODEL), _full((D_MODEL, PW)), _full((1, D_MODEL))],
        out_shape=[jax.ShapeDtypeStruct((SEQ, D_MODEL), f32), jax.ShapeDtypeStruct((D_MODEL, PW), f32),
                   jax.ShapeDtypeStruct((1, D_MODEL), f32)],
        compiler_params=_params("arbitrary"),
    )(dq, dz, dxbc, dk, dv, ddt, x, g, w, dres)


def _attn_probs(qn, kn, bias_h, sink, first, col):
    s = _bdot_nt(qn, kn) * (HEAD_DIM ** -0.5) + bias_h
    s = jnp.where(jnp.logical_and(first, col < BLOCK), NEG, s)
    m = jnp.maximum(jnp.max(s, axis=-1, keepdims=True), sink)
    p = jnp.exp(s - m)
    psink = jnp.exp(sink - m)
    inv = 1.0 / (jnp.sum(p, axis=-1, keepdims=True) + psink)
    return p * inv, psink * inv


def attn_fwd(proj, q_gain, k_gain, sinks, bias):
    kcol, vcol = OFF_K // D_KV, OFF_V // D_KV

    def body(q_ref, kc_ref, kp_ref, vc_ref, vp_ref, qg_ref, kg_ref, sink_ref, bias_ref, o_ref):
        n = pl.program_id(0)
        first = n == 0
        col = lax.broadcasted_iota(jnp.int32, (BLOCK, 2 * BLOCK), 1)
        k2 = jnp.concatenate([kp_ref[...], kc_ref[...]], axis=0)
        v2 = jnp.concatenate([vp_ref[...], vc_ref[...]], axis=0)
        qg, kg = qg_ref[...], kg_ref[...]
        for hk in range(N_KV_HEADS):
            kk = k2[:, hk * HEAD_DIM:(hk + 1) * HEAD_DIM]
            kn = (kk * _rms(kk) * kg).astype(bf16)
            vb = v2[:, hk * HEAD_DIM:(hk + 1) * HEAD_DIM].astype(bf16)
            for gq in range(Q_PER_KV):
                h = hk * Q_PER_KV + gq
                qq = q_ref[:, h * HEAD_DIM:(h + 1) * HEAD_DIM]
                qn = qq * _rms(qq) * qg
                p, _ = _attn_probs(qn, kn, bias_ref[h], sink_ref[h], first, col)
                o_ref[:, h * HEAD_DIM:(h + 1) * HEAD_DIM] = _bdot(p, vb)

    prev = lambda n: jnp.maximum(n - 1, 0)
    return pl.pallas_call(
        body, name="attn_fwd", grid=(N_BLOCKS,),
        in_specs=[pl.BlockSpec((BLOCK, D_ATTN), lambda n: (n, 0)),
                  pl.BlockSpec((BLOCK, D_KV), lambda n: (n, kcol)), pl.BlockSpec((BLOCK, D_KV), lambda n: (prev(n), kcol)),
                  pl.BlockSpec((BLOCK, D_KV), lambda n: (n, vcol)), pl.BlockSpec((BLOCK, D_KV), lambda n: (prev(n), vcol)),
                  _full((1, HEAD_DIM)), _full((1, HEAD_DIM)), pl.BlockSpec(memory_space=pltpu.SMEM),
                  _full((N_Q_HEADS, BLOCK, 2 * BLOCK))],
        out_specs=pl.BlockSpec((BLOCK, D_ATTN), lambda n: (n, 0)),
        out_shape=jax.ShapeDtypeStruct((SEQ, D_ATTN), f32),
        compiler_params=_params("arbitrary"),
    )(proj, proj, proj, proj, proj, q_gain, k_gain, sinks, bias)


def attn_bwd(proj, d_out, q_gain, k_gain, sinks, bias):
    kcol, vcol = OFF_K // D_KV, OFF_V // D_KV

    def body(q_ref, kc_ref, kp_ref, vc_ref, vp_ref, do_ref, qg_ref, kg_ref, sink_ref, bias_ref,
             dq_ref, dk_ref, dv_ref, dband_ref, dsink_ref, dqg_ref, dkg_ref, dkn_scr, dv_scr):
        i = pl.program_id(0)
        first = i == N_BLOCKS - 1

        @pl.when(i == 0)
        def _():
            for ref in (dband_ref, dsink_ref, dqg_ref, dkg_ref, dkn_scr, dv_scr):
                ref[...] = jnp.zeros_like(ref)

        col = lax.broadcasted_iota(jnp.int32, (BLOCK, 2 * BLOCK), 1)
        k2 = jnp.concatenate([kp_ref[...], kc_ref[...]], axis=0)
        v2 = jnp.concatenate([vp_ref[...], vc_ref[...]], axis=0)
        qg, kg = qg_ref[...], kg_ref[...]
        scale = HEAD_DIM ** -0.5
        for hk in range(N_KV_HEADS):
            sl = slice(hk * HEAD_DIM, (hk + 1) * HEAD_DIM)
            kk = k2[:, sl]
            rk = _rms(kk)
            khat = kk * rk
            kn = (khat * kg).astype(bf16)
            vb = v2[:, sl].astype(bf16)
            dkn = jnp.zeros((2 * BLOCK, HEAD_DIM), f32)
            dvv = jnp.zeros((2 * BLOCK, HEAD_DIM), f32)
            for gq in range(Q_PER_KV):
                h = hk * Q_PER_KV + gq
                hs = slice(h * HEAD_DIM, (h + 1) * HEAD_DIM)
                qq = q_ref[:, hs]
                rq = _rms(qq)
                qhat = qq * rq
                qn = qhat * qg
                p, psink = _attn_probs(qn, kn, bias_ref[h], sink_ref[h], first, col)
                d_o = do_ref[:, hs]
                dp = _bdot_nt(d_o, vb)
                delta = jnp.sum(p * dp, axis=-1, keepdims=True)
                ds = p * (dp - delta)
                dband_ref[h] += ds
                dsink_ref[:, h:h + 1] += -jnp.sum(psink * delta, axis=0, keepdims=True)
                dqn = _bdot(ds, kn) * scale
                dkn = dkn + _bdot_tn(ds, qn) * scale
                dvv = dvv + _bdot_tn(p, d_o)
                dqg_ref[...] += jnp.sum(dqn * qhat, axis=0, keepdims=True)
                dq_ref[:, hs] = _rms_bwd(dqn, qhat, rq, qg)
            dkn_cur = dkn[BLOCK:] + dkn_scr[:, sl]
            dkn_scr[:, sl] = dkn[:BLOCK]
            khat_c, rk_c = khat[BLOCK:], rk[BLOCK:]
            dkg_ref[...] += jnp.sum(dkn_cur * khat_c, axis=0, keepdims=True)
            dk_ref[:, sl] = _rms_bwd(dkn_cur, khat_c, rk_c, kg)
            dv_ref[:, sl] = dvv[BLOCK:] + dv_scr[:, sl]
            dv_scr[:, sl] = dvv[:BLOCK]

    blk = lambda i: N_BLOCKS - 1 - i
    prev = lambda i: jnp.maximum(N_BLOCKS - 2 - i, 0)
    return pl.pallas_call(
        body, name="attn_bwd", grid=(N_BLOCKS,),
        in_specs=[pl.BlockSpec((BLOCK, D_ATTN), lambda i: (blk(i), 0)),
                  pl.BlockSpec((BLOCK, D_KV), lambda i: (blk(i), kcol)), pl.BlockSpec((BLOCK, D_KV), lambda i: (prev(i), kcol)),
                  pl.BlockSpec((BLOCK, D_KV), lambda i: (blk(i), vcol)), pl.BlockSpec((BLOCK, D_KV), lambda i: (prev(i), vcol)),
                  pl.BlockSpec((BLOCK, D_ATTN), lambda i: (blk(i), 0)),
                  _full((1, HEAD_DIM)), _full((1, HEAD_DIM)), pl.BlockSpec(memory_space=pltpu.SMEM),
                  _full((N_Q_HEADS, BLOCK, 2 * BLOCK))],
        out_specs=[pl.BlockSpec((BLOCK, D_ATTN), lambda i: (blk(i), 0)), pl.BlockSpec((BLOCK, D_KV), lambda i: (blk(i), 0)),
                   pl.BlockSpec((BLOCK, D_KV), lambda i: (blk(i), 0)), _full((N_Q_HEADS, BLOCK, 2 * BLOCK)),
                   _full((1, LANE)), _full((1, HEAD_DIM)), _full((1, HEAD_DIM))],
        out_shape=[jax.ShapeDtypeStruct((SEQ, D_ATTN), f32), jax.ShapeDtypeStruct((SEQ, D_KV), f32),
                   jax.ShapeDtypeStruct((SEQ, D_KV), f32), jax.ShapeDtypeStruct((N_Q_HEADS, BLOCK, 2 * BLOCK), f32),
                   jax.ShapeDtypeStruct((1, LANE), f32), jax.ShapeDtypeStruct((1, HEAD_DIM), f32),
                   jax.ShapeDtypeStruct((1, HEAD_DIM), f32)],
        scratch_shapes=[pltpu.VMEM((BLOCK, D_KV), f32), pltpu.VMEM((BLOCK, D_KV), f32)],
        compiler_params=_params("arbitrary"),
    )(proj, proj, proj, proj, proj, d_out, q_gain, k_gain, sinks, bias)


def _shift_down(u, s, row):
    if s == 0:
        return u
    return jnp.where(row >= s, pltpu.roll(u, s, 0), 0.0)


def _shift_up(u, s, row):
    if s == 0:
        return u
    return jnp.where(row < SEQ - s, pltpu.roll(u, SEQ - s, 0), 0.0)


def conv_fwd(proj, conv_w, conv_b):
    xcol = OFF_X // LANE

    def body(u_ref, w_ref, b_ref, o_ref):
        u = u_ref[...]
        row = lax.broadcasted_iota(jnp.int32, u.shape, 0)
        pre = b_ref[...] + jnp.zeros_like(u)
        for k in range(CONV_WIDTH):
            pre = pre + w_ref[k:k + 1, :] * _shift_down(u, CONV_WIDTH - 1 - k, row)
        o_ref[...] = pre * _sigmoid(pre)

    return pl.pallas_call(
        body, name="conv_fwd", grid=(D_CONV // LANE,),
        in_specs=[pl.BlockSpec((SEQ, LANE), lambda j: (0, xcol + j)), pl.BlockSpec((CONV_WIDTH, LANE), lambda j: (0, j)),
                  pl.BlockSpec((1, LANE), lambda j: (0, j))],
        out_specs=pl.BlockSpec((SEQ, LANE), lambda j: (0, j)),
        out_shape=jax.ShapeDtypeStruct((SEQ, D_CONV), f32),
        compiler_params=_params("arbitrary"),
    )(proj, conv_w, conv_b)


def conv_bwd(proj, d_act, conv_w, conv_b):
    xcol = OFF_X // LANE

    def body(u_ref, da_ref, w_ref, b_ref, du_ref, dw_ref, db_ref):
        u = u_ref[...]
        row = lax.broadcasted_iota(jnp.int32, u.shape, 0)
        shifted = [_shift_down(u, CONV_WIDTH - 1 - k, row) for k in range(CONV_WIDTH)]
        pre = b_ref[...] + jnp.zeros_like(u)
        for k in range(CONV_WIDTH):
            pre = pre + w_ref[k:k + 1, :] * shifted[k]
        sg = _sigmoid(pre)
        dpre = da_ref[...] * (sg * (1.0 + pre * (1.0 - sg)))
        db_ref[...] = jnp.sum(dpre, axis=0, keepdims=True)
        du = jnp.zeros_like(u)
        for k in range(CONV_WIDTH):
            dw_ref[k:k + 1, :] = jnp.sum(dpre * shifted[k], axis=0, keepdims=True)
            du = du + w_ref[k:k + 1, :] * _shift_up(dpre, CONV_WIDTH - 1 - k, row)
        du_ref[...] = du

    return pl.pallas_call(
        body, name="conv_bwd", grid=(D_CONV // LANE,),
        in_specs=[pl.BlockSpec((SEQ, LANE), lambda j: (0, xcol + j)), pl.BlockSpec((SEQ, LANE), lambda j: (0, j)),
                  pl.BlockSpec((CONV_WIDTH, LANE), lambda j: (0, j)), pl.BlockSpec((1, LANE), lambda j: (0, j))],
        out_specs=[pl.BlockSpec((SEQ, LANE), lambda j: (0, j)), pl.BlockSpec((CONV_WIDTH, LANE), lambda j: (0, j)),
                   pl.BlockSpec((1, LANE), lambda j: (0, j))],
        out_shape=[jax.ShapeDtypeStruct((SEQ, D_CONV), f32), jax.ShapeDtypeStruct((CONV_WIDTH, D_CONV), f32),
                   jax.ShapeDtypeStruct((1, D_CONV), f32)],
        compiler_params=_params("arbitrary"),
    )(proj, d_act, conv_w, conv_b)


def _ssd_chunk_common(dt_raw, dtb, alog):
    row = lax.broadcasted_iota(jnp.int32, (CHUNK, CHUNK), 0)
    col = lax.broadcasted_iota(jnp.int32, (CHUNK, CHUNK), 1)
    tri = (row >= col).astype(f32)
    strict = (row > col).astype(f32)
    dtp = _softplus(dt_raw + dtb)
    a_row = -jnp.exp(alog)
    d_a = dtp * a_row
    cs = _hdot(tri, d_a)
    cs_last = cs[CHUNK - 1:CHUNK, :]
    return row, col, tri, strict, dtp, a_row, d_a, cs, cs_last


def _seg_decay(tri, strict, d_a_h, row, col):
    seg = _hdot(tri, d_a_h * strict)
    return jnp.where(row >= col, jnp.exp(seg), 0.0)


def ssd_fwd(act, proj, dt_bias, a_log, d_skip, norm_g):
    zcol, dtcol = OFF_Z // D_SSM, OFF_DT // LANE
    gw = D_SSM // SSM_GROUPS

    def body(act_ref, z_ref, dt_ref, dtb_ref, alog_ref, dsk_ref, ng_ref, out_ref, ypre_ref, st_ref, state, ybuf):
        c = pl.program_id(0)

        @pl.when(c == 0)
        def _():
            state[...] = jnp.zeros_like(state)

        row, col, tri, strict, dtp, a_row, d_a, cs, cs_last = _ssd_chunk_common(dt_ref[...], dtb_ref[...], alog_ref[...])
        e_cs = jnp.exp(cs)
        dte = jnp.exp(cs_last - cs)
        ecl = jnp.exp(cs_last)
        dsk = dsk_ref[...]
        for g in range(SSM_GROUPS):
            bg = act_ref[:, D_SSM + g * SSM_STATE:D_SSM + (g + 1) * SSM_STATE]
            cg = act_ref[:, D_SSM + D_BC + g * SSM_STATE:D_SSM + D_BC + (g + 1) * SSM_STATE]
            cb = _bdot_nt(cg, bg)
            for r in range(HEADS_PER_GROUP):
                hd = g * HEADS_PER_GROUP + r
                hs = slice(hd * SSM_HEAD_DIM, (hd + 1) * SSM_HEAD_DIM)
                hl = slice(hd, hd + 1)
                x_h = act_ref[:, hs]
                xdt = x_h * dtp[:, hl]
                lm = _seg_decay(tri, strict, d_a[:, hl], row, col)
                prev = state[hd]
                st_ref[0, hd] = prev
                y = _bdot(cb * lm, xdt) + e_cs[:, hl] * _bdot(cg, prev) + x_h * dsk[:, hl]
                ybuf[:, hs] = y
                state[hd] = prev * ecl[:, hl] + _bdot_tn(bg, xdt * dte[:, hl])
        y = ybuf[...]
        ypre_ref[...] = y
        z = z_ref[...]
        yz = y * (z * _sigmoid(z))
        ng = ng_ref[...]
        for g in range(SSM_GROUPS):
            gs = slice(g * gw, (g + 1) * gw)
            part = yz[:, gs]
            out_ref[:, gs] = part * _rms(part) * ng[:, gs]

    return pl.pallas_call(
        body, name="ssd_fwd", grid=(N_CHUNKS,),
        in_specs=[pl.BlockSpec((CHUNK, D_CONV), lambda c: (c, 0)), pl.BlockSpec((CHUNK, D_SSM), lambda c: (c, zcol)),
                  pl.BlockSpec((CHUNK, LANE), lambda c: (c, dtcol)), _full((1, LANE)), _full((1, LANE)), _full((1, LANE)),
                  _full((1, D_SSM))],
        out_specs=[pl.BlockSpec((CHUNK, D_SSM), lambda c: (c, 0)), pl.BlockSpec((CHUNK, D_SSM), lambda c: (c, 0)),
                   pl.BlockSpec((1, SSM_HEADS, SSM_STATE, SSM_HEAD_DIM), lambda c: (c, 0, 0, 0))],
        out_shape=[jax.ShapeDtypeStruct((SEQ, D_SSM), f32), jax.ShapeDtypeStruct((SEQ, D_SSM), f32),
                   jax.ShapeDtypeStruct((N_CHUNKS, SSM_HEADS, SSM_STATE, SSM_HEAD_DIM), f32)],
        scratch_shapes=[pltpu.VMEM((SSM_HEADS, SSM_STATE, SSM_HEAD_DIM), f32), pltpu.VMEM((CHUNK, D_SSM), f32)],
        compiler_params=_params("arbitrary"),
    )(act, proj, proj, dt_bias, a_log, d_skip, norm_g)


def ssd_bwd(act, proj, ypre, states, d_out, dt_bias, a_log, d_skip, norm_g):
    zcol, dtcol = OFF_Z // D_SSM, OFF_DT // LANE
    gw = D_SSM // SSM_GROUPS

    def body(act_ref, z_ref, dt_ref, ypre_ref, st_ref, do_ref, dtb_ref, alog_ref, dsk_ref, ng_ref,
             dact_ref, ddt_ref, dz_ref, dng_ref, dpar_ref, dstate, dybuf):
        i = pl.program_id(0)

        @pl.when(i == 0)
        def _():
            for ref in (dng_ref, dpar_ref, dstate):
                ref[...] = jnp.zeros_like(ref)

        y = ypre_ref[...]
        z = z_ref[...]
        sgz = _sigmoid(z)
        sz = z * sgz
        yz = y * sz
        ng = ng_ref[...]
        d_o = do_ref[...]
        for g in range(SSM_GROUPS):
            gs = slice(g * gw, (g + 1) * gw)
            part = yz[:, gs]
            r = _rms(part)
            yhat = part * r
            dng_ref[:, gs] += jnp.sum(d_o[:, gs] * yhat, axis=0, keepdims=True)
            dyz = _rms_bwd(d_o[:, gs], yhat, r, ng[:, gs])
            dybuf[:, gs] = dyz * sz[:, gs]
            dz_ref[:, gs] = dyz * y[:, gs] * (sgz[:, gs] * (1.0 + z[:, gs] * (1.0 - sgz[:, gs])))

        row, col, tri, strict, dtp, a_row, d_a, cs, cs_last = _ssd_chunk_common(dt_ref[...], dtb_ref[...], alog_ref[...])
        upper = (row <= col).astype(f32)
        lane = lax.broadcasted_iota(jnp.int32, (CHUNK, LANE), 1)
        lane1 = lax.broadcasted_iota(jnp.int32, (1, LANE), 1)
        e_cs = jnp.exp(cs)
        dte = jnp.exp(cs_last - cs)
        ecl = jnp.exp(cs_last)
        dsk = dsk_ref[...]
        ddt_mat = jnp.zeros((CHUNK, LANE), f32)
        dcs_mat = jnp.zeros((CHUNK, LANE), f32)
        dda_mat = jnp.zeros((CHUNK, LANE), f32)
        dcsl_row = jnp.zeros((1, LANE), f32)
        dd_row = jnp.zeros((1, LANE), f32)
        for g in range(SSM_GROUPS):
            bsl = slice(D_SSM + g * SSM_STATE, D_SSM + (g + 1) * SSM_STATE)
            csl = slice(D_SSM + D_BC + g * SSM_STATE, D_SSM + D_BC + (g + 1) * SSM_STATE)
            bg = act_ref[:, bsl]
            cg = act_ref[:, csl]
            cb = _bdot_nt(cg, bg)
            dcb = jnp.zeros((CHUNK, CHUNK), f32)
            dbg = jnp.zeros((CHUNK, SSM_STATE), f32)
            dcg = jnp.zeros((CHUNK, SSM_STATE), f32)
            for rr in range(HEADS_PER_GROUP):
                hd = g * HEADS_PER_GROUP + rr
                hs = slice(hd * SSM_HEAD_DIM, (hd + 1) * SSM_HEAD_DIM)
                hl = slice(hd, hd + 1)
                x_h = act_ref[:, hs]
                dt_h = dtp[:, hl]
                e_h = e_cs[:, hl]
                dte_h = dte[:, hl]
                ecl_h = ecl[:, hl]
                xdt = x_h * dt_h
                lm = _seg_decay(tri, strict, d_a[:, hl], row, col)
                m = cb * lm
                prev = st_ref[0, hd]
                dy = dybuf[:, hs]
                dh = dstate[hd]
                dd_row = dd_row + jnp.where(lane1 == hd, jnp.sum(jnp.sum(dy * x_h, axis=1, keepdims=True), axis=0, keepdims=True), 0.0)
                dx = dy * dsk[:, hl]
                gmat = _bdot(cg, prev)
                dg = dy * e_h
                dcg = dcg + _bdot_nt(dg, prev)
                dprev = _bdot_tn(cg, dg)
                dcs_h = jnp.sum(dy * gmat, axis=1, keepdims=True) * e_h
                dm = _bdot_nt(dy, xdt)
                dxdt = _bdot_tn(m, dy)
                dcb = dcb + dm * lm
                dseg = dm * m
                dda_h = jnp.sum(_hdot(upper, dseg) * strict, axis=1, keepdims=True)
                wmat = xdt * dte_h
                dbg = dbg + _bdot_nt(wmat, dh)
                dw = _bdot(bg, dh)
                dxdt = dxdt + dw * dte_h
                ddte = jnp.sum(dw * xdt, axis=1, keepdims=True) * dte_h
                dcs_h = dcs_h - ddte
                dcsl = jnp.sum(ddte, axis=0, keepdims=True)
                dcsl = dcsl + jnp.sum(jnp.sum(dh * prev, axis=1, keepdims=True), axis=0, keepdims=True) * ecl_h
                dstate[hd] = dprev + dh * ecl_h
                dact_ref[:, hs] = dx + dxdt * dt_h
                ddt_h = jnp.sum(dxdt * x_h, axis=1, keepdims=True)
                ddt_mat = jnp.where(lane == hd, ddt_h, ddt_mat)
                dcs_mat = jnp.where(lane == hd, dcs_h, dcs_mat)
                dda_mat = jnp.where(lane == hd, dda_h, dda_mat)
                dcsl_row = jnp.where(lane1 == hd, dcsl, dcsl_row)
            dact_ref[:, bsl] = dbg + _bdot_tn(dcb, cg)
            dact_ref[:, csl] = dcg + _bdot(dcb, bg)
        rowl = lax.broadcasted_iota(jnp.int32, (CHUNK, LANE), 0)
        dcs_mat = dcs_mat + jnp.where(rowl == CHUNK - 1, dcsl_row, 0.0)
        dda = dda_mat + _hdot(upper, dcs_mat)
        ddt_mat = ddt_mat + dda * a_row
        da_row = jnp.sum(dda * dtp, axis=0, keepdims=True)
        ddt_raw = ddt_mat * _sigmoid(dt_ref[...] + dtb_ref[...])
        ddt_ref[...] = ddt_raw
        dpar_ref[0:1, :] += jnp.sum(ddt_raw, axis=0, keepdims=True)
        dpar_ref[1:2, :] += da_row * a_row
        dpar_ref[2:3, :] += dd_row

    blk = lambda i: N_CHUNKS - 1 - i
    return pl.pallas_call(
        body, name="ssd_bwd", grid=(N_CHUNKS,),
        in_specs=[pl.BlockSpec((CHUNK, D_CONV), lambda i: (blk(i), 0)), pl.BlockSpec((CHUNK, D_SSM), lambda i: (blk(i), zcol)),
                  pl.BlockSpec((CHUNK, LANE), lambda i: (blk(i), dtcol)), pl.BlockSpec((CHUNK, D_SSM), lambda i: (blk(i), 0)),
                  pl.BlockSpec((1, SSM_HEADS, SSM_STATE, SSM_HEAD_DIM), lambda i: (blk(i), 0, 0, 0)),
                  pl.BlockSpec((CHUNK, D_SSM), lambda i: (blk(i), 0)),
                  _full((1, LANE)), _full((1, LANE)), _full((1, LANE)), _full((1, D_SSM))],
        out_specs=[pl.BlockSpec((CHUNK, D_CONV), lambda i: (blk(i), 0)), pl.BlockSpec((CHUNK, LANE), lambda i: (blk(i), 0)),
                   pl.BlockSpec((CHUNK, D_SSM), lambda i: (blk(i), 0)), _full((1, D_SSM)), _full((8, LANE))],
        out_shape=[jax.ShapeDtypeStruct((SEQ, D_CONV), f32), jax.ShapeDtypeStruct((SEQ, LANE), f32),
                   jax.ShapeDtypeStruct((SEQ, D_SSM), f32), jax.ShapeDtypeStruct((1, D_SSM), f32),
                   jax.ShapeDtypeStruct((8, LANE), f32)],
        scratch_shapes=[pltpu.VMEM((SSM_HEADS, SSM_STATE, SSM_HEAD_DIM), f32), pltpu.VMEM((CHUNK, D_SSM), f32)],
        compiler_params=_params("arbitrary"),
    )(act, proj, proj, ypre, states, d_out, dt_bias, a_log, d_skip, norm_g)


def out_fwd(x, attn, ssm, w_out, tm=512):
    def body(x_ref, a_ref, s_ref, w_ref, o_ref):
        o_ref[...] = x_ref[...] + _bdot(a_ref[...], w_ref[:D_ATTN, :]) + _bdot(s_ref[...], w_ref[D_ATTN:, :])

    tok = lambda w_: pl.BlockSpec((tm, w_), lambda i: (i, 0))
    return pl.pallas_call(
        body, name="out_fwd", grid=(SEQ // tm,),
        in_specs=[tok(D_MODEL), tok(D_ATTN), tok(D_SSM), _full((D_MODEL, D_MODEL))],
        out_specs=tok(D_MODEL), out_shape=jax.ShapeDtypeStruct((SEQ, D_MODEL), f32),
        compiler_params=_params("arbitrary"),
    )(x, attn, ssm, w_out)


def out_bwd(dx1, attn, ssm, w_out, tm=512):
    def body(d_ref, a_ref, s_ref, w_ref, da_ref, ds_ref, dw_ref):
        @pl.when(pl.program_id(0) == 0)
        def _():
            dw_ref[...] = jnp.zeros_like(dw_ref)

        d = d_ref[...].astype(bf16)
        dcat = _bdot_nt(d, w_ref[...])
        da_ref[...] = dcat[:, :D_ATTN]
        ds_ref[...] = dcat[:, D_ATTN:]
        dw_ref[:D_ATTN, :] += _bdot_tn(a_ref[...], d)
        dw_ref[D_ATTN:, :] += _bdot_tn(s_ref[...], d)

    tok = lambda w_: pl.BlockSpec((tm, w_), lambda i: (i, 0))
    return pl.pallas_call(
        body, name="out_bwd", grid=(SEQ // tm,),
        in_specs=[tok(D_MODEL), tok(D_ATTN), tok(D_SSM), _full((D_MODEL, D_MODEL))],
        out_specs=[tok(D_ATTN), tok(D_SSM), _full((D_MODEL, D_MODEL))],
        out_shape=[jax.ShapeDtypeStruct((SEQ, D_ATTN), f32), jax.ShapeDtypeStruct((SEQ, D_SSM), f32),
                   jax.ShapeDtypeStruct((D_MODEL, D_MODEL), f32)],
        compiler_params=_params("arbitrary"),
    )(dx1, attn, ssm, w_out)


def mlp_fwd(x1, g, w_up, w_down, tm=512):
    def body(x_ref, g_ref, wu_ref, wd_ref, o_ref, u_ref, h_scr):
        j = pl.program_id(1)

        @pl.when(j == 0)
        def _():
            xv = x_ref[...]
            h_scr[...] = (xv * _rms(xv) * g_ref[...]).astype(bf16)
            o_ref[...] = xv

        u = jnp.dot(h_scr[...], wu_ref[...], preferred_element_type=f32)
        u_ref[...] = u
        a = jnp.square(jnp.maximum(u, 0.0))
        o_ref[...] += _bdot(a, wd_ref[...])

    return pl.pallas_call(
        body, name="mlp_fwd", grid=(SEQ // tm, N_CHIPS),
        in_specs=[pl.BlockSpec((tm, D_MODEL), lambda i, j: (i, 0)), _full((1, D_MODEL)),
                  pl.BlockSpec((None, D_MODEL, FF_TILE), lambda i, j: (j, 0, 0)),
                  pl.BlockSpec((None, FF_TILE, D_MODEL), lambda i, j: (j, 0, 0))],
        out_specs=[pl.BlockSpec((tm, D_MODEL), lambda i, j: (i, 0)), pl.BlockSpec((tm, FF_TILE), lambda i, j: (i, j))],
        out_shape=[jax.ShapeDtypeStruct((SEQ, D_MODEL), f32), jax.ShapeDtypeStruct((SEQ, D_FF), f32)],
        scratch_shapes=[pltpu.VMEM((tm, D_MODEL), bf16)],
        compiler_params=_params("arbitrary", "arbitrary"),
    )(x1, g, w_up, w_down)


def mlp_bwd_data(dx2, u, x1, g, w_up, w_down, tm=512):
    def body(d_ref, u_ref, x_ref, g_ref, wu_ref, wd_ref, dx_ref, du_ref, dg_ref, dh_scr):
        i, j = pl.program_id(0), pl.program_id(1)

        @pl.when(jnp.logical_and(i == 0, j == 0))
        def _():
            dg_ref[...] = jnp.zeros_like(dg_ref)

        @pl.when(j == 0)
        def _():
            dh_scr[...] = jnp.zeros_like(dh_scr)

        da = _bdot_nt(d_ref[...], wd_ref[...])
        du = (da * (2.0 * jnp.maximum(u_ref[...], 0.0))).astype(bf16)
        du_ref[...] = du
        dh_scr[...] += _bdot_nt(du, wu_ref[...])

        @pl.when(j == N_CHIPS - 1)
        def _():
            xv = x_ref[...]
            r = _rms(xv)
            xhat = xv * r
            dh = dh_scr[...]
            dg_ref[...] += jnp.sum(dh * xhat, axis=0, keepdims=True)
            dx_ref[...] = d_ref[...] + _rms_bwd(dh, xhat, r, g_ref[...])

    return pl.pallas_call(
        body, name="mlp_bwd_data", grid=(SEQ // tm, N_CHIPS),
        in_specs=[pl.BlockSpec((tm, D_MODEL), lambda i, j: (i, 0)), pl.BlockSpec((tm, FF_TILE), lambda i, j: (i, j)),
                  pl.BlockSpec((tm, D_MODEL), lambda i, j: (i, 0)), _full((1, D_MODEL)),
                  pl.BlockSpec((None, D_MODEL, FF_TILE), lambda i, j: (j, 0, 0)),
                  pl.BlockSpec((None, FF_TILE, D_MODEL), lambda i, j: (j, 0, 0))],
        out_specs=[pl.BlockSpec((tm, D_MODEL), lambda i, j: (i, 0)), pl.BlockSpec((tm, FF_TILE), lambda i, j: (i, j)),
                   _full((1, D_MODEL))],
        out_shape=[jax.ShapeDtypeStruct((SEQ, D_MODEL), f32), jax.ShapeDtypeStruct((SEQ, D_FF), bf16),
                   jax.ShapeDtypeStruct((1, D_MODEL), f32)],
        scratch_shapes=[pltpu.VMEM((tm, D_MODEL), f32)],
        compiler_params=_params("arbitrary", "arbitrary"),
    )(dx2, u, x1, g, w_up, w_down)


def mlp_bwd_weights(dx2, u, du, x1, g, tm=512):
    def body(d_ref, u_ref, du_ref, x_ref, g_ref, dwu_ref, dwd_ref):
        @pl.when(pl.program_id(1) == 0)
        def _():
            dwu_ref[...] = jnp.zeros_like(dwu_ref)
            dwd_ref[...] = jnp.zeros_like(dwd_ref)

        xv = x_ref[...]
        h = xv * _rms(xv) * g_ref[...]
        dwu_ref[...] += _bdot_tn(h, du_ref[...])
        a = jnp.square(jnp.maximum(u_ref[...], 0.0))
        dwd_ref[...] += _bdot_tn(a, d_ref[...])

    return pl.pallas_call(
        body, name="mlp_bwd_weights", grid=(N_CHIPS, SEQ // tm),
        in_specs=[pl.BlockSpec((tm, D_MODEL), lambda j, i: (i, 0)), pl.BlockSpec((tm, FF_TILE), lambda j, i: (i, j)),
                  pl.BlockSpec((tm, FF_TILE), lambda j, i: (i, j)), pl.BlockSpec((tm, D_MODEL), lambda j, i: (i, 0)),
                  _full((1, D_MODEL))],
        out_specs=[pl.BlockSpec((None, D_MODEL, FF_TILE), lambda j, i: (j, 0, 0)),
                   pl.BlockSpec((None, FF_TILE, D_MODEL), lambda j, i: (j, 0, 0))],
        out_shape=[jax.ShapeDtypeStruct((N_CHIPS, D_MODEL, FF_TILE), f32), jax.ShapeDtypeStruct((N_CHIPS, FF_TILE, D_MODEL), f32)],
        compiler_params=_params("arbitrary", "arbitrary"),
    )(dx2, u, du, x1, g)


def loss_head(y, target, tm=512):
    def body(y_ref, t_ref, dy_ref, l_ref):
        @pl.when(pl.program_id(0) == 0)
        def _():
            l_ref[...] = jnp.zeros_like(l_ref)

        d = y_ref[...] - t_ref[...]
        dy_ref[...] = d * (1.0 / D_MODEL)
        part = jnp.sum(jnp.mean(d * d, axis=-1, keepdims=True), axis=0, keepdims=True)
        l_ref[...] += 0.5 * part

    tok = pl.BlockSpec((tm, D_MODEL), lambda i: (i, 0))
    return pl.pallas_call(
        body, name="loss_head", grid=(SEQ // tm,), in_specs=[tok, tok], out_specs=[tok, _full((1, 1))],
        out_shape=[jax.ShapeDtypeStruct((SEQ, D_MODEL), f32), jax.ShapeDtypeStruct((1, 1), f32)],
        compiler_params=_params("arbitrary"),
    )(y, target)


def _pad_lane(v):
    return jnp.pad(v, (0, LANE - v.shape[0]))[None, :]


def local_step(x, target, w, prov):
    bucket = jnp.asarray(_bucket_table())
    bias = bias_build(w["rel_bias"], bucket)
    saved = []
    for l in range(DEPTH):
        g_mix = w["mix_norm_g"][l][None, :] + prov.stage(("begin", l), x)
        w_in = prov.w_in(l, x)
        proj = in_fwd(x, g_mix, w_in)
        qg, kg = w["q_gain"][l][None, :], w["k_gain"][l][None, :]
        attn = attn_fwd(proj, qg, kg, w["sinks"][l], bias)
        conv_b = w["conv_b"][l][None, :]
        act = conv_fwd(proj, w["conv_w"][l], conv_b)
        dtb = _pad_lane(w["dt_bias"][l]) + prov.stage(("mid", l), act)
        alog, dsk = _pad_lane(w["a_log"][l]), _pad_lane(w["d_skip"][l])
        ng = w["ssm_norm_g"][l][None, :]
        ssm, ypre, states = ssd_fwd(act, proj, dtb, alog, dsk, ng)
        w_out = prov.w_out(l, ssm)
        x1 = out_fwd(x, attn, ssm, w_out)
        g_mlp = w["mlp_norm_g"][l][None, :] + prov.stage(("pre_mlp", l), x1)
        w_up, w_down = prov.mlp(l, x1)
        x2, u = mlp_fwd(x1, g_mlp, w_up, w_down)
        saved.append(dict(x=x, proj=proj, attn=attn, act=act, ssm=ssm, ypre=ypre, states=states, x1=x1, u=u,
                          g_mix=g_mix, qg=qg, kg=kg, conv_b=conv_b, dtb=dtb, alog=alog, dsk=dsk, ng=ng, g_mlp=g_mlp,
                          w_in=w_in, w_out=w_out, w_up=w_up, w_down=w_down))
        x = x2
    dx, loss = loss_head(x, target)
    grads = [None] * DEPTH
    dbands = [None] * DEPTH
    tok = 0.0
    for l in reversed(range(DEPTH)):
        s = saved[l]
        g_mlp = s["g_mlp"] + tok
        dx1, du, dg_mlp = mlp_bwd_data(dx, s["u"], s["x1"], g_mlp, s["w_up"], s["w_down"])
        dw_up, dw_down = mlp_bwd_weights(dx, s["u"], du, s["x1"], g_mlp)
        tok = prov.grads(("mlp", l), dict(w_up=dw_up, w_down=dw_down), dx1)
        dattn, dssm, dw_out = out_bwd(dx1, s["attn"], s["ssm"], s["w_out"])
        dact, ddt, dz, dng, dpar = ssd_bwd(s["act"], s["proj"], s["ypre"], s["states"], dssm, s["dtb"] + tok, s["alog"],
                                           s["dsk"], s["ng"])
        conv_b = s["conv_b"] + prov.stage(("bwd_mid", l), dact)
        dxbc, dconv_w, dconv_b = conv_bwd(s["proj"], dact, w["conv_w"][l], conv_b)
        dq, dk, dv, dband, dsink, dqg, dkg = attn_bwd(s["proj"], dattn, s["qg"], s["kg"], w["sinks"][l], bias)
        dx, dw_in, dg_mix = in_bwd(dq, dz, dxbc, dk, dv, ddt, s["x"], s["g_mix"], s["w_in"], dx1)
        tok = prov.grads(("mix", l), dict(w_in=dw_in, w_out=dw_out), dx)
        dbands[l] = dband
        grads[l] = dict(mix_norm_g=dg_mix[0], q_gain=dqg[0], k_gain=dkg[0], sinks=dsink[0, :N_Q_HEADS],
                        conv_w=dconv_w, conv_b=dconv_b[0], dt_bias=dpar[0, :SSM_HEADS], a_log=dpar[1, :SSM_HEADS],
                        d_skip=dpar[2, :SSM_HEADS], ssm_norm_g=dng[0], mlp_norm_g=dg_mlp[0])
    out = {k: jnp.stack([grads[l][k] for l in range(DEPTH)]) for k in grads[0]}
    out["rel_bias"] = bias_bwd(dbands[0], dbands[1], bucket)[:, :N_Q_HEADS]
    return loss, dx, out


MESH = pl.DeviceIdType.MESH
HBM = pl.BlockSpec(memory_space=pltpu.HBM)
N_PEER_CHIPS = N_CHIPS - 1
N_DEVICES = 8


def _coords():
    return lax.axis_index("x"), lax.axis_index("y"), lax.axis_index("c")


def _peer_chips(x, y):
    return [(1 - x, y), (x, 1 - y), (1 - x, 1 - y)]


def _remote(src, dst, send_sem, recv_sem, device):
    return pltpu.make_async_remote_copy(src_ref=src, dst_ref=dst, send_sem=send_sem, recv_sem=recv_sem,
                                        device_id=device, device_id_type=MESH)


SEM = pl.BlockSpec(memory_space=pltpu.SEMAPHORE)
ANY = pl.BlockSpec(memory_space=pl.ANY)
DATAFLOW = pltpu.SideEffectType.DATAFLOW_SIDE_EFFECTING


def _gather_copies(kind, src_refs, land_refs, ssem, rsem):
    x, y, c = _coords()
    k_me = 2 * x + y
    cps = []
    for p, land in enumerate(land_refs):
        hr = land.shape[1] // 2
        rows = pl.ds(c * hr, hr)
        for j, chip in enumerate(_peer_chips(x, y)):
            i = 3 * p + j
            if kind == "ici":
                cps.append(_remote(src_refs[p].at[rows, :], land.at[k_me, rows, :], ssem.at[i], rsem.at[i], (*chip, c)))
            else:
                got = land.at[2 * chip[0] + chip[1], rows, :]
                cps.append(_remote(got, got, ssem.at[i], rsem.at[i], (x, y, 1 - c)))
    return cps


def gather_now(srcs, conv):
    n = len(srcs)

    def body(*refs):
        src_refs, conv_ref = refs[:n], refs[n]
        lands, gconv = refs[n + 1:2 * n + 1], refs[2 * n + 1]
        ssem, rsem, fsem, frsem, csem, crsem = refs[2 * n + 2:]
        x, y, c = _coords()
        k_me = 2 * x + y
        chips = _peer_chips(x, y)
        ici = _gather_copies("ici", src_refs, lands, ssem, rsem)
        relay = _gather_copies("relay", src_refs, lands, fsem, frsem)
        conv_cps = [_remote(conv_ref, gconv.at[k_me], csem.at[j], crsem.at[j], (*chip, c)) for j, chip in enumerate(chips)]
        for cp in ici + conv_cps:
            cp.start()
        for cp, fw in zip(ici, relay):
            cp.wait_recv()
            fw.start()
        for cp in conv_cps + relay:
            cp.wait_recv()
        for cp in ici + relay + conv_cps:
            cp.wait_send()

    out_shape = [jax.ShapeDtypeStruct((N_CHIPS,) + s.shape, s.dtype) for s in srcs]
    out_shape.append(jax.ShapeDtypeStruct((N_CHIPS,) + conv.shape, conv.dtype))
    sems = lambda k: pltpu.SemaphoreType.DMA((k,))
    return pl.pallas_call(
        body, name="gather_now", out_shape=out_shape, in_specs=[HBM] * (n + 1), out_specs=[HBM] * (n + 1),
        scratch_shapes=[sems(3 * n), sems(3 * n), sems(3 * n), sems(3 * n), sems(N_PEER_CHIPS), sems(N_PEER_CHIPS)],
    )(*srcs, conv)


def split_start(name, kind, srcs, lands):
    ns, nl = len(srcs), len(lands)

    def body(*refs):
        ssem, rsem, token = refs[ns + nl], refs[ns + nl + 1], refs[-1]
        for cp in _gather_copies(kind, refs[:ns], refs[ns:ns + nl], ssem, rsem):
            cp.start()
        token[...] = jnp.zeros_like(token)

    ops = [pltpu.with_memory_space_constraint(a, pltpu.HBM) for a in (*srcs, *lands)]
    outs = pl.pallas_call(
        body, name=name,
        out_shape=(pltpu.SemaphoreType.DMA((3 * nl,)), pltpu.SemaphoreType.DMA((3 * nl,)),
                   *[pltpu.HBM(a.shape, a.dtype) for a in ops], jax.ShapeDtypeStruct((8, LANE), f32)),
        in_specs=[HBM] * (ns + nl), out_specs=(SEM, SEM, *[HBM] * (ns + nl), pl.BlockSpec(memory_space=pltpu.VMEM)),
        input_output_aliases={i: 2 + i for i in range(ns + nl)},
        compiler_params=pltpu.CompilerParams(has_side_effects=DATAFLOW),
    )(*ops)
    return dict(name=name, kind=kind, ssem=outs[0], rsem=outs[1], srcs=outs[2:2 + ns], lands=outs[2 + ns:2 + ns + nl],
                token=outs[-1][0, 0])


def split_wait(handle, after):
    ns, nl = len(handle["srcs"]), len(handle["lands"])

    def body(*refs):
        ssem, rsem = refs[ns + nl], refs[ns + nl + 1]
        for cp in _gather_copies(handle["kind"], refs[:ns], refs[ns:ns + nl], ssem, rsem):
            cp.wait_send()
            cp.wait_recv()

    ops = (*handle["srcs"], *handle["lands"])
    outs = pl.pallas_call(
        body, name=handle["name"].replace("start", "wait"),
        out_shape=tuple(pltpu.HBM(a.shape, a.dtype) for a in ops),
        in_specs=[HBM] * (ns + nl) + [SEM, SEM, ANY], out_specs=tuple([HBM] * (ns + nl)),
        input_output_aliases={i: i for i in range(ns + nl)},
        compiler_params=pltpu.CompilerParams(has_side_effects=DATAFLOW),
    )(*ops, handle["ssem"], handle["rsem"], after)
    return list(outs[ns:])


def small_all_reduce(vec):
    def body(v_ref, o_ref, gat, ssem, rsem):
        x, y, c = _coords()
        me = 4 * x + 2 * y + c
        gat[me] = v_ref[...]
        sends = []
        for t in range(1, N_DEVICES):
            peer = (x ^ (t >> 2), y ^ ((t >> 1) & 1), c ^ (t & 1))
            cp = _remote(v_ref, gat.at[me], ssem.at[t - 1], rsem.at[t - 1], peer)
            cp.start()
            sends.append(cp)
        for t in range(1, N_DEVICES):
            peer = (x ^ (t >> 2), y ^ ((t >> 1) & 1), c ^ (t & 1))
            slot = gat.at[4 * peer[0] + 2 * peer[1] + peer[2]]
            _remote(slot, slot, ssem.at[t - 1], rsem.at[t - 1], peer).wait_recv()
        for cp in sends:
            cp.wait_send()
        acc = gat[0]
        for d in range(1, N_DEVICES):
            acc = acc + gat[d]
        o_ref[...] = acc

    return pl.pallas_call(
        body, name="small_all_reduce", out_shape=jax.ShapeDtypeStruct(vec.shape, vec.dtype),
        in_specs=[pl.BlockSpec(memory_space=pltpu.VMEM)], out_specs=pl.BlockSpec(memory_space=pltpu.VMEM),
        scratch_shapes=[pltpu.VMEM((N_DEVICES,) + vec.shape, vec.dtype), pltpu.SemaphoreType.DMA((N_DEVICES - 1,)),
                        pltpu.SemaphoreType.DMA((N_DEVICES - 1,))],
    )(vec)


def sibling_exchange(grads):
    n = len(grads)

    def body(*refs):
        srcs, dsts, ssem, rsem = refs[:n], refs[n:2 * n], refs[2 * n], refs[2 * n + 1]
        x, y, c = _coords()
        cps = [_remote(srcs[i].at[:, 1 - c], dsts[i], ssem.at[i], rsem.at[i], (x, y, 1 - c)) for i in range(n)]
        for cp in cps:
            cp.start()
        for cp in cps:
            cp.wait()

    return pl.pallas_call(
        body, name="sibling_exchange",
        out_shape=[jax.ShapeDtypeStruct((N_CHIPS,) + g.shape[2:], g.dtype) for g in grads],
        in_specs=[HBM] * n, out_specs=[HBM] * n,
        scratch_shapes=[pltpu.SemaphoreType.DMA((n,)), pltpu.SemaphoreType.DMA((n,))],
    )(*grads)


def pair_sum(g0, g1, r0, r1, c_arr, tr):
    _, _, rb, cc = g0.shape
    nr = rb // tr

    def body(c_ref, g0_ref, g1_ref, r0_ref, r1_ref, p_ref, pb_ref):
        l = pl.program_id(0)

        @pl.when(l == 0)
        def _():
            s = g0_ref[...] + r0_ref[...]
            p_ref[...] = s
            pb_ref[...] = s.astype(bf16)

        @pl.when(l == 1)
        def _():
            s = g1_ref[...] + r1_ref[...]
            p_ref[...] = s
            pb_ref[...] = s.astype(bf16)

    lay0 = lambda l, a, last: jnp.where(l == 0, a, last)
    lay1 = lambda l, a: jnp.where(l == 1, a, 0)
    out_spec = pl.BlockSpec((None, None, tr, cc), lambda l, k, r, c: (l, k, r, 0))
    return pl.pallas_call(
        body, name="pair_sum",
        grid_spec=pltpu.PrefetchScalarGridSpec(
            num_scalar_prefetch=1, grid=(DEPTH, N_CHIPS, nr),
            in_specs=[pl.BlockSpec((None, None, tr, cc), lambda l, k, r, c: (lay0(l, k, N_CHIPS - 1), c[0], lay0(l, r, nr - 1), 0)),
                      pl.BlockSpec((None, None, tr, cc), lambda l, k, r, c: (lay1(l, k), c[0], lay1(l, r), 0)),
                      pl.BlockSpec((None, tr, cc), lambda l, k, r, c: (lay0(l, k, N_CHIPS - 1), lay0(l, r, nr - 1), 0)),
                      pl.BlockSpec((None, tr, cc), lambda l, k, r, c: (lay1(l, k), lay1(l, r), 0))],
            out_specs=[out_spec, out_spec]),
        out_shape=[jax.ShapeDtypeStruct((DEPTH, N_CHIPS, rb, cc), f32), jax.ShapeDtypeStruct((DEPTH, N_CHIPS, rb, cc), bf16)],
        compiler_params=_params("arbitrary", "arbitrary", "arbitrary"),
    )(c_arr, g0, g1, r0, r1)


def chip_exchange(parts):
    n = len(parts)

    def body(*refs):
        srcs, dsts, ssem, rsem = refs[:n], refs[n:2 * n], refs[2 * n], refs[2 * n + 1]
        x, y, c = _coords()
        cps = []
        for j, chip in enumerate(_peer_chips(x, y)):
            kj = 2 * chip[0] + chip[1]
            for i in range(n):
                cps.append(_remote(srcs[i].at[:, kj], dsts[i].at[j], ssem.at[i * 3 + j], rsem.at[i * 3 + j], (*chip, c)))
        for cp in cps:
            cp.start()
        for cp in cps:
            cp.wait()

    return pl.pallas_call(
        body, name="chip_exchange",
        out_shape=[jax.ShapeDtypeStruct((N_PEER_CHIPS, DEPTH) + p.shape[2:], p.dtype) for p in parts],
        in_specs=[HBM] * n, out_specs=[HBM] * n,
        scratch_shapes=[pltpu.SemaphoreType.DMA((n * N_PEER_CHIPS,)), pltpu.SemaphoreType.DMA((n * N_PEER_CHIPS,))],
    )(*parts)


def shard_sum(p32, r2, kc_arr, tr):
    _, _, rb, cc = p32.shape
    nr = rb // tr

    def body(kc_ref, p_ref, r_ref, o_ref):
        o_ref[...] = ((p_ref[...] + r_ref[0].astype(f32)) + r_ref[1].astype(f32)) + r_ref[2].astype(f32)

    return pl.pallas_call(
        body, name="shard_sum",
        grid_spec=pltpu.PrefetchScalarGridSpec(
            num_scalar_prefetch=1, grid=(DEPTH, nr),
            in_specs=[pl.BlockSpec((None, None, tr, cc), lambda l, r, kc: (l, kc[0], r, 0)),
                      pl.BlockSpec((N_PEER_CHIPS, None, tr, cc), lambda l, r, kc: (0, l, r, 0))],
            out_specs=pl.BlockSpec((None, None, tr, cc), lambda l, r, kc: (l, kc[1], r, 0))),
        out_shape=jax.ShapeDtypeStruct((DEPTH, 2, rb, cc), f32),
        compiler_params=_params("arbitrary", "arbitrary"),
    )(kc_arr, p32, r2)


def sibling_share(bufs):
    n = len(bufs)

    def body(*refs):
        bufs_, ssem, rsem = refs[n:2 * n], refs[2 * n], refs[2 * n + 1]
        x, y, c = _coords()
        cps = [_remote(bufs_[i].at[:, c], bufs_[i].at[:, c], ssem.at[i], rsem.at[i], (x, y, 1 - c)) for i in range(n)]
        for cp in cps:
            cp.start()
        for cp in cps:
            cp.wait_send()
        for i in range(n):
            other = bufs_[i].at[:, 1 - c]
            _remote(other, other, ssem.at[i], rsem.at[i], (x, y, 1 - c)).wait_recv()

    return pl.pallas_call(
        body, name="sibling_share",
        out_shape=[jax.ShapeDtypeStruct(b.shape, b.dtype) for b in bufs],
        in_specs=[HBM] * n, out_specs=[HBM] * n, input_output_aliases={i: i for i in range(n)},
        scratch_shapes=[pltpu.SemaphoreType.DMA((n,)), pltpu.SemaphoreType.DMA((n,))],
    )(*bufs)


def _adamw_math(w, g, m, v):
    m_new = ADAM_B1 * m + (1.0 - ADAM_B1) * g
    v_new = ADAM_B2 * v + (1.0 - ADAM_B2) * jnp.square(g)
    m_hat = m_new / (1.0 - ADAM_B1 ** ADAM_STEP)
    v_hat = v_new / (1.0 - ADAM_B2 ** ADAM_STEP)
    delta = -ADAM_LR * (m_hat / (jnp.sqrt(v_hat) + ADAM_EPS) + ADAM_WD * w)
    return delta, m_new, v_new


def adamw_shard(w, g, m, v):
    depth, rows, cols = w.shape
    half = rows // 2
    tr = min(256, half)
    nr = half // tr

    def body(w_ref, g_ref, m_ref, v_ref, go_ref, d_ref, nm_ref, nv_ref):
        gv = g_ref[...]
        go_ref[...] = gv
        d_ref[...], nm_ref[...], nv_ref[...] = _adamw_math(w_ref[...], gv, m_ref[...], v_ref[...])

    spec = pl.BlockSpec((None, tr, cols), lambda l, h, r: (l, h * nr + r, 0))
    gspec = pl.BlockSpec((None, None, tr, cols), lambda l, h, r: (l, h, r, 0))
    return pl.pallas_call(
        body, name="adamw_shard", grid=(depth, 2, nr), in_specs=[spec, gspec, spec, spec], out_specs=[spec] * 4,
        out_shape=[jax.ShapeDtypeStruct(w.shape, f32)] * 4,
        compiler_params=_params("arbitrary", "arbitrary", "arbitrary"),
    )(w, g, m, v)


def adamw_small(w, g, m, v):
    def body(w_ref, g_ref, m_ref, v_ref, d_ref, nm_ref, nv_ref):
        d_ref[...], nm_ref[...], nv_ref[...] = _adamw_math(w_ref[...], g_ref[...], m_ref[...], v_ref[...])

    return pl.pallas_call(
        body, name="adamw_small", out_shape=[jax.ShapeDtypeStruct(w.shape, f32)] * 3,
    )(w, g, m, v)


WEIGHTS = ("mix_norm_g", "w_in", "q_gain", "k_gain", "sinks", "rel_bias", "conv_w", "conv_b", "dt_bias", "a_log", "d_skip",
           "ssm_norm_g", "w_out", "mlp_norm_g", "w_up", "w_down")
BIG = ("w_in", "w_out", "w_up", "w_down")
SMALL = tuple(n for n in WEIGHTS if n not in BIG)
PACK_COLS = 1024
PACK_ROWS = 16


def _pack(named):
    flat = jnp.concatenate([named[n].reshape(-1) for n in SMALL])
    return jnp.pad(flat, (0, PACK_ROWS * PACK_COLS - flat.shape[0])).reshape(PACK_ROWS, PACK_COLS)


def _unpack(buf, shapes):
    flat = buf.reshape(-1)
    out, at = {}, 0
    for n in SMALL:
        size = int(np.prod(shapes[n]))
        out[n] = flat[at:at + size].reshape(shapes[n])
        at += size
    return out


class _Exchange:
    GROUPS = {"A": (("w_up", 0), ("w_down", 0)), "B": (("w_in", 1), ("w_out", 1)), "C": (("w_up", 1), ("w_down", 1))}
    RELAY_AT = {("mid", 0): "A", ("pre_mlp", 0): "B", ("mid", 1): "C"}

    def __init__(self, wts, k_me):
        self.k_me = k_me
        self.own = {(n, l): wts[n][l].astype(bf16) for n in BIG for l in range(DEPTH)}
        now = gather_now([self.own["w_in", 0], self.own["w_out", 0]], wts["conv_w"])
        self.ready = {("w_in", 0): self._fill(now[0], self.own["w_in", 0]),
                      ("w_out", 0): self._fill(now[1], self.own["w_out", 0])}
        conv = self._fill(now[2], wts["conv_w"])
        self.conv_w = jnp.transpose(conv, (1, 2, 0, 3)).reshape(DEPTH, CONV_WIDTH, D_CONV)
        self.ici, self.relay, self.grad = {}, {}, {}
        for g, pieces in self.GROUPS.items():
            srcs = [self.own[p] for p in pieces]
            lands = [lax.empty((N_CHIPS,) + s.shape, s.dtype) for s in srcs]
            self.ici[g] = split_start("gather%s_ici_start" % g, "ici", srcs, lands)

    def _fill(self, land, own):
        return lax.dynamic_update_slice(land, own[None], (self.k_me,) + (0,) * own.ndim)

    def stage(self, name, after):
        if name == ("begin", 0):
            return sum(h["token"] for h in self.ici.values())
        g = self.RELAY_AT.get(name)
        if g is None:
            return 0.0
        lands = split_wait(self.ici[g], after)
        self.relay[g] = split_start("gather%s_relay_start" % g, "relay", [], lands)
        return self.relay[g]["token"]

    def _get(self, piece, after):
        if piece not in self.ready:
            g = [k for k, pieces in self.GROUPS.items() if piece in pieces][0]
            lands = split_wait(self.relay[g], after)
            for p, land in zip(self.GROUPS[g], lands):
                self.ready[p] = self._fill(land, self.own[p])
        return self.ready[piece]

    def w_in(self, l, after):
        g = self._get(("w_in", l), after)
        return _to_aligned(jnp.transpose(g, (1, 0, 2)).reshape(D_MODEL, D_IN))

    def w_out(self, l, after):
        return self._get(("w_out", l), after).reshape(D_MODEL, D_MODEL)

    def mlp(self, l, after):
        return self._get(("w_up", l), after), self._get(("w_down", l), after)

    def grads(self, name, arrays, after):
        for n, g in arrays.items():
            self.grad[n, name[1]] = g
        return 0.0


def kernel(x, mix_norm_g, w_in, q_gain, k_gain, sinks, rel_bias, conv_w, conv_b, dt_bias, a_log, d_skip, ssm_norm_g, w_out, mlp_norm_g, w_up, w_down, loss_target, m_mix_norm_g, m_w_in, m_q_gain, m_k_gain, m_sinks, m_rel_bias, m_conv_w, m_conv_b, m_dt_bias, m_a_log, m_d_skip, m_ssm_norm_g, m_w_out, m_mlp_norm_g, m_w_up, m_w_down, v_mix_norm_g, v_w_in, v_q_gain, v_k_gain, v_sinks, v_rel_bias, v_conv_w, v_conv_b, v_dt_bias, v_a_log, v_d_skip, v_ssm_norm_g, v_w_out, v_mlp_norm_g, v_w_up, v_w_down):
    wts = dict(mix_norm_g=mix_norm_g, w_in=w_in, q_gain=q_gain, k_gain=k_gain, sinks=sinks, rel_bias=rel_bias, conv_w=conv_w,
               conv_b=conv_b, dt_bias=dt_bias, a_log=a_log, d_skip=d_skip, ssm_norm_g=ssm_norm_g, w_out=w_out,
               mlp_norm_g=mlp_norm_g, w_up=w_up, w_down=w_down)
    mom = dict(mix_norm_g=m_mix_norm_g, w_in=m_w_in, q_gain=m_q_gain, k_gain=m_k_gain, sinks=m_sinks, rel_bias=m_rel_bias,
               conv_w=m_conv_w, conv_b=m_conv_b, dt_bias=m_dt_bias, a_log=m_a_log, d_skip=m_d_skip, ssm_norm_g=m_ssm_norm_g,
               w_out=m_w_out, mlp_norm_g=m_mlp_norm_g, w_up=m_w_up, w_down=m_w_down)
    var = dict(mix_norm_g=v_mix_norm_g, w_in=v_w_in, q_gain=v_q_gain, k_gain=v_k_gain, sinks=v_sinks, rel_bias=v_rel_bias,
               conv_w=v_conv_w, conv_b=v_conv_b, dt_bias=v_dt_bias, a_log=v_a_log, d_skip=v_d_skip, ssm_norm_g=v_ssm_norm_g,
               w_out=v_w_out, mlp_norm_g=v_mlp_norm_g, w_up=v_w_up, w_down=v_w_down)
    xi, yi, ci = _coords()
    k_me = 2 * xi + yi
    c_arr = jnp.reshape(ci, (1,)).astype(jnp.int32)
    kc_arr = jnp.stack([k_me, ci]).astype(jnp.int32)

    prov = _Exchange(wts, k_me)
    small_w = {n: wts[n] for n in SMALL}
    small_w["conv_w"] = prov.conv_w
    loss, dx, grads = local_step(x[0], loss_target[0], small_w, prov)
    loss = lax.psum(loss[0, 0], ("x", "y", "c"))
    grads.update({n: [prov.grad[n, l] for l in range(DEPTH)] for n in BIG})

    small_shapes = {n: grads[n].shape for n in SMALL}
    small = _unpack(small_all_reduce(_pack(grads)), small_shapes)
    cols = conv_w.shape[-1]
    small["conv_w"] = lax.dynamic_slice_in_dim(small["conv_w"], k_me * cols, cols, axis=2)

    def view(name, g):
        if name == "w_in":
            g = jnp.transpose(_from_aligned(g).reshape(D_MODEL, N_CHIPS, D_IN // N_CHIPS), (1, 0, 2))
        rows = wts[name].shape[1] // 2
        return g.reshape(N_CHIPS, 2, rows, wts[name].shape[2])

    views = [view(n, grads[n][l]) for n in BIG for l in range(DEPTH)]
    recv = sibling_exchange(views)
    p32s, pbs = [], []
    for i, n in enumerate(BIG):
        tr = min(256, views[2 * i].shape[2])
        p32, pb = pair_sum(views[2 * i], views[2 * i + 1], recv[2 * i], recv[2 * i + 1], c_arr, tr)
        p32s.append(p32)
        pbs.append(pb)
    r2s = chip_exchange(pbs)
    reds = [shard_sum(p32s[i], r2s[i], kc_arr, min(256, p32s[i].shape[2])) for i in range(len(BIG))]
    fulls = sibling_share(reds)

    g_out_d, d_out_d, m_out_d, v_out_d = {}, {}, {}, {}
    for i, n in enumerate(BIG):
        g_out_d[n], d_out_d[n], m_out_d[n], v_out_d[n] = adamw_shard(wts[n], fulls[i], mom[n], var[n])
    shard_shapes = {n: wts[n].shape for n in SMALL}
    d, nm, nv = adamw_small(_pack(wts), _pack(small), _pack(mom), _pack(var))
    for dst, buf in ((d_out_d, d), (m_out_d, nm), (v_out_d, nv)):
        dst.update(_unpack(buf, shard_shapes))
    g_out_d.update(small)

    return (loss, dx[None], *[g_out_d[n] for n in WEIGHTS], *[d_out_d[n] for n in WEIGHTS],
            *[m_out_d[n] for n in WEIGHTS], *[v_out_d[n] for n in WEIGHTS])
```

```python
import functools

import numpy as np
import jax
import jax.numpy as jnp
from jax import lax
from jax.experimental import pallas as pl
from jax.experimental.pallas import tpu as pltpu

f32 = jnp.float32
bf16 = jnp.bfloat16

SEQ = 2048
D_MODEL = 1024
DEPTH = 2
HEAD_DIM = 64
N_Q_HEADS = 8
N_KV_HEADS = 2
Q_PER_KV = N_Q_HEADS // N_KV_HEADS
BLOCK = 128
N_BLOCKS = SEQ // BLOCK
N_BUCKETS = 32
MAX_DISTANCE = 128
SSM_HEADS = 8
SSM_HEAD_DIM = 64
SSM_GROUPS = 2
HEADS_PER_GROUP = SSM_HEADS // SSM_GROUPS
SSM_STATE = 128
CONV_WIDTH = 4
CHUNK = 128
N_CHUNKS = SEQ // CHUNK
D_FF = 4 * D_MODEL
D_ATTN = N_Q_HEADS * HEAD_DIM
D_KV = N_KV_HEADS * HEAD_DIM
D_SSM = SSM_HEADS * SSM_HEAD_DIM
D_BC = SSM_GROUPS * SSM_STATE
D_CONV = D_SSM + 2 * D_BC
D_IN = D_ATTN + 2 * D_KV + D_SSM + D_CONV + SSM_HEADS
EPS = 1e-6
NEG = -1e30
N_CHIPS = 4
FF_TILE = D_FF // N_CHIPS

LANE = 128
PW = D_ATTN + D_SSM + D_CONV + 2 * D_KV + LANE
OFF_Q, OFF_Z, OFF_X, OFF_K, OFF_V, OFF_DT = 0, 512, 1024, 2048, 2176, 2304

ADAM_LR = 0.001
ADAM_B1 = 0.9
ADAM_B2 = 0.999
ADAM_EPS = 1e-08
ADAM_WD = 0.01
ADAM_STEP = 10

VMEM_LIMIT = 56 * 1024 * 1024


def _params(*sem):
    return pltpu.CompilerParams(dimension_semantics=tuple(sem), vmem_limit_bytes=VMEM_LIMIT)


def _bdot(a, b):
    return jnp.dot(a.astype(bf16), b.astype(bf16), preferred_element_type=f32)


def _bdot_nt(a, b):
    return lax.dot_general(a.astype(bf16), b.astype(bf16), (((1,), (1,)), ((), ())), preferred_element_type=f32)


def _bdot_tn(a, b):
    return lax.dot_general(a.astype(bf16), b.astype(bf16), (((0,), (0,)), ((), ())), preferred_element_type=f32)


def _hdot(a, b):
    return jnp.dot(a, b, precision=lax.Precision.HIGHEST, preferred_element_type=f32)


def _sigmoid(x):
    return 1.0 / (1.0 + jnp.exp(-x))


def _softplus(x):
    return jnp.maximum(x, 0.0) + jnp.log1p(jnp.exp(-jnp.abs(x)))


def _rms(x):
    return lax.rsqrt(jnp.mean(x * x, axis=-1, keepdims=True) + EPS)


def _rms_bwd(dy, xhat, r, g):
    t = dy * g
    return r * (t - xhat * jnp.mean(t * xhat, axis=-1, keepdims=True))


def _full(shape):
    return pl.BlockSpec(shape, lambda *_: (0,) * len(shape))


def _to_aligned(w):
    q, k, v, z, xbc, dt = jnp.split(w, [512, 640, 768, 1280, 2304], axis=-1)
    pad = jnp.zeros(w.shape[:-1] + (LANE - SSM_HEADS,), w.dtype)
    return jnp.concatenate([q, z, xbc, k, v, dt, pad], axis=-1)


def _from_aligned(w):
    q, z, xbc, k, v, dt = (w[..., OFF_Q:OFF_Z], w[..., OFF_Z:OFF_X], w[..., OFF_X:OFF_K], w[..., OFF_K:OFF_V],
                           w[..., OFF_V:OFF_DT], w[..., OFF_DT:OFF_DT + SSM_HEADS])
    return jnp.concatenate([q, k, v, z, xbc, dt], axis=-1)


def _bucket_table():
    qi = np.arange(BLOCK)[:, None]
    kj = np.arange(2 * BLOCK)[None, :]
    dist = qi + BLOCK - kj
    ok = (dist >= 0) & (dist < 128)
    d = np.clip(dist, 0, None)
    max_exact = N_BUCKETS // 2
    d_f = np.maximum(d, 1).astype(np.float32)
    large = max_exact + (np.log(d_f / np.float32(max_exact)) / np.float32(np.log(MAX_DISTANCE / max_exact))
                         * np.float32(N_BUCKETS - max_exact)).astype(np.int32)
    large = np.minimum(large, N_BUCKETS - 1)
    bucket = np.where(d < max_exact, d, large)
    return np.where(ok, bucket, -1).astype(np.int32)


def bias_build(rel_bias, bucket):
    def body(rel_ref, bkt_ref, o_ref):
        bkt = bkt_ref[...]
        for h in range(N_Q_HEADS):
            acc = jnp.where(bkt < 0, NEG, 0.0).astype(f32)
            for b in range(N_BUCKETS):
                acc = acc + jnp.where(bkt == b, rel_ref[b, h], 0.0)
            o_ref[h] = acc

    return pl.pallas_call(
        body, name="bias_build", out_shape=jax.ShapeDtypeStruct((N_Q_HEADS, BLOCK, 2 * BLOCK), f32),
        in_specs=[pl.BlockSpec(memory_space=pltpu.SMEM), pl.BlockSpec(memory_space=pltpu.VMEM)],
        out_specs=pl.BlockSpec(memory_space=pltpu.VMEM),
    )(rel_bias, bucket)


def bias_bwd(dband0, dband1, bucket):
    def body(d0_ref, d1_ref, bkt_ref, o_ref):
        bkt = bkt_ref[...]
        o_ref[...] = jnp.zeros_like(o_ref)
        for h in range(N_Q_HEADS):
            d = d0_ref[h] + d1_ref[h]
            for b in range(N_BUCKETS):
                part = jnp.sum(jnp.where(bkt == b, d, 0.0), axis=1, keepdims=True)
                o_ref[b:b + 1, h:h + 1] = jnp.sum(part, axis=0, keepdims=True)

    return pl.pallas_call(
        body, name="bias_bwd", out_shape=jax.ShapeDtypeStruct((N_BUCKETS, LANE), f32),
    )(dband0, dband1, bucket)


def in_fwd(x, g, w, tm=256):
    def body(x_ref, g_ref, w_ref, o_ref):
        xv = x_ref[...]
        h = xv * _rms(xv) * g_ref[...]
        o_ref[...] = _bdot(h, w_ref[...])

    return pl.pallas_call(
        body, name="in_fwd", grid=(SEQ // tm,),
        in_specs=[pl.BlockSpec((tm, D_MODEL), lambda i: (i, 0)), _full((1, D_MODEL)), _full((D_MODEL, PW))],
        out_specs=pl.BlockSpec((tm, PW), lambda i: (i, 0)),
        out_shape=jax.ShapeDtypeStruct((SEQ, PW), f32),
        compiler_params=_params("arbitrary"),
    )(x, g, w)


def in_bwd(dq, dz, dxbc, dk, dv, ddt, x, g, w, dres, tm=256):
    def body(dq_ref, dz_ref, dx_ref, dk_ref, dv_ref, ddt_ref, x_ref, g_ref, w_ref, dres_ref, o_ref, dw_ref, dg_ref):
        i = pl.program_id(0)

        @pl.when(i == 0)
        def _():
            dw_ref[...] = jnp.zeros_like(dw_ref)
            dg_ref[...] = jnp.zeros_like(dg_ref)

        dproj = jnp.concatenate([dq_ref[...], dz_ref[...], dx_ref[...], dk_ref[...], dv_ref[...], ddt_ref[...]],
                                axis=-1).astype(bf16)
        xv = x_ref[...]
        r = _rms(xv)
        xhat = xv * r
        gv = g_ref[...]
        h = xhat * gv
        dw_ref[...] += _bdot_tn(h, dproj)
        dh = _bdot_nt(dproj, w_ref[...])
        dg_ref[...] += jnp.sum(dh * xhat, axis=0, keepdims=True)
        o_ref[...] = dres_ref[...] + _rms_bwd(dh, xhat, r, gv)

    tok = lambda w_: pl.BlockSpec((tm, w_), lambda i: (i, 0))
    return pl.pallas_call(
        body, name="in_bwd", grid=(SEQ // tm,),
        in_specs=[tok(D_ATTN), tok(D_SSM), tok(D_CONV), tok(D_KV // 1), tok(D_KV // 1), tok(LANE), tok(D_MODEL),
                  _full((1, D_MODEL)), _full((D_MODEL, PW)), tok(D_MODEL)],
        out_specs=[tok(D_MODEL), _full((D_MODEL, PW)), _full((1, D_MODEL))],
        out_shape=[jax.ShapeDtypeStruct((SEQ, D_MODEL), f32), jax.ShapeDtypeStruct((D_MODEL, PW), f32),
                   jax.ShapeDtypeStruct((1, D_MODEL), f32)],
        compiler_params=_params("arbitrary"),
    )(dq, dz, dxbc, dk, dv, ddt, x, g, w, dres)


def _attn_probs(qn, kn, bias_h, sink, first, col):
    s = _bdot_nt(qn, kn) * (HEAD_DIM ** -0.5) + bias_h
    s = jnp.where(jnp.logical_and(first, col < BLOCK), NEG, s)
    m = jnp.maximum(jnp.max(s, axis=-1, keepdims=True), sink)
    p = jnp.exp(s - m)
    psink = jnp.exp(sink - m)
    inv = 1.0 / (jnp.sum(p, axis=-1, keepdims=True) + psink)
    return p * inv, psink * inv


def attn_fwd(proj, q_gain, k_gain, sinks, bias):
    kcol, vcol = OFF_K // D_KV, OFF_V // D_KV

    def body(q_ref, kc_ref, kp_ref, vc_ref, vp_ref, qg_ref, kg_ref, sink_ref, bias_ref, o_ref):
        n = pl.program_id(0)
        first = n == 0
        col = lax.broadcasted_iota(jnp.int32, (BLOCK, 2 * BLOCK), 1)
        k2 = jnp.concatenate([kp_ref[...], kc_ref[...]], axis=0)
        v2 = jnp.concatenate([vp_ref[...], vc_ref[...]], axis=0)
        qg, kg = qg_ref[...], kg_ref[...]
        for hk in range(N_KV_HEADS):
            kk = k2[:, hk * HEAD_DIM:(hk + 1) * HEAD_DIM]
            kn = (kk * _rms(kk) * kg).astype(bf16)
            vb = v2[:, hk * HEAD_DIM:(hk + 1) * HEAD_DIM].astype(bf16)
            for gq in range(Q_PER_KV):
                h = hk * Q_PER_KV + gq
                qq = q_ref[:, h * HEAD_DIM:(h + 1) * HEAD_DIM]
                qn = qq * _rms(qq) * qg
                p, _ = _attn_probs(qn, kn, bias_ref[h], sink_ref[h], first, col)
                o_ref[:, h * HEAD_DIM:(h + 1) * HEAD_DIM] = _bdot(p, vb)

    prev = lambda n: jnp.maximum(n - 1, 0)
    return pl.pallas_call(
        body, name="attn_fwd", grid=(N_BLOCKS,),
        in_specs=[pl.BlockSpec((BLOCK, D_ATTN), lambda n: (n, 0)),
                  pl.BlockSpec((BLOCK, D_KV), lambda n: (n, kcol)), pl.BlockSpec((BLOCK, D_KV), lambda n: (prev(n), kcol)),
                  pl.BlockSpec((BLOCK, D_KV), lambda n: (n, vcol)), pl.BlockSpec((BLOCK, D_KV), lambda n: (prev(n), vcol)),
                  _full((1, HEAD_DIM)), _full((1, HEAD_DIM)), pl.BlockSpec(memory_space=pltpu.SMEM),
                  _full((N_Q_HEADS, BLOCK, 2 * BLOCK))],
        out_specs=pl.BlockSpec((BLOCK, D_ATTN), lambda n: (n, 0)),
        out_shape=jax.ShapeDtypeStruct((SEQ, D_ATTN), f32),
        compiler_params=_params("arbitrary"),
    )(proj, proj, proj, proj, proj, q_gain, k_gain, sinks, bias)


def attn_bwd(proj, d_out, q_gain, k_gain, sinks, bias):
    kcol, vcol = OFF_K // D_KV, OFF_V // D_KV

    def body(q_ref, kc_ref, kp_ref, vc_ref, vp_ref, do_ref, qg_ref, kg_ref, sink_ref, bias_ref,
             dq_ref, dk_ref, dv_ref, dband_ref, dsink_ref, dqg_ref, dkg_ref, dkn_scr, dv_scr):
        i = pl.program_id(0)
        first = i == N_BLOCKS - 1

        @pl.when(i == 0)
        def _():
            for ref in (dband_ref, dsink_ref, dqg_ref, dkg_ref, dkn_scr, dv_scr):
                ref[...] = jnp.zeros_like(ref)

        col = lax.broadcasted_iota(jnp.int32, (BLOCK, 2 * BLOCK), 1)
        k2 = jnp.concatenate([kp_ref[...], kc_ref[...]], axis=0)
        v2 = jnp.concatenate([vp_ref[...], vc_ref[...]], axis=0)
        qg, kg = qg_ref[...], kg_ref[...]
        scale = HEAD_DIM ** -0.5
        for hk in range(N_KV_HEADS):
            sl = slice(hk * HEAD_DIM, (hk + 1) * HEAD_DIM)
            kk = k2[:, sl]
            rk = _rms(kk)
            khat = kk * rk
            kn = (khat * kg).astype(bf16)
            vb = v2[:, sl].astype(bf16)
            dkn = jnp.zeros((2 * BLOCK, HEAD_DIM), f32)
            dvv = jnp.zeros((2 * BLOCK, HEAD_DIM), f32)
            for gq in range(Q_PER_KV):
                h = hk * Q_PER_KV + gq
                hs = slice(h * HEAD_DIM, (h + 1) * HEAD_DIM)
                qq = q_ref[:, hs]
                rq = _rms(qq)
                qhat = qq * rq
                qn = qhat * qg
                p, psink = _attn_probs(qn, kn, bias_ref[h], sink_ref[h], first, col)
                d_o = do_ref[:, hs]
                dp = _bdot_nt(d_o, vb)
                delta = jnp.sum(p * dp, axis=-1, keepdims=True)
                ds = p * (dp - delta)
                dband_ref[h] += ds
                dsink_ref[:, h:h + 1] += -jnp.sum(psink * delta, axis=0, keepdims=True)
                dqn = _bdot(ds, kn) * scale
                dkn = dkn + _bdot_tn(ds, qn) * scale
                dvv = dvv + _bdot_tn(p, d_o)
                dqg_ref[...] += jnp.sum(dqn * qhat, axis=0, keepdims=True)
                dq_ref[:, hs] = _rms_bwd(dqn, qhat, rq, qg)
            dkn_cur = dkn[BLOCK:] + dkn_scr[:, sl]
            dkn_scr[:, sl] = dkn[:BLOCK]
            khat_c, rk_c = khat[BLOCK:], rk[BLOCK:]
            dkg_ref[...] += jnp.sum(dkn_cur * khat_c, axis=0, keepdims=True)
            dk_ref[:, sl] = _rms_bwd(dkn_cur, khat_c, rk_c, kg)
            dv_ref[:, sl] = dvv[BLOCK:] + dv_scr[:, sl]
            dv_scr[:, sl] = dvv[:BLOCK]

    blk = lambda i: N_BLOCKS - 1 - i
    prev = lambda i: jnp.maximum(N_BLOCKS - 2 - i, 0)
    return pl.pallas_call(
        body, name="attn_bwd", grid=(N_BLOCKS,),
        in_specs=[pl.BlockSpec((BLOCK, D_ATTN), lambda i: (blk(i), 0)),
                  pl.BlockSpec((BLOCK, D_KV), lambda i: (blk(i), kcol)), pl.BlockSpec((BLOCK, D_KV), lambda i: (prev(i), kcol)),
                  pl.BlockSpec((BLOCK, D_KV), lambda i: (blk(i), vcol)), pl.BlockSpec((BLOCK, D_KV), lambda i: (prev(i), vcol)),
                  pl.BlockSpec((BLOCK, D_ATTN), lambda i: (blk(i), 0)),
                  _full((1, HEAD_DIM)), _full((1, HEAD_DIM)), pl.BlockSpec(memory_space=pltpu.SMEM),
                  _full((N_Q_HEADS, BLOCK, 2 * BLOCK))],
        out_specs=[pl.BlockSpec((BLOCK, D_ATTN), lambda i: (blk(i), 0)), pl.BlockSpec((BLOCK, D_KV), lambda i: (blk(i), 0)),
                   pl.BlockSpec((BLOCK, D_KV), lambda i: (blk(i), 0)), _full((N_Q_HEADS, BLOCK, 2 * BLOCK)),
                   _full((1, LANE)), _full((1, HEAD_DIM)), _full((1, HEAD_DIM))],
        out_shape=[jax.ShapeDtypeStruct((SEQ, D_ATTN), f32), jax.ShapeDtypeStruct((SEQ, D_KV), f32),
                   jax.ShapeDtypeStruct((SEQ, D_KV), f32), jax.ShapeDtypeStruct((N_Q_HEADS, BLOCK, 2 * BLOCK), f32),
                   jax.ShapeDtypeStruct((1, LANE), f32), jax.ShapeDtypeStruct((1, HEAD_DIM), f32),
                   jax.ShapeDtypeStruct((1, HEAD_DIM), f32)],
        scratch_shapes=[pltpu.VMEM((BLOCK, D_KV), f32), pltpu.VMEM((BLOCK, D_KV), f32)],
        compiler_params=_params("arbitrary"),
    )(proj, proj, proj, proj, proj, d_out, q_gain, k_gain, sinks, bias)


def _shift_down(u, s, row):
    if s == 0:
        return u
    return jnp.where(row >= s, pltpu.roll(u, s, 0), 0.0)


def _shift_up(u, s, row):
    if s == 0:
        return u
    return jnp.where(row < SEQ - s, pltpu.roll(u, SEQ - s, 0), 0.0)


def conv_fwd(proj, conv_w, conv_b):
    xcol = OFF_X // LANE

    def body(u_ref, w_ref, b_ref, o_ref):
        u = u_ref[...]
        row = lax.broadcasted_iota(jnp.int32, u.shape, 0)
        pre = b_ref[...] + jnp.zeros_like(u)
        for k in range(CONV_WIDTH):
            pre = pre + w_ref[k:k + 1, :] * _shift_down(u, CONV_WIDTH - 1 - k, row)
        o_ref[...] = pre * _sigmoid(pre)

    return pl.pallas_call(
        body, name="conv_fwd", grid=(D_CONV // LANE,),
        in_specs=[pl.BlockSpec((SEQ, LANE), lambda j: (0, xcol + j)), pl.BlockSpec((CONV_WIDTH, LANE), lambda j: (0, j)),
                  pl.BlockSpec((1, LANE), lambda j: (0, j))],
        out_specs=pl.BlockSpec((SEQ, LANE), lambda j: (0, j)),
        out_shape=jax.ShapeDtypeStruct((SEQ, D_CONV), f32),
        compiler_params=_params("arbitrary"),
    )(proj, conv_w, conv_b)


def conv_bwd(proj, d_act, conv_w, conv_b):
    xcol = OFF_X // LANE

    def body(u_ref, da_ref, w_ref, b_ref, du_ref, dw_ref, db_ref):
        u = u_ref[...]
        row = lax.broadcasted_iota(jnp.int32, u.shape, 0)
        shifted = [_shift_down(u, CONV_WIDTH - 1 - k, row) for k in range(CONV_WIDTH)]
        pre = b_ref[...] + jnp.zeros_like(u)
        for k in range(CONV_WIDTH):
            pre = pre + w_ref[k:k + 1, :] * shifted[k]
        sg = _sigmoid(pre)
        dpre = da_ref[...] * (sg * (1.0 + pre * (1.0 - sg)))
        db_ref[...] = jnp.sum(dpre, axis=0, keepdims=True)
        du = jnp.zeros_like(u)
        for k in range(CONV_WIDTH):
            dw_ref[k:k + 1, :] = jnp.sum(dpre * shifted[k], axis=0, keepdims=True)
            du = du + w_ref[k:k + 1, :] * _shift_up(dpre, CONV_WIDTH - 1 - k, row)
        du_ref[...] = du

    return pl.pallas_call(
        body, name="conv_bwd", grid=(D_CONV // LANE,),
        in_specs=[pl.BlockSpec((SEQ, LANE), lambda j: (0, xcol + j)), pl.BlockSpec((SEQ, LANE), lambda j: (0, j)),
                  pl.BlockSpec((CONV_WIDTH, LANE), lambda j: (0, j)), pl.BlockSpec((1, LANE), lambda j: (0, j))],
        out_specs=[pl.BlockSpec((SEQ, LANE), lambda j: (0, j)), pl.BlockSpec((CONV_WIDTH, LANE), lambda j: (0, j)),
                   pl.BlockSpec((1, LANE), lambda j: (0, j))],
        out_shape=[jax.ShapeDtypeStruct((SEQ, D_CONV), f32), jax.ShapeDtypeStruct((CONV_WIDTH, D_CONV), f32),
                   jax.ShapeDtypeStruct((1, D_CONV), f32)],
        compiler_params=_params("arbitrary"),
    )(proj, d_act, conv_w, conv_b)


def _ssd_chunk_common(dt_raw, dtb, alog):
    row = lax.broadcasted_iota(jnp.int32, (CHUNK, CHUNK), 0)
    col = lax.broadcasted_iota(jnp.int32, (CHUNK, CHUNK), 1)
    tri = (row >= col).astype(f32)
    strict = (row > col).astype(f32)
    dtp = _softplus(dt_raw + dtb)
    a_row = -jnp.exp(alog)
    d_a = dtp * a_row
    cs = _hdot(tri, d_a)
    cs_last = cs[CHUNK - 1:CHUNK, :]
    return row, col, tri, strict, dtp, a_row, d_a, cs, cs_last


def _seg_decay(tri, strict, d_a_h, row, col):
    seg = _hdot(tri, d_a_h * strict)
    return jnp.where(row >= col, jnp.exp(seg), 0.0)


def ssd_fwd(act, proj, dt_bias, a_log, d_skip, norm_g):
    zcol, dtcol = OFF_Z // D_SSM, OFF_DT // LANE
    gw = D_SSM // SSM_GROUPS

    def body(act_ref, z_ref, dt_ref, dtb_ref, alog_ref, dsk_ref, ng_ref, out_ref, ypre_ref, st_ref, state, ybuf):
        c = pl.program_id(0)

        @pl.when(c == 0)
        def _():
            state[...] = jnp.zeros_like(state)

        row, col, tri, strict, dtp, a_row, d_a, cs, cs_last = _ssd_chunk_common(dt_ref[...], dtb_ref[...], alog_ref[...])
        e_cs = jnp.exp(cs)
        dte = jnp.exp(cs_last - cs)
        ecl = jnp.exp(cs_last)
        dsk = dsk_ref[...]
        for g in range(SSM_GROUPS):
            bg = act_ref[:, D_SSM + g * SSM_STATE:D_SSM + (g + 1) * SSM_STATE]
            cg = act_ref[:, D_SSM + D_BC + g * SSM_STATE:D_SSM + D_BC + (g + 1) * SSM_STATE]
            cb = _bdot_nt(cg, bg)
            for r in range(HEADS_PER_GROUP):
                hd = g * HEADS_PER_GROUP + r
                hs = slice(hd * SSM_HEAD_DIM, (hd + 1) * SSM_HEAD_DIM)
                hl = slice(hd, hd + 1)
                x_h = act_ref[:, hs]
                xdt = x_h * dtp[:, hl]
                lm = _seg_decay(tri, strict, d_a[:, hl], row, col)
                prev = state[hd]
                st_ref[0, hd] = prev
                y = _bdot(cb * lm, xdt) + e_cs[:, hl] * _bdot(cg, prev) + x_h * dsk[:, hl]
                ybuf[:, hs] = y
                state[hd] = prev * ecl[:, hl] + _bdot_tn(bg, xdt * dte[:, hl])
        y = ybuf[...]
        ypre_ref[...] = y
        z = z_ref[...]
        yz = y * (z * _sigmoid(z))
        ng = ng_ref[...]
        for g in range(SSM_GROUPS):
            gs = slice(g * gw, (g + 1) * gw)
            part = yz[:, gs]
            out_ref[:, gs] = part * _rms(part) * ng[:, gs]

    return pl.pallas_call(
        body, name="ssd_fwd", grid=(N_CHUNKS,),
        in_specs=[pl.BlockSpec((CHUNK, D_CONV), lambda c: (c, 0)), pl.BlockSpec((CHUNK, D_SSM), lambda c: (c, zcol)),
                  pl.BlockSpec((CHUNK, LANE), lambda c: (c, dtcol)), _full((1, LANE)), _full((1, LANE)), _full((1, LANE)),
                  _full((1, D_SSM))],
        out_specs=[pl.BlockSpec((CHUNK, D_SSM), lambda c: (c, 0)), pl.BlockSpec((CHUNK, D_SSM), lambda c: (c, 0)),
                   pl.BlockSpec((1, SSM_HEADS, SSM_STATE, SSM_HEAD_DIM), lambda c: (c, 0, 0, 0))],
        out_shape=[jax.ShapeDtypeStruct((SEQ, D_SSM), f32), jax.ShapeDtypeStruct((SEQ, D_SSM), f32),
                   jax.ShapeDtypeStruct((N_CHUNKS, SSM_HEADS, SSM_STATE, SSM_HEAD_DIM), f32)],
        scratch_shapes=[pltpu.VMEM((SSM_HEADS, SSM_STATE, SSM_HEAD_DIM), f32), pltpu.VMEM((CHUNK, D_SSM), f32)],
        compiler_params=_params("arbitrary"),
    )(act, proj, proj, dt_bias, a_log, d_skip, norm_g)


def ssd_bwd(act, proj, ypre, states, d_out, dt_bias, a_log, d_skip, norm_g):
    zcol, dtcol = OFF_Z // D_SSM, OFF_DT // LANE
    gw = D_SSM // SSM_GROUPS

    def body(act_ref, z_ref, dt_ref, ypre_ref, st_ref, do_ref, dtb_ref, alog_ref, dsk_ref, ng_ref,
             dact_ref, ddt_ref, dz_ref, dng_ref, dpar_ref, dstate, dybuf):
        i = pl.program_id(0)

        @pl.when(i == 0)
        def _():
            for ref in (dng_ref, dpar_ref, dstate):
                ref[...] = jnp.zeros_like(ref)

        y = ypre_ref[...]
        z = z_ref[...]
        sgz = _sigmoid(z)
        sz = z * sgz
        yz = y * sz
        ng = ng_ref[...]
        d_o = do_ref[...]
        for g in range(SSM_GROUPS):
            gs = slice(g * gw, (g + 1) * gw)
            part = yz[:, gs]
            r = _rms(part)
            yhat = part * r
            dng_ref[:, gs] += jnp.sum(d_o[:, gs] * yhat, axis=0, keepdims=True)
            dyz = _rms_bwd(d_o[:, gs], yhat, r, ng[:, gs])
            dybuf[:, gs] = dyz * sz[:, gs]
            dz_ref[:, gs] = dyz * y[:, gs] * (sgz[:, gs] * (1.0 + z[:, gs] * (1.0 - sgz[:, gs])))

        row, col, tri, strict, dtp, a_row, d_a, cs, cs_last = _ssd_chunk_common(dt_ref[...], dtb_ref[...], alog_ref[...])
        upper = (row <= col).astype(f32)
        lane = lax.broadcasted_iota(jnp.int32, (CHUNK, LANE), 1)
        lane1 = lax.broadcasted_iota(jnp.int32, (1, LANE), 1)
        e_cs = jnp.exp(cs)
        dte = jnp.exp(cs_last - cs)
        ecl = jnp.exp(cs_last)
        dsk = dsk_ref[...]
        ddt_mat = jnp.zeros((CHUNK, LANE), f32)
        dcs_mat = jnp.zeros((CHUNK, LANE), f32)
        dda_mat = jnp.zeros((CHUNK, LANE), f32)
        dcsl_row = jnp.zeros((1, LANE), f32)
        dd_row = jnp.zeros((1, LANE), f32)
        for g in range(SSM_GROUPS):
            bsl = slice(D_SSM + g * SSM_STATE, D_SSM + (g + 1) * SSM_STATE)
            csl = slice(D_SSM + D_BC + g * SSM_STATE, D_SSM + D_BC + (g + 1) * SSM_STATE)
            bg = act_ref[:, bsl]
            cg = act_ref[:, csl]
            cb = _bdot_nt(cg, bg)
            dcb = jnp.zeros((CHUNK, CHUNK), f32)
            dbg = jnp.zeros((CHUNK, SSM_STATE), f32)
            dcg = jnp.zeros((CHUNK, SSM_STATE), f32)
            for rr in range(HEADS_PER_GROUP):
                hd = g * HEADS_PER_GROUP + rr
                hs = slice(hd * SSM_HEAD_DIM, (hd + 1) * SSM_HEAD_DIM)
                hl = slice(hd, hd + 1)
                x_h = act_ref[:, hs]
                dt_h = dtp[:, hl]
                e_h = e_cs[:, hl]
                dte_h = dte[:, hl]
                ecl_h = ecl[:, hl]
                xdt = x_h * dt_h
                lm = _seg_decay(tri, strict, d_a[:, hl], row, col)
                m = cb * lm
                prev = st_ref[0, hd]
                dy = dybuf[:, hs]
                dh = dstate[hd]
                dd_row = dd_row + jnp.where(lane1 == hd, jnp.sum(jnp.sum(dy * x_h, axis=1, keepdims=True), axis=0, keepdims=True), 0.0)
                dx = dy * dsk[:, hl]
                gmat = _bdot(cg, prev)
                dg = dy * e_h
                dcg = dcg + _bdot_nt(dg, prev)
                dprev = _bdot_tn(cg, dg)
                dcs_h = jnp.sum(dy * gmat, axis=1, keepdims=True) * e_h
                dm = _bdot_nt(dy, xdt)
                dxdt = _bdot_tn(m, dy)
                dcb = dcb + dm * lm
                dseg = dm * m
                dda_h = jnp.sum(_hdot(upper, dseg) * strict, axis=1, keepdims=True)
                wmat = xdt * dte_h
                dbg = dbg + _bdot_nt(wmat, dh)
                dw = _bdot(bg, dh)
                dxdt = dxdt + dw * dte_h
                ddte = jnp.sum(dw * xdt, axis=1, keepdims=True) * dte_h
                dcs_h = dcs_h - ddte
                dcsl = jnp.sum(ddte, axis=0, keepdims=True)
                dcsl = dcsl + jnp.sum(jnp.sum(dh * prev, axis=1, keepdims=True), axis=0, keepdims=True) * ecl_h
                dstate[hd] = dprev + dh * ecl_h
                dact_ref[:, hs] = dx + dxdt * dt_h
                ddt_h = jnp.sum(dxdt * x_h, axis=1, keepdims=True)
                ddt_mat = jnp.where(lane == hd, ddt_h, ddt_mat)
                dcs_mat = jnp.where(lane == hd, dcs_h, dcs_mat)
                dda_mat = jnp.where(lane == hd, dda_h, dda_mat)
                dcsl_row = jnp.where(lane1 == hd, dcsl, dcsl_row)
            dact_ref[:, bsl] = dbg + _bdot_tn(dcb, cg)
            dact_ref[:, csl] = dcg + _bdot(dcb, bg)
        rowl = lax.broadcasted_iota(jnp.int32, (CHUNK, LANE), 0)
        dcs_mat = dcs_mat + jnp.where(rowl == CHUNK - 1, dcsl_row, 0.0)
        dda = dda_mat + _hdot(upper, dcs_mat)
        ddt_mat = ddt_mat + dda * a_row
        da_row = jnp.sum(dda * dtp, axis=0, keepdims=True)
        ddt_raw = ddt_mat * _sigmoid(dt_ref[...] + dtb_ref[...])
        ddt_ref[...] = ddt_raw
        dpar_ref[0:1, :] += jnp.sum(ddt_raw, axis=0, keepdims=True)
        dpar_ref[1:2, :] += da_row * a_row
        dpar_ref[2:3, :] += dd_row

    blk = lambda i: N_CHUNKS - 1 - i
    return pl.pallas_call(
        body, name="ssd_bwd", grid=(N_CHUNKS,),
        in_specs=[pl.BlockSpec((CHUNK, D_CONV), lambda i: (blk(i), 0)), pl.BlockSpec((CHUNK, D_SSM), lambda i: (blk(i), zcol)),
                  pl.BlockSpec((CHUNK, LANE), lambda i: (blk(i), dtcol)), pl.BlockSpec((CHUNK, D_SSM), lambda i: (blk(i), 0)),
                  pl.BlockSpec((1, SSM_HEADS, SSM_STATE, SSM_HEAD_DIM), lambda i: (blk(i), 0, 0, 0)),
                  pl.BlockSpec((CHUNK, D_SSM), lambda i: (blk(i), 0)),
                  _full((1, LANE)), _full((1, LANE)), _full((1, LANE)), _full((1, D_SSM))],
        out_specs=[pl.BlockSpec((CHUNK, D_CONV), lambda i: (blk(i), 0)), pl.BlockSpec((CHUNK, LANE), lambda i: (blk(i), 0)),
                   pl.BlockSpec((CHUNK, D_SSM), lambda i: (blk(i), 0)), _full((1, D_SSM)), _full((8, LANE))],
        out_shape=[jax.ShapeDtypeStruct((SEQ, D_CONV), f32), jax.ShapeDtypeStruct((SEQ, LANE), f32),
                   jax.ShapeDtypeStruct((SEQ, D_SSM), f32), jax.ShapeDtypeStruct((1, D_SSM), f32),
                   jax.ShapeDtypeStruct((8, LANE), f32)],
        scratch_shapes=[pltpu.VMEM((SSM_HEADS, SSM_STATE, SSM_HEAD_DIM), f32), pltpu.VMEM((CHUNK, D_SSM), f32)],
        compiler_params=_params("arbitrary"),
    )(act, proj, proj, ypre, states, d_out, dt_bias, a_log, d_skip, norm_g)


def out_fwd(x, attn, ssm, w_out, tm=512):
    def body(x_ref, a_ref, s_ref, w_ref, o_ref):
        o_ref[...] = x_ref[...] + _bdot(a_ref[...], w_ref[:D_ATTN, :]) + _bdot(s_ref[...], w_ref[D_ATTN:, :])

    tok = lambda w_: pl.BlockSpec((tm, w_), lambda i: (i, 0))
    return pl.pallas_call(
        body, name="out_fwd", grid=(SEQ // tm,),
        in_specs=[tok(D_MODEL), tok(D_ATTN), tok(D_SSM), _full((D_MODEL, D_MODEL))],
        out_specs=tok(D_MODEL), out_shape=jax.ShapeDtypeStruct((SEQ, D_MODEL), f32),
        compiler_params=_params("arbitrary"),
    )(x, attn, ssm, w_out)


def out_bwd(dx1, attn, ssm, w_out, tm=512):
    def body(d_ref, a_ref, s_ref, w_ref, da_ref, ds_ref, dw_ref):
        @pl.when(pl.program_id(0) == 0)
        def _():
            dw_ref[...] = jnp.zeros_like(dw_ref)

        d = d_ref[...].astype(bf16)
        dcat = _bdot_nt(d, w_ref[...])
        da_ref[...] = dcat[:, :D_ATTN]
        ds_ref[...] = dcat[:, D_ATTN:]
        dw_ref[:D_ATTN, :] += _bdot_tn(a_ref[...], d)
        dw_ref[D_ATTN:, :] += _bdot_tn(s_ref[...], d)

    tok = lambda w_: pl.BlockSpec((tm, w_), lambda i: (i, 0))
    return pl.pallas_call(
        body, name="out_bwd", grid=(SEQ // tm,),
        in_specs=[tok(D_MODEL), tok(D_ATTN), tok(D_SSM), _full((D_MODEL, D_MODEL))],
        out_specs=[tok(D_ATTN), tok(D_SSM), _full((D_MODEL, D_MODEL))],
        out_shape=[jax.ShapeDtypeStruct((SEQ, D_ATTN), f32), jax.ShapeDtypeStruct((SEQ, D_SSM), f32),
                   jax.ShapeDtypeStruct((D_MODEL, D_MODEL), f32)],
        compiler_params=_params("arbitrary"),
    )(dx1, attn, ssm, w_out)


def mlp_fwd(x1, g, w_up, w_down, tm=512):
    def body(x_ref, g_ref, wu_ref, wd_ref, o_ref, u_ref, h_scr):
        j = pl.program_id(1)

        @pl.when(j == 0)
        def _():
            xv = x_ref[...]
            h_scr[...] = (xv * _rms(xv) * g_ref[...]).astype(bf16)
            o_ref[...] = xv

        u = jnp.dot(h_scr[...], wu_ref[...], preferred_element_type=f32)
        u_ref[...] = u
        a = jnp.square(jnp.maximum(u, 0.0))
        o_ref[...] += _bdot(a, wd_ref[...])

    return pl.pallas_call(
        body, name="mlp_fwd", grid=(SEQ // tm, N_CHIPS),
        in_specs=[pl.BlockSpec((tm, D_MODEL), lambda i, j: (i, 0)), _full((1, D_MODEL)),
                  pl.BlockSpec((None, D_MODEL, FF_TILE), lambda i, j: (j, 0, 0)),
                  pl.BlockSpec((None, FF_TILE, D_MODEL), lambda i, j: (j, 0, 0))],
        out_specs=[pl.BlockSpec((tm, D_MODEL), lambda i, j: (i, 0)), pl.BlockSpec((tm, FF_TILE), lambda i, j: (i, j))],
        out_shape=[jax.ShapeDtypeStruct((SEQ, D_MODEL), f32), jax.ShapeDtypeStruct((SEQ, D_FF), f32)],
        scratch_shapes=[pltpu.VMEM((tm, D_MODEL), bf16)],
        compiler_params=_params("arbitrary", "arbitrary"),
    )(x1, g, w_up, w_down)


def mlp_bwd_data(dx2, u, x1, g, w_up, w_down, tm=512):
    def body(d_ref, u_ref, x_ref, g_ref, wu_ref, wd_ref, dx_ref, du_ref, dg_ref, dh_scr):
        i, j = pl.program_id(0), pl.program_id(1)

        @pl.when(jnp.logical_and(i == 0, j == 0))
        def _():
            dg_ref[...] = jnp.zeros_like(dg_ref)

        @pl.when(j == 0)
        def _():
            dh_scr[...] = jnp.zeros_like(dh_scr)

        da = _bdot_nt(d_ref[...], wd_ref[...])
        du = (da * (2.0 * jnp.maximum(u_ref[...], 0.0))).astype(bf16)
        du_ref[...] = du
        dh_scr[...] += _bdot_nt(du, wu_ref[...])

        @pl.when(j == N_CHIPS - 1)
        def _():
            xv = x_ref[...]
            r = _rms(xv)
            xhat = xv * r
            dh = dh_scr[...]
            dg_ref[...] += jnp.sum(dh * xhat, axis=0, keepdims=True)
            dx_ref[...] = d_ref[...] + _rms_bwd(dh, xhat, r, g_ref[...])

    return pl.pallas_call(
        body, name="mlp_bwd_data", grid=(SEQ // tm, N_CHIPS),
        in_specs=[pl.BlockSpec((tm, D_MODEL), lambda i, j: (i, 0)), pl.BlockSpec((tm, FF_TILE), lambda i, j: (i, j)),
                  pl.BlockSpec((tm, D_MODEL), lambda i, j: (i, 0)), _full((1, D_MODEL)),
                  pl.BlockSpec((None, D_MODEL, FF_TILE), lambda i, j: (j, 0, 0)),
                  pl.BlockSpec((None, FF_TILE, D_MODEL), lambda i, j: (j, 0, 0))],
        out_specs=[pl.BlockSpec((tm, D_MODEL), lambda i, j: (i, 0)), pl.BlockSpec((tm, FF_TILE), lambda i, j: (i, j)),
                   _full((1, D_MODEL))],
        out_shape=[jax.ShapeDtypeStruct((SEQ, D_MODEL), f32), jax.ShapeDtypeStruct((SEQ, D_FF), bf16),
                   jax.ShapeDtypeStruct((1, D_MODEL), f32)],
        scratch_shapes=[pltpu.VMEM((tm, D_MODEL), f32)],
        compiler_params=_params("arbitrary", "arbitrary"),
    )(dx2, u, x1, g, w_up, w_down)


def mlp_bwd_weights(dx2, u, du, x1, g, tm=512):
    def body(d_ref, u_ref, du_ref, x_ref, g_ref, dwu_ref, dwd_ref):
        @pl.when(pl.program_id(1) == 0)
        def _():
            dwu_ref[...] = jnp.zeros_like(dwu_ref)
            dwd_ref[...] = jnp.zeros_like(dwd_ref)

        xv = x_ref[...]
        h = xv * _rms(xv) * g_ref[...]
        dwu_ref[...] += _bdot_tn(h, du_ref[...])
        a = jnp.square(jnp.maximum(u_ref[...], 0.0))
        dwd_ref[...] += _bdot_tn(a, d_ref[...])

    return pl.pallas_call(
        body, name="mlp_bwd_weights", grid=(N_CHIPS, SEQ // tm),
        in_specs=[pl.BlockSpec((tm, D_MODEL), lambda j, i: (i, 0)), pl.BlockSpec((tm, FF_TILE), lambda j, i: (i, j)),
                  pl.BlockSpec((tm, FF_TILE), lambda j, i: (i, j)), pl.BlockSpec((tm, D_MODEL), lambda j, i: (i, 0)),
                  _full((1, D_MODEL))],
        out_specs=[pl.BlockSpec((None, D_MODEL, FF_TILE), lambda j, i: (j, 0, 0)),
                   pl.BlockSpec((None, FF_TILE, D_MODEL), lambda j, i: (j, 0, 0))],
        out_shape=[jax.ShapeDtypeStruct((N_CHIPS, D_MODEL, FF_TILE), f32), jax.ShapeDtypeStruct((N_CHIPS, FF_TILE, D_MODEL), f32)],
        compiler_params=_params("arbitrary", "arbitrary"),
    )(dx2, u, du, x1, g)


def loss_head(y, target, tm=512):
    def body(y_ref, t_ref, dy_ref, l_ref):
        @pl.when(pl.program_id(0) == 0)
        def _():
            l_ref[...] = jnp.zeros_like(l_ref)

        d = y_ref[...] - t_ref[...]
        dy_ref[...] = d * (1.0 / D_MODEL)
        part = jnp.sum(jnp.mean(d * d, axis=-1, keepdims=True), axis=0, keepdims=True)
        l_ref[...] += 0.5 * part

    tok = pl.BlockSpec((tm, D_MODEL), lambda i: (i, 0))
    return pl.pallas_call(
        body, name="loss_head", grid=(SEQ // tm,), in_specs=[tok, tok], out_specs=[tok, _full((1, 1))],
        out_shape=[jax.ShapeDtypeStruct((SEQ, D_MODEL), f32), jax.ShapeDtypeStruct((1, 1), f32)],
        compiler_params=_params("arbitrary"),
    )(y, target)


def _pad_lane(v):
    return jnp.pad(v, (0, LANE - v.shape[0]))[None, :]


def local_step(x, target, w, prov):
    bucket = jnp.asarray(_bucket_table())
    bias = bias_build(w["rel_bias"], bucket)
    saved = []
    for l in range(DEPTH):
        g_mix = w["mix_norm_g"][l][None, :] + prov.stage(("begin", l), x)
        w_in = prov.w_in(l, x)
        proj = in_fwd(x, g_mix, w_in)
        qg, kg = w["q_gain"][l][None, :], w["k_gain"][l][None, :]
        attn = attn_fwd(proj, qg, kg, w["sinks"][l], bias)
        conv_b = w["conv_b"][l][None, :]
        act = conv_fwd(proj, w["conv_w"][l], conv_b)
        dtb = _pad_lane(w["dt_bias"][l]) + prov.stage(("mid", l), act)
        alog, dsk = _pad_lane(w["a_log"][l]), _pad_lane(w["d_skip"][l])
        ng = w["ssm_norm_g"][l][None, :]
        ssm, ypre, states = ssd_fwd(act, proj, dtb, alog, dsk, ng)
        tok = prov.stage(("pre_out", l), ssm)
        w_out = prov.w_out(l, ssm)
        x1 = out_fwd(x, attn, ssm, w_out)
        g_mlp = w["mlp_norm_g"][l][None, :] + (tok + prov.stage(("pre_mlp", l), x1))
        w_up, w_down = prov.mlp(l, x1)
        x2, u = mlp_fwd(x1, g_mlp, w_up, w_down)
        saved.append(dict(x=x, proj=proj, attn=attn, act=act, ssm=ssm, ypre=ypre, states=states, x1=x1, u=u,
                          g_mix=g_mix, qg=qg, kg=kg, conv_b=conv_b, dtb=dtb, alog=alog, dsk=dsk, ng=ng, g_mlp=g_mlp,
                          w_in=w_in, w_out=w_out, w_up=w_up, w_down=w_down))
        x = x2
    dx, loss = loss_head(x, target)
    grads = [None] * DEPTH
    dbands = [None] * DEPTH
    tok = 0.0
    for l in reversed(range(DEPTH)):
        s = saved[l]
        g_mlp = s["g_mlp"] + tok
        dx1, du, dg_mlp = mlp_bwd_data(dx, s["u"], s["x1"], g_mlp, s["w_up"], s["w_down"])
        dw_up, dw_down = mlp_bwd_weights(dx, s["u"], du, s["x1"], g_mlp)
        tok = prov.grads(("mlp", l), dict(w_up=dw_up, w_down=dw_down), dx1)
        dattn, dssm, dw_out = out_bwd(dx1, s["attn"], s["ssm"], s["w_out"])
        dact, ddt, dz, dng, dpar = ssd_bwd(s["act"], s["proj"], s["ypre"], s["states"], dssm, s["dtb"] + tok, s["alog"],
                                           s["dsk"], s["ng"])
        conv_b = s["conv_b"] + prov.stage(("bwd_mid", l), dact)
        dxbc, dconv_w, dconv_b = conv_bwd(s["proj"], dact, w["conv_w"][l], conv_b)
        dq, dk, dv, dband, dsink, dqg, dkg = attn_bwd(s["proj"], dattn, s["qg"], s["kg"], w["sinks"][l], bias)
        dx, dw_in, dg_mix = in_bwd(dq, dz, dxbc, dk, dv, ddt, s["x"], s["g_mix"], s["w_in"], dx1)
        tok = prov.grads(("mix", l), dict(w_in=dw_in, w_out=dw_out), dx)
        dbands[l] = dband
        grads[l] = dict(mix_norm_g=dg_mix[0], q_gain=dqg[0], k_gain=dkg[0], sinks=dsink[0, :N_Q_HEADS],
                        conv_w=dconv_w, conv_b=dconv_b[0], dt_bias=dpar[0, :SSM_HEADS], a_log=dpar[1, :SSM_HEADS],
                        d_skip=dpar[2, :SSM_HEADS], ssm_norm_g=dng[0], mlp_norm_g=dg_mlp[0])
    out = {k: jnp.stack([grads[l][k] for l in range(DEPTH)]) for k in grads[0]}
    out["rel_bias"] = bias_bwd(dbands[0], dbands[1], bucket)[:, :N_Q_HEADS]
    return loss, dx, out


MESH = pl.DeviceIdType.MESH
HBM = pl.BlockSpec(memory_space=pltpu.HBM)
N_PEER_CHIPS = N_CHIPS - 1
N_DEVICES = 8


def _coords():
    return lax.axis_index("x"), lax.axis_index("y"), lax.axis_index("c")


def _peer_chips(x, y):
    return [(1 - x, y), (x, 1 - y), (1 - x, 1 - y)]


def _remote(src, dst, send_sem, recv_sem, device):
    return pltpu.make_async_remote_copy(src_ref=src, dst_ref=dst, send_sem=send_sem, recv_sem=recv_sem,
                                        device_id=device, device_id_type=MESH)


SEM = pl.BlockSpec(memory_space=pltpu.SEMAPHORE)
ANY = pl.BlockSpec(memory_space=pl.ANY)
DATAFLOW = pltpu.SideEffectType.DATAFLOW_SIDE_EFFECTING


def _gather_copies(kind, src_refs, land_refs, ssem, rsem):
    x, y, c = _coords()
    k_me = 2 * x + y
    cps = []
    for p, land in enumerate(land_refs):
        hr = land.shape[1] // 2
        rows = pl.ds(c * hr, hr)
        for j, chip in enumerate(_peer_chips(x, y)):
            i = 3 * p + j
            if kind == "ici":
                cps.append(_remote(src_refs[p].at[rows, :], land.at[k_me, rows, :], ssem.at[i], rsem.at[i], (*chip, c)))
            else:
                got = land.at[2 * chip[0] + chip[1], rows, :]
                cps.append(_remote(got, got, ssem.at[i], rsem.at[i], (x, y, 1 - c)))
    return cps


def gather_now(srcs, conv):
    n = len(srcs)

    def body(*refs):
        src_refs, conv_ref = refs[:n], refs[n]
        lands, gconv = refs[n + 1:2 * n + 1], refs[2 * n + 1]
        ssem, rsem, fsem, frsem, csem, crsem = refs[2 * n + 2:]
        x, y, c = _coords()
        k_me = 2 * x + y
        chips = _peer_chips(x, y)
        ici = _gather_copies("ici", src_refs, lands, ssem, rsem)
        relay = _gather_copies("relay", src_refs, lands, fsem, frsem)
        conv_cps = [_remote(conv_ref, gconv.at[k_me], csem.at[j], crsem.at[j], (*chip, c)) for j, chip in enumerate(chips)]
        for cp in ici + conv_cps:
            cp.start()
        for cp, fw in zip(ici, relay):
            cp.wait_recv()
            fw.start()
        for cp in conv_cps + relay:
            cp.wait_recv()
        for cp in ici + relay + conv_cps:
            cp.wait_send()

    out_shape = [jax.ShapeDtypeStruct((N_CHIPS,) + s.shape, s.dtype) for s in srcs]
    out_shape.append(jax.ShapeDtypeStruct((N_CHIPS,) + conv.shape, conv.dtype))
    sems = lambda k: pltpu.SemaphoreType.DMA((k,))
    return pl.pallas_call(
        body, name="gather_now", out_shape=out_shape, in_specs=[HBM] * (n + 1), out_specs=[HBM] * (n + 1),
        scratch_shapes=[sems(3 * n), sems(3 * n), sems(3 * n), sems(3 * n), sems(N_PEER_CHIPS), sems(N_PEER_CHIPS)],
    )(*srcs, conv)


def _gather_maker(kind, n_src):
    def make(refs, ssem, rsem):
        cps = _gather_copies(kind, refs[:n_src], refs[n_src:], ssem, rsem)
        return cps, cps
    return make


def _scatter_maker(n):
    def make(refs, ssem, rsem):
        x, y, c = _coords()
        k_me = 2 * x + y
        sends, arrivals = [], []
        for p in range(n):
            src, land = refs[p], refs[n + p]
            sends.append(_remote(src.at[k_me, 1 - c], land.at[0], ssem.at[7 * p], rsem.at[7 * p], (x, y, 1 - c)))
            for j, chip in enumerate(_peer_chips(x, y)):
                for cc in range(2):
                    sends.append(_remote(src.at[2 * chip[0] + chip[1], cc], land.at[1 + 2 * j + c],
                                         ssem.at[7 * p + 1 + 2 * j + cc], rsem.at[7 * p + 1 + 2 * j + c], (*chip, cc)))
            for s in range(7):
                arrivals.append(_remote(land.at[s], land.at[s], ssem.at[7 * p + s], rsem.at[7 * p + s], (x, y, 1 - c)))
        return sends, arrivals
    return make


def _share_maker(n):
    def make(refs, ssem, rsem):
        x, y, c = _coords()
        sends = [_remote(refs[p].at[c], refs[p].at[c], ssem.at[p], rsem.at[p], (x, y, 1 - c)) for p in range(n)]
        arrivals = [_remote(refs[p].at[1 - c], refs[p].at[1 - c], ssem.at[p], rsem.at[p], (x, y, 1 - c)) for p in range(n)]
        return sends, arrivals
    return make


def split_start(name, make, n_sems, operands):
    n = len(operands)

    def body(*refs):
        ssem, rsem, token = refs[n], refs[n + 1], refs[-1]
        for cp in make(refs[:n], ssem, rsem)[0]:
            cp.start()
        token[...] = jnp.zeros_like(token)

    ops = [pltpu.with_memory_space_constraint(a, pltpu.HBM) for a in operands]
    outs = pl.pallas_call(
        body, name=name,
        out_shape=(pltpu.SemaphoreType.DMA((n_sems,)), pltpu.SemaphoreType.DMA((n_sems,)),
                   *[pltpu.HBM(a.shape, a.dtype) for a in ops], jax.ShapeDtypeStruct((8, LANE), f32)),
        in_specs=[HBM] * n, out_specs=(SEM, SEM, *[HBM] * n, pl.BlockSpec(memory_space=pltpu.VMEM)),
        input_output_aliases={i: 2 + i for i in range(n)},
        compiler_params=pltpu.CompilerParams(has_side_effects=DATAFLOW),
    )(*ops)
    return dict(name=name, make=make, ssem=outs[0], rsem=outs[1], operands=outs[2:2 + n], token=outs[-1][0, 0])


def split_wait(handle, after):
    n = len(handle["operands"])

    def body(*refs):
        sends, arrivals = handle["make"](refs[:n], refs[n], refs[n + 1])
        for cp in sends:
            cp.wait_send()
        for cp in arrivals:
            cp.wait_recv()

    outs = pl.pallas_call(
        body, name=handle["name"].replace("start", "wait"),
        out_shape=tuple(pltpu.HBM(a.shape, a.dtype) for a in handle["operands"]),
        in_specs=[HBM] * n + [SEM, SEM, ANY], out_specs=tuple([HBM] * n),
        input_output_aliases={i: i for i in range(n)},
        compiler_params=pltpu.CompilerParams(has_side_effects=DATAFLOW),
    )(*handle["operands"], handle["ssem"], handle["rsem"], after)
    return list(outs)


def piece_sum(g, recv, kc_arr):
    _, _, rb, cc = g.shape
    tr = min(256, rb)

    def body(kc_ref, g_ref, r_ref, o_ref):
        acc = g_ref[...]
        for s in range(7):
            acc = acc + r_ref[s].astype(f32)
        o_ref[...] = acc

    return pl.pallas_call(
        body, name="piece_sum",
        grid_spec=pltpu.PrefetchScalarGridSpec(
            num_scalar_prefetch=1, grid=(rb // tr,),
            in_specs=[pl.BlockSpec((None, None, tr, cc), lambda r, kc: (kc[0], kc[1], r, 0)),
                      pl.BlockSpec((7, tr, cc), lambda r, kc: (0, r, 0))],
            out_specs=pl.BlockSpec((None, tr, cc), lambda r, kc: (kc[1], r, 0))),
        out_shape=jax.ShapeDtypeStruct((2, rb, cc), f32),
        compiler_params=_params("arbitrary"),
    )(kc_arr, g, recv)


def small_all_reduce(vec):
    def body(v_ref, o_ref, gat, ssem, rsem):
        x, y, c = _coords()
        me = 4 * x + 2 * y + c
        gat[me] = v_ref[...]
        sends = []
        for t in range(1, N_DEVICES):
            peer = (x ^ (t >> 2), y ^ ((t >> 1) & 1), c ^ (t & 1))
            cp = _remote(v_ref, gat.at[me], ssem.at[t - 1], rsem.at[t - 1], peer)
            cp.start()
            sends.append(cp)
        for t in range(1, N_DEVICES):
            peer = (x ^ (t >> 2), y ^ ((t >> 1) & 1), c ^ (t & 1))
            slot = gat.at[4 * peer[0] + 2 * peer[1] + peer[2]]
            _remote(slot, slot, ssem.at[t - 1], rsem.at[t - 1], peer).wait_recv()
        for cp in sends:
            cp.wait_send()
        acc = gat[0]
        for d in range(1, N_DEVICES):
            acc = acc + gat[d]
        o_ref[...] = acc

    return pl.pallas_call(
        body, name="small_all_reduce", out_shape=jax.ShapeDtypeStruct(vec.shape, vec.dtype),
        in_specs=[pl.BlockSpec(memory_space=pltpu.VMEM)], out_specs=pl.BlockSpec(memory_space=pltpu.VMEM),
        scratch_shapes=[pltpu.VMEM((N_DEVICES,) + vec.shape, vec.dtype), pltpu.SemaphoreType.DMA((N_DEVICES - 1,)),
                        pltpu.SemaphoreType.DMA((N_DEVICES - 1,))],
    )(vec)


def _adamw_math(w, g, m, v):
    m_new = ADAM_B1 * m + (1.0 - ADAM_B1) * g
    v_new = ADAM_B2 * v + (1.0 - ADAM_B2) * jnp.square(g)
    m_hat = m_new / (1.0 - ADAM_B1 ** ADAM_STEP)
    v_hat = v_new / (1.0 - ADAM_B2 ** ADAM_STEP)
    delta = -ADAM_LR * (m_hat / (jnp.sqrt(v_hat) + ADAM_EPS) + ADAM_WD * w)
    return delta, m_new, v_new


def adamw_shard(w, g0, g1, m, v):
    depth, rows, cols = w.shape
    half = rows // 2
    tr = min(256, half)
    nr = half // tr

    def body(w_ref, g0_ref, g1_ref, m_ref, v_ref, go_ref, d_ref, nm_ref, nv_ref):
        gv = jnp.where(pl.program_id(0) == 0, g0_ref[...], g1_ref[...])
        go_ref[...] = gv
        d_ref[...], nm_ref[...], nv_ref[...] = _adamw_math(w_ref[...], gv, m_ref[...], v_ref[...])

    spec = pl.BlockSpec((None, tr, cols), lambda l, h, r: (l, h * nr + r, 0))
    g0spec = pl.BlockSpec((None, tr, cols), lambda l, h, r: (jnp.where(l == 0, h, 1), jnp.where(l == 0, r, nr - 1), 0))
    g1spec = pl.BlockSpec((None, tr, cols), lambda l, h, r: (jnp.where(l == 1, h, 0), jnp.where(l == 1, r, 0), 0))
    return pl.pallas_call(
        body, name="adamw_shard", grid=(depth, 2, nr), in_specs=[spec, g0spec, g1spec, spec, spec], out_specs=[spec] * 4,
        out_shape=[jax.ShapeDtypeStruct(w.shape, f32)] * 4,
        compiler_params=_params("arbitrary", "arbitrary", "arbitrary"),
    )(w, g0, g1, m, v)


def adamw_small(w, g, m, v):
    def body(w_ref, g_ref, m_ref, v_ref, d_ref, nm_ref, nv_ref):
        d_ref[...], nm_ref[...], nv_ref[...] = _adamw_math(w_ref[...], g_ref[...], m_ref[...], v_ref[...])

    return pl.pallas_call(
        body, name="adamw_small", out_shape=[jax.ShapeDtypeStruct(w.shape, f32)] * 3,
    )(w, g, m, v)


WEIGHTS = ("mix_norm_g", "w_in", "q_gain", "k_gain", "sinks", "rel_bias", "conv_w", "conv_b", "dt_bias", "a_log", "d_skip",
           "ssm_norm_g", "w_out", "mlp_norm_g", "w_up", "w_down")
BIG = ("w_in", "w_out", "w_up", "w_down")
SMALL = tuple(n for n in WEIGHTS if n not in BIG)
PACK_COLS = 1024
PACK_ROWS = 16


def _pack(named):
    flat = jnp.concatenate([named[n].reshape(-1) for n in SMALL])
    return jnp.pad(flat, (0, PACK_ROWS * PACK_COLS - flat.shape[0])).reshape(PACK_ROWS, PACK_COLS)


def _unpack(buf, shapes):
    flat = buf.reshape(-1)
    out, at = {}, 0
    for n in SMALL:
        size = int(np.prod(shapes[n]))
        out[n] = flat[at:at + size].reshape(shapes[n])
        at += size
    return out


class _Exchange:
    GROUPS = {"A": (("w_up", 0), ("w_down", 0)), "B": (("w_in", 1), ("w_out", 1)), "C": (("w_up", 1), ("w_down", 1))}
    RELAY_AT = {("pre_out", 0): "A", ("pre_mlp", 0): "B", ("pre_out", 1): "C"}
    NEXT_GROUP = {"A": "B", "B": "C"}

    def __init__(self, wts, k_me, kc_arr):
        self.wts, self.k_me, self.kc_arr = wts, k_me, kc_arr
        self.own = {(n, l): wts[n][l].astype(bf16) for n in BIG for l in range(DEPTH)}
        now = gather_now([self.own["w_in", 0], self.own["w_out", 0]], wts["conv_w"])
        self.ready = {("w_in", 0): self._fill(now[0], self.own["w_in", 0]),
                      ("w_out", 0): self._fill(now[1], self.own["w_out", 0])}
        conv = self._fill(now[2], wts["conv_w"])
        self.conv_w = jnp.transpose(conv, (1, 2, 0, 3)).reshape(DEPTH, CONV_WIDTH, D_CONV)
        self.ici, self.relay = {}, {}
        self.gview, self.scatter, self.share, self.reduced = {}, [], [], {}
        self._start_ici("A")

    def _start_ici(self, g):
        srcs = [self.own[p] for p in self.GROUPS[g]]
        lands = [lax.empty((N_CHIPS,) + s.shape, s.dtype) for s in srcs]
        self.ici[g] = split_start("gather%s_ici_start" % g, _gather_maker("ici", len(srcs)), 3 * len(srcs), srcs + lands)
        return self.ici[g]["token"]

    def _fill(self, land, own):
        return lax.dynamic_update_slice(land, own[None], (self.k_me,) + (0,) * own.ndim)

    def stage(self, name, after):
        if name == ("begin", 0):
            return self.ici["A"]["token"]
        g = self.RELAY_AT.get(name)
        if g is None:
            return 0.0
        n = len(self.GROUPS[g])
        lands = split_wait(self.ici[g], after)[n:]
        self.relay[g] = split_start("gather%s_relay_start" % g, _gather_maker("relay", 0), 3 * n, lands)
        tok = self.relay[g]["token"]
        if g in self.NEXT_GROUP:
            tok = tok + self._start_ici(self.NEXT_GROUP[g])
        return tok

    def _get(self, piece, after):
        if piece not in self.ready:
            g = [k for k, pieces in self.GROUPS.items() if piece in pieces][0]
            lands = split_wait(self.relay[g], after)
            for p, land in zip(self.GROUPS[g], lands):
                self.ready[p] = self._fill(land, self.own[p])
        return self.ready[piece]

    def w_in(self, l, after):
        g = self._get(("w_in", l), after)
        return _to_aligned(jnp.transpose(g, (1, 0, 2)).reshape(D_MODEL, D_IN))

    def w_out(self, l, after):
        return self._get(("w_out", l), after).reshape(D_MODEL, D_MODEL)

    def mlp(self, l, after):
        return self._get(("w_up", l), after), self._get(("w_down", l), after)

    def _view(self, n, g):
        if n == "w_in":
            g = jnp.transpose(_from_aligned(g).reshape(D_MODEL, N_CHIPS, D_IN // N_CHIPS), (1, 0, 2))
        _, rows, cols = self.wts[n].shape
        return g.reshape(N_CHIPS, 2, rows // 2, cols)

    def grads(self, name, arrays, after):
        pieces = [(n, name[1]) for n in arrays]
        views = [self._view(n, g) for n, g in arrays.items()]
        self.gview.update(zip(pieces, views))
        lands = [lax.empty((7,) + v.shape[2:], bf16) for v in views]
        h = split_start("scatter_%s%d_start" % name, _scatter_maker(len(views)), 7 * len(views),
                        [v.astype(bf16) for v in views] + lands)
        tok = h["token"] + self._advance(after)
        self.scatter.append((pieces, h))
        return tok

    def _take_share(self, after):
        pieces, h = self.share.pop(0)
        self.reduced.update(zip(pieces, split_wait(h, after)))

    def _take_scatter(self, after):
        pieces, h = self.scatter.pop(0)
        lands = split_wait(h, after)[len(pieces):]
        sums = [piece_sum(self.gview[p], land, self.kc_arr) for p, land in zip(pieces, lands)]
        hs = split_start(h["name"].replace("scatter", "share"), _share_maker(len(sums)), len(sums), sums)
        self.share.append((pieces, hs))
        return hs["token"]

    def _advance(self, after):
        if self.share:
            self._take_share(after)
        return self._take_scatter(after) if self.scatter else 0.0

    def reduced_grads(self, names, after):
        want = [(n, l) for n in names for l in range(DEPTH)]
        while not all(p in self.reduced for p in want):
            if any(p in pieces for p in want for pieces, _ in self.share):
                self._take_share(after)
            else:
                self._take_scatter(after)
        return {n: [self.reduced[n, l] for l in range(DEPTH)] for n in names}


def kernel(x, mix_norm_g, w_in, q_gain, k_gain, sinks, rel_bias, conv_w, conv_b, dt_bias, a_log, d_skip, ssm_norm_g, w_out, mlp_norm_g, w_up, w_down, loss_target, m_mix_norm_g, m_w_in, m_q_gain, m_k_gain, m_sinks, m_rel_bias, m_conv_w, m_conv_b, m_dt_bias, m_a_log, m_d_skip, m_ssm_norm_g, m_w_out, m_mlp_norm_g, m_w_up, m_w_down, v_mix_norm_g, v_w_in, v_q_gain, v_k_gain, v_sinks, v_rel_bias, v_conv_w, v_conv_b, v_dt_bias, v_a_log, v_d_skip, v_ssm_norm_g, v_w_out, v_mlp_norm_g, v_w_up, v_w_down):
    wts = dict(mix_norm_g=mix_norm_g, w_in=w_in, q_gain=q_gain, k_gain=k_gain, sinks=sinks, rel_bias=rel_bias, conv_w=conv_w,
               conv_b=conv_b, dt_bias=dt_bias, a_log=a_log, d_skip=d_skip, ssm_norm_g=ssm_norm_g, w_out=w_out,
               mlp_norm_g=mlp_norm_g, w_up=w_up, w_down=w_down)
    mom = dict(mix_norm_g=m_mix_norm_g, w_in=m_w_in, q_gain=m_q_gain, k_gain=m_k_gain, sinks=m_sinks, rel_bias=m_rel_bias,
               conv_w=m_conv_w, conv_b=m_conv_b, dt_bias=m_dt_bias, a_log=m_a_log, d_skip=m_d_skip, ssm_norm_g=m_ssm_norm_g,
               w_out=m_w_out, mlp_norm_g=m_mlp_norm_g, w_up=m_w_up, w_down=m_w_down)
    var = dict(mix_norm_g=v_mix_norm_g, w_in=v_w_in, q_gain=v_q_gain, k_gain=v_k_gain, sinks=v_sinks, rel_bias=v_rel_bias,
               conv_w=v_conv_w, conv_b=v_conv_b, dt_bias=v_dt_bias, a_log=v_a_log, d_skip=v_d_skip, ssm_norm_g=v_ssm_norm_g,
               w_out=v_w_out, mlp_norm_g=v_mlp_norm_g, w_up=v_w_up, w_down=v_w_down)
    xi, yi, ci = _coords()
    k_me = 2 * xi + yi
    kc_arr = jnp.stack([k_me, ci]).astype(jnp.int32)

    prov = _Exchange(wts, k_me, kc_arr)
    small_w = {n: wts[n] for n in SMALL}
    small_w["conv_w"] = prov.conv_w
    loss, dx, grads = local_step(x[0], loss_target[0], small_w, prov)
    loss = lax.psum(loss[0, 0], ("x", "y", "c"))

    small_shapes = {n: grads[n].shape for n in SMALL}
    small = _unpack(small_all_reduce(_pack(grads)), small_shapes)
    cols = conv_w.shape[-1]
    small["conv_w"] = lax.dynamic_slice_in_dim(small["conv_w"], k_me * cols, cols, axis=2)
    g_out_d, d_out_d, m_out_d, v_out_d = {}, {}, {}, {}
    shard_shapes = {n: wts[n].shape for n in SMALL}
    d, nm, nv = adamw_small(_pack(wts), _pack(small), _pack(mom), _pack(var))
    for dst, buf in ((d_out_d, d), (m_out_d, nm), (v_out_d, nv)):
        dst.update(_unpack(buf, shard_shapes))
    g_out_d.update(small)

    after = d
    for names in (("w_up", "w_down"), ("w_in", "w_out")):
        for n, (g0, g1) in prov.reduced_grads(names, after).items():
            g_out_d[n], d_out_d[n], m_out_d[n], v_out_d[n] = adamw_shard(wts[n], g0, g1, mom[n], var[n])
            after = d_out_d[n]

    return (loss, dx[None], *[g_out_d[n] for n in WEIGHTS], *[d_out_d[n] for n in WEIGHTS],
            *[m_out_d[n] for n in WEIGHTS], *[v_out_d[n] for n in WEIGHTS])
```

```python
import functools

import numpy as np
import jax
import jax.numpy as jnp
from jax import lax
from jax.experimental import pallas as pl
from jax.experimental.pallas import tpu as pltpu

f32 = jnp.float32
bf16 = jnp.bfloat16

SEQ = 2048
D_MODEL = 1024
DEPTH = 2
HEAD_DIM = 64
N_Q_HEADS = 8
N_KV_HEADS = 2
Q_PER_KV = N_Q_HEADS // N_KV_HEADS
BLOCK = 128
N_BLOCKS = SEQ // BLOCK
N_BUCKETS = 32
MAX_DISTANCE = 128
SSM_HEADS = 8
SSM_HEAD_DIM = 64
SSM_GROUPS = 2
HEADS_PER_GROUP = SSM_HEADS // SSM_GROUPS
SSM_STATE = 128
CONV_WIDTH = 4
CHUNK = 128
N_CHUNKS = SEQ // CHUNK
D_FF = 4 * D_MODEL
D_ATTN = N_Q_HEADS * HEAD_DIM
D_KV = N_KV_HEADS * HEAD_DIM
D_SSM = SSM_HEADS * SSM_HEAD_DIM
D_BC = SSM_GROUPS * SSM_STATE
D_CONV = D_SSM + 2 * D_BC
D_IN = D_ATTN + 2 * D_KV + D_SSM + D_CONV + SSM_HEADS
EPS = 1e-6
NEG = -1e30
N_CHIPS = 4
FF_TILE = D_FF // N_CHIPS

LANE = 128
PW = D_ATTN + D_SSM + D_CONV + 2 * D_KV + LANE
OFF_Q, OFF_Z, OFF_X, OFF_K, OFF_V, OFF_DT = 0, 512, 1024, 2048, 2176, 2304

ADAM_LR = 0.001
ADAM_B1 = 0.9
ADAM_B2 = 0.999
ADAM_EPS = 1e-08
ADAM_WD = 0.01
ADAM_STEP = 10

VMEM_LIMIT = 56 * 1024 * 1024


def _params(*sem):
    return pltpu.CompilerParams(dimension_semantics=tuple(sem), vmem_limit_bytes=VMEM_LIMIT)


def _bdot(a, b):
    return jnp.dot(a.astype(bf16), b.astype(bf16), preferred_element_type=f32)


def _bdot_nt(a, b):
    return lax.dot_general(a.astype(bf16), b.astype(bf16), (((1,), (1,)), ((), ())), preferred_element_type=f32)


def _bdot_tn(a, b):
    return lax.dot_general(a.astype(bf16), b.astype(bf16), (((0,), (0,)), ((), ())), preferred_element_type=f32)


def _hdot(a, b):
    return jnp.dot(a, b, precision=lax.Precision.HIGHEST, preferred_element_type=f32)


def _sigmoid(x):
    return 1.0 / (1.0 + jnp.exp(-x))


def _softplus(x):
    return jnp.maximum(x, 0.0) + jnp.log1p(jnp.exp(-jnp.abs(x)))


def _rms(x):
    return lax.rsqrt(jnp.mean(x * x, axis=-1, keepdims=True) + EPS)


def _rms_bwd(dy, xhat, r, g):
    t = dy * g
    return r * (t - xhat * jnp.mean(t * xhat, axis=-1, keepdims=True))


def _full(shape):
    return pl.BlockSpec(shape, lambda *_: (0,) * len(shape))


def _to_aligned(w):
    q, k, v, z, xbc, dt = jnp.split(w, [512, 640, 768, 1280, 2304], axis=-1)
    pad = jnp.zeros(w.shape[:-1] + (LANE - SSM_HEADS,), w.dtype)
    return jnp.concatenate([q, z, xbc, k, v, dt, pad], axis=-1)


def _from_aligned(w):
    q, z, xbc, k, v, dt = (w[..., OFF_Q:OFF_Z], w[..., OFF_Z:OFF_X], w[..., OFF_X:OFF_K], w[..., OFF_K:OFF_V],
                           w[..., OFF_V:OFF_DT], w[..., OFF_DT:OFF_DT + SSM_HEADS])
    return jnp.concatenate([q, k, v, z, xbc, dt], axis=-1)


def _bucket_table():
    qi = np.arange(BLOCK)[:, None]
    kj = np.arange(2 * BLOCK)[None, :]
    dist = qi + BLOCK - kj
    ok = (dist >= 0) & (dist < 128)
    d = np.clip(dist, 0, None)
    max_exact = N_BUCKETS // 2
    d_f = np.maximum(d, 1).astype(np.float32)
    large = max_exact + (np.log(d_f / np.float32(max_exact)) / np.float32(np.log(MAX_DISTANCE / max_exact))
                         * np.float32(N_BUCKETS - max_exact)).astype(np.int32)
    large = np.minimum(large, N_BUCKETS - 1)
    bucket = np.where(d < max_exact, d, large)
    return np.where(ok, bucket, -1).astype(np.int32)


def bias_build(rel_bias, bucket):
    def body(rel_ref, bkt_ref, o_ref):
        bkt = bkt_ref[...]
        for h in range(N_Q_HEADS):
            acc = jnp.where(bkt < 0, NEG, 0.0).astype(f32)
            for b in range(N_BUCKETS):
                acc = acc + jnp.where(bkt == b, rel_ref[b, h], 0.0)
            o_ref[h] = acc

    return pl.pallas_call(
        body, name="bias_build", out_shape=jax.ShapeDtypeStruct((N_Q_HEADS, BLOCK, 2 * BLOCK), f32),
        in_specs=[pl.BlockSpec(memory_space=pltpu.SMEM), pl.BlockSpec(memory_space=pltpu.VMEM)],
        out_specs=pl.BlockSpec(memory_space=pltpu.VMEM),
    )(rel_bias, bucket)


def bias_bwd(dband0, dband1, bucket):
    def body(d0_ref, d1_ref, bkt_ref, o_ref):
        bkt = bkt_ref[...]
        o_ref[...] = jnp.zeros_like(o_ref)
        for h in range(N_Q_HEADS):
            d = d0_ref[h] + d1_ref[h]
            for b in range(N_BUCKETS):
                part = jnp.sum(jnp.where(bkt == b, d, 0.0), axis=1, keepdims=True)
                o_ref[b:b + 1, h:h + 1] = jnp.sum(part, axis=0, keepdims=True)

    return pl.pallas_call(
        body, name="bias_bwd", out_shape=jax.ShapeDtypeStruct((N_BUCKETS, LANE), f32),
    )(dband0, dband1, bucket)


def in_fwd(x, g, w, tm=256):
    def body(x_ref, g_ref, w_ref, o_ref):
        xv = x_ref[...]
        h = xv * _rms(xv) * g_ref[...]
        o_ref[...] = _bdot(h, w_ref[...])

    return pl.pallas_call(
        body, name="in_fwd", grid=(SEQ // tm,),
        in_specs=[pl.BlockSpec((tm, D_MODEL), lambda i: (i, 0)), _full((1, D_MODEL)), _full((D_MODEL, PW))],
        out_specs=pl.BlockSpec((tm, PW), lambda i: (i, 0)),
        out_shape=jax.ShapeDtypeStruct((SEQ, PW), f32),
        compiler_params=_params("arbitrary"),
    )(x, g, w)


def in_bwd(dq, dz, dxbc, dk, dv, ddt, x, g, w, dres, tm=256):
    def body(dq_ref, dz_ref, dx_ref, dk_ref, dv_ref, ddt_ref, x_ref, g_ref, w_ref, dres_ref, o_ref, dw_ref, dg_ref):
        i = pl.program_id(0)

        @pl.when(i == 0)
        def _():
            dw_ref[...] = jnp.zeros_like(dw_ref)
            dg_ref[...] = jnp.zeros_like(dg_ref)

        dproj = jnp.concatenate([dq_ref[...], dz_ref[...], dx_ref[...], dk_ref[...], dv_ref[...], ddt_ref[...]],
                                axis=-1).astype(bf16)
        xv = x_ref[...]
        r = _rms(xv)
        xhat = xv * r
        gv = g_ref[...]
        h = xhat * gv
        dw_ref[...] += _bdot_tn(h, dproj)
        dh = _bdot_nt(dproj, w_ref[...])
        dg_ref[...] += jnp.sum(dh * xhat, axis=0, keepdims=True)
        o_ref[...] = dres_ref[...] + _rms_bwd(dh, xhat, r, gv)

    tok = lambda w_: pl.BlockSpec((tm, w_), lambda i: (i, 0))
    return pl.pallas_call(
        body, name="in_bwd", grid=(SEQ // tm,),
        in_specs=[tok(D_ATTN), tok(D_SSM), tok(D_CONV), tok(D_KV // 1), tok(D_KV // 1), tok(LANE), tok(D_MODEL),
                  _full((1, D_MODEL)), _full((D_MODEL, PW)), tok(D_MODEL)],
        out_specs=[tok(D_MODEL), _full((D_MODEL, PW)), _full((1, D_MODEL))],
        out_shape=[jax.ShapeDtypeStruct((SEQ, D_MODEL), f32), jax.ShapeDtypeStruct((D_MODEL, PW), f32),
                   jax.ShapeDtypeStruct((1, D_MODEL), f32)],
        compiler_params=_params("arbitrary"),
    )(dq, dz, dxbc, dk, dv, ddt, x, g, w, dres)


def _attn_probs(qn, kn, bias_h, sink, first, col):
    s = _bdot_nt(qn, kn) * (HEAD_DIM ** -0.5) + bias_h
    s = jnp.where(jnp.logical_and(first, col < BLOCK), NEG, s)
    m = jnp.maximum(jnp.max(s, axis=-1, keepdims=True), sink)
    p = jnp.exp(s - m)
    psink = jnp.exp(sink - m)
    inv = 1.0 / (jnp.sum(p, axis=-1, keepdims=True) + psink)
    return p * inv, psink * inv


def attn_fwd(proj, q_gain, k_gain, sinks, bias):
    kcol, vcol = OFF_K // D_KV, OFF_V // D_KV

    def body(q_ref, kc_ref, kp_ref, vc_ref, vp_ref, qg_ref, kg_ref, sink_ref, bias_ref, o_ref):
        n = pl.program_id(0)
        first = n == 0
        col = lax.broadcasted_iota(jnp.int32, (BLOCK, 2 * BLOCK), 1)
        k2 = jnp.concatenate([kp_ref[...], kc_ref[...]], axis=0)
        v2 = jnp.concatenate([vp_ref[...], vc_ref[...]], axis=0)
        qg, kg = qg_ref[...], kg_ref[...]
        for hk in range(N_KV_HEADS):
            kk = k2[:, hk * HEAD_DIM:(hk + 1) * HEAD_DIM]
            kn = (kk * _rms(kk) * kg).astype(bf16)
            vb = v2[:, hk * HEAD_DIM:(hk + 1) * HEAD_DIM].astype(bf16)
            for gq in range(Q_PER_KV):
                h = hk * Q_PER_KV + gq
                qq = q_ref[:, h * HEAD_DIM:(h + 1) * HEAD_DIM]
                qn = qq * _rms(qq) * qg
                p, _ = _attn_probs(qn, kn, bias_ref[h], sink_ref[h], first, col)
                o_ref[:, h * HEAD_DIM:(h + 1) * HEAD_DIM] = _bdot(p, vb)

    prev = lambda n: jnp.maximum(n - 1, 0)
    return pl.pallas_call(
        body, name="attn_fwd", grid=(N_BLOCKS,),
        in_specs=[pl.BlockSpec((BLOCK, D_ATTN), lambda n: (n, 0)),
                  pl.BlockSpec((BLOCK, D_KV), lambda n: (n, kcol)), pl.BlockSpec((BLOCK, D_KV), lambda n: (prev(n), kcol)),
                  pl.BlockSpec((BLOCK, D_KV), lambda n: (n, vcol)), pl.BlockSpec((BLOCK, D_KV), lambda n: (prev(n), vcol)),
                  _full((1, HEAD_DIM)), _full((1, HEAD_DIM)), pl.BlockSpec(memory_space=pltpu.SMEM),
                  _full((N_Q_HEADS, BLOCK, 2 * BLOCK))],
        out_specs=pl.BlockSpec((BLOCK, D_ATTN), lambda n: (n, 0)),
        out_shape=jax.ShapeDtypeStruct((SEQ, D_ATTN), f32),
        compiler_params=_params("arbitrary"),
    )(proj, proj, proj, proj, proj, q_gain, k_gain, sinks, bias)


def attn_bwd(proj, d_out, q_gain, k_gain, sinks, bias):
    kcol, vcol = OFF_K // D_KV, OFF_V // D_KV

    def body(q_ref, kc_ref, kp_ref, vc_ref, vp_ref, do_ref, qg_ref, kg_ref, sink_ref, bias_ref,
             dq_ref, dk_ref, dv_ref, dband_ref, dsink_ref, dqg_ref, dkg_ref, dkn_scr, dv_scr):
        i = pl.program_id(0)
        first = i == N_BLOCKS - 1

        @pl.when(i == 0)
        def _():
            for ref in (dband_ref, dsink_ref, dqg_ref, dkg_ref, dkn_scr, dv_scr):
                ref[...] = jnp.zeros_like(ref)

        col = lax.broadcasted_iota(jnp.int32, (BLOCK, 2 * BLOCK), 1)
        k2 = jnp.concatenate([kp_ref[...], kc_ref[...]], axis=0)
        v2 = jnp.concatenate([vp_ref[...], vc_ref[...]], axis=0)
        qg, kg = qg_ref[...], kg_ref[...]
        scale = HEAD_DIM ** -0.5
        for hk in range(N_KV_HEADS):
            sl = slice(hk * HEAD_DIM, (hk + 1) * HEAD_DIM)
            kk = k2[:, sl]
            rk = _rms(kk)
            khat = kk * rk
            kn = (khat * kg).astype(bf16)
            vb = v2[:, sl].astype(bf16)
            dkn = jnp.zeros((2 * BLOCK, HEAD_DIM), f32)
            dvv = jnp.zeros((2 * BLOCK, HEAD_DIM), f32)
            for gq in range(Q_PER_KV):
                h = hk * Q_PER_KV + gq
                hs = slice(h * HEAD_DIM, (h + 1) * HEAD_DIM)
                qq = q_ref[:, hs]
                rq = _rms(qq)
                qhat = qq * rq
                qn = qhat * qg
                p, psink = _attn_probs(qn, kn, bias_ref[h], sink_ref[h], first, col)
                d_o = do_ref[:, hs]
                dp = _bdot_nt(d_o, vb)
                delta = jnp.sum(p * dp, axis=-1, keepdims=True)
                ds = p * (dp - delta)
                dband_ref[h] += ds
                dsink_ref[:, h:h + 1] += -jnp.sum(psink * delta, axis=0, keepdims=True)
                dqn = _bdot(ds, kn) * scale
                dkn = dkn + _bdot_tn(ds, qn) * scale
                dvv = dvv + _bdot_tn(p, d_o)
                dqg_ref[...] += jnp.sum(dqn * qhat, axis=0, keepdims=True)
                dq_ref[:, hs] = _rms_bwd(dqn, qhat, rq, qg)
            dkn_cur = dkn[BLOCK:] + dkn_scr[:, sl]
            dkn_scr[:, sl] = dkn[:BLOCK]
            khat_c, rk_c = khat[BLOCK:], rk[BLOCK:]
            dkg_ref[...] += jnp.sum(dkn_cur * khat_c, axis=0, keepdims=True)
            dk_ref[:, sl] = _rms_bwd(dkn_cur, khat_c, rk_c, kg)
            dv_ref[:, sl] = dvv[BLOCK:] + dv_scr[:, sl]
            dv_scr[:, sl] = dvv[:BLOCK]

    blk = lambda i: N_BLOCKS - 1 - i
    prev = lambda i: jnp.maximum(N_BLOCKS - 2 - i, 0)
    return pl.pallas_call(
        body, name="attn_bwd", grid=(N_BLOCKS,),
        in_specs=[pl.BlockSpec((BLOCK, D_ATTN), lambda i: (blk(i), 0)),
                  pl.BlockSpec((BLOCK, D_KV), lambda i: (blk(i), kcol)), pl.BlockSpec((BLOCK, D_KV), lambda i: (prev(i), kcol)),
                  pl.BlockSpec((BLOCK, D_KV), lambda i: (blk(i), vcol)), pl.BlockSpec((BLOCK, D_KV), lambda i: (prev(i), vcol)),
                  pl.BlockSpec((BLOCK, D_ATTN), lambda i: (blk(i), 0)),
                  _full((1, HEAD_DIM)), _full((1, HEAD_DIM)), pl.BlockSpec(memory_space=pltpu.SMEM),
                  _full((N_Q_HEADS, BLOCK, 2 * BLOCK))],
        out_specs=[pl.BlockSpec((BLOCK, D_ATTN), lambda i: (blk(i), 0)), pl.BlockSpec((BLOCK, D_KV), lambda i: (blk(i), 0)),
                   pl.BlockSpec((BLOCK, D_KV), lambda i: (blk(i), 0)), _full((N_Q_HEADS, BLOCK, 2 * BLOCK)),
                   _full((1, LANE)), _full((1, HEAD_DIM)), _full((1, HEAD_DIM))],
        out_shape=[jax.ShapeDtypeStruct((SEQ, D_ATTN), f32), jax.ShapeDtypeStruct((SEQ, D_KV), f32),
                   jax.ShapeDtypeStruct((SEQ, D_KV), f32), jax.ShapeDtypeStruct((N_Q_HEADS, BLOCK, 2 * BLOCK), f32),
                   jax.ShapeDtypeStruct((1, LANE), f32), jax.ShapeDtypeStruct((1, HEAD_DIM), f32),
                   jax.ShapeDtypeStruct((1, HEAD_DIM), f32)],
        scratch_shapes=[pltpu.VMEM((BLOCK, D_KV), f32), pltpu.VMEM((BLOCK, D_KV), f32)],
        compiler_params=_params("arbitrary"),
    )(proj, proj, proj, proj, proj, d_out, q_gain, k_gain, sinks, bias)


def _shift_down(u, s, row):
    if s == 0:
        return u
    return jnp.where(row >= s, pltpu.roll(u, s, 0), 0.0)


def _shift_up(u, s, row):
    if s == 0:
        return u
    return jnp.where(row < SEQ - s, pltpu.roll(u, SEQ - s, 0), 0.0)


def conv_fwd(proj, conv_w, conv_b):
    xcol = OFF_X // LANE

    def body(u_ref, w_ref, b_ref, o_ref):
        u = u_ref[...]
        row = lax.broadcasted_iota(jnp.int32, u.shape, 0)
        pre = b_ref[...] + jnp.zeros_like(u)
        for k in range(CONV_WIDTH):
            pre = pre + w_ref[k:k + 1, :] * _shift_down(u, CONV_WIDTH - 1 - k, row)
        o_ref[...] = pre * _sigmoid(pre)

    return pl.pallas_call(
        body, name="conv_fwd", grid=(D_CONV // LANE,),
        in_specs=[pl.BlockSpec((SEQ, LANE), lambda j: (0, xcol + j)), pl.BlockSpec((CONV_WIDTH, LANE), lambda j: (0, j)),
                  pl.BlockSpec((1, LANE), lambda j: (0, j))],
        out_specs=pl.BlockSpec((SEQ, LANE), lambda j: (0, j)),
        out_shape=jax.ShapeDtypeStruct((SEQ, D_CONV), f32),
        compiler_params=_params("arbitrary"),
    )(proj, conv_w, conv_b)


def conv_bwd(proj, d_act, conv_w, conv_b):
    xcol = OFF_X // LANE

    def body(u_ref, da_ref, w_ref, b_ref, du_ref, dw_ref, db_ref):
        u = u_ref[...]
        row = lax.broadcasted_iota(jnp.int32, u.shape, 0)
        shifted = [_shift_down(u, CONV_WIDTH - 1 - k, row) for k in range(CONV_WIDTH)]
        pre = b_ref[...] + jnp.zeros_like(u)
        for k in range(CONV_WIDTH):
            pre = pre + w_ref[k:k + 1, :] * shifted[k]
        sg = _sigmoid(pre)
        dpre = da_ref[...] * (sg * (1.0 + pre * (1.0 - sg)))
        db_ref[...] = jnp.sum(dpre, axis=0, keepdims=True)
        du = jnp.zeros_like(u)
        for k in range(CONV_WIDTH):
            dw_ref[k:k + 1, :] = jnp.sum(dpre * shifted[k], axis=0, keepdims=True)
            du = du + w_ref[k:k + 1, :] * _shift_up(dpre, CONV_WIDTH - 1 - k, row)
        du_ref[...] = du

    return pl.pallas_call(
        body, name="conv_bwd", grid=(D_CONV // LANE,),
        in_specs=[pl.BlockSpec((SEQ, LANE), lambda j: (0, xcol + j)), pl.BlockSpec((SEQ, LANE), lambda j: (0, j)),
                  pl.BlockSpec((CONV_WIDTH, LANE), lambda j: (0, j)), pl.BlockSpec((1, LANE), lambda j: (0, j))],
        out_specs=[pl.BlockSpec((SEQ, LANE), lambda j: (0, j)), pl.BlockSpec((CONV_WIDTH, LANE), lambda j: (0, j)),
                   pl.BlockSpec((1, LANE), lambda j: (0, j))],
        out_shape=[jax.ShapeDtypeStruct((SEQ, D_CONV), f32), jax.ShapeDtypeStruct((CONV_WIDTH, D_CONV), f32),
                   jax.ShapeDtypeStruct((1, D_CONV), f32)],
        compiler_params=_params("arbitrary"),
    )(proj, d_act, conv_w, conv_b)


def _ssd_chunk_common(dt_raw, dtb, alog):
    row = lax.broadcasted_iota(jnp.int32, (CHUNK, CHUNK), 0)
    col = lax.broadcasted_iota(jnp.int32, (CHUNK, CHUNK), 1)
    tri = (row >= col).astype(f32)
    strict = (row > col).astype(f32)
    dtp = _softplus(dt_raw + dtb)
    a_row = -jnp.exp(alog)
    d_a = dtp * a_row
    cs = _hdot(tri, d_a)
    cs_last = cs[CHUNK - 1:CHUNK, :]
    return row, col, tri, strict, dtp, a_row, d_a, cs, cs_last


def _seg_decay(tri, strict, d_a_h, row, col):
    seg = _hdot(tri, d_a_h * strict)
    return jnp.where(row >= col, jnp.exp(seg), 0.0)


def ssd_fwd(act, proj, dt_bias, a_log, d_skip, norm_g):
    zcol, dtcol = OFF_Z // D_SSM, OFF_DT // LANE
    gw = D_SSM // SSM_GROUPS

    def body(act_ref, z_ref, dt_ref, dtb_ref, alog_ref, dsk_ref, ng_ref, out_ref, ypre_ref, st_ref, state, ybuf):
        c = pl.program_id(0)

        @pl.when(c == 0)
        def _():
            state[...] = jnp.zeros_like(state)

        row, col, tri, strict, dtp, a_row, d_a, cs, cs_last = _ssd_chunk_common(dt_ref[...], dtb_ref[...], alog_ref[...])
        e_cs = jnp.exp(cs)
        dte = jnp.exp(cs_last - cs)
        ecl = jnp.exp(cs_last)
        dsk = dsk_ref[...]
        for g in range(SSM_GROUPS):
            bg = act_ref[:, D_SSM + g * SSM_STATE:D_SSM + (g + 1) * SSM_STATE]
            cg = act_ref[:, D_SSM + D_BC + g * SSM_STATE:D_SSM + D_BC + (g + 1) * SSM_STATE]
            cb = _bdot_nt(cg, bg)
            for r in range(HEADS_PER_GROUP):
                hd = g * HEADS_PER_GROUP + r
                hs = slice(hd * SSM_HEAD_DIM, (hd + 1) * SSM_HEAD_DIM)
                hl = slice(hd, hd + 1)
                x_h = act_ref[:, hs]
                xdt = x_h * dtp[:, hl]
                lm = _seg_decay(tri, strict, d_a[:, hl], row, col)
                prev = state[hd]
                st_ref[0, hd] = prev
                y = _bdot(cb * lm, xdt) + e_cs[:, hl] * _bdot(cg, prev) + x_h * dsk[:, hl]
                ybuf[:, hs] = y
                state[hd] = prev * ecl[:, hl] + _bdot_tn(bg, xdt * dte[:, hl])
        y = ybuf[...]
        ypre_ref[...] = y
        z = z_ref[...]
        yz = y * (z * _sigmoid(z))
        ng = ng_ref[...]
        for g in range(SSM_GROUPS):
            gs = slice(g * gw, (g + 1) * gw)
            part = yz[:, gs]
            out_ref[:, gs] = part * _rms(part) * ng[:, gs]

    return pl.pallas_call(
        body, name="ssd_fwd", grid=(N_CHUNKS,),
        in_specs=[pl.BlockSpec((CHUNK, D_CONV), lambda c: (c, 0)), pl.BlockSpec((CHUNK, D_SSM), lambda c: (c, zcol)),
                  pl.BlockSpec((CHUNK, LANE), lambda c: (c, dtcol)), _full((1, LANE)), _full((1, LANE)), _full((1, LANE)),
                  _full((1, D_SSM))],
        out_specs=[pl.BlockSpec((CHUNK, D_SSM), lambda c: (c, 0)), pl.BlockSpec((CHUNK, D_SSM), lambda c: (c, 0)),
                   pl.BlockSpec((1, SSM_HEADS, SSM_STATE, SSM_HEAD_DIM), lambda c: (c, 0, 0, 0))],
        out_shape=[jax.ShapeDtypeStruct((SEQ, D_SSM), f32), jax.ShapeDtypeStruct((SEQ, D_SSM), f32),
                   jax.ShapeDtypeStruct((N_CHUNKS, SSM_HEADS, SSM_STATE, SSM_HEAD_DIM), f32)],
        scratch_shapes=[pltpu.VMEM((SSM_HEADS, SSM_STATE, SSM_HEAD_DIM), f32), pltpu.VMEM((CHUNK, D_SSM), f32)],
        compiler_params=_params("arbitrary"),
    )(act, proj, proj, dt_bias, a_log, d_skip, norm_g)


def ssd_bwd(act, proj, ypre, states, d_out, dt_bias, a_log, d_skip, norm_g):
    zcol, dtcol = OFF_Z // D_SSM, OFF_DT // LANE
    gw = D_SSM // SSM_GROUPS

    def body(act_ref, z_ref, dt_ref, ypre_ref, st_ref, do_ref, dtb_ref, alog_ref, dsk_ref, ng_ref,
             dact_ref, ddt_ref, dz_ref, dng_ref, dpar_ref, dstate, dybuf):
        i = pl.program_id(0)

        @pl.when(i == 0)
        def _():
            for ref in (dng_ref, dpar_ref, dstate):
                ref[...] = jnp.zeros_like(ref)

        y = ypre_ref[...]
        z = z_ref[...]
        sgz = _sigmoid(z)
        sz = z * sgz
        yz = y * sz
        ng = ng_ref[...]
        d_o = do_ref[...]
        for g in range(SSM_GROUPS):
            gs = slice(g * gw, (g + 1) * gw)
            part = yz[:, gs]
            r = _rms(part)
            yhat = part * r
            dng_ref[:, gs] += jnp.sum(d_o[:, gs] * yhat, axis=0, keepdims=True)
            dyz = _rms_bwd(d_o[:, gs], yhat, r, ng[:, gs])
            dybuf[:, gs] = dyz * sz[:, gs]
            dz_ref[:, gs] = dyz * y[:, gs] * (sgz[:, gs] * (1.0 + z[:, gs] * (1.0 - sgz[:, gs])))

        row, col, tri, strict, dtp, a_row, d_a, cs, cs_last = _ssd_chunk_common(dt_ref[...], dtb_ref[...], alog_ref[...])
        upper = (row <= col).astype(f32)
        lane = lax.broadcasted_iota(jnp.int32, (CHUNK, LANE), 1)
        lane1 = lax.broadcasted_iota(jnp.int32, (1, LANE), 1)
        e_cs = jnp.exp(cs)
        dte = jnp.exp(cs_last - cs)
        ecl = jnp.exp(cs_last)
        dsk = dsk_ref[...]
        ddt_mat = jnp.zeros((CHUNK, LANE), f32)
        dcs_mat = jnp.zeros((CHUNK, LANE), f32)
        dda_mat = jnp.zeros((CHUNK, LANE), f32)
        dcsl_row = jnp.zeros((1, LANE), f32)
        dd_row = jnp.zeros((1, LANE), f32)
        for g in range(SSM_GROUPS):
            bsl = slice(D_SSM + g * SSM_STATE, D_SSM + (g + 1) * SSM_STATE)
            csl = slice(D_SSM + D_BC + g * SSM_STATE, D_SSM + D_BC + (g + 1) * SSM_STATE)
            bg = act_ref[:, bsl]
            cg = act_ref[:, csl]
            cb = _bdot_nt(cg, bg)
            dcb = jnp.zeros((CHUNK, CHUNK), f32)
            dbg = jnp.zeros((CHUNK, SSM_STATE), f32)
            dcg = jnp.zeros((CHUNK, SSM_STATE), f32)
            for rr in range(HEADS_PER_GROUP):
                hd = g * HEADS_PER_GROUP + rr
                hs = slice(hd * SSM_HEAD_DIM, (hd + 1) * SSM_HEAD_DIM)
                hl = slice(hd, hd + 1)
                x_h = act_ref[:, hs]
                dt_h = dtp[:, hl]
                e_h = e_cs[:, hl]
                dte_h = dte[:, hl]
                ecl_h = ecl[:, hl]
                xdt = x_h * dt_h
                lm = _seg_decay(tri, strict, d_a[:, hl], row, col)
                m = cb * lm
                prev = st_ref[0, hd]
                dy = dybuf[:, hs]
                dh = dstate[hd]
                dd_row = dd_row + jnp.where(lane1 == hd, jnp.sum(jnp.sum(dy * x_h, axis=1, keepdims=True), axis=0, keepdims=True), 0.0)
                dx = dy * dsk[:, hl]
                gmat = _bdot(cg, prev)
                dg = dy * e_h
                dcg = dcg + _bdot_nt(dg, prev)
                dprev = _bdot_tn(cg, dg)
                dcs_h = jnp.sum(dy * gmat, axis=1, keepdims=True) * e_h
                dm = _bdot_nt(dy, xdt)
                dxdt = _bdot_tn(m, dy)
                dcb = dcb + dm * lm
                dseg = dm * m
                dda_h = jnp.sum(_hdot(upper, dseg) * strict, axis=1, keepdims=True)
                wmat = xdt * dte_h
                dbg = dbg + _bdot_nt(wmat, dh)
                dw = _bdot(bg, dh)
                dxdt = dxdt + dw * dte_h
                ddte = jnp.sum(dw * xdt, axis=1, keepdims=True) * dte_h
                dcs_h = dcs_h - ddte
                dcsl = jnp.sum(ddte, axis=0, keepdims=True)
                dcsl = dcsl + jnp.sum(jnp.sum(dh * prev, axis=1, keepdims=True), axis=0, keepdims=True) * ecl_h
                dstate[hd] = dprev + dh * ecl_h
                dact_ref[:, hs] = dx + dxdt * dt_h
                ddt_h = jnp.sum(dxdt * x_h, axis=1, keepdims=True)
                ddt_mat = jnp.where(lane == hd, ddt_h, ddt_mat)
                dcs_mat = jnp.where(lane == hd, dcs_h, dcs_mat)
                dda_mat = jnp.where(lane == hd, dda_h, dda_mat)
                dcsl_row = jnp.where(lane1 == hd, dcsl, dcsl_row)
            dact_ref[:, bsl] = dbg + _bdot_tn(dcb, cg)
            dact_ref[:, csl] = dcg + _bdot(dcb, bg)
        rowl = lax.broadcasted_iota(jnp.int32, (CHUNK, LANE), 0)
        dcs_mat = dcs_mat + jnp.where(rowl == CHUNK - 1, dcsl_row, 0.0)
        dda = dda_mat + _hdot(upper, dcs_mat)
        ddt_mat = ddt_mat + dda * a_row
        da_row = jnp.sum(dda * dtp, axis=0, keepdims=True)
        ddt_raw = ddt_mat * _sigmoid(dt_ref[...] + dtb_ref[...])
        ddt_ref[...] = ddt_raw
        dpar_ref[0:1, :] += jnp.sum(ddt_raw, axis=0, keepdims=True)
        dpar_ref[1:2, :] += da_row * a_row
        dpar_ref[2:3, :] += dd_row

    blk = lambda i: N_CHUNKS - 1 - i
    return pl.pallas_call(
        body, name="ssd_bwd", grid=(N_CHUNKS,),
        in_specs=[pl.BlockSpec((CHUNK, D_CONV), lambda i: (blk(i), 0)), pl.BlockSpec((CHUNK, D_SSM), lambda i: (blk(i), zcol)),
                  pl.BlockSpec((CHUNK, LANE), lambda i: (blk(i), dtcol)), pl.BlockSpec((CHUNK, D_SSM), lambda i: (blk(i), 0)),
                  pl.BlockSpec((1, SSM_HEADS, SSM_STATE, SSM_HEAD_DIM), lambda i: (blk(i), 0, 0, 0)),
                  pl.BlockSpec((CHUNK, D_SSM), lambda i: (blk(i), 0)),
                  _full((1, LANE)), _full((1, LANE)), _full((1, LANE)), _full((1, D_SSM))],
        out_specs=[pl.BlockSpec((CHUNK, D_CONV), lambda i: (blk(i), 0)), pl.BlockSpec((CHUNK, LANE), lambda i: (blk(i), 0)),
                   pl.BlockSpec((CHUNK, D_SSM), lambda i: (blk(i), 0)), _full((1, D_SSM)), _full((8, LANE))],
        out_shape=[jax.ShapeDtypeStruct((SEQ, D_CONV), f32), jax.ShapeDtypeStruct((SEQ, LANE), f32),
                   jax.ShapeDtypeStruct((SEQ, D_SSM), f32), jax.ShapeDtypeStruct((1, D_SSM), f32),
                   jax.ShapeDtypeStruct((8, LANE), f32)],
        scratch_shapes=[pltpu.VMEM((SSM_HEADS, SSM_STATE, SSM_HEAD_DIM), f32), pltpu.VMEM((CHUNK, D_SSM), f32)],
        compiler_params=_params("arbitrary"),
    )(act, proj, proj, ypre, states, d_out, dt_bias, a_log, d_skip, norm_g)


def out_fwd(x, attn, ssm, w_out, tm=512):
    def body(x_ref, a_ref, s_ref, w_ref, o_ref):
        o_ref[...] = x_ref[...] + _bdot(a_ref[...], w_ref[:D_ATTN, :]) + _bdot(s_ref[...], w_ref[D_ATTN:, :])

    tok = lambda w_: pl.BlockSpec((tm, w_), lambda i: (i, 0))
    return pl.pallas_call(
        body, name="out_fwd", grid=(SEQ // tm,),
        in_specs=[tok(D_MODEL), tok(D_ATTN), tok(D_SSM), _full((D_MODEL, D_MODEL))],
        out_specs=tok(D_MODEL), out_shape=jax.ShapeDtypeStruct((SEQ, D_MODEL), f32),
        compiler_params=_params("arbitrary"),
    )(x, attn, ssm, w_out)


def out_bwd(dx1, attn, ssm, w_out, tm=512):
    def body(d_ref, a_ref, s_ref, w_ref, da_ref, ds_ref, dw_ref):
        @pl.when(pl.program_id(0) == 0)
        def _():
            dw_ref[...] = jnp.zeros_like(dw_ref)

        d = d_ref[...].astype(bf16)
        dcat = _bdot_nt(d, w_ref[...])
        da_ref[...] = dcat[:, :D_ATTN]
        ds_ref[...] = dcat[:, D_ATTN:]
        dw_ref[:D_ATTN, :] += _bdot_tn(a_ref[...], d)
        dw_ref[D_ATTN:, :] += _bdot_tn(s_ref[...], d)

    tok = lambda w_: pl.BlockSpec((tm, w_), lambda i: (i, 0))
    return pl.pallas_call(
        body, name="out_bwd", grid=(SEQ // tm,),
        in_specs=[tok(D_MODEL), tok(D_ATTN), tok(D_SSM), _full((D_MODEL, D_MODEL))],
        out_specs=[tok(D_ATTN), tok(D_SSM), _full((D_MODEL, D_MODEL))],
        out_shape=[jax.ShapeDtypeStruct((SEQ, D_ATTN), f32), jax.ShapeDtypeStruct((SEQ, D_SSM), f32),
                   jax.ShapeDtypeStruct((D_MODEL, D_MODEL), f32)],
        compiler_params=_params("arbitrary"),
    )(dx1, attn, ssm, w_out)


def mlp_fwd(x1, g, w_up, w_down, tm=512):
    def body(x_ref, g_ref, wu_ref, wd_ref, o_ref, u_ref, h_scr):
        j = pl.program_id(1)

        @pl.when(j == 0)
        def _():
            xv = x_ref[...]
            h_scr[...] = (xv * _rms(xv) * g_ref[...]).astype(bf16)
            o_ref[...] = xv

        u = jnp.dot(h_scr[...], wu_ref[...], preferred_element_type=f32)
        u_ref[...] = u
        a = jnp.square(jnp.maximum(u, 0.0))
        o_ref[...] += _bdot(a, wd_ref[...])

    return pl.pallas_call(
        body, name="mlp_fwd", grid=(SEQ // tm, N_CHIPS),
        in_specs=[pl.BlockSpec((tm, D_MODEL), lambda i, j: (i, 0)), _full((1, D_MODEL)),
                  pl.BlockSpec((None, D_MODEL, FF_TILE), lambda i, j: (j, 0, 0)),
                  pl.BlockSpec((None, FF_TILE, D_MODEL), lambda i, j: (j, 0, 0))],
        out_specs=[pl.BlockSpec((tm, D_MODEL), lambda i, j: (i, 0)), pl.BlockSpec((tm, FF_TILE), lambda i, j: (i, j))],
        out_shape=[jax.ShapeDtypeStruct((SEQ, D_MODEL), f32), jax.ShapeDtypeStruct((SEQ, D_FF), f32)],
        scratch_shapes=[pltpu.VMEM((tm, D_MODEL), bf16)],
        compiler_params=_params("arbitrary", "arbitrary"),
    )(x1, g, w_up, w_down)


def mlp_bwd_data(dx2, u, x1, g, w_up, w_down, tm=512):
    def body(d_ref, u_ref, x_ref, g_ref, wu_ref, wd_ref, dx_ref, du_ref, dg_ref, dh_scr):
        i, j = pl.program_id(0), pl.program_id(1)

        @pl.when(jnp.logical_and(i == 0, j == 0))
        def _():
            dg_ref[...] = jnp.zeros_like(dg_ref)

        @pl.when(j == 0)
        def _():
            dh_scr[...] = jnp.zeros_like(dh_scr)

        da = _bdot_nt(d_ref[...], wd_ref[...])
        du = (da * (2.0 * jnp.maximum(u_ref[...], 0.0))).astype(bf16)
        du_ref[...] = du
        dh_scr[...] += _bdot_nt(du, wu_ref[...])

        @pl.when(j == N_CHIPS - 1)
        def _():
            xv = x_ref[...]
            r = _rms(xv)
            xhat = xv * r
            dh = dh_scr[...]
            dg_ref[...] += jnp.sum(dh * xhat, axis=0, keepdims=True)
            dx_ref[...] = d_ref[...] + _rms_bwd(dh, xhat, r, g_ref[...])

    return pl.pallas_call(
        body, name="mlp_bwd_data", grid=(SEQ // tm, N_CHIPS),
        in_specs=[pl.BlockSpec((tm, D_MODEL), lambda i, j: (i, 0)), pl.BlockSpec((tm, FF_TILE), lambda i, j: (i, j)),
                  pl.BlockSpec((tm, D_MODEL), lambda i, j: (i, 0)), _full((1, D_MODEL)),
                  pl.BlockSpec((None, D_MODEL, FF_TILE), lambda i, j: (j, 0, 0)),
                  pl.BlockSpec((None, FF_TILE, D_MODEL), lambda i, j: (j, 0, 0))],
        out_specs=[pl.BlockSpec((tm, D_MODEL), lambda i, j: (i, 0)), pl.BlockSpec((tm, FF_TILE), lambda i, j: (i, j)),
                   _full((1, D_MODEL))],
        out_shape=[jax.ShapeDtypeStruct((SEQ, D_MODEL), f32), jax.ShapeDtypeStruct((SEQ, D_FF), bf16),
                   jax.ShapeDtypeStruct((1, D_MODEL), f32)],
        scratch_shapes=[pltpu.VMEM((tm, D_MODEL), f32)],
        compiler_params=_params("arbitrary", "arbitrary"),
    )(dx2, u, x1, g, w_up, w_down)


def mlp_bwd_weights(dx2, u, du, x1, g, tm=512):
    def body(d_ref, u_ref, du_ref, x_ref, g_ref, dwu_ref, dwd_ref):
        @pl.when(pl.program_id(1) == 0)
        def _():
            dwu_ref[...] = jnp.zeros_like(dwu_ref)
            dwd_ref[...] = jnp.zeros_like(dwd_ref)

        xv = x_ref[...]
        h = xv * _rms(xv) * g_ref[...]
        dwu_ref[...] += _bdot_tn(h, du_ref[...])
        a = jnp.square(jnp.maximum(u_ref[...], 0.0))
        dwd_ref[...] += _bdot_tn(a, d_ref[...])

    return pl.pallas_call(
        body, name="mlp_bwd_weights", grid=(N_CHIPS, SEQ // tm),
        in_specs=[pl.BlockSpec((tm, D_MODEL), lambda j, i: (i, 0)), pl.BlockSpec((tm, FF_TILE), lambda j, i: (i, j)),
                  pl.BlockSpec((tm, FF_TILE), lambda j, i: (i, j)), pl.BlockSpec((tm, D_MODEL), lambda j, i: (i, 0)),
                  _full((1, D_MODEL))],
        out_specs=[pl.BlockSpec((None, D_MODEL, FF_TILE), lambda j, i: (j, 0, 0)),
                   pl.BlockSpec((None, FF_TILE, D_MODEL), lambda j, i: (j, 0, 0))],
        out_shape=[jax.ShapeDtypeStruct((N_CHIPS, D_MODEL, FF_TILE), f32), jax.ShapeDtypeStruct((N_CHIPS, FF_TILE, D_MODEL), f32)],
        compiler_params=_params("arbitrary", "arbitrary"),
    )(dx2, u, du, x1, g)


def loss_head(y, target, tm=512):
    def body(y_ref, t_ref, dy_ref, l_ref):
        @pl.when(pl.program_id(0) == 0)
        def _():
            l_ref[...] = jnp.zeros_like(l_ref)

        d = y_ref[...] - t_ref[...]
        dy_ref[...] = d * (1.0 / D_MODEL)
        part = jnp.sum(jnp.mean(d * d, axis=-1, keepdims=True), axis=0, keepdims=True)
        l_ref[...] += 0.5 * part

    tok = pl.BlockSpec((tm, D_MODEL), lambda i: (i, 0))
    return pl.pallas_call(
        body, name="loss_head", grid=(SEQ // tm,), in_specs=[tok, tok], out_specs=[tok, _full((1, 1))],
        out_shape=[jax.ShapeDtypeStruct((SEQ, D_MODEL), f32), jax.ShapeDtypeStruct((1, 1), f32)],
        compiler_params=_params("arbitrary"),
    )(y, target)


def _pad_lane(v):
    return jnp.pad(v, (0, LANE - v.shape[0]))[None, :]


def local_step(x, target, w, prov):
    bucket = jnp.asarray(_bucket_table())
    bias = bias_build(w["rel_bias"], bucket)
    saved = []
    for l in range(DEPTH):
        g_mix = w["mix_norm_g"][l][None, :] + prov.stage(("begin", l), x)
        w_in = prov.w_in(l, x)
        proj = in_fwd(x, g_mix, w_in)
        qg, kg = w["q_gain"][l][None, :], w["k_gain"][l][None, :]
        attn = attn_fwd(proj, qg, kg, w["sinks"][l], bias)
        conv_b = w["conv_b"][l][None, :]
        act = conv_fwd(proj, w["conv_w"][l], conv_b)
        dtb = _pad_lane(w["dt_bias"][l]) + prov.stage(("mid", l), act)
        alog, dsk = _pad_lane(w["a_log"][l]), _pad_lane(w["d_skip"][l])
        ng = w["ssm_norm_g"][l][None, :]
        ssm, ypre, states = ssd_fwd(act, proj, dtb, alog, dsk, ng)
        tok = prov.stage(("pre_out", l), ssm)
        w_out = prov.w_out(l, ssm) + jnp.asarray(tok, bf16)
        x1 = out_fwd(x, attn, ssm, w_out)
        g_mlp = w["mlp_norm_g"][l][None, :] + prov.stage(("pre_mlp", l), x1)
        w_up, w_down = prov.mlp(l, x1)
        x2, u = mlp_fwd(x1, g_mlp, w_up, w_down)
        saved.append(dict(x=x, proj=proj, attn=attn, act=act, ssm=ssm, ypre=ypre, states=states, x1=x1, u=u,
                          g_mix=g_mix, qg=qg, kg=kg, conv_b=conv_b, dtb=dtb, alog=alog, dsk=dsk, ng=ng, g_mlp=g_mlp,
                          w_in=w_in, w_out=w_out, w_up=w_up, w_down=w_down))
        x = x2
    dx, loss = loss_head(x, target)
    grads = [None] * DEPTH
    dbands = [None] * DEPTH
    tok = 0.0
    for l in reversed(range(DEPTH)):
        s = saved[l]
        g_mlp = s["g_mlp"] + tok
        dx1, du, dg_mlp = mlp_bwd_data(dx, s["u"], s["x1"], g_mlp, s["w_up"], s["w_down"])
        dw_up, dw_down = mlp_bwd_weights(dx, s["u"], du, s["x1"], g_mlp)
        tok = prov.grads(("mlp", l), dict(w_up=dw_up, w_down=dw_down), dx1)
        dattn, dssm, dw_out = out_bwd(dx1, s["attn"], s["ssm"], s["w_out"])
        dact, ddt, dz, dng, dpar = ssd_bwd(s["act"], s["proj"], s["ypre"], s["states"], dssm, s["dtb"] + tok, s["alog"],
                                           s["dsk"], s["ng"])
        conv_b = s["conv_b"] + prov.stage(("bwd_mid", l), dact)
        dxbc, dconv_w, dconv_b = conv_bwd(s["proj"], dact, w["conv_w"][l], conv_b)
        dq, dk, dv, dband, dsink, dqg, dkg = attn_bwd(s["proj"], dattn, s["qg"], s["kg"], w["sinks"][l], bias)
        dx, dw_in, dg_mix = in_bwd(dq, dz, dxbc, dk, dv, ddt, s["x"], s["g_mix"], s["w_in"], dx1)
        tok = prov.grads(("mix", l), dict(w_in=dw_in, w_out=dw_out), dx)
        dbands[l] = dband
        grads[l] = dict(mix_norm_g=dg_mix[0], q_gain=dqg[0], k_gain=dkg[0], sinks=dsink[0, :N_Q_HEADS],
                        conv_w=dconv_w, conv_b=dconv_b[0], dt_bias=dpar[0, :SSM_HEADS], a_log=dpar[1, :SSM_HEADS],
                        d_skip=dpar[2, :SSM_HEADS], ssm_norm_g=dng[0], mlp_norm_g=dg_mlp[0])
    out = {k: jnp.stack([grads[l][k] for l in range(DEPTH)]) for k in grads[0]}
    out["rel_bias"] = bias_bwd(dbands[0], dbands[1], bucket)[:, :N_Q_HEADS]
    return loss, dx, out, tok


MESH = pl.DeviceIdType.MESH
HBM = pl.BlockSpec(memory_space=pltpu.HBM)
N_PEER_CHIPS = N_CHIPS - 1
N_DEVICES = 8


def _coords():
    return lax.axis_index("x"), lax.axis_index("y"), lax.axis_index("c")


def _peer_chips(x, y):
    return [(1 - x, y), (x, 1 - y), (1 - x, 1 - y)]


def _remote(src, dst, send_sem, recv_sem, device):
    return pltpu.make_async_remote_copy(src_ref=src, dst_ref=dst, send_sem=send_sem, recv_sem=recv_sem,
                                        device_id=device, device_id_type=MESH)


SEM = pl.BlockSpec(memory_space=pltpu.SEMAPHORE)
ANY = pl.BlockSpec(memory_space=pl.ANY)
DATAFLOW = pltpu.SideEffectType.DATAFLOW_SIDE_EFFECTING


def _gather_copies(kind, src_refs, land_refs, ssem, rsem):
    x, y, c = _coords()
    k_me = 2 * x + y
    cps = []
    for p, land in enumerate(land_refs):
        hr = land.shape[1] // 2
        rows = pl.ds(c * hr, hr)
        for j, chip in enumerate(_peer_chips(x, y)):
            i = 3 * p + j
            if kind == "ici":
                cps.append(_remote(src_refs[p].at[rows, :], land.at[k_me, rows, :], ssem.at[i], rsem.at[i], (*chip, c)))
            else:
                got = land.at[2 * chip[0] + chip[1], rows, :]
                cps.append(_remote(got, got, ssem.at[i], rsem.at[i], (x, y, 1 - c)))
    return cps


def gather_now(srcs, conv):
    n = len(srcs)

    def body(*refs):
        src_refs, conv_ref = refs[:n], refs[n]
        lands, gconv = refs[n + 1:2 * n + 1], refs[2 * n + 1]
        ssem, rsem, fsem, frsem, csem, crsem = refs[2 * n + 2:]
        x, y, c = _coords()
        k_me = 2 * x + y
        chips = _peer_chips(x, y)
        ici = _gather_copies("ici", src_refs, lands, ssem, rsem)
        relay = _gather_copies("relay", src_refs, lands, fsem, frsem)
        conv_cps = [_remote(conv_ref, gconv.at[k_me], csem.at[j], crsem.at[j], (*chip, c)) for j, chip in enumerate(chips)]
        for cp in ici + conv_cps:
            cp.start()
        for cp, fw in zip(ici, relay):
            cp.wait_recv()
            fw.start()
        for cp in conv_cps + relay:
            cp.wait_recv()
        for cp in ici + relay + conv_cps:
            cp.wait_send()

    out_shape = [jax.ShapeDtypeStruct((N_CHIPS,) + s.shape, s.dtype) for s in srcs]
    out_shape.append(jax.ShapeDtypeStruct((N_CHIPS,) + conv.shape, conv.dtype))
    sems = lambda k: pltpu.SemaphoreType.DMA((k,))
    return pl.pallas_call(
        body, name="gather_now", out_shape=out_shape, in_specs=[HBM] * (n + 1), out_specs=[HBM] * (n + 1),
        scratch_shapes=[sems(3 * n), sems(3 * n), sems(3 * n), sems(3 * n), sems(N_PEER_CHIPS), sems(N_PEER_CHIPS)],
    )(*srcs, conv)


def _gather_maker(kind, n_src):
    def make(refs, ssem, rsem):
        cps = _gather_copies(kind, refs[:n_src], refs[n_src:], ssem, rsem)
        return cps, cps
    return make


def _scatter_maker(n):
    def make(refs, ssem, rsem):
        x, y, c = _coords()
        k_me = 2 * x + y
        sends, arrivals = [], []
        for p in range(n):
            src, land = refs[p], refs[n + p]
            sends.append(_remote(src.at[k_me, 1 - c], land.at[0], ssem.at[7 * p], rsem.at[7 * p], (x, y, 1 - c)))
            for j, chip in enumerate(_peer_chips(x, y)):
                for cc in range(2):
                    sends.append(_remote(src.at[2 * chip[0] + chip[1], cc], land.at[1 + 2 * j + c],
                                         ssem.at[7 * p + 1 + 2 * j + cc], rsem.at[7 * p + 1 + 2 * j + c], (*chip, cc)))
            for s in range(7):
                arrivals.append(_remote(land.at[s], land.at[s], ssem.at[7 * p + s], rsem.at[7 * p + s], (x, y, 1 - c)))
        return sends, arrivals
    return make


def _share_maker(n):
    def make(refs, ssem, rsem):
        x, y, c = _coords()
        sends = [_remote(refs[p].at[c], refs[p].at[c], ssem.at[p], rsem.at[p], (x, y, 1 - c)) for p in range(n)]
        arrivals = [_remote(refs[p].at[1 - c], refs[p].at[1 - c], ssem.at[p], rsem.at[p], (x, y, 1 - c)) for p in range(n)]
        return sends, arrivals
    return make


def split_start(name, make, n_sems, operands, after):
    n = len(operands)

    def body(*refs):
        ssem, rsem, token = refs[n + 1], refs[n + 2], refs[-1]
        for cp in make(refs[:n], ssem, rsem)[0]:
            cp.start()
        token[...] = jnp.zeros_like(token)

    ops = [pltpu.with_memory_space_constraint(a, pltpu.HBM) for a in operands]
    outs = pl.pallas_call(
        body, name=name,
        out_shape=(pltpu.SemaphoreType.DMA((n_sems,)), pltpu.SemaphoreType.DMA((n_sems,)),
                   *[pltpu.HBM(a.shape, a.dtype) for a in ops], jax.ShapeDtypeStruct((8, LANE), f32)),
        in_specs=[HBM] * n + [ANY], out_specs=(SEM, SEM, *[HBM] * n, pl.BlockSpec(memory_space=pltpu.VMEM)),
        input_output_aliases={i: 2 + i for i in range(n)},
        compiler_params=pltpu.CompilerParams(has_side_effects=DATAFLOW),
    )(*ops, after)
    return dict(name=name, make=make, ssem=outs[0], rsem=outs[1], operands=outs[2:2 + n], token=outs[-1][0, 0])


def split_wait(handle, after):
    n = len(handle["operands"])

    def body(*refs):
        sends, arrivals = handle["make"](refs[:n], refs[n], refs[n + 1])
        for cp in sends:
            cp.wait_send()
        for cp in arrivals:
            cp.wait_recv()

    outs = pl.pallas_call(
        body, name=handle["name"].replace("start", "wait"),
        out_shape=tuple(pltpu.HBM(a.shape, a.dtype) for a in handle["operands"]),
        in_specs=[HBM] * n + [SEM, SEM, ANY], out_specs=tuple([HBM] * n),
        input_output_aliases={i: i for i in range(n)},
        compiler_params=pltpu.CompilerParams(has_side_effects=DATAFLOW),
    )(*handle["operands"], handle["ssem"], handle["rsem"], after)
    return list(outs)


def piece_sum(g, recv, kc_arr):
    _, _, rb, cc = g.shape
    tr = min(256, rb)

    def body(kc_ref, g_ref, r_ref, o_ref):
        acc = g_ref[...]
        for s in range(7):
            acc = acc + r_ref[s].astype(f32)
        o_ref[...] = acc

    return pl.pallas_call(
        body, name="piece_sum",
        grid_spec=pltpu.PrefetchScalarGridSpec(
            num_scalar_prefetch=1, grid=(rb // tr,),
            in_specs=[pl.BlockSpec((None, None, tr, cc), lambda r, kc: (kc[0], kc[1], r, 0)),
                      pl.BlockSpec((7, tr, cc), lambda r, kc: (0, r, 0))],
            out_specs=pl.BlockSpec((None, tr, cc), lambda r, kc: (kc[1], r, 0))),
        out_shape=jax.ShapeDtypeStruct((2, rb, cc), f32),
        compiler_params=_params("arbitrary"),
    )(kc_arr, g, recv)


def small_all_reduce(vec):
    def body(v_ref, o_ref, gat, ssem, rsem):
        x, y, c = _coords()
        me = 4 * x + 2 * y + c
        gat[me] = v_ref[...]
        sends = []
        for t in range(1, N_DEVICES):
            peer = (x ^ (t >> 2), y ^ ((t >> 1) & 1), c ^ (t & 1))
            cp = _remote(v_ref, gat.at[me], ssem.at[t - 1], rsem.at[t - 1], peer)
            cp.start()
            sends.append(cp)
        for t in range(1, N_DEVICES):
            peer = (x ^ (t >> 2), y ^ ((t >> 1) & 1), c ^ (t & 1))
            slot = gat.at[4 * peer[0] + 2 * peer[1] + peer[2]]
            _remote(slot, slot, ssem.at[t - 1], rsem.at[t - 1], peer).wait_recv()
        for cp in sends:
            cp.wait_send()
        acc = gat[0]
        for d in range(1, N_DEVICES):
            acc = acc + gat[d]
        o_ref[...] = acc

    return pl.pallas_call(
        body, name="small_all_reduce", out_shape=jax.ShapeDtypeStruct(vec.shape, vec.dtype),
        in_specs=[pl.BlockSpec(memory_space=pltpu.VMEM)], out_specs=pl.BlockSpec(memory_space=pltpu.VMEM),
        scratch_shapes=[pltpu.VMEM((N_DEVICES,) + vec.shape, vec.dtype), pltpu.SemaphoreType.DMA((N_DEVICES - 1,)),
                        pltpu.SemaphoreType.DMA((N_DEVICES - 1,))],
    )(vec)


def _adamw_math(w, g, m, v):
    m_new = ADAM_B1 * m + (1.0 - ADAM_B1) * g
    v_new = ADAM_B2 * v + (1.0 - ADAM_B2) * jnp.square(g)
    m_hat = m_new / (1.0 - ADAM_B1 ** ADAM_STEP)
    v_hat = v_new / (1.0 - ADAM_B2 ** ADAM_STEP)
    delta = -ADAM_LR * (m_hat / (jnp.sqrt(v_hat) + ADAM_EPS) + ADAM_WD * w)
    return delta, m_new, v_new


def adamw_shard(w, g0, g1, m, v):
    depth, rows, cols = w.shape
    half = rows // 2
    tr = min(256, half)
    nr = half // tr

    def body(w_ref, g0_ref, g1_ref, m_ref, v_ref, go_ref, d_ref, nm_ref, nv_ref):
        gv = jnp.where(pl.program_id(0) == 0, g0_ref[...], g1_ref[...])
        go_ref[...] = gv
        d_ref[...], nm_ref[...], nv_ref[...] = _adamw_math(w_ref[...], gv, m_ref[...], v_ref[...])

    spec = pl.BlockSpec((None, tr, cols), lambda l, h, r: (l, h * nr + r, 0))
    g0spec = pl.BlockSpec((None, tr, cols), lambda l, h, r: (jnp.where(l == 0, h, 1), jnp.where(l == 0, r, nr - 1), 0))
    g1spec = pl.BlockSpec((None, tr, cols), lambda l, h, r: (jnp.where(l == 1, h, 0), jnp.where(l == 1, r, 0), 0))
    return pl.pallas_call(
        body, name="adamw_shard", grid=(depth, 2, nr), in_specs=[spec, g0spec, g1spec, spec, spec], out_specs=[spec] * 4,
        out_shape=[jax.ShapeDtypeStruct(w.shape, f32)] * 4,
        compiler_params=_params("arbitrary", "arbitrary", "arbitrary"),
    )(w, g0, g1, m, v)


def adamw_small(w, g, m, v):
    def body(w_ref, g_ref, m_ref, v_ref, d_ref, nm_ref, nv_ref):
        d_ref[...], nm_ref[...], nv_ref[...] = _adamw_math(w_ref[...], g_ref[...], m_ref[...], v_ref[...])

    return pl.pallas_call(
        body, name="adamw_small", out_shape=[jax.ShapeDtypeStruct(w.shape, f32)] * 3,
    )(w, g, m, v)


WEIGHTS = ("mix_norm_g", "w_in", "q_gain", "k_gain", "sinks", "rel_bias", "conv_w", "conv_b", "dt_bias", "a_log", "d_skip",
           "ssm_norm_g", "w_out", "mlp_norm_g", "w_up", "w_down")
BIG = ("w_in", "w_out", "w_up", "w_down")
SMALL = tuple(n for n in WEIGHTS if n not in BIG)
PACK_COLS = 1024
PACK_ROWS = 16


def _pack(named):
    flat = jnp.concatenate([named[n].reshape(-1) for n in SMALL])
    return jnp.pad(flat, (0, PACK_ROWS * PACK_COLS - flat.shape[0])).reshape(PACK_ROWS, PACK_COLS)


def _unpack(buf, shapes):
    flat = buf.reshape(-1)
    out, at = {}, 0
    for n in SMALL:
        size = int(np.prod(shapes[n]))
        out[n] = flat[at:at + size].reshape(shapes[n])
        at += size
    return out


class _Exchange:
    GROUPS = {"A": (("w_up", 0), ("w_down", 0)), "B": (("w_in", 1), ("w_out", 1)), "C": (("w_up", 1), ("w_down", 1))}
    RELAY_AT = {("pre_out", 0): "A", ("pre_mlp", 0): "B", ("mid", 1): "C"}
    NEXT_GROUP = {"A": "B", "B": "C"}

    def __init__(self, wts, k_me, kc_arr):
        self.wts, self.k_me, self.kc_arr = wts, k_me, kc_arr
        self.own = {(n, l): wts[n][l].astype(bf16) for n in BIG for l in range(DEPTH)}
        now = gather_now([self.own["w_in", 0], self.own["w_out", 0]], wts["conv_w"])
        self.ready = {("w_in", 0): self._fill(now[0], self.own["w_in", 0]),
                      ("w_out", 0): self._fill(now[1], self.own["w_out", 0])}
        conv = self._fill(now[2], wts["conv_w"])
        self.conv_w = jnp.transpose(conv, (1, 2, 0, 3)).reshape(DEPTH, CONV_WIDTH, D_CONV)
        self.ici, self.relay = {}, {}
        self.gview, self.scatter, self.share, self.reduced = {}, [], [], {}
        self._start_ici("A", now[2])

    def _start_ici(self, g, after):
        srcs = [self.own[p] for p in self.GROUPS[g]]
        lands = [lax.empty((N_CHIPS,) + s.shape, s.dtype) for s in srcs]
        self.ici[g] = split_start("gather%s_ici_start" % g, _gather_maker("ici", len(srcs)), 3 * len(srcs), srcs + lands,
                                  after)
        return self.ici[g]["token"]

    def _fill(self, land, own):
        return lax.dynamic_update_slice(land, own[None], (self.k_me,) + (0,) * own.ndim)

    def stage(self, name, after):
        if name == ("begin", 0):
            return self.ici["A"]["token"]
        g = self.RELAY_AT.get(name)
        if g is None:
            return 0.0
        n = len(self.GROUPS[g])
        lands = split_wait(self.ici[g], after)[n:]
        self.relay[g] = split_start("gather%s_relay_start" % g, _gather_maker("relay", 0), 3 * n, lands, after)
        tok = self.relay[g]["token"]
        if g in self.NEXT_GROUP:
            tok = tok + self._start_ici(self.NEXT_GROUP[g], after)
        return tok

    def _get(self, piece, after):
        if piece not in self.ready:
            g = [k for k, pieces in self.GROUPS.items() if piece in pieces][0]
            lands = split_wait(self.relay[g], after)
            for p, land in zip(self.GROUPS[g], lands):
                self.ready[p] = self._fill(land, self.own[p])
        return self.ready[piece]

    def w_in(self, l, after):
        g = self._get(("w_in", l), after)
        return _to_aligned(jnp.transpose(g, (1, 0, 2)).reshape(D_MODEL, D_IN))

    def w_out(self, l, after):
        return self._get(("w_out", l), after).reshape(D_MODEL, D_MODEL)

    def mlp(self, l, after):
        return self._get(("w_up", l), after), self._get(("w_down", l), after)

    def _view(self, n, g):
        if n == "w_in":
            g = jnp.transpose(_from_aligned(g).reshape(D_MODEL, N_CHIPS, D_IN // N_CHIPS), (1, 0, 2))
        _, rows, cols = self.wts[n].shape
        return g.reshape(N_CHIPS, 2, rows // 2, cols)

    def grads(self, name, arrays, after):
        pieces = [(n, name[1]) for n in arrays]
        views = [self._view(n, g) for n, g in arrays.items()]
        self.gview.update(zip(pieces, views))
        lands = [lax.empty((7,) + v.shape[2:], bf16) for v in views]
        h = split_start("scatter_%s%d_start" % name, _scatter_maker(len(views)), 7 * len(views),
                        [v.astype(bf16) for v in views] + lands, after)
        tok = h["token"] + self._advance(after)
        self.scatter.append((pieces, h))
        return tok

    def _take_share(self, after):
        pieces, h = self.share.pop(0)
        self.reduced.update(zip(pieces, split_wait(h, after)))

    def _take_scatter(self, after):
        pieces, h = self.scatter.pop(0)
        lands = split_wait(h, after)[len(pieces):]
        sums = [piece_sum(self.gview[p], land, self.kc_arr) for p, land in zip(pieces, lands)]
        hs = split_start(h["name"].replace("scatter", "share"), _share_maker(len(sums)), len(sums), sums, after)
        self.share.append((pieces, hs))
        return hs["token"]

    def _advance(self, after):
        if self.share:
            self._take_share(after)
        return self._take_scatter(after) if self.scatter else 0.0

    def reduced_grads(self, names, after):
        want = [(n, l) for n in names for l in range(DEPTH)]
        while not all(p in self.reduced for p in want):
            if any(p in pieces for p in want for pieces, _ in self.share):
                self._take_share(after)
            else:
                self._take_scatter(after)
        return {n: [self.reduced[n, l] for l in range(DEPTH)] for n in names}


def kernel(x, mix_norm_g, w_in, q_gain, k_gain, sinks, rel_bias, conv_w, conv_b, dt_bias, a_log, d_skip, ssm_norm_g, w_out, mlp_norm_g, w_up, w_down, loss_target, m_mix_norm_g, m_w_in, m_q_gain, m_k_gain, m_sinks, m_rel_bias, m_conv_w, m_conv_b, m_dt_bias, m_a_log, m_d_skip, m_ssm_norm_g, m_w_out, m_mlp_norm_g, m_w_up, m_w_down, v_mix_norm_g, v_w_in, v_q_gain, v_k_gain, v_sinks, v_rel_bias, v_conv_w, v_conv_b, v_dt_bias, v_a_log, v_d_skip, v_ssm_norm_g, v_w_out, v_mlp_norm_g, v_w_up, v_w_down):
    wts = dict(mix_norm_g=mix_norm_g, w_in=w_in, q_gain=q_gain, k_gain=k_gain, sinks=sinks, rel_bias=rel_bias, conv_w=conv_w,
               conv_b=conv_b, dt_bias=dt_bias, a_log=a_log, d_skip=d_skip, ssm_norm_g=ssm_norm_g, w_out=w_out,
               mlp_norm_g=mlp_norm_g, w_up=w_up, w_down=w_down)
    mom = dict(mix_norm_g=m_mix_norm_g, w_in=m_w_in, q_gain=m_q_gain, k_gain=m_k_gain, sinks=m_sinks, rel_bias=m_rel_bias,
               conv_w=m_conv_w, conv_b=m_conv_b, dt_bias=m_dt_bias, a_log=m_a_log, d_skip=m_d_skip, ssm_norm_g=m_ssm_norm_g,
               w_out=m_w_out, mlp_norm_g=m_mlp_norm_g, w_up=m_w_up, w_down=m_w_down)
    var = dict(mix_norm_g=v_mix_norm_g, w_in=v_w_in, q_gain=v_q_gain, k_gain=v_k_gain, sinks=v_sinks, rel_bias=v_rel_bias,
               conv_w=v_conv_w, conv_b=v_conv_b, dt_bias=v_dt_bias, a_log=v_a_log, d_skip=v_d_skip, ssm_norm_g=v_ssm_norm_g,
               w_out=v_w_out, mlp_norm_g=v_mlp_norm_g, w_up=v_w_up, w_down=v_w_down)
    xi, yi, ci = _coords()
    k_me = 2 * xi + yi
    kc_arr = jnp.stack([k_me, ci]).astype(jnp.int32)

    prov = _Exchange(wts, k_me, kc_arr)
    small_w = {n: wts[n] for n in SMALL}
    small_w["conv_w"] = prov.conv_w
    loss, dx, grads, tok = local_step(x[0], loss_target[0], small_w, prov)
    loss = lax.psum(loss[0, 0], ("x", "y", "c"))

    small_shapes = {n: grads[n].shape for n in SMALL}
    small = _unpack(small_all_reduce(_pack(grads) + tok), small_shapes)
    cols = conv_w.shape[-1]
    small["conv_w"] = lax.dynamic_slice_in_dim(small["conv_w"], k_me * cols, cols, axis=2)
    g_out_d, d_out_d, m_out_d, v_out_d = {}, {}, {}, {}
    shard_shapes = {n: wts[n].shape for n in SMALL}
    d, nm, nv = adamw_small(_pack(wts), _pack(small), _pack(mom), _pack(var))
    for dst, buf in ((d_out_d, d), (m_out_d, nm), (v_out_d, nv)):
        dst.update(_unpack(buf, shard_shapes))
    g_out_d.update(small)

    after = d
    for names in (("w_up", "w_down"), ("w_in", "w_out")):
        for n, (g0, g1) in prov.reduced_grads(names, after).items():
            g_out_d[n], d_out_d[n], m_out_d[n], v_out_d[n] = adamw_shard(wts[n], g0, g1, mom[n], var[n])
            after = d_out_d[n]

    return (loss, dx[None], *[g_out_d[n] for n in WEIGHTS], *[d_out_d[n] for n in WEIGHTS],
            *[m_out_d[n] for n in WEIGHTS], *[v_out_d[n] for n in WEIGHTS])
```

```python
import functools

import numpy as np
import jax
import jax.numpy as jnp
from jax import lax
from jax.experimental import pallas as pl
from jax.experimental.pallas import tpu as pltpu

f32 = jnp.float32
bf16 = jnp.bfloat16

SEQ = 2048
D_MODEL = 1024
DEPTH = 2
HEAD_DIM = 64
N_Q_HEADS = 8
N_KV_HEADS = 2
Q_PER_KV = N_Q_HEADS // N_KV_HEADS
BLOCK = 128
N_BLOCKS = SEQ // BLOCK
N_BUCKETS = 32
MAX_DISTANCE = 128
SSM_HEADS = 8
SSM_HEAD_DIM = 64
SSM_GROUPS = 2
HEADS_PER_GROUP = SSM_HEADS // SSM_GROUPS
SSM_STATE = 128
CONV_WIDTH = 4
CHUNK = 128
N_CHUNKS = SEQ // CHUNK
D_FF = 4 * D_MODEL
D_ATTN = N_Q_HEADS * HEAD_DIM
D_KV = N_KV_HEADS * HEAD_DIM
D_SSM = SSM_HEADS * SSM_HEAD_DIM
D_BC = SSM_GROUPS * SSM_STATE
D_CONV = D_SSM + 2 * D_BC
D_IN = D_ATTN + 2 * D_KV + D_SSM + D_CONV + SSM_HEADS
EPS = 1e-6
NEG = -1e30
N_CHIPS = 4
FF_TILE = D_FF // N_CHIPS

LANE = 128
PW = D_ATTN + D_SSM + D_CONV + 2 * D_KV + LANE
OFF_Q, OFF_Z, OFF_X, OFF_K, OFF_V, OFF_DT = 0, 512, 1024, 2048, 2176, 2304

ADAM_LR = 0.001
ADAM_B1 = 0.9
ADAM_B2 = 0.999
ADAM_EPS = 1e-08
ADAM_WD = 0.01
ADAM_STEP = 10

VMEM_LIMIT = 56 * 1024 * 1024


def _params(*sem):
    return pltpu.CompilerParams(dimension_semantics=tuple(sem), vmem_limit_bytes=VMEM_LIMIT)


def _bdot(a, b):
    return jnp.dot(a.astype(bf16), b.astype(bf16), preferred_element_type=f32)


def _bdot_nt(a, b):
    return lax.dot_general(a.astype(bf16), b.astype(bf16), (((1,), (1,)), ((), ())), preferred_element_type=f32)


def _bdot_tn(a, b):
    return lax.dot_general(a.astype(bf16), b.astype(bf16), (((0,), (0,)), ((), ())), preferred_element_type=f32)


def _hdot(a, b):
    return jnp.dot(a, b, precision=lax.Precision.HIGHEST, preferred_element_type=f32)


def _sigmoid(x):
    return 1.0 / (1.0 + jnp.exp(-x))


def _softplus(x):
    return jnp.maximum(x, 0.0) + jnp.log1p(jnp.exp(-jnp.abs(x)))


def _rms(x):
    return lax.rsqrt(jnp.mean(x * x, axis=-1, keepdims=True) + EPS)


def _rms_bwd(dy, xhat, r, g):
    t = dy * g
    return r * (t - xhat * jnp.mean(t * xhat, axis=-1, keepdims=True))


def _full(shape):
    return pl.BlockSpec(shape, lambda *_: (0,) * len(shape))


def _to_aligned(w):
    q, k, v, z, xbc, dt = jnp.split(w, [512, 640, 768, 1280, 2304], axis=-1)
    pad = jnp.zeros(w.shape[:-1] + (LANE - SSM_HEADS,), w.dtype)
    return jnp.concatenate([q, z, xbc, k, v, dt, pad], axis=-1)


def _from_aligned(w):
    q, z, xbc, k, v, dt = (w[..., OFF_Q:OFF_Z], w[..., OFF_Z:OFF_X], w[..., OFF_X:OFF_K], w[..., OFF_K:OFF_V],
                           w[..., OFF_V:OFF_DT], w[..., OFF_DT:OFF_DT + SSM_HEADS])
    return jnp.concatenate([q, k, v, z, xbc, dt], axis=-1)


def _bucket_table():
    qi = np.arange(BLOCK)[:, None]
    kj = np.arange(2 * BLOCK)[None, :]
    dist = qi + BLOCK - kj
    ok = (dist >= 0) & (dist < 128)
    d = np.clip(dist, 0, None)
    max_exact = N_BUCKETS // 2
    d_f = np.maximum(d, 1).astype(np.float32)
    large = max_exact + (np.log(d_f / np.float32(max_exact)) / np.float32(np.log(MAX_DISTANCE / max_exact))
                         * np.float32(N_BUCKETS - max_exact)).astype(np.int32)
    large = np.minimum(large, N_BUCKETS - 1)
    bucket = np.where(d < max_exact, d, large)
    return np.where(ok, bucket, -1).astype(np.int32)


def bias_build(rel_bias, bucket):
    def body(rel_ref, bkt_ref, o_ref):
        bkt = bkt_ref[...]
        for h in range(N_Q_HEADS):
            acc = jnp.where(bkt < 0, NEG, 0.0).astype(f32)
            for b in range(N_BUCKETS):
                acc = acc + jnp.where(bkt == b, rel_ref[b, h], 0.0)
            o_ref[h] = acc

    return pl.pallas_call(
        body, name="bias_build", out_shape=jax.ShapeDtypeStruct((N_Q_HEADS, BLOCK, 2 * BLOCK), f32),
        in_specs=[pl.BlockSpec(memory_space=pltpu.SMEM), pl.BlockSpec(memory_space=pltpu.VMEM)],
        out_specs=pl.BlockSpec(memory_space=pltpu.VMEM),
    )(rel_bias, bucket)


def bias_bwd(dband0, dband1, bucket):
    def body(d0_ref, d1_ref, bkt_ref, o_ref):
        bkt = bkt_ref[...]
        o_ref[...] = jnp.zeros_like(o_ref)
        for h in range(N_Q_HEADS):
            d = d0_ref[h] + d1_ref[h]
            for b in range(N_BUCKETS):
                part = jnp.sum(jnp.where(bkt == b, d, 0.0), axis=1, keepdims=True)
                o_ref[b:b + 1, h:h + 1] = jnp.sum(part, axis=0, keepdims=True)

    return pl.pallas_call(
        body, name="bias_bwd", out_shape=jax.ShapeDtypeStruct((N_BUCKETS, LANE), f32),
    )(dband0, dband1, bucket)


def in_fwd(x, g, w, tm=256):
    def body(x_ref, g_ref, w_ref, o_ref):
        xv = x_ref[...]
        h = xv * _rms(xv) * g_ref[...]
        o_ref[...] = _bdot(h, w_ref[...])

    return pl.pallas_call(
        body, name="in_fwd", grid=(SEQ // tm,),
        in_specs=[pl.BlockSpec((tm, D_MODEL), lambda i: (i, 0)), _full((1, D_MODEL)), _full((D_MODEL, PW))],
        out_specs=pl.BlockSpec((tm, PW), lambda i: (i, 0)),
        out_shape=jax.ShapeDtypeStruct((SEQ, PW), f32),
        compiler_params=_params("arbitrary"),
    )(x, g, w)


def in_bwd(dq, dz, dxbc, dk, dv, ddt, x, g, w, dres, tm=256):
    def body(dq_ref, dz_ref, dx_ref, dk_ref, dv_ref, ddt_ref, x_ref, g_ref, w_ref, dres_ref, o_ref, dw_ref, dg_ref):
        i = pl.program_id(0)

        @pl.when(i == 0)
        def _():
            dw_ref[...] = jnp.zeros_like(dw_ref)
            dg_ref[...] = jnp.zeros_like(dg_ref)

        dproj = jnp.concatenate([dq_ref[...], dz_ref[...], dx_ref[...], dk_ref[...], dv_ref[...], ddt_ref[...]],
                                axis=-1).astype(bf16)
        xv = x_ref[...]
        r = _rms(xv)
        xhat = xv * r
        gv = g_ref[...]
        h = xhat * gv
        dw_ref[...] += _bdot_tn(h, dproj)
        dh = _bdot_nt(dproj, w_ref[...])
        dg_ref[...] += jnp.sum(dh * xhat, axis=0, keepdims=True)
        o_ref[...] = dres_ref[...] + _rms_bwd(dh, xhat, r, gv)

    tok = lambda w_: pl.BlockSpec((tm, w_), lambda i: (i, 0))
    return pl.pallas_call(
        body, name="in_bwd", grid=(SEQ // tm,),
        in_specs=[tok(D_ATTN), tok(D_SSM), tok(D_CONV), tok(D_KV // 1), tok(D_KV // 1), tok(LANE), tok(D_MODEL),
                  _full((1, D_MODEL)), _full((D_MODEL, PW)), tok(D_MODEL)],
        out_specs=[tok(D_MODEL), _full((D_MODEL, PW)), _full((1, D_MODEL))],
        out_shape=[jax.ShapeDtypeStruct((SEQ, D_MODEL), f32), jax.ShapeDtypeStruct((D_MODEL, PW), f32),
                   jax.ShapeDtypeStruct((1, D_MODEL), f32)],
        compiler_params=_params("arbitrary"),
    )(dq, dz, dxbc, dk, dv, ddt, x, g, w, dres)


def _attn_probs(qn, kn, bias_h, sink, first, col):
    s = _bdot_nt(qn, kn) * (HEAD_DIM ** -0.5) + bias_h
    s = jnp.where(jnp.logical_and(first, col < BLOCK), NEG, s)
    m = jnp.maximum(jnp.max(s, axis=-1, keepdims=True), sink)
    p = jnp.exp(s - m)
    psink = jnp.exp(sink - m)
    inv = 1.0 / (jnp.sum(p, axis=-1, keepdims=True) + psink)
    return p * inv, psink * inv


def attn_fwd(proj, q_gain, k_gain, sinks, bias):
    kcol, vcol = OFF_K // D_KV, OFF_V // D_KV

    def body(q_ref, kc_ref, kp_ref, vc_ref, vp_ref, qg_ref, kg_ref, sink_ref, bias_ref, o_ref):
        n = pl.program_id(0)
        first = n == 0
        col = lax.broadcasted_iota(jnp.int32, (BLOCK, 2 * BLOCK), 1)
        k2 = jnp.concatenate([kp_ref[...], kc_ref[...]], axis=0)
        v2 = jnp.concatenate([vp_ref[...], vc_ref[...]], axis=0)
        qg, kg = qg_ref[...], kg_ref[...]
        for hk in range(N_KV_HEADS):
            kk = k2[:, hk * HEAD_DIM:(hk + 1) * HEAD_DIM]
            kn = (kk * _rms(kk) * kg).astype(bf16)
            vb = v2[:, hk * HEAD_DIM:(hk + 1) * HEAD_DIM].astype(bf16)
            for gq in range(Q_PER_KV):
                h = hk * Q_PER_KV + gq
                qq = q_ref[:, h * HEAD_DIM:(h + 1) * HEAD_DIM]
                qn = qq * _rms(qq) * qg
                p, _ = _attn_probs(qn, kn, bias_ref[h], sink_ref[h], first, col)
                o_ref[:, h * HEAD_DIM:(h + 1) * HEAD_DIM] = _bdot(p, vb)

    prev = lambda n: jnp.maximum(n - 1, 0)
    return pl.pallas_call(
        body, name="attn_fwd", grid=(N_BLOCKS,),
        in_specs=[pl.BlockSpec((BLOCK, D_ATTN), lambda n: (n, 0)),
                  pl.BlockSpec((BLOCK, D_KV), lambda n: (n, kcol)), pl.BlockSpec((BLOCK, D_KV), lambda n: (prev(n), kcol)),
                  pl.BlockSpec((BLOCK, D_KV), lambda n: (n, vcol)), pl.BlockSpec((BLOCK, D_KV), lambda n: (prev(n), vcol)),
                  _full((1, HEAD_DIM)), _full((1, HEAD_DIM)), pl.BlockSpec(memory_space=pltpu.SMEM),
                  _full((N_Q_HEADS, BLOCK, 2 * BLOCK))],
        out_specs=pl.BlockSpec((BLOCK, D_ATTN), lambda n: (n, 0)),
        out_shape=jax.ShapeDtypeStruct((SEQ, D_ATTN), f32),
        compiler_params=_params("arbitrary"),
    )(proj, proj, proj, proj, proj, q_gain, k_gain, sinks, bias)


def attn_bwd(proj, d_out, q_gain, k_gain, sinks, bias):
    kcol, vcol = OFF_K // D_KV, OFF_V // D_KV

    def body(q_ref, kc_ref, kp_ref, vc_ref, vp_ref, do_ref, qg_ref, kg_ref, sink_ref, bias_ref,
             dq_ref, dk_ref, dv_ref, dband_ref, dsink_ref, dqg_ref, dkg_ref, dkn_scr, dv_scr):
        i = pl.program_id(0)
        first = i == N_BLOCKS - 1

        @pl.when(i == 0)
        def _():
            for ref in (dband_ref, dsink_ref, dqg_ref, dkg_ref, dkn_scr, dv_scr):
                ref[...] = jnp.zeros_like(ref)

        col = lax.broadcasted_iota(jnp.int32, (BLOCK, 2 * BLOCK), 1)
        k2 = jnp.concatenate([kp_ref[...], kc_ref[...]], axis=0)
        v2 = jnp.concatenate([vp_ref[...], vc_ref[...]], axis=0)
        qg, kg = qg_ref[...], kg_ref[...]
        scale = HEAD_DIM ** -0.5
        for hk in range(N_KV_HEADS):
            sl = slice(hk * HEAD_DIM, (hk + 1) * HEAD_DIM)
            kk = k2[:, sl]
            rk = _rms(kk)
            khat = kk * rk
            kn = (khat * kg).astype(bf16)
            vb = v2[:, sl].astype(bf16)
            dkn = jnp.zeros((2 * BLOCK, HEAD_DIM), f32)
            dvv = jnp.zeros((2 * BLOCK, HEAD_DIM), f32)
            for gq in range(Q_PER_KV):
                h = hk * Q_PER_KV + gq
                hs = slice(h * HEAD_DIM, (h + 1) * HEAD_DIM)
                qq = q_ref[:, hs]
                rq = _rms(qq)
                qhat = qq * rq
                qn = qhat * qg
                p, psink = _attn_probs(qn, kn, bias_ref[h], sink_ref[h], first, col)
                d_o = do_ref[:, hs]
                dp = _bdot_nt(d_o, vb)
                delta = jnp.sum(p * dp, axis=-1, keepdims=True)
                ds = p * (dp - delta)
                dband_ref[h] += ds
                dsink_ref[:, h:h + 1] += -jnp.sum(psink * delta, axis=0, keepdims=True)
                dqn = _bdot(ds, kn) * scale
                dkn = dkn + _bdot_tn(ds, qn) * scale
                dvv = dvv + _bdot_tn(p, d_o)
                dqg_ref[...] += jnp.sum(dqn * qhat, axis=0, keepdims=True)
                dq_ref[:, hs] = _rms_bwd(dqn, qhat, rq, qg)
            dkn_cur = dkn[BLOCK:] + dkn_scr[:, sl]
            dkn_scr[:, sl] = dkn[:BLOCK]
            khat_c, rk_c = khat[BLOCK:], rk[BLOCK:]
            dkg_ref[...] += jnp.sum(dkn_cur * khat_c, axis=0, keepdims=True)
            dk_ref[:, sl] = _rms_bwd(dkn_cur, khat_c, rk_c, kg)
            dv_ref[:, sl] = dvv[BLOCK:] + dv_scr[:, sl]
            dv_scr[:, sl] = dvv[:BLOCK]

    blk = lambda i: N_BLOCKS - 1 - i
    prev = lambda i: jnp.maximum(N_BLOCKS - 2 - i, 0)
    return pl.pallas_call(
        body, name="attn_bwd", grid=(N_BLOCKS,),
        in_specs=[pl.BlockSpec((BLOCK, D_ATTN), lambda i: (blk(i), 0)),
                  pl.BlockSpec((BLOCK, D_KV), lambda i: (blk(i), kcol)), pl.BlockSpec((BLOCK, D_KV), lambda i: (prev(i), kcol)),
                  pl.BlockSpec((BLOCK, D_KV), lambda i: (blk(i), vcol)), pl.BlockSpec((BLOCK, D_KV), lambda i: (prev(i), vcol)),
                  pl.BlockSpec((BLOCK, D_ATTN), lambda i: (blk(i), 0)),
                  _full((1, HEAD_DIM)), _full((1, HEAD_DIM)), pl.BlockSpec(memory_space=pltpu.SMEM),
                  _full((N_Q_HEADS, BLOCK, 2 * BLOCK))],
        out_specs=[pl.BlockSpec((BLOCK, D_ATTN), lambda i: (blk(i), 0)), pl.BlockSpec((BLOCK, D_KV), lambda i: (blk(i), 0)),
                   pl.BlockSpec((BLOCK, D_KV), lambda i: (blk(i), 0)), _full((N_Q_HEADS, BLOCK, 2 * BLOCK)),
                   _full((1, LANE)), _full((1, HEAD_DIM)), _full((1, HEAD_DIM))],
        out_shape=[jax.ShapeDtypeStruct((SEQ, D_ATTN), f32), jax.ShapeDtypeStruct((SEQ, D_KV), f32),
                   jax.ShapeDtypeStruct((SEQ, D_KV), f32), jax.ShapeDtypeStruct((N_Q_HEADS, BLOCK, 2 * BLOCK), f32),
                   jax.ShapeDtypeStruct((1, LANE), f32), jax.ShapeDtypeStruct((1, HEAD_DIM), f32),
                   jax.ShapeDtypeStruct((1, HEAD_DIM), f32)],
        scratch_shapes=[pltpu.VMEM((BLOCK, D_KV), f32), pltpu.VMEM((BLOCK, D_KV), f32)],
        compiler_params=_params("arbitrary"),
    )(proj, proj, proj, proj, proj, d_out, q_gain, k_gain, sinks, bias)


def _shift_down(u, s, row):
    if s == 0:
        return u
    return jnp.where(row >= s, pltpu.roll(u, s, 0), 0.0)


def _shift_up(u, s, row):
    if s == 0:
        return u
    return jnp.where(row < SEQ - s, pltpu.roll(u, SEQ - s, 0), 0.0)


def conv_fwd(proj, conv_w, conv_b):
    xcol = OFF_X // LANE

    def body(u_ref, w_ref, b_ref, o_ref):
        u = u_ref[...]
        row = lax.broadcasted_iota(jnp.int32, u.shape, 0)
        pre = b_ref[...] + jnp.zeros_like(u)
        for k in range(CONV_WIDTH):
            pre = pre + w_ref[k:k + 1, :] * _shift_down(u, CONV_WIDTH - 1 - k, row)
        o_ref[...] = pre * _sigmoid(pre)

    return pl.pallas_call(
        body, name="conv_fwd", grid=(D_CONV // LANE,),
        in_specs=[pl.BlockSpec((SEQ, LANE), lambda j: (0, xcol + j)), pl.BlockSpec((CONV_WIDTH, LANE), lambda j: (0, j)),
                  pl.BlockSpec((1, LANE), lambda j: (0, j))],
        out_specs=pl.BlockSpec((SEQ, LANE), lambda j: (0, j)),
        out_shape=jax.ShapeDtypeStruct((SEQ, D_CONV), f32),
        compiler_params=_params("arbitrary"),
    )(proj, conv_w, conv_b)


def conv_bwd(proj, d_act, conv_w, conv_b):
    xcol = OFF_X // LANE

    def body(u_ref, da_ref, w_ref, b_ref, du_ref, dw_ref, db_ref):
        u = u_ref[...]
        row = lax.broadcasted_iota(jnp.int32, u.shape, 0)
        shifted = [_shift_down(u, CONV_WIDTH - 1 - k, row) for k in range(CONV_WIDTH)]
        pre = b_ref[...] + jnp.zeros_like(u)
        for k in range(CONV_WIDTH):
            pre = pre + w_ref[k:k + 1, :] * shifted[k]
        sg = _sigmoid(pre)
        dpre = da_ref[...] * (sg * (1.0 + pre * (1.0 - sg)))
        db_ref[...] = jnp.sum(dpre, axis=0, keepdims=True)
        du = jnp.zeros_like(u)
        for k in range(CONV_WIDTH):
            dw_ref[k:k + 1, :] = jnp.sum(dpre * shifted[k], axis=0, keepdims=True)
            du = du + w_ref[k:k + 1, :] * _shift_up(dpre, CONV_WIDTH - 1 - k, row)
        du_ref[...] = du

    return pl.pallas_call(
        body, name="conv_bwd", grid=(D_CONV // LANE,),
        in_specs=[pl.BlockSpec((SEQ, LANE), lambda j: (0, xcol + j)), pl.BlockSpec((SEQ, LANE), lambda j: (0, j)),
                  pl.BlockSpec((CONV_WIDTH, LANE), lambda j: (0, j)), pl.BlockSpec((1, LANE), lambda j: (0, j))],
        out_specs=[pl.BlockSpec((SEQ, LANE), lambda j: (0, j)), pl.BlockSpec((CONV_WIDTH, LANE), lambda j: (0, j)),
                   pl.BlockSpec((1, LANE), lambda j: (0, j))],
        out_shape=[jax.ShapeDtypeStruct((SEQ, D_CONV), f32), jax.ShapeDtypeStruct((CONV_WIDTH, D_CONV), f32),
                   jax.ShapeDtypeStruct((1, D_CONV), f32)],
        compiler_params=_params("arbitrary"),
    )(proj, d_act, conv_w, conv_b)


def _ssd_chunk_common(dt_raw, dtb, alog):
    row = lax.broadcasted_iota(jnp.int32, (CHUNK, CHUNK), 0)
    col = lax.broadcasted_iota(jnp.int32, (CHUNK, CHUNK), 1)
    tri = (row >= col).astype(f32)
    strict = (row > col).astype(f32)
    dtp = _softplus(dt_raw + dtb)
    a_row = -jnp.exp(alog)
    d_a = dtp * a_row
    cs = _hdot(tri, d_a)
    cs_last = cs[CHUNK - 1:CHUNK, :]
    return row, col, dtp, a_row, cs, cs.T, cs_last


def _seg_decay(cs, cs_t, hd, row, col):
    seg = cs[:, hd:hd + 1] - cs_t[hd:hd + 1, :]
    return jnp.where(row >= col, jnp.exp(seg), 0.0)


def ssd_fwd(act, proj, dt_bias, a_log, d_skip, norm_g):
    zcol, dtcol = OFF_Z // D_SSM, OFF_DT // LANE
    gw = D_SSM // SSM_GROUPS

    def body(act_ref, z_ref, dt_ref, dtb_ref, alog_ref, dsk_ref, ng_ref, out_ref, ypre_ref, st_ref, state, ybuf):
        c = pl.program_id(0)

        @pl.when(c == 0)
        def _():
            state[...] = jnp.zeros_like(state)

        row, col, dtp, a_row, cs, cs_t, cs_last = _ssd_chunk_common(dt_ref[...], dtb_ref[...], alog_ref[...])
        e_cs = jnp.exp(cs)
        dte = jnp.exp(cs_last - cs)
        ecl = jnp.exp(cs_last)
        dsk = dsk_ref[...]
        for g in range(SSM_GROUPS):
            bg = act_ref[:, D_SSM + g * SSM_STATE:D_SSM + (g + 1) * SSM_STATE]
            cg = act_ref[:, D_SSM + D_BC + g * SSM_STATE:D_SSM + D_BC + (g + 1) * SSM_STATE]
            cb = _bdot_nt(cg, bg)
            for r in range(HEADS_PER_GROUP):
                hd = g * HEADS_PER_GROUP + r
                hs = slice(hd * SSM_HEAD_DIM, (hd + 1) * SSM_HEAD_DIM)
                hl = slice(hd, hd + 1)
                x_h = act_ref[:, hs]
                xdt = x_h * dtp[:, hl]
                lm = _seg_decay(cs, cs_t, hd, row, col)
                prev = state[hd]
                st_ref[0, hd] = prev
                y = _bdot(cb * lm, xdt) + e_cs[:, hl] * _bdot(cg, prev) + x_h * dsk[:, hl]
                ybuf[:, hs] = y
                state[hd] = prev * ecl[:, hl] + _bdot_tn(bg, xdt * dte[:, hl])
        y = ybuf[...]
        ypre_ref[...] = y
        z = z_ref[...]
        yz = y * (z * _sigmoid(z))
        ng = ng_ref[...]
        for g in range(SSM_GROUPS):
            gs = slice(g * gw, (g + 1) * gw)
            part = yz[:, gs]
            out_ref[:, gs] = part * _rms(part) * ng[:, gs]

    return pl.pallas_call(
        body, name="ssd_fwd", grid=(N_CHUNKS,),
        in_specs=[pl.BlockSpec((CHUNK, D_CONV), lambda c: (c, 0)), pl.BlockSpec((CHUNK, D_SSM), lambda c: (c, zcol)),
                  pl.BlockSpec((CHUNK, LANE), lambda c: (c, dtcol)), _full((1, LANE)), _full((1, LANE)), _full((1, LANE)),
                  _full((1, D_SSM))],
        out_specs=[pl.BlockSpec((CHUNK, D_SSM), lambda c: (c, 0)), pl.BlockSpec((CHUNK, D_SSM), lambda c: (c, 0)),
                   pl.BlockSpec((1, SSM_HEADS, SSM_STATE, SSM_HEAD_DIM), lambda c: (c, 0, 0, 0))],
        out_shape=[jax.ShapeDtypeStruct((SEQ, D_SSM), f32), jax.ShapeDtypeStruct((SEQ, D_SSM), f32),
                   jax.ShapeDtypeStruct((N_CHUNKS, SSM_HEADS, SSM_STATE, SSM_HEAD_DIM), f32)],
        scratch_shapes=[pltpu.VMEM((SSM_HEADS, SSM_STATE, SSM_HEAD_DIM), f32), pltpu.VMEM((CHUNK, D_SSM), f32)],
        compiler_params=_params("arbitrary"),
    )(act, proj, proj, dt_bias, a_log, d_skip, norm_g)


def ssd_bwd(act, proj, ypre, states, d_out, dt_bias, a_log, d_skip, norm_g):
    zcol, dtcol = OFF_Z // D_SSM, OFF_DT // LANE
    gw = D_SSM // SSM_GROUPS

    def body(act_ref, z_ref, dt_ref, ypre_ref, st_ref, do_ref, dtb_ref, alog_ref, dsk_ref, ng_ref,
             dact_ref, ddt_ref, dz_ref, dng_ref, dpar_ref, dstate, dybuf):
        i = pl.program_id(0)

        @pl.when(i == 0)
        def _():
            for ref in (dng_ref, dpar_ref, dstate):
                ref[...] = jnp.zeros_like(ref)

        y = ypre_ref[...]
        z = z_ref[...]
        sgz = _sigmoid(z)
        sz = z * sgz
        yz = y * sz
        ng = ng_ref[...]
        d_o = do_ref[...]
        for g in range(SSM_GROUPS):
            gs = slice(g * gw, (g + 1) * gw)
            part = yz[:, gs]
            r = _rms(part)
            yhat = part * r
            dng_ref[:, gs] += jnp.sum(d_o[:, gs] * yhat, axis=0, keepdims=True)
            dyz = _rms_bwd(d_o[:, gs], yhat, r, ng[:, gs])
            dybuf[:, gs] = dyz * sz[:, gs]
            dz_ref[:, gs] = dyz * y[:, gs] * (sgz[:, gs] * (1.0 + z[:, gs] * (1.0 - sgz[:, gs])))

        row, col, dtp, a_row, cs, cs_t, cs_last = _ssd_chunk_common(dt_ref[...], dtb_ref[...], alog_ref[...])
        upper = (row <= col).astype(f32)
        lane = lax.broadcasted_iota(jnp.int32, (CHUNK, LANE), 1)
        lane1 = lax.broadcasted_iota(jnp.int32, (1, LANE), 1)
        e_cs = jnp.exp(cs)
        dte = jnp.exp(cs_last - cs)
        ecl = jnp.exp(cs_last)
        dsk = dsk_ref[...]
        ddt_mat = jnp.zeros((CHUNK, LANE), f32)
        dcs_mat = jnp.zeros((CHUNK, LANE), f32)
        dcs_t = jnp.zeros((LANE, CHUNK), f32)
        dcsl_row = jnp.zeros((1, LANE), f32)
        dd_row = jnp.zeros((1, LANE), f32)
        for g in range(SSM_GROUPS):
            bsl = slice(D_SSM + g * SSM_STATE, D_SSM + (g + 1) * SSM_STATE)
            csl = slice(D_SSM + D_BC + g * SSM_STATE, D_SSM + D_BC + (g + 1) * SSM_STATE)
            bg = act_ref[:, bsl]
            cg = act_ref[:, csl]
            cb = _bdot_nt(cg, bg)
            dcb = jnp.zeros((CHUNK, CHUNK), f32)
            dbg = jnp.zeros((CHUNK, SSM_STATE), f32)
            dcg = jnp.zeros((CHUNK, SSM_STATE), f32)
            for rr in range(HEADS_PER_GROUP):
                hd = g * HEADS_PER_GROUP + rr
                hs = slice(hd * SSM_HEAD_DIM, (hd + 1) * SSM_HEAD_DIM)
                hl = slice(hd, hd + 1)
                x_h = act_ref[:, hs]
                dt_h = dtp[:, hl]
                e_h = e_cs[:, hl]
                dte_h = dte[:, hl]
                ecl_h = ecl[:, hl]
                xdt = x_h * dt_h
                lm = _seg_decay(cs, cs_t, hd, row, col)
                m = cb * lm
                prev = st_ref[0, hd]
                dy = dybuf[:, hs]
                dh = dstate[hd]
                dd_row = dd_row + jnp.where(lane1 == hd, jnp.sum(jnp.sum(dy * x_h, axis=1, keepdims=True), axis=0, keepdims=True), 0.0)
                dx = dy * dsk[:, hl]
                gmat = _bdot(cg, prev)
                dg = dy * e_h
                dcg = dcg + _bdot_nt(dg, prev)
                dprev = _bdot_tn(cg, dg)
                dcs_h = jnp.sum(dy * gmat, axis=1, keepdims=True) * e_h
                dm = _bdot_nt(dy, xdt)
                dxdt = _bdot_tn(m, dy)
                dcb = dcb + dm * lm
                dseg = dm * m
                dcs_h = dcs_h + jnp.sum(dseg, axis=1, keepdims=True)
                dcs_t = jnp.where(row == hd, jnp.sum(dseg, axis=0, keepdims=True), dcs_t)
                wmat = xdt * dte_h
                dbg = dbg + _bdot_nt(wmat, dh)
                dw = _bdot(bg, dh)
                dxdt = dxdt + dw * dte_h
                ddte = jnp.sum(dw * xdt, axis=1, keepdims=True) * dte_h
                dcs_h = dcs_h - ddte
                dcsl = jnp.sum(ddte, axis=0, keepdims=True)
                dcsl = dcsl + jnp.sum(jnp.sum(dh * prev, axis=1, keepdims=True), axis=0, keepdims=True) * ecl_h
                dstate[hd] = dprev + dh * ecl_h
                dact_ref[:, hs] = dx + dxdt * dt_h
                ddt_h = jnp.sum(dxdt * x_h, axis=1, keepdims=True)
                ddt_mat = jnp.where(lane == hd, ddt_h, ddt_mat)
                dcs_mat = jnp.where(lane == hd, dcs_h, dcs_mat)
                dcsl_row = jnp.where(lane1 == hd, dcsl, dcsl_row)
            dact_ref[:, bsl] = dbg + _bdot_tn(dcb, cg)
            dact_ref[:, csl] = dcg + _bdot(dcb, bg)
        rowl = lax.broadcasted_iota(jnp.int32, (CHUNK, LANE), 0)
        dcs_mat = dcs_mat - dcs_t.T + jnp.where(rowl == CHUNK - 1, dcsl_row, 0.0)
        dda = _hdot(upper, dcs_mat)
        ddt_mat = ddt_mat + dda * a_row
        da_row = jnp.sum(dda * dtp, axis=0, keepdims=True)
        ddt_raw = ddt_mat * _sigmoid(dt_ref[...] + dtb_ref[...])
        ddt_ref[...] = ddt_raw
        dpar_ref[0:1, :] += jnp.sum(ddt_raw, axis=0, keepdims=True)
        dpar_ref[1:2, :] += da_row * a_row
        dpar_ref[2:3, :] += dd_row

    blk = lambda i: N_CHUNKS - 1 - i
    return pl.pallas_call(
        body, name="ssd_bwd", grid=(N_CHUNKS,),
        in_specs=[pl.BlockSpec((CHUNK, D_CONV), lambda i: (blk(i), 0)), pl.BlockSpec((CHUNK, D_SSM), lambda i: (blk(i), zcol)),
                  pl.BlockSpec((CHUNK, LANE), lambda i: (blk(i), dtcol)), pl.BlockSpec((CHUNK, D_SSM), lambda i: (blk(i), 0)),
                  pl.BlockSpec((1, SSM_HEADS, SSM_STATE, SSM_HEAD_DIM), lambda i: (blk(i), 0, 0, 0)),
                  pl.BlockSpec((CHUNK, D_SSM), lambda i: (blk(i), 0)),
                  _full((1, LANE)), _full((1, LANE)), _full((1, LANE)), _full((1, D_SSM))],
        out_specs=[pl.BlockSpec((CHUNK, D_CONV), lambda i: (blk(i), 0)), pl.BlockSpec((CHUNK, LANE), lambda i: (blk(i), 0)),
                   pl.BlockSpec((CHUNK, D_SSM), lambda i: (blk(i), 0)), _full((1, D_SSM)), _full((8, LANE))],
        out_shape=[jax.ShapeDtypeStruct((SEQ, D_CONV), f32), jax.ShapeDtypeStruct((SEQ, LANE), f32),
                   jax.ShapeDtypeStruct((SEQ, D_SSM), f32), jax.ShapeDtypeStruct((1, D_SSM), f32),
                   jax.ShapeDtypeStruct((8, LANE), f32)],
        scratch_shapes=[pltpu.VMEM((SSM_HEADS, SSM_STATE, SSM_HEAD_DIM), f32), pltpu.VMEM((CHUNK, D_SSM), f32)],
        compiler_params=_params("arbitrary"),
    )(act, proj, proj, ypre, states, d_out, dt_bias, a_log, d_skip, norm_g)


def out_fwd(x, attn, ssm, w_out, tm=512):
    def body(x_ref, a_ref, s_ref, w_ref, o_ref):
        o_ref[...] = x_ref[...] + _bdot(a_ref[...], w_ref[:D_ATTN, :]) + _bdot(s_ref[...], w_ref[D_ATTN:, :])

    tok = lambda w_: pl.BlockSpec((tm, w_), lambda i: (i, 0))
    return pl.pallas_call(
        body, name="out_fwd", grid=(SEQ // tm,),
        in_specs=[tok(D_MODEL), tok(D_ATTN), tok(D_SSM), _full((D_MODEL, D_MODEL))],
        out_specs=tok(D_MODEL), out_shape=jax.ShapeDtypeStruct((SEQ, D_MODEL), f32),
        compiler_params=_params("arbitrary"),
    )(x, attn, ssm, w_out)


def out_bwd(dx1, attn, ssm, w_out, tm=512):
    nt = SEQ // tm

    def body(d_ref, a_ref, s_ref, w_ref, da_ref, ds_ref, dw_ref, dw16_ref):
        i = pl.program_id(0)

        @pl.when(i == 0)
        def _():
            dw_ref[...] = jnp.zeros_like(dw_ref)

        d = d_ref[...].astype(bf16)
        dcat = _bdot_nt(d, w_ref[...])
        da_ref[...] = dcat[:, :D_ATTN]
        ds_ref[...] = dcat[:, D_ATTN:]
        dw_ref[:D_ATTN, :] += _bdot_tn(a_ref[...], d)
        dw_ref[D_ATTN:, :] += _bdot_tn(s_ref[...], d)

        @pl.when(i == nt - 1)
        def _():
            dw16_ref[...] = dw_ref[...].astype(bf16)

    tok = lambda w_: pl.BlockSpec((tm, w_), lambda i: (i, 0))
    return pl.pallas_call(
        body, name="out_bwd", grid=(nt,),
        in_specs=[tok(D_MODEL), tok(D_ATTN), tok(D_SSM), _full((D_MODEL, D_MODEL))],
        out_specs=[tok(D_ATTN), tok(D_SSM), _full((D_MODEL, D_MODEL)), _full((D_MODEL, D_MODEL))],
        out_shape=[jax.ShapeDtypeStruct((SEQ, D_ATTN), f32), jax.ShapeDtypeStruct((SEQ, D_SSM), f32),
                   jax.ShapeDtypeStruct((D_MODEL, D_MODEL), f32), jax.ShapeDtypeStruct((D_MODEL, D_MODEL), bf16)],
        compiler_params=_params("arbitrary"),
    )(dx1, attn, ssm, w_out)


def mlp_fwd(x1, g, w_up, w_down, tm=512):
    def body(x_ref, g_ref, wu_ref, wd_ref, o_ref, u_ref, h_scr):
        j = pl.program_id(1)

        @pl.when(j == 0)
        def _():
            xv = x_ref[...]
            h_scr[...] = (xv * _rms(xv) * g_ref[...]).astype(bf16)
            o_ref[...] = xv

        u = jnp.dot(h_scr[...], wu_ref[...], preferred_element_type=f32)
        u_ref[...] = u
        a = jnp.square(jnp.maximum(u, 0.0))
        o_ref[...] += _bdot(a, wd_ref[...])

    return pl.pallas_call(
        body, name="mlp_fwd", grid=(SEQ // tm, N_CHIPS),
        in_specs=[pl.BlockSpec((tm, D_MODEL), lambda i, j: (i, 0)), _full((1, D_MODEL)),
                  pl.BlockSpec((None, D_MODEL, FF_TILE), lambda i, j: (j, 0, 0)),
                  pl.BlockSpec((None, FF_TILE, D_MODEL), lambda i, j: (j, 0, 0))],
        out_specs=[pl.BlockSpec((tm, D_MODEL), lambda i, j: (i, 0)), pl.BlockSpec((tm, FF_TILE), lambda i, j: (i, j))],
        out_shape=[jax.ShapeDtypeStruct((SEQ, D_MODEL), f32), jax.ShapeDtypeStruct((SEQ, D_FF), f32)],
        scratch_shapes=[pltpu.VMEM((tm, D_MODEL), bf16)],
        compiler_params=_params("arbitrary", "arbitrary"),
    )(x1, g, w_up, w_down)


def mlp_bwd_data(dx2, u, x1, g, w_up, w_down, tm=512):
    def body(d_ref, u_ref, x_ref, g_ref, wu_ref, wd_ref, dx_ref, du_ref, dg_ref, dh_scr):
        i, j = pl.program_id(0), pl.program_id(1)

        @pl.when(jnp.logical_and(i == 0, j == 0))
        def _():
            dg_ref[...] = jnp.zeros_like(dg_ref)

        @pl.when(j == 0)
        def _():
            dh_scr[...] = jnp.zeros_like(dh_scr)

        da = _bdot_nt(d_ref[...], wd_ref[...])
        du = (da * (2.0 * jnp.maximum(u_ref[...], 0.0))).astype(bf16)
        du_ref[...] = du
        dh_scr[...] += _bdot_nt(du, wu_ref[...])

        @pl.when(j == N_CHIPS - 1)
        def _():
            xv = x_ref[...]
            r = _rms(xv)
            xhat = xv * r
            dh = dh_scr[...]
            dg_ref[...] += jnp.sum(dh * xhat, axis=0, keepdims=True)
            dx_ref[...] = d_ref[...] + _rms_bwd(dh, xhat, r, g_ref[...])

    return pl.pallas_call(
        body, name="mlp_bwd_data", grid=(SEQ // tm, N_CHIPS),
        in_specs=[pl.BlockSpec((tm, D_MODEL), lambda i, j: (i, 0)), pl.BlockSpec((tm, FF_TILE), lambda i, j: (i, j)),
                  pl.BlockSpec((tm, D_MODEL), lambda i, j: (i, 0)), _full((1, D_MODEL)),
                  pl.BlockSpec((None, D_MODEL, FF_TILE), lambda i, j: (j, 0, 0)),
                  pl.BlockSpec((None, FF_TILE, D_MODEL), lambda i, j: (j, 0, 0))],
        out_specs=[pl.BlockSpec((tm, D_MODEL), lambda i, j: (i, 0)), pl.BlockSpec((tm, FF_TILE), lambda i, j: (i, j)),
                   _full((1, D_MODEL))],
        out_shape=[jax.ShapeDtypeStruct((SEQ, D_MODEL), f32), jax.ShapeDtypeStruct((SEQ, D_FF), bf16),
                   jax.ShapeDtypeStruct((1, D_MODEL), f32)],
        scratch_shapes=[pltpu.VMEM((tm, D_MODEL), f32)],
        compiler_params=_params("arbitrary", "arbitrary"),
    )(dx2, u, x1, g, w_up, w_down)


def mlp_bwd_weights(dx2, u, du, x1, g, tm=512):
    nt = SEQ // tm

    def body(d_ref, u_ref, du_ref, x_ref, g_ref, dwu_ref, dwd_ref, dwu16_ref, dwd16_ref):
        i = pl.program_id(1)

        @pl.when(i == 0)
        def _():
            dwu_ref[...] = jnp.zeros_like(dwu_ref)
            dwd_ref[...] = jnp.zeros_like(dwd_ref)

        xv = x_ref[...]
        h = xv * _rms(xv) * g_ref[...]
        dwu_ref[...] += _bdot_tn(h, du_ref[...])
        a = jnp.square(jnp.maximum(u_ref[...], 0.0))
        dwd_ref[...] += _bdot_tn(a, d_ref[...])

        @pl.when(i == nt - 1)
        def _():
            dwu16_ref[...] = dwu_ref[...].astype(bf16)
            dwd16_ref[...] = dwd_ref[...].astype(bf16)

    up = pl.BlockSpec((None, D_MODEL, FF_TILE), lambda j, i: (j, 0, 0))
    down = pl.BlockSpec((None, FF_TILE, D_MODEL), lambda j, i: (j, 0, 0))
    return pl.pallas_call(
        body, name="mlp_bwd_weights", grid=(N_CHIPS, nt),
        in_specs=[pl.BlockSpec((tm, D_MODEL), lambda j, i: (i, 0)), pl.BlockSpec((tm, FF_TILE), lambda j, i: (i, j)),
                  pl.BlockSpec((tm, FF_TILE), lambda j, i: (i, j)), pl.BlockSpec((tm, D_MODEL), lambda j, i: (i, 0)),
                  _full((1, D_MODEL))],
        out_specs=[up, down, up, down],
        out_shape=[jax.ShapeDtypeStruct((N_CHIPS, D_MODEL, FF_TILE), f32), jax.ShapeDtypeStruct((N_CHIPS, FF_TILE, D_MODEL), f32),
                   jax.ShapeDtypeStruct((N_CHIPS, D_MODEL, FF_TILE), bf16), jax.ShapeDtypeStruct((N_CHIPS, FF_TILE, D_MODEL), bf16)],
        compiler_params=_params("arbitrary", "arbitrary"),
    )(dx2, u, du, x1, g)


def loss_head(y, target, tm=512):
    def body(y_ref, t_ref, dy_ref, l_ref):
        @pl.when(pl.program_id(0) == 0)
        def _():
            l_ref[...] = jnp.zeros_like(l_ref)

        d = y_ref[...] - t_ref[...]
        dy_ref[...] = d * (1.0 / D_MODEL)
        part = jnp.sum(jnp.mean(d * d, axis=-1, keepdims=True), axis=0, keepdims=True)
        l_ref[...] += 0.5 * part

    tok = pl.BlockSpec((tm, D_MODEL), lambda i: (i, 0))
    return pl.pallas_call(
        body, name="loss_head", grid=(SEQ // tm,), in_specs=[tok, tok], out_specs=[tok, _full((1, 1))],
        out_shape=[jax.ShapeDtypeStruct((SEQ, D_MODEL), f32), jax.ShapeDtypeStruct((1, 1), f32)],
        compiler_params=_params("arbitrary"),
    )(y, target)


def _pad_lane(v):
    return jnp.pad(v, (0, LANE - v.shape[0]))[None, :]


def local_step(x, target, w, prov):
    bucket = jnp.asarray(_bucket_table())
    bias = bias_build(w["rel_bias"], bucket)
    saved = []
    for l in range(DEPTH):
        g_mix = w["mix_norm_g"][l][None, :] + prov.stage(("begin", l), x)
        w_in = prov.w_in(l, x)
        proj = in_fwd(x, g_mix, w_in)
        qg, kg = w["q_gain"][l][None, :], w["k_gain"][l][None, :]
        attn = attn_fwd(proj, qg, kg, w["sinks"][l], bias)
        conv_b = w["conv_b"][l][None, :]
        act = conv_fwd(proj, w["conv_w"][l], conv_b)
        dtb = _pad_lane(w["dt_bias"][l]) + prov.stage(("mid", l), act)
        alog, dsk = _pad_lane(w["a_log"][l]), _pad_lane(w["d_skip"][l])
        ng = w["ssm_norm_g"][l][None, :]
        ssm, ypre, states = ssd_fwd(act, proj, dtb, alog, dsk, ng)
        tok = prov.stage(("pre_out", l), ssm)
        w_out = prov.w_out(l, ssm) + jnp.asarray(tok, bf16)
        x1 = out_fwd(x, attn, ssm, w_out)
        g_mlp = w["mlp_norm_g"][l][None, :] + prov.stage(("pre_mlp", l), x1)
        w_up, w_down = prov.mlp(l, x1)
        x2, u = mlp_fwd(x1, g_mlp, w_up, w_down)
        saved.append(dict(x=x, proj=proj, attn=attn, act=act, ssm=ssm, ypre=ypre, states=states, x1=x1, u=u,
                          g_mix=g_mix, qg=qg, kg=kg, conv_b=conv_b, dtb=dtb, alog=alog, dsk=dsk, ng=ng, g_mlp=g_mlp,
                          w_in=w_in, w_out=w_out, w_up=w_up, w_down=w_down))
        x = x2
    dx, loss = loss_head(x, target)
    grads = [None] * DEPTH
    dbands = [None] * DEPTH
    tok = 0.0
    for l in reversed(range(DEPTH)):
        s = saved[l]
        g_mlp = s["g_mlp"] + tok
        dx1, du, dg_mlp = mlp_bwd_data(dx, s["u"], s["x1"], g_mlp, s["w_up"], s["w_down"])
        dw_up, dw_down, dw_up16, dw_down16 = mlp_bwd_weights(dx, s["u"], du, s["x1"], g_mlp)
        tok = prov.grads(("mlp", l), dict(w_up=(dw_up, dw_up16), w_down=(dw_down, dw_down16)), dx1)
        dattn, dssm, dw_out, dw_out16 = out_bwd(dx1, s["attn"], s["ssm"], s["w_out"])
        dact, ddt, dz, dng, dpar = ssd_bwd(s["act"], s["proj"], s["ypre"], s["states"], dssm, s["dtb"] + tok, s["alog"],
                                           s["dsk"], s["ng"])
        conv_b = s["conv_b"] + prov.stage(("bwd_mid", l), dact)
        dxbc, dconv_w, dconv_b = conv_bwd(s["proj"], dact, w["conv_w"][l], conv_b)
        dq, dk, dv, dband, dsink, dqg, dkg = attn_bwd(s["proj"], dattn, s["qg"], s["kg"], w["sinks"][l], bias)
        dx, dw_in, dg_mix = in_bwd(dq, dz, dxbc, dk, dv, ddt, s["x"], s["g_mix"], s["w_in"], dx1)
        tok = prov.grads(("mix", l), dict(w_in=(dw_in, None), w_out=(dw_out, dw_out16)), dx)
        dbands[l] = dband
        grads[l] = dict(mix_norm_g=dg_mix[0], q_gain=dqg[0], k_gain=dkg[0], sinks=dsink[0, :N_Q_HEADS],
                        conv_w=dconv_w, conv_b=dconv_b[0], dt_bias=dpar[0, :SSM_HEADS], a_log=dpar[1, :SSM_HEADS],
                        d_skip=dpar[2, :SSM_HEADS], ssm_norm_g=dng[0], mlp_norm_g=dg_mlp[0])
    out = {k: jnp.stack([grads[l][k] for l in range(DEPTH)]) for k in grads[0]}
    out["rel_bias"] = bias_bwd(dbands[0], dbands[1], bucket)[:, :N_Q_HEADS]
    return loss, dx, out, tok


MESH = pl.DeviceIdType.MESH
HBM = pl.BlockSpec(memory_space=pltpu.HBM)
N_PEER_CHIPS = N_CHIPS - 1
N_DEVICES = 8


def _coords():
    return lax.axis_index("x"), lax.axis_index("y"), lax.axis_index("c")


def _peer_chips(x, y):
    return [(1 - x, y), (x, 1 - y), (1 - x, 1 - y)]


def _remote(src, dst, send_sem, recv_sem, device):
    return pltpu.make_async_remote_copy(src_ref=src, dst_ref=dst, send_sem=send_sem, recv_sem=recv_sem,
                                        device_id=device, device_id_type=MESH)


SEM = pl.BlockSpec(memory_space=pltpu.SEMAPHORE)
ANY = pl.BlockSpec(memory_space=pl.ANY)
DATAFLOW = pltpu.SideEffectType.DATAFLOW_SIDE_EFFECTING


def _gather_copies(kind, src_refs, land_refs, ssem, rsem):
    x, y, c = _coords()
    k_me = 2 * x + y
    cps = []
    for p, land in enumerate(land_refs):
        hr = land.shape[1] // 2
        rows = pl.ds(c * hr, hr)
        for j, chip in enumerate(_peer_chips(x, y)):
            i = 3 * p + j
            if kind == "ici":
                cps.append(_remote(src_refs[p].at[rows, :], land.at[k_me, rows, :], ssem.at[i], rsem.at[i], (*chip, c)))
            else:
                got = land.at[2 * chip[0] + chip[1], rows, :]
                cps.append(_remote(got, got, ssem.at[i], rsem.at[i], (x, y, 1 - c)))
    return cps


def gather_now(srcs, conv):
    n = len(srcs)

    def body(*refs):
        src_refs, conv_ref = refs[:n], refs[n]
        lands, gconv = refs[n + 1:2 * n + 1], refs[2 * n + 1]
        ssem, rsem, fsem, frsem, csem, crsem = refs[2 * n + 2:]
        x, y, c = _coords()
        k_me = 2 * x + y
        chips = _peer_chips(x, y)
        ici = _gather_copies("ici", src_refs, lands, ssem, rsem)
        relay = _gather_copies("relay", src_refs, lands, fsem, frsem)
        conv_cps = [_remote(conv_ref, gconv.at[k_me], csem.at[j], crsem.at[j], (*chip, c)) for j, chip in enumerate(chips)]
        for cp in ici + conv_cps:
            cp.start()
        for cp, fw in zip(ici, relay):
            cp.wait_recv()
            fw.start()
        for cp in conv_cps + relay:
            cp.wait_recv()
        for cp in ici + relay + conv_cps:
            cp.wait_send()

    out_shape = [jax.ShapeDtypeStruct((N_CHIPS,) + s.shape, s.dtype) for s in srcs]
    out_shape.append(jax.ShapeDtypeStruct((N_CHIPS,) + conv.shape, conv.dtype))
    sems = lambda k: pltpu.SemaphoreType.DMA((k,))
    return pl.pallas_call(
        body, name="gather_now", out_shape=out_shape, in_specs=[HBM] * (n + 1), out_specs=[HBM] * (n + 1),
        scratch_shapes=[sems(3 * n), sems(3 * n), sems(3 * n), sems(3 * n), sems(N_PEER_CHIPS), sems(N_PEER_CHIPS)],
    )(*srcs, conv)


def _gather_maker(kind, n_src):
    def make(refs, ssem, rsem):
        cps = _gather_copies(kind, refs[:n_src], refs[n_src:], ssem, rsem)
        return cps, cps
    return make


def _scatter_maker(n):
    def make(refs, ssem, rsem):
        x, y, c = _coords()
        k_me = 2 * x + y
        sends, arrivals = [], []
        for p in range(n):
            src, land = refs[p], refs[n + p]
            sends.append(_remote(src.at[k_me, 1 - c], land.at[0], ssem.at[7 * p], rsem.at[7 * p], (x, y, 1 - c)))
            for j, chip in enumerate(_peer_chips(x, y)):
                for cc in range(2):
                    sends.append(_remote(src.at[2 * chip[0] + chip[1], cc], land.at[1 + 2 * j + c],
                                         ssem.at[7 * p + 1 + 2 * j + cc], rsem.at[7 * p + 1 + 2 * j + c], (*chip, cc)))
            for s in range(7):
                arrivals.append(_remote(land.at[s], land.at[s], ssem.at[7 * p + s], rsem.at[7 * p + s], (x, y, 1 - c)))
        return sends, arrivals
    return make


def _share_maker(n):
    def make(refs, ssem, rsem):
        x, y, c = _coords()
        sends = [_remote(refs[p].at[c], refs[p].at[c], ssem.at[p], rsem.at[p], (x, y, 1 - c)) for p in range(n)]
        arrivals = [_remote(refs[p].at[1 - c], refs[p].at[1 - c], ssem.at[p], rsem.at[p], (x, y, 1 - c)) for p in range(n)]
        return sends, arrivals
    return make


def split_start(name, make, n_sems, operands, after):
    n = len(operands)

    def body(*refs):
        ssem, rsem, token = refs[n + 1], refs[n + 2], refs[-1]
        for cp in make(refs[:n], ssem, rsem)[0]:
            cp.start()
        token[...] = jnp.zeros_like(token)

    ops = [pltpu.with_memory_space_constraint(a, pltpu.HBM) for a in operands]
    outs = pl.pallas_call(
        body, name=name,
        out_shape=(pltpu.SemaphoreType.DMA((n_sems,)), pltpu.SemaphoreType.DMA((n_sems,)),
                   *[pltpu.HBM(a.shape, a.dtype) for a in ops], jax.ShapeDtypeStruct((8, LANE), f32)),
        in_specs=[HBM] * n + [ANY], out_specs=(SEM, SEM, *[HBM] * n, pl.BlockSpec(memory_space=pltpu.VMEM)),
        input_output_aliases={i: 2 + i for i in range(n)},
        compiler_params=pltpu.CompilerParams(has_side_effects=DATAFLOW),
    )(*ops, after)
    return dict(name=name, make=make, ssem=outs[0], rsem=outs[1], operands=outs[2:2 + n], token=outs[-1][0, 0])


def split_wait(handle, after):
    n = len(handle["operands"])

    def body(*refs):
        sends, arrivals = handle["make"](refs[:n], refs[n], refs[n + 1])
        for cp in sends:
            cp.wait_send()
        for cp in arrivals:
            cp.wait_recv()

    outs = pl.pallas_call(
        body, name=handle["name"].replace("start", "wait"),
        out_shape=tuple(pltpu.HBM(a.shape, a.dtype) for a in handle["operands"]),
        in_specs=[HBM] * n + [SEM, SEM, ANY], out_specs=tuple([HBM] * n),
        input_output_aliases={i: i for i in range(n)},
        compiler_params=pltpu.CompilerParams(has_side_effects=DATAFLOW),
    )(*handle["operands"], handle["ssem"], handle["rsem"], after)
    return list(outs)


def piece_sum(g, recv, kc_arr):
    _, _, rb, cc = g.shape
    tr = min(256, rb)

    def body(kc_ref, g_ref, r_ref, o_ref):
        acc = g_ref[...]
        for s in range(7):
            acc = acc + r_ref[s].astype(f32)
        o_ref[...] = acc

    return pl.pallas_call(
        body, name="piece_sum",
        grid_spec=pltpu.PrefetchScalarGridSpec(
            num_scalar_prefetch=1, grid=(rb // tr,),
            in_specs=[pl.BlockSpec((None, None, tr, cc), lambda r, kc: (kc[0], kc[1], r, 0)),
                      pl.BlockSpec((7, tr, cc), lambda r, kc: (0, r, 0))],
            out_specs=pl.BlockSpec((None, tr, cc), lambda r, kc: (kc[1], r, 0))),
        out_shape=jax.ShapeDtypeStruct((2, rb, cc), f32),
        compiler_params=_params("arbitrary"),
    )(kc_arr, g, recv)


def small_all_reduce(vec):
    def body(v_ref, o_ref, gat, ssem, rsem):
        x, y, c = _coords()
        me = 4 * x + 2 * y + c
        gat[me] = v_ref[...]
        sends = []
        for t in range(1, N_DEVICES):
            peer = (x ^ (t >> 2), y ^ ((t >> 1) & 1), c ^ (t & 1))
            cp = _remote(v_ref, gat.at[me], ssem.at[t - 1], rsem.at[t - 1], peer)
            cp.start()
            sends.append(cp)
        for t in range(1, N_DEVICES):
            peer = (x ^ (t >> 2), y ^ ((t >> 1) & 1), c ^ (t & 1))
            slot = gat.at[4 * peer[0] + 2 * peer[1] + peer[2]]
            _remote(slot, slot, ssem.at[t - 1], rsem.at[t - 1], peer).wait_recv()
        for cp in sends:
            cp.wait_send()
        acc = gat[0]
        for d in range(1, N_DEVICES):
            acc = acc + gat[d]
        o_ref[...] = acc

    return pl.pallas_call(
        body, name="small_all_reduce", out_shape=jax.ShapeDtypeStruct(vec.shape, vec.dtype),
        in_specs=[pl.BlockSpec(memory_space=pltpu.VMEM)], out_specs=pl.BlockSpec(memory_space=pltpu.VMEM),
        scratch_shapes=[pltpu.VMEM((N_DEVICES,) + vec.shape, vec.dtype), pltpu.SemaphoreType.DMA((N_DEVICES - 1,)),
                        pltpu.SemaphoreType.DMA((N_DEVICES - 1,))],
    )(vec)


def _adamw_math(w, g, m, v):
    m_new = ADAM_B1 * m + (1.0 - ADAM_B1) * g
    v_new = ADAM_B2 * v + (1.0 - ADAM_B2) * jnp.square(g)
    m_hat = m_new / (1.0 - ADAM_B1 ** ADAM_STEP)
    v_hat = v_new / (1.0 - ADAM_B2 ** ADAM_STEP)
    delta = -ADAM_LR * (m_hat / (jnp.sqrt(v_hat) + ADAM_EPS) + ADAM_WD * w)
    return delta, m_new, v_new


def adamw_shard(w, g0, g1, m, v):
    depth, rows, cols = w.shape
    half = rows // 2
    tr = min(256, half)
    nr = half // tr

    def body(w_ref, g0_ref, g1_ref, m_ref, v_ref, go_ref, d_ref, nm_ref, nv_ref):
        gv = jnp.where(pl.program_id(0) == 0, g0_ref[...], g1_ref[...])
        go_ref[...] = gv
        d_ref[...], nm_ref[...], nv_ref[...] = _adamw_math(w_ref[...], gv, m_ref[...], v_ref[...])

    spec = pl.BlockSpec((None, tr, cols), lambda l, h, r: (l, h * nr + r, 0))
    g0spec = pl.BlockSpec((None, tr, cols), lambda l, h, r: (jnp.where(l == 0, h, 1), jnp.where(l == 0, r, nr - 1), 0))
    g1spec = pl.BlockSpec((None, tr, cols), lambda l, h, r: (jnp.where(l == 1, h, 0), jnp.where(l == 1, r, 0), 0))
    return pl.pallas_call(
        body, name="adamw_shard", grid=(depth, 2, nr), in_specs=[spec, g0spec, g1spec, spec, spec], out_specs=[spec] * 4,
        out_shape=[jax.ShapeDtypeStruct(w.shape, f32)] * 4,
        compiler_params=_params("arbitrary", "arbitrary", "arbitrary"),
    )(w, g0, g1, m, v)


def adamw_small(w, g, m, v):
    def body(w_ref, g_ref, m_ref, v_ref, d_ref, nm_ref, nv_ref):
        d_ref[...], nm_ref[...], nv_ref[...] = _adamw_math(w_ref[...], g_ref[...], m_ref[...], v_ref[...])

    return pl.pallas_call(
        body, name="adamw_small", out_shape=[jax.ShapeDtypeStruct(w.shape, f32)] * 3,
    )(w, g, m, v)


WEIGHTS = ("mix_norm_g", "w_in", "q_gain", "k_gain", "sinks", "rel_bias", "conv_w", "conv_b", "dt_bias", "a_log", "d_skip",
           "ssm_norm_g", "w_out", "mlp_norm_g", "w_up", "w_down")
BIG = ("w_in", "w_out", "w_up", "w_down")
SMALL = tuple(n for n in WEIGHTS if n not in BIG)
PACK_COLS = 1024
PACK_ROWS = 16


def _pack(named):
    flat = jnp.concatenate([named[n].reshape(-1) for n in SMALL])
    return jnp.pad(flat, (0, PACK_ROWS * PACK_COLS - flat.shape[0])).reshape(PACK_ROWS, PACK_COLS)


def _unpack(buf, shapes):
    flat = buf.reshape(-1)
    out, at = {}, 0
    for n in SMALL:
        size = int(np.prod(shapes[n]))
        out[n] = flat[at:at + size].reshape(shapes[n])
        at += size
    return out


class _Exchange:
    GROUPS = {"A": (("w_up", 0), ("w_down", 0)), "B": (("w_in", 1), ("w_out", 1)), "C": (("w_up", 1), ("w_down", 1))}
    RELAY_AT = {("pre_out", 0): "A", ("pre_mlp", 0): "B", ("mid", 1): "C"}
    NEXT_GROUP = {"A": "B", "B": "C"}
    LAST = ("mix", 0)

    def __init__(self, wts, k_me, kc_arr):
        self.wts, self.k_me, self.kc_arr = wts, k_me, kc_arr
        self.own = {(n, l): wts[n][l].astype(bf16) for n in BIG for l in range(DEPTH)}
        now = gather_now([self.own["w_in", 0], self.own["w_out", 0]], wts["conv_w"])
        self.ready = {("w_in", 0): self._fill(now[0], self.own["w_in", 0]),
                      ("w_out", 0): self._fill(now[1], self.own["w_out", 0])}
        conv = self._fill(now[2], wts["conv_w"])
        self.conv_w = jnp.transpose(conv, (1, 2, 0, 3)).reshape(DEPTH, CONV_WIDTH, D_CONV)
        self.ici, self.relay = {}, {}
        self.gview, self.scatter, self.share, self.reduced = {}, [], [], {}
        self._start_ici("A", now[2])

    def _start_ici(self, g, after):
        srcs = [self.own[p] for p in self.GROUPS[g]]
        lands = [lax.empty((N_CHIPS,) + s.shape, s.dtype) for s in srcs]
        self.ici[g] = split_start("gather%s_ici_start" % g, _gather_maker("ici", len(srcs)), 3 * len(srcs), srcs + lands,
                                  after)
        return self.ici[g]["token"]

    def _fill(self, land, own):
        return lax.dynamic_update_slice(land, own[None], (self.k_me,) + (0,) * own.ndim)

    def stage(self, name, after):
        if name == ("begin", 0):
            return self.ici["A"]["token"]
        g = self.RELAY_AT.get(name)
        if g is None:
            return 0.0
        n = len(self.GROUPS[g])
        lands = split_wait(self.ici[g], after)[n:]
        self.relay[g] = split_start("gather%s_relay_start" % g, _gather_maker("relay", 0), 3 * n, lands, after)
        tok = self.relay[g]["token"]
        if g in self.NEXT_GROUP:
            tok = tok + self._start_ici(self.NEXT_GROUP[g], after)
        return tok

    def _get(self, piece, after):
        if piece not in self.ready:
            g = [k for k, pieces in self.GROUPS.items() if piece in pieces][0]
            lands = split_wait(self.relay[g], after)
            for p, land in zip(self.GROUPS[g], lands):
                self.ready[p] = self._fill(land, self.own[p])
        return self.ready[piece]

    def w_in(self, l, after):
        g = self._get(("w_in", l), after)
        return _to_aligned(jnp.transpose(g, (1, 0, 2)).reshape(D_MODEL, D_IN))

    def w_out(self, l, after):
        return self._get(("w_out", l), after).reshape(D_MODEL, D_MODEL)

    def mlp(self, l, after):
        return self._get(("w_up", l), after), self._get(("w_down", l), after)

    def _view(self, n, g):
        if n == "w_in":
            g = jnp.transpose(_from_aligned(g).reshape(D_MODEL, N_CHIPS, D_IN // N_CHIPS), (1, 0, 2))
        _, rows, cols = self.wts[n].shape
        return g.reshape(N_CHIPS, 2, rows // 2, cols)

    def grads(self, name, arrays, after):
        if name == self.LAST:
            self.held = (name, arrays)
            return self._advance(after, 0)
        return self._scatter(name, arrays, after) + self._advance(after, 1)

    def flush(self, after):
        return self._scatter(*self.held, after)

    def _scatter(self, name, arrays, after):
        pieces = [(n, name[1]) for n in arrays]
        views = [self._view(n, g) for n, (g, _) in arrays.items()]
        sends = [v.astype(bf16) if g16 is None else g16.reshape(v.shape) for v, (_, g16) in zip(views, arrays.values())]
        self.gview.update(zip(pieces, views))
        lands = [lax.empty((7,) + v.shape[2:], bf16) for v in views]
        h = split_start("scatter_%s%d_start" % name, _scatter_maker(len(views)), 7 * len(views), sends + lands, after)
        self.scatter.append((pieces, h))
        return h["token"]

    def _take_share(self, after):
        pieces, h = self.share.pop(0)
        self.reduced.update(zip(pieces, split_wait(h, after)))

    def _take_scatter(self, after):
        pieces, h = self.scatter.pop(0)
        lands = split_wait(h, after)[len(pieces):]
        sums = [piece_sum(self.gview[p], land, self.kc_arr) for p, land in zip(pieces, lands)]
        hs = split_start(h["name"].replace("scatter", "share"), _share_maker(len(sums)), len(sums), sums, after)
        self.share.append((pieces, hs))
        return hs["token"]

    def _advance(self, after, newest):
        if self.share:
            self._take_share(after)
        return self._take_scatter(after) if len(self.scatter) > newest else 0.0

    def reduced_grads(self, names, after):
        want = [(n, l) for n in names for l in range(DEPTH)]
        while not all(p in self.reduced for p in want):
            if any(p in pieces for p in want for pieces, _ in self.share):
                self._take_share(after)
            else:
                self._take_scatter(after)
        return {n: [self.reduced[n, l] for l in range(DEPTH)] for n in names}


def kernel(x, mix_norm_g, w_in, q_gain, k_gain, sinks, rel_bias, conv_w, conv_b, dt_bias, a_log, d_skip, ssm_norm_g, w_out, mlp_norm_g, w_up, w_down, loss_target, m_mix_norm_g, m_w_in, m_q_gain, m_k_gain, m_sinks, m_rel_bias, m_conv_w, m_conv_b, m_dt_bias, m_a_log, m_d_skip, m_ssm_norm_g, m_w_out, m_mlp_norm_g, m_w_up, m_w_down, v_mix_norm_g, v_w_in, v_q_gain, v_k_gain, v_sinks, v_rel_bias, v_conv_w, v_conv_b, v_dt_bias, v_a_log, v_d_skip, v_ssm_norm_g, v_w_out, v_mlp_norm_g, v_w_up, v_w_down):
    wts = dict(mix_norm_g=mix_norm_g, w_in=w_in, q_gain=q_gain, k_gain=k_gain, sinks=sinks, rel_bias=rel_bias, conv_w=conv_w,
               conv_b=conv_b, dt_bias=dt_bias, a_log=a_log, d_skip=d_skip, ssm_norm_g=ssm_norm_g, w_out=w_out,
               mlp_norm_g=mlp_norm_g, w_up=w_up, w_down=w_down)
    mom = dict(mix_norm_g=m_mix_norm_g, w_in=m_w_in, q_gain=m_q_gain, k_gain=m_k_gain, sinks=m_sinks, rel_bias=m_rel_bias,
               conv_w=m_conv_w, conv_b=m_conv_b, dt_bias=m_dt_bias, a_log=m_a_log, d_skip=m_d_skip, ssm_norm_g=m_ssm_norm_g,
               w_out=m_w_out, mlp_norm_g=m_mlp_norm_g, w_up=m_w_up, w_down=m_w_down)
    var = dict(mix_norm_g=v_mix_norm_g, w_in=v_w_in, q_gain=v_q_gain, k_gain=v_k_gain, sinks=v_sinks, rel_bias=v_rel_bias,
               conv_w=v_conv_w, conv_b=v_conv_b, dt_bias=v_dt_bias, a_log=v_a_log, d_skip=v_d_skip, ssm_norm_g=v_ssm_norm_g,
               w_out=v_w_out, mlp_norm_g=v_mlp_norm_g, w_up=v_w_up, w_down=v_w_down)
    xi, yi, ci = _coords()
    k_me = 2 * xi + yi
    kc_arr = jnp.stack([k_me, ci]).astype(jnp.int32)

    prov = _Exchange(wts, k_me, kc_arr)
    small_w = {n: wts[n] for n in SMALL}
    small_w["conv_w"] = prov.conv_w
    loss, dx, grads, tok = local_step(x[0], loss_target[0], small_w, prov)
    loss = lax.psum(loss[0, 0], ("x", "y", "c"))

    small_shapes = {n: grads[n].shape for n in SMALL}
    small_sum = small_all_reduce(_pack(grads) + tok)
    tok = prov.flush(small_sum)
    small = _unpack(small_sum, small_shapes)
    cols = conv_w.shape[-1]
    small["conv_w"] = lax.dynamic_slice_in_dim(small["conv_w"], k_me * cols, cols, axis=2)
    g_out_d, d_out_d, m_out_d, v_out_d = {}, {}, {}, {}
    shard_shapes = {n: wts[n].shape for n in SMALL}
    d, nm, nv = adamw_small(_pack(wts), _pack(small) + tok, _pack(mom), _pack(var))
    for dst, buf in ((d_out_d, d), (m_out_d, nm), (v_out_d, nv)):
        dst.update(_unpack(buf, shard_shapes))
    g_out_d.update(small)

    after = d
    for names in (("w_up", "w_down"), ("w_in", "w_out")):
        for n, (g0, g1) in prov.reduced_grads(names, after).items():
            g_out_d[n], d_out_d[n], m_out_d[n], v_out_d[n] = adamw_shard(wts[n], g0, g1, mom[n], var[n])
            after = d_out_d[n]

    return (loss, dx[None], *[g_out_d[n] for n in WEIGHTS], *[d_out_d[n] for n in WEIGHTS],
            *[m_out_d[n] for n in WEIGHTS], *[v_out_d[n] for n in WEIGHTS])
```

```python
import functools

import numpy as np
import jax
import jax.numpy as jnp
from jax import lax
from jax.experimental import pallas as pl
from jax.experimental.pallas import tpu as pltpu

f32 = jnp.float32
bf16 = jnp.bfloat16

SEQ = 2048
D_MODEL = 1024
DEPTH = 2
HEAD_DIM = 64
N_Q_HEADS = 8
N_KV_HEADS = 2
Q_PER_KV = N_Q_HEADS // N_KV_HEADS
BLOCK = 128
N_BLOCKS = SEQ // BLOCK
N_BUCKETS = 32
MAX_DISTANCE = 128
SSM_HEADS = 8
SSM_HEAD_DIM = 64
SSM_GROUPS = 2
HEADS_PER_GROUP = SSM_HEADS // SSM_GROUPS
SSM_STATE = 128
CONV_WIDTH = 4
CHUNK = 128
N_CHUNKS = SEQ // CHUNK
D_FF = 4 * D_MODEL
D_ATTN = N_Q_HEADS * HEAD_DIM
D_KV = N_KV_HEADS * HEAD_DIM
D_SSM = SSM_HEADS * SSM_HEAD_DIM
D_BC = SSM_GROUPS * SSM_STATE
D_CONV = D_SSM + 2 * D_BC
D_IN = D_ATTN + 2 * D_KV + D_SSM + D_CONV + SSM_HEADS
EPS = 1e-6
NEG = -1e30
N_CHIPS = 4
FF_TILE = D_FF // N_CHIPS

LANE = 128
PW = D_ATTN + D_SSM + D_CONV + 2 * D_KV + LANE
OFF_Q, OFF_Z, OFF_X, OFF_K, OFF_V, OFF_DT = 0, 512, 1024, 2048, 2176, 2304

ADAM_LR = 0.001
ADAM_B1 = 0.9
ADAM_B2 = 0.999
ADAM_EPS = 1e-08
ADAM_WD = 0.01
ADAM_STEP = 10

VMEM_LIMIT = 56 * 1024 * 1024


def _params(*sem):
    return pltpu.CompilerParams(dimension_semantics=tuple(sem), vmem_limit_bytes=VMEM_LIMIT)


def _bdot(a, b):
    return jnp.dot(a.astype(bf16), b.astype(bf16), preferred_element_type=f32)


def _bdot_nt(a, b):
    return lax.dot_general(a.astype(bf16), b.astype(bf16), (((1,), (1,)), ((), ())), preferred_element_type=f32)


def _bdot_tn(a, b):
    return lax.dot_general(a.astype(bf16), b.astype(bf16), (((0,), (0,)), ((), ())), preferred_element_type=f32)


def _hdot(a, b):
    return jnp.dot(a, b, precision=lax.Precision.HIGHEST, preferred_element_type=f32)


def _sigmoid(x):
    return 1.0 / (1.0 + jnp.exp(-x))


def _softplus(x):
    return jnp.maximum(x, 0.0) + jnp.log1p(jnp.exp(-jnp.abs(x)))


def _rms(x):
    return lax.rsqrt(jnp.mean(x * x, axis=-1, keepdims=True) + EPS)


def _rms_bwd(dy, xhat, r, g):
    t = dy * g
    return r * (t - xhat * jnp.mean(t * xhat, axis=-1, keepdims=True))


def _full(shape):
    return pl.BlockSpec(shape, lambda *_: (0,) * len(shape))


def _bucket_table():
    qi = np.arange(BLOCK)[:, None]
    kj = np.arange(2 * BLOCK)[None, :]
    dist = qi + BLOCK - kj
    ok = (dist >= 0) & (dist < 128)
    d = np.clip(dist, 0, None)
    max_exact = N_BUCKETS // 2
    d_f = np.maximum(d, 1).astype(np.float32)
    large = max_exact + (np.log(d_f / np.float32(max_exact)) / np.float32(np.log(MAX_DISTANCE / max_exact))
                         * np.float32(N_BUCKETS - max_exact)).astype(np.int32)
    large = np.minimum(large, N_BUCKETS - 1)
    bucket = np.where(d < max_exact, d, large)
    return np.where(ok, bucket, -1).astype(np.int32)


def bias_build(rel_bias, bucket):
    def body(rel_ref, bkt_ref, o_ref):
        bkt = bkt_ref[...]
        for h in range(N_Q_HEADS):
            acc = jnp.where(bkt < 0, NEG, 0.0).astype(f32)
            for b in range(N_BUCKETS):
                acc = acc + jnp.where(bkt == b, rel_ref[b, h], 0.0)
            o_ref[h] = acc

    return pl.pallas_call(
        body, name="bias_build", out_shape=jax.ShapeDtypeStruct((N_Q_HEADS, BLOCK, 2 * BLOCK), f32),
        in_specs=[pl.BlockSpec(memory_space=pltpu.SMEM), pl.BlockSpec(memory_space=pltpu.VMEM)],
        out_specs=pl.BlockSpec(memory_space=pltpu.VMEM),
    )(rel_bias, bucket)


def bias_bwd(dband0, dband1, bucket):
    def body(d0_ref, d1_ref, bkt_ref, o_ref):
        bkt = bkt_ref[...]
        o_ref[...] = jnp.zeros_like(o_ref)
        for h in range(N_Q_HEADS):
            d = d0_ref[h] + d1_ref[h]
            for b in range(N_BUCKETS):
                part = jnp.sum(jnp.where(bkt == b, d, 0.0), axis=1, keepdims=True)
                o_ref[b:b + 1, h:h + 1] = jnp.sum(part, axis=0, keepdims=True)

    return pl.pallas_call(
        body, name="bias_bwd", out_shape=jax.ShapeDtypeStruct((N_BUCKETS, LANE), f32),
    )(dband0, dband1, bucket)


W_IN_SHARD = D_IN // N_CHIPS
_ALIGNED_PIECES = ((0, 0, 512), (1, 190, 578), (2, 0, 124), (2, 124, 578), (3, 0, 570), (0, 512, 578), (1, 0, 62),
                   (1, 62, 190), (3, 570, 578))
_SHARD_PIECES = (((0, 512), (2048, 2114)), ((2114, 2176), (2176, 2304), (512, 900)), ((900, 1024), (1024, 1478)),
                 ((1478, 2048), (2304, 2312)))


def align_w_in(shards, tr=256):
    def body(s_ref, o_ref):
        parts = [s_ref[k, :, a:b] for k, a, b in _ALIGNED_PIECES]
        parts.append(jnp.zeros((tr, LANE - SSM_HEADS), s_ref.dtype))
        o_ref[...] = jnp.concatenate(parts, axis=-1)

    return pl.pallas_call(
        body, name="align_w_in", grid=(D_MODEL // tr,),
        in_specs=[pl.BlockSpec((N_CHIPS, tr, W_IN_SHARD), lambda i: (0, i, 0))],
        out_specs=pl.BlockSpec((tr, PW), lambda i: (i, 0)),
        out_shape=jax.ShapeDtypeStruct((D_MODEL, PW), shards.dtype),
        compiler_params=_params("arbitrary"),
    )(shards)


def split_w_in_grad(dw, tr=256):
    def body(d_ref, o_ref, o16_ref):
        for k, pieces in enumerate(_SHARD_PIECES):
            part = jnp.concatenate([d_ref[:, a:b] for a, b in pieces], axis=-1)
            o_ref[k] = part
            o16_ref[k] = part.astype(bf16)

    spec = pl.BlockSpec((N_CHIPS, tr, W_IN_SHARD), lambda i: (0, i, 0))
    return pl.pallas_call(
        body, name="split_w_in_grad", grid=(D_MODEL // tr,),
        in_specs=[pl.BlockSpec((tr, PW), lambda i: (i, 0))], out_specs=[spec, spec],
        out_shape=[jax.ShapeDtypeStruct((N_CHIPS, D_MODEL, W_IN_SHARD), f32),
                   jax.ShapeDtypeStruct((N_CHIPS, D_MODEL, W_IN_SHARD), bf16)],
        compiler_params=_params("arbitrary"),
    )(dw)

def in_fwd(x, g, w, tm=256):
    def body(x_ref, g_ref, w_ref, o_ref):
        xv = x_ref[...]
        h = xv * _rms(xv) * g_ref[...]
        o_ref[...] = _bdot(h, w_ref[...])

    return pl.pallas_call(
        body, name="in_fwd", grid=(SEQ // tm,),
        in_specs=[pl.BlockSpec((tm, D_MODEL), lambda i: (i, 0)), _full((1, D_MODEL)), _full((D_MODEL, PW))],
        out_specs=pl.BlockSpec((tm, PW), lambda i: (i, 0)),
        out_shape=jax.ShapeDtypeStruct((SEQ, PW), f32),
        compiler_params=_params("arbitrary"),
    )(x, g, w)


def in_bwd(dq, dz, dxbc, dk, dv, ddt, x, g, w, dres, tm=256):
    def body(dq_ref, dz_ref, dx_ref, dk_ref, dv_ref, ddt_ref, x_ref, g_ref, w_ref, dres_ref, o_ref, dw_ref, dg_ref):
        i = pl.program_id(0)

        @pl.when(i == 0)
        def _():
            dw_ref[...] = jnp.zeros_like(dw_ref)
            dg_ref[...] = jnp.zeros_like(dg_ref)

        dproj = jnp.concatenate([dq_ref[...], dz_ref[...], dx_ref[...], dk_ref[...], dv_ref[...], ddt_ref[...]],
                                axis=-1).astype(bf16)
        xv = x_ref[...]
        r = _rms(xv)
        xhat = xv * r
        gv = g_ref[...]
        h = xhat * gv
        dw_ref[...] += _bdot_tn(h, dproj)
        dh = _bdot_nt(dproj, w_ref[...])
        dg_ref[...] += jnp.sum(dh * xhat, axis=0, keepdims=True)
        o_ref[...] = dres_ref[...] + _rms_bwd(dh, xhat, r, gv)

    tok = lambda w_: pl.BlockSpec((tm, w_), lambda i: (i, 0))
    return pl.pallas_call(
        body, name="in_bwd", grid=(SEQ // tm,),
        in_specs=[tok(D_ATTN), tok(D_SSM), tok(D_CONV), tok(D_KV // 1), tok(D_KV // 1), tok(LANE), tok(D_MODEL),
                  _full((1, D_MODEL)), _full((D_MODEL, PW)), tok(D_MODEL)],
        out_specs=[tok(D_MODEL), _full((D_MODEL, PW)), _full((1, D_MODEL))],
        out_shape=[jax.ShapeDtypeStruct((SEQ, D_MODEL), f32), jax.ShapeDtypeStruct((D_MODEL, PW), f32),
                   jax.ShapeDtypeStruct((1, D_MODEL), f32)],
        compiler_params=_params("arbitrary"),
    )(dq, dz, dxbc, dk, dv, ddt, x, g, w, dres)


def _attn_probs(qn, kn, bias_h, sink, first, col):
    s = _bdot_nt(qn, kn) * (HEAD_DIM ** -0.5) + bias_h
    s = jnp.where(jnp.logical_and(first, col < BLOCK), NEG, s)
    m = jnp.maximum(jnp.max(s, axis=-1, keepdims=True), sink)
    p = jnp.exp(s - m)
    psink = jnp.exp(sink - m)
    inv = 1.0 / (jnp.sum(p, axis=-1, keepdims=True) + psink)
    return p * inv, psink * inv


def attn_fwd(proj, q_gain, k_gain, sinks, bias):
    kcol, vcol = OFF_K // D_KV, OFF_V // D_KV

    def body(q_ref, kc_ref, kp_ref, vc_ref, vp_ref, qg_ref, kg_ref, sink_ref, bias_ref, o_ref):
        n = pl.program_id(0)
        first = n == 0
        col = lax.broadcasted_iota(jnp.int32, (BLOCK, 2 * BLOCK), 1)
        k2 = jnp.concatenate([kp_ref[...], kc_ref[...]], axis=0)
        v2 = jnp.concatenate([vp_ref[...], vc_ref[...]], axis=0)
        qg, kg = qg_ref[...], kg_ref[...]
        for hk in range(N_KV_HEADS):
            kk = k2[:, hk * HEAD_DIM:(hk + 1) * HEAD_DIM]
            kn = (kk * _rms(kk) * kg).astype(bf16)
            vb = v2[:, hk * HEAD_DIM:(hk + 1) * HEAD_DIM].astype(bf16)
            for gq in range(Q_PER_KV):
                h = hk * Q_PER_KV + gq
                qq = q_ref[:, h * HEAD_DIM:(h + 1) * HEAD_DIM]
                qn = qq * _rms(qq) * qg
                p, _ = _attn_probs(qn, kn, bias_ref[h], sink_ref[h], first, col)
                o_ref[:, h * HEAD_DIM:(h + 1) * HEAD_DIM] = _bdot(p, vb)

    prev = lambda n: jnp.maximum(n - 1, 0)
    return pl.pallas_call(
        body, name="attn_fwd", grid=(N_BLOCKS,),
        in_specs=[pl.BlockSpec((BLOCK, D_ATTN), lambda n: (n, 0)),
                  pl.BlockSpec((BLOCK, D_KV), lambda n: (n, kcol)), pl.BlockSpec((BLOCK, D_KV), lambda n: (prev(n), kcol)),
                  pl.BlockSpec((BLOCK, D_KV), lambda n: (n, vcol)), pl.BlockSpec((BLOCK, D_KV), lambda n: (prev(n), vcol)),
                  _full((1, HEAD_DIM)), _full((1, HEAD_DIM)), pl.BlockSpec(memory_space=pltpu.SMEM),
                  _full((N_Q_HEADS, BLOCK, 2 * BLOCK))],
        out_specs=pl.BlockSpec((BLOCK, D_ATTN), lambda n: (n, 0)),
        out_shape=jax.ShapeDtypeStruct((SEQ, D_ATTN), f32),
        compiler_params=_params("arbitrary"),
    )(proj, proj, proj, proj, proj, q_gain, k_gain, sinks, bias)


def attn_bwd(proj, d_out, q_gain, k_gain, sinks, bias):
    kcol, vcol = OFF_K // D_KV, OFF_V // D_KV

    def body(q_ref, kc_ref, kp_ref, vc_ref, vp_ref, do_ref, qg_ref, kg_ref, sink_ref, bias_ref,
             dq_ref, dk_ref, dv_ref, dband_ref, dsink_ref, dqg_ref, dkg_ref, dkn_scr, dv_scr):
        i = pl.program_id(0)
        first = i == N_BLOCKS - 1

        @pl.when(i == 0)
        def _():
            for ref in (dband_ref, dsink_ref, dqg_ref, dkg_ref, dkn_scr, dv_scr):
                ref[...] = jnp.zeros_like(ref)

        col = lax.broadcasted_iota(jnp.int32, (BLOCK, 2 * BLOCK), 1)
        k2 = jnp.concatenate([kp_ref[...], kc_ref[...]], axis=0)
        v2 = jnp.concatenate([vp_ref[...], vc_ref[...]], axis=0)
        qg, kg = qg_ref[...], kg_ref[...]
        scale = HEAD_DIM ** -0.5
        for hk in range(N_KV_HEADS):
            sl = slice(hk * HEAD_DIM, (hk + 1) * HEAD_DIM)
            kk = k2[:, sl]
            rk = _rms(kk)
            khat = kk * rk
            kn = (khat * kg).astype(bf16)
            vb = v2[:, sl].astype(bf16)
            dkn = jnp.zeros((2 * BLOCK, HEAD_DIM), f32)
            dvv = jnp.zeros((2 * BLOCK, HEAD_DIM), f32)
            for gq in range(Q_PER_KV):
                h = hk * Q_PER_KV + gq
                hs = slice(h * HEAD_DIM, (h + 1) * HEAD_DIM)
                qq = q_ref[:, hs]
                rq = _rms(qq)
                qhat = qq * rq
                qn = qhat * qg
                p, psink = _attn_probs(qn, kn, bias_ref[h], sink_ref[h], first, col)
                d_o = do_ref[:, hs]
                dp = _bdot_nt(d_o, vb)
                delta = jnp.sum(p * dp, axis=-1, keepdims=True)
                ds = p * (dp - delta)
                dband_ref[h] += ds
                dsink_ref[:, h:h + 1] += -jnp.sum(psink * delta, axis=0, keepdims=True)
                dqn = _bdot(ds, kn) * scale
                dkn = dkn + _bdot_tn(ds, qn) * scale
                dvv = dvv + _bdot_tn(p, d_o)
                dqg_ref[...] += jnp.sum(dqn * qhat, axis=0, keepdims=True)
                dq_ref[:, hs] = _rms_bwd(dqn, qhat, rq, qg)
            dkn_cur = dkn[BLOCK:] + dkn_scr[:, sl]
            dkn_scr[:, sl] = dkn[:BLOCK]
            khat_c, rk_c = khat[BLOCK:], rk[BLOCK:]
            dkg_ref[...] += jnp.sum(dkn_cur * khat_c, axis=0, keepdims=True)
            dk_ref[:, sl] = _rms_bwd(dkn_cur, khat_c, rk_c, kg)
            dv_ref[:, sl] = dvv[BLOCK:] + dv_scr[:, sl]
            dv_scr[:, sl] = dvv[:BLOCK]

    blk = lambda i: N_BLOCKS - 1 - i
    prev = lambda i: jnp.maximum(N_BLOCKS - 2 - i, 0)
    return pl.pallas_call(
        body, name="attn_bwd", grid=(N_BLOCKS,),
        in_specs=[pl.BlockSpec((BLOCK, D_ATTN), lambda i: (blk(i), 0)),
                  pl.BlockSpec((BLOCK, D_KV), lambda i: (blk(i), kcol)), pl.BlockSpec((BLOCK, D_KV), lambda i: (prev(i), kcol)),
                  pl.BlockSpec((BLOCK, D_KV), lambda i: (blk(i), vcol)), pl.BlockSpec((BLOCK, D_KV), lambda i: (prev(i), vcol)),
                  pl.BlockSpec((BLOCK, D_ATTN), lambda i: (blk(i), 0)),
                  _full((1, HEAD_DIM)), _full((1, HEAD_DIM)), pl.BlockSpec(memory_space=pltpu.SMEM),
                  _full((N_Q_HEADS, BLOCK, 2 * BLOCK))],
        out_specs=[pl.BlockSpec((BLOCK, D_ATTN), lambda i: (blk(i), 0)), pl.BlockSpec((BLOCK, D_KV), lambda i: (blk(i), 0)),
                   pl.BlockSpec((BLOCK, D_KV), lambda i: (blk(i), 0)), _full((N_Q_HEADS, BLOCK, 2 * BLOCK)),
                   _full((1, LANE)), _full((1, HEAD_DIM)), _full((1, HEAD_DIM))],
        out_shape=[jax.ShapeDtypeStruct((SEQ, D_ATTN), f32), jax.ShapeDtypeStruct((SEQ, D_KV), f32),
                   jax.ShapeDtypeStruct((SEQ, D_KV), f32), jax.ShapeDtypeStruct((N_Q_HEADS, BLOCK, 2 * BLOCK), f32),
                   jax.ShapeDtypeStruct((1, LANE), f32), jax.ShapeDtypeStruct((1, HEAD_DIM), f32),
                   jax.ShapeDtypeStruct((1, HEAD_DIM), f32)],
        scratch_shapes=[pltpu.VMEM((BLOCK, D_KV), f32), pltpu.VMEM((BLOCK, D_KV), f32)],
        compiler_params=_params("arbitrary"),
    )(proj, proj, proj, proj, proj, d_out, q_gain, k_gain, sinks, bias)


def _shift_down(u, s, row):
    if s == 0:
        return u
    return jnp.where(row >= s, pltpu.roll(u, s, 0), 0.0)


def _shift_up(u, s, row):
    if s == 0:
        return u
    return jnp.where(row < SEQ - s, pltpu.roll(u, SEQ - s, 0), 0.0)


def conv_fwd(proj, conv_w, conv_b):
    xcol = OFF_X // LANE

    def body(u_ref, w_ref, b_ref, o_ref):
        u = u_ref[...]
        row = lax.broadcasted_iota(jnp.int32, u.shape, 0)
        pre = b_ref[...] + jnp.zeros_like(u)
        for k in range(CONV_WIDTH):
            pre = pre + w_ref[k:k + 1, :] * _shift_down(u, CONV_WIDTH - 1 - k, row)
        o_ref[...] = pre * _sigmoid(pre)

    return pl.pallas_call(
        body, name="conv_fwd", grid=(D_CONV // LANE,),
        in_specs=[pl.BlockSpec((SEQ, LANE), lambda j: (0, xcol + j)), pl.BlockSpec((CONV_WIDTH, LANE), lambda j: (0, j)),
                  pl.BlockSpec((1, LANE), lambda j: (0, j))],
        out_specs=pl.BlockSpec((SEQ, LANE), lambda j: (0, j)),
        out_shape=jax.ShapeDtypeStruct((SEQ, D_CONV), f32),
        compiler_params=_params("arbitrary"),
    )(proj, conv_w, conv_b)


def conv_bwd(proj, d_act, conv_w, conv_b):
    xcol = OFF_X // LANE

    def body(u_ref, da_ref, w_ref, b_ref, du_ref, dw_ref, db_ref):
        u = u_ref[...]
        row = lax.broadcasted_iota(jnp.int32, u.shape, 0)
        shifted = [_shift_down(u, CONV_WIDTH - 1 - k, row) for k in range(CONV_WIDTH)]
        pre = b_ref[...] + jnp.zeros_like(u)
        for k in range(CONV_WIDTH):
            pre = pre + w_ref[k:k + 1, :] * shifted[k]
        sg = _sigmoid(pre)
        dpre = da_ref[...] * (sg * (1.0 + pre * (1.0 - sg)))
        db_ref[...] = jnp.sum(dpre, axis=0, keepdims=True)
        du = jnp.zeros_like(u)
        for k in range(CONV_WIDTH):
            dw_ref[k:k + 1, :] = jnp.sum(dpre * shifted[k], axis=0, keepdims=True)
            du = du + w_ref[k:k + 1, :] * _shift_up(dpre, CONV_WIDTH - 1 - k, row)
        du_ref[...] = du

    return pl.pallas_call(
        body, name="conv_bwd", grid=(D_CONV // LANE,),
        in_specs=[pl.BlockSpec((SEQ, LANE), lambda j: (0, xcol + j)), pl.BlockSpec((SEQ, LANE), lambda j: (0, j)),
                  pl.BlockSpec((CONV_WIDTH, LANE), lambda j: (0, j)), pl.BlockSpec((1, LANE), lambda j: (0, j))],
        out_specs=[pl.BlockSpec((SEQ, LANE), lambda j: (0, j)), pl.BlockSpec((CONV_WIDTH, LANE), lambda j: (0, j)),
                   pl.BlockSpec((1, LANE), lambda j: (0, j))],
        out_shape=[jax.ShapeDtypeStruct((SEQ, D_CONV), f32), jax.ShapeDtypeStruct((CONV_WIDTH, D_CONV), f32),
                   jax.ShapeDtypeStruct((1, D_CONV), f32)],
        compiler_params=_params("arbitrary"),
    )(proj, d_act, conv_w, conv_b)


def _ssd_chunk_common(dt_raw, dtb, alog):
    row = lax.broadcasted_iota(jnp.int32, (CHUNK, CHUNK), 0)
    col = lax.broadcasted_iota(jnp.int32, (CHUNK, CHUNK), 1)
    tri = (row >= col).astype(f32)
    strict = (row > col).astype(f32)
    dtp = _softplus(dt_raw + dtb)
    a_row = -jnp.exp(alog)
    d_a = dtp * a_row
    cs = _hdot(tri, d_a)
    cs_last = cs[CHUNK - 1:CHUNK, :]
    return row, col, dtp, a_row, cs, cs.T, cs_last


def _seg_decay(cs, cs_t, hd, row, col):
    seg = cs[:, hd:hd + 1] - cs_t[hd:hd + 1, :]
    return jnp.where(row >= col, jnp.exp(seg), 0.0)


def ssd_fwd(act, proj, dt_bias, a_log, d_skip, norm_g):
    zcol, dtcol = OFF_Z // D_SSM, OFF_DT // LANE
    gw = D_SSM // SSM_GROUPS

    def body(act_ref, z_ref, dt_ref, dtb_ref, alog_ref, dsk_ref, ng_ref, out_ref, ypre_ref, st_ref, state, ybuf):
        c = pl.program_id(0)

        @pl.when(c == 0)
        def _():
            state[...] = jnp.zeros_like(state)

        row, col, dtp, a_row, cs, cs_t, cs_last = _ssd_chunk_common(dt_ref[...], dtb_ref[...], alog_ref[...])
        e_cs = jnp.exp(cs)
        dte = jnp.exp(cs_last - cs)
        ecl = jnp.exp(cs_last)
        dsk = dsk_ref[...]
        for g in range(SSM_GROUPS):
            bg = act_ref[:, D_SSM + g * SSM_STATE:D_SSM + (g + 1) * SSM_STATE]
            cg = act_ref[:, D_SSM + D_BC + g * SSM_STATE:D_SSM + D_BC + (g + 1) * SSM_STATE]
            cb = _bdot_nt(cg, bg)
            for r in range(HEADS_PER_GROUP):
                hd = g * HEADS_PER_GROUP + r
                hs = slice(hd * SSM_HEAD_DIM, (hd + 1) * SSM_HEAD_DIM)
                hl = slice(hd, hd + 1)
                x_h = act_ref[:, hs]
                xdt = x_h * dtp[:, hl]
                lm = _seg_decay(cs, cs_t, hd, row, col)
                prev = state[hd]
                st_ref[0, hd] = prev
                y = _bdot(cb * lm, xdt) + e_cs[:, hl] * _bdot(cg, prev) + x_h * dsk[:, hl]
                ybuf[:, hs] = y
                state[hd] = prev * ecl[:, hl] + _bdot_tn(bg, xdt * dte[:, hl])
        y = ybuf[...]
        ypre_ref[...] = y
        z = z_ref[...]
        yz = y * (z * _sigmoid(z))
        ng = ng_ref[...]
        for g in range(SSM_GROUPS):
            gs = slice(g * gw, (g + 1) * gw)
            part = yz[:, gs]
            out_ref[:, gs] = part * _rms(part) * ng[:, gs]

    return pl.pallas_call(
        body, name="ssd_fwd", grid=(N_CHUNKS,),
        in_specs=[pl.BlockSpec((CHUNK, D_CONV), lambda c: (c, 0)), pl.BlockSpec((CHUNK, D_SSM), lambda c: (c, zcol)),
                  pl.BlockSpec((CHUNK, LANE), lambda c: (c, dtcol)), _full((1, LANE)), _full((1, LANE)), _full((1, LANE)),
                  _full((1, D_SSM))],
        out_specs=[pl.BlockSpec((CHUNK, D_SSM), lambda c: (c, 0)), pl.BlockSpec((CHUNK, D_SSM), lambda c: (c, 0)),
                   pl.BlockSpec((1, SSM_HEADS, SSM_STATE, SSM_HEAD_DIM), lambda c: (c, 0, 0, 0))],
        out_shape=[jax.ShapeDtypeStruct((SEQ, D_SSM), f32), jax.ShapeDtypeStruct((SEQ, D_SSM), f32),
                   jax.ShapeDtypeStruct((N_CHUNKS, SSM_HEADS, SSM_STATE, SSM_HEAD_DIM), f32)],
        scratch_shapes=[pltpu.VMEM((SSM_HEADS, SSM_STATE, SSM_HEAD_DIM), f32), pltpu.VMEM((CHUNK, D_SSM), f32)],
        compiler_params=_params("arbitrary"),
    )(act, proj, proj, dt_bias, a_log, d_skip, norm_g)


def ssd_bwd(act, proj, ypre, states, d_out, dt_bias, a_log, d_skip, norm_g):
    zcol, dtcol = OFF_Z // D_SSM, OFF_DT // LANE
    gw = D_SSM // SSM_GROUPS

    def body(act_ref, z_ref, dt_ref, ypre_ref, st_ref, do_ref, dtb_ref, alog_ref, dsk_ref, ng_ref,
             dact_ref, ddt_ref, dz_ref, dng_ref, dpar_ref, dstate, dybuf):
        i = pl.program_id(0)

        @pl.when(i == 0)
        def _():
            for ref in (dng_ref, dpar_ref, dstate):
                ref[...] = jnp.zeros_like(ref)

        y = ypre_ref[...]
        z = z_ref[...]
        sgz = _sigmoid(z)
        sz = z * sgz
        yz = y * sz
        ng = ng_ref[...]
        d_o = do_ref[...]
        for g in range(SSM_GROUPS):
            gs = slice(g * gw, (g + 1) * gw)
            part = yz[:, gs]
            r = _rms(part)
            yhat = part * r
            dng_ref[:, gs] += jnp.sum(d_o[:, gs] * yhat, axis=0, keepdims=True)
            dyz = _rms_bwd(d_o[:, gs], yhat, r, ng[:, gs])
            dybuf[:, gs] = dyz * sz[:, gs]
            dz_ref[:, gs] = dyz * y[:, gs] * (sgz[:, gs] * (1.0 + z[:, gs] * (1.0 - sgz[:, gs])))

        row, col, dtp, a_row, cs, cs_t, cs_last = _ssd_chunk_common(dt_ref[...], dtb_ref[...], alog_ref[...])
        upper = (row <= col).astype(f32)
        lane = lax.broadcasted_iota(jnp.int32, (CHUNK, LANE), 1)
        lane1 = lax.broadcasted_iota(jnp.int32, (1, LANE), 1)
        e_cs = jnp.exp(cs)
        dte = jnp.exp(cs_last - cs)
        ecl = jnp.exp(cs_last)
        dsk = dsk_ref[...]
        ddt_mat = jnp.zeros((CHUNK, LANE), f32)
        dcs_mat = jnp.zeros((CHUNK, LANE), f32)
        dcs_t = jnp.zeros((LANE, CHUNK), f32)
        dcsl_row = jnp.zeros((1, LANE), f32)
        dd_row = jnp.zeros((1, LANE), f32)
        for g in range(SSM_GROUPS):
            bsl = slice(D_SSM + g * SSM_STATE, D_SSM + (g + 1) * SSM_STATE)
            csl = slice(D_SSM + D_BC + g * SSM_STATE, D_SSM + D_BC + (g + 1) * SSM_STATE)
            bg = act_ref[:, bsl]
            cg = act_ref[:, csl]
            cb = _bdot_nt(cg, bg)
            dcb = jnp.zeros((CHUNK, CHUNK), f32)
            dbg = jnp.zeros((CHUNK, SSM_STATE), f32)
            dcg = jnp.zeros((CHUNK, SSM_STATE), f32)
            for rr in range(HEADS_PER_GROUP):
                hd = g * HEADS_PER_GROUP + rr
                hs = slice(hd * SSM_HEAD_DIM, (hd + 1) * SSM_HEAD_DIM)
                hl = slice(hd, hd + 1)
                x_h = act_ref[:, hs]
                dt_h = dtp[:, hl]
                e_h = e_cs[:, hl]
                dte_h = dte[:, hl]
                ecl_h = ecl[:, hl]
                xdt = x_h * dt_h
                lm = _seg_decay(cs, cs_t, hd, row, col)
                m = cb * lm
                prev = st_ref[0, hd]
                dy = dybuf[:, hs]
                dh = dstate[hd]
                dd_row = dd_row + jnp.where(lane1 == hd, jnp.sum(jnp.sum(dy * x_h, axis=1, keepdims=True), axis=0, keepdims=True), 0.0)
                dx = dy * dsk[:, hl]
                gmat = _bdot(cg, prev)
                dg = dy * e_h
                dcg = dcg + _bdot_nt(dg, prev)
                dprev = _bdot_tn(cg, dg)
                dcs_h = jnp.sum(dy * gmat, axis=1, keepdims=True) * e_h
                dm = _bdot_nt(dy, xdt)
                dxdt = _bdot_tn(m, dy)
                dcb = dcb + dm * lm
                dseg = dm * m
                dcs_h = dcs_h + jnp.sum(dseg, axis=1, keepdims=True)
                dcs_t = jnp.where(row == hd, jnp.sum(dseg, axis=0, keepdims=True), dcs_t)
                wmat = xdt * dte_h
                dbg = dbg + _bdot_nt(wmat, dh)
                dw = _bdot(bg, dh)
                dxdt = dxdt + dw * dte_h
                ddte = jnp.sum(dw * xdt, axis=1, keepdims=True) * dte_h
                dcs_h = dcs_h - ddte
                dcsl = jnp.sum(ddte, axis=0, keepdims=True)
                dcsl = dcsl + jnp.sum(jnp.sum(dh * prev, axis=1, keepdims=True), axis=0, keepdims=True) * ecl_h
                dstate[hd] = dprev + dh * ecl_h
                dact_ref[:, hs] = dx + dxdt * dt_h
                ddt_h = jnp.sum(dxdt * x_h, axis=1, keepdims=True)
                ddt_mat = jnp.where(lane == hd, ddt_h, ddt_mat)
                dcs_mat = jnp.where(lane == hd, dcs_h, dcs_mat)
                dcsl_row = jnp.where(lane1 == hd, dcsl, dcsl_row)
            dact_ref[:, bsl] = dbg + _bdot_tn(dcb, cg)
            dact_ref[:, csl] = dcg + _bdot(dcb, bg)
        rowl = lax.broadcasted_iota(jnp.int32, (CHUNK, LANE), 0)
        dcs_mat = dcs_mat - dcs_t.T + jnp.where(rowl == CHUNK - 1, dcsl_row, 0.0)
        dda = _hdot(upper, dcs_mat)
        ddt_mat = ddt_mat + dda * a_row
        da_row = jnp.sum(dda * dtp, axis=0, keepdims=True)
        ddt_raw = ddt_mat * _sigmoid(dt_ref[...] + dtb_ref[...])
        ddt_ref[...] = ddt_raw
        dpar_ref[0:1, :] += jnp.sum(ddt_raw, axis=0, keepdims=True)
        dpar_ref[1:2, :] += da_row * a_row
        dpar_ref[2:3, :] += dd_row

    blk = lambda i: N_CHUNKS - 1 - i
    return pl.pallas_call(
        body, name="ssd_bwd", grid=(N_CHUNKS,),
        in_specs=[pl.BlockSpec((CHUNK, D_CONV), lambda i: (blk(i), 0)), pl.BlockSpec((CHUNK, D_SSM), lambda i: (blk(i), zcol)),
                  pl.BlockSpec((CHUNK, LANE), lambda i: (blk(i), dtcol)), pl.BlockSpec((CHUNK, D_SSM), lambda i: (blk(i), 0)),
                  pl.BlockSpec((1, SSM_HEADS, SSM_STATE, SSM_HEAD_DIM), lambda i: (blk(i), 0, 0, 0)),
                  pl.BlockSpec((CHUNK, D_SSM), lambda i: (blk(i), 0)),
                  _full((1, LANE)), _full((1, LANE)), _full((1, LANE)), _full((1, D_SSM))],
        out_specs=[pl.BlockSpec((CHUNK, D_CONV), lambda i: (blk(i), 0)), pl.BlockSpec((CHUNK, LANE), lambda i: (blk(i), 0)),
                   pl.BlockSpec((CHUNK, D_SSM), lambda i: (blk(i), 0)), _full((1, D_SSM)), _full((8, LANE))],
        out_shape=[jax.ShapeDtypeStruct((SEQ, D_CONV), f32), jax.ShapeDtypeStruct((SEQ, LANE), f32),
                   jax.ShapeDtypeStruct((SEQ, D_SSM), f32), jax.ShapeDtypeStruct((1, D_SSM), f32),
                   jax.ShapeDtypeStruct((8, LANE), f32)],
        scratch_shapes=[pltpu.VMEM((SSM_HEADS, SSM_STATE, SSM_HEAD_DIM), f32), pltpu.VMEM((CHUNK, D_SSM), f32)],
        compiler_params=_params("arbitrary"),
    )(act, proj, proj, ypre, states, d_out, dt_bias, a_log, d_skip, norm_g)


def out_fwd(x, attn, ssm, w_out, tm=512):
    def body(x_ref, a_ref, s_ref, w_ref, o_ref):
        o_ref[...] = x_ref[...] + _bdot(a_ref[...], w_ref[:D_ATTN, :]) + _bdot(s_ref[...], w_ref[D_ATTN:, :])

    tok = lambda w_: pl.BlockSpec((tm, w_), lambda i: (i, 0))
    return pl.pallas_call(
        body, name="out_fwd", grid=(SEQ // tm,),
        in_specs=[tok(D_MODEL), tok(D_ATTN), tok(D_SSM), _full((D_MODEL, D_MODEL))],
        out_specs=tok(D_MODEL), out_shape=jax.ShapeDtypeStruct((SEQ, D_MODEL), f32),
        compiler_params=_params("arbitrary"),
    )(x, attn, ssm, w_out)


def out_bwd(dx1, attn, ssm, w_out, tm=512):
    nt = SEQ // tm

    def body(d_ref, a_ref, s_ref, w_ref, da_ref, ds_ref, dw_ref, dw16_ref):
        i = pl.program_id(0)

        @pl.when(i == 0)
        def _():
            dw_ref[...] = jnp.zeros_like(dw_ref)

        d = d_ref[...].astype(bf16)
        dcat = _bdot_nt(d, w_ref[...])
        da_ref[...] = dcat[:, :D_ATTN]
        ds_ref[...] = dcat[:, D_ATTN:]
        dw_ref[:D_ATTN, :] += _bdot_tn(a_ref[...], d)
        dw_ref[D_ATTN:, :] += _bdot_tn(s_ref[...], d)

        @pl.when(i == nt - 1)
        def _():
            dw16_ref[...] = dw_ref[...].astype(bf16)

    tok = lambda w_: pl.BlockSpec((tm, w_), lambda i: (i, 0))
    return pl.pallas_call(
        body, name="out_bwd", grid=(nt,),
        in_specs=[tok(D_MODEL), tok(D_ATTN), tok(D_SSM), _full((D_MODEL, D_MODEL))],
        out_specs=[tok(D_ATTN), tok(D_SSM), _full((D_MODEL, D_MODEL)), _full((D_MODEL, D_MODEL))],
        out_shape=[jax.ShapeDtypeStruct((SEQ, D_ATTN), f32), jax.ShapeDtypeStruct((SEQ, D_SSM), f32),
                   jax.ShapeDtypeStruct((D_MODEL, D_MODEL), f32), jax.ShapeDtypeStruct((D_MODEL, D_MODEL), bf16)],
        compiler_params=_params("arbitrary"),
    )(dx1, attn, ssm, w_out)


def mlp_fwd(x1, g, w_up, w_down, tm=512):
    def body(x_ref, g_ref, wu_ref, wd_ref, o_ref, u_ref, h_scr):
        j = pl.program_id(1)

        @pl.when(j == 0)
        def _():
            xv = x_ref[...]
            h_scr[...] = (xv * _rms(xv) * g_ref[...]).astype(bf16)
            o_ref[...] = xv

        u = jnp.dot(h_scr[...], wu_ref[...], preferred_element_type=f32)
        u_ref[...] = u
        a = jnp.square(jnp.maximum(u, 0.0))
        o_ref[...] += _bdot(a, wd_ref[...])

    return pl.pallas_call(
        body, name="mlp_fwd", grid=(SEQ // tm, N_CHIPS),
        in_specs=[pl.BlockSpec((tm, D_MODEL), lambda i, j: (i, 0)), _full((1, D_MODEL)),
                  pl.BlockSpec((None, D_MODEL, FF_TILE), lambda i, j: (j, 0, 0)),
                  pl.BlockSpec((None, FF_TILE, D_MODEL), lambda i, j: (j, 0, 0))],
        out_specs=[pl.BlockSpec((tm, D_MODEL), lambda i, j: (i, 0)), pl.BlockSpec((tm, FF_TILE), lambda i, j: (i, j))],
        out_shape=[jax.ShapeDtypeStruct((SEQ, D_MODEL), f32), jax.ShapeDtypeStruct((SEQ, D_FF), f32)],
        scratch_shapes=[pltpu.VMEM((tm, D_MODEL), bf16)],
        compiler_params=_params("arbitrary", "arbitrary"),
    )(x1, g, w_up, w_down)


def mlp_bwd_data(dx2, u, x1, g, w_up, w_down, tm=512):
    def body(d_ref, u_ref, x_ref, g_ref, wu_ref, wd_ref, dx_ref, du_ref, dg_ref, dh_scr):
        i, j = pl.program_id(0), pl.program_id(1)

        @pl.when(jnp.logical_and(i == 0, j == 0))
        def _():
            dg_ref[...] = jnp.zeros_like(dg_ref)

        @pl.when(j == 0)
        def _():
            dh_scr[...] = jnp.zeros_like(dh_scr)

        da = _bdot_nt(d_ref[...], wd_ref[...])
        du = (da * (2.0 * jnp.maximum(u_ref[...], 0.0))).astype(bf16)
        du_ref[...] = du
        dh_scr[...] += _bdot_nt(du, wu_ref[...])

        @pl.when(j == N_CHIPS - 1)
        def _():
            xv = x_ref[...]
            r = _rms(xv)
            xhat = xv * r
            dh = dh_scr[...]
            dg_ref[...] += jnp.sum(dh * xhat, axis=0, keepdims=True)
            dx_ref[...] = d_ref[...] + _rms_bwd(dh, xhat, r, g_ref[...])

    return pl.pallas_call(
        body, name="mlp_bwd_data", grid=(SEQ // tm, N_CHIPS),
        in_specs=[pl.BlockSpec((tm, D_MODEL), lambda i, j: (i, 0)), pl.BlockSpec((tm, FF_TILE), lambda i, j: (i, j)),
                  pl.BlockSpec((tm, D_MODEL), lambda i, j: (i, 0)), _full((1, D_MODEL)),
                  pl.BlockSpec((None, D_MODEL, FF_TILE), lambda i, j: (j, 0, 0)),
                  pl.BlockSpec((None, FF_TILE, D_MODEL), lambda i, j: (j, 0, 0))],
        out_specs=[pl.BlockSpec((tm, D_MODEL), lambda i, j: (i, 0)), pl.BlockSpec((tm, FF_TILE), lambda i, j: (i, j)),
                   _full((1, D_MODEL))],
        out_shape=[jax.ShapeDtypeStruct((SEQ, D_MODEL), f32), jax.ShapeDtypeStruct((SEQ, D_FF), bf16),
                   jax.ShapeDtypeStruct((1, D_MODEL), f32)],
        scratch_shapes=[pltpu.VMEM((tm, D_MODEL), f32)],
        compiler_params=_params("arbitrary", "arbitrary"),
    )(dx2, u, x1, g, w_up, w_down)


def mlp_bwd_weights(dx2, u, du, x1, g, tm=512):
    nt = SEQ // tm

    def body(d_ref, u_ref, du_ref, x_ref, g_ref, dwu_ref, dwd_ref, dwu16_ref, dwd16_ref):
        i = pl.program_id(1)

        @pl.when(i == 0)
        def _():
            dwu_ref[...] = jnp.zeros_like(dwu_ref)
            dwd_ref[...] = jnp.zeros_like(dwd_ref)

        xv = x_ref[...]
        h = xv * _rms(xv) * g_ref[...]
        dwu_ref[...] += _bdot_tn(h, du_ref[...])
        a = jnp.square(jnp.maximum(u_ref[...], 0.0))
        dwd_ref[...] += _bdot_tn(a, d_ref[...])

        @pl.when(i == nt - 1)
        def _():
            dwu16_ref[...] = dwu_ref[...].astype(bf16)
            dwd16_ref[...] = dwd_ref[...].astype(bf16)

    up = pl.BlockSpec((None, D_MODEL, FF_TILE), lambda j, i: (j, 0, 0))
    down = pl.BlockSpec((None, FF_TILE, D_MODEL), lambda j, i: (j, 0, 0))
    return pl.pallas_call(
        body, name="mlp_bwd_weights", grid=(N_CHIPS, nt),
        in_specs=[pl.BlockSpec((tm, D_MODEL), lambda j, i: (i, 0)), pl.BlockSpec((tm, FF_TILE), lambda j, i: (i, j)),
                  pl.BlockSpec((tm, FF_TILE), lambda j, i: (i, j)), pl.BlockSpec((tm, D_MODEL), lambda j, i: (i, 0)),
                  _full((1, D_MODEL))],
        out_specs=[up, down, up, down],
        out_shape=[jax.ShapeDtypeStruct((N_CHIPS, D_MODEL, FF_TILE), f32), jax.ShapeDtypeStruct((N_CHIPS, FF_TILE, D_MODEL), f32),
                   jax.ShapeDtypeStruct((N_CHIPS, D_MODEL, FF_TILE), bf16), jax.ShapeDtypeStruct((N_CHIPS, FF_TILE, D_MODEL), bf16)],
        compiler_params=_params("arbitrary", "arbitrary"),
    )(dx2, u, du, x1, g)


def loss_head(y, target, tm=512):
    def body(y_ref, t_ref, dy_ref, l_ref):
        @pl.when(pl.program_id(0) == 0)
        def _():
            l_ref[...] = jnp.zeros_like(l_ref)

        d = y_ref[...] - t_ref[...]
        dy_ref[...] = d * (1.0 / D_MODEL)
        part = jnp.sum(jnp.mean(d * d, axis=-1, keepdims=True), axis=0, keepdims=True)
        l_ref[...] += 0.5 * part

    tok = pl.BlockSpec((tm, D_MODEL), lambda i: (i, 0))
    return pl.pallas_call(
        body, name="loss_head", grid=(SEQ // tm,), in_specs=[tok, tok], out_specs=[tok, _full((1, 1))],
        out_shape=[jax.ShapeDtypeStruct((SEQ, D_MODEL), f32), jax.ShapeDtypeStruct((1, 1), f32)],
        compiler_params=_params("arbitrary"),
    )(y, target)


def _pad_lane(v):
    return jnp.pad(v, (0, LANE - v.shape[0]))[None, :]


def local_step(x, target, w, prov):
    bucket = jnp.asarray(_bucket_table())
    bias = bias_build(w["rel_bias"], bucket)
    saved = []
    for l in range(DEPTH):
        g_mix = w["mix_norm_g"][l][None, :] + prov.stage(("begin", l), x)
        w_in = prov.w_in(l, x)
        proj = in_fwd(x, g_mix, w_in)
        qg, kg = w["q_gain"][l][None, :], w["k_gain"][l][None, :]
        attn = attn_fwd(proj, qg, kg, w["sinks"][l], bias)
        conv_b = w["conv_b"][l][None, :]
        act = conv_fwd(proj, w["conv_w"][l], conv_b)
        dtb = _pad_lane(w["dt_bias"][l]) + prov.stage(("mid", l), act)
        alog, dsk = _pad_lane(w["a_log"][l]), _pad_lane(w["d_skip"][l])
        ng = w["ssm_norm_g"][l][None, :]
        ssm, ypre, states = ssd_fwd(act, proj, dtb, alog, dsk, ng)
        tok = prov.stage(("pre_out", l), ssm)
        w_out = prov.w_out(l, ssm) + jnp.asarray(tok, bf16)
        x1 = out_fwd(x, attn, ssm, w_out)
        g_mlp = w["mlp_norm_g"][l][None, :] + prov.stage(("pre_mlp", l), x1)
        w_up, w_down = prov.mlp(l, x1)
        x2, u = mlp_fwd(x1, g_mlp, w_up, w_down)
        saved.append(dict(x=x, proj=proj, attn=attn, act=act, ssm=ssm, ypre=ypre, states=states, x1=x1, u=u,
                          g_mix=g_mix, qg=qg, kg=kg, conv_b=conv_b, dtb=dtb, alog=alog, dsk=dsk, ng=ng, g_mlp=g_mlp,
                          w_in=w_in, w_out=w_out, w_up=w_up, w_down=w_down))
        x = x2
    dx, loss = loss_head(x, target)
    grads = [None] * DEPTH
    dbands = [None] * DEPTH
    tok = 0.0
    for l in reversed(range(DEPTH)):
        s = saved[l]
        g_mlp = s["g_mlp"] + tok
        dx1, du, dg_mlp = mlp_bwd_data(dx, s["u"], s["x1"], g_mlp, s["w_up"], s["w_down"])
        dw_up, dw_down, dw_up16, dw_down16 = mlp_bwd_weights(dx, s["u"], du, s["x1"], g_mlp)
        tok = prov.grads(("mlp", l), dict(w_up=(dw_up, dw_up16), w_down=(dw_down, dw_down16)), dx1)
        dattn, dssm, dw_out, dw_out16 = out_bwd(dx1, s["attn"], s["ssm"], s["w_out"])
        dact, ddt, dz, dng, dpar = ssd_bwd(s["act"], s["proj"], s["ypre"], s["states"], dssm, s["dtb"] + tok, s["alog"],
                                           s["dsk"], s["ng"])
        conv_b = s["conv_b"] + prov.stage(("bwd_mid", l), dact)
        dxbc, dconv_w, dconv_b = conv_bwd(s["proj"], dact, w["conv_w"][l], conv_b)
        dq, dk, dv, dband, dsink, dqg, dkg = attn_bwd(s["proj"], dattn, s["qg"], s["kg"], w["sinks"][l], bias)
        dx, dw_in, dg_mix = in_bwd(dq, dz, dxbc, dk, dv, ddt, s["x"], s["g_mix"], s["w_in"], dx1)
        tok = prov.grads(("mix", l), dict(w_in=split_w_in_grad(dw_in), w_out=(dw_out, dw_out16)), dx)
        dbands[l] = dband
        grads[l] = dict(mix_norm_g=dg_mix[0], q_gain=dqg[0], k_gain=dkg[0], sinks=dsink[0, :N_Q_HEADS],
                        conv_w=dconv_w, conv_b=dconv_b[0], dt_bias=dpar[0, :SSM_HEADS], a_log=dpar[1, :SSM_HEADS],
                        d_skip=dpar[2, :SSM_HEADS], ssm_norm_g=dng[0], mlp_norm_g=dg_mlp[0])
    out = {k: jnp.stack([grads[l][k] for l in range(DEPTH)]) for k in grads[0]}
    out["rel_bias"] = bias_bwd(dbands[0], dbands[1], bucket)[:, :N_Q_HEADS]
    return loss, dx, out, tok


MESH = pl.DeviceIdType.MESH
HBM = pl.BlockSpec(memory_space=pltpu.HBM)
N_PEER_CHIPS = N_CHIPS - 1
N_DEVICES = 8


def _coords():
    return lax.axis_index("x"), lax.axis_index("y"), lax.axis_index("c")


def _peer_chips(x, y):
    return [(1 - x, y), (x, 1 - y), (1 - x, 1 - y)]


def _remote(src, dst, send_sem, recv_sem, device):
    return pltpu.make_async_remote_copy(src_ref=src, dst_ref=dst, send_sem=send_sem, recv_sem=recv_sem,
                                        device_id=device, device_id_type=MESH)


SEM = pl.BlockSpec(memory_space=pltpu.SEMAPHORE)
ANY = pl.BlockSpec(memory_space=pl.ANY)
DATAFLOW = pltpu.SideEffectType.DATAFLOW_SIDE_EFFECTING


def _gather_copies(kind, src_refs, land_refs, ssem, rsem):
    x, y, c = _coords()
    k_me = 2 * x + y
    cps = []
    for p, land in enumerate(land_refs):
        hr = land.shape[1] // 2
        rows = pl.ds(c * hr, hr)
        for j, chip in enumerate(_peer_chips(x, y)):
            i = 3 * p + j
            if kind == "ici":
                cps.append(_remote(src_refs[p].at[rows, :], land.at[k_me, rows, :], ssem.at[i], rsem.at[i], (*chip, c)))
            else:
                got = land.at[2 * chip[0] + chip[1], rows, :]
                cps.append(_remote(got, got, ssem.at[i], rsem.at[i], (x, y, 1 - c)))
    return cps


def gather_now(srcs, conv):
    n = len(srcs)

    def body(*refs):
        src_refs, conv_ref = refs[:n], refs[n]
        lands, gconv = refs[n + 1:2 * n + 1], refs[2 * n + 1]
        ssem, rsem, fsem, frsem, csem, crsem = refs[2 * n + 2:]
        x, y, c = _coords()
        k_me = 2 * x + y
        chips = _peer_chips(x, y)
        ici = _gather_copies("ici", src_refs, lands, ssem, rsem)
        relay = _gather_copies("relay", src_refs, lands, fsem, frsem)
        conv_cps = [_remote(conv_ref, gconv.at[k_me], csem.at[j], crsem.at[j], (*chip, c)) for j, chip in enumerate(chips)]
        for cp in ici + conv_cps:
            cp.start()
        for cp, fw in zip(ici, relay):
            cp.wait_recv()
            fw.start()
        for cp in conv_cps + relay:
            cp.wait_recv()
        for cp in ici + relay + conv_cps:
            cp.wait_send()

    out_shape = [jax.ShapeDtypeStruct((N_CHIPS,) + s.shape, s.dtype) for s in srcs]
    out_shape.append(jax.ShapeDtypeStruct((N_CHIPS,) + conv.shape, conv.dtype))
    sems = lambda k: pltpu.SemaphoreType.DMA((k,))
    return pl.pallas_call(
        body, name="gather_now", out_shape=out_shape, in_specs=[HBM] * (n + 1), out_specs=[HBM] * (n + 1),
        scratch_shapes=[sems(3 * n), sems(3 * n), sems(3 * n), sems(3 * n), sems(N_PEER_CHIPS), sems(N_PEER_CHIPS)],
    )(*srcs, conv)


def _gather_maker(kind, n_src):
    def make(refs, ssem, rsem):
        cps = _gather_copies(kind, refs[:n_src], refs[n_src:], ssem, rsem)
        return cps, cps
    return make


def _scatter_maker(n):
    def make(refs, ssem, rsem):
        x, y, c = _coords()
        k_me = 2 * x + y
        sends, arrivals = [], []
        for p in range(n):
            src, land = refs[p], refs[n + p]
            sends.append(_remote(src.at[k_me, 1 - c], land.at[0], ssem.at[7 * p], rsem.at[7 * p], (x, y, 1 - c)))
            for j, chip in enumerate(_peer_chips(x, y)):
                for cc in range(2):
                    sends.append(_remote(src.at[2 * chip[0] + chip[1], cc], land.at[1 + 2 * j + c],
                                         ssem.at[7 * p + 1 + 2 * j + cc], rsem.at[7 * p + 1 + 2 * j + c], (*chip, cc)))
            for s in range(7):
                arrivals.append(_remote(land.at[s], land.at[s], ssem.at[7 * p + s], rsem.at[7 * p + s], (x, y, 1 - c)))
        return sends, arrivals
    return make


def _share_maker(n):
    def make(refs, ssem, rsem):
        x, y, c = _coords()
        sends = [_remote(refs[p].at[c], refs[p].at[c], ssem.at[p], rsem.at[p], (x, y, 1 - c)) for p in range(n)]
        arrivals = [_remote(refs[p].at[1 - c], refs[p].at[1 - c], ssem.at[p], rsem.at[p], (x, y, 1 - c)) for p in range(n)]
        return sends, arrivals
    return make


def split_start(name, make, n_sems, operands, after):
    n = len(operands)

    def body(*refs):
        ssem, rsem, token = refs[n + 1], refs[n + 2], refs[-1]
        for cp in make(refs[:n], ssem, rsem)[0]:
            cp.start()
        token[...] = jnp.zeros_like(token)

    ops = [pltpu.with_memory_space_constraint(a, pltpu.HBM) for a in operands]
    outs = pl.pallas_call(
        body, name=name,
        out_shape=(pltpu.SemaphoreType.DMA((n_sems,)), pltpu.SemaphoreType.DMA((n_sems,)),
                   *[pltpu.HBM(a.shape, a.dtype) for a in ops], jax.ShapeDtypeStruct((8, LANE), f32)),
        in_specs=[HBM] * n + [ANY], out_specs=(SEM, SEM, *[HBM] * n, pl.BlockSpec(memory_space=pltpu.VMEM)),
        input_output_aliases={i: 2 + i for i in range(n)},
        compiler_params=pltpu.CompilerParams(has_side_effects=DATAFLOW),
    )(*ops, after)
    return dict(name=name, make=make, ssem=outs[0], rsem=outs[1], operands=outs[2:2 + n], token=outs[-1][0, 0])


def split_wait(handle, after):
    n = len(handle["operands"])

    def body(*refs):
        sends, arrivals = handle["make"](refs[:n], refs[n], refs[n + 1])
        for cp in sends:
            cp.wait_send()
        for cp in arrivals:
            cp.wait_recv()

    outs = pl.pallas_call(
        body, name=handle["name"].replace("start", "wait"),
        out_shape=tuple(pltpu.HBM(a.shape, a.dtype) for a in handle["operands"]),
        in_specs=[HBM] * n + [SEM, SEM, ANY], out_specs=tuple([HBM] * n),
        input_output_aliases={i: i for i in range(n)},
        compiler_params=pltpu.CompilerParams(has_side_effects=DATAFLOW),
    )(*handle["operands"], handle["ssem"], handle["rsem"], after)
    return list(outs)


def piece_sum(g, recv, kc_arr):
    _, _, rb, cc = g.shape
    tr = min(256, rb)

    def body(kc_ref, g_ref, r_ref, o_ref):
        acc = g_ref[...]
        for s in range(7):
            acc = acc + r_ref[s].astype(f32)
        o_ref[...] = acc

    return pl.pallas_call(
        body, name="piece_sum",
        grid_spec=pltpu.PrefetchScalarGridSpec(
            num_scalar_prefetch=1, grid=(rb // tr,),
            in_specs=[pl.BlockSpec((None, None, tr, cc), lambda r, kc: (kc[0], kc[1], r, 0)),
                      pl.BlockSpec((7, tr, cc), lambda r, kc: (0, r, 0))],
            out_specs=pl.BlockSpec((None, tr, cc), lambda r, kc: (kc[1], r, 0))),
        out_shape=jax.ShapeDtypeStruct((2, rb, cc), f32),
        compiler_params=_params("arbitrary"),
    )(kc_arr, g, recv)


def small_all_reduce(vec):
    def body(v_ref, o_ref, gat, ssem, rsem):
        x, y, c = _coords()
        me = 4 * x + 2 * y + c
        gat[me] = v_ref[...]
        sends = []
        for t in range(1, N_DEVICES):
            peer = (x ^ (t >> 2), y ^ ((t >> 1) & 1), c ^ (t & 1))
            cp = _remote(v_ref, gat.at[me], ssem.at[t - 1], rsem.at[t - 1], peer)
            cp.start()
            sends.append(cp)
        for t in range(1, N_DEVICES):
            peer = (x ^ (t >> 2), y ^ ((t >> 1) & 1), c ^ (t & 1))
            slot = gat.at[4 * peer[0] + 2 * peer[1] + peer[2]]
            _remote(slot, slot, ssem.at[t - 1], rsem.at[t - 1], peer).wait_recv()
        for cp in sends:
            cp.wait_send()
        acc = gat[0]
        for d in range(1, N_DEVICES):
            acc = acc + gat[d]
        o_ref[...] = acc

    return pl.pallas_call(
        body, name="small_all_reduce", out_shape=jax.ShapeDtypeStruct(vec.shape, vec.dtype),
        in_specs=[pl.BlockSpec(memory_space=pltpu.VMEM)], out_specs=pl.BlockSpec(memory_space=pltpu.VMEM),
        scratch_shapes=[pltpu.VMEM((N_DEVICES,) + vec.shape, vec.dtype), pltpu.SemaphoreType.DMA((N_DEVICES - 1,)),
                        pltpu.SemaphoreType.DMA((N_DEVICES - 1,))],
    )(vec)


def _adamw_math(w, g, m, v):
    m_new = ADAM_B1 * m + (1.0 - ADAM_B1) * g
    v_new = ADAM_B2 * v + (1.0 - ADAM_B2) * jnp.square(g)
    m_hat = m_new / (1.0 - ADAM_B1 ** ADAM_STEP)
    v_hat = v_new / (1.0 - ADAM_B2 ** ADAM_STEP)
    delta = -ADAM_LR * (m_hat / (jnp.sqrt(v_hat) + ADAM_EPS) + ADAM_WD * w)
    return delta, m_new, v_new


def adamw_shard(w, g0, g1, m, v):
    depth, rows, cols = w.shape
    half = rows // 2
    tr = min(256, half)
    nr = half // tr

    def body(w_ref, g0_ref, g1_ref, m_ref, v_ref, go_ref, d_ref, nm_ref, nv_ref):
        gv = jnp.where(pl.program_id(0) == 0, g0_ref[...], g1_ref[...])
        go_ref[...] = gv
        d_ref[...], nm_ref[...], nv_ref[...] = _adamw_math(w_ref[...], gv, m_ref[...], v_ref[...])

    spec = pl.BlockSpec((None, tr, cols), lambda l, h, r: (l, h * nr + r, 0))
    g0spec = pl.BlockSpec((None, tr, cols), lambda l, h, r: (jnp.where(l == 0, h, 1), jnp.where(l == 0, r, nr - 1), 0))
    g1spec = pl.BlockSpec((None, tr, cols), lambda l, h, r: (jnp.where(l == 1, h, 0), jnp.where(l == 1, r, 0), 0))
    return pl.pallas_call(
        body, name="adamw_shard", grid=(depth, 2, nr), in_specs=[spec, g0spec, g1spec, spec, spec], out_specs=[spec] * 4,
        out_shape=[jax.ShapeDtypeStruct(w.shape, f32)] * 4,
        compiler_params=_params("arbitrary", "arbitrary", "arbitrary"),
    )(w, g0, g1, m, v)


def adamw_small(w, g, m, v):
    def body(w_ref, g_ref, m_ref, v_ref, d_ref, nm_ref, nv_ref):
        d_ref[...], nm_ref[...], nv_ref[...] = _adamw_math(w_ref[...], g_ref[...], m_ref[...], v_ref[...])

    return pl.pallas_call(
        body, name="adamw_small", out_shape=[jax.ShapeDtypeStruct(w.shape, f32)] * 3,
    )(w, g, m, v)


WEIGHTS = ("mix_norm_g", "w_in", "q_gain", "k_gain", "sinks", "rel_bias", "conv_w", "conv_b", "dt_bias", "a_log", "d_skip",
           "ssm_norm_g", "w_out", "mlp_norm_g", "w_up", "w_down")
BIG = ("w_in", "w_out", "w_up", "w_down")
SMALL = tuple(n for n in WEIGHTS if n not in BIG)
PACK_COLS = 1024
PACK_ROWS = 16


def _pack(named):
    flat = jnp.concatenate([named[n].reshape(-1) for n in SMALL])
    return jnp.pad(flat, (0, PACK_ROWS * PACK_COLS - flat.shape[0])).reshape(PACK_ROWS, PACK_COLS)


def _unpack(buf, shapes):
    flat = buf.reshape(-1)
    out, at = {}, 0
    for n in SMALL:
        size = int(np.prod(shapes[n]))
        out[n] = flat[at:at + size].reshape(shapes[n])
        at += size
    return out


class _Exchange:
    GROUPS = {"A": (("w_up", 0), ("w_down", 0)), "B": (("w_in", 1), ("w_out", 1)), "C": (("w_up", 1), ("w_down", 1))}
    RELAY_AT = {("pre_out", 0): "A", ("pre_mlp", 0): "B", ("mid", 1): "C"}
    NEXT_GROUP = {"A": "B", "B": "C"}
    LAST = ("mix", 0)

    def __init__(self, wts, k_me, kc_arr):
        self.wts, self.k_me, self.kc_arr = wts, k_me, kc_arr
        self.own = {(n, l): wts[n][l].astype(bf16) for n in BIG for l in range(DEPTH)}
        now = gather_now([self.own["w_in", 0], self.own["w_out", 0]], wts["conv_w"])
        self.ready = {("w_in", 0): self._fill(now[0], self.own["w_in", 0]),
                      ("w_out", 0): self._fill(now[1], self.own["w_out", 0])}
        conv = self._fill(now[2], wts["conv_w"])
        self.conv_w = jnp.transpose(conv, (1, 2, 0, 3)).reshape(DEPTH, CONV_WIDTH, D_CONV)
        self.ici, self.relay = {}, {}
        self.gview, self.scatter, self.share, self.reduced = {}, [], [], {}
        self._start_ici("A", now[2])

    def _start_ici(self, g, after):
        srcs = [self.own[p] for p in self.GROUPS[g]]
        lands = [lax.empty((N_CHIPS,) + s.shape, s.dtype) for s in srcs]
        self.ici[g] = split_start("gather%s_ici_start" % g, _gather_maker("ici", len(srcs)), 3 * len(srcs), srcs + lands,
                                  after)
        return self.ici[g]["token"]

    def _fill(self, land, own):
        return lax.dynamic_update_slice(land, own[None], (self.k_me,) + (0,) * own.ndim)

    def stage(self, name, after):
        if name == ("begin", 0):
            return self.ici["A"]["token"]
        g = self.RELAY_AT.get(name)
        if g is None:
            return 0.0
        n = len(self.GROUPS[g])
        lands = split_wait(self.ici[g], after)[n:]
        self.relay[g] = split_start("gather%s_relay_start" % g, _gather_maker("relay", 0), 3 * n, lands, after)
        tok = self.relay[g]["token"]
        if g in self.NEXT_GROUP:
            tok = tok + self._start_ici(self.NEXT_GROUP[g], after)
        return tok

    def _get(self, piece, after):
        if piece not in self.ready:
            g = [k for k, pieces in self.GROUPS.items() if piece in pieces][0]
            lands = split_wait(self.relay[g], after)
            for p, land in zip(self.GROUPS[g], lands):
                self.ready[p] = self._fill(land, self.own[p])
        return self.ready[piece]

    def w_in(self, l, after):
        return align_w_in(self._get(("w_in", l), after))

    def w_out(self, l, after):
        return self._get(("w_out", l), after).reshape(D_MODEL, D_MODEL)

    def mlp(self, l, after):
        return self._get(("w_up", l), after), self._get(("w_down", l), after)

    def _view(self, n, g):
        _, rows, cols = self.wts[n].shape
        return g.reshape(N_CHIPS, 2, rows // 2, cols)

    def grads(self, name, arrays, after):
        if name == self.LAST:
            self.held = (name, arrays)
            return self._advance(after, 0)
        return self._scatter(name, arrays, after) + self._advance(after, 1)

    def flush(self, after):
        return self._scatter(*self.held, after)

    def _scatter(self, name, arrays, after):
        pieces = [(n, name[1]) for n in arrays]
        views = [self._view(n, g) for n, (g, _) in arrays.items()]
        sends = [g16.reshape(v.shape) for v, (_, g16) in zip(views, arrays.values())]
        self.gview.update(zip(pieces, views))
        lands = [lax.empty((7,) + v.shape[2:], bf16) for v in views]
        h = split_start("scatter_%s%d_start" % name, _scatter_maker(len(views)), 7 * len(views), sends + lands, after)
        self.scatter.append((pieces, h))
        return h["token"]

    def _take_share(self, after):
        pieces, h = self.share.pop(0)
        self.reduced.update(zip(pieces, split_wait(h, after)))

    def _take_scatter(self, after):
        pieces, h = self.scatter.pop(0)
        lands = split_wait(h, after)[len(pieces):]
        sums = [piece_sum(self.gview[p], land, self.kc_arr) for p, land in zip(pieces, lands)]
        hs = split_start(h["name"].replace("scatter", "share"), _share_maker(len(sums)), len(sums), sums, after)
        self.share.append((pieces, hs))
        return hs["token"]

    def _advance(self, after, newest):
        if self.share:
            self._take_share(after)
        return self._take_scatter(after) if len(self.scatter) > newest else 0.0

    def reduced_grads(self, names, after):
        want = [(n, l) for n in names for l in range(DEPTH)]
        while not all(p in self.reduced for p in want):
            if any(p in pieces for p in want for pieces, _ in self.share):
                self._take_share(after)
            else:
                self._take_scatter(after)
        return {n: [self.reduced[n, l] for l in range(DEPTH)] for n in names}


def kernel(x, mix_norm_g, w_in, q_gain, k_gain, sinks, rel_bias, conv_w, conv_b, dt_bias, a_log, d_skip, ssm_norm_g, w_out, mlp_norm_g, w_up, w_down, loss_target, m_mix_norm_g, m_w_in, m_q_gain, m_k_gain, m_sinks, m_rel_bias, m_conv_w, m_conv_b, m_dt_bias, m_a_log, m_d_skip, m_ssm_norm_g, m_w_out, m_mlp_norm_g, m_w_up, m_w_down, v_mix_norm_g, v_w_in, v_q_gain, v_k_gain, v_sinks, v_rel_bias, v_conv_w, v_conv_b, v_dt_bias, v_a_log, v_d_skip, v_ssm_norm_g, v_w_out, v_mlp_norm_g, v_w_up, v_w_down):
    wts = dict(mix_norm_g=mix_norm_g, w_in=w_in, q_gain=q_gain, k_gain=k_gain, sinks=sinks, rel_bias=rel_bias, conv_w=conv_w,
               conv_b=conv_b, dt_bias=dt_bias, a_log=a_log, d_skip=d_skip, ssm_norm_g=ssm_norm_g, w_out=w_out,
               mlp_norm_g=mlp_norm_g, w_up=w_up, w_down=w_down)
    mom = dict(mix_norm_g=m_mix_norm_g, w_in=m_w_in, q_gain=m_q_gain, k_gain=m_k_gain, sinks=m_sinks, rel_bias=m_rel_bias,
               conv_w=m_conv_w, conv_b=m_conv_b, dt_bias=m_dt_bias, a_log=m_a_log, d_skip=m_d_skip, ssm_norm_g=m_ssm_norm_g,
               w_out=m_w_out, mlp_norm_g=m_mlp_norm_g, w_up=m_w_up, w_down=m_w_down)
    var = dict(mix_norm_g=v_mix_norm_g, w_in=v_w_in, q_gain=v_q_gain, k_gain=v_k_gain, sinks=v_sinks, rel_bias=v_rel_bias,
               conv_w=v_conv_w, conv_b=v_conv_b, dt_bias=v_dt_bias, a_log=v_a_log, d_skip=v_d_skip, ssm_norm_g=v_ssm_norm_g,
               w_out=v_w_out, mlp_norm_g=v_mlp_norm_g, w_up=v_w_up, w_down=v_w_down)
    xi, yi, ci = _coords()
    k_me = 2 * xi + yi
    kc_arr = jnp.stack([k_me, ci]).astype(jnp.int32)

    prov = _Exchange(wts, k_me, kc_arr)
    small_w = {n: wts[n] for n in SMALL}
    small_w["conv_w"] = prov.conv_w
    loss, dx, grads, tok = local_step(x[0], loss_target[0], small_w, prov)
    loss = lax.psum(loss[0, 0], ("x", "y", "c"))

    small_shapes = {n: grads[n].shape for n in SMALL}
    small_sum = small_all_reduce(_pack(grads) + tok)
    tok = prov.flush(small_sum)
    small = _unpack(small_sum, small_shapes)
    cols = conv_w.shape[-1]
    small["conv_w"] = lax.dynamic_slice_in_dim(small["conv_w"], k_me * cols, cols, axis=2)
    g_out_d, d_out_d, m_out_d, v_out_d = {}, {}, {}, {}
    shard_shapes = {n: wts[n].shape for n in SMALL}
    d, nm, nv = adamw_small(_pack(wts), _pack(small) + tok, _pack(mom), _pack(var))
    for dst, buf in ((d_out_d, d), (m_out_d, nm), (v_out_d, nv)):
        dst.update(_unpack(buf, shard_shapes))
    g_out_d.update(small)

    after = d
    for names in (("w_up", "w_down"), ("w_in", "w_out")):
        for n, (g0, g1) in prov.reduced_grads(names, after).items():
            g_out_d[n], d_out_d[n], m_out_d[n], v_out_d[n] = adamw_shard(wts[n], g0, g1, mom[n], var[n])
            after = d_out_d[n]

    return (loss, dx[None], *[g_out_d[n] for n in WEIGHTS], *[d_out_d[n] for n in WEIGHTS],
            *[m_out_d[n] for n in WEIGHTS], *[v_out_d[n] for n in WEIGHTS])
```

```python
import functools

import numpy as np
import jax
import jax.numpy as jnp
from jax import lax
from jax.experimental import pallas as pl
from jax.experimental.pallas import tpu as pltpu

f32 = jnp.float32
bf16 = jnp.bfloat16

SEQ = 2048
D_MODEL = 1024
DEPTH = 2
HEAD_DIM = 64
N_Q_HEADS = 8
N_KV_HEADS = 2
Q_PER_KV = N_Q_HEADS // N_KV_HEADS
BLOCK = 128
N_BLOCKS = SEQ // BLOCK
N_BUCKETS = 32
MAX_DISTANCE = 128
SSM_HEADS = 8
SSM_HEAD_DIM = 64
SSM_GROUPS = 2
HEADS_PER_GROUP = SSM_HEADS // SSM_GROUPS
SSM_STATE = 128
CONV_WIDTH = 4
CHUNK = 128
N_CHUNKS = SEQ // CHUNK
D_FF = 4 * D_MODEL
D_ATTN = N_Q_HEADS * HEAD_DIM
D_KV = N_KV_HEADS * HEAD_DIM
D_SSM = SSM_HEADS * SSM_HEAD_DIM
D_BC = SSM_GROUPS * SSM_STATE
D_CONV = D_SSM + 2 * D_BC
D_IN = D_ATTN + 2 * D_KV + D_SSM + D_CONV + SSM_HEADS
EPS = 1e-6
NEG = -1e30
N_CHIPS = 4
FF_TILE = D_FF // N_CHIPS

LANE = 128
PW = D_ATTN + D_SSM + D_CONV + 2 * D_KV + LANE
OFF_Q, OFF_Z, OFF_X, OFF_K, OFF_V, OFF_DT = 0, 512, 1024, 2048, 2176, 2304

ADAM_LR = 0.001
ADAM_B1 = 0.9
ADAM_B2 = 0.999
ADAM_EPS = 1e-08
ADAM_WD = 0.01
ADAM_STEP = 10

VMEM_LIMIT = 56 * 1024 * 1024


def _params(*sem):
    return pltpu.CompilerParams(dimension_semantics=tuple(sem), vmem_limit_bytes=VMEM_LIMIT)


def _bdot(a, b):
    return jnp.dot(a.astype(bf16), b.astype(bf16), preferred_element_type=f32)


def _bdot_nt(a, b):
    return lax.dot_general(a.astype(bf16), b.astype(bf16), (((1,), (1,)), ((), ())), preferred_element_type=f32)


def _bdot_tn(a, b):
    return lax.dot_general(a.astype(bf16), b.astype(bf16), (((0,), (0,)), ((), ())), preferred_element_type=f32)


def _hdot(a, b):
    return jnp.dot(a, b, precision=lax.Precision.HIGHEST, preferred_element_type=f32)


def _sigmoid(x):
    return 1.0 / (1.0 + jnp.exp(-x))


def _softplus(x):
    return jnp.maximum(x, 0.0) + jnp.log1p(jnp.exp(-jnp.abs(x)))


def _rms(x):
    return lax.rsqrt(jnp.mean(x * x, axis=-1, keepdims=True) + EPS)


def _rms_bwd(dy, xhat, r, g):
    t = dy * g
    return r * (t - xhat * jnp.mean(t * xhat, axis=-1, keepdims=True))


def _full(shape):
    return pl.BlockSpec(shape, lambda *_: (0,) * len(shape))


def _bucket_table():
    qi = np.arange(BLOCK)[:, None]
    kj = np.arange(2 * BLOCK)[None, :]
    dist = qi + BLOCK - kj
    ok = (dist >= 0) & (dist < 128)
    d = np.clip(dist, 0, None)
    max_exact = N_BUCKETS // 2
    d_f = np.maximum(d, 1).astype(np.float32)
    large = max_exact + (np.log(d_f / np.float32(max_exact)) / np.float32(np.log(MAX_DISTANCE / max_exact))
                         * np.float32(N_BUCKETS - max_exact)).astype(np.int32)
    large = np.minimum(large, N_BUCKETS - 1)
    bucket = np.where(d < max_exact, d, large)
    return np.where(ok, bucket, -1).astype(np.int32)


def bias_build(rel_bias, bucket):
    def body(rel_ref, bkt_ref, o_ref):
        bkt = bkt_ref[...]
        for h in range(N_Q_HEADS):
            acc = jnp.where(bkt < 0, NEG, 0.0).astype(f32)
            for b in range(N_BUCKETS):
                acc = acc + jnp.where(bkt == b, rel_ref[b, h], 0.0)
            o_ref[h] = acc

    return pl.pallas_call(
        body, name="bias_build", out_shape=jax.ShapeDtypeStruct((N_Q_HEADS,) + bucket.shape, f32),
        in_specs=[pl.BlockSpec(memory_space=pltpu.SMEM), pl.BlockSpec(memory_space=pltpu.VMEM)],
        out_specs=pl.BlockSpec(memory_space=pltpu.VMEM),
    )(rel_bias, bucket)


def bias_bwd(dband0, dband1, bucket):
    def body(d0_ref, d1_ref, bkt_ref, o_ref):
        bkt = bkt_ref[...]
        o_ref[...] = jnp.zeros_like(o_ref)
        for h in range(N_Q_HEADS):
            d = d0_ref[h] + d1_ref[h]
            for b in range(N_BUCKETS):
                part = jnp.sum(jnp.where(bkt == b, d, 0.0), axis=1, keepdims=True)
                o_ref[b:b + 1, h:h + 1] = jnp.sum(part, axis=0, keepdims=True)

    return pl.pallas_call(
        body, name="bias_bwd", out_shape=jax.ShapeDtypeStruct((N_BUCKETS, LANE), f32),
    )(dband0, dband1, bucket)


W_IN_SHARD = D_IN // N_CHIPS
_ALIGNED_PIECES = ((0, 0, 512), (1, 190, 578), (2, 0, 124), (2, 124, 578), (3, 0, 570), (0, 512, 578), (1, 0, 62),
                   (1, 62, 190), (3, 570, 578))
_SHARD_PIECES = (((0, 512), (2048, 2114)), ((2114, 2176), (2176, 2304), (512, 900)), ((900, 1024), (1024, 1478)),
                 ((1478, 2048), (2304, 2312)))


def align_w_in(shards, tr=256):
    def body(s_ref, o_ref):
        parts = [s_ref[k, :, a:b] for k, a, b in _ALIGNED_PIECES]
        parts.append(jnp.zeros((tr, LANE - SSM_HEADS), s_ref.dtype))
        o_ref[...] = jnp.concatenate(parts, axis=-1)

    return pl.pallas_call(
        body, name="align_w_in", grid=(D_MODEL // tr,),
        in_specs=[pl.BlockSpec((N_CHIPS, tr, W_IN_SHARD), lambda i: (0, i, 0))],
        out_specs=pl.BlockSpec((tr, PW), lambda i: (i, 0)),
        out_shape=jax.ShapeDtypeStruct((D_MODEL, PW), shards.dtype),
        compiler_params=_params("arbitrary"),
    )(shards)


def split_w_in_grad(dw, tr=256):
    def body(d_ref, o_ref, o16_ref):
        for k, pieces in enumerate(_SHARD_PIECES):
            part = jnp.concatenate([d_ref[:, a:b] for a, b in pieces], axis=-1)
            o_ref[k] = part
            o16_ref[k] = part.astype(bf16)

    spec = pl.BlockSpec((N_CHIPS, tr, W_IN_SHARD), lambda i: (0, i, 0))
    return pl.pallas_call(
        body, name="split_w_in_grad", grid=(D_MODEL // tr,),
        in_specs=[pl.BlockSpec((tr, PW), lambda i: (i, 0))], out_specs=[spec, spec],
        out_shape=[jax.ShapeDtypeStruct((N_CHIPS, D_MODEL, W_IN_SHARD), f32),
                   jax.ShapeDtypeStruct((N_CHIPS, D_MODEL, W_IN_SHARD), bf16)],
        compiler_params=_params("arbitrary"),
    )(dw)

def in_fwd(x, g, w, tm=256):
    def body(x_ref, g_ref, w_ref, o_ref):
        xv = x_ref[...]
        h = xv * _rms(xv) * g_ref[...]
        o_ref[...] = _bdot(h, w_ref[...])

    return pl.pallas_call(
        body, name="in_fwd", grid=(SEQ // tm,),
        in_specs=[pl.BlockSpec((tm, D_MODEL), lambda i: (i, 0)), _full((1, D_MODEL)), _full((D_MODEL, PW))],
        out_specs=pl.BlockSpec((tm, PW), lambda i: (i, 0)),
        out_shape=jax.ShapeDtypeStruct((SEQ, PW), f32),
        compiler_params=_params("arbitrary"),
    )(x, g, w)


def in_bwd(dq, dz, dxbc, dk, dv, ddt, x, g, w, dres, tm=256):
    def body(dq_ref, dz_ref, dx_ref, dk_ref, dv_ref, ddt_ref, x_ref, g_ref, w_ref, dres_ref, o_ref, dw_ref, dg_ref):
        i = pl.program_id(0)

        @pl.when(i == 0)
        def _():
            dw_ref[...] = jnp.zeros_like(dw_ref)
            dg_ref[...] = jnp.zeros_like(dg_ref)

        dproj = jnp.concatenate([dq_ref[...], dz_ref[...], dx_ref[...], dk_ref[...], dv_ref[...], ddt_ref[...]],
                                axis=-1).astype(bf16)
        xv = x_ref[...]
        r = _rms(xv)
        xhat = xv * r
        gv = g_ref[...]
        h = xhat * gv
        dw_ref[...] += _bdot_tn(h, dproj)
        dh = _bdot_nt(dproj, w_ref[...])
        dg_ref[...] += jnp.sum(dh * xhat, axis=0, keepdims=True)
        o_ref[...] = dres_ref[...] + _rms_bwd(dh, xhat, r, gv)

    tok = lambda w_: pl.BlockSpec((tm, w_), lambda i: (i, 0))
    return pl.pallas_call(
        body, name="in_bwd", grid=(SEQ // tm,),
        in_specs=[tok(D_ATTN), tok(D_SSM), tok(D_CONV), tok(D_KV // 1), tok(D_KV // 1), tok(LANE), tok(D_MODEL),
                  _full((1, D_MODEL)), _full((D_MODEL, PW)), tok(D_MODEL)],
        out_specs=[tok(D_MODEL), _full((D_MODEL, PW)), _full((1, D_MODEL))],
        out_shape=[jax.ShapeDtypeStruct((SEQ, D_MODEL), f32), jax.ShapeDtypeStruct((D_MODEL, PW), f32),
                   jax.ShapeDtypeStruct((1, D_MODEL), f32)],
        compiler_params=_params("arbitrary"),
    )(dq, dz, dxbc, dk, dv, ddt, x, g, w, dres)


def _attn_softmax_t(qk, bias_t, sink, first, key_row):
    s = qk * (HEAD_DIM ** -0.5) + bias_t
    s = jnp.where(jnp.logical_and(first, key_row < BLOCK), NEG, s)
    m = jnp.maximum(jnp.max(s, axis=0, keepdims=True), sink)
    p = jnp.exp(s - m)
    psink = jnp.exp(sink - m)
    inv = 1.0 / (jnp.sum(p, axis=0, keepdims=True) + psink)
    return p * inv, psink * inv


def _rms_t(x_t):
    return lax.rsqrt(jnp.mean(x_t * x_t, axis=0, keepdims=True) + EPS)


def attn_fwd_t(proj, q_gain_col, k_gain, sinks, bias_t):
    kcol, vcol = OFF_K // D_KV, OFF_V // D_KV

    def body(q_ref, kc_ref, kp_ref, vc_ref, vp_ref, qg_ref, kg_ref, sink_ref, bias_ref, o_ref, ot_scr):
        n = pl.program_id(0)
        first = n == 0
        key_row = lax.broadcasted_iota(jnp.int32, (2 * BLOCK, BLOCK), 0)
        k2 = jnp.concatenate([kp_ref[...], kc_ref[...]], axis=0)
        v_t = jnp.concatenate([vp_ref[...], vc_ref[...]], axis=0).T
        q_t = q_ref[...].T
        qg = jnp.broadcast_to(qg_ref[...], (HEAD_DIM, BLOCK))
        kg = kg_ref[...]
        for hk in range(N_KV_HEADS):
            sl = slice(hk * HEAD_DIM, (hk + 1) * HEAD_DIM)
            kk = k2[:, sl]
            kn = (kk * _rms(kk) * kg).astype(bf16)
            vt = v_t[sl, :].astype(bf16)
            heads = range(hk * Q_PER_KV, (hk + 1) * Q_PER_KV)
            qns = []
            for h in heads:
                qh = q_t[h * HEAD_DIM:(h + 1) * HEAD_DIM, :]
                qns.append(qh * _rms_t(qh) * qg)
            scores = [_bdot(kn, qn) for qn in qns]
            for h, s in zip(heads, scores):
                p, _ = _attn_softmax_t(s, bias_ref[h], sink_ref[h], first, key_row)
                ot_scr[h * HEAD_DIM:(h + 1) * HEAD_DIM, :] = _bdot(vt, p)
        o_ref[...] = ot_scr[...].T

    prev = lambda n: jnp.maximum(n - 1, 0)
    return pl.pallas_call(
        body, name="attn_fwd", grid=(N_BLOCKS,),
        in_specs=[pl.BlockSpec((BLOCK, D_ATTN), lambda n: (n, 0)),
                  pl.BlockSpec((BLOCK, D_KV), lambda n: (n, kcol)), pl.BlockSpec((BLOCK, D_KV), lambda n: (prev(n), kcol)),
                  pl.BlockSpec((BLOCK, D_KV), lambda n: (n, vcol)), pl.BlockSpec((BLOCK, D_KV), lambda n: (prev(n), vcol)),
                  _full((HEAD_DIM, 1)), _full((1, HEAD_DIM)), pl.BlockSpec(memory_space=pltpu.SMEM),
                  _full((N_Q_HEADS, 2 * BLOCK, BLOCK))],
        out_specs=pl.BlockSpec((BLOCK, D_ATTN), lambda n: (n, 0)),
        out_shape=jax.ShapeDtypeStruct((SEQ, D_ATTN), f32),
        scratch_shapes=[pltpu.VMEM((D_ATTN, BLOCK), f32)],
        compiler_params=_params("arbitrary"),
    )(proj, proj, proj, proj, proj, q_gain_col, k_gain, sinks, bias_t)


def attn_bwd_t(proj, d_out, q_gain_col, k_gain, sinks, bias_t):
    kcol, vcol = OFF_K // D_KV, OFF_V // D_KV

    def body(q_ref, kc_ref, kp_ref, vc_ref, vp_ref, do_ref, qg_ref, kg_ref, sink_ref, bias_ref,
             dq_ref, dk_ref, dv_ref, dband_ref, dsink_ref, dqg_ref, dkg_ref, dkn_scr, dv_scr, dqt_scr, dsink_acc, dqg_acc):
        i = pl.program_id(0)
        first = i == N_BLOCKS - 1

        @pl.when(i == 0)
        def _():
            for ref in (dband_ref, dkg_ref, dkn_scr, dv_scr, dsink_acc, dqg_acc):
                ref[...] = jnp.zeros_like(ref)

        key_row = lax.broadcasted_iota(jnp.int32, (2 * BLOCK, BLOCK), 0)
        k2 = jnp.concatenate([kp_ref[...], kc_ref[...]], axis=0)
        v2 = jnp.concatenate([vp_ref[...], vc_ref[...]], axis=0)
        q_t = q_ref[...].T
        do_t = do_ref[...].T
        qg = jnp.broadcast_to(qg_ref[...], (HEAD_DIM, BLOCK))
        kg = kg_ref[...]
        scale = HEAD_DIM ** -0.5
        for hk in range(N_KV_HEADS):
            sl = slice(hk * HEAD_DIM, (hk + 1) * HEAD_DIM)
            kk = k2[:, sl]
            rk = _rms(kk)
            khat = kk * rk
            kn = (khat * kg).astype(bf16)
            vb = v2[:, sl].astype(bf16)
            dkn = jnp.zeros((2 * BLOCK, HEAD_DIM), f32)
            dvv = jnp.zeros((2 * BLOCK, HEAD_DIM), f32)
            heads = range(hk * Q_PER_KV, (hk + 1) * Q_PER_KV)
            rqs, qhats, qns, d_os = [], [], [], []
            for h in heads:
                hs = slice(h * HEAD_DIM, (h + 1) * HEAD_DIM)
                qh = q_t[hs, :]
                rqs.append(_rms_t(qh))
                qhats.append(qh * rqs[-1])
                qns.append((qhats[-1] * qg).astype(bf16))
                d_os.append(do_t[hs, :].astype(bf16))
            scores = [_bdot(kn, qn) for qn in qns]
            dps = [_bdot(vb, d_o) for d_o in d_os]
            ps, dss = [], []
            for h, s, dp in zip(heads, scores, dps):
                p, psink = _attn_softmax_t(s, bias_ref[h], sink_ref[h], first, key_row)
                delta = jnp.sum(p * dp, axis=0, keepdims=True)
                ds = p * (dp - delta)
                dband_ref[h] += ds
                dsink_acc[h:h + 1, :] += -(psink * delta)
                ps.append(p.astype(bf16))
                dss.append(ds.astype(bf16))
            dqns = [_bdot_tn(kn, ds) * scale for ds in dss]
            for ds, qn, p, d_o in zip(dss, qns, ps, d_os):
                dkn = dkn + _bdot_nt(ds, qn) * scale
                dvv = dvv + _bdot_nt(p, d_o)
            for h, dqn, rq, qhat in zip(heads, dqns, rqs, qhats):
                dqg_acc[...] += dqn * qhat
                t = dqn * qg
                dqt_scr[h * HEAD_DIM:(h + 1) * HEAD_DIM, :] = rq * (t - qhat * jnp.mean(t * qhat, axis=0, keepdims=True))
            dkn_cur = dkn[BLOCK:] + dkn_scr[:, sl]
            dkn_scr[:, sl] = dkn[:BLOCK]
            khat_c, rk_c = khat[BLOCK:], rk[BLOCK:]
            dkg_ref[...] += jnp.sum(dkn_cur * khat_c, axis=0, keepdims=True)
            dk_ref[:, sl] = _rms_bwd(dkn_cur, khat_c, rk_c, kg)
            dv_ref[:, sl] = dvv[BLOCK:] + dv_scr[:, sl]
            dv_scr[:, sl] = dvv[:BLOCK]
        dq_ref[...] = dqt_scr[...].T

        @pl.when(i == N_BLOCKS - 1)
        def _():
            dsink_ref[...] = jnp.sum(dsink_acc[...], axis=1, keepdims=True)
            dqg_ref[...] = jnp.sum(dqg_acc[...], axis=1, keepdims=True)

    blk = lambda i: N_BLOCKS - 1 - i
    prev = lambda i: jnp.maximum(N_BLOCKS - 2 - i, 0)
    return pl.pallas_call(
        body, name="attn_bwd", grid=(N_BLOCKS,),
        in_specs=[pl.BlockSpec((BLOCK, D_ATTN), lambda i: (blk(i), 0)),
                  pl.BlockSpec((BLOCK, D_KV), lambda i: (blk(i), kcol)), pl.BlockSpec((BLOCK, D_KV), lambda i: (prev(i), kcol)),
                  pl.BlockSpec((BLOCK, D_KV), lambda i: (blk(i), vcol)), pl.BlockSpec((BLOCK, D_KV), lambda i: (prev(i), vcol)),
                  pl.BlockSpec((BLOCK, D_ATTN), lambda i: (blk(i), 0)),
                  _full((HEAD_DIM, 1)), _full((1, HEAD_DIM)), pl.BlockSpec(memory_space=pltpu.SMEM),
                  _full((N_Q_HEADS, 2 * BLOCK, BLOCK))],
        out_specs=[pl.BlockSpec((BLOCK, D_ATTN), lambda i: (blk(i), 0)), pl.BlockSpec((BLOCK, D_KV), lambda i: (blk(i), 0)),
                   pl.BlockSpec((BLOCK, D_KV), lambda i: (blk(i), 0)), _full((N_Q_HEADS, 2 * BLOCK, BLOCK)),
                   _full((N_Q_HEADS, 1)), _full((HEAD_DIM, 1)), _full((1, HEAD_DIM))],
        out_shape=[jax.ShapeDtypeStruct((SEQ, D_ATTN), f32), jax.ShapeDtypeStruct((SEQ, D_KV), f32),
                   jax.ShapeDtypeStruct((SEQ, D_KV), f32), jax.ShapeDtypeStruct((N_Q_HEADS, 2 * BLOCK, BLOCK), f32),
                   jax.ShapeDtypeStruct((N_Q_HEADS, 1), f32), jax.ShapeDtypeStruct((HEAD_DIM, 1), f32),
                   jax.ShapeDtypeStruct((1, HEAD_DIM), f32)],
        scratch_shapes=[pltpu.VMEM((BLOCK, D_KV), f32), pltpu.VMEM((BLOCK, D_KV), f32), pltpu.VMEM((D_ATTN, BLOCK), f32),
                        pltpu.VMEM((N_Q_HEADS, BLOCK), f32), pltpu.VMEM((HEAD_DIM, BLOCK), f32)],
        compiler_params=_params("arbitrary"),
    )(proj, proj, proj, proj, proj, d_out, q_gain_col, k_gain, sinks, bias_t)


def _shift_down(u, s, row):
    if s == 0:
        return u
    return jnp.where(row >= s, pltpu.roll(u, s, 0), 0.0)


def _shift_up(u, s, row):
    if s == 0:
        return u
    return jnp.where(row < SEQ - s, pltpu.roll(u, SEQ - s, 0), 0.0)


def conv_fwd(proj, conv_w, conv_b):
    xcol = OFF_X // LANE

    def body(u_ref, w_ref, b_ref, o_ref):
        u = u_ref[...]
        row = lax.broadcasted_iota(jnp.int32, u.shape, 0)
        pre = b_ref[...] + jnp.zeros_like(u)
        for k in range(CONV_WIDTH):
            pre = pre + w_ref[k:k + 1, :] * _shift_down(u, CONV_WIDTH - 1 - k, row)
        o_ref[...] = pre * _sigmoid(pre)

    return pl.pallas_call(
        body, name="conv_fwd", grid=(D_CONV // LANE,),
        in_specs=[pl.BlockSpec((SEQ, LANE), lambda j: (0, xcol + j)), pl.BlockSpec((CONV_WIDTH, LANE), lambda j: (0, j)),
                  pl.BlockSpec((1, LANE), lambda j: (0, j))],
        out_specs=pl.BlockSpec((SEQ, LANE), lambda j: (0, j)),
        out_shape=jax.ShapeDtypeStruct((SEQ, D_CONV), f32),
        compiler_params=_params("arbitrary"),
    )(proj, conv_w, conv_b)


def conv_bwd(proj, d_act, conv_w, conv_b):
    xcol = OFF_X // LANE

    def body(u_ref, da_ref, w_ref, b_ref, du_ref, dw_ref, db_ref):
        u = u_ref[...]
        row = lax.broadcasted_iota(jnp.int32, u.shape, 0)
        shifted = [_shift_down(u, CONV_WIDTH - 1 - k, row) for k in range(CONV_WIDTH)]
        pre = b_ref[...] + jnp.zeros_like(u)
        for k in range(CONV_WIDTH):
            pre = pre + w_ref[k:k + 1, :] * shifted[k]
        sg = _sigmoid(pre)
        dpre = da_ref[...] * (sg * (1.0 + pre * (1.0 - sg)))
        db_ref[...] = jnp.sum(dpre, axis=0, keepdims=True)
        du = jnp.zeros_like(u)
        for k in range(CONV_WIDTH):
            dw_ref[k:k + 1, :] = jnp.sum(dpre * shifted[k], axis=0, keepdims=True)
            du = du + w_ref[k:k + 1, :] * _shift_up(dpre, CONV_WIDTH - 1 - k, row)
        du_ref[...] = du

    return pl.pallas_call(
        body, name="conv_bwd", grid=(D_CONV // LANE,),
        in_specs=[pl.BlockSpec((SEQ, LANE), lambda j: (0, xcol + j)), pl.BlockSpec((SEQ, LANE), lambda j: (0, j)),
                  pl.BlockSpec((CONV_WIDTH, LANE), lambda j: (0, j)), pl.BlockSpec((1, LANE), lambda j: (0, j))],
        out_specs=[pl.BlockSpec((SEQ, LANE), lambda j: (0, j)), pl.BlockSpec((CONV_WIDTH, LANE), lambda j: (0, j)),
                   pl.BlockSpec((1, LANE), lambda j: (0, j))],
        out_shape=[jax.ShapeDtypeStruct((SEQ, D_CONV), f32), jax.ShapeDtypeStruct((CONV_WIDTH, D_CONV), f32),
                   jax.ShapeDtypeStruct((1, D_CONV), f32)],
        compiler_params=_params("arbitrary"),
    )(proj, d_act, conv_w, conv_b)


def _ssd_chunk_common(dt_raw, dtb, alog):
    row = lax.broadcasted_iota(jnp.int32, (CHUNK, CHUNK), 0)
    col = lax.broadcasted_iota(jnp.int32, (CHUNK, CHUNK), 1)
    tri = (row >= col).astype(f32)
    strict = (row > col).astype(f32)
    dtp = _softplus(dt_raw + dtb)
    a_row = -jnp.exp(alog)
    d_a = dtp * a_row
    cs = _hdot(tri, d_a)
    cs_last = cs[CHUNK - 1:CHUNK, :]
    return row, col, dtp, a_row, cs, cs.T, cs_last


def _seg_decay(cs, cs_t, hd, row, col):
    seg = cs[:, hd:hd + 1] - cs_t[hd:hd + 1, :]
    return jnp.where(row >= col, jnp.exp(seg), 0.0)


def ssd_fwd(act, proj, dt_bias, a_log, d_skip, norm_g):
    zcol, dtcol = OFF_Z // D_SSM, OFF_DT // LANE
    gw = D_SSM // SSM_GROUPS

    def body(act_ref, z_ref, dt_ref, dtb_ref, alog_ref, dsk_ref, ng_ref, out_ref, ypre_ref, st_ref, state, ybuf):
        c = pl.program_id(0)

        @pl.when(c == 0)
        def _():
            state[...] = jnp.zeros_like(state)

        row, col, dtp, a_row, cs, cs_t, cs_last = _ssd_chunk_common(dt_ref[...], dtb_ref[...], alog_ref[...])
        e_cs = jnp.exp(cs)
        dte = jnp.exp(cs_last - cs)
        ecl = jnp.exp(cs_last)
        dsk = dsk_ref[...]
        for g in range(SSM_GROUPS):
            bg = act_ref[:, D_SSM + g * SSM_STATE:D_SSM + (g + 1) * SSM_STATE]
            cg = act_ref[:, D_SSM + D_BC + g * SSM_STATE:D_SSM + D_BC + (g + 1) * SSM_STATE]
            cb = _bdot_nt(cg, bg)
            for r in range(HEADS_PER_GROUP):
                hd = g * HEADS_PER_GROUP + r
                hs = slice(hd * SSM_HEAD_DIM, (hd + 1) * SSM_HEAD_DIM)
                hl = slice(hd, hd + 1)
                x_h = act_ref[:, hs]
                xdt = x_h * dtp[:, hl]
                lm = _seg_decay(cs, cs_t, hd, row, col)
                prev = state[hd]
                st_ref[0, hd] = prev
                y = _bdot(cb * lm, xdt) + e_cs[:, hl] * _bdot(cg, prev) + x_h * dsk[:, hl]
                ybuf[:, hs] = y
                state[hd] = prev * ecl[:, hl] + _bdot_tn(bg, xdt * dte[:, hl])
        y = ybuf[...]
        ypre_ref[...] = y
        z = z_ref[...]
        yz = y * (z * _sigmoid(z))
        ng = ng_ref[...]
        for g in range(SSM_GROUPS):
            gs = slice(g * gw, (g + 1) * gw)
            part = yz[:, gs]
            out_ref[:, gs] = part * _rms(part) * ng[:, gs]

    return pl.pallas_call(
        body, name="ssd_fwd", grid=(N_CHUNKS,),
        in_specs=[pl.BlockSpec((CHUNK, D_CONV), lambda c: (c, 0)), pl.BlockSpec((CHUNK, D_SSM), lambda c: (c, zcol)),
                  pl.BlockSpec((CHUNK, LANE), lambda c: (c, dtcol)), _full((1, LANE)), _full((1, LANE)), _full((1, LANE)),
                  _full((1, D_SSM))],
        out_specs=[pl.BlockSpec((CHUNK, D_SSM), lambda c: (c, 0)), pl.BlockSpec((CHUNK, D_SSM), lambda c: (c, 0)),
                   pl.BlockSpec((1, SSM_HEADS, SSM_STATE, SSM_HEAD_DIM), lambda c: (c, 0, 0, 0))],
        out_shape=[jax.ShapeDtypeStruct((SEQ, D_SSM), f32), jax.ShapeDtypeStruct((SEQ, D_SSM), f32),
                   jax.ShapeDtypeStruct((N_CHUNKS, SSM_HEADS, SSM_STATE, SSM_HEAD_DIM), f32)],
        scratch_shapes=[pltpu.VMEM((SSM_HEADS, SSM_STATE, SSM_HEAD_DIM), f32), pltpu.VMEM((CHUNK, D_SSM), f32)],
        compiler_params=_params("arbitrary"),
    )(act, proj, proj, dt_bias, a_log, d_skip, norm_g)


def ssd_bwd(act, proj, ypre, states, d_out, dt_bias, a_log, d_skip, norm_g):
    zcol, dtcol = OFF_Z // D_SSM, OFF_DT // LANE
    gw = D_SSM // SSM_GROUPS

    def body(act_ref, z_ref, dt_ref, ypre_ref, st_ref, do_ref, dtb_ref, alog_ref, dsk_ref, ng_ref,
             dact_ref, ddt_ref, dz_ref, dng_ref, dpar_ref, dstate, dybuf):
        i = pl.program_id(0)

        @pl.when(i == 0)
        def _():
            for ref in (dng_ref, dpar_ref, dstate):
                ref[...] = jnp.zeros_like(ref)

        y = ypre_ref[...]
        z = z_ref[...]
        sgz = _sigmoid(z)
        sz = z * sgz
        yz = y * sz
        ng = ng_ref[...]
        d_o = do_ref[...]
        for g in range(SSM_GROUPS):
            gs = slice(g * gw, (g + 1) * gw)
            part = yz[:, gs]
            r = _rms(part)
            yhat = part * r
            dng_ref[:, gs] += jnp.sum(d_o[:, gs] * yhat, axis=0, keepdims=True)
            dyz = _rms_bwd(d_o[:, gs], yhat, r, ng[:, gs])
            dybuf[:, gs] = dyz * sz[:, gs]
            dz_ref[:, gs] = dyz * y[:, gs] * (sgz[:, gs] * (1.0 + z[:, gs] * (1.0 - sgz[:, gs])))

        row, col, dtp, a_row, cs, cs_t, cs_last = _ssd_chunk_common(dt_ref[...], dtb_ref[...], alog_ref[...])
        upper = (row <= col).astype(f32)
        lane = lax.broadcasted_iota(jnp.int32, (CHUNK, LANE), 1)
        lane1 = lax.broadcasted_iota(jnp.int32, (1, LANE), 1)
        e_cs = jnp.exp(cs)
        dte = jnp.exp(cs_last - cs)
        ecl = jnp.exp(cs_last)
        dsk = dsk_ref[...]
        ddt_mat = jnp.zeros((CHUNK, LANE), f32)
        dcs_mat = jnp.zeros((CHUNK, LANE), f32)
        dcs_t = jnp.zeros((LANE, CHUNK), f32)
        dcsl_row = jnp.zeros((1, LANE), f32)
        dd_row = jnp.zeros((1, LANE), f32)
        for g in range(SSM_GROUPS):
            bsl = slice(D_SSM + g * SSM_STATE, D_SSM + (g + 1) * SSM_STATE)
            csl = slice(D_SSM + D_BC + g * SSM_STATE, D_SSM + D_BC + (g + 1) * SSM_STATE)
            bg = act_ref[:, bsl]
            cg = act_ref[:, csl]
            cb = _bdot_nt(cg, bg)
            dcb = jnp.zeros((CHUNK, CHUNK), f32)
            dbg = jnp.zeros((CHUNK, SSM_STATE), f32)
            dcg = jnp.zeros((CHUNK, SSM_STATE), f32)
            for rr in range(HEADS_PER_GROUP):
                hd = g * HEADS_PER_GROUP + rr
                hs = slice(hd * SSM_HEAD_DIM, (hd + 1) * SSM_HEAD_DIM)
                hl = slice(hd, hd + 1)
                x_h = act_ref[:, hs]
                dt_h = dtp[:, hl]
                e_h = e_cs[:, hl]
                dte_h = dte[:, hl]
                ecl_h = ecl[:, hl]
                xdt = x_h * dt_h
                lm = _seg_decay(cs, cs_t, hd, row, col)
                m = cb * lm
                prev = st_ref[0, hd]
                dy = dybuf[:, hs]
                dh = dstate[hd]
                dd_row = dd_row + jnp.where(lane1 == hd, jnp.sum(jnp.sum(dy * x_h, axis=1, keepdims=True), axis=0, keepdims=True), 0.0)
                dx = dy * dsk[:, hl]
                gmat = _bdot(cg, prev)
                dg = dy * e_h
                dcg = dcg + _bdot_nt(dg, prev)
                dprev = _bdot_tn(cg, dg)
                dcs_h = jnp.sum(dy * gmat, axis=1, keepdims=True) * e_h
                dm = _bdot_nt(dy, xdt)
                dxdt = _bdot_tn(m, dy)
                dcb = dcb + dm * lm
                dseg = dm * m
                dcs_h = dcs_h + jnp.sum(dseg, axis=1, keepdims=True)
                dcs_t = jnp.where(row == hd, jnp.sum(dseg, axis=0, keepdims=True), dcs_t)
                wmat = xdt * dte_h
                dbg = dbg + _bdot_nt(wmat, dh)
                dw = _bdot(bg, dh)
                dxdt = dxdt + dw * dte_h
                ddte = jnp.sum(dw * xdt, axis=1, keepdims=True) * dte_h
                dcs_h = dcs_h - ddte
                dcsl = jnp.sum(ddte, axis=0, keepdims=True)
                dcsl = dcsl + jnp.sum(jnp.sum(dh * prev, axis=1, keepdims=True), axis=0, keepdims=True) * ecl_h
                dstate[hd] = dprev + dh * ecl_h
                dact_ref[:, hs] = dx + dxdt * dt_h
                ddt_h = jnp.sum(dxdt * x_h, axis=1, keepdims=True)
                ddt_mat = jnp.where(lane == hd, ddt_h, ddt_mat)
                dcs_mat = jnp.where(lane == hd, dcs_h, dcs_mat)
                dcsl_row = jnp.where(lane1 == hd, dcsl, dcsl_row)
            dact_ref[:, bsl] = dbg + _bdot_tn(dcb, cg)
            dact_ref[:, csl] = dcg + _bdot(dcb, bg)
        rowl = lax.broadcasted_iota(jnp.int32, (CHUNK, LANE), 0)
        dcs_mat = dcs_mat - dcs_t.T + jnp.where(rowl == CHUNK - 1, dcsl_row, 0.0)
        dda = _hdot(upper, dcs_mat)
        ddt_mat = ddt_mat + dda * a_row
        da_row = jnp.sum(dda * dtp, axis=0, keepdims=True)
        ddt_raw = ddt_mat * _sigmoid(dt_ref[...] + dtb_ref[...])
        ddt_ref[...] = ddt_raw
        dpar_ref[0:1, :] += jnp.sum(ddt_raw, axis=0, keepdims=True)
        dpar_ref[1:2, :] += da_row * a_row
        dpar_ref[2:3, :] += dd_row

    blk = lambda i: N_CHUNKS - 1 - i
    return pl.pallas_call(
        body, name="ssd_bwd", grid=(N_CHUNKS,),
        in_specs=[pl.BlockSpec((CHUNK, D_CONV), lambda i: (blk(i), 0)), pl.BlockSpec((CHUNK, D_SSM), lambda i: (blk(i), zcol)),
                  pl.BlockSpec((CHUNK, LANE), lambda i: (blk(i), dtcol)), pl.BlockSpec((CHUNK, D_SSM), lambda i: (blk(i), 0)),
                  pl.BlockSpec((1, SSM_HEADS, SSM_STATE, SSM_HEAD_DIM), lambda i: (blk(i), 0, 0, 0)),
                  pl.BlockSpec((CHUNK, D_SSM), lambda i: (blk(i), 0)),
                  _full((1, LANE)), _full((1, LANE)), _full((1, LANE)), _full((1, D_SSM))],
        out_specs=[pl.BlockSpec((CHUNK, D_CONV), lambda i: (blk(i), 0)), pl.BlockSpec((CHUNK, LANE), lambda i: (blk(i), 0)),
                   pl.BlockSpec((CHUNK, D_SSM), lambda i: (blk(i), 0)), _full((1, D_SSM)), _full((8, LANE))],
        out_shape=[jax.ShapeDtypeStruct((SEQ, D_CONV), f32), jax.ShapeDtypeStruct((SEQ, LANE), f32),
                   jax.ShapeDtypeStruct((SEQ, D_SSM), f32), jax.ShapeDtypeStruct((1, D_SSM), f32),
                   jax.ShapeDtypeStruct((8, LANE), f32)],
        scratch_shapes=[pltpu.VMEM((SSM_HEADS, SSM_STATE, SSM_HEAD_DIM), f32), pltpu.VMEM((CHUNK, D_SSM), f32)],
        compiler_params=_params("arbitrary"),
    )(act, proj, proj, ypre, states, d_out, dt_bias, a_log, d_skip, norm_g)


def out_fwd(x, attn, ssm, w_out, tm=512):
    def body(x_ref, a_ref, s_ref, w_ref, o_ref):
        o_ref[...] = x_ref[...] + _bdot(a_ref[...], w_ref[:D_ATTN, :]) + _bdot(s_ref[...], w_ref[D_ATTN:, :])

    tok = lambda w_: pl.BlockSpec((tm, w_), lambda i: (i, 0))
    return pl.pallas_call(
        body, name="out_fwd", grid=(SEQ // tm,),
        in_specs=[tok(D_MODEL), tok(D_ATTN), tok(D_SSM), _full((D_MODEL, D_MODEL))],
        out_specs=tok(D_MODEL), out_shape=jax.ShapeDtypeStruct((SEQ, D_MODEL), f32),
        compiler_params=_params("arbitrary"),
    )(x, attn, ssm, w_out)


def out_bwd(dx1, attn, ssm, w_out, tm=512):
    nt = SEQ // tm

    def body(d_ref, a_ref, s_ref, w_ref, da_ref, ds_ref, dw_ref, dw16_ref):
        i = pl.program_id(0)

        @pl.when(i == 0)
        def _():
            dw_ref[...] = jnp.zeros_like(dw_ref)

        d = d_ref[...].astype(bf16)
        dcat = _bdot_nt(d, w_ref[...])
        da_ref[...] = dcat[:, :D_ATTN]
        ds_ref[...] = dcat[:, D_ATTN:]
        dw_ref[:D_ATTN, :] += _bdot_tn(a_ref[...], d)
        dw_ref[D_ATTN:, :] += _bdot_tn(s_ref[...], d)

        @pl.when(i == nt - 1)
        def _():
            dw16_ref[...] = dw_ref[...].astype(bf16)

    tok = lambda w_: pl.BlockSpec((tm, w_), lambda i: (i, 0))
    return pl.pallas_call(
        body, name="out_bwd", grid=(nt,),
        in_specs=[tok(D_MODEL), tok(D_ATTN), tok(D_SSM), _full((D_MODEL, D_MODEL))],
        out_specs=[tok(D_ATTN), tok(D_SSM), _full((D_MODEL, D_MODEL)), _full((D_MODEL, D_MODEL))],
        out_shape=[jax.ShapeDtypeStruct((SEQ, D_ATTN), f32), jax.ShapeDtypeStruct((SEQ, D_SSM), f32),
                   jax.ShapeDtypeStruct((D_MODEL, D_MODEL), f32), jax.ShapeDtypeStruct((D_MODEL, D_MODEL), bf16)],
        compiler_params=_params("arbitrary"),
    )(dx1, attn, ssm, w_out)


def mlp_fwd(x1, g, w_up, w_down, tm=512):
    def body(x_ref, g_ref, wu_ref, wd_ref, o_ref, u_ref, h_scr):
        j = pl.program_id(1)

        @pl.when(j == 0)
        def _():
            xv = x_ref[...]
            h_scr[...] = (xv * _rms(xv) * g_ref[...]).astype(bf16)
            o_ref[...] = xv

        u = jnp.dot(h_scr[...], wu_ref[...], preferred_element_type=f32)
        u_ref[...] = u
        a = jnp.square(jnp.maximum(u, 0.0))
        o_ref[...] += _bdot(a, wd_ref[...])

    return pl.pallas_call(
        body, name="mlp_fwd", grid=(SEQ // tm, N_CHIPS),
        in_specs=[pl.BlockSpec((tm, D_MODEL), lambda i, j: (i, 0)), _full((1, D_MODEL)),
                  pl.BlockSpec((None, D_MODEL, FF_TILE), lambda i, j: (j, 0, 0)),
                  pl.BlockSpec((None, FF_TILE, D_MODEL), lambda i, j: (j, 0, 0))],
        out_specs=[pl.BlockSpec((tm, D_MODEL), lambda i, j: (i, 0)), pl.BlockSpec((tm, FF_TILE), lambda i, j: (i, j))],
        out_shape=[jax.ShapeDtypeStruct((SEQ, D_MODEL), f32), jax.ShapeDtypeStruct((SEQ, D_FF), f32)],
        scratch_shapes=[pltpu.VMEM((tm, D_MODEL), bf16)],
        compiler_params=_params("arbitrary", "arbitrary"),
    )(x1, g, w_up, w_down)


def mlp_bwd_data(dx2, u, x1, g, w_up, w_down, tm=512):
    def body(d_ref, u_ref, x_ref, g_ref, wu_ref, wd_ref, dx_ref, du_ref, dg_ref, dh_scr):
        i, j = pl.program_id(0), pl.program_id(1)

        @pl.when(jnp.logical_and(i == 0, j == 0))
        def _():
            dg_ref[...] = jnp.zeros_like(dg_ref)

        @pl.when(j == 0)
        def _():
            dh_scr[...] = jnp.zeros_like(dh_scr)

        da = _bdot_nt(d_ref[...], wd_ref[...])
        du = (da * (2.0 * jnp.maximum(u_ref[...], 0.0))).astype(bf16)
        du_ref[...] = du
        dh_scr[...] += _bdot_nt(du, wu_ref[...])

        @pl.when(j == N_CHIPS - 1)
        def _():
            xv = x_ref[...]
            r = _rms(xv)
            xhat = xv * r
            dh = dh_scr[...]
            dg_ref[...] += jnp.sum(dh * xhat, axis=0, keepdims=True)
            dx_ref[...] = d_ref[...] + _rms_bwd(dh, xhat, r, g_ref[...])

    return pl.pallas_call(
        body, name="mlp_bwd_data", grid=(SEQ // tm, N_CHIPS),
        in_specs=[pl.BlockSpec((tm, D_MODEL), lambda i, j: (i, 0)), pl.BlockSpec((tm, FF_TILE), lambda i, j: (i, j)),
                  pl.BlockSpec((tm, D_MODEL), lambda i, j: (i, 0)), _full((1, D_MODEL)),
                  pl.BlockSpec((None, D_MODEL, FF_TILE), lambda i, j: (j, 0, 0)),
                  pl.BlockSpec((None, FF_TILE, D_MODEL), lambda i, j: (j, 0, 0))],
        out_specs=[pl.BlockSpec((tm, D_MODEL), lambda i, j: (i, 0)), pl.BlockSpec((tm, FF_TILE), lambda i, j: (i, j)),
                   _full((1, D_MODEL))],
        out_shape=[jax.ShapeDtypeStruct((SEQ, D_MODEL), f32), jax.ShapeDtypeStruct((SEQ, D_FF), bf16),
                   jax.ShapeDtypeStruct((1, D_MODEL), f32)],
        scratch_shapes=[pltpu.VMEM((tm, D_MODEL), f32)],
        compiler_params=_params("arbitrary", "arbitrary"),
    )(dx2, u, x1, g, w_up, w_down)


def mlp_bwd_weights(dx2, u, du, x1, g, tm=512):
    nt = SEQ // tm

    def body(d_ref, u_ref, du_ref, x_ref, g_ref, dwu_ref, dwd_ref, dwu16_ref, dwd16_ref):
        i = pl.program_id(1)

        @pl.when(i == 0)
        def _():
            dwu_ref[...] = jnp.zeros_like(dwu_ref)
            dwd_ref[...] = jnp.zeros_like(dwd_ref)

        xv = x_ref[...]
        h = xv * _rms(xv) * g_ref[...]
        dwu_ref[...] += _bdot_tn(h, du_ref[...])
        a = jnp.square(jnp.maximum(u_ref[...], 0.0))
        dwd_ref[...] += _bdot_tn(a, d_ref[...])

        @pl.when(i == nt - 1)
        def _():
            dwu16_ref[...] = dwu_ref[...].astype(bf16)
            dwd16_ref[...] = dwd_ref[...].astype(bf16)

    up = pl.BlockSpec((None, D_MODEL, FF_TILE), lambda j, i: (j, 0, 0))
    down = pl.BlockSpec((None, FF_TILE, D_MODEL), lambda j, i: (j, 0, 0))
    return pl.pallas_call(
        body, name="mlp_bwd_weights", grid=(N_CHIPS, nt),
        in_specs=[pl.BlockSpec((tm, D_MODEL), lambda j, i: (i, 0)), pl.BlockSpec((tm, FF_TILE), lambda j, i: (i, j)),
                  pl.BlockSpec((tm, FF_TILE), lambda j, i: (i, j)), pl.BlockSpec((tm, D_MODEL), lambda j, i: (i, 0)),
                  _full((1, D_MODEL))],
        out_specs=[up, down, up, down],
        out_shape=[jax.ShapeDtypeStruct((N_CHIPS, D_MODEL, FF_TILE), f32), jax.ShapeDtypeStruct((N_CHIPS, FF_TILE, D_MODEL), f32),
                   jax.ShapeDtypeStruct((N_CHIPS, D_MODEL, FF_TILE), bf16), jax.ShapeDtypeStruct((N_CHIPS, FF_TILE, D_MODEL), bf16)],
        compiler_params=_params("arbitrary", "arbitrary"),
    )(dx2, u, du, x1, g)


def loss_head(y, target, tm=512):
    def body(y_ref, t_ref, dy_ref, l_ref):
        @pl.when(pl.program_id(0) == 0)
        def _():
            l_ref[...] = jnp.zeros_like(l_ref)

        d = y_ref[...] - t_ref[...]
        dy_ref[...] = d * (1.0 / D_MODEL)
        part = jnp.sum(jnp.mean(d * d, axis=-1, keepdims=True), axis=0, keepdims=True)
        l_ref[...] += 0.5 * part

    tok = pl.BlockSpec((tm, D_MODEL), lambda i: (i, 0))
    return pl.pallas_call(
        body, name="loss_head", grid=(SEQ // tm,), in_specs=[tok, tok], out_specs=[tok, _full((1, 1))],
        out_shape=[jax.ShapeDtypeStruct((SEQ, D_MODEL), f32), jax.ShapeDtypeStruct((1, 1), f32)],
        compiler_params=_params("arbitrary"),
    )(y, target)


def _pad_lane(v):
    return jnp.pad(v, (0, LANE - v.shape[0]))[None, :]


def local_step(x, target, w, prov):
    bucket = jnp.asarray(_bucket_table().T)
    bias = bias_build(w["rel_bias"], bucket)
    saved = []
    for l in range(DEPTH):
        g_mix = w["mix_norm_g"][l][None, :] + prov.stage(("begin", l), x)
        w_in = prov.w_in(l, x)
        proj = in_fwd(x, g_mix, w_in)
        qg, kg = w["q_gain"][l][:, None], w["k_gain"][l][None, :]
        attn = attn_fwd_t(proj, qg, kg, w["sinks"][l], bias)
        conv_b = w["conv_b"][l][None, :]
        act = conv_fwd(proj, w["conv_w"][l], conv_b)
        dtb = _pad_lane(w["dt_bias"][l]) + prov.stage(("mid", l), act)
        alog, dsk = _pad_lane(w["a_log"][l]), _pad_lane(w["d_skip"][l])
        ng = w["ssm_norm_g"][l][None, :]
        ssm, ypre, states = ssd_fwd(act, proj, dtb, alog, dsk, ng)
        tok = prov.stage(("pre_out", l), ssm)
        w_out = prov.w_out(l, ssm) + jnp.asarray(tok, bf16)
        x1 = out_fwd(x, attn, ssm, w_out)
        g_mlp = w["mlp_norm_g"][l][None, :] + prov.stage(("pre_mlp", l), x1)
        w_up, w_down = prov.mlp(l, x1)
        x2, u = mlp_fwd(x1, g_mlp, w_up, w_down)
        saved.append(dict(x=x, proj=proj, attn=attn, act=act, ssm=ssm, ypre=ypre, states=states, x1=x1, u=u,
                          g_mix=g_mix, qg=qg, kg=kg, conv_b=conv_b, dtb=dtb, alog=alog, dsk=dsk, ng=ng, g_mlp=g_mlp,
                          w_in=w_in, w_out=w_out, w_up=w_up, w_down=w_down))
        x = x2
    dx, loss = loss_head(x, target)
    grads = [None] * DEPTH
    dbands = [None] * DEPTH
    tok = 0.0
    for l in reversed(range(DEPTH)):
        s = saved[l]
        g_mlp = s["g_mlp"] + tok
        dx1, du, dg_mlp = mlp_bwd_data(dx, s["u"], s["x1"], g_mlp, s["w_up"], s["w_down"])
        dw_up, dw_down, dw_up16, dw_down16 = mlp_bwd_weights(dx, s["u"], du, s["x1"], g_mlp)
        tok = prov.grads(("mlp", l), dict(w_up=(dw_up, dw_up16), w_down=(dw_down, dw_down16)), dx1)
        dattn, dssm, dw_out, dw_out16 = out_bwd(dx1, s["attn"], s["ssm"], s["w_out"])
        dact, ddt, dz, dng, dpar = ssd_bwd(s["act"], s["proj"], s["ypre"], s["states"], dssm, s["dtb"] + tok, s["alog"],
                                           s["dsk"], s["ng"])
        conv_b = s["conv_b"] + prov.stage(("bwd_mid", l), dact)
        dxbc, dconv_w, dconv_b = conv_bwd(s["proj"], dact, w["conv_w"][l], conv_b)
        dq, dk, dv, dband, dsink, dqg, dkg = attn_bwd_t(s["proj"], dattn, s["qg"], s["kg"], w["sinks"][l], bias)
        dx, dw_in, dg_mix = in_bwd(dq, dz, dxbc, dk, dv, ddt, s["x"], s["g_mix"], s["w_in"], dx1)
        tok = prov.grads(("mix", l), dict(w_in=split_w_in_grad(dw_in), w_out=(dw_out, dw_out16)), dx)
        dbands[l] = dband
        grads[l] = dict(mix_norm_g=dg_mix[0], q_gain=dqg[:, 0], k_gain=dkg[0], sinks=dsink[:, 0],
                        conv_w=dconv_w, conv_b=dconv_b[0], dt_bias=dpar[0, :SSM_HEADS], a_log=dpar[1, :SSM_HEADS],
                        d_skip=dpar[2, :SSM_HEADS], ssm_norm_g=dng[0], mlp_norm_g=dg_mlp[0])
    out = {k: jnp.stack([grads[l][k] for l in range(DEPTH)]) for k in grads[0]}
    out["rel_bias"] = bias_bwd(dbands[0], dbands[1], bucket)[:, :N_Q_HEADS]
    return loss, dx, out, tok


MESH = pl.DeviceIdType.MESH
HBM = pl.BlockSpec(memory_space=pltpu.HBM)
N_PEER_CHIPS = N_CHIPS - 1
N_DEVICES = 8


def _coords():
    return lax.axis_index("x"), lax.axis_index("y"), lax.axis_index("c")


def _peer_chips(x, y):
    return [(1 - x, y), (x, 1 - y), (1 - x, 1 - y)]


def _remote(src, dst, send_sem, recv_sem, device):
    return pltpu.make_async_remote_copy(src_ref=src, dst_ref=dst, send_sem=send_sem, recv_sem=recv_sem,
                                        device_id=device, device_id_type=MESH)


SEM = pl.BlockSpec(memory_space=pltpu.SEMAPHORE)
ANY = pl.BlockSpec(memory_space=pl.ANY)
DATAFLOW = pltpu.SideEffectType.DATAFLOW_SIDE_EFFECTING


def _gather_copies(kind, src_refs, land_refs, ssem, rsem):
    x, y, c = _coords()
    k_me = 2 * x + y
    cps = []
    for p, land in enumerate(land_refs):
        hr = land.shape[1] // 2
        rows = pl.ds(c * hr, hr)
        for j, chip in enumerate(_peer_chips(x, y)):
            i = 3 * p + j
            if kind == "ici":
                cps.append(_remote(src_refs[p].at[rows, :], land.at[k_me, rows, :], ssem.at[i], rsem.at[i], (*chip, c)))
            else:
                got = land.at[2 * chip[0] + chip[1], rows, :]
                cps.append(_remote(got, got, ssem.at[i], rsem.at[i], (x, y, 1 - c)))
    return cps


def gather_now(srcs, conv):
    n = len(srcs)

    def body(*refs):
        src_refs, conv_ref = refs[:n], refs[n]
        lands, gconv = refs[n + 1:2 * n + 1], refs[2 * n + 1]
        ssem, rsem, fsem, frsem, csem, crsem = refs[2 * n + 2:]
        x, y, c = _coords()
        k_me = 2 * x + y
        chips = _peer_chips(x, y)
        ici = _gather_copies("ici", src_refs, lands, ssem, rsem)
        relay = _gather_copies("relay", src_refs, lands, fsem, frsem)
        conv_cps = [_remote(conv_ref, gconv.at[k_me], csem.at[j], crsem.at[j], (*chip, c)) for j, chip in enumerate(chips)]
        for cp in ici + conv_cps:
            cp.start()
        for cp, fw in zip(ici, relay):
            cp.wait_recv()
            fw.start()
        for cp in conv_cps + relay:
            cp.wait_recv()
        for cp in ici + relay + conv_cps:
            cp.wait_send()

    out_shape = [jax.ShapeDtypeStruct((N_CHIPS,) + s.shape, s.dtype) for s in srcs]
    out_shape.append(jax.ShapeDtypeStruct((N_CHIPS,) + conv.shape, conv.dtype))
    sems = lambda k: pltpu.SemaphoreType.DMA((k,))
    return pl.pallas_call(
        body, name="gather_now", out_shape=out_shape, in_specs=[HBM] * (n + 1), out_specs=[HBM] * (n + 1),
        scratch_shapes=[sems(3 * n), sems(3 * n), sems(3 * n), sems(3 * n), sems(N_PEER_CHIPS), sems(N_PEER_CHIPS)],
    )(*srcs, conv)


def _gather_maker(kind, n_src):
    def make(refs, ssem, rsem):
        cps = _gather_copies(kind, refs[:n_src], refs[n_src:], ssem, rsem)
        return cps, cps
    return make


def _scatter_maker(n):
    def make(refs, ssem, rsem):
        x, y, c = _coords()
        k_me = 2 * x + y
        sends, arrivals = [], []
        for p in range(n):
            src, land = refs[p], refs[n + p]
            sends.append(_remote(src.at[k_me, 1 - c], land.at[0], ssem.at[7 * p], rsem.at[7 * p], (x, y, 1 - c)))
            for j, chip in enumerate(_peer_chips(x, y)):
                for cc in range(2):
                    sends.append(_remote(src.at[2 * chip[0] + chip[1], cc], land.at[1 + 2 * j + c],
                                         ssem.at[7 * p + 1 + 2 * j + cc], rsem.at[7 * p + 1 + 2 * j + c], (*chip, cc)))
            for s in range(7):
                arrivals.append(_remote(land.at[s], land.at[s], ssem.at[7 * p + s], rsem.at[7 * p + s], (x, y, 1 - c)))
        return sends, arrivals
    return make


def _share_maker(n):
    def make(refs, ssem, rsem):
        x, y, c = _coords()
        sends = [_remote(refs[p].at[c], refs[p].at[c], ssem.at[p], rsem.at[p], (x, y, 1 - c)) for p in range(n)]
        arrivals = [_remote(refs[p].at[1 - c], refs[p].at[1 - c], ssem.at[p], rsem.at[p], (x, y, 1 - c)) for p in range(n)]
        return sends, arrivals
    return make


def split_start(name, make, n_sems, operands, after):
    n = len(operands)

    def body(*refs):
        ssem, rsem, token = refs[n + 1], refs[n + 2], refs[-1]
        for cp in make(refs[:n], ssem, rsem)[0]:
            cp.start()
        token[...] = jnp.zeros_like(token)

    ops = [pltpu.with_memory_space_constraint(a, pltpu.HBM) for a in operands]
    outs = pl.pallas_call(
        body, name=name,
        out_shape=(pltpu.SemaphoreType.DMA((n_sems,)), pltpu.SemaphoreType.DMA((n_sems,)),
                   *[pltpu.HBM(a.shape, a.dtype) for a in ops], jax.ShapeDtypeStruct((8, LANE), f32)),
        in_specs=[HBM] * n + [ANY], out_specs=(SEM, SEM, *[HBM] * n, pl.BlockSpec(memory_space=pltpu.VMEM)),
        input_output_aliases={i: 2 + i for i in range(n)},
        compiler_params=pltpu.CompilerParams(has_side_effects=DATAFLOW),
    )(*ops, after)
    return dict(name=name, make=make, ssem=outs[0], rsem=outs[1], operands=outs[2:2 + n], token=outs[-1][0, 0])


def split_wait(handle, after):
    n = len(handle["operands"])

    def body(*refs):
        sends, arrivals = handle["make"](refs[:n], refs[n], refs[n + 1])
        for cp in sends:
            cp.wait_send()
        for cp in arrivals:
            cp.wait_recv()

    outs = pl.pallas_call(
        body, name=handle["name"].replace("start", "wait"),
        out_shape=tuple(pltpu.HBM(a.shape, a.dtype) for a in handle["operands"]),
        in_specs=[HBM] * n + [SEM, SEM, ANY], out_specs=tuple([HBM] * n),
        input_output_aliases={i: i for i in range(n)},
        compiler_params=pltpu.CompilerParams(has_side_effects=DATAFLOW),
    )(*handle["operands"], handle["ssem"], handle["rsem"], after)
    return list(outs)


def piece_sum(g, recv, kc_arr):
    _, _, rb, cc = g.shape
    tr = min(256, rb)

    def body(kc_ref, g_ref, r_ref, o_ref):
        acc = g_ref[...]
        for s in range(7):
            acc = acc + r_ref[s].astype(f32)
        o_ref[...] = acc

    return pl.pallas_call(
        body, name="piece_sum",
        grid_spec=pltpu.PrefetchScalarGridSpec(
            num_scalar_prefetch=1, grid=(rb // tr,),
            in_specs=[pl.BlockSpec((None, None, tr, cc), lambda r, kc: (kc[0], kc[1], r, 0)),
                      pl.BlockSpec((7, tr, cc), lambda r, kc: (0, r, 0))],
            out_specs=pl.BlockSpec((None, tr, cc), lambda r, kc: (kc[1], r, 0))),
        out_shape=jax.ShapeDtypeStruct((2, rb, cc), f32),
        compiler_params=_params("arbitrary"),
    )(kc_arr, g, recv)


def small_all_reduce(vec):
    def body(v_ref, o_ref, gat, ssem, rsem):
        x, y, c = _coords()
        me = 4 * x + 2 * y + c
        gat[me] = v_ref[...]
        sends = []
        for t in range(1, N_DEVICES):
            peer = (x ^ (t >> 2), y ^ ((t >> 1) & 1), c ^ (t & 1))
            cp = _remote(v_ref, gat.at[me], ssem.at[t - 1], rsem.at[t - 1], peer)
            cp.start()
            sends.append(cp)
        for t in range(1, N_DEVICES):
            peer = (x ^ (t >> 2), y ^ ((t >> 1) & 1), c ^ (t & 1))
            slot = gat.at[4 * peer[0] + 2 * peer[1] + peer[2]]
            _remote(slot, slot, ssem.at[t - 1], rsem.at[t - 1], peer).wait_recv()
        for cp in sends:
            cp.wait_send()
        acc = gat[0]
        for d in range(1, N_DEVICES):
            acc = acc + gat[d]
        o_ref[...] = acc

    return pl.pallas_call(
        body, name="small_all_reduce", out_shape=jax.ShapeDtypeStruct(vec.shape, vec.dtype),
        in_specs=[pl.BlockSpec(memory_space=pltpu.VMEM)], out_specs=pl.BlockSpec(memory_space=pltpu.VMEM),
        scratch_shapes=[pltpu.VMEM((N_DEVICES,) + vec.shape, vec.dtype), pltpu.SemaphoreType.DMA((N_DEVICES - 1,)),
                        pltpu.SemaphoreType.DMA((N_DEVICES - 1,))],
    )(vec)


def _adamw_math(w, g, m, v):
    m_new = ADAM_B1 * m + (1.0 - ADAM_B1) * g
    v_new = ADAM_B2 * v + (1.0 - ADAM_B2) * jnp.square(g)
    m_hat = m_new / (1.0 - ADAM_B1 ** ADAM_STEP)
    v_hat = v_new / (1.0 - ADAM_B2 ** ADAM_STEP)
    delta = -ADAM_LR * (m_hat / (jnp.sqrt(v_hat) + ADAM_EPS) + ADAM_WD * w)
    return delta, m_new, v_new


def adamw_shard(w, g0, g1, m, v):
    depth, rows, cols = w.shape
    half = rows // 2
    tr = min(256, half)
    nr = half // tr

    def body(w_ref, g0_ref, g1_ref, m_ref, v_ref, go_ref, d_ref, nm_ref, nv_ref):
        gv = jnp.where(pl.program_id(0) == 0, g0_ref[...], g1_ref[...])
        go_ref[...] = gv
        d_ref[...], nm_ref[...], nv_ref[...] = _adamw_math(w_ref[...], gv, m_ref[...], v_ref[...])

    spec = pl.BlockSpec((None, tr, cols), lambda l, h, r: (l, h * nr + r, 0))
    g0spec = pl.BlockSpec((None, tr, cols), lambda l, h, r: (jnp.where(l == 0, h, 1), jnp.where(l == 0, r, nr - 1), 0))
    g1spec = pl.BlockSpec((None, tr, cols), lambda l, h, r: (jnp.where(l == 1, h, 0), jnp.where(l == 1, r, 0), 0))
    return pl.pallas_call(
        body, name="adamw_shard", grid=(depth, 2, nr), in_specs=[spec, g0spec, g1spec, spec, spec], out_specs=[spec] * 4,
        out_shape=[jax.ShapeDtypeStruct(w.shape, f32)] * 4,
        compiler_params=_params("arbitrary", "arbitrary", "arbitrary"),
    )(w, g0, g1, m, v)


def adamw_small(w, g, m, v):
    def body(w_ref, g_ref, m_ref, v_ref, d_ref, nm_ref, nv_ref):
        d_ref[...], nm_ref[...], nv_ref[...] = _adamw_math(w_ref[...], g_ref[...], m_ref[...], v_ref[...])

    return pl.pallas_call(
        body, name="adamw_small", out_shape=[jax.ShapeDtypeStruct(w.shape, f32)] * 3,
    )(w, g, m, v)


WEIGHTS = ("mix_norm_g", "w_in", "q_gain", "k_gain", "sinks", "rel_bias", "conv_w", "conv_b", "dt_bias", "a_log", "d_skip",
           "ssm_norm_g", "w_out", "mlp_norm_g", "w_up", "w_down")
BIG = ("w_in", "w_out", "w_up", "w_down")
SMALL = tuple(n for n in WEIGHTS if n not in BIG)
PACK_COLS = 1024
PACK_ROWS = 16


def _pack(named):
    flat = jnp.concatenate([named[n].reshape(-1) for n in SMALL])
    return jnp.pad(flat, (0, PACK_ROWS * PACK_COLS - flat.shape[0])).reshape(PACK_ROWS, PACK_COLS)


def _unpack(buf, shapes):
    flat = buf.reshape(-1)
    out, at = {}, 0
    for n in SMALL:
        size = int(np.prod(shapes[n]))
        out[n] = flat[at:at + size].reshape(shapes[n])
        at += size
    return out


class _Exchange:
    GROUPS = {"A": (("w_up", 0), ("w_down", 0)), "B": (("w_in", 1), ("w_out", 1)), "C": (("w_up", 1), ("w_down", 1))}
    RELAY_AT = {("pre_out", 0): "A", ("pre_mlp", 0): "B", ("mid", 1): "C"}
    NEXT_GROUP = {"A": "B", "B": "C"}
    LAST = ("mix", 0)

    def __init__(self, wts, k_me, kc_arr):
        self.wts, self.k_me, self.kc_arr = wts, k_me, kc_arr
        self.own = {(n, l): wts[n][l].astype(bf16) for n in BIG for l in range(DEPTH)}
        now = gather_now([self.own["w_in", 0], self.own["w_out", 0]], wts["conv_w"])
        self.ready = {("w_in", 0): self._fill(now[0], self.own["w_in", 0]),
                      ("w_out", 0): self._fill(now[1], self.own["w_out", 0])}
        conv = self._fill(now[2], wts["conv_w"])
        self.conv_w = jnp.transpose(conv, (1, 2, 0, 3)).reshape(DEPTH, CONV_WIDTH, D_CONV)
        self.ici, self.relay = {}, {}
        self.gview, self.scatter, self.share, self.reduced = {}, [], [], {}
        self._start_ici("A", now[2])

    def _start_ici(self, g, after):
        srcs = [self.own[p] for p in self.GROUPS[g]]
        lands = [lax.empty((N_CHIPS,) + s.shape, s.dtype) for s in srcs]
        self.ici[g] = split_start("gather%s_ici_start" % g, _gather_maker("ici", len(srcs)), 3 * len(srcs), srcs + lands,
                                  after)
        return self.ici[g]["token"]

    def _fill(self, land, own):
        return lax.dynamic_update_slice(land, own[None], (self.k_me,) + (0,) * own.ndim)

    def stage(self, name, after):
        if name == ("begin", 0):
            return self.ici["A"]["token"]
        g = self.RELAY_AT.get(name)
        if g is None:
            return 0.0
        n = len(self.GROUPS[g])
        lands = split_wait(self.ici[g], after)[n:]
        self.relay[g] = split_start("gather%s_relay_start" % g, _gather_maker("relay", 0), 3 * n, lands, after)
        tok = self.relay[g]["token"]
        if g in self.NEXT_GROUP:
            tok = tok + self._start_ici(self.NEXT_GROUP[g], after)
        return tok

    def _get(self, piece, after):
        if piece not in self.ready:
            g = [k for k, pieces in self.GROUPS.items() if piece in pieces][0]
            lands = split_wait(self.relay[g], after)
            for p, land in zip(self.GROUPS[g], lands):
                self.ready[p] = self._fill(land, self.own[p])
        return self.ready[piece]

    def w_in(self, l, after):
        return align_w_in(self._get(("w_in", l), after))

    def w_out(self, l, after):
        return self._get(("w_out", l), after).reshape(D_MODEL, D_MODEL)

    def mlp(self, l, after):
        return self._get(("w_up", l), after), self._get(("w_down", l), after)

    def _view(self, n, g):
        _, rows, cols = self.wts[n].shape
        return g.reshape(N_CHIPS, 2, rows // 2, cols)

    def grads(self, name, arrays, after):
        if name == self.LAST:
            self.held = (name, arrays)
            return self._advance(after, 0)
        return self._scatter(name, arrays, after) + self._advance(after, 1)

    def flush(self, after):
        return self._scatter(*self.held, after)

    def _scatter(self, name, arrays, after):
        pieces = [(n, name[1]) for n in arrays]
        views = [self._view(n, g) for n, (g, _) in arrays.items()]
        sends = [g16.reshape(v.shape) for v, (_, g16) in zip(views, arrays.values())]
        self.gview.update(zip(pieces, views))
        lands = [lax.empty((7,) + v.shape[2:], bf16) for v in views]
        h = split_start("scatter_%s%d_start" % name, _scatter_maker(len(views)), 7 * len(views), sends + lands, after)
        self.scatter.append((pieces, h))
        return h["token"]

    def _take_share(self, after):
        pieces, h = self.share.pop(0)
        self.reduced.update(zip(pieces, split_wait(h, after)))

    def _take_scatter(self, after):
        pieces, h = self.scatter.pop(0)
        lands = split_wait(h, after)[len(pieces):]
        sums = [piece_sum(self.gview[p], land, self.kc_arr) for p, land in zip(pieces, lands)]
        hs = split_start(h["name"].replace("scatter", "share"), _share_maker(len(sums)), len(sums), sums, after)
        self.share.append((pieces, hs))
        return hs["token"]

    def _advance(self, after, newest):
        if self.share:
            self._take_share(after)
        return self._take_scatter(after) if len(self.scatter) > newest else 0.0

    def reduced_grads(self, names, after):
        want = [(n, l) for n in names for l in range(DEPTH)]
        while not all(p in self.reduced for p in want):
            if any(p in pieces for p in want for pieces, _ in self.share):
                self._take_share(after)
            else:
                self._take_scatter(after)
        return {n: [self.reduced[n, l] for l in range(DEPTH)] for n in names}


def kernel(x, mix_norm_g, w_in, q_gain, k_gain, sinks, rel_bias, conv_w, conv_b, dt_bias, a_log, d_skip, ssm_norm_g, w_out, mlp_norm_g, w_up, w_down, loss_target, m_mix_norm_g, m_w_in, m_q_gain, m_k_gain, m_sinks, m_rel_bias, m_conv_w, m_conv_b, m_dt_bias, m_a_log, m_d_skip, m_ssm_norm_g, m_w_out, m_mlp_norm_g, m_w_up, m_w_down, v_mix_norm_g, v_w_in, v_q_gain, v_k_gain, v_sinks, v_rel_bias, v_conv_w, v_conv_b, v_dt_bias, v_a_log, v_d_skip, v_ssm_norm_g, v_w_out, v_mlp_norm_g, v_w_up, v_w_down):
    wts = dict(mix_norm_g=mix_norm_g, w_in=w_in, q_gain=q_gain, k_gain=k_gain, sinks=sinks, rel_bias=rel_bias, conv_w=conv_w,
               conv_b=conv_b, dt_bias=dt_bias, a_log=a_log, d_skip=d_skip, ssm_norm_g=ssm_norm_g, w_out=w_out,
               mlp_norm_g=mlp_norm_g, w_up=w_up, w_down=w_down)
    mom = dict(mix_norm_g=m_mix_norm_g, w_in=m_w_in, q_gain=m_q_gain, k_gain=m_k_gain, sinks=m_sinks, rel_bias=m_rel_bias,
               conv_w=m_conv_w, conv_b=m_conv_b, dt_bias=m_dt_bias, a_log=m_a_log, d_skip=m_d_skip, ssm_norm_g=m_ssm_norm_g,
               w_out=m_w_out, mlp_norm_g=m_mlp_norm_g, w_up=m_w_up, w_down=m_w_down)
    var = dict(mix_norm_g=v_mix_norm_g, w_in=v_w_in, q_gain=v_q_gain, k_gain=v_k_gain, sinks=v_sinks, rel_bias=v_rel_bias,
               conv_w=v_conv_w, conv_b=v_conv_b, dt_bias=v_dt_bias, a_log=v_a_log, d_skip=v_d_skip, ssm_norm_g=v_ssm_norm_g,
               w_out=v_w_out, mlp_norm_g=v_mlp_norm_g, w_up=v_w_up, w_down=v_w_down)
    xi, yi, ci = _coords()
    k_me = 2 * xi + yi
    kc_arr = jnp.stack([k_me, ci]).astype(jnp.int32)

    prov = _Exchange(wts, k_me, kc_arr)
    small_w = {n: wts[n] for n in SMALL}
    small_w["conv_w"] = prov.conv_w
    loss, dx, grads, tok = local_step(x[0], loss_target[0], small_w, prov)
    loss = lax.psum(loss[0, 0], ("x", "y", "c"))

    small_shapes = {n: grads[n].shape for n in SMALL}
    small_sum = small_all_reduce(_pack(grads) + tok)
    tok = prov.flush(small_sum)
    small = _unpack(small_sum, small_shapes)
    cols = conv_w.shape[-1]
    small["conv_w"] = lax.dynamic_slice_in_dim(small["conv_w"], k_me * cols, cols, axis=2)
    g_out_d, d_out_d, m_out_d, v_out_d = {}, {}, {}, {}
    shard_shapes = {n: wts[n].shape for n in SMALL}
    d, nm, nv = adamw_small(_pack(wts), _pack(small) + tok, _pack(mom), _pack(var))
    for dst, buf in ((d_out_d, d), (m_out_d, nm), (v_out_d, nv)):
        dst.update(_unpack(buf, shard_shapes))
    g_out_d.update(small)

    after = d
    for names in (("w_up", "w_down"), ("w_in", "w_out")):
        for n, (g0, g1) in prov.reduced_grads(names, after).items():
            g_out_d[n], d_out_d[n], m_out_d[n], v_out_d[n] = adamw_shard(wts[n], g0, g1, mom[n], var[n])
            after = d_out_d[n]

    return (loss, dx[None], *[g_out_d[n] for n in WEIGHTS], *[d_out_d[n] for n in WEIGHTS],
            *[m_out_d[n] for n in WEIGHTS], *[v_out_d[n] for n in WEIGHTS])
```

```python
import functools

import numpy as np
import jax
import jax.numpy as jnp
from jax import lax
from jax.experimental import pallas as pl
from jax.experimental.pallas import tpu as pltpu

f32 = jnp.float32
bf16 = jnp.bfloat16

SEQ = 2048
D_MODEL = 1024
DEPTH = 2
HEAD_DIM = 64
N_Q_HEADS = 8
N_KV_HEADS = 2
Q_PER_KV = N_Q_HEADS // N_KV_HEADS
BLOCK = 128
N_BLOCKS = SEQ // BLOCK
N_BUCKETS = 32
MAX_DISTANCE = 128
SSM_HEADS = 8
SSM_HEAD_DIM = 64
SSM_GROUPS = 2
HEADS_PER_GROUP = SSM_HEADS // SSM_GROUPS
SSM_STATE = 128
CONV_WIDTH = 4
CHUNK = 128
N_CHUNKS = SEQ // CHUNK
D_FF = 4 * D_MODEL
D_ATTN = N_Q_HEADS * HEAD_DIM
D_KV = N_KV_HEADS * HEAD_DIM
D_SSM = SSM_HEADS * SSM_HEAD_DIM
D_BC = SSM_GROUPS * SSM_STATE
D_CONV = D_SSM + 2 * D_BC
D_IN = D_ATTN + 2 * D_KV + D_SSM + D_CONV + SSM_HEADS
EPS = 1e-6
NEG = -1e30
N_CHIPS = 4
FF_TILE = D_FF // N_CHIPS

LANE = 128
PW = D_ATTN + D_SSM + D_CONV + 2 * D_KV + LANE
OFF_Q, OFF_Z, OFF_X, OFF_K, OFF_V, OFF_DT = 0, 512, 1024, 2048, 2176, 2304

ADAM_LR = 0.001
ADAM_B1 = 0.9
ADAM_B2 = 0.999
ADAM_EPS = 1e-08
ADAM_WD = 0.01
ADAM_STEP = 10

VMEM_LIMIT = 56 * 1024 * 1024


def _params(*sem):
    return pltpu.CompilerParams(dimension_semantics=tuple(sem), vmem_limit_bytes=VMEM_LIMIT)


def _bdot(a, b):
    return jnp.dot(a.astype(bf16), b.astype(bf16), preferred_element_type=f32)


def _bdot_nt(a, b):
    return lax.dot_general(a.astype(bf16), b.astype(bf16), (((1,), (1,)), ((), ())), preferred_element_type=f32)


def _bdot_tn(a, b):
    return lax.dot_general(a.astype(bf16), b.astype(bf16), (((0,), (0,)), ((), ())), preferred_element_type=f32)


def _hdot(a, b):
    return jnp.dot(a, b, precision=lax.Precision.HIGHEST, preferred_element_type=f32)


def _sigmoid(x):
    return 1.0 / (1.0 + jnp.exp(-x))


def _softplus(x):
    return jnp.maximum(x, 0.0) + jnp.log1p(jnp.exp(-jnp.abs(x)))


def _rms(x):
    return lax.rsqrt(jnp.mean(x * x, axis=-1, keepdims=True) + EPS)


def _rms_bwd(dy, xhat, r, g):
    t = dy * g
    return r * (t - xhat * jnp.mean(t * xhat, axis=-1, keepdims=True))


def _full(shape):
    return pl.BlockSpec(shape, lambda *_: (0,) * len(shape))


def _bucket_table():
    qi = np.arange(BLOCK)[:, None]
    kj = np.arange(2 * BLOCK)[None, :]
    dist = qi + BLOCK - kj
    ok = (dist >= 0) & (dist < 128)
    d = np.clip(dist, 0, None)
    max_exact = N_BUCKETS // 2
    d_f = np.maximum(d, 1).astype(np.float32)
    large = max_exact + (np.log(d_f / np.float32(max_exact)) / np.float32(np.log(MAX_DISTANCE / max_exact))
                         * np.float32(N_BUCKETS - max_exact)).astype(np.int32)
    large = np.minimum(large, N_BUCKETS - 1)
    bucket = np.where(d < max_exact, d, large)
    return np.where(ok, bucket, -1).astype(np.int32)


def bias_build(rel_bias, bucket):
    def body(rel_ref, bkt_ref, o_ref):
        bkt = bkt_ref[...]
        for h in range(N_Q_HEADS):
            acc = jnp.where(bkt < 0, NEG, 0.0).astype(f32)
            for b in range(N_BUCKETS):
                acc = acc + jnp.where(bkt == b, rel_ref[b, h], 0.0)
            o_ref[h] = acc

    return pl.pallas_call(
        body, name="bias_build", out_shape=jax.ShapeDtypeStruct((N_Q_HEADS,) + bucket.shape, f32),
        in_specs=[pl.BlockSpec(memory_space=pltpu.SMEM), pl.BlockSpec(memory_space=pltpu.VMEM)],
        out_specs=pl.BlockSpec(memory_space=pltpu.VMEM),
    )(rel_bias, bucket)


def bias_bwd(dband0, dband1, bucket):
    def body(d0_ref, d1_ref, bkt_ref, o_ref):
        bkt = bkt_ref[...]
        o_ref[...] = jnp.zeros_like(o_ref)
        for h in range(N_Q_HEADS):
            d = d0_ref[h] + d1_ref[h]
            for b in range(N_BUCKETS):
                part = jnp.sum(jnp.where(bkt == b, d, 0.0), axis=1, keepdims=True)
                o_ref[b:b + 1, h:h + 1] = jnp.sum(part, axis=0, keepdims=True)

    return pl.pallas_call(
        body, name="bias_bwd", out_shape=jax.ShapeDtypeStruct((N_BUCKETS, LANE), f32),
    )(dband0, dband1, bucket)


W_IN_SHARD = D_IN // N_CHIPS
_ALIGNED_PIECES = ((0, 0, 512), (1, 190, 578), (2, 0, 124), (2, 124, 578), (3, 0, 570), (0, 512, 578), (1, 0, 62),
                   (1, 62, 190), (3, 570, 578))
_SHARD_PIECES = (((0, 512), (2048, 2114)), ((2114, 2176), (2176, 2304), (512, 900)), ((900, 1024), (1024, 1478)),
                 ((1478, 2048), (2304, 2312)))


def align_w_in(shards, tr=256):
    def body(s_ref, o_ref):
        parts = [s_ref[k, :, a:b] for k, a, b in _ALIGNED_PIECES]
        parts.append(jnp.zeros((tr, LANE - SSM_HEADS), s_ref.dtype))
        o_ref[...] = jnp.concatenate(parts, axis=-1)

    return pl.pallas_call(
        body, name="align_w_in", grid=(D_MODEL // tr,),
        in_specs=[pl.BlockSpec((N_CHIPS, tr, W_IN_SHARD), lambda i: (0, i, 0))],
        out_specs=pl.BlockSpec((tr, PW), lambda i: (i, 0)),
        out_shape=jax.ShapeDtypeStruct((D_MODEL, PW), shards.dtype),
        compiler_params=_params("arbitrary"),
    )(shards)


def split_w_in_grad(dw, tr=256):
    def body(d_ref, o_ref, o16_ref):
        for k, pieces in enumerate(_SHARD_PIECES):
            part = jnp.concatenate([d_ref[:, a:b] for a, b in pieces], axis=-1)
            o_ref[k] = part
            o16_ref[k] = part.astype(bf16)

    spec = pl.BlockSpec((N_CHIPS, tr, W_IN_SHARD), lambda i: (0, i, 0))
    return pl.pallas_call(
        body, name="split_w_in_grad", grid=(D_MODEL // tr,),
        in_specs=[pl.BlockSpec((tr, PW), lambda i: (i, 0))], out_specs=[spec, spec],
        out_shape=[jax.ShapeDtypeStruct((N_CHIPS, D_MODEL, W_IN_SHARD), f32),
                   jax.ShapeDtypeStruct((N_CHIPS, D_MODEL, W_IN_SHARD), bf16)],
        compiler_params=_params("arbitrary"),
    )(dw)

def in_fwd(x, g, w, tm=256):
    def body(x_ref, g_ref, w_ref, o_ref):
        xv = x_ref[...]
        h = xv * _rms(xv) * g_ref[...]
        o_ref[...] = _bdot(h, w_ref[...])

    return pl.pallas_call(
        body, name="in_fwd", grid=(SEQ // tm,),
        in_specs=[pl.BlockSpec((tm, D_MODEL), lambda i: (i, 0)), _full((1, D_MODEL)), _full((D_MODEL, PW))],
        out_specs=pl.BlockSpec((tm, PW), lambda i: (i, 0)),
        out_shape=jax.ShapeDtypeStruct((SEQ, PW), f32),
        compiler_params=_params("arbitrary"),
    )(x, g, w)


def in_bwd(dq, dz, dxbc, dk, dv, ddt, x, g, w, dres, tm=256):
    def body(dq_ref, dz_ref, dx_ref, dk_ref, dv_ref, ddt_ref, x_ref, g_ref, w_ref, dres_ref, o_ref, dw_ref, dg_ref):
        i = pl.program_id(0)

        @pl.when(i == 0)
        def _():
            dw_ref[...] = jnp.zeros_like(dw_ref)
            dg_ref[...] = jnp.zeros_like(dg_ref)

        dproj = jnp.concatenate([dq_ref[...], dz_ref[...], dx_ref[...], dk_ref[...], dv_ref[...], ddt_ref[...]],
                                axis=-1).astype(bf16)
        xv = x_ref[...]
        r = _rms(xv)
        xhat = xv * r
        gv = g_ref[...]
        h = xhat * gv
        dw_ref[...] += _bdot_tn(h, dproj)
        dh = _bdot_nt(dproj, w_ref[...])
        dg_ref[...] += jnp.sum(dh * xhat, axis=0, keepdims=True)
        o_ref[...] = dres_ref[...] + _rms_bwd(dh, xhat, r, gv)

    tok = lambda w_: pl.BlockSpec((tm, w_), lambda i: (i, 0))
    return pl.pallas_call(
        body, name="in_bwd", grid=(SEQ // tm,),
        in_specs=[tok(D_ATTN), tok(D_SSM), tok(D_CONV), tok(D_KV // 1), tok(D_KV // 1), tok(LANE), tok(D_MODEL),
                  _full((1, D_MODEL)), _full((D_MODEL, PW)), tok(D_MODEL)],
        out_specs=[tok(D_MODEL), _full((D_MODEL, PW)), _full((1, D_MODEL))],
        out_shape=[jax.ShapeDtypeStruct((SEQ, D_MODEL), f32), jax.ShapeDtypeStruct((D_MODEL, PW), f32),
                   jax.ShapeDtypeStruct((1, D_MODEL), f32)],
        compiler_params=_params("arbitrary"),
    )(dq, dz, dxbc, dk, dv, ddt, x, g, w, dres)


def _attn_softmax_t(qk, bias_t, sink, first, key_row):
    s = qk * (HEAD_DIM ** -0.5) + bias_t
    s = jnp.where(jnp.logical_and(first, key_row < BLOCK), NEG, s)
    m = jnp.maximum(jnp.max(s, axis=0, keepdims=True), sink)
    p = jnp.exp(s - m)
    psink = jnp.exp(sink - m)
    inv = 1.0 / (jnp.sum(p, axis=0, keepdims=True) + psink)
    return p * inv, psink * inv


def _rms_t(x_t):
    return lax.rsqrt(jnp.mean(x_t * x_t, axis=0, keepdims=True) + EPS)


def attn_fwd_t(proj, q_gain_col, k_gain, sinks, bias_t):
    kcol, vcol = OFF_K // D_KV, OFF_V // D_KV

    def body(q_ref, kc_ref, kp_ref, vc_ref, vp_ref, qg_ref, kg_ref, sink_ref, bias_ref, o_ref, ot_scr):
        n = pl.program_id(0)
        first = n == 0
        key_row = lax.broadcasted_iota(jnp.int32, (2 * BLOCK, BLOCK), 0)
        k2 = jnp.concatenate([kp_ref[...], kc_ref[...]], axis=0)
        v_t = jnp.concatenate([vp_ref[...], vc_ref[...]], axis=0).T
        q_t = q_ref[...].T
        qg = jnp.broadcast_to(qg_ref[...], (HEAD_DIM, BLOCK))
        kg = kg_ref[...]
        for hk in range(N_KV_HEADS):
            sl = slice(hk * HEAD_DIM, (hk + 1) * HEAD_DIM)
            kk = k2[:, sl]
            kn = (kk * _rms(kk) * kg).astype(bf16)
            vt = v_t[sl, :].astype(bf16)
            heads = range(hk * Q_PER_KV, (hk + 1) * Q_PER_KV)
            qns = []
            for h in heads:
                qh = q_t[h * HEAD_DIM:(h + 1) * HEAD_DIM, :]
                qns.append(qh * _rms_t(qh) * qg)
            scores = [_bdot(kn, qn) for qn in qns]
            for h, s in zip(heads, scores):
                p, _ = _attn_softmax_t(s, bias_ref[h], sink_ref[h], first, key_row)
                ot_scr[h * HEAD_DIM:(h + 1) * HEAD_DIM, :] = _bdot(vt, p)
        o_ref[...] = ot_scr[...].T

    prev = lambda n: jnp.maximum(n - 1, 0)
    return pl.pallas_call(
        body, name="attn_fwd", grid=(N_BLOCKS,),
        in_specs=[pl.BlockSpec((BLOCK, D_ATTN), lambda n: (n, 0)),
                  pl.BlockSpec((BLOCK, D_KV), lambda n: (n, kcol)), pl.BlockSpec((BLOCK, D_KV), lambda n: (prev(n), kcol)),
                  pl.BlockSpec((BLOCK, D_KV), lambda n: (n, vcol)), pl.BlockSpec((BLOCK, D_KV), lambda n: (prev(n), vcol)),
                  _full((HEAD_DIM, 1)), _full((1, HEAD_DIM)), pl.BlockSpec(memory_space=pltpu.SMEM),
                  _full((N_Q_HEADS, 2 * BLOCK, BLOCK))],
        out_specs=pl.BlockSpec((BLOCK, D_ATTN), lambda n: (n, 0)),
        out_shape=jax.ShapeDtypeStruct((SEQ, D_ATTN), f32),
        scratch_shapes=[pltpu.VMEM((D_ATTN, BLOCK), f32)],
        compiler_params=_params("arbitrary"),
    )(proj, proj, proj, proj, proj, q_gain_col, k_gain, sinks, bias_t)


def attn_bwd_t(proj, d_out, q_gain_col, k_gain, sinks, bias_t):
    kcol, vcol = OFF_K // D_KV, OFF_V // D_KV

    def body(q_ref, kc_ref, kp_ref, vc_ref, vp_ref, do_ref, qg_ref, kg_ref, sink_ref, bias_ref,
             dq_ref, dk_ref, dv_ref, dband_ref, dsink_ref, dqg_ref, dkg_ref, dkn_scr, dv_scr, dqt_scr, dsink_acc, dqg_acc):
        i = pl.program_id(0)
        first = i == N_BLOCKS - 1

        @pl.when(i == 0)
        def _():
            for ref in (dband_ref, dkg_ref, dkn_scr, dv_scr, dsink_acc, dqg_acc):
                ref[...] = jnp.zeros_like(ref)

        key_row = lax.broadcasted_iota(jnp.int32, (2 * BLOCK, BLOCK), 0)
        k2 = jnp.concatenate([kp_ref[...], kc_ref[...]], axis=0)
        v2 = jnp.concatenate([vp_ref[...], vc_ref[...]], axis=0)
        q_t = q_ref[...].T
        do_t = do_ref[...].T
        qg = jnp.broadcast_to(qg_ref[...], (HEAD_DIM, BLOCK))
        kg = kg_ref[...]
        scale = HEAD_DIM ** -0.5
        for hk in range(N_KV_HEADS):
            sl = slice(hk * HEAD_DIM, (hk + 1) * HEAD_DIM)
            kk = k2[:, sl]
            rk = _rms(kk)
            khat = kk * rk
            kn = (khat * kg).astype(bf16)
            vb = v2[:, sl].astype(bf16)
            dkn = jnp.zeros((2 * BLOCK, HEAD_DIM), f32)
            dvv = jnp.zeros((2 * BLOCK, HEAD_DIM), f32)
            heads = range(hk * Q_PER_KV, (hk + 1) * Q_PER_KV)
            rqs, qhats, qns, d_os = [], [], [], []
            for h in heads:
                hs = slice(h * HEAD_DIM, (h + 1) * HEAD_DIM)
                qh = q_t[hs, :]
                rqs.append(_rms_t(qh))
                qhats.append(qh * rqs[-1])
                qns.append((qhats[-1] * qg).astype(bf16))
                d_os.append(do_t[hs, :].astype(bf16))
            scores = [_bdot(kn, qn) for qn in qns]
            dps = [_bdot(vb, d_o) for d_o in d_os]
            ps, dss = [], []
            for h, s, dp in zip(heads, scores, dps):
                p, psink = _attn_softmax_t(s, bias_ref[h], sink_ref[h], first, key_row)
                delta = jnp.sum(p * dp, axis=0, keepdims=True)
                ds = p * (dp - delta)
                dband_ref[h] += ds
                dsink_acc[h:h + 1, :] += -(psink * delta)
                ps.append(p.astype(bf16))
                dss.append(ds.astype(bf16))
            dqns = [_bdot_tn(kn, ds) * scale for ds in dss]
            for ds, qn, p, d_o in zip(dss, qns, ps, d_os):
                dkn = dkn + _bdot_nt(ds, qn) * scale
                dvv = dvv + _bdot_nt(p, d_o)
            for h, dqn, rq, qhat in zip(heads, dqns, rqs, qhats):
                dqg_acc[...] += dqn * qhat
                t = dqn * qg
                dqt_scr[h * HEAD_DIM:(h + 1) * HEAD_DIM, :] = rq * (t - qhat * jnp.mean(t * qhat, axis=0, keepdims=True))
            dkn_cur = dkn[BLOCK:] + dkn_scr[:, sl]
            dkn_scr[:, sl] = dkn[:BLOCK]
            khat_c, rk_c = khat[BLOCK:], rk[BLOCK:]
            dkg_ref[...] += jnp.sum(dkn_cur * khat_c, axis=0, keepdims=True)
            dk_ref[:, sl] = _rms_bwd(dkn_cur, khat_c, rk_c, kg)
            dv_ref[:, sl] = dvv[BLOCK:] + dv_scr[:, sl]
            dv_scr[:, sl] = dvv[:BLOCK]
        dq_ref[...] = dqt_scr[...].T

        @pl.when(i == N_BLOCKS - 1)
        def _():
            dsink_ref[...] = jnp.sum(dsink_acc[...], axis=1, keepdims=True)
            dqg_ref[...] = jnp.sum(dqg_acc[...], axis=1, keepdims=True)

    blk = lambda i: N_BLOCKS - 1 - i
    prev = lambda i: jnp.maximum(N_BLOCKS - 2 - i, 0)
    return pl.pallas_call(
        body, name="attn_bwd", grid=(N_BLOCKS,),
        in_specs=[pl.BlockSpec((BLOCK, D_ATTN), lambda i: (blk(i), 0)),
                  pl.BlockSpec((BLOCK, D_KV), lambda i: (blk(i), kcol)), pl.BlockSpec((BLOCK, D_KV), lambda i: (prev(i), kcol)),
                  pl.BlockSpec((BLOCK, D_KV), lambda i: (blk(i), vcol)), pl.BlockSpec((BLOCK, D_KV), lambda i: (prev(i), vcol)),
                  pl.BlockSpec((BLOCK, D_ATTN), lambda i: (blk(i), 0)),
                  _full((HEAD_DIM, 1)), _full((1, HEAD_DIM)), pl.BlockSpec(memory_space=pltpu.SMEM),
                  _full((N_Q_HEADS, 2 * BLOCK, BLOCK))],
        out_specs=[pl.BlockSpec((BLOCK, D_ATTN), lambda i: (blk(i), 0)), pl.BlockSpec((BLOCK, D_KV), lambda i: (blk(i), 0)),
                   pl.BlockSpec((BLOCK, D_KV), lambda i: (blk(i), 0)), _full((N_Q_HEADS, 2 * BLOCK, BLOCK)),
                   _full((N_Q_HEADS, 1)), _full((HEAD_DIM, 1)), _full((1, HEAD_DIM))],
        out_shape=[jax.ShapeDtypeStruct((SEQ, D_ATTN), f32), jax.ShapeDtypeStruct((SEQ, D_KV), f32),
                   jax.ShapeDtypeStruct((SEQ, D_KV), f32), jax.ShapeDtypeStruct((N_Q_HEADS, 2 * BLOCK, BLOCK), f32),
                   jax.ShapeDtypeStruct((N_Q_HEADS, 1), f32), jax.ShapeDtypeStruct((HEAD_DIM, 1), f32),
                   jax.ShapeDtypeStruct((1, HEAD_DIM), f32)],
        scratch_shapes=[pltpu.VMEM((BLOCK, D_KV), f32), pltpu.VMEM((BLOCK, D_KV), f32), pltpu.VMEM((D_ATTN, BLOCK), f32),
                        pltpu.VMEM((N_Q_HEADS, BLOCK), f32), pltpu.VMEM((HEAD_DIM, BLOCK), f32)],
        compiler_params=_params("arbitrary"),
    )(proj, proj, proj, proj, proj, d_out, q_gain_col, k_gain, sinks, bias_t)


def _shift_down(u, s, row):
    if s == 0:
        return u
    return jnp.where(row >= s, pltpu.roll(u, s, 0), 0.0)


def _shift_up(u, s, row):
    if s == 0:
        return u
    return jnp.where(row < SEQ - s, pltpu.roll(u, SEQ - s, 0), 0.0)


def conv_fwd(proj, conv_w, conv_b):
    xcol = OFF_X // LANE

    def body(u_ref, w_ref, b_ref, o_ref):
        u = u_ref[...]
        row = lax.broadcasted_iota(jnp.int32, u.shape, 0)
        pre = b_ref[...] + jnp.zeros_like(u)
        for k in range(CONV_WIDTH):
            pre = pre + w_ref[k:k + 1, :] * _shift_down(u, CONV_WIDTH - 1 - k, row)
        o_ref[...] = pre * _sigmoid(pre)

    return pl.pallas_call(
        body, name="conv_fwd", grid=(D_CONV // LANE,),
        in_specs=[pl.BlockSpec((SEQ, LANE), lambda j: (0, xcol + j)), pl.BlockSpec((CONV_WIDTH, LANE), lambda j: (0, j)),
                  pl.BlockSpec((1, LANE), lambda j: (0, j))],
        out_specs=pl.BlockSpec((SEQ, LANE), lambda j: (0, j)),
        out_shape=jax.ShapeDtypeStruct((SEQ, D_CONV), f32),
        compiler_params=_params("arbitrary"),
    )(proj, conv_w, conv_b)


def conv_bwd(proj, d_act, conv_w, conv_b):
    xcol = OFF_X // LANE

    def body(u_ref, da_ref, w_ref, b_ref, du_ref, dw_ref, db_ref):
        u = u_ref[...]
        row = lax.broadcasted_iota(jnp.int32, u.shape, 0)
        shifted = [_shift_down(u, CONV_WIDTH - 1 - k, row) for k in range(CONV_WIDTH)]
        pre = b_ref[...] + jnp.zeros_like(u)
        for k in range(CONV_WIDTH):
            pre = pre + w_ref[k:k + 1, :] * shifted[k]
        sg = _sigmoid(pre)
        dpre = da_ref[...] * (sg * (1.0 + pre * (1.0 - sg)))
        db_ref[...] = jnp.sum(dpre, axis=0, keepdims=True)
        du = jnp.zeros_like(u)
        for k in range(CONV_WIDTH):
            dw_ref[k:k + 1, :] = jnp.sum(dpre * shifted[k], axis=0, keepdims=True)
            du = du + w_ref[k:k + 1, :] * _shift_up(dpre, CONV_WIDTH - 1 - k, row)
        du_ref[...] = du

    return pl.pallas_call(
        body, name="conv_bwd", grid=(D_CONV // LANE,),
        in_specs=[pl.BlockSpec((SEQ, LANE), lambda j: (0, xcol + j)), pl.BlockSpec((SEQ, LANE), lambda j: (0, j)),
                  pl.BlockSpec((CONV_WIDTH, LANE), lambda j: (0, j)), pl.BlockSpec((1, LANE), lambda j: (0, j))],
        out_specs=[pl.BlockSpec((SEQ, LANE), lambda j: (0, j)), pl.BlockSpec((CONV_WIDTH, LANE), lambda j: (0, j)),
                   pl.BlockSpec((1, LANE), lambda j: (0, j))],
        out_shape=[jax.ShapeDtypeStruct((SEQ, D_CONV), f32), jax.ShapeDtypeStruct((CONV_WIDTH, D_CONV), f32),
                   jax.ShapeDtypeStruct((1, D_CONV), f32)],
        compiler_params=_params("arbitrary"),
    )(proj, d_act, conv_w, conv_b)


def _ssd_chunk_common(dt_raw, dtb, alog):
    row = lax.broadcasted_iota(jnp.int32, (CHUNK, CHUNK), 0)
    col = lax.broadcasted_iota(jnp.int32, (CHUNK, CHUNK), 1)
    tri = (row >= col).astype(f32)
    strict = (row > col).astype(f32)
    dtp = _softplus(dt_raw + dtb)
    a_row = -jnp.exp(alog)
    d_a = dtp * a_row
    cs = _hdot(tri, d_a)
    cs_last = cs[CHUNK - 1:CHUNK, :]
    return row, col, dtp, a_row, cs, cs.T, cs_last


def _seg_decay(cs, cs_t, hd, row, col):
    seg = cs[:, hd:hd + 1] - cs_t[hd:hd + 1, :]
    return jnp.where(row >= col, jnp.exp(seg), 0.0)


GROUP_W = HEADS_PER_GROUP * SSM_HEAD_DIM


def _group_indicator(g):
    j = lax.broadcasted_iota(jnp.int32, (GROUP_W, LANE), 0)
    lane = lax.broadcasted_iota(jnp.int32, (GROUP_W, LANE), 1)
    return (lane == g * HEADS_PER_GROUP + j // SSM_HEAD_DIM).astype(f32)


def _hdot_nt(a, b):
    return lax.dot_general(a, b, (((1,), (1,)), ((), ())), precision=lax.Precision.HIGHEST, preferred_element_type=f32)


def ssd_fwd_g(act, proj, dt_bias, a_log, d_skip, norm_g):
    zcol, dtcol = OFF_Z // D_SSM, OFF_DT // LANE

    def body(act_ref, z_ref, dt_ref, dtb_ref, alog_ref, dsk_ref, ng_ref, out_ref, ypre_ref, st_ref, state):
        c = pl.program_id(0)

        @pl.when(c == 0)
        def _():
            state[...] = jnp.zeros_like(state)

        row, col, dtp, a_row, cs, cs_t, cs_last = _ssd_chunk_common(dt_ref[...], dtb_ref[...], alog_ref[...])
        e_cs = jnp.exp(cs)
        dte = jnp.exp(cs_last - cs)
        rows8 = jnp.concatenate([jnp.exp(cs_last), dsk_ref[...], jnp.zeros((6, LANE), f32)], axis=0)
        z = z_ref[...]
        sz = z * _sigmoid(z)
        ng = ng_ref[...]
        for g in range(SSM_GROUPS):
            gs = slice(g * GROUP_W, (g + 1) * GROUP_W)
            ind = _group_indicator(g)
            xg = act_ref[:, gs]
            bg = act_ref[:, D_SSM + g * SSM_STATE:D_SSM + (g + 1) * SSM_STATE]
            cg = act_ref[:, D_SSM + D_BC + g * SSM_STATE:D_SSM + D_BC + (g + 1) * SSM_STATE]
            dt_e, e_e, dte_e = _hdot_nt(dtp, ind), _hdot_nt(e_cs, ind), _hdot_nt(dte, ind)
            rows_e = _hdot_nt(rows8, ind)
            ecl_e, dsk_e = rows_e[0:1], rows_e[1:2]
            xdt = xg * dt_e
            prev = state[g]
            st_ref[0, g] = prev
            cb = _bdot_nt(cg, bg)
            goff = _bdot(cg, prev)
            snew = _bdot_tn(bg, xdt * dte_e)
            heads = range(g * HEADS_PER_GROUP, (g + 1) * HEADS_PER_GROUP)
            ms = [cb * _seg_decay(cs, cs_t, hd, row, col) for hd in heads]
            yd = [_bdot(m, xdt[:, r * SSM_HEAD_DIM:(r + 1) * SSM_HEAD_DIM]) for r, m in enumerate(ms)]
            y = jnp.concatenate(yd, axis=1) + e_e * goff + xg * dsk_e
            state[g] = prev * ecl_e + snew
            ypre_ref[:, gs] = y
            part = y * sz[:, gs]
            out_ref[:, gs] = part * _rms(part) * ng[:, gs]

    return pl.pallas_call(
        body, name="ssd_fwd", grid=(N_CHUNKS,),
        in_specs=[pl.BlockSpec((CHUNK, D_CONV), lambda c: (c, 0)), pl.BlockSpec((CHUNK, D_SSM), lambda c: (c, zcol)),
                  pl.BlockSpec((CHUNK, LANE), lambda c: (c, dtcol)), _full((1, LANE)), _full((1, LANE)), _full((1, LANE)),
                  _full((1, D_SSM))],
        out_specs=[pl.BlockSpec((CHUNK, D_SSM), lambda c: (c, 0)), pl.BlockSpec((CHUNK, D_SSM), lambda c: (c, 0)),
                   pl.BlockSpec((1, SSM_GROUPS, SSM_STATE, GROUP_W), lambda c: (c, 0, 0, 0))],
        out_shape=[jax.ShapeDtypeStruct((SEQ, D_SSM), f32), jax.ShapeDtypeStruct((SEQ, D_SSM), f32),
                   jax.ShapeDtypeStruct((N_CHUNKS, SSM_GROUPS, SSM_STATE, GROUP_W), f32)],
        scratch_shapes=[pltpu.VMEM((SSM_GROUPS, SSM_STATE, GROUP_W), f32)],
        compiler_params=_params("arbitrary"),
    )(act, proj, proj, dt_bias, a_log, d_skip, norm_g)


def ssd_bwd_g(act, proj, ypre, states, d_out, dt_bias, a_log, d_skip, norm_g):
    zcol, dtcol = OFF_Z // D_SSM, OFF_DT // LANE

    def body(act_ref, z_ref, dt_ref, ypre_ref, st_ref, do_ref, dtb_ref, alog_ref, dsk_ref, ng_ref,
             dact_ref, ddt_ref, dz_ref, dng_ref, dpar_ref, dstate):
        i = pl.program_id(0)

        @pl.when(i == 0)
        def _():
            for ref in (dng_ref, dpar_ref, dstate):
                ref[...] = jnp.zeros_like(ref)

        row, col, dtp, a_row, cs, cs_t, cs_last = _ssd_chunk_common(dt_ref[...], dtb_ref[...], alog_ref[...])
        upper = (row <= col).astype(f32)
        lane = lax.broadcasted_iota(jnp.int32, (CHUNK, LANE), 1)
        rowl = lax.broadcasted_iota(jnp.int32, (CHUNK, LANE), 0)
        e_cs = jnp.exp(cs)
        dte = jnp.exp(cs_last - cs)
        ecl = jnp.exp(cs_last)
        rows8 = jnp.concatenate([ecl, dsk_ref[...], jnp.zeros((6, LANE), f32)], axis=0)
        z = z_ref[...]
        sgz = _sigmoid(z)
        sz = z * sgz
        ng = ng_ref[...]
        ddt_mat = jnp.zeros((CHUNK, LANE), f32)
        dcs_mat = jnp.zeros((CHUNK, LANE), f32)
        dcs_t = jnp.zeros((LANE, CHUNK), f32)
        dcsl_row = jnp.zeros((1, LANE), f32)
        dd_row = jnp.zeros((1, LANE), f32)
        for g in range(SSM_GROUPS):
            gs = slice(g * GROUP_W, (g + 1) * GROUP_W)
            bsl = slice(D_SSM + g * SSM_STATE, D_SSM + (g + 1) * SSM_STATE)
            csl = slice(D_SSM + D_BC + g * SSM_STATE, D_SSM + D_BC + (g + 1) * SSM_STATE)
            ind = _group_indicator(g)
            y = ypre_ref[:, gs]
            part = y * sz[:, gs]
            r = _rms(part)
            yhat = part * r
            d_o = do_ref[:, gs]
            dng_ref[:, gs] += jnp.sum(d_o * yhat, axis=0, keepdims=True)
            dyz = _rms_bwd(d_o, yhat, r, ng[:, gs])
            dy = dyz * sz[:, gs]
            dz_ref[:, gs] = dyz * y * (sgz[:, gs] * (1.0 + z[:, gs] * (1.0 - sgz[:, gs])))

            xg = act_ref[:, gs]
            bg = act_ref[:, bsl]
            cg = act_ref[:, csl]
            dt_e, e_e, dte_e = _hdot_nt(dtp, ind), _hdot_nt(e_cs, ind), _hdot_nt(dte, ind)
            rows_e = _hdot_nt(rows8, ind)
            ecl_e, dsk_e = rows_e[0:1], rows_e[1:2]
            xdt = xg * dt_e
            prev = st_ref[0, g]
            dh = dstate[g]
            heads = range(g * HEADS_PER_GROUP, (g + 1) * HEADS_PER_GROUP)
            hsl = [slice(r_ * SSM_HEAD_DIM, (r_ + 1) * SSM_HEAD_DIM) for r_ in range(HEADS_PER_GROUP)]
            cb = _bdot_nt(cg, bg)
            lms = [_seg_decay(cs, cs_t, hd, row, col) for hd in heads]
            ms = [cb * lm for lm in lms]
            gmat = _bdot(cg, prev)
            dgm = dy * e_e
            dcg = _bdot_nt(dgm, prev)
            dprev = _bdot_tn(cg, dgm)
            dbg = _bdot_nt(xdt * dte_e, dh)
            dw = _bdot(bg, dh)
            dms = [_bdot_nt(dy[:, s_], xdt[:, s_]) for s_ in hsl]
            dxdts = [_bdot_tn(m, dy[:, s_]) for m, s_ in zip(ms, hsl)]
            dxdt = jnp.concatenate(dxdts, axis=1) + dw * dte_e
            dact_ref[:, gs] = dy * dsk_e + dxdt * dt_e
            dstate[g] = dprev + dh * ecl_e
            dcb = jnp.zeros((CHUNK, CHUNK), f32)
            for hd, dm, lm, m in zip(heads, dms, lms, ms):
                dcb = dcb + dm * lm
                dseg = dm * m
                dcs_mat = dcs_mat + jnp.where(lane == hd, jnp.sum(dseg, axis=1, keepdims=True), 0.0)
                dcs_t = jnp.where(row == hd, jnp.sum(dseg, axis=0, keepdims=True), dcs_t)
            dact_ref[:, bsl] = dbg + _bdot_tn(dcb, cg)
            dact_ref[:, csl] = dcg + _bdot(dcb, bg)
            ddte = _hdot(dw * xdt, ind) * dte
            dcs_mat = dcs_mat + _hdot(dy * gmat, ind) * e_cs - ddte
            ddt_mat = ddt_mat + _hdot(dxdt * xg, ind)
            dcsl_row = dcsl_row + jnp.sum(ddte, axis=0, keepdims=True) + jnp.sum(_hdot(dh * prev, ind), axis=0, keepdims=True) * ecl
            dd_row = dd_row + jnp.sum(_hdot(dy * xg, ind), axis=0, keepdims=True)
        dcs_mat = dcs_mat - dcs_t.T + jnp.where(rowl == CHUNK - 1, dcsl_row, 0.0)
        dda = _hdot(upper, dcs_mat)
        ddt_mat = ddt_mat + dda * a_row
        da_row = jnp.sum(dda * dtp, axis=0, keepdims=True)
        ddt_raw = ddt_mat * _sigmoid(dt_ref[...] + dtb_ref[...])
        ddt_ref[...] = ddt_raw
        dpar_ref[0:1, :] += jnp.sum(ddt_raw, axis=0, keepdims=True)
        dpar_ref[1:2, :] += da_row * a_row
        dpar_ref[2:3, :] += dd_row

    blk = lambda i: N_CHUNKS - 1 - i
    return pl.pallas_call(
        body, name="ssd_bwd", grid=(N_CHUNKS,),
        in_specs=[pl.BlockSpec((CHUNK, D_CONV), lambda i: (blk(i), 0)), pl.BlockSpec((CHUNK, D_SSM), lambda i: (blk(i), zcol)),
                  pl.BlockSpec((CHUNK, LANE), lambda i: (blk(i), dtcol)), pl.BlockSpec((CHUNK, D_SSM), lambda i: (blk(i), 0)),
                  pl.BlockSpec((1, SSM_GROUPS, SSM_STATE, GROUP_W), lambda i: (blk(i), 0, 0, 0)),
                  pl.BlockSpec((CHUNK, D_SSM), lambda i: (blk(i), 0)),
                  _full((1, LANE)), _full((1, LANE)), _full((1, LANE)), _full((1, D_SSM))],
        out_specs=[pl.BlockSpec((CHUNK, D_CONV), lambda i: (blk(i), 0)), pl.BlockSpec((CHUNK, LANE), lambda i: (blk(i), 0)),
                   pl.BlockSpec((CHUNK, D_SSM), lambda i: (blk(i), 0)), _full((1, D_SSM)), _full((8, LANE))],
        out_shape=[jax.ShapeDtypeStruct((SEQ, D_CONV), f32), jax.ShapeDtypeStruct((SEQ, LANE), f32),
                   jax.ShapeDtypeStruct((SEQ, D_SSM), f32), jax.ShapeDtypeStruct((1, D_SSM), f32),
                   jax.ShapeDtypeStruct((8, LANE), f32)],
        scratch_shapes=[pltpu.VMEM((SSM_GROUPS, SSM_STATE, GROUP_W), f32)],
        compiler_params=_params("arbitrary"),
    )(act, proj, proj, ypre, states, d_out, dt_bias, a_log, d_skip, norm_g)


def out_fwd(x, attn, ssm, w_out, tm=512):
    def body(x_ref, a_ref, s_ref, w_ref, o_ref):
        o_ref[...] = x_ref[...] + _bdot(a_ref[...], w_ref[:D_ATTN, :]) + _bdot(s_ref[...], w_ref[D_ATTN:, :])

    tok = lambda w_: pl.BlockSpec((tm, w_), lambda i: (i, 0))
    return pl.pallas_call(
        body, name="out_fwd", grid=(SEQ // tm,),
        in_specs=[tok(D_MODEL), tok(D_ATTN), tok(D_SSM), _full((D_MODEL, D_MODEL))],
        out_specs=tok(D_MODEL), out_shape=jax.ShapeDtypeStruct((SEQ, D_MODEL), f32),
        compiler_params=_params("arbitrary"),
    )(x, attn, ssm, w_out)


def out_bwd(dx1, attn, ssm, w_out, tm=512):
    nt = SEQ // tm

    def body(d_ref, a_ref, s_ref, w_ref, da_ref, ds_ref, dw_ref, dw16_ref):
        i = pl.program_id(0)

        @pl.when(i == 0)
        def _():
            dw_ref[...] = jnp.zeros_like(dw_ref)

        d = d_ref[...].astype(bf16)
        dcat = _bdot_nt(d, w_ref[...])
        da_ref[...] = dcat[:, :D_ATTN]
        ds_ref[...] = dcat[:, D_ATTN:]
        dw_ref[:D_ATTN, :] += _bdot_tn(a_ref[...], d)
        dw_ref[D_ATTN:, :] += _bdot_tn(s_ref[...], d)

        @pl.when(i == nt - 1)
        def _():
            dw16_ref[...] = dw_ref[...].astype(bf16)

    tok = lambda w_: pl.BlockSpec((tm, w_), lambda i: (i, 0))
    return pl.pallas_call(
        body, name="out_bwd", grid=(nt,),
        in_specs=[tok(D_MODEL), tok(D_ATTN), tok(D_SSM), _full((D_MODEL, D_MODEL))],
        out_specs=[tok(D_ATTN), tok(D_SSM), _full((D_MODEL, D_MODEL)), _full((D_MODEL, D_MODEL))],
        out_shape=[jax.ShapeDtypeStruct((SEQ, D_ATTN), f32), jax.ShapeDtypeStruct((SEQ, D_SSM), f32),
                   jax.ShapeDtypeStruct((D_MODEL, D_MODEL), f32), jax.ShapeDtypeStruct((D_MODEL, D_MODEL), bf16)],
        compiler_params=_params("arbitrary"),
    )(dx1, attn, ssm, w_out)


def mlp_fwd(x1, g, w_up, w_down, tm=1024):
    def body(x_ref, g_ref, wu_ref, wd_ref, o_ref, u_ref, h_scr):
        j = pl.program_id(1)

        @pl.when(j == 0)
        def _():
            xv = x_ref[...]
            h_scr[...] = (xv * _rms(xv) * g_ref[...]).astype(bf16)
            o_ref[...] = xv

        u = jnp.dot(h_scr[...], wu_ref[...], preferred_element_type=f32)
        u_ref[...] = u
        a = jnp.square(jnp.maximum(u, 0.0))
        o_ref[...] += _bdot(a, wd_ref[...])

    return pl.pallas_call(
        body, name="mlp_fwd", grid=(SEQ // tm, N_CHIPS),
        in_specs=[pl.BlockSpec((tm, D_MODEL), lambda i, j: (i, 0)), _full((1, D_MODEL)),
                  pl.BlockSpec((None, D_MODEL, FF_TILE), lambda i, j: (j, 0, 0)),
                  pl.BlockSpec((None, FF_TILE, D_MODEL), lambda i, j: (j, 0, 0))],
        out_specs=[pl.BlockSpec((tm, D_MODEL), lambda i, j: (i, 0)), pl.BlockSpec((tm, FF_TILE), lambda i, j: (i, j))],
        out_shape=[jax.ShapeDtypeStruct((SEQ, D_MODEL), f32), jax.ShapeDtypeStruct((SEQ, D_FF), f32)],
        scratch_shapes=[pltpu.VMEM((tm, D_MODEL), bf16)],
        compiler_params=_params("arbitrary", "arbitrary"),
    )(x1, g, w_up, w_down)


def mlp_bwd_data(dx2, u, x1, g, w_up, w_down, tm=1024):
    def body(d_ref, u_ref, x_ref, g_ref, wu_ref, wd_ref, dx_ref, du_ref, dg_ref, dh_scr):
        i, j = pl.program_id(0), pl.program_id(1)

        @pl.when(jnp.logical_and(i == 0, j == 0))
        def _():
            dg_ref[...] = jnp.zeros_like(dg_ref)

        @pl.when(j == 0)
        def _():
            dh_scr[...] = jnp.zeros_like(dh_scr)

        da = _bdot_nt(d_ref[...], wd_ref[...])
        du = (da * (2.0 * jnp.maximum(u_ref[...], 0.0))).astype(bf16)
        du_ref[...] = du
        dh_scr[...] += _bdot_nt(du, wu_ref[...])

        @pl.when(j == N_CHIPS - 1)
        def _():
            xv = x_ref[...]
            r = _rms(xv)
            xhat = xv * r
            dh = dh_scr[...]
            dg_ref[...] += jnp.sum(dh * xhat, axis=0, keepdims=True)
            dx_ref[...] = d_ref[...] + _rms_bwd(dh, xhat, r, g_ref[...])

    return pl.pallas_call(
        body, name="mlp_bwd_data", grid=(SEQ // tm, N_CHIPS),
        in_specs=[pl.BlockSpec((tm, D_MODEL), lambda i, j: (i, 0)), pl.BlockSpec((tm, FF_TILE), lambda i, j: (i, j)),
                  pl.BlockSpec((tm, D_MODEL), lambda i, j: (i, 0)), _full((1, D_MODEL)),
                  pl.BlockSpec((None, D_MODEL, FF_TILE), lambda i, j: (j, 0, 0)),
                  pl.BlockSpec((None, FF_TILE, D_MODEL), lambda i, j: (j, 0, 0))],
        out_specs=[pl.BlockSpec((tm, D_MODEL), lambda i, j: (i, 0)), pl.BlockSpec((tm, FF_TILE), lambda i, j: (i, j)),
                   _full((1, D_MODEL))],
        out_shape=[jax.ShapeDtypeStruct((SEQ, D_MODEL), f32), jax.ShapeDtypeStruct((SEQ, D_FF), bf16),
                   jax.ShapeDtypeStruct((1, D_MODEL), f32)],
        scratch_shapes=[pltpu.VMEM((tm, D_MODEL), f32)],
        compiler_params=_params("arbitrary", "arbitrary"),
    )(dx2, u, x1, g, w_up, w_down)


def mlp_bwd_weights(dx2, u, du, x1, g, tm=512):
    nt = SEQ // tm

    def body(d_ref, u_ref, du_ref, x_ref, g_ref, dwu_ref, dwd_ref, dwu16_ref, dwd16_ref, h_scr, d_scr):
        j, i = pl.program_id(0), pl.program_id(1)

        @pl.when(j == 0)
        def _():
            xv = x_ref[...]
            h_scr[i] = (xv * _rms(xv) * g_ref[...]).astype(bf16)
            d_scr[i] = d_ref[...].astype(bf16)

        @pl.when(i == 0)
        def _():
            dwu_ref[...] = jnp.zeros_like(dwu_ref)
            dwd_ref[...] = jnp.zeros_like(dwd_ref)

        dwu_ref[...] += _bdot_tn(h_scr[i], du_ref[...])
        a = jnp.square(jnp.maximum(u_ref[...], 0.0))
        dwd_ref[...] += _bdot_tn(a, d_scr[i])

        @pl.when(i == nt - 1)
        def _():
            dwu16_ref[...] = dwu_ref[...].astype(bf16)
            dwd16_ref[...] = dwd_ref[...].astype(bf16)

    up = pl.BlockSpec((None, D_MODEL, FF_TILE), lambda j, i: (j, 0, 0))
    down = pl.BlockSpec((None, FF_TILE, D_MODEL), lambda j, i: (j, 0, 0))
    first_pass = pl.BlockSpec((tm, D_MODEL), lambda j, i: (jnp.where(j == 0, i, nt - 1), 0))
    return pl.pallas_call(
        body, name="mlp_bwd_weights", grid=(N_CHIPS, nt),
        in_specs=[first_pass, pl.BlockSpec((tm, FF_TILE), lambda j, i: (i, j)),
                  pl.BlockSpec((tm, FF_TILE), lambda j, i: (i, j)), first_pass, _full((1, D_MODEL))],
        out_specs=[up, down, up, down],
        out_shape=[jax.ShapeDtypeStruct((N_CHIPS, D_MODEL, FF_TILE), f32), jax.ShapeDtypeStruct((N_CHIPS, FF_TILE, D_MODEL), f32),
                   jax.ShapeDtypeStruct((N_CHIPS, D_MODEL, FF_TILE), bf16), jax.ShapeDtypeStruct((N_CHIPS, FF_TILE, D_MODEL), bf16)],
        scratch_shapes=[pltpu.VMEM((nt, tm, D_MODEL), bf16), pltpu.VMEM((nt, tm, D_MODEL), bf16)],
        compiler_params=_params("arbitrary", "arbitrary"),
    )(dx2, u, du, x1, g)


def loss_head(y, target, tm=512):
    def body(y_ref, t_ref, dy_ref, l_ref):
        @pl.when(pl.program_id(0) == 0)
        def _():
            l_ref[...] = jnp.zeros_like(l_ref)

        d = y_ref[...] - t_ref[...]
        dy_ref[...] = d * (1.0 / D_MODEL)
        part = jnp.sum(jnp.mean(d * d, axis=-1, keepdims=True), axis=0, keepdims=True)
        l_ref[...] += 0.5 * part

    tok = pl.BlockSpec((tm, D_MODEL), lambda i: (i, 0))
    return pl.pallas_call(
        body, name="loss_head", grid=(SEQ // tm,), in_specs=[tok, tok], out_specs=[tok, _full((1, 1))],
        out_shape=[jax.ShapeDtypeStruct((SEQ, D_MODEL), f32), jax.ShapeDtypeStruct((1, 1), f32)],
        compiler_params=_params("arbitrary"),
    )(y, target)


def _pad_lane(v):
    return jnp.pad(v, (0, LANE - v.shape[0]))[None, :]


def local_step(x, target, w, prov):
    bucket = jnp.asarray(_bucket_table().T)
    bias = bias_build(w["rel_bias"], bucket)
    saved = []
    for l in range(DEPTH):
        g_mix = w["mix_norm_g"][l][None, :] + prov.stage(("begin", l), x)
        w_in = prov.w_in(l, x)
        proj = in_fwd(x, g_mix, w_in)
        qg, kg = w["q_gain"][l][:, None], w["k_gain"][l][None, :]
        attn = attn_fwd_t(proj, qg, kg, w["sinks"][l], bias)
        conv_b = w["conv_b"][l][None, :]
        act = conv_fwd(proj, w["conv_w"][l], conv_b)
        dtb = _pad_lane(w["dt_bias"][l]) + prov.stage(("mid", l), act)
        alog, dsk = _pad_lane(w["a_log"][l]), _pad_lane(w["d_skip"][l])
        ng = w["ssm_norm_g"][l][None, :]
        ssm, ypre, states = ssd_fwd_g(act, proj, dtb, alog, dsk, ng)
        tok = prov.stage(("pre_out", l), ssm)
        w_out = prov.w_out(l, ssm) + jnp.asarray(tok, bf16)
        x1 = out_fwd(x, attn, ssm, w_out)
        g_mlp = w["mlp_norm_g"][l][None, :] + prov.stage(("pre_mlp", l), x1)
        w_up, w_down = prov.mlp(l, x1)
        x2, u = mlp_fwd(x1, g_mlp, w_up, w_down)
        saved.append(dict(x=x, proj=proj, attn=attn, act=act, ssm=ssm, ypre=ypre, states=states, x1=x1, u=u,
                          g_mix=g_mix, qg=qg, kg=kg, conv_b=conv_b, dtb=dtb, alog=alog, dsk=dsk, ng=ng, g_mlp=g_mlp,
                          w_in=w_in, w_out=w_out, w_up=w_up, w_down=w_down))
        x = x2
    dx, loss = loss_head(x, target)
    grads = [None] * DEPTH
    dbands = [None] * DEPTH
    tok = 0.0
    for l in reversed(range(DEPTH)):
        s = saved[l]
        g_mlp = s["g_mlp"] + tok
        dx1, du, dg_mlp = mlp_bwd_data(dx, s["u"], s["x1"], g_mlp, s["w_up"], s["w_down"])
        dw_up, dw_down, dw_up16, dw_down16 = mlp_bwd_weights(dx, s["u"], du, s["x1"], g_mlp)
        tok = prov.grads(("mlp", l), dict(w_up=(dw_up, dw_up16), w_down=(dw_down, dw_down16)), dx1)
        dattn, dssm, dw_out, dw_out16 = out_bwd(dx1, s["attn"], s["ssm"], s["w_out"])
        dact, ddt, dz, dng, dpar = ssd_bwd_g(s["act"], s["proj"], s["ypre"], s["states"], dssm, s["dtb"] + tok, s["alog"],
                                           s["dsk"], s["ng"])
        conv_b = s["conv_b"] + prov.stage(("bwd_mid", l), dact)
        dxbc, dconv_w, dconv_b = conv_bwd(s["proj"], dact, w["conv_w"][l], conv_b)
        dq, dk, dv, dband, dsink, dqg, dkg = attn_bwd_t(s["proj"], dattn, s["qg"], s["kg"], w["sinks"][l], bias)
        dx, dw_in, dg_mix = in_bwd(dq, dz, dxbc, dk, dv, ddt, s["x"], s["g_mix"], s["w_in"], dx1)
        tok = prov.grads(("mix", l), dict(w_in=split_w_in_grad(dw_in), w_out=(dw_out, dw_out16)), dx)
        dbands[l] = dband
        grads[l] = dict(mix_norm_g=dg_mix[0], q_gain=dqg[:, 0], k_gain=dkg[0], sinks=dsink[:, 0],
                        conv_w=dconv_w, conv_b=dconv_b[0], dt_bias=dpar[0, :SSM_HEADS], a_log=dpar[1, :SSM_HEADS],
                        d_skip=dpar[2, :SSM_HEADS], ssm_norm_g=dng[0], mlp_norm_g=dg_mlp[0])
    out = {k: jnp.stack([grads[l][k] for l in range(DEPTH)]) for k in grads[0]}
    out["rel_bias"] = bias_bwd(dbands[0], dbands[1], bucket)[:, :N_Q_HEADS]
    return loss, dx, out, tok


MESH = pl.DeviceIdType.MESH
HBM = pl.BlockSpec(memory_space=pltpu.HBM)
N_PEER_CHIPS = N_CHIPS - 1
N_DEVICES = 8


def _coords():
    return lax.axis_index("x"), lax.axis_index("y"), lax.axis_index("c")


def _peer_chips(x, y):
    return [(1 - x, y), (x, 1 - y), (1 - x, 1 - y)]


def _remote(src, dst, send_sem, recv_sem, device):
    return pltpu.make_async_remote_copy(src_ref=src, dst_ref=dst, send_sem=send_sem, recv_sem=recv_sem,
                                        device_id=device, device_id_type=MESH)


SEM = pl.BlockSpec(memory_space=pltpu.SEMAPHORE)
ANY = pl.BlockSpec(memory_space=pl.ANY)
DATAFLOW = pltpu.SideEffectType.DATAFLOW_SIDE_EFFECTING


def _gather_copies(kind, src_refs, land_refs, ssem, rsem):
    x, y, c = _coords()
    k_me = 2 * x + y
    cps = []
    for p, land in enumerate(land_refs):
        hr = land.shape[1] // 2
        rows = pl.ds(c * hr, hr)
        for j, chip in enumerate(_peer_chips(x, y)):
            i = 3 * p + j
            if kind == "ici":
                cps.append(_remote(src_refs[p].at[rows, :], land.at[k_me, rows, :], ssem.at[i], rsem.at[i], (*chip, c)))
            else:
                got = land.at[2 * chip[0] + chip[1], rows, :]
                cps.append(_remote(got, got, ssem.at[i], rsem.at[i], (x, y, 1 - c)))
    return cps


def gather_now(srcs, conv):
    n = len(srcs)

    def body(*refs):
        src_refs, conv_ref = refs[:n], refs[n]
        lands, gconv = refs[n + 1:2 * n + 1], refs[2 * n + 1]
        ssem, rsem, fsem, frsem, csem, crsem = refs[2 * n + 2:]
        x, y, c = _coords()
        k_me = 2 * x + y
        chips = _peer_chips(x, y)
        ici = _gather_copies("ici", src_refs, lands, ssem, rsem)
        relay = _gather_copies("relay", src_refs, lands, fsem, frsem)
        conv_cps = [_remote(conv_ref, gconv.at[k_me], csem.at[j], crsem.at[j], (*chip, c)) for j, chip in enumerate(chips)]
        for cp in ici + conv_cps:
            cp.start()
        for cp, fw in zip(ici, relay):
            cp.wait_recv()
            fw.start()
        for cp in conv_cps + relay:
            cp.wait_recv()
        for cp in ici + relay + conv_cps:
            cp.wait_send()

    out_shape = [jax.ShapeDtypeStruct((N_CHIPS,) + s.shape, s.dtype) for s in srcs]
    out_shape.append(jax.ShapeDtypeStruct((N_CHIPS,) + conv.shape, conv.dtype))
    sems = lambda k: pltpu.SemaphoreType.DMA((k,))
    return pl.pallas_call(
        body, name="gather_now", out_shape=out_shape, in_specs=[HBM] * (n + 1), out_specs=[HBM] * (n + 1),
        scratch_shapes=[sems(3 * n), sems(3 * n), sems(3 * n), sems(3 * n), sems(N_PEER_CHIPS), sems(N_PEER_CHIPS)],
    )(*srcs, conv)


def _gather_maker(kind, n_src):
    def make(refs, ssem, rsem):
        cps = _gather_copies(kind, refs[:n_src], refs[n_src:], ssem, rsem)
        return cps, cps
    return make


def _scatter_maker(n):
    def make(refs, ssem, rsem):
        x, y, c = _coords()
        k_me = 2 * x + y
        sends, arrivals = [], []
        for p in range(n):
            src, land = refs[p], refs[n + p]
            sends.append(_remote(src.at[k_me, 1 - c], land.at[0], ssem.at[7 * p], rsem.at[7 * p], (x, y, 1 - c)))
            for j, chip in enumerate(_peer_chips(x, y)):
                for cc in range(2):
                    sends.append(_remote(src.at[2 * chip[0] + chip[1], cc], land.at[1 + 2 * j + c],
                                         ssem.at[7 * p + 1 + 2 * j + cc], rsem.at[7 * p + 1 + 2 * j + c], (*chip, cc)))
            for s in range(7):
                arrivals.append(_remote(land.at[s], land.at[s], ssem.at[7 * p + s], rsem.at[7 * p + s], (x, y, 1 - c)))
        return sends, arrivals
    return make


def _share_maker(n):
    def make(refs, ssem, rsem):
        x, y, c = _coords()
        sends = [_remote(refs[p].at[c], refs[p].at[c], ssem.at[p], rsem.at[p], (x, y, 1 - c)) for p in range(n)]
        arrivals = [_remote(refs[p].at[1 - c], refs[p].at[1 - c], ssem.at[p], rsem.at[p], (x, y, 1 - c)) for p in range(n)]
        return sends, arrivals
    return make


def split_start(name, make, n_sems, operands, after):
    n = len(operands)

    def body(*refs):
        ssem, rsem, token = refs[n + 1], refs[n + 2], refs[-1]
        for cp in make(refs[:n], ssem, rsem)[0]:
            cp.start()
        token[...] = jnp.zeros_like(token)

    ops = [pltpu.with_memory_space_constraint(a, pltpu.HBM) for a in operands]
    outs = pl.pallas_call(
        body, name=name,
        out_shape=(pltpu.SemaphoreType.DMA((n_sems,)), pltpu.SemaphoreType.DMA((n_sems,)),
                   *[pltpu.HBM(a.shape, a.dtype) for a in ops], jax.ShapeDtypeStruct((8, LANE), f32)),
        in_specs=[HBM] * n + [ANY], out_specs=(SEM, SEM, *[HBM] * n, pl.BlockSpec(memory_space=pltpu.VMEM)),
        input_output_aliases={i: 2 + i for i in range(n)},
        compiler_params=pltpu.CompilerParams(has_side_effects=DATAFLOW),
    )(*ops, after)
    return dict(name=name, make=make, ssem=outs[0], rsem=outs[1], operands=outs[2:2 + n], token=outs[-1][0, 0])


def split_wait(handle, after):
    n = len(handle["operands"])

    def body(*refs):
        sends, arrivals = handle["make"](refs[:n], refs[n], refs[n + 1])
        for cp in sends:
            cp.wait_send()
        for cp in arrivals:
            cp.wait_recv()

    outs = pl.pallas_call(
        body, name=handle["name"].replace("start", "wait"),
        out_shape=tuple(pltpu.HBM(a.shape, a.dtype) for a in handle["operands"]),
        in_specs=[HBM] * n + [SEM, SEM, ANY], out_specs=tuple([HBM] * n),
        input_output_aliases={i: i for i in range(n)},
        compiler_params=pltpu.CompilerParams(has_side_effects=DATAFLOW),
    )(*handle["operands"], handle["ssem"], handle["rsem"], after)
    return list(outs)


def piece_sum(g, recv, kc_arr):
    _, _, rb, cc = g.shape
    tr = min(256, rb)

    def body(kc_ref, g_ref, r_ref, o_ref):
        acc = g_ref[...]
        for s in range(7):
            acc = acc + r_ref[s].astype(f32)
        o_ref[...] = acc

    return pl.pallas_call(
        body, name="piece_sum",
        grid_spec=pltpu.PrefetchScalarGridSpec(
            num_scalar_prefetch=1, grid=(rb // tr,),
            in_specs=[pl.BlockSpec((None, None, tr, cc), lambda r, kc: (kc[0], kc[1], r, 0)),
                      pl.BlockSpec((7, tr, cc), lambda r, kc: (0, r, 0))],
            out_specs=pl.BlockSpec((None, tr, cc), lambda r, kc: (kc[1], r, 0))),
        out_shape=jax.ShapeDtypeStruct((2, rb, cc), f32),
        compiler_params=_params("arbitrary"),
    )(kc_arr, g, recv)


def small_all_reduce(vec):
    def body(v_ref, o_ref, gat, ssem, rsem):
        x, y, c = _coords()
        me = 4 * x + 2 * y + c
        gat[me] = v_ref[...]
        sends = []
        for t in range(1, N_DEVICES):
            peer = (x ^ (t >> 2), y ^ ((t >> 1) & 1), c ^ (t & 1))
            cp = _remote(v_ref, gat.at[me], ssem.at[t - 1], rsem.at[t - 1], peer)
            cp.start()
            sends.append(cp)
        for t in range(1, N_DEVICES):
            peer = (x ^ (t >> 2), y ^ ((t >> 1) & 1), c ^ (t & 1))
            slot = gat.at[4 * peer[0] + 2 * peer[1] + peer[2]]
            _remote(slot, slot, ssem.at[t - 1], rsem.at[t - 1], peer).wait_recv()
        for cp in sends:
            cp.wait_send()
        acc = gat[0]
        for d in range(1, N_DEVICES):
            acc = acc + gat[d]
        o_ref[...] = acc

    return pl.pallas_call(
        body, name="small_all_reduce", out_shape=jax.ShapeDtypeStruct(vec.shape, vec.dtype),
        in_specs=[pl.BlockSpec(memory_space=pltpu.VMEM)], out_specs=pl.BlockSpec(memory_space=pltpu.VMEM),
        scratch_shapes=[pltpu.VMEM((N_DEVICES,) + vec.shape, vec.dtype), pltpu.SemaphoreType.DMA((N_DEVICES - 1,)),
                        pltpu.SemaphoreType.DMA((N_DEVICES - 1,))],
    )(vec)


def _adamw_math(w, g, m, v):
    m_new = ADAM_B1 * m + (1.0 - ADAM_B1) * g
    v_new = ADAM_B2 * v + (1.0 - ADAM_B2) * jnp.square(g)
    m_hat = m_new / (1.0 - ADAM_B1 ** ADAM_STEP)
    v_hat = v_new / (1.0 - ADAM_B2 ** ADAM_STEP)
    delta = -ADAM_LR * (m_hat / (jnp.sqrt(v_hat) + ADAM_EPS) + ADAM_WD * w)
    return delta, m_new, v_new


def adamw_shard(w, g0, g1, m, v):
    depth, rows, cols = w.shape
    half = rows // 2
    tr = min(256, half)
    nr = half // tr

    def body(w_ref, g0_ref, g1_ref, m_ref, v_ref, go_ref, d_ref, nm_ref, nv_ref):
        gv = jnp.where(pl.program_id(0) == 0, g0_ref[...], g1_ref[...])
        go_ref[...] = gv
        d_ref[...], nm_ref[...], nv_ref[...] = _adamw_math(w_ref[...], gv, m_ref[...], v_ref[...])

    spec = pl.BlockSpec((None, tr, cols), lambda l, h, r: (l, h * nr + r, 0))
    g0spec = pl.BlockSpec((None, tr, cols), lambda l, h, r: (jnp.where(l == 0, h, 1), jnp.where(l == 0, r, nr - 1), 0))
    g1spec = pl.BlockSpec((None, tr, cols), lambda l, h, r: (jnp.where(l == 1, h, 0), jnp.where(l == 1, r, 0), 0))
    return pl.pallas_call(
        body, name="adamw_shard", grid=(depth, 2, nr), in_specs=[spec, g0spec, g1spec, spec, spec], out_specs=[spec] * 4,
        out_shape=[jax.ShapeDtypeStruct(w.shape, f32)] * 4,
        compiler_params=_params("arbitrary", "arbitrary", "arbitrary"),
    )(w, g0, g1, m, v)


def adamw_small(w, g, m, v):
    def body(w_ref, g_ref, m_ref, v_ref, d_ref, nm_ref, nv_ref):
        d_ref[...], nm_ref[...], nv_ref[...] = _adamw_math(w_ref[...], g_ref[...], m_ref[...], v_ref[...])

    return pl.pallas_call(
        body, name="adamw_small", out_shape=[jax.ShapeDtypeStruct(w.shape, f32)] * 3,
    )(w, g, m, v)


WEIGHTS = ("mix_norm_g", "w_in", "q_gain", "k_gain", "sinks", "rel_bias", "conv_w", "conv_b", "dt_bias", "a_log", "d_skip",
           "ssm_norm_g", "w_out", "mlp_norm_g", "w_up", "w_down")
BIG = ("w_in", "w_out", "w_up", "w_down")
SMALL = tuple(n for n in WEIGHTS if n not in BIG)
PACK_COLS = 1024
PACK_ROWS = 16


def _pack(named):
    flat = jnp.concatenate([named[n].reshape(-1) for n in SMALL])
    return jnp.pad(flat, (0, PACK_ROWS * PACK_COLS - flat.shape[0])).reshape(PACK_ROWS, PACK_COLS)


def _unpack(buf, shapes):
    flat = buf.reshape(-1)
    out, at = {}, 0
    for n in SMALL:
        size = int(np.prod(shapes[n]))
        out[n] = flat[at:at + size].reshape(shapes[n])
        at += size
    return out


class _Exchange:
    GROUPS = {"A": (("w_up", 0), ("w_down", 0)), "B": (("w_in", 1), ("w_out", 1)), "C": (("w_up", 1), ("w_down", 1))}
    RELAY_AT = {("pre_out", 0): "A", ("pre_mlp", 0): "B", ("mid", 1): "C"}
    NEXT_GROUP = {"A": "B", "B": "C"}
    LAST = ("mix", 0)

    def __init__(self, wts, k_me, kc_arr):
        self.wts, self.k_me, self.kc_arr = wts, k_me, kc_arr
        self.own = {(n, l): wts[n][l].astype(bf16) for n in BIG for l in range(DEPTH)}
        now = gather_now([self.own["w_in", 0], self.own["w_out", 0]], wts["conv_w"])
        self.ready = {("w_in", 0): self._fill(now[0], self.own["w_in", 0]),
                      ("w_out", 0): self._fill(now[1], self.own["w_out", 0])}
        conv = self._fill(now[2], wts["conv_w"])
        self.conv_w = jnp.transpose(conv, (1, 2, 0, 3)).reshape(DEPTH, CONV_WIDTH, D_CONV)
        self.ici, self.relay = {}, {}
        self.gview, self.scatter, self.share, self.reduced = {}, [], [], {}
        self._start_ici("A", now[2])

    def _start_ici(self, g, after):
        srcs = [self.own[p] for p in self.GROUPS[g]]
        lands = [lax.empty((N_CHIPS,) + s.shape, s.dtype) for s in srcs]
        self.ici[g] = split_start("gather%s_ici_start" % g, _gather_maker("ici", len(srcs)), 3 * len(srcs), srcs + lands,
                                  after)
        return self.ici[g]["token"]

    def _fill(self, land, own):
        return lax.dynamic_update_slice(land, own[None], (self.k_me,) + (0,) * own.ndim)

    def stage(self, name, after):
        if name == ("begin", 0):
            return self.ici["A"]["token"]
        g = self.RELAY_AT.get(name)
        if g is None:
            return 0.0
        n = len(self.GROUPS[g])
        lands = split_wait(self.ici[g], after)[n:]
        self.relay[g] = split_start("gather%s_relay_start" % g, _gather_maker("relay", 0), 3 * n, lands, after)
        tok = self.relay[g]["token"]
        if g in self.NEXT_GROUP:
            tok = tok + self._start_ici(self.NEXT_GROUP[g], after)
        return tok

    def _get(self, piece, after):
        if piece not in self.ready:
            g = [k for k, pieces in self.GROUPS.items() if piece in pieces][0]
            lands = split_wait(self.relay[g], after)
            for p, land in zip(self.GROUPS[g], lands):
                self.ready[p] = self._fill(land, self.own[p])
        return self.ready[piece]

    def w_in(self, l, after):
        return align_w_in(self._get(("w_in", l), after))

    def w_out(self, l, after):
        return self._get(("w_out", l), after).reshape(D_MODEL, D_MODEL)

    def mlp(self, l, after):
        return self._get(("w_up", l), after), self._get(("w_down", l), after)

    def _view(self, n, g):
        _, rows, cols = self.wts[n].shape
        return g.reshape(N_CHIPS, 2, rows // 2, cols)

    def grads(self, name, arrays, after):
        if name == self.LAST:
            self.held = (name, arrays)
            return self._advance(after, 0)
        return self._scatter(name, arrays, after) + self._advance(after, 1)

    def flush(self, after):
        return self._scatter(*self.held, after)

    def _scatter(self, name, arrays, after):
        pieces = [(n, name[1]) for n in arrays]
        views = [self._view(n, g) for n, (g, _) in arrays.items()]
        sends = [g16.reshape(v.shape) for v, (_, g16) in zip(views, arrays.values())]
        self.gview.update(zip(pieces, views))
        lands = [lax.empty((7,) + v.shape[2:], bf16) for v in views]
        h = split_start("scatter_%s%d_start" % name, _scatter_maker(len(views)), 7 * len(views), sends + lands, after)
        self.scatter.append((pieces, h))
        return h["token"]

    def _take_share(self, after):
        pieces, h = self.share.pop(0)
        self.reduced.update(zip(pieces, split_wait(h, after)))

    def _take_scatter(self, after):
        pieces, h = self.scatter.pop(0)
        lands = split_wait(h, after)[len(pieces):]
        sums = [piece_sum(self.gview[p], land, self.kc_arr) for p, land in zip(pieces, lands)]
        hs = split_start(h["name"].replace("scatter", "share"), _share_maker(len(sums)), len(sums), sums, after)
        self.share.append((pieces, hs))
        return hs["token"]

    def _advance(self, after, newest):
        if self.share:
            self._take_share(after)
        return self._take_scatter(after) if len(self.scatter) > newest else 0.0

    def reduced_grads(self, names, after):
        want = [(n, l) for n in names for l in range(DEPTH)]
        while not all(p in self.reduced for p in want):
            if any(p in pieces for p in want for pieces, _ in self.share):
                self._take_share(after)
            else:
                self._take_scatter(after)
        return {n: [self.reduced[n, l] for l in range(DEPTH)] for n in names}


def kernel(x, mix_norm_g, w_in, q_gain, k_gain, sinks, rel_bias, conv_w, conv_b, dt_bias, a_log, d_skip, ssm_norm_g, w_out, mlp_norm_g, w_up, w_down, loss_target, m_mix_norm_g, m_w_in, m_q_gain, m_k_gain, m_sinks, m_rel_bias, m_conv_w, m_conv_b, m_dt_bias, m_a_log, m_d_skip, m_ssm_norm_g, m_w_out, m_mlp_norm_g, m_w_up, m_w_down, v_mix_norm_g, v_w_in, v_q_gain, v_k_gain, v_sinks, v_rel_bias, v_conv_w, v_conv_b, v_dt_bias, v_a_log, v_d_skip, v_ssm_norm_g, v_w_out, v_mlp_norm_g, v_w_up, v_w_down):
    wts = dict(mix_norm_g=mix_norm_g, w_in=w_in, q_gain=q_gain, k_gain=k_gain, sinks=sinks, rel_bias=rel_bias, conv_w=conv_w,
               conv_b=conv_b, dt_bias=dt_bias, a_log=a_log, d_skip=d_skip, ssm_norm_g=ssm_norm_g, w_out=w_out,
               mlp_norm_g=mlp_norm_g, w_up=w_up, w_down=w_down)
    mom = dict(mix_norm_g=m_mix_norm_g, w_in=m_w_in, q_gain=m_q_gain, k_gain=m_k_gain, sinks=m_sinks, rel_bias=m_rel_bias,
               conv_w=m_conv_w, conv_b=m_conv_b, dt_bias=m_dt_bias, a_log=m_a_log, d_skip=m_d_skip, ssm_norm_g=m_ssm_norm_g,
               w_out=m_w_out, mlp_norm_g=m_mlp_norm_g, w_up=m_w_up, w_down=m_w_down)
    var = dict(mix_norm_g=v_mix_norm_g, w_in=v_w_in, q_gain=v_q_gain, k_gain=v_k_gain, sinks=v_sinks, rel_bias=v_rel_bias,
               conv_w=v_conv_w, conv_b=v_conv_b, dt_bias=v_dt_bias, a_log=v_a_log, d_skip=v_d_skip, ssm_norm_g=v_ssm_norm_g,
               w_out=v_w_out, mlp_norm_g=v_mlp_norm_g, w_up=v_w_up, w_down=v_w_down)
    xi, yi, ci = _coords()
    k_me = 2 * xi + yi
    kc_arr = jnp.stack([k_me, ci]).astype(jnp.int32)

    prov = _Exchange(wts, k_me, kc_arr)
    small_w = {n: wts[n] for n in SMALL}
    small_w["conv_w"] = prov.conv_w
    loss, dx, grads, tok = local_step(x[0], loss_target[0], small_w, prov)
    loss = lax.psum(loss[0, 0], ("x", "y", "c"))

    small_shapes = {n: grads[n].shape for n in SMALL}
    small_sum = small_all_reduce(_pack(grads) + tok)
    tok = prov.flush(small_sum)
    small = _unpack(small_sum, small_shapes)
    cols = conv_w.shape[-1]
    small["conv_w"] = lax.dynamic_slice_in_dim(small["conv_w"], k_me * cols, cols, axis=2)
    g_out_d, d_out_d, m_out_d, v_out_d = {}, {}, {}, {}
    shard_shapes = {n: wts[n].shape for n in SMALL}
    d, nm, nv = adamw_small(_pack(wts), _pack(small) + tok, _pack(mom), _pack(var))
    for dst, buf in ((d_out_d, d), (m_out_d, nm), (v_out_d, nv)):
        dst.update(_unpack(buf, shard_shapes))
    g_out_d.update(small)

    after = d
    for names in (("w_up", "w_down"), ("w_in", "w_out")):
        for n, (g0, g1) in prov.reduced_grads(names, after).items():
            g_out_d[n], d_out_d[n], m_out_d[n], v_out_d[n] = adamw_shard(wts[n], g0, g1, mom[n], var[n])
            after = d_out_d[n]

    return (loss, dx[None], *[g_out_d[n] for n in WEIGHTS], *[d_out_d[n] for n in WEIGHTS],
            *[m_out_d[n] for n in WEIGHTS], *[v_out_d[n] for n in WEIGHTS])
```

```python
import functools

import numpy as np
import jax
import jax.numpy as jnp
from jax import lax
from jax.experimental import pallas as pl
from jax.experimental.pallas import tpu as pltpu

f32 = jnp.float32
bf16 = jnp.bfloat16

SEQ = 2048
D_MODEL = 1024
DEPTH = 2
HEAD_DIM = 64
N_Q_HEADS = 8
N_KV_HEADS = 2
Q_PER_KV = N_Q_HEADS // N_KV_HEADS
BLOCK = 128
N_BLOCKS = SEQ // BLOCK
N_BUCKETS = 32
MAX_DISTANCE = 128
SSM_HEADS = 8
SSM_HEAD_DIM = 64
SSM_GROUPS = 2
HEADS_PER_GROUP = SSM_HEADS // SSM_GROUPS
SSM_STATE = 128
CONV_WIDTH = 4
CHUNK = 128
N_CHUNKS = SEQ // CHUNK
D_FF = 4 * D_MODEL
D_ATTN = N_Q_HEADS * HEAD_DIM
D_KV = N_KV_HEADS * HEAD_DIM
D_SSM = SSM_HEADS * SSM_HEAD_DIM
D_BC = SSM_GROUPS * SSM_STATE
D_CONV = D_SSM + 2 * D_BC
D_IN = D_ATTN + 2 * D_KV + D_SSM + D_CONV + SSM_HEADS
EPS = 1e-6
NEG = -1e30
N_CHIPS = 4
FF_TILE = D_FF // N_CHIPS

LANE = 128
PW = D_ATTN + D_SSM + D_CONV + 2 * D_KV + LANE
OFF_Q, OFF_Z, OFF_X, OFF_K, OFF_V, OFF_DT = 0, 512, 1024, 2048, 2176, 2304

ADAM_LR = 0.001
ADAM_B1 = 0.9
ADAM_B2 = 0.999
ADAM_EPS = 1e-08
ADAM_WD = 0.01
ADAM_STEP = 10

VMEM_LIMIT = 56 * 1024 * 1024


def _params(*sem):
    return pltpu.CompilerParams(dimension_semantics=tuple(sem), vmem_limit_bytes=VMEM_LIMIT)


def _bdot(a, b):
    return jnp.dot(a.astype(bf16), b.astype(bf16), preferred_element_type=f32)


def _bdot_nt(a, b):
    return lax.dot_general(a.astype(bf16), b.astype(bf16), (((1,), (1,)), ((), ())), preferred_element_type=f32)


def _bdot_tn(a, b):
    return lax.dot_general(a.astype(bf16), b.astype(bf16), (((0,), (0,)), ((), ())), preferred_element_type=f32)


def _hdot(a, b):
    return jnp.dot(a, b, precision=lax.Precision.HIGHEST, preferred_element_type=f32)


def _sigmoid(x):
    return 1.0 / (1.0 + jnp.exp(-x))


def _softplus(x):
    return jnp.maximum(x, 0.0) + jnp.log1p(jnp.exp(-jnp.abs(x)))


def _rms(x):
    return lax.rsqrt(jnp.mean(x * x, axis=-1, keepdims=True) + EPS)


def _rms_bwd(dy, xhat, r, g):
    t = dy * g
    return r * (t - xhat * jnp.mean(t * xhat, axis=-1, keepdims=True))


def _full(shape):
    return pl.BlockSpec(shape, lambda *_: (0,) * len(shape))


def _bucket_table():
    qi = np.arange(BLOCK)[:, None]
    kj = np.arange(2 * BLOCK)[None, :]
    dist = qi + BLOCK - kj
    ok = (dist >= 0) & (dist < 128)
    d = np.clip(dist, 0, None)
    max_exact = N_BUCKETS // 2
    d_f = np.maximum(d, 1).astype(np.float32)
    large = max_exact + (np.log(d_f / np.float32(max_exact)) / np.float32(np.log(MAX_DISTANCE / max_exact))
                         * np.float32(N_BUCKETS - max_exact)).astype(np.int32)
    large = np.minimum(large, N_BUCKETS - 1)
    bucket = np.where(d < max_exact, d, large)
    return np.where(ok, bucket, -1).astype(np.int32)


def bias_build(rel_bias, bucket):
    def body(rel_ref, bkt_ref, o_ref):
        bkt = bkt_ref[...]
        for h in range(N_Q_HEADS):
            acc = jnp.where(bkt < 0, NEG, 0.0).astype(f32)
            for b in range(N_BUCKETS):
                acc = acc + jnp.where(bkt == b, rel_ref[b, h], 0.0)
            o_ref[h] = acc

    return pl.pallas_call(
        body, name="bias_build", out_shape=jax.ShapeDtypeStruct((N_Q_HEADS,) + bucket.shape, f32),
        in_specs=[pl.BlockSpec(memory_space=pltpu.SMEM), pl.BlockSpec(memory_space=pltpu.VMEM)],
        out_specs=pl.BlockSpec(memory_space=pltpu.VMEM),
    )(rel_bias, bucket)


def bias_bwd(dband0, dband1, bucket):
    def body(d0_ref, d1_ref, bkt_ref, o_ref):
        bkt = bkt_ref[...]
        o_ref[...] = jnp.zeros_like(o_ref)
        for h in range(N_Q_HEADS):
            d = d0_ref[h] + d1_ref[h]
            for b in range(N_BUCKETS):
                part = jnp.sum(jnp.where(bkt == b, d, 0.0), axis=1, keepdims=True)
                o_ref[b:b + 1, h:h + 1] = jnp.sum(part, axis=0, keepdims=True)

    return pl.pallas_call(
        body, name="bias_bwd", out_shape=jax.ShapeDtypeStruct((N_BUCKETS, LANE), f32),
    )(dband0, dband1, bucket)


W_IN_SHARD = D_IN // N_CHIPS
_ALIGNED_PIECES = ((0, 0, 512), (1, 190, 578), (2, 0, 124), (2, 124, 578), (3, 0, 570), (0, 512, 578), (1, 0, 62),
                   (1, 62, 190), (3, 570, 578))
_SHARD_PIECES = (((0, 512), (2048, 2114)), ((2114, 2176), (2176, 2304), (512, 900)), ((900, 1024), (1024, 1478)),
                 ((1478, 2048), (2304, 2312)))


def align_w_in(shards, tr=256):
    def body(s_ref, o_ref):
        parts = [s_ref[k, :, a:b] for k, a, b in _ALIGNED_PIECES]
        parts.append(jnp.zeros((tr, LANE - SSM_HEADS), s_ref.dtype))
        o_ref[...] = jnp.concatenate(parts, axis=-1)

    return pl.pallas_call(
        body, name="align_w_in", grid=(D_MODEL // tr,),
        in_specs=[pl.BlockSpec((N_CHIPS, tr, W_IN_SHARD), lambda i: (0, i, 0))],
        out_specs=pl.BlockSpec((tr, PW), lambda i: (i, 0)),
        out_shape=jax.ShapeDtypeStruct((D_MODEL, PW), shards.dtype),
        compiler_params=_params("arbitrary"),
    )(shards)


def split_w_in_grad(dw, tr=256):
    def body(d_ref, o_ref, o16_ref):
        for k, pieces in enumerate(_SHARD_PIECES):
            part = jnp.concatenate([d_ref[:, a:b] for a, b in pieces], axis=-1)
            o_ref[k] = part
            o16_ref[k] = part.astype(bf16)

    spec = pl.BlockSpec((N_CHIPS, tr, W_IN_SHARD), lambda i: (0, i, 0))
    return pl.pallas_call(
        body, name="split_w_in_grad", grid=(D_MODEL // tr,),
        in_specs=[pl.BlockSpec((tr, PW), lambda i: (i, 0))], out_specs=[spec, spec],
        out_shape=[jax.ShapeDtypeStruct((N_CHIPS, D_MODEL, W_IN_SHARD), f32),
                   jax.ShapeDtypeStruct((N_CHIPS, D_MODEL, W_IN_SHARD), bf16)],
        compiler_params=_params("arbitrary"),
    )(dw)

def in_fwd(x, g, w, tm=256):
    def body(x_ref, g_ref, w_ref, o_ref):
        xv = x_ref[...]
        h = xv * _rms(xv) * g_ref[...]
        o_ref[...] = _bdot(h, w_ref[...])

    return pl.pallas_call(
        body, name="in_fwd", grid=(SEQ // tm,),
        in_specs=[pl.BlockSpec((tm, D_MODEL), lambda i: (i, 0)), _full((1, D_MODEL)), _full((D_MODEL, PW))],
        out_specs=pl.BlockSpec((tm, PW), lambda i: (i, 0)),
        out_shape=jax.ShapeDtypeStruct((SEQ, PW), f32),
        compiler_params=_params("arbitrary"),
    )(x, g, w)


def in_bwd(dq, dz, dxbc, dk, dv, ddt, x, g, w, dres, tm=256):
    def body(dq_ref, dz_ref, dx_ref, dk_ref, dv_ref, ddt_ref, x_ref, g_ref, w_ref, dres_ref, o_ref, dw_ref, dg_ref):
        i = pl.program_id(0)

        @pl.when(i == 0)
        def _():
            dw_ref[...] = jnp.zeros_like(dw_ref)
            dg_ref[...] = jnp.zeros_like(dg_ref)

        dproj = jnp.concatenate([dq_ref[...], dz_ref[...], dx_ref[...], dk_ref[...], dv_ref[...], ddt_ref[...]],
                                axis=-1).astype(bf16)
        xv = x_ref[...]
        r = _rms(xv)
        xhat = xv * r
        gv = g_ref[...]
        h = xhat * gv
        dw_ref[...] += _bdot_tn(h, dproj)
        dh = _bdot_nt(dproj, w_ref[...])
        dg_ref[...] += jnp.sum(dh * xhat, axis=0, keepdims=True)
        o_ref[...] = dres_ref[...] + _rms_bwd(dh, xhat, r, gv)

    tok = lambda w_: pl.BlockSpec((tm, w_), lambda i: (i, 0))
    return pl.pallas_call(
        body, name="in_bwd", grid=(SEQ // tm,),
        in_specs=[tok(D_ATTN), tok(D_SSM), tok(D_CONV), tok(D_KV // 1), tok(D_KV // 1), tok(LANE), tok(D_MODEL),
                  _full((1, D_MODEL)), _full((D_MODEL, PW)), tok(D_MODEL)],
        out_specs=[tok(D_MODEL), _full((D_MODEL, PW)), _full((1, D_MODEL))],
        out_shape=[jax.ShapeDtypeStruct((SEQ, D_MODEL), f32), jax.ShapeDtypeStruct((D_MODEL, PW), f32),
                   jax.ShapeDtypeStruct((1, D_MODEL), f32)],
        compiler_params=_params("arbitrary"),
    )(dq, dz, dxbc, dk, dv, ddt, x, g, w, dres)


def _attn_softmax_t(qk, bias_t, sink, first, key_row):
    s = qk * (HEAD_DIM ** -0.5) + bias_t
    s = jnp.where(jnp.logical_and(first, key_row < BLOCK), NEG, s)
    m = jnp.maximum(jnp.max(s, axis=0, keepdims=True), sink)
    p = jnp.exp(s - m)
    psink = jnp.exp(sink - m)
    inv = 1.0 / (jnp.sum(p, axis=0, keepdims=True) + psink)
    return p * inv, psink * inv


def _rms_t(x_t):
    return lax.rsqrt(jnp.mean(x_t * x_t, axis=0, keepdims=True) + EPS)


def attn_fwd_t(proj, q_gain_col, k_gain, sinks, bias_t):
    kcol, vcol = OFF_K // D_KV, OFF_V // D_KV

    def body(q_ref, kc_ref, kp_ref, vc_ref, vp_ref, qg_ref, kg_ref, sink_ref, bias_ref, o_ref, ot_scr):
        n = pl.program_id(0)
        first = n == 0
        key_row = lax.broadcasted_iota(jnp.int32, (2 * BLOCK, BLOCK), 0)
        k2 = jnp.concatenate([kp_ref[...], kc_ref[...]], axis=0)
        v_t = jnp.concatenate([vp_ref[...], vc_ref[...]], axis=0).T
        q_t = q_ref[...].T
        qg = jnp.broadcast_to(qg_ref[...], (HEAD_DIM, BLOCK))
        kg = kg_ref[...]
        for hk in range(N_KV_HEADS):
            sl = slice(hk * HEAD_DIM, (hk + 1) * HEAD_DIM)
            kk = k2[:, sl]
            kn = (kk * _rms(kk) * kg).astype(bf16)
            vt = v_t[sl, :].astype(bf16)
            heads = range(hk * Q_PER_KV, (hk + 1) * Q_PER_KV)
            qns = []
            for h in heads:
                qh = q_t[h * HEAD_DIM:(h + 1) * HEAD_DIM, :]
                qns.append(qh * _rms_t(qh) * qg)
            scores = [_bdot(kn, qn) for qn in qns]
            for h, s in zip(heads, scores):
                p, _ = _attn_softmax_t(s, bias_ref[h], sink_ref[h], first, key_row)
                ot_scr[h * HEAD_DIM:(h + 1) * HEAD_DIM, :] = _bdot(vt, p)
        o_ref[...] = ot_scr[...].T

    prev = lambda n: jnp.maximum(n - 1, 0)
    return pl.pallas_call(
        body, name="attn_fwd", grid=(N_BLOCKS,),
        in_specs=[pl.BlockSpec((BLOCK, D_ATTN), lambda n: (n, 0)),
                  pl.BlockSpec((BLOCK, D_KV), lambda n: (n, kcol)), pl.BlockSpec((BLOCK, D_KV), lambda n: (prev(n), kcol)),
                  pl.BlockSpec((BLOCK, D_KV), lambda n: (n, vcol)), pl.BlockSpec((BLOCK, D_KV), lambda n: (prev(n), vcol)),
                  _full((HEAD_DIM, 1)), _full((1, HEAD_DIM)), pl.BlockSpec(memory_space=pltpu.SMEM),
                  _full((N_Q_HEADS, 2 * BLOCK, BLOCK))],
        out_specs=pl.BlockSpec((BLOCK, D_ATTN), lambda n: (n, 0)),
        out_shape=jax.ShapeDtypeStruct((SEQ, D_ATTN), f32),
        scratch_shapes=[pltpu.VMEM((D_ATTN, BLOCK), f32)],
        compiler_params=_params("arbitrary"),
    )(proj, proj, proj, proj, proj, q_gain_col, k_gain, sinks, bias_t)


def attn_bwd_t(proj, d_out, q_gain_col, k_gain, sinks, bias_t):
    kcol, vcol = OFF_K // D_KV, OFF_V // D_KV

    def body(q_ref, kc_ref, kp_ref, vc_ref, vp_ref, do_ref, qg_ref, kg_ref, sink_ref, bias_ref,
             dq_ref, dk_ref, dv_ref, dband_ref, dsink_ref, dqg_ref, dkg_ref, dkn_scr, dv_scr, dqt_scr, dsink_acc, dqg_acc):
        i = pl.program_id(0)
        first = i == N_BLOCKS - 1

        @pl.when(i == 0)
        def _():
            for ref in (dband_ref, dkg_ref, dkn_scr, dv_scr, dsink_acc, dqg_acc):
                ref[...] = jnp.zeros_like(ref)

        key_row = lax.broadcasted_iota(jnp.int32, (2 * BLOCK, BLOCK), 0)
        k2 = jnp.concatenate([kp_ref[...], kc_ref[...]], axis=0)
        v2 = jnp.concatenate([vp_ref[...], vc_ref[...]], axis=0)
        q_t = q_ref[...].T
        do_t = do_ref[...].T
        qg = jnp.broadcast_to(qg_ref[...], (HEAD_DIM, BLOCK))
        kg = kg_ref[...]
        scale = HEAD_DIM ** -0.5
        for hk in range(N_KV_HEADS):
            sl = slice(hk * HEAD_DIM, (hk + 1) * HEAD_DIM)
            kk = k2[:, sl]
            rk = _rms(kk)
            khat = kk * rk
            kn = (khat * kg).astype(bf16)
            vb = v2[:, sl].astype(bf16)
            dkn = jnp.zeros((2 * BLOCK, HEAD_DIM), f32)
            dvv = jnp.zeros((2 * BLOCK, HEAD_DIM), f32)
            heads = range(hk * Q_PER_KV, (hk + 1) * Q_PER_KV)
            rqs, qhats, qns, d_os = [], [], [], []
            for h in heads:
                hs = slice(h * HEAD_DIM, (h + 1) * HEAD_DIM)
                qh = q_t[hs, :]
                rqs.append(_rms_t(qh))
                qhats.append(qh * rqs[-1])
                qns.append((qhats[-1] * qg).astype(bf16))
                d_os.append(do_t[hs, :].astype(bf16))
            scores = [_bdot(kn, qn) for qn in qns]
            dps = [_bdot(vb, d_o) for d_o in d_os]
            ps, dss = [], []
            for h, s, dp in zip(heads, scores, dps):
                p, psink = _attn_softmax_t(s, bias_ref[h], sink_ref[h], first, key_row)
                delta = jnp.sum(p * dp, axis=0, keepdims=True)
                ds = p * (dp - delta)
                dband_ref[h] += ds
                dsink_acc[h:h + 1, :] += -(psink * delta)
                ps.append(p.astype(bf16))
                dss.append(ds.astype(bf16))
            dqns = [_bdot_tn(kn, ds) * scale for ds in dss]
            for ds, qn, p, d_o in zip(dss, qns, ps, d_os):
                dkn = dkn + _bdot_nt(ds, qn) * scale
                dvv = dvv + _bdot_nt(p, d_o)
            for h, dqn, rq, qhat in zip(heads, dqns, rqs, qhats):
                dqg_acc[...] += dqn * qhat
                t = dqn * qg
                dqt_scr[h * HEAD_DIM:(h + 1) * HEAD_DIM, :] = rq * (t - qhat * jnp.mean(t * qhat, axis=0, keepdims=True))
            dkn_cur = dkn[BLOCK:] + dkn_scr[:, sl]
            dkn_scr[:, sl] = dkn[:BLOCK]
            khat_c, rk_c = khat[BLOCK:], rk[BLOCK:]
            dkg_ref[...] += jnp.sum(dkn_cur * khat_c, axis=0, keepdims=True)
            dk_ref[:, sl] = _rms_bwd(dkn_cur, khat_c, rk_c, kg)
            dv_ref[:, sl] = dvv[BLOCK:] + dv_scr[:, sl]
            dv_scr[:, sl] = dvv[:BLOCK]
        dq_ref[...] = dqt_scr[...].T

        @pl.when(i == N_BLOCKS - 1)
        def _():
            dsink_ref[...] = jnp.sum(dsink_acc[...], axis=1, keepdims=True)
            dqg_ref[...] = jnp.sum(dqg_acc[...], axis=1, keepdims=True)

    blk = lambda i: N_BLOCKS - 1 - i
    prev = lambda i: jnp.maximum(N_BLOCKS - 2 - i, 0)
    return pl.pallas_call(
        body, name="attn_bwd", grid=(N_BLOCKS,),
        in_specs=[pl.BlockSpec((BLOCK, D_ATTN), lambda i: (blk(i), 0)),
                  pl.BlockSpec((BLOCK, D_KV), lambda i: (blk(i), kcol)), pl.BlockSpec((BLOCK, D_KV), lambda i: (prev(i), kcol)),
                  pl.BlockSpec((BLOCK, D_KV), lambda i: (blk(i), vcol)), pl.BlockSpec((BLOCK, D_KV), lambda i: (prev(i), vcol)),
                  pl.BlockSpec((BLOCK, D_ATTN), lambda i: (blk(i), 0)),
                  _full((HEAD_DIM, 1)), _full((1, HEAD_DIM)), pl.BlockSpec(memory_space=pltpu.SMEM),
                  _full((N_Q_HEADS, 2 * BLOCK, BLOCK))],
        out_specs=[pl.BlockSpec((BLOCK, D_ATTN), lambda i: (blk(i), 0)), pl.BlockSpec((BLOCK, D_KV), lambda i: (blk(i), 0)),
                   pl.BlockSpec((BLOCK, D_KV), lambda i: (blk(i), 0)), _full((N_Q_HEADS, 2 * BLOCK, BLOCK)),
                   _full((N_Q_HEADS, 1)), _full((HEAD_DIM, 1)), _full((1, HEAD_DIM))],
        out_shape=[jax.ShapeDtypeStruct((SEQ, D_ATTN), f32), jax.ShapeDtypeStruct((SEQ, D_KV), f32),
                   jax.ShapeDtypeStruct((SEQ, D_KV), f32), jax.ShapeDtypeStruct((N_Q_HEADS, 2 * BLOCK, BLOCK), f32),
                   jax.ShapeDtypeStruct((N_Q_HEADS, 1), f32), jax.ShapeDtypeStruct((HEAD_DIM, 1), f32),
                   jax.ShapeDtypeStruct((1, HEAD_DIM), f32)],
        scratch_shapes=[pltpu.VMEM((BLOCK, D_KV), f32), pltpu.VMEM((BLOCK, D_KV), f32), pltpu.VMEM((D_ATTN, BLOCK), f32),
                        pltpu.VMEM((N_Q_HEADS, BLOCK), f32), pltpu.VMEM((HEAD_DIM, BLOCK), f32)],
        compiler_params=_params("arbitrary"),
    )(proj, proj, proj, proj, proj, d_out, q_gain_col, k_gain, sinks, bias_t)


def _shift_down(u, s, row):
    if s == 0:
        return u
    return jnp.where(row >= s, pltpu.roll(u, s, 0), 0.0)


def _shift_up(u, s, row):
    if s == 0:
        return u
    return jnp.where(row < SEQ - s, pltpu.roll(u, SEQ - s, 0), 0.0)


def conv_fwd(proj, conv_w, conv_b):
    xcol = OFF_X // LANE

    def body(u_ref, w_ref, b_ref, o_ref):
        u = u_ref[...]
        row = lax.broadcasted_iota(jnp.int32, u.shape, 0)
        pre = b_ref[...] + jnp.zeros_like(u)
        for k in range(CONV_WIDTH):
            pre = pre + w_ref[k:k + 1, :] * _shift_down(u, CONV_WIDTH - 1 - k, row)
        o_ref[...] = pre * _sigmoid(pre)

    return pl.pallas_call(
        body, name="conv_fwd", grid=(D_CONV // LANE,),
        in_specs=[pl.BlockSpec((SEQ, LANE), lambda j: (0, xcol + j)), pl.BlockSpec((CONV_WIDTH, LANE), lambda j: (0, j)),
                  pl.BlockSpec((1, LANE), lambda j: (0, j))],
        out_specs=pl.BlockSpec((SEQ, LANE), lambda j: (0, j)),
        out_shape=jax.ShapeDtypeStruct((SEQ, D_CONV), f32),
        compiler_params=_params("arbitrary"),
    )(proj, conv_w, conv_b)


def conv_bwd(proj, d_act, conv_w, conv_b):
    xcol = OFF_X // LANE

    def body(u_ref, da_ref, w_ref, b_ref, du_ref, dw_ref, db_ref):
        u = u_ref[...]
        row = lax.broadcasted_iota(jnp.int32, u.shape, 0)
        shifted = [_shift_down(u, CONV_WIDTH - 1 - k, row) for k in range(CONV_WIDTH)]
        pre = b_ref[...] + jnp.zeros_like(u)
        for k in range(CONV_WIDTH):
            pre = pre + w_ref[k:k + 1, :] * shifted[k]
        sg = _sigmoid(pre)
        dpre = da_ref[...] * (sg * (1.0 + pre * (1.0 - sg)))
        db_ref[...] = jnp.sum(dpre, axis=0, keepdims=True)
        du = jnp.zeros_like(u)
        for k in range(CONV_WIDTH):
            dw_ref[k:k + 1, :] = jnp.sum(dpre * shifted[k], axis=0, keepdims=True)
            du = du + w_ref[k:k + 1, :] * _shift_up(dpre, CONV_WIDTH - 1 - k, row)
        du_ref[...] = du

    return pl.pallas_call(
        body, name="conv_bwd", grid=(D_CONV // LANE,),
        in_specs=[pl.BlockSpec((SEQ, LANE), lambda j: (0, xcol + j)), pl.BlockSpec((SEQ, LANE), lambda j: (0, j)),
                  pl.BlockSpec((CONV_WIDTH, LANE), lambda j: (0, j)), pl.BlockSpec((1, LANE), lambda j: (0, j))],
        out_specs=[pl.BlockSpec((SEQ, LANE), lambda j: (0, j)), pl.BlockSpec((CONV_WIDTH, LANE), lambda j: (0, j)),
                   pl.BlockSpec((1, LANE), lambda j: (0, j))],
        out_shape=[jax.ShapeDtypeStruct((SEQ, D_CONV), f32), jax.ShapeDtypeStruct((CONV_WIDTH, D_CONV), f32),
                   jax.ShapeDtypeStruct((1, D_CONV), f32)],
        compiler_params=_params("arbitrary"),
    )(proj, d_act, conv_w, conv_b)


def _ssd_chunk_common(dt_raw, dtb, alog):
    row = lax.broadcasted_iota(jnp.int32, (CHUNK, CHUNK), 0)
    col = lax.broadcasted_iota(jnp.int32, (CHUNK, CHUNK), 1)
    tri = (row >= col).astype(f32)
    strict = (row > col).astype(f32)
    dtp = _softplus(dt_raw + dtb)
    a_row = -jnp.exp(alog)
    d_a = dtp * a_row
    cs = _hdot(tri, d_a)
    cs_last = cs[CHUNK - 1:CHUNK, :]
    return row, col, dtp, a_row, cs, cs.T, cs_last


def _seg_decay(cs, cs_t, hd, row, col):
    seg = cs[:, hd:hd + 1] - cs_t[hd:hd + 1, :]
    return jnp.where(row >= col, jnp.exp(seg), 0.0)


GROUP_W = HEADS_PER_GROUP * SSM_HEAD_DIM


def _group_indicator(g):
    j = lax.broadcasted_iota(jnp.int32, (GROUP_W, LANE), 0)
    lane = lax.broadcasted_iota(jnp.int32, (GROUP_W, LANE), 1)
    return (lane == g * HEADS_PER_GROUP + j // SSM_HEAD_DIM).astype(f32)


def _hdot_nt(a, b):
    return lax.dot_general(a, b, (((1,), (1,)), ((), ())), precision=lax.Precision.HIGHEST, preferred_element_type=f32)


def ssd_fwd_g(act, proj, dt_bias, a_log, d_skip, norm_g):
    zcol, dtcol = OFF_Z // D_SSM, OFF_DT // LANE

    def body(act_ref, z_ref, dt_ref, dtb_ref, alog_ref, dsk_ref, ng_ref, out_ref, ypre_ref, st_ref, state):
        c = pl.program_id(0)

        @pl.when(c == 0)
        def _():
            state[...] = jnp.zeros_like(state)

        row, col, dtp, a_row, cs, cs_t, cs_last = _ssd_chunk_common(dt_ref[...], dtb_ref[...], alog_ref[...])
        e_cs = jnp.exp(cs)
        dte = jnp.exp(cs_last - cs)
        rows8 = jnp.concatenate([jnp.exp(cs_last), dsk_ref[...], jnp.zeros((6, LANE), f32)], axis=0)
        z = z_ref[...]
        sz = z * _sigmoid(z)
        ng = ng_ref[...]
        for g in range(SSM_GROUPS):
            gs = slice(g * GROUP_W, (g + 1) * GROUP_W)
            ind = _group_indicator(g)
            xg = act_ref[:, gs]
            bg = act_ref[:, D_SSM + g * SSM_STATE:D_SSM + (g + 1) * SSM_STATE]
            cg = act_ref[:, D_SSM + D_BC + g * SSM_STATE:D_SSM + D_BC + (g + 1) * SSM_STATE]
            dt_e, e_e, dte_e = _hdot_nt(dtp, ind), _hdot_nt(e_cs, ind), _hdot_nt(dte, ind)
            rows_e = _hdot_nt(rows8, ind)
            ecl_e, dsk_e = rows_e[0:1], rows_e[1:2]
            xdt = xg * dt_e
            prev = state[g]
            st_ref[0, g] = prev
            cb = _bdot_nt(cg, bg)
            goff = _bdot(cg, prev)
            snew = _bdot_tn(bg, xdt * dte_e)
            heads = range(g * HEADS_PER_GROUP, (g + 1) * HEADS_PER_GROUP)
            ms = [cb * _seg_decay(cs, cs_t, hd, row, col) for hd in heads]
            yd = [_bdot(m, xdt[:, r * SSM_HEAD_DIM:(r + 1) * SSM_HEAD_DIM]) for r, m in enumerate(ms)]
            y = jnp.concatenate(yd, axis=1) + e_e * goff + xg * dsk_e
            state[g] = prev * ecl_e + snew
            ypre_ref[:, gs] = y
            part = y * sz[:, gs]
            out_ref[:, gs] = part * _rms(part) * ng[:, gs]

    return pl.pallas_call(
        body, name="ssd_fwd", grid=(N_CHUNKS,),
        in_specs=[pl.BlockSpec((CHUNK, D_CONV), lambda c: (c, 0)), pl.BlockSpec((CHUNK, D_SSM), lambda c: (c, zcol)),
                  pl.BlockSpec((CHUNK, LANE), lambda c: (c, dtcol)), _full((1, LANE)), _full((1, LANE)), _full((1, LANE)),
                  _full((1, D_SSM))],
        out_specs=[pl.BlockSpec((CHUNK, D_SSM), lambda c: (c, 0)), pl.BlockSpec((CHUNK, D_SSM), lambda c: (c, 0)),
                   pl.BlockSpec((1, SSM_GROUPS, SSM_STATE, GROUP_W), lambda c: (c, 0, 0, 0))],
        out_shape=[jax.ShapeDtypeStruct((SEQ, D_SSM), f32), jax.ShapeDtypeStruct((SEQ, D_SSM), f32),
                   jax.ShapeDtypeStruct((N_CHUNKS, SSM_GROUPS, SSM_STATE, GROUP_W), f32)],
        scratch_shapes=[pltpu.VMEM((SSM_GROUPS, SSM_STATE, GROUP_W), f32)],
        compiler_params=_params("arbitrary"),
    )(act, proj, proj, dt_bias, a_log, d_skip, norm_g)


def ssd_bwd_g(act, proj, ypre, states, d_out, dt_bias, a_log, d_skip, norm_g):
    zcol, dtcol = OFF_Z // D_SSM, OFF_DT // LANE

    def body(act_ref, z_ref, dt_ref, ypre_ref, st_ref, do_ref, dtb_ref, alog_ref, dsk_ref, ng_ref,
             dact_ref, ddt_ref, dz_ref, dng_ref, dpar_ref, dstate):
        i = pl.program_id(0)

        @pl.when(i == 0)
        def _():
            for ref in (dng_ref, dpar_ref, dstate):
                ref[...] = jnp.zeros_like(ref)

        row, col, dtp, a_row, cs, cs_t, cs_last = _ssd_chunk_common(dt_ref[...], dtb_ref[...], alog_ref[...])
        upper = (row <= col).astype(f32)
        lane = lax.broadcasted_iota(jnp.int32, (CHUNK, LANE), 1)
        rowl = lax.broadcasted_iota(jnp.int32, (CHUNK, LANE), 0)
        e_cs = jnp.exp(cs)
        dte = jnp.exp(cs_last - cs)
        ecl = jnp.exp(cs_last)
        rows8 = jnp.concatenate([ecl, dsk_ref[...], jnp.zeros((6, LANE), f32)], axis=0)
        z = z_ref[...]
        sgz = _sigmoid(z)
        sz = z * sgz
        ng = ng_ref[...]
        ddt_mat = jnp.zeros((CHUNK, LANE), f32)
        dcs_mat = jnp.zeros((CHUNK, LANE), f32)
        dcs_t = jnp.zeros((LANE, CHUNK), f32)
        dcsl_row = jnp.zeros((1, LANE), f32)
        dd_row = jnp.zeros((1, LANE), f32)
        for g in range(SSM_GROUPS):
            gs = slice(g * GROUP_W, (g + 1) * GROUP_W)
            bsl = slice(D_SSM + g * SSM_STATE, D_SSM + (g + 1) * SSM_STATE)
            csl = slice(D_SSM + D_BC + g * SSM_STATE, D_SSM + D_BC + (g + 1) * SSM_STATE)
            ind = _group_indicator(g)
            y = ypre_ref[:, gs]
            part = y * sz[:, gs]
            r = _rms(part)
            yhat = part * r
            d_o = do_ref[:, gs]
            dng_ref[:, gs] += jnp.sum(d_o * yhat, axis=0, keepdims=True)
            dyz = _rms_bwd(d_o, yhat, r, ng[:, gs])
            dy = dyz * sz[:, gs]
            dz_ref[:, gs] = dyz * y * (sgz[:, gs] * (1.0 + z[:, gs] * (1.0 - sgz[:, gs])))

            xg = act_ref[:, gs]
            bg = act_ref[:, bsl]
            cg = act_ref[:, csl]
            dt_e, e_e, dte_e = _hdot_nt(dtp, ind), _hdot_nt(e_cs, ind), _hdot_nt(dte, ind)
            rows_e = _hdot_nt(rows8, ind)
            ecl_e, dsk_e = rows_e[0:1], rows_e[1:2]
            xdt = xg * dt_e
            prev = st_ref[0, g]
            dh = dstate[g]
            heads = range(g * HEADS_PER_GROUP, (g + 1) * HEADS_PER_GROUP)
            hsl = [slice(r_ * SSM_HEAD_DIM, (r_ + 1) * SSM_HEAD_DIM) for r_ in range(HEADS_PER_GROUP)]
            cb = _bdot_nt(cg, bg)
            lms = [_seg_decay(cs, cs_t, hd, row, col) for hd in heads]
            ms = [cb * lm for lm in lms]
            gmat = _bdot(cg, prev)
            dgm = dy * e_e
            dcg = _bdot_nt(dgm, prev)
            dprev = _bdot_tn(cg, dgm)
            dbg = _bdot_nt(xdt * dte_e, dh)
            dw = _bdot(bg, dh)
            dms = [_bdot_nt(dy[:, s_], xdt[:, s_]) for s_ in hsl]
            dxdts = [_bdot_tn(m, dy[:, s_]) for m, s_ in zip(ms, hsl)]
            dxdt = jnp.concatenate(dxdts, axis=1) + dw * dte_e
            dact_ref[:, gs] = dy * dsk_e + dxdt * dt_e
            dstate[g] = dprev + dh * ecl_e
            dcb = jnp.zeros((CHUNK, CHUNK), f32)
            for hd, dm, lm, m in zip(heads, dms, lms, ms):
                dcb = dcb + dm * lm
                dseg = dm * m
                dcs_mat = dcs_mat + jnp.where(lane == hd, jnp.sum(dseg, axis=1, keepdims=True), 0.0)
                dcs_t = jnp.where(row == hd, jnp.sum(dseg, axis=0, keepdims=True), dcs_t)
            dact_ref[:, bsl] = dbg + _bdot_tn(dcb, cg)
            dact_ref[:, csl] = dcg + _bdot(dcb, bg)
            ddte = _hdot(dw * xdt, ind) * dte
            dcs_mat = dcs_mat + _hdot(dy * gmat, ind) * e_cs - ddte
            ddt_mat = ddt_mat + _hdot(dxdt * xg, ind)
            dcsl_row = dcsl_row + jnp.sum(ddte, axis=0, keepdims=True) + jnp.sum(_hdot(dh * prev, ind), axis=0, keepdims=True) * ecl
            dd_row = dd_row + jnp.sum(_hdot(dy * xg, ind), axis=0, keepdims=True)
        dcs_mat = dcs_mat - dcs_t.T + jnp.where(rowl == CHUNK - 1, dcsl_row, 0.0)
        dda = _hdot(upper, dcs_mat)
        ddt_mat = ddt_mat + dda * a_row
        da_row = jnp.sum(dda * dtp, axis=0, keepdims=True)
        ddt_raw = ddt_mat * _sigmoid(dt_ref[...] + dtb_ref[...])
        ddt_ref[...] = ddt_raw
        dpar_ref[0:1, :] += jnp.sum(ddt_raw, axis=0, keepdims=True)
        dpar_ref[1:2, :] += da_row * a_row
        dpar_ref[2:3, :] += dd_row

    blk = lambda i: N_CHUNKS - 1 - i
    return pl.pallas_call(
        body, name="ssd_bwd", grid=(N_CHUNKS,),
        in_specs=[pl.BlockSpec((CHUNK, D_CONV), lambda i: (blk(i), 0)), pl.BlockSpec((CHUNK, D_SSM), lambda i: (blk(i), zcol)),
                  pl.BlockSpec((CHUNK, LANE), lambda i: (blk(i), dtcol)), pl.BlockSpec((CHUNK, D_SSM), lambda i: (blk(i), 0)),
                  pl.BlockSpec((1, SSM_GROUPS, SSM_STATE, GROUP_W), lambda i: (blk(i), 0, 0, 0)),
                  pl.BlockSpec((CHUNK, D_SSM), lambda i: (blk(i), 0)),
                  _full((1, LANE)), _full((1, LANE)), _full((1, LANE)), _full((1, D_SSM))],
        out_specs=[pl.BlockSpec((CHUNK, D_CONV), lambda i: (blk(i), 0)), pl.BlockSpec((CHUNK, LANE), lambda i: (blk(i), 0)),
                   pl.BlockSpec((CHUNK, D_SSM), lambda i: (blk(i), 0)), _full((1, D_SSM)), _full((8, LANE))],
        out_shape=[jax.ShapeDtypeStruct((SEQ, D_CONV), f32), jax.ShapeDtypeStruct((SEQ, LANE), f32),
                   jax.ShapeDtypeStruct((SEQ, D_SSM), f32), jax.ShapeDtypeStruct((1, D_SSM), f32),
                   jax.ShapeDtypeStruct((8, LANE), f32)],
        scratch_shapes=[pltpu.VMEM((SSM_GROUPS, SSM_STATE, GROUP_W), f32)],
        compiler_params=_params("arbitrary"),
    )(act, proj, proj, ypre, states, d_out, dt_bias, a_log, d_skip, norm_g)


def out_fwd(x, attn, ssm, w_out, tm=512):
    def body(x_ref, a_ref, s_ref, w_ref, o_ref):
        o_ref[...] = x_ref[...] + _bdot(a_ref[...], w_ref[:D_ATTN, :]) + _bdot(s_ref[...], w_ref[D_ATTN:, :])

    tok = lambda w_: pl.BlockSpec((tm, w_), lambda i: (i, 0))
    return pl.pallas_call(
        body, name="out_fwd", grid=(SEQ // tm,),
        in_specs=[tok(D_MODEL), tok(D_ATTN), tok(D_SSM), _full((D_MODEL, D_MODEL))],
        out_specs=tok(D_MODEL), out_shape=jax.ShapeDtypeStruct((SEQ, D_MODEL), f32),
        compiler_params=_params("arbitrary"),
    )(x, attn, ssm, w_out)


def out_bwd(dx1, attn, ssm, w_out, tm=512):
    nt = SEQ // tm

    def body(d_ref, a_ref, s_ref, w_ref, da_ref, ds_ref, dw_ref, dw16_ref):
        i = pl.program_id(0)

        @pl.when(i == 0)
        def _():
            dw_ref[...] = jnp.zeros_like(dw_ref)

        d = d_ref[...].astype(bf16)
        dcat = _bdot_nt(d, w_ref[...])
        da_ref[...] = dcat[:, :D_ATTN]
        ds_ref[...] = dcat[:, D_ATTN:]
        dw_ref[:D_ATTN, :] += _bdot_tn(a_ref[...], d)
        dw_ref[D_ATTN:, :] += _bdot_tn(s_ref[...], d)

        @pl.when(i == nt - 1)
        def _():
            dw16_ref[...] = dw_ref[...].astype(bf16)

    tok = lambda w_: pl.BlockSpec((tm, w_), lambda i: (i, 0))
    return pl.pallas_call(
        body, name="out_bwd", grid=(nt,),
        in_specs=[tok(D_MODEL), tok(D_ATTN), tok(D_SSM), _full((D_MODEL, D_MODEL))],
        out_specs=[tok(D_ATTN), tok(D_SSM), _full((D_MODEL, D_MODEL)), _full((D_MODEL, D_MODEL))],
        out_shape=[jax.ShapeDtypeStruct((SEQ, D_ATTN), f32), jax.ShapeDtypeStruct((SEQ, D_SSM), f32),
                   jax.ShapeDtypeStruct((D_MODEL, D_MODEL), f32), jax.ShapeDtypeStruct((D_MODEL, D_MODEL), bf16)],
        compiler_params=_params("arbitrary"),
    )(dx1, attn, ssm, w_out)


def mlp_fwd(x1, g, w_up, w_down, tm=1024):
    def body(x_ref, g_ref, wu_ref, wd_ref, o_ref, u_ref, h_scr):
        j = pl.program_id(1)

        @pl.when(j == 0)
        def _():
            xv = x_ref[...]
            h_scr[...] = (xv * _rms(xv) * g_ref[...]).astype(bf16)
            o_ref[...] = xv

        u = jnp.dot(h_scr[...], wu_ref[...], preferred_element_type=f32)
        u_ref[...] = u
        a = jnp.square(jnp.maximum(u, 0.0))
        o_ref[...] += _bdot(a, wd_ref[...])

    return pl.pallas_call(
        body, name="mlp_fwd", grid=(SEQ // tm, N_CHIPS),
        in_specs=[pl.BlockSpec((tm, D_MODEL), lambda i, j: (i, 0)), _full((1, D_MODEL)),
                  pl.BlockSpec((None, D_MODEL, FF_TILE), lambda i, j: (j, 0, 0)),
                  pl.BlockSpec((None, FF_TILE, D_MODEL), lambda i, j: (j, 0, 0))],
        out_specs=[pl.BlockSpec((tm, D_MODEL), lambda i, j: (i, 0)), pl.BlockSpec((tm, FF_TILE), lambda i, j: (i, j))],
        out_shape=[jax.ShapeDtypeStruct((SEQ, D_MODEL), f32), jax.ShapeDtypeStruct((SEQ, D_FF), f32)],
        scratch_shapes=[pltpu.VMEM((tm, D_MODEL), bf16)],
        compiler_params=_params("arbitrary", "arbitrary"),
    )(x1, g, w_up, w_down)


def mlp_bwd_data(dx2, u, x1, g, w_up, w_down, tm=1024):
    def body(d_ref, u_ref, x_ref, g_ref, wu_ref, wd_ref, dx_ref, du_ref, dg_ref, dh_scr):
        i, j = pl.program_id(0), pl.program_id(1)

        @pl.when(jnp.logical_and(i == 0, j == 0))
        def _():
            dg_ref[...] = jnp.zeros_like(dg_ref)

        @pl.when(j == 0)
        def _():
            dh_scr[...] = jnp.zeros_like(dh_scr)

        da = _bdot_nt(d_ref[...], wd_ref[...])
        du = (da * (2.0 * jnp.maximum(u_ref[...], 0.0))).astype(bf16)
        du_ref[...] = du
        dh_scr[...] += _bdot_nt(du, wu_ref[...])

        @pl.when(j == N_CHIPS - 1)
        def _():
            xv = x_ref[...]
            r = _rms(xv)
            xhat = xv * r
            dh = dh_scr[...]
            dg_ref[...] += jnp.sum(dh * xhat, axis=0, keepdims=True)
            dx_ref[...] = d_ref[...] + _rms_bwd(dh, xhat, r, g_ref[...])

    return pl.pallas_call(
        body, name="mlp_bwd_data", grid=(SEQ // tm, N_CHIPS),
        in_specs=[pl.BlockSpec((tm, D_MODEL), lambda i, j: (i, 0)), pl.BlockSpec((tm, FF_TILE), lambda i, j: (i, j)),
                  pl.BlockSpec((tm, D_MODEL), lambda i, j: (i, 0)), _full((1, D_MODEL)),
                  pl.BlockSpec((None, D_MODEL, FF_TILE), lambda i, j: (j, 0, 0)),
                  pl.BlockSpec((None, FF_TILE, D_MODEL), lambda i, j: (j, 0, 0))],
        out_specs=[pl.BlockSpec((tm, D_MODEL), lambda i, j: (i, 0)), pl.BlockSpec((tm, FF_TILE), lambda i, j: (i, j)),
                   _full((1, D_MODEL))],
        out_shape=[jax.ShapeDtypeStruct((SEQ, D_MODEL), f32), jax.ShapeDtypeStruct((SEQ, D_FF), bf16),
                   jax.ShapeDtypeStruct((1, D_MODEL), f32)],
        scratch_shapes=[pltpu.VMEM((tm, D_MODEL), f32)],
        compiler_params=_params("arbitrary", "arbitrary"),
    )(dx2, u, x1, g, w_up, w_down)


def mlp_bwd_weights(dx2, u, du, x1, g, tm=512):
    nt = SEQ // tm

    def body(d_ref, u_ref, du_ref, x_ref, g_ref, dwu_ref, dwd_ref, dwu16_ref, dwd16_ref, h_scr, d_scr):
        j, i = pl.program_id(0), pl.program_id(1)

        @pl.when(j == 0)
        def _():
            xv = x_ref[...]
            h_scr[i] = (xv * _rms(xv) * g_ref[...]).astype(bf16)
            d_scr[i] = d_ref[...].astype(bf16)

        @pl.when(i == 0)
        def _():
            dwu_ref[...] = jnp.zeros_like(dwu_ref)
            dwd_ref[...] = jnp.zeros_like(dwd_ref)

        dwu_ref[...] += _bdot_tn(h_scr[i], du_ref[...])
        a = jnp.square(jnp.maximum(u_ref[...], 0.0))
        dwd_ref[...] += _bdot_tn(a, d_scr[i])

        @pl.when(i == nt - 1)
        def _():
            dwu16_ref[...] = dwu_ref[...].astype(bf16)
            dwd16_ref[...] = dwd_ref[...].astype(bf16)

    up = pl.BlockSpec((None, D_MODEL, FF_TILE), lambda j, i: (j, 0, 0))
    down = pl.BlockSpec((None, FF_TILE, D_MODEL), lambda j, i: (j, 0, 0))
    first_pass = pl.BlockSpec((tm, D_MODEL), lambda j, i: (jnp.where(j == 0, i, nt - 1), 0))
    return pl.pallas_call(
        body, name="mlp_bwd_weights", grid=(N_CHIPS, nt),
        in_specs=[first_pass, pl.BlockSpec((tm, FF_TILE), lambda j, i: (i, j)),
                  pl.BlockSpec((tm, FF_TILE), lambda j, i: (i, j)), first_pass, _full((1, D_MODEL))],
        out_specs=[up, down, up, down],
        out_shape=[jax.ShapeDtypeStruct((N_CHIPS, D_MODEL, FF_TILE), f32), jax.ShapeDtypeStruct((N_CHIPS, FF_TILE, D_MODEL), f32),
                   jax.ShapeDtypeStruct((N_CHIPS, D_MODEL, FF_TILE), bf16), jax.ShapeDtypeStruct((N_CHIPS, FF_TILE, D_MODEL), bf16)],
        scratch_shapes=[pltpu.VMEM((nt, tm, D_MODEL), bf16), pltpu.VMEM((nt, tm, D_MODEL), bf16)],
        compiler_params=_params("arbitrary", "arbitrary"),
    )(dx2, u, du, x1, g)


def loss_head(y, target, tm=512):
    def body(y_ref, t_ref, dy_ref, l_ref):
        @pl.when(pl.program_id(0) == 0)
        def _():
            l_ref[...] = jnp.zeros_like(l_ref)

        d = y_ref[...] - t_ref[...]
        dy_ref[...] = d * (1.0 / D_MODEL)
        part = jnp.sum(jnp.mean(d * d, axis=-1, keepdims=True), axis=0, keepdims=True)
        l_ref[...] += 0.5 * part

    tok = pl.BlockSpec((tm, D_MODEL), lambda i: (i, 0))
    return pl.pallas_call(
        body, name="loss_head", grid=(SEQ // tm,), in_specs=[tok, tok], out_specs=[tok, _full((1, 1))],
        out_shape=[jax.ShapeDtypeStruct((SEQ, D_MODEL), f32), jax.ShapeDtypeStruct((1, 1), f32)],
        compiler_params=_params("arbitrary"),
    )(y, target)


def _pad_lane(v):
    return jnp.pad(v, (0, LANE - v.shape[0]))[None, :]


def local_step(x, target, w, prov):
    bucket = jnp.asarray(_bucket_table().T)
    bias = bias_build(w["rel_bias"], bucket)
    saved = []
    for l in range(DEPTH):
        g_mix = w["mix_norm_g"][l][None, :] + prov.stage(("begin", l), x)
        w_in = prov.w_in(l, x)
        proj = in_fwd(x, g_mix, w_in)
        qg, kg = w["q_gain"][l][:, None], w["k_gain"][l][None, :]
        attn = attn_fwd_t(proj, qg, kg, w["sinks"][l], bias)
        conv_b = w["conv_b"][l][None, :]
        act = conv_fwd(proj, w["conv_w"][l], conv_b)
        dtb = _pad_lane(w["dt_bias"][l]) + prov.stage(("mid", l), act)
        alog, dsk = _pad_lane(w["a_log"][l]), _pad_lane(w["d_skip"][l])
        ng = w["ssm_norm_g"][l][None, :]
        ssm, ypre, states = ssd_fwd_g(act, proj, dtb, alog, dsk, ng)
        tok = prov.stage(("pre_out", l), ssm)
        w_out = prov.w_out(l, ssm) + jnp.asarray(tok, bf16)
        x1 = out_fwd(x, attn, ssm, w_out)
        g_mlp = w["mlp_norm_g"][l][None, :] + prov.stage(("pre_mlp", l), x1)
        w_up, w_down = prov.mlp(l, x1)
        x2, u = mlp_fwd(x1, g_mlp, w_up, w_down)
        saved.append(dict(x=x, proj=proj, attn=attn, act=act, ssm=ssm, ypre=ypre, states=states, x1=x1, u=u,
                          g_mix=g_mix, qg=qg, kg=kg, conv_b=conv_b, dtb=dtb, alog=alog, dsk=dsk, ng=ng, g_mlp=g_mlp,
                          w_in=w_in, w_out=w_out, w_up=w_up, w_down=w_down))
        x = x2
    dx, loss = loss_head(x, target)
    grads = [None] * DEPTH
    dbands = [None] * DEPTH
    tok = 0.0
    for l in reversed(range(DEPTH)):
        s = saved[l]
        g_mlp = s["g_mlp"] + tok
        dx1, du, dg_mlp = mlp_bwd_data(dx, s["u"], s["x1"], g_mlp, s["w_up"], s["w_down"])
        dw_up, dw_down, dw_up16, dw_down16 = mlp_bwd_weights(dx, s["u"], du, s["x1"], g_mlp)
        tok = prov.grads(("mlp", l), dict(w_up=(dw_up, dw_up16), w_down=(dw_down, dw_down16)), dw_down16)
        dattn, dssm, dw_out, dw_out16 = out_bwd(dx1, s["attn"], s["ssm"], s["w_out"])
        dact, ddt, dz, dng, dpar = ssd_bwd_g(s["act"], s["proj"], s["ypre"], s["states"], dssm, s["dtb"] + tok, s["alog"],
                                           s["dsk"], s["ng"])
        conv_b = s["conv_b"] + prov.stage(("bwd_mid", l), dact)
        dxbc, dconv_w, dconv_b = conv_bwd(s["proj"], dact, w["conv_w"][l], conv_b)
        dq, dk, dv, dband, dsink, dqg, dkg = attn_bwd_t(s["proj"], dattn, s["qg"], s["kg"], w["sinks"][l], bias)
        dx, dw_in, dg_mix = in_bwd(dq, dz, dxbc, dk, dv, ddt, s["x"], s["g_mix"], s["w_in"], dx1)
        tok = prov.grads(("mix", l), dict(w_in=split_w_in_grad(dw_in), w_out=(dw_out, dw_out16)), dx)
        dbands[l] = dband
        grads[l] = dict(mix_norm_g=dg_mix[0], q_gain=dqg[:, 0], k_gain=dkg[0], sinks=dsink[:, 0],
                        conv_w=dconv_w, conv_b=dconv_b[0], dt_bias=dpar[0, :SSM_HEADS], a_log=dpar[1, :SSM_HEADS],
                        d_skip=dpar[2, :SSM_HEADS], ssm_norm_g=dng[0], mlp_norm_g=dg_mlp[0])
    out = {k: jnp.stack([grads[l][k] for l in range(DEPTH)]) for k in grads[0]}
    out["rel_bias"] = bias_bwd(dbands[0], dbands[1], bucket)[:, :N_Q_HEADS]
    return loss, dx, out, tok


MESH = pl.DeviceIdType.MESH
HBM = pl.BlockSpec(memory_space=pltpu.HBM)
N_PEER_CHIPS = N_CHIPS - 1
N_DEVICES = 8


def _coords():
    return lax.axis_index("x"), lax.axis_index("y"), lax.axis_index("c")


def _peer_chips(x, y):
    return [(1 - x, y), (x, 1 - y), (1 - x, 1 - y)]


def _remote(src, dst, send_sem, recv_sem, device):
    return pltpu.make_async_remote_copy(src_ref=src, dst_ref=dst, send_sem=send_sem, recv_sem=recv_sem,
                                        device_id=device, device_id_type=MESH)


SEM = pl.BlockSpec(memory_space=pltpu.SEMAPHORE)
ANY = pl.BlockSpec(memory_space=pl.ANY)
DATAFLOW = pltpu.SideEffectType.DATAFLOW_SIDE_EFFECTING


def _gather_copies(kind, src_refs, land_refs, ssem, rsem):
    x, y, c = _coords()
    k_me = 2 * x + y
    n = len(land_refs)
    cps = []
    for p, land in enumerate(land_refs):
        hr = land.shape[1] // 2
        rows = pl.ds(c * hr, hr)
        for j, chip in enumerate(_peer_chips(x, y)):
            i = 3 * p + j
            if kind == "ici":
                cps.append(_remote(src_refs[p].at[rows, :], land.at[k_me, rows, :], ssem.at[i], rsem.at[i], (*chip, c)))
            else:
                got = land.at[2 * chip[0] + chip[1], rows, :]
                cps.append(_remote(got, got, ssem.at[i], rsem.at[i], (x, y, 1 - c)))
        if kind == "relay":
            cps.append(_remote(src_refs[p], land.at[k_me], ssem.at[3 * n + p], rsem.at[3 * n + p], (x, y, 1 - c)))
    return cps


def gather_now(srcs, conv):
    n = len(srcs)

    def body(*refs):
        src_refs, conv_ref = refs[:n], refs[n]
        lands, gconv = refs[n + 1:2 * n + 1], refs[2 * n + 1]
        ssem, rsem, fsem, frsem, csem, crsem = refs[2 * n + 2:]
        x, y, c = _coords()
        k_me = 2 * x + y
        targets = [(*chip, c) for chip in _peer_chips(x, y)] + [(x, y, 1 - c)]
        ici = _gather_copies("ici", src_refs, lands, ssem, rsem)
        relay = _gather_copies("relay", src_refs, lands, fsem, frsem)
        passed = [cp for i, cp in enumerate(relay) if i % 4 != 3]
        own = relay[3::4]
        conv_cps = [_remote(conv_ref, gconv.at[k_me], csem.at[j], crsem.at[j], t) for j, t in enumerate(targets)]
        for cp in ici + conv_cps + own:
            cp.start()
        for cp, fw in zip(ici, passed):
            cp.wait_recv()
            fw.start()
        for cp in conv_cps + relay:
            cp.wait_recv()
        for cp in ici + relay + conv_cps:
            cp.wait_send()

    out_shape = [jax.ShapeDtypeStruct((N_CHIPS,) + s.shape, s.dtype) for s in srcs]
    out_shape.append(jax.ShapeDtypeStruct((N_CHIPS,) + conv.shape, conv.dtype))
    sems = lambda k: pltpu.SemaphoreType.DMA((k,))
    return pl.pallas_call(
        body, name="gather_now", out_shape=out_shape, in_specs=[HBM] * (n + 1), out_specs=[HBM] * (n + 1),
        scratch_shapes=[sems(3 * n), sems(3 * n), sems(4 * n), sems(4 * n), sems(N_CHIPS), sems(N_CHIPS)],
    )(*srcs, conv)


def _gather_maker(kind, n_src):
    def make(refs, ssem, rsem):
        cps = _gather_copies(kind, refs[:n_src], refs[n_src:], ssem, rsem)
        return cps, cps
    return make


def _scatter_maker(n):
    def make(refs, ssem, rsem):
        x, y, c = _coords()
        k_me = 2 * x + y
        sends, arrivals = [], []
        for p in range(n):
            src, land = refs[p], refs[n + p]
            sends.append(_remote(src.at[k_me, 1 - c], land.at[0], ssem.at[7 * p], rsem.at[7 * p], (x, y, 1 - c)))
            for j, chip in enumerate(_peer_chips(x, y)):
                for cc in range(2):
                    sends.append(_remote(src.at[2 * chip[0] + chip[1], cc], land.at[1 + 2 * j + c],
                                         ssem.at[7 * p + 1 + 2 * j + cc], rsem.at[7 * p + 1 + 2 * j + c], (*chip, cc)))
            for s in range(7):
                arrivals.append(_remote(land.at[s], land.at[s], ssem.at[7 * p + s], rsem.at[7 * p + s], (x, y, 1 - c)))
        return sends, arrivals
    return make


def _share_maker(n):
    def make(refs, ssem, rsem):
        x, y, c = _coords()
        sends = [_remote(refs[p].at[c], refs[p].at[c], ssem.at[p], rsem.at[p], (x, y, 1 - c)) for p in range(n)]
        arrivals = [_remote(refs[p].at[1 - c], refs[p].at[1 - c], ssem.at[p], rsem.at[p], (x, y, 1 - c)) for p in range(n)]
        return sends, arrivals
    return make


def split_start(name, make, n_sems, operands, after):
    n = len(operands)

    def body(*refs):
        ssem, rsem, token = refs[n + 1], refs[n + 2], refs[-1]
        for cp in make(refs[:n], ssem, rsem)[0]:
            cp.start()
        token[...] = jnp.zeros_like(token)

    ops = [pltpu.with_memory_space_constraint(a, pltpu.HBM) for a in operands]
    outs = pl.pallas_call(
        body, name=name,
        out_shape=(pltpu.SemaphoreType.DMA((n_sems,)), pltpu.SemaphoreType.DMA((n_sems,)),
                   *[pltpu.HBM(a.shape, a.dtype) for a in ops], jax.ShapeDtypeStruct((8, LANE), f32)),
        in_specs=[HBM] * n + [ANY], out_specs=(SEM, SEM, *[HBM] * n, pl.BlockSpec(memory_space=pltpu.VMEM)),
        input_output_aliases={i: 2 + i for i in range(n)},
        compiler_params=pltpu.CompilerParams(has_side_effects=DATAFLOW),
    )(*ops, after)
    return dict(name=name, make=make, ssem=outs[0], rsem=outs[1], operands=outs[2:2 + n], token=outs[-1][0, 0])


def split_wait(handle, after):
    n = len(handle["operands"])

    def body(*refs):
        sends, arrivals = handle["make"](refs[:n], refs[n], refs[n + 1])
        for cp in sends:
            cp.wait_send()
        for cp in arrivals:
            cp.wait_recv()

    outs = pl.pallas_call(
        body, name=handle["name"].replace("start", "wait"),
        out_shape=tuple(pltpu.HBM(a.shape, a.dtype) for a in handle["operands"]),
        in_specs=[HBM] * n + [SEM, SEM, ANY], out_specs=tuple([HBM] * n),
        input_output_aliases={i: i for i in range(n)},
        compiler_params=pltpu.CompilerParams(has_side_effects=DATAFLOW),
    )(*handle["operands"], handle["ssem"], handle["rsem"], after)
    return list(outs)


def piece_sum(g, recv, kc_arr):
    _, _, rb, cc = g.shape
    tr = min(256, rb)

    def body(kc_ref, g_ref, r_ref, o_ref):
        acc = g_ref[...]
        for s in range(7):
            acc = acc + r_ref[s].astype(f32)
        o_ref[...] = acc

    return pl.pallas_call(
        body, name="piece_sum",
        grid_spec=pltpu.PrefetchScalarGridSpec(
            num_scalar_prefetch=1, grid=(rb // tr,),
            in_specs=[pl.BlockSpec((None, None, tr, cc), lambda r, kc: (kc[0], kc[1], r, 0)),
                      pl.BlockSpec((7, tr, cc), lambda r, kc: (0, r, 0))],
            out_specs=pl.BlockSpec((None, tr, cc), lambda r, kc: (kc[1], r, 0))),
        out_shape=jax.ShapeDtypeStruct((2, rb, cc), f32),
        compiler_params=_params("arbitrary"),
    )(kc_arr, g, recv)


def small_all_reduce(vec):
    def body(v_ref, o_ref, gat, ssem, rsem):
        x, y, c = _coords()
        me = 4 * x + 2 * y + c
        gat[me] = v_ref[...]
        sends = []
        for t in range(1, N_DEVICES):
            peer = (x ^ (t >> 2), y ^ ((t >> 1) & 1), c ^ (t & 1))
            cp = _remote(v_ref, gat.at[me], ssem.at[t - 1], rsem.at[t - 1], peer)
            cp.start()
            sends.append(cp)
        for t in range(1, N_DEVICES):
            peer = (x ^ (t >> 2), y ^ ((t >> 1) & 1), c ^ (t & 1))
            slot = gat.at[4 * peer[0] + 2 * peer[1] + peer[2]]
            _remote(slot, slot, ssem.at[t - 1], rsem.at[t - 1], peer).wait_recv()
        for cp in sends:
            cp.wait_send()
        acc = gat[0]
        for d in range(1, N_DEVICES):
            acc = acc + gat[d]
        o_ref[...] = acc

    return pl.pallas_call(
        body, name="small_all_reduce", out_shape=jax.ShapeDtypeStruct(vec.shape, vec.dtype),
        in_specs=[pl.BlockSpec(memory_space=pltpu.VMEM)], out_specs=pl.BlockSpec(memory_space=pltpu.VMEM),
        scratch_shapes=[pltpu.VMEM((N_DEVICES,) + vec.shape, vec.dtype), pltpu.SemaphoreType.DMA((N_DEVICES - 1,)),
                        pltpu.SemaphoreType.DMA((N_DEVICES - 1,))],
    )(vec)


def _adamw_math(w, g, m, v):
    m_new = ADAM_B1 * m + (1.0 - ADAM_B1) * g
    v_new = ADAM_B2 * v + (1.0 - ADAM_B2) * jnp.square(g)
    m_hat = m_new / (1.0 - ADAM_B1 ** ADAM_STEP)
    v_hat = v_new / (1.0 - ADAM_B2 ** ADAM_STEP)
    delta = -ADAM_LR * (m_hat / (jnp.sqrt(v_hat) + ADAM_EPS) + ADAM_WD * w)
    return delta, m_new, v_new


def adamw_shard(w, g0, g1, m, v):
    depth, rows, cols = w.shape
    half = rows // 2
    tr = min(256, half)
    nr = half // tr

    def body(w_ref, g0_ref, g1_ref, m_ref, v_ref, go_ref, d_ref, nm_ref, nv_ref):
        gv = jnp.where(pl.program_id(0) == 0, g0_ref[...], g1_ref[...])
        go_ref[...] = gv
        d_ref[...], nm_ref[...], nv_ref[...] = _adamw_math(w_ref[...], gv, m_ref[...], v_ref[...])

    spec = pl.BlockSpec((None, tr, cols), lambda l, h, r: (l, h * nr + r, 0))
    g0spec = pl.BlockSpec((None, tr, cols), lambda l, h, r: (jnp.where(l == 0, h, 1), jnp.where(l == 0, r, nr - 1), 0))
    g1spec = pl.BlockSpec((None, tr, cols), lambda l, h, r: (jnp.where(l == 1, h, 0), jnp.where(l == 1, r, 0), 0))
    return pl.pallas_call(
        body, name="adamw_shard", grid=(depth, 2, nr), in_specs=[spec, g0spec, g1spec, spec, spec], out_specs=[spec] * 4,
        out_shape=[jax.ShapeDtypeStruct(w.shape, f32)] * 4,
        compiler_params=_params("arbitrary", "arbitrary", "arbitrary"),
    )(w, g0, g1, m, v)


def adamw_small(w, g, m, v):
    def body(w_ref, g_ref, m_ref, v_ref, d_ref, nm_ref, nv_ref):
        d_ref[...], nm_ref[...], nv_ref[...] = _adamw_math(w_ref[...], g_ref[...], m_ref[...], v_ref[...])

    return pl.pallas_call(
        body, name="adamw_small", out_shape=[jax.ShapeDtypeStruct(w.shape, f32)] * 3,
    )(w, g, m, v)


WEIGHTS = ("mix_norm_g", "w_in", "q_gain", "k_gain", "sinks", "rel_bias", "conv_w", "conv_b", "dt_bias", "a_log", "d_skip",
           "ssm_norm_g", "w_out", "mlp_norm_g", "w_up", "w_down")
BIG = ("w_in", "w_out", "w_up", "w_down")
SMALL = tuple(n for n in WEIGHTS if n not in BIG)
PACK_COLS = 1024
PACK_ROWS = 16


def _pack(named, last=None):
    flat = jnp.concatenate([named[n].reshape(-1) for n in SMALL])
    tail = jnp.zeros((1,), f32) if last is None else last.reshape(1)
    pad = jnp.zeros((PACK_ROWS * PACK_COLS - flat.shape[0] - 1,), f32)
    return jnp.concatenate([flat, pad, tail]).reshape(PACK_ROWS, PACK_COLS)


def _unpack(buf, shapes):
    flat = buf.reshape(-1)
    out, at = {}, 0
    for n in SMALL:
        size = int(np.prod(shapes[n]))
        out[n] = flat[at:at + size].reshape(shapes[n])
        at += size
    return out


class _Exchange:
    GROUPS = {"A": (("w_up", 0), ("w_down", 0)), "B": (("w_in", 1), ("w_out", 1)), "C": (("w_up", 1), ("w_down", 1))}
    RELAY_AT = {("pre_out", 0): "A", ("pre_mlp", 0): "B", ("mid", 1): "C"}
    NEXT_GROUP = {"A": "B", "B": "C"}
    LAST = ("mix", 0)

    def __init__(self, wts, k_me, kc_arr):
        self.wts, self.k_me, self.kc_arr = wts, k_me, kc_arr
        self.own = {(n, l): wts[n][l].astype(bf16) for n in BIG for l in range(DEPTH)}
        now = gather_now([self.own["w_in", 0], self.own["w_out", 0]], wts["conv_w"])
        self.ready = {("w_in", 0): now[0], ("w_out", 0): now[1]}
        self.conv_w = jnp.transpose(now[2], (1, 2, 0, 3)).reshape(DEPTH, CONV_WIDTH, D_CONV)
        self.ici, self.relay = {}, {}
        self.gview, self.scatter, self.share, self.reduced = {}, [], [], {}
        self._start_ici("A", now[2])

    def _start_ici(self, g, after):
        srcs = [self.own[p] for p in self.GROUPS[g]]
        lands = [lax.empty((N_CHIPS,) + s.shape, s.dtype) for s in srcs]
        self.ici[g] = split_start("gather%s_ici_start" % g, _gather_maker("ici", len(srcs)), 3 * len(srcs), srcs + lands,
                                  after)
        return self.ici[g]["token"]

    def stage(self, name, after):
        if name == ("begin", 0):
            return self.ici["A"]["token"]
        g = self.RELAY_AT.get(name)
        if g is None:
            return 0.0
        n = len(self.GROUPS[g])
        self.relay[g] = split_start("gather%s_relay_start" % g, _gather_maker("relay", n), 4 * n,
                                    split_wait(self.ici[g], after), after)
        tok = self.relay[g]["token"]
        if g in self.NEXT_GROUP:
            tok = tok + self._start_ici(self.NEXT_GROUP[g], after)
        return tok

    def _get(self, piece, after):
        if piece not in self.ready:
            g = [k for k, pieces in self.GROUPS.items() if piece in pieces][0]
            lands = split_wait(self.relay[g], after)[len(self.GROUPS[g]):]
            self.ready.update(zip(self.GROUPS[g], lands))
        return self.ready[piece]

    def w_in(self, l, after):
        return align_w_in(self._get(("w_in", l), after))

    def w_out(self, l, after):
        return self._get(("w_out", l), after).reshape(D_MODEL, D_MODEL)

    def mlp(self, l, after):
        return self._get(("w_up", l), after), self._get(("w_down", l), after)

    def _view(self, n, g):
        _, rows, cols = self.wts[n].shape
        return g.reshape(N_CHIPS, 2, rows // 2, cols)

    def grads(self, name, arrays, after):
        if name == self.LAST:
            self.held = (name, arrays)
            return self._advance(after, 0)
        return self._scatter(name, arrays, after) + self._advance(after, 1)

    def flush(self, after):
        return self._scatter(*self.held, after)

    def _scatter(self, name, arrays, after):
        pieces = [(n, name[1]) for n in arrays]
        views = [self._view(n, g) for n, (g, _) in arrays.items()]
        sends = [g16.reshape(v.shape) for v, (_, g16) in zip(views, arrays.values())]
        self.gview.update(zip(pieces, views))
        lands = [lax.empty((7,) + v.shape[2:], bf16) for v in views]
        h = split_start("scatter_%s%d_start" % name, _scatter_maker(len(views)), 7 * len(views), sends + lands, after)
        self.scatter.append((pieces, h))
        return h["token"]

    def _take_share(self, after):
        pieces, h = self.share.pop(0)
        self.reduced.update(zip(pieces, split_wait(h, after)))

    def _take_scatter(self, after):
        pieces, h = self.scatter.pop(0)
        lands = split_wait(h, after)[len(pieces):]
        sums = [piece_sum(self.gview[p], land, self.kc_arr) for p, land in zip(pieces, lands)]
        hs = split_start(h["name"].replace("scatter", "share"), _share_maker(len(sums)), len(sums), sums, after)
        self.share.append((pieces, hs))
        return hs["token"]

    def _advance(self, after, newest):
        if self.share:
            self._take_share(after)
        return self._take_scatter(after) if len(self.scatter) > newest else 0.0

    def reduced_grads(self, names, after):
        want = [(n, l) for n in names for l in range(DEPTH)]
        while not all(p in self.reduced for p in want):
            if any(p in pieces for p in want for pieces, _ in self.share):
                self._take_share(after)
            else:
                self._take_scatter(after)
        return {n: [self.reduced[n, l] for l in range(DEPTH)] for n in names}


def kernel(x, mix_norm_g, w_in, q_gain, k_gain, sinks, rel_bias, conv_w, conv_b, dt_bias, a_log, d_skip, ssm_norm_g, w_out, mlp_norm_g, w_up, w_down, loss_target, m_mix_norm_g, m_w_in, m_q_gain, m_k_gain, m_sinks, m_rel_bias, m_conv_w, m_conv_b, m_dt_bias, m_a_log, m_d_skip, m_ssm_norm_g, m_w_out, m_mlp_norm_g, m_w_up, m_w_down, v_mix_norm_g, v_w_in, v_q_gain, v_k_gain, v_sinks, v_rel_bias, v_conv_w, v_conv_b, v_dt_bias, v_a_log, v_d_skip, v_ssm_norm_g, v_w_out, v_mlp_norm_g, v_w_up, v_w_down):
    wts = dict(mix_norm_g=mix_norm_g, w_in=w_in, q_gain=q_gain, k_gain=k_gain, sinks=sinks, rel_bias=rel_bias, conv_w=conv_w,
               conv_b=conv_b, dt_bias=dt_bias, a_log=a_log, d_skip=d_skip, ssm_norm_g=ssm_norm_g, w_out=w_out,
               mlp_norm_g=mlp_norm_g, w_up=w_up, w_down=w_down)
    mom = dict(mix_norm_g=m_mix_norm_g, w_in=m_w_in, q_gain=m_q_gain, k_gain=m_k_gain, sinks=m_sinks, rel_bias=m_rel_bias,
               conv_w=m_conv_w, conv_b=m_conv_b, dt_bias=m_dt_bias, a_log=m_a_log, d_skip=m_d_skip, ssm_norm_g=m_ssm_norm_g,
               w_out=m_w_out, mlp_norm_g=m_mlp_norm_g, w_up=m_w_up, w_down=m_w_down)
    var = dict(mix_norm_g=v_mix_norm_g, w_in=v_w_in, q_gain=v_q_gain, k_gain=v_k_gain, sinks=v_sinks, rel_bias=v_rel_bias,
               conv_w=v_conv_w, conv_b=v_conv_b, dt_bias=v_dt_bias, a_log=v_a_log, d_skip=v_d_skip, ssm_norm_g=v_ssm_norm_g,
               w_out=v_w_out, mlp_norm_g=v_mlp_norm_g, w_up=v_w_up, w_down=v_w_down)
    xi, yi, ci = _coords()
    k_me = 2 * xi + yi
    kc_arr = jnp.stack([k_me, ci]).astype(jnp.int32)

    prov = _Exchange(wts, k_me, kc_arr)
    small_w = {n: wts[n] for n in SMALL}
    small_w["conv_w"] = prov.conv_w
    loss, dx, grads, tok = local_step(x[0], loss_target[0], small_w, prov)

    small_shapes = {n: grads[n].shape for n in SMALL}
    small_sum = small_all_reduce(_pack(grads, loss) + tok)
    loss = small_sum[PACK_ROWS - 1, PACK_COLS - 1]
    tok = prov.flush(small_sum)
    small = _unpack(small_sum, small_shapes)
    cols = conv_w.shape[-1]
    small["conv_w"] = lax.dynamic_slice_in_dim(small["conv_w"], k_me * cols, cols, axis=2)
    g_out_d, d_out_d, m_out_d, v_out_d = {}, {}, {}, {}
    shard_shapes = {n: wts[n].shape for n in SMALL}
    d, nm, nv = adamw_small(_pack(wts), _pack(small) + tok, _pack(mom), _pack(var))
    for dst, buf in ((d_out_d, d), (m_out_d, nm), (v_out_d, nv)):
        dst.update(_unpack(buf, shard_shapes))
    g_out_d.update(small)

    after = d
    for names in (("w_up", "w_down"), ("w_in", "w_out")):
        for n, (g0, g1) in prov.reduced_grads(names, after).items():
            g_out_d[n], d_out_d[n], m_out_d[n], v_out_d[n] = adamw_shard(wts[n], g0, g1, mom[n], var[n])
            after = d_out_d[n]

    return (loss, dx[None], *[g_out_d[n] for n in WEIGHTS], *[d_out_d[n] for n in WEIGHTS],
            *[m_out_d[n] for n in WEIGHTS], *[v_out_d[n] for n in WEIGHTS])
```

```python
import functools

import numpy as np
import jax
import jax.numpy as jnp
from jax import lax
from jax.experimental import pallas as pl
from jax.experimental.pallas import tpu as pltpu

f32 = jnp.float32
bf16 = jnp.bfloat16

SEQ = 2048
D_MODEL = 1024
DEPTH = 2
HEAD_DIM = 64
N_Q_HEADS = 8
N_KV_HEADS = 2
Q_PER_KV = N_Q_HEADS // N_KV_HEADS
BLOCK = 128
N_BLOCKS = SEQ // BLOCK
N_BUCKETS = 32
MAX_DISTANCE = 128
SSM_HEADS = 8
SSM_HEAD_DIM = 64
SSM_GROUPS = 2
HEADS_PER_GROUP = SSM_HEADS // SSM_GROUPS
SSM_STATE = 128
CONV_WIDTH = 4
CHUNK = 128
N_CHUNKS = SEQ // CHUNK
D_FF = 4 * D_MODEL
D_ATTN = N_Q_HEADS * HEAD_DIM
D_KV = N_KV_HEADS * HEAD_DIM
D_SSM = SSM_HEADS * SSM_HEAD_DIM
D_BC = SSM_GROUPS * SSM_STATE
D_CONV = D_SSM + 2 * D_BC
D_IN = D_ATTN + 2 * D_KV + D_SSM + D_CONV + SSM_HEADS
EPS = 1e-6
NEG = -1e30
N_CHIPS = 4
FF_TILE = D_FF // N_CHIPS

LANE = 128
PW = D_ATTN + D_SSM + D_CONV + 2 * D_KV + LANE
OFF_Q, OFF_Z, OFF_X, OFF_K, OFF_V, OFF_DT = 0, 512, 1024, 2048, 2176, 2304

ADAM_LR = 0.001
ADAM_B1 = 0.9
ADAM_B2 = 0.999
ADAM_EPS = 1e-08
ADAM_WD = 0.01
ADAM_STEP = 10

VMEM_LIMIT = 56 * 1024 * 1024


def _params(*sem):
    return pltpu.CompilerParams(dimension_semantics=tuple(sem), vmem_limit_bytes=VMEM_LIMIT)


def _bdot(a, b):
    return jnp.dot(a.astype(bf16), b.astype(bf16), preferred_element_type=f32)


def _bdot_nt(a, b):
    return lax.dot_general(a.astype(bf16), b.astype(bf16), (((1,), (1,)), ((), ())), preferred_element_type=f32)


def _bdot_tn(a, b):
    return lax.dot_general(a.astype(bf16), b.astype(bf16), (((0,), (0,)), ((), ())), preferred_element_type=f32)


def _hdot(a, b):
    return jnp.dot(a, b, precision=lax.Precision.HIGHEST, preferred_element_type=f32)


def _sigmoid(x):
    return 1.0 / (1.0 + jnp.exp(-x))


def _softplus(x):
    return jnp.maximum(x, 0.0) + jnp.log1p(jnp.exp(-jnp.abs(x)))


def _rms(x):
    return lax.rsqrt(jnp.mean(x * x, axis=-1, keepdims=True) + EPS)


def _rms_bwd(dy, xhat, r, g):
    t = dy * g
    return r * (t - xhat * jnp.mean(t * xhat, axis=-1, keepdims=True))


def _full(shape):
    return pl.BlockSpec(shape, lambda *_: (0,) * len(shape))


def _bucket_table():
    qi = np.arange(BLOCK)[:, None]
    kj = np.arange(2 * BLOCK)[None, :]
    dist = qi + BLOCK - kj
    ok = (dist >= 0) & (dist < 128)
    d = np.clip(dist, 0, None)
    max_exact = N_BUCKETS // 2
    d_f = np.maximum(d, 1).astype(np.float32)
    large = max_exact + (np.log(d_f / np.float32(max_exact)) / np.float32(np.log(MAX_DISTANCE / max_exact))
                         * np.float32(N_BUCKETS - max_exact)).astype(np.int32)
    large = np.minimum(large, N_BUCKETS - 1)
    bucket = np.where(d < max_exact, d, large)
    return np.where(ok, bucket, -1).astype(np.int32)


def bias_build(rel_bias, bucket):
    def body(rel_ref, bkt_ref, o_ref):
        bkt = bkt_ref[...]
        for h in range(N_Q_HEADS):
            acc = jnp.where(bkt < 0, NEG, 0.0).astype(f32)
            for b in range(N_BUCKETS):
                acc = acc + jnp.where(bkt == b, rel_ref[b, h], 0.0)
            o_ref[h] = acc

    return pl.pallas_call(
        body, name="bias_build", out_shape=jax.ShapeDtypeStruct((N_Q_HEADS,) + bucket.shape, f32),
        in_specs=[pl.BlockSpec(memory_space=pltpu.SMEM), pl.BlockSpec(memory_space=pltpu.VMEM)],
        out_specs=pl.BlockSpec(memory_space=pltpu.VMEM),
    )(rel_bias, bucket)


def bias_bwd(dband0, dband1, bucket):
    def body(d0_ref, d1_ref, bkt_ref, o_ref):
        bkt = bkt_ref[...]
        o_ref[...] = jnp.zeros_like(o_ref)
        for h in range(N_Q_HEADS):
            d = d0_ref[h] + d1_ref[h]
            for b in range(N_BUCKETS):
                part = jnp.sum(jnp.where(bkt == b, d, 0.0), axis=1, keepdims=True)
                o_ref[b:b + 1, h:h + 1] = jnp.sum(part, axis=0, keepdims=True)

    return pl.pallas_call(
        body, name="bias_bwd", out_shape=jax.ShapeDtypeStruct((N_BUCKETS, LANE), f32),
    )(dband0, dband1, bucket)


W_IN_SHARD = D_IN // N_CHIPS
_ALIGNED_PIECES = ((0, 0, 512), (1, 190, 578), (2, 0, 124), (2, 124, 578), (3, 0, 570), (0, 512, 578), (1, 0, 62),
                   (1, 62, 190), (3, 570, 578))
_SHARD_PIECES = (((0, 512), (2048, 2114)), ((2114, 2176), (2176, 2304), (512, 900)), ((900, 1024), (1024, 1478)),
                 ((1478, 2048), (2304, 2312)))


def align_w_in(shards, tr=256):
    def body(s_ref, o_ref):
        parts = [s_ref[k, :, a:b] for k, a, b in _ALIGNED_PIECES]
        parts.append(jnp.zeros((tr, LANE - SSM_HEADS), s_ref.dtype))
        o_ref[...] = jnp.concatenate(parts, axis=-1)

    return pl.pallas_call(
        body, name="align_w_in", grid=(D_MODEL // tr,),
        in_specs=[pl.BlockSpec((N_CHIPS, tr, W_IN_SHARD), lambda i: (0, i, 0))],
        out_specs=pl.BlockSpec((tr, PW), lambda i: (i, 0)),
        out_shape=jax.ShapeDtypeStruct((D_MODEL, PW), shards.dtype),
        compiler_params=_params("arbitrary"),
    )(shards)


def split_w_in_grad(dw, tr=256):
    def body(d_ref, o_ref, o16_ref):
        for k, pieces in enumerate(_SHARD_PIECES):
            part = jnp.concatenate([d_ref[:, a:b] for a, b in pieces], axis=-1)
            o_ref[k] = part
            o16_ref[k] = part.astype(bf16)

    spec = pl.BlockSpec((N_CHIPS, tr, W_IN_SHARD), lambda i: (0, i, 0))
    return pl.pallas_call(
        body, name="split_w_in_grad", grid=(D_MODEL // tr,),
        in_specs=[pl.BlockSpec((tr, PW), lambda i: (i, 0))], out_specs=[spec, spec],
        out_shape=[jax.ShapeDtypeStruct((N_CHIPS, D_MODEL, W_IN_SHARD), f32),
                   jax.ShapeDtypeStruct((N_CHIPS, D_MODEL, W_IN_SHARD), bf16)],
        compiler_params=_params("arbitrary"),
    )(dw)

def in_fwd(x, g, w, tm=512):
    def body(x_ref, g_ref, w_ref, o_ref):
        xv = x_ref[...]
        h = xv * _rms(xv) * g_ref[...]
        o_ref[...] = _bdot(h, w_ref[...])

    return pl.pallas_call(
        body, name="in_fwd", grid=(SEQ // tm,),
        in_specs=[pl.BlockSpec((tm, D_MODEL), lambda i: (i, 0)), _full((1, D_MODEL)), _resident((D_MODEL, PW))],
        out_specs=pl.BlockSpec((tm, PW), lambda i: (i, 0)),
        out_shape=jax.ShapeDtypeStruct((SEQ, PW), f32),
        compiler_params=_params("arbitrary"),
    )(x, g, w)


def _resident(shape):
    return pl.BlockSpec(shape, lambda *_: (0,) * len(shape), pipeline_mode=pl.Buffered(1))


def in_bwd(dq, dz, dxbc, dk, dv, ddt, x, g, w, dres, tm=512):
    def body(dq_ref, dz_ref, dx_ref, dk_ref, dv_ref, ddt_ref, x_ref, g_ref, w_ref, dres_ref, o_ref, dw_ref, dg_ref):
        i = pl.program_id(0)

        @pl.when(i == 0)
        def _():
            dw_ref[...] = jnp.zeros_like(dw_ref)
            dg_ref[...] = jnp.zeros_like(dg_ref)

        dproj = jnp.concatenate([dq_ref[...], dz_ref[...], dx_ref[...], dk_ref[...], dv_ref[...], ddt_ref[...]],
                                axis=-1).astype(bf16)
        xv = x_ref[...]
        r = _rms(xv)
        xhat = xv * r
        gv = g_ref[...]
        h = xhat * gv
        dw_ref[...] += _bdot_tn(h, dproj)
        dh = _bdot_nt(dproj, w_ref[...])
        dg_ref[...] += jnp.sum(dh * xhat, axis=0, keepdims=True)
        o_ref[...] = dres_ref[...] + _rms_bwd(dh, xhat, r, gv)

    tok = lambda w_: pl.BlockSpec((tm, w_), lambda i: (i, 0))
    return pl.pallas_call(
        body, name="in_bwd", grid=(SEQ // tm,),
        in_specs=[tok(D_ATTN), tok(D_SSM), tok(D_CONV), tok(D_KV), tok(D_KV), tok(LANE), tok(D_MODEL),
                  _full((1, D_MODEL)), _resident((D_MODEL, PW)), tok(D_MODEL)],
        out_specs=[tok(D_MODEL), _resident((D_MODEL, PW)), _full((1, D_MODEL))],
        out_shape=[jax.ShapeDtypeStruct((SEQ, D_MODEL), f32), jax.ShapeDtypeStruct((D_MODEL, PW), f32),
                   jax.ShapeDtypeStruct((1, D_MODEL), f32)],
        compiler_params=_params("arbitrary"),
    )(dq, dz, dxbc, dk, dv, ddt, x, g, w, dres)


def _attn_softmax_t(qk, bias_t, sink, first, key_row):
    s = qk * (HEAD_DIM ** -0.5) + bias_t
    s = jnp.where(jnp.logical_and(first, key_row < BLOCK), NEG, s)
    m = jnp.maximum(jnp.max(s, axis=0, keepdims=True), sink)
    p = jnp.exp(s - m)
    psink = jnp.exp(sink - m)
    inv = 1.0 / (jnp.sum(p, axis=0, keepdims=True) + psink)
    return p * inv, psink * inv


def _rms_t(x_t):
    return lax.rsqrt(jnp.mean(x_t * x_t, axis=0, keepdims=True) + EPS)


def attn_fwd_t(proj, q_gain_col, k_gain, sinks, bias_t):
    kcol, vcol = OFF_K // D_KV, OFF_V // D_KV

    def body(q_ref, kc_ref, kp_ref, vc_ref, vp_ref, qg_ref, kg_ref, sink_ref, bias_ref, o_ref, ot_scr):
        n = pl.program_id(0)
        first = n == 0
        key_row = lax.broadcasted_iota(jnp.int32, (2 * BLOCK, BLOCK), 0)
        k2 = jnp.concatenate([kp_ref[...], kc_ref[...]], axis=0)
        v_t = jnp.concatenate([vp_ref[...], vc_ref[...]], axis=0).T
        q_t = q_ref[...].T
        qg = jnp.broadcast_to(qg_ref[...], (HEAD_DIM, BLOCK))
        kg = kg_ref[...]
        for hk in range(N_KV_HEADS):
            sl = slice(hk * HEAD_DIM, (hk + 1) * HEAD_DIM)
            kk = k2[:, sl]
            kn = (kk * _rms(kk) * kg).astype(bf16)
            vt = v_t[sl, :].astype(bf16)
            heads = range(hk * Q_PER_KV, (hk + 1) * Q_PER_KV)
            qns = []
            for h in heads:
                qh = q_t[h * HEAD_DIM:(h + 1) * HEAD_DIM, :]
                qns.append(qh * _rms_t(qh) * qg)
            scores = [_bdot(kn, qn) for qn in qns]
            for h, s in zip(heads, scores):
                p, _ = _attn_softmax_t(s, bias_ref[h], sink_ref[h], first, key_row)
                ot_scr[h * HEAD_DIM:(h + 1) * HEAD_DIM, :] = _bdot(vt, p)
        o_ref[...] = ot_scr[...].T

    prev = lambda n: jnp.maximum(n - 1, 0)
    return pl.pallas_call(
        body, name="attn_fwd", grid=(N_BLOCKS,),
        in_specs=[pl.BlockSpec((BLOCK, D_ATTN), lambda n: (n, 0)),
                  pl.BlockSpec((BLOCK, D_KV), lambda n: (n, kcol)), pl.BlockSpec((BLOCK, D_KV), lambda n: (prev(n), kcol)),
                  pl.BlockSpec((BLOCK, D_KV), lambda n: (n, vcol)), pl.BlockSpec((BLOCK, D_KV), lambda n: (prev(n), vcol)),
                  _full((HEAD_DIM, 1)), _full((1, HEAD_DIM)), pl.BlockSpec(memory_space=pltpu.SMEM),
                  _full((N_Q_HEADS, 2 * BLOCK, BLOCK))],
        out_specs=pl.BlockSpec((BLOCK, D_ATTN), lambda n: (n, 0)),
        out_shape=jax.ShapeDtypeStruct((SEQ, D_ATTN), f32),
        scratch_shapes=[pltpu.VMEM((D_ATTN, BLOCK), f32)],
        compiler_params=_params("arbitrary"),
    )(proj, proj, proj, proj, proj, q_gain_col, k_gain, sinks, bias_t)


def attn_bwd_t(proj, d_out, q_gain_col, k_gain, sinks, bias_t):
    kcol, vcol = OFF_K // D_KV, OFF_V // D_KV

    def body(q_ref, kc_ref, kp_ref, vc_ref, vp_ref, do_ref, qg_ref, kg_ref, sink_ref, bias_ref,
             dq_ref, dk_ref, dv_ref, dband_ref, dsink_ref, dqg_ref, dkg_ref, dkn_scr, dv_scr, dqt_scr, dsink_acc, dqg_acc):
        i = pl.program_id(0)
        first = i == N_BLOCKS - 1

        @pl.when(i == 0)
        def _():
            for ref in (dband_ref, dkg_ref, dkn_scr, dv_scr, dsink_acc, dqg_acc):
                ref[...] = jnp.zeros_like(ref)

        key_row = lax.broadcasted_iota(jnp.int32, (2 * BLOCK, BLOCK), 0)
        k2 = jnp.concatenate([kp_ref[...], kc_ref[...]], axis=0)
        v2 = jnp.concatenate([vp_ref[...], vc_ref[...]], axis=0)
        q_t = q_ref[...].T
        do_t = do_ref[...].T
        qg = jnp.broadcast_to(qg_ref[...], (HEAD_DIM, BLOCK))
        kg = kg_ref[...]
        scale = HEAD_DIM ** -0.5
        for hk in range(N_KV_HEADS):
            sl = slice(hk * HEAD_DIM, (hk + 1) * HEAD_DIM)
            kk = k2[:, sl]
            rk = _rms(kk)
            khat = kk * rk
            kn = (khat * kg).astype(bf16)
            vb = v2[:, sl].astype(bf16)
            dkn = jnp.zeros((2 * BLOCK, HEAD_DIM), f32)
            dvv = jnp.zeros((2 * BLOCK, HEAD_DIM), f32)
            heads = range(hk * Q_PER_KV, (hk + 1) * Q_PER_KV)
            rqs, qhats, qns, d_os = [], [], [], []
            for h in heads:
                hs = slice(h * HEAD_DIM, (h + 1) * HEAD_DIM)
                qh = q_t[hs, :]
                rqs.append(_rms_t(qh))
                qhats.append(qh * rqs[-1])
                qns.append((qhats[-1] * qg).astype(bf16))
                d_os.append(do_t[hs, :].astype(bf16))
            scores = [_bdot(kn, qn) for qn in qns]
            dps = [_bdot(vb, d_o) for d_o in d_os]
            ps, dss = [], []
            for h, s, dp in zip(heads, scores, dps):
                p, psink = _attn_softmax_t(s, bias_ref[h], sink_ref[h], first, key_row)
                delta = jnp.sum(p * dp, axis=0, keepdims=True)
                ds = p * (dp - delta)
                dband_ref[h] += ds
                dsink_acc[h:h + 1, :] += -(psink * delta)
                ps.append(p.astype(bf16))
                dss.append(ds.astype(bf16))
            dqns = [_bdot_tn(kn, ds) * scale for ds in dss]
            for ds, qn, p, d_o in zip(dss, qns, ps, d_os):
                dkn = dkn + _bdot_nt(ds, qn) * scale
                dvv = dvv + _bdot_nt(p, d_o)
            for h, dqn, rq, qhat in zip(heads, dqns, rqs, qhats):
                dqg_acc[...] += dqn * qhat
                t = dqn * qg
                dqt_scr[h * HEAD_DIM:(h + 1) * HEAD_DIM, :] = rq * (t - qhat * jnp.mean(t * qhat, axis=0, keepdims=True))
            dkn_cur = dkn[BLOCK:] + dkn_scr[:, sl]
            dkn_scr[:, sl] = dkn[:BLOCK]
            khat_c, rk_c = khat[BLOCK:], rk[BLOCK:]
            dkg_ref[...] += jnp.sum(dkn_cur * khat_c, axis=0, keepdims=True)
            dk_ref[:, sl] = _rms_bwd(dkn_cur, khat_c, rk_c, kg)
            dv_ref[:, sl] = dvv[BLOCK:] + dv_scr[:, sl]
            dv_scr[:, sl] = dvv[:BLOCK]
        dq_ref[...] = dqt_scr[...].T

        @pl.when(i == N_BLOCKS - 1)
        def _():
            dsink_ref[...] = jnp.sum(dsink_acc[...], axis=1, keepdims=True)
            dqg_ref[...] = jnp.sum(dqg_acc[...], axis=1, keepdims=True)

    blk = lambda i: N_BLOCKS - 1 - i
    prev = lambda i: jnp.maximum(N_BLOCKS - 2 - i, 0)
    return pl.pallas_call(
        body, name="attn_bwd", grid=(N_BLOCKS,),
        in_specs=[pl.BlockSpec((BLOCK, D_ATTN), lambda i: (blk(i), 0)),
                  pl.BlockSpec((BLOCK, D_KV), lambda i: (blk(i), kcol)), pl.BlockSpec((BLOCK, D_KV), lambda i: (prev(i), kcol)),
                  pl.BlockSpec((BLOCK, D_KV), lambda i: (blk(i), vcol)), pl.BlockSpec((BLOCK, D_KV), lambda i: (prev(i), vcol)),
                  pl.BlockSpec((BLOCK, D_ATTN), lambda i: (blk(i), 0)),
                  _full((HEAD_DIM, 1)), _full((1, HEAD_DIM)), pl.BlockSpec(memory_space=pltpu.SMEM),
                  _full((N_Q_HEADS, 2 * BLOCK, BLOCK))],
        out_specs=[pl.BlockSpec((BLOCK, D_ATTN), lambda i: (blk(i), 0)), pl.BlockSpec((BLOCK, D_KV), lambda i: (blk(i), 0)),
                   pl.BlockSpec((BLOCK, D_KV), lambda i: (blk(i), 0)), _full((N_Q_HEADS, 2 * BLOCK, BLOCK)),
                   _full((N_Q_HEADS, 1)), _full((HEAD_DIM, 1)), _full((1, HEAD_DIM))],
        out_shape=[jax.ShapeDtypeStruct((SEQ, D_ATTN), f32), jax.ShapeDtypeStruct((SEQ, D_KV), f32),
                   jax.ShapeDtypeStruct((SEQ, D_KV), f32), jax.ShapeDtypeStruct((N_Q_HEADS, 2 * BLOCK, BLOCK), f32),
                   jax.ShapeDtypeStruct((N_Q_HEADS, 1), f32), jax.ShapeDtypeStruct((HEAD_DIM, 1), f32),
                   jax.ShapeDtypeStruct((1, HEAD_DIM), f32)],
        scratch_shapes=[pltpu.VMEM((BLOCK, D_KV), f32), pltpu.VMEM((BLOCK, D_KV), f32), pltpu.VMEM((D_ATTN, BLOCK), f32),
                        pltpu.VMEM((N_Q_HEADS, BLOCK), f32), pltpu.VMEM((HEAD_DIM, BLOCK), f32)],
        compiler_params=_params("arbitrary"),
    )(proj, proj, proj, proj, proj, d_out, q_gain_col, k_gain, sinks, bias_t)


def _shift_down(u, s, row):
    if s == 0:
        return u
    return jnp.where(row >= s, pltpu.roll(u, s, 0), 0.0)


def _shift_up(u, s, row):
    if s == 0:
        return u
    return jnp.where(row < SEQ - s, pltpu.roll(u, SEQ - s, 0), 0.0)


def conv_fwd(proj, conv_w, conv_b):
    xcol = OFF_X // LANE

    def body(u_ref, w_ref, b_ref, o_ref):
        u = u_ref[...]
        row = lax.broadcasted_iota(jnp.int32, u.shape, 0)
        pre = b_ref[...] + jnp.zeros_like(u)
        for k in range(CONV_WIDTH):
            pre = pre + w_ref[k:k + 1, :] * _shift_down(u, CONV_WIDTH - 1 - k, row)
        o_ref[...] = pre * _sigmoid(pre)

    return pl.pallas_call(
        body, name="conv_fwd", grid=(D_CONV // LANE,),
        in_specs=[pl.BlockSpec((SEQ, LANE), lambda j: (0, xcol + j)), pl.BlockSpec((CONV_WIDTH, LANE), lambda j: (0, j)),
                  pl.BlockSpec((1, LANE), lambda j: (0, j))],
        out_specs=pl.BlockSpec((SEQ, LANE), lambda j: (0, j)),
        out_shape=jax.ShapeDtypeStruct((SEQ, D_CONV), f32),
        compiler_params=_params("arbitrary"),
    )(proj, conv_w, conv_b)


def conv_bwd(proj, d_act, conv_w, conv_b):
    xcol = OFF_X // LANE

    def body(u_ref, da_ref, w_ref, b_ref, du_ref, dw_ref, db_ref):
        u = u_ref[...]
        row = lax.broadcasted_iota(jnp.int32, u.shape, 0)
        shifted = [_shift_down(u, CONV_WIDTH - 1 - k, row) for k in range(CONV_WIDTH)]
        pre = b_ref[...] + jnp.zeros_like(u)
        for k in range(CONV_WIDTH):
            pre = pre + w_ref[k:k + 1, :] * shifted[k]
        sg = _sigmoid(pre)
        dpre = da_ref[...] * (sg * (1.0 + pre * (1.0 - sg)))
        db_ref[...] = jnp.sum(dpre, axis=0, keepdims=True)
        du = jnp.zeros_like(u)
        for k in range(CONV_WIDTH):
            dw_ref[k:k + 1, :] = jnp.sum(dpre * shifted[k], axis=0, keepdims=True)
            du = du + w_ref[k:k + 1, :] * _shift_up(dpre, CONV_WIDTH - 1 - k, row)
        du_ref[...] = du

    return pl.pallas_call(
        body, name="conv_bwd", grid=(D_CONV // LANE,),
        in_specs=[pl.BlockSpec((SEQ, LANE), lambda j: (0, xcol + j)), pl.BlockSpec((SEQ, LANE), lambda j: (0, j)),
                  pl.BlockSpec((CONV_WIDTH, LANE), lambda j: (0, j)), pl.BlockSpec((1, LANE), lambda j: (0, j))],
        out_specs=[pl.BlockSpec((SEQ, LANE), lambda j: (0, j)), pl.BlockSpec((CONV_WIDTH, LANE), lambda j: (0, j)),
                   pl.BlockSpec((1, LANE), lambda j: (0, j))],
        out_shape=[jax.ShapeDtypeStruct((SEQ, D_CONV), f32), jax.ShapeDtypeStruct((CONV_WIDTH, D_CONV), f32),
                   jax.ShapeDtypeStruct((1, D_CONV), f32)],
        compiler_params=_params("arbitrary"),
    )(proj, d_act, conv_w, conv_b)


def _ssd_chunk_common(dt_raw, dtb, alog):
    row = lax.broadcasted_iota(jnp.int32, (CHUNK, CHUNK), 0)
    col = lax.broadcasted_iota(jnp.int32, (CHUNK, CHUNK), 1)
    tri = (row >= col).astype(f32)
    strict = (row > col).astype(f32)
    dtp = _softplus(dt_raw + dtb)
    a_row = -jnp.exp(alog)
    d_a = dtp * a_row
    cs = _hdot(tri, d_a)
    cs_last = cs[CHUNK - 1:CHUNK, :]
    return row, col, dtp, a_row, cs, cs.T, cs_last


def _seg_decay(cs, cs_t, hd, row, col):
    seg = cs[:, hd:hd + 1] - cs_t[hd:hd + 1, :]
    return jnp.where(row >= col, jnp.exp(seg), 0.0)


GROUP_W = HEADS_PER_GROUP * SSM_HEAD_DIM


def _group_indicator(g):
    j = lax.broadcasted_iota(jnp.int32, (GROUP_W, LANE), 0)
    lane = lax.broadcasted_iota(jnp.int32, (GROUP_W, LANE), 1)
    return (lane == g * HEADS_PER_GROUP + j // SSM_HEAD_DIM).astype(f32)


def _hdot_nt(a, b):
    return lax.dot_general(a, b, (((1,), (1,)), ((), ())), precision=lax.Precision.HIGHEST, preferred_element_type=f32)


def ssd_fwd_g(act, proj, dt_bias, a_log, d_skip, norm_g):
    zcol, dtcol = OFF_Z // D_SSM, OFF_DT // LANE

    def body(act_ref, z_ref, dt_ref, dtb_ref, alog_ref, dsk_ref, ng_ref, out_ref, ypre_ref, st_ref, state):
        c = pl.program_id(0)

        @pl.when(c == 0)
        def _():
            state[...] = jnp.zeros_like(state)

        row, col, dtp, a_row, cs, cs_t, cs_last = _ssd_chunk_common(dt_ref[...], dtb_ref[...], alog_ref[...])
        e_cs = jnp.exp(cs)
        dte = jnp.exp(cs_last - cs)
        rows8 = jnp.concatenate([jnp.exp(cs_last), dsk_ref[...], jnp.zeros((6, LANE), f32)], axis=0)
        z = z_ref[...]
        sz = z * _sigmoid(z)
        ng = ng_ref[...]
        for g in range(SSM_GROUPS):
            gs = slice(g * GROUP_W, (g + 1) * GROUP_W)
            ind = _group_indicator(g)
            xg = act_ref[:, gs]
            bg = act_ref[:, D_SSM + g * SSM_STATE:D_SSM + (g + 1) * SSM_STATE]
            cg = act_ref[:, D_SSM + D_BC + g * SSM_STATE:D_SSM + D_BC + (g + 1) * SSM_STATE]
            dt_e, e_e, dte_e = _hdot_nt(dtp, ind), _hdot_nt(e_cs, ind), _hdot_nt(dte, ind)
            rows_e = _hdot_nt(rows8, ind)
            ecl_e, dsk_e = rows_e[0:1], rows_e[1:2]
            xdt = xg * dt_e
            prev = state[g]
            st_ref[0, g] = prev
            cb = _bdot_nt(cg, bg)
            goff = _bdot(cg, prev)
            snew = _bdot_tn(bg, xdt * dte_e)
            heads = range(g * HEADS_PER_GROUP, (g + 1) * HEADS_PER_GROUP)
            ms = [cb * _seg_decay(cs, cs_t, hd, row, col) for hd in heads]
            yd = [_bdot(m, xdt[:, r * SSM_HEAD_DIM:(r + 1) * SSM_HEAD_DIM]) for r, m in enumerate(ms)]
            y = jnp.concatenate(yd, axis=1) + e_e * goff + xg * dsk_e
            state[g] = prev * ecl_e + snew
            ypre_ref[:, gs] = y
            part = y * sz[:, gs]
            out_ref[:, gs] = part * _rms(part) * ng[:, gs]

    return pl.pallas_call(
        body, name="ssd_fwd", grid=(N_CHUNKS,),
        in_specs=[pl.BlockSpec((CHUNK, D_CONV), lambda c: (c, 0)), pl.BlockSpec((CHUNK, D_SSM), lambda c: (c, zcol)),
                  pl.BlockSpec((CHUNK, LANE), lambda c: (c, dtcol)), _full((1, LANE)), _full((1, LANE)), _full((1, LANE)),
                  _full((1, D_SSM))],
        out_specs=[pl.BlockSpec((CHUNK, D_SSM), lambda c: (c, 0)), pl.BlockSpec((CHUNK, D_SSM), lambda c: (c, 0)),
                   pl.BlockSpec((1, SSM_GROUPS, SSM_STATE, GROUP_W), lambda c: (c, 0, 0, 0))],
        out_shape=[jax.ShapeDtypeStruct((SEQ, D_SSM), f32), jax.ShapeDtypeStruct((SEQ, D_SSM), f32),
                   jax.ShapeDtypeStruct((N_CHUNKS, SSM_GROUPS, SSM_STATE, GROUP_W), f32)],
        scratch_shapes=[pltpu.VMEM((SSM_GROUPS, SSM_STATE, GROUP_W), f32)],
        compiler_params=_params("arbitrary"),
    )(act, proj, proj, dt_bias, a_log, d_skip, norm_g)


def ssd_bwd_g(act, proj, ypre, states, d_out, dt_bias, a_log, d_skip, norm_g):
    zcol, dtcol = OFF_Z // D_SSM, OFF_DT // LANE

    def body(act_ref, z_ref, dt_ref, ypre_ref, st_ref, do_ref, dtb_ref, alog_ref, dsk_ref, ng_ref,
             dact_ref, ddt_ref, dz_ref, dng_ref, dpar_ref, dstate):
        i = pl.program_id(0)

        @pl.when(i == 0)
        def _():
            for ref in (dng_ref, dpar_ref, dstate):
                ref[...] = jnp.zeros_like(ref)

        row, col, dtp, a_row, cs, cs_t, cs_last = _ssd_chunk_common(dt_ref[...], dtb_ref[...], alog_ref[...])
        upper = (row <= col).astype(f32)
        lane = lax.broadcasted_iota(jnp.int32, (CHUNK, LANE), 1)
        rowl = lax.broadcasted_iota(jnp.int32, (CHUNK, LANE), 0)
        e_cs = jnp.exp(cs)
        dte = jnp.exp(cs_last - cs)
        ecl = jnp.exp(cs_last)
        rows8 = jnp.concatenate([ecl, dsk_ref[...], jnp.zeros((6, LANE), f32)], axis=0)
        z = z_ref[...]
        sgz = _sigmoid(z)
        sz = z * sgz
        ng = ng_ref[...]
        ddt_mat = jnp.zeros((CHUNK, LANE), f32)
        dcs_mat = jnp.zeros((CHUNK, LANE), f32)
        dcs_t = jnp.zeros((LANE, CHUNK), f32)
        dcsl_row = jnp.zeros((1, LANE), f32)
        dd_row = jnp.zeros((1, LANE), f32)
        for g in range(SSM_GROUPS):
            gs = slice(g * GROUP_W, (g + 1) * GROUP_W)
            bsl = slice(D_SSM + g * SSM_STATE, D_SSM + (g + 1) * SSM_STATE)
            csl = slice(D_SSM + D_BC + g * SSM_STATE, D_SSM + D_BC + (g + 1) * SSM_STATE)
            ind = _group_indicator(g)
            y = ypre_ref[:, gs]
            part = y * sz[:, gs]
            r = _rms(part)
            yhat = part * r
            d_o = do_ref[:, gs]
            dng_ref[:, gs] += jnp.sum(d_o * yhat, axis=0, keepdims=True)
            dyz = _rms_bwd(d_o, yhat, r, ng[:, gs])
            dy = dyz * sz[:, gs]
            dz_ref[:, gs] = dyz * y * (sgz[:, gs] * (1.0 + z[:, gs] * (1.0 - sgz[:, gs])))

            xg = act_ref[:, gs]
            bg = act_ref[:, bsl]
            cg = act_ref[:, csl]
            dt_e, e_e, dte_e = _hdot_nt(dtp, ind), _hdot_nt(e_cs, ind), _hdot_nt(dte, ind)
            rows_e = _hdot_nt(rows8, ind)
            ecl_e, dsk_e = rows_e[0:1], rows_e[1:2]
            xdt = xg * dt_e
            prev = st_ref[0, g]
            dh = dstate[g]
            heads = range(g * HEADS_PER_GROUP, (g + 1) * HEADS_PER_GROUP)
            hsl = [slice(r_ * SSM_HEAD_DIM, (r_ + 1) * SSM_HEAD_DIM) for r_ in range(HEADS_PER_GROUP)]
            cb = _bdot_nt(cg, bg)
            lms = [_seg_decay(cs, cs_t, hd, row, col) for hd in heads]
            ms = [cb * lm for lm in lms]
            gmat = _bdot(cg, prev)
            dgm = dy * e_e
            dcg = _bdot_nt(dgm, prev)
            dprev = _bdot_tn(cg, dgm)
            dbg = _bdot_nt(xdt * dte_e, dh)
            dw = _bdot(bg, dh)
            dms = [_bdot_nt(dy[:, s_], xdt[:, s_]) for s_ in hsl]
            dxdts = [_bdot_tn(m, dy[:, s_]) for m, s_ in zip(ms, hsl)]
            dxdt = jnp.concatenate(dxdts, axis=1) + dw * dte_e
            dact_ref[:, gs] = dy * dsk_e + dxdt * dt_e
            dstate[g] = dprev + dh * ecl_e
            dcb = jnp.zeros((CHUNK, CHUNK), f32)
            for hd, dm, lm, m in zip(heads, dms, lms, ms):
                dcb = dcb + dm * lm
                dseg = dm * m
                dcs_mat = dcs_mat + jnp.where(lane == hd, jnp.sum(dseg, axis=1, keepdims=True), 0.0)
                dcs_t = jnp.where(row == hd, jnp.sum(dseg, axis=0, keepdims=True), dcs_t)
            dact_ref[:, bsl] = dbg + _bdot_tn(dcb, cg)
            dact_ref[:, csl] = dcg + _bdot(dcb, bg)
            ddte = _hdot(dw * xdt, ind) * dte
            dcs_mat = dcs_mat + _hdot(dy * gmat, ind) * e_cs - ddte
            ddt_mat = ddt_mat + _hdot(dxdt * xg, ind)
            dcsl_row = dcsl_row + jnp.sum(ddte, axis=0, keepdims=True) + jnp.sum(_hdot(dh * prev, ind), axis=0, keepdims=True) * ecl
            dd_row = dd_row + jnp.sum(_hdot(dy * xg, ind), axis=0, keepdims=True)
        dcs_mat = dcs_mat - dcs_t.T + jnp.where(rowl == CHUNK - 1, dcsl_row, 0.0)
        dda = _hdot(upper, dcs_mat)
        ddt_mat = ddt_mat + dda * a_row
        da_row = jnp.sum(dda * dtp, axis=0, keepdims=True)
        ddt_raw = ddt_mat * _sigmoid(dt_ref[...] + dtb_ref[...])
        ddt_ref[...] = ddt_raw
        dpar_ref[0:1, :] += jnp.sum(ddt_raw, axis=0, keepdims=True)
        dpar_ref[1:2, :] += da_row * a_row
        dpar_ref[2:3, :] += dd_row

    blk = lambda i: N_CHUNKS - 1 - i
    return pl.pallas_call(
        body, name="ssd_bwd", grid=(N_CHUNKS,),
        in_specs=[pl.BlockSpec((CHUNK, D_CONV), lambda i: (blk(i), 0)), pl.BlockSpec((CHUNK, D_SSM), lambda i: (blk(i), zcol)),
                  pl.BlockSpec((CHUNK, LANE), lambda i: (blk(i), dtcol)), pl.BlockSpec((CHUNK, D_SSM), lambda i: (blk(i), 0)),
                  pl.BlockSpec((1, SSM_GROUPS, SSM_STATE, GROUP_W), lambda i: (blk(i), 0, 0, 0)),
                  pl.BlockSpec((CHUNK, D_SSM), lambda i: (blk(i), 0)),
                  _full((1, LANE)), _full((1, LANE)), _full((1, LANE)), _full((1, D_SSM))],
        out_specs=[pl.BlockSpec((CHUNK, D_CONV), lambda i: (blk(i), 0)), pl.BlockSpec((CHUNK, LANE), lambda i: (blk(i), 0)),
                   pl.BlockSpec((CHUNK, D_SSM), lambda i: (blk(i), 0)), _full((1, D_SSM)), _full((8, LANE))],
        out_shape=[jax.ShapeDtypeStruct((SEQ, D_CONV), f32), jax.ShapeDtypeStruct((SEQ, LANE), f32),
                   jax.ShapeDtypeStruct((SEQ, D_SSM), f32), jax.ShapeDtypeStruct((1, D_SSM), f32),
                   jax.ShapeDtypeStruct((8, LANE), f32)],
        scratch_shapes=[pltpu.VMEM((SSM_GROUPS, SSM_STATE, GROUP_W), f32)],
        compiler_params=_params("arbitrary"),
    )(act, proj, proj, ypre, states, d_out, dt_bias, a_log, d_skip, norm_g)


def out_fwd(x, attn, ssm, w_out, tm=512):
    def body(x_ref, a_ref, s_ref, w_ref, o_ref):
        o_ref[...] = x_ref[...] + _bdot(a_ref[...], w_ref[:D_ATTN, :]) + _bdot(s_ref[...], w_ref[D_ATTN:, :])

    tok = lambda w_: pl.BlockSpec((tm, w_), lambda i: (i, 0))
    return pl.pallas_call(
        body, name="out_fwd", grid=(SEQ // tm,),
        in_specs=[tok(D_MODEL), tok(D_ATTN), tok(D_SSM), _full((D_MODEL, D_MODEL))],
        out_specs=tok(D_MODEL), out_shape=jax.ShapeDtypeStruct((SEQ, D_MODEL), f32),
        compiler_params=_params("arbitrary"),
    )(x, attn, ssm, w_out)


def out_bwd(dx1, attn, ssm, w_out, tm=512):
    nt = SEQ // tm

    def body(d_ref, a_ref, s_ref, w_ref, da_ref, ds_ref, dw_ref, dw16_ref):
        i = pl.program_id(0)

        @pl.when(i == 0)
        def _():
            dw_ref[...] = jnp.zeros_like(dw_ref)

        d = d_ref[...].astype(bf16)
        dcat = _bdot_nt(d, w_ref[...])
        da_ref[...] = dcat[:, :D_ATTN]
        ds_ref[...] = dcat[:, D_ATTN:]
        dw_ref[:D_ATTN, :] += _bdot_tn(a_ref[...], d)
        dw_ref[D_ATTN:, :] += _bdot_tn(s_ref[...], d)

        @pl.when(i == nt - 1)
        def _():
            dw16_ref[...] = dw_ref[...].astype(bf16)

    tok = lambda w_: pl.BlockSpec((tm, w_), lambda i: (i, 0))
    return pl.pallas_call(
        body, name="out_bwd", grid=(nt,),
        in_specs=[tok(D_MODEL), tok(D_ATTN), tok(D_SSM), _full((D_MODEL, D_MODEL))],
        out_specs=[tok(D_ATTN), tok(D_SSM), _full((D_MODEL, D_MODEL)), _full((D_MODEL, D_MODEL))],
        out_shape=[jax.ShapeDtypeStruct((SEQ, D_ATTN), f32), jax.ShapeDtypeStruct((SEQ, D_SSM), f32),
                   jax.ShapeDtypeStruct((D_MODEL, D_MODEL), f32), jax.ShapeDtypeStruct((D_MODEL, D_MODEL), bf16)],
        compiler_params=_params("arbitrary"),
    )(dx1, attn, ssm, w_out)


def mlp_fwd(x1, g, w_up, w_down, tm=1024):
    def body(x_ref, g_ref, wu_ref, wd_ref, o_ref, u_ref, h_scr):
        j = pl.program_id(1)

        @pl.when(j == 0)
        def _():
            xv = x_ref[...]
            h_scr[...] = (xv * _rms(xv) * g_ref[...]).astype(bf16)
            o_ref[...] = xv

        u = jnp.dot(h_scr[...], wu_ref[...], preferred_element_type=f32)
        u_ref[...] = u
        a = jnp.square(jnp.maximum(u, 0.0))
        o_ref[...] += _bdot(a, wd_ref[...])

    return pl.pallas_call(
        body, name="mlp_fwd", grid=(SEQ // tm, N_CHIPS),
        in_specs=[pl.BlockSpec((tm, D_MODEL), lambda i, j: (i, 0)), _full((1, D_MODEL)),
                  pl.BlockSpec((None, D_MODEL, FF_TILE), lambda i, j: (j, 0, 0)),
                  pl.BlockSpec((None, FF_TILE, D_MODEL), lambda i, j: (j, 0, 0))],
        out_specs=[pl.BlockSpec((tm, D_MODEL), lambda i, j: (i, 0)), pl.BlockSpec((tm, FF_TILE), lambda i, j: (i, j))],
        out_shape=[jax.ShapeDtypeStruct((SEQ, D_MODEL), f32), jax.ShapeDtypeStruct((SEQ, D_FF), f32)],
        scratch_shapes=[pltpu.VMEM((tm, D_MODEL), bf16)],
        compiler_params=_params("arbitrary", "arbitrary"),
    )(x1, g, w_up, w_down)


def mlp_bwd_data(dx2, u, x1, g, w_up, w_down, tm=1024):
    def body(d_ref, u_ref, x_ref, g_ref, wu_ref, wd_ref, dx_ref, du_ref, dg_ref, dh_scr):
        i, j = pl.program_id(0), pl.program_id(1)

        @pl.when(jnp.logical_and(i == 0, j == 0))
        def _():
            dg_ref[...] = jnp.zeros_like(dg_ref)

        @pl.when(j == 0)
        def _():
            dh_scr[...] = jnp.zeros_like(dh_scr)

        da = _bdot_nt(d_ref[...], wd_ref[...])
        du = (da * (2.0 * jnp.maximum(u_ref[...], 0.0))).astype(bf16)
        du_ref[...] = du
        dh_scr[...] += _bdot_nt(du, wu_ref[...])

        @pl.when(j == N_CHIPS - 1)
        def _():
            xv = x_ref[...]
            r = _rms(xv)
            xhat = xv * r
            dh = dh_scr[...]
            dg_ref[...] += jnp.sum(dh * xhat, axis=0, keepdims=True)
            dx_ref[...] = d_ref[...] + _rms_bwd(dh, xhat, r, g_ref[...])

    return pl.pallas_call(
        body, name="mlp_bwd_data", grid=(SEQ // tm, N_CHIPS),
        in_specs=[pl.BlockSpec((tm, D_MODEL), lambda i, j: (i, 0)), pl.BlockSpec((tm, FF_TILE), lambda i, j: (i, j)),
                  pl.BlockSpec((tm, D_MODEL), lambda i, j: (i, 0)), _full((1, D_MODEL)),
                  pl.BlockSpec((None, D_MODEL, FF_TILE), lambda i, j: (j, 0, 0)),
                  pl.BlockSpec((None, FF_TILE, D_MODEL), lambda i, j: (j, 0, 0))],
        out_specs=[pl.BlockSpec((tm, D_MODEL), lambda i, j: (i, 0)), pl.BlockSpec((tm, FF_TILE), lambda i, j: (i, j)),
                   _full((1, D_MODEL))],
        out_shape=[jax.ShapeDtypeStruct((SEQ, D_MODEL), f32), jax.ShapeDtypeStruct((SEQ, D_FF), bf16),
                   jax.ShapeDtypeStruct((1, D_MODEL), f32)],
        scratch_shapes=[pltpu.VMEM((tm, D_MODEL), f32)],
        compiler_params=_params("arbitrary", "arbitrary"),
    )(dx2, u, x1, g, w_up, w_down)


def mlp_bwd_weights(dx2, u, du, x1, g, tm=512):
    nt = SEQ // tm

    def body(d_ref, u_ref, du_ref, x_ref, g_ref, dwu_ref, dwd_ref, dwu16_ref, dwd16_ref, h_scr, d_scr):
        j, i = pl.program_id(0), pl.program_id(1)

        @pl.when(j == 0)
        def _():
            xv = x_ref[...]
            h_scr[i] = (xv * _rms(xv) * g_ref[...]).astype(bf16)
            d_scr[i] = d_ref[...].astype(bf16)

        @pl.when(i == 0)
        def _():
            dwu_ref[...] = jnp.zeros_like(dwu_ref)
            dwd_ref[...] = jnp.zeros_like(dwd_ref)

        dwu_ref[...] += _bdot_tn(h_scr[i], du_ref[...])
        a = jnp.square(jnp.maximum(u_ref[...], 0.0))
        dwd_ref[...] += _bdot_tn(a, d_scr[i])

        @pl.when(i == nt - 1)
        def _():
            dwu16_ref[...] = dwu_ref[...].astype(bf16)
            dwd16_ref[...] = dwd_ref[...].astype(bf16)

    up = pl.BlockSpec((None, D_MODEL, FF_TILE), lambda j, i: (j, 0, 0))
    down = pl.BlockSpec((None, FF_TILE, D_MODEL), lambda j, i: (j, 0, 0))
    first_pass = pl.BlockSpec((tm, D_MODEL), lambda j, i: (jnp.where(j == 0, i, nt - 1), 0))
    return pl.pallas_call(
        body, name="mlp_bwd_weights", grid=(N_CHIPS, nt),
        in_specs=[first_pass, pl.BlockSpec((tm, FF_TILE), lambda j, i: (i, j)),
                  pl.BlockSpec((tm, FF_TILE), lambda j, i: (i, j)), first_pass, _full((1, D_MODEL))],
        out_specs=[up, down, up, down],
        out_shape=[jax.ShapeDtypeStruct((N_CHIPS, D_MODEL, FF_TILE), f32), jax.ShapeDtypeStruct((N_CHIPS, FF_TILE, D_MODEL), f32),
                   jax.ShapeDtypeStruct((N_CHIPS, D_MODEL, FF_TILE), bf16), jax.ShapeDtypeStruct((N_CHIPS, FF_TILE, D_MODEL), bf16)],
        scratch_shapes=[pltpu.VMEM((nt, tm, D_MODEL), bf16), pltpu.VMEM((nt, tm, D_MODEL), bf16)],
        compiler_params=_params("arbitrary", "arbitrary"),
    )(dx2, u, du, x1, g)


def loss_head(y, target, tm=512):
    def body(y_ref, t_ref, dy_ref, l_ref):
        @pl.when(pl.program_id(0) == 0)
        def _():
            l_ref[...] = jnp.zeros_like(l_ref)

        d = y_ref[...] - t_ref[...]
        dy_ref[...] = d * (1.0 / D_MODEL)
        part = jnp.sum(jnp.mean(d * d, axis=-1, keepdims=True), axis=0, keepdims=True)
        l_ref[...] += 0.5 * part

    tok = pl.BlockSpec((tm, D_MODEL), lambda i: (i, 0))
    return pl.pallas_call(
        body, name="loss_head", grid=(SEQ // tm,), in_specs=[tok, tok], out_specs=[tok, _full((1, 1))],
        out_shape=[jax.ShapeDtypeStruct((SEQ, D_MODEL), f32), jax.ShapeDtypeStruct((1, 1), f32)],
        compiler_params=_params("arbitrary"),
    )(y, target)


def _pad_lane(v):
    return jnp.pad(v, (0, LANE - v.shape[0]))[None, :]


def local_step(x, target, w, prov):
    bucket = jnp.asarray(_bucket_table().T)
    bias = bias_build(w["rel_bias"], bucket)
    saved = []
    for l in range(DEPTH):
        g_mix = w["mix_norm_g"][l][None, :] + prov.stage(("begin", l), x)
        w_in = prov.w_in(l, x)
        proj = in_fwd(x, g_mix, w_in)
        conv_b = w["conv_b"][l][None, :]
        act = conv_fwd(proj, w["conv_w"][l], conv_b)
        dtb = _pad_lane(w["dt_bias"][l]) + prov.stage(("mid", l), act)
        alog, dsk = _pad_lane(w["a_log"][l]), _pad_lane(w["d_skip"][l])
        ng = w["ssm_norm_g"][l][None, :]
        ssm, ypre, states = ssd_fwd_g(act, proj, dtb, alog, dsk, ng)
        qg, kg = w["q_gain"][l][:, None] + 0.0 * ssm[:1, :1], w["k_gain"][l][None, :]
        attn = attn_fwd_t(proj, qg, kg, w["sinks"][l], bias)
        tok = prov.stage(("pre_out", l), attn)
        w_out = prov.w_out(l, attn) + jnp.asarray(tok, bf16)
        x1 = out_fwd(x, attn, ssm, w_out)
        g_mlp = w["mlp_norm_g"][l][None, :] + prov.stage(("pre_mlp", l), x1)
        w_up, w_down = prov.mlp(l, x1)
        x2, u = mlp_fwd(x1, g_mlp, w_up, w_down)
        saved.append(dict(x=x, proj=proj, attn=attn, act=act, ssm=ssm, ypre=ypre, states=states, x1=x1, u=u,
                          g_mix=g_mix, qg=qg, kg=kg, conv_b=conv_b, dtb=dtb, alog=alog, dsk=dsk, ng=ng, g_mlp=g_mlp,
                          w_in=w_in, w_out=w_out, w_up=w_up, w_down=w_down))
        x = x2
    dx, loss = loss_head(x, target)
    grads = [None] * DEPTH
    dbands = [None] * DEPTH
    tok = 0.0
    for l in reversed(range(DEPTH)):
        s = saved[l]
        g_mlp = s["g_mlp"] + tok
        dx1, du, dg_mlp = mlp_bwd_data(dx, s["u"], s["x1"], g_mlp, s["w_up"], s["w_down"])
        dw_up, dw_down, dw_up16, dw_down16 = mlp_bwd_weights(dx, s["u"], du, s["x1"], g_mlp)
        tok = prov.grads(("mlp", l), dict(w_up=(dw_up, dw_up16), w_down=(dw_down, dw_down16)), dw_down)
        dattn, dssm, dw_out, dw_out16 = out_bwd(dx1, s["attn"], s["ssm"], s["w_out"])
        dact, ddt, dz, dng, dpar = ssd_bwd_g(s["act"], s["proj"], s["ypre"], s["states"], dssm, s["dtb"] + tok, s["alog"],
                                           s["dsk"], s["ng"])
        conv_b = s["conv_b"] + prov.stage(("bwd_mid", l), dact)
        dxbc, dconv_w, dconv_b = conv_bwd(s["proj"], dact, w["conv_w"][l], conv_b)
        dq, dk, dv, dband, dsink, dqg, dkg = attn_bwd_t(s["proj"], dattn, s["qg"], s["kg"], w["sinks"][l], bias)
        dx, dw_in, dg_mix = in_bwd(dq, dz, dxbc, dk, dv, ddt, s["x"], s["g_mix"], s["w_in"], dx1)
        tok = prov.grads(("mix", l), dict(w_in=split_w_in_grad(dw_in), w_out=(dw_out, dw_out16)), dx)
        dbands[l] = dband
        grads[l] = dict(mix_norm_g=dg_mix[0], q_gain=dqg[:, 0], k_gain=dkg[0], sinks=dsink[:, 0],
                        conv_w=dconv_w, conv_b=dconv_b[0], dt_bias=dpar[0, :SSM_HEADS], a_log=dpar[1, :SSM_HEADS],
                        d_skip=dpar[2, :SSM_HEADS], ssm_norm_g=dng[0], mlp_norm_g=dg_mlp[0])
    out = {k: jnp.stack([grads[l][k] for l in range(DEPTH)]) for k in grads[0]}
    out["rel_bias"] = bias_bwd(dbands[0], dbands[1], bucket)[:, :N_Q_HEADS]
    return loss, dx, out, tok


MESH = pl.DeviceIdType.MESH
HBM = pl.BlockSpec(memory_space=pltpu.HBM)
N_PEER_CHIPS = N_CHIPS - 1
N_DEVICES = 8


def _coords():
    return lax.axis_index("x"), lax.axis_index("y"), lax.axis_index("c")


def _peer_chips(x, y):
    return [(1 - x, y), (x, 1 - y), (1 - x, 1 - y)]


def _remote(src, dst, send_sem, recv_sem, device):
    return pltpu.make_async_remote_copy(src_ref=src, dst_ref=dst, send_sem=send_sem, recv_sem=recv_sem,
                                        device_id=device, device_id_type=MESH)


SEM = pl.BlockSpec(memory_space=pltpu.SEMAPHORE)
ANY = pl.BlockSpec(memory_space=pl.ANY)
DATAFLOW = pltpu.SideEffectType.DATAFLOW_SIDE_EFFECTING


def _gather_copies(kind, src_refs, land_refs, ssem, rsem):
    x, y, c = _coords()
    k_me = 2 * x + y
    n = len(land_refs)
    cps = []
    for p, land in enumerate(land_refs):
        hr = land.shape[1] // 2
        rows = pl.ds(c * hr, hr)
        for j, chip in enumerate(_peer_chips(x, y)):
            i = 3 * p + j
            if kind == "ici":
                cps.append(_remote(src_refs[p].at[rows, :], land.at[k_me, rows, :], ssem.at[i], rsem.at[i], (*chip, c)))
            else:
                got = land.at[2 * chip[0] + chip[1], rows, :]
                cps.append(_remote(got, got, ssem.at[i], rsem.at[i], (x, y, 1 - c)))
        if kind == "relay":
            cps.append(_remote(src_refs[p], land.at[k_me], ssem.at[3 * n + p], rsem.at[3 * n + p], (x, y, 1 - c)))
    return cps


def gather_now(srcs, conv):
    n = len(srcs)

    def body(*refs):
        src_refs, conv_ref = refs[:n], refs[n]
        lands, gconv = refs[n + 1:2 * n + 1], refs[2 * n + 1]
        ssem, rsem, fsem, frsem, csem, crsem = refs[2 * n + 2:]
        x, y, c = _coords()
        k_me = 2 * x + y
        targets = [(*chip, c) for chip in _peer_chips(x, y)] + [(x, y, 1 - c)]
        ici = _gather_copies("ici", src_refs, lands, ssem, rsem)
        relay = _gather_copies("relay", src_refs, lands, fsem, frsem)
        passed = [cp for i, cp in enumerate(relay) if i % 4 != 3]
        own = relay[3::4]
        conv_cps = [_remote(conv_ref, gconv.at[k_me], csem.at[j], crsem.at[j], t) for j, t in enumerate(targets)]
        for cp in ici + conv_cps + own:
            cp.start()
        for cp, fw in zip(ici, passed):
            cp.wait_recv()
            fw.start()
        for cp in conv_cps + relay:
            cp.wait_recv()
        for cp in ici + relay + conv_cps:
            cp.wait_send()

    out_shape = [jax.ShapeDtypeStruct((N_CHIPS,) + s.shape, s.dtype) for s in srcs]
    out_shape.append(jax.ShapeDtypeStruct((N_CHIPS,) + conv.shape, conv.dtype))
    sems = lambda k: pltpu.SemaphoreType.DMA((k,))
    return pl.pallas_call(
        body, name="gather_now", out_shape=out_shape, in_specs=[HBM] * (n + 1), out_specs=[HBM] * (n + 1),
        scratch_shapes=[sems(3 * n), sems(3 * n), sems(4 * n), sems(4 * n), sems(N_CHIPS), sems(N_CHIPS)],
    )(*srcs, conv)


def _gather_maker(kind, n_src):
    def make(refs, ssem, rsem):
        cps = _gather_copies(kind, refs[:n_src], refs[n_src:], ssem, rsem)
        return cps, cps
    return make


def _scatter_maker(n):
    def make(refs, ssem, rsem):
        x, y, c = _coords()
        k_me = 2 * x + y
        sends, arrivals = [], []
        for p in range(n):
            src, land = refs[p], refs[n + p]
            sends.append(_remote(src.at[k_me, 1 - c], land.at[0], ssem.at[7 * p], rsem.at[7 * p], (x, y, 1 - c)))
            for j, chip in enumerate(_peer_chips(x, y)):
                for cc in range(2):
                    sends.append(_remote(src.at[2 * chip[0] + chip[1], cc], land.at[1 + 2 * j + c],
                                         ssem.at[7 * p + 1 + 2 * j + cc], rsem.at[7 * p + 1 + 2 * j + c], (*chip, cc)))
            for s in range(7):
                arrivals.append(_remote(land.at[s], land.at[s], ssem.at[7 * p + s], rsem.at[7 * p + s], (x, y, 1 - c)))
        return sends, arrivals
    return make


def _share_maker(n):
    def make(refs, ssem, rsem):
        x, y, c = _coords()
        sends = [_remote(refs[p].at[c], refs[p].at[c], ssem.at[p], rsem.at[p], (x, y, 1 - c)) for p in range(n)]
        arrivals = [_remote(refs[p].at[1 - c], refs[p].at[1 - c], ssem.at[p], rsem.at[p], (x, y, 1 - c)) for p in range(n)]
        return sends, arrivals
    return make


def split_start(name, make, n_sems, operands, after):
    n = len(operands)

    def body(*refs):
        ssem, rsem, token = refs[n + 1], refs[n + 2], refs[-1]
        for cp in make(refs[:n], ssem, rsem)[0]:
            cp.start()
        token[...] = jnp.zeros_like(token)

    ops = [pltpu.with_memory_space_constraint(a, pltpu.HBM) for a in operands]
    outs = pl.pallas_call(
        body, name=name,
        out_shape=(pltpu.SemaphoreType.DMA((n_sems,)), pltpu.SemaphoreType.DMA((n_sems,)),
                   *[pltpu.HBM(a.shape, a.dtype) for a in ops], jax.ShapeDtypeStruct((8, LANE), f32)),
        in_specs=[HBM] * n + [ANY], out_specs=(SEM, SEM, *[HBM] * n, pl.BlockSpec(memory_space=pltpu.VMEM)),
        input_output_aliases={i: 2 + i for i in range(n)},
        compiler_params=pltpu.CompilerParams(has_side_effects=DATAFLOW),
    )(*ops, after)
    return dict(name=name, make=make, ssem=outs[0], rsem=outs[1], operands=outs[2:2 + n], token=outs[-1][0, 0])


def split_wait(handle, after):
    n = len(handle["operands"])

    def body(*refs):
        sends, arrivals = handle["make"](refs[:n], refs[n], refs[n + 1])
        for cp in sends:
            cp.wait_send()
        for cp in arrivals:
            cp.wait_recv()

    outs = pl.pallas_call(
        body, name=handle["name"].replace("start", "wait"),
        out_shape=tuple(pltpu.HBM(a.shape, a.dtype) for a in handle["operands"]),
        in_specs=[HBM] * n + [SEM, SEM, ANY], out_specs=tuple([HBM] * n),
        input_output_aliases={i: i for i in range(n)},
        compiler_params=pltpu.CompilerParams(has_side_effects=DATAFLOW),
    )(*handle["operands"], handle["ssem"], handle["rsem"], after)
    return list(outs)


def piece_sum(g, recv, kc_arr):
    _, _, rb, cc = g.shape
    tr = min(256, rb)

    def body(kc_ref, g_ref, r_ref, o_ref):
        acc = g_ref[...]
        for s in range(7):
            acc = acc + r_ref[s].astype(f32)
        o_ref[...] = acc

    return pl.pallas_call(
        body, name="piece_sum",
        grid_spec=pltpu.PrefetchScalarGridSpec(
            num_scalar_prefetch=1, grid=(rb // tr,),
            in_specs=[pl.BlockSpec((None, None, tr, cc), lambda r, kc: (kc[0], kc[1], r, 0)),
                      pl.BlockSpec((7, tr, cc), lambda r, kc: (0, r, 0))],
            out_specs=pl.BlockSpec((None, tr, cc), lambda r, kc: (kc[1], r, 0))),
        out_shape=jax.ShapeDtypeStruct((2, rb, cc), f32),
        compiler_params=_params("arbitrary"),
    )(kc_arr, g, recv)


def small_all_reduce(vec):
    def body(v_ref, o_ref, gat, ssem, rsem):
        x, y, c = _coords()
        me = 4 * x + 2 * y + c
        gat[me] = v_ref[...]
        sends = []
        for t in range(1, N_DEVICES):
            peer = (x ^ (t >> 2), y ^ ((t >> 1) & 1), c ^ (t & 1))
            cp = _remote(v_ref, gat.at[me], ssem.at[t - 1], rsem.at[t - 1], peer)
            cp.start()
            sends.append(cp)
        for t in range(1, N_DEVICES):
            peer = (x ^ (t >> 2), y ^ ((t >> 1) & 1), c ^ (t & 1))
            slot = gat.at[4 * peer[0] + 2 * peer[1] + peer[2]]
            _remote(slot, slot, ssem.at[t - 1], rsem.at[t - 1], peer).wait_recv()
        for cp in sends:
            cp.wait_send()
        acc = gat[0]
        for d in range(1, N_DEVICES):
            acc = acc + gat[d]
        o_ref[...] = acc

    return pl.pallas_call(
        body, name="small_all_reduce", out_shape=jax.ShapeDtypeStruct(vec.shape, vec.dtype),
        in_specs=[pl.BlockSpec(memory_space=pltpu.VMEM)], out_specs=pl.BlockSpec(memory_space=pltpu.VMEM),
        scratch_shapes=[pltpu.VMEM((N_DEVICES,) + vec.shape, vec.dtype), pltpu.SemaphoreType.DMA((N_DEVICES - 1,)),
                        pltpu.SemaphoreType.DMA((N_DEVICES - 1,))],
    )(vec)


def _adamw_math(w, g, m, v):
    m_new = ADAM_B1 * m + (1.0 - ADAM_B1) * g
    v_new = ADAM_B2 * v + (1.0 - ADAM_B2) * jnp.square(g)
    m_hat = m_new / (1.0 - ADAM_B1 ** ADAM_STEP)
    v_hat = v_new / (1.0 - ADAM_B2 ** ADAM_STEP)
    delta = -ADAM_LR * (m_hat / (jnp.sqrt(v_hat) + ADAM_EPS) + ADAM_WD * w)
    return delta, m_new, v_new


def adamw_shard(w, g0, g1, m, v):
    depth, rows, cols = w.shape
    half = rows // 2
    tr = min(256, half)
    nr = half // tr

    def body(w_ref, g0_ref, g1_ref, m_ref, v_ref, go_ref, d_ref, nm_ref, nv_ref):
        gv = jnp.where(pl.program_id(0) == 0, g0_ref[...], g1_ref[...])
        go_ref[...] = gv
        d_ref[...], nm_ref[...], nv_ref[...] = _adamw_math(w_ref[...], gv, m_ref[...], v_ref[...])

    spec = pl.BlockSpec((None, tr, cols), lambda l, h, r: (l, h * nr + r, 0))
    g0spec = pl.BlockSpec((None, tr, cols), lambda l, h, r: (jnp.where(l == 0, h, 1), jnp.where(l == 0, r, nr - 1), 0))
    g1spec = pl.BlockSpec((None, tr, cols), lambda l, h, r: (jnp.where(l == 1, h, 0), jnp.where(l == 1, r, 0), 0))
    return pl.pallas_call(
        body, name="adamw_shard", grid=(depth, 2, nr), in_specs=[spec, g0spec, g1spec, spec, spec], out_specs=[spec] * 4,
        out_shape=[jax.ShapeDtypeStruct(w.shape, f32)] * 4,
        compiler_params=_params("arbitrary", "arbitrary", "arbitrary"),
    )(w, g0, g1, m, v)


def adamw_small(w, g, m, v):
    def body(w_ref, g_ref, m_ref, v_ref, d_ref, nm_ref, nv_ref):
        d_ref[...], nm_ref[...], nv_ref[...] = _adamw_math(w_ref[...], g_ref[...], m_ref[...], v_ref[...])

    return pl.pallas_call(
        body, name="adamw_small", out_shape=[jax.ShapeDtypeStruct(w.shape, f32)] * 3,
    )(w, g, m, v)


WEIGHTS = ("mix_norm_g", "w_in", "q_gain", "k_gain", "sinks", "rel_bias", "conv_w", "conv_b", "dt_bias", "a_log", "d_skip",
           "ssm_norm_g", "w_out", "mlp_norm_g", "w_up", "w_down")
BIG = ("w_in", "w_out", "w_up", "w_down")
SMALL = tuple(n for n in WEIGHTS if n not in BIG)
PACK_COLS = 1024
PACK_ROWS = 16


def _pack(named, last=None):
    flat = jnp.concatenate([named[n].reshape(-1) for n in SMALL])
    tail = jnp.zeros((1,), f32) if last is None else last.reshape(1)
    pad = jnp.zeros((PACK_ROWS * PACK_COLS - flat.shape[0] - 1,), f32)
    return jnp.concatenate([flat, pad, tail]).reshape(PACK_ROWS, PACK_COLS)


def _unpack(buf, shapes):
    flat = buf.reshape(-1)
    out, at = {}, 0
    for n in SMALL:
        size = int(np.prod(shapes[n]))
        out[n] = flat[at:at + size].reshape(shapes[n])
        at += size
    return out


class _Exchange:
    GROUPS = {"A": (("w_up", 0), ("w_down", 0)), "B": (("w_in", 1), ("w_out", 1)), "C": (("w_up", 1), ("w_down", 1))}
    RELAY_AT = {("pre_out", 0): "A", ("pre_mlp", 0): "B", ("mid", 1): "C"}
    NEXT_GROUP = {"A": "B", "B": "C"}
    LAST = ("mix", 0)

    def __init__(self, wts, k_me, kc_arr):
        self.wts, self.k_me, self.kc_arr = wts, k_me, kc_arr
        self.own = {(n, l): wts[n][l].astype(bf16) for n in BIG for l in range(DEPTH)}
        now = gather_now([self.own["w_in", 0], self.own["w_out", 0]], wts["conv_w"])
        self.ready = {("w_in", 0): now[0], ("w_out", 0): now[1]}
        self.conv_w = jnp.transpose(now[2], (1, 2, 0, 3)).reshape(DEPTH, CONV_WIDTH, D_CONV)
        self.ici, self.relay = {}, {}
        self.gview, self.scatter, self.share, self.reduced = {}, [], [], {}
        self._start_ici("A", now[2])

    def _start_ici(self, g, after):
        srcs = [self.own[p] for p in self.GROUPS[g]]
        lands = [lax.empty((N_CHIPS,) + s.shape, s.dtype) for s in srcs]
        self.ici[g] = split_start("gather%s_ici_start" % g, _gather_maker("ici", len(srcs)), 3 * len(srcs), srcs + lands,
                                  after)
        return self.ici[g]["token"]

    def stage(self, name, after):
        if name == ("begin", 0):
            return self.ici["A"]["token"]
        g = self.RELAY_AT.get(name)
        if g is None:
            return 0.0
        n = len(self.GROUPS[g])
        self.relay[g] = split_start("gather%s_relay_start" % g, _gather_maker("relay", n), 4 * n,
                                    split_wait(self.ici[g], after), after)
        tok = self.relay[g]["token"]
        if g in self.NEXT_GROUP:
            tok = tok + self._start_ici(self.NEXT_GROUP[g], after)
        return tok

    def _get(self, piece, after):
        if piece not in self.ready:
            g = [k for k, pieces in self.GROUPS.items() if piece in pieces][0]
            lands = split_wait(self.relay[g], after)[len(self.GROUPS[g]):]
            self.ready.update(zip(self.GROUPS[g], lands))
        return self.ready[piece]

    def w_in(self, l, after):
        return align_w_in(self._get(("w_in", l), after))

    def w_out(self, l, after):
        return self._get(("w_out", l), after).reshape(D_MODEL, D_MODEL)

    def mlp(self, l, after):
        return self._get(("w_up", l), after), self._get(("w_down", l), after)

    def _view(self, n, g):
        _, rows, cols = self.wts[n].shape
        return g.reshape(N_CHIPS, 2, rows // 2, cols)

    def grads(self, name, arrays, after):
        if name == self.LAST:
            self.held = (name, arrays)
            return self._advance(after, 0)
        return self._scatter(name, arrays, after) + self._advance(after, 1)

    def flush(self, after):
        return self._scatter(*self.held, after)

    def _scatter(self, name, arrays, after):
        pieces = [(n, name[1]) for n in arrays]
        views = [self._view(n, g) for n, (g, _) in arrays.items()]
        sends = [g16.reshape(v.shape) for v, (_, g16) in zip(views, arrays.values())]
        self.gview.update(zip(pieces, views))
        lands = [lax.empty((7,) + v.shape[2:], bf16) for v in views]
        h = split_start("scatter_%s%d_start" % name, _scatter_maker(len(views)), 7 * len(views), sends + lands, after)
        self.scatter.append((pieces, h))
        return h["token"]

    def _take_share(self, after):
        pieces, h = self.share.pop(0)
        self.reduced.update(zip(pieces, split_wait(h, after)))

    def _take_scatter(self, after):
        pieces, h = self.scatter.pop(0)
        lands = split_wait(h, after)[len(pieces):]
        sums = [piece_sum(self.gview[p], land, self.kc_arr) for p, land in zip(pieces, lands)]
        hs = split_start(h["name"].replace("scatter", "share"), _share_maker(len(sums)), len(sums), sums, after)
        self.share.append((pieces, hs))
        return hs["token"]

    def _advance(self, after, newest):
        if self.share:
            self._take_share(after)
        return self._take_scatter(after) if len(self.scatter) > newest else 0.0

    def reduced_grads(self, names, after):
        want = [(n, l) for n in names for l in range(DEPTH)]
        while not all(p in self.reduced for p in want):
            if any(p in pieces for p in want for pieces, _ in self.share):
                self._take_share(after)
            else:
                self._take_scatter(after)
        return {n: [self.reduced[n, l] for l in range(DEPTH)] for n in names}


def kernel(x, mix_norm_g, w_in, q_gain, k_gain, sinks, rel_bias, conv_w, conv_b, dt_bias, a_log, d_skip, ssm_norm_g, w_out, mlp_norm_g, w_up, w_down, loss_target, m_mix_norm_g, m_w_in, m_q_gain, m_k_gain, m_sinks, m_rel_bias, m_conv_w, m_conv_b, m_dt_bias, m_a_log, m_d_skip, m_ssm_norm_g, m_w_out, m_mlp_norm_g, m_w_up, m_w_down, v_mix_norm_g, v_w_in, v_q_gain, v_k_gain, v_sinks, v_rel_bias, v_conv_w, v_conv_b, v_dt_bias, v_a_log, v_d_skip, v_ssm_norm_g, v_w_out, v_mlp_norm_g, v_w_up, v_w_down):
    wts = dict(mix_norm_g=mix_norm_g, w_in=w_in, q_gain=q_gain, k_gain=k_gain, sinks=sinks, rel_bias=rel_bias, conv_w=conv_w,
               conv_b=conv_b, dt_bias=dt_bias, a_log=a_log, d_skip=d_skip, ssm_norm_g=ssm_norm_g, w_out=w_out,
               mlp_norm_g=mlp_norm_g, w_up=w_up, w_down=w_down)
    mom = dict(mix_norm_g=m_mix_norm_g, w_in=m_w_in, q_gain=m_q_gain, k_gain=m_k_gain, sinks=m_sinks, rel_bias=m_rel_bias,
               conv_w=m_conv_w, conv_b=m_conv_b, dt_bias=m_dt_bias, a_log=m_a_log, d_skip=m_d_skip, ssm_norm_g=m_ssm_norm_g,
               w_out=m_w_out, mlp_norm_g=m_mlp_norm_g, w_up=m_w_up, w_down=m_w_down)
    var = dict(mix_norm_g=v_mix_norm_g, w_in=v_w_in, q_gain=v_q_gain, k_gain=v_k_gain, sinks=v_sinks, rel_bias=v_rel_bias,
               conv_w=v_conv_w, conv_b=v_conv_b, dt_bias=v_dt_bias, a_log=v_a_log, d_skip=v_d_skip, ssm_norm_g=v_ssm_norm_g,
               w_out=v_w_out, mlp_norm_g=v_mlp_norm_g, w_up=v_w_up, w_down=v_w_down)
    xi, yi, ci = _coords()
    k_me = 2 * xi + yi
    kc_arr = jnp.stack([k_me, ci]).astype(jnp.int32)

    prov = _Exchange(wts, k_me, kc_arr)
    small_w = {n: wts[n] for n in SMALL}
    small_w["conv_w"] = prov.conv_w
    loss, dx, grads, tok = local_step(x[0], loss_target[0], small_w, prov)

    small_shapes = {n: grads[n].shape for n in SMALL}
    small_sum = small_all_reduce(_pack(grads, loss) + tok)
    loss = small_sum[PACK_ROWS - 1, PACK_COLS - 1]
    tok = prov.flush(small_sum)
    small = _unpack(small_sum, small_shapes)
    cols = conv_w.shape[-1]
    small["conv_w"] = lax.dynamic_slice_in_dim(small["conv_w"], k_me * cols, cols, axis=2)
    g_out_d, d_out_d, m_out_d, v_out_d = {}, {}, {}, {}
    shard_shapes = {n: wts[n].shape for n in SMALL}
    d, nm, nv = adamw_small(_pack(wts), _pack(small) + tok, _pack(mom), _pack(var))
    for dst, buf in ((d_out_d, d), (m_out_d, nm), (v_out_d, nv)):
        dst.update(_unpack(buf, shard_shapes))
    g_out_d.update(small)

    after = d
    for names in (("w_up", "w_down"), ("w_in", "w_out")):
        for n, (g0, g1) in prov.reduced_grads(names, after).items():
            g_out_d[n], d_out_d[n], m_out_d[n], v_out_d[n] = adamw_shard(wts[n], g0, g1, mom[n], var[n])
            after = d_out_d[n]

    return (loss, dx[None], *[g_out_d[n] for n in WEIGHTS], *[d_out_d[n] for n in WEIGHTS],
            *[m_out_d[n] for n in WEIGHTS], *[v_out_d[n] for n in WEIGHTS])
```

```python
import functools

import numpy as np
import jax
import jax.numpy as jnp
from jax import lax
from jax.experimental import pallas as pl
from jax.experimental.pallas import tpu as pltpu

f32 = jnp.float32
bf16 = jnp.bfloat16

SEQ = 2048
D_MODEL = 1024
DEPTH = 2
HEAD_DIM = 64
N_Q_HEADS = 8
N_KV_HEADS = 2
Q_PER_KV = N_Q_HEADS // N_KV_HEADS
BLOCK = 128
N_BLOCKS = SEQ // BLOCK
N_BUCKETS = 32
MAX_DISTANCE = 128
SSM_HEADS = 8
SSM_HEAD_DIM = 64
SSM_GROUPS = 2
HEADS_PER_GROUP = SSM_HEADS // SSM_GROUPS
SSM_STATE = 128
CONV_WIDTH = 4
CHUNK = 128
N_CHUNKS = SEQ // CHUNK
D_FF = 4 * D_MODEL
D_ATTN = N_Q_HEADS * HEAD_DIM
D_KV = N_KV_HEADS * HEAD_DIM
D_SSM = SSM_HEADS * SSM_HEAD_DIM
D_BC = SSM_GROUPS * SSM_STATE
D_CONV = D_SSM + 2 * D_BC
D_IN = D_ATTN + 2 * D_KV + D_SSM + D_CONV + SSM_HEADS
EPS = 1e-6
NEG = -1e30
N_CHIPS = 4
FF_TILE = D_FF // N_CHIPS

LANE = 128
PW = D_ATTN + D_SSM + D_CONV + 2 * D_KV + LANE
OFF_Q, OFF_Z, OFF_X, OFF_K, OFF_V, OFF_DT = 0, 512, 1024, 2048, 2176, 2304

ADAM_LR = 0.001
ADAM_B1 = 0.9
ADAM_B2 = 0.999
ADAM_EPS = 1e-08
ADAM_WD = 0.01
ADAM_STEP = 10

VMEM_LIMIT = 56 * 1024 * 1024


def _params(*sem):
    return pltpu.CompilerParams(dimension_semantics=tuple(sem), vmem_limit_bytes=VMEM_LIMIT)


def _bdot(a, b):
    return jnp.dot(a.astype(bf16), b.astype(bf16), preferred_element_type=f32)


def _bdot_nt(a, b):
    return lax.dot_general(a.astype(bf16), b.astype(bf16), (((1,), (1,)), ((), ())), preferred_element_type=f32)


def _bdot_tn(a, b):
    return lax.dot_general(a.astype(bf16), b.astype(bf16), (((0,), (0,)), ((), ())), preferred_element_type=f32)


def _hdot(a, b):
    return jnp.dot(a, b, precision=lax.Precision.HIGHEST, preferred_element_type=f32)


def _sigmoid(x):
    return 1.0 / (1.0 + jnp.exp(-x))


def _softplus(x):
    return jnp.maximum(x, 0.0) + jnp.log1p(jnp.exp(-jnp.abs(x)))


def _rms(x):
    return lax.rsqrt(jnp.mean(x * x, axis=-1, keepdims=True) + EPS)


def _rms_bwd(dy, xhat, r, g):
    t = dy * g
    return r * (t - xhat * jnp.mean(t * xhat, axis=-1, keepdims=True))


def _full(shape):
    return pl.BlockSpec(shape, lambda *_: (0,) * len(shape))


def _bucket_table():
    qi = np.arange(BLOCK)[:, None]
    kj = np.arange(2 * BLOCK)[None, :]
    dist = qi + BLOCK - kj
    ok = (dist >= 0) & (dist < 128)
    d = np.clip(dist, 0, None)
    max_exact = N_BUCKETS // 2
    d_f = np.maximum(d, 1).astype(np.float32)
    large = max_exact + (np.log(d_f / np.float32(max_exact)) / np.float32(np.log(MAX_DISTANCE / max_exact))
                         * np.float32(N_BUCKETS - max_exact)).astype(np.int32)
    large = np.minimum(large, N_BUCKETS - 1)
    bucket = np.where(d < max_exact, d, large)
    return np.where(ok, bucket, -1).astype(np.int32)


def bias_build(rel_bias, bucket):
    def body(rel_ref, bkt_ref, o_ref):
        bkt = bkt_ref[...]
        for h in range(N_Q_HEADS):
            acc = jnp.where(bkt < 0, NEG, 0.0).astype(f32)
            for b in range(N_BUCKETS):
                acc = acc + jnp.where(bkt == b, rel_ref[b, h], 0.0)
            o_ref[h] = acc

    return pl.pallas_call(
        body, name="bias_build", out_shape=jax.ShapeDtypeStruct((N_Q_HEADS,) + bucket.shape, f32),
        in_specs=[pl.BlockSpec(memory_space=pltpu.SMEM), pl.BlockSpec(memory_space=pltpu.VMEM)],
        out_specs=pl.BlockSpec(memory_space=pltpu.VMEM),
    )(rel_bias, bucket)


def bias_bwd(dband0, dband1, bucket):
    def body(d0_ref, d1_ref, bkt_ref, o_ref):
        bkt = bkt_ref[...]
        o_ref[...] = jnp.zeros_like(o_ref)
        for h in range(N_Q_HEADS):
            d = d0_ref[h] + d1_ref[h]
            for b in range(N_BUCKETS):
                part = jnp.sum(jnp.where(bkt == b, d, 0.0), axis=1, keepdims=True)
                o_ref[b:b + 1, h:h + 1] = jnp.sum(part, axis=0, keepdims=True)

    return pl.pallas_call(
        body, name="bias_bwd", out_shape=jax.ShapeDtypeStruct((N_BUCKETS, LANE), f32),
    )(dband0, dband1, bucket)


W_IN_SHARD = D_IN // N_CHIPS
_ALIGNED_PIECES = ((0, 0, 512), (1, 190, 578), (2, 0, 124), (2, 124, 578), (3, 0, 570), (0, 512, 578), (1, 0, 62),
                   (1, 62, 190), (3, 570, 578))
_SHARD_PIECES = (((0, 512), (2048, 2114)), ((2114, 2176), (2176, 2304), (512, 900)), ((900, 1024), (1024, 1478)),
                 ((1478, 2048), (2304, 2312)))


def align_w_in(shards, tr=256):
    def body(s_ref, o_ref):
        parts = [s_ref[k, :, a:b] for k, a, b in _ALIGNED_PIECES]
        parts.append(jnp.zeros((tr, LANE - SSM_HEADS), s_ref.dtype))
        o_ref[...] = jnp.concatenate(parts, axis=-1)

    return pl.pallas_call(
        body, name="align_w_in", grid=(D_MODEL // tr,),
        in_specs=[pl.BlockSpec((N_CHIPS, tr, W_IN_SHARD), lambda i: (0, i, 0))],
        out_specs=pl.BlockSpec((tr, PW), lambda i: (i, 0)),
        out_shape=jax.ShapeDtypeStruct((D_MODEL, PW), shards.dtype),
        compiler_params=_params("arbitrary"),
    )(shards)


def split_w_in_grad(dw, tr=256):
    def body(d_ref, o_ref, o16_ref):
        for k, pieces in enumerate(_SHARD_PIECES):
            part = jnp.concatenate([d_ref[:, a:b] for a, b in pieces], axis=-1)
            o_ref[k] = part
            o16_ref[k] = part.astype(bf16)

    spec = pl.BlockSpec((N_CHIPS, tr, W_IN_SHARD), lambda i: (0, i, 0))
    return pl.pallas_call(
        body, name="split_w_in_grad", grid=(D_MODEL // tr,),
        in_specs=[pl.BlockSpec((tr, PW), lambda i: (i, 0))], out_specs=[spec, spec],
        out_shape=[jax.ShapeDtypeStruct((N_CHIPS, D_MODEL, W_IN_SHARD), f32),
                   jax.ShapeDtypeStruct((N_CHIPS, D_MODEL, W_IN_SHARD), bf16)],
        compiler_params=_params("arbitrary"),
    )(dw)

def in_fwd(x, g, w, tm=512):
    def body(x_ref, g_ref, w_ref, o_ref):
        xv = x_ref[...]
        h = xv * _rms(xv) * g_ref[...]
        o_ref[...] = _bdot(h, w_ref[...])

    return pl.pallas_call(
        body, name="in_fwd", grid=(SEQ // tm,),
        in_specs=[pl.BlockSpec((tm, D_MODEL), lambda i: (i, 0)), _full((1, D_MODEL)), _resident((D_MODEL, PW))],
        out_specs=pl.BlockSpec((tm, PW), lambda i: (i, 0)),
        out_shape=jax.ShapeDtypeStruct((SEQ, PW), f32),
        compiler_params=_params("arbitrary"),
    )(x, g, w)


def _resident(shape):
    return pl.BlockSpec(shape, lambda *_: (0,) * len(shape), pipeline_mode=pl.Buffered(1))


def in_bwd(dq, dz, dxbc, dk, dv, ddt, x, g, w, dres, tm=512):
    def body(dq_ref, dz_ref, dx_ref, dk_ref, dv_ref, ddt_ref, x_ref, g_ref, w_ref, dres_ref, o_ref, dw_ref, dg_ref):
        i = pl.program_id(0)

        @pl.when(i == 0)
        def _():
            dw_ref[...] = jnp.zeros_like(dw_ref)
            dg_ref[...] = jnp.zeros_like(dg_ref)

        dproj = jnp.concatenate([dq_ref[...], dz_ref[...], dx_ref[...], dk_ref[...], dv_ref[...], ddt_ref[...]],
                                axis=-1).astype(bf16)
        xv = x_ref[...]
        r = _rms(xv)
        xhat = xv * r
        gv = g_ref[...]
        h = xhat * gv
        dw_ref[...] += _bdot_tn(h, dproj)
        dh = _bdot_nt(dproj, w_ref[...])
        dg_ref[...] += jnp.sum(dh * xhat, axis=0, keepdims=True)
        o_ref[...] = dres_ref[...] + _rms_bwd(dh, xhat, r, gv)

    tok = lambda w_: pl.BlockSpec((tm, w_), lambda i: (i, 0))
    return pl.pallas_call(
        body, name="in_bwd", grid=(SEQ // tm,),
        in_specs=[tok(D_ATTN), tok(D_SSM), tok(D_CONV), tok(D_KV), tok(D_KV), tok(LANE), tok(D_MODEL),
                  _full((1, D_MODEL)), _resident((D_MODEL, PW)), tok(D_MODEL)],
        out_specs=[tok(D_MODEL), _resident((D_MODEL, PW)), _full((1, D_MODEL))],
        out_shape=[jax.ShapeDtypeStruct((SEQ, D_MODEL), f32), jax.ShapeDtypeStruct((D_MODEL, PW), f32),
                   jax.ShapeDtypeStruct((1, D_MODEL), f32)],
        compiler_params=_params("arbitrary"),
    )(dq, dz, dxbc, dk, dv, ddt, x, g, w, dres)


def _attn_softmax_t(qk, bias_t, sink, first, key_row):
    s = qk * (HEAD_DIM ** -0.5) + bias_t
    s = jnp.where(jnp.logical_and(first, key_row < BLOCK), NEG, s)
    m = jnp.maximum(jnp.max(s, axis=0, keepdims=True), sink)
    p = jnp.exp(s - m)
    psink = jnp.exp(sink - m)
    inv = 1.0 / (jnp.sum(p, axis=0, keepdims=True) + psink)
    return p * inv, psink * inv


def _rms_t(x_t):
    return lax.rsqrt(jnp.mean(x_t * x_t, axis=0, keepdims=True) + EPS)


def attn_fwd_t(proj, q_gain_col, k_gain, sinks, bias_t):
    kcol, vcol = OFF_K // D_KV, OFF_V // D_KV

    def body(q_ref, kc_ref, kp_ref, vc_ref, vp_ref, qg_ref, kg_ref, sink_ref, bias_ref, o_ref, ot_scr):
        n = pl.program_id(0)
        first = n == 0
        key_row = lax.broadcasted_iota(jnp.int32, (2 * BLOCK, BLOCK), 0)
        k2 = jnp.concatenate([kp_ref[...], kc_ref[...]], axis=0)
        v_t = jnp.concatenate([vp_ref[...], vc_ref[...]], axis=0).T
        q_t = q_ref[...].T
        qg = jnp.broadcast_to(qg_ref[...], (HEAD_DIM, BLOCK))
        kg = kg_ref[...]
        for hk in range(N_KV_HEADS):
            sl = slice(hk * HEAD_DIM, (hk + 1) * HEAD_DIM)
            kk = k2[:, sl]
            kn = (kk * _rms(kk) * kg).astype(bf16)
            vt = v_t[sl, :].astype(bf16)
            heads = range(hk * Q_PER_KV, (hk + 1) * Q_PER_KV)
            qns = []
            for h in heads:
                qh = q_t[h * HEAD_DIM:(h + 1) * HEAD_DIM, :]
                qns.append(qh * _rms_t(qh) * qg)
            scores = [_bdot(kn, qn) for qn in qns]
            for h, s in zip(heads, scores):
                p, _ = _attn_softmax_t(s, bias_ref[h], sink_ref[h], first, key_row)
                ot_scr[h * HEAD_DIM:(h + 1) * HEAD_DIM, :] = _bdot(vt, p)
        o_ref[...] = ot_scr[...].T

    prev = lambda n: jnp.maximum(n - 1, 0)
    return pl.pallas_call(
        body, name="attn_fwd", grid=(N_BLOCKS,),
        in_specs=[pl.BlockSpec((BLOCK, D_ATTN), lambda n: (n, 0)),
                  pl.BlockSpec((BLOCK, D_KV), lambda n: (n, kcol)), pl.BlockSpec((BLOCK, D_KV), lambda n: (prev(n), kcol)),
                  pl.BlockSpec((BLOCK, D_KV), lambda n: (n, vcol)), pl.BlockSpec((BLOCK, D_KV), lambda n: (prev(n), vcol)),
                  _full((HEAD_DIM, 1)), _full((1, HEAD_DIM)), pl.BlockSpec(memory_space=pltpu.SMEM),
                  _full((N_Q_HEADS, 2 * BLOCK, BLOCK))],
        out_specs=pl.BlockSpec((BLOCK, D_ATTN), lambda n: (n, 0)),
        out_shape=jax.ShapeDtypeStruct((SEQ, D_ATTN), f32),
        scratch_shapes=[pltpu.VMEM((D_ATTN, BLOCK), f32)],
        compiler_params=_params("arbitrary"),
    )(proj, proj, proj, proj, proj, q_gain_col, k_gain, sinks, bias_t)


def attn_bwd_t(proj, d_out, q_gain_col, k_gain, sinks, bias_t):
    kcol, vcol = OFF_K // D_KV, OFF_V // D_KV

    def body(q_ref, kc_ref, kp_ref, vc_ref, vp_ref, do_ref, qg_ref, kg_ref, sink_ref, bias_ref,
             dq_ref, dk_ref, dv_ref, dband_ref, dsink_ref, dqg_ref, dkg_ref, dkn_scr, dv_scr, dqt_scr, dsink_acc, dqg_acc):
        i = pl.program_id(0)
        first = i == N_BLOCKS - 1

        @pl.when(i == 0)
        def _():
            for ref in (dband_ref, dkg_ref, dkn_scr, dv_scr, dsink_acc, dqg_acc):
                ref[...] = jnp.zeros_like(ref)

        key_row = lax.broadcasted_iota(jnp.int32, (2 * BLOCK, BLOCK), 0)
        k2 = jnp.concatenate([kp_ref[...], kc_ref[...]], axis=0)
        v2 = jnp.concatenate([vp_ref[...], vc_ref[...]], axis=0)
        q_t = q_ref[...].T
        do_t = do_ref[...].T
        qg = jnp.broadcast_to(qg_ref[...], (HEAD_DIM, BLOCK))
        kg = kg_ref[...]
        scale = HEAD_DIM ** -0.5
        for hk in range(N_KV_HEADS):
            sl = slice(hk * HEAD_DIM, (hk + 1) * HEAD_DIM)
            kk = k2[:, sl]
            rk = _rms(kk)
            khat = kk * rk
            kn = (khat * kg).astype(bf16)
            vb = v2[:, sl].astype(bf16)
            dkn = jnp.zeros((2 * BLOCK, HEAD_DIM), f32)
            dvv = jnp.zeros((2 * BLOCK, HEAD_DIM), f32)
            heads = range(hk * Q_PER_KV, (hk + 1) * Q_PER_KV)
            rqs, qhats, qns, d_os = [], [], [], []
            for h in heads:
                hs = slice(h * HEAD_DIM, (h + 1) * HEAD_DIM)
                qh = q_t[hs, :]
                rqs.append(_rms_t(qh))
                qhats.append(qh * rqs[-1])
                qns.append((qhats[-1] * qg).astype(bf16))
                d_os.append(do_t[hs, :].astype(bf16))
            scores = [_bdot(kn, qn) for qn in qns]
            dps = [_bdot(vb, d_o) for d_o in d_os]
            ps, dss = [], []
            for h, s, dp in zip(heads, scores, dps):
                p, psink = _attn_softmax_t(s, bias_ref[h], sink_ref[h], first, key_row)
                delta = jnp.sum(p * dp, axis=0, keepdims=True)
                ds = p * (dp - delta)
                dband_ref[h] += ds
                dsink_acc[h:h + 1, :] += -(psink * delta)
                ps.append(p.astype(bf16))
                dss.append(ds.astype(bf16))
            dqns = [_bdot_tn(kn, ds) * scale for ds in dss]
            for ds, qn, p, d_o in zip(dss, qns, ps, d_os):
                dkn = dkn + _bdot_nt(ds, qn) * scale
                dvv = dvv + _bdot_nt(p, d_o)
            for h, dqn, rq, qhat in zip(heads, dqns, rqs, qhats):
                dqg_acc[...] += dqn * qhat
                t = dqn * qg
                dqt_scr[h * HEAD_DIM:(h + 1) * HEAD_DIM, :] = rq * (t - qhat * jnp.mean(t * qhat, axis=0, keepdims=True))
            dkn_cur = dkn[BLOCK:] + dkn_scr[:, sl]
            dkn_scr[:, sl] = dkn[:BLOCK]
            khat_c, rk_c = khat[BLOCK:], rk[BLOCK:]
            dkg_ref[...] += jnp.sum(dkn_cur * khat_c, axis=0, keepdims=True)
            dk_ref[:, sl] = _rms_bwd(dkn_cur, khat_c, rk_c, kg)
            dv_ref[:, sl] = dvv[BLOCK:] + dv_scr[:, sl]
            dv_scr[:, sl] = dvv[:BLOCK]
        dq_ref[...] = dqt_scr[...].T

        @pl.when(i == N_BLOCKS - 1)
        def _():
            dsink_ref[...] = jnp.sum(dsink_acc[...], axis=1, keepdims=True)
            dqg_ref[...] = jnp.sum(dqg_acc[...], axis=1, keepdims=True)

    blk = lambda i: N_BLOCKS - 1 - i
    prev = lambda i: jnp.maximum(N_BLOCKS - 2 - i, 0)
    return pl.pallas_call(
        body, name="attn_bwd", grid=(N_BLOCKS,),
        in_specs=[pl.BlockSpec((BLOCK, D_ATTN), lambda i: (blk(i), 0)),
                  pl.BlockSpec((BLOCK, D_KV), lambda i: (blk(i), kcol)), pl.BlockSpec((BLOCK, D_KV), lambda i: (prev(i), kcol)),
                  pl.BlockSpec((BLOCK, D_KV), lambda i: (blk(i), vcol)), pl.BlockSpec((BLOCK, D_KV), lambda i: (prev(i), vcol)),
                  pl.BlockSpec((BLOCK, D_ATTN), lambda i: (blk(i), 0)),
                  _full((HEAD_DIM, 1)), _full((1, HEAD_DIM)), pl.BlockSpec(memory_space=pltpu.SMEM),
                  _full((N_Q_HEADS, 2 * BLOCK, BLOCK))],
        out_specs=[pl.BlockSpec((BLOCK, D_ATTN), lambda i: (blk(i), 0)), pl.BlockSpec((BLOCK, D_KV), lambda i: (blk(i), 0)),
                   pl.BlockSpec((BLOCK, D_KV), lambda i: (blk(i), 0)), _full((N_Q_HEADS, 2 * BLOCK, BLOCK)),
                   _full((N_Q_HEADS, 1)), _full((HEAD_DIM, 1)), _full((1, HEAD_DIM))],
        out_shape=[jax.ShapeDtypeStruct((SEQ, D_ATTN), f32), jax.ShapeDtypeStruct((SEQ, D_KV), f32),
                   jax.ShapeDtypeStruct((SEQ, D_KV), f32), jax.ShapeDtypeStruct((N_Q_HEADS, 2 * BLOCK, BLOCK), f32),
                   jax.ShapeDtypeStruct((N_Q_HEADS, 1), f32), jax.ShapeDtypeStruct((HEAD_DIM, 1), f32),
                   jax.ShapeDtypeStruct((1, HEAD_DIM), f32)],
        scratch_shapes=[pltpu.VMEM((BLOCK, D_KV), f32), pltpu.VMEM((BLOCK, D_KV), f32), pltpu.VMEM((D_ATTN, BLOCK), f32),
                        pltpu.VMEM((N_Q_HEADS, BLOCK), f32), pltpu.VMEM((HEAD_DIM, BLOCK), f32)],
        compiler_params=_params("arbitrary"),
    )(proj, proj, proj, proj, proj, d_out, q_gain_col, k_gain, sinks, bias_t)


def _shift_down(u, s, row):
    if s == 0:
        return u
    return jnp.where(row >= s, pltpu.roll(u, s, 0), 0.0)


def _shift_up(u, s, row):
    if s == 0:
        return u
    return jnp.where(row < SEQ - s, pltpu.roll(u, SEQ - s, 0), 0.0)


def conv_fwd(proj, conv_w, conv_b):
    xcol = OFF_X // LANE

    def body(u_ref, w_ref, b_ref, o_ref):
        u = u_ref[...]
        row = lax.broadcasted_iota(jnp.int32, u.shape, 0)
        pre = b_ref[...] + jnp.zeros_like(u)
        for k in range(CONV_WIDTH):
            pre = pre + w_ref[k:k + 1, :] * _shift_down(u, CONV_WIDTH - 1 - k, row)
        o_ref[...] = pre * _sigmoid(pre)

    return pl.pallas_call(
        body, name="conv_fwd", grid=(D_CONV // LANE,),
        in_specs=[pl.BlockSpec((SEQ, LANE), lambda j: (0, xcol + j)), pl.BlockSpec((CONV_WIDTH, LANE), lambda j: (0, j)),
                  pl.BlockSpec((1, LANE), lambda j: (0, j))],
        out_specs=pl.BlockSpec((SEQ, LANE), lambda j: (0, j)),
        out_shape=jax.ShapeDtypeStruct((SEQ, D_CONV), f32),
        compiler_params=_params("arbitrary"),
    )(proj, conv_w, conv_b)


def conv_bwd(proj, d_act, conv_w, conv_b):
    xcol = OFF_X // LANE

    def body(u_ref, da_ref, w_ref, b_ref, du_ref, dw_ref, db_ref):
        u = u_ref[...]
        row = lax.broadcasted_iota(jnp.int32, u.shape, 0)
        shifted = [_shift_down(u, CONV_WIDTH - 1 - k, row) for k in range(CONV_WIDTH)]
        pre = b_ref[...] + jnp.zeros_like(u)
        for k in range(CONV_WIDTH):
            pre = pre + w_ref[k:k + 1, :] * shifted[k]
        sg = _sigmoid(pre)
        dpre = da_ref[...] * (sg * (1.0 + pre * (1.0 - sg)))
        db_ref[...] = jnp.sum(dpre, axis=0, keepdims=True)
        du = jnp.zeros_like(u)
        for k in range(CONV_WIDTH):
            dw_ref[k:k + 1, :] = jnp.sum(dpre * shifted[k], axis=0, keepdims=True)
            du = du + w_ref[k:k + 1, :] * _shift_up(dpre, CONV_WIDTH - 1 - k, row)
        du_ref[...] = du

    return pl.pallas_call(
        body, name="conv_bwd", grid=(D_CONV // LANE,),
        in_specs=[pl.BlockSpec((SEQ, LANE), lambda j: (0, xcol + j)), pl.BlockSpec((SEQ, LANE), lambda j: (0, j)),
                  pl.BlockSpec((CONV_WIDTH, LANE), lambda j: (0, j)), pl.BlockSpec((1, LANE), lambda j: (0, j))],
        out_specs=[pl.BlockSpec((SEQ, LANE), lambda j: (0, j)), pl.BlockSpec((CONV_WIDTH, LANE), lambda j: (0, j)),
                   pl.BlockSpec((1, LANE), lambda j: (0, j))],
        out_shape=[jax.ShapeDtypeStruct((SEQ, D_CONV), f32), jax.ShapeDtypeStruct((CONV_WIDTH, D_CONV), f32),
                   jax.ShapeDtypeStruct((1, D_CONV), f32)],
        compiler_params=_params("arbitrary"),
    )(proj, d_act, conv_w, conv_b)


def _ssd_chunk_common(dt_raw, dtb, alog):
    row = lax.broadcasted_iota(jnp.int32, (CHUNK, CHUNK), 0)
    col = lax.broadcasted_iota(jnp.int32, (CHUNK, CHUNK), 1)
    tri = (row >= col).astype(f32)
    strict = (row > col).astype(f32)
    dtp = _softplus(dt_raw + dtb)
    a_row = -jnp.exp(alog)
    d_a = dtp * a_row
    cs = _hdot(tri, d_a)
    cs_last = cs[CHUNK - 1:CHUNK, :]
    return row, col, dtp, a_row, cs, cs.T, cs_last


def _seg_decay(cs, cs_t, hd, row, col):
    seg = cs[:, hd:hd + 1] - cs_t[hd:hd + 1, :]
    return jnp.where(row >= col, jnp.exp(seg), 0.0)


GROUP_W = HEADS_PER_GROUP * SSM_HEAD_DIM


def _group_indicator(g):
    j = lax.broadcasted_iota(jnp.int32, (GROUP_W, LANE), 0)
    lane = lax.broadcasted_iota(jnp.int32, (GROUP_W, LANE), 1)
    return (lane == g * HEADS_PER_GROUP + j // SSM_HEAD_DIM).astype(f32)


def _hdot_nt(a, b):
    return lax.dot_general(a, b, (((1,), (1,)), ((), ())), precision=lax.Precision.HIGHEST, preferred_element_type=f32)


def ssd_fwd_g(act, proj, dt_bias, a_log, d_skip, norm_g):
    zcol, dtcol = OFF_Z // D_SSM, OFF_DT // LANE

    def body(act_ref, z_ref, dt_ref, dtb_ref, alog_ref, dsk_ref, ng_ref, out_ref, ypre_ref, st_ref, state):
        c = pl.program_id(0)

        @pl.when(c == 0)
        def _():
            state[...] = jnp.zeros_like(state)

        row, col, dtp, a_row, cs, cs_t, cs_last = _ssd_chunk_common(dt_ref[...], dtb_ref[...], alog_ref[...])
        e_cs = jnp.exp(cs)
        dte = jnp.exp(cs_last - cs)
        rows8 = jnp.concatenate([jnp.exp(cs_last), dsk_ref[...], jnp.zeros((6, LANE), f32)], axis=0)
        z = z_ref[...]
        sz = z * _sigmoid(z)
        ng = ng_ref[...]
        for g in range(SSM_GROUPS):
            gs = slice(g * GROUP_W, (g + 1) * GROUP_W)
            ind = _group_indicator(g)
            xg = act_ref[:, gs]
            bg = act_ref[:, D_SSM + g * SSM_STATE:D_SSM + (g + 1) * SSM_STATE]
            cg = act_ref[:, D_SSM + D_BC + g * SSM_STATE:D_SSM + D_BC + (g + 1) * SSM_STATE]
            dt_e, e_e, dte_e = _hdot_nt(dtp, ind), _hdot_nt(e_cs, ind), _hdot_nt(dte, ind)
            rows_e = _hdot_nt(rows8, ind)
            ecl_e, dsk_e = rows_e[0:1], rows_e[1:2]
            xdt = xg * dt_e
            prev = state[g]
            st_ref[0, g] = prev
            cb = _bdot_nt(cg, bg)
            goff = _bdot(cg, prev)
            snew = _bdot_tn(bg, xdt * dte_e)
            heads = range(g * HEADS_PER_GROUP, (g + 1) * HEADS_PER_GROUP)
            ms = [cb * _seg_decay(cs, cs_t, hd, row, col) for hd in heads]
            yd = [_bdot(m, xdt[:, r * SSM_HEAD_DIM:(r + 1) * SSM_HEAD_DIM]) for r, m in enumerate(ms)]
            y = jnp.concatenate(yd, axis=1) + e_e * goff + xg * dsk_e
            state[g] = prev * ecl_e + snew
            ypre_ref[:, gs] = y
            part = y * sz[:, gs]
            out_ref[:, gs] = part * _rms(part) * ng[:, gs]

    return pl.pallas_call(
        body, name="ssd_fwd", grid=(N_CHUNKS,),
        in_specs=[pl.BlockSpec((CHUNK, D_CONV), lambda c: (c, 0)), pl.BlockSpec((CHUNK, D_SSM), lambda c: (c, zcol)),
                  pl.BlockSpec((CHUNK, LANE), lambda c: (c, dtcol)), _full((1, LANE)), _full((1, LANE)), _full((1, LANE)),
                  _full((1, D_SSM))],
        out_specs=[pl.BlockSpec((CHUNK, D_SSM), lambda c: (c, 0)), pl.BlockSpec((CHUNK, D_SSM), lambda c: (c, 0)),
                   pl.BlockSpec((1, SSM_GROUPS, SSM_STATE, GROUP_W), lambda c: (c, 0, 0, 0))],
        out_shape=[jax.ShapeDtypeStruct((SEQ, D_SSM), f32), jax.ShapeDtypeStruct((SEQ, D_SSM), f32),
                   jax.ShapeDtypeStruct((N_CHUNKS, SSM_GROUPS, SSM_STATE, GROUP_W), f32)],
        scratch_shapes=[pltpu.VMEM((SSM_GROUPS, SSM_STATE, GROUP_W), f32)],
        compiler_params=_params("arbitrary"),
    )(act, proj, proj, dt_bias, a_log, d_skip, norm_g)


def ssd_bwd_g(act, proj, ypre, states, d_out, dt_bias, a_log, d_skip, norm_g):
    zcol, dtcol = OFF_Z // D_SSM, OFF_DT // LANE

    def body(act_ref, z_ref, dt_ref, ypre_ref, st_ref, do_ref, dtb_ref, alog_ref, dsk_ref, ng_ref,
             dact_ref, ddt_ref, dz_ref, dng_ref, dpar_ref, dstate):
        i = pl.program_id(0)

        @pl.when(i == 0)
        def _():
            for ref in (dng_ref, dpar_ref, dstate):
                ref[...] = jnp.zeros_like(ref)

        row, col, dtp, a_row, cs, cs_t, cs_last = _ssd_chunk_common(dt_ref[...], dtb_ref[...], alog_ref[...])
        upper = (row <= col).astype(f32)
        lane = lax.broadcasted_iota(jnp.int32, (CHUNK, LANE), 1)
        rowl = lax.broadcasted_iota(jnp.int32, (CHUNK, LANE), 0)
        e_cs = jnp.exp(cs)
        dte = jnp.exp(cs_last - cs)
        ecl = jnp.exp(cs_last)
        rows8 = jnp.concatenate([ecl, dsk_ref[...], jnp.zeros((6, LANE), f32)], axis=0)
        z = z_ref[...]
        sgz = _sigmoid(z)
        sz = z * sgz
        ng = ng_ref[...]
        ddt_mat = jnp.zeros((CHUNK, LANE), f32)
        dcs_mat = jnp.zeros((CHUNK, LANE), f32)
        dcs_t = jnp.zeros((LANE, CHUNK), f32)
        dcsl_row = jnp.zeros((1, LANE), f32)
        dd_row = jnp.zeros((1, LANE), f32)
        for g in range(SSM_GROUPS):
            gs = slice(g * GROUP_W, (g + 1) * GROUP_W)
            bsl = slice(D_SSM + g * SSM_STATE, D_SSM + (g + 1) * SSM_STATE)
            csl = slice(D_SSM + D_BC + g * SSM_STATE, D_SSM + D_BC + (g + 1) * SSM_STATE)
            ind = _group_indicator(g)
            y = ypre_ref[:, gs]
            part = y * sz[:, gs]
            r = _rms(part)
            yhat = part * r
            d_o = do_ref[:, gs]
            dng_ref[:, gs] += jnp.sum(d_o * yhat, axis=0, keepdims=True)
            dyz = _rms_bwd(d_o, yhat, r, ng[:, gs])
            dy = dyz * sz[:, gs]
            dz_ref[:, gs] = dyz * y * (sgz[:, gs] * (1.0 + z[:, gs] * (1.0 - sgz[:, gs])))

            xg = act_ref[:, gs]
            bg = act_ref[:, bsl]
            cg = act_ref[:, csl]
            dt_e, e_e, dte_e = _hdot_nt(dtp, ind), _hdot_nt(e_cs, ind), _hdot_nt(dte, ind)
            rows_e = _hdot_nt(rows8, ind)
            ecl_e, dsk_e = rows_e[0:1], rows_e[1:2]
            xdt = xg * dt_e
            prev = st_ref[0, g]
            dh = dstate[g]
            heads = range(g * HEADS_PER_GROUP, (g + 1) * HEADS_PER_GROUP)
            hsl = [slice(r_ * SSM_HEAD_DIM, (r_ + 1) * SSM_HEAD_DIM) for r_ in range(HEADS_PER_GROUP)]
            cb = _bdot_nt(cg, bg)
            lms = [_seg_decay(cs, cs_t, hd, row, col) for hd in heads]
            ms = [cb * lm for lm in lms]
            gmat = _bdot(cg, prev)
            dgm = dy * e_e
            dcg = _bdot_nt(dgm, prev)
            dprev = _bdot_tn(cg, dgm)
            dbg = _bdot_nt(xdt * dte_e, dh)
            dw = _bdot(bg, dh)
            dms = [_bdot_nt(dy[:, s_], xdt[:, s_]) for s_ in hsl]
            dxdts = [_bdot_tn(m, dy[:, s_]) for m, s_ in zip(ms, hsl)]
            dxdt = jnp.concatenate(dxdts, axis=1) + dw * dte_e
            dact_ref[:, gs] = dy * dsk_e + dxdt * dt_e
            dstate[g] = dprev + dh * ecl_e
            dcb = jnp.zeros((CHUNK, CHUNK), f32)
            for hd, dm, lm, m in zip(heads, dms, lms, ms):
                dcb = dcb + dm * lm
                dseg = dm * m
                dcs_mat = dcs_mat + jnp.where(lane == hd, jnp.sum(dseg, axis=1, keepdims=True), 0.0)
                dcs_t = jnp.where(row == hd, jnp.sum(dseg, axis=0, keepdims=True), dcs_t)
            dact_ref[:, bsl] = dbg + _bdot_tn(dcb, cg)
            dact_ref[:, csl] = dcg + _bdot(dcb, bg)
            ddte = _hdot(dw * xdt, ind) * dte
            dcs_mat = dcs_mat + _hdot(dy * gmat, ind) * e_cs - ddte
            ddt_mat = ddt_mat + _hdot(dxdt * xg, ind)
            dcsl_row = dcsl_row + jnp.sum(ddte, axis=0, keepdims=True) + jnp.sum(_hdot(dh * prev, ind), axis=0, keepdims=True) * ecl
            dd_row = dd_row + jnp.sum(_hdot(dy * xg, ind), axis=0, keepdims=True)
        dcs_mat = dcs_mat - dcs_t.T + jnp.where(rowl == CHUNK - 1, dcsl_row, 0.0)
        dda = _hdot(upper, dcs_mat)
        ddt_mat = ddt_mat + dda * a_row
        da_row = jnp.sum(dda * dtp, axis=0, keepdims=True)
        ddt_raw = ddt_mat * _sigmoid(dt_ref[...] + dtb_ref[...])
        ddt_ref[...] = ddt_raw
        dpar_ref[0:1, :] += jnp.sum(ddt_raw, axis=0, keepdims=True)
        dpar_ref[1:2, :] += da_row * a_row
        dpar_ref[2:3, :] += dd_row

    blk = lambda i: N_CHUNKS - 1 - i
    return pl.pallas_call(
        body, name="ssd_bwd", grid=(N_CHUNKS,),
        in_specs=[pl.BlockSpec((CHUNK, D_CONV), lambda i: (blk(i), 0)), pl.BlockSpec((CHUNK, D_SSM), lambda i: (blk(i), zcol)),
                  pl.BlockSpec((CHUNK, LANE), lambda i: (blk(i), dtcol)), pl.BlockSpec((CHUNK, D_SSM), lambda i: (blk(i), 0)),
                  pl.BlockSpec((1, SSM_GROUPS, SSM_STATE, GROUP_W), lambda i: (blk(i), 0, 0, 0)),
                  pl.BlockSpec((CHUNK, D_SSM), lambda i: (blk(i), 0)),
                  _full((1, LANE)), _full((1, LANE)), _full((1, LANE)), _full((1, D_SSM))],
        out_specs=[pl.BlockSpec((CHUNK, D_CONV), lambda i: (blk(i), 0)), pl.BlockSpec((CHUNK, LANE), lambda i: (blk(i), 0)),
                   pl.BlockSpec((CHUNK, D_SSM), lambda i: (blk(i), 0)), _full((1, D_SSM)), _full((8, LANE))],
        out_shape=[jax.ShapeDtypeStruct((SEQ, D_CONV), f32), jax.ShapeDtypeStruct((SEQ, LANE), f32),
                   jax.ShapeDtypeStruct((SEQ, D_SSM), f32), jax.ShapeDtypeStruct((1, D_SSM), f32),
                   jax.ShapeDtypeStruct((8, LANE), f32)],
        scratch_shapes=[pltpu.VMEM((SSM_GROUPS, SSM_STATE, GROUP_W), f32)],
        compiler_params=_params("arbitrary"),
    )(act, proj, proj, ypre, states, d_out, dt_bias, a_log, d_skip, norm_g)


def out_fwd(x, attn, ssm, w_out, tm=512):
    def body(x_ref, a_ref, s_ref, w_ref, o_ref):
        o_ref[...] = x_ref[...] + _bdot(a_ref[...], w_ref[:D_ATTN, :]) + _bdot(s_ref[...], w_ref[D_ATTN:, :])

    tok = lambda w_: pl.BlockSpec((tm, w_), lambda i: (i, 0))
    return pl.pallas_call(
        body, name="out_fwd", grid=(SEQ // tm,),
        in_specs=[tok(D_MODEL), tok(D_ATTN), tok(D_SSM), _full((D_MODEL, D_MODEL))],
        out_specs=tok(D_MODEL), out_shape=jax.ShapeDtypeStruct((SEQ, D_MODEL), f32),
        compiler_params=_params("arbitrary"),
    )(x, attn, ssm, w_out)


def out_bwd(dx1, attn, ssm, w_out, tm=512):
    nt = SEQ // tm

    def body(d_ref, a_ref, s_ref, w_ref, da_ref, ds_ref, dw_ref, dw16_ref):
        i = pl.program_id(0)

        @pl.when(i == 0)
        def _():
            dw_ref[...] = jnp.zeros_like(dw_ref)

        d = d_ref[...].astype(bf16)
        dcat = _bdot_nt(d, w_ref[...])
        da_ref[...] = dcat[:, :D_ATTN]
        ds_ref[...] = dcat[:, D_ATTN:]
        dw_ref[:D_ATTN, :] += _bdot_tn(a_ref[...], d)
        dw_ref[D_ATTN:, :] += _bdot_tn(s_ref[...], d)

        @pl.when(i == nt - 1)
        def _():
            dw16_ref[...] = dw_ref[...].astype(bf16)

    tok = lambda w_: pl.BlockSpec((tm, w_), lambda i: (i, 0))
    return pl.pallas_call(
        body, name="out_bwd", grid=(nt,),
        in_specs=[tok(D_MODEL), tok(D_ATTN), tok(D_SSM), _full((D_MODEL, D_MODEL))],
        out_specs=[tok(D_ATTN), tok(D_SSM), _full((D_MODEL, D_MODEL)), _full((D_MODEL, D_MODEL))],
        out_shape=[jax.ShapeDtypeStruct((SEQ, D_ATTN), f32), jax.ShapeDtypeStruct((SEQ, D_SSM), f32),
                   jax.ShapeDtypeStruct((D_MODEL, D_MODEL), f32), jax.ShapeDtypeStruct((D_MODEL, D_MODEL), bf16)],
        compiler_params=_params("arbitrary"),
    )(dx1, attn, ssm, w_out)


def mlp_fwd(x1, g, w_up, w_down, tm=1024):
    def body(x_ref, g_ref, wu_ref, wd_ref, o_ref, u_ref, h_scr):
        j = pl.program_id(1)

        @pl.when(j == 0)
        def _():
            xv = x_ref[...]
            h_scr[...] = (xv * _rms(xv) * g_ref[...]).astype(bf16)
            o_ref[...] = xv

        u = jnp.dot(h_scr[...], wu_ref[...], preferred_element_type=f32)
        u_ref[...] = u
        a = jnp.square(jnp.maximum(u, 0.0))
        o_ref[...] += _bdot(a, wd_ref[...])

    return pl.pallas_call(
        body, name="mlp_fwd", grid=(SEQ // tm, N_CHIPS),
        in_specs=[pl.BlockSpec((tm, D_MODEL), lambda i, j: (i, 0)), _full((1, D_MODEL)),
                  pl.BlockSpec((None, D_MODEL, FF_TILE), lambda i, j: (j, 0, 0)),
                  pl.BlockSpec((None, FF_TILE, D_MODEL), lambda i, j: (j, 0, 0))],
        out_specs=[pl.BlockSpec((tm, D_MODEL), lambda i, j: (i, 0)), pl.BlockSpec((tm, FF_TILE), lambda i, j: (i, j))],
        out_shape=[jax.ShapeDtypeStruct((SEQ, D_MODEL), f32), jax.ShapeDtypeStruct((SEQ, D_FF), f32)],
        scratch_shapes=[pltpu.VMEM((tm, D_MODEL), bf16)],
        compiler_params=_params("arbitrary", "arbitrary"),
    )(x1, g, w_up, w_down)


def mlp_bwd_data(dx2, u, x1, g, w_up, w_down, tm=1024):
    def body(d_ref, u_ref, x_ref, g_ref, wu_ref, wd_ref, dx_ref, du_ref, dg_ref, dh_scr):
        i, j = pl.program_id(0), pl.program_id(1)

        @pl.when(jnp.logical_and(i == 0, j == 0))
        def _():
            dg_ref[...] = jnp.zeros_like(dg_ref)

        @pl.when(j == 0)
        def _():
            dh_scr[...] = jnp.zeros_like(dh_scr)

        da = _bdot_nt(d_ref[...], wd_ref[...])
        du = (da * (2.0 * jnp.maximum(u_ref[...], 0.0))).astype(bf16)
        du_ref[...] = du
        dh_scr[...] += _bdot_nt(du, wu_ref[...])

        @pl.when(j == N_CHIPS - 1)
        def _():
            xv = x_ref[...]
            r = _rms(xv)
            xhat = xv * r
            dh = dh_scr[...]
            dg_ref[...] += jnp.sum(dh * xhat, axis=0, keepdims=True)
            dx_ref[...] = d_ref[...] + _rms_bwd(dh, xhat, r, g_ref[...])

    return pl.pallas_call(
        body, name="mlp_bwd_data", grid=(SEQ // tm, N_CHIPS),
        in_specs=[pl.BlockSpec((tm, D_MODEL), lambda i, j: (i, 0)), pl.BlockSpec((tm, FF_TILE), lambda i, j: (i, j)),
                  pl.BlockSpec((tm, D_MODEL), lambda i, j: (i, 0)), _full((1, D_MODEL)),
                  pl.BlockSpec((None, D_MODEL, FF_TILE), lambda i, j: (j, 0, 0)),
                  pl.BlockSpec((None, FF_TILE, D_MODEL), lambda i, j: (j, 0, 0))],
        out_specs=[pl.BlockSpec((tm, D_MODEL), lambda i, j: (i, 0)), pl.BlockSpec((tm, FF_TILE), lambda i, j: (i, j)),
                   _full((1, D_MODEL))],
        out_shape=[jax.ShapeDtypeStruct((SEQ, D_MODEL), f32), jax.ShapeDtypeStruct((SEQ, D_FF), bf16),
                   jax.ShapeDtypeStruct((1, D_MODEL), f32)],
        scratch_shapes=[pltpu.VMEM((tm, D_MODEL), f32)],
        compiler_params=_params("arbitrary", "arbitrary"),
    )(dx2, u, x1, g, w_up, w_down)


def mlp_bwd_weights(dx2, u, du, x1, g, tm=512):
    nt = SEQ // tm

    def body(d_ref, u_ref, du_ref, x_ref, g_ref, dwu_ref, dwd_ref, dwu16_ref, dwd16_ref, h_scr, d_scr):
        j, i = pl.program_id(0), pl.program_id(1)

        @pl.when(j == 0)
        def _():
            xv = x_ref[...]
            h_scr[i] = (xv * _rms(xv) * g_ref[...]).astype(bf16)
            d_scr[i] = d_ref[...].astype(bf16)

        @pl.when(i == 0)
        def _():
            dwu_ref[...] = jnp.zeros_like(dwu_ref)
            dwd_ref[...] = jnp.zeros_like(dwd_ref)

        dwu_ref[...] += _bdot_tn(h_scr[i], du_ref[...])
        a = jnp.square(jnp.maximum(u_ref[...], 0.0))
        dwd_ref[...] += _bdot_tn(a, d_scr[i])

        @pl.when(i == nt - 1)
        def _():
            dwu16_ref[...] = dwu_ref[...].astype(bf16)
            dwd16_ref[...] = dwd_ref[...].astype(bf16)

    up = pl.BlockSpec((None, D_MODEL, FF_TILE), lambda j, i: (j, 0, 0))
    down = pl.BlockSpec((None, FF_TILE, D_MODEL), lambda j, i: (j, 0, 0))
    first_pass = pl.BlockSpec((tm, D_MODEL), lambda j, i: (jnp.where(j == 0, i, nt - 1), 0))
    return pl.pallas_call(
        body, name="mlp_bwd_weights", grid=(N_CHIPS, nt),
        in_specs=[first_pass, pl.BlockSpec((tm, FF_TILE), lambda j, i: (i, j)),
                  pl.BlockSpec((tm, FF_TILE), lambda j, i: (i, j)), first_pass, _full((1, D_MODEL))],
        out_specs=[up, down, up, down],
        out_shape=[jax.ShapeDtypeStruct((N_CHIPS, D_MODEL, FF_TILE), f32), jax.ShapeDtypeStruct((N_CHIPS, FF_TILE, D_MODEL), f32),
                   jax.ShapeDtypeStruct((N_CHIPS, D_MODEL, FF_TILE), bf16), jax.ShapeDtypeStruct((N_CHIPS, FF_TILE, D_MODEL), bf16)],
        scratch_shapes=[pltpu.VMEM((nt, tm, D_MODEL), bf16), pltpu.VMEM((nt, tm, D_MODEL), bf16)],
        compiler_params=_params("arbitrary", "arbitrary"),
    )(dx2, u, du, x1, g)


def loss_head(y, target, tm=512):
    def body(y_ref, t_ref, dy_ref, l_ref):
        @pl.when(pl.program_id(0) == 0)
        def _():
            l_ref[...] = jnp.zeros_like(l_ref)

        d = y_ref[...] - t_ref[...]
        dy_ref[...] = d * (1.0 / D_MODEL)
        part = jnp.sum(jnp.mean(d * d, axis=-1, keepdims=True), axis=0, keepdims=True)
        l_ref[...] += 0.5 * part

    tok = pl.BlockSpec((tm, D_MODEL), lambda i: (i, 0))
    return pl.pallas_call(
        body, name="loss_head", grid=(SEQ // tm,), in_specs=[tok, tok], out_specs=[tok, _full((1, 1))],
        out_shape=[jax.ShapeDtypeStruct((SEQ, D_MODEL), f32), jax.ShapeDtypeStruct((1, 1), f32)],
        compiler_params=_params("arbitrary"),
    )(y, target)


def _pad_lane(v):
    return jnp.pad(v, (0, LANE - v.shape[0]))[None, :]


def local_step(x, target, w, prov):
    bucket = jnp.asarray(_bucket_table().T)
    bias = bias_build(w["rel_bias"], bucket)
    saved = []
    for l in range(DEPTH):
        g_mix = w["mix_norm_g"][l][None, :] + prov.stage(("begin", l), x)
        w_in = prov.w_in(l, x)
        proj = in_fwd(x, g_mix, w_in)
        conv_b = w["conv_b"][l][None, :]
        act = conv_fwd(proj, w["conv_w"][l], conv_b)
        dtb = _pad_lane(w["dt_bias"][l]) + prov.stage(("mid", l), act)
        alog, dsk = _pad_lane(w["a_log"][l]), _pad_lane(w["d_skip"][l])
        ng = w["ssm_norm_g"][l][None, :]
        ssm, ypre, states = ssd_fwd_g(act, proj, dtb, alog, dsk, ng)
        qg, kg = w["q_gain"][l][:, None] + 0.0 * ssm[:1, :1], w["k_gain"][l][None, :]
        attn = attn_fwd_t(proj, qg, kg, w["sinks"][l], bias)
        tok = prov.stage(("pre_out", l), attn)
        w_out = prov.w_out(l, attn) + jnp.asarray(tok, bf16)
        x1 = out_fwd(x, attn, ssm, w_out)
        g_mlp = w["mlp_norm_g"][l][None, :] + prov.stage(("pre_mlp", l), x1)
        w_up, w_down = prov.mlp(l, x1)
        x2, u = mlp_fwd(x1, g_mlp, w_up, w_down)
        saved.append(dict(x=x, proj=proj, attn=attn, act=act, ssm=ssm, ypre=ypre, states=states, x1=x1, u=u,
                          g_mix=g_mix, qg=qg, kg=kg, conv_b=conv_b, dtb=dtb, alog=alog, dsk=dsk, ng=ng, g_mlp=g_mlp,
                          w_in=w_in, w_out=w_out, w_up=w_up, w_down=w_down))
        x = x2
    dx, loss = loss_head(x, target)
    grads = [None] * DEPTH
    dbands = [None] * DEPTH
    tok = 0.0
    for l in reversed(range(DEPTH)):
        s = saved[l]
        g_mlp = s["g_mlp"] + tok
        dx1, du, dg_mlp = mlp_bwd_data(dx, s["u"], s["x1"], g_mlp, s["w_up"], s["w_down"])
        dw_up, dw_down, dw_up16, dw_down16 = mlp_bwd_weights(dx, s["u"], du, s["x1"], g_mlp)
        tok = prov.grads(("mlp", l), dict(w_up=(dw_up, dw_up16), w_down=(dw_down, dw_down16)), dw_down)
        dattn, dssm, dw_out, dw_out16 = out_bwd(dx1, s["attn"], s["ssm"], s["w_out"])
        dact, ddt, dz, dng, dpar = ssd_bwd_g(s["act"], s["proj"], s["ypre"], s["states"], dssm, s["dtb"] + tok, s["alog"],
                                           s["dsk"], s["ng"])
        conv_b = s["conv_b"] + prov.stage(("bwd_mid", l), dact)
        dxbc, dconv_w, dconv_b = conv_bwd(s["proj"], dact, w["conv_w"][l], conv_b)
        dq, dk, dv, dband, dsink, dqg, dkg = attn_bwd_t(s["proj"], dattn, s["qg"], s["kg"], w["sinks"][l], bias)
        dbands[l] = dband
        g_mix = s["g_mix"]
        if l == 0:
            d_rel = bias_bwd(dbands[0], dbands[1], bucket)
            g_mix = g_mix + 0.0 * d_rel[:1, :1]
        dx, dw_in, dg_mix = in_bwd(dq, dz, dxbc, dk, dv, ddt, s["x"], g_mix, s["w_in"], dx1)
        tok = prov.grads(("mix", l), dict(w_in=split_w_in_grad(dw_in), w_out=(dw_out, dw_out16)), dx)
        grads[l] = dict(mix_norm_g=dg_mix[0], q_gain=dqg[:, 0], k_gain=dkg[0], sinks=dsink[:, 0],
                        conv_w=dconv_w, conv_b=dconv_b[0], dt_bias=dpar[0, :SSM_HEADS], a_log=dpar[1, :SSM_HEADS],
                        d_skip=dpar[2, :SSM_HEADS], ssm_norm_g=dng[0], mlp_norm_g=dg_mlp[0])
    out = {k: jnp.stack([grads[l][k] for l in range(DEPTH)]) for k in grads[0]}
    out["rel_bias"] = d_rel[:, :N_Q_HEADS]
    return loss, dx, out, tok


MESH = pl.DeviceIdType.MESH
HBM = pl.BlockSpec(memory_space=pltpu.HBM)
N_PEER_CHIPS = N_CHIPS - 1
N_DEVICES = 8


def _coords():
    return lax.axis_index("x"), lax.axis_index("y"), lax.axis_index("c")


def _peer_chips(x, y):
    return [(1 - x, y), (x, 1 - y), (1 - x, 1 - y)]


def _remote(src, dst, send_sem, recv_sem, device):
    return pltpu.make_async_remote_copy(src_ref=src, dst_ref=dst, send_sem=send_sem, recv_sem=recv_sem,
                                        device_id=device, device_id_type=MESH)


SEM = pl.BlockSpec(memory_space=pltpu.SEMAPHORE)
ANY = pl.BlockSpec(memory_space=pl.ANY)
DATAFLOW = pltpu.SideEffectType.DATAFLOW_SIDE_EFFECTING


def _gather_copies(kind, src_refs, land_refs, ssem, rsem):
    x, y, c = _coords()
    k_me = 2 * x + y
    n = len(land_refs)
    cps = []
    for p, land in enumerate(land_refs):
        hr = land.shape[1] // 2
        rows = pl.ds(c * hr, hr)
        for j, chip in enumerate(_peer_chips(x, y)):
            i = 3 * p + j
            if kind == "ici":
                cps.append(_remote(src_refs[p].at[rows, :], land.at[k_me, rows, :], ssem.at[i], rsem.at[i], (*chip, c)))
            else:
                got = land.at[2 * chip[0] + chip[1], rows, :]
                cps.append(_remote(got, got, ssem.at[i], rsem.at[i], (x, y, 1 - c)))
        if kind == "relay":
            cps.append(_remote(src_refs[p], land.at[k_me], ssem.at[3 * n + p], rsem.at[3 * n + p], (x, y, 1 - c)))
    return cps


def gather_now(srcs, conv):
    n = len(srcs)

    def body(*refs):
        src_refs, conv_ref = refs[:n], refs[n]
        lands, gconv = refs[n + 1:2 * n + 1], refs[2 * n + 1]
        ssem, rsem, fsem, frsem, csem, crsem = refs[2 * n + 2:]
        x, y, c = _coords()
        k_me = 2 * x + y
        targets = [(*chip, c) for chip in _peer_chips(x, y)] + [(x, y, 1 - c)]
        ici = _gather_copies("ici", src_refs, lands, ssem, rsem)
        relay = _gather_copies("relay", src_refs, lands, fsem, frsem)
        passed = [cp for i, cp in enumerate(relay) if i % 4 != 3]
        own = relay[3::4]
        conv_cps = [_remote(conv_ref, gconv.at[k_me], csem.at[j], crsem.at[j], t) for j, t in enumerate(targets)]
        for cp in ici + conv_cps + own:
            cp.start()
        for cp, fw in zip(ici, passed):
            cp.wait_recv()
            fw.start()
        for cp in conv_cps + relay:
            cp.wait_recv()
        for cp in ici + relay + conv_cps:
            cp.wait_send()

    out_shape = [jax.ShapeDtypeStruct((N_CHIPS,) + s.shape, s.dtype) for s in srcs]
    out_shape.append(jax.ShapeDtypeStruct((N_CHIPS,) + conv.shape, conv.dtype))
    sems = lambda k: pltpu.SemaphoreType.DMA((k,))
    return pl.pallas_call(
        body, name="gather_now", out_shape=out_shape, in_specs=[HBM] * (n + 1), out_specs=[HBM] * (n + 1),
        scratch_shapes=[sems(3 * n), sems(3 * n), sems(4 * n), sems(4 * n), sems(N_CHIPS), sems(N_CHIPS)],
    )(*srcs, conv)


def _gather_maker(kind, n_src):
    def make(refs, ssem, rsem):
        cps = _gather_copies(kind, refs[:n_src], refs[n_src:], ssem, rsem)
        return cps, cps
    return make


def _scatter_maker(n):
    def make(refs, ssem, rsem):
        x, y, c = _coords()
        k_me = 2 * x + y
        sends, arrivals = [], []
        for p in range(n):
            src, land = refs[p], refs[n + p]
            sends.append(_remote(src.at[k_me, 1 - c], land.at[0], ssem.at[7 * p], rsem.at[7 * p], (x, y, 1 - c)))
            for j, chip in enumerate(_peer_chips(x, y)):
                for cc in range(2):
                    sends.append(_remote(src.at[2 * chip[0] + chip[1], cc], land.at[1 + 2 * j + c],
                                         ssem.at[7 * p + 1 + 2 * j + cc], rsem.at[7 * p + 1 + 2 * j + c], (*chip, cc)))
            for s in range(7):
                arrivals.append(_remote(land.at[s], land.at[s], ssem.at[7 * p + s], rsem.at[7 * p + s], (x, y, 1 - c)))
        return sends, arrivals
    return make


def _share_maker(n):
    def make(refs, ssem, rsem):
        x, y, c = _coords()
        sends = [_remote(refs[p].at[c], refs[p].at[c], ssem.at[p], rsem.at[p], (x, y, 1 - c)) for p in range(n)]
        arrivals = [_remote(refs[p].at[1 - c], refs[p].at[1 - c], ssem.at[p], rsem.at[p], (x, y, 1 - c)) for p in range(n)]
        return sends, arrivals
    return make


def split_start(name, make, n_sems, operands, after):
    n = len(operands)

    def body(*refs):
        ssem, rsem, token = refs[n + 1], refs[n + 2], refs[-1]
        for cp in make(refs[:n], ssem, rsem)[0]:
            cp.start()
        token[...] = jnp.zeros_like(token)

    ops = [pltpu.with_memory_space_constraint(a, pltpu.HBM) for a in operands]
    outs = pl.pallas_call(
        body, name=name,
        out_shape=(pltpu.SemaphoreType.DMA((n_sems,)), pltpu.SemaphoreType.DMA((n_sems,)),
                   *[pltpu.HBM(a.shape, a.dtype) for a in ops], jax.ShapeDtypeStruct((8, LANE), f32)),
        in_specs=[HBM] * n + [ANY], out_specs=(SEM, SEM, *[HBM] * n, pl.BlockSpec(memory_space=pltpu.VMEM)),
        input_output_aliases={i: 2 + i for i in range(n)},
        compiler_params=pltpu.CompilerParams(has_side_effects=DATAFLOW),
    )(*ops, after)
    return dict(name=name, make=make, ssem=outs[0], rsem=outs[1], operands=outs[2:2 + n], token=outs[-1][0, 0])


def split_wait(handle, after):
    n = len(handle["operands"])

    def body(*refs):
        sends, arrivals = handle["make"](refs[:n], refs[n], refs[n + 1])
        for cp in sends:
            cp.wait_send()
        for cp in arrivals:
            cp.wait_recv()

    outs = pl.pallas_call(
        body, name=handle["name"].replace("start", "wait"),
        out_shape=tuple(pltpu.HBM(a.shape, a.dtype) for a in handle["operands"]),
        in_specs=[HBM] * n + [SEM, SEM, ANY], out_specs=tuple([HBM] * n),
        input_output_aliases={i: i for i in range(n)},
        compiler_params=pltpu.CompilerParams(has_side_effects=DATAFLOW),
    )(*handle["operands"], handle["ssem"], handle["rsem"], after)
    return list(outs)


def piece_sum(g, recv, kc_arr):
    _, _, rb, cc = g.shape
    tr = min(256, rb)

    def body(kc_ref, g_ref, r_ref, o_ref):
        acc = g_ref[...]
        for s in range(7):
            acc = acc + r_ref[s].astype(f32)
        o_ref[...] = acc

    return pl.pallas_call(
        body, name="piece_sum",
        grid_spec=pltpu.PrefetchScalarGridSpec(
            num_scalar_prefetch=1, grid=(rb // tr,),
            in_specs=[pl.BlockSpec((None, None, tr, cc), lambda r, kc: (kc[0], kc[1], r, 0)),
                      pl.BlockSpec((7, tr, cc), lambda r, kc: (0, r, 0))],
            out_specs=pl.BlockSpec((None, tr, cc), lambda r, kc: (kc[1], r, 0))),
        out_shape=jax.ShapeDtypeStruct((2, rb, cc), f32),
        compiler_params=_params("arbitrary"),
    )(kc_arr, g, recv)


def small_all_reduce(vec):
    def body(v_ref, o_ref, gat, ssem, rsem):
        x, y, c = _coords()
        me = 4 * x + 2 * y + c
        gat[me] = v_ref[...]
        sends = []
        for t in range(1, N_DEVICES):
            peer = (x ^ (t >> 2), y ^ ((t >> 1) & 1), c ^ (t & 1))
            cp = _remote(v_ref, gat.at[me], ssem.at[t - 1], rsem.at[t - 1], peer)
            cp.start()
            sends.append(cp)
        for t in range(1, N_DEVICES):
            peer = (x ^ (t >> 2), y ^ ((t >> 1) & 1), c ^ (t & 1))
            slot = gat.at[4 * peer[0] + 2 * peer[1] + peer[2]]
            _remote(slot, slot, ssem.at[t - 1], rsem.at[t - 1], peer).wait_recv()
        for cp in sends:
            cp.wait_send()
        acc = gat[0]
        for d in range(1, N_DEVICES):
            acc = acc + gat[d]
        o_ref[...] = acc

    return pl.pallas_call(
        body, name="small_all_reduce", out_shape=jax.ShapeDtypeStruct(vec.shape, vec.dtype),
        in_specs=[pl.BlockSpec(memory_space=pltpu.VMEM)], out_specs=pl.BlockSpec(memory_space=pltpu.VMEM),
        scratch_shapes=[pltpu.VMEM((N_DEVICES,) + vec.shape, vec.dtype), pltpu.SemaphoreType.DMA((N_DEVICES - 1,)),
                        pltpu.SemaphoreType.DMA((N_DEVICES - 1,))],
    )(vec)


def _adamw_math(w, g, m, v):
    m_new = ADAM_B1 * m + (1.0 - ADAM_B1) * g
    v_new = ADAM_B2 * v + (1.0 - ADAM_B2) * jnp.square(g)
    m_hat = m_new / (1.0 - ADAM_B1 ** ADAM_STEP)
    v_hat = v_new / (1.0 - ADAM_B2 ** ADAM_STEP)
    delta = -ADAM_LR * (m_hat / (jnp.sqrt(v_hat) + ADAM_EPS) + ADAM_WD * w)
    return delta, m_new, v_new


def adamw_shard(w, g0, g1, m, v):
    depth, rows, cols = w.shape
    half = rows // 2
    tr = min(256, half)
    nr = half // tr

    def body(w_ref, g0_ref, g1_ref, m_ref, v_ref, go_ref, d_ref, nm_ref, nv_ref):
        gv = jnp.where(pl.program_id(0) == 0, g0_ref[...], g1_ref[...])
        go_ref[...] = gv
        d_ref[...], nm_ref[...], nv_ref[...] = _adamw_math(w_ref[...], gv, m_ref[...], v_ref[...])

    spec = pl.BlockSpec((None, tr, cols), lambda l, h, r: (l, h * nr + r, 0))
    g0spec = pl.BlockSpec((None, tr, cols), lambda l, h, r: (jnp.where(l == 0, h, 1), jnp.where(l == 0, r, nr - 1), 0))
    g1spec = pl.BlockSpec((None, tr, cols), lambda l, h, r: (jnp.where(l == 1, h, 0), jnp.where(l == 1, r, 0), 0))
    return pl.pallas_call(
        body, name="adamw_shard", grid=(depth, 2, nr), in_specs=[spec, g0spec, g1spec, spec, spec], out_specs=[spec] * 4,
        out_shape=[jax.ShapeDtypeStruct(w.shape, f32)] * 4,
        compiler_params=_params("arbitrary", "arbitrary", "arbitrary"),
    )(w, g0, g1, m, v)


def adamw_small(w, g, m, v):
    def body(w_ref, g_ref, m_ref, v_ref, d_ref, nm_ref, nv_ref):
        d_ref[...], nm_ref[...], nv_ref[...] = _adamw_math(w_ref[...], g_ref[...], m_ref[...], v_ref[...])

    return pl.pallas_call(
        body, name="adamw_small", out_shape=[jax.ShapeDtypeStruct(w.shape, f32)] * 3,
    )(w, g, m, v)


WEIGHTS = ("mix_norm_g", "w_in", "q_gain", "k_gain", "sinks", "rel_bias", "conv_w", "conv_b", "dt_bias", "a_log", "d_skip",
           "ssm_norm_g", "w_out", "mlp_norm_g", "w_up", "w_down")
BIG = ("w_in", "w_out", "w_up", "w_down")
SMALL = tuple(n for n in WEIGHTS if n not in BIG)
PACK_COLS = 1024
PACK_ROWS = 16


def _pack(named, last=None):
    flat = jnp.concatenate([named[n].reshape(-1) for n in SMALL])
    tail = jnp.zeros((1,), f32) if last is None else last.reshape(1)
    pad = jnp.zeros((PACK_ROWS * PACK_COLS - flat.shape[0] - 1,), f32)
    return jnp.concatenate([flat, pad, tail]).reshape(PACK_ROWS, PACK_COLS)


def _unpack(buf, shapes):
    flat = buf.reshape(-1)
    out, at = {}, 0
    for n in SMALL:
        size = int(np.prod(shapes[n]))
        out[n] = flat[at:at + size].reshape(shapes[n])
        at += size
    return out


class _Exchange:
    GROUPS = {"A": (("w_up", 0), ("w_down", 0)), "B": (("w_in", 1), ("w_out", 1)), "C": (("w_up", 1), ("w_down", 1))}
    ICI_AT = {("mid", 0): "B", ("pre_out", 0): "C"}
    RELAY_AT = {("pre_out", 0): "A", ("pre_mlp", 0): "B", ("mid", 1): "C"}
    LAST = ("mix", 0)

    def __init__(self, wts, k_me, kc_arr):
        self.wts, self.k_me, self.kc_arr = wts, k_me, kc_arr
        self.own = {(n, l): wts[n][l].astype(bf16) for n in BIG for l in range(DEPTH)}
        now = gather_now([self.own["w_in", 0], self.own["w_out", 0]], wts["conv_w"])
        self.ready = {("w_in", 0): now[0], ("w_out", 0): now[1]}
        self.conv_w = jnp.transpose(now[2], (1, 2, 0, 3)).reshape(DEPTH, CONV_WIDTH, D_CONV)
        self.ici, self.relay = {}, {}
        self.gview, self.scatter, self.share, self.reduced = {}, [], [], {}
        self._start_ici("A", now[2])

    def _start_ici(self, g, after):
        srcs = [self.own[p] for p in self.GROUPS[g]]
        lands = [lax.empty((N_CHIPS,) + s.shape, s.dtype) for s in srcs]
        self.ici[g] = split_start("gather%s_ici_start" % g, _gather_maker("ici", len(srcs)), 3 * len(srcs), srcs + lands,
                                  after)
        return self.ici[g]["token"]

    def stage(self, name, after):
        if name == ("begin", 0):
            return self.ici["A"]["token"]
        tok = 0.0
        g = self.RELAY_AT.get(name)
        if g is not None:
            n = len(self.GROUPS[g])
            self.relay[g] = split_start("gather%s_relay_start" % g, _gather_maker("relay", n), 4 * n,
                                        split_wait(self.ici[g], after), after)
            tok = self.relay[g]["token"]
        if name in self.ICI_AT:
            tok = tok + self._start_ici(self.ICI_AT[name], after)
        return tok

    def _get(self, piece, after):
        if piece not in self.ready:
            g = [k for k, pieces in self.GROUPS.items() if piece in pieces][0]
            lands = split_wait(self.relay[g], after)[len(self.GROUPS[g]):]
            self.ready.update(zip(self.GROUPS[g], lands))
        return self.ready[piece]

    def w_in(self, l, after):
        return align_w_in(self._get(("w_in", l), after))

    def w_out(self, l, after):
        return self._get(("w_out", l), after).reshape(D_MODEL, D_MODEL)

    def mlp(self, l, after):
        return self._get(("w_up", l), after), self._get(("w_down", l), after)

    def _view(self, n, g):
        _, rows, cols = self.wts[n].shape
        return g.reshape(N_CHIPS, 2, rows // 2, cols)

    def grads(self, name, arrays, after):
        if name == self.LAST:
            self.held = (name, arrays)
            return 0.0
        return self._scatter(name, arrays, after) + self._advance(after, 1)

    def flush(self, after):
        return self._scatter(*self.held, after) + self._advance(after, 1)

    def _scatter(self, name, arrays, after):
        pieces = [(n, name[1]) for n in arrays]
        views = [self._view(n, g) for n, (g, _) in arrays.items()]
        sends = [g16.reshape(v.shape) for v, (_, g16) in zip(views, arrays.values())]
        self.gview.update(zip(pieces, views))
        lands = [lax.empty((7,) + v.shape[2:], bf16) for v in views]
        h = split_start("scatter_%s%d_start" % name, _scatter_maker(len(views)), 7 * len(views), sends + lands, after)
        self.scatter.append((pieces, h))
        return h["token"]

    def _take_share(self, after):
        pieces, h = self.share.pop(0)
        self.reduced.update(zip(pieces, split_wait(h, after)))

    def _take_scatter(self, after):
        pieces, h = self.scatter.pop(0)
        lands = split_wait(h, after)[len(pieces):]
        sums = [piece_sum(self.gview[p], land, self.kc_arr) for p, land in zip(pieces, lands)]
        hs = split_start(h["name"].replace("scatter", "share"), _share_maker(len(sums)), len(sums), sums, after)
        self.share.append((pieces, hs))
        return hs["token"]

    def _advance(self, after, newest):
        if self.share:
            self._take_share(after)
        return self._take_scatter(after) if len(self.scatter) > newest else 0.0

    def reduced_grads(self, names, after):
        want = [(n, l) for n in names for l in range(DEPTH)]
        while not all(p in self.reduced for p in want):
            if any(p in pieces for p in want for pieces, _ in self.share):
                self._take_share(after)
            else:
                self._take_scatter(after)
        return {n: [self.reduced[n, l] for l in range(DEPTH)] for n in names}


def kernel(x, mix_norm_g, w_in, q_gain, k_gain, sinks, rel_bias, conv_w, conv_b, dt_bias, a_log, d_skip, ssm_norm_g, w_out, mlp_norm_g, w_up, w_down, loss_target, m_mix_norm_g, m_w_in, m_q_gain, m_k_gain, m_sinks, m_rel_bias, m_conv_w, m_conv_b, m_dt_bias, m_a_log, m_d_skip, m_ssm_norm_g, m_w_out, m_mlp_norm_g, m_w_up, m_w_down, v_mix_norm_g, v_w_in, v_q_gain, v_k_gain, v_sinks, v_rel_bias, v_conv_w, v_conv_b, v_dt_bias, v_a_log, v_d_skip, v_ssm_norm_g, v_w_out, v_mlp_norm_g, v_w_up, v_w_down):
    wts = dict(mix_norm_g=mix_norm_g, w_in=w_in, q_gain=q_gain, k_gain=k_gain, sinks=sinks, rel_bias=rel_bias, conv_w=conv_w,
               conv_b=conv_b, dt_bias=dt_bias, a_log=a_log, d_skip=d_skip, ssm_norm_g=ssm_norm_g, w_out=w_out,
               mlp_norm_g=mlp_norm_g, w_up=w_up, w_down=w_down)
    mom = dict(mix_norm_g=m_mix_norm_g, w_in=m_w_in, q_gain=m_q_gain, k_gain=m_k_gain, sinks=m_sinks, rel_bias=m_rel_bias,
               conv_w=m_conv_w, conv_b=m_conv_b, dt_bias=m_dt_bias, a_log=m_a_log, d_skip=m_d_skip, ssm_norm_g=m_ssm_norm_g,
               w_out=m_w_out, mlp_norm_g=m_mlp_norm_g, w_up=m_w_up, w_down=m_w_down)
    var = dict(mix_norm_g=v_mix_norm_g, w_in=v_w_in, q_gain=v_q_gain, k_gain=v_k_gain, sinks=v_sinks, rel_bias=v_rel_bias,
               conv_w=v_conv_w, conv_b=v_conv_b, dt_bias=v_dt_bias, a_log=v_a_log, d_skip=v_d_skip, ssm_norm_g=v_ssm_norm_g,
               w_out=v_w_out, mlp_norm_g=v_mlp_norm_g, w_up=v_w_up, w_down=v_w_down)
    xi, yi, ci = _coords()
    k_me = 2 * xi + yi
    kc_arr = jnp.stack([k_me, ci]).astype(jnp.int32)

    prov = _Exchange(wts, k_me, kc_arr)
    small_w = {n: wts[n] for n in SMALL}
    small_w["conv_w"] = prov.conv_w
    loss, dx, grads, tok = local_step(x[0], loss_target[0], small_w, prov)

    small_shapes = {n: grads[n].shape for n in SMALL}
    small_sum = small_all_reduce(_pack(grads, loss) + tok)
    loss = small_sum[PACK_ROWS - 1, PACK_COLS - 1]
    tok = prov.flush(small_sum)
    small = _unpack(small_sum, small_shapes)
    cols = conv_w.shape[-1]
    small["conv_w"] = lax.dynamic_slice_in_dim(small["conv_w"], k_me * cols, cols, axis=2)
    g_out_d, d_out_d, m_out_d, v_out_d = {}, {}, {}, {}
    shard_shapes = {n: wts[n].shape for n in SMALL}
    d, nm, nv = adamw_small(_pack(wts), _pack(small) + tok, _pack(mom), _pack(var))
    for dst, buf in ((d_out_d, d), (m_out_d, nm), (v_out_d, nv)):
        dst.update(_unpack(buf, shard_shapes))
    g_out_d.update(small)

    after = d
    for names in (("w_up", "w_down"), ("w_in", "w_out")):
        for n, (g0, g1) in prov.reduced_grads(names, after).items():
            g_out_d[n], d_out_d[n], m_out_d[n], v_out_d[n] = adamw_shard(wts[n], g0, g1, mom[n], var[n])
            after = d_out_d[n]

    return (loss, dx[None], *[g_out_d[n] for n in WEIGHTS], *[d_out_d[n] for n in WEIGHTS],
            *[m_out_d[n] for n in WEIGHTS], *[v_out_d[n] for n in WEIGHTS])
```

```python
import functools

import numpy as np
import jax
import jax.numpy as jnp
from jax import lax
from jax.experimental import pallas as pl
from jax.experimental.pallas import tpu as pltpu

f32 = jnp.float32
bf16 = jnp.bfloat16

SEQ = 2048
D_MODEL = 1024
DEPTH = 2
HEAD_DIM = 64
N_Q_HEADS = 8
N_KV_HEADS = 2
Q_PER_KV = N_Q_HEADS // N_KV_HEADS
BLOCK = 128
N_BLOCKS = SEQ // BLOCK
N_BUCKETS = 32
MAX_DISTANCE = 128
SSM_HEADS = 8
SSM_HEAD_DIM = 64
SSM_GROUPS = 2
HEADS_PER_GROUP = SSM_HEADS // SSM_GROUPS
SSM_STATE = 128
CONV_WIDTH = 4
CHUNK = 128
N_CHUNKS = SEQ // CHUNK
D_FF = 4 * D_MODEL
D_ATTN = N_Q_HEADS * HEAD_DIM
D_KV = N_KV_HEADS * HEAD_DIM
D_SSM = SSM_HEADS * SSM_HEAD_DIM
D_BC = SSM_GROUPS * SSM_STATE
D_CONV = D_SSM + 2 * D_BC
D_IN = D_ATTN + 2 * D_KV + D_SSM + D_CONV + SSM_HEADS
EPS = 1e-6
NEG = -1e30
N_CHIPS = 4
FF_TILE = D_FF // N_CHIPS

LANE = 128
PW = D_ATTN + D_SSM + D_CONV + 2 * D_KV + LANE
OFF_Q, OFF_Z, OFF_X, OFF_K, OFF_V, OFF_DT = 0, 512, 1024, 2048, 2176, 2304

ADAM_LR = 0.001
ADAM_B1 = 0.9
ADAM_B2 = 0.999
ADAM_EPS = 1e-08
ADAM_WD = 0.01
ADAM_STEP = 10

VMEM_LIMIT = 56 * 1024 * 1024


def _params(*sem):
    return pltpu.CompilerParams(dimension_semantics=tuple(sem), vmem_limit_bytes=VMEM_LIMIT)


def _bdot(a, b):
    return jnp.dot(a.astype(bf16), b.astype(bf16), preferred_element_type=f32)


def _bdot_nt(a, b):
    return lax.dot_general(a.astype(bf16), b.astype(bf16), (((1,), (1,)), ((), ())), preferred_element_type=f32)


def _bdot_tn(a, b):
    return lax.dot_general(a.astype(bf16), b.astype(bf16), (((0,), (0,)), ((), ())), preferred_element_type=f32)


def _hdot(a, b):
    return jnp.dot(a, b, precision=lax.Precision.HIGHEST, preferred_element_type=f32)


def _sigmoid(x):
    return 1.0 / (1.0 + jnp.exp(-x))


def _softplus(x):
    return jnp.maximum(x, 0.0) + jnp.log1p(jnp.exp(-jnp.abs(x)))


def _rms(x):
    return lax.rsqrt(jnp.mean(x * x, axis=-1, keepdims=True) + EPS)


def _rms_bwd(dy, xhat, r, g):
    t = dy * g
    return r * (t - xhat * jnp.mean(t * xhat, axis=-1, keepdims=True))


def _full(shape):
    return pl.BlockSpec(shape, lambda *_: (0,) * len(shape))


def _bucket_table():
    qi = np.arange(BLOCK)[:, None]
    kj = np.arange(2 * BLOCK)[None, :]
    dist = qi + BLOCK - kj
    ok = (dist >= 0) & (dist < 128)
    d = np.clip(dist, 0, None)
    max_exact = N_BUCKETS // 2
    d_f = np.maximum(d, 1).astype(np.float32)
    large = max_exact + (np.log(d_f / np.float32(max_exact)) / np.float32(np.log(MAX_DISTANCE / max_exact))
                         * np.float32(N_BUCKETS - max_exact)).astype(np.int32)
    large = np.minimum(large, N_BUCKETS - 1)
    bucket = np.where(d < max_exact, d, large)
    return np.where(ok, bucket, -1).astype(np.int32)


def bias_build(rel_bias, bucket):
    def body(rel_ref, bkt_ref, o_ref):
        bkt = bkt_ref[...]
        for h in range(N_Q_HEADS):
            acc = jnp.where(bkt < 0, NEG, 0.0).astype(f32)
            for b in range(N_BUCKETS):
                acc = acc + jnp.where(bkt == b, rel_ref[b, h], 0.0)
            o_ref[h] = acc

    return pl.pallas_call(
        body, name="bias_build", out_shape=jax.ShapeDtypeStruct((N_Q_HEADS,) + bucket.shape, f32),
        in_specs=[pl.BlockSpec(memory_space=pltpu.SMEM), pl.BlockSpec(memory_space=pltpu.VMEM)],
        out_specs=pl.BlockSpec(memory_space=pltpu.VMEM),
    )(rel_bias, bucket)


def bias_bwd(dband0, dband1, bucket):
    def body(d0_ref, d1_ref, bkt_ref, o_ref):
        bkt = bkt_ref[...]
        o_ref[...] = jnp.zeros_like(o_ref)
        for h in range(N_Q_HEADS):
            d = d0_ref[h] + d1_ref[h]
            for b in range(N_BUCKETS):
                part = jnp.sum(jnp.where(bkt == b, d, 0.0), axis=1, keepdims=True)
                o_ref[b:b + 1, h:h + 1] = jnp.sum(part, axis=0, keepdims=True)

    return pl.pallas_call(
        body, name="bias_bwd", out_shape=jax.ShapeDtypeStruct((N_BUCKETS, LANE), f32),
    )(dband0, dband1, bucket)


W_IN_SHARD = D_IN // N_CHIPS
_ALIGNED_PIECES = ((0, 0, 512), (1, 190, 578), (2, 0, 124), (2, 124, 578), (3, 0, 570), (0, 512, 578), (1, 0, 62),
                   (1, 62, 190), (3, 570, 578))
_SHARD_PIECES = (((0, 512), (2048, 2114)), ((2114, 2176), (2176, 2304), (512, 900)), ((900, 1024), (1024, 1478)),
                 ((1478, 2048), (2304, 2312)))


def align_w_in(shards, tr=256):
    def body(s_ref, o_ref):
        parts = [s_ref[k, :, a:b] for k, a, b in _ALIGNED_PIECES]
        parts.append(jnp.zeros((tr, LANE - SSM_HEADS), s_ref.dtype))
        o_ref[...] = jnp.concatenate(parts, axis=-1)

    return pl.pallas_call(
        body, name="align_w_in", grid=(D_MODEL // tr,),
        in_specs=[pl.BlockSpec((N_CHIPS, tr, W_IN_SHARD), lambda i: (0, i, 0))],
        out_specs=pl.BlockSpec((tr, PW), lambda i: (i, 0)),
        out_shape=jax.ShapeDtypeStruct((D_MODEL, PW), shards.dtype),
        compiler_params=_params("arbitrary"),
    )(shards)


def split_w_in_grad(dw, tr=256):
    def body(d_ref, o_ref, o16_ref):
        for k, pieces in enumerate(_SHARD_PIECES):
            part = jnp.concatenate([d_ref[:, a:b] for a, b in pieces], axis=-1)
            o_ref[k] = part
            o16_ref[k] = part.astype(bf16)

    spec = pl.BlockSpec((N_CHIPS, tr, W_IN_SHARD), lambda i: (0, i, 0))
    return pl.pallas_call(
        body, name="split_w_in_grad", grid=(D_MODEL // tr,),
        in_specs=[pl.BlockSpec((tr, PW), lambda i: (i, 0))], out_specs=[spec, spec],
        out_shape=[jax.ShapeDtypeStruct((N_CHIPS, D_MODEL, W_IN_SHARD), f32),
                   jax.ShapeDtypeStruct((N_CHIPS, D_MODEL, W_IN_SHARD), bf16)],
        compiler_params=_params("arbitrary"),
    )(dw)

def in_fwd(x, g, w, tm=512):
    def body(x_ref, g_ref, w_ref, o_ref):
        xv = x_ref[...]
        h = xv * _rms(xv) * g_ref[...]
        o_ref[...] = _bdot(h, w_ref[...])

    return pl.pallas_call(
        body, name="in_fwd", grid=(SEQ // tm,),
        in_specs=[pl.BlockSpec((tm, D_MODEL), lambda i: (i, 0)), _full((1, D_MODEL)), _resident((D_MODEL, PW))],
        out_specs=pl.BlockSpec((tm, PW), lambda i: (i, 0)),
        out_shape=jax.ShapeDtypeStruct((SEQ, PW), f32),
        compiler_params=_params("arbitrary"),
    )(x, g, w)


def _resident(shape):
    return pl.BlockSpec(shape, lambda *_: (0,) * len(shape), pipeline_mode=pl.Buffered(1))


def in_bwd(dq, dz, dxbc, dk, dv, ddt, x, g, w, dres, tm=512):
    def body(dq_ref, dz_ref, dx_ref, dk_ref, dv_ref, ddt_ref, x_ref, g_ref, w_ref, dres_ref, o_ref, dw_ref, dg_ref):
        i = pl.program_id(0)

        @pl.when(i == 0)
        def _():
            dw_ref[...] = jnp.zeros_like(dw_ref)
            dg_ref[...] = jnp.zeros_like(dg_ref)

        dproj = jnp.concatenate([dq_ref[...], dz_ref[...], dx_ref[...], dk_ref[...], dv_ref[...], ddt_ref[...]],
                                axis=-1).astype(bf16)
        xv = x_ref[...]
        r = _rms(xv)
        xhat = xv * r
        gv = g_ref[...]
        h = xhat * gv
        dw_ref[...] += _bdot_tn(h, dproj)
        dh = _bdot_nt(dproj, w_ref[...])
        dg_ref[...] += jnp.sum(dh * xhat, axis=0, keepdims=True)
        o_ref[...] = dres_ref[...] + _rms_bwd(dh, xhat, r, gv)

    tok = lambda w_: pl.BlockSpec((tm, w_), lambda i: (i, 0))
    return pl.pallas_call(
        body, name="in_bwd", grid=(SEQ // tm,),
        in_specs=[tok(D_ATTN), tok(D_SSM), tok(D_CONV), tok(D_KV), tok(D_KV), tok(LANE), tok(D_MODEL),
                  _full((1, D_MODEL)), _resident((D_MODEL, PW)), tok(D_MODEL)],
        out_specs=[tok(D_MODEL), _resident((D_MODEL, PW)), _full((1, D_MODEL))],
        out_shape=[jax.ShapeDtypeStruct((SEQ, D_MODEL), f32), jax.ShapeDtypeStruct((D_MODEL, PW), f32),
                   jax.ShapeDtypeStruct((1, D_MODEL), f32)],
        compiler_params=_params("arbitrary"),
    )(dq, dz, dxbc, dk, dv, ddt, x, g, w, dres)


def _attn_softmax_t(qk, bias_t, sink, first, key_row):
    s = qk * (HEAD_DIM ** -0.5) + bias_t
    s = jnp.where(jnp.logical_and(first, key_row < BLOCK), NEG, s)
    m = jnp.maximum(jnp.max(s, axis=0, keepdims=True), sink)
    p = jnp.exp(s - m)
    psink = jnp.exp(sink - m)
    inv = 1.0 / (jnp.sum(p, axis=0, keepdims=True) + psink)
    return p * inv, psink * inv


def _rms_t(x_t):
    return lax.rsqrt(jnp.mean(x_t * x_t, axis=0, keepdims=True) + EPS)


def attn_fwd_t(proj, q_gain_col, k_gain, sinks, bias_t):
    kcol, vcol = OFF_K // D_KV, OFF_V // D_KV

    def body(q_ref, kc_ref, kp_ref, vc_ref, vp_ref, qg_ref, kg_ref, sink_ref, bias_ref, o_ref, ot_scr):
        n = pl.program_id(0)
        first = n == 0
        key_row = lax.broadcasted_iota(jnp.int32, (2 * BLOCK, BLOCK), 0)
        k2 = jnp.concatenate([kp_ref[...], kc_ref[...]], axis=0)
        v_t = jnp.concatenate([vp_ref[...], vc_ref[...]], axis=0).T
        q_t = q_ref[...].T
        qg = jnp.broadcast_to(qg_ref[...], (HEAD_DIM, BLOCK))
        kg = kg_ref[...]
        for hk in range(N_KV_HEADS):
            sl = slice(hk * HEAD_DIM, (hk + 1) * HEAD_DIM)
            kk = k2[:, sl]
            kn = (kk * _rms(kk) * kg).astype(bf16)
            vt = v_t[sl, :].astype(bf16)
            heads = range(hk * Q_PER_KV, (hk + 1) * Q_PER_KV)
            qns = []
            for h in heads:
                qh = q_t[h * HEAD_DIM:(h + 1) * HEAD_DIM, :]
                qns.append(qh * _rms_t(qh) * qg)
            scores = [_bdot(kn, qn) for qn in qns]
            for h, s in zip(heads, scores):
                p, _ = _attn_softmax_t(s, bias_ref[h], sink_ref[h], first, key_row)
                ot_scr[h * HEAD_DIM:(h + 1) * HEAD_DIM, :] = _bdot(vt, p)
        o_ref[...] = ot_scr[...].T

    prev = lambda n: jnp.maximum(n - 1, 0)
    return pl.pallas_call(
        body, name="attn_fwd", grid=(N_BLOCKS,),
        in_specs=[pl.BlockSpec((BLOCK, D_ATTN), lambda n: (n, 0)),
                  pl.BlockSpec((BLOCK, D_KV), lambda n: (n, kcol)), pl.BlockSpec((BLOCK, D_KV), lambda n: (prev(n), kcol)),
                  pl.BlockSpec((BLOCK, D_KV), lambda n: (n, vcol)), pl.BlockSpec((BLOCK, D_KV), lambda n: (prev(n), vcol)),
                  _full((HEAD_DIM, 1)), _full((1, HEAD_DIM)), pl.BlockSpec(memory_space=pltpu.SMEM),
                  _full((N_Q_HEADS, 2 * BLOCK, BLOCK))],
        out_specs=pl.BlockSpec((BLOCK, D_ATTN), lambda n: (n, 0)),
        out_shape=jax.ShapeDtypeStruct((SEQ, D_ATTN), f32),
        scratch_shapes=[pltpu.VMEM((D_ATTN, BLOCK), f32)],
        compiler_params=_params("arbitrary"),
    )(proj, proj, proj, proj, proj, q_gain_col, k_gain, sinks, bias_t)


def attn_bwd_t(proj, d_out, q_gain_col, k_gain, sinks, bias_t):
    kcol, vcol = OFF_K // D_KV, OFF_V // D_KV

    def body(q_ref, kc_ref, kp_ref, vc_ref, vp_ref, do_ref, qg_ref, kg_ref, sink_ref, bias_ref,
             dq_ref, dk_ref, dv_ref, dband_ref, dsink_ref, dqg_ref, dkg_ref, dkn_scr, dv_scr, dqt_scr, dsink_acc, dqg_acc):
        i = pl.program_id(0)
        first = i == N_BLOCKS - 1

        @pl.when(i == 0)
        def _():
            for ref in (dband_ref, dkg_ref, dkn_scr, dv_scr, dsink_acc, dqg_acc):
                ref[...] = jnp.zeros_like(ref)

        key_row = lax.broadcasted_iota(jnp.int32, (2 * BLOCK, BLOCK), 0)
        k2 = jnp.concatenate([kp_ref[...], kc_ref[...]], axis=0)
        v2 = jnp.concatenate([vp_ref[...], vc_ref[...]], axis=0)
        q_t = q_ref[...].T
        do_t = do_ref[...].T
        qg = jnp.broadcast_to(qg_ref[...], (HEAD_DIM, BLOCK))
        kg = kg_ref[...]
        scale = HEAD_DIM ** -0.5
        for hk in range(N_KV_HEADS):
            sl = slice(hk * HEAD_DIM, (hk + 1) * HEAD_DIM)
            kk = k2[:, sl]
            rk = _rms(kk)
            khat = kk * rk
            kn = (khat * kg).astype(bf16)
            vb = v2[:, sl].astype(bf16)
            dkn = jnp.zeros((2 * BLOCK, HEAD_DIM), f32)
            dvv = jnp.zeros((2 * BLOCK, HEAD_DIM), f32)
            heads = range(hk * Q_PER_KV, (hk + 1) * Q_PER_KV)
            rqs, qhats, qns, d_os = [], [], [], []
            for h in heads:
                hs = slice(h * HEAD_DIM, (h + 1) * HEAD_DIM)
                qh = q_t[hs, :]
                rqs.append(_rms_t(qh))
                qhats.append(qh * rqs[-1])
                qns.append((qhats[-1] * qg).astype(bf16))
                d_os.append(do_t[hs, :].astype(bf16))
            scores = [_bdot(kn, qn) for qn in qns]
            dps = [_bdot(vb, d_o) for d_o in d_os]
            ps, dss = [], []
            for h, s, dp in zip(heads, scores, dps):
                p, psink = _attn_softmax_t(s, bias_ref[h], sink_ref[h], first, key_row)
                delta = jnp.sum(p * dp, axis=0, keepdims=True)
                ds = p * (dp - delta)
                dband_ref[h] += ds
                dsink_acc[h:h + 1, :] += -(psink * delta)
                ps.append(p.astype(bf16))
                dss.append(ds.astype(bf16))
            dqns = [_bdot_tn(kn, ds) * scale for ds in dss]
            for ds, qn, p, d_o in zip(dss, qns, ps, d_os):
                dkn = dkn + _bdot_nt(ds, qn) * scale
                dvv = dvv + _bdot_nt(p, d_o)
            for h, dqn, rq, qhat in zip(heads, dqns, rqs, qhats):
                dqg_acc[...] += dqn * qhat
                t = dqn * qg
                dqt_scr[h * HEAD_DIM:(h + 1) * HEAD_DIM, :] = rq * (t - qhat * jnp.mean(t * qhat, axis=0, keepdims=True))
            dkn_cur = dkn[BLOCK:] + dkn_scr[:, sl]
            dkn_scr[:, sl] = dkn[:BLOCK]
            khat_c, rk_c = khat[BLOCK:], rk[BLOCK:]
            dkg_ref[...] += jnp.sum(dkn_cur * khat_c, axis=0, keepdims=True)
            dk_ref[:, sl] = _rms_bwd(dkn_cur, khat_c, rk_c, kg)
            dv_ref[:, sl] = dvv[BLOCK:] + dv_scr[:, sl]
            dv_scr[:, sl] = dvv[:BLOCK]
        dq_ref[...] = dqt_scr[...].T

        @pl.when(i == N_BLOCKS - 1)
        def _():
            dsink_ref[...] = jnp.sum(dsink_acc[...], axis=1, keepdims=True)
            dqg_ref[...] = jnp.sum(dqg_acc[...], axis=1, keepdims=True)

    blk = lambda i: N_BLOCKS - 1 - i
    prev = lambda i: jnp.maximum(N_BLOCKS - 2 - i, 0)
    return pl.pallas_call(
        body, name="attn_bwd", grid=(N_BLOCKS,),
        in_specs=[pl.BlockSpec((BLOCK, D_ATTN), lambda i: (blk(i), 0)),
                  pl.BlockSpec((BLOCK, D_KV), lambda i: (blk(i), kcol)), pl.BlockSpec((BLOCK, D_KV), lambda i: (prev(i), kcol)),
                  pl.BlockSpec((BLOCK, D_KV), lambda i: (blk(i), vcol)), pl.BlockSpec((BLOCK, D_KV), lambda i: (prev(i), vcol)),
                  pl.BlockSpec((BLOCK, D_ATTN), lambda i: (blk(i), 0)),
                  _full((HEAD_DIM, 1)), _full((1, HEAD_DIM)), pl.BlockSpec(memory_space=pltpu.SMEM),
                  _full((N_Q_HEADS, 2 * BLOCK, BLOCK))],
        out_specs=[pl.BlockSpec((BLOCK, D_ATTN), lambda i: (blk(i), 0)), pl.BlockSpec((BLOCK, D_KV), lambda i: (blk(i), 0)),
                   pl.BlockSpec((BLOCK, D_KV), lambda i: (blk(i), 0)), _full((N_Q_HEADS, 2 * BLOCK, BLOCK)),
                   _full((N_Q_HEADS, 1)), _full((HEAD_DIM, 1)), _full((1, HEAD_DIM))],
        out_shape=[jax.ShapeDtypeStruct((SEQ, D_ATTN), f32), jax.ShapeDtypeStruct((SEQ, D_KV), f32),
                   jax.ShapeDtypeStruct((SEQ, D_KV), f32), jax.ShapeDtypeStruct((N_Q_HEADS, 2 * BLOCK, BLOCK), f32),
                   jax.ShapeDtypeStruct((N_Q_HEADS, 1), f32), jax.ShapeDtypeStruct((HEAD_DIM, 1), f32),
                   jax.ShapeDtypeStruct((1, HEAD_DIM), f32)],
        scratch_shapes=[pltpu.VMEM((BLOCK, D_KV), f32), pltpu.VMEM((BLOCK, D_KV), f32), pltpu.VMEM((D_ATTN, BLOCK), f32),
                        pltpu.VMEM((N_Q_HEADS, BLOCK), f32), pltpu.VMEM((HEAD_DIM, BLOCK), f32)],
        compiler_params=_params("arbitrary"),
    )(proj, proj, proj, proj, proj, d_out, q_gain_col, k_gain, sinks, bias_t)


def _shift_down(u, s, row):
    if s == 0:
        return u
    return jnp.where(row >= s, pltpu.roll(u, s, 0), 0.0)


def _shift_up(u, s, row):
    if s == 0:
        return u
    return jnp.where(row < SEQ - s, pltpu.roll(u, SEQ - s, 0), 0.0)


def conv_fwd(proj, conv_w, conv_b):
    xcol = OFF_X // LANE

    def body(u_ref, w_ref, b_ref, o_ref):
        u = u_ref[...]
        row = lax.broadcasted_iota(jnp.int32, u.shape, 0)
        pre = b_ref[...] + jnp.zeros_like(u)
        for k in range(CONV_WIDTH):
            pre = pre + w_ref[k:k + 1, :] * _shift_down(u, CONV_WIDTH - 1 - k, row)
        o_ref[...] = pre * _sigmoid(pre)

    return pl.pallas_call(
        body, name="conv_fwd", grid=(D_CONV // LANE,),
        in_specs=[pl.BlockSpec((SEQ, LANE), lambda j: (0, xcol + j)), pl.BlockSpec((CONV_WIDTH, LANE), lambda j: (0, j)),
                  pl.BlockSpec((1, LANE), lambda j: (0, j))],
        out_specs=pl.BlockSpec((SEQ, LANE), lambda j: (0, j)),
        out_shape=jax.ShapeDtypeStruct((SEQ, D_CONV), f32),
        compiler_params=_params("arbitrary"),
    )(proj, conv_w, conv_b)


def conv_bwd(proj, d_act, conv_w, conv_b):
    xcol = OFF_X // LANE

    def body(u_ref, da_ref, w_ref, b_ref, du_ref, dw_ref, db_ref):
        u = u_ref[...]
        row = lax.broadcasted_iota(jnp.int32, u.shape, 0)
        shifted = [_shift_down(u, CONV_WIDTH - 1 - k, row) for k in range(CONV_WIDTH)]
        pre = b_ref[...] + jnp.zeros_like(u)
        for k in range(CONV_WIDTH):
            pre = pre + w_ref[k:k + 1, :] * shifted[k]
        sg = _sigmoid(pre)
        dpre = da_ref[...] * (sg * (1.0 + pre * (1.0 - sg)))
        db_ref[...] = jnp.sum(dpre, axis=0, keepdims=True)
        du = jnp.zeros_like(u)
        for k in range(CONV_WIDTH):
            dw_ref[k:k + 1, :] = jnp.sum(dpre * shifted[k], axis=0, keepdims=True)
            du = du + w_ref[k:k + 1, :] * _shift_up(dpre, CONV_WIDTH - 1 - k, row)
        du_ref[...] = du

    return pl.pallas_call(
        body, name="conv_bwd", grid=(D_CONV // LANE,),
        in_specs=[pl.BlockSpec((SEQ, LANE), lambda j: (0, xcol + j)), pl.BlockSpec((SEQ, LANE), lambda j: (0, j)),
                  pl.BlockSpec((CONV_WIDTH, LANE), lambda j: (0, j)), pl.BlockSpec((1, LANE), lambda j: (0, j))],
        out_specs=[pl.BlockSpec((SEQ, LANE), lambda j: (0, j)), pl.BlockSpec((CONV_WIDTH, LANE), lambda j: (0, j)),
                   pl.BlockSpec((1, LANE), lambda j: (0, j))],
        out_shape=[jax.ShapeDtypeStruct((SEQ, D_CONV), f32), jax.ShapeDtypeStruct((CONV_WIDTH, D_CONV), f32),
                   jax.ShapeDtypeStruct((1, D_CONV), f32)],
        compiler_params=_params("arbitrary"),
    )(proj, d_act, conv_w, conv_b)


def _ssd_chunk_common(dt_raw, dtb, alog):
    row = lax.broadcasted_iota(jnp.int32, (CHUNK, CHUNK), 0)
    col = lax.broadcasted_iota(jnp.int32, (CHUNK, CHUNK), 1)
    tri = (row >= col).astype(f32)
    strict = (row > col).astype(f32)
    dtp = _softplus(dt_raw + dtb)
    a_row = -jnp.exp(alog)
    d_a = dtp * a_row
    cs = _hdot(tri, d_a)
    cs_last = cs[CHUNK - 1:CHUNK, :]
    return row, col, dtp, a_row, cs, cs.T, cs_last


def _seg_decay(cs, cs_t, hd, row, col):
    seg = cs[:, hd:hd + 1] - cs_t[hd:hd + 1, :]
    return jnp.where(row >= col, jnp.exp(seg), 0.0)


GROUP_W = HEADS_PER_GROUP * SSM_HEAD_DIM


def _group_indicator(g):
    j = lax.broadcasted_iota(jnp.int32, (GROUP_W, LANE), 0)
    lane = lax.broadcasted_iota(jnp.int32, (GROUP_W, LANE), 1)
    return (lane == g * HEADS_PER_GROUP + j // SSM_HEAD_DIM).astype(bf16)


def _bf16_pieces(a, n):
    pieces = []
    for _ in range(n):
        p = a.astype(bf16)
        pieces.append(p)
        a = a - p.astype(f32)
    return pieces


def _head_spread(a, ind):
    return sum(lax.dot_general(p, ind, (((1,), (1,)), ((), ())), preferred_element_type=f32) for p in _bf16_pieces(a, 3))


def _head_sums(a, ind):
    return sum(jnp.dot(p, ind, preferred_element_type=f32) for p in _bf16_pieces(a, 2))


def ssd_fwd_g(act, proj, dt_bias, a_log, d_skip, norm_g):
    zcol, dtcol = OFF_Z // D_SSM, OFF_DT // LANE

    def body(act_ref, z_ref, dt_ref, dtb_ref, alog_ref, dsk_ref, ng_ref, out_ref, ypre_ref, st_ref, state):
        c = pl.program_id(0)

        @pl.when(c == 0)
        def _():
            state[...] = jnp.zeros_like(state)

        row, col, dtp, a_row, cs, cs_t, cs_last = _ssd_chunk_common(dt_ref[...], dtb_ref[...], alog_ref[...])
        e_cs = jnp.exp(cs)
        dte = jnp.exp(cs_last - cs)
        rows8 = jnp.concatenate([jnp.exp(cs_last), dsk_ref[...], jnp.zeros((6, LANE), f32)], axis=0)
        z = z_ref[...]
        sz = z * _sigmoid(z)
        ng = ng_ref[...]
        for g in range(SSM_GROUPS):
            gs = slice(g * GROUP_W, (g + 1) * GROUP_W)
            ind = _group_indicator(g)
            xg = act_ref[:, gs]
            bg = act_ref[:, D_SSM + g * SSM_STATE:D_SSM + (g + 1) * SSM_STATE]
            cg = act_ref[:, D_SSM + D_BC + g * SSM_STATE:D_SSM + D_BC + (g + 1) * SSM_STATE]
            dt_e, e_e, dte_e = _head_spread(dtp, ind), _head_spread(e_cs, ind), _head_spread(dte, ind)
            rows_e = _head_spread(rows8, ind)
            ecl_e, dsk_e = rows_e[0:1], rows_e[1:2]
            xdt = xg * dt_e
            prev = state[g]
            st_ref[0, g] = prev
            cb = _bdot_nt(cg, bg)
            goff = _bdot(cg, prev)
            snew = _bdot_tn(bg, xdt * dte_e)
            heads = range(g * HEADS_PER_GROUP, (g + 1) * HEADS_PER_GROUP)
            ms = [cb * _seg_decay(cs, cs_t, hd, row, col) for hd in heads]
            yd = [_bdot(m, xdt[:, r * SSM_HEAD_DIM:(r + 1) * SSM_HEAD_DIM]) for r, m in enumerate(ms)]
            y = jnp.concatenate(yd, axis=1) + e_e * goff + xg * dsk_e
            state[g] = prev * ecl_e + snew
            ypre_ref[:, gs] = y
            part = y * sz[:, gs]
            out_ref[:, gs] = part * _rms(part) * ng[:, gs]

    return pl.pallas_call(
        body, name="ssd_fwd", grid=(N_CHUNKS,),
        in_specs=[pl.BlockSpec((CHUNK, D_CONV), lambda c: (c, 0)), pl.BlockSpec((CHUNK, D_SSM), lambda c: (c, zcol)),
                  pl.BlockSpec((CHUNK, LANE), lambda c: (c, dtcol)), _full((1, LANE)), _full((1, LANE)), _full((1, LANE)),
                  _full((1, D_SSM))],
        out_specs=[pl.BlockSpec((CHUNK, D_SSM), lambda c: (c, 0)), pl.BlockSpec((CHUNK, D_SSM), lambda c: (c, 0)),
                   pl.BlockSpec((1, SSM_GROUPS, SSM_STATE, GROUP_W), lambda c: (c, 0, 0, 0))],
        out_shape=[jax.ShapeDtypeStruct((SEQ, D_SSM), f32), jax.ShapeDtypeStruct((SEQ, D_SSM), f32),
                   jax.ShapeDtypeStruct((N_CHUNKS, SSM_GROUPS, SSM_STATE, GROUP_W), f32)],
        scratch_shapes=[pltpu.VMEM((SSM_GROUPS, SSM_STATE, GROUP_W), f32)],
        compiler_params=_params("arbitrary"),
    )(act, proj, proj, dt_bias, a_log, d_skip, norm_g)


def ssd_bwd_g(act, proj, ypre, states, d_out, dt_bias, a_log, d_skip, norm_g):
    zcol, dtcol = OFF_Z // D_SSM, OFF_DT // LANE

    def body(act_ref, z_ref, dt_ref, ypre_ref, st_ref, do_ref, dtb_ref, alog_ref, dsk_ref, ng_ref,
             dact_ref, ddt_ref, dz_ref, dng_ref, dpar_ref, dstate):
        i = pl.program_id(0)

        @pl.when(i == 0)
        def _():
            for ref in (dng_ref, dpar_ref, dstate):
                ref[...] = jnp.zeros_like(ref)

        row, col, dtp, a_row, cs, cs_t, cs_last = _ssd_chunk_common(dt_ref[...], dtb_ref[...], alog_ref[...])
        upper = (row <= col).astype(f32)
        lane = lax.broadcasted_iota(jnp.int32, (CHUNK, LANE), 1)
        rowl = lax.broadcasted_iota(jnp.int32, (CHUNK, LANE), 0)
        e_cs = jnp.exp(cs)
        dte = jnp.exp(cs_last - cs)
        ecl = jnp.exp(cs_last)
        rows8 = jnp.concatenate([ecl, dsk_ref[...], jnp.zeros((6, LANE), f32)], axis=0)
        z = z_ref[...]
        sgz = _sigmoid(z)
        sz = z * sgz
        ng = ng_ref[...]
        ddt_mat = jnp.zeros((CHUNK, LANE), f32)
        dcs_mat = jnp.zeros((CHUNK, LANE), f32)
        dcs_t = jnp.zeros((LANE, CHUNK), f32)
        dcsl_row = jnp.zeros((1, LANE), f32)
        dd_row = jnp.zeros((1, LANE), f32)
        for g in range(SSM_GROUPS):
            gs = slice(g * GROUP_W, (g + 1) * GROUP_W)
            bsl = slice(D_SSM + g * SSM_STATE, D_SSM + (g + 1) * SSM_STATE)
            csl = slice(D_SSM + D_BC + g * SSM_STATE, D_SSM + D_BC + (g + 1) * SSM_STATE)
            ind = _group_indicator(g)
            y = ypre_ref[:, gs]
            part = y * sz[:, gs]
            r = _rms(part)
            yhat = part * r
            d_o = do_ref[:, gs]
            dng_ref[:, gs] += jnp.sum(d_o * yhat, axis=0, keepdims=True)
            dyz = _rms_bwd(d_o, yhat, r, ng[:, gs])
            dy = dyz * sz[:, gs]
            dz_ref[:, gs] = dyz * y * (sgz[:, gs] * (1.0 + z[:, gs] * (1.0 - sgz[:, gs])))

            xg = act_ref[:, gs]
            bg = act_ref[:, bsl]
            cg = act_ref[:, csl]
            dt_e, e_e, dte_e = _head_spread(dtp, ind), _head_spread(e_cs, ind), _head_spread(dte, ind)
            rows_e = _head_spread(rows8, ind)
            ecl_e, dsk_e = rows_e[0:1], rows_e[1:2]
            xdt = xg * dt_e
            prev = st_ref[0, g]
            dh = dstate[g]
            heads = range(g * HEADS_PER_GROUP, (g + 1) * HEADS_PER_GROUP)
            hsl = [slice(r_ * SSM_HEAD_DIM, (r_ + 1) * SSM_HEAD_DIM) for r_ in range(HEADS_PER_GROUP)]
            cb = _bdot_nt(cg, bg)
            lms = [_seg_decay(cs, cs_t, hd, row, col) for hd in heads]
            ms = [cb * lm for lm in lms]
            gmat = _bdot(cg, prev)
            dgm = dy * e_e
            dcg = _bdot_nt(dgm, prev)
            dprev = _bdot_tn(cg, dgm)
            dbg = _bdot_nt(xdt * dte_e, dh)
            dw = _bdot(bg, dh)
            dms = [_bdot_nt(dy[:, s_], xdt[:, s_]) for s_ in hsl]
            dxdts = [_bdot_tn(m, dy[:, s_]) for m, s_ in zip(ms, hsl)]
            dxdt = jnp.concatenate(dxdts, axis=1) + dw * dte_e
            dact_ref[:, gs] = dy * dsk_e + dxdt * dt_e
            dstate[g] = dprev + dh * ecl_e
            dcb = jnp.zeros((CHUNK, CHUNK), f32)
            for hd, dm, lm, m in zip(heads, dms, lms, ms):
                dcb = dcb + dm * lm
                dseg = dm * m
                dcs_mat = dcs_mat + jnp.where(lane == hd, jnp.sum(dseg, axis=1, keepdims=True), 0.0)
                dcs_t = jnp.where(row == hd, jnp.sum(dseg, axis=0, keepdims=True), dcs_t)
            dact_ref[:, bsl] = dbg + _bdot_tn(dcb, cg)
            dact_ref[:, csl] = dcg + _bdot(dcb, bg)
            ddte = _head_sums(dw * xdt, ind) * dte
            dcs_mat = dcs_mat + _head_sums(dy * gmat, ind) * e_cs - ddte
            ddt_mat = ddt_mat + _head_sums(dxdt * xg, ind)
            dcsl_row = (dcsl_row + jnp.sum(ddte, axis=0, keepdims=True)
                        + jnp.sum(_head_sums(dh * prev, ind), axis=0, keepdims=True) * ecl)
            dd_row = dd_row + jnp.sum(_head_sums(dy * xg, ind), axis=0, keepdims=True)
        dcs_mat = dcs_mat - dcs_t.T + jnp.where(rowl == CHUNK - 1, dcsl_row, 0.0)
        dda = _hdot(upper, dcs_mat)
        ddt_mat = ddt_mat + dda * a_row
        da_row = jnp.sum(dda * dtp, axis=0, keepdims=True)
        ddt_raw = ddt_mat * _sigmoid(dt_ref[...] + dtb_ref[...])
        ddt_ref[...] = ddt_raw
        dpar_ref[0:1, :] += jnp.sum(ddt_raw, axis=0, keepdims=True)
        dpar_ref[1:2, :] += da_row * a_row
        dpar_ref[2:3, :] += dd_row

    blk = lambda i: N_CHUNKS - 1 - i
    return pl.pallas_call(
        body, name="ssd_bwd", grid=(N_CHUNKS,),
        in_specs=[pl.BlockSpec((CHUNK, D_CONV), lambda i: (blk(i), 0)), pl.BlockSpec((CHUNK, D_SSM), lambda i: (blk(i), zcol)),
                  pl.BlockSpec((CHUNK, LANE), lambda i: (blk(i), dtcol)), pl.BlockSpec((CHUNK, D_SSM), lambda i: (blk(i), 0)),
                  pl.BlockSpec((1, SSM_GROUPS, SSM_STATE, GROUP_W), lambda i: (blk(i), 0, 0, 0)),
                  pl.BlockSpec((CHUNK, D_SSM), lambda i: (blk(i), 0)),
                  _full((1, LANE)), _full((1, LANE)), _full((1, LANE)), _full((1, D_SSM))],
        out_specs=[pl.BlockSpec((CHUNK, D_CONV), lambda i: (blk(i), 0)), pl.BlockSpec((CHUNK, LANE), lambda i: (blk(i), 0)),
                   pl.BlockSpec((CHUNK, D_SSM), lambda i: (blk(i), 0)), _full((1, D_SSM)), _full((8, LANE))],
        out_shape=[jax.ShapeDtypeStruct((SEQ, D_CONV), f32), jax.ShapeDtypeStruct((SEQ, LANE), f32),
                   jax.ShapeDtypeStruct((SEQ, D_SSM), f32), jax.ShapeDtypeStruct((1, D_SSM), f32),
                   jax.ShapeDtypeStruct((8, LANE), f32)],
        scratch_shapes=[pltpu.VMEM((SSM_GROUPS, SSM_STATE, GROUP_W), f32)],
        compiler_params=_params("arbitrary"),
    )(act, proj, proj, ypre, states, d_out, dt_bias, a_log, d_skip, norm_g)


def out_fwd(x, attn, ssm, w_out, tm=512):
    def body(x_ref, a_ref, s_ref, w_ref, o_ref):
        o_ref[...] = x_ref[...] + _bdot(a_ref[...], w_ref[:D_ATTN, :]) + _bdot(s_ref[...], w_ref[D_ATTN:, :])

    tok = lambda w_: pl.BlockSpec((tm, w_), lambda i: (i, 0))
    return pl.pallas_call(
        body, name="out_fwd", grid=(SEQ // tm,),
        in_specs=[tok(D_MODEL), tok(D_ATTN), tok(D_SSM), _full((D_MODEL, D_MODEL))],
        out_specs=tok(D_MODEL), out_shape=jax.ShapeDtypeStruct((SEQ, D_MODEL), f32),
        compiler_params=_params("arbitrary"),
    )(x, attn, ssm, w_out)


def out_bwd(dx1, attn, ssm, w_out, tm=512):
    nt = SEQ // tm

    def body(d_ref, a_ref, s_ref, w_ref, da_ref, ds_ref, dw_ref, dw16_ref):
        i = pl.program_id(0)

        @pl.when(i == 0)
        def _():
            dw_ref[...] = jnp.zeros_like(dw_ref)

        d = d_ref[...].astype(bf16)
        dcat = _bdot_nt(d, w_ref[...])
        da_ref[...] = dcat[:, :D_ATTN]
        ds_ref[...] = dcat[:, D_ATTN:]
        dw_ref[:D_ATTN, :] += _bdot_tn(a_ref[...], d)
        dw_ref[D_ATTN:, :] += _bdot_tn(s_ref[...], d)

        @pl.when(i == nt - 1)
        def _():
            dw16_ref[...] = dw_ref[...].astype(bf16)

    tok = lambda w_: pl.BlockSpec((tm, w_), lambda i: (i, 0))
    return pl.pallas_call(
        body, name="out_bwd", grid=(nt,),
        in_specs=[tok(D_MODEL), tok(D_ATTN), tok(D_SSM), _full((D_MODEL, D_MODEL))],
        out_specs=[tok(D_ATTN), tok(D_SSM), _full((D_MODEL, D_MODEL)), _full((D_MODEL, D_MODEL))],
        out_shape=[jax.ShapeDtypeStruct((SEQ, D_ATTN), f32), jax.ShapeDtypeStruct((SEQ, D_SSM), f32),
                   jax.ShapeDtypeStruct((D_MODEL, D_MODEL), f32), jax.ShapeDtypeStruct((D_MODEL, D_MODEL), bf16)],
        compiler_params=_params("arbitrary"),
    )(dx1, attn, ssm, w_out)


MLP_SUB = 256


def mlp_fwd(x1, g, w_up, w_down, tm=1024):
    def body(x_ref, g_ref, wu_ref, wd_ref, o_ref, u_ref, h_scr):
        j = pl.program_id(1)

        @pl.when(j == 0)
        def _():
            xv = x_ref[...]
            h_scr[...] = (xv * _rms(xv) * g_ref[...]).astype(bf16)
            o_ref[...] = xv

        for r in range(tm // MLP_SUB):
            rows = slice(r * MLP_SUB, (r + 1) * MLP_SUB)
            u = jnp.dot(h_scr[rows, :], wu_ref[...], preferred_element_type=f32)
            u_ref[rows, :] = u
            a = jnp.square(jnp.maximum(u, 0.0))
            o_ref[rows, :] += _bdot(a, wd_ref[...])

    return pl.pallas_call(
        body, name="mlp_fwd", grid=(SEQ // tm, N_CHIPS),
        in_specs=[pl.BlockSpec((tm, D_MODEL), lambda i, j: (i, 0)), _full((1, D_MODEL)),
                  pl.BlockSpec((None, D_MODEL, FF_TILE), lambda i, j: (j, 0, 0)),
                  pl.BlockSpec((None, FF_TILE, D_MODEL), lambda i, j: (j, 0, 0))],
        out_specs=[pl.BlockSpec((tm, D_MODEL), lambda i, j: (i, 0)), pl.BlockSpec((tm, FF_TILE), lambda i, j: (i, j))],
        out_shape=[jax.ShapeDtypeStruct((SEQ, D_MODEL), f32), jax.ShapeDtypeStruct((SEQ, D_FF), f32)],
        scratch_shapes=[pltpu.VMEM((tm, D_MODEL), bf16)],
        compiler_params=_params("arbitrary", "arbitrary"),
    )(x1, g, w_up, w_down)


def mlp_bwd_data(dx2, u, x1, g, w_up, w_down, tm=1024):
    def body(d_ref, u_ref, x_ref, g_ref, wu_ref, wd_ref, dx_ref, du_ref, dg_ref, dh_scr):
        i, j = pl.program_id(0), pl.program_id(1)

        @pl.when(jnp.logical_and(i == 0, j == 0))
        def _():
            dg_ref[...] = jnp.zeros_like(dg_ref)

        @pl.when(j == 0)
        def _():
            dh_scr[...] = jnp.zeros_like(dh_scr)

        for r in range(tm // MLP_SUB):
            rows = slice(r * MLP_SUB, (r + 1) * MLP_SUB)
            da = _bdot_nt(d_ref[rows, :], wd_ref[...])
            du = (da * (2.0 * jnp.maximum(u_ref[rows, :], 0.0))).astype(bf16)
            du_ref[rows, :] = du
            dh_scr[rows, :] += _bdot_nt(du, wu_ref[...])

        @pl.when(j == N_CHIPS - 1)
        def _():
            xv = x_ref[...]
            r = _rms(xv)
            xhat = xv * r
            dh = dh_scr[...]
            dg_ref[...] += jnp.sum(dh * xhat, axis=0, keepdims=True)
            dx_ref[...] = d_ref[...] + _rms_bwd(dh, xhat, r, g_ref[...])

    return pl.pallas_call(
        body, name="mlp_bwd_data", grid=(SEQ // tm, N_CHIPS),
        in_specs=[pl.BlockSpec((tm, D_MODEL), lambda i, j: (i, 0)), pl.BlockSpec((tm, FF_TILE), lambda i, j: (i, j)),
                  pl.BlockSpec((tm, D_MODEL), lambda i, j: (i, 0)), _full((1, D_MODEL)),
                  pl.BlockSpec((None, D_MODEL, FF_TILE), lambda i, j: (j, 0, 0)),
                  pl.BlockSpec((None, FF_TILE, D_MODEL), lambda i, j: (j, 0, 0))],
        out_specs=[pl.BlockSpec((tm, D_MODEL), lambda i, j: (i, 0)), pl.BlockSpec((tm, FF_TILE), lambda i, j: (i, j)),
                   _full((1, D_MODEL))],
        out_shape=[jax.ShapeDtypeStruct((SEQ, D_MODEL), f32), jax.ShapeDtypeStruct((SEQ, D_FF), bf16),
                   jax.ShapeDtypeStruct((1, D_MODEL), f32)],
        scratch_shapes=[pltpu.VMEM((tm, D_MODEL), f32)],
        compiler_params=_params("arbitrary", "arbitrary"),
    )(dx2, u, x1, g, w_up, w_down)


def mlp_bwd_weights(dx2, u, du, x1, g, tm=512):
    nt = SEQ // tm

    def body(d_ref, u_ref, du_ref, x_ref, g_ref, dwu_ref, dwd_ref, dwu16_ref, dwd16_ref, h_scr, d_scr):
        j, i = pl.program_id(0), pl.program_id(1)

        @pl.when(j == 0)
        def _():
            xv = x_ref[...]
            h_scr[i] = (xv * _rms(xv) * g_ref[...]).astype(bf16)
            d_scr[i] = d_ref[...].astype(bf16)

        @pl.when(i == 0)
        def _():
            dwu_ref[...] = jnp.zeros_like(dwu_ref)
            dwd_ref[...] = jnp.zeros_like(dwd_ref)

        dwu_ref[...] += _bdot_tn(h_scr[i], du_ref[...])
        a = jnp.square(jnp.maximum(u_ref[...], 0.0))
        dwd_ref[...] += _bdot_tn(a, d_scr[i])

        @pl.when(i == nt - 1)
        def _():
            dwu16_ref[...] = dwu_ref[...].astype(bf16)
            dwd16_ref[...] = dwd_ref[...].astype(bf16)

    up = pl.BlockSpec((None, D_MODEL, FF_TILE), lambda j, i: (j, 0, 0))
    down = pl.BlockSpec((None, FF_TILE, D_MODEL), lambda j, i: (j, 0, 0))
    first_pass = pl.BlockSpec((tm, D_MODEL), lambda j, i: (jnp.where(j == 0, i, nt - 1), 0))
    return pl.pallas_call(
        body, name="mlp_bwd_weights", grid=(N_CHIPS, nt),
        in_specs=[first_pass, pl.BlockSpec((tm, FF_TILE), lambda j, i: (i, j)),
                  pl.BlockSpec((tm, FF_TILE), lambda j, i: (i, j)), first_pass, _full((1, D_MODEL))],
        out_specs=[up, down, up, down],
        out_shape=[jax.ShapeDtypeStruct((N_CHIPS, D_MODEL, FF_TILE), f32), jax.ShapeDtypeStruct((N_CHIPS, FF_TILE, D_MODEL), f32),
                   jax.ShapeDtypeStruct((N_CHIPS, D_MODEL, FF_TILE), bf16), jax.ShapeDtypeStruct((N_CHIPS, FF_TILE, D_MODEL), bf16)],
        scratch_shapes=[pltpu.VMEM((nt, tm, D_MODEL), bf16), pltpu.VMEM((nt, tm, D_MODEL), bf16)],
        compiler_params=_params("arbitrary", "arbitrary"),
    )(dx2, u, du, x1, g)


def loss_head(y, target, tm=512):
    def body(y_ref, t_ref, dy_ref, l_ref):
        @pl.when(pl.program_id(0) == 0)
        def _():
            l_ref[...] = jnp.zeros_like(l_ref)

        d = y_ref[...] - t_ref[...]
        dy_ref[...] = d * (1.0 / D_MODEL)
        part = jnp.sum(jnp.mean(d * d, axis=-1, keepdims=True), axis=0, keepdims=True)
        l_ref[...] += 0.5 * part

    tok = pl.BlockSpec((tm, D_MODEL), lambda i: (i, 0))
    return pl.pallas_call(
        body, name="loss_head", grid=(SEQ // tm,), in_specs=[tok, tok], out_specs=[tok, _full((1, 1))],
        out_shape=[jax.ShapeDtypeStruct((SEQ, D_MODEL), f32), jax.ShapeDtypeStruct((1, 1), f32)],
        compiler_params=_params("arbitrary"),
    )(y, target)


def _pad_lane(v):
    return jnp.pad(v, (0, LANE - v.shape[0]))[None, :]


def local_step(x, target, w, prov):
    bucket = jnp.asarray(_bucket_table().T)
    bias = bias_build(w["rel_bias"], bucket)
    saved = []
    for l in range(DEPTH):
        g_mix = w["mix_norm_g"][l][None, :] + prov.stage(("begin", l), x)
        w_in = prov.w_in(l, x)
        proj = in_fwd(x, g_mix, w_in)
        conv_b = w["conv_b"][l][None, :]
        act = conv_fwd(proj, w["conv_w"][l], conv_b)
        dtb = _pad_lane(w["dt_bias"][l]) + prov.stage(("mid", l), act)
        alog, dsk = _pad_lane(w["a_log"][l]), _pad_lane(w["d_skip"][l])
        ng = w["ssm_norm_g"][l][None, :]
        ssm, ypre, states = ssd_fwd_g(act, proj, dtb, alog, dsk, ng)
        qg, kg = w["q_gain"][l][:, None] + 0.0 * ssm[:1, :1], w["k_gain"][l][None, :]
        attn = attn_fwd_t(proj, qg, kg, w["sinks"][l], bias)
        tok = prov.stage(("pre_out", l), attn)
        w_out = prov.w_out(l, attn) + jnp.asarray(tok, bf16)
        x1 = out_fwd(x, attn, ssm, w_out)
        g_mlp = w["mlp_norm_g"][l][None, :] + prov.stage(("pre_mlp", l), x1)
        w_up, w_down = prov.mlp(l, x1)
        x2, u = mlp_fwd(x1, g_mlp, w_up, w_down)
        saved.append(dict(x=x, proj=proj, attn=attn, act=act, ssm=ssm, ypre=ypre, states=states, x1=x1, u=u,
                          g_mix=g_mix, qg=qg, kg=kg, conv_b=conv_b, dtb=dtb, alog=alog, dsk=dsk, ng=ng, g_mlp=g_mlp,
                          w_in=w_in, w_out=w_out, w_up=w_up, w_down=w_down))
        x = x2
    dx, loss = loss_head(x, target)
    grads = [None] * DEPTH
    dbands = [None] * DEPTH
    tok = 0.0
    for l in reversed(range(DEPTH)):
        s = saved[l]
        g_mlp = s["g_mlp"] + tok
        dx1, du, dg_mlp = mlp_bwd_data(dx, s["u"], s["x1"], g_mlp, s["w_up"], s["w_down"])
        dw_up, dw_down, dw_up16, dw_down16 = mlp_bwd_weights(dx, s["u"], du, s["x1"], g_mlp)
        tok = prov.grads(("mlp", l), dict(w_up=(dw_up, dw_up16), w_down=(dw_down, dw_down16)), dw_down)
        dattn, dssm, dw_out, dw_out16 = out_bwd(dx1, s["attn"], s["ssm"], s["w_out"])
        dact, ddt, dz, dng, dpar = ssd_bwd_g(s["act"], s["proj"], s["ypre"], s["states"], dssm, s["dtb"] + tok, s["alog"],
                                           s["dsk"], s["ng"])
        conv_b = s["conv_b"] + prov.stage(("bwd_mid", l), dact)
        dxbc, dconv_w, dconv_b = conv_bwd(s["proj"], dact, w["conv_w"][l], conv_b)
        dq, dk, dv, dband, dsink, dqg, dkg = attn_bwd_t(s["proj"], dattn, s["qg"], s["kg"], w["sinks"][l], bias)
        dbands[l] = dband
        g_mix = s["g_mix"]
        if l == 0:
            d_rel = bias_bwd(dbands[0], dbands[1], bucket)
            g_mix = g_mix + 0.0 * d_rel[:1, :1]
        dx, dw_in, dg_mix = in_bwd(dq, dz, dxbc, dk, dv, ddt, s["x"], g_mix, s["w_in"], dx1)
        tok = prov.grads(("mix", l), dict(w_in=split_w_in_grad(dw_in), w_out=(dw_out, dw_out16)), dx)
        grads[l] = dict(mix_norm_g=dg_mix[0], q_gain=dqg[:, 0], k_gain=dkg[0], sinks=dsink[:, 0],
                        conv_w=dconv_w, conv_b=dconv_b[0], dt_bias=dpar[0, :SSM_HEADS], a_log=dpar[1, :SSM_HEADS],
                        d_skip=dpar[2, :SSM_HEADS], ssm_norm_g=dng[0], mlp_norm_g=dg_mlp[0])
    out = {k: jnp.stack([grads[l][k] for l in range(DEPTH)]) for k in grads[0]}
    out["rel_bias"] = d_rel[:, :N_Q_HEADS]
    return loss, dx, out, tok


MESH = pl.DeviceIdType.MESH
HBM = pl.BlockSpec(memory_space=pltpu.HBM)
N_PEER_CHIPS = N_CHIPS - 1
N_DEVICES = 8


def _coords():
    return lax.axis_index("x"), lax.axis_index("y"), lax.axis_index("c")


def _peer_chips(x, y):
    return [(1 - x, y), (x, 1 - y), (1 - x, 1 - y)]


def _remote(src, dst, send_sem, recv_sem, device):
    return pltpu.make_async_remote_copy(src_ref=src, dst_ref=dst, send_sem=send_sem, recv_sem=recv_sem,
                                        device_id=device, device_id_type=MESH)


SEM = pl.BlockSpec(memory_space=pltpu.SEMAPHORE)
ANY = pl.BlockSpec(memory_space=pl.ANY)
DATAFLOW = pltpu.SideEffectType.DATAFLOW_SIDE_EFFECTING


def _gather_copies(kind, src_refs, land_refs, ssem, rsem):
    x, y, c = _coords()
    k_me = 2 * x + y
    n = len(land_refs)
    cps = []
    for p, land in enumerate(land_refs):
        hr = land.shape[1] // 2
        rows = pl.ds(c * hr, hr)
        for j, chip in enumerate(_peer_chips(x, y)):
            i = 3 * p + j
            if kind == "ici":
                cps.append(_remote(src_refs[p].at[rows, :], land.at[k_me, rows, :], ssem.at[i], rsem.at[i], (*chip, c)))
            else:
                got = land.at[2 * chip[0] + chip[1], rows, :]
                cps.append(_remote(got, got, ssem.at[i], rsem.at[i], (x, y, 1 - c)))
        if kind == "relay":
            cps.append(_remote(src_refs[p], land.at[k_me], ssem.at[3 * n + p], rsem.at[3 * n + p], (x, y, 1 - c)))
    return cps


def gather_now(srcs, conv):
    n = len(srcs)

    def body(*refs):
        src_refs, conv_ref = refs[:n], refs[n]
        lands, gconv = refs[n + 1:2 * n + 1], refs[2 * n + 1]
        ssem, rsem, fsem, frsem, csem, crsem = refs[2 * n + 2:]
        x, y, c = _coords()
        k_me = 2 * x + y
        targets = [(*chip, c) for chip in _peer_chips(x, y)] + [(x, y, 1 - c)]
        ici = _gather_copies("ici", src_refs, lands, ssem, rsem)
        relay = _gather_copies("relay", src_refs, lands, fsem, frsem)
        passed = [cp for i, cp in enumerate(relay) if i % 4 != 3]
        own = relay[3::4]
        conv_cps = [_remote(conv_ref, gconv.at[k_me], csem.at[j], crsem.at[j], t) for j, t in enumerate(targets)]
        for cp in ici + conv_cps + own:
            cp.start()
        for cp, fw in zip(ici, passed):
            cp.wait_recv()
            fw.start()
        for cp in conv_cps + relay:
            cp.wait_recv()
        for cp in ici + relay + conv_cps:
            cp.wait_send()

    out_shape = [jax.ShapeDtypeStruct((N_CHIPS,) + s.shape, s.dtype) for s in srcs]
    out_shape.append(jax.ShapeDtypeStruct((N_CHIPS,) + conv.shape, conv.dtype))
    sems = lambda k: pltpu.SemaphoreType.DMA((k,))
    return pl.pallas_call(
        body, name="gather_now", out_shape=out_shape, in_specs=[HBM] * (n + 1), out_specs=[HBM] * (n + 1),
        scratch_shapes=[sems(3 * n), sems(3 * n), sems(4 * n), sems(4 * n), sems(N_CHIPS), sems(N_CHIPS)],
    )(*srcs, conv)


def _gather_maker(kind, n_src):
    def make(refs, ssem, rsem):
        cps = _gather_copies(kind, refs[:n_src], refs[n_src:], ssem, rsem)
        return cps, cps
    return make


def _scatter_maker(n):
    def make(refs, ssem, rsem):
        x, y, c = _coords()
        k_me = 2 * x + y
        sends, arrivals = [], []
        for p in range(n):
            src, land = refs[p], refs[n + p]
            sends.append(_remote(src.at[k_me, 1 - c], land.at[0], ssem.at[7 * p], rsem.at[7 * p], (x, y, 1 - c)))
            for j, chip in enumerate(_peer_chips(x, y)):
                for cc in range(2):
                    sends.append(_remote(src.at[2 * chip[0] + chip[1], cc], land.at[1 + 2 * j + c],
                                         ssem.at[7 * p + 1 + 2 * j + cc], rsem.at[7 * p + 1 + 2 * j + c], (*chip, cc)))
            for s in range(7):
                arrivals.append(_remote(land.at[s], land.at[s], ssem.at[7 * p + s], rsem.at[7 * p + s], (x, y, 1 - c)))
        return sends, arrivals
    return make


def _share_maker(n):
    def make(refs, ssem, rsem):
        x, y, c = _coords()
        sends = [_remote(refs[p].at[c], refs[p].at[c], ssem.at[p], rsem.at[p], (x, y, 1 - c)) for p in range(n)]
        arrivals = [_remote(refs[p].at[1 - c], refs[p].at[1 - c], ssem.at[p], rsem.at[p], (x, y, 1 - c)) for p in range(n)]
        return sends, arrivals
    return make


def split_start(name, make, n_sems, operands, after):
    n = len(operands)

    def body(*refs):
        ssem, rsem, token = refs[n + 1], refs[n + 2], refs[-1]
        for cp in make(refs[:n], ssem, rsem)[0]:
            cp.start()
        token[...] = jnp.zeros_like(token)

    ops = [pltpu.with_memory_space_constraint(a, pltpu.HBM) for a in operands]
    outs = pl.pallas_call(
        body, name=name,
        out_shape=(pltpu.SemaphoreType.DMA((n_sems,)), pltpu.SemaphoreType.DMA((n_sems,)),
                   *[pltpu.HBM(a.shape, a.dtype) for a in ops], jax.ShapeDtypeStruct((8, LANE), f32)),
        in_specs=[HBM] * n + [ANY], out_specs=(SEM, SEM, *[HBM] * n, pl.BlockSpec(memory_space=pltpu.VMEM)),
        input_output_aliases={i: 2 + i for i in range(n)},
        compiler_params=pltpu.CompilerParams(has_side_effects=DATAFLOW),
    )(*ops, after)
    return dict(name=name, make=make, ssem=outs[0], rsem=outs[1], operands=outs[2:2 + n], token=outs[-1][0, 0])


def split_wait(handle, after):
    n = len(handle["operands"])

    def body(*refs):
        sends, arrivals = handle["make"](refs[:n], refs[n], refs[n + 1])
        for cp in sends:
            cp.wait_send()
        for cp in arrivals:
            cp.wait_recv()

    outs = pl.pallas_call(
        body, name=handle["name"].replace("start", "wait"),
        out_shape=tuple(pltpu.HBM(a.shape, a.dtype) for a in handle["operands"]),
        in_specs=[HBM] * n + [SEM, SEM, ANY], out_specs=tuple([HBM] * n),
        input_output_aliases={i: i for i in range(n)},
        compiler_params=pltpu.CompilerParams(has_side_effects=DATAFLOW),
    )(*handle["operands"], handle["ssem"], handle["rsem"], after)
    return list(outs)


def piece_sum(g, recv, kc_arr):
    _, _, rb, cc = g.shape
    tr = min(256, rb)

    def body(kc_ref, g_ref, r_ref, o_ref):
        acc = g_ref[...]
        for s in range(7):
            acc = acc + r_ref[s].astype(f32)
        o_ref[...] = acc

    return pl.pallas_call(
        body, name="piece_sum",
        grid_spec=pltpu.PrefetchScalarGridSpec(
            num_scalar_prefetch=1, grid=(rb // tr,),
            in_specs=[pl.BlockSpec((None, None, tr, cc), lambda r, kc: (kc[0], kc[1], r, 0)),
                      pl.BlockSpec((7, tr, cc), lambda r, kc: (0, r, 0))],
            out_specs=pl.BlockSpec((None, tr, cc), lambda r, kc: (kc[1], r, 0))),
        out_shape=jax.ShapeDtypeStruct((2, rb, cc), f32),
        compiler_params=_params("arbitrary"),
    )(kc_arr, g, recv)


def small_all_reduce(vec):
    def body(v_ref, o_ref, gat, ssem, rsem):
        x, y, c = _coords()
        me = 4 * x + 2 * y + c
        gat[me] = v_ref[...]
        sends = []
        for t in range(1, N_DEVICES):
            peer = (x ^ (t >> 2), y ^ ((t >> 1) & 1), c ^ (t & 1))
            cp = _remote(v_ref, gat.at[me], ssem.at[t - 1], rsem.at[t - 1], peer)
            cp.start()
            sends.append(cp)
        for t in range(1, N_DEVICES):
            peer = (x ^ (t >> 2), y ^ ((t >> 1) & 1), c ^ (t & 1))
            slot = gat.at[4 * peer[0] + 2 * peer[1] + peer[2]]
            _remote(slot, slot, ssem.at[t - 1], rsem.at[t - 1], peer).wait_recv()
        for cp in sends:
            cp.wait_send()
        acc = gat[0]
        for d in range(1, N_DEVICES):
            acc = acc + gat[d]
        o_ref[...] = acc

    return pl.pallas_call(
        body, name="small_all_reduce", out_shape=jax.ShapeDtypeStruct(vec.shape, vec.dtype),
        in_specs=[pl.BlockSpec(memory_space=pltpu.VMEM)], out_specs=pl.BlockSpec(memory_space=pltpu.VMEM),
        scratch_shapes=[pltpu.VMEM((N_DEVICES,) + vec.shape, vec.dtype), pltpu.SemaphoreType.DMA((N_DEVICES - 1,)),
                        pltpu.SemaphoreType.DMA((N_DEVICES - 1,))],
    )(vec)


def _adamw_math(w, g, m, v):
    m_new = ADAM_B1 * m + (1.0 - ADAM_B1) * g
    v_new = ADAM_B2 * v + (1.0 - ADAM_B2) * jnp.square(g)
    m_hat = m_new / (1.0 - ADAM_B1 ** ADAM_STEP)
    v_hat = v_new / (1.0 - ADAM_B2 ** ADAM_STEP)
    delta = -ADAM_LR * (m_hat / (jnp.sqrt(v_hat) + ADAM_EPS) + ADAM_WD * w)
    return delta, m_new, v_new


def adamw_shard(w, g0, g1, m, v):
    depth, rows, cols = w.shape
    half = rows // 2
    tr = min(256, half)
    nr = half // tr

    def body(w_ref, g0_ref, g1_ref, m_ref, v_ref, go_ref, d_ref, nm_ref, nv_ref):
        gv = jnp.where(pl.program_id(0) == 0, g0_ref[...], g1_ref[...])
        go_ref[...] = gv
        d_ref[...], nm_ref[...], nv_ref[...] = _adamw_math(w_ref[...], gv, m_ref[...], v_ref[...])

    spec = pl.BlockSpec((None, tr, cols), lambda l, h, r: (l, h * nr + r, 0))
    g0spec = pl.BlockSpec((None, tr, cols), lambda l, h, r: (jnp.where(l == 0, h, 1), jnp.where(l == 0, r, nr - 1), 0))
    g1spec = pl.BlockSpec((None, tr, cols), lambda l, h, r: (jnp.where(l == 1, h, 0), jnp.where(l == 1, r, 0), 0))
    return pl.pallas_call(
        body, name="adamw_shard", grid=(depth, 2, nr), in_specs=[spec, g0spec, g1spec, spec, spec], out_specs=[spec] * 4,
        out_shape=[jax.ShapeDtypeStruct(w.shape, f32)] * 4,
        compiler_params=_params("arbitrary", "arbitrary", "arbitrary"),
    )(w, g0, g1, m, v)


def adamw_small(w, g, m, v):
    def body(w_ref, g_ref, m_ref, v_ref, d_ref, nm_ref, nv_ref):
        d_ref[...], nm_ref[...], nv_ref[...] = _adamw_math(w_ref[...], g_ref[...], m_ref[...], v_ref[...])

    return pl.pallas_call(
        body, name="adamw_small", out_shape=[jax.ShapeDtypeStruct(w.shape, f32)] * 3,
    )(w, g, m, v)


WEIGHTS = ("mix_norm_g", "w_in", "q_gain", "k_gain", "sinks", "rel_bias", "conv_w", "conv_b", "dt_bias", "a_log", "d_skip",
           "ssm_norm_g", "w_out", "mlp_norm_g", "w_up", "w_down")
BIG = ("w_in", "w_out", "w_up", "w_down")
SMALL = tuple(n for n in WEIGHTS if n not in BIG)
PACK_COLS = 1024
PACK_ROWS = 16


def _pack(named, last=None):
    flat = jnp.concatenate([named[n].reshape(-1) for n in SMALL])
    tail = jnp.zeros((1,), f32) if last is None else last.reshape(1)
    pad = jnp.zeros((PACK_ROWS * PACK_COLS - flat.shape[0] - 1,), f32)
    return jnp.concatenate([flat, pad, tail]).reshape(PACK_ROWS, PACK_COLS)


def _unpack(buf, shapes):
    flat = buf.reshape(-1)
    out, at = {}, 0
    for n in SMALL:
        size = int(np.prod(shapes[n]))
        out[n] = flat[at:at + size].reshape(shapes[n])
        at += size
    return out


class _Exchange:
    GROUPS = {"A": (("w_up", 0), ("w_down", 0)), "B": (("w_in", 1), ("w_out", 1)), "C": (("w_up", 1), ("w_down", 1))}
    ICI_AT = {("mid", 0): "B", ("pre_out", 0): "C"}
    RELAY_AT = {("pre_out", 0): "A", ("pre_mlp", 0): "B", ("mid", 1): "C"}
    LAST = ("mix", 0)

    def __init__(self, wts, k_me, kc_arr):
        self.wts, self.k_me, self.kc_arr = wts, k_me, kc_arr
        self.own = {(n, l): wts[n][l].astype(bf16) for n in BIG for l in range(DEPTH)}
        now = gather_now([self.own["w_in", 0], self.own["w_out", 0]], wts["conv_w"])
        self.ready = {("w_in", 0): now[0], ("w_out", 0): now[1]}
        self.conv_w = jnp.transpose(now[2], (1, 2, 0, 3)).reshape(DEPTH, CONV_WIDTH, D_CONV)
        self.ici, self.relay = {}, {}
        self.gview, self.scatter, self.share, self.reduced = {}, [], [], {}
        self._start_ici("A", now[2])

    def _start_ici(self, g, after):
        srcs = [self.own[p] for p in self.GROUPS[g]]
        lands = [lax.empty((N_CHIPS,) + s.shape, s.dtype) for s in srcs]
        self.ici[g] = split_start("gather%s_ici_start" % g, _gather_maker("ici", len(srcs)), 3 * len(srcs), srcs + lands,
                                  after)
        return self.ici[g]["token"]

    def stage(self, name, after):
        if name == ("begin", 0):
            return self.ici["A"]["token"]
        tok = 0.0
        g = self.RELAY_AT.get(name)
        if g is not None:
            n = len(self.GROUPS[g])
            self.relay[g] = split_start("gather%s_relay_start" % g, _gather_maker("relay", n), 4 * n,
                                        split_wait(self.ici[g], after), after)
            tok = self.relay[g]["token"]
        if name in self.ICI_AT:
            tok = tok + self._start_ici(self.ICI_AT[name], after)
        return tok

    def _get(self, piece, after):
        if piece not in self.ready:
            g = [k for k, pieces in self.GROUPS.items() if piece in pieces][0]
            lands = split_wait(self.relay[g], after)[len(self.GROUPS[g]):]
            self.ready.update(zip(self.GROUPS[g], lands))
        return self.ready[piece]

    def w_in(self, l, after):
        return align_w_in(self._get(("w_in", l), after))

    def w_out(self, l, after):
        return self._get(("w_out", l), after).reshape(D_MODEL, D_MODEL)

    def mlp(self, l, after):
        return self._get(("w_up", l), after), self._get(("w_down", l), after)

    def _view(self, n, g):
        _, rows, cols = self.wts[n].shape
        return g.reshape(N_CHIPS, 2, rows // 2, cols)

    def grads(self, name, arrays, after):
        if name == self.LAST:
            self.held = (name, arrays)
            return 0.0
        return self._scatter(name, arrays, after) + self._advance(after, 1)

    def flush(self, after):
        return self._scatter(*self.held, after) + self._advance(after, 1)

    def _scatter(self, name, arrays, after):
        pieces = [(n, name[1]) for n in arrays]
        views = [self._view(n, g) for n, (g, _) in arrays.items()]
        sends = [g16.reshape(v.shape) for v, (_, g16) in zip(views, arrays.values())]
        self.gview.update(zip(pieces, views))
        lands = [lax.empty((7,) + v.shape[2:], bf16) for v in views]
        h = split_start("scatter_%s%d_start" % name, _scatter_maker(len(views)), 7 * len(views), sends + lands, after)
        self.scatter.append((pieces, h))
        return h["token"]

    def _take_share(self, after):
        pieces, h = self.share.pop(0)
        self.reduced.update(zip(pieces, split_wait(h, after)))

    def _take_scatter(self, after):
        pieces, h = self.scatter.pop(0)
        lands = split_wait(h, after)[len(pieces):]
        sums = [piece_sum(self.gview[p], land, self.kc_arr) for p, land in zip(pieces, lands)]
        hs = split_start(h["name"].replace("scatter", "share"), _share_maker(len(sums)), len(sums), sums, after)
        self.share.append((pieces, hs))
        return hs["token"]

    def _advance(self, after, newest):
        if self.share:
            self._take_share(after)
        return self._take_scatter(after) if len(self.scatter) > newest else 0.0

    def reduced_grads(self, names, after):
        want = [(n, l) for n in names for l in range(DEPTH)]
        while not all(p in self.reduced for p in want):
            if any(p in pieces for p in want for pieces, _ in self.share):
                self._take_share(after)
            else:
                self._take_scatter(after)
        return {n: [self.reduced[n, l] for l in range(DEPTH)] for n in names}


def kernel(x, mix_norm_g, w_in, q_gain, k_gain, sinks, rel_bias, conv_w, conv_b, dt_bias, a_log, d_skip, ssm_norm_g, w_out, mlp_norm_g, w_up, w_down, loss_target, m_mix_norm_g, m_w_in, m_q_gain, m_k_gain, m_sinks, m_rel_bias, m_conv_w, m_conv_b, m_dt_bias, m_a_log, m_d_skip, m_ssm_norm_g, m_w_out, m_mlp_norm_g, m_w_up, m_w_down, v_mix_norm_g, v_w_in, v_q_gain, v_k_gain, v_sinks, v_rel_bias, v_conv_w, v_conv_b, v_dt_bias, v_a_log, v_d_skip, v_ssm_norm_g, v_w_out, v_mlp_norm_g, v_w_up, v_w_down):
    wts = dict(mix_norm_g=mix_norm_g, w_in=w_in, q_gain=q_gain, k_gain=k_gain, sinks=sinks, rel_bias=rel_bias, conv_w=conv_w,
               conv_b=conv_b, dt_bias=dt_bias, a_log=a_log, d_skip=d_skip, ssm_norm_g=ssm_norm_g, w_out=w_out,
               mlp_norm_g=mlp_norm_g, w_up=w_up, w_down=w_down)
    mom = dict(mix_norm_g=m_mix_norm_g, w_in=m_w_in, q_gain=m_q_gain, k_gain=m_k_gain, sinks=m_sinks, rel_bias=m_rel_bias,
               conv_w=m_conv_w, conv_b=m_conv_b, dt_bias=m_dt_bias, a_log=m_a_log, d_skip=m_d_skip, ssm_norm_g=m_ssm_norm_g,
               w_out=m_w_out, mlp_norm_g=m_mlp_norm_g, w_up=m_w_up, w_down=m_w_down)
    var = dict(mix_norm_g=v_mix_norm_g, w_in=v_w_in, q_gain=v_q_gain, k_gain=v_k_gain, sinks=v_sinks, rel_bias=v_rel_bias,
               conv_w=v_conv_w, conv_b=v_conv_b, dt_bias=v_dt_bias, a_log=v_a_log, d_skip=v_d_skip, ssm_norm_g=v_ssm_norm_g,
               w_out=v_w_out, mlp_norm_g=v_mlp_norm_g, w_up=v_w_up, w_down=v_w_down)
    xi, yi, ci = _coords()
    k_me = 2 * xi + yi
    kc_arr = jnp.stack([k_me, ci]).astype(jnp.int32)

    prov = _Exchange(wts, k_me, kc_arr)
    small_w = {n: wts[n] for n in SMALL}
    small_w["conv_w"] = prov.conv_w
    loss, dx, grads, tok = local_step(x[0], loss_target[0], small_w, prov)

    small_shapes = {n: grads[n].shape for n in SMALL}
    small_sum = small_all_reduce(_pack(grads, loss) + tok)
    loss = small_sum[PACK_ROWS - 1, PACK_COLS - 1]
    tok = prov.flush(small_sum)
    small = _unpack(small_sum, small_shapes)
    cols = conv_w.shape[-1]
    small["conv_w"] = lax.dynamic_slice_in_dim(small["conv_w"], k_me * cols, cols, axis=2)
    g_out_d, d_out_d, m_out_d, v_out_d = {}, {}, {}, {}
    shard_shapes = {n: wts[n].shape for n in SMALL}
    d, nm, nv = adamw_small(_pack(wts), _pack(small) + tok, _pack(mom), _pack(var))
    for dst, buf in ((d_out_d, d), (m_out_d, nm), (v_out_d, nv)):
        dst.update(_unpack(buf, shard_shapes))
    g_out_d.update(small)

    after = d
    for names in (("w_up", "w_down"), ("w_in", "w_out")):
        for n, (g0, g1) in prov.reduced_grads(names, after).items():
            g_out_d[n], d_out_d[n], m_out_d[n], v_out_d[n] = adamw_shard(wts[n], g0, g1, mom[n], var[n])
            after = d_out_d[n]

    return (loss, dx[None], *[g_out_d[n] for n in WEIGHTS], *[d_out_d[n] for n in WEIGHTS],
            *[m_out_d[n] for n in WEIGHTS], *[v_out_d[n] for n in WEIGHTS])
```

```python
import functools

import numpy as np
import jax
import jax.numpy as jnp
from jax import lax
from jax.experimental import pallas as pl
from jax.experimental.pallas import tpu as pltpu

f32 = jnp.float32
bf16 = jnp.bfloat16

SEQ = 2048
D_MODEL = 1024
DEPTH = 2
HEAD_DIM = 64
N_Q_HEADS = 8
N_KV_HEADS = 2
Q_PER_KV = N_Q_HEADS // N_KV_HEADS
BLOCK = 128
N_BLOCKS = SEQ // BLOCK
N_BUCKETS = 32
MAX_DISTANCE = 128
SSM_HEADS = 8
SSM_HEAD_DIM = 64
SSM_GROUPS = 2
HEADS_PER_GROUP = SSM_HEADS // SSM_GROUPS
SSM_STATE = 128
CONV_WIDTH = 4
CHUNK = 128
N_CHUNKS = SEQ // CHUNK
D_FF = 4 * D_MODEL
D_ATTN = N_Q_HEADS * HEAD_DIM
D_KV = N_KV_HEADS * HEAD_DIM
D_SSM = SSM_HEADS * SSM_HEAD_DIM
D_BC = SSM_GROUPS * SSM_STATE
D_CONV = D_SSM + 2 * D_BC
D_IN = D_ATTN + 2 * D_KV + D_SSM + D_CONV + SSM_HEADS
EPS = 1e-6
NEG = -1e30
N_CHIPS = 4
FF_TILE = D_FF // N_CHIPS

LANE = 128
PW = D_ATTN + D_SSM + D_CONV + 2 * D_KV + LANE
OFF_Q, OFF_Z, OFF_X, OFF_K, OFF_V, OFF_DT = 0, 512, 1024, 2048, 2176, 2304

ADAM_LR = 0.001
ADAM_B1 = 0.9
ADAM_B2 = 0.999
ADAM_EPS = 1e-08
ADAM_WD = 0.01
ADAM_STEP = 10

VMEM_LIMIT = 56 * 1024 * 1024


def _params(*sem):
    return pltpu.CompilerParams(dimension_semantics=tuple(sem), vmem_limit_bytes=VMEM_LIMIT)


def _bdot(a, b):
    return jnp.dot(a.astype(bf16), b.astype(bf16), preferred_element_type=f32)


def _bdot_nt(a, b):
    return lax.dot_general(a.astype(bf16), b.astype(bf16), (((1,), (1,)), ((), ())), preferred_element_type=f32)


def _bdot_tn(a, b):
    return lax.dot_general(a.astype(bf16), b.astype(bf16), (((0,), (0,)), ((), ())), preferred_element_type=f32)


def _hdot(a, b):
    return jnp.dot(a, b, precision=lax.Precision.HIGHEST, preferred_element_type=f32)


def _sigmoid(x):
    return 1.0 / (1.0 + jnp.exp(-x))


def _softplus(x):
    return jnp.maximum(x, 0.0) + jnp.log1p(jnp.exp(-jnp.abs(x)))


def _rms(x):
    return lax.rsqrt(jnp.mean(x * x, axis=-1, keepdims=True) + EPS)


def _rms_bwd(dy, xhat, r, g):
    t = dy * g
    return r * (t - xhat * jnp.mean(t * xhat, axis=-1, keepdims=True))


def _full(shape):
    return pl.BlockSpec(shape, lambda *_: (0,) * len(shape))


def _bucket_table():
    qi = np.arange(BLOCK)[:, None]
    kj = np.arange(2 * BLOCK)[None, :]
    dist = qi + BLOCK - kj
    ok = (dist >= 0) & (dist < 128)
    d = np.clip(dist, 0, None)
    max_exact = N_BUCKETS // 2
    d_f = np.maximum(d, 1).astype(np.float32)
    large = max_exact + (np.log(d_f / np.float32(max_exact)) / np.float32(np.log(MAX_DISTANCE / max_exact))
                         * np.float32(N_BUCKETS - max_exact)).astype(np.int32)
    large = np.minimum(large, N_BUCKETS - 1)
    bucket = np.where(d < max_exact, d, large)
    return np.where(ok, bucket, -1).astype(np.int32)


def bias_build(rel_bias, bucket):
    def body(rel_ref, bkt_ref, o_ref):
        bkt = bkt_ref[...]
        for h in range(N_Q_HEADS):
            acc = jnp.where(bkt < 0, NEG, 0.0).astype(f32)
            for b in range(N_BUCKETS):
                acc = acc + jnp.where(bkt == b, rel_ref[b, h], 0.0)
            o_ref[h] = acc

    return pl.pallas_call(
        body, name="bias_build", out_shape=jax.ShapeDtypeStruct((N_Q_HEADS,) + bucket.shape, f32),
        in_specs=[pl.BlockSpec(memory_space=pltpu.SMEM), pl.BlockSpec(memory_space=pltpu.VMEM)],
        out_specs=pl.BlockSpec(memory_space=pltpu.VMEM),
    )(rel_bias, bucket)


def bias_bwd(dband0, dband1, bucket):
    def body(d0_ref, d1_ref, bkt_ref, o_ref):
        bkt = bkt_ref[...]
        o_ref[...] = jnp.zeros_like(o_ref)
        for h in range(N_Q_HEADS):
            d = d0_ref[h] + d1_ref[h]
            for b in range(N_BUCKETS):
                part = jnp.sum(jnp.where(bkt == b, d, 0.0), axis=1, keepdims=True)
                o_ref[b:b + 1, h:h + 1] = jnp.sum(part, axis=0, keepdims=True)

    return pl.pallas_call(
        body, name="bias_bwd", out_shape=jax.ShapeDtypeStruct((N_BUCKETS, LANE), f32),
    )(dband0, dband1, bucket)


W_IN_SHARD = D_IN // N_CHIPS
_ALIGNED_PIECES = ((0, 0, 512), (1, 190, 578), (2, 0, 124), (2, 124, 578), (3, 0, 570), (0, 512, 578), (1, 0, 62),
                   (1, 62, 190), (3, 570, 578))
_SHARD_PIECES = (((0, 512), (2048, 2114)), ((2114, 2176), (2176, 2304), (512, 900)), ((900, 1024), (1024, 1478)),
                 ((1478, 2048), (2304, 2312)))


def align_w_in(shards, tr=256):
    def body(s_ref, o_ref):
        parts = [s_ref[k, :, a:b] for k, a, b in _ALIGNED_PIECES]
        parts.append(jnp.zeros((tr, LANE - SSM_HEADS), s_ref.dtype))
        o_ref[...] = jnp.concatenate(parts, axis=-1)

    return pl.pallas_call(
        body, name="align_w_in", grid=(D_MODEL // tr,),
        in_specs=[pl.BlockSpec((N_CHIPS, tr, W_IN_SHARD), lambda i: (0, i, 0))],
        out_specs=pl.BlockSpec((tr, PW), lambda i: (i, 0)),
        out_shape=jax.ShapeDtypeStruct((D_MODEL, PW), shards.dtype),
        compiler_params=_params("arbitrary"),
    )(shards)


def split_w_in_grad(dw, tr=256):
    def body(d_ref, o_ref, o16_ref):
        for k, pieces in enumerate(_SHARD_PIECES):
            part = jnp.concatenate([d_ref[:, a:b] for a, b in pieces], axis=-1)
            o_ref[k] = part
            o16_ref[k] = part.astype(bf16)

    spec = pl.BlockSpec((N_CHIPS, tr, W_IN_SHARD), lambda i: (0, i, 0))
    return pl.pallas_call(
        body, name="split_w_in_grad", grid=(D_MODEL // tr,),
        in_specs=[pl.BlockSpec((tr, PW), lambda i: (i, 0))], out_specs=[spec, spec],
        out_shape=[jax.ShapeDtypeStruct((N_CHIPS, D_MODEL, W_IN_SHARD), f32),
                   jax.ShapeDtypeStruct((N_CHIPS, D_MODEL, W_IN_SHARD), bf16)],
        compiler_params=_params("arbitrary"),
    )(dw)

def in_fwd(x, g, w, tm=512):
    def body(x_ref, g_ref, w_ref, o_ref):
        xv = x_ref[...]
        h = xv * _rms(xv) * g_ref[...]
        o_ref[...] = _bdot(h, w_ref[...])

    return pl.pallas_call(
        body, name="in_fwd", grid=(SEQ // tm,),
        in_specs=[pl.BlockSpec((tm, D_MODEL), lambda i: (i, 0)), _full((1, D_MODEL)), _resident((D_MODEL, PW))],
        out_specs=pl.BlockSpec((tm, PW), lambda i: (i, 0)),
        out_shape=jax.ShapeDtypeStruct((SEQ, PW), f32),
        compiler_params=_params("arbitrary"),
    )(x, g, w)


def _resident(shape):
    return pl.BlockSpec(shape, lambda *_: (0,) * len(shape), pipeline_mode=pl.Buffered(1))


def in_bwd(dq, dz, dxbc, dk, dv, ddt, x, g, w, dres, tm=512):
    def body(dq_ref, dz_ref, dx_ref, dk_ref, dv_ref, ddt_ref, x_ref, g_ref, w_ref, dres_ref, o_ref, dw_ref, dg_ref):
        i = pl.program_id(0)

        @pl.when(i == 0)
        def _():
            dw_ref[...] = jnp.zeros_like(dw_ref)
            dg_ref[...] = jnp.zeros_like(dg_ref)

        dproj = jnp.concatenate([dq_ref[...], dz_ref[...], dx_ref[...], dk_ref[...], dv_ref[...], ddt_ref[...]],
                                axis=-1).astype(bf16)
        xv = x_ref[...]
        r = _rms(xv)
        xhat = xv * r
        gv = g_ref[...]
        h = xhat * gv
        dw_ref[...] += _bdot_tn(h, dproj)
        dh = _bdot_nt(dproj, w_ref[...])
        dg_ref[...] += jnp.sum(dh * xhat, axis=0, keepdims=True)
        o_ref[...] = dres_ref[...] + _rms_bwd(dh, xhat, r, gv)

    tok = lambda w_: pl.BlockSpec((tm, w_), lambda i: (i, 0))
    return pl.pallas_call(
        body, name="in_bwd", grid=(SEQ // tm,),
        in_specs=[tok(D_ATTN), tok(D_SSM), tok(D_CONV), tok(D_KV), tok(D_KV), tok(LANE), tok(D_MODEL),
                  _full((1, D_MODEL)), _resident((D_MODEL, PW)), tok(D_MODEL)],
        out_specs=[tok(D_MODEL), _resident((D_MODEL, PW)), _full((1, D_MODEL))],
        out_shape=[jax.ShapeDtypeStruct((SEQ, D_MODEL), f32), jax.ShapeDtypeStruct((D_MODEL, PW), f32),
                   jax.ShapeDtypeStruct((1, D_MODEL), f32)],
        compiler_params=_params("arbitrary"),
    )(dq, dz, dxbc, dk, dv, ddt, x, g, w, dres)


def _attn_softmax_t(qk, bias_t, sink, first, key_row):
    s = qk * (HEAD_DIM ** -0.5) + bias_t
    s = jnp.where(jnp.logical_and(first, key_row < BLOCK), NEG, s)
    m = jnp.maximum(jnp.max(s, axis=0, keepdims=True), sink)
    p = jnp.exp(s - m)
    psink = jnp.exp(sink - m)
    inv = 1.0 / (jnp.sum(p, axis=0, keepdims=True) + psink)
    return p * inv, psink * inv


def _rms_t(x_t):
    return lax.rsqrt(jnp.mean(x_t * x_t, axis=0, keepdims=True) + EPS)


def attn_fwd_t(proj, q_gain_col, k_gain, sinks, bias_t):
    kcol, vcol = OFF_K // D_KV, OFF_V // D_KV

    def body(q_ref, kc_ref, kp_ref, vc_ref, vp_ref, qg_ref, kg_ref, sink_ref, bias_ref, o_ref, ot_scr):
        n = pl.program_id(0)
        first = n == 0
        key_row = lax.broadcasted_iota(jnp.int32, (2 * BLOCK, BLOCK), 0)
        k2 = jnp.concatenate([kp_ref[...], kc_ref[...]], axis=0)
        v_t = jnp.concatenate([vp_ref[...], vc_ref[...]], axis=0).T
        q_t = q_ref[...].T
        qg = jnp.broadcast_to(qg_ref[...], (HEAD_DIM, BLOCK))
        kg = kg_ref[...]
        for hk in range(N_KV_HEADS):
            sl = slice(hk * HEAD_DIM, (hk + 1) * HEAD_DIM)
            kk = k2[:, sl]
            kn = (kk * _rms(kk) * kg).astype(bf16)
            vt = v_t[sl, :].astype(bf16)
            heads = range(hk * Q_PER_KV, (hk + 1) * Q_PER_KV)
            qns = []
            for h in heads:
                qh = q_t[h * HEAD_DIM:(h + 1) * HEAD_DIM, :]
                qns.append(qh * _rms_t(qh) * qg)
            scores = [_bdot(kn, qn) for qn in qns]
            for h, s in zip(heads, scores):
                p, _ = _attn_softmax_t(s, bias_ref[h], sink_ref[h], first, key_row)
                ot_scr[h * HEAD_DIM:(h + 1) * HEAD_DIM, :] = _bdot(vt, p)
        o_ref[...] = ot_scr[...].T

    prev = lambda n: jnp.maximum(n - 1, 0)
    return pl.pallas_call(
        body, name="attn_fwd", grid=(N_BLOCKS,),
        in_specs=[pl.BlockSpec((BLOCK, D_ATTN), lambda n: (n, 0)),
                  pl.BlockSpec((BLOCK, D_KV), lambda n: (n, kcol)), pl.BlockSpec((BLOCK, D_KV), lambda n: (prev(n), kcol)),
                  pl.BlockSpec((BLOCK, D_KV), lambda n: (n, vcol)), pl.BlockSpec((BLOCK, D_KV), lambda n: (prev(n), vcol)),
                  _full((HEAD_DIM, 1)), _full((1, HEAD_DIM)), pl.BlockSpec(memory_space=pltpu.SMEM),
                  _full((N_Q_HEADS, 2 * BLOCK, BLOCK))],
        out_specs=pl.BlockSpec((BLOCK, D_ATTN), lambda n: (n, 0)),
        out_shape=jax.ShapeDtypeStruct((SEQ, D_ATTN), f32),
        scratch_shapes=[pltpu.VMEM((D_ATTN, BLOCK), f32)],
        compiler_params=_params("arbitrary"),
    )(proj, proj, proj, proj, proj, q_gain_col, k_gain, sinks, bias_t)


def attn_bwd_t(proj, d_out, q_gain_col, k_gain, sinks, bias_t):
    kcol, vcol = OFF_K // D_KV, OFF_V // D_KV

    def body(q_ref, kc_ref, kp_ref, vc_ref, vp_ref, do_ref, qg_ref, kg_ref, sink_ref, bias_ref,
             dq_ref, dk_ref, dv_ref, dband_ref, dsink_ref, dqg_ref, dkg_ref, dkn_scr, dv_scr, dqt_scr, dsink_acc, dqg_acc):
        i = pl.program_id(0)
        first = i == N_BLOCKS - 1

        @pl.when(i == 0)
        def _():
            for ref in (dband_ref, dkg_ref, dkn_scr, dv_scr, dsink_acc, dqg_acc):
                ref[...] = jnp.zeros_like(ref)

        key_row = lax.broadcasted_iota(jnp.int32, (2 * BLOCK, BLOCK), 0)
        k2 = jnp.concatenate([kp_ref[...], kc_ref[...]], axis=0)
        v2 = jnp.concatenate([vp_ref[...], vc_ref[...]], axis=0)
        q_t = q_ref[...].T
        do_t = do_ref[...].T
        qg = jnp.broadcast_to(qg_ref[...], (HEAD_DIM, BLOCK))
        kg = kg_ref[...]
        scale = HEAD_DIM ** -0.5
        for hk in range(N_KV_HEADS):
            sl = slice(hk * HEAD_DIM, (hk + 1) * HEAD_DIM)
            kk = k2[:, sl]
            rk = _rms(kk)
            khat = kk * rk
            kn = (khat * kg).astype(bf16)
            vb = v2[:, sl].astype(bf16)
            dkn = jnp.zeros((2 * BLOCK, HEAD_DIM), f32)
            dvv = jnp.zeros((2 * BLOCK, HEAD_DIM), f32)
            heads = range(hk * Q_PER_KV, (hk + 1) * Q_PER_KV)
            rqs, qhats, qns, d_os = [], [], [], []
            for h in heads:
                hs = slice(h * HEAD_DIM, (h + 1) * HEAD_DIM)
                qh = q_t[hs, :]
                rqs.append(_rms_t(qh))
                qhats.append(qh * rqs[-1])
                qns.append((qhats[-1] * qg).astype(bf16))
                d_os.append(do_t[hs, :].astype(bf16))
            scores = [_bdot(kn, qn) for qn in qns]
            dps = [_bdot(vb, d_o) for d_o in d_os]
            ps, dss = [], []
            for h, s, dp in zip(heads, scores, dps):
                p, psink = _attn_softmax_t(s, bias_ref[h], sink_ref[h], first, key_row)
                delta = jnp.sum(p * dp, axis=0, keepdims=True)
                ds = p * (dp - delta)
                dband_ref[h] += ds
                dsink_acc[h:h + 1, :] += -(psink * delta)
                ps.append(p.astype(bf16))
                dss.append(ds.astype(bf16))
            dqns = [_bdot_tn(kn, ds) * scale for ds in dss]
            for ds, qn, p, d_o in zip(dss, qns, ps, d_os):
                dkn = dkn + _bdot_nt(ds, qn) * scale
                dvv = dvv + _bdot_nt(p, d_o)
            for h, dqn, rq, qhat in zip(heads, dqns, rqs, qhats):
                dqg_acc[...] += dqn * qhat
                t = dqn * qg
                dqt_scr[h * HEAD_DIM:(h + 1) * HEAD_DIM, :] = rq * (t - qhat * jnp.mean(t * qhat, axis=0, keepdims=True))
            dkn_cur = dkn[BLOCK:] + dkn_scr[:, sl]
            dkn_scr[:, sl] = dkn[:BLOCK]
            khat_c, rk_c = khat[BLOCK:], rk[BLOCK:]
            dkg_ref[...] += jnp.sum(dkn_cur * khat_c, axis=0, keepdims=True)
            dk_ref[:, sl] = _rms_bwd(dkn_cur, khat_c, rk_c, kg)
            dv_ref[:, sl] = dvv[BLOCK:] + dv_scr[:, sl]
            dv_scr[:, sl] = dvv[:BLOCK]
        dq_ref[...] = dqt_scr[...].T

        @pl.when(i == N_BLOCKS - 1)
        def _():
            dsink_ref[...] = jnp.sum(dsink_acc[...], axis=1, keepdims=True)
            dqg_ref[...] = jnp.sum(dqg_acc[...], axis=1, keepdims=True)

    blk = lambda i: N_BLOCKS - 1 - i
    prev = lambda i: jnp.maximum(N_BLOCKS - 2 - i, 0)
    return pl.pallas_call(
        body, name="attn_bwd", grid=(N_BLOCKS,),
        in_specs=[pl.BlockSpec((BLOCK, D_ATTN), lambda i: (blk(i), 0)),
                  pl.BlockSpec((BLOCK, D_KV), lambda i: (blk(i), kcol)), pl.BlockSpec((BLOCK, D_KV), lambda i: (prev(i), kcol)),
                  pl.BlockSpec((BLOCK, D_KV), lambda i: (blk(i), vcol)), pl.BlockSpec((BLOCK, D_KV), lambda i: (prev(i), vcol)),
                  pl.BlockSpec((BLOCK, D_ATTN), lambda i: (blk(i), 0)),
                  _full((HEAD_DIM, 1)), _full((1, HEAD_DIM)), pl.BlockSpec(memory_space=pltpu.SMEM),
                  _full((N_Q_HEADS, 2 * BLOCK, BLOCK))],
        out_specs=[pl.BlockSpec((BLOCK, D_ATTN), lambda i: (blk(i), 0)), pl.BlockSpec((BLOCK, D_KV), lambda i: (blk(i), 0)),
                   pl.BlockSpec((BLOCK, D_KV), lambda i: (blk(i), 0)), _full((N_Q_HEADS, 2 * BLOCK, BLOCK)),
                   _full((N_Q_HEADS, 1)), _full((HEAD_DIM, 1)), _full((1, HEAD_DIM))],
        out_shape=[jax.ShapeDtypeStruct((SEQ, D_ATTN), f32), jax.ShapeDtypeStruct((SEQ, D_KV), f32),
                   jax.ShapeDtypeStruct((SEQ, D_KV), f32), jax.ShapeDtypeStruct((N_Q_HEADS, 2 * BLOCK, BLOCK), f32),
                   jax.ShapeDtypeStruct((N_Q_HEADS, 1), f32), jax.ShapeDtypeStruct((HEAD_DIM, 1), f32),
                   jax.ShapeDtypeStruct((1, HEAD_DIM), f32)],
        scratch_shapes=[pltpu.VMEM((BLOCK, D_KV), f32), pltpu.VMEM((BLOCK, D_KV), f32), pltpu.VMEM((D_ATTN, BLOCK), f32),
                        pltpu.VMEM((N_Q_HEADS, BLOCK), f32), pltpu.VMEM((HEAD_DIM, BLOCK), f32)],
        compiler_params=_params("arbitrary"),
    )(proj, proj, proj, proj, proj, d_out, q_gain_col, k_gain, sinks, bias_t)


def _shift_down(u, s, row):
    if s == 0:
        return u
    return jnp.where(row >= s, pltpu.roll(u, s, 0), 0.0)


def _shift_up(u, s, row):
    if s == 0:
        return u
    return jnp.where(row < SEQ - s, pltpu.roll(u, SEQ - s, 0), 0.0)


def conv_fwd(proj, conv_w, conv_b):
    xcol = OFF_X // LANE

    def body(u_ref, w_ref, b_ref, o_ref):
        u = u_ref[...]
        row = lax.broadcasted_iota(jnp.int32, u.shape, 0)
        pre = b_ref[...] + jnp.zeros_like(u)
        for k in range(CONV_WIDTH):
            pre = pre + w_ref[k:k + 1, :] * _shift_down(u, CONV_WIDTH - 1 - k, row)
        o_ref[...] = pre * _sigmoid(pre)

    return pl.pallas_call(
        body, name="conv_fwd", grid=(D_CONV // LANE,),
        in_specs=[pl.BlockSpec((SEQ, LANE), lambda j: (0, xcol + j)), pl.BlockSpec((CONV_WIDTH, LANE), lambda j: (0, j)),
                  pl.BlockSpec((1, LANE), lambda j: (0, j))],
        out_specs=pl.BlockSpec((SEQ, LANE), lambda j: (0, j)),
        out_shape=jax.ShapeDtypeStruct((SEQ, D_CONV), f32),
        compiler_params=_params("arbitrary"),
    )(proj, conv_w, conv_b)


def conv_bwd(proj, d_act, conv_w, conv_b):
    xcol = OFF_X // LANE

    def body(u_ref, da_ref, w_ref, b_ref, du_ref, dw_ref, db_ref):
        u = u_ref[...]
        row = lax.broadcasted_iota(jnp.int32, u.shape, 0)
        shifted = [_shift_down(u, CONV_WIDTH - 1 - k, row) for k in range(CONV_WIDTH)]
        pre = b_ref[...] + jnp.zeros_like(u)
        for k in range(CONV_WIDTH):
            pre = pre + w_ref[k:k + 1, :] * shifted[k]
        sg = _sigmoid(pre)
        dpre = da_ref[...] * (sg * (1.0 + pre * (1.0 - sg)))
        db_ref[...] = jnp.sum(dpre, axis=0, keepdims=True)
        du = jnp.zeros_like(u)
        for k in range(CONV_WIDTH):
            dw_ref[k:k + 1, :] = jnp.sum(dpre * shifted[k], axis=0, keepdims=True)
            du = du + w_ref[k:k + 1, :] * _shift_up(dpre, CONV_WIDTH - 1 - k, row)
        du_ref[...] = du

    return pl.pallas_call(
        body, name="conv_bwd", grid=(D_CONV // LANE,),
        in_specs=[pl.BlockSpec((SEQ, LANE), lambda j: (0, xcol + j)), pl.BlockSpec((SEQ, LANE), lambda j: (0, j)),
                  pl.BlockSpec((CONV_WIDTH, LANE), lambda j: (0, j)), pl.BlockSpec((1, LANE), lambda j: (0, j))],
        out_specs=[pl.BlockSpec((SEQ, LANE), lambda j: (0, j)), pl.BlockSpec((CONV_WIDTH, LANE), lambda j: (0, j)),
                   pl.BlockSpec((1, LANE), lambda j: (0, j))],
        out_shape=[jax.ShapeDtypeStruct((SEQ, D_CONV), f32), jax.ShapeDtypeStruct((CONV_WIDTH, D_CONV), f32),
                   jax.ShapeDtypeStruct((1, D_CONV), f32)],
        compiler_params=_params("arbitrary"),
    )(proj, d_act, conv_w, conv_b)


def _ssd_chunk_common(dt_raw, dtb, alog):
    row = lax.broadcasted_iota(jnp.int32, (CHUNK, CHUNK), 0)
    col = lax.broadcasted_iota(jnp.int32, (CHUNK, CHUNK), 1)
    tri = (row >= col).astype(f32)
    strict = (row > col).astype(f32)
    dtp = _softplus(dt_raw + dtb)
    a_row = -jnp.exp(alog)
    d_a = dtp * a_row
    cs = _hdot(tri, d_a)
    cs_last = cs[CHUNK - 1:CHUNK, :]
    return row, col, dtp, a_row, cs, cs.T, cs_last


def _seg_decay(cs, cs_t, hd, row, col):
    seg = cs[:, hd:hd + 1] - cs_t[hd:hd + 1, :]
    return jnp.where(row >= col, jnp.exp(seg), 0.0)


GROUP_W = HEADS_PER_GROUP * SSM_HEAD_DIM


def _group_indicator(g):
    j = lax.broadcasted_iota(jnp.int32, (GROUP_W, LANE), 0)
    lane = lax.broadcasted_iota(jnp.int32, (GROUP_W, LANE), 1)
    return (lane == g * HEADS_PER_GROUP + j // SSM_HEAD_DIM).astype(bf16)


def _bf16_pieces(a, n):
    pieces = []
    for _ in range(n):
        p = a.astype(bf16)
        pieces.append(p)
        a = a - p.astype(f32)
    return pieces


def _head_spread(a, ind):
    return sum(lax.dot_general(p, ind, (((1,), (1,)), ((), ())), preferred_element_type=f32) for p in _bf16_pieces(a, 3))


def _head_sums(a, ind):
    return sum(jnp.dot(p, ind, preferred_element_type=f32) for p in _bf16_pieces(a, 2))


def ssd_fwd_g(act, proj, dt_bias, a_log, d_skip, norm_g):
    zcol, dtcol = OFF_Z // D_SSM, OFF_DT // LANE

    def body(act_ref, z_ref, dt_ref, dtb_ref, alog_ref, dsk_ref, ng_ref, out_ref, ypre_ref, st_ref, state):
        c = pl.program_id(0)

        @pl.when(c == 0)
        def _():
            state[...] = jnp.zeros_like(state)

        row, col, dtp, a_row, cs, cs_t, cs_last = _ssd_chunk_common(dt_ref[...], dtb_ref[...], alog_ref[...])
        e_cs = jnp.exp(cs)
        dte = jnp.exp(cs_last - cs)
        rows8 = jnp.concatenate([jnp.exp(cs_last), dsk_ref[...], jnp.zeros((6, LANE), f32)], axis=0)
        z = z_ref[...]
        sz = z * _sigmoid(z)
        ng = ng_ref[...]
        for g in range(SSM_GROUPS):
            gs = slice(g * GROUP_W, (g + 1) * GROUP_W)
            ind = _group_indicator(g)
            xg = act_ref[:, gs]
            bg = act_ref[:, D_SSM + g * SSM_STATE:D_SSM + (g + 1) * SSM_STATE]
            cg = act_ref[:, D_SSM + D_BC + g * SSM_STATE:D_SSM + D_BC + (g + 1) * SSM_STATE]
            dt_e, e_e, dte_e = _head_spread(dtp, ind), _head_spread(e_cs, ind), _head_spread(dte, ind)
            rows_e = _head_spread(rows8, ind)
            ecl_e, dsk_e = rows_e[0:1], rows_e[1:2]
            xdt = xg * dt_e
            prev = state[g]
            st_ref[0, g] = prev
            cb = _bdot_nt(cg, bg)
            goff = _bdot(cg, prev)
            snew = _bdot_tn(bg, xdt * dte_e)
            heads = range(g * HEADS_PER_GROUP, (g + 1) * HEADS_PER_GROUP)
            ms = [cb * _seg_decay(cs, cs_t, hd, row, col) for hd in heads]
            yd = [_bdot(m, xdt[:, r * SSM_HEAD_DIM:(r + 1) * SSM_HEAD_DIM]) for r, m in enumerate(ms)]
            y = jnp.concatenate(yd, axis=1) + e_e * goff + xg * dsk_e
            state[g] = prev * ecl_e + snew
            ypre_ref[:, gs] = y
            part = y * sz[:, gs]
            out_ref[:, gs] = part * _rms(part) * ng[:, gs]

    return pl.pallas_call(
        body, name="ssd_fwd", grid=(N_CHUNKS,),
        in_specs=[pl.BlockSpec((CHUNK, D_CONV), lambda c: (c, 0)), pl.BlockSpec((CHUNK, D_SSM), lambda c: (c, zcol)),
                  pl.BlockSpec((CHUNK, LANE), lambda c: (c, dtcol)), _full((1, LANE)), _full((1, LANE)), _full((1, LANE)),
                  _full((1, D_SSM))],
        out_specs=[pl.BlockSpec((CHUNK, D_SSM), lambda c: (c, 0)), pl.BlockSpec((CHUNK, D_SSM), lambda c: (c, 0)),
                   pl.BlockSpec((1, SSM_GROUPS, SSM_STATE, GROUP_W), lambda c: (c, 0, 0, 0))],
        out_shape=[jax.ShapeDtypeStruct((SEQ, D_SSM), f32), jax.ShapeDtypeStruct((SEQ, D_SSM), f32),
                   jax.ShapeDtypeStruct((N_CHUNKS, SSM_GROUPS, SSM_STATE, GROUP_W), f32)],
        scratch_shapes=[pltpu.VMEM((SSM_GROUPS, SSM_STATE, GROUP_W), f32)],
        compiler_params=_params("arbitrary"),
    )(act, proj, proj, dt_bias, a_log, d_skip, norm_g)


def ssd_bwd_g(act, proj, ypre, states, d_out, dt_bias, a_log, d_skip, norm_g):
    zcol, dtcol = OFF_Z // D_SSM, OFF_DT // LANE

    def body(act_ref, z_ref, dt_ref, ypre_ref, st_ref, do_ref, dtb_ref, alog_ref, dsk_ref, ng_ref,
             dact_ref, ddt_ref, dz_ref, dng_ref, dpar_ref, dstate):
        i = pl.program_id(0)

        @pl.when(i == 0)
        def _():
            for ref in (dng_ref, dpar_ref, dstate):
                ref[...] = jnp.zeros_like(ref)

        row, col, dtp, a_row, cs, cs_t, cs_last = _ssd_chunk_common(dt_ref[...], dtb_ref[...], alog_ref[...])
        upper = (row <= col).astype(f32)
        lane = lax.broadcasted_iota(jnp.int32, (CHUNK, LANE), 1)
        rowl = lax.broadcasted_iota(jnp.int32, (CHUNK, LANE), 0)
        e_cs = jnp.exp(cs)
        dte = jnp.exp(cs_last - cs)
        ecl = jnp.exp(cs_last)
        rows8 = jnp.concatenate([ecl, dsk_ref[...], jnp.zeros((6, LANE), f32)], axis=0)
        z = z_ref[...]
        sgz = _sigmoid(z)
        sz = z * sgz
        ng = ng_ref[...]
        ddt_mat = jnp.zeros((CHUNK, LANE), f32)
        dcs_mat = jnp.zeros((CHUNK, LANE), f32)
        dcs_t = jnp.zeros((LANE, CHUNK), f32)
        dcsl_row = jnp.zeros((1, LANE), f32)
        dd_row = jnp.zeros((1, LANE), f32)
        for g in range(SSM_GROUPS):
            gs = slice(g * GROUP_W, (g + 1) * GROUP_W)
            bsl = slice(D_SSM + g * SSM_STATE, D_SSM + (g + 1) * SSM_STATE)
            csl = slice(D_SSM + D_BC + g * SSM_STATE, D_SSM + D_BC + (g + 1) * SSM_STATE)
            ind = _group_indicator(g)
            y = ypre_ref[:, gs]
            part = y * sz[:, gs]
            r = _rms(part)
            yhat = part * r
            d_o = do_ref[:, gs]
            dng_ref[:, gs] += jnp.sum(d_o * yhat, axis=0, keepdims=True)
            dyz = _rms_bwd(d_o, yhat, r, ng[:, gs])
            dy = dyz * sz[:, gs]
            dz_ref[:, gs] = dyz * y * (sgz[:, gs] * (1.0 + z[:, gs] * (1.0 - sgz[:, gs])))

            xg = act_ref[:, gs]
            bg = act_ref[:, bsl]
            cg = act_ref[:, csl]
            dt_e, e_e, dte_e = _head_spread(dtp, ind), _head_spread(e_cs, ind), _head_spread(dte, ind)
            rows_e = _head_spread(rows8, ind)
            ecl_e, dsk_e = rows_e[0:1], rows_e[1:2]
            xdt = xg * dt_e
            prev = st_ref[0, g]
            dh = dstate[g]
            heads = range(g * HEADS_PER_GROUP, (g + 1) * HEADS_PER_GROUP)
            hsl = [slice(r_ * SSM_HEAD_DIM, (r_ + 1) * SSM_HEAD_DIM) for r_ in range(HEADS_PER_GROUP)]
            cb = _bdot_nt(cg, bg)
            lms = [_seg_decay(cs, cs_t, hd, row, col) for hd in heads]
            ms = [cb * lm for lm in lms]
            gmat = _bdot(cg, prev)
            dgm = dy * e_e
            dcg = _bdot_nt(dgm, prev)
            dprev = _bdot_tn(cg, dgm)
            dbg = _bdot_nt(xdt * dte_e, dh)
            dw = _bdot(bg, dh)
            dms = [_bdot_nt(dy[:, s_], xdt[:, s_]) for s_ in hsl]
            dxdts = [_bdot_tn(m, dy[:, s_]) for m, s_ in zip(ms, hsl)]
            dxdt = jnp.concatenate(dxdts, axis=1) + dw * dte_e
            dact_ref[:, gs] = dy * dsk_e + dxdt * dt_e
            dstate[g] = dprev + dh * ecl_e
            dcb = jnp.zeros((CHUNK, CHUNK), f32)
            for hd, dm, lm, m in zip(heads, dms, lms, ms):
                dcb = dcb + dm * lm
                dseg = dm * m
                dcs_mat = dcs_mat + jnp.where(lane == hd, jnp.sum(dseg, axis=1, keepdims=True), 0.0)
                dcs_t = jnp.where(row == hd, jnp.sum(dseg, axis=0, keepdims=True), dcs_t)
            dact_ref[:, bsl] = dbg + _bdot_tn(dcb, cg)
            dact_ref[:, csl] = dcg + _bdot(dcb, bg)
            ddte = _head_sums(dw * xdt, ind) * dte
            dcs_mat = dcs_mat + _head_sums(dy * gmat, ind) * e_cs - ddte
            ddt_mat = ddt_mat + _head_sums(dxdt * xg, ind)
            dcsl_row = (dcsl_row + jnp.sum(ddte, axis=0, keepdims=True)
                        + jnp.sum(_head_sums(dh * prev, ind), axis=0, keepdims=True) * ecl)
            dd_row = dd_row + jnp.sum(_head_sums(dy * xg, ind), axis=0, keepdims=True)
        dcs_mat = dcs_mat - dcs_t.T + jnp.where(rowl == CHUNK - 1, dcsl_row, 0.0)
        dda = _hdot(upper, dcs_mat)
        ddt_mat = ddt_mat + dda * a_row
        da_row = jnp.sum(dda * dtp, axis=0, keepdims=True)
        ddt_raw = ddt_mat * _sigmoid(dt_ref[...] + dtb_ref[...])
        ddt_ref[...] = ddt_raw
        dpar_ref[0:1, :] += jnp.sum(ddt_raw, axis=0, keepdims=True)
        dpar_ref[1:2, :] += da_row * a_row
        dpar_ref[2:3, :] += dd_row

    blk = lambda i: N_CHUNKS - 1 - i
    return pl.pallas_call(
        body, name="ssd_bwd", grid=(N_CHUNKS,),
        in_specs=[pl.BlockSpec((CHUNK, D_CONV), lambda i: (blk(i), 0)), pl.BlockSpec((CHUNK, D_SSM), lambda i: (blk(i), zcol)),
                  pl.BlockSpec((CHUNK, LANE), lambda i: (blk(i), dtcol)), pl.BlockSpec((CHUNK, D_SSM), lambda i: (blk(i), 0)),
                  pl.BlockSpec((1, SSM_GROUPS, SSM_STATE, GROUP_W), lambda i: (blk(i), 0, 0, 0)),
                  pl.BlockSpec((CHUNK, D_SSM), lambda i: (blk(i), 0)),
                  _full((1, LANE)), _full((1, LANE)), _full((1, LANE)), _full((1, D_SSM))],
        out_specs=[pl.BlockSpec((CHUNK, D_CONV), lambda i: (blk(i), 0)), pl.BlockSpec((CHUNK, LANE), lambda i: (blk(i), 0)),
                   pl.BlockSpec((CHUNK, D_SSM), lambda i: (blk(i), 0)), _full((1, D_SSM)), _full((8, LANE))],
        out_shape=[jax.ShapeDtypeStruct((SEQ, D_CONV), f32), jax.ShapeDtypeStruct((SEQ, LANE), f32),
                   jax.ShapeDtypeStruct((SEQ, D_SSM), f32), jax.ShapeDtypeStruct((1, D_SSM), f32),
                   jax.ShapeDtypeStruct((8, LANE), f32)],
        scratch_shapes=[pltpu.VMEM((SSM_GROUPS, SSM_STATE, GROUP_W), f32)],
        compiler_params=_params("arbitrary"),
    )(act, proj, proj, ypre, states, d_out, dt_bias, a_log, d_skip, norm_g)


def out_fwd(x, attn, ssm, w_out, tm=512):
    def body(x_ref, a_ref, s_ref, w_ref, o_ref):
        o_ref[...] = x_ref[...] + _bdot(a_ref[...], w_ref[:D_ATTN, :]) + _bdot(s_ref[...], w_ref[D_ATTN:, :])

    tok = lambda w_: pl.BlockSpec((tm, w_), lambda i: (i, 0))
    return pl.pallas_call(
        body, name="out_fwd", grid=(SEQ // tm,),
        in_specs=[tok(D_MODEL), tok(D_ATTN), tok(D_SSM), _full((D_MODEL, D_MODEL))],
        out_specs=tok(D_MODEL), out_shape=jax.ShapeDtypeStruct((SEQ, D_MODEL), f32),
        compiler_params=_params("arbitrary"),
    )(x, attn, ssm, w_out)


def out_bwd(dx1, attn, ssm, w_out, tm=512):
    nt = SEQ // tm

    def body(d_ref, a_ref, s_ref, w_ref, da_ref, ds_ref, dw_ref, dw16_ref):
        i = pl.program_id(0)

        @pl.when(i == 0)
        def _():
            dw_ref[...] = jnp.zeros_like(dw_ref)

        d = d_ref[...].astype(bf16)
        dcat = _bdot_nt(d, w_ref[...])
        da_ref[...] = dcat[:, :D_ATTN]
        ds_ref[...] = dcat[:, D_ATTN:]
        dw_ref[:D_ATTN, :] += _bdot_tn(a_ref[...], d)
        dw_ref[D_ATTN:, :] += _bdot_tn(s_ref[...], d)

        @pl.when(i == nt - 1)
        def _():
            dw16_ref[...] = dw_ref[...].astype(bf16)

    tok = lambda w_: pl.BlockSpec((tm, w_), lambda i: (i, 0))
    return pl.pallas_call(
        body, name="out_bwd", grid=(nt,),
        in_specs=[tok(D_MODEL), tok(D_ATTN), tok(D_SSM), _full((D_MODEL, D_MODEL))],
        out_specs=[tok(D_ATTN), tok(D_SSM), _full((D_MODEL, D_MODEL)), _full((D_MODEL, D_MODEL))],
        out_shape=[jax.ShapeDtypeStruct((SEQ, D_ATTN), f32), jax.ShapeDtypeStruct((SEQ, D_SSM), f32),
                   jax.ShapeDtypeStruct((D_MODEL, D_MODEL), f32), jax.ShapeDtypeStruct((D_MODEL, D_MODEL), bf16)],
        compiler_params=_params("arbitrary"),
    )(dx1, attn, ssm, w_out)


MLP_SUB = 256


def mlp_fwd(x1, g, w_up, w_down, tm=1024):
    def body(x_ref, g_ref, wu_ref, wd_ref, o_ref, u_ref, h_scr):
        j = pl.program_id(1)

        @pl.when(j == 0)
        def _():
            xv = x_ref[...]
            h_scr[...] = (xv * _rms(xv) * g_ref[...]).astype(bf16)
            o_ref[...] = xv

        for r in range(tm // MLP_SUB):
            rows = slice(r * MLP_SUB, (r + 1) * MLP_SUB)
            u = jnp.dot(h_scr[rows, :], wu_ref[...], preferred_element_type=f32)
            u_ref[rows, :] = u
            a = jnp.square(jnp.maximum(u, 0.0))
            o_ref[rows, :] += _bdot(a, wd_ref[...])

    return pl.pallas_call(
        body, name="mlp_fwd", grid=(SEQ // tm, N_CHIPS),
        in_specs=[pl.BlockSpec((tm, D_MODEL), lambda i, j: (i, 0)), _full((1, D_MODEL)),
                  pl.BlockSpec((None, D_MODEL, FF_TILE), lambda i, j: (j, 0, 0)),
                  pl.BlockSpec((None, FF_TILE, D_MODEL), lambda i, j: (j, 0, 0))],
        out_specs=[pl.BlockSpec((tm, D_MODEL), lambda i, j: (i, 0)), pl.BlockSpec((tm, FF_TILE), lambda i, j: (i, j))],
        out_shape=[jax.ShapeDtypeStruct((SEQ, D_MODEL), f32), jax.ShapeDtypeStruct((SEQ, D_FF), f32)],
        scratch_shapes=[pltpu.VMEM((tm, D_MODEL), bf16)],
        compiler_params=_params("arbitrary", "arbitrary"),
    )(x1, g, w_up, w_down)


def mlp_bwd_data(dx2, u, x1, g, w_up, w_down, tm=1024):
    def body(d_ref, u_ref, x_ref, g_ref, wu_ref, wd_ref, dx_ref, du_ref, dg_ref, dh_scr):
        i, j = pl.program_id(0), pl.program_id(1)

        @pl.when(jnp.logical_and(i == 0, j == 0))
        def _():
            dg_ref[...] = jnp.zeros_like(dg_ref)

        @pl.when(j == 0)
        def _():
            dh_scr[...] = jnp.zeros_like(dh_scr)

        for r in range(tm // MLP_SUB):
            rows = slice(r * MLP_SUB, (r + 1) * MLP_SUB)
            da = _bdot_nt(d_ref[rows, :], wd_ref[...])
            du = (da * (2.0 * jnp.maximum(u_ref[rows, :], 0.0))).astype(bf16)
            du_ref[rows, :] = du
            dh_scr[rows, :] += _bdot_nt(du, wu_ref[...])

        @pl.when(j == N_CHIPS - 1)
        def _():
            xv = x_ref[...]
            r = _rms(xv)
            xhat = xv * r
            dh = dh_scr[...]
            dg_ref[...] += jnp.sum(dh * xhat, axis=0, keepdims=True)
            dx_ref[...] = d_ref[...] + _rms_bwd(dh, xhat, r, g_ref[...])

    return pl.pallas_call(
        body, name="mlp_bwd_data", grid=(SEQ // tm, N_CHIPS),
        in_specs=[pl.BlockSpec((tm, D_MODEL), lambda i, j: (i, 0)), pl.BlockSpec((tm, FF_TILE), lambda i, j: (i, j)),
                  pl.BlockSpec((tm, D_MODEL), lambda i, j: (i, 0)), _full((1, D_MODEL)),
                  pl.BlockSpec((None, D_MODEL, FF_TILE), lambda i, j: (j, 0, 0)),
                  pl.BlockSpec((None, FF_TILE, D_MODEL), lambda i, j: (j, 0, 0))],
        out_specs=[pl.BlockSpec((tm, D_MODEL), lambda i, j: (i, 0)), pl.BlockSpec((tm, FF_TILE), lambda i, j: (i, j)),
                   _full((1, D_MODEL))],
        out_shape=[jax.ShapeDtypeStruct((SEQ, D_MODEL), f32), jax.ShapeDtypeStruct((SEQ, D_FF), bf16),
                   jax.ShapeDtypeStruct((1, D_MODEL), f32)],
        scratch_shapes=[pltpu.VMEM((tm, D_MODEL), f32)],
        compiler_params=_params("arbitrary", "arbitrary"),
    )(dx2, u, x1, g, w_up, w_down)


def mlp_bwd_weights(dx2, u, du, x1, g, tm=512):
    nt = SEQ // tm

    def body(d_ref, u_ref, du_ref, x_ref, g_ref, dwu_ref, dwd_ref, dwu16_ref, dwd16_ref, h_scr, d_scr):
        j, i = pl.program_id(0), pl.program_id(1)

        @pl.when(j == 0)
        def _():
            xv = x_ref[...]
            h_scr[i] = (xv * _rms(xv) * g_ref[...]).astype(bf16)
            d_scr[i] = d_ref[...].astype(bf16)

        @pl.when(i == 0)
        def _():
            dwu_ref[...] = jnp.zeros_like(dwu_ref)
            dwd_ref[...] = jnp.zeros_like(dwd_ref)

        dwu_ref[...] += _bdot_tn(h_scr[i], du_ref[...])
        a = jnp.square(jnp.maximum(u_ref[...], 0.0))
        dwd_ref[...] += _bdot_tn(a, d_scr[i])

        @pl.when(i == nt - 1)
        def _():
            dwu16_ref[...] = dwu_ref[...].astype(bf16)
            dwd16_ref[...] = dwd_ref[...].astype(bf16)

    up = pl.BlockSpec((None, D_MODEL, FF_TILE), lambda j, i: (j, 0, 0))
    down = pl.BlockSpec((None, FF_TILE, D_MODEL), lambda j, i: (j, 0, 0))
    first_pass = pl.BlockSpec((tm, D_MODEL), lambda j, i: (jnp.where(j == 0, i, nt - 1), 0))
    return pl.pallas_call(
        body, name="mlp_bwd_weights", grid=(N_CHIPS, nt),
        in_specs=[first_pass, pl.BlockSpec((tm, FF_TILE), lambda j, i: (i, j)),
                  pl.BlockSpec((tm, FF_TILE), lambda j, i: (i, j)), first_pass, _full((1, D_MODEL))],
        out_specs=[up, down, up, down],
        out_shape=[jax.ShapeDtypeStruct((N_CHIPS, D_MODEL, FF_TILE), f32), jax.ShapeDtypeStruct((N_CHIPS, FF_TILE, D_MODEL), f32),
                   jax.ShapeDtypeStruct((N_CHIPS, D_MODEL, FF_TILE), bf16), jax.ShapeDtypeStruct((N_CHIPS, FF_TILE, D_MODEL), bf16)],
        scratch_shapes=[pltpu.VMEM((nt, tm, D_MODEL), bf16), pltpu.VMEM((nt, tm, D_MODEL), bf16)],
        compiler_params=_params("arbitrary", "arbitrary"),
    )(dx2, u, du, x1, g)


def loss_head(y, target, tm=512):
    def body(y_ref, t_ref, dy_ref, l_ref):
        @pl.when(pl.program_id(0) == 0)
        def _():
            l_ref[...] = jnp.zeros_like(l_ref)

        d = y_ref[...] - t_ref[...]
        dy_ref[...] = d * (1.0 / D_MODEL)
        part = jnp.sum(jnp.mean(d * d, axis=-1, keepdims=True), axis=0, keepdims=True)
        l_ref[...] += 0.5 * part

    tok = pl.BlockSpec((tm, D_MODEL), lambda i: (i, 0))
    return pl.pallas_call(
        body, name="loss_head", grid=(SEQ // tm,), in_specs=[tok, tok], out_specs=[tok, _full((1, 1))],
        out_shape=[jax.ShapeDtypeStruct((SEQ, D_MODEL), f32), jax.ShapeDtypeStruct((1, 1), f32)],
        compiler_params=_params("arbitrary"),
    )(y, target)


def _pad_lane(v):
    return jnp.pad(v, (0, LANE - v.shape[0]))[None, :]


def local_step(x, target, w, prov):
    bucket = jnp.asarray(_bucket_table().T)
    bias = bias_build(w["rel_bias"], bucket)
    saved = []
    for l in range(DEPTH):
        g_mix = w["mix_norm_g"][l][None, :] + prov.stage(("begin", l), x)
        w_in = prov.w_in(l, x)
        proj = in_fwd(x, g_mix, w_in)
        conv_b = w["conv_b"][l][None, :]
        act = conv_fwd(proj, w["conv_w"][l], conv_b)
        dtb = _pad_lane(w["dt_bias"][l]) + prov.stage(("mid", l), act)
        alog, dsk = _pad_lane(w["a_log"][l]), _pad_lane(w["d_skip"][l])
        ng = w["ssm_norm_g"][l][None, :]
        ssm, ypre, states = ssd_fwd_g(act, proj, dtb, alog, dsk, ng)
        qg, kg = w["q_gain"][l][:, None] + 0.0 * ssm[:1, :1], w["k_gain"][l][None, :]
        attn = attn_fwd_t(proj, qg, kg, w["sinks"][l], bias)
        tok = prov.stage(("pre_out", l), attn)
        w_out = prov.w_out(l, attn) + jnp.asarray(tok, bf16)
        x1 = out_fwd(x, attn, ssm, w_out)
        g_mlp = w["mlp_norm_g"][l][None, :] + prov.stage(("pre_mlp", l), x1)
        w_up, w_down = prov.mlp(l, x1)
        x2, u = mlp_fwd(x1, g_mlp, w_up, w_down)
        saved.append(dict(x=x, proj=proj, attn=attn, act=act, ssm=ssm, ypre=ypre, states=states, x1=x1, u=u,
                          g_mix=g_mix, qg=qg, kg=kg, conv_b=conv_b, dtb=dtb, alog=alog, dsk=dsk, ng=ng, g_mlp=g_mlp,
                          w_in=w_in, w_out=w_out, w_up=w_up, w_down=w_down))
        x = x2
    dx, loss = loss_head(x, target)
    grads = [None] * DEPTH
    dbands = [None] * DEPTH
    tok = 0.0
    for l in reversed(range(DEPTH)):
        s = saved[l]
        g_mlp = s["g_mlp"] + tok
        dx1, du, dg_mlp = mlp_bwd_data(dx, s["u"], s["x1"], g_mlp, s["w_up"], s["w_down"])
        dw_up, dw_down, dw_up16, dw_down16 = mlp_bwd_weights(dx, s["u"], du, s["x1"], g_mlp)
        tok = prov.grads(("mlp", l), dict(w_up=(dw_up, dw_up16), w_down=(dw_down, dw_down16)), dw_down)
        dattn, dssm, dw_out, dw_out16 = out_bwd(dx1, s["attn"], s["ssm"], s["w_out"])
        dact, ddt, dz, dng, dpar = ssd_bwd_g(s["act"], s["proj"], s["ypre"], s["states"], dssm, s["dtb"] + tok, s["alog"],
                                           s["dsk"], s["ng"])
        conv_b = s["conv_b"] + prov.stage(("bwd_mid", l), dact)
        dxbc, dconv_w, dconv_b = conv_bwd(s["proj"], dact, w["conv_w"][l], conv_b)
        dq, dk, dv, dband, dsink, dqg, dkg = attn_bwd_t(s["proj"], dattn, s["qg"], s["kg"], w["sinks"][l], bias)
        dbands[l] = dband
        g_mix = s["g_mix"]
        if l == 0:
            d_rel = bias_bwd(dbands[0], dbands[1], bucket)
            g_mix = g_mix + 0.0 * d_rel[:1, :1]
        dx, dw_in, dg_mix = in_bwd(dq, dz, dxbc, dk, dv, ddt, s["x"], g_mix, s["w_in"], dx1)
        tok = prov.grads(("mix", l), dict(w_in=split_w_in_grad(dw_in), w_out=(dw_out, dw_out16)), dx)
        grads[l] = dict(mix_norm_g=dg_mix[0], q_gain=dqg[:, 0], k_gain=dkg[0], sinks=dsink[:, 0],
                        conv_w=dconv_w, conv_b=dconv_b[0], dt_bias=dpar[0, :SSM_HEADS], a_log=dpar[1, :SSM_HEADS],
                        d_skip=dpar[2, :SSM_HEADS], ssm_norm_g=dng[0], mlp_norm_g=dg_mlp[0])
    out = {k: jnp.stack([grads[l][k] for l in range(DEPTH)]) for k in grads[0]}
    out["rel_bias"] = d_rel[:, :N_Q_HEADS]
    return loss, dx, out, tok


MESH = pl.DeviceIdType.MESH
HBM = pl.BlockSpec(memory_space=pltpu.HBM)
N_PEER_CHIPS = N_CHIPS - 1
N_DEVICES = 8


def _coords():
    return lax.axis_index("x"), lax.axis_index("y"), lax.axis_index("c")


def _peer_chips(x, y):
    return [(1 - x, y), (x, 1 - y), (1 - x, 1 - y)]


def _remote(src, dst, send_sem, recv_sem, device):
    return pltpu.make_async_remote_copy(src_ref=src, dst_ref=dst, send_sem=send_sem, recv_sem=recv_sem,
                                        device_id=device, device_id_type=MESH)


SEM = pl.BlockSpec(memory_space=pltpu.SEMAPHORE)
ANY = pl.BlockSpec(memory_space=pl.ANY)
DATAFLOW = pltpu.SideEffectType.DATAFLOW_SIDE_EFFECTING


def _gather_copies(kind, src_refs, land_refs, ssem, rsem):
    x, y, c = _coords()
    k_me = 2 * x + y
    n = len(land_refs)
    cps = []
    for p, land in enumerate(land_refs):
        hr = land.shape[1] // 2
        rows = pl.ds(c * hr, hr)
        for j, chip in enumerate(_peer_chips(x, y)):
            i = 3 * p + j
            if kind == "ici":
                cps.append(_remote(src_refs[p].at[rows, :], land.at[k_me, rows, :], ssem.at[i], rsem.at[i], (*chip, c)))
            else:
                got = land.at[2 * chip[0] + chip[1], rows, :]
                cps.append(_remote(got, got, ssem.at[i], rsem.at[i], (x, y, 1 - c)))
        if kind == "relay":
            cps.append(_remote(src_refs[p], land.at[k_me], ssem.at[3 * n + p], rsem.at[3 * n + p], (x, y, 1 - c)))
    return cps


def gather_now(srcs, conv):
    n = len(srcs)

    def body(*refs):
        src_refs, conv_ref = refs[:n], refs[n]
        lands, gconv = refs[n + 1:2 * n + 1], refs[2 * n + 1]
        ssem, rsem, fsem, frsem, csem, crsem = refs[2 * n + 2:]
        x, y, c = _coords()
        k_me = 2 * x + y
        targets = [(*chip, c) for chip in _peer_chips(x, y)] + [(x, y, 1 - c)]
        ici = _gather_copies("ici", src_refs, lands, ssem, rsem)
        relay = _gather_copies("relay", src_refs, lands, fsem, frsem)
        passed = [cp for i, cp in enumerate(relay) if i % 4 != 3]
        own = relay[3::4]
        conv_cps = [_remote(conv_ref, gconv.at[k_me], csem.at[j], crsem.at[j], t) for j, t in enumerate(targets)]
        for cp in ici + conv_cps + own:
            cp.start()
        for cp, fw in zip(ici, passed):
            cp.wait_recv()
            fw.start()
        for cp in conv_cps + relay:
            cp.wait_recv()
        for cp in ici + relay + conv_cps:
            cp.wait_send()

    out_shape = [jax.ShapeDtypeStruct((N_CHIPS,) + s.shape, s.dtype) for s in srcs]
    out_shape.append(jax.ShapeDtypeStruct((N_CHIPS,) + conv.shape, conv.dtype))
    sems = lambda k: pltpu.SemaphoreType.DMA((k,))
    return pl.pallas_call(
        body, name="gather_now", out_shape=out_shape, in_specs=[HBM] * (n + 1), out_specs=[HBM] * (n + 1),
        scratch_shapes=[sems(3 * n), sems(3 * n), sems(4 * n), sems(4 * n), sems(N_CHIPS), sems(N_CHIPS)],
    )(*srcs, conv)


def _gather_maker(kind, n_src):
    def make(refs, ssem, rsem):
        cps = _gather_copies(kind, refs[:n_src], refs[n_src:], ssem, rsem)
        return cps, cps
    return make


def _scatter_maker(n):
    def make(refs, ssem, rsem):
        x, y, c = _coords()
        k_me = 2 * x + y
        sends, arrivals = [], []
        for p in range(n):
            src, land = refs[p], refs[n + p]
            sends.append(_remote(src.at[k_me, 1 - c], land.at[0], ssem.at[7 * p], rsem.at[7 * p], (x, y, 1 - c)))
            for j, chip in enumerate(_peer_chips(x, y)):
                for cc in range(2):
                    sends.append(_remote(src.at[2 * chip[0] + chip[1], cc], land.at[1 + 2 * j + c],
                                         ssem.at[7 * p + 1 + 2 * j + cc], rsem.at[7 * p + 1 + 2 * j + c], (*chip, cc)))
            for s in range(7):
                arrivals.append(_remote(land.at[s], land.at[s], ssem.at[7 * p + s], rsem.at[7 * p + s], (x, y, 1 - c)))
        return sends, arrivals
    return make


def _share_maker(n):
    def make(refs, ssem, rsem):
        x, y, c = _coords()
        sends = [_remote(refs[p].at[c], refs[p].at[c], ssem.at[p], rsem.at[p], (x, y, 1 - c)) for p in range(n)]
        arrivals = [_remote(refs[p].at[1 - c], refs[p].at[1 - c], ssem.at[p], rsem.at[p], (x, y, 1 - c)) for p in range(n)]
        return sends, arrivals
    return make


def split_start(name, make, n_sems, operands, after):
    n = len(operands)

    def body(*refs):
        ssem, rsem, token = refs[n + 1], refs[n + 2], refs[-1]
        for cp in make(refs[:n], ssem, rsem)[0]:
            cp.start()
        token[...] = jnp.zeros_like(token)

    ops = [pltpu.with_memory_space_constraint(a, pltpu.HBM) for a in operands]
    outs = pl.pallas_call(
        body, name=name,
        out_shape=(pltpu.SemaphoreType.DMA((n_sems,)), pltpu.SemaphoreType.DMA((n_sems,)),
                   *[pltpu.HBM(a.shape, a.dtype) for a in ops], jax.ShapeDtypeStruct((8, LANE), f32)),
        in_specs=[HBM] * n + [ANY], out_specs=(SEM, SEM, *[HBM] * n, pl.BlockSpec(memory_space=pltpu.VMEM)),
        input_output_aliases={i: 2 + i for i in range(n)},
        compiler_params=pltpu.CompilerParams(has_side_effects=DATAFLOW),
    )(*ops, after)
    return dict(name=name, make=make, ssem=outs[0], rsem=outs[1], operands=outs[2:2 + n], token=outs[-1][0, 0])


def split_wait(handle, after):
    n = len(handle["operands"])

    def body(*refs):
        sends, arrivals = handle["make"](refs[:n], refs[n], refs[n + 1])
        for cp in sends:
            cp.wait_send()
        for cp in arrivals:
            cp.wait_recv()

    outs = pl.pallas_call(
        body, name=handle["name"].replace("start", "wait"),
        out_shape=tuple(pltpu.HBM(a.shape, a.dtype) for a in handle["operands"]),
        in_specs=[HBM] * n + [SEM, SEM, ANY], out_specs=tuple([HBM] * n),
        input_output_aliases={i: i for i in range(n)},
        compiler_params=pltpu.CompilerParams(has_side_effects=DATAFLOW),
    )(*handle["operands"], handle["ssem"], handle["rsem"], after)
    return list(outs)


def piece_sum(g, recv, kc_arr):
    _, _, rb, cc = g.shape
    tr = min(256, rb)

    def body(kc_ref, g_ref, r_ref, o_ref):
        acc = g_ref[...]
        for s in range(7):
            acc = acc + r_ref[s].astype(f32)
        o_ref[...] = acc

    return pl.pallas_call(
        body, name="piece_sum",
        grid_spec=pltpu.PrefetchScalarGridSpec(
            num_scalar_prefetch=1, grid=(rb // tr,),
            in_specs=[pl.BlockSpec((None, None, tr, cc), lambda r, kc: (kc[0], kc[1], r, 0)),
                      pl.BlockSpec((7, tr, cc), lambda r, kc: (0, r, 0))],
            out_specs=pl.BlockSpec((None, tr, cc), lambda r, kc: (kc[1], r, 0))),
        out_shape=jax.ShapeDtypeStruct((2, rb, cc), f32),
        compiler_params=_params("arbitrary"),
    )(kc_arr, g, recv)


def small_all_reduce(vec):
    def body(v_ref, o_ref, gat, ssem, rsem):
        x, y, c = _coords()
        me = 4 * x + 2 * y + c
        gat[me] = v_ref[...]
        sends = []
        for t in range(1, N_DEVICES):
            peer = (x ^ (t >> 2), y ^ ((t >> 1) & 1), c ^ (t & 1))
            cp = _remote(v_ref, gat.at[me], ssem.at[t - 1], rsem.at[t - 1], peer)
            cp.start()
            sends.append(cp)
        for t in range(1, N_DEVICES):
            peer = (x ^ (t >> 2), y ^ ((t >> 1) & 1), c ^ (t & 1))
            slot = gat.at[4 * peer[0] + 2 * peer[1] + peer[2]]
            _remote(slot, slot, ssem.at[t - 1], rsem.at[t - 1], peer).wait_recv()
        for cp in sends:
            cp.wait_send()
        acc = gat[0]
        for d in range(1, N_DEVICES):
            acc = acc + gat[d]
        o_ref[...] = acc

    return pl.pallas_call(
        body, name="small_all_reduce", out_shape=jax.ShapeDtypeStruct(vec.shape, vec.dtype),
        in_specs=[pl.BlockSpec(memory_space=pltpu.VMEM)], out_specs=pl.BlockSpec(memory_space=pltpu.VMEM),
        scratch_shapes=[pltpu.VMEM((N_DEVICES,) + vec.shape, vec.dtype), pltpu.SemaphoreType.DMA((N_DEVICES - 1,)),
                        pltpu.SemaphoreType.DMA((N_DEVICES - 1,))],
    )(vec)


def _adamw_math(w, g, m, v):
    m_new = ADAM_B1 * m + (1.0 - ADAM_B1) * g
    v_new = ADAM_B2 * v + (1.0 - ADAM_B2) * jnp.square(g)
    m_hat = m_new / (1.0 - ADAM_B1 ** ADAM_STEP)
    v_hat = v_new / (1.0 - ADAM_B2 ** ADAM_STEP)
    delta = -ADAM_LR * (m_hat / (jnp.sqrt(v_hat) + ADAM_EPS) + ADAM_WD * w)
    return delta, m_new, v_new


def adamw_shard(w, g0, g1, m, v):
    depth, rows, cols = w.shape
    half = rows // 2
    tr = min(256, half)
    nr = half // tr

    def body(w_ref, g0_ref, g1_ref, m_ref, v_ref, go_ref, d_ref, nm_ref, nv_ref):
        gv = jnp.where(pl.program_id(0) == 0, g0_ref[...], g1_ref[...])
        go_ref[...] = gv
        d_ref[...], nm_ref[...], nv_ref[...] = _adamw_math(w_ref[...], gv, m_ref[...], v_ref[...])

    spec = pl.BlockSpec((None, tr, cols), lambda l, h, r: (l, h * nr + r, 0))
    g0spec = pl.BlockSpec((None, tr, cols), lambda l, h, r: (jnp.where(l == 0, h, 1), jnp.where(l == 0, r, nr - 1), 0))
    g1spec = pl.BlockSpec((None, tr, cols), lambda l, h, r: (jnp.where(l == 1, h, 0), jnp.where(l == 1, r, 0), 0))
    return pl.pallas_call(
        body, name="adamw_shard", grid=(depth, 2, nr), in_specs=[spec, g0spec, g1spec, spec, spec], out_specs=[spec] * 4,
        out_shape=[jax.ShapeDtypeStruct(w.shape, f32)] * 4,
        compiler_params=_params("arbitrary", "arbitrary", "arbitrary"),
    )(w, g0, g1, m, v)


def adamw_small(w, g, m, v):
    def body(w_ref, g_ref, m_ref, v_ref, d_ref, nm_ref, nv_ref):
        d_ref[...], nm_ref[...], nv_ref[...] = _adamw_math(w_ref[...], g_ref[...], m_ref[...], v_ref[...])

    return pl.pallas_call(
        body, name="adamw_small", out_shape=[jax.ShapeDtypeStruct(w.shape, f32)] * 3,
    )(w, g, m, v)


WEIGHTS = ("mix_norm_g", "w_in", "q_gain", "k_gain", "sinks", "rel_bias", "conv_w", "conv_b", "dt_bias", "a_log", "d_skip",
           "ssm_norm_g", "w_out", "mlp_norm_g", "w_up", "w_down")
BIG = ("w_in", "w_out", "w_up", "w_down")
SMALL = tuple(n for n in WEIGHTS if n not in BIG)
PACK_COLS = 1024
PACK_ROWS = 16


def _pack(named, last=None):
    flat = jnp.concatenate([named[n].reshape(-1) for n in SMALL])
    tail = jnp.zeros((1,), f32) if last is None else last.reshape(1)
    pad = jnp.zeros((PACK_ROWS * PACK_COLS - flat.shape[0] - 1,), f32)
    return jnp.concatenate([flat, pad, tail]).reshape(PACK_ROWS, PACK_COLS)


def _unpack(buf, shapes):
    flat = buf.reshape(-1)
    out, at = {}, 0
    for n in SMALL:
        size = int(np.prod(shapes[n]))
        out[n] = flat[at:at + size].reshape(shapes[n])
        at += size
    return out


class _Exchange:
    GROUPS = {"A": (("w_up", 0), ("w_down", 0)), "B": (("w_in", 1), ("w_out", 1)), "C": (("w_up", 1), ("w_down", 1))}
    ICI_AT = {("mid", 0): "B", ("pre_out", 0): "C"}
    RELAY_AT = {("pre_out", 0): "A", ("pre_mlp", 0): "B", ("mid", 1): "C"}
    LAST = ("mix", 0)
    IN_FLIGHT = 2

    def __init__(self, wts, k_me, kc_arr):
        self.wts, self.k_me, self.kc_arr = wts, k_me, kc_arr
        self.own = {(n, l): wts[n][l].astype(bf16) for n in BIG for l in range(DEPTH)}
        now = gather_now([self.own["w_in", 0], self.own["w_out", 0]], wts["conv_w"])
        self.ready = {("w_in", 0): now[0], ("w_out", 0): now[1]}
        self.conv_w = jnp.transpose(now[2], (1, 2, 0, 3)).reshape(DEPTH, CONV_WIDTH, D_CONV)
        self.ici, self.relay = {}, {}
        self.gview, self.scatter, self.share, self.reduced = {}, [], [], {}
        self._start_ici("A", now[2])

    def _start_ici(self, g, after):
        srcs = [self.own[p] for p in self.GROUPS[g]]
        lands = [lax.empty((N_CHIPS,) + s.shape, s.dtype) for s in srcs]
        self.ici[g] = split_start("gather%s_ici_start" % g, _gather_maker("ici", len(srcs)), 3 * len(srcs), srcs + lands,
                                  after)
        return self.ici[g]["token"]

    def stage(self, name, after):
        if name == ("begin", 0):
            return self.ici["A"]["token"]
        tok = 0.0
        g = self.RELAY_AT.get(name)
        if g is not None:
            n = len(self.GROUPS[g])
            self.relay[g] = split_start("gather%s_relay_start" % g, _gather_maker("relay", n), 4 * n,
                                        split_wait(self.ici[g], after), after)
            tok = self.relay[g]["token"]
        if name in self.ICI_AT:
            tok = tok + self._start_ici(self.ICI_AT[name], after)
        return tok

    def _get(self, piece, after):
        if piece not in self.ready:
            g = [k for k, pieces in self.GROUPS.items() if piece in pieces][0]
            lands = split_wait(self.relay[g], after)[len(self.GROUPS[g]):]
            self.ready.update(zip(self.GROUPS[g], lands))
        return self.ready[piece]

    def w_in(self, l, after):
        return align_w_in(self._get(("w_in", l), after))

    def w_out(self, l, after):
        return self._get(("w_out", l), after).reshape(D_MODEL, D_MODEL)

    def mlp(self, l, after):
        return self._get(("w_up", l), after), self._get(("w_down", l), after)

    def _view(self, n, g):
        _, rows, cols = self.wts[n].shape
        return g.reshape(N_CHIPS, 2, rows // 2, cols)

    def grads(self, name, arrays, after):
        if name == self.LAST:
            self.held = (name, arrays)
            return 0.0
        return self._scatter(name, arrays, after) + self._advance(after, self.IN_FLIGHT)

    def flush(self, after):
        return self._scatter(*self.held, after) + self._advance(after, self.IN_FLIGHT)

    def _scatter(self, name, arrays, after):
        pieces = [(n, name[1]) for n in arrays]
        views = [self._view(n, g) for n, (g, _) in arrays.items()]
        sends = [g16.reshape(v.shape) for v, (_, g16) in zip(views, arrays.values())]
        self.gview.update(zip(pieces, views))
        lands = [lax.empty((7,) + v.shape[2:], bf16) for v in views]
        h = split_start("scatter_%s%d_start" % name, _scatter_maker(len(views)), 7 * len(views), sends + lands, after)
        self.scatter.append((pieces, h))
        return h["token"]

    def _take_share(self, after):
        pieces, h = self.share.pop(0)
        self.reduced.update(zip(pieces, split_wait(h, after)))

    def _take_scatter(self, after):
        pieces, h = self.scatter.pop(0)
        lands = split_wait(h, after)[len(pieces):]
        sums = [piece_sum(self.gview[p], land, self.kc_arr) for p, land in zip(pieces, lands)]
        hs = split_start(h["name"].replace("scatter", "share"), _share_maker(len(sums)), len(sums), sums, after)
        self.share.append((pieces, hs))
        return hs["token"]

    def _advance(self, after, newest):
        if self.share:
            self._take_share(after)
        return self._take_scatter(after) if len(self.scatter) > newest else 0.0

    def reduced_grads(self, names, after):
        want = [(n, l) for n in names for l in range(DEPTH)]
        while not all(p in self.reduced for p in want):
            if any(p in pieces for p in want for pieces, _ in self.share):
                self._take_share(after)
            else:
                self._take_scatter(after)
        return {n: [self.reduced[n, l] for l in range(DEPTH)] for n in names}


def kernel(x, mix_norm_g, w_in, q_gain, k_gain, sinks, rel_bias, conv_w, conv_b, dt_bias, a_log, d_skip, ssm_norm_g, w_out, mlp_norm_g, w_up, w_down, loss_target, m_mix_norm_g, m_w_in, m_q_gain, m_k_gain, m_sinks, m_rel_bias, m_conv_w, m_conv_b, m_dt_bias, m_a_log, m_d_skip, m_ssm_norm_g, m_w_out, m_mlp_norm_g, m_w_up, m_w_down, v_mix_norm_g, v_w_in, v_q_gain, v_k_gain, v_sinks, v_rel_bias, v_conv_w, v_conv_b, v_dt_bias, v_a_log, v_d_skip, v_ssm_norm_g, v_w_out, v_mlp_norm_g, v_w_up, v_w_down):
    wts = dict(mix_norm_g=mix_norm_g, w_in=w_in, q_gain=q_gain, k_gain=k_gain, sinks=sinks, rel_bias=rel_bias, conv_w=conv_w,
               conv_b=conv_b, dt_bias=dt_bias, a_log=a_log, d_skip=d_skip, ssm_norm_g=ssm_norm_g, w_out=w_out,
               mlp_norm_g=mlp_norm_g, w_up=w_up, w_down=w_down)
    mom = dict(mix_norm_g=m_mix_norm_g, w_in=m_w_in, q_gain=m_q_gain, k_gain=m_k_gain, sinks=m_sinks, rel_bias=m_rel_bias,
               conv_w=m_conv_w, conv_b=m_conv_b, dt_bias=m_dt_bias, a_log=m_a_log, d_skip=m_d_skip, ssm_norm_g=m_ssm_norm_g,
               w_out=m_w_out, mlp_norm_g=m_mlp_norm_g, w_up=m_w_up, w_down=m_w_down)
    var = dict(mix_norm_g=v_mix_norm_g, w_in=v_w_in, q_gain=v_q_gain, k_gain=v_k_gain, sinks=v_sinks, rel_bias=v_rel_bias,
               conv_w=v_conv_w, conv_b=v_conv_b, dt_bias=v_dt_bias, a_log=v_a_log, d_skip=v_d_skip, ssm_norm_g=v_ssm_norm_g,
               w_out=v_w_out, mlp_norm_g=v_mlp_norm_g, w_up=v_w_up, w_down=v_w_down)
    xi, yi, ci = _coords()
    k_me = 2 * xi + yi
    kc_arr = jnp.stack([k_me, ci]).astype(jnp.int32)

    prov = _Exchange(wts, k_me, kc_arr)
    small_w = {n: wts[n] for n in SMALL}
    small_w["conv_w"] = prov.conv_w
    loss, dx, grads, tok = local_step(x[0], loss_target[0], small_w, prov)

    small_shapes = {n: grads[n].shape for n in SMALL}
    small_sum = small_all_reduce(_pack(grads, loss) + tok)
    loss = small_sum[PACK_ROWS - 1, PACK_COLS - 1]
    tok = prov.flush(small_sum)
    small = _unpack(small_sum, small_shapes)
    cols = conv_w.shape[-1]
    small["conv_w"] = lax.dynamic_slice_in_dim(small["conv_w"], k_me * cols, cols, axis=2)
    g_out_d, d_out_d, m_out_d, v_out_d = {}, {}, {}, {}
    shard_shapes = {n: wts[n].shape for n in SMALL}
    d, nm, nv = adamw_small(_pack(wts), _pack(small) + tok, _pack(mom), _pack(var))
    for dst, buf in ((d_out_d, d), (m_out_d, nm), (v_out_d, nv)):
        dst.update(_unpack(buf, shard_shapes))
    g_out_d.update(small)

    after = d
    for names in (("w_up", "w_down"), ("w_in", "w_out")):
        for n, (g0, g1) in prov.reduced_grads(names, after).items():
            g_out_d[n], d_out_d[n], m_out_d[n], v_out_d[n] = adamw_shard(wts[n], g0, g1, mom[n], var[n])
            after = d_out_d[n]

    return (loss, dx[None], *[g_out_d[n] for n in WEIGHTS], *[d_out_d[n] for n in WEIGHTS],
            *[m_out_d[n] for n in WEIGHTS], *[v_out_d[n] for n in WEIGHTS])
```

```python
import functools

import numpy as np
import jax
import jax.numpy as jnp
from jax import lax
from jax.experimental import pallas as pl
from jax.experimental.pallas import tpu as pltpu

f32 = jnp.float32
bf16 = jnp.bfloat16

SEQ = 2048
D_MODEL = 1024
DEPTH = 2
HEAD_DIM = 64
N_Q_HEADS = 8
N_KV_HEADS = 2
Q_PER_KV = N_Q_HEADS // N_KV_HEADS
BLOCK = 128
N_BLOCKS = SEQ // BLOCK
N_BUCKETS = 32
MAX_DISTANCE = 128
SSM_HEADS = 8
SSM_HEAD_DIM = 64
SSM_GROUPS = 2
HEADS_PER_GROUP = SSM_HEADS // SSM_GROUPS
SSM_STATE = 128
CONV_WIDTH = 4
CHUNK = 128
N_CHUNKS = SEQ // CHUNK
D_FF = 4 * D_MODEL
D_ATTN = N_Q_HEADS * HEAD_DIM
D_KV = N_KV_HEADS * HEAD_DIM
D_SSM = SSM_HEADS * SSM_HEAD_DIM
D_BC = SSM_GROUPS * SSM_STATE
D_CONV = D_SSM + 2 * D_BC
D_IN = D_ATTN + 2 * D_KV + D_SSM + D_CONV + SSM_HEADS
EPS = 1e-6
NEG = -1e30
N_CHIPS = 4
FF_TILE = D_FF // N_CHIPS

LANE = 128
PW = D_ATTN + D_SSM + D_CONV + 2 * D_KV + LANE
OFF_Q, OFF_Z, OFF_X, OFF_K, OFF_V, OFF_DT = 0, 512, 1024, 2048, 2176, 2304

ADAM_LR = 0.001
ADAM_B1 = 0.9
ADAM_B2 = 0.999
ADAM_EPS = 1e-08
ADAM_WD = 0.01
ADAM_STEP = 10

VMEM_LIMIT = 56 * 1024 * 1024


def _params(*sem):
    return pltpu.CompilerParams(dimension_semantics=tuple(sem), vmem_limit_bytes=VMEM_LIMIT)


def _bdot(a, b):
    return jnp.dot(a.astype(bf16), b.astype(bf16), preferred_element_type=f32)


def _bdot_nt(a, b):
    return lax.dot_general(a.astype(bf16), b.astype(bf16), (((1,), (1,)), ((), ())), preferred_element_type=f32)


def _bdot_tn(a, b):
    return lax.dot_general(a.astype(bf16), b.astype(bf16), (((0,), (0,)), ((), ())), preferred_element_type=f32)


def _hdot(a, b):
    return jnp.dot(a, b, precision=lax.Precision.HIGHEST, preferred_element_type=f32)


def _sigmoid(x):
    return 1.0 / (1.0 + jnp.exp(-x))


def _softplus(x):
    return jnp.maximum(x, 0.0) + jnp.log1p(jnp.exp(-jnp.abs(x)))


def _rms(x):
    return lax.rsqrt(jnp.mean(x * x, axis=-1, keepdims=True) + EPS)


def _rms_bwd(dy, xhat, r, g):
    t = dy * g
    return r * (t - xhat * jnp.mean(t * xhat, axis=-1, keepdims=True))


def _full(shape):
    return pl.BlockSpec(shape, lambda *_: (0,) * len(shape))


def _bucket_table():
    qi = np.arange(BLOCK)[:, None]
    kj = np.arange(2 * BLOCK)[None, :]
    dist = qi + BLOCK - kj
    ok = (dist >= 0) & (dist < 128)
    d = np.clip(dist, 0, None)
    max_exact = N_BUCKETS // 2
    d_f = np.maximum(d, 1).astype(np.float32)
    large = max_exact + (np.log(d_f / np.float32(max_exact)) / np.float32(np.log(MAX_DISTANCE / max_exact))
                         * np.float32(N_BUCKETS - max_exact)).astype(np.int32)
    large = np.minimum(large, N_BUCKETS - 1)
    bucket = np.where(d < max_exact, d, large)
    return np.where(ok, bucket, -1).astype(np.int32)


def bias_build(rel_bias, bucket):
    def body(rel_ref, bkt_ref, o_ref):
        bkt = bkt_ref[...]
        for h in range(N_Q_HEADS):
            acc = jnp.where(bkt < 0, NEG, 0.0).astype(f32)
            for b in range(N_BUCKETS):
                acc = acc + jnp.where(bkt == b, rel_ref[b, h], 0.0)
            o_ref[h] = acc

    return pl.pallas_call(
        body, name="bias_build", out_shape=jax.ShapeDtypeStruct((N_Q_HEADS,) + bucket.shape, f32),
        in_specs=[pl.BlockSpec(memory_space=pltpu.SMEM), pl.BlockSpec(memory_space=pltpu.VMEM)],
        out_specs=pl.BlockSpec(memory_space=pltpu.VMEM),
    )(rel_bias, bucket)


def bias_bwd(dband0, dband1, bucket):
    def body(d0_ref, d1_ref, bkt_ref, o_ref):
        bkt = bkt_ref[...]
        o_ref[...] = jnp.zeros_like(o_ref)
        for h in range(N_Q_HEADS):
            d = d0_ref[h] + d1_ref[h]
            for b in range(N_BUCKETS):
                part = jnp.sum(jnp.where(bkt == b, d, 0.0), axis=1, keepdims=True)
                o_ref[b:b + 1, h:h + 1] = jnp.sum(part, axis=0, keepdims=True)

    return pl.pallas_call(
        body, name="bias_bwd", out_shape=jax.ShapeDtypeStruct((N_BUCKETS, LANE), f32),
    )(dband0, dband1, bucket)


W_IN_SHARD = D_IN // N_CHIPS
_ALIGNED_PIECES = ((0, 0, 512), (1, 190, 578), (2, 0, 124), (2, 124, 578), (3, 0, 570), (0, 512, 578), (1, 0, 62),
                   (1, 62, 190), (3, 570, 578))
_SHARD_PIECES = (((0, 512), (2048, 2114)), ((2114, 2176), (2176, 2304), (512, 900)), ((900, 1024), (1024, 1478)),
                 ((1478, 2048), (2304, 2312)))


def align_w_in(shards, tr=256):
    def body(s_ref, o_ref):
        parts = [s_ref[k, :, a:b] for k, a, b in _ALIGNED_PIECES]
        parts.append(jnp.zeros((tr, LANE - SSM_HEADS), s_ref.dtype))
        o_ref[...] = jnp.concatenate(parts, axis=-1)

    return pl.pallas_call(
        body, name="align_w_in", grid=(D_MODEL // tr,),
        in_specs=[pl.BlockSpec((N_CHIPS, tr, W_IN_SHARD), lambda i: (0, i, 0))],
        out_specs=pl.BlockSpec((tr, PW), lambda i: (i, 0)),
        out_shape=jax.ShapeDtypeStruct((D_MODEL, PW), shards.dtype),
        compiler_params=_params("arbitrary"),
    )(shards)


def split_w_in_grad(dw, tr=256):
    def body(d_ref, o_ref, o16_ref):
        for k, pieces in enumerate(_SHARD_PIECES):
            part = jnp.concatenate([d_ref[:, a:b] for a, b in pieces], axis=-1)
            o_ref[k] = part
            o16_ref[k] = part.astype(bf16)

    spec = pl.BlockSpec((N_CHIPS, tr, W_IN_SHARD), lambda i: (0, i, 0))
    return pl.pallas_call(
        body, name="split_w_in_grad", grid=(D_MODEL // tr,),
        in_specs=[pl.BlockSpec((tr, PW), lambda i: (i, 0))], out_specs=[spec, spec],
        out_shape=[jax.ShapeDtypeStruct((N_CHIPS, D_MODEL, W_IN_SHARD), f32),
                   jax.ShapeDtypeStruct((N_CHIPS, D_MODEL, W_IN_SHARD), bf16)],
        compiler_params=_params("arbitrary"),
    )(dw)

def in_fwd(x, g, w, tm=1024):
    def body(x_ref, g_ref, w_ref, o_ref):
        xv = x_ref[...]
        h = xv * _rms(xv) * g_ref[...]
        o_ref[...] = _bdot(h, w_ref[...])

    return pl.pallas_call(
        body, name="in_fwd", grid=(SEQ // tm,),
        in_specs=[pl.BlockSpec((tm, D_MODEL), lambda i: (i, 0)), _full((1, D_MODEL)), _resident((D_MODEL, PW))],
        out_specs=pl.BlockSpec((tm, PW), lambda i: (i, 0)),
        out_shape=jax.ShapeDtypeStruct((SEQ, PW), f32),
        compiler_params=_params("arbitrary"),
    )(x, g, w)


def _resident(shape):
    return pl.BlockSpec(shape, lambda *_: (0,) * len(shape), pipeline_mode=pl.Buffered(1))


def in_bwd(dq, dz, dxbc, dk, dv, ddt, x, g, w, dres, tm=512):
    def body(dq_ref, dz_ref, dx_ref, dk_ref, dv_ref, ddt_ref, x_ref, g_ref, w_ref, dres_ref, o_ref, dw_ref, dg_ref):
        i = pl.program_id(0)

        @pl.when(i == 0)
        def _():
            dw_ref[...] = jnp.zeros_like(dw_ref)
            dg_ref[...] = jnp.zeros_like(dg_ref)

        dproj = jnp.concatenate([dq_ref[...], dz_ref[...], dx_ref[...], dk_ref[...], dv_ref[...], ddt_ref[...]],
                                axis=-1).astype(bf16)
        xv = x_ref[...]
        r = _rms(xv)
        xhat = xv * r
        gv = g_ref[...]
        h = xhat * gv
        dw_ref[...] += _bdot_tn(h, dproj)
        dh = _bdot_nt(dproj, w_ref[...])
        dg_ref[...] += jnp.sum(dh * xhat, axis=0, keepdims=True)
        o_ref[...] = dres_ref[...] + _rms_bwd(dh, xhat, r, gv)

    tok = lambda w_: pl.BlockSpec((tm, w_), lambda i: (i, 0))
    return pl.pallas_call(
        body, name="in_bwd", grid=(SEQ // tm,),
        in_specs=[tok(D_ATTN), tok(D_SSM), tok(D_CONV), tok(D_KV), tok(D_KV), tok(LANE), tok(D_MODEL),
                  _full((1, D_MODEL)), _resident((D_MODEL, PW)), tok(D_MODEL)],
        out_specs=[tok(D_MODEL), _resident((D_MODEL, PW)), _full((1, D_MODEL))],
        out_shape=[jax.ShapeDtypeStruct((SEQ, D_MODEL), f32), jax.ShapeDtypeStruct((D_MODEL, PW), f32),
                   jax.ShapeDtypeStruct((1, D_MODEL), f32)],
        compiler_params=_params("arbitrary"),
    )(dq, dz, dxbc, dk, dv, ddt, x, g, w, dres)


def _attn_softmax_t(qk, bias_t, sink, first, key_row):
    s = qk * (HEAD_DIM ** -0.5) + bias_t
    s = jnp.where(jnp.logical_and(first, key_row < BLOCK), NEG, s)
    m = jnp.maximum(jnp.max(s, axis=0, keepdims=True), sink)
    p = jnp.exp(s - m)
    psink = jnp.exp(sink - m)
    inv = 1.0 / (jnp.sum(p, axis=0, keepdims=True) + psink)
    return p * inv, psink * inv


def _rms_t(x_t):
    return lax.rsqrt(jnp.mean(x_t * x_t, axis=0, keepdims=True) + EPS)


def attn_fwd_t(proj, q_gain_col, k_gain, sinks, bias_t):
    kcol, vcol = OFF_K // D_KV, OFF_V // D_KV

    def body(q_ref, kc_ref, kp_ref, vc_ref, vp_ref, qg_ref, kg_ref, sink_ref, bias_ref, o_ref, ot_scr):
        n = pl.program_id(0)
        first = n == 0
        key_row = lax.broadcasted_iota(jnp.int32, (2 * BLOCK, BLOCK), 0)
        k2 = jnp.concatenate([kp_ref[...], kc_ref[...]], axis=0)
        v_t = jnp.concatenate([vp_ref[...], vc_ref[...]], axis=0).T
        q_t = q_ref[...].T
        qg = jnp.broadcast_to(qg_ref[...], (HEAD_DIM, BLOCK))
        kg = kg_ref[...]
        for hk in range(N_KV_HEADS):
            sl = slice(hk * HEAD_DIM, (hk + 1) * HEAD_DIM)
            kk = k2[:, sl]
            kn = (kk * _rms(kk) * kg).astype(bf16)
            vt = v_t[sl, :].astype(bf16)
            heads = range(hk * Q_PER_KV, (hk + 1) * Q_PER_KV)
            qns = []
            for h in heads:
                qh = q_t[h * HEAD_DIM:(h + 1) * HEAD_DIM, :]
                qns.append(qh * _rms_t(qh) * qg)
            scores = [_bdot(kn, qn) for qn in qns]
            for h, s in zip(heads, scores):
                p, _ = _attn_softmax_t(s, bias_ref[h], sink_ref[h], first, key_row)
                ot_scr[h * HEAD_DIM:(h + 1) * HEAD_DIM, :] = _bdot(vt, p)
        o_ref[...] = ot_scr[...].T

    prev = lambda n: jnp.maximum(n - 1, 0)
    return pl.pallas_call(
        body, name="attn_fwd", grid=(N_BLOCKS,),
        in_specs=[pl.BlockSpec((BLOCK, D_ATTN), lambda n: (n, 0)),
                  pl.BlockSpec((BLOCK, D_KV), lambda n: (n, kcol)), pl.BlockSpec((BLOCK, D_KV), lambda n: (prev(n), kcol)),
                  pl.BlockSpec((BLOCK, D_KV), lambda n: (n, vcol)), pl.BlockSpec((BLOCK, D_KV), lambda n: (prev(n), vcol)),
                  _full((HEAD_DIM, 1)), _full((1, HEAD_DIM)), pl.BlockSpec(memory_space=pltpu.SMEM),
                  _full((N_Q_HEADS, 2 * BLOCK, BLOCK))],
        out_specs=pl.BlockSpec((BLOCK, D_ATTN), lambda n: (n, 0)),
        out_shape=jax.ShapeDtypeStruct((SEQ, D_ATTN), f32),
        scratch_shapes=[pltpu.VMEM((D_ATTN, BLOCK), f32)],
        compiler_params=_params("arbitrary"),
    )(proj, proj, proj, proj, proj, q_gain_col, k_gain, sinks, bias_t)


def attn_bwd_t(proj, d_out, q_gain_col, k_gain, sinks, bias_t):
    kcol, vcol = OFF_K // D_KV, OFF_V // D_KV

    def body(q_ref, kc_ref, kp_ref, vc_ref, vp_ref, do_ref, qg_ref, kg_ref, sink_ref, bias_ref,
             dq_ref, dk_ref, dv_ref, dband_ref, dsink_ref, dqg_ref, dkg_ref, dkn_scr, dv_scr, dqt_scr, dsink_acc, dqg_acc):
        i = pl.program_id(0)
        first = i == N_BLOCKS - 1

        @pl.when(i == 0)
        def _():
            for ref in (dband_ref, dkg_ref, dkn_scr, dv_scr, dsink_acc, dqg_acc):
                ref[...] = jnp.zeros_like(ref)

        key_row = lax.broadcasted_iota(jnp.int32, (2 * BLOCK, BLOCK), 0)
        k2 = jnp.concatenate([kp_ref[...], kc_ref[...]], axis=0)
        v2 = jnp.concatenate([vp_ref[...], vc_ref[...]], axis=0)
        q_t = q_ref[...].T
        do_t = do_ref[...].T
        qg = jnp.broadcast_to(qg_ref[...], (HEAD_DIM, BLOCK))
        kg = kg_ref[...]
        scale = HEAD_DIM ** -0.5
        for hk in range(N_KV_HEADS):
            sl = slice(hk * HEAD_DIM, (hk + 1) * HEAD_DIM)
            kk = k2[:, sl]
            rk = _rms(kk)
            khat = kk * rk
            kn = (khat * kg).astype(bf16)
            vb = v2[:, sl].astype(bf16)
            dkn = jnp.zeros((2 * BLOCK, HEAD_DIM), f32)
            dvv = jnp.zeros((2 * BLOCK, HEAD_DIM), f32)
            heads = range(hk * Q_PER_KV, (hk + 1) * Q_PER_KV)
            rqs, qhats, qns, d_os = [], [], [], []
            for h in heads:
                hs = slice(h * HEAD_DIM, (h + 1) * HEAD_DIM)
                qh = q_t[hs, :]
                rqs.append(_rms_t(qh))
                qhats.append(qh * rqs[-1])
                qns.append((qhats[-1] * qg).astype(bf16))
                d_os.append(do_t[hs, :].astype(bf16))
            scores = [_bdot(kn, qn) for qn in qns]
            dps = [_bdot(vb, d_o) for d_o in d_os]
            ps, dss = [], []
            for h, s, dp in zip(heads, scores, dps):
                p, psink = _attn_softmax_t(s, bias_ref[h], sink_ref[h], first, key_row)
                delta = jnp.sum(p * dp, axis=0, keepdims=True)
                ds = p * (dp - delta)
                dband_ref[h] += ds
                dsink_acc[h:h + 1, :] += -(psink * delta)
                ps.append(p.astype(bf16))
                dss.append(ds.astype(bf16))
            dqns = [_bdot_tn(kn, ds) * scale for ds in dss]
            for ds, qn, p, d_o in zip(dss, qns, ps, d_os):
                dkn = dkn + _bdot_nt(ds, qn) * scale
                dvv = dvv + _bdot_nt(p, d_o)
            for h, dqn, rq, qhat in zip(heads, dqns, rqs, qhats):
                dqg_acc[...] += dqn * qhat
                t = dqn * qg
                dqt_scr[h * HEAD_DIM:(h + 1) * HEAD_DIM, :] = rq * (t - qhat * jnp.mean(t * qhat, axis=0, keepdims=True))
            dkn_cur = dkn[BLOCK:] + dkn_scr[:, sl]
            dkn_scr[:, sl] = dkn[:BLOCK]
            khat_c, rk_c = khat[BLOCK:], rk[BLOCK:]
            dkg_ref[...] += jnp.sum(dkn_cur * khat_c, axis=0, keepdims=True)
            dk_ref[:, sl] = _rms_bwd(dkn_cur, khat_c, rk_c, kg)
            dv_ref[:, sl] = dvv[BLOCK:] + dv_scr[:, sl]
            dv_scr[:, sl] = dvv[:BLOCK]
        dq_ref[...] = dqt_scr[...].T

        @pl.when(i == N_BLOCKS - 1)
        def _():
            dsink_ref[...] = jnp.sum(dsink_acc[...], axis=1, keepdims=True)
            dqg_ref[...] = jnp.sum(dqg_acc[...], axis=1, keepdims=True)

    blk = lambda i: N_BLOCKS - 1 - i
    prev = lambda i: jnp.maximum(N_BLOCKS - 2 - i, 0)
    return pl.pallas_call(
        body, name="attn_bwd", grid=(N_BLOCKS,),
        in_specs=[pl.BlockSpec((BLOCK, D_ATTN), lambda i: (blk(i), 0)),
                  pl.BlockSpec((BLOCK, D_KV), lambda i: (blk(i), kcol)), pl.BlockSpec((BLOCK, D_KV), lambda i: (prev(i), kcol)),
                  pl.BlockSpec((BLOCK, D_KV), lambda i: (blk(i), vcol)), pl.BlockSpec((BLOCK, D_KV), lambda i: (prev(i), vcol)),
                  pl.BlockSpec((BLOCK, D_ATTN), lambda i: (blk(i), 0)),
                  _full((HEAD_DIM, 1)), _full((1, HEAD_DIM)), pl.BlockSpec(memory_space=pltpu.SMEM),
                  _full((N_Q_HEADS, 2 * BLOCK, BLOCK))],
        out_specs=[pl.BlockSpec((BLOCK, D_ATTN), lambda i: (blk(i), 0)), pl.BlockSpec((BLOCK, D_KV), lambda i: (blk(i), 0)),
                   pl.BlockSpec((BLOCK, D_KV), lambda i: (blk(i), 0)), _full((N_Q_HEADS, 2 * BLOCK, BLOCK)),
                   _full((N_Q_HEADS, 1)), _full((HEAD_DIM, 1)), _full((1, HEAD_DIM))],
        out_shape=[jax.ShapeDtypeStruct((SEQ, D_ATTN), f32), jax.ShapeDtypeStruct((SEQ, D_KV), f32),
                   jax.ShapeDtypeStruct((SEQ, D_KV), f32), jax.ShapeDtypeStruct((N_Q_HEADS, 2 * BLOCK, BLOCK), f32),
                   jax.ShapeDtypeStruct((N_Q_HEADS, 1), f32), jax.ShapeDtypeStruct((HEAD_DIM, 1), f32),
                   jax.ShapeDtypeStruct((1, HEAD_DIM), f32)],
        scratch_shapes=[pltpu.VMEM((BLOCK, D_KV), f32), pltpu.VMEM((BLOCK, D_KV), f32), pltpu.VMEM((D_ATTN, BLOCK), f32),
                        pltpu.VMEM((N_Q_HEADS, BLOCK), f32), pltpu.VMEM((HEAD_DIM, BLOCK), f32)],
        compiler_params=_params("arbitrary"),
    )(proj, proj, proj, proj, proj, d_out, q_gain_col, k_gain, sinks, bias_t)


def _shift_down(u, s, row):
    if s == 0:
        return u
    return jnp.where(row >= s, pltpu.roll(u, s, 0), 0.0)


def _shift_up(u, s, row):
    if s == 0:
        return u
    return jnp.where(row < SEQ - s, pltpu.roll(u, SEQ - s, 0), 0.0)


def conv_fwd(proj, conv_w, conv_b):
    xcol = OFF_X // LANE

    def body(u_ref, w_ref, b_ref, o_ref):
        u = u_ref[...]
        row = lax.broadcasted_iota(jnp.int32, u.shape, 0)
        pre = b_ref[...] + jnp.zeros_like(u)
        for k in range(CONV_WIDTH):
            pre = pre + w_ref[k:k + 1, :] * _shift_down(u, CONV_WIDTH - 1 - k, row)
        o_ref[...] = pre * _sigmoid(pre)

    return pl.pallas_call(
        body, name="conv_fwd", grid=(D_CONV // LANE,),
        in_specs=[pl.BlockSpec((SEQ, LANE), lambda j: (0, xcol + j)), pl.BlockSpec((CONV_WIDTH, LANE), lambda j: (0, j)),
                  pl.BlockSpec((1, LANE), lambda j: (0, j))],
        out_specs=pl.BlockSpec((SEQ, LANE), lambda j: (0, j)),
        out_shape=jax.ShapeDtypeStruct((SEQ, D_CONV), f32),
        compiler_params=_params("arbitrary"),
    )(proj, conv_w, conv_b)


def conv_bwd(proj, d_act, conv_w, conv_b):
    xcol = OFF_X // LANE

    def body(u_ref, da_ref, w_ref, b_ref, du_ref, dw_ref, db_ref):
        u = u_ref[...]
        row = lax.broadcasted_iota(jnp.int32, u.shape, 0)
        shifted = [_shift_down(u, CONV_WIDTH - 1 - k, row) for k in range(CONV_WIDTH)]
        pre = b_ref[...] + jnp.zeros_like(u)
        for k in range(CONV_WIDTH):
            pre = pre + w_ref[k:k + 1, :] * shifted[k]
        sg = _sigmoid(pre)
        dpre = da_ref[...] * (sg * (1.0 + pre * (1.0 - sg)))
        db_ref[...] = jnp.sum(dpre, axis=0, keepdims=True)
        du = jnp.zeros_like(u)
        for k in range(CONV_WIDTH):
            dw_ref[k:k + 1, :] = jnp.sum(dpre * shifted[k], axis=0, keepdims=True)
            du = du + w_ref[k:k + 1, :] * _shift_up(dpre, CONV_WIDTH - 1 - k, row)
        du_ref[...] = du

    return pl.pallas_call(
        body, name="conv_bwd", grid=(D_CONV // LANE,),
        in_specs=[pl.BlockSpec((SEQ, LANE), lambda j: (0, xcol + j)), pl.BlockSpec((SEQ, LANE), lambda j: (0, j)),
                  pl.BlockSpec((CONV_WIDTH, LANE), lambda j: (0, j)), pl.BlockSpec((1, LANE), lambda j: (0, j))],
        out_specs=[pl.BlockSpec((SEQ, LANE), lambda j: (0, j)), pl.BlockSpec((CONV_WIDTH, LANE), lambda j: (0, j)),
                   pl.BlockSpec((1, LANE), lambda j: (0, j))],
        out_shape=[jax.ShapeDtypeStruct((SEQ, D_CONV), f32), jax.ShapeDtypeStruct((CONV_WIDTH, D_CONV), f32),
                   jax.ShapeDtypeStruct((1, D_CONV), f32)],
        compiler_params=_params("arbitrary"),
    )(proj, d_act, conv_w, conv_b)


def _ssd_chunk_common(dt_raw, dtb, alog):
    row = lax.broadcasted_iota(jnp.int32, (CHUNK, CHUNK), 0)
    col = lax.broadcasted_iota(jnp.int32, (CHUNK, CHUNK), 1)
    tri = (row >= col).astype(f32)
    strict = (row > col).astype(f32)
    dtp = _softplus(dt_raw + dtb)
    a_row = -jnp.exp(alog)
    d_a = dtp * a_row
    cs = _hdot(tri, d_a)
    cs_last = cs[CHUNK - 1:CHUNK, :]
    return row, col, dtp, a_row, cs, cs.T, cs_last


def _seg_decay(cs, cs_t, hd, row, col):
    seg = cs[:, hd:hd + 1] - cs_t[hd:hd + 1, :]
    return jnp.where(row >= col, jnp.exp(seg), 0.0)


GROUP_W = HEADS_PER_GROUP * SSM_HEAD_DIM


def _group_indicator(g):
    j = lax.broadcasted_iota(jnp.int32, (GROUP_W, LANE), 0)
    lane = lax.broadcasted_iota(jnp.int32, (GROUP_W, LANE), 1)
    return (lane == g * HEADS_PER_GROUP + j // SSM_HEAD_DIM).astype(bf16)


def _bf16_pieces(a, n):
    pieces = []
    for _ in range(n):
        p = a.astype(bf16)
        pieces.append(p)
        a = a - p.astype(f32)
    return pieces


def _head_spread(a, ind):
    return sum(lax.dot_general(p, ind, (((1,), (1,)), ((), ())), preferred_element_type=f32) for p in _bf16_pieces(a, 3))


def _head_sums(a, ind):
    return sum(jnp.dot(p, ind, preferred_element_type=f32) for p in _bf16_pieces(a, 2))


def ssd_fwd_g(act, proj, dt_bias, a_log, d_skip, norm_g):
    zcol, dtcol = OFF_Z // D_SSM, OFF_DT // LANE

    def body(act_ref, z_ref, dt_ref, dtb_ref, alog_ref, dsk_ref, ng_ref, out_ref, ypre_ref, st_ref, state):
        c = pl.program_id(0)

        @pl.when(c == 0)
        def _():
            state[...] = jnp.zeros_like(state)

        row, col, dtp, a_row, cs, cs_t, cs_last = _ssd_chunk_common(dt_ref[...], dtb_ref[...], alog_ref[...])
        e_cs = jnp.exp(cs)
        dte = jnp.exp(cs_last - cs)
        rows8 = jnp.concatenate([jnp.exp(cs_last), dsk_ref[...], jnp.zeros((6, LANE), f32)], axis=0)
        z = z_ref[...]
        sz = z * _sigmoid(z)
        ng = ng_ref[...]
        for g in range(SSM_GROUPS):
            gs = slice(g * GROUP_W, (g + 1) * GROUP_W)
            ind = _group_indicator(g)
            xg = act_ref[:, gs]
            bg = act_ref[:, D_SSM + g * SSM_STATE:D_SSM + (g + 1) * SSM_STATE]
            cg = act_ref[:, D_SSM + D_BC + g * SSM_STATE:D_SSM + D_BC + (g + 1) * SSM_STATE]
            dt_e, e_e, dte_e = _head_spread(dtp, ind), _head_spread(e_cs, ind), _head_spread(dte, ind)
            rows_e = _head_spread(rows8, ind)
            ecl_e, dsk_e = rows_e[0:1], rows_e[1:2]
            xdt = xg * dt_e
            prev = state[g]
            st_ref[0, g] = prev
            cb = _bdot_nt(cg, bg)
            goff = _bdot(cg, prev)
            snew = _bdot_tn(bg, xdt * dte_e)
            heads = range(g * HEADS_PER_GROUP, (g + 1) * HEADS_PER_GROUP)
            ms = [cb * _seg_decay(cs, cs_t, hd, row, col) for hd in heads]
            yd = [_bdot(m, xdt[:, r * SSM_HEAD_DIM:(r + 1) * SSM_HEAD_DIM]) for r, m in enumerate(ms)]
            y = jnp.concatenate(yd, axis=1) + e_e * goff + xg * dsk_e
            state[g] = prev * ecl_e + snew
            ypre_ref[:, gs] = y
            part = y * sz[:, gs]
            out_ref[:, gs] = part * _rms(part) * ng[:, gs]

    return pl.pallas_call(
        body, name="ssd_fwd", grid=(N_CHUNKS,),
        in_specs=[pl.BlockSpec((CHUNK, D_CONV), lambda c: (c, 0)), pl.BlockSpec((CHUNK, D_SSM), lambda c: (c, zcol)),
                  pl.BlockSpec((CHUNK, LANE), lambda c: (c, dtcol)), _full((1, LANE)), _full((1, LANE)), _full((1, LANE)),
                  _full((1, D_SSM))],
        out_specs=[pl.BlockSpec((CHUNK, D_SSM), lambda c: (c, 0)), pl.BlockSpec((CHUNK, D_SSM), lambda c: (c, 0)),
                   pl.BlockSpec((1, SSM_GROUPS, SSM_STATE, GROUP_W), lambda c: (c, 0, 0, 0))],
        out_shape=[jax.ShapeDtypeStruct((SEQ, D_SSM), f32), jax.ShapeDtypeStruct((SEQ, D_SSM), f32),
                   jax.ShapeDtypeStruct((N_CHUNKS, SSM_GROUPS, SSM_STATE, GROUP_W), f32)],
        scratch_shapes=[pltpu.VMEM((SSM_GROUPS, SSM_STATE, GROUP_W), f32)],
        compiler_params=_params("arbitrary"),
    )(act, proj, proj, dt_bias, a_log, d_skip, norm_g)


def ssd_bwd_g(act, proj, ypre, states, d_out, dt_bias, a_log, d_skip, norm_g):
    zcol, dtcol = OFF_Z // D_SSM, OFF_DT // LANE

    def body(act_ref, z_ref, dt_ref, ypre_ref, st_ref, do_ref, dtb_ref, alog_ref, dsk_ref, ng_ref,
             dact_ref, ddt_ref, dz_ref, dng_ref, dpar_ref, dstate):
        i = pl.program_id(0)

        @pl.when(i == 0)
        def _():
            for ref in (dng_ref, dpar_ref, dstate):
                ref[...] = jnp.zeros_like(ref)

        row, col, dtp, a_row, cs, cs_t, cs_last = _ssd_chunk_common(dt_ref[...], dtb_ref[...], alog_ref[...])
        upper = (row <= col).astype(f32)
        lane = lax.broadcasted_iota(jnp.int32, (CHUNK, LANE), 1)
        rowl = lax.broadcasted_iota(jnp.int32, (CHUNK, LANE), 0)
        e_cs = jnp.exp(cs)
        dte = jnp.exp(cs_last - cs)
        ecl = jnp.exp(cs_last)
        rows8 = jnp.concatenate([ecl, dsk_ref[...], jnp.zeros((6, LANE), f32)], axis=0)
        z = z_ref[...]
        sgz = _sigmoid(z)
        sz = z * sgz
        ng = ng_ref[...]
        ddt_mat = jnp.zeros((CHUNK, LANE), f32)
        dcs_mat = jnp.zeros((CHUNK, LANE), f32)
        dcs_t = jnp.zeros((LANE, CHUNK), f32)
        dcsl_row = jnp.zeros((1, LANE), f32)
        dd_row = jnp.zeros((1, LANE), f32)
        for g in range(SSM_GROUPS):
            gs = slice(g * GROUP_W, (g + 1) * GROUP_W)
            bsl = slice(D_SSM + g * SSM_STATE, D_SSM + (g + 1) * SSM_STATE)
            csl = slice(D_SSM + D_BC + g * SSM_STATE, D_SSM + D_BC + (g + 1) * SSM_STATE)
            ind = _group_indicator(g)
            y = ypre_ref[:, gs]
            part = y * sz[:, gs]
            r = _rms(part)
            yhat = part * r
            d_o = do_ref[:, gs]
            dng_ref[:, gs] += jnp.sum(d_o * yhat, axis=0, keepdims=True)
            dyz = _rms_bwd(d_o, yhat, r, ng[:, gs])
            dy = dyz * sz[:, gs]
            dz_ref[:, gs] = dyz * y * (sgz[:, gs] * (1.0 + z[:, gs] * (1.0 - sgz[:, gs])))

            xg = act_ref[:, gs]
            bg = act_ref[:, bsl]
            cg = act_ref[:, csl]
            dt_e, e_e, dte_e = _head_spread(dtp, ind), _head_spread(e_cs, ind), _head_spread(dte, ind)
            rows_e = _head_spread(rows8, ind)
            ecl_e, dsk_e = rows_e[0:1], rows_e[1:2]
            xdt = xg * dt_e
            prev = st_ref[0, g]
            dh = dstate[g]
            heads = range(g * HEADS_PER_GROUP, (g + 1) * HEADS_PER_GROUP)
            hsl = [slice(r_ * SSM_HEAD_DIM, (r_ + 1) * SSM_HEAD_DIM) for r_ in range(HEADS_PER_GROUP)]
            cb = _bdot_nt(cg, bg)
            lms = [_seg_decay(cs, cs_t, hd, row, col) for hd in heads]
            ms = [cb * lm for lm in lms]
            gmat = _bdot(cg, prev)
            dgm = dy * e_e
            dcg = _bdot_nt(dgm, prev)
            dprev = _bdot_tn(cg, dgm)
            dbg = _bdot_nt(xdt * dte_e, dh)
            dw = _bdot(bg, dh)
            dms = [_bdot_nt(dy[:, s_], xdt[:, s_]) for s_ in hsl]
            dxdts = [_bdot_tn(m, dy[:, s_]) for m, s_ in zip(ms, hsl)]
            dxdt = jnp.concatenate(dxdts, axis=1) + dw * dte_e
            dact_ref[:, gs] = dy * dsk_e + dxdt * dt_e
            dstate[g] = dprev + dh * ecl_e
            dcb = jnp.zeros((CHUNK, CHUNK), f32)
            for hd, dm, lm, m in zip(heads, dms, lms, ms):
                dcb = dcb + dm * lm
                dseg = dm * m
                dcs_mat = dcs_mat + jnp.where(lane == hd, jnp.sum(dseg, axis=1, keepdims=True), 0.0)
                dcs_t = jnp.where(row == hd, jnp.sum(dseg, axis=0, keepdims=True), dcs_t)
            dact_ref[:, bsl] = dbg + _bdot_tn(dcb, cg)
            dact_ref[:, csl] = dcg + _bdot(dcb, bg)
            ddte = _head_sums(dw * xdt, ind) * dte
            dcs_mat = dcs_mat + _head_sums(dy * gmat, ind) * e_cs - ddte
            ddt_mat = ddt_mat + _head_sums(dxdt * xg, ind)
            dcsl_row = (dcsl_row + jnp.sum(ddte, axis=0, keepdims=True)
                        + jnp.sum(_head_sums(dh * prev, ind), axis=0, keepdims=True) * ecl)
            dd_row = dd_row + jnp.sum(_head_sums(dy * xg, ind), axis=0, keepdims=True)
        dcs_mat = dcs_mat - dcs_t.T + jnp.where(rowl == CHUNK - 1, dcsl_row, 0.0)
        dda = _hdot(upper, dcs_mat)
        ddt_mat = ddt_mat + dda * a_row
        da_row = jnp.sum(dda * dtp, axis=0, keepdims=True)
        ddt_raw = ddt_mat * _sigmoid(dt_ref[...] + dtb_ref[...])
        ddt_ref[...] = ddt_raw
        dpar_ref[0:1, :] += jnp.sum(ddt_raw, axis=0, keepdims=True)
        dpar_ref[1:2, :] += da_row * a_row
        dpar_ref[2:3, :] += dd_row

    blk = lambda i: N_CHUNKS - 1 - i
    return pl.pallas_call(
        body, name="ssd_bwd", grid=(N_CHUNKS,),
        in_specs=[pl.BlockSpec((CHUNK, D_CONV), lambda i: (blk(i), 0)), pl.BlockSpec((CHUNK, D_SSM), lambda i: (blk(i), zcol)),
                  pl.BlockSpec((CHUNK, LANE), lambda i: (blk(i), dtcol)), pl.BlockSpec((CHUNK, D_SSM), lambda i: (blk(i), 0)),
                  pl.BlockSpec((1, SSM_GROUPS, SSM_STATE, GROUP_W), lambda i: (blk(i), 0, 0, 0)),
                  pl.BlockSpec((CHUNK, D_SSM), lambda i: (blk(i), 0)),
                  _full((1, LANE)), _full((1, LANE)), _full((1, LANE)), _full((1, D_SSM))],
        out_specs=[pl.BlockSpec((CHUNK, D_CONV), lambda i: (blk(i), 0)), pl.BlockSpec((CHUNK, LANE), lambda i: (blk(i), 0)),
                   pl.BlockSpec((CHUNK, D_SSM), lambda i: (blk(i), 0)), _full((1, D_SSM)), _full((8, LANE))],
        out_shape=[jax.ShapeDtypeStruct((SEQ, D_CONV), f32), jax.ShapeDtypeStruct((SEQ, LANE), f32),
                   jax.ShapeDtypeStruct((SEQ, D_SSM), f32), jax.ShapeDtypeStruct((1, D_SSM), f32),
                   jax.ShapeDtypeStruct((8, LANE), f32)],
        scratch_shapes=[pltpu.VMEM((SSM_GROUPS, SSM_STATE, GROUP_W), f32)],
        compiler_params=_params("arbitrary"),
    )(act, proj, proj, ypre, states, d_out, dt_bias, a_log, d_skip, norm_g)


def out_fwd(x, attn, ssm, w_out, tm=1024):
    def body(x_ref, a_ref, s_ref, w_ref, o_ref):
        o_ref[...] = x_ref[...] + _bdot(a_ref[...], w_ref[:D_ATTN, :]) + _bdot(s_ref[...], w_ref[D_ATTN:, :])

    tok = lambda w_: pl.BlockSpec((tm, w_), lambda i: (i, 0))
    return pl.pallas_call(
        body, name="out_fwd", grid=(SEQ // tm,),
        in_specs=[tok(D_MODEL), tok(D_ATTN), tok(D_SSM), _full((D_MODEL, D_MODEL))],
        out_specs=tok(D_MODEL), out_shape=jax.ShapeDtypeStruct((SEQ, D_MODEL), f32),
        compiler_params=_params("arbitrary"),
    )(x, attn, ssm, w_out)


def out_bwd(dx1, attn, ssm, w_out, tm=1024):
    nt = SEQ // tm

    def body(d_ref, a_ref, s_ref, w_ref, da_ref, ds_ref, dw_ref, dw16_ref):
        i = pl.program_id(0)

        @pl.when(i == 0)
        def _():
            dw_ref[...] = jnp.zeros_like(dw_ref)

        d = d_ref[...].astype(bf16)
        dcat = _bdot_nt(d, w_ref[...])
        da_ref[...] = dcat[:, :D_ATTN]
        ds_ref[...] = dcat[:, D_ATTN:]
        dw_ref[:D_ATTN, :] += _bdot_tn(a_ref[...], d)
        dw_ref[D_ATTN:, :] += _bdot_tn(s_ref[...], d)

        @pl.when(i == nt - 1)
        def _():
            dw16_ref[...] = dw_ref[...].astype(bf16)

    tok = lambda w_: pl.BlockSpec((tm, w_), lambda i: (i, 0))
    return pl.pallas_call(
        body, name="out_bwd", grid=(nt,),
        in_specs=[tok(D_MODEL), tok(D_ATTN), tok(D_SSM), _resident((D_MODEL, D_MODEL))],
        out_specs=[tok(D_ATTN), tok(D_SSM), _resident((D_MODEL, D_MODEL)), _resident((D_MODEL, D_MODEL))],
        out_shape=[jax.ShapeDtypeStruct((SEQ, D_ATTN), f32), jax.ShapeDtypeStruct((SEQ, D_SSM), f32),
                   jax.ShapeDtypeStruct((D_MODEL, D_MODEL), f32), jax.ShapeDtypeStruct((D_MODEL, D_MODEL), bf16)],
        compiler_params=_params("arbitrary"),
    )(dx1, attn, ssm, w_out)


MLP_SUB = 256


def mlp_fwd(x1, g, w_up, w_down, tm=1024):
    def body(x_ref, g_ref, wu_ref, wd_ref, o_ref, u_ref, h_scr):
        j = pl.program_id(1)

        @pl.when(j == 0)
        def _():
            xv = x_ref[...]
            h_scr[...] = (xv * _rms(xv) * g_ref[...]).astype(bf16)
            o_ref[...] = xv

        for r in range(tm // MLP_SUB):
            rows = slice(r * MLP_SUB, (r + 1) * MLP_SUB)
            u = jnp.dot(h_scr[rows, :], wu_ref[...], preferred_element_type=f32)
            u_ref[rows, :] = u
            a = jnp.square(jnp.maximum(u, 0.0))
            o_ref[rows, :] += _bdot(a, wd_ref[...])

    return pl.pallas_call(
        body, name="mlp_fwd", grid=(SEQ // tm, N_CHIPS),
        in_specs=[pl.BlockSpec((tm, D_MODEL), lambda i, j: (i, 0)), _full((1, D_MODEL)),
                  pl.BlockSpec((None, D_MODEL, FF_TILE), lambda i, j: (j, 0, 0)),
                  pl.BlockSpec((None, FF_TILE, D_MODEL), lambda i, j: (j, 0, 0))],
        out_specs=[pl.BlockSpec((tm, D_MODEL), lambda i, j: (i, 0)), pl.BlockSpec((tm, FF_TILE), lambda i, j: (i, j))],
        out_shape=[jax.ShapeDtypeStruct((SEQ, D_MODEL), f32), jax.ShapeDtypeStruct((SEQ, D_FF), f32)],
        scratch_shapes=[pltpu.VMEM((tm, D_MODEL), bf16)],
        compiler_params=_params("arbitrary", "arbitrary"),
    )(x1, g, w_up, w_down)


def mlp_bwd_data(dx2, u, x1, g, w_up, w_down, tm=1024):
    def body(d_ref, u_ref, x_ref, g_ref, wu_ref, wd_ref, dx_ref, du_ref, dg_ref, dh_scr):
        i, j = pl.program_id(0), pl.program_id(1)

        @pl.when(jnp.logical_and(i == 0, j == 0))
        def _():
            dg_ref[...] = jnp.zeros_like(dg_ref)

        @pl.when(j == 0)
        def _():
            dh_scr[...] = jnp.zeros_like(dh_scr)

        for r in range(tm // MLP_SUB):
            rows = slice(r * MLP_SUB, (r + 1) * MLP_SUB)
            da = _bdot_nt(d_ref[rows, :], wd_ref[...])
            du = (da * (2.0 * jnp.maximum(u_ref[rows, :], 0.0))).astype(bf16)
            du_ref[rows, :] = du
            dh_scr[rows, :] += _bdot_nt(du, wu_ref[...])

        @pl.when(j == N_CHIPS - 1)
        def _():
            xv = x_ref[...]
            r = _rms(xv)
            xhat = xv * r
            dh = dh_scr[...]
            dg_ref[...] += jnp.sum(dh * xhat, axis=0, keepdims=True)
            dx_ref[...] = d_ref[...] + _rms_bwd(dh, xhat, r, g_ref[...])

    return pl.pallas_call(
        body, name="mlp_bwd_data", grid=(SEQ // tm, N_CHIPS),
        in_specs=[pl.BlockSpec((tm, D_MODEL), lambda i, j: (i, 0)), pl.BlockSpec((tm, FF_TILE), lambda i, j: (i, j)),
                  pl.BlockSpec((tm, D_MODEL), lambda i, j: (i, 0)), _full((1, D_MODEL)),
                  pl.BlockSpec((None, D_MODEL, FF_TILE), lambda i, j: (j, 0, 0)),
                  pl.BlockSpec((None, FF_TILE, D_MODEL), lambda i, j: (j, 0, 0))],
        out_specs=[pl.BlockSpec((tm, D_MODEL), lambda i, j: (i, 0)), pl.BlockSpec((tm, FF_TILE), lambda i, j: (i, j)),
                   _full((1, D_MODEL))],
        out_shape=[jax.ShapeDtypeStruct((SEQ, D_MODEL), f32), jax.ShapeDtypeStruct((SEQ, D_FF), bf16),
                   jax.ShapeDtypeStruct((1, D_MODEL), f32)],
        scratch_shapes=[pltpu.VMEM((tm, D_MODEL), f32)],
        compiler_params=_params("arbitrary", "arbitrary"),
    )(dx2, u, x1, g, w_up, w_down)


def mlp_bwd_weights(dx2, u, du, x1, g, tm=512):
    nt = SEQ // tm

    def body(d_ref, u_ref, du_ref, x_ref, g_ref, dwu_ref, dwd_ref, dwu16_ref, dwd16_ref, h_scr, d_scr):
        j, i = pl.program_id(0), pl.program_id(1)

        @pl.when(j == 0)
        def _():
            xv = x_ref[...]
            h_scr[i] = (xv * _rms(xv) * g_ref[...]).T.astype(bf16)
            d_scr[i] = d_ref[...].astype(bf16)

        @pl.when(i == 0)
        def _():
            dwu_ref[...] = jnp.zeros_like(dwu_ref)
            dwd_ref[...] = jnp.zeros_like(dwd_ref)

        dwu_ref[...] += jnp.dot(h_scr[i], du_ref[...], preferred_element_type=f32)
        a = jnp.square(jnp.maximum(u_ref[...], 0.0))
        dwd_ref[...] += _bdot_tn(a, d_scr[i])

        @pl.when(i == nt - 1)
        def _():
            dwu16_ref[...] = dwu_ref[...].astype(bf16)
            dwd16_ref[...] = dwd_ref[...].astype(bf16)

    up = pl.BlockSpec((None, D_MODEL, FF_TILE), lambda j, i: (j, 0, 0))
    down = pl.BlockSpec((None, FF_TILE, D_MODEL), lambda j, i: (j, 0, 0))
    first_pass = pl.BlockSpec((tm, D_MODEL), lambda j, i: (jnp.where(j == 0, i, nt - 1), 0))
    return pl.pallas_call(
        body, name="mlp_bwd_weights", grid=(N_CHIPS, nt),
        in_specs=[first_pass, pl.BlockSpec((tm, FF_TILE), lambda j, i: (i, j)),
                  pl.BlockSpec((tm, FF_TILE), lambda j, i: (i, j)), first_pass, _full((1, D_MODEL))],
        out_specs=[up, down, up, down],
        out_shape=[jax.ShapeDtypeStruct((N_CHIPS, D_MODEL, FF_TILE), f32), jax.ShapeDtypeStruct((N_CHIPS, FF_TILE, D_MODEL), f32),
                   jax.ShapeDtypeStruct((N_CHIPS, D_MODEL, FF_TILE), bf16), jax.ShapeDtypeStruct((N_CHIPS, FF_TILE, D_MODEL), bf16)],
        scratch_shapes=[pltpu.VMEM((nt, D_MODEL, tm), bf16), pltpu.VMEM((nt, tm, D_MODEL), bf16)],
        compiler_params=_params("arbitrary", "arbitrary"),
    )(dx2, u, du, x1, g)


def loss_head(y, target, tm=512):
    def body(y_ref, t_ref, dy_ref, l_ref):
        @pl.when(pl.program_id(0) == 0)
        def _():
            l_ref[...] = jnp.zeros_like(l_ref)

        d = y_ref[...] - t_ref[...]
        dy_ref[...] = d * (1.0 / D_MODEL)
        part = jnp.sum(jnp.mean(d * d, axis=-1, keepdims=True), axis=0, keepdims=True)
        l_ref[...] += 0.5 * part

    tok = pl.BlockSpec((tm, D_MODEL), lambda i: (i, 0))
    return pl.pallas_call(
        body, name="loss_head", grid=(SEQ // tm,), in_specs=[tok, tok], out_specs=[tok, _full((1, 1))],
        out_shape=[jax.ShapeDtypeStruct((SEQ, D_MODEL), f32), jax.ShapeDtypeStruct((1, 1), f32)],
        compiler_params=_params("arbitrary"),
    )(y, target)


def _pad_lane(v):
    return jnp.pad(v, (0, LANE - v.shape[0]))[None, :]


def local_step(x, target, w, prov):
    bucket = jnp.asarray(_bucket_table().T)
    bias = bias_build(w["rel_bias"], bucket)
    saved = []
    for l in range(DEPTH):
        g_mix = w["mix_norm_g"][l][None, :] + prov.stage(("begin", l), x)
        w_in = prov.w_in(l, x)
        proj = in_fwd(x, g_mix, w_in)
        conv_b = w["conv_b"][l][None, :]
        act = conv_fwd(proj, w["conv_w"][l], conv_b)
        dtb = _pad_lane(w["dt_bias"][l]) + prov.stage(("mid", l), act)
        alog, dsk = _pad_lane(w["a_log"][l]), _pad_lane(w["d_skip"][l])
        ng = w["ssm_norm_g"][l][None, :]
        ssm, ypre, states = ssd_fwd_g(act, proj, dtb, alog, dsk, ng)
        qg, kg = w["q_gain"][l][:, None] + 0.0 * ssm[:1, :1], w["k_gain"][l][None, :]
        attn = attn_fwd_t(proj, qg, kg, w["sinks"][l], bias)
        tok = prov.stage(("pre_out", l), attn)
        w_out = prov.w_out(l, attn) + jnp.asarray(tok, bf16)
        x1 = out_fwd(x, attn, ssm, w_out)
        g_mlp = w["mlp_norm_g"][l][None, :] + prov.stage(("pre_mlp", l), x1)
        w_up, w_down = prov.mlp(l, x1)
        x2, u = mlp_fwd(x1, g_mlp, w_up, w_down)
        saved.append(dict(x=x, proj=proj, attn=attn, act=act, ssm=ssm, ypre=ypre, states=states, x1=x1, u=u,
                          g_mix=g_mix, qg=qg, kg=kg, conv_b=conv_b, dtb=dtb, alog=alog, dsk=dsk, ng=ng, g_mlp=g_mlp,
                          w_in=w_in, w_out=w_out, w_up=w_up, w_down=w_down))
        x = x2
    dx, loss = loss_head(x, target)
    grads = [None] * DEPTH
    dbands = [None] * DEPTH
    tok = 0.0
    for l in reversed(range(DEPTH)):
        s = saved[l]
        g_mlp = s["g_mlp"] + tok
        dx1, du, dg_mlp = mlp_bwd_data(dx, s["u"], s["x1"], g_mlp, s["w_up"], s["w_down"])
        dw_up, dw_down, dw_up16, dw_down16 = mlp_bwd_weights(dx, s["u"], du, s["x1"], g_mlp)
        tok = prov.grads(("mlp", l), dict(w_up=(dw_up, dw_up16), w_down=(dw_down, dw_down16)), dw_down)
        dattn, dssm, dw_out, dw_out16 = out_bwd(dx1, s["attn"], s["ssm"], s["w_out"])
        dact, ddt, dz, dng, dpar = ssd_bwd_g(s["act"], s["proj"], s["ypre"], s["states"], dssm, s["dtb"] + tok, s["alog"],
                                           s["dsk"], s["ng"])
        conv_b = s["conv_b"] + prov.stage(("bwd_mid", l), dact)
        dxbc, dconv_w, dconv_b = conv_bwd(s["proj"], dact, w["conv_w"][l], conv_b)
        dq, dk, dv, dband, dsink, dqg, dkg = attn_bwd_t(s["proj"], dattn, s["qg"], s["kg"], w["sinks"][l], bias)
        dbands[l] = dband
        g_mix = s["g_mix"]
        if l == 0:
            d_rel = bias_bwd(dbands[0], dbands[1], bucket)
            g_mix = g_mix + 0.0 * d_rel[:1, :1]
        dx, dw_in, dg_mix = in_bwd(dq, dz, dxbc, dk, dv, ddt, s["x"], g_mix, s["w_in"], dx1)
        tok = prov.grads(("mix", l), dict(w_in=split_w_in_grad(dw_in), w_out=(dw_out, dw_out16)), dx)
        grads[l] = dict(mix_norm_g=dg_mix[0], q_gain=dqg[:, 0], k_gain=dkg[0], sinks=dsink[:, 0],
                        conv_w=dconv_w, conv_b=dconv_b[0], dt_bias=dpar[0, :SSM_HEADS], a_log=dpar[1, :SSM_HEADS],
                        d_skip=dpar[2, :SSM_HEADS], ssm_norm_g=dng[0], mlp_norm_g=dg_mlp[0])
    out = {k: jnp.stack([grads[l][k] for l in range(DEPTH)]) for k in grads[0]}
    out["rel_bias"] = d_rel[:, :N_Q_HEADS]
    return loss, dx, out, tok


MESH = pl.DeviceIdType.MESH
HBM = pl.BlockSpec(memory_space=pltpu.HBM)
N_PEER_CHIPS = N_CHIPS - 1
N_DEVICES = 8


def _coords():
    return lax.axis_index("x"), lax.axis_index("y"), lax.axis_index("c")


def _peer_chips(x, y):
    return [(1 - x, y), (x, 1 - y), (1 - x, 1 - y)]


def _remote(src, dst, send_sem, recv_sem, device):
    return pltpu.make_async_remote_copy(src_ref=src, dst_ref=dst, send_sem=send_sem, recv_sem=recv_sem,
                                        device_id=device, device_id_type=MESH)


SEM = pl.BlockSpec(memory_space=pltpu.SEMAPHORE)
ANY = pl.BlockSpec(memory_space=pl.ANY)
DATAFLOW = pltpu.SideEffectType.DATAFLOW_SIDE_EFFECTING


def _gather_copies(kind, src_refs, land_refs, ssem, rsem):
    x, y, c = _coords()
    k_me = 2 * x + y
    n = len(land_refs)
    cps = []
    for p, land in enumerate(land_refs):
        hr = land.shape[1] // 2
        rows = pl.ds(c * hr, hr)
        for j, chip in enumerate(_peer_chips(x, y)):
            i = 3 * p + j
            if kind == "ici":
                cps.append(_remote(src_refs[p].at[rows, :], land.at[k_me, rows, :], ssem.at[i], rsem.at[i], (*chip, c)))
            else:
                got = land.at[2 * chip[0] + chip[1], rows, :]
                cps.append(_remote(got, got, ssem.at[i], rsem.at[i], (x, y, 1 - c)))
        if kind == "relay":
            cps.append(_remote(src_refs[p], land.at[k_me], ssem.at[3 * n + p], rsem.at[3 * n + p], (x, y, 1 - c)))
    return cps


def gather_now(srcs, conv):
    n = len(srcs)

    def body(*refs):
        src_refs, conv_ref = refs[:n], refs[n]
        lands, gconv = refs[n + 1:2 * n + 1], refs[2 * n + 1]
        ssem, rsem, fsem, frsem, csem, crsem = refs[2 * n + 2:]
        x, y, c = _coords()
        k_me = 2 * x + y
        targets = [(*chip, c) for chip in _peer_chips(x, y)] + [(x, y, 1 - c)]
        ici = _gather_copies("ici", src_refs, lands, ssem, rsem)
        relay = _gather_copies("relay", src_refs, lands, fsem, frsem)
        passed = [cp for i, cp in enumerate(relay) if i % 4 != 3]
        own = relay[3::4]
        conv_cps = [_remote(conv_ref, gconv.at[k_me], csem.at[j], crsem.at[j], t) for j, t in enumerate(targets)]
        for cp in ici + conv_cps + own:
            cp.start()
        for cp, fw in zip(ici, passed):
            cp.wait_recv()
            fw.start()
        for cp in conv_cps + relay:
            cp.wait_recv()
        for cp in ici + relay + conv_cps:
            cp.wait_send()

    out_shape = [jax.ShapeDtypeStruct((N_CHIPS,) + s.shape, s.dtype) for s in srcs]
    out_shape.append(jax.ShapeDtypeStruct((N_CHIPS,) + conv.shape, conv.dtype))
    sems = lambda k: pltpu.SemaphoreType.DMA((k,))
    return pl.pallas_call(
        body, name="gather_now", out_shape=out_shape, in_specs=[HBM] * (n + 1), out_specs=[HBM] * (n + 1),
        scratch_shapes=[sems(3 * n), sems(3 * n), sems(4 * n), sems(4 * n), sems(N_CHIPS), sems(N_CHIPS)],
    )(*srcs, conv)


def _gather_maker(kind, n_src):
    def make(refs, ssem, rsem):
        cps = _gather_copies(kind, refs[:n_src], refs[n_src:], ssem, rsem)
        return cps, cps
    return make


def _scatter_maker(n):
    def make(refs, ssem, rsem):
        x, y, c = _coords()
        k_me = 2 * x + y
        sends, arrivals = [], []
        for p in range(n):
            src, land = refs[p], refs[n + p]
            sends.append(_remote(src.at[k_me, 1 - c], land.at[0], ssem.at[7 * p], rsem.at[7 * p], (x, y, 1 - c)))
            for j, chip in enumerate(_peer_chips(x, y)):
                for cc in range(2):
                    sends.append(_remote(src.at[2 * chip[0] + chip[1], cc], land.at[1 + 2 * j + c],
                                         ssem.at[7 * p + 1 + 2 * j + cc], rsem.at[7 * p + 1 + 2 * j + c], (*chip, cc)))
            for s in range(7):
                arrivals.append(_remote(land.at[s], land.at[s], ssem.at[7 * p + s], rsem.at[7 * p + s], (x, y, 1 - c)))
        return sends, arrivals
    return make


def _share_maker(n):
    def make(refs, ssem, rsem):
        x, y, c = _coords()
        sends = [_remote(refs[p].at[c], refs[p].at[c], ssem.at[p], rsem.at[p], (x, y, 1 - c)) for p in range(n)]
        arrivals = [_remote(refs[p].at[1 - c], refs[p].at[1 - c], ssem.at[p], rsem.at[p], (x, y, 1 - c)) for p in range(n)]
        return sends, arrivals
    return make


def split_start(name, make, n_sems, operands, after):
    n = len(operands)

    def body(*refs):
        ssem, rsem, token = refs[n + 1], refs[n + 2], refs[-1]
        for cp in make(refs[:n], ssem, rsem)[0]:
            cp.start()
        token[...] = jnp.zeros_like(token)

    ops = [pltpu.with_memory_space_constraint(a, pltpu.HBM) for a in operands]
    outs = pl.pallas_call(
        body, name=name,
        out_shape=(pltpu.SemaphoreType.DMA((n_sems,)), pltpu.SemaphoreType.DMA((n_sems,)),
                   *[pltpu.HBM(a.shape, a.dtype) for a in ops], jax.ShapeDtypeStruct((8, LANE), f32)),
        in_specs=[HBM] * n + [ANY], out_specs=(SEM, SEM, *[HBM] * n, pl.BlockSpec(memory_space=pltpu.VMEM)),
        input_output_aliases={i: 2 + i for i in range(n)},
        compiler_params=pltpu.CompilerParams(has_side_effects=DATAFLOW),
    )(*ops, after)
    return dict(name=name, make=make, ssem=outs[0], rsem=outs[1], operands=outs[2:2 + n], token=outs[-1][0, 0])


def split_wait(handle, after):
    n = len(handle["operands"])

    def body(*refs):
        sends, arrivals = handle["make"](refs[:n], refs[n], refs[n + 1])
        for cp in sends:
            cp.wait_send()
        for cp in arrivals:
            cp.wait_recv()

    outs = pl.pallas_call(
        body, name=handle["name"].replace("start", "wait"),
        out_shape=tuple(pltpu.HBM(a.shape, a.dtype) for a in handle["operands"]),
        in_specs=[HBM] * n + [SEM, SEM, ANY], out_specs=tuple([HBM] * n),
        input_output_aliases={i: i for i in range(n)},
        compiler_params=pltpu.CompilerParams(has_side_effects=DATAFLOW),
    )(*handle["operands"], handle["ssem"], handle["rsem"], after)
    return list(outs)


def piece_sum(g, recv, kc_arr):
    _, _, rb, cc = g.shape
    tr = min(256, rb)

    def body(kc_ref, g_ref, r_ref, o_ref):
        acc = g_ref[...]
        for s in range(7):
            acc = acc + r_ref[s].astype(f32)
        o_ref[...] = acc

    return pl.pallas_call(
        body, name="piece_sum",
        grid_spec=pltpu.PrefetchScalarGridSpec(
            num_scalar_prefetch=1, grid=(rb // tr,),
            in_specs=[pl.BlockSpec((None, None, tr, cc), lambda r, kc: (kc[0], kc[1], r, 0)),
                      pl.BlockSpec((7, tr, cc), lambda r, kc: (0, r, 0))],
            out_specs=pl.BlockSpec((None, tr, cc), lambda r, kc: (kc[1], r, 0))),
        out_shape=jax.ShapeDtypeStruct((2, rb, cc), f32),
        compiler_params=_params("arbitrary"),
    )(kc_arr, g, recv)


def small_all_reduce(vec):
    def body(v_ref, o_ref, gat, ssem, rsem):
        x, y, c = _coords()
        me = 4 * x + 2 * y + c
        gat[me] = v_ref[...]
        sends = []
        for t in range(1, N_DEVICES):
            peer = (x ^ (t >> 2), y ^ ((t >> 1) & 1), c ^ (t & 1))
            cp = _remote(v_ref, gat.at[me], ssem.at[t - 1], rsem.at[t - 1], peer)
            cp.start()
            sends.append(cp)
        for t in range(1, N_DEVICES):
            peer = (x ^ (t >> 2), y ^ ((t >> 1) & 1), c ^ (t & 1))
            slot = gat.at[4 * peer[0] + 2 * peer[1] + peer[2]]
            _remote(slot, slot, ssem.at[t - 1], rsem.at[t - 1], peer).wait_recv()
        for cp in sends:
            cp.wait_send()
        acc = gat[0]
        for d in range(1, N_DEVICES):
            acc = acc + gat[d]
        o_ref[...] = acc

    return pl.pallas_call(
        body, name="small_all_reduce", out_shape=jax.ShapeDtypeStruct(vec.shape, vec.dtype),
        in_specs=[pl.BlockSpec(memory_space=pltpu.VMEM)], out_specs=pl.BlockSpec(memory_space=pltpu.VMEM),
        scratch_shapes=[pltpu.VMEM((N_DEVICES,) + vec.shape, vec.dtype), pltpu.SemaphoreType.DMA((N_DEVICES - 1,)),
                        pltpu.SemaphoreType.DMA((N_DEVICES - 1,))],
    )(vec)


def _adamw_math(w, g, m, v):
    m_new = ADAM_B1 * m + (1.0 - ADAM_B1) * g
    v_new = ADAM_B2 * v + (1.0 - ADAM_B2) * jnp.square(g)
    m_hat = m_new / (1.0 - ADAM_B1 ** ADAM_STEP)
    v_hat = v_new / (1.0 - ADAM_B2 ** ADAM_STEP)
    delta = -ADAM_LR * (m_hat / (jnp.sqrt(v_hat) + ADAM_EPS) + ADAM_WD * w)
    return delta, m_new, v_new


def adamw_shard(w, g0, g1, m, v):
    depth, rows, cols = w.shape
    half = rows // 2
    tr = min(256, half)
    nr = half // tr

    def body(w_ref, g0_ref, g1_ref, m_ref, v_ref, go_ref, d_ref, nm_ref, nv_ref):
        gv = jnp.where(pl.program_id(0) == 0, g0_ref[...], g1_ref[...])
        go_ref[...] = gv
        d_ref[...], nm_ref[...], nv_ref[...] = _adamw_math(w_ref[...], gv, m_ref[...], v_ref[...])

    spec = pl.BlockSpec((None, tr, cols), lambda l, h, r: (l, h * nr + r, 0))
    g0spec = pl.BlockSpec((None, tr, cols), lambda l, h, r: (jnp.where(l == 0, h, 1), jnp.where(l == 0, r, nr - 1), 0))
    g1spec = pl.BlockSpec((None, tr, cols), lambda l, h, r: (jnp.where(l == 1, h, 0), jnp.where(l == 1, r, 0), 0))
    return pl.pallas_call(
        body, name="adamw_shard", grid=(depth, 2, nr), in_specs=[spec, g0spec, g1spec, spec, spec], out_specs=[spec] * 4,
        out_shape=[jax.ShapeDtypeStruct(w.shape, f32)] * 4,
        compiler_params=_params("arbitrary", "arbitrary", "arbitrary"),
    )(w, g0, g1, m, v)


def adamw_small(w, g, m, v):
    def body(w_ref, g_ref, m_ref, v_ref, d_ref, nm_ref, nv_ref):
        d_ref[...], nm_ref[...], nv_ref[...] = _adamw_math(w_ref[...], g_ref[...], m_ref[...], v_ref[...])

    return pl.pallas_call(
        body, name="adamw_small", out_shape=[jax.ShapeDtypeStruct(w.shape, f32)] * 3,
    )(w, g, m, v)


WEIGHTS = ("mix_norm_g", "w_in", "q_gain", "k_gain", "sinks", "rel_bias", "conv_w", "conv_b", "dt_bias", "a_log", "d_skip",
           "ssm_norm_g", "w_out", "mlp_norm_g", "w_up", "w_down")
BIG = ("w_in", "w_out", "w_up", "w_down")
SMALL = tuple(n for n in WEIGHTS if n not in BIG)
PACK_COLS = 1024
PACK_ROWS = 16


def _pack(named, last=None):
    flat = jnp.concatenate([named[n].reshape(-1) for n in SMALL])
    tail = jnp.zeros((1,), f32) if last is None else last.reshape(1)
    pad = jnp.zeros((PACK_ROWS * PACK_COLS - flat.shape[0] - 1,), f32)
    return jnp.concatenate([flat, pad, tail]).reshape(PACK_ROWS, PACK_COLS)


def _unpack(buf, shapes):
    flat = buf.reshape(-1)
    out, at = {}, 0
    for n in SMALL:
        size = int(np.prod(shapes[n]))
        out[n] = flat[at:at + size].reshape(shapes[n])
        at += size
    return out


class _Exchange:
    GROUPS = {"A": (("w_up", 0), ("w_down", 0)), "B": (("w_in", 1), ("w_out", 1)), "C": (("w_up", 1), ("w_down", 1))}
    ICI_AT = {("mid", 0): "B", ("pre_out", 0): "C"}
    RELAY_AT = {("pre_out", 0): "A", ("pre_mlp", 0): "B", ("mid", 1): "C"}
    LAST = ("mix", 0)
    IN_FLIGHT = 2

    def __init__(self, wts, k_me, kc_arr):
        self.wts, self.k_me, self.kc_arr = wts, k_me, kc_arr
        self.own = {(n, l): wts[n][l].astype(bf16) for n in BIG for l in range(DEPTH)}
        now = gather_now([self.own["w_in", 0], self.own["w_out", 0]], wts["conv_w"])
        self.ready = {("w_in", 0): now[0], ("w_out", 0): now[1]}
        self.conv_w = jnp.transpose(now[2], (1, 2, 0, 3)).reshape(DEPTH, CONV_WIDTH, D_CONV)
        self.ici, self.relay = {}, {}
        self.gview, self.scatter, self.share, self.reduced = {}, [], [], {}
        self._start_ici("A", now[2])

    def _start_ici(self, g, after):
        srcs = [self.own[p] for p in self.GROUPS[g]]
        lands = [lax.empty((N_CHIPS,) + s.shape, s.dtype) for s in srcs]
        self.ici[g] = split_start("gather%s_ici_start" % g, _gather_maker("ici", len(srcs)), 3 * len(srcs), srcs + lands,
                                  after)
        return self.ici[g]["token"]

    def stage(self, name, after):
        if name == ("begin", 0):
            return self.ici["A"]["token"]
        tok = 0.0
        g = self.RELAY_AT.get(name)
        if g is not None:
            n = len(self.GROUPS[g])
            self.relay[g] = split_start("gather%s_relay_start" % g, _gather_maker("relay", n), 4 * n,
                                        split_wait(self.ici[g], after), after)
            tok = self.relay[g]["token"]
        if name in self.ICI_AT:
            tok = tok + self._start_ici(self.ICI_AT[name], after)
        return tok

    def _get(self, piece, after):
        if piece not in self.ready:
            g = [k for k, pieces in self.GROUPS.items() if piece in pieces][0]
            lands = split_wait(self.relay[g], after)[len(self.GROUPS[g]):]
            self.ready.update(zip(self.GROUPS[g], lands))
        return self.ready[piece]

    def w_in(self, l, after):
        return align_w_in(self._get(("w_in", l), after))

    def w_out(self, l, after):
        return self._get(("w_out", l), after).reshape(D_MODEL, D_MODEL)

    def mlp(self, l, after):
        return self._get(("w_up", l), after), self._get(("w_down", l), after)

    def _view(self, n, g):
        _, rows, cols = self.wts[n].shape
        return g.reshape(N_CHIPS, 2, rows // 2, cols)

    def grads(self, name, arrays, after):
        if name == self.LAST:
            self.held = (name, arrays)
            return 0.0
        return self._scatter(name, arrays, after) + self._advance(after, self.IN_FLIGHT)

    def flush(self, after):
        return self._scatter(*self.held, after) + self._advance(after, self.IN_FLIGHT)

    def _scatter(self, name, arrays, after):
        pieces = [(n, name[1]) for n in arrays]
        views = [self._view(n, g) for n, (g, _) in arrays.items()]
        sends = [g16.reshape(v.shape) for v, (_, g16) in zip(views, arrays.values())]
        self.gview.update(zip(pieces, views))
        lands = [lax.empty((7,) + v.shape[2:], bf16) for v in views]
        h = split_start("scatter_%s%d_start" % name, _scatter_maker(len(views)), 7 * len(views), sends + lands, after)
        self.scatter.append((pieces, h))
        return h["token"]

    def _take_share(self, after):
        pieces, h = self.share.pop(0)
        self.reduced.update(zip(pieces, split_wait(h, after)))

    def _take_scatter(self, after):
        pieces, h = self.scatter.pop(0)
        lands = split_wait(h, after)[len(pieces):]
        sums = [piece_sum(self.gview[p], land, self.kc_arr) for p, land in zip(pieces, lands)]
        hs = split_start(h["name"].replace("scatter", "share"), _share_maker(len(sums)), len(sums), sums, after)
        self.share.append((pieces, hs))
        return hs["token"]

    def _advance(self, after, newest):
        if self.share:
            self._take_share(after)
        return self._take_scatter(after) if len(self.scatter) > newest else 0.0

    def reduced_grads(self, names, after):
        want = [(n, l) for n in names for l in range(DEPTH)]
        while not all(p in self.reduced for p in want):
            if any(p in pieces for p in want for pieces, _ in self.share):
                self._take_share(after)
            else:
                self._take_scatter(after)
        return {n: [self.reduced[n, l] for l in range(DEPTH)] for n in names}


def kernel(x, mix_norm_g, w_in, q_gain, k_gain, sinks, rel_bias, conv_w, conv_b, dt_bias, a_log, d_skip, ssm_norm_g, w_out, mlp_norm_g, w_up, w_down, loss_target, m_mix_norm_g, m_w_in, m_q_gain, m_k_gain, m_sinks, m_rel_bias, m_conv_w, m_conv_b, m_dt_bias, m_a_log, m_d_skip, m_ssm_norm_g, m_w_out, m_mlp_norm_g, m_w_up, m_w_down, v_mix_norm_g, v_w_in, v_q_gain, v_k_gain, v_sinks, v_rel_bias, v_conv_w, v_conv_b, v_dt_bias, v_a_log, v_d_skip, v_ssm_norm_g, v_w_out, v_mlp_norm_g, v_w_up, v_w_down):
    wts = dict(mix_norm_g=mix_norm_g, w_in=w_in, q_gain=q_gain, k_gain=k_gain, sinks=sinks, rel_bias=rel_bias, conv_w=conv_w,
               conv_b=conv_b, dt_bias=dt_bias, a_log=a_log, d_skip=d_skip, ssm_norm_g=ssm_norm_g, w_out=w_out,
               mlp_norm_g=mlp_norm_g, w_up=w_up, w_down=w_down)
    mom = dict(mix_norm_g=m_mix_norm_g, w_in=m_w_in, q_gain=m_q_gain, k_gain=m_k_gain, sinks=m_sinks, rel_bias=m_rel_bias,
               conv_w=m_conv_w, conv_b=m_conv_b, dt_bias=m_dt_bias, a_log=m_a_log, d_skip=m_d_skip, ssm_norm_g=m_ssm_norm_g,
               w_out=m_w_out, mlp_norm_g=m_mlp_norm_g, w_up=m_w_up, w_down=m_w_down)
    var = dict(mix_norm_g=v_mix_norm_g, w_in=v_w_in, q_gain=v_q_gain, k_gain=v_k_gain, sinks=v_sinks, rel_bias=v_rel_bias,
               conv_w=v_conv_w, conv_b=v_conv_b, dt_bias=v_dt_bias, a_log=v_a_log, d_skip=v_d_skip, ssm_norm_g=v_ssm_norm_g,
               w_out=v_w_out, mlp_norm_g=v_mlp_norm_g, w_up=v_w_up, w_down=v_w_down)
    xi, yi, ci = _coords()
    k_me = 2 * xi + yi
    kc_arr = jnp.stack([k_me, ci]).astype(jnp.int32)

    prov = _Exchange(wts, k_me, kc_arr)
    small_w = {n: wts[n] for n in SMALL}
    small_w["conv_w"] = prov.conv_w
    loss, dx, grads, tok = local_step(x[0], loss_target[0], small_w, prov)

    small_shapes = {n: grads[n].shape for n in SMALL}
    small_sum = small_all_reduce(_pack(grads, loss) + tok)
    loss = small_sum[PACK_ROWS - 1, PACK_COLS - 1]
    tok = prov.flush(small_sum)
    small = _unpack(small_sum, small_shapes)
    cols = conv_w.shape[-1]
    small["conv_w"] = lax.dynamic_slice_in_dim(small["conv_w"], k_me * cols, cols, axis=2)
    g_out_d, d_out_d, m_out_d, v_out_d = {}, {}, {}, {}
    shard_shapes = {n: wts[n].shape for n in SMALL}
    d, nm, nv = adamw_small(_pack(wts), _pack(small) + tok, _pack(mom), _pack(var))
    for dst, buf in ((d_out_d, d), (m_out_d, nm), (v_out_d, nv)):
        dst.update(_unpack(buf, shard_shapes))
    g_out_d.update(small)

    after = d
    for names in (("w_up", "w_down"), ("w_in", "w_out")):
        for n, (g0, g1) in prov.reduced_grads(names, after).items():
            g_out_d[n], d_out_d[n], m_out_d[n], v_out_d[n] = adamw_shard(wts[n], g0, g1, mom[n], var[n])
            after = d_out_d[n]

    return (loss, dx[None], *[g_out_d[n] for n in WEIGHTS], *[d_out_d[n] for n in WEIGHTS],
            *[m_out_d[n] for n in WEIGHTS], *[v_out_d[n] for n in WEIGHTS])
```

```python
import numpy as np
import jax
import jax.numpy as jnp
from jax import lax
from jax.experimental import pallas as pl
from jax.experimental.pallas import tpu as pltpu

f32 = jnp.float32
bf16 = jnp.bfloat16

SEQ = 2048
D_MODEL = 1024
DEPTH = 2
HEAD_DIM = 64
N_Q_HEADS = 8
N_KV_HEADS = 2
Q_PER_KV = N_Q_HEADS // N_KV_HEADS
BLOCK = 128
N_BLOCKS = SEQ // BLOCK
N_BUCKETS = 32
MAX_DISTANCE = 128
SSM_HEADS = 8
SSM_HEAD_DIM = 64
SSM_GROUPS = 2
HEADS_PER_GROUP = SSM_HEADS // SSM_GROUPS
SSM_STATE = 128
CONV_WIDTH = 4
CHUNK = 128
N_CHUNKS = SEQ // CHUNK
D_FF = 4 * D_MODEL
D_ATTN = N_Q_HEADS * HEAD_DIM
D_KV = N_KV_HEADS * HEAD_DIM
D_SSM = SSM_HEADS * SSM_HEAD_DIM
D_BC = SSM_GROUPS * SSM_STATE
D_CONV = D_SSM + 2 * D_BC
D_IN = D_ATTN + 2 * D_KV + D_SSM + D_CONV + SSM_HEADS
EPS = 1e-6
NEG = -1e30
N_CHIPS = 4
FF_TILE = D_FF // N_CHIPS

LANE = 128
PW = D_ATTN + D_SSM + D_CONV + 2 * D_KV + LANE
OFF_Q, OFF_Z, OFF_X, OFF_K, OFF_V, OFF_DT = 0, 512, 1024, 2048, 2176, 2304

ADAM_LR = 0.001
ADAM_B1 = 0.9
ADAM_B2 = 0.999
ADAM_EPS = 1e-08
ADAM_WD = 0.01
ADAM_STEP = 10

VMEM_LIMIT = 56 * 1024 * 1024


def _params(*sem):
    return pltpu.CompilerParams(dimension_semantics=tuple(sem), vmem_limit_bytes=VMEM_LIMIT)


def _bdot(a, b):
    return jnp.dot(a.astype(bf16), b.astype(bf16), preferred_element_type=f32)


def _bdot_nt(a, b):
    return lax.dot_general(a.astype(bf16), b.astype(bf16), (((1,), (1,)), ((), ())), preferred_element_type=f32)


def _bdot_tn(a, b):
    return lax.dot_general(a.astype(bf16), b.astype(bf16), (((0,), (0,)), ((), ())), preferred_element_type=f32)


def _hdot(a, b):
    return jnp.dot(a, b, precision=lax.Precision.HIGHEST, preferred_element_type=f32)


def _sigmoid(x):
    return 1.0 / (1.0 + jnp.exp(-x))


def _softplus(x):
    return jnp.maximum(x, 0.0) + jnp.log1p(jnp.exp(-jnp.abs(x)))


def _rms(x):
    return lax.rsqrt(jnp.mean(x * x, axis=-1, keepdims=True) + EPS)


def _rms_bwd(dy, xhat, r, g):
    t = dy * g
    return r * (t - xhat * jnp.mean(t * xhat, axis=-1, keepdims=True))


def _full(shape):
    return pl.BlockSpec(shape, lambda *_: (0,) * len(shape))


def _bucket_table():
    qi = np.arange(BLOCK)[:, None]
    kj = np.arange(2 * BLOCK)[None, :]
    dist = qi + BLOCK - kj
    ok = (dist >= 0) & (dist < 128)
    d = np.clip(dist, 0, None)
    max_exact = N_BUCKETS // 2
    d_f = np.maximum(d, 1).astype(np.float32)
    large = max_exact + (np.log(d_f / np.float32(max_exact)) / np.float32(np.log(MAX_DISTANCE / max_exact))
                         * np.float32(N_BUCKETS - max_exact)).astype(np.int32)
    large = np.minimum(large, N_BUCKETS - 1)
    bucket = np.where(d < max_exact, d, large)
    return np.where(ok, bucket, -1).astype(np.int32)


def bias_build(rel_bias, bucket):
    def body(rel_ref, bkt_ref, o_ref):
        bkt = bkt_ref[...]
        for h in range(N_Q_HEADS):
            acc = jnp.where(bkt < 0, NEG, 0.0).astype(f32)
            for b in range(N_BUCKETS):
                acc = acc + jnp.where(bkt == b, rel_ref[b, h], 0.0)
            o_ref[h] = acc

    return pl.pallas_call(
        body, name="bias_build", out_shape=jax.ShapeDtypeStruct((N_Q_HEADS,) + bucket.shape, f32),
        in_specs=[pl.BlockSpec(memory_space=pltpu.SMEM), pl.BlockSpec(memory_space=pltpu.VMEM)],
        out_specs=pl.BlockSpec(memory_space=pltpu.VMEM),
    )(rel_bias, bucket)


def bias_bwd(dband0, dband1, bucket):
    def body(d0_ref, d1_ref, bkt_ref, o_ref):
        bkt = bkt_ref[...]
        o_ref[...] = jnp.zeros_like(o_ref)
        for h in range(N_Q_HEADS):
            d = d0_ref[h] + d1_ref[h]
            for b in range(N_BUCKETS):
                part = jnp.sum(jnp.where(bkt == b, d, 0.0), axis=1, keepdims=True)
                o_ref[b:b + 1, h:h + 1] = jnp.sum(part, axis=0, keepdims=True)

    return pl.pallas_call(
        body, name="bias_bwd", out_shape=jax.ShapeDtypeStruct((N_BUCKETS, LANE), f32),
    )(dband0, dband1, bucket)


W_IN_SHARD = D_IN // N_CHIPS
_ALIGNED_PIECES = ((0, 0, 512), (1, 190, 578), (2, 0, 124), (2, 124, 578), (3, 0, 570), (0, 512, 578), (1, 0, 62),
                   (1, 62, 190), (3, 570, 578))
_SHARD_PIECES = (((0, 512), (2048, 2114)), ((2114, 2176), (2176, 2304), (512, 900)), ((900, 1024), (1024, 1478)),
                 ((1478, 2048), (2304, 2312)))


def align_w_in(shards, tr=256):
    def body(s_ref, o_ref):
        parts = [s_ref[k, :, a:b] for k, a, b in _ALIGNED_PIECES]
        parts.append(jnp.zeros((tr, LANE - SSM_HEADS), s_ref.dtype))
        o_ref[...] = jnp.concatenate(parts, axis=-1)

    return pl.pallas_call(
        body, name="align_w_in", grid=(D_MODEL // tr,),
        in_specs=[pl.BlockSpec((N_CHIPS, tr, W_IN_SHARD), lambda i: (0, i, 0))],
        out_specs=pl.BlockSpec((tr, PW), lambda i: (i, 0)),
        out_shape=jax.ShapeDtypeStruct((D_MODEL, PW), shards.dtype),
        compiler_params=_params("arbitrary"),
    )(shards)


def split_w_in_grad(dw, tr=256):
    def body(d_ref, o_ref, o16_ref):
        for k, pieces in enumerate(_SHARD_PIECES):
            part = jnp.concatenate([d_ref[:, a:b] for a, b in pieces], axis=-1)
            o_ref[k] = part
            o16_ref[k] = part.astype(bf16)

    spec = pl.BlockSpec((N_CHIPS, tr, W_IN_SHARD), lambda i: (0, i, 0))
    return pl.pallas_call(
        body, name="split_w_in_grad", grid=(D_MODEL // tr,),
        in_specs=[pl.BlockSpec((tr, PW), lambda i: (i, 0))], out_specs=[spec, spec],
        out_shape=[jax.ShapeDtypeStruct((N_CHIPS, D_MODEL, W_IN_SHARD), f32),
                   jax.ShapeDtypeStruct((N_CHIPS, D_MODEL, W_IN_SHARD), bf16)],
        compiler_params=_params("arbitrary"),
    )(dw)

def in_fwd(x, g, w, tm=512):
    def body(x_ref, g_ref, w_ref, o_ref):
        xv = x_ref[...]
        h = xv * _rms(xv) * g_ref[...]
        o_ref[...] = _bdot(h, w_ref[...])

    return pl.pallas_call(
        body, name="in_fwd", grid=(SEQ // tm,),
        in_specs=[pl.BlockSpec((tm, D_MODEL), lambda i: (i, 0)), _full((1, D_MODEL)), _resident((D_MODEL, PW))],
        out_specs=pl.BlockSpec((tm, PW), lambda i: (i, 0)),
        out_shape=jax.ShapeDtypeStruct((SEQ, PW), f32),
        compiler_params=_params("arbitrary"),
    )(x, g, w)


def _resident(shape):
    return pl.BlockSpec(shape, lambda *_: (0,) * len(shape), pipeline_mode=pl.Buffered(1))


def in_bwd(dq, dz, dxbc, dk, dv, ddt, x, g, w, dres, tm=512):
    def body(dq_ref, dz_ref, dx_ref, dk_ref, dv_ref, ddt_ref, x_ref, g_ref, w_ref, dres_ref, o_ref, dw_ref, dg_ref):
        i = pl.program_id(0)

        @pl.when(i == 0)
        def _():
            dw_ref[...] = jnp.zeros_like(dw_ref)
            dg_ref[...] = jnp.zeros_like(dg_ref)

        dproj = jnp.concatenate([dq_ref[...], dz_ref[...], dx_ref[...], dk_ref[...], dv_ref[...], ddt_ref[...]],
                                axis=-1).astype(bf16)
        xv = x_ref[...]
        r = _rms(xv)
        xhat = xv * r
        gv = g_ref[...]
        h = xhat * gv
        dw_ref[...] += _bdot_tn(h, dproj)
        dh = _bdot_nt(dproj, w_ref[...])
        dg_ref[...] += jnp.sum(dh * xhat, axis=0, keepdims=True)
        o_ref[...] = dres_ref[...] + _rms_bwd(dh, xhat, r, gv)

    tok = lambda w_: pl.BlockSpec((tm, w_), lambda i: (i, 0))
    return pl.pallas_call(
        body, name="in_bwd", grid=(SEQ // tm,),
        in_specs=[tok(D_ATTN), tok(D_SSM), tok(D_CONV), tok(D_KV), tok(D_KV), tok(LANE), tok(D_MODEL),
                  _full((1, D_MODEL)), _resident((D_MODEL, PW)), tok(D_MODEL)],
        out_specs=[tok(D_MODEL), _resident((D_MODEL, PW)), _full((1, D_MODEL))],
        out_shape=[jax.ShapeDtypeStruct((SEQ, D_MODEL), f32), jax.ShapeDtypeStruct((D_MODEL, PW), f32),
                   jax.ShapeDtypeStruct((1, D_MODEL), f32)],
        compiler_params=_params("arbitrary"),
    )(dq, dz, dxbc, dk, dv, ddt, x, g, w, dres)


def _attn_softmax_t(qk, bias_t, sink, first, key_row):
    s = qk * (HEAD_DIM ** -0.5) + bias_t
    s = jnp.where(jnp.logical_and(first, key_row < BLOCK), NEG, s)
    m = jnp.maximum(jnp.max(s, axis=0, keepdims=True), sink)
    p = jnp.exp(s - m)
    psink = jnp.exp(sink - m)
    inv = 1.0 / (jnp.sum(p, axis=0, keepdims=True) + psink)
    return p * inv, psink * inv


def _rms_t(x_t):
    return lax.rsqrt(jnp.mean(x_t * x_t, axis=0, keepdims=True) + EPS)


def attn_fwd_t(proj, q_gain_col, k_gain, sinks, bias_t):
    kcol, vcol = OFF_K // D_KV, OFF_V // D_KV

    def body(q_ref, kc_ref, kp_ref, vc_ref, vp_ref, qg_ref, kg_ref, sink_ref, bias_ref, o_ref, ot_scr):
        n = pl.program_id(0)
        first = n == 0
        key_row = lax.broadcasted_iota(jnp.int32, (2 * BLOCK, BLOCK), 0)
        k2 = jnp.concatenate([kp_ref[...], kc_ref[...]], axis=0)
        v_t = jnp.concatenate([vp_ref[...], vc_ref[...]], axis=0).T
        q_t = q_ref[...].T
        qg = jnp.broadcast_to(qg_ref[...], (HEAD_DIM, BLOCK))
        kg = kg_ref[...]
        for hk in range(N_KV_HEADS):
            sl = slice(hk * HEAD_DIM, (hk + 1) * HEAD_DIM)
            kk = k2[:, sl]
            kn = (kk * _rms(kk) * kg).astype(bf16)
            vt = v_t[sl, :].astype(bf16)
            heads = range(hk * Q_PER_KV, (hk + 1) * Q_PER_KV)
            qns = []
            for h in heads:
                qh = q_t[h * HEAD_DIM:(h + 1) * HEAD_DIM, :]
                qns.append(qh * _rms_t(qh) * qg)
            scores = [_bdot(kn, qn) for qn in qns]
            for h, s in zip(heads, scores):
                p, _ = _attn_softmax_t(s, bias_ref[h], sink_ref[h], first, key_row)
                ot_scr[h * HEAD_DIM:(h + 1) * HEAD_DIM, :] = _bdot(vt, p)
        o_ref[...] = ot_scr[...].T

    prev = lambda n: jnp.maximum(n - 1, 0)
    return pl.pallas_call(
        body, name="attn_fwd", grid=(N_BLOCKS,),
        in_specs=[pl.BlockSpec((BLOCK, D_ATTN), lambda n: (n, 0)),
                  pl.BlockSpec((BLOCK, D_KV), lambda n: (n, kcol)), pl.BlockSpec((BLOCK, D_KV), lambda n: (prev(n), kcol)),
                  pl.BlockSpec((BLOCK, D_KV), lambda n: (n, vcol)), pl.BlockSpec((BLOCK, D_KV), lambda n: (prev(n), vcol)),
                  _full((HEAD_DIM, 1)), _full((1, HEAD_DIM)), pl.BlockSpec(memory_space=pltpu.SMEM),
                  _full((N_Q_HEADS, 2 * BLOCK, BLOCK))],
        out_specs=pl.BlockSpec((BLOCK, D_ATTN), lambda n: (n, 0)),
        out_shape=jax.ShapeDtypeStruct((SEQ, D_ATTN), f32),
        scratch_shapes=[pltpu.VMEM((D_ATTN, BLOCK), f32)],
        compiler_params=_params("arbitrary"),
    )(proj, proj, proj, proj, proj, q_gain_col, k_gain, sinks, bias_t)


def attn_bwd_t(proj, d_out, q_gain_col, k_gain, sinks, bias_t):
    kcol, vcol = OFF_K // D_KV, OFF_V // D_KV

    def body(q_ref, kc_ref, kp_ref, vc_ref, vp_ref, do_ref, qg_ref, kg_ref, sink_ref, bias_ref,
             dq_ref, dk_ref, dv_ref, dband_ref, dsink_ref, dqg_ref, dkg_ref, dkn_scr, dv_scr, dqt_scr, dsink_acc, dqg_acc):
        i = pl.program_id(0)
        first = i == N_BLOCKS - 1

        @pl.when(i == 0)
        def _():
            for ref in (dband_ref, dkg_ref, dkn_scr, dv_scr, dsink_acc, dqg_acc):
                ref[...] = jnp.zeros_like(ref)

        key_row = lax.broadcasted_iota(jnp.int32, (2 * BLOCK, BLOCK), 0)
        k2 = jnp.concatenate([kp_ref[...], kc_ref[...]], axis=0)
        v2 = jnp.concatenate([vp_ref[...], vc_ref[...]], axis=0)
        q_t = q_ref[...].T
        do_t = do_ref[...].T
        qg = jnp.broadcast_to(qg_ref[...], (HEAD_DIM, BLOCK))
        kg = kg_ref[...]
        scale = HEAD_DIM ** -0.5
        for hk in range(N_KV_HEADS):
            sl = slice(hk * HEAD_DIM, (hk + 1) * HEAD_DIM)
            kk = k2[:, sl]
            rk = _rms(kk)
            khat = kk * rk
            kn = (khat * kg).astype(bf16)
            vb = v2[:, sl].astype(bf16)
            dkn = jnp.zeros((2 * BLOCK, HEAD_DIM), f32)
            dvv = jnp.zeros((2 * BLOCK, HEAD_DIM), f32)
            heads = range(hk * Q_PER_KV, (hk + 1) * Q_PER_KV)
            rqs, qhats, qns, d_os = [], [], [], []
            for h in heads:
                hs = slice(h * HEAD_DIM, (h + 1) * HEAD_DIM)
                qh = q_t[hs, :]
                rqs.append(_rms_t(qh))
                qhats.append(qh * rqs[-1])
                qns.append((qhats[-1] * qg).astype(bf16))
                d_os.append(do_t[hs, :].astype(bf16))
            scores = [_bdot(kn, qn) for qn in qns]
            dps = [_bdot(vb, d_o) for d_o in d_os]
            ps, dss = [], []
            for h, s, dp in zip(heads, scores, dps):
                p, psink = _attn_softmax_t(s, bias_ref[h], sink_ref[h], first, key_row)
                delta = jnp.sum(p * dp, axis=0, keepdims=True)
                ds = p * (dp - delta)
                dband_ref[h] += ds
                dsink_acc[h:h + 1, :] += -(psink * delta)
                ps.append(p.astype(bf16))
                dss.append(ds.astype(bf16))
            dqns = [_bdot_tn(kn, ds) * scale for ds in dss]
            for ds, qn, p, d_o in zip(dss, qns, ps, d_os):
                dkn = dkn + _bdot_nt(ds, qn) * scale
                dvv = dvv + _bdot_nt(p, d_o)
            for h, dqn, rq, qhat in zip(heads, dqns, rqs, qhats):
                dqg_acc[...] += dqn * qhat
                t = dqn * qg
                dqt_scr[h * HEAD_DIM:(h + 1) * HEAD_DIM, :] = rq * (t - qhat * jnp.mean(t * qhat, axis=0, keepdims=True))
            dkn_cur = dkn[BLOCK:] + dkn_scr[:, sl]
            dkn_scr[:, sl] = dkn[:BLOCK]
            khat_c, rk_c = khat[BLOCK:], rk[BLOCK:]
            dkg_ref[...] += jnp.sum(dkn_cur * khat_c, axis=0, keepdims=True)
            dk_ref[:, sl] = _rms_bwd(dkn_cur, khat_c, rk_c, kg)
            dv_ref[:, sl] = dvv[BLOCK:] + dv_scr[:, sl]
            dv_scr[:, sl] = dvv[:BLOCK]
        dq_ref[...] = dqt_scr[...].T

        @pl.when(i == N_BLOCKS - 1)
        def _():
            dsink_ref[...] = jnp.sum(dsink_acc[...], axis=1, keepdims=True)
            dqg_ref[...] = jnp.sum(dqg_acc[...], axis=1, keepdims=True)

    blk = lambda i: N_BLOCKS - 1 - i
    prev = lambda i: jnp.maximum(N_BLOCKS - 2 - i, 0)
    return pl.pallas_call(
        body, name="attn_bwd", grid=(N_BLOCKS,),
        in_specs=[pl.BlockSpec((BLOCK, D_ATTN), lambda i: (blk(i), 0)),
                  pl.BlockSpec((BLOCK, D_KV), lambda i: (blk(i), kcol)), pl.BlockSpec((BLOCK, D_KV), lambda i: (prev(i), kcol)),
                  pl.BlockSpec((BLOCK, D_KV), lambda i: (blk(i), vcol)), pl.BlockSpec((BLOCK, D_KV), lambda i: (prev(i), vcol)),
                  pl.BlockSpec((BLOCK, D_ATTN), lambda i: (blk(i), 0)),
                  _full((HEAD_DIM, 1)), _full((1, HEAD_DIM)), pl.BlockSpec(memory_space=pltpu.SMEM),
                  _full((N_Q_HEADS, 2 * BLOCK, BLOCK))],
        out_specs=[pl.BlockSpec((BLOCK, D_ATTN), lambda i: (blk(i), 0)), pl.BlockSpec((BLOCK, D_KV), lambda i: (blk(i), 0)),
                   pl.BlockSpec((BLOCK, D_KV), lambda i: (blk(i), 0)), _full((N_Q_HEADS, 2 * BLOCK, BLOCK)),
                   _full((N_Q_HEADS, 1)), _full((HEAD_DIM, 1)), _full((1, HEAD_DIM))],
        out_shape=[jax.ShapeDtypeStruct((SEQ, D_ATTN), f32), jax.ShapeDtypeStruct((SEQ, D_KV), f32),
                   jax.ShapeDtypeStruct((SEQ, D_KV), f32), jax.ShapeDtypeStruct((N_Q_HEADS, 2 * BLOCK, BLOCK), f32),
                   jax.ShapeDtypeStruct((N_Q_HEADS, 1), f32), jax.ShapeDtypeStruct((HEAD_DIM, 1), f32),
                   jax.ShapeDtypeStruct((1, HEAD_DIM), f32)],
        scratch_shapes=[pltpu.VMEM((BLOCK, D_KV), f32), pltpu.VMEM((BLOCK, D_KV), f32), pltpu.VMEM((D_ATTN, BLOCK), f32),
                        pltpu.VMEM((N_Q_HEADS, BLOCK), f32), pltpu.VMEM((HEAD_DIM, BLOCK), f32)],
        compiler_params=_params("arbitrary"),
    )(proj, proj, proj, proj, proj, d_out, q_gain_col, k_gain, sinks, bias_t)


SUBLANES = 8


def _shift_down(u, s, row8):
    if s == 0:
        return u
    r = pltpu.roll(u, s, 0)
    return jnp.concatenate([jnp.where(row8 >= s, r[:SUBLANES], 0.0), r[SUBLANES:]], axis=0)


def _shift_up(u, s, row8):
    if s == 0:
        return u
    r = pltpu.roll(u, SEQ - s, 0)
    return jnp.concatenate([r[:-SUBLANES], jnp.where(row8 < SUBLANES - s, r[-SUBLANES:], 0.0)], axis=0)


def conv_fwd(proj, conv_w, conv_b):
    xcol = OFF_X // LANE

    def body(u_ref, w_ref, b_ref, o_ref):
        u = u_ref[...]
        row = lax.broadcasted_iota(jnp.int32, (SUBLANES, LANE), 0)
        pre = b_ref[...] + jnp.zeros_like(u)
        for k in range(CONV_WIDTH):
            pre = pre + w_ref[k:k + 1, :] * _shift_down(u, CONV_WIDTH - 1 - k, row)
        o_ref[...] = pre * _sigmoid(pre)

    return pl.pallas_call(
        body, name="conv_fwd", grid=(D_CONV // LANE,),
        in_specs=[pl.BlockSpec((SEQ, LANE), lambda j: (0, xcol + j)), pl.BlockSpec((CONV_WIDTH, LANE), lambda j: (0, j)),
                  pl.BlockSpec((1, LANE), lambda j: (0, j))],
        out_specs=pl.BlockSpec((SEQ, LANE), lambda j: (0, j)),
        out_shape=jax.ShapeDtypeStruct((SEQ, D_CONV), f32),
        compiler_params=_params("arbitrary"),
    )(proj, conv_w, conv_b)


def conv_bwd(proj, d_act, conv_w, conv_b):
    xcol = OFF_X // LANE

    def body(u_ref, da_ref, w_ref, b_ref, du_ref, dw_ref, db_ref):
        u = u_ref[...]
        row = lax.broadcasted_iota(jnp.int32, (SUBLANES, LANE), 0)
        shifted = [_shift_down(u, CONV_WIDTH - 1 - k, row) for k in range(CONV_WIDTH)]
        pre = b_ref[...] + jnp.zeros_like(u)
        for k in range(CONV_WIDTH):
            pre = pre + w_ref[k:k + 1, :] * shifted[k]
        sg = _sigmoid(pre)
        dpre = da_ref[...] * (sg * (1.0 + pre * (1.0 - sg)))
        db_ref[...] = jnp.sum(dpre, axis=0, keepdims=True)
        du = jnp.zeros_like(u)
        for k in range(CONV_WIDTH):
            dw_ref[k:k + 1, :] = jnp.sum(dpre * shifted[k], axis=0, keepdims=True)
            du = du + w_ref[k:k + 1, :] * _shift_up(dpre, CONV_WIDTH - 1 - k, row)
        du_ref[...] = du

    return pl.pallas_call(
        body, name="conv_bwd", grid=(D_CONV // LANE,),
        in_specs=[pl.BlockSpec((SEQ, LANE), lambda j: (0, xcol + j)), pl.BlockSpec((SEQ, LANE), lambda j: (0, j)),
                  pl.BlockSpec((CONV_WIDTH, LANE), lambda j: (0, j)), pl.BlockSpec((1, LANE), lambda j: (0, j))],
        out_specs=[pl.BlockSpec((SEQ, LANE), lambda j: (0, j)), pl.BlockSpec((CONV_WIDTH, LANE), lambda j: (0, j)),
                   pl.BlockSpec((1, LANE), lambda j: (0, j))],
        out_shape=[jax.ShapeDtypeStruct((SEQ, D_CONV), f32), jax.ShapeDtypeStruct((CONV_WIDTH, D_CONV), f32),
                   jax.ShapeDtypeStruct((1, D_CONV), f32)],
        compiler_params=_params("arbitrary"),
    )(proj, d_act, conv_w, conv_b)


def _ssd_chunk_common(dt_raw, dtb, alog):
    row = lax.broadcasted_iota(jnp.int32, (CHUNK, CHUNK), 0)
    col = lax.broadcasted_iota(jnp.int32, (CHUNK, CHUNK), 1)
    tri = (row >= col).astype(f32)
    strict = (row > col).astype(f32)
    dtp = _softplus(dt_raw + dtb)
    a_row = -jnp.exp(alog)
    d_a = dtp * a_row
    cs = _hdot(tri, d_a)
    cs_last = cs[CHUNK - 1:CHUNK, :]
    return row, col, dtp, a_row, cs, cs.T, cs_last


def _seg_decay(cs, cs_t, hd, row, col):
    seg = cs[:, hd:hd + 1] - cs_t[hd:hd + 1, :]
    return jnp.where(row >= col, jnp.exp(seg), 0.0)


GROUP_W = HEADS_PER_GROUP * SSM_HEAD_DIM


def _group_indicator(g):
    j = lax.broadcasted_iota(jnp.int32, (GROUP_W, LANE), 0)
    lane = lax.broadcasted_iota(jnp.int32, (GROUP_W, LANE), 1)
    return (lane == g * HEADS_PER_GROUP + j // SSM_HEAD_DIM).astype(bf16)


def _bf16_pieces(a, n):
    pieces = []
    for _ in range(n):
        p = a.astype(bf16)
        pieces.append(p)
        a = a - p.astype(f32)
    return pieces


def _head_spread(a, ind):
    return sum(lax.dot_general(p, ind, (((1,), (1,)), ((), ())), preferred_element_type=f32) for p in _bf16_pieces(a, 3))


def _head_sums(a, ind):
    return sum(jnp.dot(p, ind, preferred_element_type=f32) for p in _bf16_pieces(a, 2))


def ssd_fwd_g(act, proj, dt_bias, a_log, d_skip, norm_g):
    zcol, dtcol = OFF_Z // D_SSM, OFF_DT // LANE

    def body(act_ref, z_ref, dt_ref, dtb_ref, alog_ref, dsk_ref, ng_ref, out_ref, ypre_ref, st_ref, state):
        c = pl.program_id(0)

        @pl.when(c == 0)
        def _():
            state[...] = jnp.zeros_like(state)

        row, col, dtp, a_row, cs, cs_t, cs_last = _ssd_chunk_common(dt_ref[...], dtb_ref[...], alog_ref[...])
        e_cs = jnp.exp(cs)
        dte = jnp.exp(cs_last - cs)
        rows8 = jnp.concatenate([jnp.exp(cs_last), dsk_ref[...], jnp.zeros((6, LANE), f32)], axis=0)
        z = z_ref[...]
        sz = z * _sigmoid(z)
        ng = ng_ref[...]
        for g in range(SSM_GROUPS):
            gs = slice(g * GROUP_W, (g + 1) * GROUP_W)
            ind = _group_indicator(g)
            xg = act_ref[:, gs]
            bg = act_ref[:, D_SSM + g * SSM_STATE:D_SSM + (g + 1) * SSM_STATE]
            cg = act_ref[:, D_SSM + D_BC + g * SSM_STATE:D_SSM + D_BC + (g + 1) * SSM_STATE]
            dt_e, e_e, dte_e = _head_spread(dtp, ind), _head_spread(e_cs, ind), _head_spread(dte, ind)
            rows_e = _head_spread(rows8, ind)
            ecl_e, dsk_e = rows_e[0:1], rows_e[1:2]
            xdt = xg * dt_e
            prev = state[g]
            st_ref[0, g] = prev
            cb = _bdot_nt(cg, bg)
            goff = _bdot(cg, prev)
            snew = _bdot_tn(bg, xdt * dte_e)
            heads = range(g * HEADS_PER_GROUP, (g + 1) * HEADS_PER_GROUP)
            ms = [cb * _seg_decay(cs, cs_t, hd, row, col) for hd in heads]
            yd = [_bdot(m, xdt[:, r * SSM_HEAD_DIM:(r + 1) * SSM_HEAD_DIM]) for r, m in enumerate(ms)]
            y = jnp.concatenate(yd, axis=1) + e_e * goff + xg * dsk_e
            state[g] = prev * ecl_e + snew
            ypre_ref[:, gs] = y
            part = y * sz[:, gs]
            out_ref[:, gs] = part * _rms(part) * ng[:, gs]

    return pl.pallas_call(
        body, name="ssd_fwd", grid=(N_CHUNKS,),
        in_specs=[pl.BlockSpec((CHUNK, D_CONV), lambda c: (c, 0)), pl.BlockSpec((CHUNK, D_SSM), lambda c: (c, zcol)),
                  pl.BlockSpec((CHUNK, LANE), lambda c: (c, dtcol)), _full((1, LANE)), _full((1, LANE)), _full((1, LANE)),
                  _full((1, D_SSM))],
        out_specs=[pl.BlockSpec((CHUNK, D_SSM), lambda c: (c, 0)), pl.BlockSpec((CHUNK, D_SSM), lambda c: (c, 0)),
                   pl.BlockSpec((1, SSM_GROUPS, SSM_STATE, GROUP_W), lambda c: (c, 0, 0, 0))],
        out_shape=[jax.ShapeDtypeStruct((SEQ, D_SSM), f32), jax.ShapeDtypeStruct((SEQ, D_SSM), f32),
                   jax.ShapeDtypeStruct((N_CHUNKS, SSM_GROUPS, SSM_STATE, GROUP_W), f32)],
        scratch_shapes=[pltpu.VMEM((SSM_GROUPS, SSM_STATE, GROUP_W), f32)],
        compiler_params=_params("arbitrary"),
    )(act, proj, proj, dt_bias, a_log, d_skip, norm_g)


def ssd_bwd_g(act, proj, ypre, states, d_out, dt_bias, a_log, d_skip, norm_g):
    zcol, dtcol = OFF_Z // D_SSM, OFF_DT // LANE

    def body(act_ref, z_ref, dt_ref, ypre_ref, st_ref, do_ref, dtb_ref, alog_ref, dsk_ref, ng_ref,
             dact_ref, ddt_ref, dz_ref, dng_ref, dpar_ref, dstate):
        i = pl.program_id(0)

        @pl.when(i == 0)
        def _():
            for ref in (dng_ref, dpar_ref, dstate):
                ref[...] = jnp.zeros_like(ref)

        row, col, dtp, a_row, cs, cs_t, cs_last = _ssd_chunk_common(dt_ref[...], dtb_ref[...], alog_ref[...])
        upper = (row <= col).astype(f32)
        lane = lax.broadcasted_iota(jnp.int32, (CHUNK, LANE), 1)
        rowl = lax.broadcasted_iota(jnp.int32, (CHUNK, LANE), 0)
        e_cs = jnp.exp(cs)
        dte = jnp.exp(cs_last - cs)
        ecl = jnp.exp(cs_last)
        rows8 = jnp.concatenate([ecl, dsk_ref[...], jnp.zeros((6, LANE), f32)], axis=0)
        z = z_ref[...]
        sgz = _sigmoid(z)
        sz = z * sgz
        ng = ng_ref[...]
        ddt_mat = jnp.zeros((CHUNK, LANE), f32)
        dcs_mat = jnp.zeros((CHUNK, LANE), f32)
        dcs_t = jnp.zeros((LANE, CHUNK), f32)
        dcsl_row = jnp.zeros((1, LANE), f32)
        dd_row = jnp.zeros((1, LANE), f32)
        for g in range(SSM_GROUPS):
            gs = slice(g * GROUP_W, (g + 1) * GROUP_W)
            bsl = slice(D_SSM + g * SSM_STATE, D_SSM + (g + 1) * SSM_STATE)
            csl = slice(D_SSM + D_BC + g * SSM_STATE, D_SSM + D_BC + (g + 1) * SSM_STATE)
            ind = _group_indicator(g)
            y = ypre_ref[:, gs]
            part = y * sz[:, gs]
            r = _rms(part)
            yhat = part * r
            d_o = do_ref[:, gs]
            dng_ref[:, gs] += jnp.sum(d_o * yhat, axis=0, keepdims=True)
            dyz = _rms_bwd(d_o, yhat, r, ng[:, gs])
            dy = dyz * sz[:, gs]
            dz_ref[:, gs] = dyz * y * (sgz[:, gs] * (1.0 + z[:, gs] * (1.0 - sgz[:, gs])))

            xg = act_ref[:, gs]
            bg = act_ref[:, bsl]
            cg = act_ref[:, csl]
            dt_e, e_e, dte_e = _head_spread(dtp, ind), _head_spread(e_cs, ind), _head_spread(dte, ind)
            rows_e = _head_spread(rows8, ind)
            ecl_e, dsk_e = rows_e[0:1], rows_e[1:2]
            xdt = xg * dt_e
            prev = st_ref[0, g]
            dh = dstate[g]
            heads = range(g * HEADS_PER_GROUP, (g + 1) * HEADS_PER_GROUP)
            hsl = [slice(r_ * SSM_HEAD_DIM, (r_ + 1) * SSM_HEAD_DIM) for r_ in range(HEADS_PER_GROUP)]
            cb = _bdot_nt(cg, bg)
            lms = [_seg_decay(cs, cs_t, hd, row, col) for hd in heads]
            ms = [cb * lm for lm in lms]
            gmat = _bdot(cg, prev)
            dgm = dy * e_e
            dcg = _bdot_nt(dgm, prev)
            dprev = _bdot_tn(cg, dgm)
            dbg = _bdot_nt(xdt * dte_e, dh)
            dw = _bdot(bg, dh)
            dms = [_bdot_nt(dy[:, s_], xdt[:, s_]) for s_ in hsl]
            dxdts = [_bdot_tn(m, dy[:, s_]) for m, s_ in zip(ms, hsl)]
            dxdt = jnp.concatenate(dxdts, axis=1) + dw * dte_e
            dact_ref[:, gs] = dy * dsk_e + dxdt * dt_e
            dstate[g] = dprev + dh * ecl_e
            dcb = jnp.zeros((CHUNK, CHUNK), f32)
            for hd, dm, lm, m in zip(heads, dms, lms, ms):
                dcb = dcb + dm * lm
                dseg = dm * m
                dcs_mat = dcs_mat + jnp.where(lane == hd, jnp.sum(dseg, axis=1, keepdims=True), 0.0)
                dcs_t = jnp.where(row == hd, jnp.sum(dseg, axis=0, keepdims=True), dcs_t)
            dact_ref[:, bsl] = dbg + _bdot_tn(dcb, cg)
            dact_ref[:, csl] = dcg + _bdot(dcb, bg)
            ddte = _head_sums(dw * xdt, ind) * dte
            dcs_mat = dcs_mat + _head_sums(dy * gmat, ind) * e_cs - ddte
            ddt_mat = ddt_mat + _head_sums(dxdt * xg, ind)
            dcsl_row = (dcsl_row + jnp.sum(ddte, axis=0, keepdims=True)
                        + jnp.sum(_head_sums(dh * prev, ind), axis=0, keepdims=True) * ecl)
            dd_row = dd_row + jnp.sum(_head_sums(dy * xg, ind), axis=0, keepdims=True)
        dcs_mat = dcs_mat - dcs_t.T + jnp.where(rowl == CHUNK - 1, dcsl_row, 0.0)
        dda = _hdot(upper, dcs_mat)
        ddt_mat = ddt_mat + dda * a_row
        da_row = jnp.sum(dda * dtp, axis=0, keepdims=True)
        ddt_raw = ddt_mat * _sigmoid(dt_ref[...] + dtb_ref[...])
        ddt_ref[...] = ddt_raw
        dpar_ref[0:1, :] += jnp.sum(ddt_raw, axis=0, keepdims=True)
        dpar_ref[1:2, :] += da_row * a_row
        dpar_ref[2:3, :] += dd_row

    blk = lambda i: N_CHUNKS - 1 - i
    return pl.pallas_call(
        body, name="ssd_bwd", grid=(N_CHUNKS,),
        in_specs=[pl.BlockSpec((CHUNK, D_CONV), lambda i: (blk(i), 0)), pl.BlockSpec((CHUNK, D_SSM), lambda i: (blk(i), zcol)),
                  pl.BlockSpec((CHUNK, LANE), lambda i: (blk(i), dtcol)), pl.BlockSpec((CHUNK, D_SSM), lambda i: (blk(i), 0)),
                  pl.BlockSpec((1, SSM_GROUPS, SSM_STATE, GROUP_W), lambda i: (blk(i), 0, 0, 0)),
                  pl.BlockSpec((CHUNK, D_SSM), lambda i: (blk(i), 0)),
                  _full((1, LANE)), _full((1, LANE)), _full((1, LANE)), _full((1, D_SSM))],
        out_specs=[pl.BlockSpec((CHUNK, D_CONV), lambda i: (blk(i), 0)), pl.BlockSpec((CHUNK, LANE), lambda i: (blk(i), 0)),
                   pl.BlockSpec((CHUNK, D_SSM), lambda i: (blk(i), 0)), _full((1, D_SSM)), _full((8, LANE))],
        out_shape=[jax.ShapeDtypeStruct((SEQ, D_CONV), f32), jax.ShapeDtypeStruct((SEQ, LANE), f32),
                   jax.ShapeDtypeStruct((SEQ, D_SSM), f32), jax.ShapeDtypeStruct((1, D_SSM), f32),
                   jax.ShapeDtypeStruct((8, LANE), f32)],
        scratch_shapes=[pltpu.VMEM((SSM_GROUPS, SSM_STATE, GROUP_W), f32)],
        compiler_params=_params("arbitrary"),
    )(act, proj, proj, ypre, states, d_out, dt_bias, a_log, d_skip, norm_g)


def out_fwd(x, attn, ssm, w_out, tm=512):
    def body(x_ref, a_ref, s_ref, w_ref, o_ref):
        o_ref[...] = x_ref[...] + _bdot(a_ref[...], w_ref[:D_ATTN, :]) + _bdot(s_ref[...], w_ref[D_ATTN:, :])

    tok = lambda w_: pl.BlockSpec((tm, w_), lambda i: (i, 0))
    return pl.pallas_call(
        body, name="out_fwd", grid=(SEQ // tm,),
        in_specs=[tok(D_MODEL), tok(D_ATTN), tok(D_SSM), _full((D_MODEL, D_MODEL))],
        out_specs=tok(D_MODEL), out_shape=jax.ShapeDtypeStruct((SEQ, D_MODEL), f32),
        compiler_params=_params("arbitrary"),
    )(x, attn, ssm, w_out)


def out_bwd(dx1, attn, ssm, w_out, tm=512):
    nt = SEQ // tm

    def body(d_ref, a_ref, s_ref, w_ref, da_ref, ds_ref, dw_ref, dw16_ref):
        i = pl.program_id(0)

        @pl.when(i == 0)
        def _():
            dw_ref[...] = jnp.zeros_like(dw_ref)

        d = d_ref[...].astype(bf16)
        dcat = _bdot_nt(d, w_ref[...])
        da_ref[...] = dcat[:, :D_ATTN]
        ds_ref[...] = dcat[:, D_ATTN:]
        dw_ref[:D_ATTN, :] += _bdot_tn(a_ref[...], d)
        dw_ref[D_ATTN:, :] += _bdot_tn(s_ref[...], d)

        @pl.when(i == nt - 1)
        def _():
            dw16_ref[...] = dw_ref[...].astype(bf16)

    tok = lambda w_: pl.BlockSpec((tm, w_), lambda i: (i, 0))
    return pl.pallas_call(
        body, name="out_bwd", grid=(nt,),
        in_specs=[tok(D_MODEL), tok(D_ATTN), tok(D_SSM), _resident((D_MODEL, D_MODEL))],
        out_specs=[tok(D_ATTN), tok(D_SSM), _resident((D_MODEL, D_MODEL)), _resident((D_MODEL, D_MODEL))],
        out_shape=[jax.ShapeDtypeStruct((SEQ, D_ATTN), f32), jax.ShapeDtypeStruct((SEQ, D_SSM), f32),
                   jax.ShapeDtypeStruct((D_MODEL, D_MODEL), f32), jax.ShapeDtypeStruct((D_MODEL, D_MODEL), bf16)],
        compiler_params=_params("arbitrary"),
    )(dx1, attn, ssm, w_out)


MLP_SUB = 256


def mlp_fwd(x1, g, w_up, w_down, tm=1024):
    def body(x_ref, g_ref, wu_ref, wd_ref, o_ref, u_ref, h_scr):
        j = pl.program_id(1)

        @pl.when(j == 0)
        def _():
            xv = x_ref[...]
            h_scr[...] = (xv * _rms(xv) * g_ref[...]).astype(bf16)
            o_ref[...] = xv

        for r in range(tm // MLP_SUB):
            rows = slice(r * MLP_SUB, (r + 1) * MLP_SUB)
            u = jnp.dot(h_scr[rows, :], wu_ref[...], preferred_element_type=f32)
            u_ref[rows, :] = u
            a = jnp.square(jnp.maximum(u, 0.0))
            o_ref[rows, :] += _bdot(a, wd_ref[...])

    return pl.pallas_call(
        body, name="mlp_fwd", grid=(SEQ // tm, N_CHIPS),
        in_specs=[pl.BlockSpec((tm, D_MODEL), lambda i, j: (i, 0)), _full((1, D_MODEL)),
                  pl.BlockSpec((None, D_MODEL, FF_TILE), lambda i, j: (j, 0, 0)),
                  pl.BlockSpec((None, FF_TILE, D_MODEL), lambda i, j: (j, 0, 0))],
        out_specs=[pl.BlockSpec((tm, D_MODEL), lambda i, j: (i, 0)), pl.BlockSpec((tm, FF_TILE), lambda i, j: (i, j))],
        out_shape=[jax.ShapeDtypeStruct((SEQ, D_MODEL), f32), jax.ShapeDtypeStruct((SEQ, D_FF), f32)],
        scratch_shapes=[pltpu.VMEM((tm, D_MODEL), bf16)],
        compiler_params=_params("arbitrary", "arbitrary"),
    )(x1, g, w_up, w_down)


def mlp_bwd_data(dx2, u, x1, g, w_up, w_down, tm=1024):
    def body(d_ref, u_ref, x_ref, g_ref, wu_ref, wd_ref, dx_ref, du_ref, dg_ref, dh_scr):
        i, j = pl.program_id(0), pl.program_id(1)

        @pl.when(jnp.logical_and(i == 0, j == 0))
        def _():
            dg_ref[...] = jnp.zeros_like(dg_ref)

        @pl.when(j == 0)
        def _():
            dh_scr[...] = jnp.zeros_like(dh_scr)

        for r in range(tm // MLP_SUB):
            rows = slice(r * MLP_SUB, (r + 1) * MLP_SUB)
            da = _bdot_nt(d_ref[rows, :], wd_ref[...])
            du = (da * (2.0 * jnp.maximum(u_ref[rows, :], 0.0))).astype(bf16)
            du_ref[rows, :] = du
            dh_scr[rows, :] += _bdot_nt(du, wu_ref[...])

        @pl.when(j == N_CHIPS - 1)
        def _():
            xv = x_ref[...]
            r = _rms(xv)
            xhat = xv * r
            dh = dh_scr[...]
            dg_ref[...] += jnp.sum(dh * xhat, axis=0, keepdims=True)
            dx_ref[...] = d_ref[...] + _rms_bwd(dh, xhat, r, g_ref[...])

    return pl.pallas_call(
        body, name="mlp_bwd_data", grid=(SEQ // tm, N_CHIPS),
        in_specs=[pl.BlockSpec((tm, D_MODEL), lambda i, j: (i, 0)), pl.BlockSpec((tm, FF_TILE), lambda i, j: (i, j)),
                  pl.BlockSpec((tm, D_MODEL), lambda i, j: (i, 0)), _full((1, D_MODEL)),
                  pl.BlockSpec((None, D_MODEL, FF_TILE), lambda i, j: (j, 0, 0)),
                  pl.BlockSpec((None, FF_TILE, D_MODEL), lambda i, j: (j, 0, 0))],
        out_specs=[pl.BlockSpec((tm, D_MODEL), lambda i, j: (i, 0)), pl.BlockSpec((tm, FF_TILE), lambda i, j: (i, j)),
                   _full((1, D_MODEL))],
        out_shape=[jax.ShapeDtypeStruct((SEQ, D_MODEL), f32), jax.ShapeDtypeStruct((SEQ, D_FF), bf16),
                   jax.ShapeDtypeStruct((1, D_MODEL), f32)],
        scratch_shapes=[pltpu.VMEM((tm, D_MODEL), f32)],
        compiler_params=_params("arbitrary", "arbitrary"),
    )(dx2, u, x1, g, w_up, w_down)


def mlp_bwd_weights(dx2, u, du, x1, g, tm=512):
    nt = SEQ // tm

    def body(d_ref, u_ref, du_ref, x_ref, g_ref, dwu_ref, dwd_ref, dwu16_ref, dwd16_ref, h_scr, d_scr):
        j, i = pl.program_id(0), pl.program_id(1)

        @pl.when(j == 0)
        def _():
            xv = x_ref[...]
            h_scr[i] = (xv * _rms(xv) * g_ref[...]).T.astype(bf16)
            d_scr[i] = d_ref[...].astype(bf16)

        @pl.when(i == 0)
        def _():
            dwu_ref[...] = jnp.zeros_like(dwu_ref)
            dwd_ref[...] = jnp.zeros_like(dwd_ref)

        dwu_ref[...] += jnp.dot(h_scr[i], du_ref[...], preferred_element_type=f32)
        a = jnp.square(jnp.maximum(u_ref[...], 0.0))
        dwd_ref[...] += _bdot_tn(a, d_scr[i])

        @pl.when(i == nt - 1)
        def _():
            dwu16_ref[...] = dwu_ref[...].astype(bf16)
            dwd16_ref[...] = dwd_ref[...].astype(bf16)

    up = pl.BlockSpec((None, D_MODEL, FF_TILE), lambda j, i: (j, 0, 0))
    down = pl.BlockSpec((None, FF_TILE, D_MODEL), lambda j, i: (j, 0, 0))
    first_pass = pl.BlockSpec((tm, D_MODEL), lambda j, i: (jnp.where(j == 0, i, nt - 1), 0))
    return pl.pallas_call(
        body, name="mlp_bwd_weights", grid=(N_CHIPS, nt),
        in_specs=[first_pass, pl.BlockSpec((tm, FF_TILE), lambda j, i: (i, j)),
                  pl.BlockSpec((tm, FF_TILE), lambda j, i: (i, j)), first_pass, _full((1, D_MODEL))],
        out_specs=[up, down, up, down],
        out_shape=[jax.ShapeDtypeStruct((N_CHIPS, D_MODEL, FF_TILE), f32), jax.ShapeDtypeStruct((N_CHIPS, FF_TILE, D_MODEL), f32),
                   jax.ShapeDtypeStruct((N_CHIPS, D_MODEL, FF_TILE), bf16), jax.ShapeDtypeStruct((N_CHIPS, FF_TILE, D_MODEL), bf16)],
        scratch_shapes=[pltpu.VMEM((nt, D_MODEL, tm), bf16), pltpu.VMEM((nt, tm, D_MODEL), bf16)],
        compiler_params=_params("arbitrary", "arbitrary"),
    )(dx2, u, du, x1, g)


def loss_head(y, target, tm=512):
    def body(y_ref, t_ref, dy_ref, l_ref):
        @pl.when(pl.program_id(0) == 0)
        def _():
            l_ref[...] = jnp.zeros_like(l_ref)

        d = y_ref[...] - t_ref[...]
        dy_ref[...] = d * (1.0 / D_MODEL)
        part = jnp.sum(jnp.mean(d * d, axis=-1, keepdims=True), axis=0, keepdims=True)
        l_ref[...] += 0.5 * part

    tok = pl.BlockSpec((tm, D_MODEL), lambda i: (i, 0))
    return pl.pallas_call(
        body, name="loss_head", grid=(SEQ // tm,), in_specs=[tok, tok], out_specs=[tok, _full((1, 1))],
        out_shape=[jax.ShapeDtypeStruct((SEQ, D_MODEL), f32), jax.ShapeDtypeStruct((1, 1), f32)],
        compiler_params=_params("arbitrary"),
    )(y, target)


def _pad_lane(v):
    return jnp.pad(v, (0, LANE - v.shape[0]))[None, :]


def local_step(x, target, w, prov):
    bucket = jnp.asarray(_bucket_table().T)
    bias = bias_build(w["rel_bias"], bucket)
    saved = []
    for l in range(DEPTH):
        g_mix = w["mix_norm_g"][l][None, :] + prov.stage(("begin", l), x)
        w_in = prov.w_in(l, x)
        proj = in_fwd(x, g_mix, w_in)
        conv_b = w["conv_b"][l][None, :]
        act = conv_fwd(proj, w["conv_w"][l], conv_b)
        dtb = _pad_lane(w["dt_bias"][l]) + prov.stage(("mid", l), act)
        alog, dsk = _pad_lane(w["a_log"][l]), _pad_lane(w["d_skip"][l])
        ng = w["ssm_norm_g"][l][None, :]
        ssm, ypre, states = ssd_fwd_g(act, proj, dtb, alog, dsk, ng)
        qg, kg = w["q_gain"][l][:, None] + 0.0 * ssm[:1, :1], w["k_gain"][l][None, :]
        attn = attn_fwd_t(proj, qg, kg, w["sinks"][l], bias)
        tok = prov.stage(("pre_out", l), attn)
        w_out = prov.w_out(l, attn) + jnp.asarray(tok, bf16)
        x1 = out_fwd(x, attn, ssm, w_out)
        g_mlp = w["mlp_norm_g"][l][None, :] + prov.stage(("pre_mlp", l), x1)
        w_up, w_down = prov.mlp(l, x1)
        x2, u = mlp_fwd(x1, g_mlp, w_up, w_down)
        saved.append(dict(x=x, proj=proj, attn=attn, act=act, ssm=ssm, ypre=ypre, states=states, x1=x1, u=u,
                          g_mix=g_mix, qg=qg, kg=kg, conv_b=conv_b, dtb=dtb, alog=alog, dsk=dsk, ng=ng, g_mlp=g_mlp,
                          w_in=w_in, w_out=w_out, w_up=w_up, w_down=w_down))
        x = x2
    dx, loss = loss_head(x, target)
    grads = [None] * DEPTH
    dbands = [None] * DEPTH
    tok = 0.0
    for l in reversed(range(DEPTH)):
        s = saved[l]
        g_mlp = s["g_mlp"] + tok
        dx1, du, dg_mlp = mlp_bwd_data(dx, s["u"], s["x1"], g_mlp, s["w_up"], s["w_down"])
        dw_up, dw_down, dw_up16, dw_down16 = mlp_bwd_weights(dx, s["u"], du, s["x1"], g_mlp)
        tok = prov.grads(("mlp", l), dict(w_up=(dw_up, dw_up16), w_down=(dw_down, dw_down16)), dw_down)
        dattn, dssm, dw_out, dw_out16 = out_bwd(dx1, s["attn"], s["ssm"], s["w_out"])
        dact, ddt, dz, dng, dpar = ssd_bwd_g(s["act"], s["proj"], s["ypre"], s["states"], dssm, s["dtb"] + tok, s["alog"],
                                           s["dsk"], s["ng"])
        conv_b = s["conv_b"] + prov.stage(("bwd_mid", l), dact)
        dxbc, dconv_w, dconv_b = conv_bwd(s["proj"], dact, w["conv_w"][l], conv_b)
        dq, dk, dv, dband, dsink, dqg, dkg = attn_bwd_t(s["proj"], dattn, s["qg"], s["kg"], w["sinks"][l], bias)
        dbands[l] = dband
        g_mix = s["g_mix"]
        if l == 0:
            d_rel = bias_bwd(dbands[0], dbands[1], bucket)
            g_mix = g_mix + 0.0 * d_rel[:1, :1]
        dx, dw_in, dg_mix = in_bwd(dq, dz, dxbc, dk, dv, ddt, s["x"], g_mix, s["w_in"], dx1)
        tok = prov.grads(("mix", l), dict(w_in=split_w_in_grad(dw_in), w_out=(dw_out, dw_out16)), dx)
        grads[l] = dict(mix_norm_g=dg_mix[0], q_gain=dqg[:, 0], k_gain=dkg[0], sinks=dsink[:, 0],
                        conv_w=dconv_w, conv_b=dconv_b[0], dt_bias=dpar[0, :SSM_HEADS], a_log=dpar[1, :SSM_HEADS],
                        d_skip=dpar[2, :SSM_HEADS], ssm_norm_g=dng[0], mlp_norm_g=dg_mlp[0])
    out = {k: jnp.stack([grads[l][k] for l in range(DEPTH)]) for k in grads[0]}
    out["rel_bias"] = d_rel[:, :N_Q_HEADS]
    return loss, dx, out, tok


MESH = pl.DeviceIdType.MESH
HBM = pl.BlockSpec(memory_space=pltpu.HBM)
N_DEVICES = 8


def _coords():
    return lax.axis_index("x"), lax.axis_index("y"), lax.axis_index("c")


def _peer_chips(x, y):
    return [(1 - x, y), (x, 1 - y), (1 - x, 1 - y)]


def _remote(src, dst, send_sem, recv_sem, device):
    return pltpu.make_async_remote_copy(src_ref=src, dst_ref=dst, send_sem=send_sem, recv_sem=recv_sem,
                                        device_id=device, device_id_type=MESH)


SEM = pl.BlockSpec(memory_space=pltpu.SEMAPHORE)
ANY = pl.BlockSpec(memory_space=pl.ANY)
DATAFLOW = pltpu.SideEffectType.DATAFLOW_SIDE_EFFECTING


def _gather_copies(kind, src_refs, land_refs, ssem, rsem):
    x, y, c = _coords()
    k_me = 2 * x + y
    n = len(land_refs)
    cps = []
    for p, land in enumerate(land_refs):
        hr = land.shape[1] // 2
        rows = pl.ds(c * hr, hr)
        for j, chip in enumerate(_peer_chips(x, y)):
            i = 3 * p + j
            if kind == "ici":
                cps.append(_remote(src_refs[p].at[rows, :], land.at[k_me, rows, :], ssem.at[i], rsem.at[i], (*chip, c)))
            else:
                got = land.at[2 * chip[0] + chip[1], rows, :]
                cps.append(_remote(got, got, ssem.at[i], rsem.at[i], (x, y, 1 - c)))
        if kind == "relay":
            cps.append(_remote(src_refs[p], land.at[k_me], ssem.at[3 * n + p], rsem.at[3 * n + p], (x, y, 1 - c)))
    return cps


def gather_now(srcs, conv):
    n = len(srcs)

    def body(*refs):
        src_refs, conv_ref = refs[:n], refs[n]
        lands, gconv = refs[n + 1:2 * n + 1], refs[2 * n + 1]
        ssem, rsem, fsem, frsem, csem, crsem = refs[2 * n + 2:]
        x, y, c = _coords()
        k_me = 2 * x + y
        targets = [(*chip, c) for chip in _peer_chips(x, y)] + [(x, y, 1 - c)]
        ici = _gather_copies("ici", src_refs, lands, ssem, rsem)
        relay = _gather_copies("relay", src_refs, lands, fsem, frsem)
        passed = [cp for i, cp in enumerate(relay) if i % 4 != 3]
        own = relay[3::4]
        conv_cps = [_remote(conv_ref, gconv.at[k_me], csem.at[j], crsem.at[j], t) for j, t in enumerate(targets)]
        for cp in ici + conv_cps + own:
            cp.start()
        for cp, fw in zip(ici, passed):
            cp.wait_recv()
            fw.start()
        for cp in conv_cps + relay:
            cp.wait_recv()
        for cp in ici + relay + conv_cps:
            cp.wait_send()

    out_shape = [jax.ShapeDtypeStruct((N_CHIPS,) + s.shape, s.dtype) for s in srcs]
    out_shape.append(jax.ShapeDtypeStruct((N_CHIPS,) + conv.shape, conv.dtype))
    sems = lambda k: pltpu.SemaphoreType.DMA((k,))
    return pl.pallas_call(
        body, name="gather_now", out_shape=out_shape, in_specs=[HBM] * (n + 1), out_specs=[HBM] * (n + 1),
        scratch_shapes=[sems(3 * n), sems(3 * n), sems(4 * n), sems(4 * n), sems(N_CHIPS), sems(N_CHIPS)],
    )(*srcs, conv)


def _gather_maker(kind, n_src):
    def make(refs, ssem, rsem):
        cps = _gather_copies(kind, refs[:n_src], refs[n_src:], ssem, rsem)
        return cps, cps
    return make


def _scatter_maker(n):
    def make(refs, ssem, rsem):
        x, y, c = _coords()
        k_me = 2 * x + y
        sends, arrivals = [], []
        for p in range(n):
            src, land = refs[p], refs[n + p]
            sends.append(_remote(src.at[k_me, 1 - c], land.at[0], ssem.at[7 * p], rsem.at[7 * p], (x, y, 1 - c)))
            for j, chip in enumerate(_peer_chips(x, y)):
                for cc in range(2):
                    sends.append(_remote(src.at[2 * chip[0] + chip[1], cc], land.at[1 + 2 * j + c],
                                         ssem.at[7 * p + 1 + 2 * j + cc], rsem.at[7 * p + 1 + 2 * j + c], (*chip, cc)))
            for s in range(7):
                arrivals.append(_remote(land.at[s], land.at[s], ssem.at[7 * p + s], rsem.at[7 * p + s], (x, y, 1 - c)))
        return sends, arrivals
    return make


def _share_maker(n):
    def make(refs, ssem, rsem):
        x, y, c = _coords()
        sends = [_remote(refs[p].at[c], refs[p].at[c], ssem.at[p], rsem.at[p], (x, y, 1 - c)) for p in range(n)]
        arrivals = [_remote(refs[p].at[1 - c], refs[p].at[1 - c], ssem.at[p], rsem.at[p], (x, y, 1 - c)) for p in range(n)]
        return sends, arrivals
    return make


def split_start(name, make, n_sems, operands, after):
    n = len(operands)

    def body(*refs):
        ssem, rsem, token = refs[n + 1], refs[n + 2], refs[-1]
        for cp in make(refs[:n], ssem, rsem)[0]:
            cp.start()
        token[...] = jnp.zeros_like(token)

    ops = [pltpu.with_memory_space_constraint(a, pltpu.HBM) for a in operands]
    outs = pl.pallas_call(
        body, name=name,
        out_shape=(pltpu.SemaphoreType.DMA((n_sems,)), pltpu.SemaphoreType.DMA((n_sems,)),
                   *[pltpu.HBM(a.shape, a.dtype) for a in ops], jax.ShapeDtypeStruct((8, LANE), f32)),
        in_specs=[HBM] * n + [ANY], out_specs=(SEM, SEM, *[HBM] * n, pl.BlockSpec(memory_space=pltpu.VMEM)),
        input_output_aliases={i: 2 + i for i in range(n)},
        compiler_params=pltpu.CompilerParams(has_side_effects=DATAFLOW),
    )(*ops, after)
    return dict(name=name, make=make, ssem=outs[0], rsem=outs[1], operands=outs[2:2 + n], token=outs[-1][0, 0])


def split_wait(handle, after):
    n = len(handle["operands"])

    def body(*refs):
        sends, arrivals = handle["make"](refs[:n], refs[n], refs[n + 1])
        for cp in sends:
            cp.wait_send()
        for cp in arrivals:
            cp.wait_recv()

    outs = pl.pallas_call(
        body, name=handle["name"].replace("start", "wait"),
        out_shape=tuple(pltpu.HBM(a.shape, a.dtype) for a in handle["operands"]),
        in_specs=[HBM] * n + [SEM, SEM, ANY], out_specs=tuple([HBM] * n),
        input_output_aliases={i: i for i in range(n)},
        compiler_params=pltpu.CompilerParams(has_side_effects=DATAFLOW),
    )(*handle["operands"], handle["ssem"], handle["rsem"], after)
    return list(outs)


def piece_sum(g, recv, kc_arr):
    _, _, rb, cc = g.shape
    tr = min(256, rb)

    def body(kc_ref, g_ref, r_ref, o_ref):
        acc = g_ref[...]
        for s in range(7):
            acc = acc + r_ref[s].astype(f32)
        o_ref[...] = acc

    return pl.pallas_call(
        body, name="piece_sum",
        grid_spec=pltpu.PrefetchScalarGridSpec(
            num_scalar_prefetch=1, grid=(rb // tr,),
            in_specs=[pl.BlockSpec((None, None, tr, cc), lambda r, kc: (kc[0], kc[1], r, 0)),
                      pl.BlockSpec((7, tr, cc), lambda r, kc: (0, r, 0))],
            out_specs=pl.BlockSpec((None, tr, cc), lambda r, kc: (kc[1], r, 0))),
        out_shape=jax.ShapeDtypeStruct((2, rb, cc), f32),
        compiler_params=_params("arbitrary"),
    )(kc_arr, g, recv)


def small_all_reduce(vec):
    def body(v_ref, o_ref, gat, ssem, rsem):
        x, y, c = _coords()
        me = 4 * x + 2 * y + c
        gat[me] = v_ref[...]
        sends = []
        for t in range(1, N_DEVICES):
            peer = (x ^ (t >> 2), y ^ ((t >> 1) & 1), c ^ (t & 1))
            cp = _remote(v_ref, gat.at[me], ssem.at[t - 1], rsem.at[t - 1], peer)
            cp.start()
            sends.append(cp)
        for t in range(1, N_DEVICES):
            peer = (x ^ (t >> 2), y ^ ((t >> 1) & 1), c ^ (t & 1))
            slot = gat.at[4 * peer[0] + 2 * peer[1] + peer[2]]
            _remote(slot, slot, ssem.at[t - 1], rsem.at[t - 1], peer).wait_recv()
        for cp in sends:
            cp.wait_send()
        acc = gat[0]
        for d in range(1, N_DEVICES):
            acc = acc + gat[d]
        o_ref[...] = acc

    return pl.pallas_call(
        body, name="small_all_reduce", out_shape=jax.ShapeDtypeStruct(vec.shape, vec.dtype),
        in_specs=[pl.BlockSpec(memory_space=pltpu.VMEM)], out_specs=pl.BlockSpec(memory_space=pltpu.VMEM),
        scratch_shapes=[pltpu.VMEM((N_DEVICES,) + vec.shape, vec.dtype), pltpu.SemaphoreType.DMA((N_DEVICES - 1,)),
                        pltpu.SemaphoreType.DMA((N_DEVICES - 1,))],
    )(vec)


def _adamw_math(w, g, m, v):
    m_new = ADAM_B1 * m + (1.0 - ADAM_B1) * g
    v_new = ADAM_B2 * v + (1.0 - ADAM_B2) * jnp.square(g)
    m_hat = m_new / (1.0 - ADAM_B1 ** ADAM_STEP)
    v_hat = v_new / (1.0 - ADAM_B2 ** ADAM_STEP)
    delta = -ADAM_LR * (m_hat / (jnp.sqrt(v_hat) + ADAM_EPS) + ADAM_WD * w)
    return delta, m_new, v_new


def adamw_shard(w, g0, g1, m, v):
    depth, rows, cols = w.shape
    half = rows // 2
    tr = min(256, half)
    nr = half // tr

    def body(w_ref, g0_ref, g1_ref, m_ref, v_ref, go_ref, d_ref, nm_ref, nv_ref):
        gv = jnp.where(pl.program_id(0) == 0, g0_ref[...], g1_ref[...])
        go_ref[...] = gv
        d_ref[...], nm_ref[...], nv_ref[...] = _adamw_math(w_ref[...], gv, m_ref[...], v_ref[...])

    spec = pl.BlockSpec((None, tr, cols), lambda l, h, r: (l, h * nr + r, 0))
    g0spec = pl.BlockSpec((None, tr, cols), lambda l, h, r: (jnp.where(l == 0, h, 1), jnp.where(l == 0, r, nr - 1), 0))
    g1spec = pl.BlockSpec((None, tr, cols), lambda l, h, r: (jnp.where(l == 1, h, 0), jnp.where(l == 1, r, 0), 0))
    return pl.pallas_call(
        body, name="adamw_shard", grid=(depth, 2, nr), in_specs=[spec, g0spec, g1spec, spec, spec], out_specs=[spec] * 4,
        out_shape=[jax.ShapeDtypeStruct(w.shape, f32)] * 4,
        compiler_params=_params("arbitrary", "arbitrary", "arbitrary"),
    )(w, g0, g1, m, v)


def adamw_small(w, g, m, v):
    def body(w_ref, g_ref, m_ref, v_ref, d_ref, nm_ref, nv_ref):
        d_ref[...], nm_ref[...], nv_ref[...] = _adamw_math(w_ref[...], g_ref[...], m_ref[...], v_ref[...])

    return pl.pallas_call(
        body, name="adamw_small", out_shape=[jax.ShapeDtypeStruct(w.shape, f32)] * 3,
    )(w, g, m, v)


WEIGHTS = ("mix_norm_g", "w_in", "q_gain", "k_gain", "sinks", "rel_bias", "conv_w", "conv_b", "dt_bias", "a_log", "d_skip",
           "ssm_norm_g", "w_out", "mlp_norm_g", "w_up", "w_down")
BIG = ("w_in", "w_out", "w_up", "w_down")
SMALL = tuple(n for n in WEIGHTS if n not in BIG)
PACK_COLS = 1024
PACK_ROWS = 16


def _pack(named, last=None):
    flat = jnp.concatenate([named[n].reshape(-1) for n in SMALL])
    tail = jnp.zeros((1,), f32) if last is None else last.reshape(1)
    pad = jnp.zeros((PACK_ROWS * PACK_COLS - flat.shape[0] - 1,), f32)
    return jnp.concatenate([flat, pad, tail]).reshape(PACK_ROWS, PACK_COLS)


def _unpack(buf, shapes):
    flat = buf.reshape(-1)
    out, at = {}, 0
    for n in SMALL:
        size = int(np.prod(shapes[n]))
        out[n] = flat[at:at + size].reshape(shapes[n])
        at += size
    return out


class _Exchange:
    GROUPS = {"A": (("w_up", 0), ("w_down", 0)), "B": (("w_in", 1), ("w_out", 1)), "C": (("w_up", 1), ("w_down", 1))}
    ICI_AT = {("mid", 0): "B", ("pre_out", 0): "C"}
    RELAY_AT = {("pre_out", 0): "A", ("pre_mlp", 0): "B", ("mid", 1): "C"}
    LAST = ("mix", 0)
    IN_FLIGHT = 2

    def __init__(self, wts, kc_arr):
        self.wts, self.kc_arr = wts, kc_arr
        self.own = {(n, l): wts[n][l].astype(bf16) for n in BIG for l in range(DEPTH)}
        now = gather_now([self.own["w_in", 0], self.own["w_out", 0]], wts["conv_w"])
        self.ready = {("w_in", 0): now[0], ("w_out", 0): now[1]}
        self.conv_w = jnp.transpose(now[2], (1, 2, 0, 3)).reshape(DEPTH, CONV_WIDTH, D_CONV)
        self.ici, self.relay = {}, {}
        self.gview, self.scatter, self.share, self.reduced = {}, [], [], {}
        self._start_ici("A", now[2])

    def _start_ici(self, g, after):
        srcs = [self.own[p] for p in self.GROUPS[g]]
        lands = [lax.empty((N_CHIPS,) + s.shape, s.dtype) for s in srcs]
        self.ici[g] = split_start("gather%s_ici_start" % g, _gather_maker("ici", len(srcs)), 3 * len(srcs), srcs + lands,
                                  after)
        return self.ici[g]["token"]

    def stage(self, name, after):
        if name == ("begin", 0):
            return self.ici["A"]["token"]
        tok = 0.0
        g = self.RELAY_AT.get(name)
        if g is not None:
            n = len(self.GROUPS[g])
            self.relay[g] = split_start("gather%s_relay_start" % g, _gather_maker("relay", n), 4 * n,
                                        split_wait(self.ici[g], after), after)
            tok = self.relay[g]["token"]
        if name in self.ICI_AT:
            tok = tok + self._start_ici(self.ICI_AT[name], after)
        return tok

    def _get(self, piece, after):
        if piece not in self.ready:
            g = [k for k, pieces in self.GROUPS.items() if piece in pieces][0]
            lands = split_wait(self.relay[g], after)[len(self.GROUPS[g]):]
            self.ready.update(zip(self.GROUPS[g], lands))
        return self.ready[piece]

    def w_in(self, l, after):
        return align_w_in(self._get(("w_in", l), after))

    def w_out(self, l, after):
        return self._get(("w_out", l), after).reshape(D_MODEL, D_MODEL)

    def mlp(self, l, after):
        return self._get(("w_up", l), after), self._get(("w_down", l), after)

    def _view(self, n, g):
        _, rows, cols = self.wts[n].shape
        return g.reshape(N_CHIPS, 2, rows // 2, cols)

    def grads(self, name, arrays, after):
        if name == self.LAST:
            self.held = (name, arrays)
            return 0.0
        return self._scatter(name, arrays, after) + self._advance(after, self.IN_FLIGHT)

    def flush(self, after):
        return self._scatter(*self.held, after) + self._advance(after, self.IN_FLIGHT)

    def _scatter(self, name, arrays, after):
        pieces = [(n, name[1]) for n in arrays]
        views = [self._view(n, g) for n, (g, _) in arrays.items()]
        sends = [g16.reshape(v.shape) for v, (_, g16) in zip(views, arrays.values())]
        self.gview.update(zip(pieces, views))
        lands = [lax.empty((7,) + v.shape[2:], bf16) for v in views]
        h = split_start("scatter_%s%d_start" % name, _scatter_maker(len(views)), 7 * len(views), sends + lands, after)
        self.scatter.append((pieces, h))
        return h["token"]

    def _take_share(self, after):
        pieces, h = self.share.pop(0)
        self.reduced.update(zip(pieces, split_wait(h, after)))

    def _take_scatter(self, after):
        pieces, h = self.scatter.pop(0)
        lands = split_wait(h, after)[len(pieces):]
        sums = [piece_sum(self.gview[p], land, self.kc_arr) for p, land in zip(pieces, lands)]
        hs = split_start(h["name"].replace("scatter", "share"), _share_maker(len(sums)), len(sums), sums, after)
        self.share.append((pieces, hs))
        return hs["token"]

    def _advance(self, after, newest):
        if self.share:
            self._take_share(after)
        return self._take_scatter(after) if len(self.scatter) > newest else 0.0

    def reduced_grads(self, names, after):
        want = [(n, l) for n in names for l in range(DEPTH)]
        while not all(p in self.reduced for p in want):
            if any(p in pieces for p in want for pieces, _ in self.share):
                self._take_share(after)
            else:
                self._take_scatter(after)
        return {n: [self.reduced[n, l] for l in range(DEPTH)] for n in names}


def kernel(x, mix_norm_g, w_in, q_gain, k_gain, sinks, rel_bias, conv_w, conv_b, dt_bias, a_log, d_skip, ssm_norm_g, w_out, mlp_norm_g, w_up, w_down, loss_target, m_mix_norm_g, m_w_in, m_q_gain, m_k_gain, m_sinks, m_rel_bias, m_conv_w, m_conv_b, m_dt_bias, m_a_log, m_d_skip, m_ssm_norm_g, m_w_out, m_mlp_norm_g, m_w_up, m_w_down, v_mix_norm_g, v_w_in, v_q_gain, v_k_gain, v_sinks, v_rel_bias, v_conv_w, v_conv_b, v_dt_bias, v_a_log, v_d_skip, v_ssm_norm_g, v_w_out, v_mlp_norm_g, v_w_up, v_w_down):
    wts = dict(mix_norm_g=mix_norm_g, w_in=w_in, q_gain=q_gain, k_gain=k_gain, sinks=sinks, rel_bias=rel_bias, conv_w=conv_w,
               conv_b=conv_b, dt_bias=dt_bias, a_log=a_log, d_skip=d_skip, ssm_norm_g=ssm_norm_g, w_out=w_out,
               mlp_norm_g=mlp_norm_g, w_up=w_up, w_down=w_down)
    mom = dict(mix_norm_g=m_mix_norm_g, w_in=m_w_in, q_gain=m_q_gain, k_gain=m_k_gain, sinks=m_sinks, rel_bias=m_rel_bias,
               conv_w=m_conv_w, conv_b=m_conv_b, dt_bias=m_dt_bias, a_log=m_a_log, d_skip=m_d_skip, ssm_norm_g=m_ssm_norm_g,
               w_out=m_w_out, mlp_norm_g=m_mlp_norm_g, w_up=m_w_up, w_down=m_w_down)
    var = dict(mix_norm_g=v_mix_norm_g, w_in=v_w_in, q_gain=v_q_gain, k_gain=v_k_gain, sinks=v_sinks, rel_bias=v_rel_bias,
               conv_w=v_conv_w, conv_b=v_conv_b, dt_bias=v_dt_bias, a_log=v_a_log, d_skip=v_d_skip, ssm_norm_g=v_ssm_norm_g,
               w_out=v_w_out, mlp_norm_g=v_mlp_norm_g, w_up=v_w_up, w_down=v_w_down)
    xi, yi, ci = _coords()
    k_me = 2 * xi + yi
    kc_arr = jnp.stack([k_me, ci]).astype(jnp.int32)

    prov = _Exchange(wts, kc_arr)
    small_w = {n: wts[n] for n in SMALL}
    small_w["conv_w"] = prov.conv_w
    loss, dx, grads, tok = local_step(x[0], loss_target[0], small_w, prov)

    small_shapes = {n: grads[n].shape for n in SMALL}
    small_sum = small_all_reduce(_pack(grads, loss) + tok)
    loss = small_sum[PACK_ROWS - 1, PACK_COLS - 1]
    tok = prov.flush(small_sum)
    small = _unpack(small_sum, small_shapes)
    cols = conv_w.shape[-1]
    small["conv_w"] = lax.dynamic_slice_in_dim(small["conv_w"], k_me * cols, cols, axis=2)
    g_out_d, d_out_d, m_out_d, v_out_d = {}, {}, {}, {}
    shard_shapes = {n: wts[n].shape for n in SMALL}
    d, nm, nv = adamw_small(_pack(wts), _pack(small) + tok, _pack(mom), _pack(var))
    for dst, buf in ((d_out_d, d), (m_out_d, nm), (v_out_d, nv)):
        dst.update(_unpack(buf, shard_shapes))
    g_out_d.update(small)

    after = d
    for names in (("w_up", "w_down"), ("w_in", "w_out")):
        for n, (g0, g1) in prov.reduced_grads(names, after).items():
            g_out_d[n], d_out_d[n], m_out_d[n], v_out_d[n] = adamw_shard(wts[n], g0, g1, mom[n], var[n])
            after = d_out_d[n]

    return (loss, dx[None], *[g_out_d[n] for n in WEIGHTS], *[d_out_d[n] for n in WEIGHTS],
            *[m_out_d[n] for n in WEIGHTS], *[v_out_d[n] for n in WEIGHTS])
```

```python
import numpy as np
import jax
import jax.numpy as jnp
from jax import lax
from jax.experimental import pallas as pl
from jax.experimental.pallas import tpu as pltpu

f32 = jnp.float32
bf16 = jnp.bfloat16

SEQ = 2048
D_MODEL = 1024
DEPTH = 2
HEAD_DIM = 64
N_Q_HEADS = 8
N_KV_HEADS = 2
Q_PER_KV = N_Q_HEADS // N_KV_HEADS
BLOCK = 128
N_BLOCKS = SEQ // BLOCK
N_BUCKETS = 32
MAX_DISTANCE = 128
SSM_HEADS = 8
SSM_HEAD_DIM = 64
SSM_GROUPS = 2
HEADS_PER_GROUP = SSM_HEADS // SSM_GROUPS
SSM_STATE = 128
CONV_WIDTH = 4
CHUNK = 128
N_CHUNKS = SEQ // CHUNK
D_FF = 4 * D_MODEL
D_ATTN = N_Q_HEADS * HEAD_DIM
D_KV = N_KV_HEADS * HEAD_DIM
D_SSM = SSM_HEADS * SSM_HEAD_DIM
D_BC = SSM_GROUPS * SSM_STATE
D_CONV = D_SSM + 2 * D_BC
D_IN = D_ATTN + 2 * D_KV + D_SSM + D_CONV + SSM_HEADS
EPS = 1e-6
NEG = -1e30
N_CHIPS = 4
FF_TILE = D_FF // N_CHIPS

LANE = 128
PW = D_ATTN + D_SSM + D_CONV + 2 * D_KV + LANE
OFF_Q, OFF_Z, OFF_X, OFF_K, OFF_V, OFF_DT = 0, 512, 1024, 2048, 2176, 2304

ADAM_LR = 0.001
ADAM_B1 = 0.9
ADAM_B2 = 0.999
ADAM_EPS = 1e-08
ADAM_WD = 0.01
ADAM_STEP = 10

VMEM_LIMIT = 56 * 1024 * 1024


def _params(*sem):
    return pltpu.CompilerParams(dimension_semantics=tuple(sem), vmem_limit_bytes=VMEM_LIMIT)


def _bdot(a, b):
    return jnp.dot(a.astype(bf16), b.astype(bf16), preferred_element_type=f32)


def _bdot_nt(a, b):
    return lax.dot_general(a.astype(bf16), b.astype(bf16), (((1,), (1,)), ((), ())), preferred_element_type=f32)


def _bdot_tn(a, b):
    return lax.dot_general(a.astype(bf16), b.astype(bf16), (((0,), (0,)), ((), ())), preferred_element_type=f32)


def _hdot(a, b):
    return jnp.dot(a, b, precision=lax.Precision.HIGHEST, preferred_element_type=f32)


def _sigmoid(x):
    return 1.0 / (1.0 + jnp.exp(-x))


def _softplus(x):
    return jnp.maximum(x, 0.0) + jnp.log1p(jnp.exp(-jnp.abs(x)))


def _rms(x):
    return lax.rsqrt(jnp.mean(x * x, axis=-1, keepdims=True) + EPS)


def _rms_bwd(dy, xhat, r, g):
    t = dy * g
    return r * (t - xhat * jnp.mean(t * xhat, axis=-1, keepdims=True))


def _full(shape):
    return pl.BlockSpec(shape, lambda *_: (0,) * len(shape))


def _bucket_table():
    qi = np.arange(BLOCK)[:, None]
    kj = np.arange(2 * BLOCK)[None, :]
    dist = qi + BLOCK - kj
    ok = (dist >= 0) & (dist < 128)
    d = np.clip(dist, 0, None)
    max_exact = N_BUCKETS // 2
    d_f = np.maximum(d, 1).astype(np.float32)
    large = max_exact + (np.log(d_f / np.float32(max_exact)) / np.float32(np.log(MAX_DISTANCE / max_exact))
                         * np.float32(N_BUCKETS - max_exact)).astype(np.int32)
    large = np.minimum(large, N_BUCKETS - 1)
    bucket = np.where(d < max_exact, d, large)
    return np.where(ok, bucket, -1).astype(np.int32)


def bias_build(rel_bias, bucket):
    def body(rel_ref, bkt_ref, o_ref):
        bkt = bkt_ref[...]
        for h in range(N_Q_HEADS):
            acc = jnp.where(bkt < 0, NEG, 0.0).astype(f32)
            for b in range(N_BUCKETS):
                acc = acc + jnp.where(bkt == b, rel_ref[b, h], 0.0)
            o_ref[h] = acc

    return pl.pallas_call(
        body, name="bias_build", out_shape=jax.ShapeDtypeStruct((N_Q_HEADS,) + bucket.shape, f32),
        in_specs=[pl.BlockSpec(memory_space=pltpu.SMEM), pl.BlockSpec(memory_space=pltpu.VMEM)],
        out_specs=pl.BlockSpec(memory_space=pltpu.VMEM),
    )(rel_bias, bucket)


def bias_bwd(dband0, dband1, bucket):
    def body(d0_ref, d1_ref, bkt_ref, o_ref):
        bkt = bkt_ref[...]
        o_ref[...] = jnp.zeros_like(o_ref)
        for h in range(N_Q_HEADS):
            d = d0_ref[h] + d1_ref[h]
            for b in range(N_BUCKETS):
                part = jnp.sum(jnp.where(bkt == b, d, 0.0), axis=1, keepdims=True)
                o_ref[b:b + 1, h:h + 1] = jnp.sum(part, axis=0, keepdims=True)

    return pl.pallas_call(
        body, name="bias_bwd", out_shape=jax.ShapeDtypeStruct((N_BUCKETS, LANE), f32),
    )(dband0, dband1, bucket)


W_IN_SHARD = D_IN // N_CHIPS
_ALIGNED_PIECES = ((0, 0, 512), (1, 190, 578), (2, 0, 124), (2, 124, 578), (3, 0, 570), (0, 512, 578), (1, 0, 62),
                   (1, 62, 190), (3, 570, 578))
_SHARD_PIECES = (((0, 512), (2048, 2114)), ((2114, 2176), (2176, 2304), (512, 900)), ((900, 1024), (1024, 1478)),
                 ((1478, 2048), (2304, 2312)))


def align_w_in(shards, tr=256):
    def body(s_ref, o_ref):
        parts = [s_ref[k, :, a:b] for k, a, b in _ALIGNED_PIECES]
        parts.append(jnp.zeros((tr, LANE - SSM_HEADS), s_ref.dtype))
        o_ref[...] = jnp.concatenate(parts, axis=-1)

    return pl.pallas_call(
        body, name="align_w_in", grid=(D_MODEL // tr,),
        in_specs=[pl.BlockSpec((N_CHIPS, tr, W_IN_SHARD), lambda i: (0, i, 0))],
        out_specs=pl.BlockSpec((tr, PW), lambda i: (i, 0)),
        out_shape=jax.ShapeDtypeStruct((D_MODEL, PW), shards.dtype),
        compiler_params=_params("arbitrary"),
    )(shards)


def split_w_in_grad(dw, tr=256):
    def body(d_ref, o_ref, o16_ref):
        for k, pieces in enumerate(_SHARD_PIECES):
            part = jnp.concatenate([d_ref[:, a:b] for a, b in pieces], axis=-1)
            o_ref[k] = part
            o16_ref[k] = part.astype(bf16)

    spec = pl.BlockSpec((N_CHIPS, tr, W_IN_SHARD), lambda i: (0, i, 0))
    return pl.pallas_call(
        body, name="split_w_in_grad", grid=(D_MODEL // tr,),
        in_specs=[pl.BlockSpec((tr, PW), lambda i: (i, 0))], out_specs=[spec, spec],
        out_shape=[jax.ShapeDtypeStruct((N_CHIPS, D_MODEL, W_IN_SHARD), f32),
                   jax.ShapeDtypeStruct((N_CHIPS, D_MODEL, W_IN_SHARD), bf16)],
        compiler_params=_params("arbitrary"),
    )(dw)

def in_fwd(x, g, w, tm=512):
    def body(x_ref, g_ref, w_ref, o_ref):
        xv = x_ref[...]
        h = xv * _rms(xv) * g_ref[...]
        o_ref[...] = _bdot(h, w_ref[...])

    return pl.pallas_call(
        body, name="in_fwd", grid=(SEQ // tm,),
        in_specs=[pl.BlockSpec((tm, D_MODEL), lambda i: (i, 0)), _full((1, D_MODEL)), _resident((D_MODEL, PW))],
        out_specs=pl.BlockSpec((tm, PW), lambda i: (i, 0)),
        out_shape=jax.ShapeDtypeStruct((SEQ, PW), f32),
        compiler_params=_params("arbitrary"),
    )(x, g, w)


def _resident(shape):
    return pl.BlockSpec(shape, lambda *_: (0,) * len(shape), pipeline_mode=pl.Buffered(1))


def in_bwd(dq, dz, dxbc, dk, dv, ddt, x, g, w, dres, tm=512):
    def body(dq_ref, dz_ref, dx_ref, dk_ref, dv_ref, ddt_ref, x_ref, g_ref, w_ref, dres_ref, o_ref, dw_ref, dg_ref):
        i = pl.program_id(0)

        @pl.when(i == 0)
        def _():
            dw_ref[...] = jnp.zeros_like(dw_ref)
            dg_ref[...] = jnp.zeros_like(dg_ref)

        dproj = jnp.concatenate([dq_ref[...], dz_ref[...], dx_ref[...], dk_ref[...], dv_ref[...], ddt_ref[...]],
                                axis=-1).astype(bf16)
        xv = x_ref[...]
        r = _rms(xv)
        xhat = xv * r
        gv = g_ref[...]
        h = xhat * gv
        dw_ref[...] += _bdot_tn(h, dproj)
        dh = _bdot_nt(dproj, w_ref[...])
        dg_ref[...] += jnp.sum(dh * xhat, axis=0, keepdims=True)
        o_ref[...] = dres_ref[...] + _rms_bwd(dh, xhat, r, gv)

    tok = lambda w_: pl.BlockSpec((tm, w_), lambda i: (i, 0))
    return pl.pallas_call(
        body, name="in_bwd", grid=(SEQ // tm,),
        in_specs=[tok(D_ATTN), tok(D_SSM), tok(D_CONV), tok(D_KV), tok(D_KV), tok(LANE), tok(D_MODEL),
                  _full((1, D_MODEL)), _resident((D_MODEL, PW)), tok(D_MODEL)],
        out_specs=[tok(D_MODEL), _resident((D_MODEL, PW)), _full((1, D_MODEL))],
        out_shape=[jax.ShapeDtypeStruct((SEQ, D_MODEL), f32), jax.ShapeDtypeStruct((D_MODEL, PW), f32),
                   jax.ShapeDtypeStruct((1, D_MODEL), f32)],
        compiler_params=_params("arbitrary"),
    )(dq, dz, dxbc, dk, dv, ddt, x, g, w, dres)


def _attn_softmax_t(qk, bias_t, sink, first, key_row):
    s = qk * (HEAD_DIM ** -0.5) + bias_t
    s = jnp.where(jnp.logical_and(first, key_row < BLOCK), NEG, s)
    m = jnp.maximum(jnp.max(s, axis=0, keepdims=True), sink)
    p = jnp.exp(s - m)
    psink = jnp.exp(sink - m)
    inv = 1.0 / (jnp.sum(p, axis=0, keepdims=True) + psink)
    return p * inv, psink * inv


def _rms_t(x_t):
    return lax.rsqrt(jnp.mean(x_t * x_t, axis=0, keepdims=True) + EPS)


def attn_fwd_t(proj, q_gain_col, k_gain, sinks, bias_t):
    kcol, vcol = OFF_K // D_KV, OFF_V // D_KV

    def body(q_ref, kc_ref, kp_ref, vc_ref, vp_ref, qg_ref, kg_ref, sink_ref, bias_ref, o_ref, ot_scr):
        n = pl.program_id(0)
        first = n == 0
        key_row = lax.broadcasted_iota(jnp.int32, (2 * BLOCK, BLOCK), 0)
        k2 = jnp.concatenate([kp_ref[...], kc_ref[...]], axis=0)
        v_t = jnp.concatenate([vp_ref[...], vc_ref[...]], axis=0).T
        q_t = q_ref[...].T
        qg = jnp.broadcast_to(qg_ref[...], (HEAD_DIM, BLOCK))
        kg = kg_ref[...]
        for hk in range(N_KV_HEADS):
            sl = slice(hk * HEAD_DIM, (hk + 1) * HEAD_DIM)
            kk = k2[:, sl]
            kn = (kk * _rms(kk) * kg).astype(bf16)
            vt = v_t[sl, :].astype(bf16)
            heads = range(hk * Q_PER_KV, (hk + 1) * Q_PER_KV)
            qns = []
            for h in heads:
                qh = q_t[h * HEAD_DIM:(h + 1) * HEAD_DIM, :]
                qns.append(qh * _rms_t(qh) * qg)
            scores = [_bdot(kn, qn) for qn in qns]
            for h, s in zip(heads, scores):
                p, _ = _attn_softmax_t(s, bias_ref[h], sink_ref[h], first, key_row)
                ot_scr[h * HEAD_DIM:(h + 1) * HEAD_DIM, :] = _bdot(vt, p)
        o_ref[...] = ot_scr[...].T

    prev = lambda n: jnp.maximum(n - 1, 0)
    return pl.pallas_call(
        body, name="attn_fwd", grid=(N_BLOCKS,),
        in_specs=[pl.BlockSpec((BLOCK, D_ATTN), lambda n: (n, 0)),
                  pl.BlockSpec((BLOCK, D_KV), lambda n: (n, kcol)), pl.BlockSpec((BLOCK, D_KV), lambda n: (prev(n), kcol)),
                  pl.BlockSpec((BLOCK, D_KV), lambda n: (n, vcol)), pl.BlockSpec((BLOCK, D_KV), lambda n: (prev(n), vcol)),
                  _full((HEAD_DIM, 1)), _full((1, HEAD_DIM)), pl.BlockSpec(memory_space=pltpu.SMEM),
                  _full((N_Q_HEADS, 2 * BLOCK, BLOCK))],
        out_specs=pl.BlockSpec((BLOCK, D_ATTN), lambda n: (n, 0)),
        out_shape=jax.ShapeDtypeStruct((SEQ, D_ATTN), f32),
        scratch_shapes=[pltpu.VMEM((D_ATTN, BLOCK), f32)],
        compiler_params=_params("arbitrary"),
    )(proj, proj, proj, proj, proj, q_gain_col, k_gain, sinks, bias_t)


def attn_bwd_t(proj, d_out, q_gain_col, k_gain, sinks, bias_t):
    kcol, vcol = OFF_K // D_KV, OFF_V // D_KV

    def body(q_ref, kc_ref, kp_ref, vc_ref, vp_ref, do_ref, qg_ref, kg_ref, sink_ref, bias_ref,
             dq_ref, dk_ref, dv_ref, dband_ref, dsink_ref, dqg_ref, dkg_ref, dkn_scr, dv_scr, dqt_scr, dsink_acc, dqg_acc):
        i = pl.program_id(0)
        first = i == N_BLOCKS - 1

        @pl.when(i == 0)
        def _():
            for ref in (dband_ref, dkg_ref, dkn_scr, dv_scr, dsink_acc, dqg_acc):
                ref[...] = jnp.zeros_like(ref)

        key_row = lax.broadcasted_iota(jnp.int32, (2 * BLOCK, BLOCK), 0)
        k2 = jnp.concatenate([kp_ref[...], kc_ref[...]], axis=0)
        v2 = jnp.concatenate([vp_ref[...], vc_ref[...]], axis=0)
        q_t = q_ref[...].T
        do_t = do_ref[...].T
        qg = jnp.broadcast_to(qg_ref[...], (HEAD_DIM, BLOCK))
        kg = kg_ref[...]
        scale = HEAD_DIM ** -0.5
        for hk in range(N_KV_HEADS):
            sl = slice(hk * HEAD_DIM, (hk + 1) * HEAD_DIM)
            kk = k2[:, sl]
            rk = _rms(kk)
            khat = kk * rk
            kn = (khat * kg).astype(bf16)
            vb = v2[:, sl].astype(bf16)
            dkn = jnp.zeros((2 * BLOCK, HEAD_DIM), f32)
            dvv = jnp.zeros((2 * BLOCK, HEAD_DIM), f32)
            heads = range(hk * Q_PER_KV, (hk + 1) * Q_PER_KV)
            rqs, qhats, qns, d_os = [], [], [], []
            for h in heads:
                hs = slice(h * HEAD_DIM, (h + 1) * HEAD_DIM)
                qh = q_t[hs, :]
                rqs.append(_rms_t(qh))
                qhats.append(qh * rqs[-1])
                qns.append((qhats[-1] * qg).astype(bf16))
                d_os.append(do_t[hs, :].astype(bf16))
            scores = [_bdot(kn, qn) for qn in qns]
            dps = [_bdot(vb, d_o) for d_o in d_os]
            ps, dss = [], []
            for h, s, dp in zip(heads, scores, dps):
                p, psink = _attn_softmax_t(s, bias_ref[h], sink_ref[h], first, key_row)
                delta = jnp.sum(p * dp, axis=0, keepdims=True)
                ds = p * (dp - delta)
                dband_ref[h] += ds
                dsink_acc[h:h + 1, :] += -(psink * delta)
                ps.append(p.astype(bf16))
                dss.append(ds.astype(bf16))
            dqns = [_bdot_tn(kn, ds) * scale for ds in dss]
            for ds, qn, p, d_o in zip(dss, qns, ps, d_os):
                dkn = dkn + _bdot_nt(ds, qn) * scale
                dvv = dvv + _bdot_nt(p, d_o)
            for h, dqn, rq, qhat in zip(heads, dqns, rqs, qhats):
                dqg_acc[...] += dqn * qhat
                t = dqn * qg
                dqt_scr[h * HEAD_DIM:(h + 1) * HEAD_DIM, :] = rq * (t - qhat * jnp.mean(t * qhat, axis=0, keepdims=True))
            dkn_cur = dkn[BLOCK:] + dkn_scr[:, sl]
            dkn_scr[:, sl] = dkn[:BLOCK]
            khat_c, rk_c = khat[BLOCK:], rk[BLOCK:]
            dkg_ref[...] += jnp.sum(dkn_cur * khat_c, axis=0, keepdims=True)
            dk_ref[:, sl] = _rms_bwd(dkn_cur, khat_c, rk_c, kg)
            dv_ref[:, sl] = dvv[BLOCK:] + dv_scr[:, sl]
            dv_scr[:, sl] = dvv[:BLOCK]
        dq_ref[...] = dqt_scr[...].T

        @pl.when(i == N_BLOCKS - 1)
        def _():
            dsink_ref[...] = jnp.sum(dsink_acc[...], axis=1, keepdims=True)
            dqg_ref[...] = jnp.sum(dqg_acc[...], axis=1, keepdims=True)

    blk = lambda i: N_BLOCKS - 1 - i
    prev = lambda i: jnp.maximum(N_BLOCKS - 2 - i, 0)
    return pl.pallas_call(
        body, name="attn_bwd", grid=(N_BLOCKS,),
        in_specs=[pl.BlockSpec((BLOCK, D_ATTN), lambda i: (blk(i), 0)),
                  pl.BlockSpec((BLOCK, D_KV), lambda i: (blk(i), kcol)), pl.BlockSpec((BLOCK, D_KV), lambda i: (prev(i), kcol)),
                  pl.BlockSpec((BLOCK, D_KV), lambda i: (blk(i), vcol)), pl.BlockSpec((BLOCK, D_KV), lambda i: (prev(i), vcol)),
                  pl.BlockSpec((BLOCK, D_ATTN), lambda i: (blk(i), 0)),
                  _full((HEAD_DIM, 1)), _full((1, HEAD_DIM)), pl.BlockSpec(memory_space=pltpu.SMEM),
                  _full((N_Q_HEADS, 2 * BLOCK, BLOCK))],
        out_specs=[pl.BlockSpec((BLOCK, D_ATTN), lambda i: (blk(i), 0)), pl.BlockSpec((BLOCK, D_KV), lambda i: (blk(i), 0)),
                   pl.BlockSpec((BLOCK, D_KV), lambda i: (blk(i), 0)), _full((N_Q_HEADS, 2 * BLOCK, BLOCK)),
                   _full((N_Q_HEADS, 1)), _full((HEAD_DIM, 1)), _full((1, HEAD_DIM))],
        out_shape=[jax.ShapeDtypeStruct((SEQ, D_ATTN), f32), jax.ShapeDtypeStruct((SEQ, D_KV), f32),
                   jax.ShapeDtypeStruct((SEQ, D_KV), f32), jax.ShapeDtypeStruct((N_Q_HEADS, 2 * BLOCK, BLOCK), f32),
                   jax.ShapeDtypeStruct((N_Q_HEADS, 1), f32), jax.ShapeDtypeStruct((HEAD_DIM, 1), f32),
                   jax.ShapeDtypeStruct((1, HEAD_DIM), f32)],
        scratch_shapes=[pltpu.VMEM((BLOCK, D_KV), f32), pltpu.VMEM((BLOCK, D_KV), f32), pltpu.VMEM((D_ATTN, BLOCK), f32),
                        pltpu.VMEM((N_Q_HEADS, BLOCK), f32), pltpu.VMEM((HEAD_DIM, BLOCK), f32)],
        compiler_params=_params("arbitrary"),
    )(proj, proj, proj, proj, proj, d_out, q_gain_col, k_gain, sinks, bias_t)


SUBLANES = 8


def _shift_down(u, s, row8):
    if s == 0:
        return u
    r = pltpu.roll(u, s, 0)
    return jnp.concatenate([jnp.where(row8 >= s, r[:SUBLANES], 0.0), r[SUBLANES:]], axis=0)


def _shift_up(u, s, row8):
    if s == 0:
        return u
    r = pltpu.roll(u, SEQ - s, 0)
    return jnp.concatenate([r[:-SUBLANES], jnp.where(row8 < SUBLANES - s, r[-SUBLANES:], 0.0)], axis=0)


def conv_fwd(proj, conv_w, conv_b):
    xcol = OFF_X // LANE

    def body(u_ref, w_ref, b_ref, o_ref):
        u = u_ref[...]
        row = lax.broadcasted_iota(jnp.int32, (SUBLANES, LANE), 0)
        pre = b_ref[...] + jnp.zeros_like(u)
        for k in range(CONV_WIDTH):
            pre = pre + w_ref[k:k + 1, :] * _shift_down(u, CONV_WIDTH - 1 - k, row)
        o_ref[...] = pre * _sigmoid(pre)

    return pl.pallas_call(
        body, name="conv_fwd", grid=(D_CONV // LANE,),
        in_specs=[pl.BlockSpec((SEQ, LANE), lambda j: (0, xcol + j)), pl.BlockSpec((CONV_WIDTH, LANE), lambda j: (0, j)),
                  pl.BlockSpec((1, LANE), lambda j: (0, j))],
        out_specs=pl.BlockSpec((SEQ, LANE), lambda j: (0, j)),
        out_shape=jax.ShapeDtypeStruct((SEQ, D_CONV), f32),
        compiler_params=_params("arbitrary"),
    )(proj, conv_w, conv_b)


def conv_bwd(proj, d_act, conv_w, conv_b):
    xcol = OFF_X // LANE

    def body(u_ref, da_ref, w_ref, b_ref, du_ref, dw_ref, db_ref):
        u = u_ref[...]
        row = lax.broadcasted_iota(jnp.int32, (SUBLANES, LANE), 0)
        shifted = [_shift_down(u, CONV_WIDTH - 1 - k, row) for k in range(CONV_WIDTH)]
        pre = b_ref[...] + jnp.zeros_like(u)
        for k in range(CONV_WIDTH):
            pre = pre + w_ref[k:k + 1, :] * shifted[k]
        sg = _sigmoid(pre)
        dpre = da_ref[...] * (sg * (1.0 + pre * (1.0 - sg)))
        db_ref[...] = jnp.sum(dpre, axis=0, keepdims=True)
        du = jnp.zeros_like(u)
        for k in range(CONV_WIDTH):
            dw_ref[k:k + 1, :] = jnp.sum(dpre * shifted[k], axis=0, keepdims=True)
            du = du + w_ref[k:k + 1, :] * _shift_up(dpre, CONV_WIDTH - 1 - k, row)
        du_ref[...] = du

    return pl.pallas_call(
        body, name="conv_bwd", grid=(D_CONV // LANE,),
        in_specs=[pl.BlockSpec((SEQ, LANE), lambda j: (0, xcol + j)), pl.BlockSpec((SEQ, LANE), lambda j: (0, j)),
                  pl.BlockSpec((CONV_WIDTH, LANE), lambda j: (0, j)), pl.BlockSpec((1, LANE), lambda j: (0, j))],
        out_specs=[pl.BlockSpec((SEQ, LANE), lambda j: (0, j)), pl.BlockSpec((CONV_WIDTH, LANE), lambda j: (0, j)),
                   pl.BlockSpec((1, LANE), lambda j: (0, j))],
        out_shape=[jax.ShapeDtypeStruct((SEQ, D_CONV), f32), jax.ShapeDtypeStruct((CONV_WIDTH, D_CONV), f32),
                   jax.ShapeDtypeStruct((1, D_CONV), f32)],
        compiler_params=_params("arbitrary"),
    )(proj, d_act, conv_w, conv_b)


def _ssd_chunk_common(dt_raw, dtb, alog):
    row = lax.broadcasted_iota(jnp.int32, (CHUNK, CHUNK), 0)
    col = lax.broadcasted_iota(jnp.int32, (CHUNK, CHUNK), 1)
    tri = (row >= col).astype(f32)
    strict = (row > col).astype(f32)
    dtp = _softplus(dt_raw + dtb)
    a_row = -jnp.exp(alog)
    d_a = dtp * a_row
    cs = _hdot(tri, d_a)
    cs_last = cs[CHUNK - 1:CHUNK, :]
    return row, col, dtp, a_row, cs, cs.T, cs_last


def _seg_decay(cs, cs_t, hd, row, col):
    seg = cs[:, hd:hd + 1] - cs_t[hd:hd + 1, :]
    return jnp.where(row >= col, jnp.exp(seg), 0.0)


GROUP_W = HEADS_PER_GROUP * SSM_HEAD_DIM


def _group_indicator(g):
    j = lax.broadcasted_iota(jnp.int32, (GROUP_W, LANE), 0)
    lane = lax.broadcasted_iota(jnp.int32, (GROUP_W, LANE), 1)
    return (lane == g * HEADS_PER_GROUP + j // SSM_HEAD_DIM).astype(bf16)


def _bf16_pieces(a, n):
    pieces = []
    for _ in range(n):
        p = a.astype(bf16)
        pieces.append(p)
        a = a - p.astype(f32)
    return pieces


def _head_spread(a, ind):
    return sum(lax.dot_general(p, ind, (((1,), (1,)), ((), ())), preferred_element_type=f32) for p in _bf16_pieces(a, 3))


def _head_sums(a, ind):
    return sum(jnp.dot(p, ind, preferred_element_type=f32) for p in _bf16_pieces(a, 2))


def ssd_fwd_g(act, proj, dt_bias, a_log, d_skip, norm_g):
    zcol, dtcol = OFF_Z // D_SSM, OFF_DT // LANE

    def body(act_ref, z_ref, dt_ref, dtb_ref, alog_ref, dsk_ref, ng_ref, out_ref, ypre_ref, st_ref, state):
        c = pl.program_id(0)

        @pl.when(c == 0)
        def _():
            state[...] = jnp.zeros_like(state)

        row, col, dtp, a_row, cs, cs_t, cs_last = _ssd_chunk_common(dt_ref[...], dtb_ref[...], alog_ref[...])
        e_cs = jnp.exp(cs)
        dte = jnp.exp(cs_last - cs)
        rows8 = jnp.concatenate([jnp.exp(cs_last), dsk_ref[...], jnp.zeros((6, LANE), f32)], axis=0)
        z = z_ref[...]
        sz = z * _sigmoid(z)
        ng = ng_ref[...]
        for g in range(SSM_GROUPS):
            gs = slice(g * GROUP_W, (g + 1) * GROUP_W)
            ind = _group_indicator(g)
            xg = act_ref[:, gs]
            bg = act_ref[:, D_SSM + g * SSM_STATE:D_SSM + (g + 1) * SSM_STATE]
            cg = act_ref[:, D_SSM + D_BC + g * SSM_STATE:D_SSM + D_BC + (g + 1) * SSM_STATE]
            dt_e, e_e, dte_e = _head_spread(dtp, ind), _head_spread(e_cs, ind), _head_spread(dte, ind)
            rows_e = _head_spread(rows8, ind)
            ecl_e, dsk_e = rows_e[0:1], rows_e[1:2]
            xdt = xg * dt_e
            prev = state[g]
            st_ref[0, g] = prev
            cb = _bdot_nt(cg, bg)
            goff = _bdot(cg, prev)
            snew = _bdot_tn(bg, xdt * dte_e)
            heads = range(g * HEADS_PER_GROUP, (g + 1) * HEADS_PER_GROUP)
            ms = [cb * _seg_decay(cs, cs_t, hd, row, col) for hd in heads]
            yd = [_bdot(m, xdt[:, r * SSM_HEAD_DIM:(r + 1) * SSM_HEAD_DIM]) for r, m in enumerate(ms)]
            y = jnp.concatenate(yd, axis=1) + e_e * goff + xg * dsk_e
            state[g] = prev * ecl_e + snew
            ypre_ref[:, gs] = y
            part = y * sz[:, gs]
            out_ref[:, gs] = part * _rms(part) * ng[:, gs]

    return pl.pallas_call(
        body, name="ssd_fwd", grid=(N_CHUNKS,),
        in_specs=[pl.BlockSpec((CHUNK, D_CONV), lambda c: (c, 0)), pl.BlockSpec((CHUNK, D_SSM), lambda c: (c, zcol)),
                  pl.BlockSpec((CHUNK, LANE), lambda c: (c, dtcol)), _full((1, LANE)), _full((1, LANE)), _full((1, LANE)),
                  _full((1, D_SSM))],
        out_specs=[pl.BlockSpec((CHUNK, D_SSM), lambda c: (c, 0)), pl.BlockSpec((CHUNK, D_SSM), lambda c: (c, 0)),
                   pl.BlockSpec((1, SSM_GROUPS, SSM_STATE, GROUP_W), lambda c: (c, 0, 0, 0))],
        out_shape=[jax.ShapeDtypeStruct((SEQ, D_SSM), f32), jax.ShapeDtypeStruct((SEQ, D_SSM), f32),
                   jax.ShapeDtypeStruct((N_CHUNKS, SSM_GROUPS, SSM_STATE, GROUP_W), f32)],
        scratch_shapes=[pltpu.VMEM((SSM_GROUPS, SSM_STATE, GROUP_W), f32)],
        compiler_params=_params("arbitrary"),
    )(act, proj, proj, dt_bias, a_log, d_skip, norm_g)


def ssd_bwd_g(act, proj, ypre, states, d_out, dt_bias, a_log, d_skip, norm_g):
    zcol, dtcol = OFF_Z // D_SSM, OFF_DT // LANE

    def body(act_ref, z_ref, dt_ref, ypre_ref, st_ref, do_ref, dtb_ref, alog_ref, dsk_ref, ng_ref,
             dact_ref, ddt_ref, dz_ref, dng_ref, dpar_ref, dstate):
        i = pl.program_id(0)

        @pl.when(i == 0)
        def _():
            for ref in (dng_ref, dpar_ref, dstate):
                ref[...] = jnp.zeros_like(ref)

        row, col, dtp, a_row, cs, cs_t, cs_last = _ssd_chunk_common(dt_ref[...], dtb_ref[...], alog_ref[...])
        upper = (row <= col).astype(f32)
        lane = lax.broadcasted_iota(jnp.int32, (CHUNK, LANE), 1)
        rowl = lax.broadcasted_iota(jnp.int32, (CHUNK, LANE), 0)
        e_cs = jnp.exp(cs)
        dte = jnp.exp(cs_last - cs)
        ecl = jnp.exp(cs_last)
        rows8 = jnp.concatenate([ecl, dsk_ref[...], jnp.zeros((6, LANE), f32)], axis=0)
        z = z_ref[...]
        sgz = _sigmoid(z)
        sz = z * sgz
        ng = ng_ref[...]
        ddt_mat = jnp.zeros((CHUNK, LANE), f32)
        dcs_mat = jnp.zeros((CHUNK, LANE), f32)
        dcs_t = jnp.zeros((LANE, CHUNK), f32)
        dcsl_row = jnp.zeros((1, LANE), f32)
        dd_row = jnp.zeros((1, LANE), f32)
        for g in range(SSM_GROUPS):
            gs = slice(g * GROUP_W, (g + 1) * GROUP_W)
            bsl = slice(D_SSM + g * SSM_STATE, D_SSM + (g + 1) * SSM_STATE)
            csl = slice(D_SSM + D_BC + g * SSM_STATE, D_SSM + D_BC + (g + 1) * SSM_STATE)
            ind = _group_indicator(g)
            y = ypre_ref[:, gs]
            part = y * sz[:, gs]
            r = _rms(part)
            yhat = part * r
            d_o = do_ref[:, gs]
            dng_ref[:, gs] += jnp.sum(d_o * yhat, axis=0, keepdims=True)
            dyz = _rms_bwd(d_o, yhat, r, ng[:, gs])
            dy = dyz * sz[:, gs]
            dz_ref[:, gs] = dyz * y * (sgz[:, gs] * (1.0 + z[:, gs] * (1.0 - sgz[:, gs])))

            xg = act_ref[:, gs]
            bg = act_ref[:, bsl]
            cg = act_ref[:, csl]
            dt_e, e_e, dte_e = _head_spread(dtp, ind), _head_spread(e_cs, ind), _head_spread(dte, ind)
            rows_e = _head_spread(rows8, ind)
            ecl_e, dsk_e = rows_e[0:1], rows_e[1:2]
            xdt = xg * dt_e
            prev = st_ref[0, g]
            dh = dstate[g]
            heads = range(g * HEADS_PER_GROUP, (g + 1) * HEADS_PER_GROUP)
            hsl = [slice(r_ * SSM_HEAD_DIM, (r_ + 1) * SSM_HEAD_DIM) for r_ in range(HEADS_PER_GROUP)]
            cb = _bdot_nt(cg, bg)
            lms = [_seg_decay(cs, cs_t, hd, row, col) for hd in heads]
            ms = [cb * lm for lm in lms]
            gmat = _bdot(cg, prev)
            dgm = dy * e_e
            dcg = _bdot_nt(dgm, prev)
            dprev = _bdot_tn(cg, dgm)
            dbg = _bdot_nt(xdt * dte_e, dh)
            dw = _bdot(bg, dh)
            dms = [_bdot_nt(dy[:, s_], xdt[:, s_]) for s_ in hsl]
            dxdts = [_bdot_tn(m, dy[:, s_]) for m, s_ in zip(ms, hsl)]
            dxdt = jnp.concatenate(dxdts, axis=1) + dw * dte_e
            dact_ref[:, gs] = dy * dsk_e + dxdt * dt_e
            dstate[g] = dprev + dh * ecl_e
            dcb = jnp.zeros((CHUNK, CHUNK), f32)
            for hd, dm, lm, m in zip(heads, dms, lms, ms):
                dcb = dcb + dm * lm
                dseg = dm * m
                dcs_mat = dcs_mat + jnp.where(lane == hd, jnp.sum(dseg, axis=1, keepdims=True), 0.0)
                dcs_t = jnp.where(row == hd, jnp.sum(dseg, axis=0, keepdims=True), dcs_t)
            dact_ref[:, bsl] = dbg + _bdot_tn(dcb, cg)
            dact_ref[:, csl] = dcg + _bdot(dcb, bg)
            ddte = _head_sums(dw * xdt, ind) * dte
            dcs_mat = dcs_mat + _head_sums(dy * gmat, ind) * e_cs - ddte
            ddt_mat = ddt_mat + _head_sums(dxdt * xg, ind)
            dcsl_row = (dcsl_row + jnp.sum(ddte, axis=0, keepdims=True)
                        + jnp.sum(_head_sums(dh * prev, ind), axis=0, keepdims=True) * ecl)
            dd_row = dd_row + jnp.sum(_head_sums(dy * xg, ind), axis=0, keepdims=True)
        dcs_mat = dcs_mat - dcs_t.T + jnp.where(rowl == CHUNK - 1, dcsl_row, 0.0)
        dda = _hdot(upper, dcs_mat)
        ddt_mat = ddt_mat + dda * a_row
        da_row = jnp.sum(dda * dtp, axis=0, keepdims=True)
        ddt_raw = ddt_mat * _sigmoid(dt_ref[...] + dtb_ref[...])
        ddt_ref[...] = ddt_raw
        dpar_ref[0:1, :] += jnp.sum(ddt_raw, axis=0, keepdims=True)
        dpar_ref[1:2, :] += da_row * a_row
        dpar_ref[2:3, :] += dd_row

    blk = lambda i: N_CHUNKS - 1 - i
    return pl.pallas_call(
        body, name="ssd_bwd", grid=(N_CHUNKS,),
        in_specs=[pl.BlockSpec((CHUNK, D_CONV), lambda i: (blk(i), 0)), pl.BlockSpec((CHUNK, D_SSM), lambda i: (blk(i), zcol)),
                  pl.BlockSpec((CHUNK, LANE), lambda i: (blk(i), dtcol)), pl.BlockSpec((CHUNK, D_SSM), lambda i: (blk(i), 0)),
                  pl.BlockSpec((1, SSM_GROUPS, SSM_STATE, GROUP_W), lambda i: (blk(i), 0, 0, 0)),
                  pl.BlockSpec((CHUNK, D_SSM), lambda i: (blk(i), 0)),
                  _full((1, LANE)), _full((1, LANE)), _full((1, LANE)), _full((1, D_SSM))],
        out_specs=[pl.BlockSpec((CHUNK, D_CONV), lambda i: (blk(i), 0)), pl.BlockSpec((CHUNK, LANE), lambda i: (blk(i), 0)),
                   pl.BlockSpec((CHUNK, D_SSM), lambda i: (blk(i), 0)), _full((1, D_SSM)), _full((8, LANE))],
        out_shape=[jax.ShapeDtypeStruct((SEQ, D_CONV), f32), jax.ShapeDtypeStruct((SEQ, LANE), f32),
                   jax.ShapeDtypeStruct((SEQ, D_SSM), f32), jax.ShapeDtypeStruct((1, D_SSM), f32),
                   jax.ShapeDtypeStruct((8, LANE), f32)],
        scratch_shapes=[pltpu.VMEM((SSM_GROUPS, SSM_STATE, GROUP_W), f32)],
        compiler_params=_params("arbitrary"),
    )(act, proj, proj, ypre, states, d_out, dt_bias, a_log, d_skip, norm_g)


def out_fwd(x, attn, ssm, w_out, tm=512):
    def body(x_ref, a_ref, s_ref, w_ref, o_ref):
        o_ref[...] = x_ref[...] + _bdot(a_ref[...], w_ref[:D_ATTN, :]) + _bdot(s_ref[...], w_ref[D_ATTN:, :])

    tok = lambda w_: pl.BlockSpec((tm, w_), lambda i: (i, 0))
    return pl.pallas_call(
        body, name="out_fwd", grid=(SEQ // tm,),
        in_specs=[tok(D_MODEL), tok(D_ATTN), tok(D_SSM), _full((D_MODEL, D_MODEL))],
        out_specs=tok(D_MODEL), out_shape=jax.ShapeDtypeStruct((SEQ, D_MODEL), f32),
        compiler_params=_params("arbitrary"),
    )(x, attn, ssm, w_out)


def out_bwd(dx1, attn, ssm, w_out, tm=512):
    nt = SEQ // tm

    def body(d_ref, a_ref, s_ref, w_ref, da_ref, ds_ref, dw_ref, dw16_ref):
        i = pl.program_id(0)

        @pl.when(i == 0)
        def _():
            dw_ref[...] = jnp.zeros_like(dw_ref)

        d = d_ref[...].astype(bf16)
        dcat = _bdot_nt(d, w_ref[...])
        da_ref[...] = dcat[:, :D_ATTN]
        ds_ref[...] = dcat[:, D_ATTN:]
        dw_ref[:D_ATTN, :] += _bdot_tn(a_ref[...], d)
        dw_ref[D_ATTN:, :] += _bdot_tn(s_ref[...], d)

        @pl.when(i == nt - 1)
        def _():
            dw16_ref[...] = dw_ref[...].astype(bf16)

    tok = lambda w_: pl.BlockSpec((tm, w_), lambda i: (i, 0))
    return pl.pallas_call(
        body, name="out_bwd", grid=(nt,),
        in_specs=[tok(D_MODEL), tok(D_ATTN), tok(D_SSM), _resident((D_MODEL, D_MODEL))],
        out_specs=[tok(D_ATTN), tok(D_SSM), _resident((D_MODEL, D_MODEL)), _resident((D_MODEL, D_MODEL))],
        out_shape=[jax.ShapeDtypeStruct((SEQ, D_ATTN), f32), jax.ShapeDtypeStruct((SEQ, D_SSM), f32),
                   jax.ShapeDtypeStruct((D_MODEL, D_MODEL), f32), jax.ShapeDtypeStruct((D_MODEL, D_MODEL), bf16)],
        compiler_params=_params("arbitrary"),
    )(dx1, attn, ssm, w_out)


MLP_SUB = 256


def mlp_fwd(x1, g, w_up, w_down, tm=1024):
    def body(x_ref, g_ref, wu_ref, wd_ref, o_ref, u_ref, h_scr):
        j = pl.program_id(1)

        @pl.when(j == 0)
        def _():
            xv = x_ref[...]
            h_scr[...] = (xv * _rms(xv) * g_ref[...]).astype(bf16)
            o_ref[...] = xv

        for r in range(tm // MLP_SUB):
            rows = slice(r * MLP_SUB, (r + 1) * MLP_SUB)
            u = jnp.dot(h_scr[rows, :], wu_ref[...], preferred_element_type=f32)
            u_ref[rows, :] = u
            a = jnp.square(jnp.maximum(u, 0.0))
            o_ref[rows, :] += _bdot(a, wd_ref[...])

    return pl.pallas_call(
        body, name="mlp_fwd", grid=(SEQ // tm, N_CHIPS),
        in_specs=[pl.BlockSpec((tm, D_MODEL), lambda i, j: (i, 0)), _full((1, D_MODEL)),
                  pl.BlockSpec((None, D_MODEL, FF_TILE), lambda i, j: (j, 0, 0)),
                  pl.BlockSpec((None, FF_TILE, D_MODEL), lambda i, j: (j, 0, 0))],
        out_specs=[pl.BlockSpec((tm, D_MODEL), lambda i, j: (i, 0)), pl.BlockSpec((tm, FF_TILE), lambda i, j: (i, j))],
        out_shape=[jax.ShapeDtypeStruct((SEQ, D_MODEL), f32), jax.ShapeDtypeStruct((SEQ, D_FF), f32)],
        scratch_shapes=[pltpu.VMEM((tm, D_MODEL), bf16)],
        compiler_params=_params("arbitrary", "arbitrary"),
    )(x1, g, w_up, w_down)


def mlp_bwd_data(dx2, u, x1, g, w_up, w_down, tm=1024):
    def body(d_ref, u_ref, x_ref, g_ref, wu_ref, wd_ref, dx_ref, du_ref, dg_ref, dh_scr):
        i, j = pl.program_id(0), pl.program_id(1)

        @pl.when(jnp.logical_and(i == 0, j == 0))
        def _():
            dg_ref[...] = jnp.zeros_like(dg_ref)

        @pl.when(j == 0)
        def _():
            dh_scr[...] = jnp.zeros_like(dh_scr)

        for r in range(tm // MLP_SUB):
            rows = slice(r * MLP_SUB, (r + 1) * MLP_SUB)
            da = _bdot_nt(d_ref[rows, :], wd_ref[...])
            du = (da * (2.0 * jnp.maximum(u_ref[rows, :], 0.0))).astype(bf16)
            du_ref[rows, :] = du
            dh_scr[rows, :] += _bdot_nt(du, wu_ref[...])

        @pl.when(j == N_CHIPS - 1)
        def _():
            xv = x_ref[...]
            r = _rms(xv)
            xhat = xv * r
            dh = dh_scr[...]
            dg_ref[...] += jnp.sum(dh * xhat, axis=0, keepdims=True)
            dx_ref[...] = d_ref[...] + _rms_bwd(dh, xhat, r, g_ref[...])

    return pl.pallas_call(
        body, name="mlp_bwd_data", grid=(SEQ // tm, N_CHIPS),
        in_specs=[pl.BlockSpec((tm, D_MODEL), lambda i, j: (i, 0)), pl.BlockSpec((tm, FF_TILE), lambda i, j: (i, j)),
                  pl.BlockSpec((tm, D_MODEL), lambda i, j: (i, 0)), _full((1, D_MODEL)),
                  pl.BlockSpec((None, D_MODEL, FF_TILE), lambda i, j: (j, 0, 0)),
                  pl.BlockSpec((None, FF_TILE, D_MODEL), lambda i, j: (j, 0, 0))],
        out_specs=[pl.BlockSpec((tm, D_MODEL), lambda i, j: (i, 0)), pl.BlockSpec((tm, FF_TILE), lambda i, j: (i, j)),
                   _full((1, D_MODEL))],
        out_shape=[jax.ShapeDtypeStruct((SEQ, D_MODEL), f32), jax.ShapeDtypeStruct((SEQ, D_FF), bf16),
                   jax.ShapeDtypeStruct((1, D_MODEL), f32)],
        scratch_shapes=[pltpu.VMEM((tm, D_MODEL), f32)],
        compiler_params=_params("arbitrary", "arbitrary"),
    )(dx2, u, x1, g, w_up, w_down)


def mlp_bwd_weights(dx2, u, du, x1, g, tm=512):
    nt = SEQ // tm

    def body(d_ref, u_ref, du_ref, x_ref, g_ref, dwu_ref, dwd_ref, dwu16_ref, dwd16_ref, h_scr, d_scr):
        j, i = pl.program_id(0), pl.program_id(1)

        @pl.when(j == 0)
        def _():
            xv = x_ref[...]
            h_scr[i] = (xv * _rms(xv) * g_ref[...]).T.astype(bf16)
            d_scr[i] = d_ref[...].astype(bf16)

        @pl.when(i == 0)
        def _():
            dwu_ref[...] = jnp.zeros_like(dwu_ref)
            dwd_ref[...] = jnp.zeros_like(dwd_ref)

        dwu_ref[...] += jnp.dot(h_scr[i], du_ref[...], preferred_element_type=f32)
        a = jnp.square(jnp.maximum(u_ref[...], 0.0))
        dwd_ref[...] += _bdot_tn(a, d_scr[i])

        @pl.when(i == nt - 1)
        def _():
            dwu16_ref[...] = dwu_ref[...].astype(bf16)
            dwd16_ref[...] = dwd_ref[...].astype(bf16)

    up = pl.BlockSpec((None, D_MODEL, FF_TILE), lambda j, i: (j, 0, 0))
    down = pl.BlockSpec((None, FF_TILE, D_MODEL), lambda j, i: (j, 0, 0))
    first_pass = pl.BlockSpec((tm, D_MODEL), lambda j, i: (jnp.where(j == 0, i, nt - 1), 0))
    return pl.pallas_call(
        body, name="mlp_bwd_weights", grid=(N_CHIPS, nt),
        in_specs=[first_pass, pl.BlockSpec((tm, FF_TILE), lambda j, i: (i, j)),
                  pl.BlockSpec((tm, FF_TILE), lambda j, i: (i, j)), first_pass, _full((1, D_MODEL))],
        out_specs=[up, down, up, down],
        out_shape=[jax.ShapeDtypeStruct((N_CHIPS, D_MODEL, FF_TILE), f32), jax.ShapeDtypeStruct((N_CHIPS, FF_TILE, D_MODEL), f32),
                   jax.ShapeDtypeStruct((N_CHIPS, D_MODEL, FF_TILE), bf16), jax.ShapeDtypeStruct((N_CHIPS, FF_TILE, D_MODEL), bf16)],
        scratch_shapes=[pltpu.VMEM((nt, D_MODEL, tm), bf16), pltpu.VMEM((nt, tm, D_MODEL), bf16)],
        compiler_params=_params("arbitrary", "arbitrary"),
    )(dx2, u, du, x1, g)


def loss_head(y, target, tm=512):
    def body(y_ref, t_ref, dy_ref, l_ref):
        @pl.when(pl.program_id(0) == 0)
        def _():
            l_ref[...] = jnp.zeros_like(l_ref)

        d = y_ref[...] - t_ref[...]
        dy_ref[...] = d * (1.0 / D_MODEL)
        part = jnp.sum(jnp.mean(d * d, axis=-1, keepdims=True), axis=0, keepdims=True)
        l_ref[...] += 0.5 * part

    tok = pl.BlockSpec((tm, D_MODEL), lambda i: (i, 0))
    return pl.pallas_call(
        body, name="loss_head", grid=(SEQ // tm,), in_specs=[tok, tok], out_specs=[tok, _full((1, 1))],
        out_shape=[jax.ShapeDtypeStruct((SEQ, D_MODEL), f32), jax.ShapeDtypeStruct((1, 1), f32)],
        compiler_params=_params("arbitrary"),
    )(y, target)


def _pad_lane(v):
    return jnp.pad(v, (0, LANE - v.shape[0]))[None, :]


def local_step(x, target, w, prov):
    bucket = jnp.asarray(_bucket_table().T)
    bias = bias_build(w["rel_bias"], bucket)
    saved = []
    for l in range(DEPTH):
        g_mix = w["mix_norm_g"][l][None, :] + prov.stage(("begin", l), x)
        w_in = prov.w_in(l, x)
        proj = in_fwd(x, g_mix, w_in)
        conv_b = w["conv_b"][l][None, :]
        act = conv_fwd(proj, w["conv_w"][l], conv_b)
        dtb = _pad_lane(w["dt_bias"][l]) + prov.stage(("mid", l), act)
        alog, dsk = _pad_lane(w["a_log"][l]), _pad_lane(w["d_skip"][l])
        ng = w["ssm_norm_g"][l][None, :]
        ssm, ypre, states = ssd_fwd_g(act, proj, dtb, alog, dsk, ng)
        qg, kg = w["q_gain"][l][:, None] + 0.0 * ssm[:1, :1], w["k_gain"][l][None, :]
        attn = attn_fwd_t(proj, qg, kg, w["sinks"][l], bias)
        tok = prov.stage(("pre_out", l), attn)
        w_out = prov.w_out(l, attn) + jnp.asarray(tok, bf16)
        x1 = out_fwd(x, attn, ssm, w_out)
        g_mlp = w["mlp_norm_g"][l][None, :] + prov.stage(("pre_mlp", l), x1)
        w_up, w_down = prov.mlp(l, x1)
        x2, u = mlp_fwd(x1, g_mlp, w_up, w_down)
        saved.append(dict(x=x, proj=proj, attn=attn, act=act, ssm=ssm, ypre=ypre, states=states, x1=x1, u=u,
                          g_mix=g_mix, qg=qg, kg=kg, conv_b=conv_b, dtb=dtb, alog=alog, dsk=dsk, ng=ng, g_mlp=g_mlp,
                          w_in=w_in, w_out=w_out, w_up=w_up, w_down=w_down))
        x = x2
    dx, loss = loss_head(x, target)
    grads = [None] * DEPTH
    dbands = [None] * DEPTH
    tok = 0.0
    for l in reversed(range(DEPTH)):
        s = saved[l]
        g_mlp = s["g_mlp"] + tok
        dx1, du, dg_mlp = mlp_bwd_data(dx, s["u"], s["x1"], g_mlp, s["w_up"], s["w_down"])
        dw_up, dw_down, dw_up16, dw_down16 = mlp_bwd_weights(dx, s["u"], du, s["x1"], g_mlp)
        tok = prov.grads(("mlp", l), dict(w_up=(dw_up, dw_up16), w_down=(dw_down, dw_down16)), dw_down)
        dattn, dssm, dw_out, dw_out16 = out_bwd(dx1, s["attn"], s["ssm"], s["w_out"])
        dact, ddt, dz, dng, dpar = ssd_bwd_g(s["act"], s["proj"], s["ypre"], s["states"], dssm, s["dtb"] + tok, s["alog"],
                                           s["dsk"], s["ng"])
        conv_b = s["conv_b"] + prov.stage(("bwd_mid", l), dact)
        dxbc, dconv_w, dconv_b = conv_bwd(s["proj"], dact, w["conv_w"][l], conv_b)
        dq, dk, dv, dband, dsink, dqg, dkg = attn_bwd_t(s["proj"], dattn, s["qg"], s["kg"], w["sinks"][l], bias)
        dbands[l] = dband
        g_mix = s["g_mix"]
        if l == 0:
            d_rel = bias_bwd(dbands[0], dbands[1], bucket)
            g_mix = g_mix + 0.0 * d_rel[:1, :1]
        dx, dw_in, dg_mix = in_bwd(dq, dz, dxbc, dk, dv, ddt, s["x"], g_mix, s["w_in"], dx1)
        tok = prov.grads(("mix", l), dict(w_in=split_w_in_grad(dw_in), w_out=(dw_out, dw_out16)), dx)
        grads[l] = dict(mix_norm_g=dg_mix[0], q_gain=dqg[:, 0], k_gain=dkg[0], sinks=dsink[:, 0],
                        conv_w=dconv_w, conv_b=dconv_b[0], dt_bias=dpar[0, :SSM_HEADS], a_log=dpar[1, :SSM_HEADS],
                        d_skip=dpar[2, :SSM_HEADS], ssm_norm_g=dng[0], mlp_norm_g=dg_mlp[0])
    out = {k: jnp.stack([grads[l][k] for l in range(DEPTH)]) for k in grads[0]}
    out["rel_bias"] = d_rel[:, :N_Q_HEADS]
    return loss, dx, out, tok


MESH = pl.DeviceIdType.MESH
HBM = pl.BlockSpec(memory_space=pltpu.HBM)
N_DEVICES = 8


def _coords():
    return lax.axis_index("x"), lax.axis_index("y"), lax.axis_index("c")


def _peer_chips(x, y):
    return [(1 - x, y), (x, 1 - y), (1 - x, 1 - y)]


def _remote(src, dst, send_sem, recv_sem, device):
    return pltpu.make_async_remote_copy(src_ref=src, dst_ref=dst, send_sem=send_sem, recv_sem=recv_sem,
                                        device_id=device, device_id_type=MESH)


SEM = pl.BlockSpec(memory_space=pltpu.SEMAPHORE)
ANY = pl.BlockSpec(memory_space=pl.ANY)
DATAFLOW = pltpu.SideEffectType.DATAFLOW_SIDE_EFFECTING


def _gather_copies(kind, src_refs, land_refs, ssem, rsem):
    x, y, c = _coords()
    k_me = 2 * x + y
    n = len(land_refs)
    cps = []
    for p, land in enumerate(land_refs):
        hr = land.shape[1] // 2
        rows = pl.ds(c * hr, hr)
        for j, chip in enumerate(_peer_chips(x, y)):
            i = 3 * p + j
            if kind == "ici":
                cps.append(_remote(src_refs[p].at[rows, :], land.at[k_me, rows, :], ssem.at[i], rsem.at[i], (*chip, c)))
            else:
                got = land.at[2 * chip[0] + chip[1], rows, :]
                cps.append(_remote(got, got, ssem.at[i], rsem.at[i], (x, y, 1 - c)))
        if kind == "relay":
            cps.append(_remote(src_refs[p], land.at[k_me], ssem.at[3 * n + p], rsem.at[3 * n + p], (x, y, 1 - c)))
    return cps


def gather_now(srcs, conv):
    n = len(srcs)

    def body(*refs):
        src_refs, conv_ref = refs[:n], refs[n]
        lands, gconv = refs[n + 1:2 * n + 1], refs[2 * n + 1]
        ssem, rsem, fsem, frsem, csem, crsem = refs[2 * n + 2:]
        x, y, c = _coords()
        k_me = 2 * x + y
        targets = [(*chip, c) for chip in _peer_chips(x, y)] + [(x, y, 1 - c)]
        ici = _gather_copies("ici", src_refs, lands, ssem, rsem)
        relay = _gather_copies("relay", src_refs, lands, fsem, frsem)
        passed = [cp for i, cp in enumerate(relay) if i % 4 != 3]
        own = relay[3::4]
        conv_cps = [_remote(conv_ref, gconv.at[k_me], csem.at[j], crsem.at[j], t) for j, t in enumerate(targets)]
        for cp in ici + conv_cps + own:
            cp.start()
        for cp, fw in zip(ici, passed):
            cp.wait_recv()
            fw.start()
        for cp in conv_cps + relay:
            cp.wait_recv()
        for cp in ici + relay + conv_cps:
            cp.wait_send()

    out_shape = [jax.ShapeDtypeStruct((N_CHIPS,) + s.shape, s.dtype) for s in srcs]
    out_shape.append(jax.ShapeDtypeStruct((N_CHIPS,) + conv.shape, conv.dtype))
    sems = lambda k: pltpu.SemaphoreType.DMA((k,))
    return pl.pallas_call(
        body, name="gather_now", out_shape=out_shape, in_specs=[HBM] * (n + 1), out_specs=[HBM] * (n + 1),
        scratch_shapes=[sems(3 * n), sems(3 * n), sems(4 * n), sems(4 * n), sems(N_CHIPS), sems(N_CHIPS)],
    )(*srcs, conv)


def _gather_maker(kind, n_src):
    def make(refs, ssem, rsem):
        cps = _gather_copies(kind, refs[:n_src], refs[n_src:], ssem, rsem)
        return cps, cps
    return make


def _scatter_maker(n):
    def make(refs, ssem, rsem):
        x, y, c = _coords()
        k_me = 2 * x + y
        sends, arrivals = [], []
        for p in range(n):
            src, land = refs[p], refs[n + p]
            sends.append(_remote(src.at[k_me, 1 - c], land.at[0], ssem.at[7 * p], rsem.at[7 * p], (x, y, 1 - c)))
            for j, chip in enumerate(_peer_chips(x, y)):
                for cc in range(2):
                    sends.append(_remote(src.at[2 * chip[0] + chip[1], cc], land.at[1 + 2 * j + c],
                                         ssem.at[7 * p + 1 + 2 * j + cc], rsem.at[7 * p + 1 + 2 * j + c], (*chip, cc)))
            for s in range(7):
                arrivals.append(_remote(land.at[s], land.at[s], ssem.at[7 * p + s], rsem.at[7 * p + s], (x, y, 1 - c)))
        return sends, arrivals
    return make


def _share_maker(n):
    def make(refs, ssem, rsem):
        x, y, c = _coords()
        sends = [_remote(refs[p].at[c], refs[p].at[c], ssem.at[p], rsem.at[p], (x, y, 1 - c)) for p in range(n)]
        arrivals = [_remote(refs[p].at[1 - c], refs[p].at[1 - c], ssem.at[p], rsem.at[p], (x, y, 1 - c)) for p in range(n)]
        return sends, arrivals
    return make


def split_start(name, make, n_sems, operands, after):
    n = len(operands)

    def body(*refs):
        ssem, rsem, token = refs[n + 1], refs[n + 2], refs[-1]
        for cp in make(refs[:n], ssem, rsem)[0]:
            cp.start()
        token[...] = jnp.zeros_like(token)

    ops = [pltpu.with_memory_space_constraint(a, pltpu.HBM) for a in operands]
    outs = pl.pallas_call(
        body, name=name,
        out_shape=(pltpu.SemaphoreType.DMA((n_sems,)), pltpu.SemaphoreType.DMA((n_sems,)),
                   *[pltpu.HBM(a.shape, a.dtype) for a in ops], jax.ShapeDtypeStruct((8, LANE), f32)),
        in_specs=[HBM] * n + [ANY], out_specs=(SEM, SEM, *[HBM] * n, pl.BlockSpec(memory_space=pltpu.VMEM)),
        input_output_aliases={i: 2 + i for i in range(n)},
        compiler_params=pltpu.CompilerParams(has_side_effects=DATAFLOW),
    )(*ops, after)
    return dict(name=name, make=make, ssem=outs[0], rsem=outs[1], operands=outs[2:2 + n], token=outs[-1][0, 0])


def split_wait(handle, after):
    n = len(handle["operands"])

    def body(*refs):
        sends, arrivals = handle["make"](refs[:n], refs[n], refs[n + 1])
        for cp in sends:
            cp.wait_send()
        for cp in arrivals:
            cp.wait_recv()

    outs = pl.pallas_call(
        body, name=handle["name"].replace("start", "wait"),
        out_shape=tuple(pltpu.HBM(a.shape, a.dtype) for a in handle["operands"]),
        in_specs=[HBM] * n + [SEM, SEM, ANY], out_specs=tuple([HBM] * n),
        input_output_aliases={i: i for i in range(n)},
        compiler_params=pltpu.CompilerParams(has_side_effects=DATAFLOW),
    )(*handle["operands"], handle["ssem"], handle["rsem"], after)
    return list(outs)


def piece_sum(g, recv, kc_arr):
    _, _, rb, cc = g.shape
    tr = min(256, rb)

    def body(kc_ref, g_ref, r_ref, o_ref):
        acc = g_ref[...]
        for s in range(7):
            acc = acc + r_ref[s].astype(f32)
        o_ref[...] = acc

    return pl.pallas_call(
        body, name="piece_sum",
        grid_spec=pltpu.PrefetchScalarGridSpec(
            num_scalar_prefetch=1, grid=(rb // tr,),
            in_specs=[pl.BlockSpec((None, None, tr, cc), lambda r, kc: (kc[0], kc[1], r, 0)),
                      pl.BlockSpec((7, tr, cc), lambda r, kc: (0, r, 0))],
            out_specs=pl.BlockSpec((None, tr, cc), lambda r, kc: (kc[1], r, 0))),
        out_shape=jax.ShapeDtypeStruct((2, rb, cc), f32),
        compiler_params=_params("arbitrary"),
    )(kc_arr, g, recv)


def small_all_reduce(vec):
    def body(v_ref, o_ref, gat, ssem, rsem):
        x, y, c = _coords()
        me = 4 * x + 2 * y + c
        gat[me] = v_ref[...]
        sends = []
        for t in range(1, N_DEVICES):
            peer = (x ^ (t >> 2), y ^ ((t >> 1) & 1), c ^ (t & 1))
            cp = _remote(v_ref, gat.at[me], ssem.at[t - 1], rsem.at[t - 1], peer)
            cp.start()
            sends.append(cp)
        for t in range(1, N_DEVICES):
            peer = (x ^ (t >> 2), y ^ ((t >> 1) & 1), c ^ (t & 1))
            slot = gat.at[4 * peer[0] + 2 * peer[1] + peer[2]]
            _remote(slot, slot, ssem.at[t - 1], rsem.at[t - 1], peer).wait_recv()
        for cp in sends:
            cp.wait_send()
        acc = gat[0]
        for d in range(1, N_DEVICES):
            acc = acc + gat[d]
        o_ref[...] = acc

    return pl.pallas_call(
        body, name="small_all_reduce", out_shape=jax.ShapeDtypeStruct(vec.shape, vec.dtype),
        in_specs=[pl.BlockSpec(memory_space=pltpu.VMEM)], out_specs=pl.BlockSpec(memory_space=pltpu.VMEM),
        scratch_shapes=[pltpu.VMEM((N_DEVICES,) + vec.shape, vec.dtype), pltpu.SemaphoreType.DMA((N_DEVICES - 1,)),
                        pltpu.SemaphoreType.DMA((N_DEVICES - 1,))],
    )(vec)


def _adamw_math(w, g, m, v):
    m_new = ADAM_B1 * m + (1.0 - ADAM_B1) * g
    v_new = ADAM_B2 * v + (1.0 - ADAM_B2) * jnp.square(g)
    m_hat = m_new / (1.0 - ADAM_B1 ** ADAM_STEP)
    v_hat = v_new / (1.0 - ADAM_B2 ** ADAM_STEP)
    delta = -ADAM_LR * (m_hat / (jnp.sqrt(v_hat) + ADAM_EPS) + ADAM_WD * w)
    return delta, m_new, v_new


def adamw_shard(w, g0, g1, m, v):
    depth, rows, cols = w.shape
    half = rows // 2
    tr = min(256, half)
    nr = half // tr

    def body(w_ref, g0_ref, g1_ref, m_ref, v_ref, go_ref, d_ref, nm_ref, nv_ref):
        gv = jnp.where(pl.program_id(0) == 0, g0_ref[...], g1_ref[...])
        go_ref[...] = gv
        d_ref[...], nm_ref[...], nv_ref[...] = _adamw_math(w_ref[...], gv, m_ref[...], v_ref[...])

    spec = pl.BlockSpec((None, tr, cols), lambda l, h, r: (l, h * nr + r, 0))
    g0spec = pl.BlockSpec((None, tr, cols), lambda l, h, r: (jnp.where(l == 0, h, 1), jnp.where(l == 0, r, nr - 1), 0))
    g1spec = pl.BlockSpec((None, tr, cols), lambda l, h, r: (jnp.where(l == 1, h, 0), jnp.where(l == 1, r, 0), 0))
    return pl.pallas_call(
        body, name="adamw_shard", grid=(depth, 2, nr), in_specs=[spec, g0spec, g1spec, spec, spec], out_specs=[spec] * 4,
        out_shape=[jax.ShapeDtypeStruct(w.shape, f32)] * 4,
        compiler_params=_params("arbitrary", "arbitrary", "arbitrary"),
    )(w, g0, g1, m, v)


def adamw_cols(w, g, m, v, tc=34):
    cols, depth, rows = w.shape

    def body(w_ref, g_ref, m_ref, v_ref, d_ref, nm_ref, nv_ref):
        d_ref[...], nm_ref[...], nv_ref[...] = _adamw_math(w_ref[...], g_ref[...], m_ref[...], v_ref[...])

    spec = pl.BlockSpec((tc, depth, rows), lambda i: (i, 0, 0))
    return pl.pallas_call(
        body, name="adamw_cols", grid=(cols // tc,), in_specs=[spec] * 4, out_specs=[spec] * 3,
        out_shape=[jax.ShapeDtypeStruct(w.shape, f32)] * 3,
        compiler_params=_params("arbitrary"),
    )(w, g, m, v)


def adamw_small(w, g, m, v):
    def body(w_ref, g_ref, m_ref, v_ref, d_ref, nm_ref, nv_ref):
        d_ref[...], nm_ref[...], nv_ref[...] = _adamw_math(w_ref[...], g_ref[...], m_ref[...], v_ref[...])

    return pl.pallas_call(
        body, name="adamw_small", out_shape=[jax.ShapeDtypeStruct(w.shape, f32)] * 3,
    )(w, g, m, v)


WEIGHTS = ("mix_norm_g", "w_in", "q_gain", "k_gain", "sinks", "rel_bias", "conv_w", "conv_b", "dt_bias", "a_log", "d_skip",
           "ssm_norm_g", "w_out", "mlp_norm_g", "w_up", "w_down")
BIG = ("w_in", "w_out", "w_up", "w_down")
SMALL = tuple(n for n in WEIGHTS if n not in BIG)
PACK_COLS = 1024
PACK_ROWS = 16


def _pack(named, last=None):
    flat = jnp.concatenate([named[n].reshape(-1) for n in SMALL])
    tail = jnp.zeros((1,), f32) if last is None else last.reshape(1)
    pad = jnp.zeros((PACK_ROWS * PACK_COLS - flat.shape[0] - 1,), f32)
    return jnp.concatenate([flat, pad, tail]).reshape(PACK_ROWS, PACK_COLS)


def _unpack(buf, shapes):
    flat = buf.reshape(-1)
    out, at = {}, 0
    for n in SMALL:
        size = int(np.prod(shapes[n]))
        out[n] = flat[at:at + size].reshape(shapes[n])
        at += size
    return out


class _Exchange:
    GROUPS = {"A": (("w_up", 0), ("w_down", 0)), "B": (("w_in", 1), ("w_out", 1)), "C": (("w_up", 1), ("w_down", 1))}
    ICI_AT = {("mid", 0): "B", ("pre_out", 0): "C"}
    RELAY_AT = {("pre_out", 0): "A", ("pre_mlp", 0): "B", ("mid", 1): "C"}
    LAST = ("mix", 0)
    IN_FLIGHT = 2

    def __init__(self, wts, kc_arr):
        self.wts, self.kc_arr = wts, kc_arr
        self.own = {(n, l): wts[n][l].astype(bf16) for n in BIG for l in range(DEPTH)}
        now = gather_now([self.own["w_in", 0], self.own["w_out", 0]], wts["conv_w"])
        self.ready = {("w_in", 0): now[0], ("w_out", 0): now[1]}
        self.conv_w = jnp.transpose(now[2], (1, 2, 0, 3)).reshape(DEPTH, CONV_WIDTH, D_CONV)
        self.ici, self.relay = {}, {}
        self.gview, self.scatter, self.share, self.reduced = {}, [], [], {}
        self._start_ici("A", now[2])

    def _start_ici(self, g, after):
        srcs = [self.own[p] for p in self.GROUPS[g]]
        lands = [lax.empty((N_CHIPS,) + s.shape, s.dtype) for s in srcs]
        self.ici[g] = split_start("gather%s_ici_start" % g, _gather_maker("ici", len(srcs)), 3 * len(srcs), srcs + lands,
                                  after)
        return self.ici[g]["token"]

    def stage(self, name, after):
        if name == ("begin", 0):
            return self.ici["A"]["token"]
        tok = 0.0
        g = self.RELAY_AT.get(name)
        if g is not None:
            n = len(self.GROUPS[g])
            self.relay[g] = split_start("gather%s_relay_start" % g, _gather_maker("relay", n), 4 * n,
                                        split_wait(self.ici[g], after), after)
            tok = self.relay[g]["token"]
        if name in self.ICI_AT:
            tok = tok + self._start_ici(self.ICI_AT[name], after)
        return tok

    def _get(self, piece, after):
        if piece not in self.ready:
            g = [k for k, pieces in self.GROUPS.items() if piece in pieces][0]
            lands = split_wait(self.relay[g], after)[len(self.GROUPS[g]):]
            self.ready.update(zip(self.GROUPS[g], lands))
        return self.ready[piece]

    def w_in(self, l, after):
        return align_w_in(self._get(("w_in", l), after))

    def w_out(self, l, after):
        return self._get(("w_out", l), after).reshape(D_MODEL, D_MODEL)

    def mlp(self, l, after):
        return self._get(("w_up", l), after), self._get(("w_down", l), after)

    def _view(self, n, g):
        _, rows, cols = self.wts[n].shape
        return g.reshape(N_CHIPS, 2, rows // 2, cols)

    def grads(self, name, arrays, after):
        if name == self.LAST:
            self.held = (name, arrays)
            return 0.0
        return self._scatter(name, arrays, after) + self._advance(after, self.IN_FLIGHT)

    def flush(self, after):
        return self._scatter(*self.held, after) + self._advance(after, self.IN_FLIGHT)

    def _scatter(self, name, arrays, after):
        pieces = [(n, name[1]) for n in arrays]
        views = [self._view(n, g) for n, (g, _) in arrays.items()]
        sends = [g16.reshape(v.shape) for v, (_, g16) in zip(views, arrays.values())]
        self.gview.update(zip(pieces, views))
        lands = [lax.empty((7,) + v.shape[2:], bf16) for v in views]
        h = split_start("scatter_%s%d_start" % name, _scatter_maker(len(views)), 7 * len(views), sends + lands, after)
        self.scatter.append((pieces, h))
        return h["token"]

    def _take_share(self, after):
        pieces, h = self.share.pop(0)
        self.reduced.update(zip(pieces, split_wait(h, after)))

    def _take_scatter(self, after):
        pieces, h = self.scatter.pop(0)
        lands = split_wait(h, after)[len(pieces):]
        sums = [piece_sum(self.gview[p], land, self.kc_arr) for p, land in zip(pieces, lands)]
        hs = split_start(h["name"].replace("scatter", "share"), _share_maker(len(sums)), len(sums), sums, after)
        self.share.append((pieces, hs))
        return hs["token"]

    def _advance(self, after, newest):
        if self.share:
            self._take_share(after)
        return self._take_scatter(after) if len(self.scatter) > newest else 0.0

    def reduced_grads(self, names, after):
        want = [(n, l) for n in names for l in range(DEPTH)]
        while not all(p in self.reduced for p in want):
            if any(p in pieces for p in want for pieces, _ in self.share):
                self._take_share(after)
            else:
                self._take_scatter(after)
        return {n: [self.reduced[n, l] for l in range(DEPTH)] for n in names}


def kernel(x, mix_norm_g, w_in, q_gain, k_gain, sinks, rel_bias, conv_w, conv_b, dt_bias, a_log, d_skip, ssm_norm_g, w_out, mlp_norm_g, w_up, w_down, loss_target, m_mix_norm_g, m_w_in, m_q_gain, m_k_gain, m_sinks, m_rel_bias, m_conv_w, m_conv_b, m_dt_bias, m_a_log, m_d_skip, m_ssm_norm_g, m_w_out, m_mlp_norm_g, m_w_up, m_w_down, v_mix_norm_g, v_w_in, v_q_gain, v_k_gain, v_sinks, v_rel_bias, v_conv_w, v_conv_b, v_dt_bias, v_a_log, v_d_skip, v_ssm_norm_g, v_w_out, v_mlp_norm_g, v_w_up, v_w_down):
    wts = dict(mix_norm_g=mix_norm_g, w_in=w_in, q_gain=q_gain, k_gain=k_gain, sinks=sinks, rel_bias=rel_bias, conv_w=conv_w,
               conv_b=conv_b, dt_bias=dt_bias, a_log=a_log, d_skip=d_skip, ssm_norm_g=ssm_norm_g, w_out=w_out,
               mlp_norm_g=mlp_norm_g, w_up=w_up, w_down=w_down)
    mom = dict(mix_norm_g=m_mix_norm_g, w_in=m_w_in, q_gain=m_q_gain, k_gain=m_k_gain, sinks=m_sinks, rel_bias=m_rel_bias,
               conv_w=m_conv_w, conv_b=m_conv_b, dt_bias=m_dt_bias, a_log=m_a_log, d_skip=m_d_skip, ssm_norm_g=m_ssm_norm_g,
               w_out=m_w_out, mlp_norm_g=m_mlp_norm_g, w_up=m_w_up, w_down=m_w_down)
    var = dict(mix_norm_g=v_mix_norm_g, w_in=v_w_in, q_gain=v_q_gain, k_gain=v_k_gain, sinks=v_sinks, rel_bias=v_rel_bias,
               conv_w=v_conv_w, conv_b=v_conv_b, dt_bias=v_dt_bias, a_log=v_a_log, d_skip=v_d_skip, ssm_norm_g=v_ssm_norm_g,
               w_out=v_w_out, mlp_norm_g=v_mlp_norm_g, w_up=v_w_up, w_down=v_w_down)
    xi, yi, ci = _coords()
    k_me = 2 * xi + yi
    kc_arr = jnp.stack([k_me, ci]).astype(jnp.int32)

    prov = _Exchange(wts, kc_arr)
    small_w = {n: wts[n] for n in SMALL}
    small_w["conv_w"] = prov.conv_w
    loss, dx, grads, tok = local_step(x[0], loss_target[0], small_w, prov)

    small_shapes = {n: grads[n].shape for n in SMALL}
    small_sum = small_all_reduce(_pack(grads, loss) + tok)
    loss = small_sum[PACK_ROWS - 1, PACK_COLS - 1]
    tok = prov.flush(small_sum)
    small = _unpack(small_sum, small_shapes)
    cols = conv_w.shape[-1]
    small["conv_w"] = lax.dynamic_slice_in_dim(small["conv_w"], k_me * cols, cols, axis=2)
    g_out_d, d_out_d, m_out_d, v_out_d = {}, {}, {}, {}
    shard_shapes = {n: wts[n].shape for n in SMALL}
    d, nm, nv = adamw_small(_pack(wts), _pack(small) + tok, _pack(mom), _pack(var))
    for dst, buf in ((d_out_d, d), (m_out_d, nm), (v_out_d, nv)):
        dst.update(_unpack(buf, shard_shapes))
    g_out_d.update(small)

    after = d
    for names in (("w_up", "w_down"), ("w_in", "w_out")):
        for n, (g0, g1) in prov.reduced_grads(names, after).items():
            if n == "w_in":
                rows, cols = wts[n].shape[1:]
                to_cols = lambda a: jnp.transpose(a, (2, 0, 1))
                g_t = to_cols(jnp.stack([g0.reshape(rows, cols), g1.reshape(rows, cols)]))
                res_t = adamw_cols(to_cols(wts[n]), g_t, to_cols(mom[n]), to_cols(var[n]))
                g_out_d[n], d_out_d[n], m_out_d[n], v_out_d[n] = (jnp.transpose(a, (1, 2, 0)) for a in (g_t, *res_t))
            else:
                g_out_d[n], d_out_d[n], m_out_d[n], v_out_d[n] = adamw_shard(wts[n], g0, g1, mom[n], var[n])
            after = d_out_d[n]

    return (loss, dx[None], *[g_out_d[n] for n in WEIGHTS], *[d_out_d[n] for n in WEIGHTS],
            *[m_out_d[n] for n in WEIGHTS], *[v_out_d[n] for n in WEIGHTS])
```

```python
import numpy as np
import jax
import jax.numpy as jnp
from jax import lax
from jax.experimental import pallas as pl
from jax.experimental.pallas import tpu as pltpu

f32 = jnp.float32
bf16 = jnp.bfloat16

SEQ = 2048
D_MODEL = 1024
DEPTH = 2
HEAD_DIM = 64
N_Q_HEADS = 8
N_KV_HEADS = 2
Q_PER_KV = N_Q_HEADS // N_KV_HEADS
BLOCK = 128
N_BLOCKS = SEQ // BLOCK
N_BUCKETS = 32
MAX_DISTANCE = 128
SSM_HEADS = 8
SSM_HEAD_DIM = 64
SSM_GROUPS = 2
HEADS_PER_GROUP = SSM_HEADS // SSM_GROUPS
SSM_STATE = 128
CONV_WIDTH = 4
CHUNK = 128
N_CHUNKS = SEQ // CHUNK
D_FF = 4 * D_MODEL
D_ATTN = N_Q_HEADS * HEAD_DIM
D_KV = N_KV_HEADS * HEAD_DIM
D_SSM = SSM_HEADS * SSM_HEAD_DIM
D_BC = SSM_GROUPS * SSM_STATE
D_CONV = D_SSM + 2 * D_BC
D_IN = D_ATTN + 2 * D_KV + D_SSM + D_CONV + SSM_HEADS
EPS = 1e-6
NEG = -1e30
N_CHIPS = 4
FF_TILE = D_FF // N_CHIPS

LANE = 128
PW = D_ATTN + D_SSM + D_CONV + 2 * D_KV + LANE
OFF_Q, OFF_Z, OFF_X, OFF_K, OFF_V, OFF_DT = 0, 512, 1024, 2048, 2176, 2304

ADAM_LR = 0.001
ADAM_B1 = 0.9
ADAM_B2 = 0.999
ADAM_EPS = 1e-08
ADAM_WD = 0.01
ADAM_STEP = 10

VMEM_LIMIT = 56 * 1024 * 1024


def _params(*sem):
    return pltpu.CompilerParams(dimension_semantics=tuple(sem), vmem_limit_bytes=VMEM_LIMIT)


def _bdot(a, b):
    return jnp.dot(a.astype(bf16), b.astype(bf16), preferred_element_type=f32)


def _bdot_nt(a, b):
    return lax.dot_general(a.astype(bf16), b.astype(bf16), (((1,), (1,)), ((), ())), preferred_element_type=f32)


def _bdot_tn(a, b):
    return lax.dot_general(a.astype(bf16), b.astype(bf16), (((0,), (0,)), ((), ())), preferred_element_type=f32)


def _hdot(a, b):
    return jnp.dot(a, b, precision=lax.Precision.HIGHEST, preferred_element_type=f32)


def _sigmoid(x):
    return 1.0 / (1.0 + jnp.exp(-x))


def _softplus(x):
    return jnp.maximum(x, 0.0) + jnp.log1p(jnp.exp(-jnp.abs(x)))


def _rms(x):
    return lax.rsqrt(jnp.mean(x * x, axis=-1, keepdims=True) + EPS)


def _rms_bwd(dy, xhat, r, g):
    t = dy * g
    return r * (t - xhat * jnp.mean(t * xhat, axis=-1, keepdims=True))


def _full(shape):
    return pl.BlockSpec(shape, lambda *_: (0,) * len(shape))


def _bucket_table():
    qi = np.arange(BLOCK)[:, None]
    kj = np.arange(2 * BLOCK)[None, :]
    dist = qi + BLOCK - kj
    ok = (dist >= 0) & (dist < 128)
    d = np.clip(dist, 0, None)
    max_exact = N_BUCKETS // 2
    d_f = np.maximum(d, 1).astype(np.float32)
    large = max_exact + (np.log(d_f / np.float32(max_exact)) / np.float32(np.log(MAX_DISTANCE / max_exact))
                         * np.float32(N_BUCKETS - max_exact)).astype(np.int32)
    large = np.minimum(large, N_BUCKETS - 1)
    bucket = np.where(d < max_exact, d, large)
    return np.where(ok, bucket, -1).astype(np.int32)


def bias_build(rel_bias, bucket):
    def body(rel_ref, bkt_ref, o_ref):
        bkt = bkt_ref[...]
        for h in range(N_Q_HEADS):
            acc = jnp.where(bkt < 0, NEG, 0.0).astype(f32)
            for b in range(N_BUCKETS):
                acc = acc + jnp.where(bkt == b, rel_ref[b, h], 0.0)
            o_ref[h] = acc

    return pl.pallas_call(
        body, name="bias_build", out_shape=jax.ShapeDtypeStruct((N_Q_HEADS,) + bucket.shape, f32),
        in_specs=[pl.BlockSpec(memory_space=pltpu.SMEM), pl.BlockSpec(memory_space=pltpu.VMEM)],
        out_specs=pl.BlockSpec(memory_space=pltpu.VMEM),
    )(rel_bias, bucket)


def bias_bwd(dband0, dband1, bucket):
    def body(d0_ref, d1_ref, bkt_ref, o_ref):
        bkt = bkt_ref[...]
        o_ref[...] = jnp.zeros_like(o_ref)
        for h in range(N_Q_HEADS):
            d = d0_ref[h] + d1_ref[h]
            for b in range(N_BUCKETS):
                part = jnp.sum(jnp.where(bkt == b, d, 0.0), axis=1, keepdims=True)
                o_ref[b:b + 1, h:h + 1] = jnp.sum(part, axis=0, keepdims=True)

    return pl.pallas_call(
        body, name="bias_bwd", out_shape=jax.ShapeDtypeStruct((N_BUCKETS, LANE), f32),
    )(dband0, dband1, bucket)


W_IN_SHARD = D_IN // N_CHIPS
_ALIGNED_PIECES = ((0, 0, 512), (1, 190, 578), (2, 0, 124), (2, 124, 578), (3, 0, 570), (0, 512, 578), (1, 0, 62),
                   (1, 62, 190), (3, 570, 578))
_SHARD_PIECES = (((0, 512), (2048, 2114)), ((2114, 2176), (2176, 2304), (512, 900)), ((900, 1024), (1024, 1478)),
                 ((1478, 2048), (2304, 2312)))


def align_w_in(shards, tr=256):
    def body(s_ref, o_ref):
        parts = [s_ref[k, :, a:b] for k, a, b in _ALIGNED_PIECES]
        parts.append(jnp.zeros((tr, LANE - SSM_HEADS), s_ref.dtype))
        o_ref[...] = jnp.concatenate(parts, axis=-1)

    return pl.pallas_call(
        body, name="align_w_in", grid=(D_MODEL // tr,),
        in_specs=[pl.BlockSpec((N_CHIPS, tr, W_IN_SHARD), lambda i: (0, i, 0))],
        out_specs=pl.BlockSpec((tr, PW), lambda i: (i, 0)),
        out_shape=jax.ShapeDtypeStruct((D_MODEL, PW), shards.dtype),
        compiler_params=_params("arbitrary"),
    )(shards)


def split_w_in_grad(dw, tr=256):
    def body(d_ref, o_ref, o16_ref):
        for k, pieces in enumerate(_SHARD_PIECES):
            part = jnp.concatenate([d_ref[:, a:b] for a, b in pieces], axis=-1)
            o_ref[k] = part
            o16_ref[k] = part.astype(bf16)

    spec = pl.BlockSpec((N_CHIPS, tr, W_IN_SHARD), lambda i: (0, i, 0))
    return pl.pallas_call(
        body, name="split_w_in_grad", grid=(D_MODEL // tr,),
        in_specs=[pl.BlockSpec((tr, PW), lambda i: (i, 0))], out_specs=[spec, spec],
        out_shape=[jax.ShapeDtypeStruct((N_CHIPS, D_MODEL, W_IN_SHARD), f32),
                   jax.ShapeDtypeStruct((N_CHIPS, D_MODEL, W_IN_SHARD), bf16)],
        compiler_params=_params("arbitrary"),
    )(dw)

def in_fwd(x, g, w, tm=512):
    def body(x_ref, g_ref, w_ref, o_ref):
        xv = x_ref[...]
        h = xv * _rms(xv) * g_ref[...]
        o_ref[...] = _bdot(h, w_ref[...])

    return pl.pallas_call(
        body, name="in_fwd", grid=(SEQ // tm,),
        in_specs=[pl.BlockSpec((tm, D_MODEL), lambda i: (i, 0)), _full((1, D_MODEL)), _resident((D_MODEL, PW))],
        out_specs=pl.BlockSpec((tm, PW), lambda i: (i, 0)),
        out_shape=jax.ShapeDtypeStruct((SEQ, PW), f32),
        compiler_params=_params("arbitrary"),
    )(x, g, w)


def _resident(shape):
    return pl.BlockSpec(shape, lambda *_: (0,) * len(shape), pipeline_mode=pl.Buffered(1))


def in_bwd(dq, dz, dxbc, dk, dv, ddt, x, g, w, dres, tm=512):
    def body(dq_ref, dz_ref, dx_ref, dk_ref, dv_ref, ddt_ref, x_ref, g_ref, w_ref, dres_ref, o_ref, dw_ref, dg_ref):
        i = pl.program_id(0)

        @pl.when(i == 0)
        def _():
            dw_ref[...] = jnp.zeros_like(dw_ref)
            dg_ref[...] = jnp.zeros_like(dg_ref)

        dproj = jnp.concatenate([dq_ref[...], dz_ref[...], dx_ref[...], dk_ref[...], dv_ref[...], ddt_ref[...]],
                                axis=-1).astype(bf16)
        xv = x_ref[...]
        r = _rms(xv)
        xhat = xv * r
        gv = g_ref[...]
        h = xhat * gv
        dw_ref[...] += _bdot_tn(h, dproj)
        dh = _bdot_nt(dproj, w_ref[...])
        dg_ref[...] += jnp.sum(dh * xhat, axis=0, keepdims=True)
        o_ref[...] = dres_ref[...] + _rms_bwd(dh, xhat, r, gv)

    tok = lambda w_: pl.BlockSpec((tm, w_), lambda i: (i, 0))
    return pl.pallas_call(
        body, name="in_bwd", grid=(SEQ // tm,),
        in_specs=[tok(D_ATTN), tok(D_SSM), tok(D_CONV), tok(D_KV), tok(D_KV), tok(LANE), tok(D_MODEL),
                  _full((1, D_MODEL)), _resident((D_MODEL, PW)), tok(D_MODEL)],
        out_specs=[tok(D_MODEL), _resident((D_MODEL, PW)), _full((1, D_MODEL))],
        out_shape=[jax.ShapeDtypeStruct((SEQ, D_MODEL), f32), jax.ShapeDtypeStruct((D_MODEL, PW), f32),
                   jax.ShapeDtypeStruct((1, D_MODEL), f32)],
        compiler_params=_params("arbitrary"),
    )(dq, dz, dxbc, dk, dv, ddt, x, g, w, dres)


def _attn_softmax_t(qk, bias_t, sink, first, key_row):
    s = qk * (HEAD_DIM ** -0.5) + bias_t
    s = jnp.where(jnp.logical_and(first, key_row < BLOCK), NEG, s)
    m = jnp.maximum(jnp.max(s, axis=0, keepdims=True), sink)
    p = jnp.exp(s - m)
    psink = jnp.exp(sink - m)
    inv = 1.0 / (jnp.sum(p, axis=0, keepdims=True) + psink)
    return p * inv, psink * inv


def _rms_t(x_t):
    return lax.rsqrt(jnp.mean(x_t * x_t, axis=0, keepdims=True) + EPS)


def attn_fwd_t(proj, q_gain_col, k_gain, sinks, bias_t):
    kcol, vcol = OFF_K // D_KV, OFF_V // D_KV

    def body(q_ref, kc_ref, kp_ref, vc_ref, vp_ref, qg_ref, kg_ref, sink_ref, bias_ref, o_ref, ot_scr):
        n = pl.program_id(0)
        first = n == 0
        key_row = lax.broadcasted_iota(jnp.int32, (2 * BLOCK, BLOCK), 0)
        k2 = jnp.concatenate([kp_ref[...], kc_ref[...]], axis=0)
        v_t = jnp.concatenate([vp_ref[...], vc_ref[...]], axis=0).T
        q_t = q_ref[...].T
        qg = jnp.broadcast_to(qg_ref[...], (HEAD_DIM, BLOCK))
        kg = kg_ref[...]
        for hk in range(N_KV_HEADS):
            sl = slice(hk * HEAD_DIM, (hk + 1) * HEAD_DIM)
            kk = k2[:, sl]
            kn = (kk * _rms(kk) * kg).astype(bf16)
            vt = v_t[sl, :].astype(bf16)
            heads = range(hk * Q_PER_KV, (hk + 1) * Q_PER_KV)
            qns = []
            for h in heads:
                qh = q_t[h * HEAD_DIM:(h + 1) * HEAD_DIM, :]
                qns.append(qh * _rms_t(qh) * qg)
            scores = [_bdot(kn, qn) for qn in qns]
            for h, s in zip(heads, scores):
                p, _ = _attn_softmax_t(s, bias_ref[h], sink_ref[h], first, key_row)
                ot_scr[h * HEAD_DIM:(h + 1) * HEAD_DIM, :] = _bdot(vt, p)
        o_ref[...] = ot_scr[...].T

    prev = lambda n: jnp.maximum(n - 1, 0)
    return pl.pallas_call(
        body, name="attn_fwd", grid=(N_BLOCKS,),
        in_specs=[pl.BlockSpec((BLOCK, D_ATTN), lambda n: (n, 0)),
                  pl.BlockSpec((BLOCK, D_KV), lambda n: (n, kcol)), pl.BlockSpec((BLOCK, D_KV), lambda n: (prev(n), kcol)),
                  pl.BlockSpec((BLOCK, D_KV), lambda n: (n, vcol)), pl.BlockSpec((BLOCK, D_KV), lambda n: (prev(n), vcol)),
                  _full((HEAD_DIM, 1)), _full((1, HEAD_DIM)), pl.BlockSpec(memory_space=pltpu.SMEM),
                  _full((N_Q_HEADS, 2 * BLOCK, BLOCK))],
        out_specs=pl.BlockSpec((BLOCK, D_ATTN), lambda n: (n, 0)),
        out_shape=jax.ShapeDtypeStruct((SEQ, D_ATTN), f32),
        scratch_shapes=[pltpu.VMEM((D_ATTN, BLOCK), f32)],
        compiler_params=_params("arbitrary"),
    )(proj, proj, proj, proj, proj, q_gain_col, k_gain, sinks, bias_t)


def attn_bwd_t(proj, d_out, q_gain_col, k_gain, sinks, bias_t):
    kcol, vcol = OFF_K // D_KV, OFF_V // D_KV

    def body(q_ref, kc_ref, kp_ref, vc_ref, vp_ref, do_ref, qg_ref, kg_ref, sink_ref, bias_ref,
             dq_ref, dk_ref, dv_ref, dband_ref, dsink_ref, dqg_ref, dkg_ref, dkn_scr, dv_scr, dqt_scr, dsink_acc, dqg_acc):
        i = pl.program_id(0)
        first = i == N_BLOCKS - 1

        @pl.when(i == 0)
        def _():
            for ref in (dband_ref, dkg_ref, dkn_scr, dv_scr, dsink_acc, dqg_acc):
                ref[...] = jnp.zeros_like(ref)

        key_row = lax.broadcasted_iota(jnp.int32, (2 * BLOCK, BLOCK), 0)
        k2 = jnp.concatenate([kp_ref[...], kc_ref[...]], axis=0)
        v2 = jnp.concatenate([vp_ref[...], vc_ref[...]], axis=0)
        q_t = q_ref[...].T
        do_t = do_ref[...].T
        qg = jnp.broadcast_to(qg_ref[...], (HEAD_DIM, BLOCK))
        kg = kg_ref[...]
        scale = HEAD_DIM ** -0.5
        for hk in range(N_KV_HEADS):
            sl = slice(hk * HEAD_DIM, (hk + 1) * HEAD_DIM)
            kk = k2[:, sl]
            rk = _rms(kk)
            khat = kk * rk
            kn = (khat * kg).astype(bf16)
            vb = v2[:, sl].astype(bf16)
            dkn = jnp.zeros((2 * BLOCK, HEAD_DIM), f32)
            dvv = jnp.zeros((2 * BLOCK, HEAD_DIM), f32)
            heads = range(hk * Q_PER_KV, (hk + 1) * Q_PER_KV)
            rqs, qhats, qns, d_os = [], [], [], []
            for h in heads:
                hs = slice(h * HEAD_DIM, (h + 1) * HEAD_DIM)
                qh = q_t[hs, :]
                rqs.append(_rms_t(qh))
                qhats.append(qh * rqs[-1])
                qns.append((qhats[-1] * qg).astype(bf16))
                d_os.append(do_t[hs, :].astype(bf16))
            scores = [_bdot(kn, qn) for qn in qns]
            dps = [_bdot(vb, d_o) for d_o in d_os]
            ps, dss = [], []
            for h, s, dp in zip(heads, scores, dps):
                p, psink = _attn_softmax_t(s, bias_ref[h], sink_ref[h], first, key_row)
                delta = jnp.sum(p * dp, axis=0, keepdims=True)
                ds = p * (dp - delta)
                dband_ref[h] += ds
                dsink_acc[h:h + 1, :] += -(psink * delta)
                ps.append(p.astype(bf16))
                dss.append(ds.astype(bf16))
            dqns = [_bdot_tn(kn, ds) * scale for ds in dss]
            for ds, qn, p, d_o in zip(dss, qns, ps, d_os):
                dkn = dkn + _bdot_nt(ds, qn) * scale
                dvv = dvv + _bdot_nt(p, d_o)
            for h, dqn, rq, qhat in zip(heads, dqns, rqs, qhats):
                dqg_acc[...] += dqn * qhat
                t = dqn * qg
                dqt_scr[h * HEAD_DIM:(h + 1) * HEAD_DIM, :] = rq * (t - qhat * jnp.mean(t * qhat, axis=0, keepdims=True))
            dkn_cur = dkn[BLOCK:] + dkn_scr[:, sl]
            dkn_scr[:, sl] = dkn[:BLOCK]
            khat_c, rk_c = khat[BLOCK:], rk[BLOCK:]
            dkg_ref[...] += jnp.sum(dkn_cur * khat_c, axis=0, keepdims=True)
            dk_ref[:, sl] = _rms_bwd(dkn_cur, khat_c, rk_c, kg)
            dv_ref[:, sl] = dvv[BLOCK:] + dv_scr[:, sl]
            dv_scr[:, sl] = dvv[:BLOCK]
        dq_ref[...] = dqt_scr[...].T

        @pl.when(i == N_BLOCKS - 1)
        def _():
            dsink_ref[...] = jnp.sum(dsink_acc[...], axis=1, keepdims=True)
            dqg_ref[...] = jnp.sum(dqg_acc[...], axis=1, keepdims=True)

    blk = lambda i: N_BLOCKS - 1 - i
    prev = lambda i: jnp.maximum(N_BLOCKS - 2 - i, 0)
    return pl.pallas_call(
        body, name="attn_bwd", grid=(N_BLOCKS,),
        in_specs=[pl.BlockSpec((BLOCK, D_ATTN), lambda i: (blk(i), 0)),
                  pl.BlockSpec((BLOCK, D_KV), lambda i: (blk(i), kcol)), pl.BlockSpec((BLOCK, D_KV), lambda i: (prev(i), kcol)),
                  pl.BlockSpec((BLOCK, D_KV), lambda i: (blk(i), vcol)), pl.BlockSpec((BLOCK, D_KV), lambda i: (prev(i), vcol)),
                  pl.BlockSpec((BLOCK, D_ATTN), lambda i: (blk(i), 0)),
                  _full((HEAD_DIM, 1)), _full((1, HEAD_DIM)), pl.BlockSpec(memory_space=pltpu.SMEM),
                  _full((N_Q_HEADS, 2 * BLOCK, BLOCK))],
        out_specs=[pl.BlockSpec((BLOCK, D_ATTN), lambda i: (blk(i), 0)), pl.BlockSpec((BLOCK, D_KV), lambda i: (blk(i), 0)),
                   pl.BlockSpec((BLOCK, D_KV), lambda i: (blk(i), 0)), _full((N_Q_HEADS, 2 * BLOCK, BLOCK)),
                   _full((N_Q_HEADS, 1)), _full((HEAD_DIM, 1)), _full((1, HEAD_DIM))],
        out_shape=[jax.ShapeDtypeStruct((SEQ, D_ATTN), f32), jax.ShapeDtypeStruct((SEQ, D_KV), f32),
                   jax.ShapeDtypeStruct((SEQ, D_KV), f32), jax.ShapeDtypeStruct((N_Q_HEADS, 2 * BLOCK, BLOCK), f32),
                   jax.ShapeDtypeStruct((N_Q_HEADS, 1), f32), jax.ShapeDtypeStruct((HEAD_DIM, 1), f32),
                   jax.ShapeDtypeStruct((1, HEAD_DIM), f32)],
        scratch_shapes=[pltpu.VMEM((BLOCK, D_KV), f32), pltpu.VMEM((BLOCK, D_KV), f32), pltpu.VMEM((D_ATTN, BLOCK), f32),
                        pltpu.VMEM((N_Q_HEADS, BLOCK), f32), pltpu.VMEM((HEAD_DIM, BLOCK), f32)],
        compiler_params=_params("arbitrary"),
    )(proj, proj, proj, proj, proj, d_out, q_gain_col, k_gain, sinks, bias_t)


SUBLANES = 8


def _shift_down(u, s, row8):
    if s == 0:
        return u
    r = pltpu.roll(u, s, 0)
    return jnp.concatenate([jnp.where(row8 >= s, r[:SUBLANES], 0.0), r[SUBLANES:]], axis=0)


def _shift_up(u, s, row8):
    if s == 0:
        return u
    r = pltpu.roll(u, SEQ - s, 0)
    return jnp.concatenate([r[:-SUBLANES], jnp.where(row8 < SUBLANES - s, r[-SUBLANES:], 0.0)], axis=0)


def conv_fwd(proj, conv_w, conv_b):
    xcol = OFF_X // LANE

    def body(u_ref, w_ref, b_ref, o_ref):
        u = u_ref[...]
        row = lax.broadcasted_iota(jnp.int32, (SUBLANES, LANE), 0)
        pre = b_ref[...] + jnp.zeros_like(u)
        for k in range(CONV_WIDTH):
            pre = pre + w_ref[k:k + 1, :] * _shift_down(u, CONV_WIDTH - 1 - k, row)
        o_ref[...] = pre * _sigmoid(pre)

    return pl.pallas_call(
        body, name="conv_fwd", grid=(D_CONV // LANE,),
        in_specs=[pl.BlockSpec((SEQ, LANE), lambda j: (0, xcol + j)), pl.BlockSpec((CONV_WIDTH, LANE), lambda j: (0, j)),
                  pl.BlockSpec((1, LANE), lambda j: (0, j))],
        out_specs=pl.BlockSpec((SEQ, LANE), lambda j: (0, j)),
        out_shape=jax.ShapeDtypeStruct((SEQ, D_CONV), f32),
        compiler_params=_params("arbitrary"),
    )(proj, conv_w, conv_b)


def conv_bwd(proj, d_act, conv_w, conv_b):
    xcol = OFF_X // LANE

    def body(u_ref, da_ref, w_ref, b_ref, du_ref, dw_ref, db_ref):
        u = u_ref[...]
        row = lax.broadcasted_iota(jnp.int32, (SUBLANES, LANE), 0)
        shifted = [_shift_down(u, CONV_WIDTH - 1 - k, row) for k in range(CONV_WIDTH)]
        pre = b_ref[...] + jnp.zeros_like(u)
        for k in range(CONV_WIDTH):
            pre = pre + w_ref[k:k + 1, :] * shifted[k]
        sg = _sigmoid(pre)
        dpre = da_ref[...] * (sg * (1.0 + pre * (1.0 - sg)))
        db_ref[...] = jnp.sum(dpre, axis=0, keepdims=True)
        du = jnp.zeros_like(u)
        for k in range(CONV_WIDTH):
            dw_ref[k:k + 1, :] = jnp.sum(dpre * shifted[k], axis=0, keepdims=True)
            du = du + w_ref[k:k + 1, :] * _shift_up(dpre, CONV_WIDTH - 1 - k, row)
        du_ref[...] = du

    return pl.pallas_call(
        body, name="conv_bwd", grid=(D_CONV // LANE,),
        in_specs=[pl.BlockSpec((SEQ, LANE), lambda j: (0, xcol + j)), pl.BlockSpec((SEQ, LANE), lambda j: (0, j)),
                  pl.BlockSpec((CONV_WIDTH, LANE), lambda j: (0, j)), pl.BlockSpec((1, LANE), lambda j: (0, j))],
        out_specs=[pl.BlockSpec((SEQ, LANE), lambda j: (0, j)), pl.BlockSpec((CONV_WIDTH, LANE), lambda j: (0, j)),
                   pl.BlockSpec((1, LANE), lambda j: (0, j))],
        out_shape=[jax.ShapeDtypeStruct((SEQ, D_CONV), f32), jax.ShapeDtypeStruct((CONV_WIDTH, D_CONV), f32),
                   jax.ShapeDtypeStruct((1, D_CONV), f32)],
        compiler_params=_params("arbitrary"),
    )(proj, d_act, conv_w, conv_b)


def _ssd_chunk_common(dt_raw, dtb, alog):
    row = lax.broadcasted_iota(jnp.int32, (CHUNK, CHUNK), 0)
    col = lax.broadcasted_iota(jnp.int32, (CHUNK, CHUNK), 1)
    tri = (row >= col).astype(f32)
    strict = (row > col).astype(f32)
    dtp = _softplus(dt_raw + dtb)
    a_row = -jnp.exp(alog)
    d_a = dtp * a_row
    cs = _hdot(tri, d_a)
    cs_last = cs[CHUNK - 1:CHUNK, :]
    return row, col, dtp, a_row, cs, cs.T, cs_last


def _seg_decay(cs, cs_t, hd, row, col):
    seg = cs[:, hd:hd + 1] - cs_t[hd:hd + 1, :]
    return jnp.where(row >= col, jnp.exp(seg), 0.0)


GROUP_W = HEADS_PER_GROUP * SSM_HEAD_DIM


def _group_indicator(g):
    j = lax.broadcasted_iota(jnp.int32, (GROUP_W, LANE), 0)
    lane = lax.broadcasted_iota(jnp.int32, (GROUP_W, LANE), 1)
    return (lane == g * HEADS_PER_GROUP + j // SSM_HEAD_DIM).astype(bf16)


def _bf16_pieces(a, n):
    pieces = []
    for _ in range(n):
        p = a.astype(bf16)
        pieces.append(p)
        a = a - p.astype(f32)
    return pieces


def _head_spread(a, ind):
    return sum(lax.dot_general(p, ind, (((1,), (1,)), ((), ())), preferred_element_type=f32) for p in _bf16_pieces(a, 3))


def _head_sums(a, ind):
    return sum(jnp.dot(p, ind, preferred_element_type=f32) for p in _bf16_pieces(a, 2))


def ssd_fwd_g(act, proj, dt_bias, a_log, d_skip, norm_g):
    zcol, dtcol = OFF_Z // D_SSM, OFF_DT // LANE

    def body(act_ref, z_ref, dt_ref, dtb_ref, alog_ref, dsk_ref, ng_ref, out_ref, ypre_ref, st_ref, state):
        c = pl.program_id(0)

        @pl.when(c == 0)
        def _():
            state[...] = jnp.zeros_like(state)

        row, col, dtp, a_row, cs, cs_t, cs_last = _ssd_chunk_common(dt_ref[...], dtb_ref[...], alog_ref[...])
        e_cs = jnp.exp(cs)
        dte = jnp.exp(cs_last - cs)
        rows8 = jnp.concatenate([jnp.exp(cs_last), dsk_ref[...], jnp.zeros((6, LANE), f32)], axis=0)
        z = z_ref[...]
        sz = z * _sigmoid(z)
        ng = ng_ref[...]
        for g in range(SSM_GROUPS):
            gs = slice(g * GROUP_W, (g + 1) * GROUP_W)
            ind = _group_indicator(g)
            xg = act_ref[:, gs]
            bg = act_ref[:, D_SSM + g * SSM_STATE:D_SSM + (g + 1) * SSM_STATE]
            cg = act_ref[:, D_SSM + D_BC + g * SSM_STATE:D_SSM + D_BC + (g + 1) * SSM_STATE]
            dt_e, e_e, dte_e = _head_spread(dtp, ind), _head_spread(e_cs, ind), _head_spread(dte, ind)
            rows_e = _head_spread(rows8, ind)
            ecl_e, dsk_e = rows_e[0:1], rows_e[1:2]
            xdt = xg * dt_e
            prev = state[g]
            st_ref[0, g] = prev
            cb = _bdot_nt(cg, bg)
            goff = _bdot(cg, prev)
            snew = _bdot_tn(bg, xdt * dte_e)
            heads = range(g * HEADS_PER_GROUP, (g + 1) * HEADS_PER_GROUP)
            ms = [cb * _seg_decay(cs, cs_t, hd, row, col) for hd in heads]
            yd = [_bdot(m, xdt[:, r * SSM_HEAD_DIM:(r + 1) * SSM_HEAD_DIM]) for r, m in enumerate(ms)]
            y = jnp.concatenate(yd, axis=1) + e_e * goff + xg * dsk_e
            state[g] = prev * ecl_e + snew
            ypre_ref[:, gs] = y
            part = y * sz[:, gs]
            out_ref[:, gs] = part * _rms(part) * ng[:, gs]

    return pl.pallas_call(
        body, name="ssd_fwd", grid=(N_CHUNKS,),
        in_specs=[pl.BlockSpec((CHUNK, D_CONV), lambda c: (c, 0)), pl.BlockSpec((CHUNK, D_SSM), lambda c: (c, zcol)),
                  pl.BlockSpec((CHUNK, LANE), lambda c: (c, dtcol)), _full((1, LANE)), _full((1, LANE)), _full((1, LANE)),
                  _full((1, D_SSM))],
        out_specs=[pl.BlockSpec((CHUNK, D_SSM), lambda c: (c, 0)), pl.BlockSpec((CHUNK, D_SSM), lambda c: (c, 0)),
                   pl.BlockSpec((1, SSM_GROUPS, SSM_STATE, GROUP_W), lambda c: (c, 0, 0, 0))],
        out_shape=[jax.ShapeDtypeStruct((SEQ, D_SSM), f32), jax.ShapeDtypeStruct((SEQ, D_SSM), f32),
                   jax.ShapeDtypeStruct((N_CHUNKS, SSM_GROUPS, SSM_STATE, GROUP_W), f32)],
        scratch_shapes=[pltpu.VMEM((SSM_GROUPS, SSM_STATE, GROUP_W), f32)],
        compiler_params=_params("arbitrary"),
    )(act, proj, proj, dt_bias, a_log, d_skip, norm_g)


def ssd_bwd_g(act, proj, ypre, states, d_out, dt_bias, a_log, d_skip, norm_g):
    zcol, dtcol = OFF_Z // D_SSM, OFF_DT // LANE

    def body(act_ref, z_ref, dt_ref, ypre_ref, st_ref, do_ref, dtb_ref, alog_ref, dsk_ref, ng_ref,
             dact_ref, ddt_ref, dz_ref, dng_ref, dpar_ref, dstate):
        i = pl.program_id(0)

        @pl.when(i == 0)
        def _():
            for ref in (dng_ref, dpar_ref, dstate):
                ref[...] = jnp.zeros_like(ref)

        row, col, dtp, a_row, cs, cs_t, cs_last = _ssd_chunk_common(dt_ref[...], dtb_ref[...], alog_ref[...])
        upper = (row <= col).astype(f32)
        lane = lax.broadcasted_iota(jnp.int32, (CHUNK, LANE), 1)
        rowl = lax.broadcasted_iota(jnp.int32, (CHUNK, LANE), 0)
        e_cs = jnp.exp(cs)
        dte = jnp.exp(cs_last - cs)
        ecl = jnp.exp(cs_last)
        rows8 = jnp.concatenate([ecl, dsk_ref[...], jnp.zeros((6, LANE), f32)], axis=0)
        z = z_ref[...]
        sgz = _sigmoid(z)
        sz = z * sgz
        ng = ng_ref[...]
        ddt_mat = jnp.zeros((CHUNK, LANE), f32)
        dcs_mat = jnp.zeros((CHUNK, LANE), f32)
        dcs_t = jnp.zeros((LANE, CHUNK), f32)
        dcsl_row = jnp.zeros((1, LANE), f32)
        dd_row = jnp.zeros((1, LANE), f32)
        for g in range(SSM_GROUPS):
            gs = slice(g * GROUP_W, (g + 1) * GROUP_W)
            bsl = slice(D_SSM + g * SSM_STATE, D_SSM + (g + 1) * SSM_STATE)
            csl = slice(D_SSM + D_BC + g * SSM_STATE, D_SSM + D_BC + (g + 1) * SSM_STATE)
            ind = _group_indicator(g)
            y = ypre_ref[:, gs]
            part = y * sz[:, gs]
            r = _rms(part)
            yhat = part * r
            d_o = do_ref[:, gs]
            dng_ref[:, gs] += jnp.sum(d_o * yhat, axis=0, keepdims=True)
            dyz = _rms_bwd(d_o, yhat, r, ng[:, gs])
            dy = dyz * sz[:, gs]
            dz_ref[:, gs] = dyz * y * (sgz[:, gs] * (1.0 + z[:, gs] * (1.0 - sgz[:, gs])))

            xg = act_ref[:, gs]
            bg = act_ref[:, bsl]
            cg = act_ref[:, csl]
            dt_e, e_e, dte_e = _head_spread(dtp, ind), _head_spread(e_cs, ind), _head_spread(dte, ind)
            rows_e = _head_spread(rows8, ind)
            ecl_e, dsk_e = rows_e[0:1], rows_e[1:2]
            xdt = xg * dt_e
            prev = st_ref[0, g]
            dh = dstate[g]
            heads = range(g * HEADS_PER_GROUP, (g + 1) * HEADS_PER_GROUP)
            hsl = [slice(r_ * SSM_HEAD_DIM, (r_ + 1) * SSM_HEAD_DIM) for r_ in range(HEADS_PER_GROUP)]
            cb = _bdot_nt(cg, bg)
            lms = [_seg_decay(cs, cs_t, hd, row, col) for hd in heads]
            ms = [cb * lm for lm in lms]
            gmat = _bdot(cg, prev)
            dgm = dy * e_e
            dcg = _bdot_nt(dgm, prev)
            dprev = _bdot_tn(cg, dgm)
            dbg = _bdot_nt(xdt * dte_e, dh)
            dw = _bdot(bg, dh)
            dms = [_bdot_nt(dy[:, s_], xdt[:, s_]) for s_ in hsl]
            dxdts = [_bdot_tn(m, dy[:, s_]) for m, s_ in zip(ms, hsl)]
            dxdt = jnp.concatenate(dxdts, axis=1) + dw * dte_e
            dact_ref[:, gs] = dy * dsk_e + dxdt * dt_e
            dstate[g] = dprev + dh * ecl_e
            dcb = jnp.zeros((CHUNK, CHUNK), f32)
            for hd, dm, lm, m in zip(heads, dms, lms, ms):
                dcb = dcb + dm * lm
                dseg = dm * m
                dcs_mat = dcs_mat + jnp.where(lane == hd, jnp.sum(dseg, axis=1, keepdims=True), 0.0)
                dcs_t = jnp.where(row == hd, jnp.sum(dseg, axis=0, keepdims=True), dcs_t)
            dact_ref[:, bsl] = dbg + _bdot_tn(dcb, cg)
            dact_ref[:, csl] = dcg + _bdot(dcb, bg)
            ddte = _head_sums(dw * xdt, ind) * dte
            dcs_mat = dcs_mat + _head_sums(dy * gmat, ind) * e_cs - ddte
            ddt_mat = ddt_mat + _head_sums(dxdt * xg, ind)
            dcsl_row = (dcsl_row + jnp.sum(ddte, axis=0, keepdims=True)
                        + jnp.sum(_head_sums(dh * prev, ind), axis=0, keepdims=True) * ecl)
            dd_row = dd_row + jnp.sum(_head_sums(dy * xg, ind), axis=0, keepdims=True)
        dcs_mat = dcs_mat - dcs_t.T + jnp.where(rowl == CHUNK - 1, dcsl_row, 0.0)
        dda = _hdot(upper, dcs_mat)
        ddt_mat = ddt_mat + dda * a_row
        da_row = jnp.sum(dda * dtp, axis=0, keepdims=True)
        ddt_raw = ddt_mat * _sigmoid(dt_ref[...] + dtb_ref[...])
        ddt_ref[...] = ddt_raw
        dpar_ref[0:1, :] += jnp.sum(ddt_raw, axis=0, keepdims=True)
        dpar_ref[1:2, :] += da_row * a_row
        dpar_ref[2:3, :] += dd_row

    blk = lambda i: N_CHUNKS - 1 - i
    return pl.pallas_call(
        body, name="ssd_bwd", grid=(N_CHUNKS,),
        in_specs=[pl.BlockSpec((CHUNK, D_CONV), lambda i: (blk(i), 0)), pl.BlockSpec((CHUNK, D_SSM), lambda i: (blk(i), zcol)),
                  pl.BlockSpec((CHUNK, LANE), lambda i: (blk(i), dtcol)), pl.BlockSpec((CHUNK, D_SSM), lambda i: (blk(i), 0)),
                  pl.BlockSpec((1, SSM_GROUPS, SSM_STATE, GROUP_W), lambda i: (blk(i), 0, 0, 0)),
                  pl.BlockSpec((CHUNK, D_SSM), lambda i: (blk(i), 0)),
                  _full((1, LANE)), _full((1, LANE)), _full((1, LANE)), _full((1, D_SSM))],
        out_specs=[pl.BlockSpec((CHUNK, D_CONV), lambda i: (blk(i), 0)), pl.BlockSpec((CHUNK, LANE), lambda i: (blk(i), 0)),
                   pl.BlockSpec((CHUNK, D_SSM), lambda i: (blk(i), 0)), _full((1, D_SSM)), _full((8, LANE))],
        out_shape=[jax.ShapeDtypeStruct((SEQ, D_CONV), f32), jax.ShapeDtypeStruct((SEQ, LANE), f32),
                   jax.ShapeDtypeStruct((SEQ, D_SSM), f32), jax.ShapeDtypeStruct((1, D_SSM), f32),
                   jax.ShapeDtypeStruct((8, LANE), f32)],
        scratch_shapes=[pltpu.VMEM((SSM_GROUPS, SSM_STATE, GROUP_W), f32)],
        compiler_params=_params("arbitrary"),
    )(act, proj, proj, ypre, states, d_out, dt_bias, a_log, d_skip, norm_g)


def out_fwd(x, attn, ssm, w_out, tm=512):
    def body(x_ref, a_ref, s_ref, w_ref, o_ref):
        o_ref[...] = x_ref[...] + _bdot(a_ref[...], w_ref[:D_ATTN, :]) + _bdot(s_ref[...], w_ref[D_ATTN:, :])

    tok = lambda w_: pl.BlockSpec((tm, w_), lambda i: (i, 0))
    return pl.pallas_call(
        body, name="out_fwd", grid=(SEQ // tm,),
        in_specs=[tok(D_MODEL), tok(D_ATTN), tok(D_SSM), _full((D_MODEL, D_MODEL))],
        out_specs=tok(D_MODEL), out_shape=jax.ShapeDtypeStruct((SEQ, D_MODEL), f32),
        compiler_params=_params("arbitrary"),
    )(x, attn, ssm, w_out)


def out_bwd(dx1, attn, ssm, w_out, tm=512):
    nt = SEQ // tm

    def body(d_ref, a_ref, s_ref, w_ref, da_ref, ds_ref, dw_ref, dw16_ref):
        i = pl.program_id(0)

        @pl.when(i == 0)
        def _():
            dw_ref[...] = jnp.zeros_like(dw_ref)

        d = d_ref[...].astype(bf16)
        dcat = _bdot_nt(d, w_ref[...])
        da_ref[...] = dcat[:, :D_ATTN]
        ds_ref[...] = dcat[:, D_ATTN:]
        dw_ref[:D_ATTN, :] += _bdot_tn(a_ref[...], d)
        dw_ref[D_ATTN:, :] += _bdot_tn(s_ref[...], d)

        @pl.when(i == nt - 1)
        def _():
            dw16_ref[...] = dw_ref[...].astype(bf16)

    tok = lambda w_: pl.BlockSpec((tm, w_), lambda i: (i, 0))
    return pl.pallas_call(
        body, name="out_bwd", grid=(nt,),
        in_specs=[tok(D_MODEL), tok(D_ATTN), tok(D_SSM), _resident((D_MODEL, D_MODEL))],
        out_specs=[tok(D_ATTN), tok(D_SSM), _resident((D_MODEL, D_MODEL)), _resident((D_MODEL, D_MODEL))],
        out_shape=[jax.ShapeDtypeStruct((SEQ, D_ATTN), f32), jax.ShapeDtypeStruct((SEQ, D_SSM), f32),
                   jax.ShapeDtypeStruct((D_MODEL, D_MODEL), f32), jax.ShapeDtypeStruct((D_MODEL, D_MODEL), bf16)],
        compiler_params=_params("arbitrary"),
    )(dx1, attn, ssm, w_out)


MLP_SUB = 256


def mlp_fwd(x1, g, w_up, w_down, tm=1024):
    def body(x_ref, g_ref, wu_ref, wd_ref, o_ref, u_ref, h_scr):
        j = pl.program_id(1)

        @pl.when(j == 0)
        def _():
            xv = x_ref[...]
            h_scr[...] = (xv * _rms(xv) * g_ref[...]).astype(bf16)
            o_ref[...] = xv

        for r in range(tm // MLP_SUB):
            rows = slice(r * MLP_SUB, (r + 1) * MLP_SUB)
            u = jnp.dot(h_scr[rows, :], wu_ref[...], preferred_element_type=f32)
            u_ref[rows, :] = u
            a = jnp.square(jnp.maximum(u, 0.0))
            o_ref[rows, :] += _bdot(a, wd_ref[...])

    return pl.pallas_call(
        body, name="mlp_fwd", grid=(SEQ // tm, N_CHIPS),
        in_specs=[pl.BlockSpec((tm, D_MODEL), lambda i, j: (i, 0)), _full((1, D_MODEL)),
                  pl.BlockSpec((None, D_MODEL, FF_TILE), lambda i, j: (j, 0, 0)),
                  pl.BlockSpec((None, FF_TILE, D_MODEL), lambda i, j: (j, 0, 0))],
        out_specs=[pl.BlockSpec((tm, D_MODEL), lambda i, j: (i, 0)), pl.BlockSpec((tm, FF_TILE), lambda i, j: (i, j))],
        out_shape=[jax.ShapeDtypeStruct((SEQ, D_MODEL), f32), jax.ShapeDtypeStruct((SEQ, D_FF), f32)],
        scratch_shapes=[pltpu.VMEM((tm, D_MODEL), bf16)],
        compiler_params=_params("arbitrary", "arbitrary"),
    )(x1, g, w_up, w_down)


def mlp_bwd_data(dx2, u, x1, g, w_up, w_down, tm=1024):
    def body(d_ref, u_ref, x_ref, g_ref, wu_ref, wd_ref, dx_ref, du_ref, dg_ref, dh_scr):
        i, j = pl.program_id(0), pl.program_id(1)

        @pl.when(jnp.logical_and(i == 0, j == 0))
        def _():
            dg_ref[...] = jnp.zeros_like(dg_ref)

        @pl.when(j == 0)
        def _():
            dh_scr[...] = jnp.zeros_like(dh_scr)

        for r in range(tm // MLP_SUB):
            rows = slice(r * MLP_SUB, (r + 1) * MLP_SUB)
            da = _bdot_nt(d_ref[rows, :], wd_ref[...])
            du = (da * (2.0 * jnp.maximum(u_ref[rows, :], 0.0))).astype(bf16)
            du_ref[rows, :] = du
            dh_scr[rows, :] += _bdot_nt(du, wu_ref[...])

        @pl.when(j == N_CHIPS - 1)
        def _():
            xv = x_ref[...]
            r = _rms(xv)
            xhat = xv * r
            dh = dh_scr[...]
            dg_ref[...] += jnp.sum(dh * xhat, axis=0, keepdims=True)
            dx_ref[...] = d_ref[...] + _rms_bwd(dh, xhat, r, g_ref[...])

    return pl.pallas_call(
        body, name="mlp_bwd_data", grid=(SEQ // tm, N_CHIPS),
        in_specs=[pl.BlockSpec((tm, D_MODEL), lambda i, j: (i, 0)), pl.BlockSpec((tm, FF_TILE), lambda i, j: (i, j)),
                  pl.BlockSpec((tm, D_MODEL), lambda i, j: (i, 0)), _full((1, D_MODEL)),
                  pl.BlockSpec((None, D_MODEL, FF_TILE), lambda i, j: (j, 0, 0)),
                  pl.BlockSpec((None, FF_TILE, D_MODEL), lambda i, j: (j, 0, 0))],
        out_specs=[pl.BlockSpec((tm, D_MODEL), lambda i, j: (i, 0)), pl.BlockSpec((tm, FF_TILE), lambda i, j: (i, j)),
                   _full((1, D_MODEL))],
        out_shape=[jax.ShapeDtypeStruct((SEQ, D_MODEL), f32), jax.ShapeDtypeStruct((SEQ, D_FF), bf16),
                   jax.ShapeDtypeStruct((1, D_MODEL), f32)],
        scratch_shapes=[pltpu.VMEM((tm, D_MODEL), f32)],
        compiler_params=_params("arbitrary", "arbitrary"),
    )(dx2, u, x1, g, w_up, w_down)


def mlp_bwd_weights(dx2, u, du, x1, g, tm=512):
    nt = SEQ // tm

    def body(d_ref, u_ref, du_ref, x_ref, g_ref, dwu_ref, dwd_ref, dwu16_ref, dwd16_ref, h_scr, d_scr):
        j, i = pl.program_id(0), pl.program_id(1)

        @pl.when(j == 0)
        def _():
            xv = x_ref[...]
            h_scr[i] = (xv * _rms(xv) * g_ref[...]).T.astype(bf16)
            d_scr[i] = d_ref[...].astype(bf16)

        @pl.when(i == 0)
        def _():
            dwu_ref[...] = jnp.zeros_like(dwu_ref)
            dwd_ref[...] = jnp.zeros_like(dwd_ref)

        dwu_ref[...] += jnp.dot(h_scr[i], du_ref[...], preferred_element_type=f32)
        a = jnp.square(jnp.maximum(u_ref[...], 0.0))
        dwd_ref[...] += _bdot_tn(a, d_scr[i])

        @pl.when(i == nt - 1)
        def _():
            dwu16_ref[...] = dwu_ref[...].astype(bf16)
            dwd16_ref[...] = dwd_ref[...].astype(bf16)

    up = pl.BlockSpec((None, D_MODEL, FF_TILE), lambda j, i: (j, 0, 0))
    down = pl.BlockSpec((None, FF_TILE, D_MODEL), lambda j, i: (j, 0, 0))
    first_pass = pl.BlockSpec((tm, D_MODEL), lambda j, i: (jnp.where(j == 0, i, nt - 1), 0))
    return pl.pallas_call(
        body, name="mlp_bwd_weights", grid=(N_CHIPS, nt),
        in_specs=[first_pass, pl.BlockSpec((tm, FF_TILE), lambda j, i: (i, j)),
                  pl.BlockSpec((tm, FF_TILE), lambda j, i: (i, j)), first_pass, _full((1, D_MODEL))],
        out_specs=[up, down, up, down],
        out_shape=[jax.ShapeDtypeStruct((N_CHIPS, D_MODEL, FF_TILE), f32), jax.ShapeDtypeStruct((N_CHIPS, FF_TILE, D_MODEL), f32),
                   jax.ShapeDtypeStruct((N_CHIPS, D_MODEL, FF_TILE), bf16), jax.ShapeDtypeStruct((N_CHIPS, FF_TILE, D_MODEL), bf16)],
        scratch_shapes=[pltpu.VMEM((nt, D_MODEL, tm), bf16), pltpu.VMEM((nt, tm, D_MODEL), bf16)],
        compiler_params=_params("arbitrary", "arbitrary"),
    )(dx2, u, du, x1, g)


def loss_head(y, target, tm=512):
    def body(y_ref, t_ref, dy_ref, l_ref):
        @pl.when(pl.program_id(0) == 0)
        def _():
            l_ref[...] = jnp.zeros_like(l_ref)

        d = y_ref[...] - t_ref[...]
        dy_ref[...] = d * (1.0 / D_MODEL)
        part = jnp.sum(jnp.mean(d * d, axis=-1, keepdims=True), axis=0, keepdims=True)
        l_ref[...] += 0.5 * part

    tok = pl.BlockSpec((tm, D_MODEL), lambda i: (i, 0))
    return pl.pallas_call(
        body, name="loss_head", grid=(SEQ // tm,), in_specs=[tok, tok], out_specs=[tok, _full((1, 1))],
        out_shape=[jax.ShapeDtypeStruct((SEQ, D_MODEL), f32), jax.ShapeDtypeStruct((1, 1), f32)],
        compiler_params=_params("arbitrary"),
    )(y, target)


def _pad_lane(v):
    return jnp.pad(v, (0, LANE - v.shape[0]))[None, :]


def local_step(x, target, w, prov):
    bucket = jnp.asarray(_bucket_table().T)
    bias = bias_build(w["rel_bias"], bucket)
    saved = []
    for l in range(DEPTH):
        g_mix = w["mix_norm_g"][l][None, :] + prov.stage(("begin", l), x)
        w_in = prov.w_in(l, x)
        proj = in_fwd(x, g_mix, w_in)
        conv_b = w["conv_b"][l][None, :]
        act = conv_fwd(proj, w["conv_w"][l], conv_b)
        dtb = _pad_lane(w["dt_bias"][l]) + prov.stage(("mid", l), act)
        alog, dsk = _pad_lane(w["a_log"][l]), _pad_lane(w["d_skip"][l])
        ng = w["ssm_norm_g"][l][None, :]
        ssm, ypre, states = ssd_fwd_g(act, proj, dtb, alog, dsk, ng)
        qg, kg = w["q_gain"][l][:, None] + 0.0 * ssm[:1, :1], w["k_gain"][l][None, :]
        attn = attn_fwd_t(proj, qg, kg, w["sinks"][l], bias)
        tok = prov.stage(("pre_out", l), attn)
        w_out = prov.w_out(l, attn) + jnp.asarray(tok, bf16)
        x1 = out_fwd(x, attn, ssm, w_out)
        g_mlp = w["mlp_norm_g"][l][None, :] + prov.stage(("pre_mlp", l), x1)
        w_up, w_down = prov.mlp(l, x1)
        x2, u = mlp_fwd(x1, g_mlp, w_up, w_down)
        saved.append(dict(x=x, proj=proj, attn=attn, act=act, ssm=ssm, ypre=ypre, states=states, x1=x1, u=u,
                          g_mix=g_mix, qg=qg, kg=kg, conv_b=conv_b, dtb=dtb, alog=alog, dsk=dsk, ng=ng, g_mlp=g_mlp,
                          w_in=w_in, w_out=w_out, w_up=w_up, w_down=w_down))
        x = x2
    dx, loss = loss_head(x, target)
    grads = [None] * DEPTH
    dbands = [None] * DEPTH
    tok = 0.0
    for l in reversed(range(DEPTH)):
        s = saved[l]
        g_mlp = s["g_mlp"] + tok
        dx1, du, dg_mlp = mlp_bwd_data(dx, s["u"], s["x1"], g_mlp, s["w_up"], s["w_down"])
        dw_up, dw_down, dw_up16, dw_down16 = mlp_bwd_weights(dx, s["u"], du, s["x1"], g_mlp)
        tok = prov.grads(("mlp", l), dict(w_up=(dw_up, dw_up16), w_down=(dw_down, dw_down16)), dw_down)
        dattn, dssm, dw_out, dw_out16 = out_bwd(dx1, s["attn"], s["ssm"], s["w_out"])
        dact, ddt, dz, dng, dpar = ssd_bwd_g(s["act"], s["proj"], s["ypre"], s["states"], dssm, s["dtb"] + tok, s["alog"],
                                           s["dsk"], s["ng"])
        conv_b = s["conv_b"] + prov.stage(("bwd_mid", l), dact)
        dxbc, dconv_w, dconv_b = conv_bwd(s["proj"], dact, w["conv_w"][l], conv_b)
        dq, dk, dv, dband, dsink, dqg, dkg = attn_bwd_t(s["proj"], dattn, s["qg"], s["kg"], w["sinks"][l], bias)
        dbands[l] = dband
        g_mix = s["g_mix"]
        if l == 0:
            d_rel = bias_bwd(dbands[0], dbands[1], bucket)
            g_mix = g_mix + 0.0 * d_rel[:1, :1]
        dx, dw_in, dg_mix = in_bwd(dq, dz, dxbc, dk, dv, ddt, s["x"], g_mix, s["w_in"], dx1)
        tok = prov.grads(("mix", l), dict(w_in=split_w_in_grad(dw_in), w_out=(dw_out, dw_out16)), dx)
        grads[l] = dict(mix_norm_g=dg_mix[0], q_gain=dqg[:, 0], k_gain=dkg[0], sinks=dsink[:, 0],
                        conv_w=dconv_w, conv_b=dconv_b[0], dt_bias=dpar[0, :SSM_HEADS], a_log=dpar[1, :SSM_HEADS],
                        d_skip=dpar[2, :SSM_HEADS], ssm_norm_g=dng[0], mlp_norm_g=dg_mlp[0])
    out = {k: jnp.stack([grads[l][k] for l in range(DEPTH)]) for k in grads[0]}
    out["rel_bias"] = d_rel[:, :N_Q_HEADS]
    return loss, dx, out, tok


MESH = pl.DeviceIdType.MESH
HBM = pl.BlockSpec(memory_space=pltpu.HBM)
N_DEVICES = 8


def _coords():
    return lax.axis_index("x"), lax.axis_index("y"), lax.axis_index("c")


def _peer_chips(x, y):
    return [(1 - x, y), (x, 1 - y), (1 - x, 1 - y)]


def _remote(src, dst, send_sem, recv_sem, device):
    return pltpu.make_async_remote_copy(src_ref=src, dst_ref=dst, send_sem=send_sem, recv_sem=recv_sem,
                                        device_id=device, device_id_type=MESH)


SEM = pl.BlockSpec(memory_space=pltpu.SEMAPHORE)
ANY = pl.BlockSpec(memory_space=pl.ANY)
DATAFLOW = pltpu.SideEffectType.DATAFLOW_SIDE_EFFECTING


def _gather_copies(kind, src_refs, land_refs, ssem, rsem):
    x, y, c = _coords()
    k_me = 2 * x + y
    n = len(land_refs)
    cps = []
    for p, land in enumerate(land_refs):
        hr = land.shape[1] // 2
        rows = pl.ds(c * hr, hr)
        for j, chip in enumerate(_peer_chips(x, y)):
            i = 3 * p + j
            if kind == "ici":
                cps.append(_remote(src_refs[p].at[rows, :], land.at[k_me, rows, :], ssem.at[i], rsem.at[i], (*chip, c)))
            else:
                got = land.at[2 * chip[0] + chip[1], rows, :]
                cps.append(_remote(got, got, ssem.at[i], rsem.at[i], (x, y, 1 - c)))
        if kind == "relay":
            cps.append(_remote(src_refs[p], land.at[k_me], ssem.at[3 * n + p], rsem.at[3 * n + p], (x, y, 1 - c)))
    return cps


def gather_now(srcs, conv):
    n = len(srcs)

    def body(*refs):
        src_refs, conv_ref = refs[:n], refs[n]
        lands, gconv = refs[n + 1:2 * n + 1], refs[2 * n + 1]
        ssem, rsem, fsem, frsem, csem, crsem = refs[2 * n + 2:]
        x, y, c = _coords()
        k_me = 2 * x + y
        targets = [(*chip, c) for chip in _peer_chips(x, y)] + [(x, y, 1 - c)]
        ici = _gather_copies("ici", src_refs, lands, ssem, rsem)
        relay = _gather_copies("relay", src_refs, lands, fsem, frsem)
        passed = [cp for i, cp in enumerate(relay) if i % 4 != 3]
        own = relay[3::4]
        conv_cps = [_remote(conv_ref, gconv.at[k_me], csem.at[j], crsem.at[j], t) for j, t in enumerate(targets)]
        for cp in ici + conv_cps + own:
            cp.start()
        for cp, fw in zip(ici, passed):
            cp.wait_recv()
            fw.start()
        for cp in conv_cps + relay:
            cp.wait_recv()
        for cp in ici + relay + conv_cps:
            cp.wait_send()

    out_shape = [jax.ShapeDtypeStruct((N_CHIPS,) + s.shape, s.dtype) for s in srcs]
    out_shape.append(jax.ShapeDtypeStruct((N_CHIPS,) + conv.shape, conv.dtype))
    sems = lambda k: pltpu.SemaphoreType.DMA((k,))
    return pl.pallas_call(
        body, name="gather_now", out_shape=out_shape, in_specs=[HBM] * (n + 1), out_specs=[HBM] * (n + 1),
        scratch_shapes=[sems(3 * n), sems(3 * n), sems(4 * n), sems(4 * n), sems(N_CHIPS), sems(N_CHIPS)],
    )(*srcs, conv)


def _gather_maker(kind, n_src):
    def make(refs, ssem, rsem):
        cps = _gather_copies(kind, refs[:n_src], refs[n_src:], ssem, rsem)
        return cps, cps
    return make


def _scatter_maker(n):
    def make(refs, ssem, rsem):
        x, y, c = _coords()
        k_me = 2 * x + y
        sends, arrivals = [], []
        for p in range(n):
            src, land = refs[p], refs[n + p]
            sends.append(_remote(src.at[k_me, 1 - c], land.at[0], ssem.at[7 * p], rsem.at[7 * p], (x, y, 1 - c)))
            for j, chip in enumerate(_peer_chips(x, y)):
                for cc in range(2):
                    sends.append(_remote(src.at[2 * chip[0] + chip[1], cc], land.at[1 + 2 * j + c],
                                         ssem.at[7 * p + 1 + 2 * j + cc], rsem.at[7 * p + 1 + 2 * j + c], (*chip, cc)))
            for s in range(7):
                arrivals.append(_remote(land.at[s], land.at[s], ssem.at[7 * p + s], rsem.at[7 * p + s], (x, y, 1 - c)))
        return sends, arrivals
    return make


def _share_maker(n):
    def make(refs, ssem, rsem):
        x, y, c = _coords()
        sends = [_remote(refs[p].at[c], refs[p].at[c], ssem.at[p], rsem.at[p], (x, y, 1 - c)) for p in range(n)]
        arrivals = [_remote(refs[p].at[1 - c], refs[p].at[1 - c], ssem.at[p], rsem.at[p], (x, y, 1 - c)) for p in range(n)]
        return sends, arrivals
    return make


def split_start(name, make, n_sems, operands, after):
    n = len(operands)

    def body(*refs):
        ssem, rsem, token = refs[n + 1], refs[n + 2], refs[-1]
        for cp in make(refs[:n], ssem, rsem)[0]:
            cp.start()
        token[...] = jnp.zeros_like(token)

    ops = [pltpu.with_memory_space_constraint(a, pltpu.HBM) for a in operands]
    outs = pl.pallas_call(
        body, name=name,
        out_shape=(pltpu.SemaphoreType.DMA((n_sems,)), pltpu.SemaphoreType.DMA((n_sems,)),
                   *[pltpu.HBM(a.shape, a.dtype) for a in ops], jax.ShapeDtypeStruct((8, LANE), f32)),
        in_specs=[HBM] * n + [ANY], out_specs=(SEM, SEM, *[HBM] * n, pl.BlockSpec(memory_space=pltpu.VMEM)),
        input_output_aliases={i: 2 + i for i in range(n)},
        compiler_params=pltpu.CompilerParams(has_side_effects=DATAFLOW),
    )(*ops, after)
    return dict(name=name, make=make, ssem=outs[0], rsem=outs[1], operands=outs[2:2 + n], token=outs[-1][0, 0])


def split_wait(handle, after):
    n = len(handle["operands"])

    def body(*refs):
        sends, arrivals = handle["make"](refs[:n], refs[n], refs[n + 1])
        for cp in sends:
            cp.wait_send()
        for cp in arrivals:
            cp.wait_recv()

    outs = pl.pallas_call(
        body, name=handle["name"].replace("start", "wait"),
        out_shape=tuple(pltpu.HBM(a.shape, a.dtype) for a in handle["operands"]),
        in_specs=[HBM] * n + [SEM, SEM, ANY], out_specs=tuple([HBM] * n),
        input_output_aliases={i: i for i in range(n)},
        compiler_params=pltpu.CompilerParams(has_side_effects=DATAFLOW),
    )(*handle["operands"], handle["ssem"], handle["rsem"], after)
    return list(outs)


def piece_sum(g, recv, kc_arr):
    _, _, rb, cc = g.shape
    tr = min(256, rb)

    def body(kc_ref, g_ref, r_ref, o_ref):
        acc = g_ref[...]
        for s in range(7):
            acc = acc + r_ref[s].astype(f32)
        o_ref[...] = acc

    return pl.pallas_call(
        body, name="piece_sum",
        grid_spec=pltpu.PrefetchScalarGridSpec(
            num_scalar_prefetch=1, grid=(rb // tr,),
            in_specs=[pl.BlockSpec((None, None, tr, cc), lambda r, kc: (kc[0], kc[1], r, 0)),
                      pl.BlockSpec((7, tr, cc), lambda r, kc: (0, r, 0))],
            out_specs=pl.BlockSpec((None, tr, cc), lambda r, kc: (kc[1], r, 0))),
        out_shape=jax.ShapeDtypeStruct((2, rb, cc), f32),
        compiler_params=_params("arbitrary"),
    )(kc_arr, g, recv)


def small_all_reduce(vec):
    def body(v_ref, o_ref, gat, ssem, rsem):
        x, y, c = _coords()
        me = 4 * x + 2 * y + c
        gat[me] = v_ref[...]
        sends = []
        for t in range(1, N_DEVICES):
            peer = (x ^ (t >> 2), y ^ ((t >> 1) & 1), c ^ (t & 1))
            cp = _remote(v_ref, gat.at[me], ssem.at[t - 1], rsem.at[t - 1], peer)
            cp.start()
            sends.append(cp)
        for t in range(1, N_DEVICES):
            peer = (x ^ (t >> 2), y ^ ((t >> 1) & 1), c ^ (t & 1))
            slot = gat.at[4 * peer[0] + 2 * peer[1] + peer[2]]
            _remote(slot, slot, ssem.at[t - 1], rsem.at[t - 1], peer).wait_recv()
        for cp in sends:
            cp.wait_send()
        acc = gat[0]
        for d in range(1, N_DEVICES):
            acc = acc + gat[d]
        o_ref[...] = acc

    return pl.pallas_call(
        body, name="small_all_reduce", out_shape=jax.ShapeDtypeStruct(vec.shape, vec.dtype),
        in_specs=[pl.BlockSpec(memory_space=pltpu.VMEM)], out_specs=pl.BlockSpec(memory_space=pltpu.VMEM),
        scratch_shapes=[pltpu.VMEM((N_DEVICES,) + vec.shape, vec.dtype), pltpu.SemaphoreType.DMA((N_DEVICES - 1,)),
                        pltpu.SemaphoreType.DMA((N_DEVICES - 1,))],
    )(vec)


def _adamw_math(w, g, m, v):
    m_new = ADAM_B1 * m + (1.0 - ADAM_B1) * g
    v_new = ADAM_B2 * v + (1.0 - ADAM_B2) * jnp.square(g)
    m_hat = m_new / (1.0 - ADAM_B1 ** ADAM_STEP)
    v_hat = v_new / (1.0 - ADAM_B2 ** ADAM_STEP)
    delta = -ADAM_LR * (m_hat / (jnp.sqrt(v_hat) + ADAM_EPS) + ADAM_WD * w)
    return delta, m_new, v_new


def adamw_shard(w, g0, g1, m, v):
    depth, rows, cols = w.shape
    half = rows // 2
    tr = min(256, half)
    nr = half // tr

    def body(w_ref, g0_ref, g1_ref, m_ref, v_ref, go_ref, d_ref, nm_ref, nv_ref):
        gv = jnp.where(pl.program_id(0) == 0, g0_ref[...], g1_ref[...])
        go_ref[...] = gv
        d_ref[...], nm_ref[...], nv_ref[...] = _adamw_math(w_ref[...], gv, m_ref[...], v_ref[...])

    spec = pl.BlockSpec((None, tr, cols), lambda l, h, r: (l, h * nr + r, 0))
    g0spec = pl.BlockSpec((None, tr, cols), lambda l, h, r: (jnp.where(l == 0, h, 1), jnp.where(l == 0, r, nr - 1), 0))
    g1spec = pl.BlockSpec((None, tr, cols), lambda l, h, r: (jnp.where(l == 1, h, 0), jnp.where(l == 1, r, 0), 0))
    return pl.pallas_call(
        body, name="adamw_shard", grid=(depth, 2, nr), in_specs=[spec, g0spec, g1spec, spec, spec], out_specs=[spec] * 4,
        out_shape=[jax.ShapeDtypeStruct(w.shape, f32)] * 4,
        compiler_params=_params("arbitrary", "arbitrary", "arbitrary"),
    )(w, g0, g1, m, v)


def adamw_cols(w, g, m, v, tc=34):
    cols, depth, rows = w.shape

    def body(w_ref, g_ref, m_ref, v_ref, d_ref, nm_ref, nv_ref):
        d_ref[...], nm_ref[...], nv_ref[...] = _adamw_math(w_ref[...], g_ref[...], m_ref[...], v_ref[...])

    spec = pl.BlockSpec((tc, depth, rows), lambda i: (i, 0, 0))
    return pl.pallas_call(
        body, name="adamw_cols", grid=(cols // tc,), in_specs=[spec] * 4, out_specs=[spec] * 3,
        out_shape=[jax.ShapeDtypeStruct(w.shape, f32)] * 3,
        compiler_params=_params("arbitrary"),
    )(w, g, m, v)


def adamw_small(w, g, m, v):
    def body(w_ref, g_ref, m_ref, v_ref, d_ref, nm_ref, nv_ref):
        d_ref[...], nm_ref[...], nv_ref[...] = _adamw_math(w_ref[...], g_ref[...], m_ref[...], v_ref[...])

    return pl.pallas_call(
        body, name="adamw_small", out_shape=[jax.ShapeDtypeStruct(w.shape, f32)] * 3,
    )(w, g, m, v)


WEIGHTS = ("mix_norm_g", "w_in", "q_gain", "k_gain", "sinks", "rel_bias", "conv_w", "conv_b", "dt_bias", "a_log", "d_skip",
           "ssm_norm_g", "w_out", "mlp_norm_g", "w_up", "w_down")
BIG = ("w_in", "w_out", "w_up", "w_down")
SMALL = tuple(n for n in WEIGHTS if n not in BIG)
PACK_COLS = 1024
PACK_ROWS = 16


def _pack(named, last=None):
    flat = jnp.concatenate([named[n].reshape(-1) for n in SMALL])
    tail = jnp.zeros((1,), f32) if last is None else last.reshape(1)
    pad = jnp.zeros((PACK_ROWS * PACK_COLS - flat.shape[0] - 1,), f32)
    return jnp.concatenate([flat, pad, tail]).reshape(PACK_ROWS, PACK_COLS)


def _unpack(buf, shapes):
    flat = buf.reshape(-1)
    out, at = {}, 0
    for n in SMALL:
        size = int(np.prod(shapes[n]))
        out[n] = flat[at:at + size].reshape(shapes[n])
        at += size
    return out


class _Exchange:
    GROUPS = {"A": (("w_up", 0), ("w_down", 0)), "B": (("w_in", 1), ("w_out", 1)), "C": (("w_up", 1), ("w_down", 1))}
    ICI_AT = {("mid", 0): "B", ("pre_out", 0): "C"}
    RELAY_AT = {("pre_out", 0): "A", ("pre_mlp", 0): "B", ("mid", 1): "C"}
    LAST = ("mix", 0)
    IN_FLIGHT = 2

    def __init__(self, wts, kc_arr):
        self.wts, self.kc_arr = wts, kc_arr
        self.own = {(n, l): wts[n][l].astype(bf16) for n in BIG for l in range(DEPTH)}
        now = gather_now([self.own["w_in", 0], self.own["w_out", 0]], wts["conv_w"])
        self.ready = {("w_in", 0): now[0], ("w_out", 0): now[1]}
        self.conv_w = jnp.transpose(now[2], (1, 2, 0, 3)).reshape(DEPTH, CONV_WIDTH, D_CONV)
        self.ici, self.relay = {}, {}
        self.gview, self.scatter, self.share, self.reduced = {}, [], [], {}
        self._start_ici("A", now[2])

    def _start_ici(self, g, after):
        srcs = [self.own[p] for p in self.GROUPS[g]]
        lands = [lax.empty((N_CHIPS,) + s.shape, s.dtype) for s in srcs]
        self.ici[g] = split_start("gather%s_ici_start" % g, _gather_maker("ici", len(srcs)), 3 * len(srcs), srcs + lands,
                                  after)
        return self.ici[g]["token"]

    def stage(self, name, after):
        if name == ("begin", 0):
            return self.ici["A"]["token"]
        tok = 0.0
        g = self.RELAY_AT.get(name)
        if g is not None:
            n = len(self.GROUPS[g])
            self.relay[g] = split_start("gather%s_relay_start" % g, _gather_maker("relay", n), 4 * n,
                                        split_wait(self.ici[g], after), after)
            tok = self.relay[g]["token"]
        if name in self.ICI_AT:
            tok = tok + self._start_ici(self.ICI_AT[name], after)
        return tok

    def _get(self, piece, after):
        if piece not in self.ready:
            g = [k for k, pieces in self.GROUPS.items() if piece in pieces][0]
            lands = split_wait(self.relay[g], after)[len(self.GROUPS[g]):]
            self.ready.update(zip(self.GROUPS[g], lands))
        return self.ready[piece]

    def w_in(self, l, after):
        return align_w_in(self._get(("w_in", l), after))

    def w_out(self, l, after):
        return self._get(("w_out", l), after).reshape(D_MODEL, D_MODEL)

    def mlp(self, l, after):
        return self._get(("w_up", l), after), self._get(("w_down", l), after)

    def _view(self, n, g):
        _, rows, cols = self.wts[n].shape
        return g.reshape(N_CHIPS, 2, rows // 2, cols)

    def grads(self, name, arrays, after):
        if name == self.LAST:
            self.held = (name, arrays)
            return 0.0
        return self._scatter(name, arrays, after) + self._advance(after, self.IN_FLIGHT)

    def flush(self, after):
        return self._scatter(*self.held, after) + self._advance(after, self.IN_FLIGHT)

    def _scatter(self, name, arrays, after):
        pieces = [(n, name[1]) for n in arrays]
        views = [self._view(n, g) for n, (g, _) in arrays.items()]
        sends = [g16.reshape(v.shape) for v, (_, g16) in zip(views, arrays.values())]
        self.gview.update(zip(pieces, views))
        lands = [lax.empty((7,) + v.shape[2:], bf16) for v in views]
        h = split_start("scatter_%s%d_start" % name, _scatter_maker(len(views)), 7 * len(views), sends + lands, after)
        self.scatter.append((pieces, h))
        return h["token"]

    def _take_share(self, after):
        pieces, h = self.share.pop(0)
        self.reduced.update(zip(pieces, split_wait(h, after)))

    def _take_scatter(self, after):
        pieces, h = self.scatter.pop(0)
        lands = split_wait(h, after)[len(pieces):]
        sums = [piece_sum(self.gview[p], land, self.kc_arr) for p, land in zip(pieces, lands)]
        hs = split_start(h["name"].replace("scatter", "share"), _share_maker(len(sums)), len(sums), sums, after)
        self.share.append((pieces, hs))
        return hs["token"]

    def _advance(self, after, newest):
        if self.share:
            self._take_share(after)
        return self._take_scatter(after) if len(self.scatter) > newest else 0.0

    def reduced_grads(self, names, after):
        want = [(n, l) for n in names for l in range(DEPTH)]
        while not all(p in self.reduced for p in want):
            if any(p in pieces for p in want for pieces, _ in self.share):
                self._take_share(after)
            else:
                self._take_scatter(after)
        return {n: [self.reduced[n, l] for l in range(DEPTH)] for n in names}


def kernel(x, mix_norm_g, w_in, q_gain, k_gain, sinks, rel_bias, conv_w, conv_b, dt_bias, a_log, d_skip, ssm_norm_g, w_out, mlp_norm_g, w_up, w_down, loss_target, m_mix_norm_g, m_w_in, m_q_gain, m_k_gain, m_sinks, m_rel_bias, m_conv_w, m_conv_b, m_dt_bias, m_a_log, m_d_skip, m_ssm_norm_g, m_w_out, m_mlp_norm_g, m_w_up, m_w_down, v_mix_norm_g, v_w_in, v_q_gain, v_k_gain, v_sinks, v_rel_bias, v_conv_w, v_conv_b, v_dt_bias, v_a_log, v_d_skip, v_ssm_norm_g, v_w_out, v_mlp_norm_g, v_w_up, v_w_down):
    wts = dict(mix_norm_g=mix_norm_g, w_in=w_in, q_gain=q_gain, k_gain=k_gain, sinks=sinks, rel_bias=rel_bias, conv_w=conv_w,
               conv_b=conv_b, dt_bias=dt_bias, a_log=a_log, d_skip=d_skip, ssm_norm_g=ssm_norm_g, w_out=w_out,
               mlp_norm_g=mlp_norm_g, w_up=w_up, w_down=w_down)
    mom = dict(mix_norm_g=m_mix_norm_g, w_in=m_w_in, q_gain=m_q_gain, k_gain=m_k_gain, sinks=m_sinks, rel_bias=m_rel_bias,
               conv_w=m_conv_w, conv_b=m_conv_b, dt_bias=m_dt_bias, a_log=m_a_log, d_skip=m_d_skip, ssm_norm_g=m_ssm_norm_g,
               w_out=m_w_out, mlp_norm_g=m_mlp_norm_g, w_up=m_w_up, w_down=m_w_down)
    var = dict(mix_norm_g=v_mix_norm_g, w_in=v_w_in, q_gain=v_q_gain, k_gain=v_k_gain, sinks=v_sinks, rel_bias=v_rel_bias,
               conv_w=v_conv_w, conv_b=v_conv_b, dt_bias=v_dt_bias, a_log=v_a_log, d_skip=v_d_skip, ssm_norm_g=v_ssm_norm_g,
               w_out=v_w_out, mlp_norm_g=v_mlp_norm_g, w_up=v_w_up, w_down=v_w_down)
    xi, yi, ci = _coords()
    k_me = 2 * xi + yi
    kc_arr = jnp.stack([k_me, ci]).astype(jnp.int32)

    prov = _Exchange(wts, kc_arr)
    small_w = {n: wts[n] for n in SMALL}
    small_w["conv_w"] = prov.conv_w
    loss, dx, grads, tok = local_step(x[0], loss_target[0], small_w, prov)

    small_shapes = {n: grads[n].shape for n in SMALL}
    small_sum = small_all_reduce(_pack(grads, loss) + tok)
    loss = small_sum[PACK_ROWS - 1, PACK_COLS - 1]
    tok = prov.flush(small_sum)
    small = _unpack(small_sum, small_shapes)
    cols = conv_w.shape[-1]
    small["conv_w"] = lax.dynamic_slice_in_dim(small["conv_w"], k_me * cols, cols, axis=2)
    g_out_d, d_out_d, m_out_d, v_out_d = {}, {}, {}, {}
    shard_shapes = {n: wts[n].shape for n in SMALL}
    d, nm, nv = adamw_small(_pack(wts), _pack(small) + tok, _pack(mom), _pack(var))
    for dst, buf in ((d_out_d, d), (m_out_d, nm), (v_out_d, nv)):
        dst.update(_unpack(buf, shard_shapes))
    g_out_d.update(small)

    after = d
    for names in (("w_up", "w_down"), ("w_in", "w_out")):
        for n, (g0, g1) in prov.reduced_grads(names, after).items():
            if n == "w_in":
                rows, cols = wts[n].shape[1:]
                to_cols = lambda a: jnp.transpose(a, (2, 0, 1))
                g_t = jnp.stack([to_cols(g).reshape(cols, rows) for g in (g0, g1)], axis=1)
                res_t = adamw_cols(to_cols(wts[n]), g_t, to_cols(mom[n]), to_cols(var[n]))
                g_out_d[n], d_out_d[n], m_out_d[n], v_out_d[n] = (jnp.transpose(a, (1, 2, 0)) for a in (g_t, *res_t))
            else:
                g_out_d[n], d_out_d[n], m_out_d[n], v_out_d[n] = adamw_shard(wts[n], g0, g1, mom[n], var[n])
            after = d_out_d[n]

    return (loss, dx[None], *[g_out_d[n] for n in WEIGHTS], *[d_out_d[n] for n in WEIGHTS],
            *[m_out_d[n] for n in WEIGHTS], *[v_out_d[n] for n in WEIGHTS])
```

```python
import numpy as np
import jax
import jax.numpy as jnp
from jax import lax
from jax.experimental import pallas as pl
from jax.experimental.pallas import tpu as pltpu

f32 = jnp.float32
bf16 = jnp.bfloat16

SEQ = 2048
D_MODEL = 1024
DEPTH = 2
HEAD_DIM = 64
N_Q_HEADS = 8
N_KV_HEADS = 2
Q_PER_KV = N_Q_HEADS // N_KV_HEADS
BLOCK = 128
N_BLOCKS = SEQ // BLOCK
N_BUCKETS = 32
MAX_DISTANCE = 128
SSM_HEADS = 8
SSM_HEAD_DIM = 64
SSM_GROUPS = 2
HEADS_PER_GROUP = SSM_HEADS // SSM_GROUPS
SSM_STATE = 128
CONV_WIDTH = 4
CHUNK = 128
N_CHUNKS = SEQ // CHUNK
D_FF = 4 * D_MODEL
D_ATTN = N_Q_HEADS * HEAD_DIM
D_KV = N_KV_HEADS * HEAD_DIM
D_SSM = SSM_HEADS * SSM_HEAD_DIM
D_BC = SSM_GROUPS * SSM_STATE
D_CONV = D_SSM + 2 * D_BC
D_IN = D_ATTN + 2 * D_KV + D_SSM + D_CONV + SSM_HEADS
EPS = 1e-6
NEG = -1e30
N_CHIPS = 4
FF_TILE = D_FF // N_CHIPS

LANE = 128
PW = D_ATTN + D_SSM + D_CONV + 2 * D_KV + LANE
OFF_Q, OFF_Z, OFF_X, OFF_K, OFF_V, OFF_DT = 0, 512, 1024, 2048, 2176, 2304

ADAM_LR = 0.001
ADAM_B1 = 0.9
ADAM_B2 = 0.999
ADAM_EPS = 1e-08
ADAM_WD = 0.01
ADAM_STEP = 10

VMEM_LIMIT = 56 * 1024 * 1024


def _params(*sem):
    return pltpu.CompilerParams(dimension_semantics=tuple(sem), vmem_limit_bytes=VMEM_LIMIT)


def _bdot(a, b):
    return jnp.dot(a.astype(bf16), b.astype(bf16), preferred_element_type=f32)


def _bdot_nt(a, b):
    return lax.dot_general(a.astype(bf16), b.astype(bf16), (((1,), (1,)), ((), ())), preferred_element_type=f32)


def _bdot_tn(a, b):
    return lax.dot_general(a.astype(bf16), b.astype(bf16), (((0,), (0,)), ((), ())), preferred_element_type=f32)


def _hdot(a, b):
    return jnp.dot(a, b, precision=lax.Precision.HIGHEST, preferred_element_type=f32)


def _sigmoid(x):
    return 1.0 / (1.0 + jnp.exp(-x))


def _softplus(x):
    return jnp.maximum(x, 0.0) + jnp.log1p(jnp.exp(-jnp.abs(x)))


def _rms(x):
    return lax.rsqrt(jnp.mean(x * x, axis=-1, keepdims=True) + EPS)


def _rms_bwd(dy, xhat, r, g):
    t = dy * g
    return r * (t - xhat * jnp.mean(t * xhat, axis=-1, keepdims=True))


def _full(shape):
    return pl.BlockSpec(shape, lambda *_: (0,) * len(shape))


def _bucket_table():
    qi = np.arange(BLOCK)[:, None]
    kj = np.arange(2 * BLOCK)[None, :]
    dist = qi + BLOCK - kj
    ok = (dist >= 0) & (dist < 128)
    d = np.clip(dist, 0, None)
    max_exact = N_BUCKETS // 2
    d_f = np.maximum(d, 1).astype(np.float32)
    large = max_exact + (np.log(d_f / np.float32(max_exact)) / np.float32(np.log(MAX_DISTANCE / max_exact))
                         * np.float32(N_BUCKETS - max_exact)).astype(np.int32)
    large = np.minimum(large, N_BUCKETS - 1)
    bucket = np.where(d < max_exact, d, large)
    return np.where(ok, bucket, -1).astype(np.int32)


def bias_build(rel_bias, bucket):
    def body(rel_ref, bkt_ref, o_ref):
        bkt = bkt_ref[...]
        for h in range(N_Q_HEADS):
            acc = jnp.where(bkt < 0, NEG, 0.0).astype(f32)
            for b in range(N_BUCKETS):
                acc = acc + jnp.where(bkt == b, rel_ref[b, h], 0.0)
            o_ref[h] = acc

    return pl.pallas_call(
        body, name="bias_build", out_shape=jax.ShapeDtypeStruct((N_Q_HEADS,) + bucket.shape, f32),
        in_specs=[pl.BlockSpec(memory_space=pltpu.SMEM), pl.BlockSpec(memory_space=pltpu.VMEM)],
        out_specs=pl.BlockSpec(memory_space=pltpu.VMEM),
    )(rel_bias, bucket)


def bias_bwd(dband0, dband1, bucket):
    def body(d0_ref, d1_ref, bkt_ref, o_ref):
        bkt = bkt_ref[...]
        o_ref[...] = jnp.zeros_like(o_ref)
        for h in range(N_Q_HEADS):
            d = d0_ref[h] + d1_ref[h]
            for b in range(N_BUCKETS):
                part = jnp.sum(jnp.where(bkt == b, d, 0.0), axis=1, keepdims=True)
                o_ref[b:b + 1, h:h + 1] = jnp.sum(part, axis=0, keepdims=True)

    return pl.pallas_call(
        body, name="bias_bwd", out_shape=jax.ShapeDtypeStruct((N_BUCKETS, LANE), f32),
    )(dband0, dband1, bucket)


W_IN_SHARD = D_IN // N_CHIPS
_ALIGNED_PIECES = ((0, 0, 512), (1, 190, 578), (2, 0, 124), (2, 124, 578), (3, 0, 570), (0, 512, 578), (1, 0, 62),
                   (1, 62, 190), (3, 570, 578))
_SHARD_PIECES = (((0, 512), (2048, 2114)), ((2114, 2176), (2176, 2304), (512, 900)), ((900, 1024), (1024, 1478)),
                 ((1478, 2048), (2304, 2312)))


def align_w_in(shards, tr=256):
    def body(s_ref, o_ref):
        parts = [s_ref[k, :, a:b] for k, a, b in _ALIGNED_PIECES]
        parts.append(jnp.zeros((tr, LANE - SSM_HEADS), s_ref.dtype))
        o_ref[...] = jnp.concatenate(parts, axis=-1)

    return pl.pallas_call(
        body, name="align_w_in", grid=(D_MODEL // tr,),
        in_specs=[pl.BlockSpec((N_CHIPS, tr, W_IN_SHARD), lambda i: (0, i, 0))],
        out_specs=pl.BlockSpec((tr, PW), lambda i: (i, 0)),
        out_shape=jax.ShapeDtypeStruct((D_MODEL, PW), shards.dtype),
        compiler_params=_params("arbitrary"),
    )(shards)


def split_w_in_grad(dw, tr=256):
    def body(d_ref, o16_ref):
        for k, pieces in enumerate(_SHARD_PIECES):
            o16_ref[k] = jnp.concatenate([d_ref[:, a:b] for a, b in pieces], axis=-1).astype(bf16)

    return pl.pallas_call(
        body, name="split_w_in_grad", grid=(D_MODEL // tr,),
        in_specs=[pl.BlockSpec((tr, PW), lambda i: (i, 0))],
        out_specs=pl.BlockSpec((N_CHIPS, tr, W_IN_SHARD), lambda i: (0, i, 0)),
        out_shape=jax.ShapeDtypeStruct((N_CHIPS, D_MODEL, W_IN_SHARD), bf16),
        compiler_params=_params("arbitrary"),
    )(dw)

def in_fwd(x, g, w, tm=512):
    def body(x_ref, g_ref, w_ref, o_ref):
        xv = x_ref[...]
        h = xv * _rms(xv) * g_ref[...]
        o_ref[...] = _bdot(h, w_ref[...])

    return pl.pallas_call(
        body, name="in_fwd", grid=(SEQ // tm,),
        in_specs=[pl.BlockSpec((tm, D_MODEL), lambda i: (i, 0)), _full((1, D_MODEL)), _resident((D_MODEL, PW))],
        out_specs=pl.BlockSpec((tm, PW), lambda i: (i, 0)),
        out_shape=jax.ShapeDtypeStruct((SEQ, PW), f32),
        compiler_params=_params("arbitrary"),
    )(x, g, w)


def _resident(shape):
    return pl.BlockSpec(shape, lambda *_: (0,) * len(shape), pipeline_mode=pl.Buffered(1))


def in_bwd(dq, dz, dxbc, dk, dv, ddt, x, g, w, dres, tm=512):
    def body(dq_ref, dz_ref, dx_ref, dk_ref, dv_ref, ddt_ref, x_ref, g_ref, w_ref, dres_ref, o_ref, dw_ref, dg_ref):
        i = pl.program_id(0)

        @pl.when(i == 0)
        def _():
            dw_ref[...] = jnp.zeros_like(dw_ref)
            dg_ref[...] = jnp.zeros_like(dg_ref)

        dproj = jnp.concatenate([dq_ref[...], dz_ref[...], dx_ref[...], dk_ref[...], dv_ref[...], ddt_ref[...]],
                                axis=-1).astype(bf16)
        xv = x_ref[...]
        r = _rms(xv)
        xhat = xv * r
        gv = g_ref[...]
        h = xhat * gv
        dw_ref[...] += _bdot_tn(h, dproj)
        dh = _bdot_nt(dproj, w_ref[...])
        dg_ref[...] += jnp.sum(dh * xhat, axis=0, keepdims=True)
        o_ref[...] = dres_ref[...] + _rms_bwd(dh, xhat, r, gv)

    tok = lambda w_: pl.BlockSpec((tm, w_), lambda i: (i, 0))
    return pl.pallas_call(
        body, name="in_bwd", grid=(SEQ // tm,),
        in_specs=[tok(D_ATTN), tok(D_SSM), tok(D_CONV), tok(D_KV), tok(D_KV), tok(LANE), tok(D_MODEL),
                  _full((1, D_MODEL)), _resident((D_MODEL, PW)), tok(D_MODEL)],
        out_specs=[tok(D_MODEL), _resident((D_MODEL, PW)), _full((1, D_MODEL))],
        out_shape=[jax.ShapeDtypeStruct((SEQ, D_MODEL), f32), jax.ShapeDtypeStruct((D_MODEL, PW), f32),
                   jax.ShapeDtypeStruct((1, D_MODEL), f32)],
        compiler_params=_params("arbitrary"),
    )(dq, dz, dxbc, dk, dv, ddt, x, g, w, dres)


def _attn_softmax_t(qk, bias_t, sink, first, key_row):
    s = qk * (HEAD_DIM ** -0.5) + bias_t
    s = jnp.where(jnp.logical_and(first, key_row < BLOCK), NEG, s)
    m = jnp.maximum(jnp.max(s, axis=0, keepdims=True), sink)
    p = jnp.exp(s - m)
    psink = jnp.exp(sink - m)
    inv = 1.0 / (jnp.sum(p, axis=0, keepdims=True) + psink)
    return p * inv, psink * inv


def _rms_t(x_t):
    return lax.rsqrt(jnp.mean(x_t * x_t, axis=0, keepdims=True) + EPS)


def attn_fwd_t(proj, q_gain_col, k_gain, sinks, bias_t):
    kcol, vcol = OFF_K // D_KV, OFF_V // D_KV

    def body(q_ref, kc_ref, kp_ref, vc_ref, vp_ref, qg_ref, kg_ref, sink_ref, bias_ref, o_ref, ot_scr):
        n = pl.program_id(0)
        first = n == 0
        key_row = lax.broadcasted_iota(jnp.int32, (2 * BLOCK, BLOCK), 0)
        k2 = jnp.concatenate([kp_ref[...], kc_ref[...]], axis=0)
        v_t = jnp.concatenate([vp_ref[...], vc_ref[...]], axis=0).T
        q_t = q_ref[...].T
        qg = jnp.broadcast_to(qg_ref[...], (HEAD_DIM, BLOCK))
        kg = kg_ref[...]
        for hk in range(N_KV_HEADS):
            sl = slice(hk * HEAD_DIM, (hk + 1) * HEAD_DIM)
            kk = k2[:, sl]
            kn = (kk * _rms(kk) * kg).astype(bf16)
            vt = v_t[sl, :].astype(bf16)
            heads = range(hk * Q_PER_KV, (hk + 1) * Q_PER_KV)
            qns = []
            for h in heads:
                qh = q_t[h * HEAD_DIM:(h + 1) * HEAD_DIM, :]
                qns.append(qh * _rms_t(qh) * qg)
            scores = [_bdot(kn, qn) for qn in qns]
            for h, s in zip(heads, scores):
                p, _ = _attn_softmax_t(s, bias_ref[h], sink_ref[h], first, key_row)
                ot_scr[h * HEAD_DIM:(h + 1) * HEAD_DIM, :] = _bdot(vt, p)
        o_ref[...] = ot_scr[...].T

    prev = lambda n: jnp.maximum(n - 1, 0)
    return pl.pallas_call(
        body, name="attn_fwd", grid=(N_BLOCKS,),
        in_specs=[pl.BlockSpec((BLOCK, D_ATTN), lambda n: (n, 0)),
                  pl.BlockSpec((BLOCK, D_KV), lambda n: (n, kcol)), pl.BlockSpec((BLOCK, D_KV), lambda n: (prev(n), kcol)),
                  pl.BlockSpec((BLOCK, D_KV), lambda n: (n, vcol)), pl.BlockSpec((BLOCK, D_KV), lambda n: (prev(n), vcol)),
                  _full((HEAD_DIM, 1)), _full((1, HEAD_DIM)), pl.BlockSpec(memory_space=pltpu.SMEM),
                  _full((N_Q_HEADS, 2 * BLOCK, BLOCK))],
        out_specs=pl.BlockSpec((BLOCK, D_ATTN), lambda n: (n, 0)),
        out_shape=jax.ShapeDtypeStruct((SEQ, D_ATTN), f32),
        scratch_shapes=[pltpu.VMEM((D_ATTN, BLOCK), f32)],
        compiler_params=_params("arbitrary"),
    )(proj, proj, proj, proj, proj, q_gain_col, k_gain, sinks, bias_t)


def attn_bwd_t(proj, d_out, q_gain_col, k_gain, sinks, bias_t):
    kcol, vcol = OFF_K // D_KV, OFF_V // D_KV

    def body(q_ref, kc_ref, kp_ref, vc_ref, vp_ref, do_ref, qg_ref, kg_ref, sink_ref, bias_ref,
             dq_ref, dk_ref, dv_ref, dband_ref, dsink_ref, dqg_ref, dkg_ref, dkn_scr, dv_scr, dqt_scr, dsink_acc, dqg_acc):
        i = pl.program_id(0)
        first = i == N_BLOCKS - 1

        @pl.when(i == 0)
        def _():
            for ref in (dband_ref, dkg_ref, dkn_scr, dv_scr, dsink_acc, dqg_acc):
                ref[...] = jnp.zeros_like(ref)

        key_row = lax.broadcasted_iota(jnp.int32, (2 * BLOCK, BLOCK), 0)
        k2 = jnp.concatenate([kp_ref[...], kc_ref[...]], axis=0)
        v2 = jnp.concatenate([vp_ref[...], vc_ref[...]], axis=0)
        q_t = q_ref[...].T
        do_t = do_ref[...].T
        qg = jnp.broadcast_to(qg_ref[...], (HEAD_DIM, BLOCK))
        kg = kg_ref[...]
        scale = HEAD_DIM ** -0.5
        for hk in range(N_KV_HEADS):
            sl = slice(hk * HEAD_DIM, (hk + 1) * HEAD_DIM)
            kk = k2[:, sl]
            rk = _rms(kk)
            khat = kk * rk
            kn = (khat * kg).astype(bf16)
            vb = v2[:, sl].astype(bf16)
            dkn = jnp.zeros((2 * BLOCK, HEAD_DIM), f32)
            dvv = jnp.zeros((2 * BLOCK, HEAD_DIM), f32)
            heads = range(hk * Q_PER_KV, (hk + 1) * Q_PER_KV)
            rqs, qhats, qns, d_os = [], [], [], []
            for h in heads:
                hs = slice(h * HEAD_DIM, (h + 1) * HEAD_DIM)
                qh = q_t[hs, :]
                rqs.append(_rms_t(qh))
                qhats.append(qh * rqs[-1])
                qns.append((qhats[-1] * qg).astype(bf16))
                d_os.append(do_t[hs, :].astype(bf16))
            scores = [_bdot(kn, qn) for qn in qns]
            dps = [_bdot(vb, d_o) for d_o in d_os]
            ps, dss = [], []
            for h, s, dp in zip(heads, scores, dps):
                p, psink = _attn_softmax_t(s, bias_ref[h], sink_ref[h], first, key_row)
                delta = jnp.sum(p * dp, axis=0, keepdims=True)
                ds = p * (dp - delta)
                dband_ref[h] += ds
                dsink_acc[h:h + 1, :] += -(psink * delta)
                ps.append(p.astype(bf16))
                dss.append(ds.astype(bf16))
            dqns = [_bdot_tn(kn, ds) * scale for ds in dss]
            for ds, qn, p, d_o in zip(dss, qns, ps, d_os):
                dkn = dkn + _bdot_nt(ds, qn) * scale
                dvv = dvv + _bdot_nt(p, d_o)
            for h, dqn, rq, qhat in zip(heads, dqns, rqs, qhats):
                dqg_acc[...] += dqn * qhat
                t = dqn * qg
                dqt_scr[h * HEAD_DIM:(h + 1) * HEAD_DIM, :] = rq * (t - qhat * jnp.mean(t * qhat, axis=0, keepdims=True))
            dkn_cur = dkn[BLOCK:] + dkn_scr[:, sl]
            dkn_scr[:, sl] = dkn[:BLOCK]
            khat_c, rk_c = khat[BLOCK:], rk[BLOCK:]
            dkg_ref[...] += jnp.sum(dkn_cur * khat_c, axis=0, keepdims=True)
            dk_ref[:, sl] = _rms_bwd(dkn_cur, khat_c, rk_c, kg)
            dv_ref[:, sl] = dvv[BLOCK:] + dv_scr[:, sl]
            dv_scr[:, sl] = dvv[:BLOCK]
        dq_ref[...] = dqt_scr[...].T

        @pl.when(i == N_BLOCKS - 1)
        def _():
            dsink_ref[...] = jnp.sum(dsink_acc[...], axis=1, keepdims=True)
            dqg_ref[...] = jnp.sum(dqg_acc[...], axis=1, keepdims=True)

    blk = lambda i: N_BLOCKS - 1 - i
    prev = lambda i: jnp.maximum(N_BLOCKS - 2 - i, 0)
    return pl.pallas_call(
        body, name="attn_bwd", grid=(N_BLOCKS,),
        in_specs=[pl.BlockSpec((BLOCK, D_ATTN), lambda i: (blk(i), 0)),
                  pl.BlockSpec((BLOCK, D_KV), lambda i: (blk(i), kcol)), pl.BlockSpec((BLOCK, D_KV), lambda i: (prev(i), kcol)),
                  pl.BlockSpec((BLOCK, D_KV), lambda i: (blk(i), vcol)), pl.BlockSpec((BLOCK, D_KV), lambda i: (prev(i), vcol)),
                  pl.BlockSpec((BLOCK, D_ATTN), lambda i: (blk(i), 0)),
                  _full((HEAD_DIM, 1)), _full((1, HEAD_DIM)), pl.BlockSpec(memory_space=pltpu.SMEM),
                  _full((N_Q_HEADS, 2 * BLOCK, BLOCK))],
        out_specs=[pl.BlockSpec((BLOCK, D_ATTN), lambda i: (blk(i), 0)), pl.BlockSpec((BLOCK, D_KV), lambda i: (blk(i), 0)),
                   pl.BlockSpec((BLOCK, D_KV), lambda i: (blk(i), 0)), _full((N_Q_HEADS, 2 * BLOCK, BLOCK)),
                   _full((N_Q_HEADS, 1)), _full((HEAD_DIM, 1)), _full((1, HEAD_DIM))],
        out_shape=[jax.ShapeDtypeStruct((SEQ, D_ATTN), f32), jax.ShapeDtypeStruct((SEQ, D_KV), f32),
                   jax.ShapeDtypeStruct((SEQ, D_KV), f32), jax.ShapeDtypeStruct((N_Q_HEADS, 2 * BLOCK, BLOCK), f32),
                   jax.ShapeDtypeStruct((N_Q_HEADS, 1), f32), jax.ShapeDtypeStruct((HEAD_DIM, 1), f32),
                   jax.ShapeDtypeStruct((1, HEAD_DIM), f32)],
        scratch_shapes=[pltpu.VMEM((BLOCK, D_KV), f32), pltpu.VMEM((BLOCK, D_KV), f32), pltpu.VMEM((D_ATTN, BLOCK), f32),
                        pltpu.VMEM((N_Q_HEADS, BLOCK), f32), pltpu.VMEM((HEAD_DIM, BLOCK), f32)],
        compiler_params=_params("arbitrary"),
    )(proj, proj, proj, proj, proj, d_out, q_gain_col, k_gain, sinks, bias_t)


SUBLANES = 8


def _shift_down(u, s, row8):
    if s == 0:
        return u
    r = pltpu.roll(u, s, 0)
    return jnp.concatenate([jnp.where(row8 >= s, r[:SUBLANES], 0.0), r[SUBLANES:]], axis=0)


def _shift_up(u, s, row8):
    if s == 0:
        return u
    r = pltpu.roll(u, SEQ - s, 0)
    return jnp.concatenate([r[:-SUBLANES], jnp.where(row8 < SUBLANES - s, r[-SUBLANES:], 0.0)], axis=0)


def conv_fwd(proj, conv_w, conv_b):
    xcol = OFF_X // LANE

    def body(u_ref, w_ref, b_ref, o_ref):
        u = u_ref[...]
        row = lax.broadcasted_iota(jnp.int32, (SUBLANES, LANE), 0)
        pre = b_ref[...] + jnp.zeros_like(u)
        for k in range(CONV_WIDTH):
            pre = pre + w_ref[k:k + 1, :] * _shift_down(u, CONV_WIDTH - 1 - k, row)
        o_ref[...] = pre * _sigmoid(pre)

    return pl.pallas_call(
        body, name="conv_fwd", grid=(D_CONV // LANE,),
        in_specs=[pl.BlockSpec((SEQ, LANE), lambda j: (0, xcol + j)), pl.BlockSpec((CONV_WIDTH, LANE), lambda j: (0, j)),
                  pl.BlockSpec((1, LANE), lambda j: (0, j))],
        out_specs=pl.BlockSpec((SEQ, LANE), lambda j: (0, j)),
        out_shape=jax.ShapeDtypeStruct((SEQ, D_CONV), f32),
        compiler_params=_params("arbitrary"),
    )(proj, conv_w, conv_b)


def conv_bwd(proj, d_act, conv_w, conv_b):
    xcol = OFF_X // LANE

    def body(u_ref, da_ref, w_ref, b_ref, du_ref, dw_ref, db_ref):
        u = u_ref[...]
        row = lax.broadcasted_iota(jnp.int32, (SUBLANES, LANE), 0)
        shifted = [_shift_down(u, CONV_WIDTH - 1 - k, row) for k in range(CONV_WIDTH)]
        pre = b_ref[...] + jnp.zeros_like(u)
        for k in range(CONV_WIDTH):
            pre = pre + w_ref[k:k + 1, :] * shifted[k]
        sg = _sigmoid(pre)
        dpre = da_ref[...] * (sg * (1.0 + pre * (1.0 - sg)))
        db_ref[...] = jnp.sum(dpre, axis=0, keepdims=True)
        du = jnp.zeros_like(u)
        for k in range(CONV_WIDTH):
            dw_ref[k:k + 1, :] = jnp.sum(dpre * shifted[k], axis=0, keepdims=True)
            du = du + w_ref[k:k + 1, :] * _shift_up(dpre, CONV_WIDTH - 1 - k, row)
        du_ref[...] = du

    return pl.pallas_call(
        body, name="conv_bwd", grid=(D_CONV // LANE,),
        in_specs=[pl.BlockSpec((SEQ, LANE), lambda j: (0, xcol + j)), pl.BlockSpec((SEQ, LANE), lambda j: (0, j)),
                  pl.BlockSpec((CONV_WIDTH, LANE), lambda j: (0, j)), pl.BlockSpec((1, LANE), lambda j: (0, j))],
        out_specs=[pl.BlockSpec((SEQ, LANE), lambda j: (0, j)), pl.BlockSpec((CONV_WIDTH, LANE), lambda j: (0, j)),
                   pl.BlockSpec((1, LANE), lambda j: (0, j))],
        out_shape=[jax.ShapeDtypeStruct((SEQ, D_CONV), f32), jax.ShapeDtypeStruct((CONV_WIDTH, D_CONV), f32),
                   jax.ShapeDtypeStruct((1, D_CONV), f32)],
        compiler_params=_params("arbitrary"),
    )(proj, d_act, conv_w, conv_b)


def _ssd_chunk_common(dt_raw, dtb, alog):
    row = lax.broadcasted_iota(jnp.int32, (CHUNK, CHUNK), 0)
    col = lax.broadcasted_iota(jnp.int32, (CHUNK, CHUNK), 1)
    tri = (row >= col).astype(f32)
    strict = (row > col).astype(f32)
    dtp = _softplus(dt_raw + dtb)
    a_row = -jnp.exp(alog)
    d_a = dtp * a_row
    cs = _hdot(tri, d_a)
    cs_last = cs[CHUNK - 1:CHUNK, :]
    return row, col, dtp, a_row, cs, cs.T, cs_last


def _seg_decay(cs, cs_t, hd, row, col):
    seg = cs[:, hd:hd + 1] - cs_t[hd:hd + 1, :]
    return jnp.where(row >= col, jnp.exp(seg), 0.0)


GROUP_W = HEADS_PER_GROUP * SSM_HEAD_DIM


def _group_indicator(g):
    j = lax.broadcasted_iota(jnp.int32, (GROUP_W, LANE), 0)
    lane = lax.broadcasted_iota(jnp.int32, (GROUP_W, LANE), 1)
    return (lane == g * HEADS_PER_GROUP + j // SSM_HEAD_DIM).astype(bf16)


def _bf16_pieces(a, n):
    pieces = []
    for _ in range(n):
        p = a.astype(bf16)
        pieces.append(p)
        a = a - p.astype(f32)
    return pieces


def _head_spread(a, ind):
    return sum(lax.dot_general(p, ind, (((1,), (1,)), ((), ())), preferred_element_type=f32) for p in _bf16_pieces(a, 3))


def _head_sums(a, ind):
    return sum(jnp.dot(p, ind, preferred_element_type=f32) for p in _bf16_pieces(a, 2))


def ssd_fwd_g(act, proj, dt_bias, a_log, d_skip, norm_g):
    zcol, dtcol = OFF_Z // D_SSM, OFF_DT // LANE

    def body(act_ref, z_ref, dt_ref, dtb_ref, alog_ref, dsk_ref, ng_ref, out_ref, ypre_ref, st_ref, state):
        c = pl.program_id(0)

        @pl.when(c == 0)
        def _():
            state[...] = jnp.zeros_like(state)

        row, col, dtp, a_row, cs, cs_t, cs_last = _ssd_chunk_common(dt_ref[...], dtb_ref[...], alog_ref[...])
        e_cs = jnp.exp(cs)
        dte = jnp.exp(cs_last - cs)
        rows8 = jnp.concatenate([jnp.exp(cs_last), dsk_ref[...], jnp.zeros((6, LANE), f32)], axis=0)
        z = z_ref[...]
        sz = z * _sigmoid(z)
        ng = ng_ref[...]
        for g in range(SSM_GROUPS):
            gs = slice(g * GROUP_W, (g + 1) * GROUP_W)
            ind = _group_indicator(g)
            xg = act_ref[:, gs]
            bg = act_ref[:, D_SSM + g * SSM_STATE:D_SSM + (g + 1) * SSM_STATE]
            cg = act_ref[:, D_SSM + D_BC + g * SSM_STATE:D_SSM + D_BC + (g + 1) * SSM_STATE]
            dt_e, e_e, dte_e = _head_spread(dtp, ind), _head_spread(e_cs, ind), _head_spread(dte, ind)
            rows_e = _head_spread(rows8, ind)
            ecl_e, dsk_e = rows_e[0:1], rows_e[1:2]
            xdt = xg * dt_e
            prev = state[g]
            st_ref[0, g] = prev
            cb = _bdot_nt(cg, bg)
            goff = _bdot(cg, prev)
            snew = _bdot_tn(bg, xdt * dte_e)
            heads = range(g * HEADS_PER_GROUP, (g + 1) * HEADS_PER_GROUP)
            ms = [cb * _seg_decay(cs, cs_t, hd, row, col) for hd in heads]
            yd = [_bdot(m, xdt[:, r * SSM_HEAD_DIM:(r + 1) * SSM_HEAD_DIM]) for r, m in enumerate(ms)]
            y = jnp.concatenate(yd, axis=1) + e_e * goff + xg * dsk_e
            state[g] = prev * ecl_e + snew
            ypre_ref[:, gs] = y
            part = y * sz[:, gs]
            out_ref[:, gs] = part * _rms(part) * ng[:, gs]

    return pl.pallas_call(
        body, name="ssd_fwd", grid=(N_CHUNKS,),
        in_specs=[pl.BlockSpec((CHUNK, D_CONV), lambda c: (c, 0)), pl.BlockSpec((CHUNK, D_SSM), lambda c: (c, zcol)),
                  pl.BlockSpec((CHUNK, LANE), lambda c: (c, dtcol)), _full((1, LANE)), _full((1, LANE)), _full((1, LANE)),
                  _full((1, D_SSM))],
        out_specs=[pl.BlockSpec((CHUNK, D_SSM), lambda c: (c, 0)), pl.BlockSpec((CHUNK, D_SSM), lambda c: (c, 0)),
                   pl.BlockSpec((1, SSM_GROUPS, SSM_STATE, GROUP_W), lambda c: (c, 0, 0, 0))],
        out_shape=[jax.ShapeDtypeStruct((SEQ, D_SSM), f32), jax.ShapeDtypeStruct((SEQ, D_SSM), f32),
                   jax.ShapeDtypeStruct((N_CHUNKS, SSM_GROUPS, SSM_STATE, GROUP_W), f32)],
        scratch_shapes=[pltpu.VMEM((SSM_GROUPS, SSM_STATE, GROUP_W), f32)],
        compiler_params=_params("arbitrary"),
    )(act, proj, proj, dt_bias, a_log, d_skip, norm_g)


def ssd_bwd_g(act, proj, ypre, states, d_out, dt_bias, a_log, d_skip, norm_g):
    zcol, dtcol = OFF_Z // D_SSM, OFF_DT // LANE

    def body(act_ref, z_ref, dt_ref, ypre_ref, st_ref, do_ref, dtb_ref, alog_ref, dsk_ref, ng_ref,
             dact_ref, ddt_ref, dz_ref, dng_ref, dpar_ref, dstate):
        i = pl.program_id(0)

        @pl.when(i == 0)
        def _():
            for ref in (dng_ref, dpar_ref, dstate):
                ref[...] = jnp.zeros_like(ref)

        row, col, dtp, a_row, cs, cs_t, cs_last = _ssd_chunk_common(dt_ref[...], dtb_ref[...], alog_ref[...])
        upper = (row <= col).astype(f32)
        lane = lax.broadcasted_iota(jnp.int32, (CHUNK, LANE), 1)
        rowl = lax.broadcasted_iota(jnp.int32, (CHUNK, LANE), 0)
        e_cs = jnp.exp(cs)
        dte = jnp.exp(cs_last - cs)
        ecl = jnp.exp(cs_last)
        rows8 = jnp.concatenate([ecl, dsk_ref[...], jnp.zeros((6, LANE), f32)], axis=0)
        z = z_ref[...]
        sgz = _sigmoid(z)
        sz = z * sgz
        ng = ng_ref[...]
        ddt_mat = jnp.zeros((CHUNK, LANE), f32)
        dcs_mat = jnp.zeros((CHUNK, LANE), f32)
        dcs_t = jnp.zeros((LANE, CHUNK), f32)
        dcsl_row = jnp.zeros((1, LANE), f32)
        dd_row = jnp.zeros((1, LANE), f32)
        for g in range(SSM_GROUPS):
            gs = slice(g * GROUP_W, (g + 1) * GROUP_W)
            bsl = slice(D_SSM + g * SSM_STATE, D_SSM + (g + 1) * SSM_STATE)
            csl = slice(D_SSM + D_BC + g * SSM_STATE, D_SSM + D_BC + (g + 1) * SSM_STATE)
            ind = _group_indicator(g)
            y = ypre_ref[:, gs]
            part = y * sz[:, gs]
            r = _rms(part)
            yhat = part * r
            d_o = do_ref[:, gs]
            dng_ref[:, gs] += jnp.sum(d_o * yhat, axis=0, keepdims=True)
            dyz = _rms_bwd(d_o, yhat, r, ng[:, gs])
            dy = dyz * sz[:, gs]
            dz_ref[:, gs] = dyz * y * (sgz[:, gs] * (1.0 + z[:, gs] * (1.0 - sgz[:, gs])))

            xg = act_ref[:, gs]
            bg = act_ref[:, bsl]
            cg = act_ref[:, csl]
            dt_e, e_e, dte_e = _head_spread(dtp, ind), _head_spread(e_cs, ind), _head_spread(dte, ind)
            rows_e = _head_spread(rows8, ind)
            ecl_e, dsk_e = rows_e[0:1], rows_e[1:2]
            xdt = xg * dt_e
            prev = st_ref[0, g]
            dh = dstate[g]
            heads = range(g * HEADS_PER_GROUP, (g + 1) * HEADS_PER_GROUP)
            hsl = [slice(r_ * SSM_HEAD_DIM, (r_ + 1) * SSM_HEAD_DIM) for r_ in range(HEADS_PER_GROUP)]
            cb = _bdot_nt(cg, bg)
            lms = [_seg_decay(cs, cs_t, hd, row, col) for hd in heads]
            ms = [cb * lm for lm in lms]
            gmat = _bdot(cg, prev)
            dgm = dy * e_e
            dcg = _bdot_nt(dgm, prev)
            dprev = _bdot_tn(cg, dgm)
            dbg = _bdot_nt(xdt * dte_e, dh)
            dw = _bdot(bg, dh)
            dms = [_bdot_nt(dy[:, s_], xdt[:, s_]) for s_ in hsl]
            dxdts = [_bdot_tn(m, dy[:, s_]) for m, s_ in zip(ms, hsl)]
            dxdt = jnp.concatenate(dxdts, axis=1) + dw * dte_e
            dact_ref[:, gs] = dy * dsk_e + dxdt * dt_e
            dstate[g] = dprev + dh * ecl_e
            dcb = jnp.zeros((CHUNK, CHUNK), f32)
            for hd, dm, lm, m in zip(heads, dms, lms, ms):
                dcb = dcb + dm * lm
                dseg = dm * m
                dcs_mat = dcs_mat + jnp.where(lane == hd, jnp.sum(dseg, axis=1, keepdims=True), 0.0)
                dcs_t = jnp.where(row == hd, jnp.sum(dseg, axis=0, keepdims=True), dcs_t)
            dact_ref[:, bsl] = dbg + _bdot_tn(dcb, cg)
            dact_ref[:, csl] = dcg + _bdot(dcb, bg)
            ddte = _head_sums(dw * xdt, ind) * dte
            dcs_mat = dcs_mat + _head_sums(dy * gmat, ind) * e_cs - ddte
            ddt_mat = ddt_mat + _head_sums(dxdt * xg, ind)
            dcsl_row = (dcsl_row + jnp.sum(ddte, axis=0, keepdims=True)
                        + jnp.sum(_head_sums(dh * prev, ind), axis=0, keepdims=True) * ecl)
            dd_row = dd_row + jnp.sum(_head_sums(dy * xg, ind), axis=0, keepdims=True)
        dcs_mat = dcs_mat - dcs_t.T + jnp.where(rowl == CHUNK - 1, dcsl_row, 0.0)
        dda = _hdot(upper, dcs_mat)
        ddt_mat = ddt_mat + dda * a_row
        da_row = jnp.sum(dda * dtp, axis=0, keepdims=True)
        ddt_raw = ddt_mat * _sigmoid(dt_ref[...] + dtb_ref[...])
        ddt_ref[...] = ddt_raw
        dpar_ref[0:1, :] += jnp.sum(ddt_raw, axis=0, keepdims=True)
        dpar_ref[1:2, :] += da_row * a_row
        dpar_ref[2:3, :] += dd_row

    blk = lambda i: N_CHUNKS - 1 - i
    return pl.pallas_call(
        body, name="ssd_bwd", grid=(N_CHUNKS,),
        in_specs=[pl.BlockSpec((CHUNK, D_CONV), lambda i: (blk(i), 0)), pl.BlockSpec((CHUNK, D_SSM), lambda i: (blk(i), zcol)),
                  pl.BlockSpec((CHUNK, LANE), lambda i: (blk(i), dtcol)), pl.BlockSpec((CHUNK, D_SSM), lambda i: (blk(i), 0)),
                  pl.BlockSpec((1, SSM_GROUPS, SSM_STATE, GROUP_W), lambda i: (blk(i), 0, 0, 0)),
                  pl.BlockSpec((CHUNK, D_SSM), lambda i: (blk(i), 0)),
                  _full((1, LANE)), _full((1, LANE)), _full((1, LANE)), _full((1, D_SSM))],
        out_specs=[pl.BlockSpec((CHUNK, D_CONV), lambda i: (blk(i), 0)), pl.BlockSpec((CHUNK, LANE), lambda i: (blk(i), 0)),
                   pl.BlockSpec((CHUNK, D_SSM), lambda i: (blk(i), 0)), _full((1, D_SSM)), _full((8, LANE))],
        out_shape=[jax.ShapeDtypeStruct((SEQ, D_CONV), f32), jax.ShapeDtypeStruct((SEQ, LANE), f32),
                   jax.ShapeDtypeStruct((SEQ, D_SSM), f32), jax.ShapeDtypeStruct((1, D_SSM), f32),
                   jax.ShapeDtypeStruct((8, LANE), f32)],
        scratch_shapes=[pltpu.VMEM((SSM_GROUPS, SSM_STATE, GROUP_W), f32)],
        compiler_params=_params("arbitrary"),
    )(act, proj, proj, ypre, states, d_out, dt_bias, a_log, d_skip, norm_g)


def out_fwd(x, attn, ssm, w_out, tm=512):
    def body(x_ref, a_ref, s_ref, w_ref, o_ref):
        o_ref[...] = x_ref[...] + _bdot(a_ref[...], w_ref[:D_ATTN, :]) + _bdot(s_ref[...], w_ref[D_ATTN:, :])

    tok = lambda w_: pl.BlockSpec((tm, w_), lambda i: (i, 0))
    return pl.pallas_call(
        body, name="out_fwd", grid=(SEQ // tm,),
        in_specs=[tok(D_MODEL), tok(D_ATTN), tok(D_SSM), _full((D_MODEL, D_MODEL))],
        out_specs=tok(D_MODEL), out_shape=jax.ShapeDtypeStruct((SEQ, D_MODEL), f32),
        compiler_params=_params("arbitrary"),
    )(x, attn, ssm, w_out)


def out_bwd(dx1, attn, ssm, w_out, tm=512):
    nt = SEQ // tm

    def body(d_ref, a_ref, s_ref, w_ref, da_ref, ds_ref, dw16_ref, dw_ref):
        i = pl.program_id(0)

        @pl.when(i == 0)
        def _():
            dw_ref[...] = jnp.zeros_like(dw_ref)

        d = d_ref[...].astype(bf16)
        dcat = _bdot_nt(d, w_ref[...])
        da_ref[...] = dcat[:, :D_ATTN]
        ds_ref[...] = dcat[:, D_ATTN:]
        dw_ref[:D_ATTN, :] += _bdot_tn(a_ref[...], d)
        dw_ref[D_ATTN:, :] += _bdot_tn(s_ref[...], d)

        @pl.when(i == nt - 1)
        def _():
            dw16_ref[...] = dw_ref[...].astype(bf16)

    tok = lambda w_: pl.BlockSpec((tm, w_), lambda i: (i, 0))
    return pl.pallas_call(
        body, name="out_bwd", grid=(nt,),
        in_specs=[tok(D_MODEL), tok(D_ATTN), tok(D_SSM), _resident((D_MODEL, D_MODEL))],
        out_specs=[tok(D_ATTN), tok(D_SSM), _resident((D_MODEL, D_MODEL))],
        out_shape=[jax.ShapeDtypeStruct((SEQ, D_ATTN), f32), jax.ShapeDtypeStruct((SEQ, D_SSM), f32),
                   jax.ShapeDtypeStruct((D_MODEL, D_MODEL), bf16)],
        scratch_shapes=[pltpu.VMEM((D_MODEL, D_MODEL), f32)],
        compiler_params=_params("arbitrary"),
    )(dx1, attn, ssm, w_out)


MLP_SUB = 256


def mlp_fwd(x1, g, w_up, w_down, tm=1024):
    def body(x_ref, g_ref, wu_ref, wd_ref, o_ref, u_ref, h_scr):
        j = pl.program_id(1)

        @pl.when(j == 0)
        def _():
            xv = x_ref[...]
            h_scr[...] = (xv * _rms(xv) * g_ref[...]).astype(bf16)
            o_ref[...] = xv

        for r in range(tm // MLP_SUB):
            rows = slice(r * MLP_SUB, (r + 1) * MLP_SUB)
            u = jnp.dot(h_scr[rows, :], wu_ref[...], preferred_element_type=f32)
            u_ref[rows, :] = u
            a = jnp.square(jnp.maximum(u, 0.0))
            o_ref[rows, :] += _bdot(a, wd_ref[...])

    return pl.pallas_call(
        body, name="mlp_fwd", grid=(SEQ // tm, N_CHIPS),
        in_specs=[pl.BlockSpec((tm, D_MODEL), lambda i, j: (i, 0)), _full((1, D_MODEL)),
                  pl.BlockSpec((None, D_MODEL, FF_TILE), lambda i, j: (j, 0, 0)),
                  pl.BlockSpec((None, FF_TILE, D_MODEL), lambda i, j: (j, 0, 0))],
        out_specs=[pl.BlockSpec((tm, D_MODEL), lambda i, j: (i, 0)), pl.BlockSpec((tm, FF_TILE), lambda i, j: (i, j))],
        out_shape=[jax.ShapeDtypeStruct((SEQ, D_MODEL), f32), jax.ShapeDtypeStruct((SEQ, D_FF), f32)],
        scratch_shapes=[pltpu.VMEM((tm, D_MODEL), bf16)],
        compiler_params=_params("arbitrary", "arbitrary"),
    )(x1, g, w_up, w_down)


def mlp_bwd_data(dx2, u, x1, g, w_up, w_down, tm=1024):
    def body(d_ref, u_ref, x_ref, g_ref, wu_ref, wd_ref, dx_ref, du_ref, dg_ref, dh_scr):
        i, j = pl.program_id(0), pl.program_id(1)

        @pl.when(jnp.logical_and(i == 0, j == 0))
        def _():
            dg_ref[...] = jnp.zeros_like(dg_ref)

        @pl.when(j == 0)
        def _():
            dh_scr[...] = jnp.zeros_like(dh_scr)

        for r in range(tm // MLP_SUB):
            rows = slice(r * MLP_SUB, (r + 1) * MLP_SUB)
            da = _bdot_nt(d_ref[rows, :], wd_ref[...])
            du = (da * (2.0 * jnp.maximum(u_ref[rows, :], 0.0))).astype(bf16)
            du_ref[rows, :] = du
            dh_scr[rows, :] += _bdot_nt(du, wu_ref[...])

        @pl.when(j == N_CHIPS - 1)
        def _():
            xv = x_ref[...]
            r = _rms(xv)
            xhat = xv * r
            dh = dh_scr[...]
            dg_ref[...] += jnp.sum(dh * xhat, axis=0, keepdims=True)
            dx_ref[...] = d_ref[...] + _rms_bwd(dh, xhat, r, g_ref[...])

    return pl.pallas_call(
        body, name="mlp_bwd_data", grid=(SEQ // tm, N_CHIPS),
        in_specs=[pl.BlockSpec((tm, D_MODEL), lambda i, j: (i, 0)), pl.BlockSpec((tm, FF_TILE), lambda i, j: (i, j)),
                  pl.BlockSpec((tm, D_MODEL), lambda i, j: (i, 0)), _full((1, D_MODEL)),
                  pl.BlockSpec((None, D_MODEL, FF_TILE), lambda i, j: (j, 0, 0)),
                  pl.BlockSpec((None, FF_TILE, D_MODEL), lambda i, j: (j, 0, 0))],
        out_specs=[pl.BlockSpec((tm, D_MODEL), lambda i, j: (i, 0)), pl.BlockSpec((tm, FF_TILE), lambda i, j: (i, j)),
                   _full((1, D_MODEL))],
        out_shape=[jax.ShapeDtypeStruct((SEQ, D_MODEL), f32), jax.ShapeDtypeStruct((SEQ, D_FF), bf16),
                   jax.ShapeDtypeStruct((1, D_MODEL), f32)],
        scratch_shapes=[pltpu.VMEM((tm, D_MODEL), f32)],
        compiler_params=_params("arbitrary", "arbitrary"),
    )(dx2, u, x1, g, w_up, w_down)


def mlp_bwd_weights(dx2, u, du, x1, g, tm=512):
    nt = SEQ // tm

    def body(d_ref, u_ref, du_ref, x_ref, g_ref, dwu16_ref, dwd16_ref, h_scr, d_scr, dwu_ref, dwd_ref):
        j, i = pl.program_id(0), pl.program_id(1)

        @pl.when(j == 0)
        def _():
            xv = x_ref[...]
            h_scr[i] = (xv * _rms(xv) * g_ref[...]).T.astype(bf16)
            d_scr[i] = d_ref[...].astype(bf16)

        @pl.when(i == 0)
        def _():
            dwu_ref[...] = jnp.zeros_like(dwu_ref)
            dwd_ref[...] = jnp.zeros_like(dwd_ref)

        dwu_ref[...] += jnp.dot(h_scr[i], du_ref[...], preferred_element_type=f32)
        a = jnp.square(jnp.maximum(u_ref[...], 0.0))
        dwd_ref[...] += _bdot_tn(a, d_scr[i])

        @pl.when(i == nt - 1)
        def _():
            dwu16_ref[...] = dwu_ref[...].astype(bf16)
            dwd16_ref[...] = dwd_ref[...].astype(bf16)

    up = pl.BlockSpec((None, D_MODEL, FF_TILE), lambda j, i: (j, 0, 0))
    down = pl.BlockSpec((None, FF_TILE, D_MODEL), lambda j, i: (j, 0, 0))
    first_pass = pl.BlockSpec((tm, D_MODEL), lambda j, i: (jnp.where(j == 0, i, nt - 1), 0))
    return pl.pallas_call(
        body, name="mlp_bwd_weights", grid=(N_CHIPS, nt),
        in_specs=[first_pass, pl.BlockSpec((tm, FF_TILE), lambda j, i: (i, j)),
                  pl.BlockSpec((tm, FF_TILE), lambda j, i: (i, j)), first_pass, _full((1, D_MODEL))],
        out_specs=[up, down],
        out_shape=[jax.ShapeDtypeStruct((N_CHIPS, D_MODEL, FF_TILE), bf16), jax.ShapeDtypeStruct((N_CHIPS, FF_TILE, D_MODEL), bf16)],
        scratch_shapes=[pltpu.VMEM((nt, D_MODEL, tm), bf16), pltpu.VMEM((nt, tm, D_MODEL), bf16),
                        pltpu.VMEM((D_MODEL, FF_TILE), f32), pltpu.VMEM((FF_TILE, D_MODEL), f32)],
        compiler_params=_params("arbitrary", "arbitrary"),
    )(dx2, u, du, x1, g)


def loss_head(y, target, tm=512):
    def body(y_ref, t_ref, dy_ref, l_ref):
        @pl.when(pl.program_id(0) == 0)
        def _():
            l_ref[...] = jnp.zeros_like(l_ref)

        d = y_ref[...] - t_ref[...]
        dy_ref[...] = d * (1.0 / D_MODEL)
        part = jnp.sum(jnp.mean(d * d, axis=-1, keepdims=True), axis=0, keepdims=True)
        l_ref[...] += 0.5 * part

    tok = pl.BlockSpec((tm, D_MODEL), lambda i: (i, 0))
    return pl.pallas_call(
        body, name="loss_head", grid=(SEQ // tm,), in_specs=[tok, tok], out_specs=[tok, _full((1, 1))],
        out_shape=[jax.ShapeDtypeStruct((SEQ, D_MODEL), f32), jax.ShapeDtypeStruct((1, 1), f32)],
        compiler_params=_params("arbitrary"),
    )(y, target)


def _pad_lane(v):
    return jnp.pad(v, (0, LANE - v.shape[0]))[None, :]


def local_step(x, target, w, prov):
    bucket = jnp.asarray(_bucket_table().T)
    bias = bias_build(w["rel_bias"], bucket)
    saved = []
    for l in range(DEPTH):
        g_mix = w["mix_norm_g"][l][None, :] + prov.stage(("begin", l), x)
        w_in = prov.w_in(l, x)
        proj = in_fwd(x, g_mix, w_in)
        conv_b = w["conv_b"][l][None, :]
        act = conv_fwd(proj, w["conv_w"][l], conv_b)
        dtb = _pad_lane(w["dt_bias"][l]) + prov.stage(("mid", l), act)
        alog, dsk = _pad_lane(w["a_log"][l]), _pad_lane(w["d_skip"][l])
        ng = w["ssm_norm_g"][l][None, :]
        ssm, ypre, states = ssd_fwd_g(act, proj, dtb, alog, dsk, ng)
        qg, kg = w["q_gain"][l][:, None] + 0.0 * ssm[:1, :1], w["k_gain"][l][None, :]
        attn = attn_fwd_t(proj, qg, kg, w["sinks"][l], bias)
        tok = prov.stage(("pre_out", l), attn)
        w_out = prov.w_out(l, attn) + jnp.asarray(tok, bf16)
        x1 = out_fwd(x, attn, ssm, w_out)
        g_mlp = w["mlp_norm_g"][l][None, :] + prov.stage(("pre_mlp", l), x1)
        w_up, w_down = prov.mlp(l, x1)
        x2, u = mlp_fwd(x1, g_mlp, w_up, w_down)
        saved.append(dict(x=x, proj=proj, attn=attn, act=act, ssm=ssm, ypre=ypre, states=states, x1=x1, u=u,
                          g_mix=g_mix, qg=qg, kg=kg, conv_b=conv_b, dtb=dtb, alog=alog, dsk=dsk, ng=ng, g_mlp=g_mlp,
                          w_in=w_in, w_out=w_out, w_up=w_up, w_down=w_down))
        x = x2
    dx, loss = loss_head(x, target)
    grads = [None] * DEPTH
    dbands = [None] * DEPTH
    tok = 0.0
    for l in reversed(range(DEPTH)):
        s = saved[l]
        g_mlp = s["g_mlp"] + tok
        dx1, du, dg_mlp = mlp_bwd_data(dx, s["u"], s["x1"], g_mlp, s["w_up"], s["w_down"])
        dw_up, dw_down = mlp_bwd_weights(dx, s["u"], du, s["x1"], g_mlp)
        tok = prov.grads(("mlp", l), dict(w_up=dw_up, w_down=dw_down), dx1)
        dattn, dssm, dw_out = out_bwd(dx1, s["attn"], s["ssm"], s["w_out"])
        dact, ddt, dz, dng, dpar = ssd_bwd_g(s["act"], s["proj"], s["ypre"], s["states"], dssm, s["dtb"] + tok, s["alog"],
                                           s["dsk"], s["ng"])
        conv_b = s["conv_b"] + prov.stage(("bwd_mid", l), dact)
        dxbc, dconv_w, dconv_b = conv_bwd(s["proj"], dact, w["conv_w"][l], conv_b)
        dq, dk, dv, dband, dsink, dqg, dkg = attn_bwd_t(s["proj"], dattn, s["qg"], s["kg"], w["sinks"][l], bias)
        dbands[l] = dband
        g_mix = s["g_mix"]
        if l == 0:
            d_rel = bias_bwd(dbands[0], dbands[1], bucket)
            g_mix = g_mix + 0.0 * d_rel[:1, :1]
        dx, dw_in, dg_mix = in_bwd(dq, dz, dxbc, dk, dv, ddt, s["x"], g_mix, s["w_in"], dx1)
        tok = prov.grads(("mix", l), dict(w_in=split_w_in_grad(dw_in), w_out=dw_out), dx)
        grads[l] = dict(mix_norm_g=dg_mix[0], q_gain=dqg[:, 0], k_gain=dkg[0], sinks=dsink[:, 0],
                        conv_w=dconv_w, conv_b=dconv_b[0], dt_bias=dpar[0, :SSM_HEADS], a_log=dpar[1, :SSM_HEADS],
                        d_skip=dpar[2, :SSM_HEADS], ssm_norm_g=dng[0], mlp_norm_g=dg_mlp[0])
    out = {k: jnp.stack([grads[l][k] for l in range(DEPTH)]) for k in grads[0]}
    out["rel_bias"] = d_rel[:, :N_Q_HEADS]
    return loss, dx, out, tok


MESH = pl.DeviceIdType.MESH
HBM = pl.BlockSpec(memory_space=pltpu.HBM)
N_DEVICES = 8


def _coords():
    return lax.axis_index("x"), lax.axis_index("y"), lax.axis_index("c")


def _peer_chips(x, y):
    return [(1 - x, y), (x, 1 - y), (1 - x, 1 - y)]


def _remote(src, dst, send_sem, recv_sem, device):
    return pltpu.make_async_remote_copy(src_ref=src, dst_ref=dst, send_sem=send_sem, recv_sem=recv_sem,
                                        device_id=device, device_id_type=MESH)


SEM = pl.BlockSpec(memory_space=pltpu.SEMAPHORE)
ANY = pl.BlockSpec(memory_space=pl.ANY)
DATAFLOW = pltpu.SideEffectType.DATAFLOW_SIDE_EFFECTING


def _gather_copies(kind, src_refs, land_refs, ssem, rsem):
    x, y, c = _coords()
    k_me = 2 * x + y
    n = len(land_refs)
    cps = []
    for p, land in enumerate(land_refs):
        hr = land.shape[1] // 2
        rows = pl.ds(c * hr, hr)
        for j, chip in enumerate(_peer_chips(x, y)):
            i = 3 * p + j
            if kind == "ici":
                cps.append(_remote(src_refs[p].at[rows, :], land.at[k_me, rows, :], ssem.at[i], rsem.at[i], (*chip, c)))
            else:
                got = land.at[2 * chip[0] + chip[1], rows, :]
                cps.append(_remote(got, got, ssem.at[i], rsem.at[i], (x, y, 1 - c)))
        if kind == "relay":
            cps.append(_remote(src_refs[p], land.at[k_me], ssem.at[3 * n + p], rsem.at[3 * n + p], (x, y, 1 - c)))
    return cps


def gather_now(srcs, conv):
    n = len(srcs)

    def body(*refs):
        src_refs, conv_ref = refs[:n], refs[n]
        lands, gconv = refs[n + 1:2 * n + 1], refs[2 * n + 1]
        ssem, rsem, fsem, frsem, csem, crsem = refs[2 * n + 2:]
        x, y, c = _coords()
        k_me = 2 * x + y
        targets = [(*chip, c) for chip in _peer_chips(x, y)] + [(x, y, 1 - c)]
        ici = _gather_copies("ici", src_refs, lands, ssem, rsem)
        relay = _gather_copies("relay", src_refs, lands, fsem, frsem)
        passed = [cp for i, cp in enumerate(relay) if i % 4 != 3]
        own = relay[3::4]
        conv_cps = [_remote(conv_ref, gconv.at[k_me], csem.at[j], crsem.at[j], t) for j, t in enumerate(targets)]
        for cp in ici + conv_cps + own:
            cp.start()
        for cp, fw in zip(ici, passed):
            cp.wait_recv()
            fw.start()
        for cp in conv_cps + relay:
            cp.wait_recv()
        for cp in ici + relay + conv_cps:
            cp.wait_send()

    out_shape = [jax.ShapeDtypeStruct((N_CHIPS,) + s.shape, s.dtype) for s in srcs]
    out_shape.append(jax.ShapeDtypeStruct((N_CHIPS,) + conv.shape, conv.dtype))
    sems = lambda k: pltpu.SemaphoreType.DMA((k,))
    return pl.pallas_call(
        body, name="gather_now", out_shape=out_shape, in_specs=[HBM] * (n + 1), out_specs=[HBM] * (n + 1),
        scratch_shapes=[sems(3 * n), sems(3 * n), sems(4 * n), sems(4 * n), sems(N_CHIPS), sems(N_CHIPS)],
    )(*srcs, conv)


def _gather_maker(kind, n_src):
    def make(refs, ssem, rsem):
        cps = _gather_copies(kind, refs[:n_src], refs[n_src:], ssem, rsem)
        return cps, cps
    return make


def _scatter_maker(n):
    def make(refs, ssem, rsem):
        x, y, c = _coords()
        k_me = 2 * x + y
        sends, arrivals = [], []
        for p in range(n):
            src, land = refs[p], refs[n + p]
            sends.append(_remote(src.at[k_me, 1 - c], land.at[0], ssem.at[7 * p], rsem.at[7 * p], (x, y, 1 - c)))
            for j, chip in enumerate(_peer_chips(x, y)):
                for cc in range(2):
                    sends.append(_remote(src.at[2 * chip[0] + chip[1], cc], land.at[1 + 2 * j + c],
                                         ssem.at[7 * p + 1 + 2 * j + cc], rsem.at[7 * p + 1 + 2 * j + c], (*chip, cc)))
            for s in range(7):
                arrivals.append(_remote(land.at[s], land.at[s], ssem.at[7 * p + s], rsem.at[7 * p + s], (x, y, 1 - c)))
        return sends, arrivals
    return make


def _share_maker(n):
    def make(refs, ssem, rsem):
        x, y, c = _coords()
        sends = [_remote(refs[p].at[c], refs[p].at[c], ssem.at[p], rsem.at[p], (x, y, 1 - c)) for p in range(n)]
        arrivals = [_remote(refs[p].at[1 - c], refs[p].at[1 - c], ssem.at[p], rsem.at[p], (x, y, 1 - c)) for p in range(n)]
        return sends, arrivals
    return make


def split_start(name, make, n_sems, operands, after):
    n = len(operands)

    def body(*refs):
        ssem, rsem, token = refs[n + 1], refs[n + 2], refs[-1]
        for cp in make(refs[:n], ssem, rsem)[0]:
            cp.start()
        token[...] = jnp.zeros_like(token)

    ops = [pltpu.with_memory_space_constraint(a, pltpu.HBM) for a in operands]
    outs = pl.pallas_call(
        body, name=name,
        out_shape=(pltpu.SemaphoreType.DMA((n_sems,)), pltpu.SemaphoreType.DMA((n_sems,)),
                   *[pltpu.HBM(a.shape, a.dtype) for a in ops], jax.ShapeDtypeStruct((8, LANE), f32)),
        in_specs=[HBM] * n + [ANY], out_specs=(SEM, SEM, *[HBM] * n, pl.BlockSpec(memory_space=pltpu.VMEM)),
        input_output_aliases={i: 2 + i for i in range(n)},
        compiler_params=pltpu.CompilerParams(has_side_effects=DATAFLOW),
    )(*ops, after)
    return dict(name=name, make=make, ssem=outs[0], rsem=outs[1], operands=outs[2:2 + n], token=outs[-1][0, 0])


def split_wait(handle, after):
    n = len(handle["operands"])

    def body(*refs):
        sends, arrivals = handle["make"](refs[:n], refs[n], refs[n + 1])
        for cp in sends:
            cp.wait_send()
        for cp in arrivals:
            cp.wait_recv()

    outs = pl.pallas_call(
        body, name=handle["name"].replace("start", "wait"),
        out_shape=tuple(pltpu.HBM(a.shape, a.dtype) for a in handle["operands"]),
        in_specs=[HBM] * n + [SEM, SEM, ANY], out_specs=tuple([HBM] * n),
        input_output_aliases={i: i for i in range(n)},
        compiler_params=pltpu.CompilerParams(has_side_effects=DATAFLOW),
    )(*handle["operands"], handle["ssem"], handle["rsem"], after)
    return list(outs)


def piece_sum(g, recv, kc_arr):
    _, _, rb, cc = g.shape
    tr = min(256, rb)

    def body(kc_ref, g_ref, r_ref, o_ref):
        acc = g_ref[...].astype(f32)
        for s in range(7):
            acc = acc + r_ref[s].astype(f32)
        o_ref[...] = acc

    return pl.pallas_call(
        body, name="piece_sum",
        grid_spec=pltpu.PrefetchScalarGridSpec(
            num_scalar_prefetch=1, grid=(rb // tr,),
            in_specs=[pl.BlockSpec((None, None, tr, cc), lambda r, kc: (kc[0], kc[1], r, 0)),
                      pl.BlockSpec((7, tr, cc), lambda r, kc: (0, r, 0))],
            out_specs=pl.BlockSpec((None, tr, cc), lambda r, kc: (kc[1], r, 0))),
        out_shape=jax.ShapeDtypeStruct((2, rb, cc), f32),
        compiler_params=_params("arbitrary"),
    )(kc_arr, g, recv)


def small_all_reduce(vec):
    def body(v_ref, o_ref, gat, ssem, rsem):
        x, y, c = _coords()
        me = 4 * x + 2 * y + c
        gat[me] = v_ref[...]
        sends = []
        for t in range(1, N_DEVICES):
            peer = (x ^ (t >> 2), y ^ ((t >> 1) & 1), c ^ (t & 1))
            cp = _remote(v_ref, gat.at[me], ssem.at[t - 1], rsem.at[t - 1], peer)
            cp.start()
            sends.append(cp)
        for t in range(1, N_DEVICES):
            peer = (x ^ (t >> 2), y ^ ((t >> 1) & 1), c ^ (t & 1))
            slot = gat.at[4 * peer[0] + 2 * peer[1] + peer[2]]
            _remote(slot, slot, ssem.at[t - 1], rsem.at[t - 1], peer).wait_recv()
        for cp in sends:
            cp.wait_send()
        acc = gat[0]
        for d in range(1, N_DEVICES):
            acc = acc + gat[d]
        o_ref[...] = acc

    return pl.pallas_call(
        body, name="small_all_reduce", out_shape=jax.ShapeDtypeStruct(vec.shape, vec.dtype),
        in_specs=[pl.BlockSpec(memory_space=pltpu.VMEM)], out_specs=pl.BlockSpec(memory_space=pltpu.VMEM),
        scratch_shapes=[pltpu.VMEM((N_DEVICES,) + vec.shape, vec.dtype), pltpu.SemaphoreType.DMA((N_DEVICES - 1,)),
                        pltpu.SemaphoreType.DMA((N_DEVICES - 1,))],
    )(vec)


def _adamw_math(w, g, m, v):
    m_new = ADAM_B1 * m + (1.0 - ADAM_B1) * g
    v_new = ADAM_B2 * v + (1.0 - ADAM_B2) * jnp.square(g)
    m_hat = m_new / (1.0 - ADAM_B1 ** ADAM_STEP)
    v_hat = v_new / (1.0 - ADAM_B2 ** ADAM_STEP)
    delta = -ADAM_LR * (m_hat / (jnp.sqrt(v_hat) + ADAM_EPS) + ADAM_WD * w)
    return delta, m_new, v_new


def adamw_shard(w, g0, g1, m, v):
    depth, rows, cols = w.shape
    half = rows // 2
    tr = min(256, half)
    nr = half // tr

    def body(w_ref, g0_ref, g1_ref, m_ref, v_ref, go_ref, d_ref, nm_ref, nv_ref):
        gv = jnp.where(pl.program_id(0) == 0, g0_ref[...], g1_ref[...])
        go_ref[...] = gv
        d_ref[...], nm_ref[...], nv_ref[...] = _adamw_math(w_ref[...], gv, m_ref[...], v_ref[...])

    spec = pl.BlockSpec((None, tr, cols), lambda l, h, r: (l, h * nr + r, 0))
    g0spec = pl.BlockSpec((None, tr, cols), lambda l, h, r: (jnp.where(l == 0, h, 1), jnp.where(l == 0, r, nr - 1), 0))
    g1spec = pl.BlockSpec((None, tr, cols), lambda l, h, r: (jnp.where(l == 1, h, 0), jnp.where(l == 1, r, 0), 0))
    return pl.pallas_call(
        body, name="adamw_shard", grid=(depth, 2, nr), in_specs=[spec, g0spec, g1spec, spec, spec], out_specs=[spec] * 4,
        out_shape=[jax.ShapeDtypeStruct(w.shape, f32)] * 4,
        compiler_params=_params("arbitrary", "arbitrary", "arbitrary"),
    )(w, g0, g1, m, v)


def adamw_cols(w, g, m, v, tc=34):
    cols, depth, rows = w.shape

    def body(w_ref, g_ref, m_ref, v_ref, d_ref, nm_ref, nv_ref):
        d_ref[...], nm_ref[...], nv_ref[...] = _adamw_math(w_ref[...], g_ref[...], m_ref[...], v_ref[...])

    spec = pl.BlockSpec((tc, depth, rows), lambda i: (i, 0, 0))
    return pl.pallas_call(
        body, name="adamw_cols", grid=(cols // tc,), in_specs=[spec] * 4, out_specs=[spec] * 3,
        out_shape=[jax.ShapeDtypeStruct(w.shape, f32)] * 3,
        compiler_params=_params("arbitrary"),
    )(w, g, m, v)


def adamw_small(w, g, m, v):
    def body(w_ref, g_ref, m_ref, v_ref, d_ref, nm_ref, nv_ref):
        d_ref[...], nm_ref[...], nv_ref[...] = _adamw_math(w_ref[...], g_ref[...], m_ref[...], v_ref[...])

    return pl.pallas_call(
        body, name="adamw_small", out_shape=[jax.ShapeDtypeStruct(w.shape, f32)] * 3,
    )(w, g, m, v)


WEIGHTS = ("mix_norm_g", "w_in", "q_gain", "k_gain", "sinks", "rel_bias", "conv_w", "conv_b", "dt_bias", "a_log", "d_skip",
           "ssm_norm_g", "w_out", "mlp_norm_g", "w_up", "w_down")
BIG = ("w_in", "w_out", "w_up", "w_down")
SMALL = tuple(n for n in WEIGHTS if n not in BIG)
PACK_COLS = 1024
PACK_ROWS = 16


def _pack(named, last=None):
    flat = jnp.concatenate([named[n].reshape(-1) for n in SMALL])
    tail = jnp.zeros((1,), f32) if last is None else last.reshape(1)
    pad = jnp.zeros((PACK_ROWS * PACK_COLS - flat.shape[0] - 1,), f32)
    return jnp.concatenate([flat, pad, tail]).reshape(PACK_ROWS, PACK_COLS)


def _unpack(buf, shapes):
    flat = buf.reshape(-1)
    out, at = {}, 0
    for n in SMALL:
        size = int(np.prod(shapes[n]))
        out[n] = flat[at:at + size].reshape(shapes[n])
        at += size
    return out


class _Exchange:
    GROUPS = {"A": (("w_up", 0), ("w_down", 0)), "B": (("w_in", 1), ("w_out", 1)), "C": (("w_up", 1), ("w_down", 1))}
    ICI_AT = {("mid", 0): "B", ("pre_out", 0): "C"}
    RELAY_AT = {("pre_out", 0): "A", ("pre_mlp", 0): "B", ("mid", 1): "C"}
    LAST = ("mix", 0)
    IN_FLIGHT = 2

    def __init__(self, wts, kc_arr):
        self.wts, self.kc_arr = wts, kc_arr
        self.own = {(n, l): wts[n][l].astype(bf16) for n in BIG for l in range(DEPTH)}
        now = gather_now([self.own["w_in", 0], self.own["w_out", 0]], wts["conv_w"])
        self.ready = {("w_in", 0): now[0], ("w_out", 0): now[1]}
        self.conv_w = jnp.transpose(now[2], (1, 2, 0, 3)).reshape(DEPTH, CONV_WIDTH, D_CONV)
        self.ici, self.relay = {}, {}
        self.scatter, self.share, self.reduced = [], [], {}
        self._start_ici("A", now[2])

    def _start_ici(self, g, after):
        srcs = [self.own[p] for p in self.GROUPS[g]]
        lands = [lax.empty((N_CHIPS,) + s.shape, s.dtype) for s in srcs]
        self.ici[g] = split_start("gather%s_ici_start" % g, _gather_maker("ici", len(srcs)), 3 * len(srcs), srcs + lands,
                                  after)
        return self.ici[g]["token"]

    def stage(self, name, after):
        if name == ("begin", 0):
            return self.ici["A"]["token"]
        tok = 0.0
        g = self.RELAY_AT.get(name)
        if g is not None:
            n = len(self.GROUPS[g])
            self.relay[g] = split_start("gather%s_relay_start" % g, _gather_maker("relay", n), 4 * n,
                                        split_wait(self.ici[g], after), after)
            tok = self.relay[g]["token"]
        if name in self.ICI_AT:
            tok = tok + self._start_ici(self.ICI_AT[name], after)
        return tok

    def _get(self, piece, after):
        if piece not in self.ready:
            g = [k for k, pieces in self.GROUPS.items() if piece in pieces][0]
            lands = split_wait(self.relay[g], after)[len(self.GROUPS[g]):]
            self.ready.update(zip(self.GROUPS[g], lands))
        return self.ready[piece]

    def w_in(self, l, after):
        return align_w_in(self._get(("w_in", l), after))

    def w_out(self, l, after):
        return self._get(("w_out", l), after).reshape(D_MODEL, D_MODEL)

    def mlp(self, l, after):
        return self._get(("w_up", l), after), self._get(("w_down", l), after)

    def _view(self, n, g):
        _, rows, cols = self.wts[n].shape
        return g.reshape(N_CHIPS, 2, rows // 2, cols)

    def grads(self, name, arrays, after):
        if name == self.LAST:
            self.held = (name, arrays)
            return 0.0
        return self._scatter(name, arrays, after) + self._advance(after, self.IN_FLIGHT)

    def flush(self, after):
        return self._scatter(*self.held, after) + self._advance(after, self.IN_FLIGHT)

    def _scatter(self, name, arrays, after):
        pieces = [(n, name[1]) for n in arrays]
        views = [self._view(n, g) for n, g in arrays.items()]
        lands = [lax.empty((7,) + v.shape[2:], bf16) for v in views]
        h = split_start("scatter_%s%d_start" % name, _scatter_maker(len(views)), 7 * len(views), views + lands, after)
        self.scatter.append((pieces, h))
        return h["token"]

    def _take_share(self, after):
        pieces, h = self.share.pop(0)
        self.reduced.update(zip(pieces, split_wait(h, after)))

    def _take_scatter(self, after):
        pieces, h = self.scatter.pop(0)
        done = split_wait(h, after)
        views, lands = done[:len(pieces)], done[len(pieces):]
        sums = [piece_sum(v, land, self.kc_arr) for v, land in zip(views, lands)]
        hs = split_start(h["name"].replace("scatter", "share"), _share_maker(len(sums)), len(sums), sums, after)
        self.share.append((pieces, hs))
        return hs["token"]

    def _advance(self, after, newest):
        if self.share:
            self._take_share(after)
        return self._take_scatter(after) if len(self.scatter) > newest else 0.0

    def reduced_grads(self, names, after):
        want = [(n, l) for n in names for l in range(DEPTH)]
        while not all(p in self.reduced for p in want):
            if any(p in pieces for p in want for pieces, _ in self.share):
                self._take_share(after)
            else:
                self._take_scatter(after)
        return {n: [self.reduced[n, l] for l in range(DEPTH)] for n in names}


def kernel(x, mix_norm_g, w_in, q_gain, k_gain, sinks, rel_bias, conv_w, conv_b, dt_bias, a_log, d_skip, ssm_norm_g, w_out, mlp_norm_g, w_up, w_down, loss_target, m_mix_norm_g, m_w_in, m_q_gain, m_k_gain, m_sinks, m_rel_bias, m_conv_w, m_conv_b, m_dt_bias, m_a_log, m_d_skip, m_ssm_norm_g, m_w_out, m_mlp_norm_g, m_w_up, m_w_down, v_mix_norm_g, v_w_in, v_q_gain, v_k_gain, v_sinks, v_rel_bias, v_conv_w, v_conv_b, v_dt_bias, v_a_log, v_d_skip, v_ssm_norm_g, v_w_out, v_mlp_norm_g, v_w_up, v_w_down):
    wts = dict(mix_norm_g=mix_norm_g, w_in=w_in, q_gain=q_gain, k_gain=k_gain, sinks=sinks, rel_bias=rel_bias, conv_w=conv_w,
               conv_b=conv_b, dt_bias=dt_bias, a_log=a_log, d_skip=d_skip, ssm_norm_g=ssm_norm_g, w_out=w_out,
               mlp_norm_g=mlp_norm_g, w_up=w_up, w_down=w_down)
    mom = dict(mix_norm_g=m_mix_norm_g, w_in=m_w_in, q_gain=m_q_gain, k_gain=m_k_gain, sinks=m_sinks, rel_bias=m_rel_bias,
               conv_w=m_conv_w, conv_b=m_conv_b, dt_bias=m_dt_bias, a_log=m_a_log, d_skip=m_d_skip, ssm_norm_g=m_ssm_norm_g,
               w_out=m_w_out, mlp_norm_g=m_mlp_norm_g, w_up=m_w_up, w_down=m_w_down)
    var = dict(mix_norm_g=v_mix_norm_g, w_in=v_w_in, q_gain=v_q_gain, k_gain=v_k_gain, sinks=v_sinks, rel_bias=v_rel_bias,
               conv_w=v_conv_w, conv_b=v_conv_b, dt_bias=v_dt_bias, a_log=v_a_log, d_skip=v_d_skip, ssm_norm_g=v_ssm_norm_g,
               w_out=v_w_out, mlp_norm_g=v_mlp_norm_g, w_up=v_w_up, w_down=v_w_down)
    xi, yi, ci = _coords()
    k_me = 2 * xi + yi
    kc_arr = jnp.stack([k_me, ci]).astype(jnp.int32)

    prov = _Exchange(wts, kc_arr)
    small_w = {n: wts[n] for n in SMALL}
    small_w["conv_w"] = prov.conv_w
    loss, dx, grads, tok = local_step(x[0], loss_target[0], small_w, prov)

    small_shapes = {n: grads[n].shape for n in SMALL}
    small_sum = small_all_reduce(_pack(grads, loss) + tok)
    loss = small_sum[PACK_ROWS - 1, PACK_COLS - 1]
    tok = prov.flush(small_sum)
    small = _unpack(small_sum, small_shapes)
    cols = conv_w.shape[-1]
    small["conv_w"] = lax.dynamic_slice_in_dim(small["conv_w"], k_me * cols, cols, axis=2)
    g_out_d, d_out_d, m_out_d, v_out_d = {}, {}, {}, {}
    shard_shapes = {n: wts[n].shape for n in SMALL}
    d, nm, nv = adamw_small(_pack(wts), _pack(small) + tok, _pack(mom), _pack(var))
    for dst, buf in ((d_out_d, d), (m_out_d, nm), (v_out_d, nv)):
        dst.update(_unpack(buf, shard_shapes))
    g_out_d.update(small)

    after = d
    for names in (("w_up", "w_down"), ("w_in", "w_out")):
        for n, (g0, g1) in prov.reduced_grads(names, after).items():
            if n == "w_in":
                rows, cols = wts[n].shape[1:]
                to_cols = lambda a: jnp.transpose(a, (2, 0, 1))
                g_t = jnp.stack([to_cols(g).reshape(cols, rows) for g in (g0, g1)], axis=1)
                res_t = adamw_cols(to_cols(wts[n]), g_t, to_cols(mom[n]), to_cols(var[n]))
                g_out_d[n], d_out_d[n], m_out_d[n], v_out_d[n] = (jnp.transpose(a, (1, 2, 0)) for a in (g_t, *res_t))
            else:
                g_out_d[n], d_out_d[n], m_out_d[n], v_out_d[n] = adamw_shard(wts[n], g0, g1, mom[n], var[n])
            after = d_out_d[n]

    return (loss, dx[None], *[g_out_d[n] for n in WEIGHTS], *[d_out_d[n] for n in WEIGHTS],
            *[m_out_d[n] for n in WEIGHTS], *[v_out_d[n] for n in WEIGHTS])
```

```python
import numpy as np
import jax
import jax.numpy as jnp
from jax import lax
from jax.experimental import pallas as pl
from jax.experimental.pallas import tpu as pltpu

f32 = jnp.float32
bf16 = jnp.bfloat16

SEQ = 2048
D_MODEL = 1024
DEPTH = 2
HEAD_DIM = 64
N_Q_HEADS = 8
N_KV_HEADS = 2
Q_PER_KV = N_Q_HEADS // N_KV_HEADS
BLOCK = 128
N_BLOCKS = SEQ // BLOCK
N_BUCKETS = 32
MAX_DISTANCE = 128
SSM_HEADS = 8
SSM_HEAD_DIM = 64
SSM_GROUPS = 2
HEADS_PER_GROUP = SSM_HEADS // SSM_GROUPS
SSM_STATE = 128
CONV_WIDTH = 4
CHUNK = 128
N_CHUNKS = SEQ // CHUNK
D_FF = 4 * D_MODEL
D_ATTN = N_Q_HEADS * HEAD_DIM
D_KV = N_KV_HEADS * HEAD_DIM
D_SSM = SSM_HEADS * SSM_HEAD_DIM
D_BC = SSM_GROUPS * SSM_STATE
D_CONV = D_SSM + 2 * D_BC
D_IN = D_ATTN + 2 * D_KV + D_SSM + D_CONV + SSM_HEADS
EPS = 1e-6
NEG = -1e30
N_CHIPS = 4
FF_TILE = D_FF // N_CHIPS

LANE = 128
PW = D_ATTN + D_SSM + D_CONV + 2 * D_KV + LANE
OFF_Q, OFF_Z, OFF_X, OFF_K, OFF_V, OFF_DT = 0, 512, 1024, 2048, 2176, 2304

ADAM_LR = 0.001
ADAM_B1 = 0.9
ADAM_B2 = 0.999
ADAM_EPS = 1e-08
ADAM_WD = 0.01
ADAM_STEP = 10

VMEM_LIMIT = 56 * 1024 * 1024


def _params(*sem):
    return pltpu.CompilerParams(dimension_semantics=tuple(sem), vmem_limit_bytes=VMEM_LIMIT)


def _bdot(a, b):
    return jnp.dot(a.astype(bf16), b.astype(bf16), preferred_element_type=f32)


def _bdot_nt(a, b):
    return lax.dot_general(a.astype(bf16), b.astype(bf16), (((1,), (1,)), ((), ())), preferred_element_type=f32)


def _bdot_tn(a, b):
    return lax.dot_general(a.astype(bf16), b.astype(bf16), (((0,), (0,)), ((), ())), preferred_element_type=f32)


def _hdot(a, b):
    return jnp.dot(a, b, precision=lax.Precision.HIGHEST, preferred_element_type=f32)


def _sigmoid(x):
    return 1.0 / (1.0 + jnp.exp(-x))


def _softplus(x):
    return jnp.maximum(x, 0.0) + jnp.log1p(jnp.exp(-jnp.abs(x)))


def _rms(x):
    return lax.rsqrt(jnp.mean(x * x, axis=-1, keepdims=True) + EPS)


def _rms_bwd(dy, xhat, r, g):
    t = dy * g
    return r * (t - xhat * jnp.mean(t * xhat, axis=-1, keepdims=True))


def _full(shape):
    return pl.BlockSpec(shape, lambda *_: (0,) * len(shape))


def _bucket_table():
    qi = np.arange(BLOCK)[:, None]
    kj = np.arange(2 * BLOCK)[None, :]
    dist = qi + BLOCK - kj
    ok = (dist >= 0) & (dist < 128)
    d = np.clip(dist, 0, None)
    max_exact = N_BUCKETS // 2
    d_f = np.maximum(d, 1).astype(np.float32)
    large = max_exact + (np.log(d_f / np.float32(max_exact)) / np.float32(np.log(MAX_DISTANCE / max_exact))
                         * np.float32(N_BUCKETS - max_exact)).astype(np.int32)
    large = np.minimum(large, N_BUCKETS - 1)
    bucket = np.where(d < max_exact, d, large)
    return np.where(ok, bucket, -1).astype(np.int32)


def bias_build(rel_bias, bucket):
    def body(rel_ref, bkt_ref, o_ref):
        bkt = bkt_ref[...]
        for h in range(N_Q_HEADS):
            acc = jnp.where(bkt < 0, NEG, 0.0).astype(f32)
            for b in range(N_BUCKETS):
                acc = acc + jnp.where(bkt == b, rel_ref[b, h], 0.0)
            o_ref[h] = acc

    return pl.pallas_call(
        body, name="bias_build", out_shape=jax.ShapeDtypeStruct((N_Q_HEADS,) + bucket.shape, f32),
        in_specs=[pl.BlockSpec(memory_space=pltpu.SMEM), pl.BlockSpec(memory_space=pltpu.VMEM)],
        out_specs=pl.BlockSpec(memory_space=pltpu.VMEM),
    )(rel_bias, bucket)


def bias_bwd(dband0, dband1, bucket):
    def body(d0_ref, d1_ref, bkt_ref, o_ref):
        bkt = bkt_ref[...]
        o_ref[...] = jnp.zeros_like(o_ref)
        for h in range(N_Q_HEADS):
            d = d0_ref[h] + d1_ref[h]
            for b in range(N_BUCKETS):
                part = jnp.sum(jnp.where(bkt == b, d, 0.0), axis=1, keepdims=True)
                o_ref[b:b + 1, h:h + 1] = jnp.sum(part, axis=0, keepdims=True)

    return pl.pallas_call(
        body, name="bias_bwd", out_shape=jax.ShapeDtypeStruct((N_BUCKETS, LANE), f32),
    )(dband0, dband1, bucket)


W_IN_SHARD = D_IN // N_CHIPS
_ALIGNED_PIECES = ((0, 0, 512), (1, 190, 578), (2, 0, 124), (2, 124, 578), (3, 0, 570), (0, 512, 578), (1, 0, 62),
                   (1, 62, 190), (3, 570, 578))
_SHARD_PIECES = (((0, 512), (2048, 2114)), ((2114, 2176), (2176, 2304), (512, 900)), ((900, 1024), (1024, 1478)),
                 ((1478, 2048), (2304, 2312)))


def align_w_in(shards, tr=256):
    def body(s_ref, o_ref):
        parts = [s_ref[k, :, a:b] for k, a, b in _ALIGNED_PIECES]
        parts.append(jnp.zeros((tr, LANE - SSM_HEADS), s_ref.dtype))
        o_ref[...] = jnp.concatenate(parts, axis=-1)

    return pl.pallas_call(
        body, name="align_w_in", grid=(D_MODEL // tr,),
        in_specs=[pl.BlockSpec((N_CHIPS, tr, W_IN_SHARD), lambda i: (0, i, 0))],
        out_specs=pl.BlockSpec((tr, PW), lambda i: (i, 0)),
        out_shape=jax.ShapeDtypeStruct((D_MODEL, PW), shards.dtype),
        compiler_params=_params("arbitrary"),
    )(shards)


def split_w_in_grad(dw, tr=256):
    def body(d_ref, o_ref):
        for k, pieces in enumerate(_SHARD_PIECES):
            o_ref[k] = jnp.concatenate([d_ref[:, a:b] for a, b in pieces], axis=-1)

    return pl.pallas_call(
        body, name="split_w_in_grad", grid=(D_MODEL // tr,),
        in_specs=[pl.BlockSpec((tr, PW), lambda i: (i, 0))],
        out_specs=pl.BlockSpec((N_CHIPS, tr, W_IN_SHARD), lambda i: (0, i, 0)),
        out_shape=jax.ShapeDtypeStruct((N_CHIPS, D_MODEL, W_IN_SHARD), dw.dtype),
        compiler_params=_params("arbitrary"),
    )(dw)

def in_fwd(x, g, w, tm=512):
    def body(x_ref, g_ref, w_ref, o_ref):
        xv = x_ref[...]
        h = xv * _rms(xv) * g_ref[...]
        o_ref[...] = _bdot(h, w_ref[...])

    return pl.pallas_call(
        body, name="in_fwd", grid=(SEQ // tm,),
        in_specs=[pl.BlockSpec((tm, D_MODEL), lambda i: (i, 0)), _full((1, D_MODEL)), _resident((D_MODEL, PW))],
        out_specs=pl.BlockSpec((tm, PW), lambda i: (i, 0)),
        out_shape=jax.ShapeDtypeStruct((SEQ, PW), f32),
        compiler_params=_params("arbitrary"),
    )(x, g, w)


def _resident(shape):
    return pl.BlockSpec(shape, lambda *_: (0,) * len(shape), pipeline_mode=pl.Buffered(1))


def in_bwd(dq, dz, dxbc, dk, dv, ddt, x, g, w, dres, tm=512):
    nt = SEQ // tm

    def body(dq_ref, dz_ref, dx_ref, dk_ref, dv_ref, ddt_ref, x_ref, g_ref, w_ref, dres_ref, o_ref, dw16_ref, dg_ref, dw_ref):
        i = pl.program_id(0)

        @pl.when(i == 0)
        def _():
            dw_ref[...] = jnp.zeros_like(dw_ref)
            dg_ref[...] = jnp.zeros_like(dg_ref)

        dproj = jnp.concatenate([dq_ref[...], dz_ref[...], dx_ref[...], dk_ref[...], dv_ref[...], ddt_ref[...]],
                                axis=-1).astype(bf16)
        xv = x_ref[...]
        r = _rms(xv)
        xhat = xv * r
        gv = g_ref[...]
        h = xhat * gv
        dw_ref[...] += _bdot_tn(h, dproj)
        dh = _bdot_nt(dproj, w_ref[...])
        dg_ref[...] += jnp.sum(dh * xhat, axis=0, keepdims=True)
        o_ref[...] = dres_ref[...] + _rms_bwd(dh, xhat, r, gv)

        @pl.when(i == nt - 1)
        def _():
            dw16_ref[...] = dw_ref[...].astype(bf16)

    tok = lambda w_: pl.BlockSpec((tm, w_), lambda i: (i, 0))
    return pl.pallas_call(
        body, name="in_bwd", grid=(nt,),
        in_specs=[tok(D_ATTN), tok(D_SSM), tok(D_CONV), tok(D_KV), tok(D_KV), tok(LANE), tok(D_MODEL),
                  _full((1, D_MODEL)), _resident((D_MODEL, PW)), tok(D_MODEL)],
        out_specs=[tok(D_MODEL), _resident((D_MODEL, PW)), _full((1, D_MODEL))],
        out_shape=[jax.ShapeDtypeStruct((SEQ, D_MODEL), f32), jax.ShapeDtypeStruct((D_MODEL, PW), bf16),
                   jax.ShapeDtypeStruct((1, D_MODEL), f32)],
        scratch_shapes=[pltpu.VMEM((D_MODEL, PW), f32)],
        compiler_params=_params("arbitrary"),
    )(dq, dz, dxbc, dk, dv, ddt, x, g, w, dres)


def _attn_softmax_t(qk, bias_t, sink, first, key_row):
    s = qk * (HEAD_DIM ** -0.5) + bias_t
    s = jnp.where(jnp.logical_and(first, key_row < BLOCK), NEG, s)
    m = jnp.maximum(jnp.max(s, axis=0, keepdims=True), sink)
    p = jnp.exp(s - m)
    psink = jnp.exp(sink - m)
    inv = 1.0 / (jnp.sum(p, axis=0, keepdims=True) + psink)
    return p * inv, psink * inv


def _rms_t(x_t):
    return lax.rsqrt(jnp.mean(x_t * x_t, axis=0, keepdims=True) + EPS)


def attn_fwd_t(proj, q_gain_col, k_gain, sinks, bias_t):
    kcol, vcol = OFF_K // D_KV, OFF_V // D_KV

    def body(q_ref, kc_ref, kp_ref, vc_ref, vp_ref, qg_ref, kg_ref, sink_ref, bias_ref, o_ref, ot_scr):
        n = pl.program_id(0)
        first = n == 0
        key_row = lax.broadcasted_iota(jnp.int32, (2 * BLOCK, BLOCK), 0)
        k2 = jnp.concatenate([kp_ref[...], kc_ref[...]], axis=0)
        v_t = jnp.concatenate([vp_ref[...], vc_ref[...]], axis=0).T
        q_t = q_ref[...].T
        qg = jnp.broadcast_to(qg_ref[...], (HEAD_DIM, BLOCK))
        kg = kg_ref[...]
        for hk in range(N_KV_HEADS):
            sl = slice(hk * HEAD_DIM, (hk + 1) * HEAD_DIM)
            kk = k2[:, sl]
            kn = (kk * _rms(kk) * kg).astype(bf16)
            vt = v_t[sl, :].astype(bf16)
            heads = range(hk * Q_PER_KV, (hk + 1) * Q_PER_KV)
            qns = []
            for h in heads:
                qh = q_t[h * HEAD_DIM:(h + 1) * HEAD_DIM, :]
                qns.append(qh * _rms_t(qh) * qg)
            scores = [_bdot(kn, qn) for qn in qns]
            for h, s in zip(heads, scores):
                p, _ = _attn_softmax_t(s, bias_ref[h], sink_ref[h], first, key_row)
                ot_scr[h * HEAD_DIM:(h + 1) * HEAD_DIM, :] = _bdot(vt, p)
        o_ref[...] = ot_scr[...].T

    prev = lambda n: jnp.maximum(n - 1, 0)
    return pl.pallas_call(
        body, name="attn_fwd", grid=(N_BLOCKS,),
        in_specs=[pl.BlockSpec((BLOCK, D_ATTN), lambda n: (n, 0)),
                  pl.BlockSpec((BLOCK, D_KV), lambda n: (n, kcol)), pl.BlockSpec((BLOCK, D_KV), lambda n: (prev(n), kcol)),
                  pl.BlockSpec((BLOCK, D_KV), lambda n: (n, vcol)), pl.BlockSpec((BLOCK, D_KV), lambda n: (prev(n), vcol)),
                  _full((HEAD_DIM, 1)), _full((1, HEAD_DIM)), pl.BlockSpec(memory_space=pltpu.SMEM),
                  _full((N_Q_HEADS, 2 * BLOCK, BLOCK))],
        out_specs=pl.BlockSpec((BLOCK, D_ATTN), lambda n: (n, 0)),
        out_shape=jax.ShapeDtypeStruct((SEQ, D_ATTN), f32),
        scratch_shapes=[pltpu.VMEM((D_ATTN, BLOCK), f32)],
        compiler_params=_params("arbitrary"),
    )(proj, proj, proj, proj, proj, q_gain_col, k_gain, sinks, bias_t)


def attn_bwd_t(proj, d_out, q_gain_col, k_gain, sinks, bias_t):
    kcol, vcol = OFF_K // D_KV, OFF_V // D_KV

    def body(q_ref, kc_ref, kp_ref, vc_ref, vp_ref, do_ref, qg_ref, kg_ref, sink_ref, bias_ref,
             dq_ref, dk_ref, dv_ref, dband_ref, dsink_ref, dqg_ref, dkg_ref, dkn_scr, dv_scr, dqt_scr, dsink_acc, dqg_acc):
        i = pl.program_id(0)
        first = i == N_BLOCKS - 1

        @pl.when(i == 0)
        def _():
            for ref in (dband_ref, dkg_ref, dkn_scr, dv_scr, dsink_acc, dqg_acc):
                ref[...] = jnp.zeros_like(ref)

        key_row = lax.broadcasted_iota(jnp.int32, (2 * BLOCK, BLOCK), 0)
        k2 = jnp.concatenate([kp_ref[...], kc_ref[...]], axis=0)
        v2 = jnp.concatenate([vp_ref[...], vc_ref[...]], axis=0)
        q_t = q_ref[...].T
        do_t = do_ref[...].T
        qg = jnp.broadcast_to(qg_ref[...], (HEAD_DIM, BLOCK))
        kg = kg_ref[...]
        scale = HEAD_DIM ** -0.5
        for hk in range(N_KV_HEADS):
            sl = slice(hk * HEAD_DIM, (hk + 1) * HEAD_DIM)
            kk = k2[:, sl]
            rk = _rms(kk)
            khat = kk * rk
            kn = (khat * kg).astype(bf16)
            vb = v2[:, sl].astype(bf16)
            dkn = jnp.zeros((2 * BLOCK, HEAD_DIM), f32)
            dvv = jnp.zeros((2 * BLOCK, HEAD_DIM), f32)
            heads = range(hk * Q_PER_KV, (hk + 1) * Q_PER_KV)
            rqs, qhats, qns, d_os = [], [], [], []
            for h in heads:
                hs = slice(h * HEAD_DIM, (h + 1) * HEAD_DIM)
                qh = q_t[hs, :]
                rqs.append(_rms_t(qh))
                qhats.append(qh * rqs[-1])
                qns.append((qhats[-1] * qg).astype(bf16))
                d_os.append(do_t[hs, :].astype(bf16))
            scores = [_bdot(kn, qn) for qn in qns]
            dps = [_bdot(vb, d_o) for d_o in d_os]
            ps, dss = [], []
            for h, s, dp in zip(heads, scores, dps):
                p, psink = _attn_softmax_t(s, bias_ref[h], sink_ref[h], first, key_row)
                delta = jnp.sum(p * dp, axis=0, keepdims=True)
                ds = p * (dp - delta)
                dband_ref[h] += ds
                dsink_acc[h:h + 1, :] += -(psink * delta)
                ps.append(p.astype(bf16))
                dss.append(ds.astype(bf16))
            dqns = [_bdot_tn(kn, ds) * scale for ds in dss]
            for ds, qn, p, d_o in zip(dss, qns, ps, d_os):
                dkn = dkn + _bdot_nt(ds, qn) * scale
                dvv = dvv + _bdot_nt(p, d_o)
            for h, dqn, rq, qhat in zip(heads, dqns, rqs, qhats):
                dqg_acc[...] += dqn * qhat
                t = dqn * qg
                dqt_scr[h * HEAD_DIM:(h + 1) * HEAD_DIM, :] = rq * (t - qhat * jnp.mean(t * qhat, axis=0, keepdims=True))
            dkn_cur = dkn[BLOCK:] + dkn_scr[:, sl]
            dkn_scr[:, sl] = dkn[:BLOCK]
            khat_c, rk_c = khat[BLOCK:], rk[BLOCK:]
            dkg_ref[...] += jnp.sum(dkn_cur * khat_c, axis=0, keepdims=True)
            dk_ref[:, sl] = _rms_bwd(dkn_cur, khat_c, rk_c, kg)
            dv_ref[:, sl] = dvv[BLOCK:] + dv_scr[:, sl]
            dv_scr[:, sl] = dvv[:BLOCK]
        dq_ref[...] = dqt_scr[...].T

        @pl.when(i == N_BLOCKS - 1)
        def _():
            dsink_ref[...] = jnp.sum(dsink_acc[...], axis=1, keepdims=True)
            dqg_ref[...] = jnp.sum(dqg_acc[...], axis=1, keepdims=True)

    blk = lambda i: N_BLOCKS - 1 - i
    prev = lambda i: jnp.maximum(N_BLOCKS - 2 - i, 0)
    return pl.pallas_call(
        body, name="attn_bwd", grid=(N_BLOCKS,),
        in_specs=[pl.BlockSpec((BLOCK, D_ATTN), lambda i: (blk(i), 0)),
                  pl.BlockSpec((BLOCK, D_KV), lambda i: (blk(i), kcol)), pl.BlockSpec((BLOCK, D_KV), lambda i: (prev(i), kcol)),
                  pl.BlockSpec((BLOCK, D_KV), lambda i: (blk(i), vcol)), pl.BlockSpec((BLOCK, D_KV), lambda i: (prev(i), vcol)),
                  pl.BlockSpec((BLOCK, D_ATTN), lambda i: (blk(i), 0)),
                  _full((HEAD_DIM, 1)), _full((1, HEAD_DIM)), pl.BlockSpec(memory_space=pltpu.SMEM),
                  _full((N_Q_HEADS, 2 * BLOCK, BLOCK))],
        out_specs=[pl.BlockSpec((BLOCK, D_ATTN), lambda i: (blk(i), 0)), pl.BlockSpec((BLOCK, D_KV), lambda i: (blk(i), 0)),
                   pl.BlockSpec((BLOCK, D_KV), lambda i: (blk(i), 0)), _full((N_Q_HEADS, 2 * BLOCK, BLOCK)),
                   _full((N_Q_HEADS, 1)), _full((HEAD_DIM, 1)), _full((1, HEAD_DIM))],
        out_shape=[jax.ShapeDtypeStruct((SEQ, D_ATTN), f32), jax.ShapeDtypeStruct((SEQ, D_KV), f32),
                   jax.ShapeDtypeStruct((SEQ, D_KV), f32), jax.ShapeDtypeStruct((N_Q_HEADS, 2 * BLOCK, BLOCK), f32),
                   jax.ShapeDtypeStruct((N_Q_HEADS, 1), f32), jax.ShapeDtypeStruct((HEAD_DIM, 1), f32),
                   jax.ShapeDtypeStruct((1, HEAD_DIM), f32)],
        scratch_shapes=[pltpu.VMEM((BLOCK, D_KV), f32), pltpu.VMEM((BLOCK, D_KV), f32), pltpu.VMEM((D_ATTN, BLOCK), f32),
                        pltpu.VMEM((N_Q_HEADS, BLOCK), f32), pltpu.VMEM((HEAD_DIM, BLOCK), f32)],
        compiler_params=_params("arbitrary"),
    )(proj, proj, proj, proj, proj, d_out, q_gain_col, k_gain, sinks, bias_t)


SUBLANES = 8


def _shift_down(u, s, row8):
    if s == 0:
        return u
    r = pltpu.roll(u, s, 0)
    return jnp.concatenate([jnp.where(row8 >= s, r[:SUBLANES], 0.0), r[SUBLANES:]], axis=0)


def _shift_up(u, s, row8):
    if s == 0:
        return u
    r = pltpu.roll(u, SEQ - s, 0)
    return jnp.concatenate([r[:-SUBLANES], jnp.where(row8 < SUBLANES - s, r[-SUBLANES:], 0.0)], axis=0)


def conv_fwd(proj, conv_w, conv_b):
    xcol = OFF_X // LANE

    def body(u_ref, w_ref, b_ref, o_ref):
        u = u_ref[...]
        row = lax.broadcasted_iota(jnp.int32, (SUBLANES, LANE), 0)
        pre = b_ref[...] + jnp.zeros_like(u)
        for k in range(CONV_WIDTH):
            pre = pre + w_ref[k:k + 1, :] * _shift_down(u, CONV_WIDTH - 1 - k, row)
        o_ref[...] = pre * _sigmoid(pre)

    return pl.pallas_call(
        body, name="conv_fwd", grid=(D_CONV // LANE,),
        in_specs=[pl.BlockSpec((SEQ, LANE), lambda j: (0, xcol + j)), pl.BlockSpec((CONV_WIDTH, LANE), lambda j: (0, j)),
                  pl.BlockSpec((1, LANE), lambda j: (0, j))],
        out_specs=pl.BlockSpec((SEQ, LANE), lambda j: (0, j)),
        out_shape=jax.ShapeDtypeStruct((SEQ, D_CONV), f32),
        compiler_params=_params("arbitrary"),
    )(proj, conv_w, conv_b)


def conv_bwd(proj, d_act, conv_w, conv_b):
    xcol = OFF_X // LANE

    def body(u_ref, da_ref, w_ref, b_ref, du_ref, dw_ref, db_ref):
        u = u_ref[...]
        row = lax.broadcasted_iota(jnp.int32, (SUBLANES, LANE), 0)
        shifted = [_shift_down(u, CONV_WIDTH - 1 - k, row) for k in range(CONV_WIDTH)]
        pre = b_ref[...] + jnp.zeros_like(u)
        for k in range(CONV_WIDTH):
            pre = pre + w_ref[k:k + 1, :] * shifted[k]
        sg = _sigmoid(pre)
        dpre = da_ref[...] * (sg * (1.0 + pre * (1.0 - sg)))
        db_ref[...] = jnp.sum(dpre, axis=0, keepdims=True)
        du = jnp.zeros_like(u)
        for k in range(CONV_WIDTH):
            dw_ref[k:k + 1, :] = jnp.sum(dpre * shifted[k], axis=0, keepdims=True)
            du = du + w_ref[k:k + 1, :] * _shift_up(dpre, CONV_WIDTH - 1 - k, row)
        du_ref[...] = du

    return pl.pallas_call(
        body, name="conv_bwd", grid=(D_CONV // LANE,),
        in_specs=[pl.BlockSpec((SEQ, LANE), lambda j: (0, xcol + j)), pl.BlockSpec((SEQ, LANE), lambda j: (0, j)),
                  pl.BlockSpec((CONV_WIDTH, LANE), lambda j: (0, j)), pl.BlockSpec((1, LANE), lambda j: (0, j))],
        out_specs=[pl.BlockSpec((SEQ, LANE), lambda j: (0, j)), pl.BlockSpec((CONV_WIDTH, LANE), lambda j: (0, j)),
                   pl.BlockSpec((1, LANE), lambda j: (0, j))],
        out_shape=[jax.ShapeDtypeStruct((SEQ, D_CONV), f32), jax.ShapeDtypeStruct((CONV_WIDTH, D_CONV), f32),
                   jax.ShapeDtypeStruct((1, D_CONV), f32)],
        compiler_params=_params("arbitrary"),
    )(proj, d_act, conv_w, conv_b)


def _ssd_chunk_common(dt_raw, dtb, alog):
    row = lax.broadcasted_iota(jnp.int32, (CHUNK, CHUNK), 0)
    col = lax.broadcasted_iota(jnp.int32, (CHUNK, CHUNK), 1)
    tri = (row >= col).astype(f32)
    strict = (row > col).astype(f32)
    dtp = _softplus(dt_raw + dtb)
    a_row = -jnp.exp(alog)
    d_a = dtp * a_row
    cs = _hdot(tri, d_a)
    cs_last = cs[CHUNK - 1:CHUNK, :]
    return row, col, dtp, a_row, cs, cs.T, cs_last


def _seg_decay(cs, cs_t, hd, row, col):
    seg = cs[:, hd:hd + 1] - cs_t[hd:hd + 1, :]
    return jnp.where(row >= col, jnp.exp(seg), 0.0)


GROUP_W = HEADS_PER_GROUP * SSM_HEAD_DIM


def _group_indicator(g):
    j = lax.broadcasted_iota(jnp.int32, (GROUP_W, LANE), 0)
    lane = lax.broadcasted_iota(jnp.int32, (GROUP_W, LANE), 1)
    return (lane == g * HEADS_PER_GROUP + j // SSM_HEAD_DIM).astype(bf16)


def _bf16_pieces(a, n):
    pieces = []
    for _ in range(n):
        p = a.astype(bf16)
        pieces.append(p)
        a = a - p.astype(f32)
    return pieces


def _head_spread(a, ind):
    return sum(lax.dot_general(p, ind, (((1,), (1,)), ((), ())), preferred_element_type=f32) for p in _bf16_pieces(a, 3))


def _head_sums(a, ind):
    return sum(jnp.dot(p, ind, preferred_element_type=f32) for p in _bf16_pieces(a, 2))


def ssd_fwd_g(act, proj, dt_bias, a_log, d_skip, norm_g):
    zcol, dtcol = OFF_Z // D_SSM, OFF_DT // LANE

    def body(act_ref, z_ref, dt_ref, dtb_ref, alog_ref, dsk_ref, ng_ref, out_ref, ypre_ref, st_ref, state):
        c = pl.program_id(0)

        @pl.when(c == 0)
        def _():
            state[...] = jnp.zeros_like(state)

        row, col, dtp, a_row, cs, cs_t, cs_last = _ssd_chunk_common(dt_ref[...], dtb_ref[...], alog_ref[...])
        e_cs = jnp.exp(cs)
        dte = jnp.exp(cs_last - cs)
        rows8 = jnp.concatenate([jnp.exp(cs_last), dsk_ref[...], jnp.zeros((6, LANE), f32)], axis=0)
        z = z_ref[...]
        sz = z * _sigmoid(z)
        ng = ng_ref[...]
        for g in range(SSM_GROUPS):
            gs = slice(g * GROUP_W, (g + 1) * GROUP_W)
            ind = _group_indicator(g)
            xg = act_ref[:, gs]
            bg = act_ref[:, D_SSM + g * SSM_STATE:D_SSM + (g + 1) * SSM_STATE]
            cg = act_ref[:, D_SSM + D_BC + g * SSM_STATE:D_SSM + D_BC + (g + 1) * SSM_STATE]
            dt_e, e_e, dte_e = _head_spread(dtp, ind), _head_spread(e_cs, ind), _head_spread(dte, ind)
            rows_e = _head_spread(rows8, ind)
            ecl_e, dsk_e = rows_e[0:1], rows_e[1:2]
            xdt = xg * dt_e
            prev = state[g]
            st_ref[0, g] = prev
            cb = _bdot_nt(cg, bg)
            goff = _bdot(cg, prev)
            snew = _bdot_tn(bg, xdt * dte_e)
            heads = range(g * HEADS_PER_GROUP, (g + 1) * HEADS_PER_GROUP)
            ms = [cb * _seg_decay(cs, cs_t, hd, row, col) for hd in heads]
            yd = [_bdot(m, xdt[:, r * SSM_HEAD_DIM:(r + 1) * SSM_HEAD_DIM]) for r, m in enumerate(ms)]
            y = jnp.concatenate(yd, axis=1) + e_e * goff + xg * dsk_e
            state[g] = prev * ecl_e + snew
            ypre_ref[:, gs] = y
            part = y * sz[:, gs]
            out_ref[:, gs] = part * _rms(part) * ng[:, gs]

    return pl.pallas_call(
        body, name="ssd_fwd", grid=(N_CHUNKS,),
        in_specs=[pl.BlockSpec((CHUNK, D_CONV), lambda c: (c, 0)), pl.BlockSpec((CHUNK, D_SSM), lambda c: (c, zcol)),
                  pl.BlockSpec((CHUNK, LANE), lambda c: (c, dtcol)), _full((1, LANE)), _full((1, LANE)), _full((1, LANE)),
                  _full((1, D_SSM))],
        out_specs=[pl.BlockSpec((CHUNK, D_SSM), lambda c: (c, 0)), pl.BlockSpec((CHUNK, D_SSM), lambda c: (c, 0)),
                   pl.BlockSpec((1, SSM_GROUPS, SSM_STATE, GROUP_W), lambda c: (c, 0, 0, 0))],
        out_shape=[jax.ShapeDtypeStruct((SEQ, D_SSM), f32), jax.ShapeDtypeStruct((SEQ, D_SSM), f32),
                   jax.ShapeDtypeStruct((N_CHUNKS, SSM_GROUPS, SSM_STATE, GROUP_W), f32)],
        scratch_shapes=[pltpu.VMEM((SSM_GROUPS, SSM_STATE, GROUP_W), f32)],
        compiler_params=_params("arbitrary"),
    )(act, proj, proj, dt_bias, a_log, d_skip, norm_g)


def ssd_bwd_g(act, proj, ypre, states, d_out, dt_bias, a_log, d_skip, norm_g):
    zcol, dtcol = OFF_Z // D_SSM, OFF_DT // LANE

    def body(act_ref, z_ref, dt_ref, ypre_ref, st_ref, do_ref, dtb_ref, alog_ref, dsk_ref, ng_ref,
             dact_ref, ddt_ref, dz_ref, dng_ref, dpar_ref, dstate):
        i = pl.program_id(0)

        @pl.when(i == 0)
        def _():
            for ref in (dng_ref, dpar_ref, dstate):
                ref[...] = jnp.zeros_like(ref)

        row, col, dtp, a_row, cs, cs_t, cs_last = _ssd_chunk_common(dt_ref[...], dtb_ref[...], alog_ref[...])
        upper = (row <= col).astype(f32)
        lane = lax.broadcasted_iota(jnp.int32, (CHUNK, LANE), 1)
        rowl = lax.broadcasted_iota(jnp.int32, (CHUNK, LANE), 0)
        e_cs = jnp.exp(cs)
        dte = jnp.exp(cs_last - cs)
        ecl = jnp.exp(cs_last)
        rows8 = jnp.concatenate([ecl, dsk_ref[...], jnp.zeros((6, LANE), f32)], axis=0)
        z = z_ref[...]
        sgz = _sigmoid(z)
        sz = z * sgz
        ng = ng_ref[...]
        ddt_mat = jnp.zeros((CHUNK, LANE), f32)
        dcs_mat = jnp.zeros((CHUNK, LANE), f32)
        dcs_t = jnp.zeros((LANE, CHUNK), f32)
        dcsl_row = jnp.zeros((1, LANE), f32)
        dd_row = jnp.zeros((1, LANE), f32)
        for g in range(SSM_GROUPS):
            gs = slice(g * GROUP_W, (g + 1) * GROUP_W)
            bsl = slice(D_SSM + g * SSM_STATE, D_SSM + (g + 1) * SSM_STATE)
            csl = slice(D_SSM + D_BC + g * SSM_STATE, D_SSM + D_BC + (g + 1) * SSM_STATE)
            ind = _group_indicator(g)
            y = ypre_ref[:, gs]
            part = y * sz[:, gs]
            r = _rms(part)
            yhat = part * r
            d_o = do_ref[:, gs]
            dng_ref[:, gs] += jnp.sum(d_o * yhat, axis=0, keepdims=True)
            dyz = _rms_bwd(d_o, yhat, r, ng[:, gs])
            dy = dyz * sz[:, gs]
            dz_ref[:, gs] = dyz * y * (sgz[:, gs] * (1.0 + z[:, gs] * (1.0 - sgz[:, gs])))

            xg = act_ref[:, gs]
            bg = act_ref[:, bsl]
            cg = act_ref[:, csl]
            dt_e, e_e, dte_e = _head_spread(dtp, ind), _head_spread(e_cs, ind), _head_spread(dte, ind)
            rows_e = _head_spread(rows8, ind)
            ecl_e, dsk_e = rows_e[0:1], rows_e[1:2]
            xdt = xg * dt_e
            prev = st_ref[0, g]
            dh = dstate[g]
            heads = range(g * HEADS_PER_GROUP, (g + 1) * HEADS_PER_GROUP)
            hsl = [slice(r_ * SSM_HEAD_DIM, (r_ + 1) * SSM_HEAD_DIM) for r_ in range(HEADS_PER_GROUP)]
            cb = _bdot_nt(cg, bg)
            lms = [_seg_decay(cs, cs_t, hd, row, col) for hd in heads]
            ms = [cb * lm for lm in lms]
            gmat = _bdot(cg, prev)
            dgm = dy * e_e
            dcg = _bdot_nt(dgm, prev)
            dprev = _bdot_tn(cg, dgm)
            dbg = _bdot_nt(xdt * dte_e, dh)
            dw = _bdot(bg, dh)
            dms = [_bdot_nt(dy[:, s_], xdt[:, s_]) for s_ in hsl]
            dxdts = [_bdot_tn(m, dy[:, s_]) for m, s_ in zip(ms, hsl)]
            dxdt = jnp.concatenate(dxdts, axis=1) + dw * dte_e
            dact_ref[:, gs] = dy * dsk_e + dxdt * dt_e
            dstate[g] = dprev + dh * ecl_e
            dcb = jnp.zeros((CHUNK, CHUNK), f32)
            for hd, dm, lm, m in zip(heads, dms, lms, ms):
                dcb = dcb + dm * lm
                dseg = dm * m
                dcs_mat = dcs_mat + jnp.where(lane == hd, jnp.sum(dseg, axis=1, keepdims=True), 0.0)
                dcs_t = jnp.where(row == hd, jnp.sum(dseg, axis=0, keepdims=True), dcs_t)
            dact_ref[:, bsl] = dbg + _bdot_tn(dcb, cg)
            dact_ref[:, csl] = dcg + _bdot(dcb, bg)
            ddte = _head_sums(dw * xdt, ind) * dte
            dcs_mat = dcs_mat + _head_sums(dy * gmat, ind) * e_cs - ddte
            ddt_mat = ddt_mat + _head_sums(dxdt * xg, ind)
            dcsl_row = (dcsl_row + jnp.sum(ddte, axis=0, keepdims=True)
                        + jnp.sum(_head_sums(dh * prev, ind), axis=0, keepdims=True) * ecl)
            dd_row = dd_row + jnp.sum(_head_sums(dy * xg, ind), axis=0, keepdims=True)
        dcs_mat = dcs_mat - dcs_t.T + jnp.where(rowl == CHUNK - 1, dcsl_row, 0.0)
        dda = _hdot(upper, dcs_mat)
        ddt_mat = ddt_mat + dda * a_row
        da_row = jnp.sum(dda * dtp, axis=0, keepdims=True)
        ddt_raw = ddt_mat * _sigmoid(dt_ref[...] + dtb_ref[...])
        ddt_ref[...] = ddt_raw
        dpar_ref[0:1, :] += jnp.sum(ddt_raw, axis=0, keepdims=True)
        dpar_ref[1:2, :] += da_row * a_row
        dpar_ref[2:3, :] += dd_row

    blk = lambda i: N_CHUNKS - 1 - i
    return pl.pallas_call(
        body, name="ssd_bwd", grid=(N_CHUNKS,),
        in_specs=[pl.BlockSpec((CHUNK, D_CONV), lambda i: (blk(i), 0)), pl.BlockSpec((CHUNK, D_SSM), lambda i: (blk(i), zcol)),
                  pl.BlockSpec((CHUNK, LANE), lambda i: (blk(i), dtcol)), pl.BlockSpec((CHUNK, D_SSM), lambda i: (blk(i), 0)),
                  pl.BlockSpec((1, SSM_GROUPS, SSM_STATE, GROUP_W), lambda i: (blk(i), 0, 0, 0)),
                  pl.BlockSpec((CHUNK, D_SSM), lambda i: (blk(i), 0)),
                  _full((1, LANE)), _full((1, LANE)), _full((1, LANE)), _full((1, D_SSM))],
        out_specs=[pl.BlockSpec((CHUNK, D_CONV), lambda i: (blk(i), 0)), pl.BlockSpec((CHUNK, LANE), lambda i: (blk(i), 0)),
                   pl.BlockSpec((CHUNK, D_SSM), lambda i: (blk(i), 0)), _full((1, D_SSM)), _full((8, LANE))],
        out_shape=[jax.ShapeDtypeStruct((SEQ, D_CONV), f32), jax.ShapeDtypeStruct((SEQ, LANE), f32),
                   jax.ShapeDtypeStruct((SEQ, D_SSM), f32), jax.ShapeDtypeStruct((1, D_SSM), f32),
                   jax.ShapeDtypeStruct((8, LANE), f32)],
        scratch_shapes=[pltpu.VMEM((SSM_GROUPS, SSM_STATE, GROUP_W), f32)],
        compiler_params=_params("arbitrary"),
    )(act, proj, proj, ypre, states, d_out, dt_bias, a_log, d_skip, norm_g)


def out_fwd(x, attn, ssm, w_out, tm=512):
    def body(x_ref, a_ref, s_ref, w_ref, o_ref):
        o_ref[...] = x_ref[...] + _bdot(a_ref[...], w_ref[:D_ATTN, :]) + _bdot(s_ref[...], w_ref[D_ATTN:, :])

    tok = lambda w_: pl.BlockSpec((tm, w_), lambda i: (i, 0))
    return pl.pallas_call(
        body, name="out_fwd", grid=(SEQ // tm,),
        in_specs=[tok(D_MODEL), tok(D_ATTN), tok(D_SSM), _full((D_MODEL, D_MODEL))],
        out_specs=tok(D_MODEL), out_shape=jax.ShapeDtypeStruct((SEQ, D_MODEL), f32),
        compiler_params=_params("arbitrary"),
    )(x, attn, ssm, w_out)


def out_bwd(dx1, attn, ssm, w_out, tm=512):
    nt = SEQ // tm

    def body(d_ref, a_ref, s_ref, w_ref, da_ref, ds_ref, dw16_ref, dw_ref):
        i = pl.program_id(0)

        @pl.when(i == 0)
        def _():
            dw_ref[...] = jnp.zeros_like(dw_ref)

        d = d_ref[...].astype(bf16)
        dcat = _bdot_nt(d, w_ref[...])
        da_ref[...] = dcat[:, :D_ATTN]
        ds_ref[...] = dcat[:, D_ATTN:]
        dw_ref[:D_ATTN, :] += _bdot_tn(a_ref[...], d)
        dw_ref[D_ATTN:, :] += _bdot_tn(s_ref[...], d)

        @pl.when(i == nt - 1)
        def _():
            dw16_ref[...] = dw_ref[...].astype(bf16)

    tok = lambda w_: pl.BlockSpec((tm, w_), lambda i: (i, 0))
    return pl.pallas_call(
        body, name="out_bwd", grid=(nt,),
        in_specs=[tok(D_MODEL), tok(D_ATTN), tok(D_SSM), _resident((D_MODEL, D_MODEL))],
        out_specs=[tok(D_ATTN), tok(D_SSM), _resident((D_MODEL, D_MODEL))],
        out_shape=[jax.ShapeDtypeStruct((SEQ, D_ATTN), f32), jax.ShapeDtypeStruct((SEQ, D_SSM), f32),
                   jax.ShapeDtypeStruct((D_MODEL, D_MODEL), bf16)],
        scratch_shapes=[pltpu.VMEM((D_MODEL, D_MODEL), f32)],
        compiler_params=_params("arbitrary"),
    )(dx1, attn, ssm, w_out)


MLP_SUB = 256


def mlp_fwd(x1, g, w_up, w_down, tm=1024):
    def body(x_ref, g_ref, wu_ref, wd_ref, o_ref, u_ref, h_scr):
        j = pl.program_id(1)

        @pl.when(j == 0)
        def _():
            xv = x_ref[...]
            h_scr[...] = (xv * _rms(xv) * g_ref[...]).astype(bf16)
            o_ref[...] = xv

        for r in range(tm // MLP_SUB):
            rows = slice(r * MLP_SUB, (r + 1) * MLP_SUB)
            u = jnp.dot(h_scr[rows, :], wu_ref[...], preferred_element_type=f32)
            u_ref[rows, :] = u.astype(bf16)
            a = jnp.square(jnp.maximum(u, 0.0))
            o_ref[rows, :] += _bdot(a, wd_ref[...])

    return pl.pallas_call(
        body, name="mlp_fwd", grid=(SEQ // tm, N_CHIPS),
        in_specs=[pl.BlockSpec((tm, D_MODEL), lambda i, j: (i, 0)), _full((1, D_MODEL)),
                  pl.BlockSpec((None, D_MODEL, FF_TILE), lambda i, j: (j, 0, 0)),
                  pl.BlockSpec((None, FF_TILE, D_MODEL), lambda i, j: (j, 0, 0))],
        out_specs=[pl.BlockSpec((tm, D_MODEL), lambda i, j: (i, 0)), pl.BlockSpec((tm, FF_TILE), lambda i, j: (i, j))],
        out_shape=[jax.ShapeDtypeStruct((SEQ, D_MODEL), f32), jax.ShapeDtypeStruct((SEQ, D_FF), bf16)],
        scratch_shapes=[pltpu.VMEM((tm, D_MODEL), bf16)],
        compiler_params=_params("arbitrary", "arbitrary"),
    )(x1, g, w_up, w_down)


def mlp_bwd_data(dx2, u, x1, g, w_up, w_down, tm=1024):
    def body(d_ref, u_ref, x_ref, g_ref, wu_ref, wd_ref, dx_ref, du_ref, dg_ref, dh_scr):
        i, j = pl.program_id(0), pl.program_id(1)

        @pl.when(jnp.logical_and(i == 0, j == 0))
        def _():
            dg_ref[...] = jnp.zeros_like(dg_ref)

        @pl.when(j == 0)
        def _():
            dh_scr[...] = jnp.zeros_like(dh_scr)

        for r in range(tm // MLP_SUB):
            rows = slice(r * MLP_SUB, (r + 1) * MLP_SUB)
            da = _bdot_nt(d_ref[rows, :], wd_ref[...])
            du = (da * (2.0 * jnp.maximum(u_ref[rows, :].astype(f32), 0.0))).astype(bf16)
            du_ref[rows, :] = du
            dh_scr[rows, :] += _bdot_nt(du, wu_ref[...])

        @pl.when(j == N_CHIPS - 1)
        def _():
            xv = x_ref[...]
            r = _rms(xv)
            xhat = xv * r
            dh = dh_scr[...]
            dg_ref[...] += jnp.sum(dh * xhat, axis=0, keepdims=True)
            dx_ref[...] = d_ref[...] + _rms_bwd(dh, xhat, r, g_ref[...])

    return pl.pallas_call(
        body, name="mlp_bwd_data", grid=(SEQ // tm, N_CHIPS),
        in_specs=[pl.BlockSpec((tm, D_MODEL), lambda i, j: (i, 0)), pl.BlockSpec((tm, FF_TILE), lambda i, j: (i, j)),
                  pl.BlockSpec((tm, D_MODEL), lambda i, j: (i, 0)), _full((1, D_MODEL)),
                  pl.BlockSpec((None, D_MODEL, FF_TILE), lambda i, j: (j, 0, 0)),
                  pl.BlockSpec((None, FF_TILE, D_MODEL), lambda i, j: (j, 0, 0))],
        out_specs=[pl.BlockSpec((tm, D_MODEL), lambda i, j: (i, 0)), pl.BlockSpec((tm, FF_TILE), lambda i, j: (i, j)),
                   _full((1, D_MODEL))],
        out_shape=[jax.ShapeDtypeStruct((SEQ, D_MODEL), f32), jax.ShapeDtypeStruct((SEQ, D_FF), bf16),
                   jax.ShapeDtypeStruct((1, D_MODEL), f32)],
        scratch_shapes=[pltpu.VMEM((tm, D_MODEL), f32)],
        compiler_params=_params("arbitrary", "arbitrary"),
    )(dx2, u, x1, g, w_up, w_down)


def mlp_bwd_weights(dx2, u, du, x1, g, tm=512):
    nt = SEQ // tm

    def body(d_ref, u_ref, du_ref, x_ref, g_ref, dwu16_ref, dwd16_ref, h_scr, d_scr, dwu_ref, dwd_ref):
        j, i = pl.program_id(0), pl.program_id(1)

        @pl.when(j == 0)
        def _():
            xv = x_ref[...]
            h_scr[i] = (xv * _rms(xv) * g_ref[...]).T.astype(bf16)
            d_scr[i] = d_ref[...].astype(bf16)

        @pl.when(i == 0)
        def _():
            dwu_ref[...] = jnp.zeros_like(dwu_ref)
            dwd_ref[...] = jnp.zeros_like(dwd_ref)

        dwu_ref[...] += jnp.dot(h_scr[i], du_ref[...], preferred_element_type=f32)
        a = jnp.square(jnp.maximum(u_ref[...].astype(f32), 0.0))
        dwd_ref[...] += _bdot_tn(a, d_scr[i])

        @pl.when(i == nt - 1)
        def _():
            dwu16_ref[...] = dwu_ref[...].astype(bf16)
            dwd16_ref[...] = dwd_ref[...].astype(bf16)

    up = pl.BlockSpec((None, D_MODEL, FF_TILE), lambda j, i: (j, 0, 0))
    down = pl.BlockSpec((None, FF_TILE, D_MODEL), lambda j, i: (j, 0, 0))
    first_pass = pl.BlockSpec((tm, D_MODEL), lambda j, i: (jnp.where(j == 0, i, nt - 1), 0))
    return pl.pallas_call(
        body, name="mlp_bwd_weights", grid=(N_CHIPS, nt),
        in_specs=[first_pass, pl.BlockSpec((tm, FF_TILE), lambda j, i: (i, j)),
                  pl.BlockSpec((tm, FF_TILE), lambda j, i: (i, j)), first_pass, _full((1, D_MODEL))],
        out_specs=[up, down],
        out_shape=[jax.ShapeDtypeStruct((N_CHIPS, D_MODEL, FF_TILE), bf16), jax.ShapeDtypeStruct((N_CHIPS, FF_TILE, D_MODEL), bf16)],
        scratch_shapes=[pltpu.VMEM((nt, D_MODEL, tm), bf16), pltpu.VMEM((nt, tm, D_MODEL), bf16),
                        pltpu.VMEM((D_MODEL, FF_TILE), f32), pltpu.VMEM((FF_TILE, D_MODEL), f32)],
        compiler_params=_params("arbitrary", "arbitrary"),
    )(dx2, u, du, x1, g)


def loss_head(y, target, tm=512):
    def body(y_ref, t_ref, dy_ref, l_ref):
        @pl.when(pl.program_id(0) == 0)
        def _():
            l_ref[...] = jnp.zeros_like(l_ref)

        d = y_ref[...] - t_ref[...]
        dy_ref[...] = d * (1.0 / D_MODEL)
        part = jnp.sum(jnp.mean(d * d, axis=-1, keepdims=True), axis=0, keepdims=True)
        l_ref[...] += 0.5 * part

    tok = pl.BlockSpec((tm, D_MODEL), lambda i: (i, 0))
    return pl.pallas_call(
        body, name="loss_head", grid=(SEQ // tm,), in_specs=[tok, tok], out_specs=[tok, _full((1, 1))],
        out_shape=[jax.ShapeDtypeStruct((SEQ, D_MODEL), f32), jax.ShapeDtypeStruct((1, 1), f32)],
        compiler_params=_params("arbitrary"),
    )(y, target)


def _pad_lane(v):
    return jnp.pad(v, (0, LANE - v.shape[0]))[None, :]


def local_step(x, target, w, prov):
    bucket = jnp.asarray(_bucket_table().T)
    bias = bias_build(w["rel_bias"], bucket)
    saved = []
    for l in range(DEPTH):
        g_mix = w["mix_norm_g"][l][None, :] + prov.stage(("begin", l), x)
        w_in = prov.w_in(l, x)
        proj = in_fwd(x, g_mix, w_in)
        conv_b = w["conv_b"][l][None, :]
        act = conv_fwd(proj, w["conv_w"][l], conv_b)
        dtb = _pad_lane(w["dt_bias"][l]) + prov.stage(("mid", l), act)
        alog, dsk = _pad_lane(w["a_log"][l]), _pad_lane(w["d_skip"][l])
        ng = w["ssm_norm_g"][l][None, :]
        ssm, ypre, states = ssd_fwd_g(act, proj, dtb, alog, dsk, ng)
        qg, kg = w["q_gain"][l][:, None] + 0.0 * ssm[:1, :1], w["k_gain"][l][None, :]
        attn = attn_fwd_t(proj, qg, kg, w["sinks"][l], bias)
        tok = prov.stage(("pre_out", l), attn)
        w_out = prov.w_out(l, attn) + jnp.asarray(tok, bf16)
        x1 = out_fwd(x, attn, ssm, w_out)
        g_mlp = w["mlp_norm_g"][l][None, :] + prov.stage(("pre_mlp", l), x1)
        w_up, w_down = prov.mlp(l, x1)
        x2, u = mlp_fwd(x1, g_mlp, w_up, w_down)
        saved.append(dict(x=x, proj=proj, attn=attn, act=act, ssm=ssm, ypre=ypre, states=states, x1=x1, u=u,
                          g_mix=g_mix, qg=qg, kg=kg, conv_b=conv_b, dtb=dtb, alog=alog, dsk=dsk, ng=ng, g_mlp=g_mlp,
                          w_in=w_in, w_out=w_out, w_up=w_up, w_down=w_down))
        x = x2
    dx, loss = loss_head(x, target)
    grads = [None] * DEPTH
    dbands = [None] * DEPTH
    tok = 0.0
    for l in reversed(range(DEPTH)):
        s = saved[l]
        g_mlp = s["g_mlp"] + tok
        dx1, du, dg_mlp = mlp_bwd_data(dx, s["u"], s["x1"], g_mlp, s["w_up"], s["w_down"])
        dw_up, dw_down = mlp_bwd_weights(dx, s["u"], du, s["x1"], g_mlp)
        tok = prov.grads(("mlp", l), dict(w_up=dw_up, w_down=dw_down), dx1)
        dattn, dssm, dw_out = out_bwd(dx1, s["attn"], s["ssm"], s["w_out"])
        dact, ddt, dz, dng, dpar = ssd_bwd_g(s["act"], s["proj"], s["ypre"], s["states"], dssm, s["dtb"] + tok, s["alog"],
                                           s["dsk"], s["ng"])
        conv_b = s["conv_b"] + prov.stage(("bwd_mid", l), dact)
        dxbc, dconv_w, dconv_b = conv_bwd(s["proj"], dact, w["conv_w"][l], conv_b)
        dq, dk, dv, dband, dsink, dqg, dkg = attn_bwd_t(s["proj"], dattn, s["qg"], s["kg"], w["sinks"][l], bias)
        dbands[l] = dband
        g_mix = s["g_mix"]
        if l == 0:
            d_rel = bias_bwd(dbands[0], dbands[1], bucket)
            g_mix = g_mix + 0.0 * d_rel[:1, :1]
        dx, dw_in, dg_mix = in_bwd(dq, dz, dxbc, dk, dv, ddt, s["x"], g_mix, s["w_in"], dx1)
        tok = prov.grads(("mix", l), dict(w_in=split_w_in_grad(dw_in), w_out=dw_out), dx)
        grads[l] = dict(mix_norm_g=dg_mix[0], q_gain=dqg[:, 0], k_gain=dkg[0], sinks=dsink[:, 0],
                        conv_w=dconv_w, conv_b=dconv_b[0], dt_bias=dpar[0, :SSM_HEADS], a_log=dpar[1, :SSM_HEADS],
                        d_skip=dpar[2, :SSM_HEADS], ssm_norm_g=dng[0], mlp_norm_g=dg_mlp[0])
    out = {k: jnp.stack([grads[l][k] for l in range(DEPTH)]) for k in grads[0]}
    out["rel_bias"] = d_rel[:, :N_Q_HEADS]
    return loss, dx, out, tok


MESH = pl.DeviceIdType.MESH
HBM = pl.BlockSpec(memory_space=pltpu.HBM)
N_DEVICES = 8


def _coords():
    return lax.axis_index("x"), lax.axis_index("y"), lax.axis_index("c")


def _peer_chips(x, y):
    return [(1 - x, y), (x, 1 - y), (1 - x, 1 - y)]


def _remote(src, dst, send_sem, recv_sem, device):
    return pltpu.make_async_remote_copy(src_ref=src, dst_ref=dst, send_sem=send_sem, recv_sem=recv_sem,
                                        device_id=device, device_id_type=MESH)


SEM = pl.BlockSpec(memory_space=pltpu.SEMAPHORE)
ANY = pl.BlockSpec(memory_space=pl.ANY)
DATAFLOW = pltpu.SideEffectType.DATAFLOW_SIDE_EFFECTING


def _gather_copies(kind, src_refs, land_refs, ssem, rsem):
    x, y, c = _coords()
    k_me = 2 * x + y
    n = len(land_refs)
    cps = []
    for p, land in enumerate(land_refs):
        hr = land.shape[1] // 2
        rows = pl.ds(c * hr, hr)
        for j, chip in enumerate(_peer_chips(x, y)):
            i = 3 * p + j
            if kind == "ici":
                cps.append(_remote(src_refs[p].at[rows, :], land.at[k_me, rows, :], ssem.at[i], rsem.at[i], (*chip, c)))
            else:
                got = land.at[2 * chip[0] + chip[1], rows, :]
                cps.append(_remote(got, got, ssem.at[i], rsem.at[i], (x, y, 1 - c)))
        if kind == "relay":
            cps.append(_remote(src_refs[p], land.at[k_me], ssem.at[3 * n + p], rsem.at[3 * n + p], (x, y, 1 - c)))
    return cps


def gather_now(srcs, conv):
    n = len(srcs)

    def body(*refs):
        src_refs, conv_ref = refs[:n], refs[n]
        lands, gconv = refs[n + 1:2 * n + 1], refs[2 * n + 1]
        ssem, rsem, fsem, frsem, csem, crsem = refs[2 * n + 2:]
        x, y, c = _coords()
        k_me = 2 * x + y
        targets = [(*chip, c) for chip in _peer_chips(x, y)] + [(x, y, 1 - c)]
        ici = _gather_copies("ici", src_refs, lands, ssem, rsem)
        relay = _gather_copies("relay", src_refs, lands, fsem, frsem)
        passed = [cp for i, cp in enumerate(relay) if i % 4 != 3]
        own = relay[3::4]
        conv_cps = [_remote(conv_ref, gconv.at[k_me], csem.at[j], crsem.at[j], t) for j, t in enumerate(targets)]
        for cp in ici + conv_cps + own:
            cp.start()
        for cp, fw in zip(ici, passed):
            cp.wait_recv()
            fw.start()
        for cp in conv_cps + relay:
            cp.wait_recv()
        for cp in ici + relay + conv_cps:
            cp.wait_send()

    out_shape = [jax.ShapeDtypeStruct((N_CHIPS,) + s.shape, s.dtype) for s in srcs]
    out_shape.append(jax.ShapeDtypeStruct((N_CHIPS,) + conv.shape, conv.dtype))
    sems = lambda k: pltpu.SemaphoreType.DMA((k,))
    return pl.pallas_call(
        body, name="gather_now", out_shape=out_shape, in_specs=[HBM] * (n + 1), out_specs=[HBM] * (n + 1),
        scratch_shapes=[sems(3 * n), sems(3 * n), sems(4 * n), sems(4 * n), sems(N_CHIPS), sems(N_CHIPS)],
    )(*srcs, conv)


def _gather_maker(kind, n_src):
    def make(refs, ssem, rsem):
        cps = _gather_copies(kind, refs[:n_src], refs[n_src:], ssem, rsem)
        return cps, cps
    return make


def _scatter_maker(n):
    def make(refs, ssem, rsem):
        x, y, c = _coords()
        k_me = 2 * x + y
        sends, arrivals = [], []
        for p in range(n):
            src, land = refs[p], refs[n + p]
            sends.append(_remote(src.at[k_me, 1 - c], land.at[0], ssem.at[7 * p], rsem.at[7 * p], (x, y, 1 - c)))
            for j, chip in enumerate(_peer_chips(x, y)):
                for cc in range(2):
                    sends.append(_remote(src.at[2 * chip[0] + chip[1], cc], land.at[1 + 2 * j + c],
                                         ssem.at[7 * p + 1 + 2 * j + cc], rsem.at[7 * p + 1 + 2 * j + c], (*chip, cc)))
            for s in range(7):
                arrivals.append(_remote(land.at[s], land.at[s], ssem.at[7 * p + s], rsem.at[7 * p + s], (x, y, 1 - c)))
        return sends, arrivals
    return make


def _share_maker(n):
    def make(refs, ssem, rsem):
        x, y, c = _coords()
        sends = [_remote(refs[p].at[c], refs[p].at[c], ssem.at[p], rsem.at[p], (x, y, 1 - c)) for p in range(n)]
        arrivals = [_remote(refs[p].at[1 - c], refs[p].at[1 - c], ssem.at[p], rsem.at[p], (x, y, 1 - c)) for p in range(n)]
        return sends, arrivals
    return make


def split_start(name, make, n_sems, operands, after):
    n = len(operands)

    def body(*refs):
        ssem, rsem, token = refs[n + 1], refs[n + 2], refs[-1]
        for cp in make(refs[:n], ssem, rsem)[0]:
            cp.start()
        token[...] = jnp.zeros_like(token)

    ops = [pltpu.with_memory_space_constraint(a, pltpu.HBM) for a in operands]
    outs = pl.pallas_call(
        body, name=name,
        out_shape=(pltpu.SemaphoreType.DMA((n_sems,)), pltpu.SemaphoreType.DMA((n_sems,)),
                   *[pltpu.HBM(a.shape, a.dtype) for a in ops], jax.ShapeDtypeStruct((8, LANE), f32)),
        in_specs=[HBM] * n + [ANY], out_specs=(SEM, SEM, *[HBM] * n, pl.BlockSpec(memory_space=pltpu.VMEM)),
        input_output_aliases={i: 2 + i for i in range(n)},
        compiler_params=pltpu.CompilerParams(has_side_effects=DATAFLOW),
    )(*ops, after)
    return dict(name=name, make=make, ssem=outs[0], rsem=outs[1], operands=outs[2:2 + n], token=outs[-1][0, 0])


def split_wait(handle, after):
    n = len(handle["operands"])

    def body(*refs):
        sends, arrivals = handle["make"](refs[:n], refs[n], refs[n + 1])
        for cp in sends:
            cp.wait_send()
        for cp in arrivals:
            cp.wait_recv()

    outs = pl.pallas_call(
        body, name=handle["name"].replace("start", "wait"),
        out_shape=tuple(pltpu.HBM(a.shape, a.dtype) for a in handle["operands"]),
        in_specs=[HBM] * n + [SEM, SEM, ANY], out_specs=tuple([HBM] * n),
        input_output_aliases={i: i for i in range(n)},
        compiler_params=pltpu.CompilerParams(has_side_effects=DATAFLOW),
    )(*handle["operands"], handle["ssem"], handle["rsem"], after)
    return list(outs)


def piece_sum(g, recv, kc_arr):
    _, _, rb, cc = g.shape
    tr = min(256, rb)

    def body(kc_ref, g_ref, r_ref, o_ref):
        acc = g_ref[...].astype(f32)
        for s in range(7):
            acc = acc + r_ref[s].astype(f32)
        o_ref[...] = acc

    return pl.pallas_call(
        body, name="piece_sum",
        grid_spec=pltpu.PrefetchScalarGridSpec(
            num_scalar_prefetch=1, grid=(rb // tr,),
            in_specs=[pl.BlockSpec((None, None, tr, cc), lambda r, kc: (kc[0], kc[1], r, 0)),
                      pl.BlockSpec((7, tr, cc), lambda r, kc: (0, r, 0))],
            out_specs=pl.BlockSpec((None, tr, cc), lambda r, kc: (kc[1], r, 0))),
        out_shape=jax.ShapeDtypeStruct((2, rb, cc), f32),
        compiler_params=_params("arbitrary"),
    )(kc_arr, g, recv)


def small_all_reduce(vec):
    def body(v_ref, o_ref, gat, ssem, rsem):
        x, y, c = _coords()
        me = 4 * x + 2 * y + c
        gat[me] = v_ref[...]
        sends = []
        for t in range(1, N_DEVICES):
            peer = (x ^ (t >> 2), y ^ ((t >> 1) & 1), c ^ (t & 1))
            cp = _remote(v_ref, gat.at[me], ssem.at[t - 1], rsem.at[t - 1], peer)
            cp.start()
            sends.append(cp)
        for t in range(1, N_DEVICES):
            peer = (x ^ (t >> 2), y ^ ((t >> 1) & 1), c ^ (t & 1))
            slot = gat.at[4 * peer[0] + 2 * peer[1] + peer[2]]
            _remote(slot, slot, ssem.at[t - 1], rsem.at[t - 1], peer).wait_recv()
        for cp in sends:
            cp.wait_send()
        acc = gat[0]
        for d in range(1, N_DEVICES):
            acc = acc + gat[d]
        o_ref[...] = acc

    return pl.pallas_call(
        body, name="small_all_reduce", out_shape=jax.ShapeDtypeStruct(vec.shape, vec.dtype),
        in_specs=[pl.BlockSpec(memory_space=pltpu.VMEM)], out_specs=pl.BlockSpec(memory_space=pltpu.VMEM),
        scratch_shapes=[pltpu.VMEM((N_DEVICES,) + vec.shape, vec.dtype), pltpu.SemaphoreType.DMA((N_DEVICES - 1,)),
                        pltpu.SemaphoreType.DMA((N_DEVICES - 1,))],
    )(vec)


def _adamw_math(w, g, m, v):
    m_new = ADAM_B1 * m + (1.0 - ADAM_B1) * g
    v_new = ADAM_B2 * v + (1.0 - ADAM_B2) * jnp.square(g)
    m_hat = m_new / (1.0 - ADAM_B1 ** ADAM_STEP)
    v_hat = v_new / (1.0 - ADAM_B2 ** ADAM_STEP)
    delta = -ADAM_LR * (m_hat / (jnp.sqrt(v_hat) + ADAM_EPS) + ADAM_WD * w)
    return delta, m_new, v_new


def adamw_shard(w, g0, g1, m, v):
    depth, rows, cols = w.shape
    half = rows // 2
    tr = min(256, half)
    nr = half // tr

    def body(w_ref, g0_ref, g1_ref, m_ref, v_ref, go_ref, d_ref, nm_ref, nv_ref):
        gv = jnp.where(pl.program_id(0) == 0, g0_ref[...], g1_ref[...])
        go_ref[...] = gv
        d_ref[...], nm_ref[...], nv_ref[...] = _adamw_math(w_ref[...], gv, m_ref[...], v_ref[...])

    spec = pl.BlockSpec((None, tr, cols), lambda l, h, r: (l, h * nr + r, 0))
    g0spec = pl.BlockSpec((None, tr, cols), lambda l, h, r: (jnp.where(l == 0, h, 1), jnp.where(l == 0, r, nr - 1), 0))
    g1spec = pl.BlockSpec((None, tr, cols), lambda l, h, r: (jnp.where(l == 1, h, 0), jnp.where(l == 1, r, 0), 0))
    return pl.pallas_call(
        body, name="adamw_shard", grid=(depth, 2, nr), in_specs=[spec, g0spec, g1spec, spec, spec], out_specs=[spec] * 4,
        out_shape=[jax.ShapeDtypeStruct(w.shape, f32)] * 4,
        compiler_params=_params("arbitrary", "arbitrary", "arbitrary"),
    )(w, g0, g1, m, v)


def adamw_cols(w, g, m, v, tc=34):
    cols, depth, rows = w.shape

    def body(w_ref, g_ref, m_ref, v_ref, d_ref, nm_ref, nv_ref):
        d_ref[...], nm_ref[...], nv_ref[...] = _adamw_math(w_ref[...], g_ref[...], m_ref[...], v_ref[...])

    spec = pl.BlockSpec((tc, depth, rows), lambda i: (i, 0, 0))
    return pl.pallas_call(
        body, name="adamw_cols", grid=(cols // tc,), in_specs=[spec] * 4, out_specs=[spec] * 3,
        out_shape=[jax.ShapeDtypeStruct(w.shape, f32)] * 3,
        compiler_params=_params("arbitrary"),
    )(w, g, m, v)


def adamw_small(w, g, m, v):
    def body(w_ref, g_ref, m_ref, v_ref, d_ref, nm_ref, nv_ref):
        d_ref[...], nm_ref[...], nv_ref[...] = _adamw_math(w_ref[...], g_ref[...], m_ref[...], v_ref[...])

    return pl.pallas_call(
        body, name="adamw_small", out_shape=[jax.ShapeDtypeStruct(w.shape, f32)] * 3,
    )(w, g, m, v)


WEIGHTS = ("mix_norm_g", "w_in", "q_gain", "k_gain", "sinks", "rel_bias", "conv_w", "conv_b", "dt_bias", "a_log", "d_skip",
           "ssm_norm_g", "w_out", "mlp_norm_g", "w_up", "w_down")
BIG = ("w_in", "w_out", "w_up", "w_down")
SMALL = tuple(n for n in WEIGHTS if n not in BIG)
PACK_COLS = 1024
PACK_ROWS = 16


def _pack(named, last=None):
    flat = jnp.concatenate([named[n].reshape(-1) for n in SMALL])
    tail = jnp.zeros((1,), f32) if last is None else last.reshape(1)
    pad = jnp.zeros((PACK_ROWS * PACK_COLS - flat.shape[0] - 1,), f32)
    return jnp.concatenate([flat, pad, tail]).reshape(PACK_ROWS, PACK_COLS)


def _unpack(buf, shapes):
    flat = buf.reshape(-1)
    out, at = {}, 0
    for n in SMALL:
        size = int(np.prod(shapes[n]))
        out[n] = flat[at:at + size].reshape(shapes[n])
        at += size
    return out


class _Exchange:
    GROUPS = {"A": (("w_up", 0), ("w_down", 0)), "B": (("w_in", 1), ("w_out", 1)), "C": (("w_up", 1), ("w_down", 1))}
    ICI_AT = {("mid", 0): "B", ("pre_out", 0): "C"}
    RELAY_AT = {("pre_out", 0): "A", ("pre_mlp", 0): "B", ("mid", 1): "C"}
    LAST = ("mix", 0)
    IN_FLIGHT = 2

    def __init__(self, wts, kc_arr):
        self.wts, self.kc_arr = wts, kc_arr
        self.own = {(n, l): wts[n][l].astype(bf16) for n in BIG for l in range(DEPTH)}
        now = gather_now([self.own["w_in", 0], self.own["w_out", 0]], wts["conv_w"])
        self.ready = {("w_in", 0): now[0], ("w_out", 0): now[1]}
        self.conv_w = jnp.transpose(now[2], (1, 2, 0, 3)).reshape(DEPTH, CONV_WIDTH, D_CONV)
        self.ici, self.relay = {}, {}
        self.scatter, self.share, self.reduced = [], [], {}
        self._start_ici("A", now[2])

    def _start_ici(self, g, after):
        srcs = [self.own[p] for p in self.GROUPS[g]]
        lands = [lax.empty((N_CHIPS,) + s.shape, s.dtype) for s in srcs]
        self.ici[g] = split_start("gather%s_ici_start" % g, _gather_maker("ici", len(srcs)), 3 * len(srcs), srcs + lands,
                                  after)
        return self.ici[g]["token"]

    def stage(self, name, after):
        if name == ("begin", 0):
            return self.ici["A"]["token"]
        tok = 0.0
        g = self.RELAY_AT.get(name)
        if g is not None:
            n = len(self.GROUPS[g])
            self.relay[g] = split_start("gather%s_relay_start" % g, _gather_maker("relay", n), 4 * n,
                                        split_wait(self.ici[g], after), after)
            tok = self.relay[g]["token"]
        if name in self.ICI_AT:
            tok = tok + self._start_ici(self.ICI_AT[name], after)
        return tok

    def _get(self, piece, after):
        if piece not in self.ready:
            g = [k for k, pieces in self.GROUPS.items() if piece in pieces][0]
            lands = split_wait(self.relay[g], after)[len(self.GROUPS[g]):]
            self.ready.update(zip(self.GROUPS[g], lands))
        return self.ready[piece]

    def w_in(self, l, after):
        return align_w_in(self._get(("w_in", l), after))

    def w_out(self, l, after):
        return self._get(("w_out", l), after).reshape(D_MODEL, D_MODEL)

    def mlp(self, l, after):
        return self._get(("w_up", l), after), self._get(("w_down", l), after)

    def _view(self, n, g):
        _, rows, cols = self.wts[n].shape
        return g.reshape(N_CHIPS, 2, rows // 2, cols)

    def grads(self, name, arrays, after):
        if name == self.LAST:
            self.held = (name, arrays)
            return 0.0
        return self._scatter(name, arrays, after) + self._advance(after, self.IN_FLIGHT)

    def flush(self, after):
        return self._scatter(*self.held, after) + self._advance(after, self.IN_FLIGHT)

    def _scatter(self, name, arrays, after):
        pieces = [(n, name[1]) for n in arrays]
        views = [self._view(n, g) for n, g in arrays.items()]
        lands = [lax.empty((7,) + v.shape[2:], bf16) for v in views]
        h = split_start("scatter_%s%d_start" % name, _scatter_maker(len(views)), 7 * len(views), views + lands, after)
        self.scatter.append((pieces, h))
        return h["token"]

    def _take_share(self, after):
        pieces, h = self.share.pop(0)
        self.reduced.update(zip(pieces, split_wait(h, after)))

    def _take_scatter(self, after):
        pieces, h = self.scatter.pop(0)
        done = split_wait(h, after)
        views, lands = done[:len(pieces)], done[len(pieces):]
        sums = [piece_sum(v, land, self.kc_arr) for v, land in zip(views, lands)]
        hs = split_start(h["name"].replace("scatter", "share"), _share_maker(len(sums)), len(sums), sums, after)
        self.share.append((pieces, hs))
        return hs["token"]

    def _advance(self, after, newest):
        if self.share:
            self._take_share(after)
        return self._take_scatter(after) if len(self.scatter) > newest else 0.0

    def reduced_grads(self, names, after):
        want = [(n, l) for n in names for l in range(DEPTH)]
        while not all(p in self.reduced for p in want):
            if any(p in pieces for p in want for pieces, _ in self.share):
                self._take_share(after)
            else:
                self._take_scatter(after)
        return {n: [self.reduced[n, l] for l in range(DEPTH)] for n in names}


def kernel(x, mix_norm_g, w_in, q_gain, k_gain, sinks, rel_bias, conv_w, conv_b, dt_bias, a_log, d_skip, ssm_norm_g, w_out, mlp_norm_g, w_up, w_down, loss_target, m_mix_norm_g, m_w_in, m_q_gain, m_k_gain, m_sinks, m_rel_bias, m_conv_w, m_conv_b, m_dt_bias, m_a_log, m_d_skip, m_ssm_norm_g, m_w_out, m_mlp_norm_g, m_w_up, m_w_down, v_mix_norm_g, v_w_in, v_q_gain, v_k_gain, v_sinks, v_rel_bias, v_conv_w, v_conv_b, v_dt_bias, v_a_log, v_d_skip, v_ssm_norm_g, v_w_out, v_mlp_norm_g, v_w_up, v_w_down):
    wts = dict(mix_norm_g=mix_norm_g, w_in=w_in, q_gain=q_gain, k_gain=k_gain, sinks=sinks, rel_bias=rel_bias, conv_w=conv_w,
               conv_b=conv_b, dt_bias=dt_bias, a_log=a_log, d_skip=d_skip, ssm_norm_g=ssm_norm_g, w_out=w_out,
               mlp_norm_g=mlp_norm_g, w_up=w_up, w_down=w_down)
    mom = dict(mix_norm_g=m_mix_norm_g, w_in=m_w_in, q_gain=m_q_gain, k_gain=m_k_gain, sinks=m_sinks, rel_bias=m_rel_bias,
               conv_w=m_conv_w, conv_b=m_conv_b, dt_bias=m_dt_bias, a_log=m_a_log, d_skip=m_d_skip, ssm_norm_g=m_ssm_norm_g,
               w_out=m_w_out, mlp_norm_g=m_mlp_norm_g, w_up=m_w_up, w_down=m_w_down)
    var = dict(mix_norm_g=v_mix_norm_g, w_in=v_w_in, q_gain=v_q_gain, k_gain=v_k_gain, sinks=v_sinks, rel_bias=v_rel_bias,
               conv_w=v_conv_w, conv_b=v_conv_b, dt_bias=v_dt_bias, a_log=v_a_log, d_skip=v_d_skip, ssm_norm_g=v_ssm_norm_g,
               w_out=v_w_out, mlp_norm_g=v_mlp_norm_g, w_up=v_w_up, w_down=v_w_down)
    xi, yi, ci = _coords()
    k_me = 2 * xi + yi
    kc_arr = jnp.stack([k_me, ci]).astype(jnp.int32)

    prov = _Exchange(wts, kc_arr)
    small_w = {n: wts[n] for n in SMALL}
    small_w["conv_w"] = prov.conv_w
    loss, dx, grads, tok = local_step(x[0], loss_target[0], small_w, prov)

    small_shapes = {n: grads[n].shape for n in SMALL}
    small_sum = small_all_reduce(_pack(grads, loss) + tok)
    loss = small_sum[PACK_ROWS - 1, PACK_COLS - 1]
    tok = prov.flush(small_sum)
    small = _unpack(small_sum, small_shapes)
    cols = conv_w.shape[-1]
    small["conv_w"] = lax.dynamic_slice_in_dim(small["conv_w"], k_me * cols, cols, axis=2)
    g_out_d, d_out_d, m_out_d, v_out_d = {}, {}, {}, {}
    shard_shapes = {n: wts[n].shape for n in SMALL}
    d, nm, nv = adamw_small(_pack(wts), _pack(small) + tok, _pack(mom), _pack(var))
    for dst, buf in ((d_out_d, d), (m_out_d, nm), (v_out_d, nv)):
        dst.update(_unpack(buf, shard_shapes))
    g_out_d.update(small)

    after = d
    for names in (("w_up", "w_down"), ("w_in", "w_out")):
        for n, (g0, g1) in prov.reduced_grads(names, after).items():
            if n == "w_in":
                rows, cols = wts[n].shape[1:]
                to_cols = lambda a: jnp.transpose(a, (2, 0, 1))
                g_t = jnp.stack([to_cols(g).reshape(cols, rows) for g in (g0, g1)], axis=1)
                res_t = adamw_cols(to_cols(wts[n]), g_t, to_cols(mom[n]), to_cols(var[n]))
                g_out_d[n], d_out_d[n], m_out_d[n], v_out_d[n] = (jnp.transpose(a, (1, 2, 0)) for a in (g_t, *res_t))
            else:
                g_out_d[n], d_out_d[n], m_out_d[n], v_out_d[n] = adamw_shard(wts[n], g0, g1, mom[n], var[n])
            after = d_out_d[n]

    return (loss, dx[None], *[g_out_d[n] for n in WEIGHTS], *[d_out_d[n] for n in WEIGHTS],
            *[m_out_d[n] for n in WEIGHTS], *[v_out_d[n] for n in WEIGHTS])
```

```python
import numpy as np
import jax
import jax.numpy as jnp
from jax import lax
from jax.experimental import pallas as pl
from jax.experimental.pallas import tpu as pltpu

f32 = jnp.float32
bf16 = jnp.bfloat16

SEQ = 2048
D_MODEL = 1024
DEPTH = 2
HEAD_DIM = 64
N_Q_HEADS = 8
N_KV_HEADS = 2
Q_PER_KV = N_Q_HEADS // N_KV_HEADS
BLOCK = 128
N_BLOCKS = SEQ // BLOCK
N_BUCKETS = 32
MAX_DISTANCE = 128
SSM_HEADS = 8
SSM_HEAD_DIM = 64
SSM_GROUPS = 2
HEADS_PER_GROUP = SSM_HEADS // SSM_GROUPS
SSM_STATE = 128
CONV_WIDTH = 4
CHUNK = 128
N_CHUNKS = SEQ // CHUNK
D_FF = 4 * D_MODEL
D_ATTN = N_Q_HEADS * HEAD_DIM
D_KV = N_KV_HEADS * HEAD_DIM
D_SSM = SSM_HEADS * SSM_HEAD_DIM
D_BC = SSM_GROUPS * SSM_STATE
D_CONV = D_SSM + 2 * D_BC
D_IN = D_ATTN + 2 * D_KV + D_SSM + D_CONV + SSM_HEADS
EPS = 1e-6
NEG = -1e30
N_CHIPS = 4
FF_TILE = D_FF // N_CHIPS

LANE = 128
PW = D_ATTN + D_SSM + D_CONV + 2 * D_KV + LANE
OFF_Q, OFF_Z, OFF_X, OFF_K, OFF_V, OFF_DT = 0, 512, 1024, 2048, 2176, 2304

ADAM_LR = 0.001
ADAM_B1 = 0.9
ADAM_B2 = 0.999
ADAM_EPS = 1e-08
ADAM_WD = 0.01
ADAM_STEP = 10

VMEM_LIMIT = 56 * 1024 * 1024


def _params(*sem):
    return pltpu.CompilerParams(dimension_semantics=tuple(sem), vmem_limit_bytes=VMEM_LIMIT)


def _bdot(a, b):
    return jnp.dot(a.astype(bf16), b.astype(bf16), preferred_element_type=f32)


def _bdot_nt(a, b):
    return lax.dot_general(a.astype(bf16), b.astype(bf16), (((1,), (1,)), ((), ())), preferred_element_type=f32)


def _bdot_tn(a, b):
    return lax.dot_general(a.astype(bf16), b.astype(bf16), (((0,), (0,)), ((), ())), preferred_element_type=f32)


def _hdot(a, b):
    return jnp.dot(a, b, precision=lax.Precision.HIGHEST, preferred_element_type=f32)


def _sigmoid(x):
    return 1.0 / (1.0 + jnp.exp(-x))


def _softplus(x):
    return jnp.maximum(x, 0.0) + jnp.log1p(jnp.exp(-jnp.abs(x)))


def _rms(x):
    return lax.rsqrt(jnp.mean(x * x, axis=-1, keepdims=True) + EPS)


def _rms_bwd(dy, xhat, r, g):
    t = dy * g
    return r * (t - xhat * jnp.mean(t * xhat, axis=-1, keepdims=True))


def _full(shape):
    return pl.BlockSpec(shape, lambda *_: (0,) * len(shape))


def _bucket_table():
    qi = np.arange(BLOCK)[:, None]
    kj = np.arange(2 * BLOCK)[None, :]
    dist = qi + BLOCK - kj
    ok = (dist >= 0) & (dist < 128)
    d = np.clip(dist, 0, None)
    max_exact = N_BUCKETS // 2
    d_f = np.maximum(d, 1).astype(np.float32)
    large = max_exact + (np.log(d_f / np.float32(max_exact)) / np.float32(np.log(MAX_DISTANCE / max_exact))
                         * np.float32(N_BUCKETS - max_exact)).astype(np.int32)
    large = np.minimum(large, N_BUCKETS - 1)
    bucket = np.where(d < max_exact, d, large)
    return np.where(ok, bucket, -1).astype(np.int32)


def bias_build(rel_bias, bucket):
    def body(rel_ref, bkt_ref, o_ref):
        bkt = bkt_ref[...]
        for h in range(N_Q_HEADS):
            acc = jnp.where(bkt < 0, NEG, 0.0).astype(f32)
            for b in range(N_BUCKETS):
                acc = acc + jnp.where(bkt == b, rel_ref[b, h], 0.0)
            o_ref[h] = acc

    return pl.pallas_call(
        body, name="bias_build", out_shape=jax.ShapeDtypeStruct((N_Q_HEADS,) + bucket.shape, f32),
        in_specs=[pl.BlockSpec(memory_space=pltpu.SMEM), pl.BlockSpec(memory_space=pltpu.VMEM)],
        out_specs=pl.BlockSpec(memory_space=pltpu.VMEM),
    )(rel_bias, bucket)


def bias_bwd(dband0, dband1, bucket):
    def body(d0_ref, d1_ref, bkt_ref, o_ref):
        bkt = bkt_ref[...]
        o_ref[...] = jnp.zeros_like(o_ref)
        for h in range(N_Q_HEADS):
            d = d0_ref[h] + d1_ref[h]
            for b in range(N_BUCKETS):
                part = jnp.sum(jnp.where(bkt == b, d, 0.0), axis=1, keepdims=True)
                o_ref[b:b + 1, h:h + 1] = jnp.sum(part, axis=0, keepdims=True)

    return pl.pallas_call(
        body, name="bias_bwd", out_shape=jax.ShapeDtypeStruct((N_BUCKETS, LANE), f32),
    )(dband0, dband1, bucket)


W_IN_SHARD = D_IN // N_CHIPS
_ALIGNED_PIECES = ((0, 0, 512), (1, 190, 578), (2, 0, 124), (2, 124, 578), (3, 0, 570), (0, 512, 578), (1, 0, 62),
                   (1, 62, 190), (3, 570, 578))
_SHARD_PIECES = (((0, 512), (2048, 2114)), ((2114, 2176), (2176, 2304), (512, 900)), ((900, 1024), (1024, 1478)),
                 ((1478, 2048), (2304, 2312)))


def align_w_in(shards, tr=256):
    def body(s_ref, o_ref):
        parts = [s_ref[k, :, a:b] for k, a, b in _ALIGNED_PIECES]
        parts.append(jnp.zeros((tr, LANE - SSM_HEADS), s_ref.dtype))
        o_ref[...] = jnp.concatenate(parts, axis=-1)

    return pl.pallas_call(
        body, name="align_w_in", grid=(D_MODEL // tr,),
        in_specs=[pl.BlockSpec((N_CHIPS, tr, W_IN_SHARD), lambda i: (0, i, 0))],
        out_specs=pl.BlockSpec((tr, PW), lambda i: (i, 0)),
        out_shape=jax.ShapeDtypeStruct((D_MODEL, PW), shards.dtype),
        compiler_params=_params("arbitrary"),
    )(shards)


def split_w_in_grad(dw, tr=256):
    def body(d_ref, o16_ref):
        for k, pieces in enumerate(_SHARD_PIECES):
            o16_ref[k] = jnp.concatenate([d_ref[:, a:b] for a, b in pieces], axis=-1).astype(bf16)

    return pl.pallas_call(
        body, name="split_w_in_grad", grid=(D_MODEL // tr,),
        in_specs=[pl.BlockSpec((tr, PW), lambda i: (i, 0))],
        out_specs=pl.BlockSpec((N_CHIPS, tr, W_IN_SHARD), lambda i: (0, i, 0)),
        out_shape=jax.ShapeDtypeStruct((N_CHIPS, D_MODEL, W_IN_SHARD), bf16),
        compiler_params=_params("arbitrary"),
    )(dw)

def in_fwd(x, g, w, tm=512):
    def body(x_ref, g_ref, w_ref, o_ref):
        xv = x_ref[...]
        h = xv * _rms(xv) * g_ref[...]
        o_ref[...] = _bdot(h, w_ref[...])

    return pl.pallas_call(
        body, name="in_fwd", grid=(SEQ // tm,),
        in_specs=[pl.BlockSpec((tm, D_MODEL), lambda i: (i, 0)), _full((1, D_MODEL)), _resident((D_MODEL, PW))],
        out_specs=pl.BlockSpec((tm, PW), lambda i: (i, 0)),
        out_shape=jax.ShapeDtypeStruct((SEQ, PW), f32),
        compiler_params=_params("arbitrary"),
    )(x, g, w)


def _resident(shape):
    return pl.BlockSpec(shape, lambda *_: (0,) * len(shape), pipeline_mode=pl.Buffered(1))


def in_bwd(dq, dz, dxbc, dk, dv, ddt, x, g, w, dres, tm=512):
    def body(dq_ref, dz_ref, dx_ref, dk_ref, dv_ref, ddt_ref, x_ref, g_ref, w_ref, dres_ref, o_ref, dw_ref, dg_ref):
        i = pl.program_id(0)

        @pl.when(i == 0)
        def _():
            dw_ref[...] = jnp.zeros_like(dw_ref)
            dg_ref[...] = jnp.zeros_like(dg_ref)

        dproj = jnp.concatenate([dq_ref[...], dz_ref[...], dx_ref[...], dk_ref[...], dv_ref[...], ddt_ref[...]],
                                axis=-1).astype(bf16)
        xv = x_ref[...]
        r = _rms(xv)
        xhat = xv * r
        gv = g_ref[...]
        h = xhat * gv
        dw_ref[...] += _bdot_tn(h, dproj)
        dh = _bdot_nt(dproj, w_ref[...])
        dg_ref[...] += jnp.sum(dh * xhat, axis=0, keepdims=True)
        o_ref[...] = dres_ref[...] + _rms_bwd(dh, xhat, r, gv)

    tok = lambda w_: pl.BlockSpec((tm, w_), lambda i: (i, 0))
    return pl.pallas_call(
        body, name="in_bwd", grid=(SEQ // tm,),
        in_specs=[tok(D_ATTN), tok(D_SSM), tok(D_CONV), tok(D_KV), tok(D_KV), tok(LANE), tok(D_MODEL),
                  _full((1, D_MODEL)), _resident((D_MODEL, PW)), tok(D_MODEL)],
        out_specs=[tok(D_MODEL), _resident((D_MODEL, PW)), _full((1, D_MODEL))],
        out_shape=[jax.ShapeDtypeStruct((SEQ, D_MODEL), f32), jax.ShapeDtypeStruct((D_MODEL, PW), f32),
                   jax.ShapeDtypeStruct((1, D_MODEL), f32)],
        compiler_params=_params("arbitrary"),
    )(dq, dz, dxbc, dk, dv, ddt, x, g, w, dres)


def _attn_softmax_t(qk, bias_t, sink, first, key_row):
    s = qk * (HEAD_DIM ** -0.5) + bias_t
    s = jnp.where(jnp.logical_and(first, key_row < BLOCK), NEG, s)
    m = jnp.maximum(jnp.max(s, axis=0, keepdims=True), sink)
    p = jnp.exp(s - m)
    psink = jnp.exp(sink - m)
    inv = 1.0 / (jnp.sum(p, axis=0, keepdims=True) + psink)
    return p * inv, psink * inv


def _rms_t(x_t):
    return lax.rsqrt(jnp.mean(x_t * x_t, axis=0, keepdims=True) + EPS)


def attn_fwd_t(proj, q_gain_col, k_gain, sinks, bias_t):
    kcol, vcol = OFF_K // D_KV, OFF_V // D_KV

    def body(q_ref, kc_ref, kp_ref, vc_ref, vp_ref, qg_ref, kg_ref, sink_ref, bias_ref, o_ref, ot_scr):
        n = pl.program_id(0)
        first = n == 0
        key_row = lax.broadcasted_iota(jnp.int32, (2 * BLOCK, BLOCK), 0)
        k2 = jnp.concatenate([kp_ref[...], kc_ref[...]], axis=0)
        v_t = jnp.concatenate([vp_ref[...], vc_ref[...]], axis=0).T
        q_t = q_ref[...].T
        qg = jnp.broadcast_to(qg_ref[...], (HEAD_DIM, BLOCK))
        kg = kg_ref[...]
        for hk in range(N_KV_HEADS):
            sl = slice(hk * HEAD_DIM, (hk + 1) * HEAD_DIM)
            kk = k2[:, sl]
            kn = (kk * _rms(kk) * kg).astype(bf16)
            vt = v_t[sl, :].astype(bf16)
            heads = range(hk * Q_PER_KV, (hk + 1) * Q_PER_KV)
            qns = []
            for h in heads:
                qh = q_t[h * HEAD_DIM:(h + 1) * HEAD_DIM, :]
                qns.append(qh * _rms_t(qh) * qg)
            scores = [_bdot(kn, qn) for qn in qns]
            for h, s in zip(heads, scores):
                p, _ = _attn_softmax_t(s, bias_ref[h], sink_ref[h], first, key_row)
                ot_scr[h * HEAD_DIM:(h + 1) * HEAD_DIM, :] = _bdot(vt, p)
        o_ref[...] = ot_scr[...].T

    prev = lambda n: jnp.maximum(n - 1, 0)
    return pl.pallas_call(
        body, name="attn_fwd", grid=(N_BLOCKS,),
        in_specs=[pl.BlockSpec((BLOCK, D_ATTN), lambda n: (n, 0)),
                  pl.BlockSpec((BLOCK, D_KV), lambda n: (n, kcol)), pl.BlockSpec((BLOCK, D_KV), lambda n: (prev(n), kcol)),
                  pl.BlockSpec((BLOCK, D_KV), lambda n: (n, vcol)), pl.BlockSpec((BLOCK, D_KV), lambda n: (prev(n), vcol)),
                  _full((HEAD_DIM, 1)), _full((1, HEAD_DIM)), pl.BlockSpec(memory_space=pltpu.SMEM),
                  _full((N_Q_HEADS, 2 * BLOCK, BLOCK))],
        out_specs=pl.BlockSpec((BLOCK, D_ATTN), lambda n: (n, 0)),
        out_shape=jax.ShapeDtypeStruct((SEQ, D_ATTN), f32),
        scratch_shapes=[pltpu.VMEM((D_ATTN, BLOCK), f32)],
        compiler_params=_params("arbitrary"),
    )(proj, proj, proj, proj, proj, q_gain_col, k_gain, sinks, bias_t)


def attn_bwd_t(proj, d_out, q_gain_col, k_gain, sinks, bias_t):
    kcol, vcol = OFF_K // D_KV, OFF_V // D_KV

    def body(q_ref, kc_ref, kp_ref, vc_ref, vp_ref, do_ref, qg_ref, kg_ref, sink_ref, bias_ref,
             dq_ref, dk_ref, dv_ref, dband_ref, dsink_ref, dqg_ref, dkg_ref, dkn_scr, dv_scr, dqt_scr, dsink_acc, dqg_acc):
        i = pl.program_id(0)
        first = i == N_BLOCKS - 1

        @pl.when(i == 0)
        def _():
            for ref in (dband_ref, dkg_ref, dkn_scr, dv_scr, dsink_acc, dqg_acc):
                ref[...] = jnp.zeros_like(ref)

        key_row = lax.broadcasted_iota(jnp.int32, (2 * BLOCK, BLOCK), 0)
        k2 = jnp.concatenate([kp_ref[...], kc_ref[...]], axis=0)
        v2 = jnp.concatenate([vp_ref[...], vc_ref[...]], axis=0)
        q_t = q_ref[...].T
        do_t = do_ref[...].T
        qg = jnp.broadcast_to(qg_ref[...], (HEAD_DIM, BLOCK))
        kg = kg_ref[...]
        scale = HEAD_DIM ** -0.5
        for hk in range(N_KV_HEADS):
            sl = slice(hk * HEAD_DIM, (hk + 1) * HEAD_DIM)
            kk = k2[:, sl]
            rk = _rms(kk)
            khat = kk * rk
            kn = (khat * kg).astype(bf16)
            vb = v2[:, sl].astype(bf16)
            dkn = jnp.zeros((2 * BLOCK, HEAD_DIM), f32)
            dvv = jnp.zeros((2 * BLOCK, HEAD_DIM), f32)
            heads = range(hk * Q_PER_KV, (hk + 1) * Q_PER_KV)
            rqs, qhats, qns, d_os = [], [], [], []
            for h in heads:
                hs = slice(h * HEAD_DIM, (h + 1) * HEAD_DIM)
                qh = q_t[hs, :]
                rqs.append(_rms_t(qh))
                qhats.append(qh * rqs[-1])
                qns.append((qhats[-1] * qg).astype(bf16))
                d_os.append(do_t[hs, :].astype(bf16))
            scores = [_bdot(kn, qn) for qn in qns]
            dps = [_bdot(vb, d_o) for d_o in d_os]
            ps, dss = [], []
            for h, s, dp in zip(heads, scores, dps):
                p, psink = _attn_softmax_t(s, bias_ref[h], sink_ref[h], first, key_row)
                delta = jnp.sum(p * dp, axis=0, keepdims=True)
                ds = p * (dp - delta)
                dband_ref[h] += ds
                dsink_acc[h:h + 1, :] += -(psink * delta)
                ps.append(p.astype(bf16))
                dss.append(ds.astype(bf16))
            dqns = [_bdot_tn(kn, ds) * scale for ds in dss]
            for ds, qn, p, d_o in zip(dss, qns, ps, d_os):
                dkn = dkn + _bdot_nt(ds, qn) * scale
                dvv = dvv + _bdot_nt(p, d_o)
            for h, dqn, rq, qhat in zip(heads, dqns, rqs, qhats):
                dqg_acc[...] += dqn * qhat
                t = dqn * qg
                dqt_scr[h * HEAD_DIM:(h + 1) * HEAD_DIM, :] = rq * (t - qhat * jnp.mean(t * qhat, axis=0, keepdims=True))
            dkn_cur = dkn[BLOCK:] + dkn_scr[:, sl]
            dkn_scr[:, sl] = dkn[:BLOCK]
            khat_c, rk_c = khat[BLOCK:], rk[BLOCK:]
            dkg_ref[...] += jnp.sum(dkn_cur * khat_c, axis=0, keepdims=True)
            dk_ref[:, sl] = _rms_bwd(dkn_cur, khat_c, rk_c, kg)
            dv_ref[:, sl] = dvv[BLOCK:] + dv_scr[:, sl]
            dv_scr[:, sl] = dvv[:BLOCK]
        dq_ref[...] = dqt_scr[...].T

        @pl.when(i == N_BLOCKS - 1)
        def _():
            dsink_ref[...] = jnp.sum(dsink_acc[...], axis=1, keepdims=True)
            dqg_ref[...] = jnp.sum(dqg_acc[...], axis=1, keepdims=True)

    blk = lambda i: N_BLOCKS - 1 - i
    prev = lambda i: jnp.maximum(N_BLOCKS - 2 - i, 0)
    return pl.pallas_call(
        body, name="attn_bwd", grid=(N_BLOCKS,),
        in_specs=[pl.BlockSpec((BLOCK, D_ATTN), lambda i: (blk(i), 0)),
                  pl.BlockSpec((BLOCK, D_KV), lambda i: (blk(i), kcol)), pl.BlockSpec((BLOCK, D_KV), lambda i: (prev(i), kcol)),
                  pl.BlockSpec((BLOCK, D_KV), lambda i: (blk(i), vcol)), pl.BlockSpec((BLOCK, D_KV), lambda i: (prev(i), vcol)),
                  pl.BlockSpec((BLOCK, D_ATTN), lambda i: (blk(i), 0)),
                  _full((HEAD_DIM, 1)), _full((1, HEAD_DIM)), pl.BlockSpec(memory_space=pltpu.SMEM),
                  _full((N_Q_HEADS, 2 * BLOCK, BLOCK))],
        out_specs=[pl.BlockSpec((BLOCK, D_ATTN), lambda i: (blk(i), 0)), pl.BlockSpec((BLOCK, D_KV), lambda i: (blk(i), 0)),
                   pl.BlockSpec((BLOCK, D_KV), lambda i: (blk(i), 0)), _full((N_Q_HEADS, 2 * BLOCK, BLOCK)),
                   _full((N_Q_HEADS, 1)), _full((HEAD_DIM, 1)), _full((1, HEAD_DIM))],
        out_shape=[jax.ShapeDtypeStruct((SEQ, D_ATTN), f32), jax.ShapeDtypeStruct((SEQ, D_KV), f32),
                   jax.ShapeDtypeStruct((SEQ, D_KV), f32), jax.ShapeDtypeStruct((N_Q_HEADS, 2 * BLOCK, BLOCK), f32),
                   jax.ShapeDtypeStruct((N_Q_HEADS, 1), f32), jax.ShapeDtypeStruct((HEAD_DIM, 1), f32),
                   jax.ShapeDtypeStruct((1, HEAD_DIM), f32)],
        scratch_shapes=[pltpu.VMEM((BLOCK, D_KV), f32), pltpu.VMEM((BLOCK, D_KV), f32), pltpu.VMEM((D_ATTN, BLOCK), f32),
                        pltpu.VMEM((N_Q_HEADS, BLOCK), f32), pltpu.VMEM((HEAD_DIM, BLOCK), f32)],
        compiler_params=_params("arbitrary"),
    )(proj, proj, proj, proj, proj, d_out, q_gain_col, k_gain, sinks, bias_t)


SUBLANES = 8


def _shift_down(u, s, row8):
    if s == 0:
        return u
    r = pltpu.roll(u, s, 0)
    return jnp.concatenate([jnp.where(row8 >= s, r[:SUBLANES], 0.0), r[SUBLANES:]], axis=0)


def _shift_up(u, s, row8):
    if s == 0:
        return u
    r = pltpu.roll(u, SEQ - s, 0)
    return jnp.concatenate([r[:-SUBLANES], jnp.where(row8 < SUBLANES - s, r[-SUBLANES:], 0.0)], axis=0)


def conv_fwd(proj, conv_w, conv_b):
    xcol = OFF_X // LANE

    def body(u_ref, w_ref, b_ref, o_ref):
        u = u_ref[...]
        row = lax.broadcasted_iota(jnp.int32, (SUBLANES, LANE), 0)
        pre = b_ref[...] + jnp.zeros_like(u)
        for k in range(CONV_WIDTH):
            pre = pre + w_ref[k:k + 1, :] * _shift_down(u, CONV_WIDTH - 1 - k, row)
        o_ref[...] = pre * _sigmoid(pre)

    return pl.pallas_call(
        body, name="conv_fwd", grid=(D_CONV // LANE,),
        in_specs=[pl.BlockSpec((SEQ, LANE), lambda j: (0, xcol + j)), pl.BlockSpec((CONV_WIDTH, LANE), lambda j: (0, j)),
                  pl.BlockSpec((1, LANE), lambda j: (0, j))],
        out_specs=pl.BlockSpec((SEQ, LANE), lambda j: (0, j)),
        out_shape=jax.ShapeDtypeStruct((SEQ, D_CONV), f32),
        compiler_params=_params("arbitrary"),
    )(proj, conv_w, conv_b)


def conv_bwd(proj, d_act, conv_w, conv_b):
    xcol = OFF_X // LANE

    def body(u_ref, da_ref, w_ref, b_ref, du_ref, dw_ref, db_ref):
        u = u_ref[...]
        row = lax.broadcasted_iota(jnp.int32, (SUBLANES, LANE), 0)
        shifted = [_shift_down(u, CONV_WIDTH - 1 - k, row) for k in range(CONV_WIDTH)]
        pre = b_ref[...] + jnp.zeros_like(u)
        for k in range(CONV_WIDTH):
            pre = pre + w_ref[k:k + 1, :] * shifted[k]
        sg = _sigmoid(pre)
        dpre = da_ref[...] * (sg * (1.0 + pre * (1.0 - sg)))
        db_ref[...] = jnp.sum(dpre, axis=0, keepdims=True)
        du = jnp.zeros_like(u)
        for k in range(CONV_WIDTH):
            dw_ref[k:k + 1, :] = jnp.sum(dpre * shifted[k], axis=0, keepdims=True)
            du = du + w_ref[k:k + 1, :] * _shift_up(dpre, CONV_WIDTH - 1 - k, row)
        du_ref[...] = du

    return pl.pallas_call(
        body, name="conv_bwd", grid=(D_CONV // LANE,),
        in_specs=[pl.BlockSpec((SEQ, LANE), lambda j: (0, xcol + j)), pl.BlockSpec((SEQ, LANE), lambda j: (0, j)),
                  pl.BlockSpec((CONV_WIDTH, LANE), lambda j: (0, j)), pl.BlockSpec((1, LANE), lambda j: (0, j))],
        out_specs=[pl.BlockSpec((SEQ, LANE), lambda j: (0, j)), pl.BlockSpec((CONV_WIDTH, LANE), lambda j: (0, j)),
                   pl.BlockSpec((1, LANE), lambda j: (0, j))],
        out_shape=[jax.ShapeDtypeStruct((SEQ, D_CONV), f32), jax.ShapeDtypeStruct((CONV_WIDTH, D_CONV), f32),
                   jax.ShapeDtypeStruct((1, D_CONV), f32)],
        compiler_params=_params("arbitrary"),
    )(proj, d_act, conv_w, conv_b)


def _ssd_chunk_common(dt_raw, dtb, alog):
    row = lax.broadcasted_iota(jnp.int32, (CHUNK, CHUNK), 0)
    col = lax.broadcasted_iota(jnp.int32, (CHUNK, CHUNK), 1)
    tri = (row >= col).astype(f32)
    strict = (row > col).astype(f32)
    dtp = _softplus(dt_raw + dtb)
    a_row = -jnp.exp(alog)
    d_a = dtp * a_row
    cs = _hdot(tri, d_a)
    cs_last = cs[CHUNK - 1:CHUNK, :]
    return row, col, dtp, a_row, cs, cs.T, cs_last


def _seg_decay(cs, cs_t, hd, row, col):
    seg = cs[:, hd:hd + 1] - cs_t[hd:hd + 1, :]
    return jnp.where(row >= col, jnp.exp(seg), 0.0)


GROUP_W = HEADS_PER_GROUP * SSM_HEAD_DIM


def _group_indicator(g):
    j = lax.broadcasted_iota(jnp.int32, (GROUP_W, LANE), 0)
    lane = lax.broadcasted_iota(jnp.int32, (GROUP_W, LANE), 1)
    return (lane == g * HEADS_PER_GROUP + j // SSM_HEAD_DIM).astype(bf16)


def _bf16_pieces(a, n):
    pieces = []
    for _ in range(n):
        p = a.astype(bf16)
        pieces.append(p)
        a = a - p.astype(f32)
    return pieces


def _head_spread(a, ind):
    return sum(lax.dot_general(p, ind, (((1,), (1,)), ((), ())), preferred_element_type=f32) for p in _bf16_pieces(a, 3))


def _head_sums(a, ind):
    return sum(jnp.dot(p, ind, preferred_element_type=f32) for p in _bf16_pieces(a, 2))


def ssd_fwd_g(act, proj, dt_bias, a_log, d_skip, norm_g):
    zcol, dtcol = OFF_Z // D_SSM, OFF_DT // LANE

    def body(act_ref, z_ref, dt_ref, dtb_ref, alog_ref, dsk_ref, ng_ref, out_ref, ypre_ref, st_ref, state):
        c = pl.program_id(0)

        @pl.when(c == 0)
        def _():
            state[...] = jnp.zeros_like(state)

        row, col, dtp, a_row, cs, cs_t, cs_last = _ssd_chunk_common(dt_ref[...], dtb_ref[...], alog_ref[...])
        e_cs = jnp.exp(cs)
        dte = jnp.exp(cs_last - cs)
        rows8 = jnp.concatenate([jnp.exp(cs_last), dsk_ref[...], jnp.zeros((6, LANE), f32)], axis=0)
        z = z_ref[...]
        sz = z * _sigmoid(z)
        ng = ng_ref[...]
        for g in range(SSM_GROUPS):
            gs = slice(g * GROUP_W, (g + 1) * GROUP_W)
            ind = _group_indicator(g)
            xg = act_ref[:, gs]
            bg = act_ref[:, D_SSM + g * SSM_STATE:D_SSM + (g + 1) * SSM_STATE]
            cg = act_ref[:, D_SSM + D_BC + g * SSM_STATE:D_SSM + D_BC + (g + 1) * SSM_STATE]
            dt_e, e_e, dte_e = _head_spread(dtp, ind), _head_spread(e_cs, ind), _head_spread(dte, ind)
            rows_e = _head_spread(rows8, ind)
            ecl_e, dsk_e = rows_e[0:1], rows_e[1:2]
            xdt = xg * dt_e
            prev = state[g]
            st_ref[0, g] = prev
            cb = _bdot_nt(cg, bg)
            goff = _bdot(cg, prev)
            snew = _bdot_tn(bg, xdt * dte_e)
            heads = range(g * HEADS_PER_GROUP, (g + 1) * HEADS_PER_GROUP)
            ms = [cb * _seg_decay(cs, cs_t, hd, row, col) for hd in heads]
            yd = [_bdot(m, xdt[:, r * SSM_HEAD_DIM:(r + 1) * SSM_HEAD_DIM]) for r, m in enumerate(ms)]
            y = jnp.concatenate(yd, axis=1) + e_e * goff + xg * dsk_e
            state[g] = prev * ecl_e + snew
            ypre_ref[:, gs] = y
            part = y * sz[:, gs]
            out_ref[:, gs] = part * _rms(part) * ng[:, gs]

    return pl.pallas_call(
        body, name="ssd_fwd", grid=(N_CHUNKS,),
        in_specs=[pl.BlockSpec((CHUNK, D_CONV), lambda c: (c, 0)), pl.BlockSpec((CHUNK, D_SSM), lambda c: (c, zcol)),
                  pl.BlockSpec((CHUNK, LANE), lambda c: (c, dtcol)), _full((1, LANE)), _full((1, LANE)), _full((1, LANE)),
                  _full((1, D_SSM))],
        out_specs=[pl.BlockSpec((CHUNK, D_SSM), lambda c: (c, 0)), pl.BlockSpec((CHUNK, D_SSM), lambda c: (c, 0)),
                   pl.BlockSpec((1, SSM_GROUPS, SSM_STATE, GROUP_W), lambda c: (c, 0, 0, 0))],
        out_shape=[jax.ShapeDtypeStruct((SEQ, D_SSM), f32), jax.ShapeDtypeStruct((SEQ, D_SSM), f32),
                   jax.ShapeDtypeStruct((N_CHUNKS, SSM_GROUPS, SSM_STATE, GROUP_W), f32)],
        scratch_shapes=[pltpu.VMEM((SSM_GROUPS, SSM_STATE, GROUP_W), f32)],
        compiler_params=_params("arbitrary"),
    )(act, proj, proj, dt_bias, a_log, d_skip, norm_g)


def ssd_bwd_g(act, proj, ypre, states, d_out, dt_bias, a_log, d_skip, norm_g):
    zcol, dtcol = OFF_Z // D_SSM, OFF_DT // LANE

    def body(act_ref, z_ref, dt_ref, ypre_ref, st_ref, do_ref, dtb_ref, alog_ref, dsk_ref, ng_ref,
             dact_ref, ddt_ref, dz_ref, dng_ref, dpar_ref, dstate):
        i = pl.program_id(0)

        @pl.when(i == 0)
        def _():
            for ref in (dng_ref, dpar_ref, dstate):
                ref[...] = jnp.zeros_like(ref)

        row, col, dtp, a_row, cs, cs_t, cs_last = _ssd_chunk_common(dt_ref[...], dtb_ref[...], alog_ref[...])
        upper = (row <= col).astype(f32)
        lane = lax.broadcasted_iota(jnp.int32, (CHUNK, LANE), 1)
        rowl = lax.broadcasted_iota(jnp.int32, (CHUNK, LANE), 0)
        e_cs = jnp.exp(cs)
        dte = jnp.exp(cs_last - cs)
        ecl = jnp.exp(cs_last)
        rows8 = jnp.concatenate([ecl, dsk_ref[...], jnp.zeros((6, LANE), f32)], axis=0)
        z = z_ref[...]
        sgz = _sigmoid(z)
        sz = z * sgz
        ng = ng_ref[...]
        ddt_mat = jnp.zeros((CHUNK, LANE), f32)
        dcs_mat = jnp.zeros((CHUNK, LANE), f32)
        dcs_t = jnp.zeros((LANE, CHUNK), f32)
        dcsl_row = jnp.zeros((1, LANE), f32)
        dd_row = jnp.zeros((1, LANE), f32)
        for g in range(SSM_GROUPS):
            gs = slice(g * GROUP_W, (g + 1) * GROUP_W)
            bsl = slice(D_SSM + g * SSM_STATE, D_SSM + (g + 1) * SSM_STATE)
            csl = slice(D_SSM + D_BC + g * SSM_STATE, D_SSM + D_BC + (g + 1) * SSM_STATE)
            ind = _group_indicator(g)
            y = ypre_ref[:, gs]
            part = y * sz[:, gs]
            r = _rms(part)
            yhat = part * r
            d_o = do_ref[:, gs]
            dng_ref[:, gs] += jnp.sum(d_o * yhat, axis=0, keepdims=True)
            dyz = _rms_bwd(d_o, yhat, r, ng[:, gs])
            dy = dyz * sz[:, gs]
            dz_ref[:, gs] = dyz * y * (sgz[:, gs] * (1.0 + z[:, gs] * (1.0 - sgz[:, gs])))

            xg = act_ref[:, gs]
            bg = act_ref[:, bsl]
            cg = act_ref[:, csl]
            dt_e, e_e, dte_e = _head_spread(dtp, ind), _head_spread(e_cs, ind), _head_spread(dte, ind)
            rows_e = _head_spread(rows8, ind)
            ecl_e, dsk_e = rows_e[0:1], rows_e[1:2]
            xdt = xg * dt_e
            prev = st_ref[0, g]
            dh = dstate[g]
            heads = range(g * HEADS_PER_GROUP, (g + 1) * HEADS_PER_GROUP)
            hsl = [slice(r_ * SSM_HEAD_DIM, (r_ + 1) * SSM_HEAD_DIM) for r_ in range(HEADS_PER_GROUP)]
            cb = _bdot_nt(cg, bg)
            lms = [_seg_decay(cs, cs_t, hd, row, col) for hd in heads]
            ms = [cb * lm for lm in lms]
            gmat = _bdot(cg, prev)
            dgm = dy * e_e
            dcg = _bdot_nt(dgm, prev)
            dprev = _bdot_tn(cg, dgm)
            dbg = _bdot_nt(xdt * dte_e, dh)
            dw = _bdot(bg, dh)
            dms = [_bdot_nt(dy[:, s_], xdt[:, s_]) for s_ in hsl]
            dxdts = [_bdot_tn(m, dy[:, s_]) for m, s_ in zip(ms, hsl)]
            dxdt = jnp.concatenate(dxdts, axis=1) + dw * dte_e
            dact_ref[:, gs] = dy * dsk_e + dxdt * dt_e
            dstate[g] = dprev + dh * ecl_e
            dcb = jnp.zeros((CHUNK, CHUNK), f32)
            for hd, dm, lm, m in zip(heads, dms, lms, ms):
                dcb = dcb + dm * lm
                dseg = dm * m
                dcs_mat = dcs_mat + jnp.where(lane == hd, jnp.sum(dseg, axis=1, keepdims=True), 0.0)
                dcs_t = jnp.where(row == hd, jnp.sum(dseg, axis=0, keepdims=True), dcs_t)
            dact_ref[:, bsl] = dbg + _bdot_tn(dcb, cg)
            dact_ref[:, csl] = dcg + _bdot(dcb, bg)
            ddte = _head_sums(dw * xdt, ind) * dte
            dcs_mat = dcs_mat + _head_sums(dy * gmat, ind) * e_cs - ddte
            ddt_mat = ddt_mat + _head_sums(dxdt * xg, ind)
            dcsl_row = (dcsl_row + jnp.sum(ddte, axis=0, keepdims=True)
                        + jnp.sum(_head_sums(dh * prev, ind), axis=0, keepdims=True) * ecl)
            dd_row = dd_row + jnp.sum(_head_sums(dy * xg, ind), axis=0, keepdims=True)
        dcs_mat = dcs_mat - dcs_t.T + jnp.where(rowl == CHUNK - 1, dcsl_row, 0.0)
        dda = _hdot(upper, dcs_mat)
        ddt_mat = ddt_mat + dda * a_row
        da_row = jnp.sum(dda * dtp, axis=0, keepdims=True)
        ddt_raw = ddt_mat * _sigmoid(dt_ref[...] + dtb_ref[...])
        ddt_ref[...] = ddt_raw
        dpar_ref[0:1, :] += jnp.sum(ddt_raw, axis=0, keepdims=True)
        dpar_ref[1:2, :] += da_row * a_row
        dpar_ref[2:3, :] += dd_row

    blk = lambda i: N_CHUNKS - 1 - i
    return pl.pallas_call(
        body, name="ssd_bwd", grid=(N_CHUNKS,),
        in_specs=[pl.BlockSpec((CHUNK, D_CONV), lambda i: (blk(i), 0)), pl.BlockSpec((CHUNK, D_SSM), lambda i: (blk(i), zcol)),
                  pl.BlockSpec((CHUNK, LANE), lambda i: (blk(i), dtcol)), pl.BlockSpec((CHUNK, D_SSM), lambda i: (blk(i), 0)),
                  pl.BlockSpec((1, SSM_GROUPS, SSM_STATE, GROUP_W), lambda i: (blk(i), 0, 0, 0)),
                  pl.BlockSpec((CHUNK, D_SSM), lambda i: (blk(i), 0)),
                  _full((1, LANE)), _full((1, LANE)), _full((1, LANE)), _full((1, D_SSM))],
        out_specs=[pl.BlockSpec((CHUNK, D_CONV), lambda i: (blk(i), 0)), pl.BlockSpec((CHUNK, LANE), lambda i: (blk(i), 0)),
                   pl.BlockSpec((CHUNK, D_SSM), lambda i: (blk(i), 0)), _full((1, D_SSM)), _full((8, LANE))],
        out_shape=[jax.ShapeDtypeStruct((SEQ, D_CONV), f32), jax.ShapeDtypeStruct((SEQ, LANE), f32),
                   jax.ShapeDtypeStruct((SEQ, D_SSM), f32), jax.ShapeDtypeStruct((1, D_SSM), f32),
                   jax.ShapeDtypeStruct((8, LANE), f32)],
        scratch_shapes=[pltpu.VMEM((SSM_GROUPS, SSM_STATE, GROUP_W), f32)],
        compiler_params=_params("arbitrary"),
    )(act, proj, proj, ypre, states, d_out, dt_bias, a_log, d_skip, norm_g)


def out_fwd(x, attn, ssm, w_out, tm=512):
    def body(x_ref, a_ref, s_ref, w_ref, o_ref):
        o_ref[...] = x_ref[...] + _bdot(a_ref[...], w_ref[:D_ATTN, :]) + _bdot(s_ref[...], w_ref[D_ATTN:, :])

    tok = lambda w_: pl.BlockSpec((tm, w_), lambda i: (i, 0))
    return pl.pallas_call(
        body, name="out_fwd", grid=(SEQ // tm,),
        in_specs=[tok(D_MODEL), tok(D_ATTN), tok(D_SSM), _full((D_MODEL, D_MODEL))],
        out_specs=tok(D_MODEL), out_shape=jax.ShapeDtypeStruct((SEQ, D_MODEL), f32),
        compiler_params=_params("arbitrary"),
    )(x, attn, ssm, w_out)


def out_bwd(dx1, attn, ssm, w_out, tm=512):
    nt = SEQ // tm

    def body(d_ref, a_ref, s_ref, w_ref, da_ref, ds_ref, dw16_ref, dw_ref):
        i = pl.program_id(0)

        @pl.when(i == 0)
        def _():
            dw_ref[...] = jnp.zeros_like(dw_ref)

        d = d_ref[...].astype(bf16)
        dcat = _bdot_nt(d, w_ref[...])
        da_ref[...] = dcat[:, :D_ATTN]
        ds_ref[...] = dcat[:, D_ATTN:]
        dw_ref[:D_ATTN, :] += _bdot_tn(a_ref[...], d)
        dw_ref[D_ATTN:, :] += _bdot_tn(s_ref[...], d)

        @pl.when(i == nt - 1)
        def _():
            dw16_ref[...] = dw_ref[...].astype(bf16)

    tok = lambda w_: pl.BlockSpec((tm, w_), lambda i: (i, 0))
    return pl.pallas_call(
        body, name="out_bwd", grid=(nt,),
        in_specs=[tok(D_MODEL), tok(D_ATTN), tok(D_SSM), _resident((D_MODEL, D_MODEL))],
        out_specs=[tok(D_ATTN), tok(D_SSM), _resident((D_MODEL, D_MODEL))],
        out_shape=[jax.ShapeDtypeStruct((SEQ, D_ATTN), f32), jax.ShapeDtypeStruct((SEQ, D_SSM), f32),
                   jax.ShapeDtypeStruct((D_MODEL, D_MODEL), bf16)],
        scratch_shapes=[pltpu.VMEM((D_MODEL, D_MODEL), f32)],
        compiler_params=_params("arbitrary"),
    )(dx1, attn, ssm, w_out)


MLP_SUB = 256


def mlp_fwd(x1, g, w_up, w_down, tm=1024):
    def body(x_ref, g_ref, wu_ref, wd_ref, o_ref, u_ref, h_scr):
        j = pl.program_id(1)

        @pl.when(j == 0)
        def _():
            xv = x_ref[...]
            h_scr[...] = (xv * _rms(xv) * g_ref[...]).astype(bf16)
            o_ref[...] = xv

        for r in range(tm // MLP_SUB):
            rows = slice(r * MLP_SUB, (r + 1) * MLP_SUB)
            u = jnp.dot(h_scr[rows, :], wu_ref[...], preferred_element_type=f32)
            u_ref[rows, :] = u
            a = jnp.square(jnp.maximum(u, 0.0))
            o_ref[rows, :] += _bdot(a, wd_ref[...])

    return pl.pallas_call(
        body, name="mlp_fwd", grid=(SEQ // tm, N_CHIPS),
        in_specs=[pl.BlockSpec((tm, D_MODEL), lambda i, j: (i, 0)), _full((1, D_MODEL)),
                  pl.BlockSpec((None, D_MODEL, FF_TILE), lambda i, j: (j, 0, 0)),
                  pl.BlockSpec((None, FF_TILE, D_MODEL), lambda i, j: (j, 0, 0))],
        out_specs=[pl.BlockSpec((tm, D_MODEL), lambda i, j: (i, 0)), pl.BlockSpec((tm, FF_TILE), lambda i, j: (i, j))],
        out_shape=[jax.ShapeDtypeStruct((SEQ, D_MODEL), f32), jax.ShapeDtypeStruct((SEQ, D_FF), f32)],
        scratch_shapes=[pltpu.VMEM((tm, D_MODEL), bf16)],
        compiler_params=_params("arbitrary", "arbitrary"),
    )(x1, g, w_up, w_down)


def mlp_bwd_data(dx2, u, x1, g, w_up, w_down, tm=1024):
    def body(d_ref, u_ref, x_ref, g_ref, wu_ref, wd_ref, dx_ref, du_ref, dg_ref, dh_scr):
        i, j = pl.program_id(0), pl.program_id(1)

        @pl.when(jnp.logical_and(i == 0, j == 0))
        def _():
            dg_ref[...] = jnp.zeros_like(dg_ref)

        @pl.when(j == 0)
        def _():
            dh_scr[...] = jnp.zeros_like(dh_scr)

        for r in range(tm // MLP_SUB):
            rows = slice(r * MLP_SUB, (r + 1) * MLP_SUB)
            da = _bdot_nt(d_ref[rows, :], wd_ref[...])
            du = (da * (2.0 * jnp.maximum(u_ref[rows, :], 0.0))).astype(bf16)
            du_ref[rows, :] = du
            dh_scr[rows, :] += _bdot_nt(du, wu_ref[...])

        @pl.when(j == N_CHIPS - 1)
        def _():
            xv = x_ref[...]
            r = _rms(xv)
            xhat = xv * r
            dh = dh_scr[...]
            dg_ref[...] += jnp.sum(dh * xhat, axis=0, keepdims=True)
            dx_ref[...] = d_ref[...] + _rms_bwd(dh, xhat, r, g_ref[...])

    return pl.pallas_call(
        body, name="mlp_bwd_data", grid=(SEQ // tm, N_CHIPS),
        in_specs=[pl.BlockSpec((tm, D_MODEL), lambda i, j: (i, 0)), pl.BlockSpec((tm, FF_TILE), lambda i, j: (i, j)),
                  pl.BlockSpec((tm, D_MODEL), lambda i, j: (i, 0)), _full((1, D_MODEL)),
                  pl.BlockSpec((None, D_MODEL, FF_TILE), lambda i, j: (j, 0, 0)),
                  pl.BlockSpec((None, FF_TILE, D_MODEL), lambda i, j: (j, 0, 0))],
        out_specs=[pl.BlockSpec((tm, D_MODEL), lambda i, j: (i, 0)), pl.BlockSpec((tm, FF_TILE), lambda i, j: (i, j)),
                   _full((1, D_MODEL))],
        out_shape=[jax.ShapeDtypeStruct((SEQ, D_MODEL), f32), jax.ShapeDtypeStruct((SEQ, D_FF), bf16),
                   jax.ShapeDtypeStruct((1, D_MODEL), f32)],
        scratch_shapes=[pltpu.VMEM((tm, D_MODEL), f32)],
        compiler_params=_params("arbitrary", "arbitrary"),
    )(dx2, u, x1, g, w_up, w_down)


def mlp_bwd_weights(dx2, u, du, x1, g, tm=512):
    nt = SEQ // tm

    def body(d_ref, u_ref, du_ref, x_ref, g_ref, dwu16_ref, dwd16_ref, h_scr, d_scr, dwu_ref, dwd_ref):
        j, i = pl.program_id(0), pl.program_id(1)

        @pl.when(j == 0)
        def _():
            xv = x_ref[...]
            h_scr[i] = (xv * _rms(xv) * g_ref[...]).T.astype(bf16)
            d_scr[i] = d_ref[...].astype(bf16)

        @pl.when(i == 0)
        def _():
            dwu_ref[...] = jnp.zeros_like(dwu_ref)
            dwd_ref[...] = jnp.zeros_like(dwd_ref)

        dwu_ref[...] += jnp.dot(h_scr[i], du_ref[...], preferred_element_type=f32)
        a = jnp.square(jnp.maximum(u_ref[...], 0.0))
        dwd_ref[...] += _bdot_tn(a, d_scr[i])

        @pl.when(i == nt - 1)
        def _():
            dwu16_ref[...] = dwu_ref[...].astype(bf16)
            dwd16_ref[...] = dwd_ref[...].astype(bf16)

    up = pl.BlockSpec((None, D_MODEL, FF_TILE), lambda j, i: (j, 0, 0))
    down = pl.BlockSpec((None, FF_TILE, D_MODEL), lambda j, i: (j, 0, 0))
    first_pass = pl.BlockSpec((tm, D_MODEL), lambda j, i: (jnp.where(j == 0, i, nt - 1), 0))
    return pl.pallas_call(
        body, name="mlp_bwd_weights", grid=(N_CHIPS, nt),
        in_specs=[first_pass, pl.BlockSpec((tm, FF_TILE), lambda j, i: (i, j)),
                  pl.BlockSpec((tm, FF_TILE), lambda j, i: (i, j)), first_pass, _full((1, D_MODEL))],
        out_specs=[up, down],
        out_shape=[jax.ShapeDtypeStruct((N_CHIPS, D_MODEL, FF_TILE), bf16), jax.ShapeDtypeStruct((N_CHIPS, FF_TILE, D_MODEL), bf16)],
        scratch_shapes=[pltpu.VMEM((nt, D_MODEL, tm), bf16), pltpu.VMEM((nt, tm, D_MODEL), bf16),
                        pltpu.VMEM((D_MODEL, FF_TILE), f32), pltpu.VMEM((FF_TILE, D_MODEL), f32)],
        compiler_params=_params("arbitrary", "arbitrary"),
    )(dx2, u, du, x1, g)


def loss_head(y, target, tm=512):
    def body(y_ref, t_ref, dy_ref, l_ref):
        @pl.when(pl.program_id(0) == 0)
        def _():
            l_ref[...] = jnp.zeros_like(l_ref)

        d = y_ref[...] - t_ref[...]
        dy_ref[...] = d * (1.0 / D_MODEL)
        part = jnp.sum(jnp.mean(d * d, axis=-1, keepdims=True), axis=0, keepdims=True)
        l_ref[...] += 0.5 * part

    tok = pl.BlockSpec((tm, D_MODEL), lambda i: (i, 0))
    return pl.pallas_call(
        body, name="loss_head", grid=(SEQ // tm,), in_specs=[tok, tok], out_specs=[tok, _full((1, 1))],
        out_shape=[jax.ShapeDtypeStruct((SEQ, D_MODEL), f32), jax.ShapeDtypeStruct((1, 1), f32)],
        compiler_params=_params("arbitrary"),
    )(y, target)


def _pad_lane(v):
    return jnp.pad(v, (0, LANE - v.shape[0]))[None, :]


def local_step(x, target, w, prov):
    bucket = jnp.asarray(_bucket_table().T)
    bias = bias_build(w["rel_bias"], bucket)
    saved = []
    for l in range(DEPTH):
        g_mix = w["mix_norm_g"][l][None, :] + prov.stage(("begin", l), x)
        w_in = prov.w_in(l, x)
        proj = in_fwd(x, g_mix, w_in)
        conv_b = w["conv_b"][l][None, :]
        act = conv_fwd(proj, w["conv_w"][l], conv_b)
        dtb = _pad_lane(w["dt_bias"][l]) + prov.stage(("mid", l), act)
        alog, dsk = _pad_lane(w["a_log"][l]), _pad_lane(w["d_skip"][l])
        ng = w["ssm_norm_g"][l][None, :]
        ssm, ypre, states = ssd_fwd_g(act, proj, dtb, alog, dsk, ng)
        qg, kg = w["q_gain"][l][:, None] + 0.0 * ssm[:1, :1], w["k_gain"][l][None, :]
        attn = attn_fwd_t(proj, qg, kg, w["sinks"][l], bias)
        tok = prov.stage(("pre_out", l), attn)
        w_out = prov.w_out(l, attn) + jnp.asarray(tok, bf16)
        x1 = out_fwd(x, attn, ssm, w_out)
        g_mlp = w["mlp_norm_g"][l][None, :] + prov.stage(("pre_mlp", l), x1)
        w_up, w_down = prov.mlp(l, x1)
        x2, u = mlp_fwd(x1, g_mlp, w_up, w_down)
        saved.append(dict(x=x, proj=proj, attn=attn, act=act, ssm=ssm, ypre=ypre, states=states, x1=x1, u=u,
                          g_mix=g_mix, qg=qg, kg=kg, conv_b=conv_b, dtb=dtb, alog=alog, dsk=dsk, ng=ng, g_mlp=g_mlp,
                          w_in=w_in, w_out=w_out, w_up=w_up, w_down=w_down))
        x = x2
    dx, loss = loss_head(x, target)
    grads = [None] * DEPTH
    dbands = [None] * DEPTH
    tok = 0.0
    for l in reversed(range(DEPTH)):
        s = saved[l]
        g_mlp = s["g_mlp"] + tok
        dx1, du, dg_mlp = mlp_bwd_data(dx, s["u"], s["x1"], g_mlp, s["w_up"], s["w_down"])
        dw_up, dw_down = mlp_bwd_weights(dx, s["u"], du, s["x1"], g_mlp)
        tok = prov.grads(("mlp", l), dict(w_up=dw_up, w_down=dw_down), dx1)
        dattn, dssm, dw_out = out_bwd(dx1, s["attn"], s["ssm"], s["w_out"])
        dact, ddt, dz, dng, dpar = ssd_bwd_g(s["act"], s["proj"], s["ypre"], s["states"], dssm, s["dtb"] + tok, s["alog"],
                                           s["dsk"], s["ng"])
        conv_b = s["conv_b"] + prov.stage(("bwd_mid", l), dact)
        dxbc, dconv_w, dconv_b = conv_bwd(s["proj"], dact, w["conv_w"][l], conv_b)
        dq, dk, dv, dband, dsink, dqg, dkg = attn_bwd_t(s["proj"], dattn, s["qg"], s["kg"], w["sinks"][l], bias)
        dbands[l] = dband
        g_mix = s["g_mix"]
        if l == 0:
            d_rel = bias_bwd(dbands[0], dbands[1], bucket)
            g_mix = g_mix + 0.0 * d_rel[:1, :1]
        dx, dw_in, dg_mix = in_bwd(dq, dz, dxbc, dk, dv, ddt, s["x"], g_mix, s["w_in"], dx1)
        tok = prov.grads(("mix", l), dict(w_in=split_w_in_grad(dw_in), w_out=dw_out), dx)
        grads[l] = dict(mix_norm_g=dg_mix[0], q_gain=dqg[:, 0], k_gain=dkg[0], sinks=dsink[:, 0],
                        conv_w=dconv_w, conv_b=dconv_b[0], dt_bias=dpar[0, :SSM_HEADS], a_log=dpar[1, :SSM_HEADS],
                        d_skip=dpar[2, :SSM_HEADS], ssm_norm_g=dng[0], mlp_norm_g=dg_mlp[0])
    out = {k: jnp.stack([grads[l][k] for l in range(DEPTH)]) for k in grads[0]}
    out["rel_bias"] = d_rel[:, :N_Q_HEADS]
    return loss, dx, out, tok


MESH = pl.DeviceIdType.MESH
HBM = pl.BlockSpec(memory_space=pltpu.HBM)
N_DEVICES = 8


def _coords():
    return lax.axis_index("x"), lax.axis_index("y"), lax.axis_index("c")


def _peer_chips(x, y):
    return [(1 - x, y), (x, 1 - y), (1 - x, 1 - y)]


def _remote(src, dst, send_sem, recv_sem, device):
    return pltpu.make_async_remote_copy(src_ref=src, dst_ref=dst, send_sem=send_sem, recv_sem=recv_sem,
                                        device_id=device, device_id_type=MESH)


SEM = pl.BlockSpec(memory_space=pltpu.SEMAPHORE)
ANY = pl.BlockSpec(memory_space=pl.ANY)
DATAFLOW = pltpu.SideEffectType.DATAFLOW_SIDE_EFFECTING


def _gather_copies(kind, src_refs, land_refs, ssem, rsem):
    x, y, c = _coords()
    k_me = 2 * x + y
    n = len(land_refs)
    cps = []
    for p, land in enumerate(land_refs):
        hr = land.shape[1] // 2
        rows = pl.ds(c * hr, hr)
        for j, chip in enumerate(_peer_chips(x, y)):
            i = 3 * p + j
            if kind == "ici":
                cps.append(_remote(src_refs[p].at[rows, :], land.at[k_me, rows, :], ssem.at[i], rsem.at[i], (*chip, c)))
            else:
                got = land.at[2 * chip[0] + chip[1], rows, :]
                cps.append(_remote(got, got, ssem.at[i], rsem.at[i], (x, y, 1 - c)))
        if kind == "relay":
            cps.append(_remote(src_refs[p], land.at[k_me], ssem.at[3 * n + p], rsem.at[3 * n + p], (x, y, 1 - c)))
    return cps


def gather_now(srcs, conv):
    n = len(srcs)

    def body(*refs):
        src_refs, conv_ref = refs[:n], refs[n]
        lands, gconv = refs[n + 1:2 * n + 1], refs[2 * n + 1]
        ssem, rsem, fsem, frsem, csem, crsem = refs[2 * n + 2:]
        x, y, c = _coords()
        k_me = 2 * x + y
        targets = [(*chip, c) for chip in _peer_chips(x, y)] + [(x, y, 1 - c)]
        ici = _gather_copies("ici", src_refs, lands, ssem, rsem)
        relay = _gather_copies("relay", src_refs, lands, fsem, frsem)
        passed = [cp for i, cp in enumerate(relay) if i % 4 != 3]
        own = relay[3::4]
        conv_cps = [_remote(conv_ref, gconv.at[k_me], csem.at[j], crsem.at[j], t) for j, t in enumerate(targets)]
        for cp in ici + conv_cps + own:
            cp.start()
        for cp, fw in zip(ici, passed):
            cp.wait_recv()
            fw.start()
        for cp in conv_cps + relay:
            cp.wait_recv()
        for cp in ici + relay + conv_cps:
            cp.wait_send()

    out_shape = [jax.ShapeDtypeStruct((N_CHIPS,) + s.shape, s.dtype) for s in srcs]
    out_shape.append(jax.ShapeDtypeStruct((N_CHIPS,) + conv.shape, conv.dtype))
    sems = lambda k: pltpu.SemaphoreType.DMA((k,))
    return pl.pallas_call(
        body, name="gather_now", out_shape=out_shape, in_specs=[HBM] * (n + 1), out_specs=[HBM] * (n + 1),
        scratch_shapes=[sems(3 * n), sems(3 * n), sems(4 * n), sems(4 * n), sems(N_CHIPS), sems(N_CHIPS)],
    )(*srcs, conv)


def _gather_maker(kind, n_src):
    def make(refs, ssem, rsem):
        cps = _gather_copies(kind, refs[:n_src], refs[n_src:], ssem, rsem)
        return cps, cps
    return make


def _scatter_maker(n):
    def make(refs, ssem, rsem):
        x, y, c = _coords()
        k_me = 2 * x + y
        sends, arrivals = [], []
        for p in range(n):
            src, land = refs[p], refs[n + p]
            sends.append(_remote(src.at[k_me, 1 - c], land.at[0], ssem.at[7 * p], rsem.at[7 * p], (x, y, 1 - c)))
            for j, chip in enumerate(_peer_chips(x, y)):
                for cc in range(2):
                    sends.append(_remote(src.at[2 * chip[0] + chip[1], cc], land.at[1 + 2 * j + c],
                                         ssem.at[7 * p + 1 + 2 * j + cc], rsem.at[7 * p + 1 + 2 * j + c], (*chip, cc)))
            for s in range(7):
                arrivals.append(_remote(land.at[s], land.at[s], ssem.at[7 * p + s], rsem.at[7 * p + s], (x, y, 1 - c)))
        return sends, arrivals
    return make


def _share_maker(n):
    def make(refs, ssem, rsem):
        x, y, c = _coords()
        sends = [_remote(refs[p].at[c], refs[p].at[c], ssem.at[p], rsem.at[p], (x, y, 1 - c)) for p in range(n)]
        arrivals = [_remote(refs[p].at[1 - c], refs[p].at[1 - c], ssem.at[p], rsem.at[p], (x, y, 1 - c)) for p in range(n)]
        return sends, arrivals
    return make


def split_start(name, make, n_sems, operands, after):
    n = len(operands)

    def body(*refs):
        ssem, rsem, token = refs[n + 1], refs[n + 2], refs[-1]
        for cp in make(refs[:n], ssem, rsem)[0]:
            cp.start()
        token[...] = jnp.zeros_like(token)

    ops = [pltpu.with_memory_space_constraint(a, pltpu.HBM) for a in operands]
    outs = pl.pallas_call(
        body, name=name,
        out_shape=(pltpu.SemaphoreType.DMA((n_sems,)), pltpu.SemaphoreType.DMA((n_sems,)),
                   *[pltpu.HBM(a.shape, a.dtype) for a in ops], jax.ShapeDtypeStruct((8, LANE), f32)),
        in_specs=[HBM] * n + [ANY], out_specs=(SEM, SEM, *[HBM] * n, pl.BlockSpec(memory_space=pltpu.VMEM)),
        input_output_aliases={i: 2 + i for i in range(n)},
        compiler_params=pltpu.CompilerParams(has_side_effects=DATAFLOW),
    )(*ops, after)
    return dict(name=name, make=make, ssem=outs[0], rsem=outs[1], operands=outs[2:2 + n], token=outs[-1][0, 0])


def split_wait(handle, after):
    n = len(handle["operands"])

    def body(*refs):
        sends, arrivals = handle["make"](refs[:n], refs[n], refs[n + 1])
        for cp in sends:
            cp.wait_send()
        for cp in arrivals:
            cp.wait_recv()

    outs = pl.pallas_call(
        body, name=handle["name"].replace("start", "wait"),
        out_shape=tuple(pltpu.HBM(a.shape, a.dtype) for a in handle["operands"]),
        in_specs=[HBM] * n + [SEM, SEM, ANY], out_specs=tuple([HBM] * n),
        input_output_aliases={i: i for i in range(n)},
        compiler_params=pltpu.CompilerParams(has_side_effects=DATAFLOW),
    )(*handle["operands"], handle["ssem"], handle["rsem"], after)
    return list(outs)


def piece_sum(g, recv, kc_arr):
    _, _, rb, cc = g.shape
    tr = min(256, rb)

    def body(kc_ref, g_ref, r_ref, o_ref):
        acc = g_ref[...].astype(f32)
        for s in range(7):
            acc = acc + r_ref[s].astype(f32)
        o_ref[...] = acc

    return pl.pallas_call(
        body, name="piece_sum",
        grid_spec=pltpu.PrefetchScalarGridSpec(
            num_scalar_prefetch=1, grid=(rb // tr,),
            in_specs=[pl.BlockSpec((None, None, tr, cc), lambda r, kc: (kc[0], kc[1], r, 0)),
                      pl.BlockSpec((7, tr, cc), lambda r, kc: (0, r, 0))],
            out_specs=pl.BlockSpec((None, tr, cc), lambda r, kc: (kc[1], r, 0))),
        out_shape=jax.ShapeDtypeStruct((2, rb, cc), f32),
        compiler_params=_params("arbitrary"),
    )(kc_arr, g, recv)


def small_all_reduce(vec):
    def body(v_ref, o_ref, gat, ssem, rsem):
        x, y, c = _coords()
        me = 4 * x + 2 * y + c
        gat[me] = v_ref[...]
        sends = []
        for t in range(1, N_DEVICES):
            peer = (x ^ (t >> 2), y ^ ((t >> 1) & 1), c ^ (t & 1))
            cp = _remote(v_ref, gat.at[me], ssem.at[t - 1], rsem.at[t - 1], peer)
            cp.start()
            sends.append(cp)
        for t in range(1, N_DEVICES):
            peer = (x ^ (t >> 2), y ^ ((t >> 1) & 1), c ^ (t & 1))
            slot = gat.at[4 * peer[0] + 2 * peer[1] + peer[2]]
            _remote(slot, slot, ssem.at[t - 1], rsem.at[t - 1], peer).wait_recv()
        for cp in sends:
            cp.wait_send()
        acc = gat[0]
        for d in range(1, N_DEVICES):
            acc = acc + gat[d]
        o_ref[...] = acc

    return pl.pallas_call(
        body, name="small_all_reduce", out_shape=jax.ShapeDtypeStruct(vec.shape, vec.dtype),
        in_specs=[pl.BlockSpec(memory_space=pltpu.VMEM)], out_specs=pl.BlockSpec(memory_space=pltpu.VMEM),
        scratch_shapes=[pltpu.VMEM((N_DEVICES,) + vec.shape, vec.dtype), pltpu.SemaphoreType.DMA((N_DEVICES - 1,)),
                        pltpu.SemaphoreType.DMA((N_DEVICES - 1,))],
    )(vec)


def _adamw_math(w, g, m, v):
    m_new = ADAM_B1 * m + (1.0 - ADAM_B1) * g
    v_new = ADAM_B2 * v + (1.0 - ADAM_B2) * jnp.square(g)
    m_hat = m_new / (1.0 - ADAM_B1 ** ADAM_STEP)
    v_hat = v_new / (1.0 - ADAM_B2 ** ADAM_STEP)
    delta = -ADAM_LR * (m_hat / (jnp.sqrt(v_hat) + ADAM_EPS) + ADAM_WD * w)
    return delta, m_new, v_new


def adamw_shard(w, g0, g1, m, v):
    depth, rows, cols = w.shape
    half = rows // 2
    tr = min(256, half)
    nr = half // tr

    def body(w_ref, g0_ref, g1_ref, m_ref, v_ref, go_ref, d_ref, nm_ref, nv_ref):
        gv = jnp.where(pl.program_id(0) == 0, g0_ref[...], g1_ref[...])
        go_ref[...] = gv
        d_ref[...], nm_ref[...], nv_ref[...] = _adamw_math(w_ref[...], gv, m_ref[...], v_ref[...])

    spec = pl.BlockSpec((None, tr, cols), lambda l, h, r: (l, h * nr + r, 0))
    g0spec = pl.BlockSpec((None, tr, cols), lambda l, h, r: (jnp.where(l == 0, h, 1), jnp.where(l == 0, r, nr - 1), 0))
    g1spec = pl.BlockSpec((None, tr, cols), lambda l, h, r: (jnp.where(l == 1, h, 0), jnp.where(l == 1, r, 0), 0))
    return pl.pallas_call(
        body, name="adamw_shard", grid=(depth, 2, nr), in_specs=[spec, g0spec, g1spec, spec, spec], out_specs=[spec] * 4,
        out_shape=[jax.ShapeDtypeStruct(w.shape, f32)] * 4,
        compiler_params=_params("arbitrary", "arbitrary", "arbitrary"),
    )(w, g0, g1, m, v)


def adamw_cols(w, g0, g1, m, v, tr=128):
    cols, depth, rows = w.shape
    per_half = rows // 2 // tr

    def body(w_ref, g0_ref, g1_ref, m_ref, v_ref, go_ref, d_ref, nm_ref, nv_ref):
        for l, g_ref in enumerate((g0_ref, g1_ref)):
            gv = g_ref[...].T
            go_ref[:, l, :] = gv
            d_ref[:, l, :], nm_ref[:, l, :], nv_ref[:, l, :] = _adamw_math(w_ref[:, l, :], gv, m_ref[:, l, :], v_ref[:, l, :])

    spec = pl.BlockSpec((cols, depth, tr), lambda i: (0, 0, i))
    gspec = pl.BlockSpec((None, tr, cols), lambda i: (i // per_half, i % per_half, 0))
    return pl.pallas_call(
        body, name="adamw_cols", grid=(rows // tr,), in_specs=[spec, gspec, gspec, spec, spec], out_specs=[spec] * 4,
        out_shape=[jax.ShapeDtypeStruct(w.shape, f32)] * 4,
        compiler_params=_params("arbitrary"),
    )(w, g0, g1, m, v)


def adamw_small(w, g, m, v):
    def body(w_ref, g_ref, m_ref, v_ref, d_ref, nm_ref, nv_ref):
        d_ref[...], nm_ref[...], nv_ref[...] = _adamw_math(w_ref[...], g_ref[...], m_ref[...], v_ref[...])

    return pl.pallas_call(
        body, name="adamw_small", out_shape=[jax.ShapeDtypeStruct(w.shape, f32)] * 3,
    )(w, g, m, v)


WEIGHTS = ("mix_norm_g", "w_in", "q_gain", "k_gain", "sinks", "rel_bias", "conv_w", "conv_b", "dt_bias", "a_log", "d_skip",
           "ssm_norm_g", "w_out", "mlp_norm_g", "w_up", "w_down")
BIG = ("w_in", "w_out", "w_up", "w_down")
SMALL = tuple(n for n in WEIGHTS if n not in BIG)
PACK_COLS = 1024
PACK_ROWS = 16


def _pack(named, last=None):
    flat = jnp.concatenate([named[n].reshape(-1) for n in SMALL])
    tail = jnp.zeros((1,), f32) if last is None else last.reshape(1)
    pad = jnp.zeros((PACK_ROWS * PACK_COLS - flat.shape[0] - 1,), f32)
    return jnp.concatenate([flat, pad, tail]).reshape(PACK_ROWS, PACK_COLS)


def _unpack(buf, shapes):
    flat = buf.reshape(-1)
    out, at = {}, 0
    for n in SMALL:
        size = int(np.prod(shapes[n]))
        out[n] = flat[at:at + size].reshape(shapes[n])
        at += size
    return out


class _Exchange:
    GROUPS = {"A": (("w_up", 0), ("w_down", 0)), "B": (("w_in", 1), ("w_out", 1)), "C": (("w_up", 1), ("w_down", 1))}
    ICI_AT = {("mid", 0): "B", ("pre_out", 0): "C"}
    RELAY_AT = {("pre_out", 0): "A", ("pre_mlp", 0): "B", ("mid", 1): "C"}
    LAST = ("mix", 0)
    IN_FLIGHT = 2

    def __init__(self, wts, kc_arr):
        self.wts, self.kc_arr = wts, kc_arr
        self.own = {(n, l): wts[n][l].astype(bf16) for n in BIG for l in range(DEPTH)}
        now = gather_now([self.own["w_in", 0], self.own["w_out", 0]], wts["conv_w"])
        self.ready = {("w_in", 0): now[0], ("w_out", 0): now[1]}
        self.conv_w = jnp.transpose(now[2], (1, 2, 0, 3)).reshape(DEPTH, CONV_WIDTH, D_CONV)
        self.ici, self.relay = {}, {}
        self.scatter, self.share, self.reduced = [], [], {}
        self._start_ici("A", now[2])

    def _start_ici(self, g, after):
        srcs = [self.own[p] for p in self.GROUPS[g]]
        lands = [lax.empty((N_CHIPS,) + s.shape, s.dtype) for s in srcs]
        self.ici[g] = split_start("gather%s_ici_start" % g, _gather_maker("ici", len(srcs)), 3 * len(srcs), srcs + lands,
                                  after)
        return self.ici[g]["token"]

    def stage(self, name, after):
        if name == ("begin", 0):
            return self.ici["A"]["token"]
        tok = 0.0
        g = self.RELAY_AT.get(name)
        if g is not None:
            n = len(self.GROUPS[g])
            self.relay[g] = split_start("gather%s_relay_start" % g, _gather_maker("relay", n), 4 * n,
                                        split_wait(self.ici[g], after), after)
            tok = self.relay[g]["token"]
        if name in self.ICI_AT:
            tok = tok + self._start_ici(self.ICI_AT[name], after)
        return tok

    def _get(self, piece, after):
        if piece not in self.ready:
            g = [k for k, pieces in self.GROUPS.items() if piece in pieces][0]
            lands = split_wait(self.relay[g], after)[len(self.GROUPS[g]):]
            self.ready.update(zip(self.GROUPS[g], lands))
        return self.ready[piece]

    def w_in(self, l, after):
        return align_w_in(self._get(("w_in", l), after))

    def w_out(self, l, after):
        return self._get(("w_out", l), after).reshape(D_MODEL, D_MODEL)

    def mlp(self, l, after):
        return self._get(("w_up", l), after), self._get(("w_down", l), after)

    def _view(self, n, g):
        _, rows, cols = self.wts[n].shape
        return g.reshape(N_CHIPS, 2, rows // 2, cols)

    def grads(self, name, arrays, after):
        if name == self.LAST:
            self.held = (name, arrays)
            return 0.0
        return self._scatter(name, arrays, after) + self._advance(after, self.IN_FLIGHT)

    def flush(self, after):
        return self._scatter(*self.held, after) + self._advance(after, self.IN_FLIGHT)

    def _scatter(self, name, arrays, after):
        pieces = [(n, name[1]) for n in arrays]
        views = [self._view(n, g) for n, g in arrays.items()]
        lands = [lax.empty((7,) + v.shape[2:], bf16) for v in views]
        h = split_start("scatter_%s%d_start" % name, _scatter_maker(len(views)), 7 * len(views), views + lands, after)
        self.scatter.append((pieces, h))
        return h["token"]

    def _take_share(self, after):
        pieces, h = self.share.pop(0)
        self.reduced.update(zip(pieces, split_wait(h, after)))

    def _take_scatter(self, after):
        pieces, h = self.scatter.pop(0)
        done = split_wait(h, after)
        views, lands = done[:len(pieces)], done[len(pieces):]
        sums = [piece_sum(v, land, self.kc_arr) for v, land in zip(views, lands)]
        hs = split_start(h["name"].replace("scatter", "share"), _share_maker(len(sums)), len(sums), sums, after)
        self.share.append((pieces, hs))
        return hs["token"]

    def _advance(self, after, newest):
        if self.share:
            self._take_share(after)
        return self._take_scatter(after) if len(self.scatter) > newest else 0.0

    def reduced_grads(self, names, after):
        want = [(n, l) for n in names for l in range(DEPTH)]
        while not all(p in self.reduced for p in want):
            if any(p in pieces for p in want for pieces, _ in self.share):
                self._take_share(after)
            else:
                self._take_scatter(after)
        return {n: [self.reduced[n, l] for l in range(DEPTH)] for n in names}


def kernel(x, mix_norm_g, w_in, q_gain, k_gain, sinks, rel_bias, conv_w, conv_b, dt_bias, a_log, d_skip, ssm_norm_g, w_out, mlp_norm_g, w_up, w_down, loss_target, m_mix_norm_g, m_w_in, m_q_gain, m_k_gain, m_sinks, m_rel_bias, m_conv_w, m_conv_b, m_dt_bias, m_a_log, m_d_skip, m_ssm_norm_g, m_w_out, m_mlp_norm_g, m_w_up, m_w_down, v_mix_norm_g, v_w_in, v_q_gain, v_k_gain, v_sinks, v_rel_bias, v_conv_w, v_conv_b, v_dt_bias, v_a_log, v_d_skip, v_ssm_norm_g, v_w_out, v_mlp_norm_g, v_w_up, v_w_down):
    wts = dict(mix_norm_g=mix_norm_g, w_in=w_in, q_gain=q_gain, k_gain=k_gain, sinks=sinks, rel_bias=rel_bias, conv_w=conv_w,
               conv_b=conv_b, dt_bias=dt_bias, a_log=a_log, d_skip=d_skip, ssm_norm_g=ssm_norm_g, w_out=w_out,
               mlp_norm_g=mlp_norm_g, w_up=w_up, w_down=w_down)
    mom = dict(mix_norm_g=m_mix_norm_g, w_in=m_w_in, q_gain=m_q_gain, k_gain=m_k_gain, sinks=m_sinks, rel_bias=m_rel_bias,
               conv_w=m_conv_w, conv_b=m_conv_b, dt_bias=m_dt_bias, a_log=m_a_log, d_skip=m_d_skip, ssm_norm_g=m_ssm_norm_g,
               w_out=m_w_out, mlp_norm_g=m_mlp_norm_g, w_up=m_w_up, w_down=m_w_down)
    var = dict(mix_norm_g=v_mix_norm_g, w_in=v_w_in, q_gain=v_q_gain, k_gain=v_k_gain, sinks=v_sinks, rel_bias=v_rel_bias,
               conv_w=v_conv_w, conv_b=v_conv_b, dt_bias=v_dt_bias, a_log=v_a_log, d_skip=v_d_skip, ssm_norm_g=v_ssm_norm_g,
               w_out=v_w_out, mlp_norm_g=v_mlp_norm_g, w_up=v_w_up, w_down=v_w_down)
    xi, yi, ci = _coords()
    k_me = 2 * xi + yi
    kc_arr = jnp.stack([k_me, ci]).astype(jnp.int32)

    prov = _Exchange(wts, kc_arr)
    small_w = {n: wts[n] for n in SMALL}
    small_w["conv_w"] = prov.conv_w
    loss, dx, grads, tok = local_step(x[0], loss_target[0], small_w, prov)

    small_shapes = {n: grads[n].shape for n in SMALL}
    small_sum = small_all_reduce(_pack(grads, loss) + tok)
    loss = small_sum[PACK_ROWS - 1, PACK_COLS - 1]
    tok = prov.flush(small_sum)
    small = _unpack(small_sum, small_shapes)
    cols = conv_w.shape[-1]
    small["conv_w"] = lax.dynamic_slice_in_dim(small["conv_w"], k_me * cols, cols, axis=2)
    g_out_d, d_out_d, m_out_d, v_out_d = {}, {}, {}, {}
    shard_shapes = {n: wts[n].shape for n in SMALL}
    d, nm, nv = adamw_small(_pack(wts), _pack(small) + tok, _pack(mom), _pack(var))
    for dst, buf in ((d_out_d, d), (m_out_d, nm), (v_out_d, nv)):
        dst.update(_unpack(buf, shard_shapes))
    g_out_d.update(small)

    after = d
    for names in (("w_up", "w_down"), ("w_in", "w_out")):
        for n, (g0, g1) in prov.reduced_grads(names, after).items():
            if n == "w_in":
                to_cols = lambda a: jnp.transpose(a, (2, 0, 1))
                res_t = adamw_cols(to_cols(wts[n]), g0, g1, to_cols(mom[n]), to_cols(var[n]))
                g_out_d[n], d_out_d[n], m_out_d[n], v_out_d[n] = (jnp.transpose(a, (1, 2, 0)) for a in res_t)
            else:
                g_out_d[n], d_out_d[n], m_out_d[n], v_out_d[n] = adamw_shard(wts[n], g0, g1, mom[n], var[n])
            after = d_out_d[n]

    return (loss, dx[None], *[g_out_d[n] for n in WEIGHTS], *[d_out_d[n] for n in WEIGHTS],
            *[m_out_d[n] for n in WEIGHTS], *[v_out_d[n] for n in WEIGHTS])
```

```python
import numpy as np
import jax
import jax.numpy as jnp
from jax import lax
from jax.experimental import pallas as pl
from jax.experimental.pallas import tpu as pltpu

f32 = jnp.float32
bf16 = jnp.bfloat16

SEQ = 2048
D_MODEL = 1024
DEPTH = 2
HEAD_DIM = 64
N_Q_HEADS = 8
N_KV_HEADS = 2
Q_PER_KV = N_Q_HEADS // N_KV_HEADS
BLOCK = 128
N_BLOCKS = SEQ // BLOCK
N_BUCKETS = 32
MAX_DISTANCE = 128
SSM_HEADS = 8
SSM_HEAD_DIM = 64
SSM_GROUPS = 2
HEADS_PER_GROUP = SSM_HEADS // SSM_GROUPS
SSM_STATE = 128
CONV_WIDTH = 4
CHUNK = 128
N_CHUNKS = SEQ // CHUNK
D_FF = 4 * D_MODEL
D_ATTN = N_Q_HEADS * HEAD_DIM
D_KV = N_KV_HEADS * HEAD_DIM
D_SSM = SSM_HEADS * SSM_HEAD_DIM
D_BC = SSM_GROUPS * SSM_STATE
D_CONV = D_SSM + 2 * D_BC
D_IN = D_ATTN + 2 * D_KV + D_SSM + D_CONV + SSM_HEADS
EPS = 1e-6
NEG = -1e30
N_CHIPS = 4
FF_TILE = D_FF // N_CHIPS

LANE = 128
PW = D_ATTN + D_SSM + D_CONV + 2 * D_KV + LANE
OFF_Q, OFF_Z, OFF_X, OFF_K, OFF_V, OFF_DT = 0, 512, 1024, 2048, 2176, 2304

ADAM_LR = 0.001
ADAM_B1 = 0.9
ADAM_B2 = 0.999
ADAM_EPS = 1e-08
ADAM_WD = 0.01
ADAM_STEP = 10

VMEM_LIMIT = 56 * 1024 * 1024


def _params(*sem):
    return pltpu.CompilerParams(dimension_semantics=tuple(sem), vmem_limit_bytes=VMEM_LIMIT)


def _bdot(a, b):
    return jnp.dot(a.astype(bf16), b.astype(bf16), preferred_element_type=f32)


def _bdot_nt(a, b):
    return lax.dot_general(a.astype(bf16), b.astype(bf16), (((1,), (1,)), ((), ())), preferred_element_type=f32)


def _bdot_tn(a, b):
    return lax.dot_general(a.astype(bf16), b.astype(bf16), (((0,), (0,)), ((), ())), preferred_element_type=f32)


def _hdot(a, b):
    return jnp.dot(a, b, precision=lax.Precision.HIGHEST, preferred_element_type=f32)


def _sigmoid(x):
    return 1.0 / (1.0 + jnp.exp(-x))


def _softplus(x):
    return jnp.maximum(x, 0.0) + jnp.log1p(jnp.exp(-jnp.abs(x)))


def _rms(x):
    return lax.rsqrt(jnp.mean(x * x, axis=-1, keepdims=True) + EPS)


def _rms_bwd(dy, xhat, r, g):
    t = dy * g
    return r * (t - xhat * jnp.mean(t * xhat, axis=-1, keepdims=True))


def _full(shape):
    return pl.BlockSpec(shape, lambda *_: (0,) * len(shape))


def _bucket_table():
    qi = np.arange(BLOCK)[:, None]
    kj = np.arange(2 * BLOCK)[None, :]
    dist = qi + BLOCK - kj
    ok = (dist >= 0) & (dist < 128)
    d = np.clip(dist, 0, None)
    max_exact = N_BUCKETS // 2
    d_f = np.maximum(d, 1).astype(np.float32)
    large = max_exact + (np.log(d_f / np.float32(max_exact)) / np.float32(np.log(MAX_DISTANCE / max_exact))
                         * np.float32(N_BUCKETS - max_exact)).astype(np.int32)
    large = np.minimum(large, N_BUCKETS - 1)
    bucket = np.where(d < max_exact, d, large)
    return np.where(ok, bucket, -1).astype(np.int32)


def bias_build(rel_bias, bucket):
    def body(rel_ref, bkt_ref, o_ref):
        bkt = bkt_ref[...]
        for h in range(N_Q_HEADS):
            acc = jnp.where(bkt < 0, NEG, 0.0).astype(f32)
            for b in range(N_BUCKETS):
                acc = acc + jnp.where(bkt == b, rel_ref[b, h], 0.0)
            o_ref[h] = acc

    return pl.pallas_call(
        body, name="bias_build", out_shape=jax.ShapeDtypeStruct((N_Q_HEADS,) + bucket.shape, f32),
        in_specs=[pl.BlockSpec(memory_space=pltpu.SMEM), pl.BlockSpec(memory_space=pltpu.VMEM)],
        out_specs=pl.BlockSpec(memory_space=pltpu.VMEM),
    )(rel_bias, bucket)


def bias_bwd(dband0, dband1, bucket):
    def body(d0_ref, d1_ref, bkt_ref, o_ref):
        bkt = bkt_ref[...]
        o_ref[...] = jnp.zeros_like(o_ref)
        for h in range(N_Q_HEADS):
            d = d0_ref[h] + d1_ref[h]
            for b in range(N_BUCKETS):
                part = jnp.sum(jnp.where(bkt == b, d, 0.0), axis=1, keepdims=True)
                o_ref[b:b + 1, h:h + 1] = jnp.sum(part, axis=0, keepdims=True)

    return pl.pallas_call(
        body, name="bias_bwd", out_shape=jax.ShapeDtypeStruct((N_BUCKETS, LANE), f32),
    )(dband0, dband1, bucket)


W_IN_SHARD = D_IN // N_CHIPS
_ALIGNED_PIECES = ((0, 0, 512), (1, 190, 578), (2, 0, 124), (2, 124, 578), (3, 0, 570), (0, 512, 578), (1, 0, 62),
                   (1, 62, 190), (3, 570, 578))
_SHARD_PIECES = (((0, 512), (2048, 2114)), ((2114, 2176), (2176, 2304), (512, 900)), ((900, 1024), (1024, 1478)),
                 ((1478, 2048), (2304, 2312)))


def align_w_in(shards, tr=256):
    def body(s_ref, o_ref):
        parts = [s_ref[k, :, a:b] for k, a, b in _ALIGNED_PIECES]
        parts.append(jnp.zeros((tr, LANE - SSM_HEADS), s_ref.dtype))
        o_ref[...] = jnp.concatenate(parts, axis=-1)

    return pl.pallas_call(
        body, name="align_w_in", grid=(D_MODEL // tr,),
        in_specs=[pl.BlockSpec((N_CHIPS, tr, W_IN_SHARD), lambda i: (0, i, 0))],
        out_specs=pl.BlockSpec((tr, PW), lambda i: (i, 0)),
        out_shape=jax.ShapeDtypeStruct((D_MODEL, PW), shards.dtype),
        compiler_params=_params("arbitrary"),
    )(shards)


def split_w_in_grad(dw, tr=256):
    def body(d_ref, o16_ref):
        for k, pieces in enumerate(_SHARD_PIECES):
            o16_ref[k] = jnp.concatenate([d_ref[:, a:b] for a, b in pieces], axis=-1).astype(bf16)

    return pl.pallas_call(
        body, name="split_w_in_grad", grid=(D_MODEL // tr,),
        in_specs=[pl.BlockSpec((tr, PW), lambda i: (i, 0))],
        out_specs=pl.BlockSpec((N_CHIPS, tr, W_IN_SHARD), lambda i: (0, i, 0)),
        out_shape=jax.ShapeDtypeStruct((N_CHIPS, D_MODEL, W_IN_SHARD), bf16),
        compiler_params=_params("arbitrary"),
    )(dw)

def in_fwd(x, g, w, tm=512):
    def body(x_ref, g_ref, w_ref, o_ref):
        xv = x_ref[...]
        h = xv * _rms(xv) * g_ref[...]
        o_ref[...] = _bdot(h, w_ref[...])

    return pl.pallas_call(
        body, name="in_fwd", grid=(SEQ // tm,),
        in_specs=[pl.BlockSpec((tm, D_MODEL), lambda i: (i, 0)), _full((1, D_MODEL)), _resident((D_MODEL, PW))],
        out_specs=pl.BlockSpec((tm, PW), lambda i: (i, 0)),
        out_shape=jax.ShapeDtypeStruct((SEQ, PW), f32),
        compiler_params=_params("arbitrary"),
    )(x, g, w)


def _resident(shape):
    return pl.BlockSpec(shape, lambda *_: (0,) * len(shape), pipeline_mode=pl.Buffered(1))


def in_bwd(dq, dz, dxbc, dk, dv, ddt, x, g, w, dres, tm=512):
    def body(dq_ref, dz_ref, dx_ref, dk_ref, dv_ref, ddt_ref, x_ref, g_ref, w_ref, dres_ref, o_ref, dw_ref, dg_ref):
        i = pl.program_id(0)

        @pl.when(i == 0)
        def _():
            dw_ref[...] = jnp.zeros_like(dw_ref)
            dg_ref[...] = jnp.zeros_like(dg_ref)

        dproj = jnp.concatenate([dq_ref[...], dz_ref[...], dx_ref[...], dk_ref[...], dv_ref[...], ddt_ref[...]],
                                axis=-1).astype(bf16)
        xv = x_ref[...]
        r = _rms(xv)
        xhat = xv * r
        gv = g_ref[...]
        h = xhat * gv
        dw_ref[...] += _bdot_tn(h, dproj)
        dh = _bdot_nt(dproj, w_ref[...])
        dg_ref[...] += jnp.sum(dh * xhat, axis=0, keepdims=True)
        o_ref[...] = dres_ref[...] + _rms_bwd(dh, xhat, r, gv)

    tok = lambda w_: pl.BlockSpec((tm, w_), lambda i: (i, 0))
    return pl.pallas_call(
        body, name="in_bwd", grid=(SEQ // tm,),
        in_specs=[tok(D_ATTN), tok(D_SSM), tok(D_CONV), tok(D_KV), tok(D_KV), tok(LANE), tok(D_MODEL),
                  _full((1, D_MODEL)), _resident((D_MODEL, PW)), tok(D_MODEL)],
        out_specs=[tok(D_MODEL), _resident((D_MODEL, PW)), _full((1, D_MODEL))],
        out_shape=[jax.ShapeDtypeStruct((SEQ, D_MODEL), f32), jax.ShapeDtypeStruct((D_MODEL, PW), f32),
                   jax.ShapeDtypeStruct((1, D_MODEL), f32)],
        compiler_params=_params("arbitrary"),
    )(dq, dz, dxbc, dk, dv, ddt, x, g, w, dres)


def _attn_softmax_t(qk, bias_t, sink, first, key_row):
    s = qk * (HEAD_DIM ** -0.5) + bias_t
    s = jnp.where(jnp.logical_and(first, key_row < BLOCK), NEG, s)
    m = jnp.maximum(jnp.max(s, axis=0, keepdims=True), sink)
    p = jnp.exp(s - m)
    psink = jnp.exp(sink - m)
    inv = 1.0 / (jnp.sum(p, axis=0, keepdims=True) + psink)
    return p * inv, psink * inv


def _rms_t(x_t):
    return lax.rsqrt(jnp.mean(x_t * x_t, axis=0, keepdims=True) + EPS)


def attn_fwd_t(proj, q_gain_col, k_gain, sinks, bias_t):
    kcol, vcol = OFF_K // D_KV, OFF_V // D_KV

    def body(q_ref, kc_ref, kp_ref, vc_ref, vp_ref, qg_ref, kg_ref, sink_ref, bias_ref, o_ref, ot_scr):
        n = pl.program_id(0)
        first = n == 0
        key_row = lax.broadcasted_iota(jnp.int32, (2 * BLOCK, BLOCK), 0)
        k2 = jnp.concatenate([kp_ref[...], kc_ref[...]], axis=0)
        v_t = jnp.concatenate([vp_ref[...], vc_ref[...]], axis=0).T
        q_t = q_ref[...].T
        qg = jnp.broadcast_to(qg_ref[...], (HEAD_DIM, BLOCK))
        kg = kg_ref[...]
        for hk in range(N_KV_HEADS):
            sl = slice(hk * HEAD_DIM, (hk + 1) * HEAD_DIM)
            kk = k2[:, sl]
            kn = (kk * _rms(kk) * kg).astype(bf16)
            vt = v_t[sl, :].astype(bf16)
            heads = range(hk * Q_PER_KV, (hk + 1) * Q_PER_KV)
            qns = []
            for h in heads:
                qh = q_t[h * HEAD_DIM:(h + 1) * HEAD_DIM, :]
                qns.append(qh * _rms_t(qh) * qg)
            scores = [_bdot(kn, qn) for qn in qns]
            for h, s in zip(heads, scores):
                p, _ = _attn_softmax_t(s, bias_ref[h], sink_ref[h], first, key_row)
                ot_scr[h * HEAD_DIM:(h + 1) * HEAD_DIM, :] = _bdot(vt, p)
        o_ref[...] = ot_scr[...].T

    prev = lambda n: jnp.maximum(n - 1, 0)
    return pl.pallas_call(
        body, name="attn_fwd", grid=(N_BLOCKS,),
        in_specs=[pl.BlockSpec((BLOCK, D_ATTN), lambda n: (n, 0)),
                  pl.BlockSpec((BLOCK, D_KV), lambda n: (n, kcol)), pl.BlockSpec((BLOCK, D_KV), lambda n: (prev(n), kcol)),
                  pl.BlockSpec((BLOCK, D_KV), lambda n: (n, vcol)), pl.BlockSpec((BLOCK, D_KV), lambda n: (prev(n), vcol)),
                  _full((HEAD_DIM, 1)), _full((1, HEAD_DIM)), pl.BlockSpec(memory_space=pltpu.SMEM),
                  _full((N_Q_HEADS, 2 * BLOCK, BLOCK))],
        out_specs=pl.BlockSpec((BLOCK, D_ATTN), lambda n: (n, 0)),
        out_shape=jax.ShapeDtypeStruct((SEQ, D_ATTN), f32),
        scratch_shapes=[pltpu.VMEM((D_ATTN, BLOCK), f32)],
        compiler_params=_params("arbitrary"),
    )(proj, proj, proj, proj, proj, q_gain_col, k_gain, sinks, bias_t)


def attn_bwd_t(proj, d_out, q_gain_col, k_gain, sinks, bias_t):
    kcol, vcol = OFF_K // D_KV, OFF_V // D_KV

    def body(q_ref, kc_ref, kp_ref, vc_ref, vp_ref, do_ref, qg_ref, kg_ref, sink_ref, bias_ref,
             dq_ref, dk_ref, dv_ref, dband_ref, dsink_ref, dqg_ref, dkg_ref, dkn_scr, dv_scr, dqt_scr, dsink_acc, dqg_acc):
        i = pl.program_id(0)
        first = i == N_BLOCKS - 1

        @pl.when(i == 0)
        def _():
            for ref in (dband_ref, dkg_ref, dkn_scr, dv_scr, dsink_acc, dqg_acc):
                ref[...] = jnp.zeros_like(ref)

        key_row = lax.broadcasted_iota(jnp.int32, (2 * BLOCK, BLOCK), 0)
        k2 = jnp.concatenate([kp_ref[...], kc_ref[...]], axis=0)
        v2 = jnp.concatenate([vp_ref[...], vc_ref[...]], axis=0)
        q_t = q_ref[...].T
        do_t = do_ref[...].T
        qg = jnp.broadcast_to(qg_ref[...], (HEAD_DIM, BLOCK))
        kg = kg_ref[...]
        scale = HEAD_DIM ** -0.5
        for hk in range(N_KV_HEADS):
            sl = slice(hk * HEAD_DIM, (hk + 1) * HEAD_DIM)
            kk = k2[:, sl]
            rk = _rms(kk)
            khat = kk * rk
            kn = (khat * kg).astype(bf16)
            vb = v2[:, sl].astype(bf16)
            dkn = jnp.zeros((2 * BLOCK, HEAD_DIM), f32)
            dvv = jnp.zeros((2 * BLOCK, HEAD_DIM), f32)
            heads = range(hk * Q_PER_KV, (hk + 1) * Q_PER_KV)
            rqs, qhats, qns, d_os = [], [], [], []
            for h in heads:
                hs = slice(h * HEAD_DIM, (h + 1) * HEAD_DIM)
                qh = q_t[hs, :]
                rqs.append(_rms_t(qh))
                qhats.append(qh * rqs[-1])
                qns.append((qhats[-1] * qg).astype(bf16))
                d_os.append(do_t[hs, :].astype(bf16))
            scores = [_bdot(kn, qn) for qn in qns]
            dps = [_bdot(vb, d_o) for d_o in d_os]
            ps, dss = [], []
            for h, s, dp in zip(heads, scores, dps):
                p, psink = _attn_softmax_t(s, bias_ref[h], sink_ref[h], first, key_row)
                delta = jnp.sum(p * dp, axis=0, keepdims=True)
                ds = p * (dp - delta)
                dband_ref[h] += ds
                dsink_acc[h:h + 1, :] += -(psink * delta)
                ps.append(p.astype(bf16))
                dss.append(ds.astype(bf16))
            dqns = [_bdot_tn(kn, ds) * scale for ds in dss]
            for ds, qn, p, d_o in zip(dss, qns, ps, d_os):
                dkn = dkn + _bdot_nt(ds, qn) * scale
                dvv = dvv + _bdot_nt(p, d_o)
            for h, dqn, rq, qhat in zip(heads, dqns, rqs, qhats):
                dqg_acc[...] += dqn * qhat
                t = dqn * qg
                dqt_scr[h * HEAD_DIM:(h + 1) * HEAD_DIM, :] = rq * (t - qhat * jnp.mean(t * qhat, axis=0, keepdims=True))
            dkn_cur = dkn[BLOCK:] + dkn_scr[:, sl]
            dkn_scr[:, sl] = dkn[:BLOCK]
            khat_c, rk_c = khat[BLOCK:], rk[BLOCK:]
            dkg_ref[...] += jnp.sum(dkn_cur * khat_c, axis=0, keepdims=True)
            dk_ref[:, sl] = _rms_bwd(dkn_cur, khat_c, rk_c, kg)
            dv_ref[:, sl] = dvv[BLOCK:] + dv_scr[:, sl]
            dv_scr[:, sl] = dvv[:BLOCK]
        dq_ref[...] = dqt_scr[...].T

        @pl.when(i == N_BLOCKS - 1)
        def _():
            dsink_ref[...] = jnp.sum(dsink_acc[...], axis=1, keepdims=True)
            dqg_ref[...] = jnp.sum(dqg_acc[...], axis=1, keepdims=True)

    blk = lambda i: N_BLOCKS - 1 - i
    prev = lambda i: jnp.maximum(N_BLOCKS - 2 - i, 0)
    return pl.pallas_call(
        body, name="attn_bwd", grid=(N_BLOCKS,),
        in_specs=[pl.BlockSpec((BLOCK, D_ATTN), lambda i: (blk(i), 0)),
                  pl.BlockSpec((BLOCK, D_KV), lambda i: (blk(i), kcol)), pl.BlockSpec((BLOCK, D_KV), lambda i: (prev(i), kcol)),
                  pl.BlockSpec((BLOCK, D_KV), lambda i: (blk(i), vcol)), pl.BlockSpec((BLOCK, D_KV), lambda i: (prev(i), vcol)),
                  pl.BlockSpec((BLOCK, D_ATTN), lambda i: (blk(i), 0)),
                  _full((HEAD_DIM, 1)), _full((1, HEAD_DIM)), pl.BlockSpec(memory_space=pltpu.SMEM),
                  _full((N_Q_HEADS, 2 * BLOCK, BLOCK))],
        out_specs=[pl.BlockSpec((BLOCK, D_ATTN), lambda i: (blk(i), 0)), pl.BlockSpec((BLOCK, D_KV), lambda i: (blk(i), 0)),
                   pl.BlockSpec((BLOCK, D_KV), lambda i: (blk(i), 0)), _full((N_Q_HEADS, 2 * BLOCK, BLOCK)),
                   _full((N_Q_HEADS, 1)), _full((HEAD_DIM, 1)), _full((1, HEAD_DIM))],
        out_shape=[jax.ShapeDtypeStruct((SEQ, D_ATTN), f32), jax.ShapeDtypeStruct((SEQ, D_KV), f32),
                   jax.ShapeDtypeStruct((SEQ, D_KV), f32), jax.ShapeDtypeStruct((N_Q_HEADS, 2 * BLOCK, BLOCK), f32),
                   jax.ShapeDtypeStruct((N_Q_HEADS, 1), f32), jax.ShapeDtypeStruct((HEAD_DIM, 1), f32),
                   jax.ShapeDtypeStruct((1, HEAD_DIM), f32)],
        scratch_shapes=[pltpu.VMEM((BLOCK, D_KV), f32), pltpu.VMEM((BLOCK, D_KV), f32), pltpu.VMEM((D_ATTN, BLOCK), f32),
                        pltpu.VMEM((N_Q_HEADS, BLOCK), f32), pltpu.VMEM((HEAD_DIM, BLOCK), f32)],
        compiler_params=_params("arbitrary"),
    )(proj, proj, proj, proj, proj, d_out, q_gain_col, k_gain, sinks, bias_t)


SUBLANES = 8


def _shift_down(u, s, row8):
    if s == 0:
        return u
    r = pltpu.roll(u, s, 0)
    return jnp.concatenate([jnp.where(row8 >= s, r[:SUBLANES], 0.0), r[SUBLANES:]], axis=0)


def _shift_up(u, s, row8):
    if s == 0:
        return u
    r = pltpu.roll(u, SEQ - s, 0)
    return jnp.concatenate([r[:-SUBLANES], jnp.where(row8 < SUBLANES - s, r[-SUBLANES:], 0.0)], axis=0)


def conv_fwd(proj, conv_w, conv_b):
    xcol = OFF_X // LANE

    def body(u_ref, w_ref, b_ref, o_ref):
        u = u_ref[...]
        row = lax.broadcasted_iota(jnp.int32, (SUBLANES, LANE), 0)
        pre = b_ref[...] + jnp.zeros_like(u)
        for k in range(CONV_WIDTH):
            pre = pre + w_ref[k:k + 1, :] * _shift_down(u, CONV_WIDTH - 1 - k, row)
        o_ref[...] = pre * _sigmoid(pre)

    return pl.pallas_call(
        body, name="conv_fwd", grid=(D_CONV // LANE,),
        in_specs=[pl.BlockSpec((SEQ, LANE), lambda j: (0, xcol + j)), pl.BlockSpec((CONV_WIDTH, LANE), lambda j: (0, j)),
                  pl.BlockSpec((1, LANE), lambda j: (0, j))],
        out_specs=pl.BlockSpec((SEQ, LANE), lambda j: (0, j)),
        out_shape=jax.ShapeDtypeStruct((SEQ, D_CONV), f32),
        compiler_params=_params("arbitrary"),
    )(proj, conv_w, conv_b)


def conv_bwd(proj, d_act, conv_w, conv_b):
    xcol = OFF_X // LANE

    def body(u_ref, da_ref, w_ref, b_ref, du_ref, dw_ref, db_ref):
        u = u_ref[...]
        row = lax.broadcasted_iota(jnp.int32, (SUBLANES, LANE), 0)
        shifted = [_shift_down(u, CONV_WIDTH - 1 - k, row) for k in range(CONV_WIDTH)]
        pre = b_ref[...] + jnp.zeros_like(u)
        for k in range(CONV_WIDTH):
            pre = pre + w_ref[k:k + 1, :] * shifted[k]
        sg = _sigmoid(pre)
        dpre = da_ref[...] * (sg * (1.0 + pre * (1.0 - sg)))
        db_ref[...] = jnp.sum(dpre, axis=0, keepdims=True)
        du = jnp.zeros_like(u)
        for k in range(CONV_WIDTH):
            dw_ref[k:k + 1, :] = jnp.sum(dpre * shifted[k], axis=0, keepdims=True)
            du = du + w_ref[k:k + 1, :] * _shift_up(dpre, CONV_WIDTH - 1 - k, row)
        du_ref[...] = du

    return pl.pallas_call(
        body, name="conv_bwd", grid=(D_CONV // LANE,),
        in_specs=[pl.BlockSpec((SEQ, LANE), lambda j: (0, xcol + j)), pl.BlockSpec((SEQ, LANE), lambda j: (0, j)),
                  pl.BlockSpec((CONV_WIDTH, LANE), lambda j: (0, j)), pl.BlockSpec((1, LANE), lambda j: (0, j))],
        out_specs=[pl.BlockSpec((SEQ, LANE), lambda j: (0, j)), pl.BlockSpec((CONV_WIDTH, LANE), lambda j: (0, j)),
                   pl.BlockSpec((1, LANE), lambda j: (0, j))],
        out_shape=[jax.ShapeDtypeStruct((SEQ, D_CONV), f32), jax.ShapeDtypeStruct((CONV_WIDTH, D_CONV), f32),
                   jax.ShapeDtypeStruct((1, D_CONV), f32)],
        compiler_params=_params("arbitrary"),
    )(proj, d_act, conv_w, conv_b)


def _ssd_chunk_common(dt_raw, dtb, alog):
    row = lax.broadcasted_iota(jnp.int32, (CHUNK, CHUNK), 0)
    col = lax.broadcasted_iota(jnp.int32, (CHUNK, CHUNK), 1)
    tri = (row >= col).astype(f32)
    strict = (row > col).astype(f32)
    dtp = _softplus(dt_raw + dtb)
    a_row = -jnp.exp(alog)
    d_a = dtp * a_row
    cs = _hdot(tri, d_a)
    cs_last = cs[CHUNK - 1:CHUNK, :]
    return row, col, dtp, a_row, cs, cs.T, cs_last


def _seg_decay(cs, cs_t, hd, row, col):
    seg = cs[:, hd:hd + 1] - cs_t[hd:hd + 1, :]
    return jnp.where(row >= col, jnp.exp(seg), 0.0)


GROUP_W = HEADS_PER_GROUP * SSM_HEAD_DIM


def _group_indicator(g):
    j = lax.broadcasted_iota(jnp.int32, (GROUP_W, LANE), 0)
    lane = lax.broadcasted_iota(jnp.int32, (GROUP_W, LANE), 1)
    return (lane == g * HEADS_PER_GROUP + j // SSM_HEAD_DIM).astype(bf16)


def _bf16_pieces(a, n):
    pieces = []
    for _ in range(n):
        p = a.astype(bf16)
        pieces.append(p)
        a = a - p.astype(f32)
    return pieces


def _head_spread(a, ind):
    return sum(lax.dot_general(p, ind, (((1,), (1,)), ((), ())), preferred_element_type=f32) for p in _bf16_pieces(a, 3))


def _head_sums(a, ind):
    return sum(jnp.dot(p, ind, preferred_element_type=f32) for p in _bf16_pieces(a, 2))


def ssd_fwd_g(act, proj, dt_bias, a_log, d_skip, norm_g):
    zcol, dtcol = OFF_Z // D_SSM, OFF_DT // LANE

    def body(act_ref, z_ref, dt_ref, dtb_ref, alog_ref, dsk_ref, ng_ref, out_ref, ypre_ref, st_ref, state):
        c = pl.program_id(0)

        @pl.when(c == 0)
        def _():
            state[...] = jnp.zeros_like(state)

        row, col, dtp, a_row, cs, cs_t, cs_last = _ssd_chunk_common(dt_ref[...], dtb_ref[...], alog_ref[...])
        e_cs = jnp.exp(cs)
        dte = jnp.exp(cs_last - cs)
        rows8 = jnp.concatenate([jnp.exp(cs_last), dsk_ref[...], jnp.zeros((6, LANE), f32)], axis=0)
        z = z_ref[...]
        sz = z * _sigmoid(z)
        ng = ng_ref[...]
        for g in range(SSM_GROUPS):
            gs = slice(g * GROUP_W, (g + 1) * GROUP_W)
            ind = _group_indicator(g)
            xg = act_ref[:, gs]
            bg = act_ref[:, D_SSM + g * SSM_STATE:D_SSM + (g + 1) * SSM_STATE]
            cg = act_ref[:, D_SSM + D_BC + g * SSM_STATE:D_SSM + D_BC + (g + 1) * SSM_STATE]
            dt_e, e_e, dte_e = _head_spread(dtp, ind), _head_spread(e_cs, ind), _head_spread(dte, ind)
            rows_e = _head_spread(rows8, ind)
            ecl_e, dsk_e = rows_e[0:1], rows_e[1:2]
            xdt = xg * dt_e
            prev = state[g]
            st_ref[0, g] = prev
            cb = _bdot_nt(cg, bg)
            goff = _bdot(cg, prev)
            snew = _bdot_tn(bg, xdt * dte_e)
            heads = range(g * HEADS_PER_GROUP, (g + 1) * HEADS_PER_GROUP)
            ms = [cb * _seg_decay(cs, cs_t, hd, row, col) for hd in heads]
            yd = [_bdot(m, xdt[:, r * SSM_HEAD_DIM:(r + 1) * SSM_HEAD_DIM]) for r, m in enumerate(ms)]
            y = jnp.concatenate(yd, axis=1) + e_e * goff + xg * dsk_e
            state[g] = prev * ecl_e + snew
            ypre_ref[:, gs] = y
            part = y * sz[:, gs]
            out_ref[:, gs] = part * _rms(part) * ng[:, gs]

    return pl.pallas_call(
        body, name="ssd_fwd", grid=(N_CHUNKS,),
        in_specs=[pl.BlockSpec((CHUNK, D_CONV), lambda c: (c, 0)), pl.BlockSpec((CHUNK, D_SSM), lambda c: (c, zcol)),
                  pl.BlockSpec((CHUNK, LANE), lambda c: (c, dtcol)), _full((1, LANE)), _full((1, LANE)), _full((1, LANE)),
                  _full((1, D_SSM))],
        out_specs=[pl.BlockSpec((CHUNK, D_SSM), lambda c: (c, 0)), pl.BlockSpec((CHUNK, D_SSM), lambda c: (c, 0)),
                   pl.BlockSpec((1, SSM_GROUPS, SSM_STATE, GROUP_W), lambda c: (c, 0, 0, 0))],
        out_shape=[jax.ShapeDtypeStruct((SEQ, D_SSM), f32), jax.ShapeDtypeStruct((SEQ, D_SSM), f32),
                   jax.ShapeDtypeStruct((N_CHUNKS, SSM_GROUPS, SSM_STATE, GROUP_W), f32)],
        scratch_shapes=[pltpu.VMEM((SSM_GROUPS, SSM_STATE, GROUP_W), f32)],
        compiler_params=_params("arbitrary"),
    )(act, proj, proj, dt_bias, a_log, d_skip, norm_g)


def ssd_bwd_g(act, proj, ypre, states, d_out, dt_bias, a_log, d_skip, norm_g):
    zcol, dtcol = OFF_Z // D_SSM, OFF_DT // LANE

    def body(act_ref, z_ref, dt_ref, ypre_ref, st_ref, do_ref, dtb_ref, alog_ref, dsk_ref, ng_ref,
             dact_ref, ddt_ref, dz_ref, dng_ref, dpar_ref, dstate):
        i = pl.program_id(0)

        @pl.when(i == 0)
        def _():
            for ref in (dng_ref, dpar_ref, dstate):
                ref[...] = jnp.zeros_like(ref)

        row, col, dtp, a_row, cs, cs_t, cs_last = _ssd_chunk_common(dt_ref[...], dtb_ref[...], alog_ref[...])
        upper = (row <= col).astype(f32)
        lane = lax.broadcasted_iota(jnp.int32, (CHUNK, LANE), 1)
        rowl = lax.broadcasted_iota(jnp.int32, (CHUNK, LANE), 0)
        e_cs = jnp.exp(cs)
        dte = jnp.exp(cs_last - cs)
        ecl = jnp.exp(cs_last)
        rows8 = jnp.concatenate([ecl, dsk_ref[...], jnp.zeros((6, LANE), f32)], axis=0)
        z = z_ref[...]
        sgz = _sigmoid(z)
        sz = z * sgz
        ng = ng_ref[...]
        ddt_mat = jnp.zeros((CHUNK, LANE), f32)
        dcs_mat = jnp.zeros((CHUNK, LANE), f32)
        dcs_t = jnp.zeros((LANE, CHUNK), f32)
        dcsl_row = jnp.zeros((1, LANE), f32)
        dd_row = jnp.zeros((1, LANE), f32)
        for g in range(SSM_GROUPS):
            gs = slice(g * GROUP_W, (g + 1) * GROUP_W)
            bsl = slice(D_SSM + g * SSM_STATE, D_SSM + (g + 1) * SSM_STATE)
            csl = slice(D_SSM + D_BC + g * SSM_STATE, D_SSM + D_BC + (g + 1) * SSM_STATE)
            ind = _group_indicator(g)
            y = ypre_ref[:, gs]
            part = y * sz[:, gs]
            r = _rms(part)
            yhat = part * r
            d_o = do_ref[:, gs]
            dng_ref[:, gs] += jnp.sum(d_o * yhat, axis=0, keepdims=True)
            dyz = _rms_bwd(d_o, yhat, r, ng[:, gs])
            dy = dyz * sz[:, gs]
            dz_ref[:, gs] = dyz * y * (sgz[:, gs] * (1.0 + z[:, gs] * (1.0 - sgz[:, gs])))

            xg = act_ref[:, gs]
            bg = act_ref[:, bsl]
            cg = act_ref[:, csl]
            dt_e, e_e, dte_e = _head_spread(dtp, ind), _head_spread(e_cs, ind), _head_spread(dte, ind)
            rows_e = _head_spread(rows8, ind)
            ecl_e, dsk_e = rows_e[0:1], rows_e[1:2]
            xdt = xg * dt_e
            prev = st_ref[0, g]
            dh = dstate[g]
            heads = range(g * HEADS_PER_GROUP, (g + 1) * HEADS_PER_GROUP)
            hsl = [slice(r_ * SSM_HEAD_DIM, (r_ + 1) * SSM_HEAD_DIM) for r_ in range(HEADS_PER_GROUP)]
            cb = _bdot_nt(cg, bg)
            lms = [_seg_decay(cs, cs_t, hd, row, col) for hd in heads]
            ms = [cb * lm for lm in lms]
            gmat = _bdot(cg, prev)
            dgm = dy * e_e
            dcg = _bdot_nt(dgm, prev)
            dprev = _bdot_tn(cg, dgm)
            dbg = _bdot_nt(xdt * dte_e, dh)
            dw = _bdot(bg, dh)
            dms = [_bdot_nt(dy[:, s_], xdt[:, s_]) for s_ in hsl]
            dxdts = [_bdot_tn(m, dy[:, s_]) for m, s_ in zip(ms, hsl)]
            dxdt = jnp.concatenate(dxdts, axis=1) + dw * dte_e
            dact_ref[:, gs] = dy * dsk_e + dxdt * dt_e
            dstate[g] = dprev + dh * ecl_e
            dcb = jnp.zeros((CHUNK, CHUNK), f32)
            for hd, dm, lm, m in zip(heads, dms, lms, ms):
                dcb = dcb + dm * lm
                dseg = dm * m
                dcs_mat = dcs_mat + jnp.where(lane == hd, jnp.sum(dseg, axis=1, keepdims=True), 0.0)
                dcs_t = jnp.where(row == hd, jnp.sum(dseg, axis=0, keepdims=True), dcs_t)
            dact_ref[:, bsl] = dbg + _bdot_tn(dcb, cg)
            dact_ref[:, csl] = dcg + _bdot(dcb, bg)
            ddte = _head_sums(dw * xdt, ind) * dte
            dcs_mat = dcs_mat + _head_sums(dy * gmat, ind) * e_cs - ddte
            ddt_mat = ddt_mat + _head_sums(dxdt * xg, ind)
            dcsl_row = (dcsl_row + jnp.sum(ddte, axis=0, keepdims=True)
                        + jnp.sum(_head_sums(dh * prev, ind), axis=0, keepdims=True) * ecl)
            dd_row = dd_row + jnp.sum(_head_sums(dy * xg, ind), axis=0, keepdims=True)
        dcs_mat = dcs_mat - dcs_t.T + jnp.where(rowl == CHUNK - 1, dcsl_row, 0.0)
        dda = _hdot(upper, dcs_mat)
        ddt_mat = ddt_mat + dda * a_row
        da_row = jnp.sum(dda * dtp, axis=0, keepdims=True)
        ddt_raw = ddt_mat * _sigmoid(dt_ref[...] + dtb_ref[...])
        ddt_ref[...] = ddt_raw
        dpar_ref[0:1, :] += jnp.sum(ddt_raw, axis=0, keepdims=True)
        dpar_ref[1:2, :] += da_row * a_row
        dpar_ref[2:3, :] += dd_row

    blk = lambda i: N_CHUNKS - 1 - i
    return pl.pallas_call(
        body, name="ssd_bwd", grid=(N_CHUNKS,),
        in_specs=[pl.BlockSpec((CHUNK, D_CONV), lambda i: (blk(i), 0)), pl.BlockSpec((CHUNK, D_SSM), lambda i: (blk(i), zcol)),
                  pl.BlockSpec((CHUNK, LANE), lambda i: (blk(i), dtcol)), pl.BlockSpec((CHUNK, D_SSM), lambda i: (blk(i), 0)),
                  pl.BlockSpec((1, SSM_GROUPS, SSM_STATE, GROUP_W), lambda i: (blk(i), 0, 0, 0)),
                  pl.BlockSpec((CHUNK, D_SSM), lambda i: (blk(i), 0)),
                  _full((1, LANE)), _full((1, LANE)), _full((1, LANE)), _full((1, D_SSM))],
        out_specs=[pl.BlockSpec((CHUNK, D_CONV), lambda i: (blk(i), 0)), pl.BlockSpec((CHUNK, LANE), lambda i: (blk(i), 0)),
                   pl.BlockSpec((CHUNK, D_SSM), lambda i: (blk(i), 0)), _full((1, D_SSM)), _full((8, LANE))],
        out_shape=[jax.ShapeDtypeStruct((SEQ, D_CONV), f32), jax.ShapeDtypeStruct((SEQ, LANE), f32),
                   jax.ShapeDtypeStruct((SEQ, D_SSM), f32), jax.ShapeDtypeStruct((1, D_SSM), f32),
                   jax.ShapeDtypeStruct((8, LANE), f32)],
        scratch_shapes=[pltpu.VMEM((SSM_GROUPS, SSM_STATE, GROUP_W), f32)],
        compiler_params=_params("arbitrary"),
    )(act, proj, proj, ypre, states, d_out, dt_bias, a_log, d_skip, norm_g)


def out_fwd(x, attn, ssm, w_out, tm=512):
    def body(x_ref, a_ref, s_ref, w_ref, o_ref):
        o_ref[...] = x_ref[...] + _bdot(a_ref[...], w_ref[:D_ATTN, :]) + _bdot(s_ref[...], w_ref[D_ATTN:, :])

    tok = lambda w_: pl.BlockSpec((tm, w_), lambda i: (i, 0))
    return pl.pallas_call(
        body, name="out_fwd", grid=(SEQ // tm,),
        in_specs=[tok(D_MODEL), tok(D_ATTN), tok(D_SSM), _full((D_MODEL, D_MODEL))],
        out_specs=tok(D_MODEL), out_shape=jax.ShapeDtypeStruct((SEQ, D_MODEL), f32),
        compiler_params=_params("arbitrary"),
    )(x, attn, ssm, w_out)


def out_bwd(dx1, attn, ssm, w_out, tm=512):
    nt = SEQ // tm

    def body(d_ref, a_ref, s_ref, w_ref, da_ref, ds_ref, dw16_ref, dw_ref):
        i = pl.program_id(0)

        @pl.when(i == 0)
        def _():
            dw_ref[...] = jnp.zeros_like(dw_ref)

        d = d_ref[...].astype(bf16)
        dcat = _bdot_nt(d, w_ref[...])
        da_ref[...] = dcat[:, :D_ATTN]
        ds_ref[...] = dcat[:, D_ATTN:]
        dw_ref[:D_ATTN, :] += _bdot_tn(a_ref[...], d)
        dw_ref[D_ATTN:, :] += _bdot_tn(s_ref[...], d)

        @pl.when(i == nt - 1)
        def _():
            dw16_ref[...] = dw_ref[...].astype(bf16)

    tok = lambda w_: pl.BlockSpec((tm, w_), lambda i: (i, 0))
    return pl.pallas_call(
        body, name="out_bwd", grid=(nt,),
        in_specs=[tok(D_MODEL), tok(D_ATTN), tok(D_SSM), _resident((D_MODEL, D_MODEL))],
        out_specs=[tok(D_ATTN), tok(D_SSM), _resident((D_MODEL, D_MODEL))],
        out_shape=[jax.ShapeDtypeStruct((SEQ, D_ATTN), f32), jax.ShapeDtypeStruct((SEQ, D_SSM), f32),
                   jax.ShapeDtypeStruct((D_MODEL, D_MODEL), bf16)],
        scratch_shapes=[pltpu.VMEM((D_MODEL, D_MODEL), f32)],
        compiler_params=_params("arbitrary"),
    )(dx1, attn, ssm, w_out)


MLP_SUB = 256


def mlp_fwd(x1, g, w_up, w_down, tm=1024):
    def body(x_ref, g_ref, wu_ref, wd_ref, o_ref, u_ref, h_scr):
        j = pl.program_id(1)

        @pl.when(j == 0)
        def _():
            xv = x_ref[...]
            h_scr[...] = (xv * _rms(xv) * g_ref[...]).astype(bf16)
            o_ref[...] = xv

        for r in range(tm // MLP_SUB):
            rows = slice(r * MLP_SUB, (r + 1) * MLP_SUB)
            u = jnp.dot(h_scr[rows, :], wu_ref[...], preferred_element_type=f32)
            u_ref[rows, :] = u
            a = jnp.square(jnp.maximum(u, 0.0))
            o_ref[rows, :] += _bdot(a, wd_ref[...])

    return pl.pallas_call(
        body, name="mlp_fwd", grid=(SEQ // tm, N_CHIPS),
        in_specs=[pl.BlockSpec((tm, D_MODEL), lambda i, j: (i, 0)), _full((1, D_MODEL)),
                  pl.BlockSpec((None, D_MODEL, FF_TILE), lambda i, j: (j, 0, 0)),
                  pl.BlockSpec((None, FF_TILE, D_MODEL), lambda i, j: (j, 0, 0))],
        out_specs=[pl.BlockSpec((tm, D_MODEL), lambda i, j: (i, 0)), pl.BlockSpec((tm, FF_TILE), lambda i, j: (i, j))],
        out_shape=[jax.ShapeDtypeStruct((SEQ, D_MODEL), f32), jax.ShapeDtypeStruct((SEQ, D_FF), f32)],
        scratch_shapes=[pltpu.VMEM((tm, D_MODEL), bf16)],
        compiler_params=_params("arbitrary", "arbitrary"),
    )(x1, g, w_up, w_down)


def mlp_bwd_data(dx2, u, x1, g, w_up, w_down, tm=1024):
    def body(d_ref, u_ref, x_ref, g_ref, wu_ref, wd_ref, dx_ref, du_ref, dg_ref, dh_scr):
        i, j = pl.program_id(0), pl.program_id(1)

        @pl.when(jnp.logical_and(i == 0, j == 0))
        def _():
            dg_ref[...] = jnp.zeros_like(dg_ref)

        @pl.when(j == 0)
        def _():
            dh_scr[...] = jnp.zeros_like(dh_scr)

        for r in range(tm // MLP_SUB):
            rows = slice(r * MLP_SUB, (r + 1) * MLP_SUB)
            da = _bdot_nt(d_ref[rows, :], wd_ref[...])
            du = (da * (2.0 * jnp.maximum(u_ref[rows, :], 0.0))).astype(bf16)
            du_ref[rows, :] = du
            dh_scr[rows, :] += _bdot_nt(du, wu_ref[...])

        @pl.when(j == N_CHIPS - 1)
        def _():
            xv = x_ref[...]
            r = _rms(xv)
            xhat = xv * r
            dh = dh_scr[...]
            dg_ref[...] += jnp.sum(dh * xhat, axis=0, keepdims=True)
            dx_ref[...] = d_ref[...] + _rms_bwd(dh, xhat, r, g_ref[...])

    return pl.pallas_call(
        body, name="mlp_bwd_data", grid=(SEQ // tm, N_CHIPS),
        in_specs=[pl.BlockSpec((tm, D_MODEL), lambda i, j: (i, 0)), pl.BlockSpec((tm, FF_TILE), lambda i, j: (i, j)),
                  pl.BlockSpec((tm, D_MODEL), lambda i, j: (i, 0)), _full((1, D_MODEL)),
                  pl.BlockSpec((None, D_MODEL, FF_TILE), lambda i, j: (j, 0, 0)),
                  pl.BlockSpec((None, FF_TILE, D_MODEL), lambda i, j: (j, 0, 0))],
        out_specs=[pl.BlockSpec((tm, D_MODEL), lambda i, j: (i, 0)), pl.BlockSpec((tm, FF_TILE), lambda i, j: (i, j)),
                   _full((1, D_MODEL))],
        out_shape=[jax.ShapeDtypeStruct((SEQ, D_MODEL), f32), jax.ShapeDtypeStruct((SEQ, D_FF), bf16),
                   jax.ShapeDtypeStruct((1, D_MODEL), f32)],
        scratch_shapes=[pltpu.VMEM((tm, D_MODEL), f32)],
        compiler_params=_params("arbitrary", "arbitrary"),
    )(dx2, u, x1, g, w_up, w_down)


def mlp_bwd_weights(dx2, u, du, x1, g, tm=512):
    nt = SEQ // tm

    def body(d_ref, u_ref, du_ref, x_ref, g_ref, dwu16_ref, dwd16_ref, h_scr, d_scr, dwu_ref, dwd_ref):
        j, i = pl.program_id(0), pl.program_id(1)

        @pl.when(j == 0)
        def _():
            xv = x_ref[...]
            h_scr[i] = (xv * _rms(xv) * g_ref[...]).T.astype(bf16)
            d_scr[i] = d_ref[...].astype(bf16)

        @pl.when(i == 0)
        def _():
            dwu_ref[...] = jnp.zeros_like(dwu_ref)
            dwd_ref[...] = jnp.zeros_like(dwd_ref)

        dwu_ref[...] += jnp.dot(h_scr[i], du_ref[...], preferred_element_type=f32)
        a = jnp.square(jnp.maximum(u_ref[...], 0.0))
        dwd_ref[...] += _bdot_tn(a, d_scr[i])

        @pl.when(i == nt - 1)
        def _():
            dwu16_ref[...] = dwu_ref[...].astype(bf16)
            dwd16_ref[...] = dwd_ref[...].astype(bf16)

    up = pl.BlockSpec((None, D_MODEL, FF_TILE), lambda j, i: (j, 0, 0))
    down = pl.BlockSpec((None, FF_TILE, D_MODEL), lambda j, i: (j, 0, 0))
    first_pass = pl.BlockSpec((tm, D_MODEL), lambda j, i: (jnp.where(j == 0, i, nt - 1), 0))
    return pl.pallas_call(
        body, name="mlp_bwd_weights", grid=(N_CHIPS, nt),
        in_specs=[first_pass, pl.BlockSpec((tm, FF_TILE), lambda j, i: (i, j)),
                  pl.BlockSpec((tm, FF_TILE), lambda j, i: (i, j)), first_pass, _full((1, D_MODEL))],
        out_specs=[up, down],
        out_shape=[jax.ShapeDtypeStruct((N_CHIPS, D_MODEL, FF_TILE), bf16), jax.ShapeDtypeStruct((N_CHIPS, FF_TILE, D_MODEL), bf16)],
        scratch_shapes=[pltpu.VMEM((nt, D_MODEL, tm), bf16), pltpu.VMEM((nt, tm, D_MODEL), bf16),
                        pltpu.VMEM((D_MODEL, FF_TILE), f32), pltpu.VMEM((FF_TILE, D_MODEL), f32)],
        compiler_params=_params("arbitrary", "arbitrary"),
    )(dx2, u, du, x1, g)


def loss_head(y, target, tm=512):
    def body(y_ref, t_ref, dy_ref, l_ref):
        @pl.when(pl.program_id(0) == 0)
        def _():
            l_ref[...] = jnp.zeros_like(l_ref)

        d = y_ref[...] - t_ref[...]
        dy_ref[...] = d * (1.0 / D_MODEL)
        part = jnp.sum(jnp.mean(d * d, axis=-1, keepdims=True), axis=0, keepdims=True)
        l_ref[...] += 0.5 * part

    tok = pl.BlockSpec((tm, D_MODEL), lambda i: (i, 0))
    return pl.pallas_call(
        body, name="loss_head", grid=(SEQ // tm,), in_specs=[tok, tok], out_specs=[tok, _full((1, 1))],
        out_shape=[jax.ShapeDtypeStruct((SEQ, D_MODEL), f32), jax.ShapeDtypeStruct((1, 1), f32)],
        compiler_params=_params("arbitrary"),
    )(y, target)


def _pad_lane(v):
    return jnp.pad(v, (0, LANE - v.shape[0]))[None, :]


def local_step(x, target, w, prov):
    bucket = jnp.asarray(_bucket_table().T)
    bias = bias_build(w["rel_bias"], bucket)
    saved = []
    for l in range(DEPTH):
        g_mix = w["mix_norm_g"][l][None, :] + prov.stage(("begin", l), x)
        w_in = prov.w_in(l, x)
        proj = in_fwd(x, g_mix, w_in)
        conv_b = w["conv_b"][l][None, :]
        act = conv_fwd(proj, w["conv_w"][l], conv_b)
        dtb = _pad_lane(w["dt_bias"][l]) + prov.stage(("mid", l), act)
        alog, dsk = _pad_lane(w["a_log"][l]), _pad_lane(w["d_skip"][l])
        ng = w["ssm_norm_g"][l][None, :]
        ssm, ypre, states = ssd_fwd_g(act, proj, dtb, alog, dsk, ng)
        qg, kg = w["q_gain"][l][:, None] + 0.0 * ssm[:1, :1], w["k_gain"][l][None, :]
        attn = attn_fwd_t(proj, qg, kg, w["sinks"][l], bias)
        tok = prov.stage(("pre_out", l), attn)
        w_out = prov.w_out(l, attn) + jnp.asarray(tok, bf16)
        x1 = out_fwd(x, attn, ssm, w_out)
        g_mlp = w["mlp_norm_g"][l][None, :] + prov.stage(("pre_mlp", l), x1)
        w_up, w_down = prov.mlp(l, x1)
        x2, u = mlp_fwd(x1, g_mlp, w_up, w_down)
        saved.append(dict(x=x, proj=proj, attn=attn, act=act, ssm=ssm, ypre=ypre, states=states, x1=x1, u=u,
                          g_mix=g_mix, qg=qg, kg=kg, conv_b=conv_b, dtb=dtb, alog=alog, dsk=dsk, ng=ng, g_mlp=g_mlp,
                          w_in=w_in, w_out=w_out, w_up=w_up, w_down=w_down))
        x = x2
    dx, loss = loss_head(x, target)
    grads = [None] * DEPTH
    dbands = [None] * DEPTH
    tok = 0.0
    for l in reversed(range(DEPTH)):
        s = saved[l]
        g_mlp = s["g_mlp"] + tok
        dx1, du, dg_mlp = mlp_bwd_data(dx, s["u"], s["x1"], g_mlp, s["w_up"], s["w_down"])
        dw_up, dw_down = mlp_bwd_weights(dx, s["u"], du, s["x1"], g_mlp)
        tok = prov.grads(("mlp", l), dict(w_up=dw_up, w_down=dw_down), dx1)
        dattn, dssm, dw_out = out_bwd(dx1, s["attn"], s["ssm"], s["w_out"])
        dact, ddt, dz, dng, dpar = ssd_bwd_g(s["act"], s["proj"], s["ypre"], s["states"], dssm, s["dtb"] + tok, s["alog"],
                                           s["dsk"], s["ng"])
        conv_b = s["conv_b"] + prov.stage(("bwd_mid", l), dact)
        dxbc, dconv_w, dconv_b = conv_bwd(s["proj"], dact, w["conv_w"][l], conv_b)
        dq, dk, dv, dband, dsink, dqg, dkg = attn_bwd_t(s["proj"], dattn, s["qg"], s["kg"], w["sinks"][l], bias)
        dbands[l] = dband
        g_mix = s["g_mix"]
        if l == 0:
            d_rel = bias_bwd(dbands[0], dbands[1], bucket)
            g_mix = g_mix + 0.0 * d_rel[:1, :1]
        dx, dw_in, dg_mix = in_bwd(dq, dz, dxbc, dk, dv, ddt, s["x"], g_mix, s["w_in"], dx1)
        tok = prov.grads(("mix", l), dict(w_in=split_w_in_grad(dw_in), w_out=dw_out), dx)
        grads[l] = dict(mix_norm_g=dg_mix[0], q_gain=dqg[:, 0], k_gain=dkg[0], sinks=dsink[:, 0],
                        conv_w=dconv_w, conv_b=dconv_b[0], dt_bias=dpar[0, :SSM_HEADS], a_log=dpar[1, :SSM_HEADS],
                        d_skip=dpar[2, :SSM_HEADS], ssm_norm_g=dng[0], mlp_norm_g=dg_mlp[0])
    out = {k: jnp.stack([grads[l][k] for l in range(DEPTH)]) for k in grads[0]}
    out["rel_bias"] = d_rel[:, :N_Q_HEADS]
    return loss, dx, out, tok


MESH = pl.DeviceIdType.MESH
HBM = pl.BlockSpec(memory_space=pltpu.HBM)
N_DEVICES = 8


def _coords():
    return lax.axis_index("x"), lax.axis_index("y"), lax.axis_index("c")


def _peer_chips(x, y):
    return [(1 - x, y), (x, 1 - y), (1 - x, 1 - y)]


def _remote(src, dst, send_sem, recv_sem, device):
    return pltpu.make_async_remote_copy(src_ref=src, dst_ref=dst, send_sem=send_sem, recv_sem=recv_sem,
                                        device_id=device, device_id_type=MESH)


SEM = pl.BlockSpec(memory_space=pltpu.SEMAPHORE)
ANY = pl.BlockSpec(memory_space=pl.ANY)
DATAFLOW = pltpu.SideEffectType.DATAFLOW_SIDE_EFFECTING


def _gather_copies(kind, src_refs, land_refs, ssem, rsem):
    x, y, c = _coords()
    k_me = 2 * x + y
    n = len(land_refs)
    cps = []
    for p, land in enumerate(land_refs):
        hr = land.shape[1] // 2
        rows = pl.ds(c * hr, hr)
        for j, chip in enumerate(_peer_chips(x, y)):
            i = 3 * p + j
            if kind == "ici":
                cps.append(_remote(src_refs[p].at[rows, :], land.at[k_me, rows, :], ssem.at[i], rsem.at[i], (*chip, c)))
            else:
                got = land.at[2 * chip[0] + chip[1], rows, :]
                cps.append(_remote(got, got, ssem.at[i], rsem.at[i], (x, y, 1 - c)))
        if kind == "relay":
            cps.append(_remote(src_refs[p], land.at[k_me], ssem.at[3 * n + p], rsem.at[3 * n + p], (x, y, 1 - c)))
    return cps


def gather_now(srcs, conv):
    n = len(srcs)

    def body(*refs):
        src_refs, conv_ref = refs[:n], refs[n]
        lands, gconv = refs[n + 1:2 * n + 1], refs[2 * n + 1]
        ssem, rsem, fsem, frsem, csem, crsem = refs[2 * n + 2:]
        x, y, c = _coords()
        k_me = 2 * x + y
        targets = [(*chip, c) for chip in _peer_chips(x, y)] + [(x, y, 1 - c)]
        ici = _gather_copies("ici", src_refs, lands, ssem, rsem)
        relay = _gather_copies("relay", src_refs, lands, fsem, frsem)
        passed = [cp for i, cp in enumerate(relay) if i % 4 != 3]
        own = relay[3::4]
        conv_cps = [_remote(conv_ref, gconv.at[k_me], csem.at[j], crsem.at[j], t) for j, t in enumerate(targets)]
        for cp in ici + conv_cps + own:
            cp.start()
        for cp, fw in zip(ici, passed):
            cp.wait_recv()
            fw.start()
        for cp in conv_cps + relay:
            cp.wait_recv()
        for cp in ici + relay + conv_cps:
            cp.wait_send()

    out_shape = [jax.ShapeDtypeStruct((N_CHIPS,) + s.shape, s.dtype) for s in srcs]
    out_shape.append(jax.ShapeDtypeStruct((N_CHIPS,) + conv.shape, conv.dtype))
    sems = lambda k: pltpu.SemaphoreType.DMA((k,))
    return pl.pallas_call(
        body, name="gather_now", out_shape=out_shape, in_specs=[HBM] * (n + 1), out_specs=[HBM] * (n + 1),
        scratch_shapes=[sems(3 * n), sems(3 * n), sems(4 * n), sems(4 * n), sems(N_CHIPS), sems(N_CHIPS)],
    )(*srcs, conv)


def _gather_maker(kind, n_src):
    def make(refs, ssem, rsem):
        cps = _gather_copies(kind, refs[:n_src], refs[n_src:], ssem, rsem)
        return cps, cps
    return make


def _scatter_maker(n):
    def make(refs, ssem, rsem):
        x, y, c = _coords()
        k_me = 2 * x + y
        sends, arrivals = [], []
        for p in range(n):
            src, land = refs[p], refs[n + p]
            sends.append(_remote(src.at[k_me, 1 - c], land.at[0], ssem.at[7 * p], rsem.at[7 * p], (x, y, 1 - c)))
            for j, chip in enumerate(_peer_chips(x, y)):
                for cc in range(2):
                    sends.append(_remote(src.at[2 * chip[0] + chip[1], cc], land.at[1 + 2 * j + c],
                                         ssem.at[7 * p + 1 + 2 * j + cc], rsem.at[7 * p + 1 + 2 * j + c], (*chip, cc)))
            for s in range(7):
                arrivals.append(_remote(land.at[s], land.at[s], ssem.at[7 * p + s], rsem.at[7 * p + s], (x, y, 1 - c)))
        return sends, arrivals
    return make


def _share_maker(n):
    def make(refs, ssem, rsem):
        x, y, c = _coords()
        sends = [_remote(refs[p].at[c], refs[p].at[c], ssem.at[p], rsem.at[p], (x, y, 1 - c)) for p in range(n)]
        arrivals = [_remote(refs[p].at[1 - c], refs[p].at[1 - c], ssem.at[p], rsem.at[p], (x, y, 1 - c)) for p in range(n)]
        return sends, arrivals
    return make


def split_start(name, make, n_sems, operands, after):
    n = len(operands)

    def body(*refs):
        ssem, rsem, token = refs[n + 1], refs[n + 2], refs[-1]
        for cp in make(refs[:n], ssem, rsem)[0]:
            cp.start()
        token[...] = jnp.zeros_like(token)

    ops = [pltpu.with_memory_space_constraint(a, pltpu.HBM) for a in operands]
    outs = pl.pallas_call(
        body, name=name,
        out_shape=(pltpu.SemaphoreType.DMA((n_sems,)), pltpu.SemaphoreType.DMA((n_sems,)),
                   *[pltpu.HBM(a.shape, a.dtype) for a in ops], jax.ShapeDtypeStruct((8, LANE), f32)),
        in_specs=[HBM] * n + [ANY], out_specs=(SEM, SEM, *[HBM] * n, pl.BlockSpec(memory_space=pltpu.VMEM)),
        input_output_aliases={i: 2 + i for i in range(n)},
        compiler_params=pltpu.CompilerParams(has_side_effects=DATAFLOW),
    )(*ops, after)
    return dict(name=name, make=make, ssem=outs[0], rsem=outs[1], operands=outs[2:2 + n], token=outs[-1][0, 0])


def split_wait(handle, after):
    n = len(handle["operands"])

    def body(*refs):
        sends, arrivals = handle["make"](refs[:n], refs[n], refs[n + 1])
        for cp in sends:
            cp.wait_send()
        for cp in arrivals:
            cp.wait_recv()

    outs = pl.pallas_call(
        body, name=handle["name"].replace("start", "wait"),
        out_shape=tuple(pltpu.HBM(a.shape, a.dtype) for a in handle["operands"]),
        in_specs=[HBM] * n + [SEM, SEM, ANY], out_specs=tuple([HBM] * n),
        input_output_aliases={i: i for i in range(n)},
        compiler_params=pltpu.CompilerParams(has_side_effects=DATAFLOW),
    )(*handle["operands"], handle["ssem"], handle["rsem"], after)
    return list(outs)


def piece_sum(g, recv, kc_arr):
    _, _, rb, cc = g.shape
    tr = min(256, rb)

    def body(kc_ref, g_ref, r_ref, o_ref):
        acc = g_ref[...].astype(f32)
        for s in range(7):
            acc = acc + r_ref[s].astype(f32)
        o_ref[...] = acc

    return pl.pallas_call(
        body, name="piece_sum",
        grid_spec=pltpu.PrefetchScalarGridSpec(
            num_scalar_prefetch=1, grid=(rb // tr,),
            in_specs=[pl.BlockSpec((None, None, tr, cc), lambda r, kc: (kc[0], kc[1], r, 0)),
                      pl.BlockSpec((7, tr, cc), lambda r, kc: (0, r, 0))],
            out_specs=pl.BlockSpec((None, tr, cc), lambda r, kc: (kc[1], r, 0))),
        out_shape=jax.ShapeDtypeStruct((2, rb, cc), f32),
        compiler_params=_params("arbitrary"),
    )(kc_arr, g, recv)


def small_all_reduce(vec):
    def body(v_ref, o_ref, gat, ssem, rsem):
        x, y, c = _coords()
        me = 4 * x + 2 * y + c
        gat[me] = v_ref[...]
        sends = []
        for t in range(1, N_DEVICES):
            peer = (x ^ (t >> 2), y ^ ((t >> 1) & 1), c ^ (t & 1))
            cp = _remote(v_ref, gat.at[me], ssem.at[t - 1], rsem.at[t - 1], peer)
            cp.start()
            sends.append(cp)
        for t in range(1, N_DEVICES):
            peer = (x ^ (t >> 2), y ^ ((t >> 1) & 1), c ^ (t & 1))
            slot = gat.at[4 * peer[0] + 2 * peer[1] + peer[2]]
            _remote(slot, slot, ssem.at[t - 1], rsem.at[t - 1], peer).wait_recv()
        for cp in sends:
            cp.wait_send()
        acc = gat[0]
        for d in range(1, N_DEVICES):
            acc = acc + gat[d]
        o_ref[...] = acc

    return pl.pallas_call(
        body, name="small_all_reduce", out_shape=jax.ShapeDtypeStruct(vec.shape, vec.dtype),
        in_specs=[pl.BlockSpec(memory_space=pltpu.VMEM)], out_specs=pl.BlockSpec(memory_space=pltpu.VMEM),
        scratch_shapes=[pltpu.VMEM((N_DEVICES,) + vec.shape, vec.dtype), pltpu.SemaphoreType.DMA((N_DEVICES - 1,)),
                        pltpu.SemaphoreType.DMA((N_DEVICES - 1,))],
    )(vec)


def _adamw_math(w, g, m, v):
    m_new = ADAM_B1 * m + (1.0 - ADAM_B1) * g
    v_new = ADAM_B2 * v + (1.0 - ADAM_B2) * jnp.square(g)
    m_hat = m_new / (1.0 - ADAM_B1 ** ADAM_STEP)
    v_hat = v_new / (1.0 - ADAM_B2 ** ADAM_STEP)
    delta = -ADAM_LR * (m_hat / (jnp.sqrt(v_hat) + ADAM_EPS) + ADAM_WD * w)
    return delta, m_new, v_new


def adamw_shard(w, g0, g1, m, v):
    depth, rows, cols = w.shape
    half = rows // 2
    tr = min(256, half)
    nr = half // tr

    def body(w_ref, g0_ref, g1_ref, m_ref, v_ref, go_ref, d_ref, nm_ref, nv_ref):
        gv = jnp.where(pl.program_id(0) == 0, g0_ref[...], g1_ref[...])
        go_ref[...] = gv
        d_ref[...], nm_ref[...], nv_ref[...] = _adamw_math(w_ref[...], gv, m_ref[...], v_ref[...])

    spec = pl.BlockSpec((None, tr, cols), lambda l, h, r: (l, h * nr + r, 0))
    g0spec = pl.BlockSpec((None, tr, cols), lambda l, h, r: (jnp.where(l == 0, h, 1), jnp.where(l == 0, r, nr - 1), 0))
    g1spec = pl.BlockSpec((None, tr, cols), lambda l, h, r: (jnp.where(l == 1, h, 0), jnp.where(l == 1, r, 0), 0))
    return pl.pallas_call(
        body, name="adamw_shard", grid=(depth, 2, nr), in_specs=[spec, g0spec, g1spec, spec, spec], out_specs=[spec] * 4,
        out_shape=[jax.ShapeDtypeStruct(w.shape, f32)] * 4,
        compiler_params=_params("arbitrary", "arbitrary", "arbitrary"),
    )(w, g0, g1, m, v)


def adamw_cols(w, g, m, v, tc=34):
    cols, depth, rows = w.shape

    def body(w_ref, g_ref, m_ref, v_ref, d_ref, nm_ref, nv_ref):
        d_ref[...], nm_ref[...], nv_ref[...] = _adamw_math(w_ref[...], g_ref[...], m_ref[...], v_ref[...])

    spec = pl.BlockSpec((tc, depth, rows), lambda i: (i, 0, 0))
    return pl.pallas_call(
        body, name="adamw_cols", grid=(cols // tc,), in_specs=[spec] * 4, out_specs=[spec] * 3,
        out_shape=[jax.ShapeDtypeStruct(w.shape, f32)] * 3,
        compiler_params=_params("arbitrary"),
    )(w, g, m, v)


def adamw_small(ws, gs, ms, vs):
    n = len(ws)

    def body(*refs):
        ins, outs = refs[:4 * n], refs[4 * n:]
        for i in range(n):
            w_ref, g_ref, m_ref, v_ref = (ins[k * n + i] for k in range(4))
            outs[i][...], outs[n + i][...], outs[2 * n + i][...] = _adamw_math(w_ref[...], g_ref[...], m_ref[...], v_ref[...])

    outs = pl.pallas_call(
        body, name="adamw_small", out_shape=[jax.ShapeDtypeStruct(w.shape, f32) for w in ws] * 3,
    )(*ws, *gs, *ms, *vs)
    return outs[:n], outs[n:2 * n], outs[2 * n:]


WEIGHTS = ("mix_norm_g", "w_in", "q_gain", "k_gain", "sinks", "rel_bias", "conv_w", "conv_b", "dt_bias", "a_log", "d_skip",
           "ssm_norm_g", "w_out", "mlp_norm_g", "w_up", "w_down")
BIG = ("w_in", "w_out", "w_up", "w_down")
SMALL = tuple(n for n in WEIGHTS if n not in BIG)
PACK_COLS = 1024
PACK_ROWS = 16


def _pack(named, last=None):
    flat = jnp.concatenate([named[n].reshape(-1) for n in SMALL])
    tail = jnp.zeros((1,), f32) if last is None else last.reshape(1)
    pad = jnp.zeros((PACK_ROWS * PACK_COLS - flat.shape[0] - 1,), f32)
    return jnp.concatenate([flat, pad, tail]).reshape(PACK_ROWS, PACK_COLS)


def _unpack(buf, shapes):
    flat = buf.reshape(-1)
    out, at = {}, 0
    for n in SMALL:
        size = int(np.prod(shapes[n]))
        out[n] = flat[at:at + size].reshape(shapes[n])
        at += size
    return out


class _Exchange:
    GROUPS = {"A": (("w_up", 0), ("w_down", 0)), "B": (("w_in", 1), ("w_out", 1)), "C": (("w_up", 1), ("w_down", 1))}
    ICI_AT = {("mid", 0): "B", ("pre_out", 0): "C"}
    RELAY_AT = {("pre_out", 0): "A", ("pre_mlp", 0): "B", ("mid", 1): "C"}
    LAST = ("mix", 0)
    IN_FLIGHT = 2

    def __init__(self, wts, kc_arr):
        self.wts, self.kc_arr = wts, kc_arr
        self.own = {(n, l): wts[n][l].astype(bf16) for n in BIG for l in range(DEPTH)}
        now = gather_now([self.own["w_in", 0], self.own["w_out", 0]], wts["conv_w"])
        self.ready = {("w_in", 0): now[0], ("w_out", 0): now[1]}
        self.conv_w = jnp.transpose(now[2], (1, 2, 0, 3)).reshape(DEPTH, CONV_WIDTH, D_CONV)
        self.ici, self.relay = {}, {}
        self.scatter, self.share, self.reduced = [], [], {}
        self._start_ici("A", now[2])

    def _start_ici(self, g, after):
        srcs = [self.own[p] for p in self.GROUPS[g]]
        lands = [lax.empty((N_CHIPS,) + s.shape, s.dtype) for s in srcs]
        self.ici[g] = split_start("gather%s_ici_start" % g, _gather_maker("ici", len(srcs)), 3 * len(srcs), srcs + lands,
                                  after)
        return self.ici[g]["token"]

    def stage(self, name, after):
        if name == ("begin", 0):
            return self.ici["A"]["token"]
        tok = 0.0
        g = self.RELAY_AT.get(name)
        if g is not None:
            n = len(self.GROUPS[g])
            self.relay[g] = split_start("gather%s_relay_start" % g, _gather_maker("relay", n), 4 * n,
                                        split_wait(self.ici[g], after), after)
            tok = self.relay[g]["token"]
        if name in self.ICI_AT:
            tok = tok + self._start_ici(self.ICI_AT[name], after)
        return tok

    def _get(self, piece, after):
        if piece not in self.ready:
            g = [k for k, pieces in self.GROUPS.items() if piece in pieces][0]
            lands = split_wait(self.relay[g], after)[len(self.GROUPS[g]):]
            self.ready.update(zip(self.GROUPS[g], lands))
        return self.ready[piece]

    def w_in(self, l, after):
        return align_w_in(self._get(("w_in", l), after))

    def w_out(self, l, after):
        return self._get(("w_out", l), after).reshape(D_MODEL, D_MODEL)

    def mlp(self, l, after):
        return self._get(("w_up", l), after), self._get(("w_down", l), after)

    def _view(self, n, g):
        _, rows, cols = self.wts[n].shape
        return g.reshape(N_CHIPS, 2, rows // 2, cols)

    def grads(self, name, arrays, after):
        if name == self.LAST:
            self.held = (name, arrays)
            return 0.0
        return self._scatter(name, arrays, after) + self._advance(after, self.IN_FLIGHT)

    def flush(self, after):
        return self._scatter(*self.held, after) + self._advance(after, self.IN_FLIGHT)

    def _scatter(self, name, arrays, after):
        pieces = [(n, name[1]) for n in arrays]
        views = [self._view(n, g) for n, g in arrays.items()]
        lands = [lax.empty((7,) + v.shape[2:], bf16) for v in views]
        h = split_start("scatter_%s%d_start" % name, _scatter_maker(len(views)), 7 * len(views), views + lands, after)
        self.scatter.append((pieces, h))
        return h["token"]

    def _take_share(self, after):
        pieces, h = self.share.pop(0)
        self.reduced.update(zip(pieces, split_wait(h, after)))

    def _take_scatter(self, after):
        pieces, h = self.scatter.pop(0)
        done = split_wait(h, after)
        views, lands = done[:len(pieces)], done[len(pieces):]
        sums = [piece_sum(v, land, self.kc_arr) for v, land in zip(views, lands)]
        hs = split_start(h["name"].replace("scatter", "share"), _share_maker(len(sums)), len(sums), sums, after)
        self.share.append((pieces, hs))
        return hs["token"]

    def _advance(self, after, newest):
        if self.share:
            self._take_share(after)
        return self._take_scatter(after) if len(self.scatter) > newest else 0.0

    def reduced_grads(self, names, after):
        want = [(n, l) for n in names for l in range(DEPTH)]
        while not all(p in self.reduced for p in want):
            if any(p in pieces for p in want for pieces, _ in self.share):
                self._take_share(after)
            else:
                self._take_scatter(after)
        return {n: [self.reduced[n, l] for l in range(DEPTH)] for n in names}


def kernel(x, mix_norm_g, w_in, q_gain, k_gain, sinks, rel_bias, conv_w, conv_b, dt_bias, a_log, d_skip, ssm_norm_g, w_out, mlp_norm_g, w_up, w_down, loss_target, m_mix_norm_g, m_w_in, m_q_gain, m_k_gain, m_sinks, m_rel_bias, m_conv_w, m_conv_b, m_dt_bias, m_a_log, m_d_skip, m_ssm_norm_g, m_w_out, m_mlp_norm_g, m_w_up, m_w_down, v_mix_norm_g, v_w_in, v_q_gain, v_k_gain, v_sinks, v_rel_bias, v_conv_w, v_conv_b, v_dt_bias, v_a_log, v_d_skip, v_ssm_norm_g, v_w_out, v_mlp_norm_g, v_w_up, v_w_down):
    wts = dict(mix_norm_g=mix_norm_g, w_in=w_in, q_gain=q_gain, k_gain=k_gain, sinks=sinks, rel_bias=rel_bias, conv_w=conv_w,
               conv_b=conv_b, dt_bias=dt_bias, a_log=a_log, d_skip=d_skip, ssm_norm_g=ssm_norm_g, w_out=w_out,
               mlp_norm_g=mlp_norm_g, w_up=w_up, w_down=w_down)
    mom = dict(mix_norm_g=m_mix_norm_g, w_in=m_w_in, q_gain=m_q_gain, k_gain=m_k_gain, sinks=m_sinks, rel_bias=m_rel_bias,
               conv_w=m_conv_w, conv_b=m_conv_b, dt_bias=m_dt_bias, a_log=m_a_log, d_skip=m_d_skip, ssm_norm_g=m_ssm_norm_g,
               w_out=m_w_out, mlp_norm_g=m_mlp_norm_g, w_up=m_w_up, w_down=m_w_down)
    var = dict(mix_norm_g=v_mix_norm_g, w_in=v_w_in, q_gain=v_q_gain, k_gain=v_k_gain, sinks=v_sinks, rel_bias=v_rel_bias,
               conv_w=v_conv_w, conv_b=v_conv_b, dt_bias=v_dt_bias, a_log=v_a_log, d_skip=v_d_skip, ssm_norm_g=v_ssm_norm_g,
               w_out=v_w_out, mlp_norm_g=v_mlp_norm_g, w_up=v_w_up, w_down=v_w_down)
    xi, yi, ci = _coords()
    k_me = 2 * xi + yi
    kc_arr = jnp.stack([k_me, ci]).astype(jnp.int32)

    prov = _Exchange(wts, kc_arr)
    small_w = {n: wts[n] for n in SMALL}
    small_w["conv_w"] = prov.conv_w
    loss, dx, grads, tok = local_step(x[0], loss_target[0], small_w, prov)

    small_shapes = {n: grads[n].shape for n in SMALL}
    small_sum = small_all_reduce(_pack(grads, loss) + tok)
    loss = small_sum[PACK_ROWS - 1, PACK_COLS - 1]
    tok = prov.flush(small_sum)
    small = _unpack(small_sum, small_shapes)
    cols = conv_w.shape[-1]
    small["conv_w"] = lax.dynamic_slice_in_dim(small["conv_w"], k_me * cols, cols, axis=2)
    g_out_d = dict(small)
    gs = [small[n] for n in SMALL]
    gs[0] = gs[0] + tok
    ds, nms, nvs = adamw_small([wts[n] for n in SMALL], gs, [mom[n] for n in SMALL], [var[n] for n in SMALL])
    d_out_d, m_out_d, v_out_d = dict(zip(SMALL, ds)), dict(zip(SMALL, nms)), dict(zip(SMALL, nvs))

    after = ds[0]
    for names in (("w_up", "w_down"), ("w_in", "w_out")):
        for n, (g0, g1) in prov.reduced_grads(names, after).items():
            if n == "w_in":
                rows, cols = wts[n].shape[1:]
                to_cols = lambda a: jnp.transpose(a, (2, 0, 1))
                g_t = jnp.stack([to_cols(g).reshape(cols, rows) for g in (g0, g1)], axis=1)
                res_t = adamw_cols(to_cols(wts[n]), g_t, to_cols(mom[n]), to_cols(var[n]))
                g_out_d[n], d_out_d[n], m_out_d[n], v_out_d[n] = (jnp.transpose(a, (1, 2, 0)) for a in (g_t, *res_t))
            else:
                g_out_d[n], d_out_d[n], m_out_d[n], v_out_d[n] = adamw_shard(wts[n], g0, g1, mom[n], var[n])
            after = d_out_d[n]

    return (loss, dx[None], *[g_out_d[n] for n in WEIGHTS], *[d_out_d[n] for n in WEIGHTS],
            *[m_out_d[n] for n in WEIGHTS], *[v_out_d[n] for n in WEIGHTS])
```

```python
import numpy as np
import jax
import jax.numpy as jnp
from jax import lax
from jax.experimental import pallas as pl
from jax.experimental.pallas import tpu as pltpu

f32 = jnp.float32
bf16 = jnp.bfloat16

SEQ = 2048
D_MODEL = 1024
DEPTH = 2
HEAD_DIM = 64
N_Q_HEADS = 8
N_KV_HEADS = 2
Q_PER_KV = N_Q_HEADS // N_KV_HEADS
BLOCK = 128
N_BLOCKS = SEQ // BLOCK
N_BUCKETS = 32
MAX_DISTANCE = 128
SSM_HEADS = 8
SSM_HEAD_DIM = 64
SSM_GROUPS = 2
HEADS_PER_GROUP = SSM_HEADS // SSM_GROUPS
SSM_STATE = 128
CONV_WIDTH = 4
CHUNK = 128
N_CHUNKS = SEQ // CHUNK
D_FF = 4 * D_MODEL
D_ATTN = N_Q_HEADS * HEAD_DIM
D_KV = N_KV_HEADS * HEAD_DIM
D_SSM = SSM_HEADS * SSM_HEAD_DIM
D_BC = SSM_GROUPS * SSM_STATE
D_CONV = D_SSM + 2 * D_BC
D_IN = D_ATTN + 2 * D_KV + D_SSM + D_CONV + SSM_HEADS
EPS = 1e-6
NEG = -1e30
N_CHIPS = 4
FF_TILE = D_FF // N_CHIPS

LANE = 128
PW = D_ATTN + D_SSM + D_CONV + 2 * D_KV + LANE
OFF_Q, OFF_Z, OFF_X, OFF_K, OFF_V, OFF_DT = 0, 512, 1024, 2048, 2176, 2304

ADAM_LR = 0.001
ADAM_B1 = 0.9
ADAM_B2 = 0.999
ADAM_EPS = 1e-08
ADAM_WD = 0.01
ADAM_STEP = 10

VMEM_LIMIT = 56 * 1024 * 1024


def _params(*sem):
    return pltpu.CompilerParams(dimension_semantics=tuple(sem), vmem_limit_bytes=VMEM_LIMIT)


def _bdot(a, b):
    return jnp.dot(a.astype(bf16), b.astype(bf16), preferred_element_type=f32)


def _bdot_nt(a, b):
    return lax.dot_general(a.astype(bf16), b.astype(bf16), (((1,), (1,)), ((), ())), preferred_element_type=f32)


def _bdot_tn(a, b):
    return lax.dot_general(a.astype(bf16), b.astype(bf16), (((0,), (0,)), ((), ())), preferred_element_type=f32)


def _hdot(a, b):
    return jnp.dot(a, b, precision=lax.Precision.HIGHEST, preferred_element_type=f32)


def _sigmoid(x):
    return 1.0 / (1.0 + jnp.exp(-x))


def _softplus(x):
    return jnp.maximum(x, 0.0) + jnp.log1p(jnp.exp(-jnp.abs(x)))


def _rms(x):
    return lax.rsqrt(jnp.mean(x * x, axis=-1, keepdims=True) + EPS)


def _rms_bwd(dy, xhat, r, g):
    t = dy * g
    return r * (t - xhat * jnp.mean(t * xhat, axis=-1, keepdims=True))


def _full(shape):
    return pl.BlockSpec(shape, lambda *_: (0,) * len(shape))


def _bucket_table():
    qi = np.arange(BLOCK)[:, None]
    kj = np.arange(2 * BLOCK)[None, :]
    dist = qi + BLOCK - kj
    ok = (dist >= 0) & (dist < 128)
    d = np.clip(dist, 0, None)
    max_exact = N_BUCKETS // 2
    d_f = np.maximum(d, 1).astype(np.float32)
    large = max_exact + (np.log(d_f / np.float32(max_exact)) / np.float32(np.log(MAX_DISTANCE / max_exact))
                         * np.float32(N_BUCKETS - max_exact)).astype(np.int32)
    large = np.minimum(large, N_BUCKETS - 1)
    bucket = np.where(d < max_exact, d, large)
    return np.where(ok, bucket, -1).astype(np.int32)


def bias_build(rel_bias, bucket):
    def body(rel_ref, bkt_ref, o_ref):
        bkt = bkt_ref[...]
        for h in range(N_Q_HEADS):
            acc = jnp.where(bkt < 0, NEG, 0.0).astype(f32)
            for b in range(N_BUCKETS):
                acc = acc + jnp.where(bkt == b, rel_ref[b, h], 0.0)
            o_ref[h] = acc

    return pl.pallas_call(
        body, name="bias_build", out_shape=jax.ShapeDtypeStruct((N_Q_HEADS,) + bucket.shape, f32),
        in_specs=[pl.BlockSpec(memory_space=pltpu.SMEM), pl.BlockSpec(memory_space=pltpu.VMEM)],
        out_specs=pl.BlockSpec(memory_space=pltpu.VMEM),
    )(rel_bias, bucket)


def bias_bwd(dband0, dband1, bucket):
    def body(d0_ref, d1_ref, bkt_ref, o_ref):
        bkt = bkt_ref[...]
        o_ref[...] = jnp.zeros_like(o_ref)
        for h in range(N_Q_HEADS):
            d = d0_ref[h] + d1_ref[h]
            for b in range(N_BUCKETS):
                part = jnp.sum(jnp.where(bkt == b, d, 0.0), axis=1, keepdims=True)
                o_ref[b:b + 1, h:h + 1] = jnp.sum(part, axis=0, keepdims=True)

    return pl.pallas_call(
        body, name="bias_bwd", out_shape=jax.ShapeDtypeStruct((N_BUCKETS, LANE), f32),
    )(dband0, dband1, bucket)


W_IN_SHARD = D_IN // N_CHIPS
_ALIGNED_PIECES = ((0, 0, 512), (1, 190, 578), (2, 0, 124), (2, 124, 578), (3, 0, 570), (0, 512, 578), (1, 0, 62),
                   (1, 62, 190), (3, 570, 578))
_SHARD_PIECES = (((0, 512), (2048, 2114)), ((2114, 2176), (2176, 2304), (512, 900)), ((900, 1024), (1024, 1478)),
                 ((1478, 2048), (2304, 2312)))


def align_w_in(shards, tr=256):
    def body(s_ref, o_ref):
        parts = [s_ref[k, :, a:b] for k, a, b in _ALIGNED_PIECES]
        parts.append(jnp.zeros((tr, LANE - SSM_HEADS), s_ref.dtype))
        o_ref[...] = jnp.concatenate(parts, axis=-1)

    return pl.pallas_call(
        body, name="align_w_in", grid=(D_MODEL // tr,),
        in_specs=[pl.BlockSpec((N_CHIPS, tr, W_IN_SHARD), lambda i: (0, i, 0))],
        out_specs=pl.BlockSpec((tr, PW), lambda i: (i, 0)),
        out_shape=jax.ShapeDtypeStruct((D_MODEL, PW), shards.dtype),
        compiler_params=_params("arbitrary"),
    )(shards)


def split_w_in_grad(dw, tr=256):
    def body(d_ref, o16_ref):
        for k, pieces in enumerate(_SHARD_PIECES):
            o16_ref[k] = jnp.concatenate([d_ref[:, a:b] for a, b in pieces], axis=-1).astype(bf16)

    return pl.pallas_call(
        body, name="split_w_in_grad", grid=(D_MODEL // tr,),
        in_specs=[pl.BlockSpec((tr, PW), lambda i: (i, 0))],
        out_specs=pl.BlockSpec((N_CHIPS, tr, W_IN_SHARD), lambda i: (0, i, 0)),
        out_shape=jax.ShapeDtypeStruct((N_CHIPS, D_MODEL, W_IN_SHARD), bf16),
        compiler_params=_params("arbitrary"),
    )(dw)

def in_fwd(x, g, w, tm=512):
    def body(x_ref, g_ref, w_ref, o_ref):
        xv = x_ref[...]
        h = xv * _rms(xv) * g_ref[...]
        o_ref[...] = _bdot(h, w_ref[...])

    return pl.pallas_call(
        body, name="in_fwd", grid=(SEQ // tm,),
        in_specs=[pl.BlockSpec((tm, D_MODEL), lambda i: (i, 0)), _full((1, D_MODEL)), _resident((D_MODEL, PW))],
        out_specs=pl.BlockSpec((tm, PW), lambda i: (i, 0)),
        out_shape=jax.ShapeDtypeStruct((SEQ, PW), f32),
        compiler_params=_params("arbitrary"),
    )(x, g, w)


def _resident(shape):
    return pl.BlockSpec(shape, lambda *_: (0,) * len(shape), pipeline_mode=pl.Buffered(1))


def in_bwd(dq, dz, dxbc, dk, dv, ddt, x, g, w, dres, tm=512):
    def body(dq_ref, dz_ref, dx_ref, dk_ref, dv_ref, ddt_ref, x_ref, g_ref, w_ref, dres_ref, o_ref, dw_ref, dg_ref):
        i = pl.program_id(0)

        @pl.when(i == 0)
        def _():
            dw_ref[...] = jnp.zeros_like(dw_ref)
            dg_ref[...] = jnp.zeros_like(dg_ref)

        dproj = jnp.concatenate([dq_ref[...], dz_ref[...], dx_ref[...], dk_ref[...], dv_ref[...], ddt_ref[...]],
                                axis=-1).astype(bf16)
        xv = x_ref[...]
        r = _rms(xv)
        xhat = xv * r
        gv = g_ref[...]
        h = xhat * gv
        dw_ref[...] += _bdot_tn(h, dproj)
        dh = _bdot_nt(dproj, w_ref[...])
        dg_ref[...] += jnp.sum(dh * xhat, axis=0, keepdims=True)
        o_ref[...] = dres_ref[...] + _rms_bwd(dh, xhat, r, gv)

    tok = lambda w_: pl.BlockSpec((tm, w_), lambda i: (i, 0))
    return pl.pallas_call(
        body, name="in_bwd", grid=(SEQ // tm,),
        in_specs=[tok(D_ATTN), tok(D_SSM), tok(D_CONV), tok(D_KV), tok(D_KV), tok(LANE), tok(D_MODEL),
                  _full((1, D_MODEL)), _resident((D_MODEL, PW)), tok(D_MODEL)],
        out_specs=[tok(D_MODEL), _resident((D_MODEL, PW)), _full((1, D_MODEL))],
        out_shape=[jax.ShapeDtypeStruct((SEQ, D_MODEL), f32), jax.ShapeDtypeStruct((D_MODEL, PW), f32),
                   jax.ShapeDtypeStruct((1, D_MODEL), f32)],
        compiler_params=_params("arbitrary"),
    )(dq, dz, dxbc, dk, dv, ddt, x, g, w, dres)


GATHER_SUB = D_MODEL // 4
TOKEN_TILE = 512


def in_fwd_gather(x, g, win, wout, conv):
    half = D_MODEL // 2
    n_sub = half // GATHER_SUB
    n_w = n_sub * 3

    def body(x_ref, g_ref, win_ref, wout_ref, conv_ref, proj_ref, w_ref, lout_ref, gconv_ref, land, h_scr,
             i_ssem, i_rsem, r_ssem, r_rsem, own_sem, o_ssem, o_rsem, of_ssem, of_rsem, c_ssem, c_rsem):
        xc, yc, c = _coords()
        k_me = 2 * xc + yc
        chips = _peer_chips(xc, yc)
        sibling = (xc, yc, 1 - c)

        def rows(cc, s):
            return pl.ds(pl.multiple_of(cc * half + s * GATHER_SUB, GATHER_SUB), GATHER_SUB)

        own = pltpu.make_async_copy(win_ref, land.at[k_me], own_sem)
        own.start()
        ici, relay, relayed = [], [], []
        for s in range(n_sub):
            for j, chip in enumerate(chips):
                i = 3 * s + j
                kj = 2 * chip[0] + chip[1]
                ici.append(_remote(win_ref.at[rows(c, s), :], land.at[k_me, rows(c, s), :], i_ssem.at[i], i_rsem.at[i], (*chip, c)))
                got = land.at[kj, rows(c, s), :]
                relay.append(_remote(got, got, r_ssem.at[i], r_rsem.at[i], sibling))
                other = land.at[kj, rows(1 - c, s), :]
                relayed.append(_remote(other, other, r_ssem.at[i], r_rsem.at[i], sibling))
        o_ici = _gather_copies("ici", [wout_ref], [lout_ref], o_ssem, o_rsem)
        o_relay = _gather_copies("relay", [wout_ref], [lout_ref], of_ssem, of_rsem)
        targets = [(*chip, c) for chip in chips] + [sibling]
        conv_cps = [_remote(conv_ref, gconv_ref.at[k_me], c_ssem.at[j], c_rsem.at[j], t) for j, t in enumerate(targets)]
        for cp in ici + o_ici + conv_cps + [o_relay[3]]:
            cp.start()

        for t in range(SEQ // TOKEN_TILE):
            tok = slice(t * TOKEN_TILE, (t + 1) * TOKEN_TILE)
            xv = x_ref[tok, :]
            h_scr[tok, :] = (xv * _rms(xv) * g_ref[...]).astype(bf16)

        for s in range(n_sub):
            for j in range(3):
                ici[3 * s + j].wait_recv()
                relay[3 * s + j].start()
            for j in range(3):
                relayed[3 * s + j].wait_recv()
            if s == 0:
                own.wait()
            for hq in range(2):
                q = hq * n_sub + s
                kr = slice(q * GATHER_SUB, (q + 1) * GATHER_SUB)
                parts = [land[k, kr, a:b] for k, a, b in _ALIGNED_PIECES]
                parts.append(jnp.zeros((GATHER_SUB, LANE - SSM_HEADS), bf16))
                w_ref[kr, :] = jnp.concatenate(parts, axis=-1)
                for t in range(SEQ // TOKEN_TILE):
                    tok = slice(t * TOKEN_TILE, (t + 1) * TOKEN_TILE)
                    part = jnp.dot(h_scr[tok, kr], w_ref[kr, :], preferred_element_type=f32)
                    if s == 0 and hq == 0:
                        proj_ref[tok, :] = part
                    else:
                        proj_ref[tok, :] += part

        for cp, fw in zip(o_ici, o_relay[:3]):
            cp.wait_recv()
            fw.start()
        for cp in conv_cps + o_relay:
            cp.wait_recv()
        for cp in ici + relay + o_ici + o_relay + conv_cps:
            cp.wait_send()

    sems = lambda k: pltpu.SemaphoreType.DMA((k,))
    vmem = pl.BlockSpec(memory_space=pltpu.VMEM)
    return pl.pallas_call(
        body, name="in_fwd_gather",
        out_shape=[jax.ShapeDtypeStruct((SEQ, PW), f32), jax.ShapeDtypeStruct((D_MODEL, PW), bf16),
                   jax.ShapeDtypeStruct((N_CHIPS,) + wout.shape, wout.dtype), jax.ShapeDtypeStruct((N_CHIPS,) + conv.shape, conv.dtype)],
        in_specs=[vmem, vmem, HBM, HBM, HBM], out_specs=[vmem, vmem, HBM, HBM],
        scratch_shapes=[pltpu.VMEM((N_CHIPS, D_MODEL, W_IN_SHARD), bf16), pltpu.VMEM((SEQ, D_MODEL), bf16),
                        sems(n_w), sems(n_w), sems(n_w), sems(n_w), pltpu.SemaphoreType.DMA,
                        sems(3), sems(3), sems(4), sems(4), sems(N_CHIPS), sems(N_CHIPS)],
        compiler_params=pltpu.CompilerParams(vmem_limit_bytes=VMEM_LIMIT),
    )(x, g, win, wout, conv)


def _attn_softmax_t(qk, bias_t, sink, first, key_row):
    s = qk * (HEAD_DIM ** -0.5) + bias_t
    s = jnp.where(jnp.logical_and(first, key_row < BLOCK), NEG, s)
    m = jnp.maximum(jnp.max(s, axis=0, keepdims=True), sink)
    p = jnp.exp(s - m)
    psink = jnp.exp(sink - m)
    inv = 1.0 / (jnp.sum(p, axis=0, keepdims=True) + psink)
    return p * inv, psink * inv


def _rms_t(x_t):
    return lax.rsqrt(jnp.mean(x_t * x_t, axis=0, keepdims=True) + EPS)


def attn_fwd_t(proj, q_gain_col, k_gain, sinks, bias_t):
    kcol, vcol = OFF_K // D_KV, OFF_V // D_KV

    def body(q_ref, kc_ref, kp_ref, vc_ref, vp_ref, qg_ref, kg_ref, sink_ref, bias_ref, o_ref, ot_scr):
        n = pl.program_id(0)
        first = n == 0
        key_row = lax.broadcasted_iota(jnp.int32, (2 * BLOCK, BLOCK), 0)
        k2 = jnp.concatenate([kp_ref[...], kc_ref[...]], axis=0)
        v_t = jnp.concatenate([vp_ref[...], vc_ref[...]], axis=0).T
        q_t = q_ref[...].T
        qg = jnp.broadcast_to(qg_ref[...], (HEAD_DIM, BLOCK))
        kg = kg_ref[...]
        for hk in range(N_KV_HEADS):
            sl = slice(hk * HEAD_DIM, (hk + 1) * HEAD_DIM)
            kk = k2[:, sl]
            kn = (kk * _rms(kk) * kg).astype(bf16)
            vt = v_t[sl, :].astype(bf16)
            heads = range(hk * Q_PER_KV, (hk + 1) * Q_PER_KV)
            qns = []
            for h in heads:
                qh = q_t[h * HEAD_DIM:(h + 1) * HEAD_DIM, :]
                qns.append(qh * _rms_t(qh) * qg)
            scores = [_bdot(kn, qn) for qn in qns]
            for h, s in zip(heads, scores):
                p, _ = _attn_softmax_t(s, bias_ref[h], sink_ref[h], first, key_row)
                ot_scr[h * HEAD_DIM:(h + 1) * HEAD_DIM, :] = _bdot(vt, p)
        o_ref[...] = ot_scr[...].T

    prev = lambda n: jnp.maximum(n - 1, 0)
    return pl.pallas_call(
        body, name="attn_fwd", grid=(N_BLOCKS,),
        in_specs=[pl.BlockSpec((BLOCK, D_ATTN), lambda n: (n, 0)),
                  pl.BlockSpec((BLOCK, D_KV), lambda n: (n, kcol)), pl.BlockSpec((BLOCK, D_KV), lambda n: (prev(n), kcol)),
                  pl.BlockSpec((BLOCK, D_KV), lambda n: (n, vcol)), pl.BlockSpec((BLOCK, D_KV), lambda n: (prev(n), vcol)),
                  _full((HEAD_DIM, 1)), _full((1, HEAD_DIM)), pl.BlockSpec(memory_space=pltpu.SMEM),
                  _full((N_Q_HEADS, 2 * BLOCK, BLOCK))],
        out_specs=pl.BlockSpec((BLOCK, D_ATTN), lambda n: (n, 0)),
        out_shape=jax.ShapeDtypeStruct((SEQ, D_ATTN), f32),
        scratch_shapes=[pltpu.VMEM((D_ATTN, BLOCK), f32)],
        compiler_params=_params("arbitrary"),
    )(proj, proj, proj, proj, proj, q_gain_col, k_gain, sinks, bias_t)


def attn_bwd_t(proj, d_out, q_gain_col, k_gain, sinks, bias_t):
    kcol, vcol = OFF_K // D_KV, OFF_V // D_KV

    def body(q_ref, kc_ref, kp_ref, vc_ref, vp_ref, do_ref, qg_ref, kg_ref, sink_ref, bias_ref,
             dq_ref, dk_ref, dv_ref, dband_ref, dsink_ref, dqg_ref, dkg_ref, dkn_scr, dv_scr, dqt_scr, dsink_acc, dqg_acc):
        i = pl.program_id(0)
        first = i == N_BLOCKS - 1

        @pl.when(i == 0)
        def _():
            for ref in (dband_ref, dkg_ref, dkn_scr, dv_scr, dsink_acc, dqg_acc):
                ref[...] = jnp.zeros_like(ref)

        key_row = lax.broadcasted_iota(jnp.int32, (2 * BLOCK, BLOCK), 0)
        k2 = jnp.concatenate([kp_ref[...], kc_ref[...]], axis=0)
        v2 = jnp.concatenate([vp_ref[...], vc_ref[...]], axis=0)
        q_t = q_ref[...].T
        do_t = do_ref[...].T
        qg = jnp.broadcast_to(qg_ref[...], (HEAD_DIM, BLOCK))
        kg = kg_ref[...]
        scale = HEAD_DIM ** -0.5
        for hk in range(N_KV_HEADS):
            sl = slice(hk * HEAD_DIM, (hk + 1) * HEAD_DIM)
            kk = k2[:, sl]
            rk = _rms(kk)
            khat = kk * rk
            kn = (khat * kg).astype(bf16)
            vb = v2[:, sl].astype(bf16)
            dkn = jnp.zeros((2 * BLOCK, HEAD_DIM), f32)
            dvv = jnp.zeros((2 * BLOCK, HEAD_DIM), f32)
            heads = range(hk * Q_PER_KV, (hk + 1) * Q_PER_KV)
            rqs, qhats, qns, d_os = [], [], [], []
            for h in heads:
                hs = slice(h * HEAD_DIM, (h + 1) * HEAD_DIM)
                qh = q_t[hs, :]
                rqs.append(_rms_t(qh))
                qhats.append(qh * rqs[-1])
                qns.append((qhats[-1] * qg).astype(bf16))
                d_os.append(do_t[hs, :].astype(bf16))
            scores = [_bdot(kn, qn) for qn in qns]
            dps = [_bdot(vb, d_o) for d_o in d_os]
            ps, dss = [], []
            for h, s, dp in zip(heads, scores, dps):
                p, psink = _attn_softmax_t(s, bias_ref[h], sink_ref[h], first, key_row)
                delta = jnp.sum(p * dp, axis=0, keepdims=True)
                ds = p * (dp - delta)
                dband_ref[h] += ds
                dsink_acc[h:h + 1, :] += -(psink * delta)
                ps.append(p.astype(bf16))
                dss.append(ds.astype(bf16))
            dqns = [_bdot_tn(kn, ds) * scale for ds in dss]
            for ds, qn, p, d_o in zip(dss, qns, ps, d_os):
                dkn = dkn + _bdot_nt(ds, qn) * scale
                dvv = dvv + _bdot_nt(p, d_o)
            for h, dqn, rq, qhat in zip(heads, dqns, rqs, qhats):
                dqg_acc[...] += dqn * qhat
                t = dqn * qg
                dqt_scr[h * HEAD_DIM:(h + 1) * HEAD_DIM, :] = rq * (t - qhat * jnp.mean(t * qhat, axis=0, keepdims=True))
            dkn_cur = dkn[BLOCK:] + dkn_scr[:, sl]
            dkn_scr[:, sl] = dkn[:BLOCK]
            khat_c, rk_c = khat[BLOCK:], rk[BLOCK:]
            dkg_ref[...] += jnp.sum(dkn_cur * khat_c, axis=0, keepdims=True)
            dk_ref[:, sl] = _rms_bwd(dkn_cur, khat_c, rk_c, kg)
            dv_ref[:, sl] = dvv[BLOCK:] + dv_scr[:, sl]
            dv_scr[:, sl] = dvv[:BLOCK]
        dq_ref[...] = dqt_scr[...].T

        @pl.when(i == N_BLOCKS - 1)
        def _():
            dsink_ref[...] = jnp.sum(dsink_acc[...], axis=1, keepdims=True)
            dqg_ref[...] = jnp.sum(dqg_acc[...], axis=1, keepdims=True)

    blk = lambda i: N_BLOCKS - 1 - i
    prev = lambda i: jnp.maximum(N_BLOCKS - 2 - i, 0)
    return pl.pallas_call(
        body, name="attn_bwd", grid=(N_BLOCKS,),
        in_specs=[pl.BlockSpec((BLOCK, D_ATTN), lambda i: (blk(i), 0)),
                  pl.BlockSpec((BLOCK, D_KV), lambda i: (blk(i), kcol)), pl.BlockSpec((BLOCK, D_KV), lambda i: (prev(i), kcol)),
                  pl.BlockSpec((BLOCK, D_KV), lambda i: (blk(i), vcol)), pl.BlockSpec((BLOCK, D_KV), lambda i: (prev(i), vcol)),
                  pl.BlockSpec((BLOCK, D_ATTN), lambda i: (blk(i), 0)),
                  _full((HEAD_DIM, 1)), _full((1, HEAD_DIM)), pl.BlockSpec(memory_space=pltpu.SMEM),
                  _full((N_Q_HEADS, 2 * BLOCK, BLOCK))],
        out_specs=[pl.BlockSpec((BLOCK, D_ATTN), lambda i: (blk(i), 0)), pl.BlockSpec((BLOCK, D_KV), lambda i: (blk(i), 0)),
                   pl.BlockSpec((BLOCK, D_KV), lambda i: (blk(i), 0)), _full((N_Q_HEADS, 2 * BLOCK, BLOCK)),
                   _full((N_Q_HEADS, 1)), _full((HEAD_DIM, 1)), _full((1, HEAD_DIM))],
        out_shape=[jax.ShapeDtypeStruct((SEQ, D_ATTN), f32), jax.ShapeDtypeStruct((SEQ, D_KV), f32),
                   jax.ShapeDtypeStruct((SEQ, D_KV), f32), jax.ShapeDtypeStruct((N_Q_HEADS, 2 * BLOCK, BLOCK), f32),
                   jax.ShapeDtypeStruct((N_Q_HEADS, 1), f32), jax.ShapeDtypeStruct((HEAD_DIM, 1), f32),
                   jax.ShapeDtypeStruct((1, HEAD_DIM), f32)],
        scratch_shapes=[pltpu.VMEM((BLOCK, D_KV), f32), pltpu.VMEM((BLOCK, D_KV), f32), pltpu.VMEM((D_ATTN, BLOCK), f32),
                        pltpu.VMEM((N_Q_HEADS, BLOCK), f32), pltpu.VMEM((HEAD_DIM, BLOCK), f32)],
        compiler_params=_params("arbitrary"),
    )(proj, proj, proj, proj, proj, d_out, q_gain_col, k_gain, sinks, bias_t)


SUBLANES = 8


def _shift_down(u, s, row8):
    if s == 0:
        return u
    r = pltpu.roll(u, s, 0)
    return jnp.concatenate([jnp.where(row8 >= s, r[:SUBLANES], 0.0), r[SUBLANES:]], axis=0)


def _shift_up(u, s, row8):
    if s == 0:
        return u
    r = pltpu.roll(u, SEQ - s, 0)
    return jnp.concatenate([r[:-SUBLANES], jnp.where(row8 < SUBLANES - s, r[-SUBLANES:], 0.0)], axis=0)


def conv_fwd(proj, conv_w, conv_b):
    xcol = OFF_X // LANE

    def body(u_ref, w_ref, b_ref, o_ref):
        u = u_ref[...]
        row = lax.broadcasted_iota(jnp.int32, (SUBLANES, LANE), 0)
        pre = b_ref[...] + jnp.zeros_like(u)
        for k in range(CONV_WIDTH):
            pre = pre + w_ref[k:k + 1, :] * _shift_down(u, CONV_WIDTH - 1 - k, row)
        o_ref[...] = pre * _sigmoid(pre)

    return pl.pallas_call(
        body, name="conv_fwd", grid=(D_CONV // LANE,),
        in_specs=[pl.BlockSpec((SEQ, LANE), lambda j: (0, xcol + j)), pl.BlockSpec((CONV_WIDTH, LANE), lambda j: (0, j)),
                  pl.BlockSpec((1, LANE), lambda j: (0, j))],
        out_specs=pl.BlockSpec((SEQ, LANE), lambda j: (0, j)),
        out_shape=jax.ShapeDtypeStruct((SEQ, D_CONV), f32),
        compiler_params=_params("arbitrary"),
    )(proj, conv_w, conv_b)


def conv_bwd(proj, d_act, conv_w, conv_b):
    xcol = OFF_X // LANE

    def body(u_ref, da_ref, w_ref, b_ref, du_ref, dw_ref, db_ref):
        u = u_ref[...]
        row = lax.broadcasted_iota(jnp.int32, (SUBLANES, LANE), 0)
        shifted = [_shift_down(u, CONV_WIDTH - 1 - k, row) for k in range(CONV_WIDTH)]
        pre = b_ref[...] + jnp.zeros_like(u)
        for k in range(CONV_WIDTH):
            pre = pre + w_ref[k:k + 1, :] * shifted[k]
        sg = _sigmoid(pre)
        dpre = da_ref[...] * (sg * (1.0 + pre * (1.0 - sg)))
        db_ref[...] = jnp.sum(dpre, axis=0, keepdims=True)
        du = jnp.zeros_like(u)
        for k in range(CONV_WIDTH):
            dw_ref[k:k + 1, :] = jnp.sum(dpre * shifted[k], axis=0, keepdims=True)
            du = du + w_ref[k:k + 1, :] * _shift_up(dpre, CONV_WIDTH - 1 - k, row)
        du_ref[...] = du

    return pl.pallas_call(
        body, name="conv_bwd", grid=(D_CONV // LANE,),
        in_specs=[pl.BlockSpec((SEQ, LANE), lambda j: (0, xcol + j)), pl.BlockSpec((SEQ, LANE), lambda j: (0, j)),
                  pl.BlockSpec((CONV_WIDTH, LANE), lambda j: (0, j)), pl.BlockSpec((1, LANE), lambda j: (0, j))],
        out_specs=[pl.BlockSpec((SEQ, LANE), lambda j: (0, j)), pl.BlockSpec((CONV_WIDTH, LANE), lambda j: (0, j)),
                   pl.BlockSpec((1, LANE), lambda j: (0, j))],
        out_shape=[jax.ShapeDtypeStruct((SEQ, D_CONV), f32), jax.ShapeDtypeStruct((CONV_WIDTH, D_CONV), f32),
                   jax.ShapeDtypeStruct((1, D_CONV), f32)],
        compiler_params=_params("arbitrary"),
    )(proj, d_act, conv_w, conv_b)


def _ssd_chunk_common(dt_raw, dtb, alog):
    row = lax.broadcasted_iota(jnp.int32, (CHUNK, CHUNK), 0)
    col = lax.broadcasted_iota(jnp.int32, (CHUNK, CHUNK), 1)
    tri = (row >= col).astype(f32)
    strict = (row > col).astype(f32)
    dtp = _softplus(dt_raw + dtb)
    a_row = -jnp.exp(alog)
    d_a = dtp * a_row
    cs = _hdot(tri, d_a)
    cs_last = cs[CHUNK - 1:CHUNK, :]
    return row, col, dtp, a_row, cs, cs.T, cs_last


def _seg_decay(cs, cs_t, hd, row, col):
    seg = cs[:, hd:hd + 1] - cs_t[hd:hd + 1, :]
    return jnp.where(row >= col, jnp.exp(seg), 0.0)


GROUP_W = HEADS_PER_GROUP * SSM_HEAD_DIM


def _group_indicator(g):
    j = lax.broadcasted_iota(jnp.int32, (GROUP_W, LANE), 0)
    lane = lax.broadcasted_iota(jnp.int32, (GROUP_W, LANE), 1)
    return (lane == g * HEADS_PER_GROUP + j // SSM_HEAD_DIM).astype(bf16)


def _bf16_pieces(a, n):
    pieces = []
    for _ in range(n):
        p = a.astype(bf16)
        pieces.append(p)
        a = a - p.astype(f32)
    return pieces


def _head_spread(a, ind):
    return sum(lax.dot_general(p, ind, (((1,), (1,)), ((), ())), preferred_element_type=f32) for p in _bf16_pieces(a, 3))


def _head_sums(a, ind):
    return sum(jnp.dot(p, ind, preferred_element_type=f32) for p in _bf16_pieces(a, 2))


def ssd_fwd_g(act, proj, dt_bias, a_log, d_skip, norm_g):
    zcol, dtcol = OFF_Z // D_SSM, OFF_DT // LANE

    def body(act_ref, z_ref, dt_ref, dtb_ref, alog_ref, dsk_ref, ng_ref, out_ref, ypre_ref, st_ref, state):
        c = pl.program_id(0)

        @pl.when(c == 0)
        def _():
            state[...] = jnp.zeros_like(state)

        row, col, dtp, a_row, cs, cs_t, cs_last = _ssd_chunk_common(dt_ref[...], dtb_ref[...], alog_ref[...])
        e_cs = jnp.exp(cs)
        dte = jnp.exp(cs_last - cs)
        rows8 = jnp.concatenate([jnp.exp(cs_last), dsk_ref[...], jnp.zeros((6, LANE), f32)], axis=0)
        z = z_ref[...]
        sz = z * _sigmoid(z)
        ng = ng_ref[...]
        for g in range(SSM_GROUPS):
            gs = slice(g * GROUP_W, (g + 1) * GROUP_W)
            ind = _group_indicator(g)
            xg = act_ref[:, gs]
            bg = act_ref[:, D_SSM + g * SSM_STATE:D_SSM + (g + 1) * SSM_STATE]
            cg = act_ref[:, D_SSM + D_BC + g * SSM_STATE:D_SSM + D_BC + (g + 1) * SSM_STATE]
            dt_e, e_e, dte_e = _head_spread(dtp, ind), _head_spread(e_cs, ind), _head_spread(dte, ind)
            rows_e = _head_spread(rows8, ind)
            ecl_e, dsk_e = rows_e[0:1], rows_e[1:2]
            xdt = xg * dt_e
            prev = state[g]
            st_ref[0, g] = prev
            cb = _bdot_nt(cg, bg)
            goff = _bdot(cg, prev)
            snew = _bdot_tn(bg, xdt * dte_e)
            heads = range(g * HEADS_PER_GROUP, (g + 1) * HEADS_PER_GROUP)
            ms = [cb * _seg_decay(cs, cs_t, hd, row, col) for hd in heads]
            yd = [_bdot(m, xdt[:, r * SSM_HEAD_DIM:(r + 1) * SSM_HEAD_DIM]) for r, m in enumerate(ms)]
            y = jnp.concatenate(yd, axis=1) + e_e * goff + xg * dsk_e
            state[g] = prev * ecl_e + snew
            ypre_ref[:, gs] = y
            part = y * sz[:, gs]
            out_ref[:, gs] = part * _rms(part) * ng[:, gs]

    return pl.pallas_call(
        body, name="ssd_fwd", grid=(N_CHUNKS,),
        in_specs=[pl.BlockSpec((CHUNK, D_CONV), lambda c: (c, 0)), pl.BlockSpec((CHUNK, D_SSM), lambda c: (c, zcol)),
                  pl.BlockSpec((CHUNK, LANE), lambda c: (c, dtcol)), _full((1, LANE)), _full((1, LANE)), _full((1, LANE)),
                  _full((1, D_SSM))],
        out_specs=[pl.BlockSpec((CHUNK, D_SSM), lambda c: (c, 0)), pl.BlockSpec((CHUNK, D_SSM), lambda c: (c, 0)),
                   pl.BlockSpec((1, SSM_GROUPS, SSM_STATE, GROUP_W), lambda c: (c, 0, 0, 0))],
        out_shape=[jax.ShapeDtypeStruct((SEQ, D_SSM), f32), jax.ShapeDtypeStruct((SEQ, D_SSM), f32),
                   jax.ShapeDtypeStruct((N_CHUNKS, SSM_GROUPS, SSM_STATE, GROUP_W), f32)],
        scratch_shapes=[pltpu.VMEM((SSM_GROUPS, SSM_STATE, GROUP_W), f32)],
        compiler_params=_params("arbitrary"),
    )(act, proj, proj, dt_bias, a_log, d_skip, norm_g)


def ssd_bwd_g(act, proj, ypre, states, d_out, dt_bias, a_log, d_skip, norm_g):
    zcol, dtcol = OFF_Z // D_SSM, OFF_DT // LANE

    def body(act_ref, z_ref, dt_ref, ypre_ref, st_ref, do_ref, dtb_ref, alog_ref, dsk_ref, ng_ref,
             dact_ref, ddt_ref, dz_ref, dng_ref, dpar_ref, dstate):
        i = pl.program_id(0)

        @pl.when(i == 0)
        def _():
            for ref in (dng_ref, dpar_ref, dstate):
                ref[...] = jnp.zeros_like(ref)

        row, col, dtp, a_row, cs, cs_t, cs_last = _ssd_chunk_common(dt_ref[...], dtb_ref[...], alog_ref[...])
        upper = (row <= col).astype(f32)
        lane = lax.broadcasted_iota(jnp.int32, (CHUNK, LANE), 1)
        rowl = lax.broadcasted_iota(jnp.int32, (CHUNK, LANE), 0)
        e_cs = jnp.exp(cs)
        dte = jnp.exp(cs_last - cs)
        ecl = jnp.exp(cs_last)
        rows8 = jnp.concatenate([ecl, dsk_ref[...], jnp.zeros((6, LANE), f32)], axis=0)
        z = z_ref[...]
        sgz = _sigmoid(z)
        sz = z * sgz
        ng = ng_ref[...]
        ddt_mat = jnp.zeros((CHUNK, LANE), f32)
        dcs_mat = jnp.zeros((CHUNK, LANE), f32)
        dcs_t = jnp.zeros((LANE, CHUNK), f32)
        dcsl_row = jnp.zeros((1, LANE), f32)
        dd_row = jnp.zeros((1, LANE), f32)
        for g in range(SSM_GROUPS):
            gs = slice(g * GROUP_W, (g + 1) * GROUP_W)
            bsl = slice(D_SSM + g * SSM_STATE, D_SSM + (g + 1) * SSM_STATE)
            csl = slice(D_SSM + D_BC + g * SSM_STATE, D_SSM + D_BC + (g + 1) * SSM_STATE)
            ind = _group_indicator(g)
            y = ypre_ref[:, gs]
            part = y * sz[:, gs]
            r = _rms(part)
            yhat = part * r
            d_o = do_ref[:, gs]
            dng_ref[:, gs] += jnp.sum(d_o * yhat, axis=0, keepdims=True)
            dyz = _rms_bwd(d_o, yhat, r, ng[:, gs])
            dy = dyz * sz[:, gs]
            dz_ref[:, gs] = dyz * y * (sgz[:, gs] * (1.0 + z[:, gs] * (1.0 - sgz[:, gs])))

            xg = act_ref[:, gs]
            bg = act_ref[:, bsl]
            cg = act_ref[:, csl]
            dt_e, e_e, dte_e = _head_spread(dtp, ind), _head_spread(e_cs, ind), _head_spread(dte, ind)
            rows_e = _head_spread(rows8, ind)
            ecl_e, dsk_e = rows_e[0:1], rows_e[1:2]
            xdt = xg * dt_e
            prev = st_ref[0, g]
            dh = dstate[g]
            heads = range(g * HEADS_PER_GROUP, (g + 1) * HEADS_PER_GROUP)
            hsl = [slice(r_ * SSM_HEAD_DIM, (r_ + 1) * SSM_HEAD_DIM) for r_ in range(HEADS_PER_GROUP)]
            cb = _bdot_nt(cg, bg)
            lms = [_seg_decay(cs, cs_t, hd, row, col) for hd in heads]
            ms = [cb * lm for lm in lms]
            gmat = _bdot(cg, prev)
            dgm = dy * e_e
            dcg = _bdot_nt(dgm, prev)
            dprev = _bdot_tn(cg, dgm)
            dbg = _bdot_nt(xdt * dte_e, dh)
            dw = _bdot(bg, dh)
            dms = [_bdot_nt(dy[:, s_], xdt[:, s_]) for s_ in hsl]
            dxdts = [_bdot_tn(m, dy[:, s_]) for m, s_ in zip(ms, hsl)]
            dxdt = jnp.concatenate(dxdts, axis=1) + dw * dte_e
            dact_ref[:, gs] = dy * dsk_e + dxdt * dt_e
            dstate[g] = dprev + dh * ecl_e
            dcb = jnp.zeros((CHUNK, CHUNK), f32)
            for hd, dm, lm, m in zip(heads, dms, lms, ms):
                dcb = dcb + dm * lm
                dseg = dm * m
                dcs_mat = dcs_mat + jnp.where(lane == hd, jnp.sum(dseg, axis=1, keepdims=True), 0.0)
                dcs_t = jnp.where(row == hd, jnp.sum(dseg, axis=0, keepdims=True), dcs_t)
            dact_ref[:, bsl] = dbg + _bdot_tn(dcb, cg)
            dact_ref[:, csl] = dcg + _bdot(dcb, bg)
            ddte = _head_sums(dw * xdt, ind) * dte
            dcs_mat = dcs_mat + _head_sums(dy * gmat, ind) * e_cs - ddte
            ddt_mat = ddt_mat + _head_sums(dxdt * xg, ind)
            dcsl_row = (dcsl_row + jnp.sum(ddte, axis=0, keepdims=True)
                        + jnp.sum(_head_sums(dh * prev, ind), axis=0, keepdims=True) * ecl)
            dd_row = dd_row + jnp.sum(_head_sums(dy * xg, ind), axis=0, keepdims=True)
        dcs_mat = dcs_mat - dcs_t.T + jnp.where(rowl == CHUNK - 1, dcsl_row, 0.0)
        dda = _hdot(upper, dcs_mat)
        ddt_mat = ddt_mat + dda * a_row
        da_row = jnp.sum(dda * dtp, axis=0, keepdims=True)
        ddt_raw = ddt_mat * _sigmoid(dt_ref[...] + dtb_ref[...])
        ddt_ref[...] = ddt_raw
        dpar_ref[0:1, :] += jnp.sum(ddt_raw, axis=0, keepdims=True)
        dpar_ref[1:2, :] += da_row * a_row
        dpar_ref[2:3, :] += dd_row

    blk = lambda i: N_CHUNKS - 1 - i
    return pl.pallas_call(
        body, name="ssd_bwd", grid=(N_CHUNKS,),
        in_specs=[pl.BlockSpec((CHUNK, D_CONV), lambda i: (blk(i), 0)), pl.BlockSpec((CHUNK, D_SSM), lambda i: (blk(i), zcol)),
                  pl.BlockSpec((CHUNK, LANE), lambda i: (blk(i), dtcol)), pl.BlockSpec((CHUNK, D_SSM), lambda i: (blk(i), 0)),
                  pl.BlockSpec((1, SSM_GROUPS, SSM_STATE, GROUP_W), lambda i: (blk(i), 0, 0, 0)),
                  pl.BlockSpec((CHUNK, D_SSM), lambda i: (blk(i), 0)),
                  _full((1, LANE)), _full((1, LANE)), _full((1, LANE)), _full((1, D_SSM))],
        out_specs=[pl.BlockSpec((CHUNK, D_CONV), lambda i: (blk(i), 0)), pl.BlockSpec((CHUNK, LANE), lambda i: (blk(i), 0)),
                   pl.BlockSpec((CHUNK, D_SSM), lambda i: (blk(i), 0)), _full((1, D_SSM)), _full((8, LANE))],
        out_shape=[jax.ShapeDtypeStruct((SEQ, D_CONV), f32), jax.ShapeDtypeStruct((SEQ, LANE), f32),
                   jax.ShapeDtypeStruct((SEQ, D_SSM), f32), jax.ShapeDtypeStruct((1, D_SSM), f32),
                   jax.ShapeDtypeStruct((8, LANE), f32)],
        scratch_shapes=[pltpu.VMEM((SSM_GROUPS, SSM_STATE, GROUP_W), f32)],
        compiler_params=_params("arbitrary"),
    )(act, proj, proj, ypre, states, d_out, dt_bias, a_log, d_skip, norm_g)


def out_fwd(x, attn, ssm, w_out, tm=512):
    def body(x_ref, a_ref, s_ref, w_ref, o_ref):
        o_ref[...] = x_ref[...] + _bdot(a_ref[...], w_ref[:D_ATTN, :]) + _bdot(s_ref[...], w_ref[D_ATTN:, :])

    tok = lambda w_: pl.BlockSpec((tm, w_), lambda i: (i, 0))
    return pl.pallas_call(
        body, name="out_fwd", grid=(SEQ // tm,),
        in_specs=[tok(D_MODEL), tok(D_ATTN), tok(D_SSM), _full((D_MODEL, D_MODEL))],
        out_specs=tok(D_MODEL), out_shape=jax.ShapeDtypeStruct((SEQ, D_MODEL), f32),
        compiler_params=_params("arbitrary"),
    )(x, attn, ssm, w_out)


def out_bwd(dx1, attn, ssm, w_out, tm=512):
    nt = SEQ // tm

    def body(d_ref, a_ref, s_ref, w_ref, da_ref, ds_ref, dw16_ref, dw_ref):
        i = pl.program_id(0)

        @pl.when(i == 0)
        def _():
            dw_ref[...] = jnp.zeros_like(dw_ref)

        d = d_ref[...].astype(bf16)
        dcat = _bdot_nt(d, w_ref[...])
        da_ref[...] = dcat[:, :D_ATTN]
        ds_ref[...] = dcat[:, D_ATTN:]
        dw_ref[:D_ATTN, :] += _bdot_tn(a_ref[...], d)
        dw_ref[D_ATTN:, :] += _bdot_tn(s_ref[...], d)

        @pl.when(i == nt - 1)
        def _():
            dw16_ref[...] = dw_ref[...].astype(bf16)

    tok = lambda w_: pl.BlockSpec((tm, w_), lambda i: (i, 0))
    return pl.pallas_call(
        body, name="out_bwd", grid=(nt,),
        in_specs=[tok(D_MODEL), tok(D_ATTN), tok(D_SSM), _resident((D_MODEL, D_MODEL))],
        out_specs=[tok(D_ATTN), tok(D_SSM), _resident((D_MODEL, D_MODEL))],
        out_shape=[jax.ShapeDtypeStruct((SEQ, D_ATTN), f32), jax.ShapeDtypeStruct((SEQ, D_SSM), f32),
                   jax.ShapeDtypeStruct((D_MODEL, D_MODEL), bf16)],
        scratch_shapes=[pltpu.VMEM((D_MODEL, D_MODEL), f32)],
        compiler_params=_params("arbitrary"),
    )(dx1, attn, ssm, w_out)


MLP_SUB = 256


def mlp_fwd(x1, g, w_up, w_down, tm=1024):
    def body(x_ref, g_ref, wu_ref, wd_ref, o_ref, u_ref, h_scr):
        j = pl.program_id(1)

        @pl.when(j == 0)
        def _():
            xv = x_ref[...]
            h_scr[...] = (xv * _rms(xv) * g_ref[...]).astype(bf16)
            o_ref[...] = xv

        for r in range(tm // MLP_SUB):
            rows = slice(r * MLP_SUB, (r + 1) * MLP_SUB)
            u = jnp.dot(h_scr[rows, :], wu_ref[...], preferred_element_type=f32)
            u_ref[rows, :] = u
            a = jnp.square(jnp.maximum(u, 0.0))
            o_ref[rows, :] += _bdot(a, wd_ref[...])

    return pl.pallas_call(
        body, name="mlp_fwd", grid=(SEQ // tm, N_CHIPS),
        in_specs=[pl.BlockSpec((tm, D_MODEL), lambda i, j: (i, 0)), _full((1, D_MODEL)),
                  pl.BlockSpec((None, D_MODEL, FF_TILE), lambda i, j: (j, 0, 0)),
                  pl.BlockSpec((None, FF_TILE, D_MODEL), lambda i, j: (j, 0, 0))],
        out_specs=[pl.BlockSpec((tm, D_MODEL), lambda i, j: (i, 0)), pl.BlockSpec((tm, FF_TILE), lambda i, j: (i, j))],
        out_shape=[jax.ShapeDtypeStruct((SEQ, D_MODEL), f32), jax.ShapeDtypeStruct((SEQ, D_FF), f32)],
        scratch_shapes=[pltpu.VMEM((tm, D_MODEL), bf16)],
        compiler_params=_params("arbitrary", "arbitrary"),
    )(x1, g, w_up, w_down)


def mlp_bwd_data(dx2, u, x1, g, w_up, w_down, tm=1024):
    def body(d_ref, u_ref, x_ref, g_ref, wu_ref, wd_ref, dx_ref, du_ref, dg_ref, dh_scr):
        i, j = pl.program_id(0), pl.program_id(1)

        @pl.when(jnp.logical_and(i == 0, j == 0))
        def _():
            dg_ref[...] = jnp.zeros_like(dg_ref)

        @pl.when(j == 0)
        def _():
            dh_scr[...] = jnp.zeros_like(dh_scr)

        for r in range(tm // MLP_SUB):
            rows = slice(r * MLP_SUB, (r + 1) * MLP_SUB)
            da = _bdot_nt(d_ref[rows, :], wd_ref[...])
            du = (da * (2.0 * jnp.maximum(u_ref[rows, :], 0.0))).astype(bf16)
            du_ref[rows, :] = du
            dh_scr[rows, :] += _bdot_nt(du, wu_ref[...])

        @pl.when(j == N_CHIPS - 1)
        def _():
            xv = x_ref[...]
            r = _rms(xv)
            xhat = xv * r
            dh = dh_scr[...]
            dg_ref[...] += jnp.sum(dh * xhat, axis=0, keepdims=True)
            dx_ref[...] = d_ref[...] + _rms_bwd(dh, xhat, r, g_ref[...])

    return pl.pallas_call(
        body, name="mlp_bwd_data", grid=(SEQ // tm, N_CHIPS),
        in_specs=[pl.BlockSpec((tm, D_MODEL), lambda i, j: (i, 0)), pl.BlockSpec((tm, FF_TILE), lambda i, j: (i, j)),
                  pl.BlockSpec((tm, D_MODEL), lambda i, j: (i, 0)), _full((1, D_MODEL)),
                  pl.BlockSpec((None, D_MODEL, FF_TILE), lambda i, j: (j, 0, 0)),
                  pl.BlockSpec((None, FF_TILE, D_MODEL), lambda i, j: (j, 0, 0))],
        out_specs=[pl.BlockSpec((tm, D_MODEL), lambda i, j: (i, 0)), pl.BlockSpec((tm, FF_TILE), lambda i, j: (i, j)),
                   _full((1, D_MODEL))],
        out_shape=[jax.ShapeDtypeStruct((SEQ, D_MODEL), f32), jax.ShapeDtypeStruct((SEQ, D_FF), bf16),
                   jax.ShapeDtypeStruct((1, D_MODEL), f32)],
        scratch_shapes=[pltpu.VMEM((tm, D_MODEL), f32)],
        compiler_params=_params("arbitrary", "arbitrary"),
    )(dx2, u, x1, g, w_up, w_down)


def mlp_bwd_weights(dx2, u, du, x1, g, tm=512):
    nt = SEQ // tm

    def body(d_ref, u_ref, du_ref, x_ref, g_ref, dwu16_ref, dwd16_ref, h_scr, d_scr, dwu_ref, dwd_ref):
        j, i = pl.program_id(0), pl.program_id(1)

        @pl.when(j == 0)
        def _():
            xv = x_ref[...]
            h_scr[i] = (xv * _rms(xv) * g_ref[...]).T.astype(bf16)
            d_scr[i] = d_ref[...].astype(bf16)

        @pl.when(i == 0)
        def _():
            dwu_ref[...] = jnp.zeros_like(dwu_ref)
            dwd_ref[...] = jnp.zeros_like(dwd_ref)

        dwu_ref[...] += jnp.dot(h_scr[i], du_ref[...], preferred_element_type=f32)
        a = jnp.square(jnp.maximum(u_ref[...], 0.0))
        dwd_ref[...] += _bdot_tn(a, d_scr[i])

        @pl.when(i == nt - 1)
        def _():
            dwu16_ref[...] = dwu_ref[...].astype(bf16)
            dwd16_ref[...] = dwd_ref[...].astype(bf16)

    up = pl.BlockSpec((None, D_MODEL, FF_TILE), lambda j, i: (j, 0, 0))
    down = pl.BlockSpec((None, FF_TILE, D_MODEL), lambda j, i: (j, 0, 0))
    first_pass = pl.BlockSpec((tm, D_MODEL), lambda j, i: (jnp.where(j == 0, i, nt - 1), 0))
    return pl.pallas_call(
        body, name="mlp_bwd_weights", grid=(N_CHIPS, nt),
        in_specs=[first_pass, pl.BlockSpec((tm, FF_TILE), lambda j, i: (i, j)),
                  pl.BlockSpec((tm, FF_TILE), lambda j, i: (i, j)), first_pass, _full((1, D_MODEL))],
        out_specs=[up, down],
        out_shape=[jax.ShapeDtypeStruct((N_CHIPS, D_MODEL, FF_TILE), bf16), jax.ShapeDtypeStruct((N_CHIPS, FF_TILE, D_MODEL), bf16)],
        scratch_shapes=[pltpu.VMEM((nt, D_MODEL, tm), bf16), pltpu.VMEM((nt, tm, D_MODEL), bf16),
                        pltpu.VMEM((D_MODEL, FF_TILE), f32), pltpu.VMEM((FF_TILE, D_MODEL), f32)],
        compiler_params=_params("arbitrary", "arbitrary"),
    )(dx2, u, du, x1, g)


def loss_head(y, target, tm=512):
    def body(y_ref, t_ref, dy_ref, l_ref):
        @pl.when(pl.program_id(0) == 0)
        def _():
            l_ref[...] = jnp.zeros_like(l_ref)

        d = y_ref[...] - t_ref[...]
        dy_ref[...] = d * (1.0 / D_MODEL)
        part = jnp.sum(jnp.mean(d * d, axis=-1, keepdims=True), axis=0, keepdims=True)
        l_ref[...] += 0.5 * part

    tok = pl.BlockSpec((tm, D_MODEL), lambda i: (i, 0))
    return pl.pallas_call(
        body, name="loss_head", grid=(SEQ // tm,), in_specs=[tok, tok], out_specs=[tok, _full((1, 1))],
        out_shape=[jax.ShapeDtypeStruct((SEQ, D_MODEL), f32), jax.ShapeDtypeStruct((1, 1), f32)],
        compiler_params=_params("arbitrary"),
    )(y, target)


def _pad_lane(v):
    return jnp.pad(v, (0, LANE - v.shape[0]))[None, :]


def local_step(x, target, w, prov):
    bucket = jnp.asarray(_bucket_table().T)
    bias = bias_build(w["rel_bias"], bucket)
    saved = []
    for l in range(DEPTH):
        g_mix = w["mix_norm_g"][l][None, :]
        proj, w_in = prov.project(l, x, g_mix)
        conv_b = w["conv_b"][l][None, :] + prov.stage(("post_in", l), proj)
        conv_w = prov.conv_w(l)
        act = conv_fwd(proj, conv_w, conv_b)
        dtb = _pad_lane(w["dt_bias"][l]) + prov.stage(("mid", l), act)
        alog, dsk = _pad_lane(w["a_log"][l]), _pad_lane(w["d_skip"][l])
        ng = w["ssm_norm_g"][l][None, :]
        ssm, ypre, states = ssd_fwd_g(act, proj, dtb, alog, dsk, ng)
        qg, kg = w["q_gain"][l][:, None] + 0.0 * ssm[:1, :1], w["k_gain"][l][None, :]
        attn = attn_fwd_t(proj, qg, kg, w["sinks"][l], bias)
        tok = prov.stage(("pre_out", l), attn)
        w_out = prov.w_out(l, attn) + jnp.asarray(tok, bf16)
        x1 = out_fwd(x, attn, ssm, w_out)
        g_mlp = w["mlp_norm_g"][l][None, :] + prov.stage(("pre_mlp", l), x1)
        w_up, w_down = prov.mlp(l, x1)
        x2, u = mlp_fwd(x1, g_mlp, w_up, w_down)
        saved.append(dict(x=x, proj=proj, attn=attn, act=act, ssm=ssm, ypre=ypre, states=states, x1=x1, u=u,
                          g_mix=g_mix, qg=qg, kg=kg, conv_b=conv_b, dtb=dtb, alog=alog, dsk=dsk, ng=ng, g_mlp=g_mlp,
                          w_in=w_in, w_out=w_out, w_up=w_up, w_down=w_down, conv_w=conv_w))
        x = x2
    dx, loss = loss_head(x, target)
    grads = [None] * DEPTH
    dbands = [None] * DEPTH
    tok = 0.0
    for l in reversed(range(DEPTH)):
        s = saved[l]
        g_mlp = s["g_mlp"] + tok
        dx1, du, dg_mlp = mlp_bwd_data(dx, s["u"], s["x1"], g_mlp, s["w_up"], s["w_down"])
        dw_up, dw_down = mlp_bwd_weights(dx, s["u"], du, s["x1"], g_mlp)
        tok = prov.grads(("mlp", l), dict(w_up=dw_up, w_down=dw_down), dx1)
        dattn, dssm, dw_out = out_bwd(dx1, s["attn"], s["ssm"], s["w_out"])
        dact, ddt, dz, dng, dpar = ssd_bwd_g(s["act"], s["proj"], s["ypre"], s["states"], dssm, s["dtb"] + tok, s["alog"],
                                           s["dsk"], s["ng"])
        conv_b = s["conv_b"] + prov.stage(("bwd_mid", l), dact)
        dxbc, dconv_w, dconv_b = conv_bwd(s["proj"], dact, s["conv_w"], conv_b)
        dq, dk, dv, dband, dsink, dqg, dkg = attn_bwd_t(s["proj"], dattn, s["qg"], s["kg"], w["sinks"][l], bias)
        dbands[l] = dband
        g_mix = s["g_mix"]
        if l == 0:
            d_rel = bias_bwd(dbands[0], dbands[1], bucket)
            g_mix = g_mix + 0.0 * d_rel[:1, :1]
        dx, dw_in, dg_mix = in_bwd(dq, dz, dxbc, dk, dv, ddt, s["x"], g_mix, s["w_in"], dx1)
        tok = prov.grads(("mix", l), dict(w_in=split_w_in_grad(dw_in), w_out=dw_out), dx)
        grads[l] = dict(mix_norm_g=dg_mix[0], q_gain=dqg[:, 0], k_gain=dkg[0], sinks=dsink[:, 0],
                        conv_w=dconv_w, conv_b=dconv_b[0], dt_bias=dpar[0, :SSM_HEADS], a_log=dpar[1, :SSM_HEADS],
                        d_skip=dpar[2, :SSM_HEADS], ssm_norm_g=dng[0], mlp_norm_g=dg_mlp[0])
    out = {k: jnp.stack([grads[l][k] for l in range(DEPTH)]) for k in grads[0]}
    out["rel_bias"] = d_rel[:, :N_Q_HEADS]
    return loss, dx, out, tok


MESH = pl.DeviceIdType.MESH
HBM = pl.BlockSpec(memory_space=pltpu.HBM)
N_DEVICES = 8


def _coords():
    return lax.axis_index("x"), lax.axis_index("y"), lax.axis_index("c")


def _peer_chips(x, y):
    return [(1 - x, y), (x, 1 - y), (1 - x, 1 - y)]


def _remote(src, dst, send_sem, recv_sem, device):
    return pltpu.make_async_remote_copy(src_ref=src, dst_ref=dst, send_sem=send_sem, recv_sem=recv_sem,
                                        device_id=device, device_id_type=MESH)


SEM = pl.BlockSpec(memory_space=pltpu.SEMAPHORE)
ANY = pl.BlockSpec(memory_space=pl.ANY)
DATAFLOW = pltpu.SideEffectType.DATAFLOW_SIDE_EFFECTING


def _gather_copies(kind, src_refs, land_refs, ssem, rsem):
    x, y, c = _coords()
    k_me = 2 * x + y
    n = len(land_refs)
    cps = []
    for p, land in enumerate(land_refs):
        hr = land.shape[1] // 2
        rows = pl.ds(c * hr, hr)
        for j, chip in enumerate(_peer_chips(x, y)):
            i = 3 * p + j
            if kind == "ici":
                cps.append(_remote(src_refs[p].at[rows, :], land.at[k_me, rows, :], ssem.at[i], rsem.at[i], (*chip, c)))
            else:
                got = land.at[2 * chip[0] + chip[1], rows, :]
                cps.append(_remote(got, got, ssem.at[i], rsem.at[i], (x, y, 1 - c)))
        if kind == "relay":
            cps.append(_remote(src_refs[p], land.at[k_me], ssem.at[3 * n + p], rsem.at[3 * n + p], (x, y, 1 - c)))
    return cps


def _gather_maker(kind, n_src):
    def make(refs, ssem, rsem):
        cps = _gather_copies(kind, refs[:n_src], refs[n_src:], ssem, rsem)
        return cps, cps
    return make


def _scatter_maker(n):
    def make(refs, ssem, rsem):
        x, y, c = _coords()
        k_me = 2 * x + y
        sends, arrivals = [], []
        for p in range(n):
            src, land = refs[p], refs[n + p]
            sends.append(_remote(src.at[k_me, 1 - c], land.at[0], ssem.at[7 * p], rsem.at[7 * p], (x, y, 1 - c)))
            for j, chip in enumerate(_peer_chips(x, y)):
                for cc in range(2):
                    sends.append(_remote(src.at[2 * chip[0] + chip[1], cc], land.at[1 + 2 * j + c],
                                         ssem.at[7 * p + 1 + 2 * j + cc], rsem.at[7 * p + 1 + 2 * j + c], (*chip, cc)))
            for s in range(7):
                arrivals.append(_remote(land.at[s], land.at[s], ssem.at[7 * p + s], rsem.at[7 * p + s], (x, y, 1 - c)))
        return sends, arrivals
    return make


def _share_maker(n):
    def make(refs, ssem, rsem):
        x, y, c = _coords()
        sends = [_remote(refs[p].at[c], refs[p].at[c], ssem.at[p], rsem.at[p], (x, y, 1 - c)) for p in range(n)]
        arrivals = [_remote(refs[p].at[1 - c], refs[p].at[1 - c], ssem.at[p], rsem.at[p], (x, y, 1 - c)) for p in range(n)]
        return sends, arrivals
    return make


def split_start(name, make, n_sems, operands, after):
    n = len(operands)

    def body(*refs):
        ssem, rsem, token = refs[n + 1], refs[n + 2], refs[-1]
        for cp in make(refs[:n], ssem, rsem)[0]:
            cp.start()
        token[...] = jnp.zeros_like(token)

    ops = [pltpu.with_memory_space_constraint(a, pltpu.HBM) for a in operands]
    outs = pl.pallas_call(
        body, name=name,
        out_shape=(pltpu.SemaphoreType.DMA((n_sems,)), pltpu.SemaphoreType.DMA((n_sems,)),
                   *[pltpu.HBM(a.shape, a.dtype) for a in ops], jax.ShapeDtypeStruct((8, LANE), f32)),
        in_specs=[HBM] * n + [ANY], out_specs=(SEM, SEM, *[HBM] * n, pl.BlockSpec(memory_space=pltpu.VMEM)),
        input_output_aliases={i: 2 + i for i in range(n)},
        compiler_params=pltpu.CompilerParams(has_side_effects=DATAFLOW),
    )(*ops, after)
    return dict(name=name, make=make, ssem=outs[0], rsem=outs[1], operands=outs[2:2 + n], token=outs[-1][0, 0])


def split_wait(handle, after):
    n = len(handle["operands"])

    def body(*refs):
        sends, arrivals = handle["make"](refs[:n], refs[n], refs[n + 1])
        for cp in sends:
            cp.wait_send()
        for cp in arrivals:
            cp.wait_recv()

    outs = pl.pallas_call(
        body, name=handle["name"].replace("start", "wait"),
        out_shape=tuple(pltpu.HBM(a.shape, a.dtype) for a in handle["operands"]),
        in_specs=[HBM] * n + [SEM, SEM, ANY], out_specs=tuple([HBM] * n),
        input_output_aliases={i: i for i in range(n)},
        compiler_params=pltpu.CompilerParams(has_side_effects=DATAFLOW),
    )(*handle["operands"], handle["ssem"], handle["rsem"], after)
    return list(outs)


def piece_sum(g, recv, kc_arr):
    _, _, rb, cc = g.shape
    tr = min(256, rb)

    def body(kc_ref, g_ref, r_ref, o_ref):
        acc = g_ref[...].astype(f32)
        for s in range(7):
            acc = acc + r_ref[s].astype(f32)
        o_ref[...] = acc

    return pl.pallas_call(
        body, name="piece_sum",
        grid_spec=pltpu.PrefetchScalarGridSpec(
            num_scalar_prefetch=1, grid=(rb // tr,),
            in_specs=[pl.BlockSpec((None, None, tr, cc), lambda r, kc: (kc[0], kc[1], r, 0)),
                      pl.BlockSpec((7, tr, cc), lambda r, kc: (0, r, 0))],
            out_specs=pl.BlockSpec((None, tr, cc), lambda r, kc: (kc[1], r, 0))),
        out_shape=jax.ShapeDtypeStruct((2, rb, cc), f32),
        compiler_params=_params("arbitrary"),
    )(kc_arr, g, recv)


def small_all_reduce(vec):
    def body(v_ref, o_ref, gat, ssem, rsem):
        x, y, c = _coords()
        me = 4 * x + 2 * y + c
        gat[me] = v_ref[...]
        sends = []
        for t in range(1, N_DEVICES):
            peer = (x ^ (t >> 2), y ^ ((t >> 1) & 1), c ^ (t & 1))
            cp = _remote(v_ref, gat.at[me], ssem.at[t - 1], rsem.at[t - 1], peer)
            cp.start()
            sends.append(cp)
        for t in range(1, N_DEVICES):
            peer = (x ^ (t >> 2), y ^ ((t >> 1) & 1), c ^ (t & 1))
            slot = gat.at[4 * peer[0] + 2 * peer[1] + peer[2]]
            _remote(slot, slot, ssem.at[t - 1], rsem.at[t - 1], peer).wait_recv()
        for cp in sends:
            cp.wait_send()
        acc = gat[0]
        for d in range(1, N_DEVICES):
            acc = acc + gat[d]
        o_ref[...] = acc

    return pl.pallas_call(
        body, name="small_all_reduce", out_shape=jax.ShapeDtypeStruct(vec.shape, vec.dtype),
        in_specs=[pl.BlockSpec(memory_space=pltpu.VMEM)], out_specs=pl.BlockSpec(memory_space=pltpu.VMEM),
        scratch_shapes=[pltpu.VMEM((N_DEVICES,) + vec.shape, vec.dtype), pltpu.SemaphoreType.DMA((N_DEVICES - 1,)),
                        pltpu.SemaphoreType.DMA((N_DEVICES - 1,))],
    )(vec)


def _adamw_math(w, g, m, v):
    m_new = ADAM_B1 * m + (1.0 - ADAM_B1) * g
    v_new = ADAM_B2 * v + (1.0 - ADAM_B2) * jnp.square(g)
    m_hat = m_new / (1.0 - ADAM_B1 ** ADAM_STEP)
    v_hat = v_new / (1.0 - ADAM_B2 ** ADAM_STEP)
    delta = -ADAM_LR * (m_hat / (jnp.sqrt(v_hat) + ADAM_EPS) + ADAM_WD * w)
    return delta, m_new, v_new


def adamw_shard(w, g0, g1, m, v):
    depth, rows, cols = w.shape
    half = rows // 2
    tr = min(256, half)
    nr = half // tr

    def body(w_ref, g0_ref, g1_ref, m_ref, v_ref, go_ref, d_ref, nm_ref, nv_ref):
        gv = jnp.where(pl.program_id(0) == 0, g0_ref[...], g1_ref[...])
        go_ref[...] = gv
        d_ref[...], nm_ref[...], nv_ref[...] = _adamw_math(w_ref[...], gv, m_ref[...], v_ref[...])

    spec = pl.BlockSpec((None, tr, cols), lambda l, h, r: (l, h * nr + r, 0))
    g0spec = pl.BlockSpec((None, tr, cols), lambda l, h, r: (jnp.where(l == 0, h, 1), jnp.where(l == 0, r, nr - 1), 0))
    g1spec = pl.BlockSpec((None, tr, cols), lambda l, h, r: (jnp.where(l == 1, h, 0), jnp.where(l == 1, r, 0), 0))
    return pl.pallas_call(
        body, name="adamw_shard", grid=(depth, 2, nr), in_specs=[spec, g0spec, g1spec, spec, spec], out_specs=[spec] * 4,
        out_shape=[jax.ShapeDtypeStruct(w.shape, f32)] * 4,
        compiler_params=_params("arbitrary", "arbitrary", "arbitrary"),
    )(w, g0, g1, m, v)


def adamw_cols(w, g, m, v, tc=34):
    cols, depth, rows = w.shape

    def body(w_ref, g_ref, m_ref, v_ref, d_ref, nm_ref, nv_ref):
        d_ref[...], nm_ref[...], nv_ref[...] = _adamw_math(w_ref[...], g_ref[...], m_ref[...], v_ref[...])

    spec = pl.BlockSpec((tc, depth, rows), lambda i: (i, 0, 0))
    return pl.pallas_call(
        body, name="adamw_cols", grid=(cols // tc,), in_specs=[spec] * 4, out_specs=[spec] * 3,
        out_shape=[jax.ShapeDtypeStruct(w.shape, f32)] * 3,
        compiler_params=_params("arbitrary"),
    )(w, g, m, v)


def adamw_small(ws, gs, ms, vs):
    n = len(ws)

    def body(*refs):
        ins, outs = refs[:4 * n], refs[4 * n:]
        for i in range(n):
            w_ref, g_ref, m_ref, v_ref = (ins[k * n + i] for k in range(4))
            outs[i][...], outs[n + i][...], outs[2 * n + i][...] = _adamw_math(w_ref[...], g_ref[...], m_ref[...], v_ref[...])

    outs = pl.pallas_call(
        body, name="adamw_small", out_shape=[jax.ShapeDtypeStruct(w.shape, f32) for w in ws] * 3,
    )(*ws, *gs, *ms, *vs)
    return outs[:n], outs[n:2 * n], outs[2 * n:]


WEIGHTS = ("mix_norm_g", "w_in", "q_gain", "k_gain", "sinks", "rel_bias", "conv_w", "conv_b", "dt_bias", "a_log", "d_skip",
           "ssm_norm_g", "w_out", "mlp_norm_g", "w_up", "w_down")
BIG = ("w_in", "w_out", "w_up", "w_down")
SMALL = tuple(n for n in WEIGHTS if n not in BIG)
PACK_COLS = 1024
PACK_ROWS = 16


def _pack(named, last=None):
    flat = jnp.concatenate([named[n].reshape(-1) for n in SMALL])
    tail = jnp.zeros((1,), f32) if last is None else last.reshape(1)
    pad = jnp.zeros((PACK_ROWS * PACK_COLS - flat.shape[0] - 1,), f32)
    return jnp.concatenate([flat, pad, tail]).reshape(PACK_ROWS, PACK_COLS)


def _unpack(buf, shapes):
    flat = buf.reshape(-1)
    out, at = {}, 0
    for n in SMALL:
        size = int(np.prod(shapes[n]))
        out[n] = flat[at:at + size].reshape(shapes[n])
        at += size
    return out


class _Exchange:
    GROUPS = {"A": (("w_up", 0), ("w_down", 0)), "B": (("w_in", 1), ("w_out", 1)), "C": (("w_up", 1), ("w_down", 1))}
    ICI_AT = {("post_in", 0): "A", ("mid", 0): "B", ("pre_out", 0): "C"}
    RELAY_AT = {("pre_out", 0): "A", ("pre_mlp", 0): "B", ("mid", 1): "C"}
    LAST = ("mix", 0)
    IN_FLIGHT = 2

    def __init__(self, wts, kc_arr):
        self.wts, self.kc_arr = wts, kc_arr
        self.own = {(n, l): wts[n][l].astype(bf16) for n in BIG for l in range(DEPTH)}
        self.ready, self.ici, self.relay = {}, {}, {}
        self.scatter, self.share, self.reduced = [], [], {}

    def project(self, l, x, g):
        if l > 0:
            w_in = align_w_in(self._get(("w_in", l), x))
            return in_fwd(x, g, w_in), w_in
        proj, w_in, w_out, conv = in_fwd_gather(x, g, self.own["w_in", 0], self.own["w_out", 0], self.wts["conv_w"])
        self.ready["w_out", 0] = w_out
        self.conv = jnp.transpose(conv, (1, 2, 0, 3)).reshape(DEPTH, CONV_WIDTH, D_CONV)
        return proj, w_in

    def conv_w(self, l):
        return self.conv[l]

    def _start_ici(self, g, after):
        srcs = [self.own[p] for p in self.GROUPS[g]]
        lands = [lax.empty((N_CHIPS,) + s.shape, s.dtype) for s in srcs]
        self.ici[g] = split_start("gather%s_ici_start" % g, _gather_maker("ici", len(srcs)), 3 * len(srcs), srcs + lands,
                                  after)
        return self.ici[g]["token"]

    def stage(self, name, after):
        tok = 0.0
        g = self.RELAY_AT.get(name)
        if g is not None:
            n = len(self.GROUPS[g])
            self.relay[g] = split_start("gather%s_relay_start" % g, _gather_maker("relay", n), 4 * n,
                                        split_wait(self.ici[g], after), after)
            tok = self.relay[g]["token"]
        if name in self.ICI_AT:
            tok = tok + self._start_ici(self.ICI_AT[name], after)
        return tok

    def _get(self, piece, after):
        if piece not in self.ready:
            g = [k for k, pieces in self.GROUPS.items() if piece in pieces][0]
            lands = split_wait(self.relay[g], after)[len(self.GROUPS[g]):]
            self.ready.update(zip(self.GROUPS[g], lands))
        return self.ready[piece]

    def w_out(self, l, after):
        return self._get(("w_out", l), after).reshape(D_MODEL, D_MODEL)

    def mlp(self, l, after):
        return self._get(("w_up", l), after), self._get(("w_down", l), after)

    def _view(self, n, g):
        _, rows, cols = self.wts[n].shape
        return g.reshape(N_CHIPS, 2, rows // 2, cols)

    def grads(self, name, arrays, after):
        if name == self.LAST:
            self.held = (name, arrays)
            return 0.0
        return self._scatter(name, arrays, after) + self._advance(after, self.IN_FLIGHT)

    def flush(self, after):
        return self._scatter(*self.held, after) + self._advance(after, self.IN_FLIGHT)

    def _scatter(self, name, arrays, after):
        pieces = [(n, name[1]) for n in arrays]
        views = [self._view(n, g) for n, g in arrays.items()]
        lands = [lax.empty((7,) + v.shape[2:], bf16) for v in views]
        h = split_start("scatter_%s%d_start" % name, _scatter_maker(len(views)), 7 * len(views), views + lands, after)
        self.scatter.append((pieces, h))
        return h["token"]

    def _take_share(self, after):
        pieces, h = self.share.pop(0)
        self.reduced.update(zip(pieces, split_wait(h, after)))

    def _take_scatter(self, after):
        pieces, h = self.scatter.pop(0)
        done = split_wait(h, after)
        views, lands = done[:len(pieces)], done[len(pieces):]
        sums = [piece_sum(v, land, self.kc_arr) for v, land in zip(views, lands)]
        hs = split_start(h["name"].replace("scatter", "share"), _share_maker(len(sums)), len(sums), sums, after)
        self.share.append((pieces, hs))
        return hs["token"]

    def _advance(self, after, newest):
        if self.share:
            self._take_share(after)
        return self._take_scatter(after) if len(self.scatter) > newest else 0.0

    def reduced_grads(self, names, after):
        want = [(n, l) for n in names for l in range(DEPTH)]
        while not all(p in self.reduced for p in want):
            if any(p in pieces for p in want for pieces, _ in self.share):
                self._take_share(after)
            else:
                self._take_scatter(after)
        return {n: [self.reduced[n, l] for l in range(DEPTH)] for n in names}


def kernel(x, mix_norm_g, w_in, q_gain, k_gain, sinks, rel_bias, conv_w, conv_b, dt_bias, a_log, d_skip, ssm_norm_g, w_out, mlp_norm_g, w_up, w_down, loss_target, m_mix_norm_g, m_w_in, m_q_gain, m_k_gain, m_sinks, m_rel_bias, m_conv_w, m_conv_b, m_dt_bias, m_a_log, m_d_skip, m_ssm_norm_g, m_w_out, m_mlp_norm_g, m_w_up, m_w_down, v_mix_norm_g, v_w_in, v_q_gain, v_k_gain, v_sinks, v_rel_bias, v_conv_w, v_conv_b, v_dt_bias, v_a_log, v_d_skip, v_ssm_norm_g, v_w_out, v_mlp_norm_g, v_w_up, v_w_down):
    wts = dict(mix_norm_g=mix_norm_g, w_in=w_in, q_gain=q_gain, k_gain=k_gain, sinks=sinks, rel_bias=rel_bias, conv_w=conv_w,
               conv_b=conv_b, dt_bias=dt_bias, a_log=a_log, d_skip=d_skip, ssm_norm_g=ssm_norm_g, w_out=w_out,
               mlp_norm_g=mlp_norm_g, w_up=w_up, w_down=w_down)
    mom = dict(mix_norm_g=m_mix_norm_g, w_in=m_w_in, q_gain=m_q_gain, k_gain=m_k_gain, sinks=m_sinks, rel_bias=m_rel_bias,
               conv_w=m_conv_w, conv_b=m_conv_b, dt_bias=m_dt_bias, a_log=m_a_log, d_skip=m_d_skip, ssm_norm_g=m_ssm_norm_g,
               w_out=m_w_out, mlp_norm_g=m_mlp_norm_g, w_up=m_w_up, w_down=m_w_down)
    var = dict(mix_norm_g=v_mix_norm_g, w_in=v_w_in, q_gain=v_q_gain, k_gain=v_k_gain, sinks=v_sinks, rel_bias=v_rel_bias,
               conv_w=v_conv_w, conv_b=v_conv_b, dt_bias=v_dt_bias, a_log=v_a_log, d_skip=v_d_skip, ssm_norm_g=v_ssm_norm_g,
               w_out=v_w_out, mlp_norm_g=v_mlp_norm_g, w_up=v_w_up, w_down=v_w_down)
    xi, yi, ci = _coords()
    k_me = 2 * xi + yi
    kc_arr = jnp.stack([k_me, ci]).astype(jnp.int32)

    prov = _Exchange(wts, kc_arr)
    small_w = {n: wts[n] for n in SMALL}
    loss, dx, grads, tok = local_step(x[0], loss_target[0], small_w, prov)

    small_shapes = {n: grads[n].shape for n in SMALL}
    small_sum = small_all_reduce(_pack(grads, loss) + tok)
    loss = small_sum[PACK_ROWS - 1, PACK_COLS - 1]
    tok = prov.flush(small_sum)
    small = _unpack(small_sum, small_shapes)
    cols = conv_w.shape[-1]
    small["conv_w"] = lax.dynamic_slice_in_dim(small["conv_w"], k_me * cols, cols, axis=2)
    g_out_d = dict(small)
    gs = [small[n] for n in SMALL]
    gs[0] = gs[0] + tok
    ds, nms, nvs = adamw_small([wts[n] for n in SMALL], gs, [mom[n] for n in SMALL], [var[n] for n in SMALL])
    d_out_d, m_out_d, v_out_d = dict(zip(SMALL, ds)), dict(zip(SMALL, nms)), dict(zip(SMALL, nvs))

    after = ds[0]
    for names in (("w_up", "w_down"), ("w_in", "w_out")):
        for n, (g0, g1) in prov.reduced_grads(names, after).items():
            if n == "w_in":
                rows, cols = wts[n].shape[1:]
                to_cols = lambda a: jnp.transpose(a, (2, 0, 1))
                g_t = jnp.stack([to_cols(g).reshape(cols, rows) for g in (g0, g1)], axis=1)
                res_t = adamw_cols(to_cols(wts[n]), g_t, to_cols(mom[n]), to_cols(var[n]))
                g_out_d[n], d_out_d[n], m_out_d[n], v_out_d[n] = (jnp.transpose(a, (1, 2, 0)) for a in (g_t, *res_t))
            else:
                g_out_d[n], d_out_d[n], m_out_d[n], v_out_d[n] = adamw_shard(wts[n], g0, g1, mom[n], var[n])
            after = d_out_d[n]

    return (loss, dx[None], *[g_out_d[n] for n in WEIGHTS], *[d_out_d[n] for n in WEIGHTS],
            *[m_out_d[n] for n in WEIGHTS], *[v_out_d[n] for n in WEIGHTS])
```

```python
import numpy as np
import jax
import jax.numpy as jnp
from jax import lax
from jax.experimental import pallas as pl
from jax.experimental.pallas import tpu as pltpu

f32 = jnp.float32
bf16 = jnp.bfloat16

SEQ = 2048
D_MODEL = 1024
DEPTH = 2
HEAD_DIM = 64
N_Q_HEADS = 8
N_KV_HEADS = 2
Q_PER_KV = N_Q_HEADS // N_KV_HEADS
BLOCK = 128
N_BLOCKS = SEQ // BLOCK
N_BUCKETS = 32
MAX_DISTANCE = 128
SSM_HEADS = 8
SSM_HEAD_DIM = 64
SSM_GROUPS = 2
HEADS_PER_GROUP = SSM_HEADS // SSM_GROUPS
SSM_STATE = 128
CONV_WIDTH = 4
CHUNK = 128
N_CHUNKS = SEQ // CHUNK
D_FF = 4 * D_MODEL
D_ATTN = N_Q_HEADS * HEAD_DIM
D_KV = N_KV_HEADS * HEAD_DIM
D_SSM = SSM_HEADS * SSM_HEAD_DIM
D_BC = SSM_GROUPS * SSM_STATE
D_CONV = D_SSM + 2 * D_BC
D_IN = D_ATTN + 2 * D_KV + D_SSM + D_CONV + SSM_HEADS
EPS = 1e-6
NEG = -1e30
N_CHIPS = 4
FF_TILE = D_FF // N_CHIPS

LANE = 128
PW = D_ATTN + D_SSM + D_CONV + 2 * D_KV + LANE
OFF_Q, OFF_Z, OFF_X, OFF_K, OFF_V, OFF_DT = 0, 512, 1024, 2048, 2176, 2304

ADAM_LR = 0.001
ADAM_B1 = 0.9
ADAM_B2 = 0.999
ADAM_EPS = 1e-08
ADAM_WD = 0.01
ADAM_STEP = 10

VMEM_LIMIT = 56 * 1024 * 1024


def _params(*sem):
    return pltpu.CompilerParams(dimension_semantics=tuple(sem), vmem_limit_bytes=VMEM_LIMIT)


def _bdot(a, b):
    return jnp.dot(a.astype(bf16), b.astype(bf16), preferred_element_type=f32)


def _bdot_nt(a, b):
    return lax.dot_general(a.astype(bf16), b.astype(bf16), (((1,), (1,)), ((), ())), preferred_element_type=f32)


def _bdot_tn(a, b):
    return lax.dot_general(a.astype(bf16), b.astype(bf16), (((0,), (0,)), ((), ())), preferred_element_type=f32)


def _hdot(a, b):
    return jnp.dot(a, b, precision=lax.Precision.HIGHEST, preferred_element_type=f32)


def _sigmoid(x):
    return 1.0 / (1.0 + jnp.exp(-x))


def _softplus(x):
    return jnp.maximum(x, 0.0) + jnp.log1p(jnp.exp(-jnp.abs(x)))


def _rms(x):
    return lax.rsqrt(jnp.mean(x * x, axis=-1, keepdims=True) + EPS)


def _rms_bwd(dy, xhat, r, g):
    t = dy * g
    return r * (t - xhat * jnp.mean(t * xhat, axis=-1, keepdims=True))


def _full(shape):
    return pl.BlockSpec(shape, lambda *_: (0,) * len(shape))


def _bucket_table():
    qi = np.arange(BLOCK)[:, None]
    kj = np.arange(2 * BLOCK)[None, :]
    dist = qi + BLOCK - kj
    ok = (dist >= 0) & (dist < 128)
    d = np.clip(dist, 0, None)
    max_exact = N_BUCKETS // 2
    d_f = np.maximum(d, 1).astype(np.float32)
    large = max_exact + (np.log(d_f / np.float32(max_exact)) / np.float32(np.log(MAX_DISTANCE / max_exact))
                         * np.float32(N_BUCKETS - max_exact)).astype(np.int32)
    large = np.minimum(large, N_BUCKETS - 1)
    bucket = np.where(d < max_exact, d, large)
    return np.where(ok, bucket, -1).astype(np.int32)


def bias_build(rel_bias, bucket):
    def body(rel_ref, bkt_ref, o_ref):
        bkt = bkt_ref[...]
        for h in range(N_Q_HEADS):
            acc = jnp.where(bkt < 0, NEG, 0.0).astype(f32)
            for b in range(N_BUCKETS):
                acc = acc + jnp.where(bkt == b, rel_ref[b, h], 0.0)
            o_ref[h] = acc

    return pl.pallas_call(
        body, name="bias_build", out_shape=jax.ShapeDtypeStruct((N_Q_HEADS,) + bucket.shape, f32),
        in_specs=[pl.BlockSpec(memory_space=pltpu.SMEM), pl.BlockSpec(memory_space=pltpu.VMEM)],
        out_specs=pl.BlockSpec(memory_space=pltpu.VMEM),
    )(rel_bias, bucket)


def bias_bwd(dband0, dband1, bucket):
    def body(d0_ref, d1_ref, bkt_ref, o_ref):
        bkt = bkt_ref[...]
        o_ref[...] = jnp.zeros_like(o_ref)
        for h in range(N_Q_HEADS):
            d = d0_ref[h] + d1_ref[h]
            for b in range(N_BUCKETS):
                part = jnp.sum(jnp.where(bkt == b, d, 0.0), axis=1, keepdims=True)
                o_ref[b:b + 1, h:h + 1] = jnp.sum(part, axis=0, keepdims=True)

    return pl.pallas_call(
        body, name="bias_bwd", out_shape=jax.ShapeDtypeStruct((N_BUCKETS, LANE), f32),
    )(dband0, dband1, bucket)


W_IN_SHARD = D_IN // N_CHIPS
_ALIGNED_PIECES = ((0, 0, 512), (1, 190, 578), (2, 0, 124), (2, 124, 578), (3, 0, 570), (0, 512, 578), (1, 0, 62),
                   (1, 62, 190), (3, 570, 578))
_SHARD_PIECES = (((0, 512), (2048, 2114)), ((2114, 2176), (2176, 2304), (512, 900)), ((900, 1024), (1024, 1478)),
                 ((1478, 2048), (2304, 2312)))


def align_w_in(shards, tr=256):
    def body(s_ref, o_ref):
        parts = [s_ref[k, :, a:b] for k, a, b in _ALIGNED_PIECES]
        parts.append(jnp.zeros((tr, LANE - SSM_HEADS), s_ref.dtype))
        o_ref[...] = jnp.concatenate(parts, axis=-1)

    return pl.pallas_call(
        body, name="align_w_in", grid=(D_MODEL // tr,),
        in_specs=[pl.BlockSpec((N_CHIPS, tr, W_IN_SHARD), lambda i: (0, i, 0))],
        out_specs=pl.BlockSpec((tr, PW), lambda i: (i, 0)),
        out_shape=jax.ShapeDtypeStruct((D_MODEL, PW), shards.dtype),
        compiler_params=_params("arbitrary"),
    )(shards)


def split_w_in_grad(dw, tr=256):
    def body(d_ref, o16_ref):
        for k, pieces in enumerate(_SHARD_PIECES):
            o16_ref[k] = jnp.concatenate([d_ref[:, a:b] for a, b in pieces], axis=-1).astype(bf16)

    return pl.pallas_call(
        body, name="split_w_in_grad", grid=(D_MODEL // tr,),
        in_specs=[pl.BlockSpec((tr, PW), lambda i: (i, 0))],
        out_specs=pl.BlockSpec((N_CHIPS, tr, W_IN_SHARD), lambda i: (0, i, 0)),
        out_shape=jax.ShapeDtypeStruct((N_CHIPS, D_MODEL, W_IN_SHARD), bf16),
        compiler_params=_params("arbitrary"),
    )(dw)

def in_fwd(x, g, w, tm=512):
    def body(x_ref, g_ref, w_ref, o_ref):
        xv = x_ref[...]
        h = xv * _rms(xv) * g_ref[...]
        o_ref[...] = _bdot(h, w_ref[...])

    return pl.pallas_call(
        body, name="in_fwd", grid=(SEQ // tm,),
        in_specs=[pl.BlockSpec((tm, D_MODEL), lambda i: (i, 0)), _full((1, D_MODEL)), _resident((D_MODEL, PW))],
        out_specs=pl.BlockSpec((tm, PW), lambda i: (i, 0)),
        out_shape=jax.ShapeDtypeStruct((SEQ, PW), f32),
        compiler_params=_params("arbitrary"),
    )(x, g, w)


def _resident(shape):
    return pl.BlockSpec(shape, lambda *_: (0,) * len(shape), pipeline_mode=pl.Buffered(1))


def in_bwd(dq, dz, dxbc, dk, dv, ddt, x, g, w, dres, tm=512):
    def body(dq_ref, dz_ref, dx_ref, dk_ref, dv_ref, ddt_ref, x_ref, g_ref, w_ref, dres_ref, o_ref, dw_ref, dg_ref):
        i = pl.program_id(0)

        @pl.when(i == 0)
        def _():
            dw_ref[...] = jnp.zeros_like(dw_ref)
            dg_ref[...] = jnp.zeros_like(dg_ref)

        dproj = jnp.concatenate([dq_ref[...], dz_ref[...], dx_ref[...], dk_ref[...], dv_ref[...], ddt_ref[...]],
                                axis=-1).astype(bf16)
        xv = x_ref[...]
        r = _rms(xv)
        xhat = xv * r
        gv = g_ref[...]
        h = xhat * gv
        dw_ref[...] += _bdot_tn(h, dproj)
        dh = _bdot_nt(dproj, w_ref[...])
        dg_ref[...] += jnp.sum(dh * xhat, axis=0, keepdims=True)
        o_ref[...] = dres_ref[...] + _rms_bwd(dh, xhat, r, gv)

    tok = lambda w_: pl.BlockSpec((tm, w_), lambda i: (i, 0))
    return pl.pallas_call(
        body, name="in_bwd", grid=(SEQ // tm,),
        in_specs=[tok(D_ATTN), tok(D_SSM), tok(D_CONV), tok(D_KV), tok(D_KV), tok(LANE), tok(D_MODEL),
                  _full((1, D_MODEL)), _resident((D_MODEL, PW)), tok(D_MODEL)],
        out_specs=[tok(D_MODEL), _resident((D_MODEL, PW)), _full((1, D_MODEL))],
        out_shape=[jax.ShapeDtypeStruct((SEQ, D_MODEL), f32), jax.ShapeDtypeStruct((D_MODEL, PW), f32),
                   jax.ShapeDtypeStruct((1, D_MODEL), f32)],
        compiler_params=_params("arbitrary"),
    )(dq, dz, dxbc, dk, dv, ddt, x, g, w, dres)


def _attn_softmax_t(qk, bias_t, sink, first, key_row):
    s = qk * (HEAD_DIM ** -0.5) + bias_t
    s = jnp.where(jnp.logical_and(first, key_row < BLOCK), NEG, s)
    m = jnp.maximum(jnp.max(s, axis=0, keepdims=True), sink)
    p = jnp.exp(s - m)
    psink = jnp.exp(sink - m)
    inv = 1.0 / (jnp.sum(p, axis=0, keepdims=True) + psink)
    return p * inv, psink * inv


def _rms_t(x_t):
    return lax.rsqrt(jnp.mean(x_t * x_t, axis=0, keepdims=True) + EPS)


def attn_fwd_t(proj, q_gain_col, k_gain, sinks, bias_t):
    kcol, vcol = OFF_K // D_KV, OFF_V // D_KV

    def body(q_ref, kc_ref, kp_ref, vc_ref, vp_ref, qg_ref, kg_ref, sink_ref, bias_ref, o_ref, ot_scr):
        n = pl.program_id(0)
        first = n == 0
        key_row = lax.broadcasted_iota(jnp.int32, (2 * BLOCK, BLOCK), 0)
        k2 = jnp.concatenate([kp_ref[...], kc_ref[...]], axis=0)
        v_t = jnp.concatenate([vp_ref[...], vc_ref[...]], axis=0).T
        q_t = q_ref[...].T
        qg = jnp.broadcast_to(qg_ref[...], (HEAD_DIM, BLOCK))
        kg = kg_ref[...]
        for hk in range(N_KV_HEADS):
            sl = slice(hk * HEAD_DIM, (hk + 1) * HEAD_DIM)
            kk = k2[:, sl]
            kn = (kk * _rms(kk) * kg).astype(bf16)
            vt = v_t[sl, :].astype(bf16)
            heads = range(hk * Q_PER_KV, (hk + 1) * Q_PER_KV)
            qns = []
            for h in heads:
                qh = q_t[h * HEAD_DIM:(h + 1) * HEAD_DIM, :]
                qns.append(qh * _rms_t(qh) * qg)
            scores = [_bdot(kn, qn) for qn in qns]
            for h, s in zip(heads, scores):
                p, _ = _attn_softmax_t(s, bias_ref[h], sink_ref[h], first, key_row)
                ot_scr[h * HEAD_DIM:(h + 1) * HEAD_DIM, :] = _bdot(vt, p)
        o_ref[...] = ot_scr[...].T

    prev = lambda n: jnp.maximum(n - 1, 0)
    return pl.pallas_call(
        body, name="attn_fwd", grid=(N_BLOCKS,),
        in_specs=[pl.BlockSpec((BLOCK, D_ATTN), lambda n: (n, 0)),
                  pl.BlockSpec((BLOCK, D_KV), lambda n: (n, kcol)), pl.BlockSpec((BLOCK, D_KV), lambda n: (prev(n), kcol)),
                  pl.BlockSpec((BLOCK, D_KV), lambda n: (n, vcol)), pl.BlockSpec((BLOCK, D_KV), lambda n: (prev(n), vcol)),
                  _full((HEAD_DIM, 1)), _full((1, HEAD_DIM)), pl.BlockSpec(memory_space=pltpu.SMEM),
                  _full((N_Q_HEADS, 2 * BLOCK, BLOCK))],
        out_specs=pl.BlockSpec((BLOCK, D_ATTN), lambda n: (n, 0)),
        out_shape=jax.ShapeDtypeStruct((SEQ, D_ATTN), f32),
        scratch_shapes=[pltpu.VMEM((D_ATTN, BLOCK), f32)],
        compiler_params=_params("arbitrary"),
    )(proj, proj, proj, proj, proj, q_gain_col, k_gain, sinks, bias_t)


def attn_bwd_t(proj, d_out, q_gain_col, k_gain, sinks, bias_t):
    kcol, vcol = OFF_K // D_KV, OFF_V // D_KV

    def body(q_ref, kc_ref, kp_ref, vc_ref, vp_ref, do_ref, qg_ref, kg_ref, sink_ref, bias_ref,
             dq_ref, dk_ref, dv_ref, dband_ref, dsink_ref, dqg_ref, dkg_ref, dkn_scr, dv_scr, dqt_scr, dsink_acc, dqg_acc):
        i = pl.program_id(0)
        first = i == N_BLOCKS - 1

        @pl.when(i == 0)
        def _():
            for ref in (dband_ref, dkg_ref, dkn_scr, dv_scr, dsink_acc, dqg_acc):
                ref[...] = jnp.zeros_like(ref)

        key_row = lax.broadcasted_iota(jnp.int32, (2 * BLOCK, BLOCK), 0)
        k2 = jnp.concatenate([kp_ref[...], kc_ref[...]], axis=0)
        v2 = jnp.concatenate([vp_ref[...], vc_ref[...]], axis=0)
        q_t = q_ref[...].T
        do_t = do_ref[...].T
        qg = jnp.broadcast_to(qg_ref[...], (HEAD_DIM, BLOCK))
        kg = kg_ref[...]
        scale = HEAD_DIM ** -0.5
        for hk in range(N_KV_HEADS):
            sl = slice(hk * HEAD_DIM, (hk + 1) * HEAD_DIM)
            kk = k2[:, sl]
            rk = _rms(kk)
            khat = kk * rk
            kn = (khat * kg).astype(bf16)
            vb = v2[:, sl].astype(bf16)
            dkn = jnp.zeros((2 * BLOCK, HEAD_DIM), f32)
            dvv = jnp.zeros((2 * BLOCK, HEAD_DIM), f32)
            heads = range(hk * Q_PER_KV, (hk + 1) * Q_PER_KV)
            rqs, qhats, qns, d_os = [], [], [], []
            for h in heads:
                hs = slice(h * HEAD_DIM, (h + 1) * HEAD_DIM)
                qh = q_t[hs, :]
                rqs.append(_rms_t(qh))
                qhats.append(qh * rqs[-1])
                qns.append((qhats[-1] * qg).astype(bf16))
                d_os.append(do_t[hs, :].astype(bf16))
            scores = [_bdot(kn, qn) for qn in qns]
            dps = [_bdot(vb, d_o) for d_o in d_os]
            ps, dss = [], []
            for h, s, dp in zip(heads, scores, dps):
                p, psink = _attn_softmax_t(s, bias_ref[h], sink_ref[h], first, key_row)
                delta = jnp.sum(p * dp, axis=0, keepdims=True)
                ds = p * (dp - delta)
                dband_ref[h] += ds
                dsink_acc[h:h + 1, :] += -(psink * delta)
                ps.append(p.astype(bf16))
                dss.append(ds.astype(bf16))
            dqns = [_bdot_tn(kn, ds) * scale for ds in dss]
            for ds, qn, p, d_o in zip(dss, qns, ps, d_os):
                dkn = dkn + _bdot_nt(ds, qn) * scale
                dvv = dvv + _bdot_nt(p, d_o)
            for h, dqn, rq, qhat in zip(heads, dqns, rqs, qhats):
                dqg_acc[...] += dqn * qhat
                t = dqn * qg
                dqt_scr[h * HEAD_DIM:(h + 1) * HEAD_DIM, :] = rq * (t - qhat * jnp.mean(t * qhat, axis=0, keepdims=True))
            dkn_cur = dkn[BLOCK:] + dkn_scr[:, sl]
            dkn_scr[:, sl] = dkn[:BLOCK]
            khat_c, rk_c = khat[BLOCK:], rk[BLOCK:]
            dkg_ref[...] += jnp.sum(dkn_cur * khat_c, axis=0, keepdims=True)
            dk_ref[:, sl] = _rms_bwd(dkn_cur, khat_c, rk_c, kg)
            dv_ref[:, sl] = dvv[BLOCK:] + dv_scr[:, sl]
            dv_scr[:, sl] = dvv[:BLOCK]
        dq_ref[...] = dqt_scr[...].T

        @pl.when(i == N_BLOCKS - 1)
        def _():
            dsink_ref[...] = jnp.sum(dsink_acc[...], axis=1, keepdims=True)
            dqg_ref[...] = jnp.sum(dqg_acc[...], axis=1, keepdims=True)

    blk = lambda i: N_BLOCKS - 1 - i
    prev = lambda i: jnp.maximum(N_BLOCKS - 2 - i, 0)
    return pl.pallas_call(
        body, name="attn_bwd", grid=(N_BLOCKS,),
        in_specs=[pl.BlockSpec((BLOCK, D_ATTN), lambda i: (blk(i), 0)),
                  pl.BlockSpec((BLOCK, D_KV), lambda i: (blk(i), kcol)), pl.BlockSpec((BLOCK, D_KV), lambda i: (prev(i), kcol)),
                  pl.BlockSpec((BLOCK, D_KV), lambda i: (blk(i), vcol)), pl.BlockSpec((BLOCK, D_KV), lambda i: (prev(i), vcol)),
                  pl.BlockSpec((BLOCK, D_ATTN), lambda i: (blk(i), 0)),
                  _full((HEAD_DIM, 1)), _full((1, HEAD_DIM)), pl.BlockSpec(memory_space=pltpu.SMEM),
                  _full((N_Q_HEADS, 2 * BLOCK, BLOCK))],
        out_specs=[pl.BlockSpec((BLOCK, D_ATTN), lambda i: (blk(i), 0)), pl.BlockSpec((BLOCK, D_KV), lambda i: (blk(i), 0)),
                   pl.BlockSpec((BLOCK, D_KV), lambda i: (blk(i), 0)), _full((N_Q_HEADS, 2 * BLOCK, BLOCK)),
                   _full((N_Q_HEADS, 1)), _full((HEAD_DIM, 1)), _full((1, HEAD_DIM))],
        out_shape=[jax.ShapeDtypeStruct((SEQ, D_ATTN), f32), jax.ShapeDtypeStruct((SEQ, D_KV), f32),
                   jax.ShapeDtypeStruct((SEQ, D_KV), f32), jax.ShapeDtypeStruct((N_Q_HEADS, 2 * BLOCK, BLOCK), f32),
                   jax.ShapeDtypeStruct((N_Q_HEADS, 1), f32), jax.ShapeDtypeStruct((HEAD_DIM, 1), f32),
                   jax.ShapeDtypeStruct((1, HEAD_DIM), f32)],
        scratch_shapes=[pltpu.VMEM((BLOCK, D_KV), f32), pltpu.VMEM((BLOCK, D_KV), f32), pltpu.VMEM((D_ATTN, BLOCK), f32),
                        pltpu.VMEM((N_Q_HEADS, BLOCK), f32), pltpu.VMEM((HEAD_DIM, BLOCK), f32)],
        compiler_params=_params("arbitrary"),
    )(proj, proj, proj, proj, proj, d_out, q_gain_col, k_gain, sinks, bias_t)


SUBLANES = 8


def _shift_down(u, s, row8):
    if s == 0:
        return u
    r = pltpu.roll(u, s, 0)
    return jnp.concatenate([jnp.where(row8 >= s, r[:SUBLANES], 0.0), r[SUBLANES:]], axis=0)


def _shift_up(u, s, row8):
    if s == 0:
        return u
    r = pltpu.roll(u, SEQ - s, 0)
    return jnp.concatenate([r[:-SUBLANES], jnp.where(row8 < SUBLANES - s, r[-SUBLANES:], 0.0)], axis=0)


def conv_fwd(proj, conv_w, conv_b):
    xcol = OFF_X // LANE

    def body(u_ref, w_ref, b_ref, o_ref):
        u = u_ref[...]
        row = lax.broadcasted_iota(jnp.int32, (SUBLANES, LANE), 0)
        pre = b_ref[...] + jnp.zeros_like(u)
        for k in range(CONV_WIDTH):
            pre = pre + w_ref[k:k + 1, :] * _shift_down(u, CONV_WIDTH - 1 - k, row)
        o_ref[...] = pre * _sigmoid(pre)

    return pl.pallas_call(
        body, name="conv_fwd", grid=(D_CONV // LANE,),
        in_specs=[pl.BlockSpec((SEQ, LANE), lambda j: (0, xcol + j)), pl.BlockSpec((CONV_WIDTH, LANE), lambda j: (0, j)),
                  pl.BlockSpec((1, LANE), lambda j: (0, j))],
        out_specs=pl.BlockSpec((SEQ, LANE), lambda j: (0, j)),
        out_shape=jax.ShapeDtypeStruct((SEQ, D_CONV), f32),
        compiler_params=_params("arbitrary"),
    )(proj, conv_w, conv_b)


def conv_bwd(proj, d_act, conv_w, conv_b):
    xcol = OFF_X // LANE

    def body(u_ref, da_ref, w_ref, b_ref, du_ref, dw_ref, db_ref):
        u = u_ref[...]
        row = lax.broadcasted_iota(jnp.int32, (SUBLANES, LANE), 0)
        shifted = [_shift_down(u, CONV_WIDTH - 1 - k, row) for k in range(CONV_WIDTH)]
        pre = b_ref[...] + jnp.zeros_like(u)
        for k in range(CONV_WIDTH):
            pre = pre + w_ref[k:k + 1, :] * shifted[k]
        sg = _sigmoid(pre)
        dpre = da_ref[...] * (sg * (1.0 + pre * (1.0 - sg)))
        db_ref[...] = jnp.sum(dpre, axis=0, keepdims=True)
        du = jnp.zeros_like(u)
        for k in range(CONV_WIDTH):
            dw_ref[k:k + 1, :] = jnp.sum(dpre * shifted[k], axis=0, keepdims=True)
            du = du + w_ref[k:k + 1, :] * _shift_up(dpre, CONV_WIDTH - 1 - k, row)
        du_ref[...] = du

    return pl.pallas_call(
        body, name="conv_bwd", grid=(D_CONV // LANE,),
        in_specs=[pl.BlockSpec((SEQ, LANE), lambda j: (0, xcol + j)), pl.BlockSpec((SEQ, LANE), lambda j: (0, j)),
                  pl.BlockSpec((CONV_WIDTH, LANE), lambda j: (0, j)), pl.BlockSpec((1, LANE), lambda j: (0, j))],
        out_specs=[pl.BlockSpec((SEQ, LANE), lambda j: (0, j)), pl.BlockSpec((CONV_WIDTH, LANE), lambda j: (0, j)),
                   pl.BlockSpec((1, LANE), lambda j: (0, j))],
        out_shape=[jax.ShapeDtypeStruct((SEQ, D_CONV), f32), jax.ShapeDtypeStruct((CONV_WIDTH, D_CONV), f32),
                   jax.ShapeDtypeStruct((1, D_CONV), f32)],
        compiler_params=_params("arbitrary"),
    )(proj, d_act, conv_w, conv_b)


def _ssd_chunk_common(dt_raw, dtb, alog):
    row = lax.broadcasted_iota(jnp.int32, (CHUNK, CHUNK), 0)
    col = lax.broadcasted_iota(jnp.int32, (CHUNK, CHUNK), 1)
    tri = (row >= col).astype(f32)
    strict = (row > col).astype(f32)
    dtp = _softplus(dt_raw + dtb)
    a_row = -jnp.exp(alog)
    d_a = dtp * a_row
    cs = _hdot(tri, d_a)
    cs_last = cs[CHUNK - 1:CHUNK, :]
    return row, col, dtp, a_row, cs, cs.T, cs_last


def _seg_decay(cs, cs_t, hd, row, col):
    seg = cs[:, hd:hd + 1] - cs_t[hd:hd + 1, :]
    return jnp.where(row >= col, jnp.exp(seg), 0.0)


GROUP_W = HEADS_PER_GROUP * SSM_HEAD_DIM


def _group_indicator(g):
    j = lax.broadcasted_iota(jnp.int32, (GROUP_W, LANE), 0)
    lane = lax.broadcasted_iota(jnp.int32, (GROUP_W, LANE), 1)
    return (lane == g * HEADS_PER_GROUP + j // SSM_HEAD_DIM).astype(bf16)


def _bf16_pieces(a, n):
    pieces = []
    for _ in range(n):
        p = a.astype(bf16)
        pieces.append(p)
        a = a - p.astype(f32)
    return pieces


def _head_spread(a, ind):
    return sum(lax.dot_general(p, ind, (((1,), (1,)), ((), ())), preferred_element_type=f32) for p in _bf16_pieces(a, 3))


def _head_sums(a, ind):
    return sum(jnp.dot(p, ind, preferred_element_type=f32) for p in _bf16_pieces(a, 2))


def ssd_fwd_g(act, proj, dt_bias, a_log, d_skip, norm_g):
    zcol, dtcol = OFF_Z // D_SSM, OFF_DT // LANE

    def body(act_ref, z_ref, dt_ref, dtb_ref, alog_ref, dsk_ref, ng_ref, out_ref, ypre_ref, st_ref, state):
        c = pl.program_id(0)

        @pl.when(c == 0)
        def _():
            state[...] = jnp.zeros_like(state)

        row, col, dtp, a_row, cs, cs_t, cs_last = _ssd_chunk_common(dt_ref[...], dtb_ref[...], alog_ref[...])
        e_cs = jnp.exp(cs)
        dte = jnp.exp(cs_last - cs)
        rows8 = jnp.concatenate([jnp.exp(cs_last), dsk_ref[...], jnp.zeros((6, LANE), f32)], axis=0)
        z = z_ref[...]
        sz = z * _sigmoid(z)
        ng = ng_ref[...]
        for g in range(SSM_GROUPS):
            gs = slice(g * GROUP_W, (g + 1) * GROUP_W)
            ind = _group_indicator(g)
            xg = act_ref[:, gs]
            bg = act_ref[:, D_SSM + g * SSM_STATE:D_SSM + (g + 1) * SSM_STATE]
            cg = act_ref[:, D_SSM + D_BC + g * SSM_STATE:D_SSM + D_BC + (g + 1) * SSM_STATE]
            dt_e, e_e, dte_e = _head_spread(dtp, ind), _head_spread(e_cs, ind), _head_spread(dte, ind)
            rows_e = _head_spread(rows8, ind)
            ecl_e, dsk_e = rows_e[0:1], rows_e[1:2]
            xdt = xg * dt_e
            prev = state[g]
            st_ref[0, g] = prev
            cb = _bdot_nt(cg, bg)
            goff = _bdot(cg, prev)
            snew = _bdot_tn(bg, xdt * dte_e)
            heads = range(g * HEADS_PER_GROUP, (g + 1) * HEADS_PER_GROUP)
            ms = [cb * _seg_decay(cs, cs_t, hd, row, col) for hd in heads]
            yd = [_bdot(m, xdt[:, r * SSM_HEAD_DIM:(r + 1) * SSM_HEAD_DIM]) for r, m in enumerate(ms)]
            y = jnp.concatenate(yd, axis=1) + e_e * goff + xg * dsk_e
            state[g] = prev * ecl_e + snew
            ypre_ref[:, gs] = y
            part = y * sz[:, gs]
            out_ref[:, gs] = part * _rms(part) * ng[:, gs]

    return pl.pallas_call(
        body, name="ssd_fwd", grid=(N_CHUNKS,),
        in_specs=[pl.BlockSpec((CHUNK, D_CONV), lambda c: (c, 0)), pl.BlockSpec((CHUNK, D_SSM), lambda c: (c, zcol)),
                  pl.BlockSpec((CHUNK, LANE), lambda c: (c, dtcol)), _full((1, LANE)), _full((1, LANE)), _full((1, LANE)),
                  _full((1, D_SSM))],
        out_specs=[pl.BlockSpec((CHUNK, D_SSM), lambda c: (c, 0)), pl.BlockSpec((CHUNK, D_SSM), lambda c: (c, 0)),
                   pl.BlockSpec((1, SSM_GROUPS, SSM_STATE, GROUP_W), lambda c: (c, 0, 0, 0))],
        out_shape=[jax.ShapeDtypeStruct((SEQ, D_SSM), f32), jax.ShapeDtypeStruct((SEQ, D_SSM), f32),
                   jax.ShapeDtypeStruct((N_CHUNKS, SSM_GROUPS, SSM_STATE, GROUP_W), f32)],
        scratch_shapes=[pltpu.VMEM((SSM_GROUPS, SSM_STATE, GROUP_W), f32)],
        compiler_params=_params("arbitrary"),
    )(act, proj, proj, dt_bias, a_log, d_skip, norm_g)


def ssd_bwd_g(act, proj, ypre, states, d_out, dt_bias, a_log, d_skip, norm_g):
    zcol, dtcol = OFF_Z // D_SSM, OFF_DT // LANE

    def body(act_ref, z_ref, dt_ref, ypre_ref, st_ref, do_ref, dtb_ref, alog_ref, dsk_ref, ng_ref,
             dact_ref, ddt_ref, dz_ref, dng_ref, dpar_ref, dstate):
        i = pl.program_id(0)

        @pl.when(i == 0)
        def _():
            for ref in (dng_ref, dpar_ref, dstate):
                ref[...] = jnp.zeros_like(ref)

        row, col, dtp, a_row, cs, cs_t, cs_last = _ssd_chunk_common(dt_ref[...], dtb_ref[...], alog_ref[...])
        upper = (row <= col).astype(f32)
        lane = lax.broadcasted_iota(jnp.int32, (CHUNK, LANE), 1)
        rowl = lax.broadcasted_iota(jnp.int32, (CHUNK, LANE), 0)
        e_cs = jnp.exp(cs)
        dte = jnp.exp(cs_last - cs)
        ecl = jnp.exp(cs_last)
        rows8 = jnp.concatenate([ecl, dsk_ref[...], jnp.zeros((6, LANE), f32)], axis=0)
        z = z_ref[...]
        sgz = _sigmoid(z)
        sz = z * sgz
        ng = ng_ref[...]
        ddt_mat = jnp.zeros((CHUNK, LANE), f32)
        dcs_mat = jnp.zeros((CHUNK, LANE), f32)
        dcs_t = jnp.zeros((LANE, CHUNK), f32)
        dcsl_row = jnp.zeros((1, LANE), f32)
        dd_row = jnp.zeros((1, LANE), f32)
        for g in range(SSM_GROUPS):
            gs = slice(g * GROUP_W, (g + 1) * GROUP_W)
            bsl = slice(D_SSM + g * SSM_STATE, D_SSM + (g + 1) * SSM_STATE)
            csl = slice(D_SSM + D_BC + g * SSM_STATE, D_SSM + D_BC + (g + 1) * SSM_STATE)
            ind = _group_indicator(g)
            y = ypre_ref[:, gs]
            part = y * sz[:, gs]
            r = _rms(part)
            yhat = part * r
            d_o = do_ref[:, gs]
            dng_ref[:, gs] += jnp.sum(d_o * yhat, axis=0, keepdims=True)
            dyz = _rms_bwd(d_o, yhat, r, ng[:, gs])
            dy = dyz * sz[:, gs]
            dz_ref[:, gs] = dyz * y * (sgz[:, gs] * (1.0 + z[:, gs] * (1.0 - sgz[:, gs])))

            xg = act_ref[:, gs]
            bg = act_ref[:, bsl]
            cg = act_ref[:, csl]
            dt_e, e_e, dte_e = _head_spread(dtp, ind), _head_spread(e_cs, ind), _head_spread(dte, ind)
            rows_e = _head_spread(rows8, ind)
            ecl_e, dsk_e = rows_e[0:1], rows_e[1:2]
            xdt = xg * dt_e
            prev = st_ref[0, g]
            dh = dstate[g]
            heads = range(g * HEADS_PER_GROUP, (g + 1) * HEADS_PER_GROUP)
            hsl = [slice(r_ * SSM_HEAD_DIM, (r_ + 1) * SSM_HEAD_DIM) for r_ in range(HEADS_PER_GROUP)]
            cb = _bdot_nt(cg, bg)
            lms = [_seg_decay(cs, cs_t, hd, row, col) for hd in heads]
            ms = [cb * lm for lm in lms]
            gmat = _bdot(cg, prev)
            dgm = dy * e_e
            dcg = _bdot_nt(dgm, prev)
            dprev = _bdot_tn(cg, dgm)
            dbg = _bdot_nt(xdt * dte_e, dh)
            dw = _bdot(bg, dh)
            dms = [_bdot_nt(dy[:, s_], xdt[:, s_]) for s_ in hsl]
            dxdts = [_bdot_tn(m, dy[:, s_]) for m, s_ in zip(ms, hsl)]
            dxdt = jnp.concatenate(dxdts, axis=1) + dw * dte_e
            dact_ref[:, gs] = dy * dsk_e + dxdt * dt_e
            dstate[g] = dprev + dh * ecl_e
            dcb = jnp.zeros((CHUNK, CHUNK), f32)
            for hd, dm, lm, m in zip(heads, dms, lms, ms):
                dcb = dcb + dm * lm
                dseg = dm * m
                dcs_mat = dcs_mat + jnp.where(lane == hd, jnp.sum(dseg, axis=1, keepdims=True), 0.0)
                dcs_t = jnp.where(row == hd, jnp.sum(dseg, axis=0, keepdims=True), dcs_t)
            dact_ref[:, bsl] = dbg + _bdot_tn(dcb, cg)
            dact_ref[:, csl] = dcg + _bdot(dcb, bg)
            ddte = _head_sums(dw * xdt, ind) * dte
            dcs_mat = dcs_mat + _head_sums(dy * gmat, ind) * e_cs - ddte
            ddt_mat = ddt_mat + _head_sums(dxdt * xg, ind)
            dcsl_row = (dcsl_row + jnp.sum(ddte, axis=0, keepdims=True)
                        + jnp.sum(_head_sums(dh * prev, ind), axis=0, keepdims=True) * ecl)
            dd_row = dd_row + jnp.sum(_head_sums(dy * xg, ind), axis=0, keepdims=True)
        dcs_mat = dcs_mat - dcs_t.T + jnp.where(rowl == CHUNK - 1, dcsl_row, 0.0)
        dda = _hdot(upper, dcs_mat)
        ddt_mat = ddt_mat + dda * a_row
        da_row = jnp.sum(dda * dtp, axis=0, keepdims=True)
        ddt_raw = ddt_mat * _sigmoid(dt_ref[...] + dtb_ref[...])
        ddt_ref[...] = ddt_raw
        dpar_ref[0:1, :] += jnp.sum(ddt_raw, axis=0, keepdims=True)
        dpar_ref[1:2, :] += da_row * a_row
        dpar_ref[2:3, :] += dd_row

    blk = lambda i: N_CHUNKS - 1 - i
    return pl.pallas_call(
        body, name="ssd_bwd", grid=(N_CHUNKS,),
        in_specs=[pl.BlockSpec((CHUNK, D_CONV), lambda i: (blk(i), 0)), pl.BlockSpec((CHUNK, D_SSM), lambda i: (blk(i), zcol)),
                  pl.BlockSpec((CHUNK, LANE), lambda i: (blk(i), dtcol)), pl.BlockSpec((CHUNK, D_SSM), lambda i: (blk(i), 0)),
                  pl.BlockSpec((1, SSM_GROUPS, SSM_STATE, GROUP_W), lambda i: (blk(i), 0, 0, 0)),
                  pl.BlockSpec((CHUNK, D_SSM), lambda i: (blk(i), 0)),
                  _full((1, LANE)), _full((1, LANE)), _full((1, LANE)), _full((1, D_SSM))],
        out_specs=[pl.BlockSpec((CHUNK, D_CONV), lambda i: (blk(i), 0)), pl.BlockSpec((CHUNK, LANE), lambda i: (blk(i), 0)),
                   pl.BlockSpec((CHUNK, D_SSM), lambda i: (blk(i), 0)), _full((1, D_SSM)), _full((8, LANE))],
        out_shape=[jax.ShapeDtypeStruct((SEQ, D_CONV), f32), jax.ShapeDtypeStruct((SEQ, LANE), f32),
                   jax.ShapeDtypeStruct((SEQ, D_SSM), f32), jax.ShapeDtypeStruct((1, D_SSM), f32),
                   jax.ShapeDtypeStruct((8, LANE), f32)],
        scratch_shapes=[pltpu.VMEM((SSM_GROUPS, SSM_STATE, GROUP_W), f32)],
        compiler_params=_params("arbitrary"),
    )(act, proj, proj, ypre, states, d_out, dt_bias, a_log, d_skip, norm_g)


def out_fwd(x, attn, ssm, w_out, tm=512):
    def body(x_ref, a_ref, s_ref, w_ref, o_ref):
        o_ref[...] = x_ref[...] + _bdot(a_ref[...], w_ref[:D_ATTN, :]) + _bdot(s_ref[...], w_ref[D_ATTN:, :])

    tok = lambda w_: pl.BlockSpec((tm, w_), lambda i: (i, 0))
    return pl.pallas_call(
        body, name="out_fwd", grid=(SEQ // tm,),
        in_specs=[tok(D_MODEL), tok(D_ATTN), tok(D_SSM), _full((D_MODEL, D_MODEL))],
        out_specs=tok(D_MODEL), out_shape=jax.ShapeDtypeStruct((SEQ, D_MODEL), f32),
        compiler_params=_params("arbitrary"),
    )(x, attn, ssm, w_out)


def out_bwd(dx1, attn, ssm, w_out, tm=512):
    nt = SEQ // tm

    def body(d_ref, a_ref, s_ref, w_ref, da_ref, ds_ref, dw16_ref, dw_ref):
        i = pl.program_id(0)

        @pl.when(i == 0)
        def _():
            dw_ref[...] = jnp.zeros_like(dw_ref)

        d = d_ref[...].astype(bf16)
        dcat = _bdot_nt(d, w_ref[...])
        da_ref[...] = dcat[:, :D_ATTN]
        ds_ref[...] = dcat[:, D_ATTN:]
        dw_ref[:D_ATTN, :] += _bdot_tn(a_ref[...], d)
        dw_ref[D_ATTN:, :] += _bdot_tn(s_ref[...], d)

        @pl.when(i == nt - 1)
        def _():
            dw16_ref[...] = dw_ref[...].astype(bf16)

    tok = lambda w_: pl.BlockSpec((tm, w_), lambda i: (i, 0))
    return pl.pallas_call(
        body, name="out_bwd", grid=(nt,),
        in_specs=[tok(D_MODEL), tok(D_ATTN), tok(D_SSM), _resident((D_MODEL, D_MODEL))],
        out_specs=[tok(D_ATTN), tok(D_SSM), _resident((D_MODEL, D_MODEL))],
        out_shape=[jax.ShapeDtypeStruct((SEQ, D_ATTN), f32), jax.ShapeDtypeStruct((SEQ, D_SSM), f32),
                   jax.ShapeDtypeStruct((D_MODEL, D_MODEL), bf16)],
        scratch_shapes=[pltpu.VMEM((D_MODEL, D_MODEL), f32)],
        compiler_params=_params("arbitrary"),
    )(dx1, attn, ssm, w_out)


MLP_SUB = 256


def mlp_fwd(x1, g, w_up, w_down, tm=1024):
    def body(x_ref, g_ref, wu_ref, wd_ref, o_ref, u_ref, h_scr):
        j = pl.program_id(1)

        @pl.when(j == 0)
        def _():
            xv = x_ref[...]
            h_scr[...] = (xv * _rms(xv) * g_ref[...]).astype(bf16)
            o_ref[...] = xv

        for r in range(tm // MLP_SUB):
            rows = slice(r * MLP_SUB, (r + 1) * MLP_SUB)
            u = jnp.dot(h_scr[rows, :], wu_ref[...], preferred_element_type=f32)
            u_ref[rows, :] = u
            a = jnp.square(jnp.maximum(u, 0.0))
            o_ref[rows, :] += _bdot(a, wd_ref[...])

    return pl.pallas_call(
        body, name="mlp_fwd", grid=(SEQ // tm, N_CHIPS),
        in_specs=[pl.BlockSpec((tm, D_MODEL), lambda i, j: (i, 0)), _full((1, D_MODEL)),
                  pl.BlockSpec((None, D_MODEL, FF_TILE), lambda i, j: (j, 0, 0)),
                  pl.BlockSpec((None, FF_TILE, D_MODEL), lambda i, j: (j, 0, 0))],
        out_specs=[pl.BlockSpec((tm, D_MODEL), lambda i, j: (i, 0)), pl.BlockSpec((tm, FF_TILE), lambda i, j: (i, j))],
        out_shape=[jax.ShapeDtypeStruct((SEQ, D_MODEL), f32), jax.ShapeDtypeStruct((SEQ, D_FF), f32)],
        scratch_shapes=[pltpu.VMEM((tm, D_MODEL), bf16)],
        compiler_params=_params("arbitrary", "arbitrary"),
    )(x1, g, w_up, w_down)


def mlp_bwd_data(dx2, u, x1, g, w_up, w_down, tm=1024):
    def body(d_ref, u_ref, x_ref, g_ref, wu_ref, wd_ref, dx_ref, du_ref, dg_ref, dh_scr):
        i, j = pl.program_id(0), pl.program_id(1)

        @pl.when(jnp.logical_and(i == 0, j == 0))
        def _():
            dg_ref[...] = jnp.zeros_like(dg_ref)

        @pl.when(j == 0)
        def _():
            dh_scr[...] = jnp.zeros_like(dh_scr)

        for r in range(tm // MLP_SUB):
            rows = slice(r * MLP_SUB, (r + 1) * MLP_SUB)
            da = _bdot_nt(d_ref[rows, :], wd_ref[...])
            du = (da * (2.0 * jnp.maximum(u_ref[rows, :], 0.0))).astype(bf16)
            du_ref[rows, :] = du
            dh_scr[rows, :] += _bdot_nt(du, wu_ref[...])

        @pl.when(j == N_CHIPS - 1)
        def _():
            xv = x_ref[...]
            r = _rms(xv)
            xhat = xv * r
            dh = dh_scr[...]
            dg_ref[...] += jnp.sum(dh * xhat, axis=0, keepdims=True)
            dx_ref[...] = d_ref[...] + _rms_bwd(dh, xhat, r, g_ref[...])

    return pl.pallas_call(
        body, name="mlp_bwd_data", grid=(SEQ // tm, N_CHIPS),
        in_specs=[pl.BlockSpec((tm, D_MODEL), lambda i, j: (i, 0)), pl.BlockSpec((tm, FF_TILE), lambda i, j: (i, j)),
                  pl.BlockSpec((tm, D_MODEL), lambda i, j: (i, 0)), _full((1, D_MODEL)),
                  pl.BlockSpec((None, D_MODEL, FF_TILE), lambda i, j: (j, 0, 0)),
                  pl.BlockSpec((None, FF_TILE, D_MODEL), lambda i, j: (j, 0, 0))],
        out_specs=[pl.BlockSpec((tm, D_MODEL), lambda i, j: (i, 0)), pl.BlockSpec((tm, FF_TILE), lambda i, j: (i, j)),
                   _full((1, D_MODEL))],
        out_shape=[jax.ShapeDtypeStruct((SEQ, D_MODEL), f32), jax.ShapeDtypeStruct((SEQ, D_FF), bf16),
                   jax.ShapeDtypeStruct((1, D_MODEL), f32)],
        scratch_shapes=[pltpu.VMEM((tm, D_MODEL), f32)],
        compiler_params=_params("arbitrary", "arbitrary"),
    )(dx2, u, x1, g, w_up, w_down)


def mlp_bwd_weights(dx2, u, du, x1, g, tm=512):
    nt = SEQ // tm

    def body(d_ref, u_ref, du_ref, x_ref, g_ref, dwu16_ref, dwd16_ref, h_scr, d_scr, dwu_ref, dwd_ref):
        j, i = pl.program_id(0), pl.program_id(1)

        @pl.when(j == 0)
        def _():
            xv = x_ref[...]
            h_scr[i] = (xv * _rms(xv) * g_ref[...]).T.astype(bf16)
            d_scr[i] = d_ref[...].astype(bf16)

        @pl.when(i == 0)
        def _():
            dwu_ref[...] = jnp.zeros_like(dwu_ref)
            dwd_ref[...] = jnp.zeros_like(dwd_ref)

        dwu_ref[...] += jnp.dot(h_scr[i], du_ref[...], preferred_element_type=f32)
        a = jnp.square(jnp.maximum(u_ref[...], 0.0))
        dwd_ref[...] += _bdot_tn(a, d_scr[i])

        @pl.when(i == nt - 1)
        def _():
            dwu16_ref[...] = dwu_ref[...].astype(bf16)
            dwd16_ref[...] = dwd_ref[...].astype(bf16)

    up = pl.BlockSpec((None, D_MODEL, FF_TILE), lambda j, i: (j, 0, 0))
    down = pl.BlockSpec((None, FF_TILE, D_MODEL), lambda j, i: (j, 0, 0))
    first_pass = pl.BlockSpec((tm, D_MODEL), lambda j, i: (jnp.where(j == 0, i, nt - 1), 0))
    return pl.pallas_call(
        body, name="mlp_bwd_weights", grid=(N_CHIPS, nt),
        in_specs=[first_pass, pl.BlockSpec((tm, FF_TILE), lambda j, i: (i, j)),
                  pl.BlockSpec((tm, FF_TILE), lambda j, i: (i, j)), first_pass, _full((1, D_MODEL))],
        out_specs=[up, down],
        out_shape=[jax.ShapeDtypeStruct((N_CHIPS, D_MODEL, FF_TILE), bf16), jax.ShapeDtypeStruct((N_CHIPS, FF_TILE, D_MODEL), bf16)],
        scratch_shapes=[pltpu.VMEM((nt, D_MODEL, tm), bf16), pltpu.VMEM((nt, tm, D_MODEL), bf16),
                        pltpu.VMEM((D_MODEL, FF_TILE), f32), pltpu.VMEM((FF_TILE, D_MODEL), f32)],
        compiler_params=_params("arbitrary", "arbitrary"),
    )(dx2, u, du, x1, g)


def loss_head(y, target, tm=512):
    def body(y_ref, t_ref, dy_ref, l_ref):
        @pl.when(pl.program_id(0) == 0)
        def _():
            l_ref[...] = jnp.zeros_like(l_ref)

        d = y_ref[...] - t_ref[...]
        dy_ref[...] = d * (1.0 / D_MODEL)
        part = jnp.sum(jnp.mean(d * d, axis=-1, keepdims=True), axis=0, keepdims=True)
        l_ref[...] += 0.5 * part

    tok = pl.BlockSpec((tm, D_MODEL), lambda i: (i, 0))
    return pl.pallas_call(
        body, name="loss_head", grid=(SEQ // tm,), in_specs=[tok, tok], out_specs=[tok, _full((1, 1))],
        out_shape=[jax.ShapeDtypeStruct((SEQ, D_MODEL), f32), jax.ShapeDtypeStruct((1, 1), f32)],
        compiler_params=_params("arbitrary"),
    )(y, target)


def _pad_lane(v):
    return jnp.pad(v, (0, LANE - v.shape[0]))[None, :]


def local_step(x, target, w, prov):
    bucket = jnp.asarray(_bucket_table().T)
    bias = bias_build(w["rel_bias"], bucket)
    saved = []
    for l in range(DEPTH):
        g_mix = w["mix_norm_g"][l][None, :] + prov.stage(("begin", l), x)
        w_in = prov.w_in(l, x)
        proj = in_fwd(x, g_mix, w_in)
        conv_b = w["conv_b"][l][None, :]
        act = conv_fwd(proj, w["conv_w"][l], conv_b)
        dtb = _pad_lane(w["dt_bias"][l]) + prov.stage(("mid", l), act)
        alog, dsk = _pad_lane(w["a_log"][l]), _pad_lane(w["d_skip"][l])
        ng = w["ssm_norm_g"][l][None, :]
        ssm, ypre, states = ssd_fwd_g(act, proj, dtb, alog, dsk, ng)
        qg, kg = w["q_gain"][l][:, None] + 0.0 * ssm[:1, :1], w["k_gain"][l][None, :]
        attn = attn_fwd_t(proj, qg, kg, w["sinks"][l], bias)
        tok = prov.stage(("pre_out", l), attn)
        w_out = prov.w_out(l, attn) + jnp.asarray(tok, bf16)
        x1 = out_fwd(x, attn, ssm, w_out)
        g_mlp = w["mlp_norm_g"][l][None, :] + prov.stage(("pre_mlp", l), x1)
        w_up, w_down = prov.mlp(l, x1)
        x2, u = mlp_fwd(x1, g_mlp, w_up, w_down)
        saved.append(dict(x=x, proj=proj, attn=attn, act=act, ssm=ssm, ypre=ypre, states=states, x1=x1, u=u,
                          g_mix=g_mix, qg=qg, kg=kg, conv_b=conv_b, dtb=dtb, alog=alog, dsk=dsk, ng=ng, g_mlp=g_mlp,
                          w_in=w_in, w_out=w_out, w_up=w_up, w_down=w_down))
        x = x2
    dx, loss = loss_head(x, target)
    grads = [None] * DEPTH
    dbands = [None] * DEPTH
    tok = 0.0
    for l in reversed(range(DEPTH)):
        s = saved[l]
        g_mlp = s["g_mlp"] + tok
        dx1, du, dg_mlp = mlp_bwd_data(dx, s["u"], s["x1"], g_mlp, s["w_up"], s["w_down"])
        dw_up, dw_down = mlp_bwd_weights(dx, s["u"], du, s["x1"], g_mlp)
        tok = prov.grads(("mlp", l), dict(w_up=dw_up, w_down=dw_down), dx1)
        dattn, dssm, dw_out = out_bwd(dx1, s["attn"], s["ssm"], s["w_out"])
        dact, ddt, dz, dng, dpar = ssd_bwd_g(s["act"], s["proj"], s["ypre"], s["states"], dssm, s["dtb"] + tok, s["alog"],
                                           s["dsk"], s["ng"])
        conv_b = s["conv_b"] + prov.stage(("bwd_mid", l), dact)
        dxbc, dconv_w, dconv_b = conv_bwd(s["proj"], dact, w["conv_w"][l], conv_b)
        dq, dk, dv, dband, dsink, dqg, dkg = attn_bwd_t(s["proj"], dattn, s["qg"], s["kg"], w["sinks"][l], bias)
        dbands[l] = dband
        g_mix = s["g_mix"]
        if l == 0:
            d_rel = bias_bwd(dbands[0], dbands[1], bucket)
            g_mix = g_mix + 0.0 * d_rel[:1, :1]
        dx, dw_in, dg_mix = in_bwd(dq, dz, dxbc, dk, dv, ddt, s["x"], g_mix, s["w_in"], dx1)
        tok = prov.grads(("mix", l), dict(w_in=split_w_in_grad(dw_in), w_out=dw_out), dx)
        grads[l] = dict(mix_norm_g=dg_mix[0], q_gain=dqg[:, 0], k_gain=dkg[0], sinks=dsink[:, 0],
                        conv_w=dconv_w, conv_b=dconv_b[0], dt_bias=dpar[0, :SSM_HEADS], a_log=dpar[1, :SSM_HEADS],
                        d_skip=dpar[2, :SSM_HEADS], ssm_norm_g=dng[0], mlp_norm_g=dg_mlp[0])
    out = {k: jnp.stack([grads[l][k] for l in range(DEPTH)]) for k in grads[0]}
    out["rel_bias"] = d_rel[:, :N_Q_HEADS]
    return loss, dx, out, tok


MESH = pl.DeviceIdType.MESH
HBM = pl.BlockSpec(memory_space=pltpu.HBM)
N_DEVICES = 8


def _coords():
    return lax.axis_index("x"), lax.axis_index("y"), lax.axis_index("c")


def _peer_chips(x, y):
    return [(1 - x, y), (x, 1 - y), (1 - x, 1 - y)]


def _remote(src, dst, send_sem, recv_sem, device):
    return pltpu.make_async_remote_copy(src_ref=src, dst_ref=dst, send_sem=send_sem, recv_sem=recv_sem,
                                        device_id=device, device_id_type=MESH)


SEM = pl.BlockSpec(memory_space=pltpu.SEMAPHORE)
ANY = pl.BlockSpec(memory_space=pl.ANY)
DATAFLOW = pltpu.SideEffectType.DATAFLOW_SIDE_EFFECTING


def _gather_copies(kind, src_refs, land_refs, ssem, rsem):
    x, y, c = _coords()
    k_me = 2 * x + y
    n = len(land_refs)
    cps = []
    for p, land in enumerate(land_refs):
        hr = land.shape[1] // 2
        rows = pl.ds(c * hr, hr)
        for j, chip in enumerate(_peer_chips(x, y)):
            i = 3 * p + j
            if kind == "ici":
                cps.append(_remote(src_refs[p].at[rows, :], land.at[k_me, rows, :], ssem.at[i], rsem.at[i], (*chip, c)))
            else:
                got = land.at[2 * chip[0] + chip[1], rows, :]
                cps.append(_remote(got, got, ssem.at[i], rsem.at[i], (x, y, 1 - c)))
        if kind == "relay":
            cps.append(_remote(src_refs[p], land.at[k_me], ssem.at[3 * n + p], rsem.at[3 * n + p], (x, y, 1 - c)))
    return cps


def gather_now(srcs, conv):
    n = len(srcs)

    def body(*refs):
        src_refs, conv_ref = refs[:n], refs[n]
        lands, gconv = refs[n + 1:2 * n + 1], refs[2 * n + 1]
        ssem, rsem, fsem, frsem, csem, crsem = refs[2 * n + 2:]
        x, y, c = _coords()
        k_me = 2 * x + y
        targets = [(*chip, c) for chip in _peer_chips(x, y)] + [(x, y, 1 - c)]
        ici = _gather_copies("ici", src_refs, lands, ssem, rsem)
        relay = _gather_copies("relay", src_refs, lands, fsem, frsem)
        passed = [cp for i, cp in enumerate(relay) if i % 4 != 3]
        own = relay[3::4]
        conv_cps = [_remote(conv_ref, gconv.at[k_me], csem.at[j], crsem.at[j], t) for j, t in enumerate(targets)]
        for cp in ici + conv_cps + own:
            cp.start()
        for cp, fw in zip(ici, passed):
            cp.wait_recv()
            fw.start()
        for cp in conv_cps + relay:
            cp.wait_recv()
        for cp in ici + relay + conv_cps:
            cp.wait_send()

    out_shape = [jax.ShapeDtypeStruct((N_CHIPS,) + s.shape, s.dtype) for s in srcs]
    out_shape.append(jax.ShapeDtypeStruct((N_CHIPS,) + conv.shape, conv.dtype))
    sems = lambda k: pltpu.SemaphoreType.DMA((k,))
    return pl.pallas_call(
        body, name="gather_now", out_shape=out_shape, in_specs=[HBM] * (n + 1), out_specs=[HBM] * (n + 1),
        scratch_shapes=[sems(3 * n), sems(3 * n), sems(4 * n), sems(4 * n), sems(N_CHIPS), sems(N_CHIPS)],
    )(*srcs, conv)


def _gather_maker(kind, n_src):
    def make(refs, ssem, rsem):
        cps = _gather_copies(kind, refs[:n_src], refs[n_src:], ssem, rsem)
        return cps, cps
    return make


def _scatter_maker(n):
    def make(refs, ssem, rsem):
        x, y, c = _coords()
        k_me = 2 * x + y
        sends, arrivals = [], []
        for p in range(n):
            src, land = refs[p], refs[n + p]
            sends.append(_remote(src.at[k_me, 1 - c], land.at[0], ssem.at[7 * p], rsem.at[7 * p], (x, y, 1 - c)))
            for j, chip in enumerate(_peer_chips(x, y)):
                for cc in range(2):
                    sends.append(_remote(src.at[2 * chip[0] + chip[1], cc], land.at[1 + 2 * j + c],
                                         ssem.at[7 * p + 1 + 2 * j + cc], rsem.at[7 * p + 1 + 2 * j + c], (*chip, cc)))
            for s in range(7):
                arrivals.append(_remote(land.at[s], land.at[s], ssem.at[7 * p + s], rsem.at[7 * p + s], (x, y, 1 - c)))
        return sends, arrivals
    return make


def _share_maker(n):
    def make(refs, ssem, rsem):
        x, y, c = _coords()
        sends = [_remote(refs[p].at[c], refs[p].at[c], ssem.at[p], rsem.at[p], (x, y, 1 - c)) for p in range(n)]
        arrivals = [_remote(refs[p].at[1 - c], refs[p].at[1 - c], ssem.at[p], rsem.at[p], (x, y, 1 - c)) for p in range(n)]
        return sends, arrivals
    return make


def split_start(name, make, n_sems, operands, after):
    n = len(operands)

    def body(*refs):
        ssem, rsem, token = refs[n + 1], refs[n + 2], refs[-1]
        for cp in make(refs[:n], ssem, rsem)[0]:
            cp.start()
        token[...] = jnp.zeros_like(token)

    ops = [pltpu.with_memory_space_constraint(a, pltpu.HBM) for a in operands]
    outs = pl.pallas_call(
        body, name=name,
        out_shape=(pltpu.SemaphoreType.DMA((n_sems,)), pltpu.SemaphoreType.DMA((n_sems,)),
                   *[pltpu.HBM(a.shape, a.dtype) for a in ops], jax.ShapeDtypeStruct((8, LANE), f32)),
        in_specs=[HBM] * n + [ANY], out_specs=(SEM, SEM, *[HBM] * n, pl.BlockSpec(memory_space=pltpu.VMEM)),
        input_output_aliases={i: 2 + i for i in range(n)},
        compiler_params=pltpu.CompilerParams(has_side_effects=DATAFLOW),
    )(*ops, after)
    return dict(name=name, make=make, ssem=outs[0], rsem=outs[1], operands=outs[2:2 + n], token=outs[-1][0, 0],
                tokens=outs[-1])


def split_wait(handle, after):
    n = len(handle["operands"])

    def body(*refs):
        sends, arrivals = handle["make"](refs[:n], refs[n], refs[n + 1])
        for cp in sends:
            cp.wait_send()
        for cp in arrivals:
            cp.wait_recv()

    outs = pl.pallas_call(
        body, name=handle["name"].replace("start", "wait"),
        out_shape=tuple(pltpu.HBM(a.shape, a.dtype) for a in handle["operands"]),
        in_specs=[HBM] * n + [SEM, SEM, ANY], out_specs=tuple([HBM] * n),
        input_output_aliases={i: i for i in range(n)},
        compiler_params=pltpu.CompilerParams(has_side_effects=DATAFLOW),
    )(*handle["operands"], handle["ssem"], handle["rsem"], after)
    return list(outs)


def piece_sum(g, recv, kc_arr):
    _, _, rb, cc = g.shape
    tr = min(256, rb)

    def body(kc_ref, g_ref, r_ref, o_ref):
        acc = g_ref[...].astype(f32)
        for s in range(7):
            acc = acc + r_ref[s].astype(f32)
        o_ref[...] = acc

    return pl.pallas_call(
        body, name="piece_sum",
        grid_spec=pltpu.PrefetchScalarGridSpec(
            num_scalar_prefetch=1, grid=(rb // tr,),
            in_specs=[pl.BlockSpec((None, None, tr, cc), lambda r, kc: (kc[0], kc[1], r, 0)),
                      pl.BlockSpec((7, tr, cc), lambda r, kc: (0, r, 0))],
            out_specs=pl.BlockSpec((None, tr, cc), lambda r, kc: (kc[1], r, 0))),
        out_shape=jax.ShapeDtypeStruct((2, rb, cc), f32),
        compiler_params=_params("arbitrary"),
    )(kc_arr, g, recv)


def _spread_maker():
    def make(refs, ssem, rsem):
        vec, land = refs
        x, y, c = _coords()
        me = 4 * x + 2 * y + c
        sends, arrivals = [], []
        for t in range(1, N_DEVICES):
            peer = (x ^ (t >> 2), y ^ ((t >> 1) & 1), c ^ (t & 1))
            sends.append(_remote(vec, land.at[me], ssem.at[t - 1], rsem.at[t - 1], peer))
            slot = land.at[4 * peer[0] + 2 * peer[1] + peer[2]]
            arrivals.append(_remote(slot, slot, ssem.at[t - 1], rsem.at[t - 1], peer))
        return sends, arrivals
    return make


def device_sum(vec, parts, kc_arr):
    def body(kc_ref, v_ref, p_ref, o_ref):
        me = 2 * kc_ref[0] + kc_ref[1]
        acc = jnp.where(me == 0, v_ref[...], p_ref[0])
        for d in range(1, N_DEVICES):
            acc = acc + jnp.where(me == d, v_ref[...], p_ref[d])
        o_ref[...] = acc

    vmem = pl.BlockSpec(memory_space=pltpu.VMEM)
    return pl.pallas_call(
        body, name="device_sum", out_shape=jax.ShapeDtypeStruct(vec.shape, vec.dtype),
        in_specs=[pl.BlockSpec(memory_space=pltpu.SMEM), vmem, vmem], out_specs=vmem,
    )(kc_arr, vec, parts)


def _adamw_math(w, g, m, v):
    m_new = ADAM_B1 * m + (1.0 - ADAM_B1) * g
    v_new = ADAM_B2 * v + (1.0 - ADAM_B2) * jnp.square(g)
    m_hat = m_new / (1.0 - ADAM_B1 ** ADAM_STEP)
    v_hat = v_new / (1.0 - ADAM_B2 ** ADAM_STEP)
    delta = -ADAM_LR * (m_hat / (jnp.sqrt(v_hat) + ADAM_EPS) + ADAM_WD * w)
    return delta, m_new, v_new


def adamw_layer(l, w, g, m, v, after, prev=None):
    depth, rows, cols = w.shape
    half = rows // 2
    tr = min(256, half)
    nr = half // tr
    carried = [] if prev is None else list(prev)

    def body(w_ref, g_ref, m_ref, v_ref, *rest):
        go_ref, d_ref, nm_ref, nv_ref, tok_ref = rest[-5:]
        gv = g_ref[...]
        go_ref[...] = gv
        d_ref[...], nm_ref[...], nv_ref[...] = _adamw_math(w_ref[...], gv, m_ref[...], v_ref[...])
        tok_ref[...] = jnp.zeros_like(tok_ref)

    spec = pl.BlockSpec((None, tr, cols), lambda h, r: (l, h * nr + r, 0))
    gspec = pl.BlockSpec((None, tr, cols), lambda h, r: (h, r, 0))
    return pl.pallas_call(
        body, name="adamw_layer%d" % l, grid=(2, nr), in_specs=[spec, gspec, spec, spec, ANY] + [ANY] * len(carried),
        out_specs=[spec] * 4 + [pl.BlockSpec((8, LANE), lambda h, r: (0, 0))],
        out_shape=[jax.ShapeDtypeStruct(w.shape, f32)] * 4 + [jax.ShapeDtypeStruct((8, LANE), f32)],
        input_output_aliases={5 + i: i for i in range(len(carried))},
        compiler_params=_params("arbitrary", "arbitrary"),
    )(w, g, m, v, after, *carried)


def adamw_cols(w, g, m, v, tc=34):
    cols, depth, rows = w.shape

    def body(w_ref, g_ref, m_ref, v_ref, d_ref, nm_ref, nv_ref):
        d_ref[...], nm_ref[...], nv_ref[...] = _adamw_math(w_ref[...], g_ref[...], m_ref[...], v_ref[...])

    spec = pl.BlockSpec((tc, depth, rows), lambda i: (i, 0, 0))
    return pl.pallas_call(
        body, name="adamw_cols", grid=(cols // tc,), in_specs=[spec] * 4, out_specs=[spec] * 3,
        out_shape=[jax.ShapeDtypeStruct(w.shape, f32)] * 3,
        compiler_params=_params("arbitrary"),
    )(w, g, m, v)


def adamw_small(ws, gs, ms, vs):
    n = len(ws)

    def body(*refs):
        ins, outs = refs[:4 * n], refs[4 * n:]
        for i in range(n):
            w_ref, g_ref, m_ref, v_ref = (ins[k * n + i] for k in range(4))
            outs[i][...], outs[n + i][...], outs[2 * n + i][...] = _adamw_math(w_ref[...], g_ref[...], m_ref[...], v_ref[...])

    outs = pl.pallas_call(
        body, name="adamw_small", out_shape=[jax.ShapeDtypeStruct(w.shape, f32) for w in ws] * 3,
    )(*ws, *gs, *ms, *vs)
    return outs[:n], outs[n:2 * n], outs[2 * n:]


WEIGHTS = ("mix_norm_g", "w_in", "q_gain", "k_gain", "sinks", "rel_bias", "conv_w", "conv_b", "dt_bias", "a_log", "d_skip",
           "ssm_norm_g", "w_out", "mlp_norm_g", "w_up", "w_down")
BIG = ("w_in", "w_out", "w_up", "w_down")
SMALL = tuple(n for n in WEIGHTS if n not in BIG)
PACK_COLS = 1024
PACK_ROWS = 16


def _pack(named, last=None):
    flat = jnp.concatenate([named[n].reshape(-1) for n in SMALL])
    tail = jnp.zeros((1,), f32) if last is None else last.reshape(1)
    pad = jnp.zeros((PACK_ROWS * PACK_COLS - flat.shape[0] - 1,), f32)
    return jnp.concatenate([flat, pad, tail]).reshape(PACK_ROWS, PACK_COLS)


def _unpack(buf, shapes):
    flat = buf.reshape(-1)
    out, at = {}, 0
    for n in SMALL:
        size = int(np.prod(shapes[n]))
        out[n] = flat[at:at + size].reshape(shapes[n])
        at += size
    return out


class _Exchange:
    GROUPS = {"A": (("w_up", 0), ("w_down", 0)), "B": (("w_in", 1), ("w_out", 1)), "C": (("w_up", 1), ("w_down", 1))}
    ICI_AT = {("mid", 0): "B", ("pre_out", 0): "C"}
    RELAY_AT = {("pre_out", 0): "A", ("pre_mlp", 0): "B", ("mid", 1): "C"}
    LAST = ("mix", 0)
    IN_FLIGHT = 2

    def __init__(self, wts, kc_arr):
        self.wts, self.kc_arr = wts, kc_arr
        self.own = {(n, l): wts[n][l].astype(bf16) for n in BIG for l in range(DEPTH)}
        now = gather_now([self.own["w_in", 0], self.own["w_out", 0]], wts["conv_w"])
        self.ready = {("w_in", 0): now[0], ("w_out", 0): now[1]}
        self.conv_w = jnp.transpose(now[2], (1, 2, 0, 3)).reshape(DEPTH, CONV_WIDTH, D_CONV)
        self.ici, self.relay = {}, {}
        self.scatter, self.share, self.reduced = [], [], {}
        self._start_ici("A", now[2])

    def _start_ici(self, g, after):
        srcs = [self.own[p] for p in self.GROUPS[g]]
        lands = [lax.empty((N_CHIPS,) + s.shape, s.dtype) for s in srcs]
        self.ici[g] = split_start("gather%s_ici_start" % g, _gather_maker("ici", len(srcs)), 3 * len(srcs), srcs + lands,
                                  after)
        return self.ici[g]["token"]

    def stage(self, name, after):
        if name == ("begin", 0):
            return self.ici["A"]["token"]
        tok = 0.0
        g = self.RELAY_AT.get(name)
        if g is not None:
            n = len(self.GROUPS[g])
            self.relay[g] = split_start("gather%s_relay_start" % g, _gather_maker("relay", n), 4 * n,
                                        split_wait(self.ici[g], after), after)
            tok = self.relay[g]["token"]
        if name in self.ICI_AT:
            tok = tok + self._start_ici(self.ICI_AT[name], after)
        return tok

    def _get(self, piece, after):
        if piece not in self.ready:
            g = [k for k, pieces in self.GROUPS.items() if piece in pieces][0]
            lands = split_wait(self.relay[g], after)[len(self.GROUPS[g]):]
            self.ready.update(zip(self.GROUPS[g], lands))
        return self.ready[piece]

    def w_in(self, l, after):
        return align_w_in(self._get(("w_in", l), after))

    def w_out(self, l, after):
        return self._get(("w_out", l), after).reshape(D_MODEL, D_MODEL)

    def mlp(self, l, after):
        return self._get(("w_up", l), after), self._get(("w_down", l), after)

    def _view(self, n, g):
        _, rows, cols = self.wts[n].shape
        return g.reshape(N_CHIPS, 2, rows // 2, cols)

    def grads(self, name, arrays, after):
        if name == self.LAST:
            self.held = (name, arrays)
            return 0.0
        return self._scatter(name, arrays, after) + self._advance(after, self.IN_FLIGHT)

    def flush(self, after):
        return self._scatter(*self.held, after) + self._advance(after, self.IN_FLIGHT)

    def _scatter(self, name, arrays, after):
        pieces = [(n, name[1]) for n in arrays]
        views = [self._view(n, g) for n, g in arrays.items()]
        lands = [lax.empty((7,) + v.shape[2:], bf16) for v in views]
        h = split_start("scatter_%s%d_start" % name, _scatter_maker(len(views)), 7 * len(views), views + lands, after)
        self.scatter.append((pieces, h))
        return h["token"]

    def _take_share(self, after):
        pieces, h = self.share.pop(0)
        self.reduced.update(zip(pieces, split_wait(h, after)))

    def _take_scatter(self, after):
        pieces, h = self.scatter.pop(0)
        done = split_wait(h, after)
        views, lands = done[:len(pieces)], done[len(pieces):]
        sums = [piece_sum(v, land, self.kc_arr) for v, land in zip(views, lands)]
        hs = split_start(h["name"].replace("scatter", "share"), _share_maker(len(sums)), len(sums), sums, after)
        self.share.append((pieces, hs))
        return hs["token"]

    def _advance(self, after, newest):
        if self.share:
            self._take_share(after)
        return self._take_scatter(after) if len(self.scatter) > newest else 0.0

    def prepare(self, piece, after):
        while piece not in self.reduced and not any(piece in pieces for pieces, _ in self.share):
            self._take_scatter(after)
        return self.share[-1][1]["tokens"] if self.share else after

    def reduced_piece(self, piece, after):
        while piece not in self.reduced:
            if any(piece in pieces for pieces, _ in self.share):
                self._take_share(after)
            else:
                self._take_scatter(after)
        return self.reduced[piece]


def kernel(x, mix_norm_g, w_in, q_gain, k_gain, sinks, rel_bias, conv_w, conv_b, dt_bias, a_log, d_skip, ssm_norm_g, w_out, mlp_norm_g, w_up, w_down, loss_target, m_mix_norm_g, m_w_in, m_q_gain, m_k_gain, m_sinks, m_rel_bias, m_conv_w, m_conv_b, m_dt_bias, m_a_log, m_d_skip, m_ssm_norm_g, m_w_out, m_mlp_norm_g, m_w_up, m_w_down, v_mix_norm_g, v_w_in, v_q_gain, v_k_gain, v_sinks, v_rel_bias, v_conv_w, v_conv_b, v_dt_bias, v_a_log, v_d_skip, v_ssm_norm_g, v_w_out, v_mlp_norm_g, v_w_up, v_w_down):
    wts = dict(mix_norm_g=mix_norm_g, w_in=w_in, q_gain=q_gain, k_gain=k_gain, sinks=sinks, rel_bias=rel_bias, conv_w=conv_w,
               conv_b=conv_b, dt_bias=dt_bias, a_log=a_log, d_skip=d_skip, ssm_norm_g=ssm_norm_g, w_out=w_out,
               mlp_norm_g=mlp_norm_g, w_up=w_up, w_down=w_down)
    mom = dict(mix_norm_g=m_mix_norm_g, w_in=m_w_in, q_gain=m_q_gain, k_gain=m_k_gain, sinks=m_sinks, rel_bias=m_rel_bias,
               conv_w=m_conv_w, conv_b=m_conv_b, dt_bias=m_dt_bias, a_log=m_a_log, d_skip=m_d_skip, ssm_norm_g=m_ssm_norm_g,
               w_out=m_w_out, mlp_norm_g=m_mlp_norm_g, w_up=m_w_up, w_down=m_w_down)
    var = dict(mix_norm_g=v_mix_norm_g, w_in=v_w_in, q_gain=v_q_gain, k_gain=v_k_gain, sinks=v_sinks, rel_bias=v_rel_bias,
               conv_w=v_conv_w, conv_b=v_conv_b, dt_bias=v_dt_bias, a_log=v_a_log, d_skip=v_d_skip, ssm_norm_g=v_ssm_norm_g,
               w_out=v_w_out, mlp_norm_g=v_mlp_norm_g, w_up=v_w_up, w_down=v_w_down)
    xi, yi, ci = _coords()
    k_me = 2 * xi + yi
    kc_arr = jnp.stack([k_me, ci]).astype(jnp.int32)

    prov = _Exchange(wts, kc_arr)
    small_w = {n: wts[n] for n in SMALL}
    small_w["conv_w"] = prov.conv_w
    loss, dx, grads, tok = local_step(x[0], loss_target[0], small_w, prov)

    small_shapes = {n: grads[n].shape for n in SMALL}
    pack = _pack(grads, loss) + tok
    adam = lambda n, l, g, pin, prev=None: adamw_layer(l, wts[n], g, mom[n], var[n], pin, prev)
    g_up1 = prov.reduced_piece(("w_up", 1), dx)
    spread = split_start("small_spread_start", _spread_maker(), N_DEVICES - 1,
                         [pack, lax.empty((N_DEVICES,) + pack.shape, f32)], g_up1)
    up = adam("w_up", 1, g_up1, spread["tokens"])
    small_sum = device_sum(*split_wait(spread, up[4]), kc_arr)
    loss = small_sum[PACK_ROWS - 1, PACK_COLS - 1]
    tok = prov.flush(small_sum)
    small = _unpack(small_sum, small_shapes)
    cols = conv_w.shape[-1]
    small["conv_w"] = lax.dynamic_slice_in_dim(small["conv_w"], k_me * cols, cols, axis=2)
    g_out_d = dict(small)
    gs = [small[n] for n in SMALL]
    gs[0] = gs[0] + tok
    ds, nms, nvs = adamw_small([wts[n] for n in SMALL], gs, [mom[n] for n in SMALL], [var[n] for n in SMALL])
    d_out_d, m_out_d, v_out_d = dict(zip(SMALL, ds)), dict(zip(SMALL, nms)), dict(zip(SMALL, nvs))

    pin = prov.prepare(("w_down", 0), ds[0])
    down = adam("w_down", 1, prov.reduced_piece(("w_down", 1), ds[0]), pin)
    up = adam("w_up", 0, prov.reduced_piece(("w_up", 0), down[4]), kc_arr, up[:4])
    down = adam("w_down", 0, prov.reduced_piece(("w_down", 0), down[4]), kc_arr, down[:4])
    pin = prov.prepare(("w_in", 0), down[4])
    out = adam("w_out", 1, prov.reduced_piece(("w_out", 1), down[4]), pin)
    g_in = [prov.reduced_piece(("w_in", l), out[4]) for l in range(DEPTH)]
    out = adam("w_out", 0, prov.reduced_piece(("w_out", 0), out[4]), kc_arr, out[:4])
    for n, res in (("w_up", up), ("w_down", down), ("w_out", out)):
        g_out_d[n], d_out_d[n], m_out_d[n], v_out_d[n] = res[:4]
    n = "w_in"
    rows, cols = wts[n].shape[1:]
    to_cols = lambda a: jnp.transpose(a, (2, 0, 1))
    g_t = jnp.stack([to_cols(g).reshape(cols, rows) for g in g_in], axis=1)
    res_t = adamw_cols(to_cols(wts[n]), g_t, to_cols(mom[n]), to_cols(var[n]))
    g_out_d[n], d_out_d[n], m_out_d[n], v_out_d[n] = (jnp.transpose(a, (1, 2, 0)) for a in (g_t, *res_t))

    return (loss, dx[None], *[g_out_d[n] for n in WEIGHTS], *[d_out_d[n] for n in WEIGHTS],
            *[m_out_d[n] for n in WEIGHTS], *[v_out_d[n] for n in WEIGHTS])
```

```python
import numpy as np
import jax
import jax.numpy as jnp
from jax import lax
from jax.experimental import pallas as pl
from jax.experimental.pallas import tpu as pltpu

f32 = jnp.float32
bf16 = jnp.bfloat16

SEQ = 2048
D_MODEL = 1024
DEPTH = 2
HEAD_DIM = 64
N_Q_HEADS = 8
N_KV_HEADS = 2
Q_PER_KV = N_Q_HEADS // N_KV_HEADS
BLOCK = 128
N_BLOCKS = SEQ // BLOCK
N_BUCKETS = 32
MAX_DISTANCE = 128
SSM_HEADS = 8
SSM_HEAD_DIM = 64
SSM_GROUPS = 2
HEADS_PER_GROUP = SSM_HEADS // SSM_GROUPS
SSM_STATE = 128
CONV_WIDTH = 4
CHUNK = 128
N_CHUNKS = SEQ // CHUNK
D_FF = 4 * D_MODEL
D_ATTN = N_Q_HEADS * HEAD_DIM
D_KV = N_KV_HEADS * HEAD_DIM
D_SSM = SSM_HEADS * SSM_HEAD_DIM
D_BC = SSM_GROUPS * SSM_STATE
D_CONV = D_SSM + 2 * D_BC
D_IN = D_ATTN + 2 * D_KV + D_SSM + D_CONV + SSM_HEADS
EPS = 1e-6
NEG = -1e30
N_CHIPS = 4
FF_TILE = D_FF // N_CHIPS

LANE = 128
PW = D_ATTN + D_SSM + D_CONV + 2 * D_KV + LANE
OFF_Q, OFF_Z, OFF_X, OFF_K, OFF_V, OFF_DT = 0, 512, 1024, 2048, 2176, 2304

ADAM_LR = 0.001
ADAM_B1 = 0.9
ADAM_B2 = 0.999
ADAM_EPS = 1e-08
ADAM_WD = 0.01
ADAM_STEP = 10

VMEM_LIMIT = 56 * 1024 * 1024


def _params(*sem):
    return pltpu.CompilerParams(dimension_semantics=tuple(sem), vmem_limit_bytes=VMEM_LIMIT)


def _bdot(a, b):
    return jnp.dot(a.astype(bf16), b.astype(bf16), preferred_element_type=f32)


def _bdot_nt(a, b):
    return lax.dot_general(a.astype(bf16), b.astype(bf16), (((1,), (1,)), ((), ())), preferred_element_type=f32)


def _bdot_tn(a, b):
    return lax.dot_general(a.astype(bf16), b.astype(bf16), (((0,), (0,)), ((), ())), preferred_element_type=f32)


def _hdot(a, b):
    return jnp.dot(a, b, precision=lax.Precision.HIGHEST, preferred_element_type=f32)


def _sigmoid(x):
    return 1.0 / (1.0 + jnp.exp(-x))


def _softplus(x):
    return jnp.maximum(x, 0.0) + jnp.log1p(jnp.exp(-jnp.abs(x)))


def _rms(x):
    return lax.rsqrt(jnp.mean(x * x, axis=-1, keepdims=True) + EPS)


def _rms_bwd(dy, xhat, r, g):
    t = dy * g
    return r * (t - xhat * jnp.mean(t * xhat, axis=-1, keepdims=True))


def _full(shape):
    return pl.BlockSpec(shape, lambda *_: (0,) * len(shape))


def _bucket_table():
    qi = np.arange(BLOCK)[:, None]
    kj = np.arange(2 * BLOCK)[None, :]
    dist = qi + BLOCK - kj
    ok = (dist >= 0) & (dist < 128)
    d = np.clip(dist, 0, None)
    max_exact = N_BUCKETS // 2
    d_f = np.maximum(d, 1).astype(np.float32)
    large = max_exact + (np.log(d_f / np.float32(max_exact)) / np.float32(np.log(MAX_DISTANCE / max_exact))
                         * np.float32(N_BUCKETS - max_exact)).astype(np.int32)
    large = np.minimum(large, N_BUCKETS - 1)
    bucket = np.where(d < max_exact, d, large)
    return np.where(ok, bucket, -1).astype(np.int32)


def bias_build(rel_bias, bucket):
    def body(rel_ref, bkt_ref, o_ref):
        bkt = bkt_ref[...]
        for h in range(N_Q_HEADS):
            acc = jnp.where(bkt < 0, NEG, 0.0).astype(f32)
            for b in range(N_BUCKETS):
                acc = acc + jnp.where(bkt == b, rel_ref[b, h], 0.0)
            o_ref[h] = acc

    return pl.pallas_call(
        body, name="bias_build", out_shape=jax.ShapeDtypeStruct((N_Q_HEADS,) + bucket.shape, f32),
        in_specs=[pl.BlockSpec(memory_space=pltpu.SMEM), pl.BlockSpec(memory_space=pltpu.VMEM)],
        out_specs=pl.BlockSpec(memory_space=pltpu.VMEM),
    )(rel_bias, bucket)


def bias_bwd(dband0, dband1, bucket):
    def body(d0_ref, d1_ref, bkt_ref, o_ref):
        bkt = bkt_ref[...]
        o_ref[...] = jnp.zeros_like(o_ref)
        for h in range(N_Q_HEADS):
            d = d0_ref[h] + d1_ref[h]
            for b in range(N_BUCKETS):
                part = jnp.sum(jnp.where(bkt == b, d, 0.0), axis=1, keepdims=True)
                o_ref[b:b + 1, h:h + 1] = jnp.sum(part, axis=0, keepdims=True)

    return pl.pallas_call(
        body, name="bias_bwd", out_shape=jax.ShapeDtypeStruct((N_BUCKETS, LANE), f32),
    )(dband0, dband1, bucket)


W_IN_SHARD = D_IN // N_CHIPS
_ALIGNED_PIECES = ((0, 0, 512), (1, 190, 578), (2, 0, 124), (2, 124, 578), (3, 0, 570), (0, 512, 578), (1, 0, 62),
                   (1, 62, 190), (3, 570, 578))
_SHARD_PIECES = (((0, 512), (2048, 2114)), ((2114, 2176), (2176, 2304), (512, 900)), ((900, 1024), (1024, 1478)),
                 ((1478, 2048), (2304, 2312)))


def align_w_in(shards, tr=256):
    def body(s_ref, o_ref):
        parts = [s_ref[k, :, a:b] for k, a, b in _ALIGNED_PIECES]
        parts.append(jnp.zeros((tr, LANE - SSM_HEADS), s_ref.dtype))
        o_ref[...] = jnp.concatenate(parts, axis=-1)

    return pl.pallas_call(
        body, name="align_w_in", grid=(D_MODEL // tr,),
        in_specs=[pl.BlockSpec((N_CHIPS, tr, W_IN_SHARD), lambda i: (0, i, 0))],
        out_specs=pl.BlockSpec((tr, PW), lambda i: (i, 0)),
        out_shape=jax.ShapeDtypeStruct((D_MODEL, PW), shards.dtype),
        compiler_params=_params("arbitrary"),
    )(shards)


def split_w_in_grad(dw, tr=256):
    def body(d_ref, o16_ref):
        for k, pieces in enumerate(_SHARD_PIECES):
            o16_ref[k] = jnp.concatenate([d_ref[:, a:b] for a, b in pieces], axis=-1).astype(bf16)

    return pl.pallas_call(
        body, name="split_w_in_grad", grid=(D_MODEL // tr,),
        in_specs=[pl.BlockSpec((tr, PW), lambda i: (i, 0))],
        out_specs=pl.BlockSpec((N_CHIPS, tr, W_IN_SHARD), lambda i: (0, i, 0)),
        out_shape=jax.ShapeDtypeStruct((N_CHIPS, D_MODEL, W_IN_SHARD), bf16),
        compiler_params=_params("arbitrary"),
    )(dw)

def in_fwd(x, g, w, tm=512):
    def body(x_ref, g_ref, w_ref, o_ref):
        xv = x_ref[...]
        h = xv * _rms(xv) * g_ref[...]
        o_ref[...] = _bdot(h, w_ref[...])

    return pl.pallas_call(
        body, name="in_fwd", grid=(SEQ // tm,),
        in_specs=[pl.BlockSpec((tm, D_MODEL), lambda i: (i, 0)), _full((1, D_MODEL)), _resident((D_MODEL, PW))],
        out_specs=pl.BlockSpec((tm, PW), lambda i: (i, 0)),
        out_shape=jax.ShapeDtypeStruct((SEQ, PW), f32),
        compiler_params=_params("arbitrary"),
    )(x, g, w)


def _resident(shape):
    return pl.BlockSpec(shape, lambda *_: (0,) * len(shape), pipeline_mode=pl.Buffered(1))


def in_bwd(dq, dz, dxbc, dk, dv, ddt, x, g, w, dres, tm=512):
    def body(dq_ref, dz_ref, dx_ref, dk_ref, dv_ref, ddt_ref, x_ref, g_ref, w_ref, dres_ref, o_ref, dw_ref, dg_ref):
        i = pl.program_id(0)

        @pl.when(i == 0)
        def _():
            dw_ref[...] = jnp.zeros_like(dw_ref)
            dg_ref[...] = jnp.zeros_like(dg_ref)

        dproj = jnp.concatenate([dq_ref[...], dz_ref[...], dx_ref[...], dk_ref[...], dv_ref[...], ddt_ref[...]],
                                axis=-1).astype(bf16)
        xv = x_ref[...]
        r = _rms(xv)
        xhat = xv * r
        gv = g_ref[...]
        h = xhat * gv
        dw_ref[...] += _bdot_tn(h, dproj)
        dh = _bdot_nt(dproj, w_ref[...])
        dg_ref[...] += jnp.sum(dh * xhat, axis=0, keepdims=True)
        o_ref[...] = dres_ref[...] + _rms_bwd(dh, xhat, r, gv)

    tok = lambda w_: pl.BlockSpec((tm, w_), lambda i: (i, 0))
    return pl.pallas_call(
        body, name="in_bwd", grid=(SEQ // tm,),
        in_specs=[tok(D_ATTN), tok(D_SSM), tok(D_CONV), tok(D_KV), tok(D_KV), tok(LANE), tok(D_MODEL),
                  _full((1, D_MODEL)), _resident((D_MODEL, PW)), tok(D_MODEL)],
        out_specs=[tok(D_MODEL), _resident((D_MODEL, PW)), _full((1, D_MODEL))],
        out_shape=[jax.ShapeDtypeStruct((SEQ, D_MODEL), f32), jax.ShapeDtypeStruct((D_MODEL, PW), f32),
                   jax.ShapeDtypeStruct((1, D_MODEL), f32)],
        compiler_params=_params("arbitrary"),
    )(dq, dz, dxbc, dk, dv, ddt, x, g, w, dres)


def _attn_softmax_t(qk, bias_t, sink, first, key_row):
    s = qk * (HEAD_DIM ** -0.5) + bias_t
    s = jnp.where(jnp.logical_and(first, key_row < BLOCK), NEG, s)
    m = jnp.maximum(jnp.max(s, axis=0, keepdims=True), sink)
    p = jnp.exp(s - m)
    psink = jnp.exp(sink - m)
    inv = 1.0 / (jnp.sum(p, axis=0, keepdims=True) + psink)
    return p * inv, psink * inv


def _rms_t(x_t):
    return lax.rsqrt(jnp.mean(x_t * x_t, axis=0, keepdims=True) + EPS)


def attn_fwd_t(proj, q_gain_col, k_gain, sinks, bias_t):
    kcol, vcol = OFF_K // D_KV, OFF_V // D_KV

    def body(q_ref, kc_ref, kp_ref, vc_ref, vp_ref, qg_ref, kg_ref, sink_ref, bias_ref, o_ref, ot_scr):
        n = pl.program_id(0)
        first = n == 0
        key_row = lax.broadcasted_iota(jnp.int32, (2 * BLOCK, BLOCK), 0)
        k2 = jnp.concatenate([kp_ref[...], kc_ref[...]], axis=0)
        v_t = jnp.concatenate([vp_ref[...], vc_ref[...]], axis=0).T
        q_t = q_ref[...].T
        qg = jnp.broadcast_to(qg_ref[...], (HEAD_DIM, BLOCK))
        kg = kg_ref[...]
        for hk in range(N_KV_HEADS):
            sl = slice(hk * HEAD_DIM, (hk + 1) * HEAD_DIM)
            kk = k2[:, sl]
            kn = (kk * _rms(kk) * kg).astype(bf16)
            vt = v_t[sl, :].astype(bf16)
            heads = range(hk * Q_PER_KV, (hk + 1) * Q_PER_KV)
            qns = []
            for h in heads:
                qh = q_t[h * HEAD_DIM:(h + 1) * HEAD_DIM, :]
                qns.append(qh * _rms_t(qh) * qg)
            scores = [_bdot(kn, qn) for qn in qns]
            for h, s in zip(heads, scores):
                p, _ = _attn_softmax_t(s, bias_ref[h], sink_ref[h], first, key_row)
                ot_scr[h * HEAD_DIM:(h + 1) * HEAD_DIM, :] = _bdot(vt, p)
        o_ref[...] = ot_scr[...].T

    prev = lambda n: jnp.maximum(n - 1, 0)
    return pl.pallas_call(
        body, name="attn_fwd", grid=(N_BLOCKS,),
        in_specs=[pl.BlockSpec((BLOCK, D_ATTN), lambda n: (n, 0)),
                  pl.BlockSpec((BLOCK, D_KV), lambda n: (n, kcol)), pl.BlockSpec((BLOCK, D_KV), lambda n: (prev(n), kcol)),
                  pl.BlockSpec((BLOCK, D_KV), lambda n: (n, vcol)), pl.BlockSpec((BLOCK, D_KV), lambda n: (prev(n), vcol)),
                  _full((HEAD_DIM, 1)), _full((1, HEAD_DIM)), pl.BlockSpec(memory_space=pltpu.SMEM),
                  _full((N_Q_HEADS, 2 * BLOCK, BLOCK))],
        out_specs=pl.BlockSpec((BLOCK, D_ATTN), lambda n: (n, 0)),
        out_shape=jax.ShapeDtypeStruct((SEQ, D_ATTN), f32),
        scratch_shapes=[pltpu.VMEM((D_ATTN, BLOCK), f32)],
        compiler_params=_params("arbitrary"),
    )(proj, proj, proj, proj, proj, q_gain_col, k_gain, sinks, bias_t)


def attn_bwd_t(proj, d_out, q_gain_col, k_gain, sinks, bias_t):
    kcol, vcol = OFF_K // D_KV, OFF_V // D_KV

    def body(q_ref, kc_ref, kp_ref, vc_ref, vp_ref, do_ref, qg_ref, kg_ref, sink_ref, bias_ref,
             dq_ref, dk_ref, dv_ref, dband_ref, dsink_ref, dqg_ref, dkg_ref, dkn_scr, dv_scr, dqt_scr, dsink_acc, dqg_acc):
        i = pl.program_id(0)
        first = i == N_BLOCKS - 1

        @pl.when(i == 0)
        def _():
            for ref in (dband_ref, dkg_ref, dkn_scr, dv_scr, dsink_acc, dqg_acc):
                ref[...] = jnp.zeros_like(ref)

        key_row = lax.broadcasted_iota(jnp.int32, (2 * BLOCK, BLOCK), 0)
        k2 = jnp.concatenate([kp_ref[...], kc_ref[...]], axis=0)
        v2 = jnp.concatenate([vp_ref[...], vc_ref[...]], axis=0)
        q_t = q_ref[...].T
        do_t = do_ref[...].T
        qg = jnp.broadcast_to(qg_ref[...], (HEAD_DIM, BLOCK))
        kg = kg_ref[...]
        scale = HEAD_DIM ** -0.5
        for hk in range(N_KV_HEADS):
            sl = slice(hk * HEAD_DIM, (hk + 1) * HEAD_DIM)
            kk = k2[:, sl]
            rk = _rms(kk)
            khat = kk * rk
            kn = (khat * kg).astype(bf16)
            vb = v2[:, sl].astype(bf16)
            dkn = jnp.zeros((2 * BLOCK, HEAD_DIM), f32)
            dvv = jnp.zeros((2 * BLOCK, HEAD_DIM), f32)
            heads = range(hk * Q_PER_KV, (hk + 1) * Q_PER_KV)
            rqs, qhats, qns, d_os = [], [], [], []
            for h in heads:
                hs = slice(h * HEAD_DIM, (h + 1) * HEAD_DIM)
                qh = q_t[hs, :]
                rqs.append(_rms_t(qh))
                qhats.append(qh * rqs[-1])
                qns.append((qhats[-1] * qg).astype(bf16))
                d_os.append(do_t[hs, :].astype(bf16))
            scores = [_bdot(kn, qn) for qn in qns]
            dps = [_bdot(vb, d_o) for d_o in d_os]
            ps, dss = [], []
            for h, s, dp in zip(heads, scores, dps):
                p, psink = _attn_softmax_t(s, bias_ref[h], sink_ref[h], first, key_row)
                delta = jnp.sum(p * dp, axis=0, keepdims=True)
                ds = p * (dp - delta)
                dband_ref[h] += ds
                dsink_acc[h:h + 1, :] += -(psink * delta)
                ps.append(p.astype(bf16))
                dss.append(ds.astype(bf16))
            dqns = [_bdot_tn(kn, ds) * scale for ds in dss]
            for ds, qn, p, d_o in zip(dss, qns, ps, d_os):
                dkn = dkn + _bdot_nt(ds, qn) * scale
                dvv = dvv + _bdot_nt(p, d_o)
            for h, dqn, rq, qhat in zip(heads, dqns, rqs, qhats):
                dqg_acc[...] += dqn * qhat
                t = dqn * qg
                dqt_scr[h * HEAD_DIM:(h + 1) * HEAD_DIM, :] = rq * (t - qhat * jnp.mean(t * qhat, axis=0, keepdims=True))
            dkn_cur = dkn[BLOCK:] + dkn_scr[:, sl]
            dkn_scr[:, sl] = dkn[:BLOCK]
            khat_c, rk_c = khat[BLOCK:], rk[BLOCK:]
            dkg_ref[...] += jnp.sum(dkn_cur * khat_c, axis=0, keepdims=True)
            dk_ref[:, sl] = _rms_bwd(dkn_cur, khat_c, rk_c, kg)
            dv_ref[:, sl] = dvv[BLOCK:] + dv_scr[:, sl]
            dv_scr[:, sl] = dvv[:BLOCK]
        dq_ref[...] = dqt_scr[...].T

        @pl.when(i == N_BLOCKS - 1)
        def _():
            dsink_ref[...] = jnp.sum(dsink_acc[...], axis=1, keepdims=True)
            dqg_ref[...] = jnp.sum(dqg_acc[...], axis=1, keepdims=True)

    blk = lambda i: N_BLOCKS - 1 - i
    prev = lambda i: jnp.maximum(N_BLOCKS - 2 - i, 0)
    return pl.pallas_call(
        body, name="attn_bwd", grid=(N_BLOCKS,),
        in_specs=[pl.BlockSpec((BLOCK, D_ATTN), lambda i: (blk(i), 0)),
                  pl.BlockSpec((BLOCK, D_KV), lambda i: (blk(i), kcol)), pl.BlockSpec((BLOCK, D_KV), lambda i: (prev(i), kcol)),
                  pl.BlockSpec((BLOCK, D_KV), lambda i: (blk(i), vcol)), pl.BlockSpec((BLOCK, D_KV), lambda i: (prev(i), vcol)),
                  pl.BlockSpec((BLOCK, D_ATTN), lambda i: (blk(i), 0)),
                  _full((HEAD_DIM, 1)), _full((1, HEAD_DIM)), pl.BlockSpec(memory_space=pltpu.SMEM),
                  _full((N_Q_HEADS, 2 * BLOCK, BLOCK))],
        out_specs=[pl.BlockSpec((BLOCK, D_ATTN), lambda i: (blk(i), 0)), pl.BlockSpec((BLOCK, D_KV), lambda i: (blk(i), 0)),
                   pl.BlockSpec((BLOCK, D_KV), lambda i: (blk(i), 0)), _full((N_Q_HEADS, 2 * BLOCK, BLOCK)),
                   _full((N_Q_HEADS, 1)), _full((HEAD_DIM, 1)), _full((1, HEAD_DIM))],
        out_shape=[jax.ShapeDtypeStruct((SEQ, D_ATTN), f32), jax.ShapeDtypeStruct((SEQ, D_KV), f32),
                   jax.ShapeDtypeStruct((SEQ, D_KV), f32), jax.ShapeDtypeStruct((N_Q_HEADS, 2 * BLOCK, BLOCK), f32),
                   jax.ShapeDtypeStruct((N_Q_HEADS, 1), f32), jax.ShapeDtypeStruct((HEAD_DIM, 1), f32),
                   jax.ShapeDtypeStruct((1, HEAD_DIM), f32)],
        scratch_shapes=[pltpu.VMEM((BLOCK, D_KV), f32), pltpu.VMEM((BLOCK, D_KV), f32), pltpu.VMEM((D_ATTN, BLOCK), f32),
                        pltpu.VMEM((N_Q_HEADS, BLOCK), f32), pltpu.VMEM((HEAD_DIM, BLOCK), f32)],
        compiler_params=_params("arbitrary"),
    )(proj, proj, proj, proj, proj, d_out, q_gain_col, k_gain, sinks, bias_t)


SUBLANES = 8


def _shift_down(u, s, row8):
    if s == 0:
        return u
    r = pltpu.roll(u, s, 0)
    return jnp.concatenate([jnp.where(row8 >= s, r[:SUBLANES], 0.0), r[SUBLANES:]], axis=0)


def _shift_up(u, s, row8):
    if s == 0:
        return u
    r = pltpu.roll(u, SEQ - s, 0)
    return jnp.concatenate([r[:-SUBLANES], jnp.where(row8 < SUBLANES - s, r[-SUBLANES:], 0.0)], axis=0)


def conv_fwd(proj, conv_w, conv_b):
    xcol = OFF_X // LANE

    def body(u_ref, w_ref, b_ref, o_ref):
        u = u_ref[...]
        row = lax.broadcasted_iota(jnp.int32, (SUBLANES, LANE), 0)
        pre = b_ref[...] + jnp.zeros_like(u)
        for k in range(CONV_WIDTH):
            pre = pre + w_ref[k:k + 1, :] * _shift_down(u, CONV_WIDTH - 1 - k, row)
        o_ref[...] = pre * _sigmoid(pre)

    return pl.pallas_call(
        body, name="conv_fwd", grid=(D_CONV // LANE,),
        in_specs=[pl.BlockSpec((SEQ, LANE), lambda j: (0, xcol + j)), pl.BlockSpec((CONV_WIDTH, LANE), lambda j: (0, j)),
                  pl.BlockSpec((1, LANE), lambda j: (0, j))],
        out_specs=pl.BlockSpec((SEQ, LANE), lambda j: (0, j)),
        out_shape=jax.ShapeDtypeStruct((SEQ, D_CONV), f32),
        compiler_params=_params("arbitrary"),
    )(proj, conv_w, conv_b)


def conv_bwd(proj, d_act, conv_w, conv_b):
    xcol = OFF_X // LANE

    def body(u_ref, da_ref, w_ref, b_ref, du_ref, dw_ref, db_ref):
        u = u_ref[...]
        row = lax.broadcasted_iota(jnp.int32, (SUBLANES, LANE), 0)
        shifted = [_shift_down(u, CONV_WIDTH - 1 - k, row) for k in range(CONV_WIDTH)]
        pre = b_ref[...] + jnp.zeros_like(u)
        for k in range(CONV_WIDTH):
            pre = pre + w_ref[k:k + 1, :] * shifted[k]
        sg = _sigmoid(pre)
        dpre = da_ref[...] * (sg * (1.0 + pre * (1.0 - sg)))
        db_ref[...] = jnp.sum(dpre, axis=0, keepdims=True)
        du = jnp.zeros_like(u)
        for k in range(CONV_WIDTH):
            dw_ref[k:k + 1, :] = jnp.sum(dpre * shifted[k], axis=0, keepdims=True)
            du = du + w_ref[k:k + 1, :] * _shift_up(dpre, CONV_WIDTH - 1 - k, row)
        du_ref[...] = du

    return pl.pallas_call(
        body, name="conv_bwd", grid=(D_CONV // LANE,),
        in_specs=[pl.BlockSpec((SEQ, LANE), lambda j: (0, xcol + j)), pl.BlockSpec((SEQ, LANE), lambda j: (0, j)),
                  pl.BlockSpec((CONV_WIDTH, LANE), lambda j: (0, j)), pl.BlockSpec((1, LANE), lambda j: (0, j))],
        out_specs=[pl.BlockSpec((SEQ, LANE), lambda j: (0, j)), pl.BlockSpec((CONV_WIDTH, LANE), lambda j: (0, j)),
                   pl.BlockSpec((1, LANE), lambda j: (0, j))],
        out_shape=[jax.ShapeDtypeStruct((SEQ, D_CONV), f32), jax.ShapeDtypeStruct((CONV_WIDTH, D_CONV), f32),
                   jax.ShapeDtypeStruct((1, D_CONV), f32)],
        compiler_params=_params("arbitrary"),
    )(proj, d_act, conv_w, conv_b)


def _ssd_chunk_common(dt_raw, dtb, alog):
    row = lax.broadcasted_iota(jnp.int32, (CHUNK, CHUNK), 0)
    col = lax.broadcasted_iota(jnp.int32, (CHUNK, CHUNK), 1)
    tri = (row >= col).astype(f32)
    strict = (row > col).astype(f32)
    dtp = _softplus(dt_raw + dtb)
    a_row = -jnp.exp(alog)
    d_a = dtp * a_row
    cs = _hdot(tri, d_a)
    cs_last = cs[CHUNK - 1:CHUNK, :]
    return row, col, dtp, a_row, cs, cs.T, cs_last


def _seg_decay(cs, cs_t, hd, row, col):
    seg = cs[:, hd:hd + 1] - cs_t[hd:hd + 1, :]
    return jnp.where(row >= col, jnp.exp(seg), 0.0)


GROUP_W = HEADS_PER_GROUP * SSM_HEAD_DIM


def _group_indicator(g):
    j = lax.broadcasted_iota(jnp.int32, (GROUP_W, LANE), 0)
    lane = lax.broadcasted_iota(jnp.int32, (GROUP_W, LANE), 1)
    return (lane == g * HEADS_PER_GROUP + j // SSM_HEAD_DIM).astype(bf16)


def _bf16_pieces(a, n):
    pieces = []
    for _ in range(n):
        p = a.astype(bf16)
        pieces.append(p)
        a = a - p.astype(f32)
    return pieces


def _head_spread(a, ind):
    return sum(lax.dot_general(p, ind, (((1,), (1,)), ((), ())), preferred_element_type=f32) for p in _bf16_pieces(a, 3))


def _head_sums(a, ind):
    return sum(jnp.dot(p, ind, preferred_element_type=f32) for p in _bf16_pieces(a, 2))


def ssd_fwd_g(act, proj, dt_bias, a_log, d_skip, norm_g):
    zcol, dtcol = OFF_Z // D_SSM, OFF_DT // LANE

    def body(act_ref, z_ref, dt_ref, dtb_ref, alog_ref, dsk_ref, ng_ref, out_ref, ypre_ref, st_ref, state):
        c = pl.program_id(0)

        @pl.when(c == 0)
        def _():
            state[...] = jnp.zeros_like(state)

        row, col, dtp, a_row, cs, cs_t, cs_last = _ssd_chunk_common(dt_ref[...], dtb_ref[...], alog_ref[...])
        e_cs = jnp.exp(cs)
        dte = jnp.exp(cs_last - cs)
        rows8 = jnp.concatenate([jnp.exp(cs_last), dsk_ref[...], jnp.zeros((6, LANE), f32)], axis=0)
        z = z_ref[...]
        sz = z * _sigmoid(z)
        ng = ng_ref[...]
        for g in range(SSM_GROUPS):
            gs = slice(g * GROUP_W, (g + 1) * GROUP_W)
            ind = _group_indicator(g)
            xg = act_ref[:, gs]
            bg = act_ref[:, D_SSM + g * SSM_STATE:D_SSM + (g + 1) * SSM_STATE]
            cg = act_ref[:, D_SSM + D_BC + g * SSM_STATE:D_SSM + D_BC + (g + 1) * SSM_STATE]
            dt_e, e_e, dte_e = _head_spread(dtp, ind), _head_spread(e_cs, ind), _head_spread(dte, ind)
            rows_e = _head_spread(rows8, ind)
            ecl_e, dsk_e = rows_e[0:1], rows_e[1:2]
            xdt = xg * dt_e
            prev = state[g]
            st_ref[0, g] = prev
            cb = _bdot_nt(cg, bg)
            goff = _bdot(cg, prev)
            snew = _bdot_tn(bg, xdt * dte_e)
            heads = range(g * HEADS_PER_GROUP, (g + 1) * HEADS_PER_GROUP)
            ms = [cb * _seg_decay(cs, cs_t, hd, row, col) for hd in heads]
            yd = [_bdot(m, xdt[:, r * SSM_HEAD_DIM:(r + 1) * SSM_HEAD_DIM]) for r, m in enumerate(ms)]
            y = jnp.concatenate(yd, axis=1) + e_e * goff + xg * dsk_e
            state[g] = prev * ecl_e + snew
            ypre_ref[:, gs] = y
            part = y * sz[:, gs]
            out_ref[:, gs] = part * _rms(part) * ng[:, gs]

    return pl.pallas_call(
        body, name="ssd_fwd", grid=(N_CHUNKS,),
        in_specs=[pl.BlockSpec((CHUNK, D_CONV), lambda c: (c, 0)), pl.BlockSpec((CHUNK, D_SSM), lambda c: (c, zcol)),
                  pl.BlockSpec((CHUNK, LANE), lambda c: (c, dtcol)), _full((1, LANE)), _full((1, LANE)), _full((1, LANE)),
                  _full((1, D_SSM))],
        out_specs=[pl.BlockSpec((CHUNK, D_SSM), lambda c: (c, 0)), pl.BlockSpec((CHUNK, D_SSM), lambda c: (c, 0)),
                   pl.BlockSpec((1, SSM_GROUPS, SSM_STATE, GROUP_W), lambda c: (c, 0, 0, 0))],
        out_shape=[jax.ShapeDtypeStruct((SEQ, D_SSM), f32), jax.ShapeDtypeStruct((SEQ, D_SSM), f32),
                   jax.ShapeDtypeStruct((N_CHUNKS, SSM_GROUPS, SSM_STATE, GROUP_W), f32)],
        scratch_shapes=[pltpu.VMEM((SSM_GROUPS, SSM_STATE, GROUP_W), f32)],
        compiler_params=_params("arbitrary"),
    )(act, proj, proj, dt_bias, a_log, d_skip, norm_g)


def ssd_bwd_g(act, proj, ypre, states, d_out, dt_bias, a_log, d_skip, norm_g):
    zcol, dtcol = OFF_Z // D_SSM, OFF_DT // LANE

    def body(act_ref, z_ref, dt_ref, ypre_ref, st_ref, do_ref, dtb_ref, alog_ref, dsk_ref, ng_ref,
             dact_ref, ddt_ref, dz_ref, dng_ref, dpar_ref, dstate):
        i = pl.program_id(0)

        @pl.when(i == 0)
        def _():
            for ref in (dng_ref, dpar_ref, dstate):
                ref[...] = jnp.zeros_like(ref)

        row, col, dtp, a_row, cs, cs_t, cs_last = _ssd_chunk_common(dt_ref[...], dtb_ref[...], alog_ref[...])
        upper = (row <= col).astype(f32)
        lane = lax.broadcasted_iota(jnp.int32, (CHUNK, LANE), 1)
        rowl = lax.broadcasted_iota(jnp.int32, (CHUNK, LANE), 0)
        e_cs = jnp.exp(cs)
        dte = jnp.exp(cs_last - cs)
        ecl = jnp.exp(cs_last)
        rows8 = jnp.concatenate([ecl, dsk_ref[...], jnp.zeros((6, LANE), f32)], axis=0)
        z = z_ref[...]
        sgz = _sigmoid(z)
        sz = z * sgz
        ng = ng_ref[...]
        ddt_mat = jnp.zeros((CHUNK, LANE), f32)
        dcs_mat = jnp.zeros((CHUNK, LANE), f32)
        dcs_t = jnp.zeros((LANE, CHUNK), f32)
        dcsl_row = jnp.zeros((1, LANE), f32)
        dd_row = jnp.zeros((1, LANE), f32)
        for g in range(SSM_GROUPS):
            gs = slice(g * GROUP_W, (g + 1) * GROUP_W)
            bsl = slice(D_SSM + g * SSM_STATE, D_SSM + (g + 1) * SSM_STATE)
            csl = slice(D_SSM + D_BC + g * SSM_STATE, D_SSM + D_BC + (g + 1) * SSM_STATE)
            ind = _group_indicator(g)
            y = ypre_ref[:, gs]
            part = y * sz[:, gs]
            r = _rms(part)
            yhat = part * r
            d_o = do_ref[:, gs]
            dng_ref[:, gs] += jnp.sum(d_o * yhat, axis=0, keepdims=True)
            dyz = _rms_bwd(d_o, yhat, r, ng[:, gs])
            dy = dyz * sz[:, gs]
            dz_ref[:, gs] = dyz * y * (sgz[:, gs] * (1.0 + z[:, gs] * (1.0 - sgz[:, gs])))

            xg = act_ref[:, gs]
            bg = act_ref[:, bsl]
            cg = act_ref[:, csl]
            dt_e, e_e, dte_e = _head_spread(dtp, ind), _head_spread(e_cs, ind), _head_spread(dte, ind)
            rows_e = _head_spread(rows8, ind)
            ecl_e, dsk_e = rows_e[0:1], rows_e[1:2]
            xdt = xg * dt_e
            prev = st_ref[0, g]
            dh = dstate[g]
            heads = range(g * HEADS_PER_GROUP, (g + 1) * HEADS_PER_GROUP)
            hsl = [slice(r_ * SSM_HEAD_DIM, (r_ + 1) * SSM_HEAD_DIM) for r_ in range(HEADS_PER_GROUP)]
            cb = _bdot_nt(cg, bg)
            lms = [_seg_decay(cs, cs_t, hd, row, col) for hd in heads]
            ms = [cb * lm for lm in lms]
            gmat = _bdot(cg, prev)
            dgm = dy * e_e
            dcg = _bdot_nt(dgm, prev)
            dprev = _bdot_tn(cg, dgm)
            dbg = _bdot_nt(xdt * dte_e, dh)
            dw = _bdot(bg, dh)
            dms = [_bdot_nt(dy[:, s_], xdt[:, s_]) for s_ in hsl]
            dxdts = [_bdot_tn(m, dy[:, s_]) for m, s_ in zip(ms, hsl)]
            dxdt = jnp.concatenate(dxdts, axis=1) + dw * dte_e
            dact_ref[:, gs] = dy * dsk_e + dxdt * dt_e
            dstate[g] = dprev + dh * ecl_e
            dcb = jnp.zeros((CHUNK, CHUNK), f32)
            for hd, dm, lm, m in zip(heads, dms, lms, ms):
                dcb = dcb + dm * lm
                dseg = dm * m
                dcs_mat = dcs_mat + jnp.where(lane == hd, jnp.sum(dseg, axis=1, keepdims=True), 0.0)
                dcs_t = jnp.where(row == hd, jnp.sum(dseg, axis=0, keepdims=True), dcs_t)
            dact_ref[:, bsl] = dbg + _bdot_tn(dcb, cg)
            dact_ref[:, csl] = dcg + _bdot(dcb, bg)
            ddte = _head_sums(dw * xdt, ind) * dte
            dcs_mat = dcs_mat + _head_sums(dy * gmat, ind) * e_cs - ddte
            ddt_mat = ddt_mat + _head_sums(dxdt * xg, ind)
            dcsl_row = (dcsl_row + jnp.sum(ddte, axis=0, keepdims=True)
                        + jnp.sum(_head_sums(dh * prev, ind), axis=0, keepdims=True) * ecl)
            dd_row = dd_row + jnp.sum(_head_sums(dy * xg, ind), axis=0, keepdims=True)
        dcs_mat = dcs_mat - dcs_t.T + jnp.where(rowl == CHUNK - 1, dcsl_row, 0.0)
        dda = _hdot(upper, dcs_mat)
        ddt_mat = ddt_mat + dda * a_row
        da_row = jnp.sum(dda * dtp, axis=0, keepdims=True)
        ddt_raw = ddt_mat * _sigmoid(dt_ref[...] + dtb_ref[...])
        ddt_ref[...] = ddt_raw
        dpar_ref[0:1, :] += jnp.sum(ddt_raw, axis=0, keepdims=True)
        dpar_ref[1:2, :] += da_row * a_row
        dpar_ref[2:3, :] += dd_row

    blk = lambda i: N_CHUNKS - 1 - i
    return pl.pallas_call(
        body, name="ssd_bwd", grid=(N_CHUNKS,),
        in_specs=[pl.BlockSpec((CHUNK, D_CONV), lambda i: (blk(i), 0)), pl.BlockSpec((CHUNK, D_SSM), lambda i: (blk(i), zcol)),
                  pl.BlockSpec((CHUNK, LANE), lambda i: (blk(i), dtcol)), pl.BlockSpec((CHUNK, D_SSM), lambda i: (blk(i), 0)),
                  pl.BlockSpec((1, SSM_GROUPS, SSM_STATE, GROUP_W), lambda i: (blk(i), 0, 0, 0)),
                  pl.BlockSpec((CHUNK, D_SSM), lambda i: (blk(i), 0)),
                  _full((1, LANE)), _full((1, LANE)), _full((1, LANE)), _full((1, D_SSM))],
        out_specs=[pl.BlockSpec((CHUNK, D_CONV), lambda i: (blk(i), 0)), pl.BlockSpec((CHUNK, LANE), lambda i: (blk(i), 0)),
                   pl.BlockSpec((CHUNK, D_SSM), lambda i: (blk(i), 0)), _full((1, D_SSM)), _full((8, LANE))],
        out_shape=[jax.ShapeDtypeStruct((SEQ, D_CONV), f32), jax.ShapeDtypeStruct((SEQ, LANE), f32),
                   jax.ShapeDtypeStruct((SEQ, D_SSM), f32), jax.ShapeDtypeStruct((1, D_SSM), f32),
                   jax.ShapeDtypeStruct((8, LANE), f32)],
        scratch_shapes=[pltpu.VMEM((SSM_GROUPS, SSM_STATE, GROUP_W), f32)],
        compiler_params=_params("arbitrary"),
    )(act, proj, proj, ypre, states, d_out, dt_bias, a_log, d_skip, norm_g)


def out_fwd(x, attn, ssm, w_out, tm=512):
    def body(x_ref, a_ref, s_ref, w_ref, o_ref):
        o_ref[...] = x_ref[...] + _bdot(a_ref[...], w_ref[:D_ATTN, :]) + _bdot(s_ref[...], w_ref[D_ATTN:, :])

    tok = lambda w_: pl.BlockSpec((tm, w_), lambda i: (i, 0))
    return pl.pallas_call(
        body, name="out_fwd", grid=(SEQ // tm,),
        in_specs=[tok(D_MODEL), tok(D_ATTN), tok(D_SSM), _full((D_MODEL, D_MODEL))],
        out_specs=tok(D_MODEL), out_shape=jax.ShapeDtypeStruct((SEQ, D_MODEL), f32),
        compiler_params=_params("arbitrary"),
    )(x, attn, ssm, w_out)


def out_bwd(dx1, attn, ssm, w_out, tm=512):
    nt = SEQ // tm

    def body(d_ref, a_ref, s_ref, w_ref, da_ref, ds_ref, dw16_ref, dw_ref):
        i = pl.program_id(0)

        @pl.when(i == 0)
        def _():
            dw_ref[...] = jnp.zeros_like(dw_ref)

        d = d_ref[...].astype(bf16)
        dcat = _bdot_nt(d, w_ref[...])
        da_ref[...] = dcat[:, :D_ATTN]
        ds_ref[...] = dcat[:, D_ATTN:]
        dw_ref[:D_ATTN, :] += _bdot_tn(a_ref[...], d)
        dw_ref[D_ATTN:, :] += _bdot_tn(s_ref[...], d)

        @pl.when(i == nt - 1)
        def _():
            dw16_ref[...] = dw_ref[...].astype(bf16)

    tok = lambda w_: pl.BlockSpec((tm, w_), lambda i: (i, 0))
    return pl.pallas_call(
        body, name="out_bwd", grid=(nt,),
        in_specs=[tok(D_MODEL), tok(D_ATTN), tok(D_SSM), _resident((D_MODEL, D_MODEL))],
        out_specs=[tok(D_ATTN), tok(D_SSM), _resident((D_MODEL, D_MODEL))],
        out_shape=[jax.ShapeDtypeStruct((SEQ, D_ATTN), f32), jax.ShapeDtypeStruct((SEQ, D_SSM), f32),
                   jax.ShapeDtypeStruct((D_MODEL, D_MODEL), bf16)],
        scratch_shapes=[pltpu.VMEM((D_MODEL, D_MODEL), f32)],
        compiler_params=_params("arbitrary"),
    )(dx1, attn, ssm, w_out)


MLP_SUB = 256


def mlp_fwd(x1, g, w_up, w_down, tm=1024):
    def body(x_ref, g_ref, wu_ref, wd_ref, o_ref, u_ref, h_scr):
        j = pl.program_id(1)

        @pl.when(j == 0)
        def _():
            xv = x_ref[...]
            h_scr[...] = (xv * _rms(xv) * g_ref[...]).astype(bf16)
            o_ref[...] = xv

        for r in range(tm // MLP_SUB):
            rows = slice(r * MLP_SUB, (r + 1) * MLP_SUB)
            u = jnp.dot(h_scr[rows, :], wu_ref[...], preferred_element_type=f32)
            u_ref[rows, :] = u
            a = jnp.square(jnp.maximum(u, 0.0))
            o_ref[rows, :] += _bdot(a, wd_ref[...])

    return pl.pallas_call(
        body, name="mlp_fwd", grid=(SEQ // tm, N_CHIPS),
        in_specs=[pl.BlockSpec((tm, D_MODEL), lambda i, j: (i, 0)), _full((1, D_MODEL)),
                  pl.BlockSpec((None, D_MODEL, FF_TILE), lambda i, j: (j, 0, 0)),
                  pl.BlockSpec((None, FF_TILE, D_MODEL), lambda i, j: (j, 0, 0))],
        out_specs=[pl.BlockSpec((tm, D_MODEL), lambda i, j: (i, 0)), pl.BlockSpec((tm, FF_TILE), lambda i, j: (i, j))],
        out_shape=[jax.ShapeDtypeStruct((SEQ, D_MODEL), f32), jax.ShapeDtypeStruct((SEQ, D_FF), f32)],
        scratch_shapes=[pltpu.VMEM((tm, D_MODEL), bf16)],
        compiler_params=_params("arbitrary", "arbitrary"),
    )(x1, g, w_up, w_down)


def mlp_bwd_data(dx2, u, x1, g, w_up, w_down, tm=1024):
    def body(d_ref, u_ref, x_ref, g_ref, wu_ref, wd_ref, dx_ref, du_ref, dg_ref, dh_scr):
        i, j = pl.program_id(0), pl.program_id(1)

        @pl.when(jnp.logical_and(i == 0, j == 0))
        def _():
            dg_ref[...] = jnp.zeros_like(dg_ref)

        @pl.when(j == 0)
        def _():
            dh_scr[...] = jnp.zeros_like(dh_scr)

        for r in range(tm // MLP_SUB):
            rows = slice(r * MLP_SUB, (r + 1) * MLP_SUB)
            da = _bdot_nt(d_ref[rows, :], wd_ref[...])
            du = (da * (2.0 * jnp.maximum(u_ref[rows, :], 0.0))).astype(bf16)
            du_ref[rows, :] = du
            dh_scr[rows, :] += _bdot_nt(du, wu_ref[...])

        @pl.when(j == N_CHIPS - 1)
        def _():
            xv = x_ref[...]
            r = _rms(xv)
            xhat = xv * r
            dh = dh_scr[...]
            dg_ref[...] += jnp.sum(dh * xhat, axis=0, keepdims=True)
            dx_ref[...] = d_ref[...] + _rms_bwd(dh, xhat, r, g_ref[...])

    return pl.pallas_call(
        body, name="mlp_bwd_data", grid=(SEQ // tm, N_CHIPS),
        in_specs=[pl.BlockSpec((tm, D_MODEL), lambda i, j: (i, 0)), pl.BlockSpec((tm, FF_TILE), lambda i, j: (i, j)),
                  pl.BlockSpec((tm, D_MODEL), lambda i, j: (i, 0)), _full((1, D_MODEL)),
                  pl.BlockSpec((None, D_MODEL, FF_TILE), lambda i, j: (j, 0, 0)),
                  pl.BlockSpec((None, FF_TILE, D_MODEL), lambda i, j: (j, 0, 0))],
        out_specs=[pl.BlockSpec((tm, D_MODEL), lambda i, j: (i, 0)), pl.BlockSpec((tm, FF_TILE), lambda i, j: (i, j)),
                   _full((1, D_MODEL))],
        out_shape=[jax.ShapeDtypeStruct((SEQ, D_MODEL), f32), jax.ShapeDtypeStruct((SEQ, D_FF), bf16),
                   jax.ShapeDtypeStruct((1, D_MODEL), f32)],
        scratch_shapes=[pltpu.VMEM((tm, D_MODEL), f32)],
        compiler_params=_params("arbitrary", "arbitrary"),
    )(dx2, u, x1, g, w_up, w_down)


def mlp_bwd_weights(dx2, u, du, x1, g, tm=512):
    nt = SEQ // tm

    def body(d_ref, u_ref, du_ref, x_ref, g_ref, dwu16_ref, dwd16_ref, h_scr, d_scr, dwu_ref, dwd_ref):
        j, i = pl.program_id(0), pl.program_id(1)

        @pl.when(j == 0)
        def _():
            xv = x_ref[...]
            h_scr[i] = (xv * _rms(xv) * g_ref[...]).T.astype(bf16)
            d_scr[i] = d_ref[...].astype(bf16)

        @pl.when(i == 0)
        def _():
            dwu_ref[...] = jnp.zeros_like(dwu_ref)
            dwd_ref[...] = jnp.zeros_like(dwd_ref)

        dwu_ref[...] += jnp.dot(h_scr[i], du_ref[...], preferred_element_type=f32)
        a = jnp.square(jnp.maximum(u_ref[...], 0.0))
        dwd_ref[...] += _bdot_tn(a, d_scr[i])

        @pl.when(i == nt - 1)
        def _():
            dwu16_ref[...] = dwu_ref[...].astype(bf16)
            dwd16_ref[...] = dwd_ref[...].astype(bf16)

    up = pl.BlockSpec((None, D_MODEL, FF_TILE), lambda j, i: (j, 0, 0))
    down = pl.BlockSpec((None, FF_TILE, D_MODEL), lambda j, i: (j, 0, 0))
    first_pass = pl.BlockSpec((tm, D_MODEL), lambda j, i: (jnp.where(j == 0, i, nt - 1), 0))
    return pl.pallas_call(
        body, name="mlp_bwd_weights", grid=(N_CHIPS, nt),
        in_specs=[first_pass, pl.BlockSpec((tm, FF_TILE), lambda j, i: (i, j)),
                  pl.BlockSpec((tm, FF_TILE), lambda j, i: (i, j)), first_pass, _full((1, D_MODEL))],
        out_specs=[up, down],
        out_shape=[jax.ShapeDtypeStruct((N_CHIPS, D_MODEL, FF_TILE), bf16), jax.ShapeDtypeStruct((N_CHIPS, FF_TILE, D_MODEL), bf16)],
        scratch_shapes=[pltpu.VMEM((nt, D_MODEL, tm), bf16), pltpu.VMEM((nt, tm, D_MODEL), bf16),
                        pltpu.VMEM((D_MODEL, FF_TILE), f32), pltpu.VMEM((FF_TILE, D_MODEL), f32)],
        compiler_params=_params("arbitrary", "arbitrary"),
    )(dx2, u, du, x1, g)


def loss_head(y, target, tm=512):
    def body(y_ref, t_ref, dy_ref, l_ref):
        @pl.when(pl.program_id(0) == 0)
        def _():
            l_ref[...] = jnp.zeros_like(l_ref)

        d = y_ref[...] - t_ref[...]
        dy_ref[...] = d * (1.0 / D_MODEL)
        part = jnp.sum(jnp.mean(d * d, axis=-1, keepdims=True), axis=0, keepdims=True)
        l_ref[...] += 0.5 * part

    tok = pl.BlockSpec((tm, D_MODEL), lambda i: (i, 0))
    return pl.pallas_call(
        body, name="loss_head", grid=(SEQ // tm,), in_specs=[tok, tok], out_specs=[tok, _full((1, 1))],
        out_shape=[jax.ShapeDtypeStruct((SEQ, D_MODEL), f32), jax.ShapeDtypeStruct((1, 1), f32)],
        compiler_params=_params("arbitrary"),
    )(y, target)


def _pad_lane(v):
    return jnp.pad(v, (0, LANE - v.shape[0]))[None, :]


def local_step(x, target, w, prov):
    bucket = jnp.asarray(_bucket_table().T)
    bias = bias_build(w["rel_bias"], bucket)
    saved = []
    for l in range(DEPTH):
        g_mix = w["mix_norm_g"][l][None, :] + prov.stage(("begin", l), x)
        w_in = prov.w_in(l, x)
        proj = in_fwd(x, g_mix, w_in)
        conv_b = w["conv_b"][l][None, :]
        act = conv_fwd(proj, w["conv_w"][l], conv_b)
        dtb = _pad_lane(w["dt_bias"][l]) + prov.stage(("mid", l), act)
        alog, dsk = _pad_lane(w["a_log"][l]), _pad_lane(w["d_skip"][l])
        ng = w["ssm_norm_g"][l][None, :]
        ssm, ypre, states = ssd_fwd_g(act, proj, dtb, alog, dsk, ng)
        qg, kg = w["q_gain"][l][:, None] + 0.0 * ssm[:1, :1], w["k_gain"][l][None, :]
        attn = attn_fwd_t(proj, qg, kg, w["sinks"][l], bias)
        tok = prov.stage(("pre_out", l), attn)
        w_out = prov.w_out(l, attn) + jnp.asarray(tok, bf16)
        x1 = out_fwd(x, attn, ssm, w_out)
        g_mlp = w["mlp_norm_g"][l][None, :] + prov.stage(("pre_mlp", l), x1)
        w_up, w_down = prov.mlp(l, x1)
        x2, u = mlp_fwd(x1, g_mlp, w_up, w_down)
        saved.append(dict(x=x, proj=proj, attn=attn, act=act, ssm=ssm, ypre=ypre, states=states, x1=x1, u=u,
                          g_mix=g_mix, qg=qg, kg=kg, conv_b=conv_b, dtb=dtb, alog=alog, dsk=dsk, ng=ng, g_mlp=g_mlp,
                          w_in=w_in, w_out=w_out, w_up=w_up, w_down=w_down))
        x = x2
    dx, loss = loss_head(x, target)
    grads = [None] * DEPTH
    dbands = [None] * DEPTH
    tok = 0.0
    for l in reversed(range(DEPTH)):
        s = saved[l]
        g_mlp = s["g_mlp"] + tok
        dx1, du, dg_mlp = mlp_bwd_data(dx, s["u"], s["x1"], g_mlp, s["w_up"], s["w_down"])
        dw_up, dw_down = mlp_bwd_weights(dx, s["u"], du, s["x1"], g_mlp)
        tok = prov.grads(("mlp", l), dict(w_up=dw_up, w_down=dw_down), dx1)
        dattn, dssm, dw_out = out_bwd(dx1, s["attn"], s["ssm"], s["w_out"])
        dact, ddt, dz, dng, dpar = ssd_bwd_g(s["act"], s["proj"], s["ypre"], s["states"], dssm, s["dtb"] + tok, s["alog"],
                                           s["dsk"], s["ng"])
        conv_b = s["conv_b"] + prov.stage(("bwd_mid", l), dact)
        dxbc, dconv_w, dconv_b = conv_bwd(s["proj"], dact, w["conv_w"][l], conv_b)
        dq, dk, dv, dband, dsink, dqg, dkg = attn_bwd_t(s["proj"], dattn, s["qg"], s["kg"], w["sinks"][l], bias)
        dbands[l] = dband
        g_mix = s["g_mix"]
        if l == 0:
            d_rel = bias_bwd(dbands[0], dbands[1], bucket)
            g_mix = g_mix + 0.0 * d_rel[:1, :1]
        dx, dw_in, dg_mix = in_bwd(dq, dz, dxbc, dk, dv, ddt, s["x"], g_mix, s["w_in"], dx1)
        tok = prov.grads(("mix", l), dict(w_in=split_w_in_grad(dw_in), w_out=dw_out), dx)
        grads[l] = dict(mix_norm_g=dg_mix[0], q_gain=dqg[:, 0], k_gain=dkg[0], sinks=dsink[:, 0],
                        conv_w=dconv_w, conv_b=dconv_b[0], dt_bias=dpar[0, :SSM_HEADS], a_log=dpar[1, :SSM_HEADS],
                        d_skip=dpar[2, :SSM_HEADS], ssm_norm_g=dng[0], mlp_norm_g=dg_mlp[0])
    out = {k: jnp.stack([grads[l][k] for l in range(DEPTH)]) for k in grads[0]}
    out["rel_bias"] = d_rel[:, :N_Q_HEADS]
    return loss, dx, out, tok


MESH = pl.DeviceIdType.MESH
HBM = pl.BlockSpec(memory_space=pltpu.HBM)
N_DEVICES = 8


def _coords():
    return lax.axis_index("x"), lax.axis_index("y"), lax.axis_index("c")


def _peer_chips(x, y):
    return [(1 - x, y), (x, 1 - y), (1 - x, 1 - y)]


def _remote(src, dst, send_sem, recv_sem, device):
    return pltpu.make_async_remote_copy(src_ref=src, dst_ref=dst, send_sem=send_sem, recv_sem=recv_sem,
                                        device_id=device, device_id_type=MESH)


SEM = pl.BlockSpec(memory_space=pltpu.SEMAPHORE)
ANY = pl.BlockSpec(memory_space=pl.ANY)
DATAFLOW = pltpu.SideEffectType.DATAFLOW_SIDE_EFFECTING


def _gather_copies(kind, src_refs, land_refs, ssem, rsem):
    x, y, c = _coords()
    k_me = 2 * x + y
    n = len(land_refs)
    cps = []
    for p, land in enumerate(land_refs):
        hr = land.shape[1] // 2
        rows = pl.ds(c * hr, hr)
        for j, chip in enumerate(_peer_chips(x, y)):
            i = 3 * p + j
            if kind == "ici":
                cps.append(_remote(src_refs[p].at[rows, :], land.at[k_me, rows, :], ssem.at[i], rsem.at[i], (*chip, c)))
            else:
                got = land.at[2 * chip[0] + chip[1], rows, :]
                cps.append(_remote(got, got, ssem.at[i], rsem.at[i], (x, y, 1 - c)))
        if kind == "relay":
            cps.append(_remote(src_refs[p], land.at[k_me], ssem.at[3 * n + p], rsem.at[3 * n + p], (x, y, 1 - c)))
    return cps


def gather_now(srcs, conv):
    n = len(srcs)

    def body(*refs):
        src_refs, conv_ref = refs[:n], refs[n]
        lands, gconv = refs[n + 1:2 * n + 1], refs[2 * n + 1]
        ssem, rsem, fsem, frsem, csem, crsem = refs[2 * n + 2:]
        x, y, c = _coords()
        k_me = 2 * x + y
        targets = [(*chip, c) for chip in _peer_chips(x, y)] + [(x, y, 1 - c)]
        ici = _gather_copies("ici", src_refs, lands, ssem, rsem)
        relay = _gather_copies("relay", src_refs, lands, fsem, frsem)
        passed = [cp for i, cp in enumerate(relay) if i % 4 != 3]
        own = relay[3::4]
        conv_cps = [_remote(conv_ref, gconv.at[k_me], csem.at[j], crsem.at[j], t) for j, t in enumerate(targets)]
        for cp in ici + conv_cps + own:
            cp.start()
        for cp, fw in zip(ici, passed):
            cp.wait_recv()
            fw.start()
        for cp in conv_cps + relay:
            cp.wait_recv()
        for cp in ici + relay + conv_cps:
            cp.wait_send()

    out_shape = [jax.ShapeDtypeStruct((N_CHIPS,) + s.shape, s.dtype) for s in srcs]
    out_shape.append(jax.ShapeDtypeStruct((N_CHIPS,) + conv.shape, conv.dtype))
    sems = lambda k: pltpu.SemaphoreType.DMA((k,))
    return pl.pallas_call(
        body, name="gather_now", out_shape=out_shape, in_specs=[HBM] * (n + 1), out_specs=[HBM] * (n + 1),
        scratch_shapes=[sems(3 * n), sems(3 * n), sems(4 * n), sems(4 * n), sems(N_CHIPS), sems(N_CHIPS)],
    )(*srcs, conv)


def _gather_maker(kind, n_src):
    def make(refs, ssem, rsem):
        cps = _gather_copies(kind, refs[:n_src], refs[n_src:], ssem, rsem)
        return cps, cps
    return make


def _scatter_maker(n):
    def make(refs, ssem, rsem):
        x, y, c = _coords()
        k_me = 2 * x + y
        sends, arrivals = [], []
        for p in range(n):
            src, land = refs[p], refs[n + p]
            sends.append(_remote(src.at[k_me, 1 - c], land.at[0], ssem.at[7 * p], rsem.at[7 * p], (x, y, 1 - c)))
            for j, chip in enumerate(_peer_chips(x, y)):
                for cc in range(2):
                    sends.append(_remote(src.at[2 * chip[0] + chip[1], cc], land.at[1 + 2 * j + c],
                                         ssem.at[7 * p + 1 + 2 * j + cc], rsem.at[7 * p + 1 + 2 * j + c], (*chip, cc)))
            for s in range(7):
                arrivals.append(_remote(land.at[s], land.at[s], ssem.at[7 * p + s], rsem.at[7 * p + s], (x, y, 1 - c)))
        return sends, arrivals
    return make


def _share_maker(n):
    def make(refs, ssem, rsem):
        x, y, c = _coords()
        sends = [_remote(refs[p].at[c], refs[p].at[c], ssem.at[p], rsem.at[p], (x, y, 1 - c)) for p in range(n)]
        arrivals = [_remote(refs[p].at[1 - c], refs[p].at[1 - c], ssem.at[p], rsem.at[p], (x, y, 1 - c)) for p in range(n)]
        return sends, arrivals
    return make


def split_start(name, make, n_sems, operands, after):
    n = len(operands)

    def body(*refs):
        ssem, rsem, token = refs[n + 1], refs[n + 2], refs[-1]
        for cp in make(refs[:n], ssem, rsem)[0]:
            cp.start()
        token[...] = jnp.zeros_like(token)

    ops = [pltpu.with_memory_space_constraint(a, pltpu.HBM) for a in operands]
    outs = pl.pallas_call(
        body, name=name,
        out_shape=(pltpu.SemaphoreType.DMA((n_sems,)), pltpu.SemaphoreType.DMA((n_sems,)),
                   *[pltpu.HBM(a.shape, a.dtype) for a in ops], jax.ShapeDtypeStruct((8, LANE), f32)),
        in_specs=[HBM] * n + [ANY], out_specs=(SEM, SEM, *[HBM] * n, pl.BlockSpec(memory_space=pltpu.VMEM)),
        input_output_aliases={i: 2 + i for i in range(n)},
        compiler_params=pltpu.CompilerParams(has_side_effects=DATAFLOW),
    )(*ops, after)
    return dict(name=name, make=make, ssem=outs[0], rsem=outs[1], operands=outs[2:2 + n], token=outs[-1][0, 0],
                tokens=outs[-1])


def split_wait(handle, after):
    n = len(handle["operands"])

    def body(*refs):
        sends, arrivals = handle["make"](refs[:n], refs[n], refs[n + 1])
        for cp in sends:
            cp.wait_send()
        for cp in arrivals:
            cp.wait_recv()

    outs = pl.pallas_call(
        body, name=handle["name"].replace("start", "wait"),
        out_shape=tuple(pltpu.HBM(a.shape, a.dtype) for a in handle["operands"]),
        in_specs=[HBM] * n + [SEM, SEM, ANY], out_specs=tuple([HBM] * n),
        input_output_aliases={i: i for i in range(n)},
        compiler_params=pltpu.CompilerParams(has_side_effects=DATAFLOW),
    )(*handle["operands"], handle["ssem"], handle["rsem"], after)
    return list(outs)


def piece_sum(g, recv, kc_arr):
    _, _, rb, cc = g.shape
    tr = min(256, rb)

    def body(kc_ref, g_ref, r_ref, o_ref):
        acc = g_ref[...].astype(f32)
        for s in range(7):
            acc = acc + r_ref[s].astype(f32)
        o_ref[...] = acc

    return pl.pallas_call(
        body, name="piece_sum",
        grid_spec=pltpu.PrefetchScalarGridSpec(
            num_scalar_prefetch=1, grid=(rb // tr,),
            in_specs=[pl.BlockSpec((None, None, tr, cc), lambda r, kc: (kc[0], kc[1], r, 0)),
                      pl.BlockSpec((7, tr, cc), lambda r, kc: (0, r, 0))],
            out_specs=pl.BlockSpec((None, tr, cc), lambda r, kc: (kc[1], r, 0))),
        out_shape=jax.ShapeDtypeStruct((2, rb, cc), f32),
        compiler_params=_params("arbitrary"),
    )(kc_arr, g, recv)


def small_all_reduce(vec):
    def body(v_ref, o_ref, gat, ssem, rsem):
        x, y, c = _coords()
        me = 4 * x + 2 * y + c
        gat[me] = v_ref[...]
        sends = []
        for t in range(1, N_DEVICES):
            peer = (x ^ (t >> 2), y ^ ((t >> 1) & 1), c ^ (t & 1))
            cp = _remote(v_ref, gat.at[me], ssem.at[t - 1], rsem.at[t - 1], peer)
            cp.start()
            sends.append(cp)
        for t in range(1, N_DEVICES):
            peer = (x ^ (t >> 2), y ^ ((t >> 1) & 1), c ^ (t & 1))
            slot = gat.at[4 * peer[0] + 2 * peer[1] + peer[2]]
            _remote(slot, slot, ssem.at[t - 1], rsem.at[t - 1], peer).wait_recv()
        for cp in sends:
            cp.wait_send()
        acc = gat[0]
        for d in range(1, N_DEVICES):
            acc = acc + gat[d]
        o_ref[...] = acc

    return pl.pallas_call(
        body, name="small_all_reduce", out_shape=jax.ShapeDtypeStruct(vec.shape, vec.dtype),
        in_specs=[pl.BlockSpec(memory_space=pltpu.VMEM)], out_specs=pl.BlockSpec(memory_space=pltpu.VMEM),
        scratch_shapes=[pltpu.VMEM((N_DEVICES,) + vec.shape, vec.dtype), pltpu.SemaphoreType.DMA((N_DEVICES - 1,)),
                        pltpu.SemaphoreType.DMA((N_DEVICES - 1,))],
    )(vec)


def _adamw_math(w, g, m, v):
    m_new = ADAM_B1 * m + (1.0 - ADAM_B1) * g
    v_new = ADAM_B2 * v + (1.0 - ADAM_B2) * jnp.square(g)
    m_hat = m_new / (1.0 - ADAM_B1 ** ADAM_STEP)
    v_hat = v_new / (1.0 - ADAM_B2 ** ADAM_STEP)
    delta = -ADAM_LR * (m_hat / (jnp.sqrt(v_hat) + ADAM_EPS) + ADAM_WD * w)
    return delta, m_new, v_new


def adamw_shard(w, g0, g1, m, v):
    depth, rows, cols = w.shape
    half = rows // 2
    tr = min(256, half)
    nr = half // tr

    def body(w_ref, g0_ref, g1_ref, m_ref, v_ref, go_ref, d_ref, nm_ref, nv_ref):
        gv = jnp.where(pl.program_id(0) == 0, g0_ref[...], g1_ref[...])
        go_ref[...] = gv
        d_ref[...], nm_ref[...], nv_ref[...] = _adamw_math(w_ref[...], gv, m_ref[...], v_ref[...])

    spec = pl.BlockSpec((None, tr, cols), lambda l, h, r: (l, h * nr + r, 0))
    g0spec = pl.BlockSpec((None, tr, cols), lambda l, h, r: (jnp.where(l == 0, h, 1), jnp.where(l == 0, r, nr - 1), 0))
    g1spec = pl.BlockSpec((None, tr, cols), lambda l, h, r: (jnp.where(l == 1, h, 0), jnp.where(l == 1, r, 0), 0))
    return pl.pallas_call(
        body, name="adamw_shard", grid=(depth, 2, nr), in_specs=[spec, g0spec, g1spec, spec, spec], out_specs=[spec] * 4,
        out_shape=[jax.ShapeDtypeStruct(w.shape, f32)] * 4,
        compiler_params=_params("arbitrary", "arbitrary", "arbitrary"),
    )(w, g0, g1, m, v)


def adamw_cols(w, g, m, v, tc=34):
    cols, depth, rows = w.shape

    def body(w_ref, g_ref, m_ref, v_ref, d_ref, nm_ref, nv_ref):
        d_ref[...], nm_ref[...], nv_ref[...] = _adamw_math(w_ref[...], g_ref[...], m_ref[...], v_ref[...])

    spec = pl.BlockSpec((tc, depth, rows), lambda i: (i, 0, 0))
    return pl.pallas_call(
        body, name="adamw_cols", grid=(cols // tc,), in_specs=[spec] * 4, out_specs=[spec] * 3,
        out_shape=[jax.ShapeDtypeStruct(w.shape, f32)] * 3,
        compiler_params=_params("arbitrary"),
    )(w, g, m, v)


def adamw_small(ws, gs, ms, vs):
    n = len(ws)

    def body(*refs):
        ins, outs = refs[:4 * n], refs[4 * n:]
        for i in range(n):
            w_ref, g_ref, m_ref, v_ref = (ins[k * n + i] for k in range(4))
            outs[i][...], outs[n + i][...], outs[2 * n + i][...] = _adamw_math(w_ref[...], g_ref[...], m_ref[...], v_ref[...])

    outs = pl.pallas_call(
        body, name="adamw_small", out_shape=[jax.ShapeDtypeStruct(w.shape, f32) for w in ws] * 3,
    )(*ws, *gs, *ms, *vs)
    return outs[:n], outs[n:2 * n], outs[2 * n:]


WEIGHTS = ("mix_norm_g", "w_in", "q_gain", "k_gain", "sinks", "rel_bias", "conv_w", "conv_b", "dt_bias", "a_log", "d_skip",
           "ssm_norm_g", "w_out", "mlp_norm_g", "w_up", "w_down")
BIG = ("w_in", "w_out", "w_up", "w_down")
SMALL = tuple(n for n in WEIGHTS if n not in BIG)
PACK_COLS = 1024
PACK_ROWS = 16


def _pack(named, last=None):
    flat = jnp.concatenate([named[n].reshape(-1) for n in SMALL])
    tail = jnp.zeros((1,), f32) if last is None else last.reshape(1)
    pad = jnp.zeros((PACK_ROWS * PACK_COLS - flat.shape[0] - 1,), f32)
    return jnp.concatenate([flat, pad, tail]).reshape(PACK_ROWS, PACK_COLS)


def _unpack(buf, shapes):
    flat = buf.reshape(-1)
    out, at = {}, 0
    for n in SMALL:
        size = int(np.prod(shapes[n]))
        out[n] = flat[at:at + size].reshape(shapes[n])
        at += size
    return out


class _Exchange:
    GROUPS = {"A": (("w_up", 0), ("w_down", 0)), "B": (("w_in", 1), ("w_out", 1)), "C": (("w_up", 1), ("w_down", 1))}
    ICI_AT = {("mid", 0): "B", ("pre_out", 0): "C"}
    RELAY_AT = {("pre_out", 0): "A", ("pre_mlp", 0): "B", ("mid", 1): "C"}
    LAST = ("mix", 0)
    IN_FLIGHT = 2

    def __init__(self, wts, kc_arr):
        self.wts, self.kc_arr = wts, kc_arr
        self.own = {(n, l): wts[n][l].astype(bf16) for n in BIG for l in range(DEPTH)}
        now = gather_now([self.own["w_in", 0], self.own["w_out", 0]], wts["conv_w"])
        self.ready = {("w_in", 0): now[0], ("w_out", 0): now[1]}
        self.conv_w = jnp.transpose(now[2], (1, 2, 0, 3)).reshape(DEPTH, CONV_WIDTH, D_CONV)
        self.ici, self.relay = {}, {}
        self.scatter, self.share, self.reduced = [], [], {}
        self._start_ici("A", now[2])

    def _start_ici(self, g, after):
        srcs = [self.own[p] for p in self.GROUPS[g]]
        lands = [lax.empty((N_CHIPS,) + s.shape, s.dtype) for s in srcs]
        self.ici[g] = split_start("gather%s_ici_start" % g, _gather_maker("ici", len(srcs)), 3 * len(srcs), srcs + lands,
                                  after)
        return self.ici[g]["token"]

    def stage(self, name, after):
        if name == ("begin", 0):
            return self.ici["A"]["token"]
        tok = 0.0
        g = self.RELAY_AT.get(name)
        if g is not None:
            n = len(self.GROUPS[g])
            self.relay[g] = split_start("gather%s_relay_start" % g, _gather_maker("relay", n), 4 * n,
                                        split_wait(self.ici[g], after), after)
            tok = self.relay[g]["token"]
        if name in self.ICI_AT:
            tok = tok + self._start_ici(self.ICI_AT[name], after)
        return tok

    def _get(self, piece, after):
        if piece not in self.ready:
            g = [k for k, pieces in self.GROUPS.items() if piece in pieces][0]
            lands = split_wait(self.relay[g], after)[len(self.GROUPS[g]):]
            self.ready.update(zip(self.GROUPS[g], lands))
        return self.ready[piece]

    def w_in(self, l, after):
        return align_w_in(self._get(("w_in", l), after))

    def w_out(self, l, after):
        return self._get(("w_out", l), after).reshape(D_MODEL, D_MODEL)

    def mlp(self, l, after):
        return self._get(("w_up", l), after), self._get(("w_down", l), after)

    def _view(self, n, g):
        _, rows, cols = self.wts[n].shape
        return g.reshape(N_CHIPS, 2, rows // 2, cols)

    def grads(self, name, arrays, after):
        if name == self.LAST:
            self.held = (name, arrays)
            return 0.0
        return self._scatter(name, arrays, after) + self._advance(after, self.IN_FLIGHT)

    def flush(self, after):
        return self._scatter(*self.held, after) + self._advance(after, self.IN_FLIGHT)

    def _scatter(self, name, arrays, after):
        pieces = [(n, name[1]) for n in arrays]
        views = [self._view(n, g) for n, g in arrays.items()]
        lands = [lax.empty((7,) + v.shape[2:], bf16) for v in views]
        h = split_start("scatter_%s%d_start" % name, _scatter_maker(len(views)), 7 * len(views), views + lands, after)
        self.scatter.append((pieces, h))
        return h["token"]

    def _take_share(self, after):
        pieces, h = self.share.pop(0)
        self.reduced.update(zip(pieces, split_wait(h, after)))

    def _take_scatter(self, after):
        pieces, h = self.scatter.pop(0)
        done = split_wait(h, after)
        views, lands = done[:len(pieces)], done[len(pieces):]
        sums = [piece_sum(v, land, self.kc_arr) for v, land in zip(views, lands)]
        hs = split_start(h["name"].replace("scatter", "share"), _share_maker(len(sums)), len(sums), sums, after)
        self.share.append((pieces, hs))
        return hs["token"]

    def _advance(self, after, newest):
        if self.share:
            self._take_share(after)
        return self._take_scatter(after) if len(self.scatter) > newest else 0.0

    def prepare(self, piece, after):
        while piece not in self.reduced and not any(piece in pieces for pieces, _ in self.share):
            self._take_scatter(after)
        return self.share[-1][1]["tokens"] if self.share else after

    def reduced_piece(self, piece, after):
        while piece not in self.reduced:
            if any(piece in pieces for pieces, _ in self.share):
                self._take_share(after)
            else:
                self._take_scatter(after)
        return self.reduced[piece]


def kernel(x, mix_norm_g, w_in, q_gain, k_gain, sinks, rel_bias, conv_w, conv_b, dt_bias, a_log, d_skip, ssm_norm_g, w_out, mlp_norm_g, w_up, w_down, loss_target, m_mix_norm_g, m_w_in, m_q_gain, m_k_gain, m_sinks, m_rel_bias, m_conv_w, m_conv_b, m_dt_bias, m_a_log, m_d_skip, m_ssm_norm_g, m_w_out, m_mlp_norm_g, m_w_up, m_w_down, v_mix_norm_g, v_w_in, v_q_gain, v_k_gain, v_sinks, v_rel_bias, v_conv_w, v_conv_b, v_dt_bias, v_a_log, v_d_skip, v_ssm_norm_g, v_w_out, v_mlp_norm_g, v_w_up, v_w_down):
    wts = dict(mix_norm_g=mix_norm_g, w_in=w_in, q_gain=q_gain, k_gain=k_gain, sinks=sinks, rel_bias=rel_bias, conv_w=conv_w,
               conv_b=conv_b, dt_bias=dt_bias, a_log=a_log, d_skip=d_skip, ssm_norm_g=ssm_norm_g, w_out=w_out,
               mlp_norm_g=mlp_norm_g, w_up=w_up, w_down=w_down)
    mom = dict(mix_norm_g=m_mix_norm_g, w_in=m_w_in, q_gain=m_q_gain, k_gain=m_k_gain, sinks=m_sinks, rel_bias=m_rel_bias,
               conv_w=m_conv_w, conv_b=m_conv_b, dt_bias=m_dt_bias, a_log=m_a_log, d_skip=m_d_skip, ssm_norm_g=m_ssm_norm_g,
               w_out=m_w_out, mlp_norm_g=m_mlp_norm_g, w_up=m_w_up, w_down=m_w_down)
    var = dict(mix_norm_g=v_mix_norm_g, w_in=v_w_in, q_gain=v_q_gain, k_gain=v_k_gain, sinks=v_sinks, rel_bias=v_rel_bias,
               conv_w=v_conv_w, conv_b=v_conv_b, dt_bias=v_dt_bias, a_log=v_a_log, d_skip=v_d_skip, ssm_norm_g=v_ssm_norm_g,
               w_out=v_w_out, mlp_norm_g=v_mlp_norm_g, w_up=v_w_up, w_down=v_w_down)
    xi, yi, ci = _coords()
    k_me = 2 * xi + yi
    kc_arr = jnp.stack([k_me, ci]).astype(jnp.int32)

    prov = _Exchange(wts, kc_arr)
    small_w = {n: wts[n] for n in SMALL}
    small_w["conv_w"] = prov.conv_w
    loss, dx, grads, tok = local_step(x[0], loss_target[0], small_w, prov)

    small_shapes = {n: grads[n].shape for n in SMALL}
    small_sum = small_all_reduce(_pack(grads, loss) + tok)
    loss = small_sum[PACK_ROWS - 1, PACK_COLS - 1]
    tok = prov.flush(small_sum)
    small = _unpack(small_sum, small_shapes)
    cols = conv_w.shape[-1]
    small["conv_w"] = lax.dynamic_slice_in_dim(small["conv_w"], k_me * cols, cols, axis=2)
    g_out_d = dict(small)
    gs = [small[n] for n in SMALL]
    gs[0] = gs[0] + tok
    ds, nms, nvs = adamw_small([wts[n] for n in SMALL], gs, [mom[n] for n in SMALL], [var[n] for n in SMALL])
    d_out_d, m_out_d, v_out_d = dict(zip(SMALL, ds)), dict(zip(SMALL, nms)), dict(zip(SMALL, nvs))

    rows, cols = wts["w_in"].shape[1:]
    to_cols = lambda a: jnp.transpose(a, (2, 0, 1))
    place = lambda g_t, l, g, pin: g_t.at[:, l, :].set(to_cols(g).reshape(cols, rows) + pin)
    pin = prov.prepare(("w_up", 0), ds[0])
    g_t = place(lax.empty((cols, DEPTH, rows), f32), 1, prov.reduced_piece(("w_in", 1), ds[0]), pin[0, 0])
    after = g_t
    for n in ("w_up", "w_down", "w_in", "w_out"):
        g0, g1 = (prov.reduced_piece((n, l), after) for l in range(DEPTH))
        if n == "w_in":
            g_t = place(g_t, 0, g0, 0.0)
            res_t = adamw_cols(to_cols(wts[n]), g_t, to_cols(mom[n]), to_cols(var[n]))
            g_out_d[n], d_out_d[n], m_out_d[n], v_out_d[n] = (jnp.transpose(a, (1, 2, 0)) for a in (g_t, *res_t))
        else:
            g_out_d[n], d_out_d[n], m_out_d[n], v_out_d[n] = adamw_shard(wts[n], g0, g1, mom[n], var[n])
        after = d_out_d[n]

    return (loss, dx[None], *[g_out_d[n] for n in WEIGHTS], *[d_out_d[n] for n in WEIGHTS],
            *[m_out_d[n] for n in WEIGHTS], *[v_out_d[n] for n in WEIGHTS])
```

```python
import numpy as np
import jax
import jax.numpy as jnp
from jax import lax
from jax.experimental import pallas as pl
from jax.experimental.pallas import tpu as pltpu

f32 = jnp.float32
bf16 = jnp.bfloat16

SEQ = 2048
D_MODEL = 1024
DEPTH = 2
HEAD_DIM = 64
N_Q_HEADS = 8
N_KV_HEADS = 2
Q_PER_KV = N_Q_HEADS // N_KV_HEADS
BLOCK = 128
N_BLOCKS = SEQ // BLOCK
N_BUCKETS = 32
MAX_DISTANCE = 128
SSM_HEADS = 8
SSM_HEAD_DIM = 64
SSM_GROUPS = 2
HEADS_PER_GROUP = SSM_HEADS // SSM_GROUPS
SSM_STATE = 128
CONV_WIDTH = 4
CHUNK = 128
N_CHUNKS = SEQ // CHUNK
D_FF = 4 * D_MODEL
D_ATTN = N_Q_HEADS * HEAD_DIM
D_KV = N_KV_HEADS * HEAD_DIM
D_SSM = SSM_HEADS * SSM_HEAD_DIM
D_BC = SSM_GROUPS * SSM_STATE
D_CONV = D_SSM + 2 * D_BC
D_IN = D_ATTN + 2 * D_KV + D_SSM + D_CONV + SSM_HEADS
EPS = 1e-6
NEG = -1e30
N_CHIPS = 4
FF_TILE = D_FF // N_CHIPS

LANE = 128
PW = D_ATTN + D_SSM + D_CONV + 2 * D_KV + LANE
OFF_Q, OFF_Z, OFF_X, OFF_K, OFF_V, OFF_DT = 0, 512, 1024, 2048, 2176, 2304

ADAM_LR = 0.001
ADAM_B1 = 0.9
ADAM_B2 = 0.999
ADAM_EPS = 1e-08
ADAM_WD = 0.01
ADAM_STEP = 10

VMEM_LIMIT = 56 * 1024 * 1024


def _params(*sem):
    return pltpu.CompilerParams(dimension_semantics=tuple(sem), vmem_limit_bytes=VMEM_LIMIT)


def _bdot(a, b):
    return jnp.dot(a.astype(bf16), b.astype(bf16), preferred_element_type=f32)


def _bdot_nt(a, b):
    return lax.dot_general(a.astype(bf16), b.astype(bf16), (((1,), (1,)), ((), ())), preferred_element_type=f32)


def _bdot_tn(a, b):
    return lax.dot_general(a.astype(bf16), b.astype(bf16), (((0,), (0,)), ((), ())), preferred_element_type=f32)


def _hdot(a, b):
    return jnp.dot(a, b, precision=lax.Precision.HIGHEST, preferred_element_type=f32)


def _sigmoid(x):
    return 1.0 / (1.0 + jnp.exp(-x))


def _softplus(x):
    return jnp.maximum(x, 0.0) + jnp.log1p(jnp.exp(-jnp.abs(x)))


def _rms(x):
    return lax.rsqrt(jnp.mean(x * x, axis=-1, keepdims=True) + EPS)


def _rms_bwd(dy, xhat, r, g):
    t = dy * g
    return r * (t - xhat * jnp.mean(t * xhat, axis=-1, keepdims=True))


def _full(shape):
    return pl.BlockSpec(shape, lambda *_: (0,) * len(shape))


def _bucket_table():
    qi = np.arange(BLOCK)[:, None]
    kj = np.arange(2 * BLOCK)[None, :]
    dist = qi + BLOCK - kj
    ok = (dist >= 0) & (dist < 128)
    d = np.clip(dist, 0, None)
    max_exact = N_BUCKETS // 2
    d_f = np.maximum(d, 1).astype(np.float32)
    large = max_exact + (np.log(d_f / np.float32(max_exact)) / np.float32(np.log(MAX_DISTANCE / max_exact))
                         * np.float32(N_BUCKETS - max_exact)).astype(np.int32)
    large = np.minimum(large, N_BUCKETS - 1)
    bucket = np.where(d < max_exact, d, large)
    return np.where(ok, bucket, -1).astype(np.int32)


def bias_build(rel_bias, bucket):
    def body(rel_ref, bkt_ref, o_ref):
        bkt = bkt_ref[...]
        for h in range(N_Q_HEADS):
            acc = jnp.where(bkt < 0, NEG, 0.0).astype(f32)
            for b in range(N_BUCKETS):
                acc = acc + jnp.where(bkt == b, rel_ref[b, h], 0.0)
            o_ref[h] = acc

    return pl.pallas_call(
        body, name="bias_build", out_shape=jax.ShapeDtypeStruct((N_Q_HEADS,) + bucket.shape, f32),
        in_specs=[pl.BlockSpec(memory_space=pltpu.SMEM), pl.BlockSpec(memory_space=pltpu.VMEM)],
        out_specs=pl.BlockSpec(memory_space=pltpu.VMEM),
    )(rel_bias, bucket)


def bias_bwd(dband0, dband1, bucket):
    def body(d0_ref, d1_ref, bkt_ref, o_ref):
        bkt = bkt_ref[...]
        o_ref[...] = jnp.zeros_like(o_ref)
        for h in range(N_Q_HEADS):
            d = d0_ref[h] + d1_ref[h]
            for b in range(N_BUCKETS):
                part = jnp.sum(jnp.where(bkt == b, d, 0.0), axis=1, keepdims=True)
                o_ref[b:b + 1, h:h + 1] = jnp.sum(part, axis=0, keepdims=True)

    return pl.pallas_call(
        body, name="bias_bwd", out_shape=jax.ShapeDtypeStruct((N_BUCKETS, LANE), f32),
    )(dband0, dband1, bucket)


W_IN_SHARD = D_IN // N_CHIPS
_ALIGNED_PIECES = ((0, 0, 512), (1, 190, 578), (2, 0, 124), (2, 124, 578), (3, 0, 570), (0, 512, 578), (1, 0, 62),
                   (1, 62, 190), (3, 570, 578))
_SHARD_PIECES = (((0, 512), (2048, 2114)), ((2114, 2176), (2176, 2304), (512, 900)), ((900, 1024), (1024, 1478)),
                 ((1478, 2048), (2304, 2312)))


def align_w_in(shards, tr=256):
    def body(s_ref, o_ref):
        parts = [s_ref[k, :, a:b] for k, a, b in _ALIGNED_PIECES]
        parts.append(jnp.zeros((tr, LANE - SSM_HEADS), s_ref.dtype))
        o_ref[...] = jnp.concatenate(parts, axis=-1)

    return pl.pallas_call(
        body, name="align_w_in", grid=(D_MODEL // tr,),
        in_specs=[pl.BlockSpec((N_CHIPS, tr, W_IN_SHARD), lambda i: (0, i, 0))],
        out_specs=pl.BlockSpec((tr, PW), lambda i: (i, 0)),
        out_shape=jax.ShapeDtypeStruct((D_MODEL, PW), shards.dtype),
        compiler_params=_params("arbitrary"),
    )(shards)


def split_w_in_grad(dw, tr=256):
    def body(d_ref, o16_ref):
        for k, pieces in enumerate(_SHARD_PIECES):
            o16_ref[k] = jnp.concatenate([d_ref[:, a:b] for a, b in pieces], axis=-1).astype(bf16)

    return pl.pallas_call(
        body, name="split_w_in_grad", grid=(D_MODEL // tr,),
        in_specs=[pl.BlockSpec((tr, PW), lambda i: (i, 0))],
        out_specs=pl.BlockSpec((N_CHIPS, tr, W_IN_SHARD), lambda i: (0, i, 0)),
        out_shape=jax.ShapeDtypeStruct((N_CHIPS, D_MODEL, W_IN_SHARD), bf16),
        compiler_params=_params("arbitrary"),
    )(dw)

def in_fwd(x, g, w, tm=512):
    def body(x_ref, g_ref, w_ref, o_ref):
        xv = x_ref[...]
        h = xv * _rms(xv) * g_ref[...]
        o_ref[...] = _bdot(h, w_ref[...])

    return pl.pallas_call(
        body, name="in_fwd", grid=(SEQ // tm,),
        in_specs=[pl.BlockSpec((tm, D_MODEL), lambda i: (i, 0)), _full((1, D_MODEL)), _resident((D_MODEL, PW))],
        out_specs=pl.BlockSpec((tm, PW), lambda i: (i, 0)),
        out_shape=jax.ShapeDtypeStruct((SEQ, PW), f32),
        compiler_params=_params("arbitrary"),
    )(x, g, w)


def _resident(shape):
    return pl.BlockSpec(shape, lambda *_: (0,) * len(shape), pipeline_mode=pl.Buffered(1))


def in_bwd(dq, dz, dxbc, dk, dv, ddt, x, g, w, dres, tm=512):
    def body(dq_ref, dz_ref, dx_ref, dk_ref, dv_ref, ddt_ref, x_ref, g_ref, w_ref, dres_ref, o_ref, dw_ref, dg_ref):
        i = pl.program_id(0)

        @pl.when(i == 0)
        def _():
            dw_ref[...] = jnp.zeros_like(dw_ref)
            dg_ref[...] = jnp.zeros_like(dg_ref)

        dproj = jnp.concatenate([dq_ref[...], dz_ref[...], dx_ref[...], dk_ref[...], dv_ref[...], ddt_ref[...]],
                                axis=-1).astype(bf16)
        xv = x_ref[...]
        r = _rms(xv)
        xhat = xv * r
        gv = g_ref[...]
        h = xhat * gv
        dw_ref[...] += _bdot_tn(h, dproj)
        dh = _bdot_nt(dproj, w_ref[...])
        dg_ref[...] += jnp.sum(dh * xhat, axis=0, keepdims=True)
        o_ref[...] = dres_ref[...] + _rms_bwd(dh, xhat, r, gv)

    tok = lambda w_: pl.BlockSpec((tm, w_), lambda i: (i, 0))
    return pl.pallas_call(
        body, name="in_bwd", grid=(SEQ // tm,),
        in_specs=[tok(D_ATTN), tok(D_SSM), tok(D_CONV), tok(D_KV), tok(D_KV), tok(LANE), tok(D_MODEL),
                  _full((1, D_MODEL)), _resident((D_MODEL, PW)), tok(D_MODEL)],
        out_specs=[tok(D_MODEL), _resident((D_MODEL, PW)), _full((1, D_MODEL))],
        out_shape=[jax.ShapeDtypeStruct((SEQ, D_MODEL), f32), jax.ShapeDtypeStruct((D_MODEL, PW), f32),
                   jax.ShapeDtypeStruct((1, D_MODEL), f32)],
        compiler_params=_params("arbitrary"),
    )(dq, dz, dxbc, dk, dv, ddt, x, g, w, dres)


def _attn_softmax_t(qk, bias_t, sink, first, key_row):
    s = qk * (HEAD_DIM ** -0.5) + bias_t
    s = jnp.where(jnp.logical_and(first, key_row < BLOCK), NEG, s)
    m = jnp.maximum(jnp.max(s, axis=0, keepdims=True), sink)
    p = jnp.exp(s - m)
    psink = jnp.exp(sink - m)
    inv = 1.0 / (jnp.sum(p, axis=0, keepdims=True) + psink)
    return p * inv, psink * inv


def _rms_t(x_t):
    return lax.rsqrt(jnp.mean(x_t * x_t, axis=0, keepdims=True) + EPS)


def attn_fwd_t(proj, q_gain_col, k_gain, sinks, bias_t):
    kcol, vcol = OFF_K // D_KV, OFF_V // D_KV

    def body(q_ref, kc_ref, kp_ref, vc_ref, vp_ref, qg_ref, kg_ref, sink_ref, bias_ref, o_ref, ot_scr):
        n = pl.program_id(0)
        first = n == 0
        key_row = lax.broadcasted_iota(jnp.int32, (2 * BLOCK, BLOCK), 0)
        k2 = jnp.concatenate([kp_ref[...], kc_ref[...]], axis=0)
        v_t = jnp.concatenate([vp_ref[...], vc_ref[...]], axis=0).T
        q_t = q_ref[...].T
        qg = jnp.broadcast_to(qg_ref[...], (HEAD_DIM, BLOCK))
        kg = kg_ref[...]
        for hk in range(N_KV_HEADS):
            sl = slice(hk * HEAD_DIM, (hk + 1) * HEAD_DIM)
            kk = k2[:, sl]
            kn = (kk * _rms(kk) * kg).astype(bf16)
            vt = v_t[sl, :].astype(bf16)
            heads = range(hk * Q_PER_KV, (hk + 1) * Q_PER_KV)
            qns = []
            for h in heads:
                qh = q_t[h * HEAD_DIM:(h + 1) * HEAD_DIM, :]
                qns.append(qh * _rms_t(qh) * qg)
            scores = [_bdot(kn, qn) for qn in qns]
            for h, s in zip(heads, scores):
                p, _ = _attn_softmax_t(s, bias_ref[h], sink_ref[h], first, key_row)
                ot_scr[h * HEAD_DIM:(h + 1) * HEAD_DIM, :] = _bdot(vt, p)
        o_ref[...] = ot_scr[...].T

    prev = lambda n: jnp.maximum(n - 1, 0)
    return pl.pallas_call(
        body, name="attn_fwd", grid=(N_BLOCKS,),
        in_specs=[pl.BlockSpec((BLOCK, D_ATTN), lambda n: (n, 0)),
                  pl.BlockSpec((BLOCK, D_KV), lambda n: (n, kcol)), pl.BlockSpec((BLOCK, D_KV), lambda n: (prev(n), kcol)),
                  pl.BlockSpec((BLOCK, D_KV), lambda n: (n, vcol)), pl.BlockSpec((BLOCK, D_KV), lambda n: (prev(n), vcol)),
                  _full((HEAD_DIM, 1)), _full((1, HEAD_DIM)), pl.BlockSpec(memory_space=pltpu.SMEM),
                  _full((N_Q_HEADS, 2 * BLOCK, BLOCK))],
        out_specs=pl.BlockSpec((BLOCK, D_ATTN), lambda n: (n, 0)),
        out_shape=jax.ShapeDtypeStruct((SEQ, D_ATTN), f32),
        scratch_shapes=[pltpu.VMEM((D_ATTN, BLOCK), f32)],
        compiler_params=_params("arbitrary"),
    )(proj, proj, proj, proj, proj, q_gain_col, k_gain, sinks, bias_t)


def attn_bwd_t(proj, d_out, q_gain_col, k_gain, sinks, bias_t):
    kcol, vcol = OFF_K // D_KV, OFF_V // D_KV

    def body(q_ref, kc_ref, kp_ref, vc_ref, vp_ref, do_ref, qg_ref, kg_ref, sink_ref, bias_ref,
             dq_ref, dk_ref, dv_ref, dband_ref, dsink_ref, dqg_ref, dkg_ref, dkn_scr, dv_scr, dqt_scr, dsink_acc, dqg_acc):
        i = pl.program_id(0)
        first = i == N_BLOCKS - 1

        @pl.when(i == 0)
        def _():
            for ref in (dband_ref, dkg_ref, dkn_scr, dv_scr, dsink_acc, dqg_acc):
                ref[...] = jnp.zeros_like(ref)

        key_row = lax.broadcasted_iota(jnp.int32, (2 * BLOCK, BLOCK), 0)
        k2 = jnp.concatenate([kp_ref[...], kc_ref[...]], axis=0)
        v2 = jnp.concatenate([vp_ref[...], vc_ref[...]], axis=0)
        q_t = q_ref[...].T
        do_t = do_ref[...].T
        qg = jnp.broadcast_to(qg_ref[...], (HEAD_DIM, BLOCK))
        kg = kg_ref[...]
        scale = HEAD_DIM ** -0.5
        for hk in range(N_KV_HEADS):
            sl = slice(hk * HEAD_DIM, (hk + 1) * HEAD_DIM)
            kk = k2[:, sl]
            rk = _rms(kk)
            khat = kk * rk
            kn = (khat * kg).astype(bf16)
            vb = v2[:, sl].astype(bf16)
            dkn = jnp.zeros((2 * BLOCK, HEAD_DIM), f32)
            dvv = jnp.zeros((2 * BLOCK, HEAD_DIM), f32)
            heads = range(hk * Q_PER_KV, (hk + 1) * Q_PER_KV)
            rqs, qhats, qns, d_os = [], [], [], []
            for h in heads:
                hs = slice(h * HEAD_DIM, (h + 1) * HEAD_DIM)
                qh = q_t[hs, :]
                rqs.append(_rms_t(qh))
                qhats.append(qh * rqs[-1])
                qns.append((qhats[-1] * qg).astype(bf16))
                d_os.append(do_t[hs, :].astype(bf16))
            scores = [_bdot(kn, qn) for qn in qns]
            dps = [_bdot(vb, d_o) for d_o in d_os]
            ps, dss = [], []
            for h, s, dp in zip(heads, scores, dps):
                p, psink = _attn_softmax_t(s, bias_ref[h], sink_ref[h], first, key_row)
                delta = jnp.sum(p * dp, axis=0, keepdims=True)
                ds = p * (dp - delta)
                dband_ref[h] += ds
                dsink_acc[h:h + 1, :] += -(psink * delta)
                ps.append(p.astype(bf16))
                dss.append(ds.astype(bf16))
            dqns = [_bdot_tn(kn, ds) * scale for ds in dss]
            for ds, qn, p, d_o in zip(dss, qns, ps, d_os):
                dkn = dkn + _bdot_nt(ds, qn) * scale
                dvv = dvv + _bdot_nt(p, d_o)
            for h, dqn, rq, qhat in zip(heads, dqns, rqs, qhats):
                dqg_acc[...] += dqn * qhat
                t = dqn * qg
                dqt_scr[h * HEAD_DIM:(h + 1) * HEAD_DIM, :] = rq * (t - qhat * jnp.mean(t * qhat, axis=0, keepdims=True))
            dkn_cur = dkn[BLOCK:] + dkn_scr[:, sl]
            dkn_scr[:, sl] = dkn[:BLOCK]
            khat_c, rk_c = khat[BLOCK:], rk[BLOCK:]
            dkg_ref[...] += jnp.sum(dkn_cur * khat_c, axis=0, keepdims=True)
            dk_ref[:, sl] = _rms_bwd(dkn_cur, khat_c, rk_c, kg)
            dv_ref[:, sl] = dvv[BLOCK:] + dv_scr[:, sl]
            dv_scr[:, sl] = dvv[:BLOCK]
        dq_ref[...] = dqt_scr[...].T

        @pl.when(i == N_BLOCKS - 1)
        def _():
            dsink_ref[...] = jnp.sum(dsink_acc[...], axis=1, keepdims=True)
            dqg_ref[...] = jnp.sum(dqg_acc[...], axis=1, keepdims=True)

    blk = lambda i: N_BLOCKS - 1 - i
    prev = lambda i: jnp.maximum(N_BLOCKS - 2 - i, 0)
    return pl.pallas_call(
        body, name="attn_bwd", grid=(N_BLOCKS,),
        in_specs=[pl.BlockSpec((BLOCK, D_ATTN), lambda i: (blk(i), 0)),
                  pl.BlockSpec((BLOCK, D_KV), lambda i: (blk(i), kcol)), pl.BlockSpec((BLOCK, D_KV), lambda i: (prev(i), kcol)),
                  pl.BlockSpec((BLOCK, D_KV), lambda i: (blk(i), vcol)), pl.BlockSpec((BLOCK, D_KV), lambda i: (prev(i), vcol)),
                  pl.BlockSpec((BLOCK, D_ATTN), lambda i: (blk(i), 0)),
                  _full((HEAD_DIM, 1)), _full((1, HEAD_DIM)), pl.BlockSpec(memory_space=pltpu.SMEM),
                  _full((N_Q_HEADS, 2 * BLOCK, BLOCK))],
        out_specs=[pl.BlockSpec((BLOCK, D_ATTN), lambda i: (blk(i), 0)), pl.BlockSpec((BLOCK, D_KV), lambda i: (blk(i), 0)),
                   pl.BlockSpec((BLOCK, D_KV), lambda i: (blk(i), 0)), _full((N_Q_HEADS, 2 * BLOCK, BLOCK)),
                   _full((N_Q_HEADS, 1)), _full((HEAD_DIM, 1)), _full((1, HEAD_DIM))],
        out_shape=[jax.ShapeDtypeStruct((SEQ, D_ATTN), f32), jax.ShapeDtypeStruct((SEQ, D_KV), f32),
                   jax.ShapeDtypeStruct((SEQ, D_KV), f32), jax.ShapeDtypeStruct((N_Q_HEADS, 2 * BLOCK, BLOCK), f32),
                   jax.ShapeDtypeStruct((N_Q_HEADS, 1), f32), jax.ShapeDtypeStruct((HEAD_DIM, 1), f32),
                   jax.ShapeDtypeStruct((1, HEAD_DIM), f32)],
        scratch_shapes=[pltpu.VMEM((BLOCK, D_KV), f32), pltpu.VMEM((BLOCK, D_KV), f32), pltpu.VMEM((D_ATTN, BLOCK), f32),
                        pltpu.VMEM((N_Q_HEADS, BLOCK), f32), pltpu.VMEM((HEAD_DIM, BLOCK), f32)],
        compiler_params=_params("arbitrary"),
    )(proj, proj, proj, proj, proj, d_out, q_gain_col, k_gain, sinks, bias_t)


SUBLANES = 8


def _shift_down(u, s, row8):
    if s == 0:
        return u
    r = pltpu.roll(u, s, 0)
    return jnp.concatenate([jnp.where(row8 >= s, r[:SUBLANES], 0.0), r[SUBLANES:]], axis=0)


def _shift_up(u, s, row8):
    if s == 0:
        return u
    r = pltpu.roll(u, SEQ - s, 0)
    return jnp.concatenate([r[:-SUBLANES], jnp.where(row8 < SUBLANES - s, r[-SUBLANES:], 0.0)], axis=0)


def conv_fwd(proj, conv_w, conv_b):
    xcol = OFF_X // LANE

    def body(u_ref, w_ref, b_ref, o_ref):
        u = u_ref[...]
        row = lax.broadcasted_iota(jnp.int32, (SUBLANES, LANE), 0)
        pre = b_ref[...] + jnp.zeros_like(u)
        for k in range(CONV_WIDTH):
            pre = pre + w_ref[k:k + 1, :] * _shift_down(u, CONV_WIDTH - 1 - k, row)
        o_ref[...] = pre * _sigmoid(pre)

    return pl.pallas_call(
        body, name="conv_fwd", grid=(D_CONV // LANE,),
        in_specs=[pl.BlockSpec((SEQ, LANE), lambda j: (0, xcol + j)), pl.BlockSpec((CONV_WIDTH, LANE), lambda j: (0, j)),
                  pl.BlockSpec((1, LANE), lambda j: (0, j))],
        out_specs=pl.BlockSpec((SEQ, LANE), lambda j: (0, j)),
        out_shape=jax.ShapeDtypeStruct((SEQ, D_CONV), f32),
        compiler_params=_params("arbitrary"),
    )(proj, conv_w, conv_b)


def conv_bwd(proj, d_act, conv_w, conv_b):
    xcol = OFF_X // LANE

    def body(u_ref, da_ref, w_ref, b_ref, du_ref, dw_ref, db_ref):
        u = u_ref[...]
        row = lax.broadcasted_iota(jnp.int32, (SUBLANES, LANE), 0)
        shifted = [_shift_down(u, CONV_WIDTH - 1 - k, row) for k in range(CONV_WIDTH)]
        pre = b_ref[...] + jnp.zeros_like(u)
        for k in range(CONV_WIDTH):
            pre = pre + w_ref[k:k + 1, :] * shifted[k]
        sg = _sigmoid(pre)
        dpre = da_ref[...] * (sg * (1.0 + pre * (1.0 - sg)))
        db_ref[...] = jnp.sum(dpre, axis=0, keepdims=True)
        du = jnp.zeros_like(u)
        for k in range(CONV_WIDTH):
            dw_ref[k:k + 1, :] = jnp.sum(dpre * shifted[k], axis=0, keepdims=True)
            du = du + w_ref[k:k + 1, :] * _shift_up(dpre, CONV_WIDTH - 1 - k, row)
        du_ref[...] = du

    return pl.pallas_call(
        body, name="conv_bwd", grid=(D_CONV // LANE,),
        in_specs=[pl.BlockSpec((SEQ, LANE), lambda j: (0, xcol + j)), pl.BlockSpec((SEQ, LANE), lambda j: (0, j)),
                  pl.BlockSpec((CONV_WIDTH, LANE), lambda j: (0, j)), pl.BlockSpec((1, LANE), lambda j: (0, j))],
        out_specs=[pl.BlockSpec((SEQ, LANE), lambda j: (0, j)), pl.BlockSpec((CONV_WIDTH, LANE), lambda j: (0, j)),
                   pl.BlockSpec((1, LANE), lambda j: (0, j))],
        out_shape=[jax.ShapeDtypeStruct((SEQ, D_CONV), f32), jax.ShapeDtypeStruct((CONV_WIDTH, D_CONV), f32),
                   jax.ShapeDtypeStruct((1, D_CONV), f32)],
        compiler_params=_params("arbitrary"),
    )(proj, d_act, conv_w, conv_b)


def _ssd_chunk_common(dt_raw, dtb, alog):
    row = lax.broadcasted_iota(jnp.int32, (CHUNK, CHUNK), 0)
    col = lax.broadcasted_iota(jnp.int32, (CHUNK, CHUNK), 1)
    tri = (row >= col).astype(f32)
    strict = (row > col).astype(f32)
    dtp = _softplus(dt_raw + dtb)
    a_row = -jnp.exp(alog)
    d_a = dtp * a_row
    cs = _hdot(tri, d_a)
    cs_last = cs[CHUNK - 1:CHUNK, :]
    return row, col, dtp, a_row, cs, cs.T, cs_last


def _seg_decay(cs, cs_t, hd, row, col):
    seg = cs[:, hd:hd + 1] - cs_t[hd:hd + 1, :]
    return jnp.where(row >= col, jnp.exp(seg), 0.0)


GROUP_W = HEADS_PER_GROUP * SSM_HEAD_DIM


def _group_indicator(g):
    j = lax.broadcasted_iota(jnp.int32, (GROUP_W, LANE), 0)
    lane = lax.broadcasted_iota(jnp.int32, (GROUP_W, LANE), 1)
    return (lane == g * HEADS_PER_GROUP + j // SSM_HEAD_DIM).astype(bf16)


def _bf16_pieces(a, n):
    pieces = []
    for _ in range(n):
        p = a.astype(bf16)
        pieces.append(p)
        a = a - p.astype(f32)
    return pieces


def _head_spread(a, ind):
    return sum(lax.dot_general(p, ind, (((1,), (1,)), ((), ())), preferred_element_type=f32) for p in _bf16_pieces(a, 3))


def _head_sums(a, ind):
    return sum(jnp.dot(p, ind, preferred_element_type=f32) for p in _bf16_pieces(a, 2))


def ssd_fwd_g(act, proj, dt_bias, a_log, d_skip, norm_g):
    zcol, dtcol = OFF_Z // D_SSM, OFF_DT // LANE

    def body(act_ref, z_ref, dt_ref, dtb_ref, alog_ref, dsk_ref, ng_ref, out_ref, ypre_ref, st_ref, state):
        c = pl.program_id(0)

        @pl.when(c == 0)
        def _():
            state[...] = jnp.zeros_like(state)

        row, col, dtp, a_row, cs, cs_t, cs_last = _ssd_chunk_common(dt_ref[...], dtb_ref[...], alog_ref[...])
        e_cs = jnp.exp(cs)
        dte = jnp.exp(cs_last - cs)
        rows8 = jnp.concatenate([jnp.exp(cs_last), dsk_ref[...], jnp.zeros((6, LANE), f32)], axis=0)
        z = z_ref[...]
        sz = z * _sigmoid(z)
        ng = ng_ref[...]
        for g in range(SSM_GROUPS):
            gs = slice(g * GROUP_W, (g + 1) * GROUP_W)
            ind = _group_indicator(g)
            xg = act_ref[:, gs]
            bg = act_ref[:, D_SSM + g * SSM_STATE:D_SSM + (g + 1) * SSM_STATE]
            cg = act_ref[:, D_SSM + D_BC + g * SSM_STATE:D_SSM + D_BC + (g + 1) * SSM_STATE]
            dt_e, e_e, dte_e = _head_spread(dtp, ind), _head_spread(e_cs, ind), _head_spread(dte, ind)
            rows_e = _head_spread(rows8, ind)
            ecl_e, dsk_e = rows_e[0:1], rows_e[1:2]
            xdt = xg * dt_e
            prev = state[g]
            st_ref[0, g] = prev
            cb = _bdot_nt(cg, bg)
            goff = _bdot(cg, prev)
            snew = _bdot_tn(bg, xdt * dte_e)
            heads = range(g * HEADS_PER_GROUP, (g + 1) * HEADS_PER_GROUP)
            ms = [cb * _seg_decay(cs, cs_t, hd, row, col) for hd in heads]
            yd = [_bdot(m, xdt[:, r * SSM_HEAD_DIM:(r + 1) * SSM_HEAD_DIM]) for r, m in enumerate(ms)]
            y = jnp.concatenate(yd, axis=1) + e_e * goff + xg * dsk_e
            state[g] = prev * ecl_e + snew
            ypre_ref[:, gs] = y
            part = y * sz[:, gs]
            out_ref[:, gs] = part * _rms(part) * ng[:, gs]

    return pl.pallas_call(
        body, name="ssd_fwd", grid=(N_CHUNKS,),
        in_specs=[pl.BlockSpec((CHUNK, D_CONV), lambda c: (c, 0)), pl.BlockSpec((CHUNK, D_SSM), lambda c: (c, zcol)),
                  pl.BlockSpec((CHUNK, LANE), lambda c: (c, dtcol)), _full((1, LANE)), _full((1, LANE)), _full((1, LANE)),
                  _full((1, D_SSM))],
        out_specs=[pl.BlockSpec((CHUNK, D_SSM), lambda c: (c, 0)), pl.BlockSpec((CHUNK, D_SSM), lambda c: (c, 0)),
                   pl.BlockSpec((1, SSM_GROUPS, SSM_STATE, GROUP_W), lambda c: (c, 0, 0, 0))],
        out_shape=[jax.ShapeDtypeStruct((SEQ, D_SSM), f32), jax.ShapeDtypeStruct((SEQ, D_SSM), f32),
                   jax.ShapeDtypeStruct((N_CHUNKS, SSM_GROUPS, SSM_STATE, GROUP_W), f32)],
        scratch_shapes=[pltpu.VMEM((SSM_GROUPS, SSM_STATE, GROUP_W), f32)],
        compiler_params=_params("arbitrary"),
    )(act, proj, proj, dt_bias, a_log, d_skip, norm_g)


def ssd_bwd_g(act, proj, ypre, states, d_out, dt_bias, a_log, d_skip, norm_g):
    zcol, dtcol = OFF_Z // D_SSM, OFF_DT // LANE

    def body(act_ref, z_ref, dt_ref, ypre_ref, st_ref, do_ref, dtb_ref, alog_ref, dsk_ref, ng_ref,
             dact_ref, ddt_ref, dz_ref, dng_ref, dpar_ref, dstate):
        i = pl.program_id(0)

        @pl.when(i == 0)
        def _():
            for ref in (dng_ref, dpar_ref, dstate):
                ref[...] = jnp.zeros_like(ref)

        row, col, dtp, a_row, cs, cs_t, cs_last = _ssd_chunk_common(dt_ref[...], dtb_ref[...], alog_ref[...])
        upper = (row <= col).astype(f32)
        lane = lax.broadcasted_iota(jnp.int32, (CHUNK, LANE), 1)
        rowl = lax.broadcasted_iota(jnp.int32, (CHUNK, LANE), 0)
        e_cs = jnp.exp(cs)
        dte = jnp.exp(cs_last - cs)
        ecl = jnp.exp(cs_last)
        rows8 = jnp.concatenate([ecl, dsk_ref[...], jnp.zeros((6, LANE), f32)], axis=0)
        z = z_ref[...]
        sgz = _sigmoid(z)
        sz = z * sgz
        ng = ng_ref[...]
        ddt_mat = jnp.zeros((CHUNK, LANE), f32)
        dcs_mat = jnp.zeros((CHUNK, LANE), f32)
        dcs_t = jnp.zeros((LANE, CHUNK), f32)
        dcsl_row = jnp.zeros((1, LANE), f32)
        dd_row = jnp.zeros((1, LANE), f32)
        for g in range(SSM_GROUPS):
            gs = slice(g * GROUP_W, (g + 1) * GROUP_W)
            bsl = slice(D_SSM + g * SSM_STATE, D_SSM + (g + 1) * SSM_STATE)
            csl = slice(D_SSM + D_BC + g * SSM_STATE, D_SSM + D_BC + (g + 1) * SSM_STATE)
            ind = _group_indicator(g)
            y = ypre_ref[:, gs]
            part = y * sz[:, gs]
            r = _rms(part)
            yhat = part * r
            d_o = do_ref[:, gs]
            dng_ref[:, gs] += jnp.sum(d_o * yhat, axis=0, keepdims=True)
            dyz = _rms_bwd(d_o, yhat, r, ng[:, gs])
            dy = dyz * sz[:, gs]
            dz_ref[:, gs] = dyz * y * (sgz[:, gs] * (1.0 + z[:, gs] * (1.0 - sgz[:, gs])))

            xg = act_ref[:, gs]
            bg = act_ref[:, bsl]
            cg = act_ref[:, csl]
            dt_e, e_e, dte_e = _head_spread(dtp, ind), _head_spread(e_cs, ind), _head_spread(dte, ind)
            rows_e = _head_spread(rows8, ind)
            ecl_e, dsk_e = rows_e[0:1], rows_e[1:2]
            xdt = xg * dt_e
            prev = st_ref[0, g]
            dh = dstate[g]
            heads = range(g * HEADS_PER_GROUP, (g + 1) * HEADS_PER_GROUP)
            hsl = [slice(r_ * SSM_HEAD_DIM, (r_ + 1) * SSM_HEAD_DIM) for r_ in range(HEADS_PER_GROUP)]
            cb = _bdot_nt(cg, bg)
            lms = [_seg_decay(cs, cs_t, hd, row, col) for hd in heads]
            ms = [cb * lm for lm in lms]
            gmat = _bdot(cg, prev)
            dgm = dy * e_e
            dcg = _bdot_nt(dgm, prev)
            dprev = _bdot_tn(cg, dgm)
            dbg = _bdot_nt(xdt * dte_e, dh)
            dw = _bdot(bg, dh)
            dms = [_bdot_nt(dy[:, s_], xdt[:, s_]) for s_ in hsl]
            dxdts = [_bdot_tn(m, dy[:, s_]) for m, s_ in zip(ms, hsl)]
            dxdt = jnp.concatenate(dxdts, axis=1) + dw * dte_e
            dact_ref[:, gs] = dy * dsk_e + dxdt * dt_e
            dstate[g] = dprev + dh * ecl_e
            dcb = jnp.zeros((CHUNK, CHUNK), f32)
            for hd, dm, lm, m in zip(heads, dms, lms, ms):
                dcb = dcb + dm * lm
                dseg = dm * m
                dcs_mat = dcs_mat + jnp.where(lane == hd, jnp.sum(dseg, axis=1, keepdims=True), 0.0)
                dcs_t = jnp.where(row == hd, jnp.sum(dseg, axis=0, keepdims=True), dcs_t)
            dact_ref[:, bsl] = dbg + _bdot_tn(dcb, cg)
            dact_ref[:, csl] = dcg + _bdot(dcb, bg)
            ddte = _head_sums(dw * xdt, ind) * dte
            dcs_mat = dcs_mat + _head_sums(dy * gmat, ind) * e_cs - ddte
            ddt_mat = ddt_mat + _head_sums(dxdt * xg, ind)
            dcsl_row = (dcsl_row + jnp.sum(ddte, axis=0, keepdims=True)
                        + jnp.sum(_head_sums(dh * prev, ind), axis=0, keepdims=True) * ecl)
            dd_row = dd_row + jnp.sum(_head_sums(dy * xg, ind), axis=0, keepdims=True)
        dcs_mat = dcs_mat - dcs_t.T + jnp.where(rowl == CHUNK - 1, dcsl_row, 0.0)
        dda = _hdot(upper, dcs_mat)
        ddt_mat = ddt_mat + dda * a_row
        da_row = jnp.sum(dda * dtp, axis=0, keepdims=True)
        ddt_raw = ddt_mat * _sigmoid(dt_ref[...] + dtb_ref[...])
        ddt_ref[...] = ddt_raw
        dpar_ref[0:1, :] += jnp.sum(ddt_raw, axis=0, keepdims=True)
        dpar_ref[1:2, :] += da_row * a_row
        dpar_ref[2:3, :] += dd_row

    blk = lambda i: N_CHUNKS - 1 - i
    return pl.pallas_call(
        body, name="ssd_bwd", grid=(N_CHUNKS,),
        in_specs=[pl.BlockSpec((CHUNK, D_CONV), lambda i: (blk(i), 0)), pl.BlockSpec((CHUNK, D_SSM), lambda i: (blk(i), zcol)),
                  pl.BlockSpec((CHUNK, LANE), lambda i: (blk(i), dtcol)), pl.BlockSpec((CHUNK, D_SSM), lambda i: (blk(i), 0)),
                  pl.BlockSpec((1, SSM_GROUPS, SSM_STATE, GROUP_W), lambda i: (blk(i), 0, 0, 0)),
                  pl.BlockSpec((CHUNK, D_SSM), lambda i: (blk(i), 0)),
                  _full((1, LANE)), _full((1, LANE)), _full((1, LANE)), _full((1, D_SSM))],
        out_specs=[pl.BlockSpec((CHUNK, D_CONV), lambda i: (blk(i), 0)), pl.BlockSpec((CHUNK, LANE), lambda i: (blk(i), 0)),
                   pl.BlockSpec((CHUNK, D_SSM), lambda i: (blk(i), 0)), _full((1, D_SSM)), _full((8, LANE))],
        out_shape=[jax.ShapeDtypeStruct((SEQ, D_CONV), f32), jax.ShapeDtypeStruct((SEQ, LANE), f32),
                   jax.ShapeDtypeStruct((SEQ, D_SSM), f32), jax.ShapeDtypeStruct((1, D_SSM), f32),
                   jax.ShapeDtypeStruct((8, LANE), f32)],
        scratch_shapes=[pltpu.VMEM((SSM_GROUPS, SSM_STATE, GROUP_W), f32)],
        compiler_params=_params("arbitrary"),
    )(act, proj, proj, ypre, states, d_out, dt_bias, a_log, d_skip, norm_g)


def out_fwd(x, attn, ssm, w_out, tm=512):
    def body(x_ref, a_ref, s_ref, w_ref, o_ref):
        o_ref[...] = x_ref[...] + _bdot(a_ref[...], w_ref[:D_ATTN, :]) + _bdot(s_ref[...], w_ref[D_ATTN:, :])

    tok = lambda w_: pl.BlockSpec((tm, w_), lambda i: (i, 0))
    return pl.pallas_call(
        body, name="out_fwd", grid=(SEQ // tm,),
        in_specs=[tok(D_MODEL), tok(D_ATTN), tok(D_SSM), _full((D_MODEL, D_MODEL))],
        out_specs=tok(D_MODEL), out_shape=jax.ShapeDtypeStruct((SEQ, D_MODEL), f32),
        compiler_params=_params("arbitrary"),
    )(x, attn, ssm, w_out)


def out_bwd(dx1, attn, ssm, w_out, tm=512):
    nt = SEQ // tm

    def body(d_ref, a_ref, s_ref, w_ref, da_ref, ds_ref, dw16_ref, dw_ref):
        i = pl.program_id(0)

        @pl.when(i == 0)
        def _():
            dw_ref[...] = jnp.zeros_like(dw_ref)

        d = d_ref[...].astype(bf16)
        dcat = _bdot_nt(d, w_ref[...])
        da_ref[...] = dcat[:, :D_ATTN]
        ds_ref[...] = dcat[:, D_ATTN:]
        dw_ref[:D_ATTN, :] += _bdot_tn(a_ref[...], d)
        dw_ref[D_ATTN:, :] += _bdot_tn(s_ref[...], d)

        @pl.when(i == nt - 1)
        def _():
            dw16_ref[...] = dw_ref[...].astype(bf16)

    tok = lambda w_: pl.BlockSpec((tm, w_), lambda i: (i, 0))
    return pl.pallas_call(
        body, name="out_bwd", grid=(nt,),
        in_specs=[tok(D_MODEL), tok(D_ATTN), tok(D_SSM), _resident((D_MODEL, D_MODEL))],
        out_specs=[tok(D_ATTN), tok(D_SSM), _resident((D_MODEL, D_MODEL))],
        out_shape=[jax.ShapeDtypeStruct((SEQ, D_ATTN), f32), jax.ShapeDtypeStruct((SEQ, D_SSM), f32),
                   jax.ShapeDtypeStruct((D_MODEL, D_MODEL), bf16)],
        scratch_shapes=[pltpu.VMEM((D_MODEL, D_MODEL), f32)],
        compiler_params=_params("arbitrary"),
    )(dx1, attn, ssm, w_out)


MLP_SUB = 256


def mlp_fwd(x1, g, w_up, w_down, tm=1024):
    def body(x_ref, g_ref, wu_ref, wd_ref, o_ref, u_ref, h_scr):
        j = pl.program_id(1)

        @pl.when(j == 0)
        def _():
            xv = x_ref[...]
            h_scr[...] = (xv * _rms(xv) * g_ref[...]).astype(bf16)
            o_ref[...] = xv

        for r in range(tm // MLP_SUB):
            rows = slice(r * MLP_SUB, (r + 1) * MLP_SUB)
            u = jnp.dot(h_scr[rows, :], wu_ref[...], preferred_element_type=f32)
            u_ref[rows, :] = u
            a = jnp.square(jnp.maximum(u, 0.0))
            o_ref[rows, :] += _bdot(a, wd_ref[...])

    return pl.pallas_call(
        body, name="mlp_fwd", grid=(SEQ // tm, N_CHIPS),
        in_specs=[pl.BlockSpec((tm, D_MODEL), lambda i, j: (i, 0)), _full((1, D_MODEL)),
                  pl.BlockSpec((None, D_MODEL, FF_TILE), lambda i, j: (j, 0, 0)),
                  pl.BlockSpec((None, FF_TILE, D_MODEL), lambda i, j: (j, 0, 0))],
        out_specs=[pl.BlockSpec((tm, D_MODEL), lambda i, j: (i, 0)), pl.BlockSpec((tm, FF_TILE), lambda i, j: (i, j))],
        out_shape=[jax.ShapeDtypeStruct((SEQ, D_MODEL), f32), jax.ShapeDtypeStruct((SEQ, D_FF), f32)],
        scratch_shapes=[pltpu.VMEM((tm, D_MODEL), bf16)],
        compiler_params=_params("arbitrary", "arbitrary"),
    )(x1, g, w_up, w_down)


def mlp_bwd_data(dx2, u, x1, g, w_up, w_down, tm=1024):
    def body(d_ref, u_ref, x_ref, g_ref, wu_ref, wd_ref, dx_ref, du_ref, dg_ref, dh_scr):
        i, j = pl.program_id(0), pl.program_id(1)

        @pl.when(jnp.logical_and(i == 0, j == 0))
        def _():
            dg_ref[...] = jnp.zeros_like(dg_ref)

        @pl.when(j == 0)
        def _():
            dh_scr[...] = jnp.zeros_like(dh_scr)

        for r in range(tm // MLP_SUB):
            rows = slice(r * MLP_SUB, (r + 1) * MLP_SUB)
            da = _bdot_nt(d_ref[rows, :], wd_ref[...])
            du = (da * (2.0 * jnp.maximum(u_ref[rows, :], 0.0))).astype(bf16)
            du_ref[rows, :] = du
            dh_scr[rows, :] += _bdot_nt(du, wu_ref[...])

        @pl.when(j == N_CHIPS - 1)
        def _():
            xv = x_ref[...]
            r = _rms(xv)
            xhat = xv * r
            dh = dh_scr[...]
            dg_ref[...] += jnp.sum(dh * xhat, axis=0, keepdims=True)
            dx_ref[...] = d_ref[...] + _rms_bwd(dh, xhat, r, g_ref[...])

    return pl.pallas_call(
        body, name="mlp_bwd_data", grid=(SEQ // tm, N_CHIPS),
        in_specs=[pl.BlockSpec((tm, D_MODEL), lambda i, j: (i, 0)), pl.BlockSpec((tm, FF_TILE), lambda i, j: (i, j)),
                  pl.BlockSpec((tm, D_MODEL), lambda i, j: (i, 0)), _full((1, D_MODEL)),
                  pl.BlockSpec((None, D_MODEL, FF_TILE), lambda i, j: (j, 0, 0)),
                  pl.BlockSpec((None, FF_TILE, D_MODEL), lambda i, j: (j, 0, 0))],
        out_specs=[pl.BlockSpec((tm, D_MODEL), lambda i, j: (i, 0)), pl.BlockSpec((tm, FF_TILE), lambda i, j: (i, j)),
                   _full((1, D_MODEL))],
        out_shape=[jax.ShapeDtypeStruct((SEQ, D_MODEL), f32), jax.ShapeDtypeStruct((SEQ, D_FF), bf16),
                   jax.ShapeDtypeStruct((1, D_MODEL), f32)],
        scratch_shapes=[pltpu.VMEM((tm, D_MODEL), f32)],
        compiler_params=_params("arbitrary", "arbitrary"),
    )(dx2, u, x1, g, w_up, w_down)


def mlp_bwd_weights(dx2, u, du, x1, g, tm=512):
    nt = SEQ // tm

    def body(d_ref, u_ref, du_ref, x_ref, g_ref, dwu16_ref, dwd16_ref, h_scr, d_scr, dwu_ref, dwd_ref):
        j, i = pl.program_id(0), pl.program_id(1)

        @pl.when(j == 0)
        def _():
            xv = x_ref[...]
            h_scr[i] = (xv * _rms(xv) * g_ref[...]).T.astype(bf16)
            d_scr[i] = d_ref[...].astype(bf16)

        @pl.when(i == 0)
        def _():
            dwu_ref[...] = jnp.zeros_like(dwu_ref)
            dwd_ref[...] = jnp.zeros_like(dwd_ref)

        dwu_ref[...] += jnp.dot(h_scr[i], du_ref[...], preferred_element_type=f32)
        a = jnp.square(jnp.maximum(u_ref[...], 0.0))
        dwd_ref[...] += _bdot_tn(a, d_scr[i])

        @pl.when(i == nt - 1)
        def _():
            dwu16_ref[...] = dwu_ref[...].astype(bf16)
            dwd16_ref[...] = dwd_ref[...].astype(bf16)

    up = pl.BlockSpec((None, D_MODEL, FF_TILE), lambda j, i: (j, 0, 0))
    down = pl.BlockSpec((None, FF_TILE, D_MODEL), lambda j, i: (j, 0, 0))
    first_pass = pl.BlockSpec((tm, D_MODEL), lambda j, i: (jnp.where(j == 0, i, nt - 1), 0))
    return pl.pallas_call(
        body, name="mlp_bwd_weights", grid=(N_CHIPS, nt),
        in_specs=[first_pass, pl.BlockSpec((tm, FF_TILE), lambda j, i: (i, j)),
                  pl.BlockSpec((tm, FF_TILE), lambda j, i: (i, j)), first_pass, _full((1, D_MODEL))],
        out_specs=[up, down],
        out_shape=[jax.ShapeDtypeStruct((N_CHIPS, D_MODEL, FF_TILE), bf16), jax.ShapeDtypeStruct((N_CHIPS, FF_TILE, D_MODEL), bf16)],
        scratch_shapes=[pltpu.VMEM((nt, D_MODEL, tm), bf16), pltpu.VMEM((nt, tm, D_MODEL), bf16),
                        pltpu.VMEM((D_MODEL, FF_TILE), f32), pltpu.VMEM((FF_TILE, D_MODEL), f32)],
        compiler_params=_params("arbitrary", "arbitrary"),
    )(dx2, u, du, x1, g)


def loss_head(y, target, tm=512):
    def body(y_ref, t_ref, dy_ref, l_ref):
        @pl.when(pl.program_id(0) == 0)
        def _():
            l_ref[...] = jnp.zeros_like(l_ref)

        d = y_ref[...] - t_ref[...]
        dy_ref[...] = d * (1.0 / D_MODEL)
        part = jnp.sum(jnp.mean(d * d, axis=-1, keepdims=True), axis=0, keepdims=True)
        l_ref[...] += 0.5 * part

    tok = pl.BlockSpec((tm, D_MODEL), lambda i: (i, 0))
    return pl.pallas_call(
        body, name="loss_head", grid=(SEQ // tm,), in_specs=[tok, tok], out_specs=[tok, _full((1, 1))],
        out_shape=[jax.ShapeDtypeStruct((SEQ, D_MODEL), f32), jax.ShapeDtypeStruct((1, 1), f32)],
        compiler_params=_params("arbitrary"),
    )(y, target)


def _pad_lane(v):
    return jnp.pad(v, (0, LANE - v.shape[0]))[None, :]


def local_step(x, target, w, prov):
    bucket = jnp.asarray(_bucket_table().T)
    bias = bias_build(w["rel_bias"], bucket)
    saved = []
    for l in range(DEPTH):
        g_mix = w["mix_norm_g"][l][None, :] + prov.stage(("begin", l), x)
        w_in = prov.w_in(l, x)
        proj = in_fwd(x, g_mix, w_in)
        conv_b = w["conv_b"][l][None, :]
        act = conv_fwd(proj, w["conv_w"][l], conv_b)
        dtb = _pad_lane(w["dt_bias"][l]) + prov.stage(("mid", l), act)
        alog, dsk = _pad_lane(w["a_log"][l]), _pad_lane(w["d_skip"][l])
        ng = w["ssm_norm_g"][l][None, :]
        ssm, ypre, states = ssd_fwd_g(act, proj, dtb, alog, dsk, ng)
        qg, kg = w["q_gain"][l][:, None] + 0.0 * ssm[:1, :1], w["k_gain"][l][None, :]
        attn = attn_fwd_t(proj, qg, kg, w["sinks"][l], bias)
        tok = prov.stage(("pre_out", l), attn)
        w_out = prov.w_out(l, attn) + jnp.asarray(tok, bf16)
        x1 = out_fwd(x, attn, ssm, w_out)
        g_mlp = w["mlp_norm_g"][l][None, :] + prov.stage(("pre_mlp", l), x1)
        w_up, w_down = prov.mlp(l, x1)
        x2, u = mlp_fwd(x1, g_mlp, w_up, w_down)
        saved.append(dict(x=x, proj=proj, attn=attn, act=act, ssm=ssm, ypre=ypre, states=states, x1=x1, u=u,
                          g_mix=g_mix, qg=qg, kg=kg, conv_b=conv_b, dtb=dtb, alog=alog, dsk=dsk, ng=ng, g_mlp=g_mlp,
                          w_in=w_in, w_out=w_out, w_up=w_up, w_down=w_down))
        x = x2
    dx, loss = loss_head(x, target)
    grads = [None] * DEPTH
    dbands = [None] * DEPTH
    tok = 0.0
    for l in reversed(range(DEPTH)):
        s = saved[l]
        g_mlp = s["g_mlp"] + tok
        dx1, du, dg_mlp = mlp_bwd_data(dx, s["u"], s["x1"], g_mlp, s["w_up"], s["w_down"])
        dw_up, dw_down = mlp_bwd_weights(dx, s["u"], du, s["x1"], g_mlp)
        tok = prov.grads(("mlp", l), dict(w_up=dw_up, w_down=dw_down), dx1)
        dattn, dssm, dw_out = out_bwd(dx1, s["attn"], s["ssm"], s["w_out"])
        dact, ddt, dz, dng, dpar = ssd_bwd_g(s["act"], s["proj"], s["ypre"], s["states"], dssm, s["dtb"] + tok, s["alog"],
                                           s["dsk"], s["ng"])
        conv_b = s["conv_b"] + prov.stage(("bwd_mid", l), dact)
        dxbc, dconv_w, dconv_b = conv_bwd(s["proj"], dact, w["conv_w"][l], conv_b)
        dq, dk, dv, dband, dsink, dqg, dkg = attn_bwd_t(s["proj"], dattn, s["qg"], s["kg"], w["sinks"][l], bias)
        dbands[l] = dband
        g_mix = s["g_mix"]
        if l == 0:
            d_rel = bias_bwd(dbands[0], dbands[1], bucket)
            g_mix = g_mix + 0.0 * d_rel[:1, :1]
        dx, dw_in, dg_mix = in_bwd(dq, dz, dxbc, dk, dv, ddt, s["x"], g_mix, s["w_in"], dx1)
        tok = prov.grads(("mix", l), dict(w_in=split_w_in_grad(dw_in), w_out=dw_out), dx)
        grads[l] = dict(mix_norm_g=dg_mix[0], q_gain=dqg[:, 0], k_gain=dkg[0], sinks=dsink[:, 0],
                        conv_w=dconv_w, conv_b=dconv_b[0], dt_bias=dpar[0, :SSM_HEADS], a_log=dpar[1, :SSM_HEADS],
                        d_skip=dpar[2, :SSM_HEADS], ssm_norm_g=dng[0], mlp_norm_g=dg_mlp[0])
    out = {k: jnp.stack([grads[l][k] for l in range(DEPTH)]) for k in grads[0]}
    out["rel_bias"] = d_rel[:, :N_Q_HEADS]
    return loss, dx, out, tok


MESH = pl.DeviceIdType.MESH
HBM = pl.BlockSpec(memory_space=pltpu.HBM)
N_DEVICES = 8


def _coords():
    return lax.axis_index("x"), lax.axis_index("y"), lax.axis_index("c")


def _peer_chips(x, y):
    return [(1 - x, y), (x, 1 - y), (1 - x, 1 - y)]


def _remote(src, dst, send_sem, recv_sem, device):
    return pltpu.make_async_remote_copy(src_ref=src, dst_ref=dst, send_sem=send_sem, recv_sem=recv_sem,
                                        device_id=device, device_id_type=MESH)


SEM = pl.BlockSpec(memory_space=pltpu.SEMAPHORE)
ANY = pl.BlockSpec(memory_space=pl.ANY)
DATAFLOW = pltpu.SideEffectType.DATAFLOW_SIDE_EFFECTING


def _gather_copies(kind, src_refs, land_refs, ssem, rsem):
    x, y, c = _coords()
    k_me = 2 * x + y
    n = len(land_refs)
    cps = []
    for p, land in enumerate(land_refs):
        hr = land.shape[1] // 2
        rows = pl.ds(c * hr, hr)
        for j, chip in enumerate(_peer_chips(x, y)):
            i = 3 * p + j
            if kind == "ici":
                cps.append(_remote(src_refs[p].at[rows, :], land.at[k_me, rows, :], ssem.at[i], rsem.at[i], (*chip, c)))
            else:
                got = land.at[2 * chip[0] + chip[1], rows, :]
                cps.append(_remote(got, got, ssem.at[i], rsem.at[i], (x, y, 1 - c)))
        if kind == "relay":
            cps.append(_remote(src_refs[p], land.at[k_me], ssem.at[3 * n + p], rsem.at[3 * n + p], (x, y, 1 - c)))
    return cps


def gather_now(srcs, conv):
    n = len(srcs)

    def body(*refs):
        src_refs, conv_ref = refs[:n], refs[n]
        lands, gconv = refs[n + 1:2 * n + 1], refs[2 * n + 1]
        ssem, rsem, fsem, frsem, csem, crsem = refs[2 * n + 2:]
        x, y, c = _coords()
        k_me = 2 * x + y
        targets = [(*chip, c) for chip in _peer_chips(x, y)] + [(x, y, 1 - c)]
        ici = _gather_copies("ici", src_refs, lands, ssem, rsem)
        relay = _gather_copies("relay", src_refs, lands, fsem, frsem)
        passed = [cp for i, cp in enumerate(relay) if i % 4 != 3]
        own = relay[3::4]
        conv_cps = [_remote(conv_ref, gconv.at[k_me], csem.at[j], crsem.at[j], t) for j, t in enumerate(targets)]
        for cp in ici + conv_cps + own:
            cp.start()
        for cp, fw in zip(ici, passed):
            cp.wait_recv()
            fw.start()
        for cp in conv_cps + relay:
            cp.wait_recv()
        for cp in ici + relay + conv_cps:
            cp.wait_send()

    out_shape = [jax.ShapeDtypeStruct((N_CHIPS,) + s.shape, s.dtype) for s in srcs]
    out_shape.append(jax.ShapeDtypeStruct((N_CHIPS,) + conv.shape, conv.dtype))
    sems = lambda k: pltpu.SemaphoreType.DMA((k,))
    return pl.pallas_call(
        body, name="gather_now", out_shape=out_shape, in_specs=[HBM] * (n + 1), out_specs=[HBM] * (n + 1),
        scratch_shapes=[sems(3 * n), sems(3 * n), sems(4 * n), sems(4 * n), sems(N_CHIPS), sems(N_CHIPS)],
    )(*srcs, conv)


def _gather_maker(kind, n_src):
    def make(refs, ssem, rsem):
        cps = _gather_copies(kind, refs[:n_src], refs[n_src:], ssem, rsem)
        return cps, cps
    return make


def _scatter_maker(n):
    def make(refs, ssem, rsem):
        x, y, c = _coords()
        k_me = 2 * x + y
        sends, arrivals = [], []
        for p in range(n):
            src, land = refs[p], refs[n + p]
            sends.append(_remote(src.at[k_me, 1 - c], land.at[0], ssem.at[7 * p], rsem.at[7 * p], (x, y, 1 - c)))
            for j, chip in enumerate(_peer_chips(x, y)):
                for cc in range(2):
                    sends.append(_remote(src.at[2 * chip[0] + chip[1], cc], land.at[1 + 2 * j + c],
                                         ssem.at[7 * p + 1 + 2 * j + cc], rsem.at[7 * p + 1 + 2 * j + c], (*chip, cc)))
            for s in range(7):
                arrivals.append(_remote(land.at[s], land.at[s], ssem.at[7 * p + s], rsem.at[7 * p + s], (x, y, 1 - c)))
        return sends, arrivals
    return make


def _share_maker(n):
    def make(refs, ssem, rsem):
        x, y, c = _coords()
        sends = [_remote(refs[p].at[c], refs[p].at[c], ssem.at[p], rsem.at[p], (x, y, 1 - c)) for p in range(n)]
        arrivals = [_remote(refs[p].at[1 - c], refs[p].at[1 - c], ssem.at[p], rsem.at[p], (x, y, 1 - c)) for p in range(n)]
        return sends, arrivals
    return make


def split_start(name, make, n_sems, operands, after):
    n = len(operands)

    def body(*refs):
        ssem, rsem, token = refs[n + 1], refs[n + 2], refs[-1]
        for cp in make(refs[:n], ssem, rsem)[0]:
            cp.start()
        token[...] = jnp.zeros_like(token)

    ops = [pltpu.with_memory_space_constraint(a, pltpu.HBM) for a in operands]
    outs = pl.pallas_call(
        body, name=name,
        out_shape=(pltpu.SemaphoreType.DMA((n_sems,)), pltpu.SemaphoreType.DMA((n_sems,)),
                   *[pltpu.HBM(a.shape, a.dtype) for a in ops], jax.ShapeDtypeStruct((8, LANE), f32)),
        in_specs=[HBM] * n + [ANY], out_specs=(SEM, SEM, *[HBM] * n, pl.BlockSpec(memory_space=pltpu.VMEM)),
        input_output_aliases={i: 2 + i for i in range(n)},
        compiler_params=pltpu.CompilerParams(has_side_effects=DATAFLOW),
    )(*ops, after)
    return dict(name=name, make=make, ssem=outs[0], rsem=outs[1], operands=outs[2:2 + n], token=outs[-1][0, 0],
                tokens=outs[-1])


def split_wait(handle, after):
    n = len(handle["operands"])

    def body(*refs):
        sends, arrivals = handle["make"](refs[:n], refs[n], refs[n + 1])
        for cp in sends:
            cp.wait_send()
        for cp in arrivals:
            cp.wait_recv()

    outs = pl.pallas_call(
        body, name=handle["name"].replace("start", "wait"),
        out_shape=tuple(pltpu.HBM(a.shape, a.dtype) for a in handle["operands"]),
        in_specs=[HBM] * n + [SEM, SEM, ANY], out_specs=tuple([HBM] * n),
        input_output_aliases={i: i for i in range(n)},
        compiler_params=pltpu.CompilerParams(has_side_effects=DATAFLOW),
    )(*handle["operands"], handle["ssem"], handle["rsem"], after)
    return list(outs)


def piece_sum(g, recv, kc_arr):
    _, _, rb, cc = g.shape
    tr = min(256, rb)

    def body(kc_ref, g_ref, r_ref, o_ref):
        acc = g_ref[...].astype(f32)
        for s in range(7):
            acc = acc + r_ref[s].astype(f32)
        o_ref[...] = acc

    return pl.pallas_call(
        body, name="piece_sum",
        grid_spec=pltpu.PrefetchScalarGridSpec(
            num_scalar_prefetch=1, grid=(rb // tr,),
            in_specs=[pl.BlockSpec((None, None, tr, cc), lambda r, kc: (kc[0], kc[1], r, 0)),
                      pl.BlockSpec((7, tr, cc), lambda r, kc: (0, r, 0))],
            out_specs=pl.BlockSpec((None, tr, cc), lambda r, kc: (kc[1], r, 0))),
        out_shape=jax.ShapeDtypeStruct((2, rb, cc), f32),
        compiler_params=_params("arbitrary"),
    )(kc_arr, g, recv)


def small_all_reduce(vec):
    def body(v_ref, o_ref, gat, ssem, rsem):
        x, y, c = _coords()
        me = 4 * x + 2 * y + c
        gat[me] = v_ref[...]
        sends = []
        for t in range(1, N_DEVICES):
            peer = (x ^ (t >> 2), y ^ ((t >> 1) & 1), c ^ (t & 1))
            cp = _remote(v_ref, gat.at[me], ssem.at[t - 1], rsem.at[t - 1], peer)
            cp.start()
            sends.append(cp)
        for t in range(1, N_DEVICES):
            peer = (x ^ (t >> 2), y ^ ((t >> 1) & 1), c ^ (t & 1))
            slot = gat.at[4 * peer[0] + 2 * peer[1] + peer[2]]
            _remote(slot, slot, ssem.at[t - 1], rsem.at[t - 1], peer).wait_recv()
        for cp in sends:
            cp.wait_send()
        acc = gat[0]
        for d in range(1, N_DEVICES):
            acc = acc + gat[d]
        o_ref[...] = acc

    return pl.pallas_call(
        body, name="small_all_reduce", out_shape=jax.ShapeDtypeStruct(vec.shape, vec.dtype),
        in_specs=[pl.BlockSpec(memory_space=pltpu.VMEM)], out_specs=pl.BlockSpec(memory_space=pltpu.VMEM),
        scratch_shapes=[pltpu.VMEM((N_DEVICES,) + vec.shape, vec.dtype), pltpu.SemaphoreType.DMA((N_DEVICES - 1,)),
                        pltpu.SemaphoreType.DMA((N_DEVICES - 1,))],
    )(vec)


def _adamw_math(w, g, m, v):
    m_new = ADAM_B1 * m + (1.0 - ADAM_B1) * g
    v_new = ADAM_B2 * v + (1.0 - ADAM_B2) * jnp.square(g)
    m_hat = m_new / (1.0 - ADAM_B1 ** ADAM_STEP)
    v_hat = v_new / (1.0 - ADAM_B2 ** ADAM_STEP)
    delta = -ADAM_LR * (m_hat / (jnp.sqrt(v_hat) + ADAM_EPS) + ADAM_WD * w)
    return delta, m_new, v_new


def adamw_shard(w, g0, g1, m, v):
    depth, rows, cols = w.shape
    half = rows // 2
    tr = min(256, half)
    nr = half // tr

    def body(w_ref, g0_ref, g1_ref, m_ref, v_ref, go_ref, d_ref, nm_ref, nv_ref):
        gv = jnp.where(pl.program_id(0) == 0, g0_ref[...], g1_ref[...])
        go_ref[...] = gv
        d_ref[...], nm_ref[...], nv_ref[...] = _adamw_math(w_ref[...], gv, m_ref[...], v_ref[...])

    spec = pl.BlockSpec((None, tr, cols), lambda l, h, r: (l, h * nr + r, 0))
    g0spec = pl.BlockSpec((None, tr, cols), lambda l, h, r: (jnp.where(l == 0, h, 1), jnp.where(l == 0, r, nr - 1), 0))
    g1spec = pl.BlockSpec((None, tr, cols), lambda l, h, r: (jnp.where(l == 1, h, 0), jnp.where(l == 1, r, 0), 0))
    return pl.pallas_call(
        body, name="adamw_shard", grid=(depth, 2, nr), in_specs=[spec, g0spec, g1spec, spec, spec], out_specs=[spec] * 4,
        out_shape=[jax.ShapeDtypeStruct(w.shape, f32)] * 4,
        compiler_params=_params("arbitrary", "arbitrary", "arbitrary"),
    )(w, g0, g1, m, v)


def adamw_cols(w, g, m, v, tc=34):
    cols, depth, rows = w.shape

    def body(w_ref, g_ref, m_ref, v_ref, d_ref, nm_ref, nv_ref):
        d_ref[...], nm_ref[...], nv_ref[...] = _adamw_math(w_ref[...], g_ref[...], m_ref[...], v_ref[...])

    spec = pl.BlockSpec((tc, depth, rows), lambda i: (i, 0, 0))
    return pl.pallas_call(
        body, name="adamw_cols", grid=(cols // tc,), in_specs=[spec] * 4, out_specs=[spec] * 3,
        out_shape=[jax.ShapeDtypeStruct(w.shape, f32)] * 3,
        compiler_params=_params("arbitrary"),
    )(w, g, m, v)


def adamw_small(ws, gs, ms, vs):
    n = len(ws)

    def body(*refs):
        ins, outs = refs[:4 * n], refs[4 * n:]
        for i in range(n):
            w_ref, g_ref, m_ref, v_ref = (ins[k * n + i] for k in range(4))
            outs[i][...], outs[n + i][...], outs[2 * n + i][...] = _adamw_math(w_ref[...], g_ref[...], m_ref[...], v_ref[...])

    outs = pl.pallas_call(
        body, name="adamw_small", out_shape=[jax.ShapeDtypeStruct(w.shape, f32) for w in ws] * 3,
    )(*ws, *gs, *ms, *vs)
    return outs[:n], outs[n:2 * n], outs[2 * n:]


WEIGHTS = ("mix_norm_g", "w_in", "q_gain", "k_gain", "sinks", "rel_bias", "conv_w", "conv_b", "dt_bias", "a_log", "d_skip",
           "ssm_norm_g", "w_out", "mlp_norm_g", "w_up", "w_down")
BIG = ("w_in", "w_out", "w_up", "w_down")
SMALL = tuple(n for n in WEIGHTS if n not in BIG)
PACK_COLS = 1024
PACK_ROWS = 16


def _pack(named, last=None):
    flat = jnp.concatenate([named[n].reshape(-1) for n in SMALL])
    tail = jnp.zeros((1,), f32) if last is None else last.reshape(1)
    pad = jnp.zeros((PACK_ROWS * PACK_COLS - flat.shape[0] - 1,), f32)
    return jnp.concatenate([flat, pad, tail]).reshape(PACK_ROWS, PACK_COLS)


def _unpack(buf, shapes):
    flat = buf.reshape(-1)
    out, at = {}, 0
    for n in SMALL:
        size = int(np.prod(shapes[n]))
        out[n] = flat[at:at + size].reshape(shapes[n])
        at += size
    return out


class _Exchange:
    GROUPS = {"A": (("w_up", 0), ("w_down", 0)), "B": (("w_in", 1), ("w_out", 1)), "C": (("w_up", 1), ("w_down", 1))}
    ICI_AT = {("mid", 0): "B", ("pre_out", 0): "C"}
    RELAY_AT = {("pre_out", 0): "A", ("pre_mlp", 0): "B", ("mid", 1): "C"}
    LAST = ("mix", 0)
    IN_FLIGHT = 2

    def __init__(self, wts, kc_arr):
        self.wts, self.kc_arr = wts, kc_arr
        self.own = {(n, l): wts[n][l].astype(bf16) for n in BIG for l in range(DEPTH)}
        now = gather_now([self.own["w_in", 0], self.own["w_out", 0]], wts["conv_w"])
        self.ready = {("w_in", 0): now[0], ("w_out", 0): now[1]}
        self.conv_w = jnp.transpose(now[2], (1, 2, 0, 3)).reshape(DEPTH, CONV_WIDTH, D_CONV)
        self.ici, self.relay = {}, {}
        self.scatter, self.share, self.reduced = [], [], {}
        self._start_ici("A", now[2])

    def _start_ici(self, g, after):
        srcs = [self.own[p] for p in self.GROUPS[g]]
        lands = [lax.empty((N_CHIPS,) + s.shape, s.dtype) for s in srcs]
        self.ici[g] = split_start("gather%s_ici_start" % g, _gather_maker("ici", len(srcs)), 3 * len(srcs), srcs + lands,
                                  after)
        return self.ici[g]["token"]

    def stage(self, name, after):
        if name == ("begin", 0):
            return self.ici["A"]["token"]
        tok = 0.0
        g = self.RELAY_AT.get(name)
        if g is not None:
            n = len(self.GROUPS[g])
            self.relay[g] = split_start("gather%s_relay_start" % g, _gather_maker("relay", n), 4 * n,
                                        split_wait(self.ici[g], after), after)
            tok = self.relay[g]["token"]
        if name in self.ICI_AT:
            tok = tok + self._start_ici(self.ICI_AT[name], after)
        return tok

    def _get(self, piece, after):
        if piece not in self.ready:
            g = [k for k, pieces in self.GROUPS.items() if piece in pieces][0]
            lands = split_wait(self.relay[g], after)[len(self.GROUPS[g]):]
            self.ready.update(zip(self.GROUPS[g], lands))
        return self.ready[piece]

    def w_in(self, l, after):
        return align_w_in(self._get(("w_in", l), after))

    def w_out(self, l, after):
        return self._get(("w_out", l), after).reshape(D_MODEL, D_MODEL)

    def mlp(self, l, after):
        return self._get(("w_up", l), after), self._get(("w_down", l), after)

    def _view(self, n, g):
        _, rows, cols = self.wts[n].shape
        return g.reshape(N_CHIPS, 2, rows // 2, cols)

    def grads(self, name, arrays, after):
        if name == self.LAST:
            self.held = (name, arrays)
            return 0.0
        return self._scatter(name, arrays, after) + self._advance(after, self.IN_FLIGHT)

    def flush(self, after):
        return self._scatter(*self.held, after) + self._advance(after, self.IN_FLIGHT)

    def _scatter(self, name, arrays, after):
        pieces = [(n, name[1]) for n in arrays]
        views = [self._view(n, g) for n, g in arrays.items()]
        lands = [lax.empty((7,) + v.shape[2:], bf16) for v in views]
        h = split_start("scatter_%s%d_start" % name, _scatter_maker(len(views)), 7 * len(views), views + lands, after)
        self.scatter.append((pieces, h))
        return h["token"]

    def _take_share(self, after):
        pieces, h = self.share.pop(0)
        self.reduced.update(zip(pieces, split_wait(h, after)))

    def _take_scatter(self, after):
        pieces, h = self.scatter.pop(0)
        done = split_wait(h, after)
        views, lands = done[:len(pieces)], done[len(pieces):]
        sums = [piece_sum(v, land, self.kc_arr) for v, land in zip(views, lands)]
        hs = split_start(h["name"].replace("scatter", "share"), _share_maker(len(sums)), len(sums), sums, after)
        self.share.append((pieces, hs))
        return hs["token"]

    def _advance(self, after, newest):
        if self.share:
            self._take_share(after)
        return self._take_scatter(after) if len(self.scatter) > newest else 0.0

    def prepare(self, piece, after):
        while piece not in self.reduced and not any(piece in pieces for pieces, _ in self.share):
            self._take_scatter(after)
        return self.share[-1][1]["tokens"] if self.share else after

    def reduced_piece(self, piece, after):
        while piece not in self.reduced:
            if any(piece in pieces for pieces, _ in self.share):
                self._take_share(after)
            else:
                self._take_scatter(after)
        return self.reduced[piece]


def kernel(x, mix_norm_g, w_in, q_gain, k_gain, sinks, rel_bias, conv_w, conv_b, dt_bias, a_log, d_skip, ssm_norm_g, w_out, mlp_norm_g, w_up, w_down, loss_target, m_mix_norm_g, m_w_in, m_q_gain, m_k_gain, m_sinks, m_rel_bias, m_conv_w, m_conv_b, m_dt_bias, m_a_log, m_d_skip, m_ssm_norm_g, m_w_out, m_mlp_norm_g, m_w_up, m_w_down, v_mix_norm_g, v_w_in, v_q_gain, v_k_gain, v_sinks, v_rel_bias, v_conv_w, v_conv_b, v_dt_bias, v_a_log, v_d_skip, v_ssm_norm_g, v_w_out, v_mlp_norm_g, v_w_up, v_w_down):
    wts = dict(mix_norm_g=mix_norm_g, w_in=w_in, q_gain=q_gain, k_gain=k_gain, sinks=sinks, rel_bias=rel_bias, conv_w=conv_w,
               conv_b=conv_b, dt_bias=dt_bias, a_log=a_log, d_skip=d_skip, ssm_norm_g=ssm_norm_g, w_out=w_out,
               mlp_norm_g=mlp_norm_g, w_up=w_up, w_down=w_down)
    mom = dict(mix_norm_g=m_mix_norm_g, w_in=m_w_in, q_gain=m_q_gain, k_gain=m_k_gain, sinks=m_sinks, rel_bias=m_rel_bias,
               conv_w=m_conv_w, conv_b=m_conv_b, dt_bias=m_dt_bias, a_log=m_a_log, d_skip=m_d_skip, ssm_norm_g=m_ssm_norm_g,
               w_out=m_w_out, mlp_norm_g=m_mlp_norm_g, w_up=m_w_up, w_down=m_w_down)
    var = dict(mix_norm_g=v_mix_norm_g, w_in=v_w_in, q_gain=v_q_gain, k_gain=v_k_gain, sinks=v_sinks, rel_bias=v_rel_bias,
               conv_w=v_conv_w, conv_b=v_conv_b, dt_bias=v_dt_bias, a_log=v_a_log, d_skip=v_d_skip, ssm_norm_g=v_ssm_norm_g,
               w_out=v_w_out, mlp_norm_g=v_mlp_norm_g, w_up=v_w_up, w_down=v_w_down)
    xi, yi, ci = _coords()
    k_me = 2 * xi + yi
    kc_arr = jnp.stack([k_me, ci]).astype(jnp.int32)

    prov = _Exchange(wts, kc_arr)
    small_w = {n: wts[n] for n in SMALL}
    small_w["conv_w"] = prov.conv_w
    loss, dx, grads, tok = local_step(x[0], loss_target[0], small_w, prov)

    small_shapes = {n: grads[n].shape for n in SMALL}
    small_sum = small_all_reduce(_pack(grads, loss) + tok)
    loss = small_sum[PACK_ROWS - 1, PACK_COLS - 1]
    tok = prov.flush(small_sum)
    small = _unpack(small_sum, small_shapes)
    cols = conv_w.shape[-1]
    small["conv_w"] = lax.dynamic_slice_in_dim(small["conv_w"], k_me * cols, cols, axis=2)
    g_out_d = dict(small)
    gs = [small[n] for n in SMALL]
    gs[0] = gs[0] + tok
    ds, nms, nvs = adamw_small([wts[n] for n in SMALL], gs, [mom[n] for n in SMALL], [var[n] for n in SMALL])
    d_out_d, m_out_d, v_out_d = dict(zip(SMALL, ds)), dict(zip(SMALL, nms)), dict(zip(SMALL, nvs))

    rows, cols = wts["w_in"].shape[1:]
    to_cols = lambda a: jnp.transpose(a, (2, 0, 1))
    place = lambda g_t, l, g, pin: lax.dynamic_update_slice(g_t, to_cols(g).reshape(cols, 1, rows) + pin, (0, l, 0))
    pin = prov.prepare(("w_up", 0), ds[0])
    g_t = place(lax.empty((cols, DEPTH, rows), f32), 1, prov.reduced_piece(("w_in", 1), ds[0]), pin[0, 0])
    after = g_t
    for n in ("w_up", "w_down", "w_in", "w_out"):
        g0, g1 = (prov.reduced_piece((n, l), after) for l in range(DEPTH))
        if n == "w_in":
            g_t = place(g_t, 0, g0, 0.0)
            res_t = adamw_cols(to_cols(wts[n]), g_t, to_cols(mom[n]), to_cols(var[n]))
            g_out_d[n], d_out_d[n], m_out_d[n], v_out_d[n] = (jnp.transpose(a, (1, 2, 0)) for a in (g_t, *res_t))
        else:
            g_out_d[n], d_out_d[n], m_out_d[n], v_out_d[n] = adamw_shard(wts[n], g0, g1, mom[n], var[n])
        after = d_out_d[n]

    return (loss, dx[None], *[g_out_d[n] for n in WEIGHTS], *[d_out_d[n] for n in WEIGHTS],
            *[m_out_d[n] for n in WEIGHTS], *[v_out_d[n] for n in WEIGHTS])
```

```python
import numpy as np
import jax
import jax.numpy as jnp
from jax import lax
from jax.experimental import pallas as pl
from jax.experimental.pallas import tpu as pltpu

f32 = jnp.float32
bf16 = jnp.bfloat16

SEQ = 2048
D_MODEL = 1024
DEPTH = 2
HEAD_DIM = 64
N_Q_HEADS = 8
N_KV_HEADS = 2
Q_PER_KV = N_Q_HEADS // N_KV_HEADS
BLOCK = 128
N_BLOCKS = SEQ // BLOCK
N_BUCKETS = 32
MAX_DISTANCE = 128
SSM_HEADS = 8
SSM_HEAD_DIM = 64
SSM_GROUPS = 2
HEADS_PER_GROUP = SSM_HEADS // SSM_GROUPS
SSM_STATE = 128
CONV_WIDTH = 4
CHUNK = 128
N_CHUNKS = SEQ // CHUNK
D_FF = 4 * D_MODEL
D_ATTN = N_Q_HEADS * HEAD_DIM
D_KV = N_KV_HEADS * HEAD_DIM
D_SSM = SSM_HEADS * SSM_HEAD_DIM
D_BC = SSM_GROUPS * SSM_STATE
D_CONV = D_SSM + 2 * D_BC
D_IN = D_ATTN + 2 * D_KV + D_SSM + D_CONV + SSM_HEADS
EPS = 1e-6
NEG = -1e30
N_CHIPS = 4
FF_TILE = D_FF // N_CHIPS

LANE = 128
PW = D_ATTN + D_SSM + D_CONV + 2 * D_KV + LANE
OFF_Q, OFF_Z, OFF_X, OFF_K, OFF_V, OFF_DT = 0, 512, 1024, 2048, 2176, 2304

ADAM_LR = 0.001
ADAM_B1 = 0.9
ADAM_B2 = 0.999
ADAM_EPS = 1e-08
ADAM_WD = 0.01
ADAM_STEP = 10

VMEM_LIMIT = 56 * 1024 * 1024


def _params(*sem):
    return pltpu.CompilerParams(dimension_semantics=tuple(sem), vmem_limit_bytes=VMEM_LIMIT)


def _bdot(a, b):
    return jnp.dot(a.astype(bf16), b.astype(bf16), preferred_element_type=f32)


def _bdot_nt(a, b):
    return lax.dot_general(a.astype(bf16), b.astype(bf16), (((1,), (1,)), ((), ())), preferred_element_type=f32)


def _bdot_tn(a, b):
    return lax.dot_general(a.astype(bf16), b.astype(bf16), (((0,), (0,)), ((), ())), preferred_element_type=f32)


def _hdot(a, b):
    return jnp.dot(a, b, precision=lax.Precision.HIGHEST, preferred_element_type=f32)


def _sigmoid(x):
    return 1.0 / (1.0 + jnp.exp(-x))


def _softplus(x):
    return jnp.maximum(x, 0.0) + jnp.log1p(jnp.exp(-jnp.abs(x)))


def _rms(x):
    return lax.rsqrt(jnp.mean(x * x, axis=-1, keepdims=True) + EPS)


def _rms_bwd(dy, xhat, r, g):
    t = dy * g
    return r * (t - xhat * jnp.mean(t * xhat, axis=-1, keepdims=True))


def _full(shape):
    return pl.BlockSpec(shape, lambda *_: (0,) * len(shape))


def _bucket_table():
    qi = np.arange(BLOCK)[:, None]
    kj = np.arange(2 * BLOCK)[None, :]
    dist = qi + BLOCK - kj
    ok = (dist >= 0) & (dist < 128)
    d = np.clip(dist, 0, None)
    max_exact = N_BUCKETS // 2
    d_f = np.maximum(d, 1).astype(np.float32)
    large = max_exact + (np.log(d_f / np.float32(max_exact)) / np.float32(np.log(MAX_DISTANCE / max_exact))
                         * np.float32(N_BUCKETS - max_exact)).astype(np.int32)
    large = np.minimum(large, N_BUCKETS - 1)
    bucket = np.where(d < max_exact, d, large)
    return np.where(ok, bucket, -1).astype(np.int32)


def bias_build(rel_bias, bucket):
    def body(rel_ref, bkt_ref, o_ref):
        bkt = bkt_ref[...]
        for h in range(N_Q_HEADS):
            acc = jnp.where(bkt < 0, NEG, 0.0).astype(f32)
            for b in range(N_BUCKETS):
                acc = acc + jnp.where(bkt == b, rel_ref[b, h], 0.0)
            o_ref[h] = acc

    return pl.pallas_call(
        body, name="bias_build", out_shape=jax.ShapeDtypeStruct((N_Q_HEADS,) + bucket.shape, f32),
        in_specs=[pl.BlockSpec(memory_space=pltpu.SMEM), pl.BlockSpec(memory_space=pltpu.VMEM)],
        out_specs=pl.BlockSpec(memory_space=pltpu.VMEM),
    )(rel_bias, bucket)


def bias_bwd(dband0, dband1, bucket):
    def body(d0_ref, d1_ref, bkt_ref, o_ref):
        bkt = bkt_ref[...]
        o_ref[...] = jnp.zeros_like(o_ref)
        for h in range(N_Q_HEADS):
            d = d0_ref[h] + d1_ref[h]
            for b in range(N_BUCKETS):
                part = jnp.sum(jnp.where(bkt == b, d, 0.0), axis=1, keepdims=True)
                o_ref[b:b + 1, h:h + 1] = jnp.sum(part, axis=0, keepdims=True)

    return pl.pallas_call(
        body, name="bias_bwd", out_shape=jax.ShapeDtypeStruct((N_BUCKETS, LANE), f32),
    )(dband0, dband1, bucket)


W_IN_SHARD = D_IN // N_CHIPS
_ALIGNED_PIECES = ((0, 0, 512), (1, 190, 578), (2, 0, 124), (2, 124, 578), (3, 0, 570), (0, 512, 578), (1, 0, 62),
                   (1, 62, 190), (3, 570, 578))
_SHARD_PIECES = (((0, 512), (2048, 2114)), ((2114, 2176), (2176, 2304), (512, 900)), ((900, 1024), (1024, 1478)),
                 ((1478, 2048), (2304, 2312)))


def align_w_in(shards, tr=256):
    def body(s_ref, o_ref):
        parts = [s_ref[k, :, a:b] for k, a, b in _ALIGNED_PIECES]
        parts.append(jnp.zeros((tr, LANE - SSM_HEADS), s_ref.dtype))
        o_ref[...] = jnp.concatenate(parts, axis=-1)

    return pl.pallas_call(
        body, name="align_w_in", grid=(D_MODEL // tr,),
        in_specs=[pl.BlockSpec((N_CHIPS, tr, W_IN_SHARD), lambda i: (0, i, 0))],
        out_specs=pl.BlockSpec((tr, PW), lambda i: (i, 0)),
        out_shape=jax.ShapeDtypeStruct((D_MODEL, PW), shards.dtype),
        compiler_params=_params("arbitrary"),
    )(shards)


def split_w_in_grad(dw, tr=256):
    def body(d_ref, o16_ref):
        for k, pieces in enumerate(_SHARD_PIECES):
            o16_ref[k] = jnp.concatenate([d_ref[:, a:b] for a, b in pieces], axis=-1).astype(bf16)

    return pl.pallas_call(
        body, name="split_w_in_grad", grid=(D_MODEL // tr,),
        in_specs=[pl.BlockSpec((tr, PW), lambda i: (i, 0))],
        out_specs=pl.BlockSpec((N_CHIPS, tr, W_IN_SHARD), lambda i: (0, i, 0)),
        out_shape=jax.ShapeDtypeStruct((N_CHIPS, D_MODEL, W_IN_SHARD), bf16),
        compiler_params=_params("arbitrary"),
    )(dw)

def in_fwd(x, g, w, tm=512):
    def body(x_ref, g_ref, w_ref, o_ref):
        xv = x_ref[...]
        h = xv * _rms(xv) * g_ref[...]
        o_ref[...] = _bdot(h, w_ref[...])

    return pl.pallas_call(
        body, name="in_fwd", grid=(SEQ // tm,),
        in_specs=[pl.BlockSpec((tm, D_MODEL), lambda i: (i, 0)), _full((1, D_MODEL)), _resident((D_MODEL, PW))],
        out_specs=pl.BlockSpec((tm, PW), lambda i: (i, 0)),
        out_shape=jax.ShapeDtypeStruct((SEQ, PW), f32),
        compiler_params=_params("arbitrary"),
    )(x, g, w)


def _resident(shape):
    return pl.BlockSpec(shape, lambda *_: (0,) * len(shape), pipeline_mode=pl.Buffered(1))


def in_bwd(dq, dz, dxbc, dk, dv, ddt, x, g, w, dres, tm=512):
    def body(dq_ref, dz_ref, dx_ref, dk_ref, dv_ref, ddt_ref, x_ref, g_ref, w_ref, dres_ref, o_ref, dw_ref, dg_ref):
        i = pl.program_id(0)

        @pl.when(i == 0)
        def _():
            dw_ref[...] = jnp.zeros_like(dw_ref)
            dg_ref[...] = jnp.zeros_like(dg_ref)

        dproj = jnp.concatenate([dq_ref[...], dz_ref[...], dx_ref[...], dk_ref[...], dv_ref[...], ddt_ref[...]],
                                axis=-1).astype(bf16)
        xv = x_ref[...]
        r = _rms(xv)
        xhat = xv * r
        gv = g_ref[...]
        h = xhat * gv
        dw_ref[...] += _bdot_tn(h, dproj)
        dh = _bdot_nt(dproj, w_ref[...])
        dg_ref[...] += jnp.sum(dh * xhat, axis=0, keepdims=True)
        o_ref[...] = dres_ref[...] + _rms_bwd(dh, xhat, r, gv)

    tok = lambda w_: pl.BlockSpec((tm, w_), lambda i: (i, 0))
    return pl.pallas_call(
        body, name="in_bwd", grid=(SEQ // tm,),
        in_specs=[tok(D_ATTN), tok(D_SSM), tok(D_CONV), tok(D_KV), tok(D_KV), tok(LANE), tok(D_MODEL),
                  _full((1, D_MODEL)), _resident((D_MODEL, PW)), tok(D_MODEL)],
        out_specs=[tok(D_MODEL), _resident((D_MODEL, PW)), _full((1, D_MODEL))],
        out_shape=[jax.ShapeDtypeStruct((SEQ, D_MODEL), f32), jax.ShapeDtypeStruct((D_MODEL, PW), f32),
                   jax.ShapeDtypeStruct((1, D_MODEL), f32)],
        compiler_params=_params("arbitrary"),
    )(dq, dz, dxbc, dk, dv, ddt, x, g, w, dres)


def _attn_softmax_t(qk, bias_t, sink, first, key_row):
    s = qk * (HEAD_DIM ** -0.5) + bias_t
    s = jnp.where(jnp.logical_and(first, key_row < BLOCK), NEG, s)
    m = jnp.maximum(jnp.max(s, axis=0, keepdims=True), sink)
    p = jnp.exp(s - m)
    psink = jnp.exp(sink - m)
    inv = 1.0 / (jnp.sum(p, axis=0, keepdims=True) + psink)
    return p * inv, psink * inv


def _rms_t(x_t):
    return lax.rsqrt(jnp.mean(x_t * x_t, axis=0, keepdims=True) + EPS)


def attn_fwd_t(proj, q_gain_col, k_gain, sinks, bias_t):
    kcol, vcol = OFF_K // D_KV, OFF_V // D_KV

    def body(q_ref, kc_ref, kp_ref, vc_ref, vp_ref, qg_ref, kg_ref, sink_ref, bias_ref, o_ref, ot_scr):
        n = pl.program_id(0)
        first = n == 0
        key_row = lax.broadcasted_iota(jnp.int32, (2 * BLOCK, BLOCK), 0)
        k2 = jnp.concatenate([kp_ref[...], kc_ref[...]], axis=0)
        v_t = jnp.concatenate([vp_ref[...], vc_ref[...]], axis=0).T
        q_t = q_ref[...].T
        qg = jnp.broadcast_to(qg_ref[...], (HEAD_DIM, BLOCK))
        kg = kg_ref[...]
        for hk in range(N_KV_HEADS):
            sl = slice(hk * HEAD_DIM, (hk + 1) * HEAD_DIM)
            kk = k2[:, sl]
            kn = (kk * _rms(kk) * kg).astype(bf16)
            vt = v_t[sl, :].astype(bf16)
            heads = range(hk * Q_PER_KV, (hk + 1) * Q_PER_KV)
            qns = []
            for h in heads:
                qh = q_t[h * HEAD_DIM:(h + 1) * HEAD_DIM, :]
                qns.append(qh * _rms_t(qh) * qg)
            scores = [_bdot(kn, qn) for qn in qns]
            for h, s in zip(heads, scores):
                p, _ = _attn_softmax_t(s, bias_ref[h], sink_ref[h], first, key_row)
                ot_scr[h * HEAD_DIM:(h + 1) * HEAD_DIM, :] = _bdot(vt, p)
        o_ref[...] = ot_scr[...].T

    prev = lambda n: jnp.maximum(n - 1, 0)
    return pl.pallas_call(
        body, name="attn_fwd", grid=(N_BLOCKS,),
        in_specs=[pl.BlockSpec((BLOCK, D_ATTN), lambda n: (n, 0)),
                  pl.BlockSpec((BLOCK, D_KV), lambda n: (n, kcol)), pl.BlockSpec((BLOCK, D_KV), lambda n: (prev(n), kcol)),
                  pl.BlockSpec((BLOCK, D_KV), lambda n: (n, vcol)), pl.BlockSpec((BLOCK, D_KV), lambda n: (prev(n), vcol)),
                  _full((HEAD_DIM, 1)), _full((1, HEAD_DIM)), pl.BlockSpec(memory_space=pltpu.SMEM),
                  _full((N_Q_HEADS, 2 * BLOCK, BLOCK))],
        out_specs=pl.BlockSpec((BLOCK, D_ATTN), lambda n: (n, 0)),
        out_shape=jax.ShapeDtypeStruct((SEQ, D_ATTN), f32),
        scratch_shapes=[pltpu.VMEM((D_ATTN, BLOCK), f32)],
        compiler_params=_params("arbitrary"),
    )(proj, proj, proj, proj, proj, q_gain_col, k_gain, sinks, bias_t)


def attn_bwd_t(proj, d_out, q_gain_col, k_gain, sinks, bias_t):
    kcol, vcol = OFF_K // D_KV, OFF_V // D_KV

    def body(q_ref, kc_ref, kp_ref, vc_ref, vp_ref, do_ref, qg_ref, kg_ref, sink_ref, bias_ref,
             dq_ref, dk_ref, dv_ref, dband_ref, dsink_ref, dqg_ref, dkg_ref, dkn_scr, dv_scr, dqt_scr, dsink_acc, dqg_acc):
        i = pl.program_id(0)
        first = i == N_BLOCKS - 1

        @pl.when(i == 0)
        def _():
            for ref in (dband_ref, dkg_ref, dkn_scr, dv_scr, dsink_acc, dqg_acc):
                ref[...] = jnp.zeros_like(ref)

        key_row = lax.broadcasted_iota(jnp.int32, (2 * BLOCK, BLOCK), 0)
        k2 = jnp.concatenate([kp_ref[...], kc_ref[...]], axis=0)
        v2 = jnp.concatenate([vp_ref[...], vc_ref[...]], axis=0)
        q_t = q_ref[...].T
        do_t = do_ref[...].T
        qg = jnp.broadcast_to(qg_ref[...], (HEAD_DIM, BLOCK))
        kg = kg_ref[...]
        scale = HEAD_DIM ** -0.5
        for hk in range(N_KV_HEADS):
            sl = slice(hk * HEAD_DIM, (hk + 1) * HEAD_DIM)
            kk = k2[:, sl]
            rk = _rms(kk)
            khat = kk * rk
            kn = (khat * kg).astype(bf16)
            vb = v2[:, sl].astype(bf16)
            dkn = jnp.zeros((2 * BLOCK, HEAD_DIM), f32)
            dvv = jnp.zeros((2 * BLOCK, HEAD_DIM), f32)
            heads = range(hk * Q_PER_KV, (hk + 1) * Q_PER_KV)
            rqs, qhats, qns, d_os = [], [], [], []
            for h in heads:
                hs = slice(h * HEAD_DIM, (h + 1) * HEAD_DIM)
                qh = q_t[hs, :]
                rqs.append(_rms_t(qh))
                qhats.append(qh * rqs[-1])
                qns.append((qhats[-1] * qg).astype(bf16))
                d_os.append(do_t[hs, :].astype(bf16))
            scores = [_bdot(kn, qn) for qn in qns]
            dps = [_bdot(vb, d_o) for d_o in d_os]
            ps, dss = [], []
            for h, s, dp in zip(heads, scores, dps):
                p, psink = _attn_softmax_t(s, bias_ref[h], sink_ref[h], first, key_row)
                delta = jnp.sum(p * dp, axis=0, keepdims=True)
                ds = p * (dp - delta)
                dband_ref[h] += ds
                dsink_acc[h:h + 1, :] += -(psink * delta)
                ps.append(p.astype(bf16))
                dss.append(ds.astype(bf16))
            dqns = [_bdot_tn(kn, ds) * scale for ds in dss]
            for ds, qn, p, d_o in zip(dss, qns, ps, d_os):
                dkn = dkn + _bdot_nt(ds, qn) * scale
                dvv = dvv + _bdot_nt(p, d_o)
            for h, dqn, rq, qhat in zip(heads, dqns, rqs, qhats):
                dqg_acc[...] += dqn * qhat
                t = dqn * qg
                dqt_scr[h * HEAD_DIM:(h + 1) * HEAD_DIM, :] = rq * (t - qhat * jnp.mean(t * qhat, axis=0, keepdims=True))
            dkn_cur = dkn[BLOCK:] + dkn_scr[:, sl]
            dkn_scr[:, sl] = dkn[:BLOCK]
            khat_c, rk_c = khat[BLOCK:], rk[BLOCK:]
            dkg_ref[...] += jnp.sum(dkn_cur * khat_c, axis=0, keepdims=True)
            dk_ref[:, sl] = _rms_bwd(dkn_cur, khat_c, rk_c, kg)
            dv_ref[:, sl] = dvv[BLOCK:] + dv_scr[:, sl]
            dv_scr[:, sl] = dvv[:BLOCK]
        dq_ref[...] = dqt_scr[...].T

        @pl.when(i == N_BLOCKS - 1)
        def _():
            dsink_ref[...] = jnp.sum(dsink_acc[...], axis=1, keepdims=True)
            dqg_ref[...] = jnp.sum(dqg_acc[...], axis=1, keepdims=True)

    blk = lambda i: N_BLOCKS - 1 - i
    prev = lambda i: jnp.maximum(N_BLOCKS - 2 - i, 0)
    return pl.pallas_call(
        body, name="attn_bwd", grid=(N_BLOCKS,),
        in_specs=[pl.BlockSpec((BLOCK, D_ATTN), lambda i: (blk(i), 0)),
                  pl.BlockSpec((BLOCK, D_KV), lambda i: (blk(i), kcol)), pl.BlockSpec((BLOCK, D_KV), lambda i: (prev(i), kcol)),
                  pl.BlockSpec((BLOCK, D_KV), lambda i: (blk(i), vcol)), pl.BlockSpec((BLOCK, D_KV), lambda i: (prev(i), vcol)),
                  pl.BlockSpec((BLOCK, D_ATTN), lambda i: (blk(i), 0)),
                  _full((HEAD_DIM, 1)), _full((1, HEAD_DIM)), pl.BlockSpec(memory_space=pltpu.SMEM),
                  _full((N_Q_HEADS, 2 * BLOCK, BLOCK))],
        out_specs=[pl.BlockSpec((BLOCK, D_ATTN), lambda i: (blk(i), 0)), pl.BlockSpec((BLOCK, D_KV), lambda i: (blk(i), 0)),
                   pl.BlockSpec((BLOCK, D_KV), lambda i: (blk(i), 0)), _full((N_Q_HEADS, 2 * BLOCK, BLOCK)),
                   _full((N_Q_HEADS, 1)), _full((HEAD_DIM, 1)), _full((1, HEAD_DIM))],
        out_shape=[jax.ShapeDtypeStruct((SEQ, D_ATTN), f32), jax.ShapeDtypeStruct((SEQ, D_KV), f32),
                   jax.ShapeDtypeStruct((SEQ, D_KV), f32), jax.ShapeDtypeStruct((N_Q_HEADS, 2 * BLOCK, BLOCK), f32),
                   jax.ShapeDtypeStruct((N_Q_HEADS, 1), f32), jax.ShapeDtypeStruct((HEAD_DIM, 1), f32),
                   jax.ShapeDtypeStruct((1, HEAD_DIM), f32)],
        scratch_shapes=[pltpu.VMEM((BLOCK, D_KV), f32), pltpu.VMEM((BLOCK, D_KV), f32), pltpu.VMEM((D_ATTN, BLOCK), f32),
                        pltpu.VMEM((N_Q_HEADS, BLOCK), f32), pltpu.VMEM((HEAD_DIM, BLOCK), f32)],
        compiler_params=_params("arbitrary"),
    )(proj, proj, proj, proj, proj, d_out, q_gain_col, k_gain, sinks, bias_t)


SUBLANES = 8


def _shift_down(u, s, row8):
    if s == 0:
        return u
    r = pltpu.roll(u, s, 0)
    return jnp.concatenate([jnp.where(row8 >= s, r[:SUBLANES], 0.0), r[SUBLANES:]], axis=0)


def _shift_up(u, s, row8):
    if s == 0:
        return u
    r = pltpu.roll(u, SEQ - s, 0)
    return jnp.concatenate([r[:-SUBLANES], jnp.where(row8 < SUBLANES - s, r[-SUBLANES:], 0.0)], axis=0)


def conv_fwd(proj, conv_w, conv_b):
    xcol = OFF_X // LANE

    def body(u_ref, w_ref, b_ref, o_ref):
        u = u_ref[...]
        row = lax.broadcasted_iota(jnp.int32, (SUBLANES, LANE), 0)
        pre = b_ref[...] + jnp.zeros_like(u)
        for k in range(CONV_WIDTH):
            pre = pre + w_ref[k:k + 1, :] * _shift_down(u, CONV_WIDTH - 1 - k, row)
        o_ref[...] = pre * _sigmoid(pre)

    return pl.pallas_call(
        body, name="conv_fwd", grid=(D_CONV // LANE,),
        in_specs=[pl.BlockSpec((SEQ, LANE), lambda j: (0, xcol + j)), pl.BlockSpec((CONV_WIDTH, LANE), lambda j: (0, j)),
                  pl.BlockSpec((1, LANE), lambda j: (0, j))],
        out_specs=pl.BlockSpec((SEQ, LANE), lambda j: (0, j)),
        out_shape=jax.ShapeDtypeStruct((SEQ, D_CONV), f32),
        compiler_params=_params("arbitrary"),
    )(proj, conv_w, conv_b)


def conv_bwd(proj, d_act, conv_w, conv_b):
    xcol = OFF_X // LANE

    def body(u_ref, da_ref, w_ref, b_ref, du_ref, dw_ref, db_ref):
        u = u_ref[...]
        row = lax.broadcasted_iota(jnp.int32, (SUBLANES, LANE), 0)
        shifted = [_shift_down(u, CONV_WIDTH - 1 - k, row) for k in range(CONV_WIDTH)]
        pre = b_ref[...] + jnp.zeros_like(u)
        for k in range(CONV_WIDTH):
            pre = pre + w_ref[k:k + 1, :] * shifted[k]
        sg = _sigmoid(pre)
        dpre = da_ref[...] * (sg * (1.0 + pre * (1.0 - sg)))
        db_ref[...] = jnp.sum(dpre, axis=0, keepdims=True)
        du = jnp.zeros_like(u)
        for k in range(CONV_WIDTH):
            dw_ref[k:k + 1, :] = jnp.sum(dpre * shifted[k], axis=0, keepdims=True)
            du = du + w_ref[k:k + 1, :] * _shift_up(dpre, CONV_WIDTH - 1 - k, row)
        du_ref[...] = du

    return pl.pallas_call(
        body, name="conv_bwd", grid=(D_CONV // LANE,),
        in_specs=[pl.BlockSpec((SEQ, LANE), lambda j: (0, xcol + j)), pl.BlockSpec((SEQ, LANE), lambda j: (0, j)),
                  pl.BlockSpec((CONV_WIDTH, LANE), lambda j: (0, j)), pl.BlockSpec((1, LANE), lambda j: (0, j))],
        out_specs=[pl.BlockSpec((SEQ, LANE), lambda j: (0, j)), pl.BlockSpec((CONV_WIDTH, LANE), lambda j: (0, j)),
                   pl.BlockSpec((1, LANE), lambda j: (0, j))],
        out_shape=[jax.ShapeDtypeStruct((SEQ, D_CONV), f32), jax.ShapeDtypeStruct((CONV_WIDTH, D_CONV), f32),
                   jax.ShapeDtypeStruct((1, D_CONV), f32)],
        compiler_params=_params("arbitrary"),
    )(proj, d_act, conv_w, conv_b)


def _ssd_chunk_common(dt_raw, dtb, alog):
    row = lax.broadcasted_iota(jnp.int32, (CHUNK, CHUNK), 0)
    col = lax.broadcasted_iota(jnp.int32, (CHUNK, CHUNK), 1)
    tri = (row >= col).astype(f32)
    strict = (row > col).astype(f32)
    dtp = _softplus(dt_raw + dtb)
    a_row = -jnp.exp(alog)
    d_a = dtp * a_row
    cs = _hdot(tri, d_a)
    cs_last = cs[CHUNK - 1:CHUNK, :]
    return row, col, dtp, a_row, cs, cs.T, cs_last


def _seg_decay(cs, cs_t, hd, row, col):
    seg = cs[:, hd:hd + 1] - cs_t[hd:hd + 1, :]
    return jnp.where(row >= col, jnp.exp(seg), 0.0)


GROUP_W = HEADS_PER_GROUP * SSM_HEAD_DIM


def _group_indicator(g):
    j = lax.broadcasted_iota(jnp.int32, (GROUP_W, LANE), 0)
    lane = lax.broadcasted_iota(jnp.int32, (GROUP_W, LANE), 1)
    return (lane == g * HEADS_PER_GROUP + j // SSM_HEAD_DIM).astype(bf16)


def _bf16_pieces(a, n):
    pieces = []
    for _ in range(n):
        p = a.astype(bf16)
        pieces.append(p)
        a = a - p.astype(f32)
    return pieces


def _head_spread(a, ind):
    return sum(lax.dot_general(p, ind, (((1,), (1,)), ((), ())), preferred_element_type=f32) for p in _bf16_pieces(a, 3))


def _head_sums(a, ind):
    return sum(jnp.dot(p, ind, preferred_element_type=f32) for p in _bf16_pieces(a, 2))


def ssd_fwd_g(act, proj, dt_bias, a_log, d_skip, norm_g):
    zcol, dtcol = OFF_Z // D_SSM, OFF_DT // LANE

    def body(act_ref, z_ref, dt_ref, dtb_ref, alog_ref, dsk_ref, ng_ref, out_ref, ypre_ref, st_ref, state):
        c = pl.program_id(0)

        @pl.when(c == 0)
        def _():
            state[...] = jnp.zeros_like(state)

        row, col, dtp, a_row, cs, cs_t, cs_last = _ssd_chunk_common(dt_ref[...], dtb_ref[...], alog_ref[...])
        e_cs = jnp.exp(cs)
        dte = jnp.exp(cs_last - cs)
        rows8 = jnp.concatenate([jnp.exp(cs_last), dsk_ref[...], jnp.zeros((6, LANE), f32)], axis=0)
        z = z_ref[...]
        sz = z * _sigmoid(z)
        ng = ng_ref[...]
        for g in range(SSM_GROUPS):
            gs = slice(g * GROUP_W, (g + 1) * GROUP_W)
            ind = _group_indicator(g)
            xg = act_ref[:, gs]
            bg = act_ref[:, D_SSM + g * SSM_STATE:D_SSM + (g + 1) * SSM_STATE]
            cg = act_ref[:, D_SSM + D_BC + g * SSM_STATE:D_SSM + D_BC + (g + 1) * SSM_STATE]
            dt_e, e_e, dte_e = _head_spread(dtp, ind), _head_spread(e_cs, ind), _head_spread(dte, ind)
            rows_e = _head_spread(rows8, ind)
            ecl_e, dsk_e = rows_e[0:1], rows_e[1:2]
            xdt = xg * dt_e
            prev = state[g]
            st_ref[0, g] = prev
            cb = _bdot_nt(cg, bg)
            goff = _bdot(cg, prev)
            snew = _bdot_tn(bg, xdt * dte_e)
            heads = range(g * HEADS_PER_GROUP, (g + 1) * HEADS_PER_GROUP)
            ms = [cb * _seg_decay(cs, cs_t, hd, row, col) for hd in heads]
            yd = [_bdot(m, xdt[:, r * SSM_HEAD_DIM:(r + 1) * SSM_HEAD_DIM]) for r, m in enumerate(ms)]
            y = jnp.concatenate(yd, axis=1) + e_e * goff + xg * dsk_e
            state[g] = prev * ecl_e + snew
            ypre_ref[:, gs] = y
            part = y * sz[:, gs]
            out_ref[:, gs] = part * _rms(part) * ng[:, gs]

    return pl.pallas_call(
        body, name="ssd_fwd", grid=(N_CHUNKS,),
        in_specs=[pl.BlockSpec((CHUNK, D_CONV), lambda c: (c, 0)), pl.BlockSpec((CHUNK, D_SSM), lambda c: (c, zcol)),
                  pl.BlockSpec((CHUNK, LANE), lambda c: (c, dtcol)), _full((1, LANE)), _full((1, LANE)), _full((1, LANE)),
                  _full((1, D_SSM))],
        out_specs=[pl.BlockSpec((CHUNK, D_SSM), lambda c: (c, 0)), pl.BlockSpec((CHUNK, D_SSM), lambda c: (c, 0)),
                   pl.BlockSpec((1, SSM_GROUPS, SSM_STATE, GROUP_W), lambda c: (c, 0, 0, 0))],
        out_shape=[jax.ShapeDtypeStruct((SEQ, D_SSM), f32), jax.ShapeDtypeStruct((SEQ, D_SSM), f32),
                   jax.ShapeDtypeStruct((N_CHUNKS, SSM_GROUPS, SSM_STATE, GROUP_W), f32)],
        scratch_shapes=[pltpu.VMEM((SSM_GROUPS, SSM_STATE, GROUP_W), f32)],
        compiler_params=_params("arbitrary"),
    )(act, proj, proj, dt_bias, a_log, d_skip, norm_g)


def ssd_bwd_g(act, proj, ypre, states, d_out, dt_bias, a_log, d_skip, norm_g):
    zcol, dtcol = OFF_Z // D_SSM, OFF_DT // LANE

    def body(act_ref, z_ref, dt_ref, ypre_ref, st_ref, do_ref, dtb_ref, alog_ref, dsk_ref, ng_ref,
             dact_ref, ddt_ref, dz_ref, dng_ref, dpar_ref, dstate):
        i = pl.program_id(0)

        @pl.when(i == 0)
        def _():
            for ref in (dng_ref, dpar_ref, dstate):
                ref[...] = jnp.zeros_like(ref)

        row, col, dtp, a_row, cs, cs_t, cs_last = _ssd_chunk_common(dt_ref[...], dtb_ref[...], alog_ref[...])
        upper = (row <= col).astype(f32)
        lane = lax.broadcasted_iota(jnp.int32, (CHUNK, LANE), 1)
        rowl = lax.broadcasted_iota(jnp.int32, (CHUNK, LANE), 0)
        e_cs = jnp.exp(cs)
        dte = jnp.exp(cs_last - cs)
        ecl = jnp.exp(cs_last)
        rows8 = jnp.concatenate([ecl, dsk_ref[...], jnp.zeros((6, LANE), f32)], axis=0)
        z = z_ref[...]
        sgz = _sigmoid(z)
        sz = z * sgz
        ng = ng_ref[...]
        ddt_mat = jnp.zeros((CHUNK, LANE), f32)
        dcs_mat = jnp.zeros((CHUNK, LANE), f32)
        dcs_t = jnp.zeros((LANE, CHUNK), f32)
        dcsl_row = jnp.zeros((1, LANE), f32)
        dd_row = jnp.zeros((1, LANE), f32)
        for g in range(SSM_GROUPS):
            gs = slice(g * GROUP_W, (g + 1) * GROUP_W)
            bsl = slice(D_SSM + g * SSM_STATE, D_SSM + (g + 1) * SSM_STATE)
            csl = slice(D_SSM + D_BC + g * SSM_STATE, D_SSM + D_BC + (g + 1) * SSM_STATE)
            ind = _group_indicator(g)
            y = ypre_ref[:, gs]
            part = y * sz[:, gs]
            r = _rms(part)
            yhat = part * r
            d_o = do_ref[:, gs]
            dng_ref[:, gs] += jnp.sum(d_o * yhat, axis=0, keepdims=True)
            dyz = _rms_bwd(d_o, yhat, r, ng[:, gs])
            dy = dyz * sz[:, gs]
            dz_ref[:, gs] = dyz * y * (sgz[:, gs] * (1.0 + z[:, gs] * (1.0 - sgz[:, gs])))

            xg = act_ref[:, gs]
            bg = act_ref[:, bsl]
            cg = act_ref[:, csl]
            dt_e, e_e, dte_e = _head_spread(dtp, ind), _head_spread(e_cs, ind), _head_spread(dte, ind)
            rows_e = _head_spread(rows8, ind)
            ecl_e, dsk_e = rows_e[0:1], rows_e[1:2]
            xdt = xg * dt_e
            prev = st_ref[0, g]
            dh = dstate[g]
            heads = range(g * HEADS_PER_GROUP, (g + 1) * HEADS_PER_GROUP)
            hsl = [slice(r_ * SSM_HEAD_DIM, (r_ + 1) * SSM_HEAD_DIM) for r_ in range(HEADS_PER_GROUP)]
            cb = _bdot_nt(cg, bg)
            lms = [_seg_decay(cs, cs_t, hd, row, col) for hd in heads]
            ms = [cb * lm for lm in lms]
            gmat = _bdot(cg, prev)
            dgm = dy * e_e
            dcg = _bdot_nt(dgm, prev)
            dprev = _bdot_tn(cg, dgm)
            dbg = _bdot_nt(xdt * dte_e, dh)
            dw = _bdot(bg, dh)
            dms = [_bdot_nt(dy[:, s_], xdt[:, s_]) for s_ in hsl]
            dxdts = [_bdot_tn(m, dy[:, s_]) for m, s_ in zip(ms, hsl)]
            dxdt = jnp.concatenate(dxdts, axis=1) + dw * dte_e
            dact_ref[:, gs] = dy * dsk_e + dxdt * dt_e
            dstate[g] = dprev + dh * ecl_e
            dcb = jnp.zeros((CHUNK, CHUNK), f32)
            for hd, dm, lm, m in zip(heads, dms, lms, ms):
                dcb = dcb + dm * lm
                dseg = dm * m
                dcs_mat = dcs_mat + jnp.where(lane == hd, jnp.sum(dseg, axis=1, keepdims=True), 0.0)
                dcs_t = jnp.where(row == hd, jnp.sum(dseg, axis=0, keepdims=True), dcs_t)
            dact_ref[:, bsl] = dbg + _bdot_tn(dcb, cg)
            dact_ref[:, csl] = dcg + _bdot(dcb, bg)
            ddte = _head_sums(dw * xdt, ind) * dte
            dcs_mat = dcs_mat + _head_sums(dy * gmat, ind) * e_cs - ddte
            ddt_mat = ddt_mat + _head_sums(dxdt * xg, ind)
            dcsl_row = (dcsl_row + jnp.sum(ddte, axis=0, keepdims=True)
                        + jnp.sum(_head_sums(dh * prev, ind), axis=0, keepdims=True) * ecl)
            dd_row = dd_row + jnp.sum(_head_sums(dy * xg, ind), axis=0, keepdims=True)
        dcs_mat = dcs_mat - dcs_t.T + jnp.where(rowl == CHUNK - 1, dcsl_row, 0.0)
        dda = _hdot(upper, dcs_mat)
        ddt_mat = ddt_mat + dda * a_row
        da_row = jnp.sum(dda * dtp, axis=0, keepdims=True)
        ddt_raw = ddt_mat * _sigmoid(dt_ref[...] + dtb_ref[...])
        ddt_ref[...] = ddt_raw
        dpar_ref[0:1, :] += jnp.sum(ddt_raw, axis=0, keepdims=True)
        dpar_ref[1:2, :] += da_row * a_row
        dpar_ref[2:3, :] += dd_row

    blk = lambda i: N_CHUNKS - 1 - i
    return pl.pallas_call(
        body, name="ssd_bwd", grid=(N_CHUNKS,),
        in_specs=[pl.BlockSpec((CHUNK, D_CONV), lambda i: (blk(i), 0)), pl.BlockSpec((CHUNK, D_SSM), lambda i: (blk(i), zcol)),
                  pl.BlockSpec((CHUNK, LANE), lambda i: (blk(i), dtcol)), pl.BlockSpec((CHUNK, D_SSM), lambda i: (blk(i), 0)),
                  pl.BlockSpec((1, SSM_GROUPS, SSM_STATE, GROUP_W), lambda i: (blk(i), 0, 0, 0)),
                  pl.BlockSpec((CHUNK, D_SSM), lambda i: (blk(i), 0)),
                  _full((1, LANE)), _full((1, LANE)), _full((1, LANE)), _full((1, D_SSM))],
        out_specs=[pl.BlockSpec((CHUNK, D_CONV), lambda i: (blk(i), 0)), pl.BlockSpec((CHUNK, LANE), lambda i: (blk(i), 0)),
                   pl.BlockSpec((CHUNK, D_SSM), lambda i: (blk(i), 0)), _full((1, D_SSM)), _full((8, LANE))],
        out_shape=[jax.ShapeDtypeStruct((SEQ, D_CONV), f32), jax.ShapeDtypeStruct((SEQ, LANE), f32),
                   jax.ShapeDtypeStruct((SEQ, D_SSM), f32), jax.ShapeDtypeStruct((1, D_SSM), f32),
                   jax.ShapeDtypeStruct((8, LANE), f32)],
        scratch_shapes=[pltpu.VMEM((SSM_GROUPS, SSM_STATE, GROUP_W), f32)],
        compiler_params=_params("arbitrary"),
    )(act, proj, proj, ypre, states, d_out, dt_bias, a_log, d_skip, norm_g)


def out_fwd(x, attn, ssm, w_out, tm=512):
    def body(x_ref, a_ref, s_ref, w_ref, o_ref):
        o_ref[...] = x_ref[...] + _bdot(a_ref[...], w_ref[:D_ATTN, :]) + _bdot(s_ref[...], w_ref[D_ATTN:, :])

    tok = lambda w_: pl.BlockSpec((tm, w_), lambda i: (i, 0))
    return pl.pallas_call(
        body, name="out_fwd", grid=(SEQ // tm,),
        in_specs=[tok(D_MODEL), tok(D_ATTN), tok(D_SSM), _full((D_MODEL, D_MODEL))],
        out_specs=tok(D_MODEL), out_shape=jax.ShapeDtypeStruct((SEQ, D_MODEL), f32),
        compiler_params=_params("arbitrary"),
    )(x, attn, ssm, w_out)


def out_bwd(dx1, attn, ssm, w_out, tm=512):
    nt = SEQ // tm

    def body(d_ref, a_ref, s_ref, w_ref, da_ref, ds_ref, dw16_ref, dw_ref):
        i = pl.program_id(0)

        @pl.when(i == 0)
        def _():
            dw_ref[...] = jnp.zeros_like(dw_ref)

        d = d_ref[...].astype(bf16)
        dcat = _bdot_nt(d, w_ref[...])
        da_ref[...] = dcat[:, :D_ATTN]
        ds_ref[...] = dcat[:, D_ATTN:]
        dw_ref[:D_ATTN, :] += _bdot_tn(a_ref[...], d)
        dw_ref[D_ATTN:, :] += _bdot_tn(s_ref[...], d)

        @pl.when(i == nt - 1)
        def _():
            dw16_ref[...] = dw_ref[...].astype(bf16)

    tok = lambda w_: pl.BlockSpec((tm, w_), lambda i: (i, 0))
    return pl.pallas_call(
        body, name="out_bwd", grid=(nt,),
        in_specs=[tok(D_MODEL), tok(D_ATTN), tok(D_SSM), _resident((D_MODEL, D_MODEL))],
        out_specs=[tok(D_ATTN), tok(D_SSM), _resident((D_MODEL, D_MODEL))],
        out_shape=[jax.ShapeDtypeStruct((SEQ, D_ATTN), f32), jax.ShapeDtypeStruct((SEQ, D_SSM), f32),
                   jax.ShapeDtypeStruct((D_MODEL, D_MODEL), bf16)],
        scratch_shapes=[pltpu.VMEM((D_MODEL, D_MODEL), f32)],
        compiler_params=_params("arbitrary"),
    )(dx1, attn, ssm, w_out)


MLP_SUB = 256


def mlp_fwd(x1, g, w_up, w_down, tm=1024):
    def body(x_ref, g_ref, wu_ref, wd_ref, o_ref, u_ref, h_scr):
        j = pl.program_id(1)

        @pl.when(j == 0)
        def _():
            xv = x_ref[...]
            h_scr[...] = (xv * _rms(xv) * g_ref[...]).astype(bf16)
            o_ref[...] = xv

        for r in range(tm // MLP_SUB):
            rows = slice(r * MLP_SUB, (r + 1) * MLP_SUB)
            u = jnp.dot(h_scr[rows, :], wu_ref[...], preferred_element_type=f32)
            u_ref[rows, :] = u
            a = jnp.square(jnp.maximum(u, 0.0))
            o_ref[rows, :] += _bdot(a, wd_ref[...])

    return pl.pallas_call(
        body, name="mlp_fwd", grid=(SEQ // tm, N_CHIPS),
        in_specs=[pl.BlockSpec((tm, D_MODEL), lambda i, j: (i, 0)), _full((1, D_MODEL)),
                  pl.BlockSpec((None, D_MODEL, FF_TILE), lambda i, j: (j, 0, 0)),
                  pl.BlockSpec((None, FF_TILE, D_MODEL), lambda i, j: (j, 0, 0))],
        out_specs=[pl.BlockSpec((tm, D_MODEL), lambda i, j: (i, 0)), pl.BlockSpec((tm, FF_TILE), lambda i, j: (i, j))],
        out_shape=[jax.ShapeDtypeStruct((SEQ, D_MODEL), f32), jax.ShapeDtypeStruct((SEQ, D_FF), f32)],
        scratch_shapes=[pltpu.VMEM((tm, D_MODEL), bf16)],
        compiler_params=_params("arbitrary", "arbitrary"),
    )(x1, g, w_up, w_down)


def mlp_bwd_data(dx2, u, x1, g, w_up, w_down, tm=1024):
    def body(d_ref, u_ref, x_ref, g_ref, wu_ref, wd_ref, dx_ref, du_ref, dg_ref, dh_scr):
        i, j = pl.program_id(0), pl.program_id(1)

        @pl.when(jnp.logical_and(i == 0, j == 0))
        def _():
            dg_ref[...] = jnp.zeros_like(dg_ref)

        @pl.when(j == 0)
        def _():
            dh_scr[...] = jnp.zeros_like(dh_scr)

        for r in range(tm // MLP_SUB):
            rows = slice(r * MLP_SUB, (r + 1) * MLP_SUB)
            da = _bdot_nt(d_ref[rows, :], wd_ref[...])
            du = (da * (2.0 * jnp.maximum(u_ref[rows, :], 0.0))).astype(bf16)
            du_ref[rows, :] = du
            dh_scr[rows, :] += _bdot_nt(du, wu_ref[...])

        @pl.when(j == N_CHIPS - 1)
        def _():
            xv = x_ref[...]
            r = _rms(xv)
            xhat = xv * r
            dh = dh_scr[...]
            dg_ref[...] += jnp.sum(dh * xhat, axis=0, keepdims=True)
            dx_ref[...] = d_ref[...] + _rms_bwd(dh, xhat, r, g_ref[...])

    return pl.pallas_call(
        body, name="mlp_bwd_data", grid=(SEQ // tm, N_CHIPS),
        in_specs=[pl.BlockSpec((tm, D_MODEL), lambda i, j: (i, 0)), pl.BlockSpec((tm, FF_TILE), lambda i, j: (i, j)),
                  pl.BlockSpec((tm, D_MODEL), lambda i, j: (i, 0)), _full((1, D_MODEL)),
                  pl.BlockSpec((None, D_MODEL, FF_TILE), lambda i, j: (j, 0, 0)),
                  pl.BlockSpec((None, FF_TILE, D_MODEL), lambda i, j: (j, 0, 0))],
        out_specs=[pl.BlockSpec((tm, D_MODEL), lambda i, j: (i, 0)), pl.BlockSpec((tm, FF_TILE), lambda i, j: (i, j)),
                   _full((1, D_MODEL))],
        out_shape=[jax.ShapeDtypeStruct((SEQ, D_MODEL), f32), jax.ShapeDtypeStruct((SEQ, D_FF), bf16),
                   jax.ShapeDtypeStruct((1, D_MODEL), f32)],
        scratch_shapes=[pltpu.VMEM((tm, D_MODEL), f32)],
        compiler_params=_params("arbitrary", "arbitrary"),
    )(dx2, u, x1, g, w_up, w_down)


def mlp_bwd_weights(dx2, u, du, x1, g, tm=512):
    nt = SEQ // tm

    def body(d_ref, u_ref, du_ref, x_ref, g_ref, dwu16_ref, dwd16_ref, h_scr, d_scr, dwu_ref, dwd_ref):
        j, i = pl.program_id(0), pl.program_id(1)

        @pl.when(j == 0)
        def _():
            xv = x_ref[...]
            h_scr[i] = (xv * _rms(xv) * g_ref[...]).T.astype(bf16)
            d_scr[i] = d_ref[...].astype(bf16)

        @pl.when(i == 0)
        def _():
            dwu_ref[...] = jnp.zeros_like(dwu_ref)
            dwd_ref[...] = jnp.zeros_like(dwd_ref)

        dwu_ref[...] += jnp.dot(h_scr[i], du_ref[...], preferred_element_type=f32)
        a = jnp.square(jnp.maximum(u_ref[...], 0.0))
        dwd_ref[...] += _bdot_tn(a, d_scr[i])

        @pl.when(i == nt - 1)
        def _():
            dwu16_ref[...] = dwu_ref[...].astype(bf16)
            dwd16_ref[...] = dwd_ref[...].astype(bf16)

    up = pl.BlockSpec((None, D_MODEL, FF_TILE), lambda j, i: (j, 0, 0))
    down = pl.BlockSpec((None, FF_TILE, D_MODEL), lambda j, i: (j, 0, 0))
    first_pass = pl.BlockSpec((tm, D_MODEL), lambda j, i: (jnp.where(j == 0, i, nt - 1), 0))
    return pl.pallas_call(
        body, name="mlp_bwd_weights", grid=(N_CHIPS, nt),
        in_specs=[first_pass, pl.BlockSpec((tm, FF_TILE), lambda j, i: (i, j)),
                  pl.BlockSpec((tm, FF_TILE), lambda j, i: (i, j)), first_pass, _full((1, D_MODEL))],
        out_specs=[up, down],
        out_shape=[jax.ShapeDtypeStruct((N_CHIPS, D_MODEL, FF_TILE), bf16), jax.ShapeDtypeStruct((N_CHIPS, FF_TILE, D_MODEL), bf16)],
        scratch_shapes=[pltpu.VMEM((nt, D_MODEL, tm), bf16), pltpu.VMEM((nt, tm, D_MODEL), bf16),
                        pltpu.VMEM((D_MODEL, FF_TILE), f32), pltpu.VMEM((FF_TILE, D_MODEL), f32)],
        compiler_params=_params("arbitrary", "arbitrary"),
    )(dx2, u, du, x1, g)


def loss_head(y, target, tm=512):
    def body(y_ref, t_ref, dy_ref, l_ref):
        @pl.when(pl.program_id(0) == 0)
        def _():
            l_ref[...] = jnp.zeros_like(l_ref)

        d = y_ref[...] - t_ref[...]
        dy_ref[...] = d * (1.0 / D_MODEL)
        part = jnp.sum(jnp.mean(d * d, axis=-1, keepdims=True), axis=0, keepdims=True)
        l_ref[...] += 0.5 * part

    tok = pl.BlockSpec((tm, D_MODEL), lambda i: (i, 0))
    return pl.pallas_call(
        body, name="loss_head", grid=(SEQ // tm,), in_specs=[tok, tok], out_specs=[tok, _full((1, 1))],
        out_shape=[jax.ShapeDtypeStruct((SEQ, D_MODEL), f32), jax.ShapeDtypeStruct((1, 1), f32)],
        compiler_params=_params("arbitrary"),
    )(y, target)


def _pad_lane(v):
    return jnp.pad(v, (0, LANE - v.shape[0]))[None, :]


def local_step(x, target, w, prov):
    bucket = jnp.asarray(_bucket_table().T)
    bias = bias_build(w["rel_bias"], bucket)
    saved = []
    for l in range(DEPTH):
        g_mix = w["mix_norm_g"][l][None, :] + prov.stage(("begin", l), x)
        w_in = prov.w_in(l, x)
        proj = in_fwd(x, g_mix, w_in)
        conv_b = w["conv_b"][l][None, :]
        act = conv_fwd(proj, w["conv_w"][l], conv_b)
        dtb = _pad_lane(w["dt_bias"][l]) + prov.stage(("mid", l), act)
        alog, dsk = _pad_lane(w["a_log"][l]), _pad_lane(w["d_skip"][l])
        ng = w["ssm_norm_g"][l][None, :]
        ssm, ypre, states = ssd_fwd_g(act, proj, dtb, alog, dsk, ng)
        qg, kg = w["q_gain"][l][:, None] + 0.0 * ssm[:1, :1], w["k_gain"][l][None, :]
        attn = attn_fwd_t(proj, qg, kg, w["sinks"][l], bias)
        tok = prov.stage(("pre_out", l), attn)
        w_out = prov.w_out(l, attn) + jnp.asarray(tok, bf16)
        x1 = out_fwd(x, attn, ssm, w_out)
        g_mlp = w["mlp_norm_g"][l][None, :] + prov.stage(("pre_mlp", l), x1)
        w_up, w_down = prov.mlp(l, x1)
        x2, u = mlp_fwd(x1, g_mlp, w_up, w_down)
        saved.append(dict(x=x, proj=proj, attn=attn, act=act, ssm=ssm, ypre=ypre, states=states, x1=x1, u=u,
                          g_mix=g_mix, qg=qg, kg=kg, conv_b=conv_b, dtb=dtb, alog=alog, dsk=dsk, ng=ng, g_mlp=g_mlp,
                          w_in=w_in, w_out=w_out, w_up=w_up, w_down=w_down))
        x = x2
    dx, loss = loss_head(x, target)
    grads = [None] * DEPTH
    dbands = [None] * DEPTH
    tok = 0.0
    for l in reversed(range(DEPTH)):
        s = saved[l]
        g_mlp = s["g_mlp"] + tok
        dx1, du, dg_mlp = mlp_bwd_data(dx, s["u"], s["x1"], g_mlp, s["w_up"], s["w_down"])
        dw_up, dw_down = mlp_bwd_weights(dx, s["u"], du, s["x1"], g_mlp)
        tok = prov.grads(("mlp", l), dict(w_up=dw_up, w_down=dw_down), dx1)
        dattn, dssm, dw_out = out_bwd(dx1, s["attn"], s["ssm"], s["w_out"])
        dact, ddt, dz, dng, dpar = ssd_bwd_g(s["act"], s["proj"], s["ypre"], s["states"], dssm, s["dtb"] + tok, s["alog"],
                                           s["dsk"], s["ng"])
        conv_b = s["conv_b"] + prov.stage(("bwd_mid", l), dact)
        dxbc, dconv_w, dconv_b = conv_bwd(s["proj"], dact, w["conv_w"][l], conv_b)
        dq, dk, dv, dband, dsink, dqg, dkg = attn_bwd_t(s["proj"], dattn, s["qg"], s["kg"], w["sinks"][l], bias)
        dbands[l] = dband
        g_mix = s["g_mix"]
        if l == 0:
            d_rel = bias_bwd(dbands[0], dbands[1], bucket)
            g_mix = g_mix + 0.0 * d_rel[:1, :1]
        dx, dw_in, dg_mix = in_bwd(dq, dz, dxbc, dk, dv, ddt, s["x"], g_mix, s["w_in"], dx1)
        tok = prov.grads(("mix", l), dict(w_in=split_w_in_grad(dw_in), w_out=dw_out), dx)
        grads[l] = dict(mix_norm_g=dg_mix[0], q_gain=dqg[:, 0], k_gain=dkg[0], sinks=dsink[:, 0],
                        conv_w=dconv_w, conv_b=dconv_b[0], dt_bias=dpar[0, :SSM_HEADS], a_log=dpar[1, :SSM_HEADS],
                        d_skip=dpar[2, :SSM_HEADS], ssm_norm_g=dng[0], mlp_norm_g=dg_mlp[0])
    out = {k: jnp.stack([grads[l][k] for l in range(DEPTH)]) for k in grads[0]}
    out["rel_bias"] = d_rel[:, :N_Q_HEADS]
    return loss, dx, out, tok


MESH = pl.DeviceIdType.MESH
HBM = pl.BlockSpec(memory_space=pltpu.HBM)
N_DEVICES = 8


def _coords():
    return lax.axis_index("x"), lax.axis_index("y"), lax.axis_index("c")


def _peer_chips(x, y):
    return [(1 - x, y), (x, 1 - y), (1 - x, 1 - y)]


def _remote(src, dst, send_sem, recv_sem, device):
    return pltpu.make_async_remote_copy(src_ref=src, dst_ref=dst, send_sem=send_sem, recv_sem=recv_sem,
                                        device_id=device, device_id_type=MESH)


SEM = pl.BlockSpec(memory_space=pltpu.SEMAPHORE)
ANY = pl.BlockSpec(memory_space=pl.ANY)
DATAFLOW = pltpu.SideEffectType.DATAFLOW_SIDE_EFFECTING


def _gather_copies(kind, src_refs, land_refs, ssem, rsem):
    x, y, c = _coords()
    k_me = 2 * x + y
    n = len(land_refs)
    cps = []
    for p, land in enumerate(land_refs):
        hr = land.shape[1] // 2
        rows = pl.ds(c * hr, hr)
        for j, chip in enumerate(_peer_chips(x, y)):
            i = 3 * p + j
            if kind == "ici":
                cps.append(_remote(src_refs[p].at[rows, :], land.at[k_me, rows, :], ssem.at[i], rsem.at[i], (*chip, c)))
            else:
                got = land.at[2 * chip[0] + chip[1], rows, :]
                cps.append(_remote(got, got, ssem.at[i], rsem.at[i], (x, y, 1 - c)))
        if kind == "relay":
            cps.append(_remote(src_refs[p], land.at[k_me], ssem.at[3 * n + p], rsem.at[3 * n + p], (x, y, 1 - c)))
    return cps


def gather_now(srcs, conv):
    n = len(srcs)

    def body(*refs):
        src_refs, conv_ref = refs[:n], refs[n]
        lands, gconv = refs[n + 1:2 * n + 1], refs[2 * n + 1]
        ssem, rsem, fsem, frsem, csem, crsem = refs[2 * n + 2:]
        x, y, c = _coords()
        k_me = 2 * x + y
        targets = [(*chip, c) for chip in _peer_chips(x, y)] + [(x, y, 1 - c)]
        ici = _gather_copies("ici", src_refs, lands, ssem, rsem)
        relay = _gather_copies("relay", src_refs, lands, fsem, frsem)
        passed = [cp for i, cp in enumerate(relay) if i % 4 != 3]
        own = relay[3::4]
        conv_cps = [_remote(conv_ref, gconv.at[k_me], csem.at[j], crsem.at[j], t) for j, t in enumerate(targets)]
        for cp in ici + conv_cps + own:
            cp.start()
        for cp, fw in zip(ici, passed):
            cp.wait_recv()
            fw.start()
        for cp in conv_cps + relay:
            cp.wait_recv()
        for cp in ici + relay + conv_cps:
            cp.wait_send()

    out_shape = [jax.ShapeDtypeStruct((N_CHIPS,) + s.shape, s.dtype) for s in srcs]
    out_shape.append(jax.ShapeDtypeStruct((N_CHIPS,) + conv.shape, conv.dtype))
    sems = lambda k: pltpu.SemaphoreType.DMA((k,))
    return pl.pallas_call(
        body, name="gather_now", out_shape=out_shape, in_specs=[HBM] * (n + 1), out_specs=[HBM] * (n + 1),
        scratch_shapes=[sems(3 * n), sems(3 * n), sems(4 * n), sems(4 * n), sems(N_CHIPS), sems(N_CHIPS)],
    )(*srcs, conv)


def _gather_maker(kind, n_src):
    def make(refs, ssem, rsem):
        cps = _gather_copies(kind, refs[:n_src], refs[n_src:], ssem, rsem)
        return cps, cps
    return make


def _scatter_maker(n):
    def make(refs, ssem, rsem):
        x, y, c = _coords()
        k_me = 2 * x + y
        sends, arrivals = [], []
        for p in range(n):
            src, land = refs[p], refs[n + p]
            sends.append(_remote(src.at[k_me, 1 - c], land.at[0], ssem.at[7 * p], rsem.at[7 * p], (x, y, 1 - c)))
            for j, chip in enumerate(_peer_chips(x, y)):
                for cc in range(2):
                    sends.append(_remote(src.at[2 * chip[0] + chip[1], cc], land.at[1 + 2 * j + c],
                                         ssem.at[7 * p + 1 + 2 * j + cc], rsem.at[7 * p + 1 + 2 * j + c], (*chip, cc)))
            for s in range(7):
                arrivals.append(_remote(land.at[s], land.at[s], ssem.at[7 * p + s], rsem.at[7 * p + s], (x, y, 1 - c)))
        return sends, arrivals
    return make


def _share_maker(n):
    def make(refs, ssem, rsem):
        x, y, c = _coords()
        sends = [_remote(refs[p].at[c], refs[p].at[c], ssem.at[p], rsem.at[p], (x, y, 1 - c)) for p in range(n)]
        arrivals = [_remote(refs[p].at[1 - c], refs[p].at[1 - c], ssem.at[p], rsem.at[p], (x, y, 1 - c)) for p in range(n)]
        return sends, arrivals
    return make


def split_start(name, make, n_sems, operands, after):
    n = len(operands)

    def body(*refs):
        ssem, rsem, token = refs[n + 1], refs[n + 2], refs[-1]
        for cp in make(refs[:n], ssem, rsem)[0]:
            cp.start()
        token[...] = jnp.zeros_like(token)

    ops = [pltpu.with_memory_space_constraint(a, pltpu.HBM) for a in operands]
    outs = pl.pallas_call(
        body, name=name,
        out_shape=(pltpu.SemaphoreType.DMA((n_sems,)), pltpu.SemaphoreType.DMA((n_sems,)),
                   *[pltpu.HBM(a.shape, a.dtype) for a in ops], jax.ShapeDtypeStruct((8, LANE), f32)),
        in_specs=[HBM] * n + [ANY], out_specs=(SEM, SEM, *[HBM] * n, pl.BlockSpec(memory_space=pltpu.VMEM)),
        input_output_aliases={i: 2 + i for i in range(n)},
        compiler_params=pltpu.CompilerParams(has_side_effects=DATAFLOW),
    )(*ops, after)
    return dict(name=name, make=make, ssem=outs[0], rsem=outs[1], operands=outs[2:2 + n], token=outs[-1][0, 0],
                tokens=outs[-1])


def split_wait(handle, after):
    n = len(handle["operands"])

    def body(*refs):
        sends, arrivals = handle["make"](refs[:n], refs[n], refs[n + 1])
        for cp in sends:
            cp.wait_send()
        for cp in arrivals:
            cp.wait_recv()

    outs = pl.pallas_call(
        body, name=handle["name"].replace("start", "wait"),
        out_shape=tuple(pltpu.HBM(a.shape, a.dtype) for a in handle["operands"]),
        in_specs=[HBM] * n + [SEM, SEM, ANY], out_specs=tuple([HBM] * n),
        input_output_aliases={i: i for i in range(n)},
        compiler_params=pltpu.CompilerParams(has_side_effects=DATAFLOW),
    )(*handle["operands"], handle["ssem"], handle["rsem"], after)
    return list(outs)


def piece_sum(g, recv, kc_arr):
    _, _, rb, cc = g.shape
    tr = min(256, rb)

    def body(kc_ref, g_ref, r_ref, o_ref):
        acc = g_ref[...].astype(f32)
        for s in range(7):
            acc = acc + r_ref[s].astype(f32)
        o_ref[...] = acc

    return pl.pallas_call(
        body, name="piece_sum",
        grid_spec=pltpu.PrefetchScalarGridSpec(
            num_scalar_prefetch=1, grid=(rb // tr,),
            in_specs=[pl.BlockSpec((None, None, tr, cc), lambda r, kc: (kc[0], kc[1], r, 0)),
                      pl.BlockSpec((7, tr, cc), lambda r, kc: (0, r, 0))],
            out_specs=pl.BlockSpec((None, tr, cc), lambda r, kc: (kc[1], r, 0))),
        out_shape=jax.ShapeDtypeStruct((2, rb, cc), f32),
        compiler_params=_params("arbitrary"),
    )(kc_arr, g, recv)


def small_all_reduce(vec):
    def body(v_ref, o_ref, gat, ssem, rsem):
        x, y, c = _coords()
        me = 4 * x + 2 * y + c
        gat[me] = v_ref[...]
        sends = []
        for t in range(1, N_DEVICES):
            peer = (x ^ (t >> 2), y ^ ((t >> 1) & 1), c ^ (t & 1))
            cp = _remote(v_ref, gat.at[me], ssem.at[t - 1], rsem.at[t - 1], peer)
            cp.start()
            sends.append(cp)
        for t in range(1, N_DEVICES):
            peer = (x ^ (t >> 2), y ^ ((t >> 1) & 1), c ^ (t & 1))
            slot = gat.at[4 * peer[0] + 2 * peer[1] + peer[2]]
            _remote(slot, slot, ssem.at[t - 1], rsem.at[t - 1], peer).wait_recv()
        for cp in sends:
            cp.wait_send()
        acc = gat[0]
        for d in range(1, N_DEVICES):
            acc = acc + gat[d]
        o_ref[...] = acc

    return pl.pallas_call(
        body, name="small_all_reduce", out_shape=jax.ShapeDtypeStruct(vec.shape, vec.dtype),
        in_specs=[pl.BlockSpec(memory_space=pltpu.VMEM)], out_specs=pl.BlockSpec(memory_space=pltpu.VMEM),
        scratch_shapes=[pltpu.VMEM((N_DEVICES,) + vec.shape, vec.dtype), pltpu.SemaphoreType.DMA((N_DEVICES - 1,)),
                        pltpu.SemaphoreType.DMA((N_DEVICES - 1,))],
    )(vec)


def _adamw_math(w, g, m, v):
    m_new = ADAM_B1 * m + (1.0 - ADAM_B1) * g
    v_new = ADAM_B2 * v + (1.0 - ADAM_B2) * jnp.square(g)
    m_hat = m_new / (1.0 - ADAM_B1 ** ADAM_STEP)
    v_hat = v_new / (1.0 - ADAM_B2 ** ADAM_STEP)
    delta = -ADAM_LR * (m_hat / (jnp.sqrt(v_hat) + ADAM_EPS) + ADAM_WD * w)
    return delta, m_new, v_new


def adamw_shard(w, g0, g1, m, v):
    depth, rows, cols = w.shape
    half = rows // 2
    tr = min(256, half)
    nr = half // tr

    def body(w_ref, g0_ref, g1_ref, m_ref, v_ref, go_ref, d_ref, nm_ref, nv_ref):
        gv = jnp.where(pl.program_id(0) == 0, g0_ref[...], g1_ref[...])
        go_ref[...] = gv
        d_ref[...], nm_ref[...], nv_ref[...] = _adamw_math(w_ref[...], gv, m_ref[...], v_ref[...])

    spec = pl.BlockSpec((None, tr, cols), lambda l, h, r: (l, h * nr + r, 0))
    g0spec = pl.BlockSpec((None, tr, cols), lambda l, h, r: (jnp.where(l == 0, h, 1), jnp.where(l == 0, r, nr - 1), 0))
    g1spec = pl.BlockSpec((None, tr, cols), lambda l, h, r: (jnp.where(l == 1, h, 0), jnp.where(l == 1, r, 0), 0))
    return pl.pallas_call(
        body, name="adamw_shard", grid=(depth, 2, nr), in_specs=[spec, g0spec, g1spec, spec, spec], out_specs=[spec] * 4,
        out_shape=[jax.ShapeDtypeStruct(w.shape, f32)] * 4,
        compiler_params=_params("arbitrary", "arbitrary", "arbitrary"),
    )(w, g0, g1, m, v)


def adamw_cols(w, g, m, v, tc=34):
    cols, depth, rows = w.shape

    def body(w_ref, g_ref, m_ref, v_ref, d_ref, nm_ref, nv_ref):
        d_ref[...], nm_ref[...], nv_ref[...] = _adamw_math(w_ref[...], g_ref[...], m_ref[...], v_ref[...])

    spec = pl.BlockSpec((tc, depth, rows), lambda i: (i, 0, 0))
    return pl.pallas_call(
        body, name="adamw_cols", grid=(cols // tc,), in_specs=[spec] * 4, out_specs=[spec] * 3,
        out_shape=[jax.ShapeDtypeStruct(w.shape, f32)] * 3,
        compiler_params=_params("arbitrary"),
    )(w, g, m, v)


def adamw_small(ws, gs, ms, vs):
    n = len(ws)

    def body(*refs):
        ins, outs = refs[:4 * n], refs[4 * n:]
        for i in range(n):
            w_ref, g_ref, m_ref, v_ref = (ins[k * n + i] for k in range(4))
            outs[i][...], outs[n + i][...], outs[2 * n + i][...] = _adamw_math(w_ref[...], g_ref[...], m_ref[...], v_ref[...])

    outs = pl.pallas_call(
        body, name="adamw_small", out_shape=[jax.ShapeDtypeStruct(w.shape, f32) for w in ws] * 3,
    )(*ws, *gs, *ms, *vs)
    return outs[:n], outs[n:2 * n], outs[2 * n:]


WEIGHTS = ("mix_norm_g", "w_in", "q_gain", "k_gain", "sinks", "rel_bias", "conv_w", "conv_b", "dt_bias", "a_log", "d_skip",
           "ssm_norm_g", "w_out", "mlp_norm_g", "w_up", "w_down")
BIG = ("w_in", "w_out", "w_up", "w_down")
SMALL = tuple(n for n in WEIGHTS if n not in BIG)
PACK_COLS = 1024
PACK_ROWS = 16


def _pack(named, last=None):
    flat = jnp.concatenate([named[n].reshape(-1) for n in SMALL])
    tail = jnp.zeros((1,), f32) if last is None else last.reshape(1)
    pad = jnp.zeros((PACK_ROWS * PACK_COLS - flat.shape[0] - 1,), f32)
    return jnp.concatenate([flat, pad, tail]).reshape(PACK_ROWS, PACK_COLS)


def _unpack(buf, shapes):
    flat = buf.reshape(-1)
    out, at = {}, 0
    for n in SMALL:
        size = int(np.prod(shapes[n]))
        out[n] = flat[at:at + size].reshape(shapes[n])
        at += size
    return out


class _Exchange:
    GROUPS = {"A": (("w_up", 0), ("w_down", 0)), "B": (("w_in", 1), ("w_out", 1)), "C": (("w_up", 1), ("w_down", 1))}
    ICI_AT = {("mid", 0): "B", ("pre_out", 0): "C"}
    RELAY_AT = {("pre_out", 0): "A", ("pre_mlp", 0): "B", ("mid", 1): "C"}
    LAST = ("mix", 0)
    IN_FLIGHT = 2

    def __init__(self, wts, kc_arr):
        self.wts, self.kc_arr = wts, kc_arr
        self.own = {(n, l): wts[n][l].astype(bf16) for n in BIG for l in range(DEPTH)}
        now = gather_now([self.own["w_in", 0], self.own["w_out", 0]], wts["conv_w"])
        self.ready = {("w_in", 0): now[0], ("w_out", 0): now[1]}
        self.conv_w = jnp.transpose(now[2], (1, 2, 0, 3)).reshape(DEPTH, CONV_WIDTH, D_CONV)
        self.ici, self.relay = {}, {}
        self.scatter, self.share, self.reduced = [], [], {}
        self._start_ici("A", now[2])

    def _start_ici(self, g, after):
        srcs = [self.own[p] for p in self.GROUPS[g]]
        lands = [lax.empty((N_CHIPS,) + s.shape, s.dtype) for s in srcs]
        self.ici[g] = split_start("gather%s_ici_start" % g, _gather_maker("ici", len(srcs)), 3 * len(srcs), srcs + lands,
                                  after)
        return self.ici[g]["token"]

    def stage(self, name, after):
        if name == ("begin", 0):
            return self.ici["A"]["token"]
        tok = 0.0
        g = self.RELAY_AT.get(name)
        if g is not None:
            n = len(self.GROUPS[g])
            self.relay[g] = split_start("gather%s_relay_start" % g, _gather_maker("relay", n), 4 * n,
                                        split_wait(self.ici[g], after), after)
            tok = self.relay[g]["token"]
        if name in self.ICI_AT:
            tok = tok + self._start_ici(self.ICI_AT[name], after)
        return tok

    def _get(self, piece, after):
        if piece not in self.ready:
            g = [k for k, pieces in self.GROUPS.items() if piece in pieces][0]
            lands = split_wait(self.relay[g], after)[len(self.GROUPS[g]):]
            self.ready.update(zip(self.GROUPS[g], lands))
        return self.ready[piece]

    def w_in(self, l, after):
        return align_w_in(self._get(("w_in", l), after))

    def w_out(self, l, after):
        return self._get(("w_out", l), after).reshape(D_MODEL, D_MODEL)

    def mlp(self, l, after):
        return self._get(("w_up", l), after), self._get(("w_down", l), after)

    def _view(self, n, g):
        _, rows, cols = self.wts[n].shape
        return g.reshape(N_CHIPS, 2, rows // 2, cols)

    def grads(self, name, arrays, after):
        if name == self.LAST:
            self.held = (name, arrays)
            return 0.0
        return self._scatter(name, arrays, after) + self._advance(after, self.IN_FLIGHT)

    def flush(self, after):
        return self._scatter(*self.held, after) + self._advance(after, self.IN_FLIGHT)

    def _scatter(self, name, arrays, after):
        pieces = [(n, name[1]) for n in arrays]
        views = [self._view(n, g) for n, g in arrays.items()]
        lands = [lax.empty((7,) + v.shape[2:], bf16) for v in views]
        h = split_start("scatter_%s%d_start" % name, _scatter_maker(len(views)), 7 * len(views), views + lands, after)
        self.scatter.append((pieces, h))
        return h["token"]

    def _take_share(self, after):
        pieces, h = self.share.pop(0)
        self.reduced.update(zip(pieces, split_wait(h, after)))

    def _take_scatter(self, after):
        pieces, h = self.scatter.pop(0)
        done = split_wait(h, after)
        views, lands = done[:len(pieces)], done[len(pieces):]
        sums = [piece_sum(v, land, self.kc_arr) for v, land in zip(views, lands)]
        hs = split_start(h["name"].replace("scatter", "share"), _share_maker(len(sums)), len(sums), sums, after)
        self.share.append((pieces, hs))
        return hs["token"]

    def _advance(self, after, newest):
        if self.share:
            self._take_share(after)
        return self._take_scatter(after) if len(self.scatter) > newest else 0.0

    def prepare(self, piece, after):
        while piece not in self.reduced and not any(piece in pieces for pieces, _ in self.share):
            self._take_scatter(after)
        return self.share[-1][1]["tokens"] if self.share else after

    def reduced_piece(self, piece, after):
        while piece not in self.reduced:
            if any(piece in pieces for pieces, _ in self.share):
                self._take_share(after)
            else:
                self._take_scatter(after)
        return self.reduced[piece]


def kernel(x, mix_norm_g, w_in, q_gain, k_gain, sinks, rel_bias, conv_w, conv_b, dt_bias, a_log, d_skip, ssm_norm_g, w_out, mlp_norm_g, w_up, w_down, loss_target, m_mix_norm_g, m_w_in, m_q_gain, m_k_gain, m_sinks, m_rel_bias, m_conv_w, m_conv_b, m_dt_bias, m_a_log, m_d_skip, m_ssm_norm_g, m_w_out, m_mlp_norm_g, m_w_up, m_w_down, v_mix_norm_g, v_w_in, v_q_gain, v_k_gain, v_sinks, v_rel_bias, v_conv_w, v_conv_b, v_dt_bias, v_a_log, v_d_skip, v_ssm_norm_g, v_w_out, v_mlp_norm_g, v_w_up, v_w_down):
    wts = dict(mix_norm_g=mix_norm_g, w_in=w_in, q_gain=q_gain, k_gain=k_gain, sinks=sinks, rel_bias=rel_bias, conv_w=conv_w,
               conv_b=conv_b, dt_bias=dt_bias, a_log=a_log, d_skip=d_skip, ssm_norm_g=ssm_norm_g, w_out=w_out,
               mlp_norm_g=mlp_norm_g, w_up=w_up, w_down=w_down)
    mom = dict(mix_norm_g=m_mix_norm_g, w_in=m_w_in, q_gain=m_q_gain, k_gain=m_k_gain, sinks=m_sinks, rel_bias=m_rel_bias,
               conv_w=m_conv_w, conv_b=m_conv_b, dt_bias=m_dt_bias, a_log=m_a_log, d_skip=m_d_skip, ssm_norm_g=m_ssm_norm_g,
               w_out=m_w_out, mlp_norm_g=m_mlp_norm_g, w_up=m_w_up, w_down=m_w_down)
    var = dict(mix_norm_g=v_mix_norm_g, w_in=v_w_in, q_gain=v_q_gain, k_gain=v_k_gain, sinks=v_sinks, rel_bias=v_rel_bias,
               conv_w=v_conv_w, conv_b=v_conv_b, dt_bias=v_dt_bias, a_log=v_a_log, d_skip=v_d_skip, ssm_norm_g=v_ssm_norm_g,
               w_out=v_w_out, mlp_norm_g=v_mlp_norm_g, w_up=v_w_up, w_down=v_w_down)
    xi, yi, ci = _coords()
    k_me = 2 * xi + yi
    kc_arr = jnp.stack([k_me, ci]).astype(jnp.int32)

    prov = _Exchange(wts, kc_arr)
    small_w = {n: wts[n] for n in SMALL}
    small_w["conv_w"] = prov.conv_w
    loss, dx, grads, tok = local_step(x[0], loss_target[0], small_w, prov)

    small_shapes = {n: grads[n].shape for n in SMALL}
    small_sum = small_all_reduce(_pack(grads, loss) + tok)
    loss = small_sum[PACK_ROWS - 1, PACK_COLS - 1]
    tok = prov.flush(small_sum)
    small = _unpack(small_sum, small_shapes)
    cols = conv_w.shape[-1]
    small["conv_w"] = lax.dynamic_slice_in_dim(small["conv_w"], k_me * cols, cols, axis=2)
    g_out_d = dict(small)
    gs = [small[n] for n in SMALL]
    gs[0] = gs[0] + tok
    ds, nms, nvs = adamw_small([wts[n] for n in SMALL], gs, [mom[n] for n in SMALL], [var[n] for n in SMALL])
    d_out_d, m_out_d, v_out_d = dict(zip(SMALL, ds)), dict(zip(SMALL, nms)), dict(zip(SMALL, nvs))

    rows, cols = wts["w_in"].shape[1:]
    to_cols = lambda a: jnp.transpose(a, (2, 0, 1))
    pin = prov.prepare(("w_up", 0), ds[0])
    g1_t = to_cols(prov.reduced_piece(("w_in", 1), ds[0])).reshape(cols, rows) + pin[0, 0]
    after = g1_t
    for n in ("w_up", "w_down", "w_in", "w_out"):
        g0, g1 = (prov.reduced_piece((n, l), after) for l in range(DEPTH))
        if n == "w_in":
            g_t = jnp.stack([to_cols(g0).reshape(cols, rows), g1_t], axis=1)
            res_t = adamw_cols(to_cols(wts[n]), g_t, to_cols(mom[n]), to_cols(var[n]))
            g_out_d[n], d_out_d[n], m_out_d[n], v_out_d[n] = (jnp.transpose(a, (1, 2, 0)) for a in (g_t, *res_t))
        else:
            g_out_d[n], d_out_d[n], m_out_d[n], v_out_d[n] = adamw_shard(wts[n], g0, g1, mom[n], var[n])
        after = d_out_d[n]

    return (loss, dx[None], *[g_out_d[n] for n in WEIGHTS], *[d_out_d[n] for n in WEIGHTS],
            *[m_out_d[n] for n in WEIGHTS], *[v_out_d[n] for n in WEIGHTS])
```

```python
import numpy as np
import jax
import jax.numpy as jnp
from jax import lax
from jax.experimental import pallas as pl
from jax.experimental.pallas import tpu as pltpu

f32 = jnp.float32
bf16 = jnp.bfloat16

SEQ = 2048
D_MODEL = 1024
DEPTH = 2
HEAD_DIM = 64
N_Q_HEADS = 8
N_KV_HEADS = 2
Q_PER_KV = N_Q_HEADS // N_KV_HEADS
BLOCK = 128
N_BLOCKS = SEQ // BLOCK
N_BUCKETS = 32
MAX_DISTANCE = 128
SSM_HEADS = 8
SSM_HEAD_DIM = 64
SSM_GROUPS = 2
HEADS_PER_GROUP = SSM_HEADS // SSM_GROUPS
SSM_STATE = 128
CONV_WIDTH = 4
CHUNK = 128
N_CHUNKS = SEQ // CHUNK
D_FF = 4 * D_MODEL
D_ATTN = N_Q_HEADS * HEAD_DIM
D_KV = N_KV_HEADS * HEAD_DIM
D_SSM = SSM_HEADS * SSM_HEAD_DIM
D_BC = SSM_GROUPS * SSM_STATE
D_CONV = D_SSM + 2 * D_BC
D_IN = D_ATTN + 2 * D_KV + D_SSM + D_CONV + SSM_HEADS
EPS = 1e-6
NEG = -1e30
N_CHIPS = 4
FF_TILE = D_FF // N_CHIPS

LANE = 128
PW = D_ATTN + D_SSM + D_CONV + 2 * D_KV + LANE
OFF_Q, OFF_Z, OFF_X, OFF_K, OFF_V, OFF_DT = 0, 512, 1024, 2048, 2176, 2304

ADAM_LR = 0.001
ADAM_B1 = 0.9
ADAM_B2 = 0.999
ADAM_EPS = 1e-08
ADAM_WD = 0.01
ADAM_STEP = 10

VMEM_LIMIT = 56 * 1024 * 1024


def _params(*sem):
    return pltpu.CompilerParams(dimension_semantics=tuple(sem), vmem_limit_bytes=VMEM_LIMIT)


def _bdot(a, b):
    return jnp.dot(a.astype(bf16), b.astype(bf16), preferred_element_type=f32)


def _bdot_nt(a, b):
    return lax.dot_general(a.astype(bf16), b.astype(bf16), (((1,), (1,)), ((), ())), preferred_element_type=f32)


def _bdot_tn(a, b):
    return lax.dot_general(a.astype(bf16), b.astype(bf16), (((0,), (0,)), ((), ())), preferred_element_type=f32)


def _hdot(a, b):
    return jnp.dot(a, b, precision=lax.Precision.HIGHEST, preferred_element_type=f32)


def _sigmoid(x):
    return 1.0 / (1.0 + jnp.exp(-x))


def _softplus(x):
    return jnp.maximum(x, 0.0) + jnp.log1p(jnp.exp(-jnp.abs(x)))


def _rms(x):
    return lax.rsqrt(jnp.mean(x * x, axis=-1, keepdims=True) + EPS)


def _rms_bwd(dy, xhat, r, g):
    t = dy * g
    return r * (t - xhat * jnp.mean(t * xhat, axis=-1, keepdims=True))


def _full(shape):
    return pl.BlockSpec(shape, lambda *_: (0,) * len(shape))


def _bucket_table():
    qi = np.arange(BLOCK)[:, None]
    kj = np.arange(2 * BLOCK)[None, :]
    dist = qi + BLOCK - kj
    ok = (dist >= 0) & (dist < 128)
    d = np.clip(dist, 0, None)
    max_exact = N_BUCKETS // 2
    d_f = np.maximum(d, 1).astype(np.float32)
    large = max_exact + (np.log(d_f / np.float32(max_exact)) / np.float32(np.log(MAX_DISTANCE / max_exact))
                         * np.float32(N_BUCKETS - max_exact)).astype(np.int32)
    large = np.minimum(large, N_BUCKETS - 1)
    bucket = np.where(d < max_exact, d, large)
    return np.where(ok, bucket, -1).astype(np.int32)


def bias_build(rel_bias, bucket):
    def body(rel_ref, bkt_ref, o_ref):
        bkt = bkt_ref[...]
        for h in range(N_Q_HEADS):
            acc = jnp.where(bkt < 0, NEG, 0.0).astype(f32)
            for b in range(N_BUCKETS):
                acc = acc + jnp.where(bkt == b, rel_ref[b, h], 0.0)
            o_ref[h] = acc

    return pl.pallas_call(
        body, name="bias_build", out_shape=jax.ShapeDtypeStruct((N_Q_HEADS,) + bucket.shape, f32),
        in_specs=[pl.BlockSpec(memory_space=pltpu.SMEM), pl.BlockSpec(memory_space=pltpu.VMEM)],
        out_specs=pl.BlockSpec(memory_space=pltpu.VMEM),
    )(rel_bias, bucket)


def bias_bwd(dband0, dband1, bucket):
    def body(d0_ref, d1_ref, bkt_ref, o_ref):
        bkt = bkt_ref[...]
        o_ref[...] = jnp.zeros_like(o_ref)
        for h in range(N_Q_HEADS):
            d = d0_ref[h] + d1_ref[h]
            for b in range(N_BUCKETS):
                part = jnp.sum(jnp.where(bkt == b, d, 0.0), axis=1, keepdims=True)
                o_ref[b:b + 1, h:h + 1] = jnp.sum(part, axis=0, keepdims=True)

    return pl.pallas_call(
        body, name="bias_bwd", out_shape=jax.ShapeDtypeStruct((N_BUCKETS, LANE), f32),
    )(dband0, dband1, bucket)


W_IN_SHARD = D_IN // N_CHIPS
_ALIGNED_PIECES = ((0, 0, 512), (1, 190, 578), (2, 0, 124), (2, 124, 578), (3, 0, 570), (0, 512, 578), (1, 0, 62),
                   (1, 62, 190), (3, 570, 578))
_SHARD_PIECES = (((0, 512), (2048, 2114)), ((2114, 2176), (2176, 2304), (512, 900)), ((900, 1024), (1024, 1478)),
                 ((1478, 2048), (2304, 2312)))


def align_w_in(shards, tr=256):
    def body(s_ref, o_ref):
        parts = [s_ref[k, :, a:b] for k, a, b in _ALIGNED_PIECES]
        parts.append(jnp.zeros((tr, LANE - SSM_HEADS), s_ref.dtype))
        o_ref[...] = jnp.concatenate(parts, axis=-1)

    return pl.pallas_call(
        body, name="align_w_in", grid=(D_MODEL // tr,),
        in_specs=[pl.BlockSpec((N_CHIPS, tr, W_IN_SHARD), lambda i: (0, i, 0))],
        out_specs=pl.BlockSpec((tr, PW), lambda i: (i, 0)),
        out_shape=jax.ShapeDtypeStruct((D_MODEL, PW), shards.dtype),
        compiler_params=_params("arbitrary"),
    )(shards)


def split_w_in_grad(dw, tr=256):
    def body(d_ref, o16_ref):
        for k, pieces in enumerate(_SHARD_PIECES):
            o16_ref[k] = jnp.concatenate([d_ref[:, a:b] for a, b in pieces], axis=-1).astype(bf16)

    return pl.pallas_call(
        body, name="split_w_in_grad", grid=(D_MODEL // tr,),
        in_specs=[pl.BlockSpec((tr, PW), lambda i: (i, 0))],
        out_specs=pl.BlockSpec((N_CHIPS, tr, W_IN_SHARD), lambda i: (0, i, 0)),
        out_shape=jax.ShapeDtypeStruct((N_CHIPS, D_MODEL, W_IN_SHARD), bf16),
        compiler_params=_params("arbitrary"),
    )(dw)

def in_fwd(x, g, w, tm=512):
    def body(x_ref, g_ref, w_ref, o_ref):
        xv = x_ref[...]
        h = xv * _rms(xv) * g_ref[...]
        o_ref[...] = _bdot(h, w_ref[...])

    return pl.pallas_call(
        body, name="in_fwd", grid=(SEQ // tm,),
        in_specs=[pl.BlockSpec((tm, D_MODEL), lambda i: (i, 0)), _full((1, D_MODEL)), _resident((D_MODEL, PW))],
        out_specs=pl.BlockSpec((tm, PW), lambda i: (i, 0)),
        out_shape=jax.ShapeDtypeStruct((SEQ, PW), f32),
        compiler_params=_params("arbitrary"),
    )(x, g, w)


def _resident(shape):
    return pl.BlockSpec(shape, lambda *_: (0,) * len(shape), pipeline_mode=pl.Buffered(1))


def in_bwd(dq, dz, dxbc, dk, dv, ddt, x, g, w, dres, tm=512):
    def body(dq_ref, dz_ref, dx_ref, dk_ref, dv_ref, ddt_ref, x_ref, g_ref, w_ref, dres_ref, o_ref, dw_ref, dg_ref):
        i = pl.program_id(0)

        @pl.when(i == 0)
        def _():
            dw_ref[...] = jnp.zeros_like(dw_ref)
            dg_ref[...] = jnp.zeros_like(dg_ref)

        dproj = jnp.concatenate([dq_ref[...], dz_ref[...], dx_ref[...], dk_ref[...], dv_ref[...], ddt_ref[...]],
                                axis=-1).astype(bf16)
        xv = x_ref[...]
        r = _rms(xv)
        xhat = xv * r
        gv = g_ref[...]
        h = xhat * gv
        dw_ref[...] += _bdot_tn(h, dproj)
        dh = _bdot_nt(dproj, w_ref[...])
        dg_ref[...] += jnp.sum(dh * xhat, axis=0, keepdims=True)
        o_ref[...] = dres_ref[...] + _rms_bwd(dh, xhat, r, gv)

    tok = lambda w_: pl.BlockSpec((tm, w_), lambda i: (i, 0))
    return pl.pallas_call(
        body, name="in_bwd", grid=(SEQ // tm,),
        in_specs=[tok(D_ATTN), tok(D_SSM), tok(D_CONV), tok(D_KV), tok(D_KV), tok(LANE), tok(D_MODEL),
                  _full((1, D_MODEL)), _resident((D_MODEL, PW)), tok(D_MODEL)],
        out_specs=[tok(D_MODEL), _resident((D_MODEL, PW)), _full((1, D_MODEL))],
        out_shape=[jax.ShapeDtypeStruct((SEQ, D_MODEL), f32), jax.ShapeDtypeStruct((D_MODEL, PW), f32),
                   jax.ShapeDtypeStruct((1, D_MODEL), f32)],
        compiler_params=_params("arbitrary"),
    )(dq, dz, dxbc, dk, dv, ddt, x, g, w, dres)


def _attn_softmax_t(qk, bias_t, sink, first, key_row):
    s = qk * (HEAD_DIM ** -0.5) + bias_t
    s = jnp.where(jnp.logical_and(first, key_row < BLOCK), NEG, s)
    m = jnp.maximum(jnp.max(s, axis=0, keepdims=True), sink)
    p = jnp.exp(s - m)
    psink = jnp.exp(sink - m)
    inv = 1.0 / (jnp.sum(p, axis=0, keepdims=True) + psink)
    return p * inv, psink * inv


def _rms_t(x_t):
    return lax.rsqrt(jnp.mean(x_t * x_t, axis=0, keepdims=True) + EPS)


def attn_fwd_t(proj, q_gain_col, k_gain, sinks, bias_t):
    kcol, vcol = OFF_K // D_KV, OFF_V // D_KV

    def body(q_ref, kc_ref, kp_ref, vc_ref, vp_ref, qg_ref, kg_ref, sink_ref, bias_ref, o_ref, ot_scr):
        n = pl.program_id(0)
        first = n == 0
        key_row = lax.broadcasted_iota(jnp.int32, (2 * BLOCK, BLOCK), 0)
        k2 = jnp.concatenate([kp_ref[...], kc_ref[...]], axis=0)
        v_t = jnp.concatenate([vp_ref[...], vc_ref[...]], axis=0).T
        q_t = q_ref[...].T
        qg = jnp.broadcast_to(qg_ref[...], (HEAD_DIM, BLOCK))
        kg = kg_ref[...]
        for hk in range(N_KV_HEADS):
            sl = slice(hk * HEAD_DIM, (hk + 1) * HEAD_DIM)
            kk = k2[:, sl]
            kn = (kk * _rms(kk) * kg).astype(bf16)
            vt = v_t[sl, :].astype(bf16)
            heads = range(hk * Q_PER_KV, (hk + 1) * Q_PER_KV)
            qns = []
            for h in heads:
                qh = q_t[h * HEAD_DIM:(h + 1) * HEAD_DIM, :]
                qns.append(qh * _rms_t(qh) * qg)
            scores = [_bdot(kn, qn) for qn in qns]
            for h, s in zip(heads, scores):
                p, _ = _attn_softmax_t(s, bias_ref[h], sink_ref[h], first, key_row)
                ot_scr[h * HEAD_DIM:(h + 1) * HEAD_DIM, :] = _bdot(vt, p)
        o_ref[...] = ot_scr[...].T

    prev = lambda n: jnp.maximum(n - 1, 0)
    return pl.pallas_call(
        body, name="attn_fwd", grid=(N_BLOCKS,),
        in_specs=[pl.BlockSpec((BLOCK, D_ATTN), lambda n: (n, 0)),
                  pl.BlockSpec((BLOCK, D_KV), lambda n: (n, kcol)), pl.BlockSpec((BLOCK, D_KV), lambda n: (prev(n), kcol)),
                  pl.BlockSpec((BLOCK, D_KV), lambda n: (n, vcol)), pl.BlockSpec((BLOCK, D_KV), lambda n: (prev(n), vcol)),
                  _full((HEAD_DIM, 1)), _full((1, HEAD_DIM)), pl.BlockSpec(memory_space=pltpu.SMEM),
                  _full((N_Q_HEADS, 2 * BLOCK, BLOCK))],
        out_specs=pl.BlockSpec((BLOCK, D_ATTN), lambda n: (n, 0)),
        out_shape=jax.ShapeDtypeStruct((SEQ, D_ATTN), f32),
        scratch_shapes=[pltpu.VMEM((D_ATTN, BLOCK), f32)],
        compiler_params=_params("arbitrary"),
    )(proj, proj, proj, proj, proj, q_gain_col, k_gain, sinks, bias_t)


def attn_bwd_t(proj, d_out, q_gain_col, k_gain, sinks, bias_t):
    kcol, vcol = OFF_K // D_KV, OFF_V // D_KV

    def body(q_ref, kc_ref, kp_ref, vc_ref, vp_ref, do_ref, qg_ref, kg_ref, sink_ref, bias_ref,
             dq_ref, dk_ref, dv_ref, dband_ref, dsink_ref, dqg_ref, dkg_ref, dkn_scr, dv_scr, dqt_scr, dsink_acc, dqg_acc):
        i = pl.program_id(0)
        first = i == N_BLOCKS - 1

        @pl.when(i == 0)
        def _():
            for ref in (dband_ref, dkg_ref, dkn_scr, dv_scr, dsink_acc, dqg_acc):
                ref[...] = jnp.zeros_like(ref)

        key_row = lax.broadcasted_iota(jnp.int32, (2 * BLOCK, BLOCK), 0)
        k2 = jnp.concatenate([kp_ref[...], kc_ref[...]], axis=0)
        v2 = jnp.concatenate([vp_ref[...], vc_ref[...]], axis=0)
        q_t = q_ref[...].T
        do_t = do_ref[...].T
        qg = jnp.broadcast_to(qg_ref[...], (HEAD_DIM, BLOCK))
        kg = kg_ref[...]
        scale = HEAD_DIM ** -0.5
        for hk in range(N_KV_HEADS):
            sl = slice(hk * HEAD_DIM, (hk + 1) * HEAD_DIM)
            kk = k2[:, sl]
            rk = _rms(kk)
            khat = kk * rk
            kn = (khat * kg).astype(bf16)
            vb = v2[:, sl].astype(bf16)
            dkn = jnp.zeros((2 * BLOCK, HEAD_DIM), f32)
            dvv = jnp.zeros((2 * BLOCK, HEAD_DIM), f32)
            heads = range(hk * Q_PER_KV, (hk + 1) * Q_PER_KV)
            rqs, qhats, qns, d_os = [], [], [], []
            for h in heads:
                hs = slice(h * HEAD_DIM, (h + 1) * HEAD_DIM)
                qh = q_t[hs, :]
                rqs.append(_rms_t(qh))
                qhats.append(qh * rqs[-1])
                qns.append((qhats[-1] * qg).astype(bf16))
                d_os.append(do_t[hs, :].astype(bf16))
            scores = [_bdot(kn, qn) for qn in qns]
            dps = [_bdot(vb, d_o) for d_o in d_os]
            ps, dss = [], []
            for h, s, dp in zip(heads, scores, dps):
                p, psink = _attn_softmax_t(s, bias_ref[h], sink_ref[h], first, key_row)
                delta = jnp.sum(p * dp, axis=0, keepdims=True)
                ds = p * (dp - delta)
                dband_ref[h] += ds
                dsink_acc[h:h + 1, :] += -(psink * delta)
                ps.append(p.astype(bf16))
                dss.append(ds.astype(bf16))
            dqns = [_bdot_tn(kn, ds) * scale for ds in dss]
            for ds, qn, p, d_o in zip(dss, qns, ps, d_os):
                dkn = dkn + _bdot_nt(ds, qn) * scale
                dvv = dvv + _bdot_nt(p, d_o)
            for h, dqn, rq, qhat in zip(heads, dqns, rqs, qhats):
                dqg_acc[...] += dqn * qhat
                t = dqn * qg
                dqt_scr[h * HEAD_DIM:(h + 1) * HEAD_DIM, :] = rq * (t - qhat * jnp.mean(t * qhat, axis=0, keepdims=True))
            dkn_cur = dkn[BLOCK:] + dkn_scr[:, sl]
            dkn_scr[:, sl] = dkn[:BLOCK]
            khat_c, rk_c = khat[BLOCK:], rk[BLOCK:]
            dkg_ref[...] += jnp.sum(dkn_cur * khat_c, axis=0, keepdims=True)
            dk_ref[:, sl] = _rms_bwd(dkn_cur, khat_c, rk_c, kg)
            dv_ref[:, sl] = dvv[BLOCK:] + dv_scr[:, sl]
            dv_scr[:, sl] = dvv[:BLOCK]
        dq_ref[...] = dqt_scr[...].T

        @pl.when(i == N_BLOCKS - 1)
        def _():
            dsink_ref[...] = jnp.sum(dsink_acc[...], axis=1, keepdims=True)
            dqg_ref[...] = jnp.sum(dqg_acc[...], axis=1, keepdims=True)

    blk = lambda i: N_BLOCKS - 1 - i
    prev = lambda i: jnp.maximum(N_BLOCKS - 2 - i, 0)
    return pl.pallas_call(
        body, name="attn_bwd", grid=(N_BLOCKS,),
        in_specs=[pl.BlockSpec((BLOCK, D_ATTN), lambda i: (blk(i), 0)),
                  pl.BlockSpec((BLOCK, D_KV), lambda i: (blk(i), kcol)), pl.BlockSpec((BLOCK, D_KV), lambda i: (prev(i), kcol)),
                  pl.BlockSpec((BLOCK, D_KV), lambda i: (blk(i), vcol)), pl.BlockSpec((BLOCK, D_KV), lambda i: (prev(i), vcol)),
                  pl.BlockSpec((BLOCK, D_ATTN), lambda i: (blk(i), 0)),
                  _full((HEAD_DIM, 1)), _full((1, HEAD_DIM)), pl.BlockSpec(memory_space=pltpu.SMEM),
                  _full((N_Q_HEADS, 2 * BLOCK, BLOCK))],
        out_specs=[pl.BlockSpec((BLOCK, D_ATTN), lambda i: (blk(i), 0)), pl.BlockSpec((BLOCK, D_KV), lambda i: (blk(i), 0)),
                   pl.BlockSpec((BLOCK, D_KV), lambda i: (blk(i), 0)), _full((N_Q_HEADS, 2 * BLOCK, BLOCK)),
                   _full((N_Q_HEADS, 1)), _full((HEAD_DIM, 1)), _full((1, HEAD_DIM))],
        out_shape=[jax.ShapeDtypeStruct((SEQ, D_ATTN), f32), jax.ShapeDtypeStruct((SEQ, D_KV), f32),
                   jax.ShapeDtypeStruct((SEQ, D_KV), f32), jax.ShapeDtypeStruct((N_Q_HEADS, 2 * BLOCK, BLOCK), f32),
                   jax.ShapeDtypeStruct((N_Q_HEADS, 1), f32), jax.ShapeDtypeStruct((HEAD_DIM, 1), f32),
                   jax.ShapeDtypeStruct((1, HEAD_DIM), f32)],
        scratch_shapes=[pltpu.VMEM((BLOCK, D_KV), f32), pltpu.VMEM((BLOCK, D_KV), f32), pltpu.VMEM((D_ATTN, BLOCK), f32),
                        pltpu.VMEM((N_Q_HEADS, BLOCK), f32), pltpu.VMEM((HEAD_DIM, BLOCK), f32)],
        compiler_params=_params("arbitrary"),
    )(proj, proj, proj, proj, proj, d_out, q_gain_col, k_gain, sinks, bias_t)


SUBLANES = 8


def _shift_down(u, s, row8):
    if s == 0:
        return u
    r = pltpu.roll(u, s, 0)
    return jnp.concatenate([jnp.where(row8 >= s, r[:SUBLANES], 0.0), r[SUBLANES:]], axis=0)


def _shift_up(u, s, row8):
    if s == 0:
        return u
    r = pltpu.roll(u, SEQ - s, 0)
    return jnp.concatenate([r[:-SUBLANES], jnp.where(row8 < SUBLANES - s, r[-SUBLANES:], 0.0)], axis=0)


def conv_fwd(proj, conv_w, conv_b):
    xcol = OFF_X // LANE

    def body(u_ref, w_ref, b_ref, o_ref):
        u = u_ref[...]
        row = lax.broadcasted_iota(jnp.int32, (SUBLANES, LANE), 0)
        pre = b_ref[...] + jnp.zeros_like(u)
        for k in range(CONV_WIDTH):
            pre = pre + w_ref[k:k + 1, :] * _shift_down(u, CONV_WIDTH - 1 - k, row)
        o_ref[...] = pre * _sigmoid(pre)

    return pl.pallas_call(
        body, name="conv_fwd", grid=(D_CONV // LANE,),
        in_specs=[pl.BlockSpec((SEQ, LANE), lambda j: (0, xcol + j)), pl.BlockSpec((CONV_WIDTH, LANE), lambda j: (0, j)),
                  pl.BlockSpec((1, LANE), lambda j: (0, j))],
        out_specs=pl.BlockSpec((SEQ, LANE), lambda j: (0, j)),
        out_shape=jax.ShapeDtypeStruct((SEQ, D_CONV), f32),
        compiler_params=_params("arbitrary"),
    )(proj, conv_w, conv_b)


def conv_bwd(proj, d_act, conv_w, conv_b):
    xcol = OFF_X // LANE

    def body(u_ref, da_ref, w_ref, b_ref, du_ref, dw_ref, db_ref):
        u = u_ref[...]
        row = lax.broadcasted_iota(jnp.int32, (SUBLANES, LANE), 0)
        shifted = [_shift_down(u, CONV_WIDTH - 1 - k, row) for k in range(CONV_WIDTH)]
        pre = b_ref[...] + jnp.zeros_like(u)
        for k in range(CONV_WIDTH):
            pre = pre + w_ref[k:k + 1, :] * shifted[k]
        sg = _sigmoid(pre)
        dpre = da_ref[...] * (sg * (1.0 + pre * (1.0 - sg)))
        db_ref[...] = jnp.sum(dpre, axis=0, keepdims=True)
        du = jnp.zeros_like(u)
        for k in range(CONV_WIDTH):
            dw_ref[k:k + 1, :] = jnp.sum(dpre * shifted[k], axis=0, keepdims=True)
            du = du + w_ref[k:k + 1, :] * _shift_up(dpre, CONV_WIDTH - 1 - k, row)
        du_ref[...] = du

    return pl.pallas_call(
        body, name="conv_bwd", grid=(D_CONV // LANE,),
        in_specs=[pl.BlockSpec((SEQ, LANE), lambda j: (0, xcol + j)), pl.BlockSpec((SEQ, LANE), lambda j: (0, j)),
                  pl.BlockSpec((CONV_WIDTH, LANE), lambda j: (0, j)), pl.BlockSpec((1, LANE), lambda j: (0, j))],
        out_specs=[pl.BlockSpec((SEQ, LANE), lambda j: (0, j)), pl.BlockSpec((CONV_WIDTH, LANE), lambda j: (0, j)),
                   pl.BlockSpec((1, LANE), lambda j: (0, j))],
        out_shape=[jax.ShapeDtypeStruct((SEQ, D_CONV), f32), jax.ShapeDtypeStruct((CONV_WIDTH, D_CONV), f32),
                   jax.ShapeDtypeStruct((1, D_CONV), f32)],
        compiler_params=_params("arbitrary"),
    )(proj, d_act, conv_w, conv_b)


def _ssd_chunk_common(dt_raw, dtb, alog):
    row = lax.broadcasted_iota(jnp.int32, (CHUNK, CHUNK), 0)
    col = lax.broadcasted_iota(jnp.int32, (CHUNK, CHUNK), 1)
    tri = (row >= col).astype(f32)
    strict = (row > col).astype(f32)
    dtp = _softplus(dt_raw + dtb)
    a_row = -jnp.exp(alog)
    d_a = dtp * a_row
    cs = _hdot(tri, d_a)
    cs_last = cs[CHUNK - 1:CHUNK, :]
    return row, col, dtp, a_row, cs, cs.T, cs_last


def _seg_decay(cs, cs_t, hd, row, col):
    seg = cs[:, hd:hd + 1] - cs_t[hd:hd + 1, :]
    return jnp.where(row >= col, jnp.exp(seg), 0.0)


GROUP_W = HEADS_PER_GROUP * SSM_HEAD_DIM


def _group_indicator(g):
    j = lax.broadcasted_iota(jnp.int32, (GROUP_W, LANE), 0)
    lane = lax.broadcasted_iota(jnp.int32, (GROUP_W, LANE), 1)
    return (lane == g * HEADS_PER_GROUP + j // SSM_HEAD_DIM).astype(bf16)


def _bf16_pieces(a, n):
    pieces = []
    for _ in range(n):
        p = a.astype(bf16)
        pieces.append(p)
        a = a - p.astype(f32)
    return pieces


def _head_spread(a, ind):
    return sum(lax.dot_general(p, ind, (((1,), (1,)), ((), ())), preferred_element_type=f32) for p in _bf16_pieces(a, 3))


def _head_sums(a, ind):
    return sum(jnp.dot(p, ind, preferred_element_type=f32) for p in _bf16_pieces(a, 2))


def ssd_fwd_g(act, proj, dt_bias, a_log, d_skip, norm_g):
    zcol, dtcol = OFF_Z // D_SSM, OFF_DT // LANE

    def body(act_ref, z_ref, dt_ref, dtb_ref, alog_ref, dsk_ref, ng_ref, out_ref, ypre_ref, st_ref, state):
        c = pl.program_id(0)

        @pl.when(c == 0)
        def _():
            state[...] = jnp.zeros_like(state)

        row, col, dtp, a_row, cs, cs_t, cs_last = _ssd_chunk_common(dt_ref[...], dtb_ref[...], alog_ref[...])
        e_cs = jnp.exp(cs)
        dte = jnp.exp(cs_last - cs)
        rows8 = jnp.concatenate([jnp.exp(cs_last), dsk_ref[...], jnp.zeros((6, LANE), f32)], axis=0)
        z = z_ref[...]
        sz = z * _sigmoid(z)
        ng = ng_ref[...]
        for g in range(SSM_GROUPS):
            gs = slice(g * GROUP_W, (g + 1) * GROUP_W)
            ind = _group_indicator(g)
            xg = act_ref[:, gs]
            bg = act_ref[:, D_SSM + g * SSM_STATE:D_SSM + (g + 1) * SSM_STATE]
            cg = act_ref[:, D_SSM + D_BC + g * SSM_STATE:D_SSM + D_BC + (g + 1) * SSM_STATE]
            dt_e, e_e, dte_e = _head_spread(dtp, ind), _head_spread(e_cs, ind), _head_spread(dte, ind)
            rows_e = _head_spread(rows8, ind)
            ecl_e, dsk_e = rows_e[0:1], rows_e[1:2]
            xdt = xg * dt_e
            prev = state[g]
            st_ref[0, g] = prev
            cb = _bdot_nt(cg, bg)
            goff = _bdot(cg, prev)
            snew = _bdot_tn(bg, xdt * dte_e)
            heads = range(g * HEADS_PER_GROUP, (g + 1) * HEADS_PER_GROUP)
            ms = [cb * _seg_decay(cs, cs_t, hd, row, col) for hd in heads]
            yd = [_bdot(m, xdt[:, r * SSM_HEAD_DIM:(r + 1) * SSM_HEAD_DIM]) for r, m in enumerate(ms)]
            y = jnp.concatenate(yd, axis=1) + e_e * goff + xg * dsk_e
            state[g] = prev * ecl_e + snew
            ypre_ref[:, gs] = y
            part = y * sz[:, gs]
            out_ref[:, gs] = part * _rms(part) * ng[:, gs]

    return pl.pallas_call(
        body, name="ssd_fwd", grid=(N_CHUNKS,),
        in_specs=[pl.BlockSpec((CHUNK, D_CONV), lambda c: (c, 0)), pl.BlockSpec((CHUNK, D_SSM), lambda c: (c, zcol)),
                  pl.BlockSpec((CHUNK, LANE), lambda c: (c, dtcol)), _full((1, LANE)), _full((1, LANE)), _full((1, LANE)),
                  _full((1, D_SSM))],
        out_specs=[pl.BlockSpec((CHUNK, D_SSM), lambda c: (c, 0)), pl.BlockSpec((CHUNK, D_SSM), lambda c: (c, 0)),
                   pl.BlockSpec((1, SSM_GROUPS, SSM_STATE, GROUP_W), lambda c: (c, 0, 0, 0))],
        out_shape=[jax.ShapeDtypeStruct((SEQ, D_SSM), f32), jax.ShapeDtypeStruct((SEQ, D_SSM), f32),
                   jax.ShapeDtypeStruct((N_CHUNKS, SSM_GROUPS, SSM_STATE, GROUP_W), f32)],
        scratch_shapes=[pltpu.VMEM((SSM_GROUPS, SSM_STATE, GROUP_W), f32)],
        compiler_params=_params("arbitrary"),
    )(act, proj, proj, dt_bias, a_log, d_skip, norm_g)


def ssd_bwd_g(act, proj, ypre, states, d_out, dt_bias, a_log, d_skip, norm_g):
    zcol, dtcol = OFF_Z // D_SSM, OFF_DT // LANE

    def body(act_ref, z_ref, dt_ref, ypre_ref, st_ref, do_ref, dtb_ref, alog_ref, dsk_ref, ng_ref,
             dact_ref, ddt_ref, dz_ref, dng_ref, dpar_ref, dstate):
        i = pl.program_id(0)

        @pl.when(i == 0)
        def _():
            for ref in (dng_ref, dpar_ref, dstate):
                ref[...] = jnp.zeros_like(ref)

        row, col, dtp, a_row, cs, cs_t, cs_last = _ssd_chunk_common(dt_ref[...], dtb_ref[...], alog_ref[...])
        upper = (row <= col).astype(f32)
        lane = lax.broadcasted_iota(jnp.int32, (CHUNK, LANE), 1)
        rowl = lax.broadcasted_iota(jnp.int32, (CHUNK, LANE), 0)
        e_cs = jnp.exp(cs)
        dte = jnp.exp(cs_last - cs)
        ecl = jnp.exp(cs_last)
        rows8 = jnp.concatenate([ecl, dsk_ref[...], jnp.zeros((6, LANE), f32)], axis=0)
        z = z_ref[...]
        sgz = _sigmoid(z)
        sz = z * sgz
        ng = ng_ref[...]
        ddt_mat = jnp.zeros((CHUNK, LANE), f32)
        dcs_mat = jnp.zeros((CHUNK, LANE), f32)
        dcs_t = jnp.zeros((LANE, CHUNK), f32)
        dcsl_row = jnp.zeros((1, LANE), f32)
        dd_row = jnp.zeros((1, LANE), f32)
        for g in range(SSM_GROUPS):
            gs = slice(g * GROUP_W, (g + 1) * GROUP_W)
            bsl = slice(D_SSM + g * SSM_STATE, D_SSM + (g + 1) * SSM_STATE)
            csl = slice(D_SSM + D_BC + g * SSM_STATE, D_SSM + D_BC + (g + 1) * SSM_STATE)
            ind = _group_indicator(g)
            y = ypre_ref[:, gs]
            part = y * sz[:, gs]
            r = _rms(part)
            yhat = part * r
            d_o = do_ref[:, gs]
            dng_ref[:, gs] += jnp.sum(d_o * yhat, axis=0, keepdims=True)
            dyz = _rms_bwd(d_o, yhat, r, ng[:, gs])
            dy = dyz * sz[:, gs]
            dz_ref[:, gs] = dyz * y * (sgz[:, gs] * (1.0 + z[:, gs] * (1.0 - sgz[:, gs])))

            xg = act_ref[:, gs]
            bg = act_ref[:, bsl]
            cg = act_ref[:, csl]
            dt_e, e_e, dte_e = _head_spread(dtp, ind), _head_spread(e_cs, ind), _head_spread(dte, ind)
            rows_e = _head_spread(rows8, ind)
            ecl_e, dsk_e = rows_e[0:1], rows_e[1:2]
            xdt = xg * dt_e
            prev = st_ref[0, g]
            dh = dstate[g]
            heads = range(g * HEADS_PER_GROUP, (g + 1) * HEADS_PER_GROUP)
            hsl = [slice(r_ * SSM_HEAD_DIM, (r_ + 1) * SSM_HEAD_DIM) for r_ in range(HEADS_PER_GROUP)]
            cb = _bdot_nt(cg, bg)
            lms = [_seg_decay(cs, cs_t, hd, row, col) for hd in heads]
            ms = [cb * lm for lm in lms]
            gmat = _bdot(cg, prev)
            dgm = dy * e_e
            dcg = _bdot_nt(dgm, prev)
            dprev = _bdot_tn(cg, dgm)
            dbg = _bdot_nt(xdt * dte_e, dh)
            dw = _bdot(bg, dh)
            dms = [_bdot_nt(dy[:, s_], xdt[:, s_]) for s_ in hsl]
            dxdts = [_bdot_tn(m, dy[:, s_]) for m, s_ in zip(ms, hsl)]
            dxdt = jnp.concatenate(dxdts, axis=1) + dw * dte_e
            dact_ref[:, gs] = dy * dsk_e + dxdt * dt_e
            dstate[g] = dprev + dh * ecl_e
            dcb = jnp.zeros((CHUNK, CHUNK), f32)
            for hd, dm, lm, m in zip(heads, dms, lms, ms):
                dcb = dcb + dm * lm
                dseg = dm * m
                dcs_mat = dcs_mat + jnp.where(lane == hd, jnp.sum(dseg, axis=1, keepdims=True), 0.0)
                dcs_t = jnp.where(row == hd, jnp.sum(dseg, axis=0, keepdims=True), dcs_t)
            dact_ref[:, bsl] = dbg + _bdot_tn(dcb, cg)
            dact_ref[:, csl] = dcg + _bdot(dcb, bg)
            ddte = _head_sums(dw * xdt, ind) * dte
            dcs_mat = dcs_mat + _head_sums(dy * gmat, ind) * e_cs - ddte
            ddt_mat = ddt_mat + _head_sums(dxdt * xg, ind)
            dcsl_row = (dcsl_row + jnp.sum(ddte, axis=0, keepdims=True)
                        + jnp.sum(_head_sums(dh * prev, ind), axis=0, keepdims=True) * ecl)
            dd_row = dd_row + jnp.sum(_head_sums(dy * xg, ind), axis=0, keepdims=True)
        dcs_mat = dcs_mat - dcs_t.T + jnp.where(rowl == CHUNK - 1, dcsl_row, 0.0)
        dda = _hdot(upper, dcs_mat)
        ddt_mat = ddt_mat + dda * a_row
        da_row = jnp.sum(dda * dtp, axis=0, keepdims=True)
        ddt_raw = ddt_mat * _sigmoid(dt_ref[...] + dtb_ref[...])
        ddt_ref[...] = ddt_raw
        dpar_ref[0:1, :] += jnp.sum(ddt_raw, axis=0, keepdims=True)
        dpar_ref[1:2, :] += da_row * a_row
        dpar_ref[2:3, :] += dd_row

    blk = lambda i: N_CHUNKS - 1 - i
    return pl.pallas_call(
        body, name="ssd_bwd", grid=(N_CHUNKS,),
        in_specs=[pl.BlockSpec((CHUNK, D_CONV), lambda i: (blk(i), 0)), pl.BlockSpec((CHUNK, D_SSM), lambda i: (blk(i), zcol)),
                  pl.BlockSpec((CHUNK, LANE), lambda i: (blk(i), dtcol)), pl.BlockSpec((CHUNK, D_SSM), lambda i: (blk(i), 0)),
                  pl.BlockSpec((1, SSM_GROUPS, SSM_STATE, GROUP_W), lambda i: (blk(i), 0, 0, 0)),
                  pl.BlockSpec((CHUNK, D_SSM), lambda i: (blk(i), 0)),
                  _full((1, LANE)), _full((1, LANE)), _full((1, LANE)), _full((1, D_SSM))],
        out_specs=[pl.BlockSpec((CHUNK, D_CONV), lambda i: (blk(i), 0)), pl.BlockSpec((CHUNK, LANE), lambda i: (blk(i), 0)),
                   pl.BlockSpec((CHUNK, D_SSM), lambda i: (blk(i), 0)), _full((1, D_SSM)), _full((8, LANE))],
        out_shape=[jax.ShapeDtypeStruct((SEQ, D_CONV), f32), jax.ShapeDtypeStruct((SEQ, LANE), f32),
                   jax.ShapeDtypeStruct((SEQ, D_SSM), f32), jax.ShapeDtypeStruct((1, D_SSM), f32),
                   jax.ShapeDtypeStruct((8, LANE), f32)],
        scratch_shapes=[pltpu.VMEM((SSM_GROUPS, SSM_STATE, GROUP_W), f32)],
        compiler_params=_params("arbitrary"),
    )(act, proj, proj, ypre, states, d_out, dt_bias, a_log, d_skip, norm_g)


def out_fwd(x, attn, ssm, w_out, tm=512):
    def body(x_ref, a_ref, s_ref, w_ref, o_ref):
        o_ref[...] = x_ref[...] + _bdot(a_ref[...], w_ref[:D_ATTN, :]) + _bdot(s_ref[...], w_ref[D_ATTN:, :])

    tok = lambda w_: pl.BlockSpec((tm, w_), lambda i: (i, 0))
    return pl.pallas_call(
        body, name="out_fwd", grid=(SEQ // tm,),
        in_specs=[tok(D_MODEL), tok(D_ATTN), tok(D_SSM), _full((D_MODEL, D_MODEL))],
        out_specs=tok(D_MODEL), out_shape=jax.ShapeDtypeStruct((SEQ, D_MODEL), f32),
        compiler_params=_params("arbitrary"),
    )(x, attn, ssm, w_out)


def out_bwd(dx1, attn, ssm, w_out, tm=512):
    nt = SEQ // tm

    def body(d_ref, a_ref, s_ref, w_ref, da_ref, ds_ref, dw16_ref, dw_ref):
        i = pl.program_id(0)

        @pl.when(i == 0)
        def _():
            dw_ref[...] = jnp.zeros_like(dw_ref)

        d = d_ref[...].astype(bf16)
        dcat = _bdot_nt(d, w_ref[...])
        da_ref[...] = dcat[:, :D_ATTN]
        ds_ref[...] = dcat[:, D_ATTN:]
        dw_ref[:D_ATTN, :] += _bdot_tn(a_ref[...], d)
        dw_ref[D_ATTN:, :] += _bdot_tn(s_ref[...], d)

        @pl.when(i == nt - 1)
        def _():
            dw16_ref[...] = dw_ref[...].astype(bf16)

    tok = lambda w_: pl.BlockSpec((tm, w_), lambda i: (i, 0))
    return pl.pallas_call(
        body, name="out_bwd", grid=(nt,),
        in_specs=[tok(D_MODEL), tok(D_ATTN), tok(D_SSM), _resident((D_MODEL, D_MODEL))],
        out_specs=[tok(D_ATTN), tok(D_SSM), _resident((D_MODEL, D_MODEL))],
        out_shape=[jax.ShapeDtypeStruct((SEQ, D_ATTN), f32), jax.ShapeDtypeStruct((SEQ, D_SSM), f32),
                   jax.ShapeDtypeStruct((D_MODEL, D_MODEL), bf16)],
        scratch_shapes=[pltpu.VMEM((D_MODEL, D_MODEL), f32)],
        compiler_params=_params("arbitrary"),
    )(dx1, attn, ssm, w_out)


MLP_SUB = 256


def mlp_fwd(x1, g, w_up, w_down, tm=1024):
    def body(x_ref, g_ref, wu_ref, wd_ref, o_ref, u_ref, h_scr):
        j = pl.program_id(1)

        @pl.when(j == 0)
        def _():
            xv = x_ref[...]
            h_scr[...] = (xv * _rms(xv) * g_ref[...]).astype(bf16)
            o_ref[...] = xv

        for r in range(tm // MLP_SUB):
            rows = slice(r * MLP_SUB, (r + 1) * MLP_SUB)
            u = jnp.dot(h_scr[rows, :], wu_ref[...], preferred_element_type=f32)
            u_ref[rows, :] = u
            a = jnp.square(jnp.maximum(u, 0.0))
            o_ref[rows, :] += _bdot(a, wd_ref[...])

    return pl.pallas_call(
        body, name="mlp_fwd", grid=(SEQ // tm, N_CHIPS),
        in_specs=[pl.BlockSpec((tm, D_MODEL), lambda i, j: (i, 0)), _full((1, D_MODEL)),
                  pl.BlockSpec((None, D_MODEL, FF_TILE), lambda i, j: (j, 0, 0)),
                  pl.BlockSpec((None, FF_TILE, D_MODEL), lambda i, j: (j, 0, 0))],
        out_specs=[pl.BlockSpec((tm, D_MODEL), lambda i, j: (i, 0)), pl.BlockSpec((tm, FF_TILE), lambda i, j: (i, j))],
        out_shape=[jax.ShapeDtypeStruct((SEQ, D_MODEL), f32), jax.ShapeDtypeStruct((SEQ, D_FF), f32)],
        scratch_shapes=[pltpu.VMEM((tm, D_MODEL), bf16)],
        compiler_params=_params("arbitrary", "arbitrary"),
    )(x1, g, w_up, w_down)


def mlp_bwd_data(dx2, u, x1, g, w_up, w_down, tm=1024):
    def body(d_ref, u_ref, x_ref, g_ref, wu_ref, wd_ref, dx_ref, du_ref, dg_ref, dh_scr):
        i, j = pl.program_id(0), pl.program_id(1)

        @pl.when(jnp.logical_and(i == 0, j == 0))
        def _():
            dg_ref[...] = jnp.zeros_like(dg_ref)

        @pl.when(j == 0)
        def _():
            dh_scr[...] = jnp.zeros_like(dh_scr)

        for r in range(tm // MLP_SUB):
            rows = slice(r * MLP_SUB, (r + 1) * MLP_SUB)
            da = _bdot_nt(d_ref[rows, :], wd_ref[...])
            du = (da * (2.0 * jnp.maximum(u_ref[rows, :], 0.0))).astype(bf16)
            du_ref[rows, :] = du
            dh_scr[rows, :] += _bdot_nt(du, wu_ref[...])

        @pl.when(j == N_CHIPS - 1)
        def _():
            xv = x_ref[...]
            r = _rms(xv)
            xhat = xv * r
            dh = dh_scr[...]
            dg_ref[...] += jnp.sum(dh * xhat, axis=0, keepdims=True)
            dx_ref[...] = d_ref[...] + _rms_bwd(dh, xhat, r, g_ref[...])

    return pl.pallas_call(
        body, name="mlp_bwd_data", grid=(SEQ // tm, N_CHIPS),
        in_specs=[pl.BlockSpec((tm, D_MODEL), lambda i, j: (i, 0)), pl.BlockSpec((tm, FF_TILE), lambda i, j: (i, j)),
                  pl.BlockSpec((tm, D_MODEL), lambda i, j: (i, 0)), _full((1, D_MODEL)),
                  pl.BlockSpec((None, D_MODEL, FF_TILE), lambda i, j: (j, 0, 0)),
                  pl.BlockSpec((None, FF_TILE, D_MODEL), lambda i, j: (j, 0, 0))],
        out_specs=[pl.BlockSpec((tm, D_MODEL), lambda i, j: (i, 0)), pl.BlockSpec((tm, FF_TILE), lambda i, j: (i, j)),
                   _full((1, D_MODEL))],
        out_shape=[jax.ShapeDtypeStruct((SEQ, D_MODEL), f32), jax.ShapeDtypeStruct((SEQ, D_FF), bf16),
                   jax.ShapeDtypeStruct((1, D_MODEL), f32)],
        scratch_shapes=[pltpu.VMEM((tm, D_MODEL), f32)],
        compiler_params=_params("arbitrary", "arbitrary"),
    )(dx2, u, x1, g, w_up, w_down)


def mlp_bwd_weights(dx2, u, du, x1, g, tm=512):
    nt = SEQ // tm

    def body(d_ref, u_ref, du_ref, x_ref, g_ref, dwu16_ref, dwd16_ref, h_scr, d_scr, dwu_ref, dwd_ref):
        j, i = pl.program_id(0), pl.program_id(1)

        @pl.when(j == 0)
        def _():
            xv = x_ref[...]
            h_scr[i] = (xv * _rms(xv) * g_ref[...]).T.astype(bf16)
            d_scr[i] = d_ref[...].astype(bf16)

        @pl.when(i == 0)
        def _():
            dwu_ref[...] = jnp.zeros_like(dwu_ref)
            dwd_ref[...] = jnp.zeros_like(dwd_ref)

        dwu_ref[...] += jnp.dot(h_scr[i], du_ref[...], preferred_element_type=f32)
        a = jnp.square(jnp.maximum(u_ref[...], 0.0))
        dwd_ref[...] += _bdot_tn(a, d_scr[i])

        @pl.when(i == nt - 1)
        def _():
            dwu16_ref[...] = dwu_ref[...].astype(bf16)
            dwd16_ref[...] = dwd_ref[...].astype(bf16)

    up = pl.BlockSpec((None, D_MODEL, FF_TILE), lambda j, i: (j, 0, 0))
    down = pl.BlockSpec((None, FF_TILE, D_MODEL), lambda j, i: (j, 0, 0))
    first_pass = pl.BlockSpec((tm, D_MODEL), lambda j, i: (jnp.where(j == 0, i, nt - 1), 0))
    return pl.pallas_call(
        body, name="mlp_bwd_weights", grid=(N_CHIPS, nt),
        in_specs=[first_pass, pl.BlockSpec((tm, FF_TILE), lambda j, i: (i, j)),
                  pl.BlockSpec((tm, FF_TILE), lambda j, i: (i, j)), first_pass, _full((1, D_MODEL))],
        out_specs=[up, down],
        out_shape=[jax.ShapeDtypeStruct((N_CHIPS, D_MODEL, FF_TILE), bf16), jax.ShapeDtypeStruct((N_CHIPS, FF_TILE, D_MODEL), bf16)],
        scratch_shapes=[pltpu.VMEM((nt, D_MODEL, tm), bf16), pltpu.VMEM((nt, tm, D_MODEL), bf16),
                        pltpu.VMEM((D_MODEL, FF_TILE), f32), pltpu.VMEM((FF_TILE, D_MODEL), f32)],
        compiler_params=_params("arbitrary", "arbitrary"),
    )(dx2, u, du, x1, g)


def loss_head(y, target, tm=512):
    def body(y_ref, t_ref, dy_ref, l_ref):
        @pl.when(pl.program_id(0) == 0)
        def _():
            l_ref[...] = jnp.zeros_like(l_ref)

        d = y_ref[...] - t_ref[...]
        dy_ref[...] = d * (1.0 / D_MODEL)
        part = jnp.sum(jnp.mean(d * d, axis=-1, keepdims=True), axis=0, keepdims=True)
        l_ref[...] += 0.5 * part

    tok = pl.BlockSpec((tm, D_MODEL), lambda i: (i, 0))
    return pl.pallas_call(
        body, name="loss_head", grid=(SEQ // tm,), in_specs=[tok, tok], out_specs=[tok, _full((1, 1))],
        out_shape=[jax.ShapeDtypeStruct((SEQ, D_MODEL), f32), jax.ShapeDtypeStruct((1, 1), f32)],
        compiler_params=_params("arbitrary"),
    )(y, target)


def _pad_lane(v):
    return jnp.pad(v, (0, LANE - v.shape[0]))[None, :]


def local_step(x, target, w, prov):
    bucket = jnp.asarray(_bucket_table().T)
    bias = bias_build(w["rel_bias"], bucket)
    saved = []
    for l in range(DEPTH):
        g_mix = w["mix_norm_g"][l][None, :] + prov.stage(("begin", l), x)
        w_in = prov.w_in(l, x)
        proj = in_fwd(x, g_mix, w_in)
        conv_b = w["conv_b"][l][None, :]
        act = conv_fwd(proj, w["conv_w"][l], conv_b)
        dtb = _pad_lane(w["dt_bias"][l]) + prov.stage(("mid", l), act)
        alog, dsk = _pad_lane(w["a_log"][l]), _pad_lane(w["d_skip"][l])
        ng = w["ssm_norm_g"][l][None, :]
        ssm, ypre, states = ssd_fwd_g(act, proj, dtb, alog, dsk, ng)
        qg, kg = w["q_gain"][l][:, None] + 0.0 * ssm[:1, :1], w["k_gain"][l][None, :]
        attn = attn_fwd_t(proj, qg, kg, w["sinks"][l], bias)
        tok = prov.stage(("pre_out", l), attn)
        w_out = prov.w_out(l, attn) + jnp.asarray(tok, bf16)
        x1 = out_fwd(x, attn, ssm, w_out)
        g_mlp = w["mlp_norm_g"][l][None, :] + prov.stage(("pre_mlp", l), x1)
        w_up, w_down = prov.mlp(l, x1)
        x2, u = mlp_fwd(x1, g_mlp, w_up, w_down)
        saved.append(dict(x=x, proj=proj, attn=attn, act=act, ssm=ssm, ypre=ypre, states=states, x1=x1, u=u,
                          g_mix=g_mix, qg=qg, kg=kg, conv_b=conv_b, dtb=dtb, alog=alog, dsk=dsk, ng=ng, g_mlp=g_mlp,
                          w_in=w_in, w_out=w_out, w_up=w_up, w_down=w_down))
        x = x2
    dx, loss = loss_head(x, target)
    grads = [None] * DEPTH
    dbands = [None] * DEPTH
    tok = 0.0
    for l in reversed(range(DEPTH)):
        s = saved[l]
        g_mlp = s["g_mlp"] + tok
        dx1, du, dg_mlp = mlp_bwd_data(dx, s["u"], s["x1"], g_mlp, s["w_up"], s["w_down"])
        dw_up, dw_down = mlp_bwd_weights(dx, s["u"], du, s["x1"], g_mlp)
        tok = prov.grads(("mlp", l), dict(w_up=dw_up, w_down=dw_down), dx1)
        dattn, dssm, dw_out = out_bwd(dx1, s["attn"], s["ssm"], s["w_out"])
        dact, ddt, dz, dng, dpar = ssd_bwd_g(s["act"], s["proj"], s["ypre"], s["states"], dssm, s["dtb"] + tok, s["alog"],
                                           s["dsk"], s["ng"])
        conv_b = s["conv_b"] + prov.stage(("bwd_mid", l), dact)
        dxbc, dconv_w, dconv_b = conv_bwd(s["proj"], dact, w["conv_w"][l], conv_b)
        dq, dk, dv, dband, dsink, dqg, dkg = attn_bwd_t(s["proj"], dattn, s["qg"], s["kg"], w["sinks"][l], bias)
        dbands[l] = dband
        g_mix = s["g_mix"]
        if l == 0:
            d_rel = bias_bwd(dbands[0], dbands[1], bucket)
            g_mix = g_mix + 0.0 * d_rel[:1, :1]
        dx, dw_in, dg_mix = in_bwd(dq, dz, dxbc, dk, dv, ddt, s["x"], g_mix, s["w_in"], dx1)
        tok = prov.grads(("mix", l), dict(w_in=split_w_in_grad(dw_in), w_out=dw_out), dx)
        grads[l] = dict(mix_norm_g=dg_mix[0], q_gain=dqg[:, 0], k_gain=dkg[0], sinks=dsink[:, 0],
                        conv_w=dconv_w, conv_b=dconv_b[0], dt_bias=dpar[0, :SSM_HEADS], a_log=dpar[1, :SSM_HEADS],
                        d_skip=dpar[2, :SSM_HEADS], ssm_norm_g=dng[0], mlp_norm_g=dg_mlp[0])
    out = {k: jnp.stack([grads[l][k] for l in range(DEPTH)]) for k in grads[0]}
    out["rel_bias"] = d_rel[:, :N_Q_HEADS]
    return loss, dx, out, tok


MESH = pl.DeviceIdType.MESH
HBM = pl.BlockSpec(memory_space=pltpu.HBM)
N_DEVICES = 8


def _coords():
    return lax.axis_index("x"), lax.axis_index("y"), lax.axis_index("c")


def _peer_chips(x, y):
    return [(1 - x, y), (x, 1 - y), (1 - x, 1 - y)]


def _remote(src, dst, send_sem, recv_sem, device):
    return pltpu.make_async_remote_copy(src_ref=src, dst_ref=dst, send_sem=send_sem, recv_sem=recv_sem,
                                        device_id=device, device_id_type=MESH)


SEM = pl.BlockSpec(memory_space=pltpu.SEMAPHORE)
ANY = pl.BlockSpec(memory_space=pl.ANY)
DATAFLOW = pltpu.SideEffectType.DATAFLOW_SIDE_EFFECTING


def _gather_copies(kind, src_refs, land_refs, ssem, rsem):
    x, y, c = _coords()
    k_me = 2 * x + y
    n = len(land_refs)
    cps = []
    for p, land in enumerate(land_refs):
        hr = land.shape[1] // 2
        rows = pl.ds(c * hr, hr)
        for j, chip in enumerate(_peer_chips(x, y)):
            i = 3 * p + j
            if kind == "ici":
                cps.append(_remote(src_refs[p].at[rows, :], land.at[k_me, rows, :], ssem.at[i], rsem.at[i], (*chip, c)))
            else:
                got = land.at[2 * chip[0] + chip[1], rows, :]
                cps.append(_remote(got, got, ssem.at[i], rsem.at[i], (x, y, 1 - c)))
        if kind == "relay":
            cps.append(_remote(src_refs[p], land.at[k_me], ssem.at[3 * n + p], rsem.at[3 * n + p], (x, y, 1 - c)))
    return cps


def gather_now(srcs, conv):
    n = len(srcs)

    def body(*refs):
        src_refs, conv_ref = refs[:n], refs[n]
        lands, gconv = refs[n + 1:2 * n + 1], refs[2 * n + 1]
        ssem, rsem, fsem, frsem, csem, crsem = refs[2 * n + 2:]
        x, y, c = _coords()
        k_me = 2 * x + y
        targets = [(*chip, c) for chip in _peer_chips(x, y)] + [(x, y, 1 - c)]
        ici = _gather_copies("ici", src_refs, lands, ssem, rsem)
        relay = _gather_copies("relay", src_refs, lands, fsem, frsem)
        passed = [cp for i, cp in enumerate(relay) if i % 4 != 3]
        own = relay[3::4]
        conv_cps = [_remote(conv_ref, gconv.at[k_me], csem.at[j], crsem.at[j], t) for j, t in enumerate(targets)]
        for cp in ici + conv_cps + own:
            cp.start()
        for cp, fw in zip(ici, passed):
            cp.wait_recv()
            fw.start()
        for cp in conv_cps + relay:
            cp.wait_recv()
        for cp in ici + relay + conv_cps:
            cp.wait_send()

    out_shape = [jax.ShapeDtypeStruct((N_CHIPS,) + s.shape, s.dtype) for s in srcs]
    out_shape.append(jax.ShapeDtypeStruct((N_CHIPS,) + conv.shape, conv.dtype))
    sems = lambda k: pltpu.SemaphoreType.DMA((k,))
    return pl.pallas_call(
        body, name="gather_now", out_shape=out_shape, in_specs=[HBM] * (n + 1), out_specs=[HBM] * (n + 1),
        scratch_shapes=[sems(3 * n), sems(3 * n), sems(4 * n), sems(4 * n), sems(N_CHIPS), sems(N_CHIPS)],
    )(*srcs, conv)


def _gather_maker(kind, n_src):
    def make(refs, ssem, rsem):
        cps = _gather_copies(kind, refs[:n_src], refs[n_src:], ssem, rsem)
        return cps, cps
    return make


def _scatter_maker(n):
    def make(refs, ssem, rsem):
        x, y, c = _coords()
        k_me = 2 * x + y
        sends, arrivals = [], []
        for p in range(n):
            src, land = refs[p], refs[n + p]
            sends.append(_remote(src.at[k_me, 1 - c], land.at[0], ssem.at[7 * p], rsem.at[7 * p], (x, y, 1 - c)))
            for j, chip in enumerate(_peer_chips(x, y)):
                for cc in range(2):
                    sends.append(_remote(src.at[2 * chip[0] + chip[1], cc], land.at[1 + 2 * j + c],
                                         ssem.at[7 * p + 1 + 2 * j + cc], rsem.at[7 * p + 1 + 2 * j + c], (*chip, cc)))
            for s in range(7):
                arrivals.append(_remote(land.at[s], land.at[s], ssem.at[7 * p + s], rsem.at[7 * p + s], (x, y, 1 - c)))
        return sends, arrivals
    return make


def _share_maker(n):
    def make(refs, ssem, rsem):
        x, y, c = _coords()
        sends = [_remote(refs[p].at[c], refs[p].at[c], ssem.at[p], rsem.at[p], (x, y, 1 - c)) for p in range(n)]
        arrivals = [_remote(refs[p].at[1 - c], refs[p].at[1 - c], ssem.at[p], rsem.at[p], (x, y, 1 - c)) for p in range(n)]
        return sends, arrivals
    return make


def split_start(name, make, n_sems, operands, after):
    n = len(operands)

    def body(*refs):
        ssem, rsem, token = refs[n + 1], refs[n + 2], refs[-1]
        for cp in make(refs[:n], ssem, rsem)[0]:
            cp.start()
        token[...] = jnp.zeros_like(token)

    ops = [pltpu.with_memory_space_constraint(a, pltpu.HBM) for a in operands]
    outs = pl.pallas_call(
        body, name=name,
        out_shape=(pltpu.SemaphoreType.DMA((n_sems,)), pltpu.SemaphoreType.DMA((n_sems,)),
                   *[pltpu.HBM(a.shape, a.dtype) for a in ops], jax.ShapeDtypeStruct((8, LANE), f32)),
        in_specs=[HBM] * n + [ANY], out_specs=(SEM, SEM, *[HBM] * n, pl.BlockSpec(memory_space=pltpu.VMEM)),
        input_output_aliases={i: 2 + i for i in range(n)},
        compiler_params=pltpu.CompilerParams(has_side_effects=DATAFLOW),
    )(*ops, after)
    return dict(name=name, make=make, ssem=outs[0], rsem=outs[1], operands=outs[2:2 + n], token=outs[-1][0, 0],
                tokens=outs[-1])


def split_wait(handle, after):
    n = len(handle["operands"])

    def body(*refs):
        sends, arrivals = handle["make"](refs[:n], refs[n], refs[n + 1])
        for cp in sends:
            cp.wait_send()
        for cp in arrivals:
            cp.wait_recv()

    outs = pl.pallas_call(
        body, name=handle["name"].replace("start", "wait"),
        out_shape=tuple(pltpu.HBM(a.shape, a.dtype) for a in handle["operands"]),
        in_specs=[HBM] * n + [SEM, SEM, ANY], out_specs=tuple([HBM] * n),
        input_output_aliases={i: i for i in range(n)},
        compiler_params=pltpu.CompilerParams(has_side_effects=DATAFLOW),
    )(*handle["operands"], handle["ssem"], handle["rsem"], after)
    return list(outs)


def piece_sum(g, recv, kc_arr):
    _, _, rb, cc = g.shape
    tr = min(256, rb)

    def body(kc_ref, g_ref, r_ref, o_ref):
        acc = g_ref[...].astype(f32)
        for s in range(7):
            acc = acc + r_ref[s].astype(f32)
        o_ref[...] = acc

    return pl.pallas_call(
        body, name="piece_sum",
        grid_spec=pltpu.PrefetchScalarGridSpec(
            num_scalar_prefetch=1, grid=(rb // tr,),
            in_specs=[pl.BlockSpec((None, None, tr, cc), lambda r, kc: (kc[0], kc[1], r, 0)),
                      pl.BlockSpec((7, tr, cc), lambda r, kc: (0, r, 0))],
            out_specs=pl.BlockSpec((None, tr, cc), lambda r, kc: (kc[1], r, 0))),
        out_shape=jax.ShapeDtypeStruct((2, rb, cc), f32),
        compiler_params=_params("arbitrary"),
    )(kc_arr, g, recv)


def small_all_reduce(vec):
    def body(v_ref, o_ref, gat, ssem, rsem):
        x, y, c = _coords()
        me = 4 * x + 2 * y + c
        gat[me] = v_ref[...]
        sends = []
        for t in range(1, N_DEVICES):
            peer = (x ^ (t >> 2), y ^ ((t >> 1) & 1), c ^ (t & 1))
            cp = _remote(v_ref, gat.at[me], ssem.at[t - 1], rsem.at[t - 1], peer)
            cp.start()
            sends.append(cp)
        for t in range(1, N_DEVICES):
            peer = (x ^ (t >> 2), y ^ ((t >> 1) & 1), c ^ (t & 1))
            slot = gat.at[4 * peer[0] + 2 * peer[1] + peer[2]]
            _remote(slot, slot, ssem.at[t - 1], rsem.at[t - 1], peer).wait_recv()
        for cp in sends:
            cp.wait_send()
        acc = gat[0]
        for d in range(1, N_DEVICES):
            acc = acc + gat[d]
        o_ref[...] = acc

    return pl.pallas_call(
        body, name="small_all_reduce", out_shape=jax.ShapeDtypeStruct(vec.shape, vec.dtype),
        in_specs=[pl.BlockSpec(memory_space=pltpu.VMEM)], out_specs=pl.BlockSpec(memory_space=pltpu.VMEM),
        scratch_shapes=[pltpu.VMEM((N_DEVICES,) + vec.shape, vec.dtype), pltpu.SemaphoreType.DMA((N_DEVICES - 1,)),
                        pltpu.SemaphoreType.DMA((N_DEVICES - 1,))],
    )(vec)


def _adamw_math(w, g, m, v):
    m_new = ADAM_B1 * m + (1.0 - ADAM_B1) * g
    v_new = ADAM_B2 * v + (1.0 - ADAM_B2) * jnp.square(g)
    m_hat = m_new / (1.0 - ADAM_B1 ** ADAM_STEP)
    v_hat = v_new / (1.0 - ADAM_B2 ** ADAM_STEP)
    delta = -ADAM_LR * (m_hat / (jnp.sqrt(v_hat) + ADAM_EPS) + ADAM_WD * w)
    return delta, m_new, v_new


def adamw_shard(w, g0, g1, m, v):
    depth, rows, cols = w.shape
    half = rows // 2
    tr = min(256, half)
    nr = half // tr

    def body(w_ref, g0_ref, g1_ref, m_ref, v_ref, go_ref, d_ref, nm_ref, nv_ref):
        gv = jnp.where(pl.program_id(0) == 0, g0_ref[...], g1_ref[...])
        go_ref[...] = gv
        d_ref[...], nm_ref[...], nv_ref[...] = _adamw_math(w_ref[...], gv, m_ref[...], v_ref[...])

    spec = pl.BlockSpec((None, tr, cols), lambda l, h, r: (l, h * nr + r, 0))
    g0spec = pl.BlockSpec((None, tr, cols), lambda l, h, r: (jnp.where(l == 0, h, 1), jnp.where(l == 0, r, nr - 1), 0))
    g1spec = pl.BlockSpec((None, tr, cols), lambda l, h, r: (jnp.where(l == 1, h, 0), jnp.where(l == 1, r, 0), 0))
    return pl.pallas_call(
        body, name="adamw_shard", grid=(depth, 2, nr), in_specs=[spec, g0spec, g1spec, spec, spec], out_specs=[spec] * 4,
        out_shape=[jax.ShapeDtypeStruct(w.shape, f32)] * 4,
        compiler_params=_params("arbitrary", "arbitrary", "arbitrary"),
    )(w, g0, g1, m, v)


def adamw_cols(w, g, m, v, tc=34):
    cols, depth, rows = w.shape

    def body(w_ref, g_ref, m_ref, v_ref, d_ref, nm_ref, nv_ref):
        d_ref[...], nm_ref[...], nv_ref[...] = _adamw_math(w_ref[...], g_ref[...], m_ref[...], v_ref[...])

    spec = pl.BlockSpec((tc, depth, rows), lambda i: (i, 0, 0))
    return pl.pallas_call(
        body, name="adamw_cols", grid=(cols // tc,), in_specs=[spec] * 4, out_specs=[spec] * 3,
        out_shape=[jax.ShapeDtypeStruct(w.shape, f32)] * 3,
        compiler_params=_params("arbitrary"),
    )(w, g, m, v)


def adamw_small(ws, gs, ms, vs):
    n = len(ws)

    def body(*refs):
        ins, outs = refs[:4 * n], refs[4 * n:]
        for i in range(n):
            w_ref, g_ref, m_ref, v_ref = (ins[k * n + i] for k in range(4))
            outs[i][...], outs[n + i][...], outs[2 * n + i][...] = _adamw_math(w_ref[...], g_ref[...], m_ref[...], v_ref[...])

    outs = pl.pallas_call(
        body, name="adamw_small", out_shape=[jax.ShapeDtypeStruct(w.shape, f32) for w in ws] * 3,
    )(*ws, *gs, *ms, *vs)
    return outs[:n], outs[n:2 * n], outs[2 * n:]


WEIGHTS = ("mix_norm_g", "w_in", "q_gain", "k_gain", "sinks", "rel_bias", "conv_w", "conv_b", "dt_bias", "a_log", "d_skip",
           "ssm_norm_g", "w_out", "mlp_norm_g", "w_up", "w_down")
BIG = ("w_in", "w_out", "w_up", "w_down")
SMALL = tuple(n for n in WEIGHTS if n not in BIG)
PACK_COLS = 1024
PACK_ROWS = 16


def _pack(named, last=None):
    flat = jnp.concatenate([named[n].reshape(-1) for n in SMALL])
    tail = jnp.zeros((1,), f32) if last is None else last.reshape(1)
    pad = jnp.zeros((PACK_ROWS * PACK_COLS - flat.shape[0] - 1,), f32)
    return jnp.concatenate([flat, pad, tail]).reshape(PACK_ROWS, PACK_COLS)


def _unpack(buf, shapes):
    flat = buf.reshape(-1)
    out, at = {}, 0
    for n in SMALL:
        size = int(np.prod(shapes[n]))
        out[n] = flat[at:at + size].reshape(shapes[n])
        at += size
    return out


class _Exchange:
    GROUPS = {"A": (("w_up", 0), ("w_down", 0)), "B": (("w_in", 1), ("w_out", 1)), "C": (("w_up", 1), ("w_down", 1))}
    ICI_AT = {("mid", 0): "B", ("pre_out", 0): "C"}
    RELAY_AT = {("pre_out", 0): "A", ("pre_mlp", 0): "B", ("mid", 1): "C"}
    LAST = ("mix", 0)
    IN_FLIGHT = 2

    def __init__(self, wts, kc_arr):
        self.wts, self.kc_arr = wts, kc_arr
        self.own = {(n, l): wts[n][l].astype(bf16) for n in BIG for l in range(DEPTH)}
        now = gather_now([self.own["w_in", 0], self.own["w_out", 0]], wts["conv_w"])
        self.ready = {("w_in", 0): now[0], ("w_out", 0): now[1]}
        self.conv_w = jnp.transpose(now[2], (1, 2, 0, 3)).reshape(DEPTH, CONV_WIDTH, D_CONV)
        self.ici, self.relay = {}, {}
        self.scatter, self.share, self.reduced = [], [], {}
        self._start_ici("A", now[2])

    def _start_ici(self, g, after):
        srcs = [self.own[p] for p in self.GROUPS[g]]
        lands = [lax.empty((N_CHIPS,) + s.shape, s.dtype) for s in srcs]
        self.ici[g] = split_start("gather%s_ici_start" % g, _gather_maker("ici", len(srcs)), 3 * len(srcs), srcs + lands,
                                  after)
        return self.ici[g]["token"]

    def stage(self, name, after):
        if name == ("begin", 0):
            return self.ici["A"]["token"]
        tok = 0.0
        g = self.RELAY_AT.get(name)
        if g is not None:
            n = len(self.GROUPS[g])
            self.relay[g] = split_start("gather%s_relay_start" % g, _gather_maker("relay", n), 4 * n,
                                        split_wait(self.ici[g], after), after)
            tok = self.relay[g]["token"]
        if name in self.ICI_AT:
            tok = tok + self._start_ici(self.ICI_AT[name], after)
        return tok

    def _get(self, piece, after):
        if piece not in self.ready:
            g = [k for k, pieces in self.GROUPS.items() if piece in pieces][0]
            lands = split_wait(self.relay[g], after)[len(self.GROUPS[g]):]
            self.ready.update(zip(self.GROUPS[g], lands))
        return self.ready[piece]

    def w_in(self, l, after):
        return align_w_in(self._get(("w_in", l), after))

    def w_out(self, l, after):
        return self._get(("w_out", l), after).reshape(D_MODEL, D_MODEL)

    def mlp(self, l, after):
        return self._get(("w_up", l), after), self._get(("w_down", l), after)

    def _view(self, n, g):
        _, rows, cols = self.wts[n].shape
        return g.reshape(N_CHIPS, 2, rows // 2, cols)

    def grads(self, name, arrays, after):
        if name == self.LAST:
            self.held = (name, arrays)
            return 0.0
        return self._scatter(name, arrays, after) + self._advance(after, self.IN_FLIGHT)

    def flush(self, after):
        return self._scatter(*self.held, after) + self._advance(after, self.IN_FLIGHT)

    def _scatter(self, name, arrays, after):
        pieces = [(n, name[1]) for n in arrays]
        views = [self._view(n, g) for n, g in arrays.items()]
        lands = [lax.empty((7,) + v.shape[2:], bf16) for v in views]
        h = split_start("scatter_%s%d_start" % name, _scatter_maker(len(views)), 7 * len(views), views + lands, after)
        self.scatter.append((pieces, h))
        return h["token"]

    def _take_share(self, after):
        pieces, h = self.share.pop(0)
        self.reduced.update(zip(pieces, split_wait(h, after)))

    def _take_scatter(self, after):
        pieces, h = self.scatter.pop(0)
        done = split_wait(h, after)
        views, lands = done[:len(pieces)], done[len(pieces):]
        sums = [piece_sum(v, land, self.kc_arr) for v, land in zip(views, lands)]
        hs = split_start(h["name"].replace("scatter", "share"), _share_maker(len(sums)), len(sums), sums, after)
        self.share.append((pieces, hs))
        return hs["token"]

    def _advance(self, after, newest):
        if self.share:
            self._take_share(after)
        return self._take_scatter(after) if len(self.scatter) > newest else 0.0

    def prepare(self, piece, after):
        while piece not in self.reduced and not any(piece in pieces for pieces, _ in self.share):
            self._take_scatter(after)
        return self.share[-1][1]["tokens"] if self.share else after

    def reduced_piece(self, piece, after):
        while piece not in self.reduced:
            if any(piece in pieces for pieces, _ in self.share):
                self._take_share(after)
            else:
                self._take_scatter(after)
        return self.reduced[piece]


def kernel(x, mix_norm_g, w_in, q_gain, k_gain, sinks, rel_bias, conv_w, conv_b, dt_bias, a_log, d_skip, ssm_norm_g, w_out, mlp_norm_g, w_up, w_down, loss_target, m_mix_norm_g, m_w_in, m_q_gain, m_k_gain, m_sinks, m_rel_bias, m_conv_w, m_conv_b, m_dt_bias, m_a_log, m_d_skip, m_ssm_norm_g, m_w_out, m_mlp_norm_g, m_w_up, m_w_down, v_mix_norm_g, v_w_in, v_q_gain, v_k_gain, v_sinks, v_rel_bias, v_conv_w, v_conv_b, v_dt_bias, v_a_log, v_d_skip, v_ssm_norm_g, v_w_out, v_mlp_norm_g, v_w_up, v_w_down):
    wts = dict(mix_norm_g=mix_norm_g, w_in=w_in, q_gain=q_gain, k_gain=k_gain, sinks=sinks, rel_bias=rel_bias, conv_w=conv_w,
               conv_b=conv_b, dt_bias=dt_bias, a_log=a_log, d_skip=d_skip, ssm_norm_g=ssm_norm_g, w_out=w_out,
               mlp_norm_g=mlp_norm_g, w_up=w_up, w_down=w_down)
    mom = dict(mix_norm_g=m_mix_norm_g, w_in=m_w_in, q_gain=m_q_gain, k_gain=m_k_gain, sinks=m_sinks, rel_bias=m_rel_bias,
               conv_w=m_conv_w, conv_b=m_conv_b, dt_bias=m_dt_bias, a_log=m_a_log, d_skip=m_d_skip, ssm_norm_g=m_ssm_norm_g,
               w_out=m_w_out, mlp_norm_g=m_mlp_norm_g, w_up=m_w_up, w_down=m_w_down)
    var = dict(mix_norm_g=v_mix_norm_g, w_in=v_w_in, q_gain=v_q_gain, k_gain=v_k_gain, sinks=v_sinks, rel_bias=v_rel_bias,
               conv_w=v_conv_w, conv_b=v_conv_b, dt_bias=v_dt_bias, a_log=v_a_log, d_skip=v_d_skip, ssm_norm_g=v_ssm_norm_g,
               w_out=v_w_out, mlp_norm_g=v_mlp_norm_g, w_up=v_w_up, w_down=v_w_down)
    xi, yi, ci = _coords()
    k_me = 2 * xi + yi
    kc_arr = jnp.stack([k_me, ci]).astype(jnp.int32)

    prov = _Exchange(wts, kc_arr)
    small_w = {n: wts[n] for n in SMALL}
    small_w["conv_w"] = prov.conv_w
    loss, dx, grads, tok = local_step(x[0], loss_target[0], small_w, prov)

    small_shapes = {n: grads[n].shape for n in SMALL}
    small_sum = small_all_reduce(_pack(grads, loss) + tok)
    loss = small_sum[PACK_ROWS - 1, PACK_COLS - 1]
    tok = prov.flush(small_sum)
    small = _unpack(small_sum, small_shapes)
    cols = conv_w.shape[-1]
    small["conv_w"] = lax.dynamic_slice_in_dim(small["conv_w"], k_me * cols, cols, axis=2)
    g_out_d = dict(small)
    gs = [small[n] for n in SMALL]
    gs[0] = gs[0] + tok
    ds, nms, nvs = adamw_small([wts[n] for n in SMALL], gs, [mom[n] for n in SMALL], [var[n] for n in SMALL])
    d_out_d, m_out_d, v_out_d = dict(zip(SMALL, ds)), dict(zip(SMALL, nms)), dict(zip(SMALL, nvs))

    rows, cols = wts["w_in"].shape[1:]
    to_cols = lambda a: jnp.transpose(a, (2, 0, 1))
    prov.prepare(("w_up", 0), ds[0])
    g1_t = to_cols(prov.reduced_piece(("w_in", 1), ds[0])).reshape(cols, rows)
    after = g1_t
    for n in ("w_up", "w_down", "w_in", "w_out"):
        g0, g1 = (prov.reduced_piece((n, l), after) for l in range(DEPTH))
        if n == "w_in":
            g_t = jnp.stack([to_cols(g0).reshape(cols, rows), g1_t], axis=1)
            res_t = adamw_cols(to_cols(wts[n]), g_t, to_cols(mom[n]), to_cols(var[n]))
            g_out_d[n], d_out_d[n], m_out_d[n], v_out_d[n] = (jnp.transpose(a, (1, 2, 0)) for a in (g_t, *res_t))
        else:
            g_out_d[n], d_out_d[n], m_out_d[n], v_out_d[n] = adamw_shard(wts[n], g0, g1, mom[n], var[n])
        after = d_out_d[n]

    return (loss, dx[None], *[g_out_d[n] for n in WEIGHTS], *[d_out_d[n] for n in WEIGHTS],
            *[m_out_d[n] for n in WEIGHTS], *[v_out_d[n] for n in WEIGHTS])
```

```python
import numpy as np
import jax
import jax.numpy as jnp
from jax import lax
from jax.experimental import pallas as pl
from jax.experimental.pallas import tpu as pltpu

f32 = jnp.float32
bf16 = jnp.bfloat16

SEQ = 2048
D_MODEL = 1024
DEPTH = 2
HEAD_DIM = 64
N_Q_HEADS = 8
N_KV_HEADS = 2
Q_PER_KV = N_Q_HEADS // N_KV_HEADS
BLOCK = 128
N_BLOCKS = SEQ // BLOCK
N_BUCKETS = 32
MAX_DISTANCE = 128
SSM_HEADS = 8
SSM_HEAD_DIM = 64
SSM_GROUPS = 2
HEADS_PER_GROUP = SSM_HEADS // SSM_GROUPS
SSM_STATE = 128
CONV_WIDTH = 4
CHUNK = 128
N_CHUNKS = SEQ // CHUNK
D_FF = 4 * D_MODEL
D_ATTN = N_Q_HEADS * HEAD_DIM
D_KV = N_KV_HEADS * HEAD_DIM
D_SSM = SSM_HEADS * SSM_HEAD_DIM
D_BC = SSM_GROUPS * SSM_STATE
D_CONV = D_SSM + 2 * D_BC
D_IN = D_ATTN + 2 * D_KV + D_SSM + D_CONV + SSM_HEADS
EPS = 1e-6
NEG = -1e30
N_CHIPS = 4
FF_TILE = D_FF // N_CHIPS

LANE = 128
PW = D_ATTN + D_SSM + D_CONV + 2 * D_KV + LANE
OFF_Q, OFF_Z, OFF_X, OFF_K, OFF_V, OFF_DT = 0, 512, 1024, 2048, 2176, 2304

ADAM_LR = 0.001
ADAM_B1 = 0.9
ADAM_B2 = 0.999
ADAM_EPS = 1e-08
ADAM_WD = 0.01
ADAM_STEP = 10

VMEM_LIMIT = 56 * 1024 * 1024


def _params(*sem):
    return pltpu.CompilerParams(dimension_semantics=tuple(sem), vmem_limit_bytes=VMEM_LIMIT)


def _bdot(a, b):
    return jnp.dot(a.astype(bf16), b.astype(bf16), preferred_element_type=f32)


def _bdot_nt(a, b):
    return lax.dot_general(a.astype(bf16), b.astype(bf16), (((1,), (1,)), ((), ())), preferred_element_type=f32)


def _bdot_tn(a, b):
    return lax.dot_general(a.astype(bf16), b.astype(bf16), (((0,), (0,)), ((), ())), preferred_element_type=f32)


def _hdot(a, b):
    return jnp.dot(a, b, precision=lax.Precision.HIGHEST, preferred_element_type=f32)


def _sigmoid(x):
    return 1.0 / (1.0 + jnp.exp(-x))


def _softplus(x):
    return jnp.maximum(x, 0.0) + jnp.log1p(jnp.exp(-jnp.abs(x)))


def _rms(x):
    return lax.rsqrt(jnp.mean(x * x, axis=-1, keepdims=True) + EPS)


def _rms_bwd(dy, xhat, r, g):
    t = dy * g
    return r * (t - xhat * jnp.mean(t * xhat, axis=-1, keepdims=True))


def _full(shape):
    return pl.BlockSpec(shape, lambda *_: (0,) * len(shape))


def _bucket_table():
    qi = np.arange(BLOCK)[:, None]
    kj = np.arange(2 * BLOCK)[None, :]
    dist = qi + BLOCK - kj
    ok = (dist >= 0) & (dist < 128)
    d = np.clip(dist, 0, None)
    max_exact = N_BUCKETS // 2
    d_f = np.maximum(d, 1).astype(np.float32)
    large = max_exact + (np.log(d_f / np.float32(max_exact)) / np.float32(np.log(MAX_DISTANCE / max_exact))
                         * np.float32(N_BUCKETS - max_exact)).astype(np.int32)
    large = np.minimum(large, N_BUCKETS - 1)
    bucket = np.where(d < max_exact, d, large)
    return np.where(ok, bucket, -1).astype(np.int32)


def bias_build(rel_bias, bucket):
    def body(rel_ref, bkt_ref, o_ref):
        bkt = bkt_ref[...]
        for h in range(N_Q_HEADS):
            acc = jnp.where(bkt < 0, NEG, 0.0).astype(f32)
            for b in range(N_BUCKETS):
                acc = acc + jnp.where(bkt == b, rel_ref[b, h], 0.0)
            o_ref[h] = acc

    return pl.pallas_call(
        body, name="bias_build", out_shape=jax.ShapeDtypeStruct((N_Q_HEADS,) + bucket.shape, f32),
        in_specs=[pl.BlockSpec(memory_space=pltpu.SMEM), pl.BlockSpec(memory_space=pltpu.VMEM)],
        out_specs=pl.BlockSpec(memory_space=pltpu.VMEM),
    )(rel_bias, bucket)


def bias_bwd(dband0, dband1, bucket):
    def body(d0_ref, d1_ref, bkt_ref, o_ref):
        bkt = bkt_ref[...]
        o_ref[...] = jnp.zeros_like(o_ref)
        for h in range(N_Q_HEADS):
            d = d0_ref[h] + d1_ref[h]
            for b in range(N_BUCKETS):
                part = jnp.sum(jnp.where(bkt == b, d, 0.0), axis=1, keepdims=True)
                o_ref[b:b + 1, h:h + 1] = jnp.sum(part, axis=0, keepdims=True)

    return pl.pallas_call(
        body, name="bias_bwd", out_shape=jax.ShapeDtypeStruct((N_BUCKETS, LANE), f32),
    )(dband0, dband1, bucket)


W_IN_SHARD = D_IN // N_CHIPS
_ALIGNED_PIECES = ((0, 0, 512), (1, 190, 578), (2, 0, 124), (2, 124, 578), (3, 0, 570), (0, 512, 578), (1, 0, 62),
                   (1, 62, 190), (3, 570, 578))
_SHARD_PIECES = (((0, 512), (2048, 2114)), ((2114, 2176), (2176, 2304), (512, 900)), ((900, 1024), (1024, 1478)),
                 ((1478, 2048), (2304, 2312)))


def align_w_in(shards, tr=256):
    def body(s_ref, o_ref):
        parts = [s_ref[k, :, a:b] for k, a, b in _ALIGNED_PIECES]
        parts.append(jnp.zeros((tr, LANE - SSM_HEADS), s_ref.dtype))
        o_ref[...] = jnp.concatenate(parts, axis=-1)

    return pl.pallas_call(
        body, name="align_w_in", grid=(D_MODEL // tr,),
        in_specs=[pl.BlockSpec((N_CHIPS, tr, W_IN_SHARD), lambda i: (0, i, 0))],
        out_specs=pl.BlockSpec((tr, PW), lambda i: (i, 0)),
        out_shape=jax.ShapeDtypeStruct((D_MODEL, PW), shards.dtype),
        compiler_params=_params("arbitrary"),
    )(shards)


def split_w_in_grad(dw, tr=256):
    def body(d_ref, o16_ref):
        for k, pieces in enumerate(_SHARD_PIECES):
            o16_ref[k] = jnp.concatenate([d_ref[:, a:b] for a, b in pieces], axis=-1).astype(bf16)

    return pl.pallas_call(
        body, name="split_w_in_grad", grid=(D_MODEL // tr,),
        in_specs=[pl.BlockSpec((tr, PW), lambda i: (i, 0))],
        out_specs=pl.BlockSpec((N_CHIPS, tr, W_IN_SHARD), lambda i: (0, i, 0)),
        out_shape=jax.ShapeDtypeStruct((N_CHIPS, D_MODEL, W_IN_SHARD), bf16),
        compiler_params=_params("arbitrary"),
    )(dw)

def in_fwd(x, g, w, tm=512):
    def body(x_ref, g_ref, w_ref, o_ref):
        xv = x_ref[...]
        h = xv * _rms(xv) * g_ref[...]
        o_ref[...] = _bdot(h, w_ref[...])

    return pl.pallas_call(
        body, name="in_fwd", grid=(SEQ // tm,),
        in_specs=[pl.BlockSpec((tm, D_MODEL), lambda i: (i, 0)), _full((1, D_MODEL)), _resident((D_MODEL, PW))],
        out_specs=pl.BlockSpec((tm, PW), lambda i: (i, 0)),
        out_shape=jax.ShapeDtypeStruct((SEQ, PW), f32),
        compiler_params=_params("arbitrary"),
    )(x, g, w)


def _resident(shape):
    return pl.BlockSpec(shape, lambda *_: (0,) * len(shape), pipeline_mode=pl.Buffered(1))


def in_bwd(dq, dz, dxbc, dk, dv, ddt, x, g, w, dres, tm=512):
    def body(dq_ref, dz_ref, dx_ref, dk_ref, dv_ref, ddt_ref, x_ref, g_ref, w_ref, dres_ref, o_ref, dw_ref, dg_ref):
        i = pl.program_id(0)

        @pl.when(i == 0)
        def _():
            dw_ref[...] = jnp.zeros_like(dw_ref)
            dg_ref[...] = jnp.zeros_like(dg_ref)

        dproj = jnp.concatenate([dq_ref[...], dz_ref[...], dx_ref[...], dk_ref[...], dv_ref[...], ddt_ref[...]],
                                axis=-1).astype(bf16)
        xv = x_ref[...]
        r = _rms(xv)
        xhat = xv * r
        gv = g_ref[...]
        h = xhat * gv
        dw_ref[...] += _bdot_tn(h, dproj)
        dh = _bdot_nt(dproj, w_ref[...])
        dg_ref[...] += jnp.sum(dh * xhat, axis=0, keepdims=True)
        o_ref[...] = dres_ref[...] + _rms_bwd(dh, xhat, r, gv)

    tok = lambda w_: pl.BlockSpec((tm, w_), lambda i: (i, 0))
    return pl.pallas_call(
        body, name="in_bwd", grid=(SEQ // tm,),
        in_specs=[tok(D_ATTN), tok(D_SSM), tok(D_CONV), tok(D_KV), tok(D_KV), tok(LANE), tok(D_MODEL),
                  _full((1, D_MODEL)), _resident((D_MODEL, PW)), tok(D_MODEL)],
        out_specs=[tok(D_MODEL), _resident((D_MODEL, PW)), _full((1, D_MODEL))],
        out_shape=[jax.ShapeDtypeStruct((SEQ, D_MODEL), f32), jax.ShapeDtypeStruct((D_MODEL, PW), f32),
                   jax.ShapeDtypeStruct((1, D_MODEL), f32)],
        compiler_params=_params("arbitrary"),
    )(dq, dz, dxbc, dk, dv, ddt, x, g, w, dres)


def _attn_softmax_t(qk, bias_t, sink, first, key_row):
    s = qk * (HEAD_DIM ** -0.5) + bias_t
    s = jnp.where(jnp.logical_and(first, key_row < BLOCK), NEG, s)
    m = jnp.maximum(jnp.max(s, axis=0, keepdims=True), sink)
    p = jnp.exp(s - m)
    psink = jnp.exp(sink - m)
    inv = 1.0 / (jnp.sum(p, axis=0, keepdims=True) + psink)
    return p * inv, psink * inv


def _rms_t(x_t):
    return lax.rsqrt(jnp.mean(x_t * x_t, axis=0, keepdims=True) + EPS)


def attn_fwd_t(proj, q_gain_col, k_gain, sinks, bias_t):
    kcol, vcol = OFF_K // D_KV, OFF_V // D_KV

    def body(q_ref, kc_ref, kp_ref, vc_ref, vp_ref, qg_ref, kg_ref, sink_ref, bias_ref, o_ref, ot_scr):
        n = pl.program_id(0)
        first = n == 0
        key_row = lax.broadcasted_iota(jnp.int32, (2 * BLOCK, BLOCK), 0)
        k2 = jnp.concatenate([kp_ref[...], kc_ref[...]], axis=0)
        v_t = jnp.concatenate([vp_ref[...], vc_ref[...]], axis=0).T
        q_t = q_ref[...].T
        qg = jnp.broadcast_to(qg_ref[...], (HEAD_DIM, BLOCK))
        kg = kg_ref[...]
        for hk in range(N_KV_HEADS):
            sl = slice(hk * HEAD_DIM, (hk + 1) * HEAD_DIM)
            kk = k2[:, sl]
            kn = (kk * _rms(kk) * kg).astype(bf16)
            vt = v_t[sl, :].astype(bf16)
            heads = range(hk * Q_PER_KV, (hk + 1) * Q_PER_KV)
            qns = []
            for h in heads:
                qh = q_t[h * HEAD_DIM:(h + 1) * HEAD_DIM, :]
                qns.append(qh * _rms_t(qh) * qg)
            scores = [_bdot(kn, qn) for qn in qns]
            for h, s in zip(heads, scores):
                p, _ = _attn_softmax_t(s, bias_ref[h], sink_ref[h], first, key_row)
                ot_scr[h * HEAD_DIM:(h + 1) * HEAD_DIM, :] = _bdot(vt, p)
        o_ref[...] = ot_scr[...].T

    prev = lambda n: jnp.maximum(n - 1, 0)
    return pl.pallas_call(
        body, name="attn_fwd", grid=(N_BLOCKS,),
        in_specs=[pl.BlockSpec((BLOCK, D_ATTN), lambda n: (n, 0)),
                  pl.BlockSpec((BLOCK, D_KV), lambda n: (n, kcol)), pl.BlockSpec((BLOCK, D_KV), lambda n: (prev(n), kcol)),
                  pl.BlockSpec((BLOCK, D_KV), lambda n: (n, vcol)), pl.BlockSpec((BLOCK, D_KV), lambda n: (prev(n), vcol)),
                  _full((HEAD_DIM, 1)), _full((1, HEAD_DIM)), pl.BlockSpec(memory_space=pltpu.SMEM),
                  _full((N_Q_HEADS, 2 * BLOCK, BLOCK))],
        out_specs=pl.BlockSpec((BLOCK, D_ATTN), lambda n: (n, 0)),
        out_shape=jax.ShapeDtypeStruct((SEQ, D_ATTN), f32),
        scratch_shapes=[pltpu.VMEM((D_ATTN, BLOCK), f32)],
        compiler_params=_params("arbitrary"),
    )(proj, proj, proj, proj, proj, q_gain_col, k_gain, sinks, bias_t)


def attn_bwd_t(proj, d_out, q_gain_col, k_gain, sinks, bias_t):
    kcol, vcol = OFF_K // D_KV, OFF_V // D_KV

    def body(q_ref, kc_ref, kp_ref, vc_ref, vp_ref, do_ref, qg_ref, kg_ref, sink_ref, bias_ref,
             dq_ref, dk_ref, dv_ref, dband_ref, dsink_ref, dqg_ref, dkg_ref, dkn_scr, dv_scr, dqt_scr, dsink_acc, dqg_acc):
        i = pl.program_id(0)
        first = i == N_BLOCKS - 1

        @pl.when(i == 0)
        def _():
            for ref in (dband_ref, dkg_ref, dkn_scr, dv_scr, dsink_acc, dqg_acc):
                ref[...] = jnp.zeros_like(ref)

        key_row = lax.broadcasted_iota(jnp.int32, (2 * BLOCK, BLOCK), 0)
        k2 = jnp.concatenate([kp_ref[...], kc_ref[...]], axis=0)
        v2 = jnp.concatenate([vp_ref[...], vc_ref[...]], axis=0)
        q_t = q_ref[...].T
        do_t = do_ref[...].T
        qg = jnp.broadcast_to(qg_ref[...], (HEAD_DIM, BLOCK))
        kg = kg_ref[...]
        scale = HEAD_DIM ** -0.5
        for hk in range(N_KV_HEADS):
            sl = slice(hk * HEAD_DIM, (hk + 1) * HEAD_DIM)
            kk = k2[:, sl]
            rk = _rms(kk)
            khat = kk * rk
            kn = (khat * kg).astype(bf16)
            vb = v2[:, sl].astype(bf16)
            dkn = jnp.zeros((2 * BLOCK, HEAD_DIM), f32)
            dvv = jnp.zeros((2 * BLOCK, HEAD_DIM), f32)
            heads = range(hk * Q_PER_KV, (hk + 1) * Q_PER_KV)
            rqs, qhats, qns, d_os = [], [], [], []
            for h in heads:
                hs = slice(h * HEAD_DIM, (h + 1) * HEAD_DIM)
                qh = q_t[hs, :]
                rqs.append(_rms_t(qh))
                qhats.append(qh * rqs[-1])
                qns.append((qhats[-1] * qg).astype(bf16))
                d_os.append(do_t[hs, :].astype(bf16))
            scores = [_bdot(kn, qn) for qn in qns]
            dps = [_bdot(vb, d_o) for d_o in d_os]
            ps, dss = [], []
            for h, s, dp in zip(heads, scores, dps):
                p, psink = _attn_softmax_t(s, bias_ref[h], sink_ref[h], first, key_row)
                delta = jnp.sum(p * dp, axis=0, keepdims=True)
                ds = p * (dp - delta)
                dband_ref[h] += ds
                dsink_acc[h:h + 1, :] += -(psink * delta)
                ps.append(p.astype(bf16))
                dss.append(ds.astype(bf16))
            dqns = [_bdot_tn(kn, ds) * scale for ds in dss]
            for ds, qn, p, d_o in zip(dss, qns, ps, d_os):
                dkn = dkn + _bdot_nt(ds, qn) * scale
                dvv = dvv + _bdot_nt(p, d_o)
            for h, dqn, rq, qhat in zip(heads, dqns, rqs, qhats):
                dqg_acc[...] += dqn * qhat
                t = dqn * qg
                dqt_scr[h * HEAD_DIM:(h + 1) * HEAD_DIM, :] = rq * (t - qhat * jnp.mean(t * qhat, axis=0, keepdims=True))
            dkn_cur = dkn[BLOCK:] + dkn_scr[:, sl]
            dkn_scr[:, sl] = dkn[:BLOCK]
            khat_c, rk_c = khat[BLOCK:], rk[BLOCK:]
            dkg_ref[...] += jnp.sum(dkn_cur * khat_c, axis=0, keepdims=True)
            dk_ref[:, sl] = _rms_bwd(dkn_cur, khat_c, rk_c, kg)
            dv_ref[:, sl] = dvv[BLOCK:] + dv_scr[:, sl]
            dv_scr[:, sl] = dvv[:BLOCK]
        dq_ref[...] = dqt_scr[...].T

        @pl.when(i == N_BLOCKS - 1)
        def _():
            dsink_ref[...] = jnp.sum(dsink_acc[...], axis=1, keepdims=True)
            dqg_ref[...] = jnp.sum(dqg_acc[...], axis=1, keepdims=True)

    blk = lambda i: N_BLOCKS - 1 - i
    prev = lambda i: jnp.maximum(N_BLOCKS - 2 - i, 0)
    return pl.pallas_call(
        body, name="attn_bwd", grid=(N_BLOCKS,),
        in_specs=[pl.BlockSpec((BLOCK, D_ATTN), lambda i: (blk(i), 0)),
                  pl.BlockSpec((BLOCK, D_KV), lambda i: (blk(i), kcol)), pl.BlockSpec((BLOCK, D_KV), lambda i: (prev(i), kcol)),
                  pl.BlockSpec((BLOCK, D_KV), lambda i: (blk(i), vcol)), pl.BlockSpec((BLOCK, D_KV), lambda i: (prev(i), vcol)),
                  pl.BlockSpec((BLOCK, D_ATTN), lambda i: (blk(i), 0)),
                  _full((HEAD_DIM, 1)), _full((1, HEAD_DIM)), pl.BlockSpec(memory_space=pltpu.SMEM),
                  _full((N_Q_HEADS, 2 * BLOCK, BLOCK))],
        out_specs=[pl.BlockSpec((BLOCK, D_ATTN), lambda i: (blk(i), 0)), pl.BlockSpec((BLOCK, D_KV), lambda i: (blk(i), 0)),
                   pl.BlockSpec((BLOCK, D_KV), lambda i: (blk(i), 0)), _full((N_Q_HEADS, 2 * BLOCK, BLOCK)),
                   _full((N_Q_HEADS, 1)), _full((HEAD_DIM, 1)), _full((1, HEAD_DIM))],
        out_shape=[jax.ShapeDtypeStruct((SEQ, D_ATTN), f32), jax.ShapeDtypeStruct((SEQ, D_KV), f32),
                   jax.ShapeDtypeStruct((SEQ, D_KV), f32), jax.ShapeDtypeStruct((N_Q_HEADS, 2 * BLOCK, BLOCK), f32),
                   jax.ShapeDtypeStruct((N_Q_HEADS, 1), f32), jax.ShapeDtypeStruct((HEAD_DIM, 1), f32),
                   jax.ShapeDtypeStruct((1, HEAD_DIM), f32)],
        scratch_shapes=[pltpu.VMEM((BLOCK, D_KV), f32), pltpu.VMEM((BLOCK, D_KV), f32), pltpu.VMEM((D_ATTN, BLOCK), f32),
                        pltpu.VMEM((N_Q_HEADS, BLOCK), f32), pltpu.VMEM((HEAD_DIM, BLOCK), f32)],
        compiler_params=_params("arbitrary"),
    )(proj, proj, proj, proj, proj, d_out, q_gain_col, k_gain, sinks, bias_t)


SUBLANES = 8


def _shift_down(u, s, row8):
    if s == 0:
        return u
    r = pltpu.roll(u, s, 0)
    return jnp.concatenate([jnp.where(row8 >= s, r[:SUBLANES], 0.0), r[SUBLANES:]], axis=0)


def _shift_up(u, s, row8):
    if s == 0:
        return u
    r = pltpu.roll(u, SEQ - s, 0)
    return jnp.concatenate([r[:-SUBLANES], jnp.where(row8 < SUBLANES - s, r[-SUBLANES:], 0.0)], axis=0)


def conv_fwd(proj, conv_w, conv_b):
    xcol = OFF_X // LANE

    def body(u_ref, w_ref, b_ref, o_ref):
        u = u_ref[...]
        row = lax.broadcasted_iota(jnp.int32, (SUBLANES, LANE), 0)
        pre = b_ref[...] + jnp.zeros_like(u)
        for k in range(CONV_WIDTH):
            pre = pre + w_ref[k:k + 1, :] * _shift_down(u, CONV_WIDTH - 1 - k, row)
        o_ref[...] = pre * _sigmoid(pre)

    return pl.pallas_call(
        body, name="conv_fwd", grid=(D_CONV // LANE,),
        in_specs=[pl.BlockSpec((SEQ, LANE), lambda j: (0, xcol + j)), pl.BlockSpec((CONV_WIDTH, LANE), lambda j: (0, j)),
                  pl.BlockSpec((1, LANE), lambda j: (0, j))],
        out_specs=pl.BlockSpec((SEQ, LANE), lambda j: (0, j)),
        out_shape=jax.ShapeDtypeStruct((SEQ, D_CONV), f32),
        compiler_params=_params("arbitrary"),
    )(proj, conv_w, conv_b)


def conv_bwd(proj, d_act, conv_w, conv_b):
    xcol = OFF_X // LANE

    def body(u_ref, da_ref, w_ref, b_ref, du_ref, dw_ref, db_ref):
        u = u_ref[...]
        row = lax.broadcasted_iota(jnp.int32, (SUBLANES, LANE), 0)
        shifted = [_shift_down(u, CONV_WIDTH - 1 - k, row) for k in range(CONV_WIDTH)]
        pre = b_ref[...] + jnp.zeros_like(u)
        for k in range(CONV_WIDTH):
            pre = pre + w_ref[k:k + 1, :] * shifted[k]
        sg = _sigmoid(pre)
        dpre = da_ref[...] * (sg * (1.0 + pre * (1.0 - sg)))
        db_ref[...] = jnp.sum(dpre, axis=0, keepdims=True)
        du = jnp.zeros_like(u)
        for k in range(CONV_WIDTH):
            dw_ref[k:k + 1, :] = jnp.sum(dpre * shifted[k], axis=0, keepdims=True)
            du = du + w_ref[k:k + 1, :] * _shift_up(dpre, CONV_WIDTH - 1 - k, row)
        du_ref[...] = du

    return pl.pallas_call(
        body, name="conv_bwd", grid=(D_CONV // LANE,),
        in_specs=[pl.BlockSpec((SEQ, LANE), lambda j: (0, xcol + j)), pl.BlockSpec((SEQ, LANE), lambda j: (0, j)),
                  pl.BlockSpec((CONV_WIDTH, LANE), lambda j: (0, j)), pl.BlockSpec((1, LANE), lambda j: (0, j))],
        out_specs=[pl.BlockSpec((SEQ, LANE), lambda j: (0, j)), pl.BlockSpec((CONV_WIDTH, LANE), lambda j: (0, j)),
                   pl.BlockSpec((1, LANE), lambda j: (0, j))],
        out_shape=[jax.ShapeDtypeStruct((SEQ, D_CONV), f32), jax.ShapeDtypeStruct((CONV_WIDTH, D_CONV), f32),
                   jax.ShapeDtypeStruct((1, D_CONV), f32)],
        compiler_params=_params("arbitrary"),
    )(proj, d_act, conv_w, conv_b)


def _ssd_chunk_common(dt_raw, dtb, alog):
    row = lax.broadcasted_iota(jnp.int32, (CHUNK, CHUNK), 0)
    col = lax.broadcasted_iota(jnp.int32, (CHUNK, CHUNK), 1)
    tri = (row >= col).astype(f32)
    strict = (row > col).astype(f32)
    dtp = _softplus(dt_raw + dtb)
    a_row = -jnp.exp(alog)
    d_a = dtp * a_row
    cs = _hdot(tri, d_a)
    cs_last = cs[CHUNK - 1:CHUNK, :]
    return row, col, dtp, a_row, cs, cs.T, cs_last


def _seg_decay(cs, cs_t, hd, row, col):
    seg = cs[:, hd:hd + 1] - cs_t[hd:hd + 1, :]
    return jnp.where(row >= col, jnp.exp(seg), 0.0)


GROUP_W = HEADS_PER_GROUP * SSM_HEAD_DIM


def _group_indicator(g):
    j = lax.broadcasted_iota(jnp.int32, (GROUP_W, LANE), 0)
    lane = lax.broadcasted_iota(jnp.int32, (GROUP_W, LANE), 1)
    return (lane == g * HEADS_PER_GROUP + j // SSM_HEAD_DIM).astype(bf16)


def _bf16_pieces(a, n):
    pieces = []
    for _ in range(n):
        p = a.astype(bf16)
        pieces.append(p)
        a = a - p.astype(f32)
    return pieces


def _head_spread(a, ind):
    return sum(lax.dot_general(p, ind, (((1,), (1,)), ((), ())), preferred_element_type=f32) for p in _bf16_pieces(a, 3))


def _head_sums(a, ind):
    return sum(jnp.dot(p, ind, preferred_element_type=f32) for p in _bf16_pieces(a, 2))


def ssd_fwd_g(act, proj, dt_bias, a_log, d_skip, norm_g):
    zcol, dtcol = OFF_Z // D_SSM, OFF_DT // LANE

    def body(act_ref, z_ref, dt_ref, dtb_ref, alog_ref, dsk_ref, ng_ref, out_ref, ypre_ref, st_ref, state):
        c = pl.program_id(0)

        @pl.when(c == 0)
        def _():
            state[...] = jnp.zeros_like(state)

        row, col, dtp, a_row, cs, cs_t, cs_last = _ssd_chunk_common(dt_ref[...], dtb_ref[...], alog_ref[...])
        e_cs = jnp.exp(cs)
        dte = jnp.exp(cs_last - cs)
        rows8 = jnp.concatenate([jnp.exp(cs_last), dsk_ref[...], jnp.zeros((6, LANE), f32)], axis=0)
        z = z_ref[...]
        sz = z * _sigmoid(z)
        ng = ng_ref[...]
        for g in range(SSM_GROUPS):
            gs = slice(g * GROUP_W, (g + 1) * GROUP_W)
            ind = _group_indicator(g)
            xg = act_ref[:, gs]
            bg = act_ref[:, D_SSM + g * SSM_STATE:D_SSM + (g + 1) * SSM_STATE]
            cg = act_ref[:, D_SSM + D_BC + g * SSM_STATE:D_SSM + D_BC + (g + 1) * SSM_STATE]
            dt_e, e_e, dte_e = _head_spread(dtp, ind), _head_spread(e_cs, ind), _head_spread(dte, ind)
            rows_e = _head_spread(rows8, ind)
            ecl_e, dsk_e = rows_e[0:1], rows_e[1:2]
            xdt = xg * dt_e
            prev = state[g]
            st_ref[0, g] = prev
            cb = _bdot_nt(cg, bg)
            goff = _bdot(cg, prev)
            snew = _bdot_tn(bg, xdt * dte_e)
            heads = range(g * HEADS_PER_GROUP, (g + 1) * HEADS_PER_GROUP)
            ms = [cb * _seg_decay(cs, cs_t, hd, row, col) for hd in heads]
            yd = [_bdot(m, xdt[:, r * SSM_HEAD_DIM:(r + 1) * SSM_HEAD_DIM]) for r, m in enumerate(ms)]
            y = jnp.concatenate(yd, axis=1) + e_e * goff + xg * dsk_e
            state[g] = prev * ecl_e + snew
            ypre_ref[:, gs] = y
            part = y * sz[:, gs]
            out_ref[:, gs] = part * _rms(part) * ng[:, gs]

    return pl.pallas_call(
        body, name="ssd_fwd", grid=(N_CHUNKS,),
        in_specs=[pl.BlockSpec((CHUNK, D_CONV), lambda c: (c, 0)), pl.BlockSpec((CHUNK, D_SSM), lambda c: (c, zcol)),
                  pl.BlockSpec((CHUNK, LANE), lambda c: (c, dtcol)), _full((1, LANE)), _full((1, LANE)), _full((1, LANE)),
                  _full((1, D_SSM))],
        out_specs=[pl.BlockSpec((CHUNK, D_SSM), lambda c: (c, 0)), pl.BlockSpec((CHUNK, D_SSM), lambda c: (c, 0)),
                   pl.BlockSpec((1, SSM_GROUPS, SSM_STATE, GROUP_W), lambda c: (c, 0, 0, 0))],
        out_shape=[jax.ShapeDtypeStruct((SEQ, D_SSM), f32), jax.ShapeDtypeStruct((SEQ, D_SSM), f32),
                   jax.ShapeDtypeStruct((N_CHUNKS, SSM_GROUPS, SSM_STATE, GROUP_W), f32)],
        scratch_shapes=[pltpu.VMEM((SSM_GROUPS, SSM_STATE, GROUP_W), f32)],
        compiler_params=_params("arbitrary"),
    )(act, proj, proj, dt_bias, a_log, d_skip, norm_g)


def ssd_bwd_g(act, proj, ypre, states, d_out, dt_bias, a_log, d_skip, norm_g):
    zcol, dtcol = OFF_Z // D_SSM, OFF_DT // LANE

    def body(act_ref, z_ref, dt_ref, ypre_ref, st_ref, do_ref, dtb_ref, alog_ref, dsk_ref, ng_ref,
             dact_ref, ddt_ref, dz_ref, dng_ref, dpar_ref, dstate):
        i = pl.program_id(0)

        @pl.when(i == 0)
        def _():
            for ref in (dng_ref, dpar_ref, dstate):
                ref[...] = jnp.zeros_like(ref)

        row, col, dtp, a_row, cs, cs_t, cs_last = _ssd_chunk_common(dt_ref[...], dtb_ref[...], alog_ref[...])
        upper = (row <= col).astype(f32)
        lane = lax.broadcasted_iota(jnp.int32, (CHUNK, LANE), 1)
        rowl = lax.broadcasted_iota(jnp.int32, (CHUNK, LANE), 0)
        e_cs = jnp.exp(cs)
        dte = jnp.exp(cs_last - cs)
        ecl = jnp.exp(cs_last)
        rows8 = jnp.concatenate([ecl, dsk_ref[...], jnp.zeros((6, LANE), f32)], axis=0)
        z = z_ref[...]
        sgz = _sigmoid(z)
        sz = z * sgz
        ng = ng_ref[...]
        ddt_mat = jnp.zeros((CHUNK, LANE), f32)
        dcs_mat = jnp.zeros((CHUNK, LANE), f32)
        dcs_t = jnp.zeros((LANE, CHUNK), f32)
        dcsl_row = jnp.zeros((1, LANE), f32)
        dd_row = jnp.zeros((1, LANE), f32)
        for g in range(SSM_GROUPS):
            gs = slice(g * GROUP_W, (g + 1) * GROUP_W)
            bsl = slice(D_SSM + g * SSM_STATE, D_SSM + (g + 1) * SSM_STATE)
            csl = slice(D_SSM + D_BC + g * SSM_STATE, D_SSM + D_BC + (g + 1) * SSM_STATE)
            ind = _group_indicator(g)
            y = ypre_ref[:, gs]
            part = y * sz[:, gs]
            r = _rms(part)
            yhat = part * r
            d_o = do_ref[:, gs]
            dng_ref[:, gs] += jnp.sum(d_o * yhat, axis=0, keepdims=True)
            dyz = _rms_bwd(d_o, yhat, r, ng[:, gs])
            dy = dyz * sz[:, gs]
            dz_ref[:, gs] = dyz * y * (sgz[:, gs] * (1.0 + z[:, gs] * (1.0 - sgz[:, gs])))

            xg = act_ref[:, gs]
            bg = act_ref[:, bsl]
            cg = act_ref[:, csl]
            dt_e, e_e, dte_e = _head_spread(dtp, ind), _head_spread(e_cs, ind), _head_spread(dte, ind)
            rows_e = _head_spread(rows8, ind)
            ecl_e, dsk_e = rows_e[0:1], rows_e[1:2]
            xdt = xg * dt_e
            prev = st_ref[0, g]
            dh = dstate[g]
            heads = range(g * HEADS_PER_GROUP, (g + 1) * HEADS_PER_GROUP)
            hsl = [slice(r_ * SSM_HEAD_DIM, (r_ + 1) * SSM_HEAD_DIM) for r_ in range(HEADS_PER_GROUP)]
            cb = _bdot_nt(cg, bg)
            lms = [_seg_decay(cs, cs_t, hd, row, col) for hd in heads]
            ms = [cb * lm for lm in lms]
            gmat = _bdot(cg, prev)
            dgm = dy * e_e
            dcg = _bdot_nt(dgm, prev)
            dprev = _bdot_tn(cg, dgm)
            dbg = _bdot_nt(xdt * dte_e, dh)
            dw = _bdot(bg, dh)
            dms = [_bdot_nt(dy[:, s_], xdt[:, s_]) for s_ in hsl]
            dxdts = [_bdot_tn(m, dy[:, s_]) for m, s_ in zip(ms, hsl)]
            dxdt = jnp.concatenate(dxdts, axis=1) + dw * dte_e
            dact_ref[:, gs] = dy * dsk_e + dxdt * dt_e
            dstate[g] = dprev + dh * ecl_e
            dcb = jnp.zeros((CHUNK, CHUNK), f32)
            for hd, dm, lm, m in zip(heads, dms, lms, ms):
                dcb = dcb + dm * lm
                dseg = dm * m
                dcs_mat = dcs_mat + jnp.where(lane == hd, jnp.sum(dseg, axis=1, keepdims=True), 0.0)
                dcs_t = jnp.where(row == hd, jnp.sum(dseg, axis=0, keepdims=True), dcs_t)
            dact_ref[:, bsl] = dbg + _bdot_tn(dcb, cg)
            dact_ref[:, csl] = dcg + _bdot(dcb, bg)
            ddte = _head_sums(dw * xdt, ind) * dte
            dcs_mat = dcs_mat + _head_sums(dy * gmat, ind) * e_cs - ddte
            ddt_mat = ddt_mat + _head_sums(dxdt * xg, ind)
            dcsl_row = (dcsl_row + jnp.sum(ddte, axis=0, keepdims=True)
                        + jnp.sum(_head_sums(dh * prev, ind), axis=0, keepdims=True) * ecl)
            dd_row = dd_row + jnp.sum(_head_sums(dy * xg, ind), axis=0, keepdims=True)
        dcs_mat = dcs_mat - dcs_t.T + jnp.where(rowl == CHUNK - 1, dcsl_row, 0.0)
        dda = _hdot(upper, dcs_mat)
        ddt_mat = ddt_mat + dda * a_row
        da_row = jnp.sum(dda * dtp, axis=0, keepdims=True)
        ddt_raw = ddt_mat * _sigmoid(dt_ref[...] + dtb_ref[...])
        ddt_ref[...] = ddt_raw
        dpar_ref[0:1, :] += jnp.sum(ddt_raw, axis=0, keepdims=True)
        dpar_ref[1:2, :] += da_row * a_row
        dpar_ref[2:3, :] += dd_row

    blk = lambda i: N_CHUNKS - 1 - i
    return pl.pallas_call(
        body, name="ssd_bwd", grid=(N_CHUNKS,),
        in_specs=[pl.BlockSpec((CHUNK, D_CONV), lambda i: (blk(i), 0)), pl.BlockSpec((CHUNK, D_SSM), lambda i: (blk(i), zcol)),
                  pl.BlockSpec((CHUNK, LANE), lambda i: (blk(i), dtcol)), pl.BlockSpec((CHUNK, D_SSM), lambda i: (blk(i), 0)),
                  pl.BlockSpec((1, SSM_GROUPS, SSM_STATE, GROUP_W), lambda i: (blk(i), 0, 0, 0)),
                  pl.BlockSpec((CHUNK, D_SSM), lambda i: (blk(i), 0)),
                  _full((1, LANE)), _full((1, LANE)), _full((1, LANE)), _full((1, D_SSM))],
        out_specs=[pl.BlockSpec((CHUNK, D_CONV), lambda i: (blk(i), 0)), pl.BlockSpec((CHUNK, LANE), lambda i: (blk(i), 0)),
                   pl.BlockSpec((CHUNK, D_SSM), lambda i: (blk(i), 0)), _full((1, D_SSM)), _full((8, LANE))],
        out_shape=[jax.ShapeDtypeStruct((SEQ, D_CONV), f32), jax.ShapeDtypeStruct((SEQ, LANE), f32),
                   jax.ShapeDtypeStruct((SEQ, D_SSM), f32), jax.ShapeDtypeStruct((1, D_SSM), f32),
                   jax.ShapeDtypeStruct((8, LANE), f32)],
        scratch_shapes=[pltpu.VMEM((SSM_GROUPS, SSM_STATE, GROUP_W), f32)],
        compiler_params=_params("arbitrary"),
    )(act, proj, proj, ypre, states, d_out, dt_bias, a_log, d_skip, norm_g)


def out_fwd(x, attn, ssm, w_out, tm=512):
    def body(x_ref, a_ref, s_ref, w_ref, o_ref):
        o_ref[...] = x_ref[...] + _bdot(a_ref[...], w_ref[:D_ATTN, :]) + _bdot(s_ref[...], w_ref[D_ATTN:, :])

    tok = lambda w_: pl.BlockSpec((tm, w_), lambda i: (i, 0))
    return pl.pallas_call(
        body, name="out_fwd", grid=(SEQ // tm,),
        in_specs=[tok(D_MODEL), tok(D_ATTN), tok(D_SSM), _full((D_MODEL, D_MODEL))],
        out_specs=tok(D_MODEL), out_shape=jax.ShapeDtypeStruct((SEQ, D_MODEL), f32),
        compiler_params=_params("arbitrary"),
    )(x, attn, ssm, w_out)


def out_bwd(dx1, attn, ssm, w_out, tm=512):
    nt = SEQ // tm

    def body(d_ref, a_ref, s_ref, w_ref, da_ref, ds_ref, dw16_ref, dw_ref):
        i = pl.program_id(0)

        @pl.when(i == 0)
        def _():
            dw_ref[...] = jnp.zeros_like(dw_ref)

        d = d_ref[...].astype(bf16)
        dcat = _bdot_nt(d, w_ref[...])
        da_ref[...] = dcat[:, :D_ATTN]
        ds_ref[...] = dcat[:, D_ATTN:]
        dw_ref[:D_ATTN, :] += _bdot_tn(a_ref[...], d)
        dw_ref[D_ATTN:, :] += _bdot_tn(s_ref[...], d)

        @pl.when(i == nt - 1)
        def _():
            dw16_ref[...] = dw_ref[...].astype(bf16)

    tok = lambda w_: pl.BlockSpec((tm, w_), lambda i: (i, 0))
    return pl.pallas_call(
        body, name="out_bwd", grid=(nt,),
        in_specs=[tok(D_MODEL), tok(D_ATTN), tok(D_SSM), _resident((D_MODEL, D_MODEL))],
        out_specs=[tok(D_ATTN), tok(D_SSM), _resident((D_MODEL, D_MODEL))],
        out_shape=[jax.ShapeDtypeStruct((SEQ, D_ATTN), f32), jax.ShapeDtypeStruct((SEQ, D_SSM), f32),
                   jax.ShapeDtypeStruct((D_MODEL, D_MODEL), bf16)],
        scratch_shapes=[pltpu.VMEM((D_MODEL, D_MODEL), f32)],
        compiler_params=_params("arbitrary"),
    )(dx1, attn, ssm, w_out)


MLP_SUB = 256


def mlp_fwd(x1, g, w_up, w_down, tm=1024):
    def body(x_ref, g_ref, wu_ref, wd_ref, o_ref, u_ref, h_scr):
        j = pl.program_id(1)

        @pl.when(j == 0)
        def _():
            xv = x_ref[...]
            h_scr[...] = (xv * _rms(xv) * g_ref[...]).astype(bf16)
            o_ref[...] = xv

        for r in range(tm // MLP_SUB):
            rows = slice(r * MLP_SUB, (r + 1) * MLP_SUB)
            u = jnp.dot(h_scr[rows, :], wu_ref[...], preferred_element_type=f32)
            u_ref[rows, :] = u
            a = jnp.square(jnp.maximum(u, 0.0))
            o_ref[rows, :] += _bdot(a, wd_ref[...])

    return pl.pallas_call(
        body, name="mlp_fwd", grid=(SEQ // tm, N_CHIPS),
        in_specs=[pl.BlockSpec((tm, D_MODEL), lambda i, j: (i, 0)), _full((1, D_MODEL)),
                  pl.BlockSpec((None, D_MODEL, FF_TILE), lambda i, j: (j, 0, 0)),
                  pl.BlockSpec((None, FF_TILE, D_MODEL), lambda i, j: (j, 0, 0))],
        out_specs=[pl.BlockSpec((tm, D_MODEL), lambda i, j: (i, 0)), pl.BlockSpec((tm, FF_TILE), lambda i, j: (i, j))],
        out_shape=[jax.ShapeDtypeStruct((SEQ, D_MODEL), f32), jax.ShapeDtypeStruct((SEQ, D_FF), f32)],
        scratch_shapes=[pltpu.VMEM((tm, D_MODEL), bf16)],
        compiler_params=_params("arbitrary", "arbitrary"),
    )(x1, g, w_up, w_down)


def mlp_bwd_data(dx2, u, x1, g, w_up, w_down, tm=1024):
    def body(d_ref, u_ref, x_ref, g_ref, wu_ref, wd_ref, dx_ref, du_ref, dg_ref, dh_scr):
        i, j = pl.program_id(0), pl.program_id(1)

        @pl.when(jnp.logical_and(i == 0, j == 0))
        def _():
            dg_ref[...] = jnp.zeros_like(dg_ref)

        @pl.when(j == 0)
        def _():
            dh_scr[...] = jnp.zeros_like(dh_scr)

        for r in range(tm // MLP_SUB):
            rows = slice(r * MLP_SUB, (r + 1) * MLP_SUB)
            da = _bdot_nt(d_ref[rows, :], wd_ref[...])
            du = (da * (2.0 * jnp.maximum(u_ref[rows, :], 0.0))).astype(bf16)
            du_ref[rows, :] = du
            dh_scr[rows, :] += _bdot_nt(du, wu_ref[...])

        @pl.when(j == N_CHIPS - 1)
        def _():
            xv = x_ref[...]
            r = _rms(xv)
            xhat = xv * r
            dh = dh_scr[...]
            dg_ref[...] += jnp.sum(dh * xhat, axis=0, keepdims=True)
            dx_ref[...] = d_ref[...] + _rms_bwd(dh, xhat, r, g_ref[...])

    return pl.pallas_call(
        body, name="mlp_bwd_data", grid=(SEQ // tm, N_CHIPS),
        in_specs=[pl.BlockSpec((tm, D_MODEL), lambda i, j: (i, 0)), pl.BlockSpec((tm, FF_TILE), lambda i, j: (i, j)),
                  pl.BlockSpec((tm, D_MODEL), lambda i, j: (i, 0)), _full((1, D_MODEL)),
                  pl.BlockSpec((None, D_MODEL, FF_TILE), lambda i, j: (j, 0, 0)),
                  pl.BlockSpec((None, FF_TILE, D_MODEL), lambda i, j: (j, 0, 0))],
        out_specs=[pl.BlockSpec((tm, D_MODEL), lambda i, j: (i, 0)), pl.BlockSpec((tm, FF_TILE), lambda i, j: (i, j)),
                   _full((1, D_MODEL))],
        out_shape=[jax.ShapeDtypeStruct((SEQ, D_MODEL), f32), jax.ShapeDtypeStruct((SEQ, D_FF), bf16),
                   jax.ShapeDtypeStruct((1, D_MODEL), f32)],
        scratch_shapes=[pltpu.VMEM((tm, D_MODEL), f32)],
        compiler_params=_params("arbitrary", "arbitrary"),
    )(dx2, u, x1, g, w_up, w_down)


def mlp_bwd_weights(dx2, u, du, x1, g, tm=512):
    nt = SEQ // tm

    def body(d_ref, u_ref, du_ref, x_ref, g_ref, dwu16_ref, dwd16_ref, h_scr, d_scr, dwu_ref, dwd_ref):
        j, i = pl.program_id(0), pl.program_id(1)

        @pl.when(j == 0)
        def _():
            xv = x_ref[...]
            h_scr[i] = (xv * _rms(xv) * g_ref[...]).T.astype(bf16)
            d_scr[i] = d_ref[...].astype(bf16)

        @pl.when(i == 0)
        def _():
            dwu_ref[...] = jnp.zeros_like(dwu_ref)
            dwd_ref[...] = jnp.zeros_like(dwd_ref)

        dwu_ref[...] += jnp.dot(h_scr[i], du_ref[...], preferred_element_type=f32)
        a = jnp.square(jnp.maximum(u_ref[...], 0.0))
        dwd_ref[...] += _bdot_tn(a, d_scr[i])

        @pl.when(i == nt - 1)
        def _():
            dwu16_ref[...] = dwu_ref[...].astype(bf16)
            dwd16_ref[...] = dwd_ref[...].astype(bf16)

    up = pl.BlockSpec((None, D_MODEL, FF_TILE), lambda j, i: (j, 0, 0))
    down = pl.BlockSpec((None, FF_TILE, D_MODEL), lambda j, i: (j, 0, 0))
    first_pass = pl.BlockSpec((tm, D_MODEL), lambda j, i: (jnp.where(j == 0, i, nt - 1), 0))
    return pl.pallas_call(
        body, name="mlp_bwd_weights", grid=(N_CHIPS, nt),
        in_specs=[first_pass, pl.BlockSpec((tm, FF_TILE), lambda j, i: (i, j)),
                  pl.BlockSpec((tm, FF_TILE), lambda j, i: (i, j)), first_pass, _full((1, D_MODEL))],
        out_specs=[up, down],
        out_shape=[jax.ShapeDtypeStruct((N_CHIPS, D_MODEL, FF_TILE), bf16), jax.ShapeDtypeStruct((N_CHIPS, FF_TILE, D_MODEL), bf16)],
        scratch_shapes=[pltpu.VMEM((nt, D_MODEL, tm), bf16), pltpu.VMEM((nt, tm, D_MODEL), bf16),
                        pltpu.VMEM((D_MODEL, FF_TILE), f32), pltpu.VMEM((FF_TILE, D_MODEL), f32)],
        compiler_params=_params("arbitrary", "arbitrary"),
    )(dx2, u, du, x1, g)


def loss_head(y, target, tm=512):
    def body(y_ref, t_ref, dy_ref, l_ref):
        @pl.when(pl.program_id(0) == 0)
        def _():
            l_ref[...] = jnp.zeros_like(l_ref)

        d = y_ref[...] - t_ref[...]
        dy_ref[...] = d * (1.0 / D_MODEL)
        part = jnp.sum(jnp.mean(d * d, axis=-1, keepdims=True), axis=0, keepdims=True)
        l_ref[...] += 0.5 * part

    tok = pl.BlockSpec((tm, D_MODEL), lambda i: (i, 0))
    return pl.pallas_call(
        body, name="loss_head", grid=(SEQ // tm,), in_specs=[tok, tok], out_specs=[tok, _full((1, 1))],
        out_shape=[jax.ShapeDtypeStruct((SEQ, D_MODEL), f32), jax.ShapeDtypeStruct((1, 1), f32)],
        compiler_params=_params("arbitrary"),
    )(y, target)


def _pad_lane(v):
    return jnp.pad(v, (0, LANE - v.shape[0]))[None, :]


def local_step(x, target, w, prov):
    bucket = jnp.asarray(_bucket_table().T)
    bias = bias_build(w["rel_bias"], bucket)
    saved = []
    for l in range(DEPTH):
        g_mix = w["mix_norm_g"][l][None, :] + prov.stage(("begin", l), x)
        w_in = prov.w_in(l, x)
        proj = in_fwd(x, g_mix, w_in)
        conv_b = w["conv_b"][l][None, :]
        act = conv_fwd(proj, w["conv_w"][l], conv_b)
        dtb = _pad_lane(w["dt_bias"][l]) + prov.stage(("mid", l), act)
        alog, dsk = _pad_lane(w["a_log"][l]), _pad_lane(w["d_skip"][l])
        ng = w["ssm_norm_g"][l][None, :]
        ssm, ypre, states = ssd_fwd_g(act, proj, dtb, alog, dsk, ng)
        qg, kg = w["q_gain"][l][:, None] + 0.0 * ssm[:1, :1], w["k_gain"][l][None, :]
        attn = attn_fwd_t(proj, qg, kg, w["sinks"][l], bias)
        tok = prov.stage(("pre_out", l), attn)
        w_out = prov.w_out(l, attn) + jnp.asarray(tok, bf16)
        x1 = out_fwd(x, attn, ssm, w_out)
        g_mlp = w["mlp_norm_g"][l][None, :] + prov.stage(("pre_mlp", l), x1)
        w_up, w_down = prov.mlp(l, x1)
        x2, u = mlp_fwd(x1, g_mlp, w_up, w_down)
        saved.append(dict(x=x, proj=proj, attn=attn, act=act, ssm=ssm, ypre=ypre, states=states, x1=x1, u=u,
                          g_mix=g_mix, qg=qg, kg=kg, conv_b=conv_b, dtb=dtb, alog=alog, dsk=dsk, ng=ng, g_mlp=g_mlp,
                          w_in=w_in, w_out=w_out, w_up=w_up, w_down=w_down))
        x = x2
    dx, loss = loss_head(x, target)
    grads = [None] * DEPTH
    dbands = [None] * DEPTH
    tok = 0.0
    for l in reversed(range(DEPTH)):
        s = saved[l]
        g_mlp = s["g_mlp"] + tok
        dx1, du, dg_mlp = mlp_bwd_data(dx, s["u"], s["x1"], g_mlp, s["w_up"], s["w_down"])
        dw_up, dw_down = mlp_bwd_weights(dx, s["u"], du, s["x1"], g_mlp)
        tok = prov.grads(("mlp", l), dict(w_up=dw_up, w_down=dw_down), dx1)
        dattn, dssm, dw_out = out_bwd(dx1, s["attn"], s["ssm"], s["w_out"])
        dact, ddt, dz, dng, dpar = ssd_bwd_g(s["act"], s["proj"], s["ypre"], s["states"], dssm, s["dtb"] + tok, s["alog"],
                                           s["dsk"], s["ng"])
        conv_b = s["conv_b"] + prov.stage(("bwd_mid", l), dact)
        dxbc, dconv_w, dconv_b = conv_bwd(s["proj"], dact, w["conv_w"][l], conv_b)
        dq, dk, dv, dband, dsink, dqg, dkg = attn_bwd_t(s["proj"], dattn, s["qg"], s["kg"], w["sinks"][l], bias)
        dbands[l] = dband
        g_mix = s["g_mix"]
        if l == 0:
            d_rel = bias_bwd(dbands[0], dbands[1], bucket)
            g_mix = g_mix + 0.0 * d_rel[:1, :1]
        dx, dw_in, dg_mix = in_bwd(dq, dz, dxbc, dk, dv, ddt, s["x"], g_mix, s["w_in"], dx1)
        tok = prov.grads(("mix", l), dict(w_in=split_w_in_grad(dw_in), w_out=dw_out), dx)
        grads[l] = dict(mix_norm_g=dg_mix[0], q_gain=dqg[:, 0], k_gain=dkg[0], sinks=dsink[:, 0],
                        conv_w=dconv_w, conv_b=dconv_b[0], dt_bias=dpar[0, :SSM_HEADS], a_log=dpar[1, :SSM_HEADS],
                        d_skip=dpar[2, :SSM_HEADS], ssm_norm_g=dng[0], mlp_norm_g=dg_mlp[0])
    out = {k: jnp.stack([grads[l][k] for l in range(DEPTH)]) for k in grads[0]}
    out["rel_bias"] = d_rel[:, :N_Q_HEADS]
    return loss, dx, out, tok


MESH = pl.DeviceIdType.MESH
HBM = pl.BlockSpec(memory_space=pltpu.HBM)
N_DEVICES = 8


def _coords():
    return lax.axis_index("x"), lax.axis_index("y"), lax.axis_index("c")


def _peer_chips(x, y):
    return [(1 - x, y), (x, 1 - y), (1 - x, 1 - y)]


def _remote(src, dst, send_sem, recv_sem, device):
    return pltpu.make_async_remote_copy(src_ref=src, dst_ref=dst, send_sem=send_sem, recv_sem=recv_sem,
                                        device_id=device, device_id_type=MESH)


SEM = pl.BlockSpec(memory_space=pltpu.SEMAPHORE)
ANY = pl.BlockSpec(memory_space=pl.ANY)
DATAFLOW = pltpu.SideEffectType.DATAFLOW_SIDE_EFFECTING


def _gather_copies(kind, src_refs, land_refs, ssem, rsem):
    x, y, c = _coords()
    k_me = 2 * x + y
    n = len(land_refs)
    cps = []
    for p, land in enumerate(land_refs):
        hr = land.shape[1] // 2
        rows = pl.ds(c * hr, hr)
        for j, chip in enumerate(_peer_chips(x, y)):
            i = 3 * p + j
            if kind == "ici":
                cps.append(_remote(src_refs[p].at[rows, :], land.at[k_me, rows, :], ssem.at[i], rsem.at[i], (*chip, c)))
            else:
                got = land.at[2 * chip[0] + chip[1], rows, :]
                cps.append(_remote(got, got, ssem.at[i], rsem.at[i], (x, y, 1 - c)))
        if kind == "relay":
            cps.append(_remote(src_refs[p], land.at[k_me], ssem.at[3 * n + p], rsem.at[3 * n + p], (x, y, 1 - c)))
    return cps


def gather_now(srcs, conv):
    n = len(srcs)

    def body(*refs):
        src_refs, conv_ref = refs[:n], refs[n]
        lands, gconv = refs[n + 1:2 * n + 1], refs[2 * n + 1]
        ssem, rsem, fsem, frsem, csem, crsem = refs[2 * n + 2:]
        x, y, c = _coords()
        k_me = 2 * x + y
        targets = [(*chip, c) for chip in _peer_chips(x, y)] + [(x, y, 1 - c)]
        ici = _gather_copies("ici", src_refs, lands, ssem, rsem)
        relay = _gather_copies("relay", src_refs, lands, fsem, frsem)
        passed = [cp for i, cp in enumerate(relay) if i % 4 != 3]
        own = relay[3::4]
        conv_cps = [_remote(conv_ref, gconv.at[k_me], csem.at[j], crsem.at[j], t) for j, t in enumerate(targets)]
        for cp in ici + conv_cps + own:
            cp.start()
        for cp, fw in zip(ici, passed):
            cp.wait_recv()
            fw.start()
        for cp in conv_cps + relay:
            cp.wait_recv()
        for cp in ici + relay + conv_cps:
            cp.wait_send()

    out_shape = [jax.ShapeDtypeStruct((N_CHIPS,) + s.shape, s.dtype) for s in srcs]
    out_shape.append(jax.ShapeDtypeStruct((N_CHIPS,) + conv.shape, conv.dtype))
    sems = lambda k: pltpu.SemaphoreType.DMA((k,))
    return pl.pallas_call(
        body, name="gather_now", out_shape=out_shape, in_specs=[HBM] * (n + 1), out_specs=[HBM] * (n + 1),
        scratch_shapes=[sems(3 * n), sems(3 * n), sems(4 * n), sems(4 * n), sems(N_CHIPS), sems(N_CHIPS)],
    )(*srcs, conv)


def _gather_maker(kind, n_src):
    def make(refs, ssem, rsem):
        cps = _gather_copies(kind, refs[:n_src], refs[n_src:], ssem, rsem)
        return cps, cps
    return make


def _scatter_maker(n):
    def make(refs, ssem, rsem):
        x, y, c = _coords()
        k_me = 2 * x + y
        sends, arrivals = [], []
        for p in range(n):
            src, land = refs[p], refs[n + p]
            sends.append(_remote(src.at[k_me, 1 - c], land.at[0], ssem.at[7 * p], rsem.at[7 * p], (x, y, 1 - c)))
            for j, chip in enumerate(_peer_chips(x, y)):
                for cc in range(2):
                    sends.append(_remote(src.at[2 * chip[0] + chip[1], cc], land.at[1 + 2 * j + c],
                                         ssem.at[7 * p + 1 + 2 * j + cc], rsem.at[7 * p + 1 + 2 * j + c], (*chip, cc)))
            for s in range(7):
                arrivals.append(_remote(land.at[s], land.at[s], ssem.at[7 * p + s], rsem.at[7 * p + s], (x, y, 1 - c)))
        return sends, arrivals
    return make


def _share_maker(n):
    def make(refs, ssem, rsem):
        x, y, c = _coords()
        sends = [_remote(refs[p].at[c], refs[p].at[c], ssem.at[p], rsem.at[p], (x, y, 1 - c)) for p in range(n)]
        arrivals = [_remote(refs[p].at[1 - c], refs[p].at[1 - c], ssem.at[p], rsem.at[p], (x, y, 1 - c)) for p in range(n)]
        return sends, arrivals
    return make


def split_start(name, make, n_sems, operands, after):
    n = len(operands)

    def body(*refs):
        ssem, rsem, token = refs[n + 1], refs[n + 2], refs[-1]
        for cp in make(refs[:n], ssem, rsem)[0]:
            cp.start()
        token[...] = jnp.zeros_like(token)

    ops = [pltpu.with_memory_space_constraint(a, pltpu.HBM) for a in operands]
    outs = pl.pallas_call(
        body, name=name,
        out_shape=(pltpu.SemaphoreType.DMA((n_sems,)), pltpu.SemaphoreType.DMA((n_sems,)),
                   *[pltpu.HBM(a.shape, a.dtype) for a in ops], jax.ShapeDtypeStruct((8, LANE), f32)),
        in_specs=[HBM] * n + [ANY], out_specs=(SEM, SEM, *[HBM] * n, pl.BlockSpec(memory_space=pltpu.VMEM)),
        input_output_aliases={i: 2 + i for i in range(n)},
        compiler_params=pltpu.CompilerParams(has_side_effects=DATAFLOW),
    )(*ops, after)
    return dict(name=name, make=make, ssem=outs[0], rsem=outs[1], operands=outs[2:2 + n], token=outs[-1][0, 0])


def split_wait(handle, after):
    n = len(handle["operands"])

    def body(*refs):
        sends, arrivals = handle["make"](refs[:n], refs[n], refs[n + 1])
        for cp in sends:
            cp.wait_send()
        for cp in arrivals:
            cp.wait_recv()

    outs = pl.pallas_call(
        body, name=handle["name"].replace("start", "wait"),
        out_shape=tuple(pltpu.HBM(a.shape, a.dtype) for a in handle["operands"]),
        in_specs=[HBM] * n + [SEM, SEM, ANY], out_specs=tuple([HBM] * n),
        input_output_aliases={i: i for i in range(n)},
        compiler_params=pltpu.CompilerParams(has_side_effects=DATAFLOW),
    )(*handle["operands"], handle["ssem"], handle["rsem"], after)
    return list(outs)


def piece_sums(gs, recvs, kc_arr):
    n, steps = len(gs), 2

    def body(kc_ref, *refs):
        for p in range(n):
            g_ref, r_ref, o_ref = refs[p], refs[n + p], refs[2 * n + p]
            acc = g_ref[...].astype(f32)
            for s in range(7):
                acc = acc + r_ref[s].astype(f32)
            o_ref[...] = acc

    tiles = [(g.shape[2] // steps, g.shape[3]) for g in gs]
    return pl.pallas_call(
        body, name="piece_sums",
        grid_spec=pltpu.PrefetchScalarGridSpec(
            num_scalar_prefetch=1, grid=(steps,),
            in_specs=[pl.BlockSpec((None, None, tr, cc), lambda r, kc: (kc[0], kc[1], r, 0)) for tr, cc in tiles]
                     + [pl.BlockSpec((7, tr, cc), lambda r, kc: (0, r, 0)) for tr, cc in tiles],
            out_specs=[pl.BlockSpec((None, tr, cc), lambda r, kc: (kc[1], r, 0)) for tr, cc in tiles]),
        out_shape=[jax.ShapeDtypeStruct((2,) + g.shape[2:], f32) for g in gs],
        compiler_params=_params("arbitrary"),
    )(kc_arr, *gs, *recvs)


def small_all_reduce(vec):
    def body(v_ref, o_ref, gat, ssem, rsem):
        x, y, c = _coords()
        me = 4 * x + 2 * y + c
        gat[me] = v_ref[...]
        sends = []
        for t in range(1, N_DEVICES):
            peer = (x ^ (t >> 2), y ^ ((t >> 1) & 1), c ^ (t & 1))
            cp = _remote(v_ref, gat.at[me], ssem.at[t - 1], rsem.at[t - 1], peer)
            cp.start()
            sends.append(cp)
        for t in range(1, N_DEVICES):
            peer = (x ^ (t >> 2), y ^ ((t >> 1) & 1), c ^ (t & 1))
            slot = gat.at[4 * peer[0] + 2 * peer[1] + peer[2]]
            _remote(slot, slot, ssem.at[t - 1], rsem.at[t - 1], peer).wait_recv()
        for cp in sends:
            cp.wait_send()
        acc = gat[0]
        for d in range(1, N_DEVICES):
            acc = acc + gat[d]
        o_ref[...] = acc

    return pl.pallas_call(
        body, name="small_all_reduce", out_shape=jax.ShapeDtypeStruct(vec.shape, vec.dtype),
        in_specs=[pl.BlockSpec(memory_space=pltpu.VMEM)], out_specs=pl.BlockSpec(memory_space=pltpu.VMEM),
        scratch_shapes=[pltpu.VMEM((N_DEVICES,) + vec.shape, vec.dtype), pltpu.SemaphoreType.DMA((N_DEVICES - 1,)),
                        pltpu.SemaphoreType.DMA((N_DEVICES - 1,))],
    )(vec)


def _adamw_math(w, g, m, v):
    m_new = ADAM_B1 * m + (1.0 - ADAM_B1) * g
    v_new = ADAM_B2 * v + (1.0 - ADAM_B2) * jnp.square(g)
    m_hat = m_new / (1.0 - ADAM_B1 ** ADAM_STEP)
    v_hat = v_new / (1.0 - ADAM_B2 ** ADAM_STEP)
    delta = -ADAM_LR * (m_hat / (jnp.sqrt(v_hat) + ADAM_EPS) + ADAM_WD * w)
    return delta, m_new, v_new


def adamw_shard(w, g0, g1, m, v):
    depth, rows, cols = w.shape
    half = rows // 2
    tr = min(256, half)
    nr = half // tr

    def body(w_ref, g0_ref, g1_ref, m_ref, v_ref, go_ref, d_ref, nm_ref, nv_ref):
        gv = jnp.where(pl.program_id(0) == 0, g0_ref[...], g1_ref[...])
        go_ref[...] = gv
        d_ref[...], nm_ref[...], nv_ref[...] = _adamw_math(w_ref[...], gv, m_ref[...], v_ref[...])

    spec = pl.BlockSpec((None, tr, cols), lambda l, h, r: (l, h * nr + r, 0))
    g0spec = pl.BlockSpec((None, tr, cols), lambda l, h, r: (jnp.where(l == 0, h, 1), jnp.where(l == 0, r, nr - 1), 0))
    g1spec = pl.BlockSpec((None, tr, cols), lambda l, h, r: (jnp.where(l == 1, h, 0), jnp.where(l == 1, r, 0), 0))
    return pl.pallas_call(
        body, name="adamw_shard", grid=(depth, 2, nr), in_specs=[spec, g0spec, g1spec, spec, spec], out_specs=[spec] * 4,
        out_shape=[jax.ShapeDtypeStruct(w.shape, f32)] * 4,
        compiler_params=_params("arbitrary", "arbitrary", "arbitrary"),
    )(w, g0, g1, m, v)


def adamw_cols(w, g, m, v, tc=34):
    cols, depth, rows = w.shape

    def body(w_ref, g_ref, m_ref, v_ref, d_ref, nm_ref, nv_ref):
        d_ref[...], nm_ref[...], nv_ref[...] = _adamw_math(w_ref[...], g_ref[...], m_ref[...], v_ref[...])

    spec = pl.BlockSpec((tc, depth, rows), lambda i: (i, 0, 0))
    return pl.pallas_call(
        body, name="adamw_cols", grid=(cols // tc,), in_specs=[spec] * 4, out_specs=[spec] * 3,
        out_shape=[jax.ShapeDtypeStruct(w.shape, f32)] * 3,
        compiler_params=_params("arbitrary"),
    )(w, g, m, v)


def adamw_small(ws, gs, ms, vs):
    n = len(ws)

    def body(*refs):
        ins, outs = refs[:4 * n], refs[4 * n:]
        for i in range(n):
            w_ref, g_ref, m_ref, v_ref = (ins[k * n + i] for k in range(4))
            outs[i][...], outs[n + i][...], outs[2 * n + i][...] = _adamw_math(w_ref[...], g_ref[...], m_ref[...], v_ref[...])

    outs = pl.pallas_call(
        body, name="adamw_small", out_shape=[jax.ShapeDtypeStruct(w.shape, f32) for w in ws] * 3,
    )(*ws, *gs, *ms, *vs)
    return outs[:n], outs[n:2 * n], outs[2 * n:]


WEIGHTS = ("mix_norm_g", "w_in", "q_gain", "k_gain", "sinks", "rel_bias", "conv_w", "conv_b", "dt_bias", "a_log", "d_skip",
           "ssm_norm_g", "w_out", "mlp_norm_g", "w_up", "w_down")
BIG = ("w_in", "w_out", "w_up", "w_down")
SMALL = tuple(n for n in WEIGHTS if n not in BIG)
PACK_COLS = 1024
PACK_ROWS = 16


def _pack(named, last=None):
    flat = jnp.concatenate([named[n].reshape(-1) for n in SMALL])
    tail = jnp.zeros((1,), f32) if last is None else last.reshape(1)
    pad = jnp.zeros((PACK_ROWS * PACK_COLS - flat.shape[0] - 1,), f32)
    return jnp.concatenate([flat, pad, tail]).reshape(PACK_ROWS, PACK_COLS)


def _unpack(buf, shapes):
    flat = buf.reshape(-1)
    out, at = {}, 0
    for n in SMALL:
        size = int(np.prod(shapes[n]))
        out[n] = flat[at:at + size].reshape(shapes[n])
        at += size
    return out


class _Exchange:
    GROUPS = {"A": (("w_up", 0), ("w_down", 0)), "B": (("w_in", 1), ("w_out", 1)), "C": (("w_up", 1), ("w_down", 1))}
    ICI_AT = {("mid", 0): "B", ("pre_out", 0): "C"}
    RELAY_AT = {("pre_out", 0): "A", ("pre_mlp", 0): "B", ("mid", 1): "C"}
    LAST = ("mix", 0)
    IN_FLIGHT = 2

    def __init__(self, wts, kc_arr):
        self.wts, self.kc_arr = wts, kc_arr
        self.own = {(n, l): wts[n][l].astype(bf16) for n in BIG for l in range(DEPTH)}
        now = gather_now([self.own["w_in", 0], self.own["w_out", 0]], wts["conv_w"])
        self.ready = {("w_in", 0): now[0], ("w_out", 0): now[1]}
        self.conv_w = jnp.transpose(now[2], (1, 2, 0, 3)).reshape(DEPTH, CONV_WIDTH, D_CONV)
        self.ici, self.relay = {}, {}
        self.scatter, self.share, self.reduced = [], [], {}
        self._start_ici("A", now[2])

    def _start_ici(self, g, after):
        srcs = [self.own[p] for p in self.GROUPS[g]]
        lands = [lax.empty((N_CHIPS,) + s.shape, s.dtype) for s in srcs]
        self.ici[g] = split_start("gather%s_ici_start" % g, _gather_maker("ici", len(srcs)), 3 * len(srcs), srcs + lands,
                                  after)
        return self.ici[g]["token"]

    def stage(self, name, after):
        if name == ("begin", 0):
            return self.ici["A"]["token"]
        tok = 0.0
        g = self.RELAY_AT.get(name)
        if g is not None:
            n = len(self.GROUPS[g])
            self.relay[g] = split_start("gather%s_relay_start" % g, _gather_maker("relay", n), 4 * n,
                                        split_wait(self.ici[g], after), after)
            tok = self.relay[g]["token"]
        if name in self.ICI_AT:
            tok = tok + self._start_ici(self.ICI_AT[name], after)
        return tok

    def _get(self, piece, after):
        if piece not in self.ready:
            g = [k for k, pieces in self.GROUPS.items() if piece in pieces][0]
            lands = split_wait(self.relay[g], after)[len(self.GROUPS[g]):]
            self.ready.update(zip(self.GROUPS[g], lands))
        return self.ready[piece]

    def w_in(self, l, after):
        return align_w_in(self._get(("w_in", l), after))

    def w_out(self, l, after):
        return self._get(("w_out", l), after).reshape(D_MODEL, D_MODEL)

    def mlp(self, l, after):
        return self._get(("w_up", l), after), self._get(("w_down", l), after)

    def _view(self, n, g):
        _, rows, cols = self.wts[n].shape
        return g.reshape(N_CHIPS, 2, rows // 2, cols)

    def grads(self, name, arrays, after):
        if name == self.LAST:
            self.held = (name, arrays)
            return 0.0
        return self._scatter(name, arrays, after) + self._advance(after, self.IN_FLIGHT)

    def flush(self, after):
        return self._scatter(*self.held, after) + self._advance(after, self.IN_FLIGHT)

    def _scatter(self, name, arrays, after):
        pieces = [(n, name[1]) for n in arrays]
        views = [self._view(n, g) for n, g in arrays.items()]
        lands = [lax.empty((7,) + v.shape[2:], bf16) for v in views]
        h = split_start("scatter_%s%d_start" % name, _scatter_maker(len(views)), 7 * len(views), views + lands, after)
        self.scatter.append((pieces, h))
        return h["token"]

    def _take_share(self, after):
        pieces, h = self.share.pop(0)
        self.reduced.update(zip(pieces, split_wait(h, after)))

    def _take_scatter(self, after):
        pieces, h = self.scatter.pop(0)
        done = split_wait(h, after)
        views, lands = done[:len(pieces)], done[len(pieces):]
        sums = list(piece_sums(views, lands, self.kc_arr))
        hs = split_start(h["name"].replace("scatter", "share"), _share_maker(len(sums)), len(sums), sums, after)
        self.share.append((pieces, hs))
        return hs["token"]

    def _advance(self, after, newest):
        if self.share:
            self._take_share(after)
        return self._take_scatter(after) if len(self.scatter) > newest else 0.0

    def reduced_grads(self, names, after):
        want = [(n, l) for n in names for l in range(DEPTH)]
        while not all(p in self.reduced for p in want):
            if any(p in pieces for p in want for pieces, _ in self.share):
                self._take_share(after)
            else:
                self._take_scatter(after)
        return {n: [self.reduced[n, l] for l in range(DEPTH)] for n in names}


def kernel(x, mix_norm_g, w_in, q_gain, k_gain, sinks, rel_bias, conv_w, conv_b, dt_bias, a_log, d_skip, ssm_norm_g, w_out, mlp_norm_g, w_up, w_down, loss_target, m_mix_norm_g, m_w_in, m_q_gain, m_k_gain, m_sinks, m_rel_bias, m_conv_w, m_conv_b, m_dt_bias, m_a_log, m_d_skip, m_ssm_norm_g, m_w_out, m_mlp_norm_g, m_w_up, m_w_down, v_mix_norm_g, v_w_in, v_q_gain, v_k_gain, v_sinks, v_rel_bias, v_conv_w, v_conv_b, v_dt_bias, v_a_log, v_d_skip, v_ssm_norm_g, v_w_out, v_mlp_norm_g, v_w_up, v_w_down):
    wts = dict(mix_norm_g=mix_norm_g, w_in=w_in, q_gain=q_gain, k_gain=k_gain, sinks=sinks, rel_bias=rel_bias, conv_w=conv_w,
               conv_b=conv_b, dt_bias=dt_bias, a_log=a_log, d_skip=d_skip, ssm_norm_g=ssm_norm_g, w_out=w_out,
               mlp_norm_g=mlp_norm_g, w_up=w_up, w_down=w_down)
    mom = dict(mix_norm_g=m_mix_norm_g, w_in=m_w_in, q_gain=m_q_gain, k_gain=m_k_gain, sinks=m_sinks, rel_bias=m_rel_bias,
               conv_w=m_conv_w, conv_b=m_conv_b, dt_bias=m_dt_bias, a_log=m_a_log, d_skip=m_d_skip, ssm_norm_g=m_ssm_norm_g,
               w_out=m_w_out, mlp_norm_g=m_mlp_norm_g, w_up=m_w_up, w_down=m_w_down)
    var = dict(mix_norm_g=v_mix_norm_g, w_in=v_w_in, q_gain=v_q_gain, k_gain=v_k_gain, sinks=v_sinks, rel_bias=v_rel_bias,
               conv_w=v_conv_w, conv_b=v_conv_b, dt_bias=v_dt_bias, a_log=v_a_log, d_skip=v_d_skip, ssm_norm_g=v_ssm_norm_g,
               w_out=v_w_out, mlp_norm_g=v_mlp_norm_g, w_up=v_w_up, w_down=v_w_down)
    xi, yi, ci = _coords()
    k_me = 2 * xi + yi
    kc_arr = jnp.stack([k_me, ci]).astype(jnp.int32)

    prov = _Exchange(wts, kc_arr)
    small_w = {n: wts[n] for n in SMALL}
    small_w["conv_w"] = prov.conv_w
    loss, dx, grads, tok = local_step(x[0], loss_target[0], small_w, prov)

    small_shapes = {n: grads[n].shape for n in SMALL}
    small_sum = small_all_reduce(_pack(grads, loss) + tok)
    loss = small_sum[PACK_ROWS - 1, PACK_COLS - 1]
    tok = prov.flush(small_sum)
    small = _unpack(small_sum, small_shapes)
    cols = conv_w.shape[-1]
    small["conv_w"] = lax.dynamic_slice_in_dim(small["conv_w"], k_me * cols, cols, axis=2)
    g_out_d = dict(small)
    gs = [small[n] for n in SMALL]
    gs[0] = gs[0] + tok
    ds, nms, nvs = adamw_small([wts[n] for n in SMALL], gs, [mom[n] for n in SMALL], [var[n] for n in SMALL])
    d_out_d, m_out_d, v_out_d = dict(zip(SMALL, ds)), dict(zip(SMALL, nms)), dict(zip(SMALL, nvs))

    after = ds[0]
    for names in (("w_up", "w_down"), ("w_in", "w_out")):
        for n, (g0, g1) in prov.reduced_grads(names, after).items():
            if n == "w_in":
                rows, cols = wts[n].shape[1:]
                to_cols = lambda a: jnp.transpose(a, (2, 0, 1))
                g_t = jnp.stack([to_cols(g).reshape(cols, rows) for g in (g0, g1)], axis=1)
                res_t = adamw_cols(to_cols(wts[n]), g_t, to_cols(mom[n]), to_cols(var[n]))
                g_out_d[n], d_out_d[n], m_out_d[n], v_out_d[n] = (jnp.transpose(a, (1, 2, 0)) for a in (g_t, *res_t))
            else:
                g_out_d[n], d_out_d[n], m_out_d[n], v_out_d[n] = adamw_shard(wts[n], g0, g1, mom[n], var[n])
            after = d_out_d[n]

    return (loss, dx[None], *[g_out_d[n] for n in WEIGHTS], *[d_out_d[n] for n in WEIGHTS],
            *[m_out_d[n] for n in WEIGHTS], *[v_out_d[n] for n in WEIGHTS])
```

```python
import numpy as np
import jax
import jax.numpy as jnp
from jax import lax
from jax.experimental import pallas as pl
from jax.experimental.pallas import tpu as pltpu

f32 = jnp.float32
bf16 = jnp.bfloat16

SEQ = 2048
D_MODEL = 1024
DEPTH = 2
HEAD_DIM = 64
N_Q_HEADS = 8
N_KV_HEADS = 2
Q_PER_KV = N_Q_HEADS // N_KV_HEADS
BLOCK = 128
N_BLOCKS = SEQ // BLOCK
N_BUCKETS = 32
MAX_DISTANCE = 128
SSM_HEADS = 8
SSM_HEAD_DIM = 64
SSM_GROUPS = 2
HEADS_PER_GROUP = SSM_HEADS // SSM_GROUPS
SSM_STATE = 128
CONV_WIDTH = 4
CHUNK = 128
N_CHUNKS = SEQ // CHUNK
D_FF = 4 * D_MODEL
D_ATTN = N_Q_HEADS * HEAD_DIM
D_KV = N_KV_HEADS * HEAD_DIM
D_SSM = SSM_HEADS * SSM_HEAD_DIM
D_BC = SSM_GROUPS * SSM_STATE
D_CONV = D_SSM + 2 * D_BC
D_IN = D_ATTN + 2 * D_KV + D_SSM + D_CONV + SSM_HEADS
EPS = 1e-6
NEG = -1e30
N_CHIPS = 4
FF_TILE = D_FF // N_CHIPS

LANE = 128
PW = D_ATTN + D_SSM + D_CONV + 2 * D_KV + LANE
OFF_Q, OFF_Z, OFF_X, OFF_K, OFF_V, OFF_DT = 0, 512, 1024, 2048, 2176, 2304

ADAM_LR = 0.001
ADAM_B1 = 0.9
ADAM_B2 = 0.999
ADAM_EPS = 1e-08
ADAM_WD = 0.01
ADAM_STEP = 10

VMEM_LIMIT = 56 * 1024 * 1024


def _params(*sem):
    return pltpu.CompilerParams(dimension_semantics=tuple(sem), vmem_limit_bytes=VMEM_LIMIT)


def _bdot(a, b):
    return jnp.dot(a.astype(bf16), b.astype(bf16), preferred_element_type=f32)


def _bdot_nt(a, b):
    return lax.dot_general(a.astype(bf16), b.astype(bf16), (((1,), (1,)), ((), ())), preferred_element_type=f32)


def _bdot_tn(a, b):
    return lax.dot_general(a.astype(bf16), b.astype(bf16), (((0,), (0,)), ((), ())), preferred_element_type=f32)


def _hdot(a, b):
    return jnp.dot(a, b, precision=lax.Precision.HIGHEST, preferred_element_type=f32)


def _sigmoid(x):
    return 1.0 / (1.0 + jnp.exp(-x))


def _softplus(x):
    return jnp.maximum(x, 0.0) + jnp.log1p(jnp.exp(-jnp.abs(x)))


def _rms(x):
    return lax.rsqrt(jnp.mean(x * x, axis=-1, keepdims=True) + EPS)


def _rms_bwd(dy, xhat, r, g):
    t = dy * g
    return r * (t - xhat * jnp.mean(t * xhat, axis=-1, keepdims=True))


def _full(shape):
    return pl.BlockSpec(shape, lambda *_: (0,) * len(shape))


def _bucket_table():
    qi = np.arange(BLOCK)[:, None]
    kj = np.arange(2 * BLOCK)[None, :]
    dist = qi + BLOCK - kj
    ok = (dist >= 0) & (dist < 128)
    d = np.clip(dist, 0, None)
    max_exact = N_BUCKETS // 2
    d_f = np.maximum(d, 1).astype(np.float32)
    large = max_exact + (np.log(d_f / np.float32(max_exact)) / np.float32(np.log(MAX_DISTANCE / max_exact))
                         * np.float32(N_BUCKETS - max_exact)).astype(np.int32)
    large = np.minimum(large, N_BUCKETS - 1)
    bucket = np.where(d < max_exact, d, large)
    return np.where(ok, bucket, -1).astype(np.int32)


def bias_build(rel_bias, bucket):
    def body(rel_ref, bkt_ref, o_ref):
        bkt = bkt_ref[...]
        for h in range(N_Q_HEADS):
            acc = jnp.where(bkt < 0, NEG, 0.0).astype(f32)
            for b in range(N_BUCKETS):
                acc = acc + jnp.where(bkt == b, rel_ref[b, h], 0.0)
            o_ref[h] = acc

    return pl.pallas_call(
        body, name="bias_build", out_shape=jax.ShapeDtypeStruct((N_Q_HEADS,) + bucket.shape, f32),
        in_specs=[pl.BlockSpec(memory_space=pltpu.SMEM), pl.BlockSpec(memory_space=pltpu.VMEM)],
        out_specs=pl.BlockSpec(memory_space=pltpu.VMEM),
    )(rel_bias, bucket)


def bias_bwd(dband0, dband1, bucket):
    def body(d0_ref, d1_ref, bkt_ref, o_ref):
        bkt = bkt_ref[...]
        o_ref[...] = jnp.zeros_like(o_ref)
        for h in range(N_Q_HEADS):
            d = d0_ref[h] + d1_ref[h]
            for b in range(N_BUCKETS):
                part = jnp.sum(jnp.where(bkt == b, d, 0.0), axis=1, keepdims=True)
                o_ref[b:b + 1, h:h + 1] = jnp.sum(part, axis=0, keepdims=True)

    return pl.pallas_call(
        body, name="bias_bwd", out_shape=jax.ShapeDtypeStruct((N_BUCKETS, LANE), f32),
    )(dband0, dband1, bucket)


W_IN_SHARD = D_IN // N_CHIPS
_ALIGNED_PIECES = ((0, 0, 512), (1, 190, 578), (2, 0, 124), (2, 124, 578), (3, 0, 570), (0, 512, 578), (1, 0, 62),
                   (1, 62, 190), (3, 570, 578))
_SHARD_PIECES = (((0, 512), (2048, 2114)), ((2114, 2176), (2176, 2304), (512, 900)), ((900, 1024), (1024, 1478)),
                 ((1478, 2048), (2304, 2312)))


def align_w_in(shards, tr=256):
    def body(s_ref, o_ref):
        parts = [s_ref[k, :, a:b] for k, a, b in _ALIGNED_PIECES]
        parts.append(jnp.zeros((tr, LANE - SSM_HEADS), s_ref.dtype))
        o_ref[...] = jnp.concatenate(parts, axis=-1)

    return pl.pallas_call(
        body, name="align_w_in", grid=(D_MODEL // tr,),
        in_specs=[pl.BlockSpec((N_CHIPS, tr, W_IN_SHARD), lambda i: (0, i, 0))],
        out_specs=pl.BlockSpec((tr, PW), lambda i: (i, 0)),
        out_shape=jax.ShapeDtypeStruct((D_MODEL, PW), shards.dtype),
        compiler_params=_params("arbitrary"),
    )(shards)


def split_w_in_grad(dw, tr=256):
    def body(d_ref, o16_ref):
        for k, pieces in enumerate(_SHARD_PIECES):
            o16_ref[k] = jnp.concatenate([d_ref[:, a:b] for a, b in pieces], axis=-1).astype(bf16)

    return pl.pallas_call(
        body, name="split_w_in_grad", grid=(D_MODEL // tr,),
        in_specs=[pl.BlockSpec((tr, PW), lambda i: (i, 0))],
        out_specs=pl.BlockSpec((N_CHIPS, tr, W_IN_SHARD), lambda i: (0, i, 0)),
        out_shape=jax.ShapeDtypeStruct((N_CHIPS, D_MODEL, W_IN_SHARD), bf16),
        compiler_params=_params("arbitrary"),
    )(dw)

def in_fwd(x, g, w, tm=512):
    def body(x_ref, g_ref, w_ref, o_ref):
        xv = x_ref[...]
        h = xv * _rms(xv) * g_ref[...]
        o_ref[...] = _bdot(h, w_ref[...])

    return pl.pallas_call(
        body, name="in_fwd", grid=(SEQ // tm,),
        in_specs=[pl.BlockSpec((tm, D_MODEL), lambda i: (i, 0)), _full((1, D_MODEL)), _resident((D_MODEL, PW))],
        out_specs=pl.BlockSpec((tm, PW), lambda i: (i, 0)),
        out_shape=jax.ShapeDtypeStruct((SEQ, PW), f32),
        compiler_params=_params("arbitrary"),
    )(x, g, w)


def _resident(shape):
    return pl.BlockSpec(shape, lambda *_: (0,) * len(shape), pipeline_mode=pl.Buffered(1))


def in_bwd(dq, dz, dxbc, dk, dv, ddt, x, g, w, dres, tm=512):
    def body(dq_ref, dz_ref, dx_ref, dk_ref, dv_ref, ddt_ref, x_ref, g_ref, w_ref, dres_ref, o_ref, dw_ref, dg_ref):
        i = pl.program_id(0)

        @pl.when(i == 0)
        def _():
            dw_ref[...] = jnp.zeros_like(dw_ref)
            dg_ref[...] = jnp.zeros_like(dg_ref)

        dproj = jnp.concatenate([dq_ref[...], dz_ref[...], dx_ref[...], dk_ref[...], dv_ref[...], ddt_ref[...]],
                                axis=-1).astype(bf16)
        xv = x_ref[...]
        r = _rms(xv)
        xhat = xv * r
        gv = g_ref[...]
        h = xhat * gv
        dw_ref[...] += _bdot_tn(h, dproj)
        dh = _bdot_nt(dproj, w_ref[...])
        dg_ref[...] += jnp.sum(dh * xhat, axis=0, keepdims=True)
        o_ref[...] = dres_ref[...] + _rms_bwd(dh, xhat, r, gv)

    tok = lambda w_: pl.BlockSpec((tm, w_), lambda i: (i, 0))
    return pl.pallas_call(
        body, name="in_bwd", grid=(SEQ // tm,),
        in_specs=[tok(D_ATTN), tok(D_SSM), tok(D_CONV), tok(D_KV), tok(D_KV), tok(LANE), tok(D_MODEL),
                  _full((1, D_MODEL)), _resident((D_MODEL, PW)), tok(D_MODEL)],
        out_specs=[tok(D_MODEL), _resident((D_MODEL, PW)), _full((1, D_MODEL))],
        out_shape=[jax.ShapeDtypeStruct((SEQ, D_MODEL), f32), jax.ShapeDtypeStruct((D_MODEL, PW), f32),
                   jax.ShapeDtypeStruct((1, D_MODEL), f32)],
        compiler_params=_params("arbitrary"),
    )(dq, dz, dxbc, dk, dv, ddt, x, g, w, dres)


def _attn_softmax_t(qk, bias_t, sink, first, key_row):
    s = qk * (HEAD_DIM ** -0.5) + bias_t
    s = jnp.where(jnp.logical_and(first, key_row < BLOCK), NEG, s)
    m = jnp.maximum(jnp.max(s, axis=0, keepdims=True), sink)
    p = jnp.exp(s - m)
    psink = jnp.exp(sink - m)
    inv = 1.0 / (jnp.sum(p, axis=0, keepdims=True) + psink)
    return p * inv, psink * inv


def _rms_t(x_t):
    return lax.rsqrt(jnp.mean(x_t * x_t, axis=0, keepdims=True) + EPS)


def attn_fwd_t(proj, q_gain_col, k_gain, sinks, bias_t):
    kcol, vcol = OFF_K // D_KV, OFF_V // D_KV

    def body(q_ref, kc_ref, kp_ref, vc_ref, vp_ref, qg_ref, kg_ref, sink_ref, bias_ref, o_ref, ot_scr):
        n = pl.program_id(0)
        first = n == 0
        key_row = lax.broadcasted_iota(jnp.int32, (2 * BLOCK, BLOCK), 0)
        k2 = jnp.concatenate([kp_ref[...], kc_ref[...]], axis=0)
        v_t = jnp.concatenate([vp_ref[...], vc_ref[...]], axis=0).T
        q_t = q_ref[...].T
        qg = jnp.broadcast_to(qg_ref[...], (HEAD_DIM, BLOCK))
        kg = kg_ref[...]
        for hk in range(N_KV_HEADS):
            sl = slice(hk * HEAD_DIM, (hk + 1) * HEAD_DIM)
            kk = k2[:, sl]
            kn = (kk * _rms(kk) * kg).astype(bf16)
            vt = v_t[sl, :].astype(bf16)
            heads = range(hk * Q_PER_KV, (hk + 1) * Q_PER_KV)
            qns = []
            for h in heads:
                qh = q_t[h * HEAD_DIM:(h + 1) * HEAD_DIM, :]
                qns.append(qh * _rms_t(qh) * qg)
            scores = [_bdot(kn, qn) for qn in qns]
            for h, s in zip(heads, scores):
                p, _ = _attn_softmax_t(s, bias_ref[h], sink_ref[h], first, key_row)
                ot_scr[h * HEAD_DIM:(h + 1) * HEAD_DIM, :] = _bdot(vt, p)
        o_ref[...] = ot_scr[...].T

    prev = lambda n: jnp.maximum(n - 1, 0)
    return pl.pallas_call(
        body, name="attn_fwd", grid=(N_BLOCKS,),
        in_specs=[pl.BlockSpec((BLOCK, D_ATTN), lambda n: (n, 0)),
                  pl.BlockSpec((BLOCK, D_KV), lambda n: (n, kcol)), pl.BlockSpec((BLOCK, D_KV), lambda n: (prev(n), kcol)),
                  pl.BlockSpec((BLOCK, D_KV), lambda n: (n, vcol)), pl.BlockSpec((BLOCK, D_KV), lambda n: (prev(n), vcol)),
                  _full((HEAD_DIM, 1)), _full((1, HEAD_DIM)), pl.BlockSpec(memory_space=pltpu.SMEM),
                  _full((N_Q_HEADS, 2 * BLOCK, BLOCK))],
        out_specs=pl.BlockSpec((BLOCK, D_ATTN), lambda n: (n, 0)),
        out_shape=jax.ShapeDtypeStruct((SEQ, D_ATTN), f32),
        scratch_shapes=[pltpu.VMEM((D_ATTN, BLOCK), f32)],
        compiler_params=_params("arbitrary"),
    )(proj, proj, proj, proj, proj, q_gain_col, k_gain, sinks, bias_t)


def attn_bwd_t(proj, d_out, q_gain_col, k_gain, sinks, bias_t):
    kcol, vcol = OFF_K // D_KV, OFF_V // D_KV

    def body(q_ref, kc_ref, kp_ref, vc_ref, vp_ref, do_ref, qg_ref, kg_ref, sink_ref, bias_ref,
             dq_ref, dk_ref, dv_ref, dband_ref, dsink_ref, dqg_ref, dkg_ref, dkn_scr, dv_scr, dqt_scr, dsink_acc, dqg_acc):
        i = pl.program_id(0)
        first = i == N_BLOCKS - 1

        @pl.when(i == 0)
        def _():
            for ref in (dband_ref, dkg_ref, dkn_scr, dv_scr, dsink_acc, dqg_acc):
                ref[...] = jnp.zeros_like(ref)

        key_row = lax.broadcasted_iota(jnp.int32, (2 * BLOCK, BLOCK), 0)
        k2 = jnp.concatenate([kp_ref[...], kc_ref[...]], axis=0)
        v2 = jnp.concatenate([vp_ref[...], vc_ref[...]], axis=0)
        q_t = q_ref[...].T
        do_t = do_ref[...].T
        qg = jnp.broadcast_to(qg_ref[...], (HEAD_DIM, BLOCK))
        kg = kg_ref[...]
        scale = HEAD_DIM ** -0.5
        for hk in range(N_KV_HEADS):
            sl = slice(hk * HEAD_DIM, (hk + 1) * HEAD_DIM)
            kk = k2[:, sl]
            rk = _rms(kk)
            khat = kk * rk
            kn = (khat * kg).astype(bf16)
            vb = v2[:, sl].astype(bf16)
            dkn = jnp.zeros((2 * BLOCK, HEAD_DIM), f32)
            dvv = jnp.zeros((2 * BLOCK, HEAD_DIM), f32)
            heads = range(hk * Q_PER_KV, (hk + 1) * Q_PER_KV)
            rqs, qhats, qns, d_os = [], [], [], []
            for h in heads:
                hs = slice(h * HEAD_DIM, (h + 1) * HEAD_DIM)
                qh = q_t[hs, :]
                rqs.append(_rms_t(qh))
                qhats.append(qh * rqs[-1])
                qns.append((qhats[-1] * qg).astype(bf16))
                d_os.append(do_t[hs, :].astype(bf16))
            scores = [_bdot(kn, qn) for qn in qns]
            dps = [_bdot(vb, d_o) for d_o in d_os]
            ps, dss = [], []
            for h, s, dp in zip(heads, scores, dps):
                p, psink = _attn_softmax_t(s, bias_ref[h], sink_ref[h], first, key_row)
                delta = jnp.sum(p * dp, axis=0, keepdims=True)
                ds = p * (dp - delta)
                dband_ref[h] += ds
                dsink_acc[h:h + 1, :] += -(psink * delta)
                ps.append(p.astype(bf16))
                dss.append(ds.astype(bf16))
            dqns = [_bdot_tn(kn, ds) * scale for ds in dss]
            for ds, qn, p, d_o in zip(dss, qns, ps, d_os):
                dkn = dkn + _bdot_nt(ds, qn) * scale
                dvv = dvv + _bdot_nt(p, d_o)
            for h, dqn, rq, qhat in zip(heads, dqns, rqs, qhats):
                dqg_acc[...] += dqn * qhat
                t = dqn * qg
                dqt_scr[h * HEAD_DIM:(h + 1) * HEAD_DIM, :] = rq * (t - qhat * jnp.mean(t * qhat, axis=0, keepdims=True))
            dkn_cur = dkn[BLOCK:] + dkn_scr[:, sl]
            dkn_scr[:, sl] = dkn[:BLOCK]
            khat_c, rk_c = khat[BLOCK:], rk[BLOCK:]
            dkg_ref[...] += jnp.sum(dkn_cur * khat_c, axis=0, keepdims=True)
            dk_ref[:, sl] = _rms_bwd(dkn_cur, khat_c, rk_c, kg)
            dv_ref[:, sl] = dvv[BLOCK:] + dv_scr[:, sl]
            dv_scr[:, sl] = dvv[:BLOCK]
        dq_ref[...] = dqt_scr[...].T

        @pl.when(i == N_BLOCKS - 1)
        def _():
            dsink_ref[...] = jnp.sum(dsink_acc[...], axis=1, keepdims=True)
            dqg_ref[...] = jnp.sum(dqg_acc[...], axis=1, keepdims=True)

    blk = lambda i: N_BLOCKS - 1 - i
    prev = lambda i: jnp.maximum(N_BLOCKS - 2 - i, 0)
    return pl.pallas_call(
        body, name="attn_bwd", grid=(N_BLOCKS,),
        in_specs=[pl.BlockSpec((BLOCK, D_ATTN), lambda i: (blk(i), 0)),
                  pl.BlockSpec((BLOCK, D_KV), lambda i: (blk(i), kcol)), pl.BlockSpec((BLOCK, D_KV), lambda i: (prev(i), kcol)),
                  pl.BlockSpec((BLOCK, D_KV), lambda i: (blk(i), vcol)), pl.BlockSpec((BLOCK, D_KV), lambda i: (prev(i), vcol)),
                  pl.BlockSpec((BLOCK, D_ATTN), lambda i: (blk(i), 0)),
                  _full((HEAD_DIM, 1)), _full((1, HEAD_DIM)), pl.BlockSpec(memory_space=pltpu.SMEM),
                  _full((N_Q_HEADS, 2 * BLOCK, BLOCK))],
        out_specs=[pl.BlockSpec((BLOCK, D_ATTN), lambda i: (blk(i), 0)), pl.BlockSpec((BLOCK, D_KV), lambda i: (blk(i), 0)),
                   pl.BlockSpec((BLOCK, D_KV), lambda i: (blk(i), 0)), _full((N_Q_HEADS, 2 * BLOCK, BLOCK)),
                   _full((N_Q_HEADS, 1)), _full((HEAD_DIM, 1)), _full((1, HEAD_DIM))],
        out_shape=[jax.ShapeDtypeStruct((SEQ, D_ATTN), f32), jax.ShapeDtypeStruct((SEQ, D_KV), f32),
                   jax.ShapeDtypeStruct((SEQ, D_KV), f32), jax.ShapeDtypeStruct((N_Q_HEADS, 2 * BLOCK, BLOCK), f32),
                   jax.ShapeDtypeStruct((N_Q_HEADS, 1), f32), jax.ShapeDtypeStruct((HEAD_DIM, 1), f32),
                   jax.ShapeDtypeStruct((1, HEAD_DIM), f32)],
        scratch_shapes=[pltpu.VMEM((BLOCK, D_KV), f32), pltpu.VMEM((BLOCK, D_KV), f32), pltpu.VMEM((D_ATTN, BLOCK), f32),
                        pltpu.VMEM((N_Q_HEADS, BLOCK), f32), pltpu.VMEM((HEAD_DIM, BLOCK), f32)],
        compiler_params=_params("arbitrary"),
    )(proj, proj, proj, proj, proj, d_out, q_gain_col, k_gain, sinks, bias_t)


SUBLANES = 8


def _shift_down(u, s, row8):
    if s == 0:
        return u
    r = pltpu.roll(u, s, 0)
    return jnp.concatenate([jnp.where(row8 >= s, r[:SUBLANES], 0.0), r[SUBLANES:]], axis=0)


def _shift_up(u, s, row8):
    if s == 0:
        return u
    r = pltpu.roll(u, SEQ - s, 0)
    return jnp.concatenate([r[:-SUBLANES], jnp.where(row8 < SUBLANES - s, r[-SUBLANES:], 0.0)], axis=0)


def conv_fwd(proj, conv_w, conv_b):
    xcol = OFF_X // LANE

    def body(u_ref, w_ref, b_ref, o_ref):
        u = u_ref[...]
        row = lax.broadcasted_iota(jnp.int32, (SUBLANES, LANE), 0)
        pre = b_ref[...] + jnp.zeros_like(u)
        for k in range(CONV_WIDTH):
            pre = pre + w_ref[k:k + 1, :] * _shift_down(u, CONV_WIDTH - 1 - k, row)
        o_ref[...] = pre * _sigmoid(pre)

    return pl.pallas_call(
        body, name="conv_fwd", grid=(D_CONV // LANE,),
        in_specs=[pl.BlockSpec((SEQ, LANE), lambda j: (0, xcol + j)), pl.BlockSpec((CONV_WIDTH, LANE), lambda j: (0, j)),
                  pl.BlockSpec((1, LANE), lambda j: (0, j))],
        out_specs=pl.BlockSpec((SEQ, LANE), lambda j: (0, j)),
        out_shape=jax.ShapeDtypeStruct((SEQ, D_CONV), f32),
        compiler_params=_params("arbitrary"),
    )(proj, conv_w, conv_b)


def conv_bwd(proj, d_act, conv_w, conv_b):
    xcol = OFF_X // LANE

    def body(u_ref, da_ref, w_ref, b_ref, du_ref, dw_ref, db_ref):
        u = u_ref[...]
        row = lax.broadcasted_iota(jnp.int32, (SUBLANES, LANE), 0)
        shifted = [_shift_down(u, CONV_WIDTH - 1 - k, row) for k in range(CONV_WIDTH)]
        pre = b_ref[...] + jnp.zeros_like(u)
        for k in range(CONV_WIDTH):
            pre = pre + w_ref[k:k + 1, :] * shifted[k]
        sg = _sigmoid(pre)
        dpre = da_ref[...] * (sg * (1.0 + pre * (1.0 - sg)))
        db_ref[...] = jnp.sum(dpre, axis=0, keepdims=True)
        du = jnp.zeros_like(u)
        for k in range(CONV_WIDTH):
            dw_ref[k:k + 1, :] = jnp.sum(dpre * shifted[k], axis=0, keepdims=True)
            du = du + w_ref[k:k + 1, :] * _shift_up(dpre, CONV_WIDTH - 1 - k, row)
        du_ref[...] = du

    return pl.pallas_call(
        body, name="conv_bwd", grid=(D_CONV // LANE,),
        in_specs=[pl.BlockSpec((SEQ, LANE), lambda j: (0, xcol + j)), pl.BlockSpec((SEQ, LANE), lambda j: (0, j)),
                  pl.BlockSpec((CONV_WIDTH, LANE), lambda j: (0, j)), pl.BlockSpec((1, LANE), lambda j: (0, j))],
        out_specs=[pl.BlockSpec((SEQ, LANE), lambda j: (0, j)), pl.BlockSpec((CONV_WIDTH, LANE), lambda j: (0, j)),
                   pl.BlockSpec((1, LANE), lambda j: (0, j))],
        out_shape=[jax.ShapeDtypeStruct((SEQ, D_CONV), f32), jax.ShapeDtypeStruct((CONV_WIDTH, D_CONV), f32),
                   jax.ShapeDtypeStruct((1, D_CONV), f32)],
        compiler_params=_params("arbitrary"),
    )(proj, d_act, conv_w, conv_b)


def _ssd_chunk_common(dt_raw, dtb, alog):
    row = lax.broadcasted_iota(jnp.int32, (CHUNK, CHUNK), 0)
    col = lax.broadcasted_iota(jnp.int32, (CHUNK, CHUNK), 1)
    tri = (row >= col).astype(f32)
    strict = (row > col).astype(f32)
    dtp = _softplus(dt_raw + dtb)
    a_row = -jnp.exp(alog)
    d_a = dtp * a_row
    cs = _hdot(tri, d_a)
    cs_last = cs[CHUNK - 1:CHUNK, :]
    return row, col, dtp, a_row, cs, cs.T, cs_last


def _seg_decay(cs, cs_t, hd, row, col):
    seg = cs[:, hd:hd + 1] - cs_t[hd:hd + 1, :]
    return jnp.where(row >= col, jnp.exp(seg), 0.0)


GROUP_W = HEADS_PER_GROUP * SSM_HEAD_DIM


def _group_indicator(g):
    j = lax.broadcasted_iota(jnp.int32, (GROUP_W, LANE), 0)
    lane = lax.broadcasted_iota(jnp.int32, (GROUP_W, LANE), 1)
    return (lane == g * HEADS_PER_GROUP + j // SSM_HEAD_DIM).astype(bf16)


def _bf16_pieces(a, n):
    pieces = []
    for _ in range(n):
        p = a.astype(bf16)
        pieces.append(p)
        a = a - p.astype(f32)
    return pieces


def _head_spread(a, ind):
    return sum(lax.dot_general(p, ind, (((1,), (1,)), ((), ())), preferred_element_type=f32) for p in _bf16_pieces(a, 3))


def _head_sums(a, ind):
    return sum(jnp.dot(p, ind, preferred_element_type=f32) for p in _bf16_pieces(a, 2))


def ssd_fwd_g(act, proj, dt_bias, a_log, d_skip, norm_g):
    zcol, dtcol = OFF_Z // D_SSM, OFF_DT // LANE

    def body(act_ref, z_ref, dt_ref, dtb_ref, alog_ref, dsk_ref, ng_ref, out_ref, ypre_ref, st_ref, state):
        c = pl.program_id(0)

        @pl.when(c == 0)
        def _():
            state[...] = jnp.zeros_like(state)

        row, col, dtp, a_row, cs, cs_t, cs_last = _ssd_chunk_common(dt_ref[...], dtb_ref[...], alog_ref[...])
        e_cs = jnp.exp(cs)
        dte = jnp.exp(cs_last - cs)
        rows8 = jnp.concatenate([jnp.exp(cs_last), dsk_ref[...], jnp.zeros((6, LANE), f32)], axis=0)
        z = z_ref[...]
        sz = z * _sigmoid(z)
        ng = ng_ref[...]
        for g in range(SSM_GROUPS):
            gs = slice(g * GROUP_W, (g + 1) * GROUP_W)
            ind = _group_indicator(g)
            xg = act_ref[:, gs]
            bg = act_ref[:, D_SSM + g * SSM_STATE:D_SSM + (g + 1) * SSM_STATE]
            cg = act_ref[:, D_SSM + D_BC + g * SSM_STATE:D_SSM + D_BC + (g + 1) * SSM_STATE]
            dt_e, e_e, dte_e = _head_spread(dtp, ind), _head_spread(e_cs, ind), _head_spread(dte, ind)
            rows_e = _head_spread(rows8, ind)
            ecl_e, dsk_e = rows_e[0:1], rows_e[1:2]
            xdt = xg * dt_e
            prev = state[g]
            st_ref[0, g] = prev
            cb = _bdot_nt(cg, bg)
            goff = _bdot(cg, prev)
            snew = _bdot_tn(bg, xdt * dte_e)
            heads = range(g * HEADS_PER_GROUP, (g + 1) * HEADS_PER_GROUP)
            ms = [cb * _seg_decay(cs, cs_t, hd, row, col) for hd in heads]
            yd = [_bdot(m, xdt[:, r * SSM_HEAD_DIM:(r + 1) * SSM_HEAD_DIM]) for r, m in enumerate(ms)]
            y = jnp.concatenate(yd, axis=1) + e_e * goff + xg * dsk_e
            state[g] = prev * ecl_e + snew
            ypre_ref[:, gs] = y
            part = y * sz[:, gs]
            out_ref[:, gs] = part * _rms(part) * ng[:, gs]

    return pl.pallas_call(
        body, name="ssd_fwd", grid=(N_CHUNKS,),
        in_specs=[pl.BlockSpec((CHUNK, D_CONV), lambda c: (c, 0)), pl.BlockSpec((CHUNK, D_SSM), lambda c: (c, zcol)),
                  pl.BlockSpec((CHUNK, LANE), lambda c: (c, dtcol)), _full((1, LANE)), _full((1, LANE)), _full((1, LANE)),
                  _full((1, D_SSM))],
        out_specs=[pl.BlockSpec((CHUNK, D_SSM), lambda c: (c, 0)), pl.BlockSpec((CHUNK, D_SSM), lambda c: (c, 0)),
                   pl.BlockSpec((1, SSM_GROUPS, SSM_STATE, GROUP_W), lambda c: (c, 0, 0, 0))],
        out_shape=[jax.ShapeDtypeStruct((SEQ, D_SSM), f32), jax.ShapeDtypeStruct((SEQ, D_SSM), f32),
                   jax.ShapeDtypeStruct((N_CHUNKS, SSM_GROUPS, SSM_STATE, GROUP_W), f32)],
        scratch_shapes=[pltpu.VMEM((SSM_GROUPS, SSM_STATE, GROUP_W), f32)],
        compiler_params=_params("arbitrary"),
    )(act, proj, proj, dt_bias, a_log, d_skip, norm_g)


def ssd_bwd_g(act, proj, ypre, states, d_out, dt_bias, a_log, d_skip, norm_g):
    zcol, dtcol = OFF_Z // D_SSM, OFF_DT // LANE

    def body(act_ref, z_ref, dt_ref, ypre_ref, st_ref, do_ref, dtb_ref, alog_ref, dsk_ref, ng_ref,
             dact_ref, ddt_ref, dz_ref, dng_ref, dpar_ref, dstate):
        i = pl.program_id(0)

        @pl.when(i == 0)
        def _():
            for ref in (dng_ref, dpar_ref, dstate):
                ref[...] = jnp.zeros_like(ref)

        row, col, dtp, a_row, cs, cs_t, cs_last = _ssd_chunk_common(dt_ref[...], dtb_ref[...], alog_ref[...])
        upper = (row <= col).astype(f32)
        lane = lax.broadcasted_iota(jnp.int32, (CHUNK, LANE), 1)
        rowl = lax.broadcasted_iota(jnp.int32, (CHUNK, LANE), 0)
        e_cs = jnp.exp(cs)
        dte = jnp.exp(cs_last - cs)
        ecl = jnp.exp(cs_last)
        rows8 = jnp.concatenate([ecl, dsk_ref[...], jnp.zeros((6, LANE), f32)], axis=0)
        z = z_ref[...]
        sgz = _sigmoid(z)
        sz = z * sgz
        ng = ng_ref[...]
        ddt_mat = jnp.zeros((CHUNK, LANE), f32)
        dcs_mat = jnp.zeros((CHUNK, LANE), f32)
        dcs_t = jnp.zeros((LANE, CHUNK), f32)
        dcsl_row = jnp.zeros((1, LANE), f32)
        dd_row = jnp.zeros((1, LANE), f32)
        for g in range(SSM_GROUPS):
            gs = slice(g * GROUP_W, (g + 1) * GROUP_W)
            bsl = slice(D_SSM + g * SSM_STATE, D_SSM + (g + 1) * SSM_STATE)
            csl = slice(D_SSM + D_BC + g * SSM_STATE, D_SSM + D_BC + (g + 1) * SSM_STATE)
            ind = _group_indicator(g)
            y = ypre_ref[:, gs]
            part = y * sz[:, gs]
            r = _rms(part)
            yhat = part * r
            d_o = do_ref[:, gs]
            dng_ref[:, gs] += jnp.sum(d_o * yhat, axis=0, keepdims=True)
            dyz = _rms_bwd(d_o, yhat, r, ng[:, gs])
            dy = dyz * sz[:, gs]
            dz_ref[:, gs] = dyz * y * (sgz[:, gs] * (1.0 + z[:, gs] * (1.0 - sgz[:, gs])))

            xg = act_ref[:, gs]
            bg = act_ref[:, bsl]
            cg = act_ref[:, csl]
            dt_e, e_e, dte_e = _head_spread(dtp, ind), _head_spread(e_cs, ind), _head_spread(dte, ind)
            rows_e = _head_spread(rows8, ind)
            ecl_e, dsk_e = rows_e[0:1], rows_e[1:2]
            xdt = xg * dt_e
            prev = st_ref[0, g]
            dh = dstate[g]
            heads = range(g * HEADS_PER_GROUP, (g + 1) * HEADS_PER_GROUP)
            hsl = [slice(r_ * SSM_HEAD_DIM, (r_ + 1) * SSM_HEAD_DIM) for r_ in range(HEADS_PER_GROUP)]
            cb = _bdot_nt(cg, bg)
            lms = [_seg_decay(cs, cs_t, hd, row, col) for hd in heads]
            ms = [cb * lm for lm in lms]
            gmat = _bdot(cg, prev)
            dgm = dy * e_e
            dcg = _bdot_nt(dgm, prev)
            dprev = _bdot_tn(cg, dgm)
            dbg = _bdot_nt(xdt * dte_e, dh)
            dw = _bdot(bg, dh)
            dms = [_bdot_nt(dy[:, s_], xdt[:, s_]) for s_ in hsl]
            dxdts = [_bdot_tn(m, dy[:, s_]) for m, s_ in zip(ms, hsl)]
            dxdt = jnp.concatenate(dxdts, axis=1) + dw * dte_e
            dact_ref[:, gs] = dy * dsk_e + dxdt * dt_e
            dstate[g] = dprev + dh * ecl_e
            dcb = jnp.zeros((CHUNK, CHUNK), f32)
            for hd, dm, lm, m in zip(heads, dms, lms, ms):
                dcb = dcb + dm * lm
                dseg = dm * m
                dcs_mat = dcs_mat + jnp.where(lane == hd, jnp.sum(dseg, axis=1, keepdims=True), 0.0)
                dcs_t = jnp.where(row == hd, jnp.sum(dseg, axis=0, keepdims=True), dcs_t)
            dact_ref[:, bsl] = dbg + _bdot_tn(dcb, cg)
            dact_ref[:, csl] = dcg + _bdot(dcb, bg)
            ddte = _head_sums(dw * xdt, ind) * dte
            dcs_mat = dcs_mat + _head_sums(dy * gmat, ind) * e_cs - ddte
            ddt_mat = ddt_mat + _head_sums(dxdt * xg, ind)
            dcsl_row = (dcsl_row + jnp.sum(ddte, axis=0, keepdims=True)
                        + jnp.sum(_head_sums(dh * prev, ind), axis=0, keepdims=True) * ecl)
            dd_row = dd_row + jnp.sum(_head_sums(dy * xg, ind), axis=0, keepdims=True)
        dcs_mat = dcs_mat - dcs_t.T + jnp.where(rowl == CHUNK - 1, dcsl_row, 0.0)
        dda = _hdot(upper, dcs_mat)
        ddt_mat = ddt_mat + dda * a_row
        da_row = jnp.sum(dda * dtp, axis=0, keepdims=True)
        ddt_raw = ddt_mat * _sigmoid(dt_ref[...] + dtb_ref[...])
        ddt_ref[...] = ddt_raw
        dpar_ref[0:1, :] += jnp.sum(ddt_raw, axis=0, keepdims=True)
        dpar_ref[1:2, :] += da_row * a_row
        dpar_ref[2:3, :] += dd_row

    blk = lambda i: N_CHUNKS - 1 - i
    return pl.pallas_call(
        body, name="ssd_bwd", grid=(N_CHUNKS,),
        in_specs=[pl.BlockSpec((CHUNK, D_CONV), lambda i: (blk(i), 0)), pl.BlockSpec((CHUNK, D_SSM), lambda i: (blk(i), zcol)),
                  pl.BlockSpec((CHUNK, LANE), lambda i: (blk(i), dtcol)), pl.BlockSpec((CHUNK, D_SSM), lambda i: (blk(i), 0)),
                  pl.BlockSpec((1, SSM_GROUPS, SSM_STATE, GROUP_W), lambda i: (blk(i), 0, 0, 0)),
                  pl.BlockSpec((CHUNK, D_SSM), lambda i: (blk(i), 0)),
                  _full((1, LANE)), _full((1, LANE)), _full((1, LANE)), _full((1, D_SSM))],
        out_specs=[pl.BlockSpec((CHUNK, D_CONV), lambda i: (blk(i), 0)), pl.BlockSpec((CHUNK, LANE), lambda i: (blk(i), 0)),
                   pl.BlockSpec((CHUNK, D_SSM), lambda i: (blk(i), 0)), _full((1, D_SSM)), _full((8, LANE))],
        out_shape=[jax.ShapeDtypeStruct((SEQ, D_CONV), f32), jax.ShapeDtypeStruct((SEQ, LANE), f32),
                   jax.ShapeDtypeStruct((SEQ, D_SSM), f32), jax.ShapeDtypeStruct((1, D_SSM), f32),
                   jax.ShapeDtypeStruct((8, LANE), f32)],
        scratch_shapes=[pltpu.VMEM((SSM_GROUPS, SSM_STATE, GROUP_W), f32)],
        compiler_params=_params("arbitrary"),
    )(act, proj, proj, ypre, states, d_out, dt_bias, a_log, d_skip, norm_g)


def out_fwd(x, attn, ssm, w_out, tm=512):
    def body(x_ref, a_ref, s_ref, w_ref, o_ref):
        o_ref[...] = x_ref[...] + _bdot(a_ref[...], w_ref[:D_ATTN, :]) + _bdot(s_ref[...], w_ref[D_ATTN:, :])

    tok = lambda w_: pl.BlockSpec((tm, w_), lambda i: (i, 0))
    return pl.pallas_call(
        body, name="out_fwd", grid=(SEQ // tm,),
        in_specs=[tok(D_MODEL), tok(D_ATTN), tok(D_SSM), _full((D_MODEL, D_MODEL))],
        out_specs=tok(D_MODEL), out_shape=jax.ShapeDtypeStruct((SEQ, D_MODEL), f32),
        compiler_params=_params("arbitrary"),
    )(x, attn, ssm, w_out)


def out_bwd(dx1, attn, ssm, w_out, tm=512):
    nt = SEQ // tm

    def body(d_ref, a_ref, s_ref, w_ref, da_ref, ds_ref, dw16_ref, dw_ref):
        i = pl.program_id(0)

        @pl.when(i == 0)
        def _():
            dw_ref[...] = jnp.zeros_like(dw_ref)

        d = d_ref[...].astype(bf16)
        dcat = _bdot_nt(d, w_ref[...])
        da_ref[...] = dcat[:, :D_ATTN]
        ds_ref[...] = dcat[:, D_ATTN:]
        dw_ref[:D_ATTN, :] += _bdot_tn(a_ref[...], d)
        dw_ref[D_ATTN:, :] += _bdot_tn(s_ref[...], d)

        @pl.when(i == nt - 1)
        def _():
            dw16_ref[...] = dw_ref[...].astype(bf16)

    tok = lambda w_: pl.BlockSpec((tm, w_), lambda i: (i, 0))
    return pl.pallas_call(
        body, name="out_bwd", grid=(nt,),
        in_specs=[tok(D_MODEL), tok(D_ATTN), tok(D_SSM), _resident((D_MODEL, D_MODEL))],
        out_specs=[tok(D_ATTN), tok(D_SSM), _resident((D_MODEL, D_MODEL))],
        out_shape=[jax.ShapeDtypeStruct((SEQ, D_ATTN), f32), jax.ShapeDtypeStruct((SEQ, D_SSM), f32),
                   jax.ShapeDtypeStruct((D_MODEL, D_MODEL), bf16)],
        scratch_shapes=[pltpu.VMEM((D_MODEL, D_MODEL), f32)],
        compiler_params=_params("arbitrary"),
    )(dx1, attn, ssm, w_out)


MLP_SUB = 256


def mlp_fwd(x1, g, w_up, w_down, tm=1024):
    def body(x_ref, g_ref, wu_ref, wd_ref, o_ref, u_ref, h_scr):
        j = pl.program_id(1)

        @pl.when(j == 0)
        def _():
            xv = x_ref[...]
            h_scr[...] = (xv * _rms(xv) * g_ref[...]).astype(bf16)
            o_ref[...] = xv

        for r in range(tm // MLP_SUB):
            rows = slice(r * MLP_SUB, (r + 1) * MLP_SUB)
            u = jnp.dot(h_scr[rows, :], wu_ref[...], preferred_element_type=f32)
            u_ref[rows, :] = u
            a = jnp.square(jnp.maximum(u, 0.0))
            o_ref[rows, :] += _bdot(a, wd_ref[...])

    return pl.pallas_call(
        body, name="mlp_fwd", grid=(SEQ // tm, N_CHIPS),
        in_specs=[pl.BlockSpec((tm, D_MODEL), lambda i, j: (i, 0)), _full((1, D_MODEL)),
                  pl.BlockSpec((None, D_MODEL, FF_TILE), lambda i, j: (j, 0, 0)),
                  pl.BlockSpec((None, FF_TILE, D_MODEL), lambda i, j: (j, 0, 0))],
        out_specs=[pl.BlockSpec((tm, D_MODEL), lambda i, j: (i, 0)), pl.BlockSpec((tm, FF_TILE), lambda i, j: (i, j))],
        out_shape=[jax.ShapeDtypeStruct((SEQ, D_MODEL), f32), jax.ShapeDtypeStruct((SEQ, D_FF), f32)],
        scratch_shapes=[pltpu.VMEM((tm, D_MODEL), bf16)],
        compiler_params=_params("arbitrary", "arbitrary"),
    )(x1, g, w_up, w_down)


def mlp_bwd_data(dx2, u, x1, g, w_up, w_down, tm=1024):
    def body(d_ref, u_ref, x_ref, g_ref, wu_ref, wd_ref, dx_ref, du_ref, dg_ref, dh_scr):
        i, j = pl.program_id(0), pl.program_id(1)

        @pl.when(jnp.logical_and(i == 0, j == 0))
        def _():
            dg_ref[...] = jnp.zeros_like(dg_ref)

        @pl.when(j == 0)
        def _():
            dh_scr[...] = jnp.zeros_like(dh_scr)

        for r in range(tm // MLP_SUB):
            rows = slice(r * MLP_SUB, (r + 1) * MLP_SUB)
            da = _bdot_nt(d_ref[rows, :], wd_ref[...])
            du = (da * (2.0 * jnp.maximum(u_ref[rows, :], 0.0))).astype(bf16)
            du_ref[rows, :] = du
            dh_scr[rows, :] += _bdot_nt(du, wu_ref[...])

        @pl.when(j == N_CHIPS - 1)
        def _():
            xv = x_ref[...]
            r = _rms(xv)
            xhat = xv * r
            dh = dh_scr[...]
            dg_ref[...] += jnp.sum(dh * xhat, axis=0, keepdims=True)
            dx_ref[...] = d_ref[...] + _rms_bwd(dh, xhat, r, g_ref[...])

    return pl.pallas_call(
        body, name="mlp_bwd_data", grid=(SEQ // tm, N_CHIPS),
        in_specs=[pl.BlockSpec((tm, D_MODEL), lambda i, j: (i, 0)), pl.BlockSpec((tm, FF_TILE), lambda i, j: (i, j)),
                  pl.BlockSpec((tm, D_MODEL), lambda i, j: (i, 0)), _full((1, D_MODEL)),
                  pl.BlockSpec((None, D_MODEL, FF_TILE), lambda i, j: (j, 0, 0)),
                  pl.BlockSpec((None, FF_TILE, D_MODEL), lambda i, j: (j, 0, 0))],
        out_specs=[pl.BlockSpec((tm, D_MODEL), lambda i, j: (i, 0)), pl.BlockSpec((tm, FF_TILE), lambda i, j: (i, j)),
                   _full((1, D_MODEL))],
        out_shape=[jax.ShapeDtypeStruct((SEQ, D_MODEL), f32), jax.ShapeDtypeStruct((SEQ, D_FF), bf16),
                   jax.ShapeDtypeStruct((1, D_MODEL), f32)],
        scratch_shapes=[pltpu.VMEM((tm, D_MODEL), f32)],
        compiler_params=_params("arbitrary", "arbitrary"),
    )(dx2, u, x1, g, w_up, w_down)


def mlp_bwd_weights(dx2, u, du, x1, g, tm=512):
    nt = SEQ // tm

    def body(d_ref, u_ref, du_ref, x_ref, g_ref, dwu16_ref, dwd16_ref, h_scr, d_scr, dwu_ref, dwd_ref):
        j, i = pl.program_id(0), pl.program_id(1)

        @pl.when(j == 0)
        def _():
            xv = x_ref[...]
            h_scr[i] = (xv * _rms(xv) * g_ref[...]).T.astype(bf16)
            d_scr[i] = d_ref[...].astype(bf16)

        @pl.when(i == 0)
        def _():
            dwu_ref[...] = jnp.zeros_like(dwu_ref)
            dwd_ref[...] = jnp.zeros_like(dwd_ref)

        dwu_ref[...] += jnp.dot(h_scr[i], du_ref[...], preferred_element_type=f32)
        a = jnp.square(jnp.maximum(u_ref[...], 0.0))
        dwd_ref[...] += _bdot_tn(a, d_scr[i])

        @pl.when(i == nt - 1)
        def _():
            dwu16_ref[...] = dwu_ref[...].astype(bf16)
            dwd16_ref[...] = dwd_ref[...].astype(bf16)

    up = pl.BlockSpec((None, D_MODEL, FF_TILE), lambda j, i: (j, 0, 0))
    down = pl.BlockSpec((None, FF_TILE, D_MODEL), lambda j, i: (j, 0, 0))
    first_pass = pl.BlockSpec((tm, D_MODEL), lambda j, i: (jnp.where(j == 0, i, nt - 1), 0))
    return pl.pallas_call(
        body, name="mlp_bwd_weights", grid=(N_CHIPS, nt),
        in_specs=[first_pass, pl.BlockSpec((tm, FF_TILE), lambda j, i: (i, j)),
                  pl.BlockSpec((tm, FF_TILE), lambda j, i: (i, j)), first_pass, _full((1, D_MODEL))],
        out_specs=[up, down],
        out_shape=[jax.ShapeDtypeStruct((N_CHIPS, D_MODEL, FF_TILE), bf16), jax.ShapeDtypeStruct((N_CHIPS, FF_TILE, D_MODEL), bf16)],
        scratch_shapes=[pltpu.VMEM((nt, D_MODEL, tm), bf16), pltpu.VMEM((nt, tm, D_MODEL), bf16),
                        pltpu.VMEM((D_MODEL, FF_TILE), f32), pltpu.VMEM((FF_TILE, D_MODEL), f32)],
        compiler_params=_params("arbitrary", "arbitrary"),
    )(dx2, u, du, x1, g)


def loss_head(y, target, tm=512):
    def body(y_ref, t_ref, dy_ref, l_ref):
        @pl.when(pl.program_id(0) == 0)
        def _():
            l_ref[...] = jnp.zeros_like(l_ref)

        d = y_ref[...] - t_ref[...]
        dy_ref[...] = d * (1.0 / D_MODEL)
        part = jnp.sum(jnp.mean(d * d, axis=-1, keepdims=True), axis=0, keepdims=True)
        l_ref[...] += 0.5 * part

    tok = pl.BlockSpec((tm, D_MODEL), lambda i: (i, 0))
    return pl.pallas_call(
        body, name="loss_head", grid=(SEQ // tm,), in_specs=[tok, tok], out_specs=[tok, _full((1, 1))],
        out_shape=[jax.ShapeDtypeStruct((SEQ, D_MODEL), f32), jax.ShapeDtypeStruct((1, 1), f32)],
        compiler_params=_params("arbitrary"),
    )(y, target)


def _pad_lane(v):
    return jnp.pad(v, (0, LANE - v.shape[0]))[None, :]


def local_step(x, target, w, prov):
    bucket = jnp.asarray(_bucket_table().T)
    bias = bias_build(w["rel_bias"], bucket)
    saved = []
    for l in range(DEPTH):
        g_mix = w["mix_norm_g"][l][None, :] + prov.stage(("begin", l), x)
        w_in = prov.w_in(l, x)
        proj = in_fwd(x, g_mix, w_in)
        conv_b = w["conv_b"][l][None, :]
        act = conv_fwd(proj, w["conv_w"][l], conv_b)
        dtb = _pad_lane(w["dt_bias"][l]) + prov.stage(("mid", l), act)
        alog, dsk = _pad_lane(w["a_log"][l]), _pad_lane(w["d_skip"][l])
        ng = w["ssm_norm_g"][l][None, :]
        ssm, ypre, states = ssd_fwd_g(act, proj, dtb, alog, dsk, ng)
        qg, kg = w["q_gain"][l][:, None] + 0.0 * ssm[:1, :1], w["k_gain"][l][None, :]
        attn = attn_fwd_t(proj, qg, kg, w["sinks"][l], bias)
        tok = prov.stage(("pre_out", l), attn)
        w_out = prov.w_out(l, attn) + jnp.asarray(tok, bf16)
        x1 = out_fwd(x, attn, ssm, w_out)
        g_mlp = w["mlp_norm_g"][l][None, :] + prov.stage(("pre_mlp", l), x1)
        w_up, w_down = prov.mlp(l, x1)
        x2, u = mlp_fwd(x1, g_mlp, w_up, w_down)
        saved.append(dict(x=x, proj=proj, attn=attn, act=act, ssm=ssm, ypre=ypre, states=states, x1=x1, u=u,
                          g_mix=g_mix, qg=qg, kg=kg, conv_b=conv_b, dtb=dtb, alog=alog, dsk=dsk, ng=ng, g_mlp=g_mlp,
                          w_in=w_in, w_out=w_out, w_up=w_up, w_down=w_down))
        x = x2
    dx, loss = loss_head(x, target)
    grads = [None] * DEPTH
    dbands = [None] * DEPTH
    tok = 0.0
    for l in reversed(range(DEPTH)):
        s = saved[l]
        g_mlp = s["g_mlp"] + tok
        dx1, du, dg_mlp = mlp_bwd_data(dx, s["u"], s["x1"], g_mlp, s["w_up"], s["w_down"])
        dw_up, dw_down = mlp_bwd_weights(dx, s["u"], du, s["x1"], g_mlp)
        tok = prov.grads(("mlp", l), dict(w_up=dw_up, w_down=dw_down), dx1)
        dattn, dssm, dw_out = out_bwd(dx1, s["attn"], s["ssm"], s["w_out"])
        dact, ddt, dz, dng, dpar = ssd_bwd_g(s["act"], s["proj"], s["ypre"], s["states"], dssm, s["dtb"] + tok, s["alog"],
                                           s["dsk"], s["ng"])
        conv_b = s["conv_b"] + prov.stage(("bwd_mid", l), dact)
        dxbc, dconv_w, dconv_b = conv_bwd(s["proj"], dact, w["conv_w"][l], conv_b)
        dq, dk, dv, dband, dsink, dqg, dkg = attn_bwd_t(s["proj"], dattn, s["qg"], s["kg"], w["sinks"][l], bias)
        dbands[l] = dband
        g_mix = s["g_mix"]
        if l == 0:
            d_rel = bias_bwd(dbands[0], dbands[1], bucket)
            g_mix = g_mix + 0.0 * d_rel[:1, :1]
        dx, dw_in, dg_mix = in_bwd(dq, dz, dxbc, dk, dv, ddt, s["x"], g_mix, s["w_in"], dx1)
        tok = prov.grads(("mix", l), dict(w_in=split_w_in_grad(dw_in), w_out=dw_out), dx)
        grads[l] = dict(mix_norm_g=dg_mix[0], q_gain=dqg[:, 0], k_gain=dkg[0], sinks=dsink[:, 0],
                        conv_w=dconv_w, conv_b=dconv_b[0], dt_bias=dpar[0, :SSM_HEADS], a_log=dpar[1, :SSM_HEADS],
                        d_skip=dpar[2, :SSM_HEADS], ssm_norm_g=dng[0], mlp_norm_g=dg_mlp[0])
    out = {k: jnp.stack([grads[l][k] for l in range(DEPTH)]) for k in grads[0]}
    out["rel_bias"] = d_rel[:, :N_Q_HEADS]
    return loss, dx, out, tok


MESH = pl.DeviceIdType.MESH
HBM = pl.BlockSpec(memory_space=pltpu.HBM)
N_DEVICES = 8


def _coords():
    return lax.axis_index("x"), lax.axis_index("y"), lax.axis_index("c")


def _peer_chips(x, y):
    return [(1 - x, y), (x, 1 - y), (1 - x, 1 - y)]


def _remote(src, dst, send_sem, recv_sem, device):
    return pltpu.make_async_remote_copy(src_ref=src, dst_ref=dst, send_sem=send_sem, recv_sem=recv_sem,
                                        device_id=device, device_id_type=MESH)


SEM = pl.BlockSpec(memory_space=pltpu.SEMAPHORE)
ANY = pl.BlockSpec(memory_space=pl.ANY)
DATAFLOW = pltpu.SideEffectType.DATAFLOW_SIDE_EFFECTING


def _gather_copies(kind, src_refs, land_refs, ssem, rsem):
    x, y, c = _coords()
    k_me = 2 * x + y
    n = len(land_refs)
    cps = []
    for p, land in enumerate(land_refs):
        hr = land.shape[1] // 2
        rows = pl.ds(c * hr, hr)
        for j, chip in enumerate(_peer_chips(x, y)):
            i = 3 * p + j
            if kind == "ici":
                cps.append(_remote(src_refs[p].at[rows, :], land.at[k_me, rows, :], ssem.at[i], rsem.at[i], (*chip, c)))
            else:
                got = land.at[2 * chip[0] + chip[1], rows, :]
                cps.append(_remote(got, got, ssem.at[i], rsem.at[i], (x, y, 1 - c)))
        if kind == "relay":
            cps.append(_remote(src_refs[p], land.at[k_me], ssem.at[3 * n + p], rsem.at[3 * n + p], (x, y, 1 - c)))
    return cps


def gather_now(srcs, conv):
    n = len(srcs)

    def body(*refs):
        src_refs, conv_ref = refs[:n], refs[n]
        lands, gconv = refs[n + 1:2 * n + 1], refs[2 * n + 1]
        ssem, rsem, fsem, frsem, csem, crsem = refs[2 * n + 2:]
        x, y, c = _coords()
        k_me = 2 * x + y
        targets = [(*chip, c) for chip in _peer_chips(x, y)] + [(x, y, 1 - c)]
        ici = _gather_copies("ici", src_refs, lands, ssem, rsem)
        relay = _gather_copies("relay", src_refs, lands, fsem, frsem)
        passed = [cp for i, cp in enumerate(relay) if i % 4 != 3]
        own = relay[3::4]
        conv_cps = [_remote(conv_ref, gconv.at[k_me], csem.at[j], crsem.at[j], t) for j, t in enumerate(targets)]
        for cp in ici + conv_cps + own:
            cp.start()
        for cp, fw in zip(ici, passed):
            cp.wait_recv()
            fw.start()
        for cp in conv_cps + relay:
            cp.wait_recv()
        for cp in ici + relay + conv_cps:
            cp.wait_send()

    out_shape = [jax.ShapeDtypeStruct((N_CHIPS,) + s.shape, s.dtype) for s in srcs]
    out_shape.append(jax.ShapeDtypeStruct((N_CHIPS,) + conv.shape, conv.dtype))
    sems = lambda k: pltpu.SemaphoreType.DMA((k,))
    return pl.pallas_call(
        body, name="gather_now", out_shape=out_shape, in_specs=[HBM] * (n + 1), out_specs=[HBM] * (n + 1),
        scratch_shapes=[sems(3 * n), sems(3 * n), sems(4 * n), sems(4 * n), sems(N_CHIPS), sems(N_CHIPS)],
    )(*srcs, conv)


def _gather_maker(kind, n_src):
    def make(refs, ssem, rsem):
        cps = _gather_copies(kind, refs[:n_src], refs[n_src:], ssem, rsem)
        return cps, cps
    return make


def _scatter_maker(n):
    def make(refs, ssem, rsem):
        x, y, c = _coords()
        k_me = 2 * x + y
        sends, arrivals = [], []
        for p in range(n):
            src, land = refs[p], refs[n + p]
            sends.append(_remote(src.at[k_me, 1 - c], land.at[0], ssem.at[7 * p], rsem.at[7 * p], (x, y, 1 - c)))
            for j, chip in enumerate(_peer_chips(x, y)):
                for cc in range(2):
                    sends.append(_remote(src.at[2 * chip[0] + chip[1], cc], land.at[1 + 2 * j + c],
                                         ssem.at[7 * p + 1 + 2 * j + cc], rsem.at[7 * p + 1 + 2 * j + c], (*chip, cc)))
            for s in range(7):
                arrivals.append(_remote(land.at[s], land.at[s], ssem.at[7 * p + s], rsem.at[7 * p + s], (x, y, 1 - c)))
        return sends, arrivals
    return make


def _share_maker(n):
    def make(refs, ssem, rsem):
        x, y, c = _coords()
        sends = [_remote(refs[p].at[c], refs[p].at[c], ssem.at[p], rsem.at[p], (x, y, 1 - c)) for p in range(n)]
        arrivals = [_remote(refs[p].at[1 - c], refs[p].at[1 - c], ssem.at[p], rsem.at[p], (x, y, 1 - c)) for p in range(n)]
        return sends, arrivals
    return make


def split_start(name, make, n_sems, operands, after):
    n = len(operands)

    def body(*refs):
        ssem, rsem, token = refs[n + 1], refs[n + 2], refs[-1]
        for cp in make(refs[:n], ssem, rsem)[0]:
            cp.start()
        token[...] = jnp.zeros_like(token)

    ops = [pltpu.with_memory_space_constraint(a, pltpu.HBM) for a in operands]
    outs = pl.pallas_call(
        body, name=name,
        out_shape=(pltpu.SemaphoreType.DMA((n_sems,)), pltpu.SemaphoreType.DMA((n_sems,)),
                   *[pltpu.HBM(a.shape, a.dtype) for a in ops], jax.ShapeDtypeStruct((8, LANE), f32)),
        in_specs=[HBM] * n + [ANY], out_specs=(SEM, SEM, *[HBM] * n, pl.BlockSpec(memory_space=pltpu.VMEM)),
        input_output_aliases={i: 2 + i for i in range(n)},
        compiler_params=pltpu.CompilerParams(has_side_effects=DATAFLOW),
    )(*ops, after)
    return dict(name=name, make=make, ssem=outs[0], rsem=outs[1], operands=outs[2:2 + n], token=outs[-1][0, 0])


def split_wait(handle, after):
    n = len(handle["operands"])

    def body(*refs):
        sends, arrivals = handle["make"](refs[:n], refs[n], refs[n + 1])
        for cp in sends:
            cp.wait_send()
        for cp in arrivals:
            cp.wait_recv()

    outs = pl.pallas_call(
        body, name=handle["name"].replace("start", "wait"),
        out_shape=tuple(pltpu.HBM(a.shape, a.dtype) for a in handle["operands"]),
        in_specs=[HBM] * n + [SEM, SEM, ANY], out_specs=tuple([HBM] * n),
        input_output_aliases={i: i for i in range(n)},
        compiler_params=pltpu.CompilerParams(has_side_effects=DATAFLOW),
    )(*handle["operands"], handle["ssem"], handle["rsem"], after)
    return list(outs)


def piece_sums(gs, recvs, kc_arr):
    n, steps = len(gs), 2

    def body(kc_ref, *refs):
        for p in range(n):
            g_ref, r_ref, o_ref = refs[p], refs[n + p], refs[2 * n + p]
            acc = g_ref[...].astype(f32)
            for s in range(7):
                acc = acc + r_ref[s].astype(f32)
            o_ref[...] = acc

    tiles = [(g.shape[2] // steps, g.shape[3]) for g in gs]
    return pl.pallas_call(
        body, name="piece_sums",
        grid_spec=pltpu.PrefetchScalarGridSpec(
            num_scalar_prefetch=1, grid=(steps,),
            in_specs=[pl.BlockSpec((None, None, tr, cc), lambda r, kc: (kc[0], kc[1], r, 0)) for tr, cc in tiles]
                     + [pl.BlockSpec((7, tr, cc), lambda r, kc: (0, r, 0)) for tr, cc in tiles],
            out_specs=[pl.BlockSpec((None, tr, cc), lambda r, kc: (kc[1], r, 0)) for tr, cc in tiles]),
        out_shape=[jax.ShapeDtypeStruct((2,) + g.shape[2:], f32) for g in gs],
        compiler_params=_params("arbitrary"),
    )(kc_arr, *gs, *recvs)


def small_all_reduce(vec):
    def body(v_ref, o_ref, gat, ssem, rsem):
        x, y, c = _coords()
        me = 4 * x + 2 * y + c
        gat[me] = v_ref[...]
        sends = []
        for t in range(1, N_DEVICES):
            peer = (x ^ (t >> 2), y ^ ((t >> 1) & 1), c ^ (t & 1))
            cp = _remote(v_ref, gat.at[me], ssem.at[t - 1], rsem.at[t - 1], peer)
            cp.start()
            sends.append(cp)
        for t in range(1, N_DEVICES):
            peer = (x ^ (t >> 2), y ^ ((t >> 1) & 1), c ^ (t & 1))
            slot = gat.at[4 * peer[0] + 2 * peer[1] + peer[2]]
            _remote(slot, slot, ssem.at[t - 1], rsem.at[t - 1], peer).wait_recv()
        for cp in sends:
            cp.wait_send()
        acc = gat[0]
        for d in range(1, N_DEVICES):
            acc = acc + gat[d]
        o_ref[...] = acc

    return pl.pallas_call(
        body, name="small_all_reduce", out_shape=jax.ShapeDtypeStruct(vec.shape, vec.dtype),
        in_specs=[pl.BlockSpec(memory_space=pltpu.VMEM)], out_specs=pl.BlockSpec(memory_space=pltpu.VMEM),
        scratch_shapes=[pltpu.VMEM((N_DEVICES,) + vec.shape, vec.dtype), pltpu.SemaphoreType.DMA((N_DEVICES - 1,)),
                        pltpu.SemaphoreType.DMA((N_DEVICES - 1,))],
    )(vec)


def _adamw_math(w, g, m, v):
    m_new = ADAM_B1 * m + (1.0 - ADAM_B1) * g
    v_new = ADAM_B2 * v + (1.0 - ADAM_B2) * jnp.square(g)
    m_hat = m_new / (1.0 - ADAM_B1 ** ADAM_STEP)
    v_hat = v_new / (1.0 - ADAM_B2 ** ADAM_STEP)
    delta = -ADAM_LR * (m_hat / (jnp.sqrt(v_hat) + ADAM_EPS) + ADAM_WD * w)
    return delta, m_new, v_new


def adamw_shards(ws, g0s, g1s, ms, vs):
    n = len(ws)
    depth, rows, cols = ws[0].shape
    half = rows // 2
    tr = min(256, half)
    nr = half // tr

    def body(*refs):
        for i in range(n):
            w_ref, g0_ref, g1_ref, m_ref, v_ref = refs[5 * i:5 * i + 5]
            go_ref, d_ref, nm_ref, nv_ref = refs[5 * n + 4 * i:5 * n + 4 * i + 4]
            gv = jnp.where(pl.program_id(0) == 0, g0_ref[...], g1_ref[...])
            go_ref[...] = gv
            d_ref[...], nm_ref[...], nv_ref[...] = _adamw_math(w_ref[...], gv, m_ref[...], v_ref[...])

    spec = pl.BlockSpec((None, tr, cols), lambda l, h, r: (l, h * nr + r, 0))
    g0spec = pl.BlockSpec((None, tr, cols), lambda l, h, r: (jnp.where(l == 0, h, 1), jnp.where(l == 0, r, nr - 1), 0))
    g1spec = pl.BlockSpec((None, tr, cols), lambda l, h, r: (jnp.where(l == 1, h, 0), jnp.where(l == 1, r, 0), 0))
    outs = pl.pallas_call(
        body, name="adamw_shards", grid=(depth, 2, nr), in_specs=[spec, g0spec, g1spec, spec, spec] * n,
        out_specs=[spec] * (4 * n), out_shape=[jax.ShapeDtypeStruct(ws[0].shape, f32)] * (4 * n),
        compiler_params=_params("arbitrary", "arbitrary", "arbitrary"),
    )(*[a for i in range(n) for a in (ws[i], g0s[i], g1s[i], ms[i], vs[i])])
    return [outs[4 * i:4 * i + 4] for i in range(n)]


def adamw_cols(w, g, m, v, tc=34):
    cols, depth, rows = w.shape

    def body(w_ref, g_ref, m_ref, v_ref, d_ref, nm_ref, nv_ref):
        d_ref[...], nm_ref[...], nv_ref[...] = _adamw_math(w_ref[...], g_ref[...], m_ref[...], v_ref[...])

    spec = pl.BlockSpec((tc, depth, rows), lambda i: (i, 0, 0))
    return pl.pallas_call(
        body, name="adamw_cols", grid=(cols // tc,), in_specs=[spec] * 4, out_specs=[spec] * 3,
        out_shape=[jax.ShapeDtypeStruct(w.shape, f32)] * 3,
        compiler_params=_params("arbitrary"),
    )(w, g, m, v)


def adamw_small(ws, gs, ms, vs):
    n = len(ws)

    def body(*refs):
        ins, outs = refs[:4 * n], refs[4 * n:]
        for i in range(n):
            w_ref, g_ref, m_ref, v_ref = (ins[k * n + i] for k in range(4))
            outs[i][...], outs[n + i][...], outs[2 * n + i][...] = _adamw_math(w_ref[...], g_ref[...], m_ref[...], v_ref[...])

    outs = pl.pallas_call(
        body, name="adamw_small", out_shape=[jax.ShapeDtypeStruct(w.shape, f32) for w in ws] * 3,
    )(*ws, *gs, *ms, *vs)
    return outs[:n], outs[n:2 * n], outs[2 * n:]


WEIGHTS = ("mix_norm_g", "w_in", "q_gain", "k_gain", "sinks", "rel_bias", "conv_w", "conv_b", "dt_bias", "a_log", "d_skip",
           "ssm_norm_g", "w_out", "mlp_norm_g", "w_up", "w_down")
BIG = ("w_in", "w_out", "w_up", "w_down")
SMALL = tuple(n for n in WEIGHTS if n not in BIG)
PACK_COLS = 1024
PACK_ROWS = 16


def _pack(named, last=None):
    flat = jnp.concatenate([named[n].reshape(-1) for n in SMALL])
    tail = jnp.zeros((1,), f32) if last is None else last.reshape(1)
    pad = jnp.zeros((PACK_ROWS * PACK_COLS - flat.shape[0] - 1,), f32)
    return jnp.concatenate([flat, pad, tail]).reshape(PACK_ROWS, PACK_COLS)


def _unpack(buf, shapes):
    flat = buf.reshape(-1)
    out, at = {}, 0
    for n in SMALL:
        size = int(np.prod(shapes[n]))
        out[n] = flat[at:at + size].reshape(shapes[n])
        at += size
    return out


class _Exchange:
    GROUPS = {"A": (("w_up", 0), ("w_down", 0)), "B": (("w_in", 1), ("w_out", 1)), "C": (("w_up", 1), ("w_down", 1))}
    ICI_AT = {("mid", 0): "B", ("pre_out", 0): "C"}
    RELAY_AT = {("pre_out", 0): "A", ("pre_mlp", 0): "B", ("mid", 1): "C"}
    LAST = ("mix", 0)
    IN_FLIGHT = 2

    def __init__(self, wts, kc_arr):
        self.wts, self.kc_arr = wts, kc_arr
        self.own = {(n, l): wts[n][l].astype(bf16) for n in BIG for l in range(DEPTH)}
        now = gather_now([self.own["w_in", 0], self.own["w_out", 0]], wts["conv_w"])
        self.ready = {("w_in", 0): now[0], ("w_out", 0): now[1]}
        self.conv_w = jnp.transpose(now[2], (1, 2, 0, 3)).reshape(DEPTH, CONV_WIDTH, D_CONV)
        self.ici, self.relay = {}, {}
        self.scatter, self.share, self.reduced = [], [], {}
        self._start_ici("A", now[2])

    def _start_ici(self, g, after):
        srcs = [self.own[p] for p in self.GROUPS[g]]
        lands = [lax.empty((N_CHIPS,) + s.shape, s.dtype) for s in srcs]
        self.ici[g] = split_start("gather%s_ici_start" % g, _gather_maker("ici", len(srcs)), 3 * len(srcs), srcs + lands,
                                  after)
        return self.ici[g]["token"]

    def stage(self, name, after):
        if name == ("begin", 0):
            return self.ici["A"]["token"]
        tok = 0.0
        g = self.RELAY_AT.get(name)
        if g is not None:
            n = len(self.GROUPS[g])
            self.relay[g] = split_start("gather%s_relay_start" % g, _gather_maker("relay", n), 4 * n,
                                        split_wait(self.ici[g], after), after)
            tok = self.relay[g]["token"]
        if name in self.ICI_AT:
            tok = tok + self._start_ici(self.ICI_AT[name], after)
        return tok

    def _get(self, piece, after):
        if piece not in self.ready:
            g = [k for k, pieces in self.GROUPS.items() if piece in pieces][0]
            lands = split_wait(self.relay[g], after)[len(self.GROUPS[g]):]
            self.ready.update(zip(self.GROUPS[g], lands))
        return self.ready[piece]

    def w_in(self, l, after):
        return align_w_in(self._get(("w_in", l), after))

    def w_out(self, l, after):
        return self._get(("w_out", l), after).reshape(D_MODEL, D_MODEL)

    def mlp(self, l, after):
        return self._get(("w_up", l), after), self._get(("w_down", l), after)

    def _view(self, n, g):
        _, rows, cols = self.wts[n].shape
        return g.reshape(N_CHIPS, 2, rows // 2, cols)

    def grads(self, name, arrays, after):
        if name == self.LAST:
            self.held = (name, arrays)
            return 0.0
        return self._scatter(name, arrays, after) + self._advance(after, self.IN_FLIGHT)

    def flush(self, after):
        return self._scatter(*self.held, after) + self._advance(after, self.IN_FLIGHT)

    def _scatter(self, name, arrays, after):
        pieces = [(n, name[1]) for n in arrays]
        views = [self._view(n, g) for n, g in arrays.items()]
        lands = [lax.empty((7,) + v.shape[2:], bf16) for v in views]
        h = split_start("scatter_%s%d_start" % name, _scatter_maker(len(views)), 7 * len(views), views + lands, after)
        self.scatter.append((pieces, h))
        return h["token"]

    def _take_share(self, after):
        pieces, h = self.share.pop(0)
        self.reduced.update(zip(pieces, split_wait(h, after)))

    def _take_scatter(self, after):
        pieces, h = self.scatter.pop(0)
        done = split_wait(h, after)
        views, lands = done[:len(pieces)], done[len(pieces):]
        sums = list(piece_sums(views, lands, self.kc_arr))
        hs = split_start(h["name"].replace("scatter", "share"), _share_maker(len(sums)), len(sums), sums, after)
        self.share.append((pieces, hs))
        return hs["token"]

    def _advance(self, after, newest):
        if self.share:
            self._take_share(after)
        return self._take_scatter(after) if len(self.scatter) > newest else 0.0

    def reduced_grads(self, names, after):
        want = [(n, l) for n in names for l in range(DEPTH)]
        while not all(p in self.reduced for p in want):
            if any(p in pieces for p in want for pieces, _ in self.share):
                self._take_share(after)
            else:
                self._take_scatter(after)
        return {n: [self.reduced[n, l] for l in range(DEPTH)] for n in names}


def kernel(x, mix_norm_g, w_in, q_gain, k_gain, sinks, rel_bias, conv_w, conv_b, dt_bias, a_log, d_skip, ssm_norm_g, w_out, mlp_norm_g, w_up, w_down, loss_target, m_mix_norm_g, m_w_in, m_q_gain, m_k_gain, m_sinks, m_rel_bias, m_conv_w, m_conv_b, m_dt_bias, m_a_log, m_d_skip, m_ssm_norm_g, m_w_out, m_mlp_norm_g, m_w_up, m_w_down, v_mix_norm_g, v_w_in, v_q_gain, v_k_gain, v_sinks, v_rel_bias, v_conv_w, v_conv_b, v_dt_bias, v_a_log, v_d_skip, v_ssm_norm_g, v_w_out, v_mlp_norm_g, v_w_up, v_w_down):
    wts = dict(mix_norm_g=mix_norm_g, w_in=w_in, q_gain=q_gain, k_gain=k_gain, sinks=sinks, rel_bias=rel_bias, conv_w=conv_w,
               conv_b=conv_b, dt_bias=dt_bias, a_log=a_log, d_skip=d_skip, ssm_norm_g=ssm_norm_g, w_out=w_out,
               mlp_norm_g=mlp_norm_g, w_up=w_up, w_down=w_down)
    mom = dict(mix_norm_g=m_mix_norm_g, w_in=m_w_in, q_gain=m_q_gain, k_gain=m_k_gain, sinks=m_sinks, rel_bias=m_rel_bias,
               conv_w=m_conv_w, conv_b=m_conv_b, dt_bias=m_dt_bias, a_log=m_a_log, d_skip=m_d_skip, ssm_norm_g=m_ssm_norm_g,
               w_out=m_w_out, mlp_norm_g=m_mlp_norm_g, w_up=m_w_up, w_down=m_w_down)
    var = dict(mix_norm_g=v_mix_norm_g, w_in=v_w_in, q_gain=v_q_gain, k_gain=v_k_gain, sinks=v_sinks, rel_bias=v_rel_bias,
               conv_w=v_conv_w, conv_b=v_conv_b, dt_bias=v_dt_bias, a_log=v_a_log, d_skip=v_d_skip, ssm_norm_g=v_ssm_norm_g,
               w_out=v_w_out, mlp_norm_g=v_mlp_norm_g, w_up=v_w_up, w_down=v_w_down)
    xi, yi, ci = _coords()
    k_me = 2 * xi + yi
    kc_arr = jnp.stack([k_me, ci]).astype(jnp.int32)

    prov = _Exchange(wts, kc_arr)
    small_w = {n: wts[n] for n in SMALL}
    small_w["conv_w"] = prov.conv_w
    loss, dx, grads, tok = local_step(x[0], loss_target[0], small_w, prov)

    small_shapes = {n: grads[n].shape for n in SMALL}
    small_sum = small_all_reduce(_pack(grads, loss) + tok)
    loss = small_sum[PACK_ROWS - 1, PACK_COLS - 1]
    tok = prov.flush(small_sum)
    small = _unpack(small_sum, small_shapes)
    cols = conv_w.shape[-1]
    small["conv_w"] = lax.dynamic_slice_in_dim(small["conv_w"], k_me * cols, cols, axis=2)
    g_out_d = dict(small)
    gs = [small[n] for n in SMALL]
    gs[0] = gs[0] + tok
    ds, nms, nvs = adamw_small([wts[n] for n in SMALL], gs, [mom[n] for n in SMALL], [var[n] for n in SMALL])
    d_out_d, m_out_d, v_out_d = dict(zip(SMALL, ds)), dict(zip(SMALL, nms)), dict(zip(SMALL, nvs))

    after = ds[0]
    for names in (("w_up", "w_down"), ("w_in", "w_out")):
        red = prov.reduced_grads(names, after)
        if "w_in" in red:
            n = "w_in"
            rows, cols = wts[n].shape[1:]
            to_cols = lambda a: jnp.transpose(a, (2, 0, 1))
            g_t = jnp.stack([to_cols(g).reshape(cols, rows) for g in red[n]], axis=1)
            res_t = adamw_cols(to_cols(wts[n]), g_t, to_cols(mom[n]), to_cols(var[n]))
            g_out_d[n], d_out_d[n], m_out_d[n], v_out_d[n] = (jnp.transpose(a, (1, 2, 0)) for a in (g_t, *res_t))
        plain = [n for n in names if n != "w_in"]
        res = adamw_shards([wts[n] for n in plain], [red[n][0] for n in plain], [red[n][1] for n in plain],
                           [mom[n] for n in plain], [var[n] for n in plain])
        for n, r in zip(plain, res):
            g_out_d[n], d_out_d[n], m_out_d[n], v_out_d[n] = r
        after = d_out_d[plain[-1]]

    return (loss, dx[None], *[g_out_d[n] for n in WEIGHTS], *[d_out_d[n] for n in WEIGHTS],
            *[m_out_d[n] for n in WEIGHTS], *[v_out_d[n] for n in WEIGHTS])
```
